```python
import math
import jax, jax.numpy as jnp
from jax import lax
import numpy as np

D_MODEL = 1024
BATCH = 8
SEQ = 4096
DEPTH = 2

S5_WIDTH = D_MODEL // 2
S5_GROUP = 16
S5_GROUPS = S5_WIDTH // S5_GROUP
S5_STATE = 64
FOX_HEAD_DIM = 64
FOX_HEADS = (D_MODEL - S5_WIDTH) // FOX_HEAD_DIM
FOX_WIDTH = FOX_HEADS * FOX_HEAD_DIM
Q_BLOCK = 128
EVEN_IN = S5_WIDTH + 3 * FOX_WIDTH + FOX_HEADS
EVEN_MIX = S5_WIDTH + FOX_WIDTH
POOL_WIDTH = D_MODEL // 2
POOL_WINDOWS = (2, 4, 8, 16)
POOL_GROUPS = len(POOL_WINDOWS)
POOL_GROUP_DIM = POOL_WIDTH // POOL_GROUPS
SGU_WIDTH = D_MODEL // 2
SGU_GROUPS = 4
SGU_GROUP_DIM = SGU_WIDTH // SGU_GROUPS
CHUNK = 128
ODD_IN = POOL_WIDTH + 2 * SGU_WIDTH
ODD_MIX = POOL_WIDTH + SGU_WIDTH
D_FF = 4 * D_MODEL
N_EVEN = (DEPTH + 1) // 2
N_ODD = DEPTH // 2
EPS = 1e-6

kernel_name = 'hybrid_s5_fox_pool_sgu'


def rms_norm(x, g):
    xf = x.astype(jnp.float32)
    y = xf * lax.rsqrt(jnp.mean(xf * xf, axis=-1, keepdims=True) + EPS)
    return (y * g.astype(jnp.float32)).astype(x.dtype)


def layer_norm(x, g, b):
    xf = x.astype(jnp.float32)
    mu = jnp.mean(xf, axis=-1, keepdims=True)
    xc = xf - mu
    y = xc * lax.rsqrt(jnp.mean(xc * xc, axis=-1, keepdims=True) + EPS)
    return (y * g.astype(jnp.float32) + b.astype(jnp.float32)).astype(x.dtype)


def _complex_scan_combine(e_i, e_j):
    ar_i, ai_i, br_i, bi_i = e_i
    ar_j, ai_j, br_j, bi_j = e_j
    ar = ar_j * ar_i - ai_j * ai_i
    ai = ar_j * ai_i + ai_j * ar_i
    br = ar_j * br_i - ai_j * bi_i + br_j
    bi = ar_j * bi_i + ai_j * br_i + bi_j
    return (ar, ai, br, bi)


def s5_mixer(u, lam_re, lam_im, log_dt, b_re, b_im, c_re, c_im, d, w_glu):
    f32 = jnp.float32
    bsz, L, _ = u.shape
    uf = u.astype(f32)
    dt = jnp.exp(log_dt.astype(f32))[:, None]
    lr = lam_re.astype(f32)
    li = lam_im.astype(f32)
    mag = jnp.exp(lr * dt)
    ab_re = mag * jnp.cos(li * dt)
    ab_im = mag * jnp.sin(li * dt)
    den = lr * lr + li * li
    nr = ab_re - 1.0
    ni = ab_im
    q_re = (nr * lr + ni * li) / den
    q_im = (ni * lr - nr * li) / den
    br = b_re.astype(f32)
    bi = b_im.astype(f32)
    bb_re = q_re[..., None] * br - q_im[..., None] * bi
    bb_im = q_re[..., None] * bi + q_im[..., None] * br
    ut = jnp.swapaxes(uf.reshape(bsz, L, S5_GROUPS, S5_GROUP), 0, 1)
    bu_re = jnp.einsum('lbgh,gph->lbgp', ut, bb_re)
    bu_im = jnp.einsum('lbgh,gph->lbgp', ut, bb_im)
    a_re = jnp.broadcast_to(ab_re[None, None], (L, 1, S5_GROUPS, S5_STATE))
    a_im = jnp.broadcast_to(ab_im[None, None], (L, 1, S5_GROUPS, S5_STATE))
    _, _, x_re, x_im = lax.associative_scan(_complex_scan_combine, (a_re, a_im, bu_re, bu_im), axis=0)
    y = (jnp.einsum('lbgp,ghp->lbgh', x_re, c_re.astype(f32))
         - jnp.einsum('lbgp,ghp->lbgh', x_im, c_im.astype(f32)))
    y = jnp.swapaxes(y, 0, 1).reshape(bsz, L, S5_WIDTH) + d.astype(f32) * uf
    y = jax.nn.gelu(y)
    y = y * jax.nn.sigmoid(y @ w_glu.astype(f32))
    return y.astype(u.dtype)


def fox_attention(q, k, v, f_logit, b_f):
    f32 = jnp.float32
    bsz, L, H, Dh = q.shape
    log_f = jax.nn.log_sigmoid(f_logit.astype(f32) + b_f.astype(f32))
    F = jnp.cumsum(log_f, axis=1)
    F_k = jnp.transpose(F, (0, 2, 1))
    n_blk = L // Q_BLOCK
    q_blocks = jnp.swapaxes(q.reshape(bsz, n_blk, Q_BLOCK, H, Dh), 0, 1)
    F_blocks = jnp.swapaxes(F_k.reshape(bsz, H, n_blk, Q_BLOCK), 0, 2).swapaxes(1, 2)
    k_pos = jnp.arange(L)
    scale = Dh ** -0.5

    def block(args):
        i, qi, Fi = args
        s = jnp.einsum('bqhd,bkhd->bhqk', qi, k).astype(f32) * scale
        s = s + (Fi[..., None] - F_k[:, :, None, :])
        q_pos = i * Q_BLOCK + jnp.arange(Q_BLOCK)
        mask = k_pos[None, :] <= q_pos[:, None]
        s = jnp.where(mask[None, None], s, -jnp.inf)
        p = jax.nn.softmax(s, axis=-1).astype(v.dtype)
        return jnp.einsum('bhqk,bkhd->bqhd', p, v)

    out = lax.map(block, (jnp.arange(n_blk), q_blocks, F_blocks))
    return jnp.swapaxes(out, 0, 1).reshape(bsz, L, H * Dh)


def pool_mixer(xc, pool_w, pool_scale):
    f32 = jnp.float32
    bsz, L, _ = xc.shape
    xg = xc.astype(f32).reshape(bsz, L, POOL_GROUPS, POOL_GROUP_DIM)
    csum = jnp.cumsum(xg, axis=1)
    t = jnp.arange(L, dtype=f32)
    outs = []
    for g, w in enumerate(POOL_WINDOWS):
        cg = csum[:, :, g]
        lagged = jnp.pad(cg, ((0, 0), (w, 0), (0, 0)))[:, :L]
        cnt = jnp.minimum(t + 1.0, float(w))[None, :, None]
        outs.append((cg - lagged) / cnt - xg[:, :, g])
    pooled = jnp.stack(outs, axis=2)
    y = jnp.einsum('blgc,gcd->blgd', pooled, pool_w.astype(f32)).reshape(bsz, L, POOL_WIDTH)
    return (y * pool_scale.astype(f32)).astype(xc.dtype)


def sgu_mixer(u, v, ln_g, ln_b, w_s, b_s):
    bsz, L, _ = u.shape
    u = jax.nn.gelu(u)
    v = layer_norm(jax.nn.gelu(v), ln_g, ln_b)
    n_chunk = L // CHUNK
    vg = v.reshape(bsz, n_chunk, CHUNK, SGU_GROUPS, SGU_GROUP_DIM)
    causal = jnp.tril(jnp.ones((CHUNK, CHUNK), dtype=bool))
    ws = jnp.where(causal[None], w_s, jnp.zeros_like(w_s))
    mixed = jnp.einsum('gts,bnsgc->bntgc', ws, vg) + jnp.transpose(b_s)[None, None, :, :, None]
    return u * mixed.reshape(bsz, L, SGU_WIDTH)


def even_mixer(h, w_in, lam_re, lam_im, log_dt, b_re, b_im, c_re, c_im, d, w_glu, b_f, w_out):
    bsz, L, _ = h.shape
    z = h @ w_in
    s1 = S5_WIDTH
    s2 = s1 + FOX_WIDTH
    s3 = s2 + FOX_WIDTH
    s4 = s3 + FOX_WIDTH
    u, q, k, v, fl = z[..., :s1], z[..., s1:s2], z[..., s2:s3], z[..., s3:s4], z[..., s4:]
    y_a = s5_mixer(u, lam_re, lam_im, log_dt, b_re, b_im, c_re, c_im, d, w_glu)
    shp = (bsz, L, FOX_HEADS, FOX_HEAD_DIM)
    y_b = fox_attention(q.reshape(shp), k.reshape(shp), v.reshape(shp), fl, b_f)
    return jnp.concatenate([y_a, y_b], axis=-1) @ w_out


def odd_mixer(h, w_in, pool_w, pool_scale, ln_g, ln_b, w_s, b_s, w_out):
    z = h @ w_in
    s1 = POOL_WIDTH
    s2 = s1 + SGU_WIDTH
    xc, u, v = z[..., :s1], z[..., s1:s2], z[..., s2:]
    y_c = pool_mixer(xc, pool_w, pool_scale)
    y_d = sgu_mixer(u, v, ln_g, ln_b, w_s, b_s)
    return jnp.concatenate([y_c, y_d], axis=-1) @ w_out


def sq_relu_mlp(h, w1, w2):
    return jnp.square(jax.nn.relu(h @ w1)) @ w2


def _fwd_setup_inputs(seed: int = 0) -> dict:
    key = jax.random.key(seed)
    ks = jax.random.split(key, 32)
    f32 = jnp.float32

    def nrm(k, shape, scale):
        return scale * jax.random.normal(k, shape, f32)

    G, P, H = S5_GROUPS, S5_STATE, S5_GROUP
    return {
        'x': nrm(ks[0], (BATCH, SEQ, D_MODEL), 1.0),
        'mix_pre_g': 1.0 + nrm(ks[1], (DEPTH, D_MODEL), 0.02),
        'mix_post_g': 1.0 + nrm(ks[2], (DEPTH, D_MODEL), 0.02),
        'mlp_pre_g': 1.0 + nrm(ks[3], (DEPTH, D_MODEL), 0.02),
        'mlp_post_g': 1.0 + nrm(ks[4], (DEPTH, D_MODEL), 0.02),
        'w_in_even': nrm(ks[5], (N_EVEN, D_MODEL, EVEN_IN), D_MODEL ** -0.5),
        's5_lam_re': -0.5 + nrm(ks[6], (N_EVEN, G, P), 0.01),
        's5_lam_im': jnp.pi * jnp.arange(P, dtype=f32) + nrm(ks[7], (N_EVEN, G, P), 0.01),
        's5_log_dt': jax.random.uniform(ks[8], (N_EVEN, G), f32, math.log(1e-3), math.log(1e-1)),
        's5_b_re': nrm(ks[9], (N_EVEN, G, P, H), (2 * H) ** -0.5),
        's5_b_im': nrm(ks[10], (N_EVEN, G, P, H), (2 * H) ** -0.5),
        's5_c_re': nrm(ks[11], (N_EVEN, G, H, P), P ** -0.5),
        's5_c_im': nrm(ks[12], (N_EVEN, G, H, P), P ** -0.5),
        's5_d': nrm(ks[13], (N_EVEN, S5_WIDTH), 1.0),
        's5_w_glu': nrm(ks[14], (N_EVEN, S5_WIDTH, S5_WIDTH), S5_WIDTH ** -0.5),
        'fox_b_f': jax.random.uniform(ks[15], (N_EVEN, FOX_HEADS), f32, 0.0, 3.0),
        'w_out_even': nrm(ks[16], (N_EVEN, EVEN_MIX, D_MODEL), EVEN_MIX ** -0.5),
        'w_in_odd': nrm(ks[17], (N_ODD, D_MODEL, ODD_IN), D_MODEL ** -0.5),
        'pool_w': nrm(ks[18], (N_ODD, POOL_GROUPS, POOL_GROUP_DIM, POOL_GROUP_DIM), POOL_GROUP_DIM ** -0.5),
        'pool_scale': 1.0 + nrm(ks[19], (N_ODD, POOL_WIDTH), 0.02),
        'sgu_ln_g': 1.0 + nrm(ks[20], (N_ODD, SGU_WIDTH), 0.02),
        'sgu_ln_b': nrm(ks[21], (N_ODD, SGU_WIDTH), 0.02),
        'sgu_w_s': nrm(ks[22], (N_ODD, SGU_GROUPS, CHUNK, CHUNK), CHUNK ** -0.5),
        'sgu_b_s': 1.0 + nrm(ks[23], (N_ODD, SGU_GROUPS, CHUNK), 0.1),
        'w_out_odd': nrm(ks[24], (N_ODD, ODD_MIX, D_MODEL), ODD_MIX ** -0.5),
        'mlp_w1': nrm(ks[25], (DEPTH, D_MODEL, D_FF), D_MODEL ** -0.5),
        'mlp_w2': nrm(ks[26], (DEPTH, D_FF, D_MODEL), D_FF ** -0.5),
    }


def _fwd_reference(x, mix_pre_g, mix_post_g, mlp_pre_g, mlp_post_g,
              w_in_even, s5_lam_re, s5_lam_im, s5_log_dt, s5_b_re, s5_b_im, s5_c_re, s5_c_im,
              s5_d, s5_w_glu, fox_b_f, w_out_even,
              w_in_odd, pool_w, pool_scale, sgu_ln_g, sgu_ln_b, sgu_w_s, sgu_b_s, w_out_odd,
              mlp_w1, mlp_w2):
    for l in range(DEPTH):
        h = rms_norm(x, mix_pre_g[l])
        if l % 2 == 0:
            e = l // 2
            y = even_mixer(h, w_in_even[e], s5_lam_re[e], s5_lam_im[e], s5_log_dt[e],
                           s5_b_re[e], s5_b_im[e], s5_c_re[e], s5_c_im[e], s5_d[e], s5_w_glu[e],
                           fox_b_f[e], w_out_even[e])
        else:
            o = l // 2
            y = odd_mixer(h, w_in_odd[o], pool_w[o], pool_scale[o], sgu_ln_g[o], sgu_ln_b[o],
                          sgu_w_s[o], sgu_b_s[o], w_out_odd[o])
        x = x + rms_norm(y, mix_post_g[l])
        h = rms_norm(x, mlp_pre_g[l])
        x = x + rms_norm(sq_relu_mlp(h, mlp_w1[l], mlp_w2[l]), mlp_post_g[l])
    return x


import jax as _jax
import jax.numpy as _jnp

TWIN_FORMAT = 'train_step'
FWD_PARAMS = ['x', 'mix_pre_g', 'mix_post_g', 'mlp_pre_g', 'mlp_post_g', 'w_in_even', 's5_lam_re', 's5_lam_im', 's5_log_dt', 's5_b_re', 's5_b_im', 's5_c_re', 's5_c_im', 's5_d', 's5_w_glu', 'fox_b_f', 'w_out_even', 'w_in_odd', 'pool_w', 'pool_scale', 'sgu_ln_g', 'sgu_ln_b', 'sgu_w_s', 'sgu_b_s', 'w_out_odd', 'mlp_w1', 'mlp_w2']
TWIN_WEIGHTS = ['mix_pre_g', 'mix_post_g', 'mlp_pre_g', 'mlp_post_g', 'w_in_even', 's5_lam_re', 's5_lam_im', 's5_log_dt', 's5_b_re', 's5_b_im', 's5_c_re', 's5_c_im', 's5_d', 's5_w_glu', 'fox_b_f', 'w_out_even', 'w_in_odd', 'pool_w', 'pool_scale', 'sgu_ln_g', 'sgu_ln_b', 'sgu_w_s', 'sgu_b_s', 'w_out_odd', 'mlp_w1', 'mlp_w2']
TWIN_DIFF_INPUT = 'x'
TWIN_INPUTS = ['x', 'mix_pre_g', 'mix_post_g', 'mlp_pre_g', 'mlp_post_g', 'w_in_even', 's5_lam_re', 's5_lam_im', 's5_log_dt', 's5_b_re', 's5_b_im', 's5_c_re', 's5_c_im', 's5_d', 's5_w_glu', 'fox_b_f', 'w_out_even', 'w_in_odd', 'pool_w', 'pool_scale', 'sgu_ln_g', 'sgu_ln_b', 'sgu_w_s', 'sgu_b_s', 'w_out_odd', 'mlp_w1', 'mlp_w2', 'loss_target', 'm_mix_pre_g', 'm_mix_post_g', 'm_mlp_pre_g', 'm_mlp_post_g', 'm_w_in_even', 'm_s5_lam_re', 'm_s5_lam_im', 'm_s5_log_dt', 'm_s5_b_re', 'm_s5_b_im', 'm_s5_c_re', 'm_s5_c_im', 'm_s5_d', 'm_s5_w_glu', 'm_fox_b_f', 'm_w_out_even', 'm_w_in_odd', 'm_pool_w', 'm_pool_scale', 'm_sgu_ln_g', 'm_sgu_ln_b', 'm_sgu_w_s', 'm_sgu_b_s', 'm_w_out_odd', 'm_mlp_w1', 'm_mlp_w2', 'v_mix_pre_g', 'v_mix_post_g', 'v_mlp_pre_g', 'v_mlp_post_g', 'v_w_in_even', 'v_s5_lam_re', 'v_s5_lam_im', 'v_s5_log_dt', 'v_s5_b_re', 'v_s5_b_im', 'v_s5_c_re', 'v_s5_c_im', 'v_s5_d', 'v_s5_w_glu', 'v_fox_b_f', 'v_w_out_even', 'v_w_in_odd', 'v_pool_w', 'v_pool_scale', 'v_sgu_ln_g', 'v_sgu_ln_b', 'v_sgu_w_s', 'v_sgu_b_s', 'v_w_out_odd', 'v_mlp_w1', 'v_mlp_w2']
TWIN_OUTPUTS = ['loss', 'grad_x', 'grad_mix_pre_g', 'grad_mix_post_g', 'grad_mlp_pre_g', 'grad_mlp_post_g', 'grad_w_in_even', 'grad_s5_lam_re', 'grad_s5_lam_im', 'grad_s5_log_dt', 'grad_s5_b_re', 'grad_s5_b_im', 'grad_s5_c_re', 'grad_s5_c_im', 'grad_s5_d', 'grad_s5_w_glu', 'grad_fox_b_f', 'grad_w_out_even', 'grad_w_in_odd', 'grad_pool_w', 'grad_pool_scale', 'grad_sgu_ln_g', 'grad_sgu_ln_b', 'grad_sgu_w_s', 'grad_sgu_b_s', 'grad_w_out_odd', 'grad_mlp_w1', 'grad_mlp_w2', 'delta_mix_pre_g', 'delta_mix_post_g', 'delta_mlp_pre_g', 'delta_mlp_post_g', 'delta_w_in_even', 'delta_s5_lam_re', 'delta_s5_lam_im', 'delta_s5_log_dt', 'delta_s5_b_re', 'delta_s5_b_im', 'delta_s5_c_re', 'delta_s5_c_im', 'delta_s5_d', 'delta_s5_w_glu', 'delta_fox_b_f', 'delta_w_out_even', 'delta_w_in_odd', 'delta_pool_w', 'delta_pool_scale', 'delta_sgu_ln_g', 'delta_sgu_ln_b', 'delta_sgu_w_s', 'delta_sgu_b_s', 'delta_w_out_odd', 'delta_mlp_w1', 'delta_mlp_w2', 'new_m_mix_pre_g', 'new_m_mix_post_g', 'new_m_mlp_pre_g', 'new_m_mlp_post_g', 'new_m_w_in_even', 'new_m_s5_lam_re', 'new_m_s5_lam_im', 'new_m_s5_log_dt', 'new_m_s5_b_re', 'new_m_s5_b_im', 'new_m_s5_c_re', 'new_m_s5_c_im', 'new_m_s5_d', 'new_m_s5_w_glu', 'new_m_fox_b_f', 'new_m_w_out_even', 'new_m_w_in_odd', 'new_m_pool_w', 'new_m_pool_scale', 'new_m_sgu_ln_g', 'new_m_sgu_ln_b', 'new_m_sgu_w_s', 'new_m_sgu_b_s', 'new_m_w_out_odd', 'new_m_mlp_w1', 'new_m_mlp_w2', 'new_v_mix_pre_g', 'new_v_mix_post_g', 'new_v_mlp_pre_g', 'new_v_mlp_post_g', 'new_v_w_in_even', 'new_v_s5_lam_re', 'new_v_s5_lam_im', 'new_v_s5_log_dt', 'new_v_s5_b_re', 'new_v_s5_b_im', 'new_v_s5_c_re', 'new_v_s5_c_im', 'new_v_s5_d', 'new_v_s5_w_glu', 'new_v_fox_b_f', 'new_v_w_out_even', 'new_v_w_in_odd', 'new_v_pool_w', 'new_v_pool_scale', 'new_v_sgu_ln_g', 'new_v_sgu_ln_b', 'new_v_sgu_w_s', 'new_v_sgu_b_s', 'new_v_w_out_odd', 'new_v_mlp_w1', 'new_v_mlp_w2']
TWIN_LEAF_KINDS = {'loss': 'loss', 'grad_x': 'grad_x', 'grad_mix_pre_g': 'grad_w', 'grad_mix_post_g': 'grad_w', 'grad_mlp_pre_g': 'grad_w', 'grad_mlp_post_g': 'grad_w', 'grad_w_in_even': 'grad_w', 'grad_s5_lam_re': 'grad_w', 'grad_s5_lam_im': 'grad_w', 'grad_s5_log_dt': 'grad_w', 'grad_s5_b_re': 'grad_w', 'grad_s5_b_im': 'grad_w', 'grad_s5_c_re': 'grad_w', 'grad_s5_c_im': 'grad_w', 'grad_s5_d': 'grad_w', 'grad_s5_w_glu': 'grad_w', 'grad_fox_b_f': 'grad_w', 'grad_w_out_even': 'grad_w', 'grad_w_in_odd': 'grad_w', 'grad_pool_w': 'grad_w', 'grad_pool_scale': 'grad_w', 'grad_sgu_ln_g': 'grad_w', 'grad_sgu_ln_b': 'grad_w', 'grad_sgu_w_s': 'grad_w', 'grad_sgu_b_s': 'grad_w', 'grad_w_out_odd': 'grad_w', 'grad_mlp_w1': 'grad_w', 'grad_mlp_w2': 'grad_w', 'delta_mix_pre_g': 'delta_w', 'delta_mix_post_g': 'delta_w', 'delta_mlp_pre_g': 'delta_w', 'delta_mlp_post_g': 'delta_w', 'delta_w_in_even': 'delta_w', 'delta_s5_lam_re': 'delta_w', 'delta_s5_lam_im': 'delta_w', 'delta_s5_log_dt': 'delta_w', 'delta_s5_b_re': 'delta_w', 'delta_s5_b_im': 'delta_w', 'delta_s5_c_re': 'delta_w', 'delta_s5_c_im': 'delta_w', 'delta_s5_d': 'delta_w', 'delta_s5_w_glu': 'delta_w', 'delta_fox_b_f': 'delta_w', 'delta_w_out_even': 'delta_w', 'delta_w_in_odd': 'delta_w', 'delta_pool_w': 'delta_w', 'delta_pool_scale': 'delta_w', 'delta_sgu_ln_g': 'delta_w', 'delta_sgu_ln_b': 'delta_w', 'delta_sgu_w_s': 'delta_w', 'delta_sgu_b_s': 'delta_w', 'delta_w_out_odd': 'delta_w', 'delta_mlp_w1': 'delta_w', 'delta_mlp_w2': 'delta_w', 'new_m_mix_pre_g': 'new_m', 'new_m_mix_post_g': 'new_m', 'new_m_mlp_pre_g': 'new_m', 'new_m_mlp_post_g': 'new_m', 'new_m_w_in_even': 'new_m', 'new_m_s5_lam_re': 'new_m', 'new_m_s5_lam_im': 'new_m', 'new_m_s5_log_dt': 'new_m', 'new_m_s5_b_re': 'new_m', 'new_m_s5_b_im': 'new_m', 'new_m_s5_c_re': 'new_m', 'new_m_s5_c_im': 'new_m', 'new_m_s5_d': 'new_m', 'new_m_s5_w_glu': 'new_m', 'new_m_fox_b_f': 'new_m', 'new_m_w_out_even': 'new_m', 'new_m_w_in_odd': 'new_m', 'new_m_pool_w': 'new_m', 'new_m_pool_scale': 'new_m', 'new_m_sgu_ln_g': 'new_m', 'new_m_sgu_ln_b': 'new_m', 'new_m_sgu_w_s': 'new_m', 'new_m_sgu_b_s': 'new_m', 'new_m_w_out_odd': 'new_m', 'new_m_mlp_w1': 'new_m', 'new_m_mlp_w2': 'new_m', 'new_v_mix_pre_g': 'new_v', 'new_v_mix_post_g': 'new_v', 'new_v_mlp_pre_g': 'new_v', 'new_v_mlp_post_g': 'new_v', 'new_v_w_in_even': 'new_v', 'new_v_s5_lam_re': 'new_v', 'new_v_s5_lam_im': 'new_v', 'new_v_s5_log_dt': 'new_v', 'new_v_s5_b_re': 'new_v', 'new_v_s5_b_im': 'new_v', 'new_v_s5_c_re': 'new_v', 'new_v_s5_c_im': 'new_v', 'new_v_s5_d': 'new_v', 'new_v_s5_w_glu': 'new_v', 'new_v_fox_b_f': 'new_v', 'new_v_w_out_even': 'new_v', 'new_v_w_in_odd': 'new_v', 'new_v_pool_w': 'new_v', 'new_v_pool_scale': 'new_v', 'new_v_sgu_ln_g': 'new_v', 'new_v_sgu_ln_b': 'new_v', 'new_v_sgu_w_s': 'new_v', 'new_v_sgu_b_s': 'new_v', 'new_v_w_out_odd': 'new_v', 'new_v_mlp_w1': 'new_v', 'new_v_mlp_w2': 'new_v'}


def _forward(args):
    return _fwd_reference(*[args[k] for k in FWD_PARAMS])


def _output_shape():
    def fwd():
        inp = _fwd_setup_inputs(0)
        return _fwd_reference(*[inp[k] for k in FWD_PARAMS])
    out = _jax.eval_shape(fwd)
    return out.shape, out.dtype

N_MICROBATCH = 1
ADAM_LR = 0.001
ADAM_B1 = 0.9
ADAM_B2 = 0.999
ADAM_EPS = 1e-08
ADAM_WD = 0.01
ADAM_STEP = 10
PER_EXAMPLE_BATCH_AXIS = {'x': 0, 'loss_target': 0}
SHARED_INPUTS = []
_WEIGHT_DTYPES = {'mix_pre_g': _jnp.float32, 'mix_post_g': _jnp.float32, 'mlp_pre_g': _jnp.float32, 'mlp_post_g': _jnp.float32, 'w_in_even': _jnp.float32, 's5_lam_re': _jnp.float32, 's5_lam_im': _jnp.float32, 's5_log_dt': _jnp.float32, 's5_b_re': _jnp.float32, 's5_b_im': _jnp.float32, 's5_c_re': _jnp.float32, 's5_c_im': _jnp.float32, 's5_d': _jnp.float32, 's5_w_glu': _jnp.float32, 'fox_b_f': _jnp.float32, 'w_out_even': _jnp.float32, 'w_in_odd': _jnp.float32, 'pool_w': _jnp.float32, 'pool_scale': _jnp.float32, 'sgu_ln_g': _jnp.float32, 'sgu_ln_b': _jnp.float32, 'sgu_w_s': _jnp.float32, 'sgu_b_s': _jnp.float32, 'w_out_odd': _jnp.float32, 'mlp_w1': _jnp.float32, 'mlp_w2': _jnp.float32}
MOMENT_SCALE = {'mix_pre_g': 2.493807e+00, 'mix_post_g': 3.327302e+01, 'mlp_pre_g': 3.983127e+00, 'mlp_post_g': 3.462774e+01, 'w_in_even': 9.045492e-01, 's5_lam_re': 5.226215e-02, 's5_lam_im': 3.911707e-02, 's5_log_dt': 2.746883e+01, 's5_b_re': 3.452917e-02, 's5_b_im': 3.402524e-02, 's5_c_re': 4.633525e-02, 's5_c_im': 5.090986e-02, 's5_d': 1.066771e+01, 's5_w_glu': 1.701570e+00, 'fox_b_f': 6.802147e+00, 'w_out_even': 7.274733e+00, 'w_in_odd': 2.859024e+00, 'pool_w': 2.019965e+00, 'pool_scale': 1.984818e+00, 'sgu_ln_g': 3.516057e-01, 'sgu_ln_b': 4.763705e-01, 'sgu_w_s': 2.617883e-01, 'sgu_b_s': 6.489216e-01, 'w_out_odd': 6.636540e+00, 'mlp_w1': 1.858760e+00, 'mlp_w2': 8.929107e+00}


def _to_microbatches(a, axis):
    t = _jnp.moveaxis(a, axis, 0)
    t = t.reshape((N_MICROBATCH, t.shape[0] // N_MICROBATCH) + t.shape[1:])
    return _jnp.moveaxis(t, 1, axis + 1)


def setup_inputs(seed: int = 0) -> dict:
    inp = _fwd_setup_inputs(seed)
    key = _jax.random.fold_in(_jax.random.key(seed), 7919)
    shape, _ = _output_shape()
    out = dict(inp)
    out["loss_target"] = _jax.random.normal(_jax.random.fold_in(key, 0), shape, _jnp.float32)
    for i, name in enumerate(TWIN_WEIGHTS):
        w = inp[name].astype(_jnp.float32)
        if MOMENT_SCALE is None:
            s = _jnp.sqrt(_jnp.mean(_jnp.square(w)) + 1e-30)
        else:
            s = MOMENT_SCALE[name]
        km, kv = _jax.random.split(_jax.random.fold_in(key, i + 1))
        out[name] = w
        out["m_" + name] = s * _jax.random.normal(km, w.shape, _jnp.float32)
        out["v_" + name] = (s * s) * _jax.random.uniform(kv, w.shape, _jnp.float32, 0.5, 1.5)
    if N_MICROBATCH > 1:
        for name, axis in PER_EXAMPLE_BATCH_AXIS.items():
            out[name] = _to_microbatches(out[name], axis)
    return {'x': out['x'], 'mix_pre_g': out['mix_pre_g'], 'mix_post_g': out['mix_post_g'], 'mlp_pre_g': out['mlp_pre_g'], 'mlp_post_g': out['mlp_post_g'], 'w_in_even': out['w_in_even'], 's5_lam_re': out['s5_lam_re'], 's5_lam_im': out['s5_lam_im'], 's5_log_dt': out['s5_log_dt'], 's5_b_re': out['s5_b_re'], 's5_b_im': out['s5_b_im'], 's5_c_re': out['s5_c_re'], 's5_c_im': out['s5_c_im'], 's5_d': out['s5_d'], 's5_w_glu': out['s5_w_glu'], 'fox_b_f': out['fox_b_f'], 'w_out_even': out['w_out_even'], 'w_in_odd': out['w_in_odd'], 'pool_w': out['pool_w'], 'pool_scale': out['pool_scale'], 'sgu_ln_g': out['sgu_ln_g'], 'sgu_ln_b': out['sgu_ln_b'], 'sgu_w_s': out['sgu_w_s'], 'sgu_b_s': out['sgu_b_s'], 'w_out_odd': out['w_out_odd'], 'mlp_w1': out['mlp_w1'], 'mlp_w2': out['mlp_w2'], 'loss_target': out['loss_target'], 'm_mix_pre_g': out['m_mix_pre_g'], 'm_mix_post_g': out['m_mix_post_g'], 'm_mlp_pre_g': out['m_mlp_pre_g'], 'm_mlp_post_g': out['m_mlp_post_g'], 'm_w_in_even': out['m_w_in_even'], 'm_s5_lam_re': out['m_s5_lam_re'], 'm_s5_lam_im': out['m_s5_lam_im'], 'm_s5_log_dt': out['m_s5_log_dt'], 'm_s5_b_re': out['m_s5_b_re'], 'm_s5_b_im': out['m_s5_b_im'], 'm_s5_c_re': out['m_s5_c_re'], 'm_s5_c_im': out['m_s5_c_im'], 'm_s5_d': out['m_s5_d'], 'm_s5_w_glu': out['m_s5_w_glu'], 'm_fox_b_f': out['m_fox_b_f'], 'm_w_out_even': out['m_w_out_even'], 'm_w_in_odd': out['m_w_in_odd'], 'm_pool_w': out['m_pool_w'], 'm_pool_scale': out['m_pool_scale'], 'm_sgu_ln_g': out['m_sgu_ln_g'], 'm_sgu_ln_b': out['m_sgu_ln_b'], 'm_sgu_w_s': out['m_sgu_w_s'], 'm_sgu_b_s': out['m_sgu_b_s'], 'm_w_out_odd': out['m_w_out_odd'], 'm_mlp_w1': out['m_mlp_w1'], 'm_mlp_w2': out['m_mlp_w2'], 'v_mix_pre_g': out['v_mix_pre_g'], 'v_mix_post_g': out['v_mix_post_g'], 'v_mlp_pre_g': out['v_mlp_pre_g'], 'v_mlp_post_g': out['v_mlp_post_g'], 'v_w_in_even': out['v_w_in_even'], 'v_s5_lam_re': out['v_s5_lam_re'], 'v_s5_lam_im': out['v_s5_lam_im'], 'v_s5_log_dt': out['v_s5_log_dt'], 'v_s5_b_re': out['v_s5_b_re'], 'v_s5_b_im': out['v_s5_b_im'], 'v_s5_c_re': out['v_s5_c_re'], 'v_s5_c_im': out['v_s5_c_im'], 'v_s5_d': out['v_s5_d'], 'v_s5_w_glu': out['v_s5_w_glu'], 'v_fox_b_f': out['v_fox_b_f'], 'v_w_out_even': out['v_w_out_even'], 'v_w_in_odd': out['v_w_in_odd'], 'v_pool_w': out['v_pool_w'], 'v_pool_scale': out['v_pool_scale'], 'v_sgu_ln_g': out['v_sgu_ln_g'], 'v_sgu_ln_b': out['v_sgu_ln_b'], 'v_sgu_w_s': out['v_sgu_w_s'], 'v_sgu_b_s': out['v_sgu_b_s'], 'v_w_out_odd': out['v_w_out_odd'], 'v_mlp_w1': out['v_mlp_w1'], 'v_mlp_w2': out['v_mlp_w2']}


def _loss(weights, diff, rest, loss_target):
    with _jax.named_scope("forward"):
        args = {**rest, TWIN_DIFF_INPUT: diff, **{k: w.astype(_WEIGHT_DTYPES[k]) for k, w in weights.items()}}
        y = _forward(args)
    with _jax.named_scope("loss_head"):
        err = _jnp.square(y.astype(_jnp.float32) - loss_target)
        return 0.5 * _jnp.sum(_jnp.mean(err, axis=-1)) if err.ndim else 0.5 * err


def _adamw(w, g, m, v):
    m = ADAM_B1 * m + (1.0 - ADAM_B1) * g
    v = ADAM_B2 * v + (1.0 - ADAM_B2) * _jnp.square(g)
    m_hat = m / (1.0 - ADAM_B1 ** ADAM_STEP)
    v_hat = v / (1.0 - ADAM_B2 ** ADAM_STEP)
    delta = -ADAM_LR * (m_hat / (_jnp.sqrt(v_hat) + ADAM_EPS) + ADAM_WD * w)
    return delta, m, v


def reference(x, mix_pre_g, mix_post_g, mlp_pre_g, mlp_post_g, w_in_even, s5_lam_re, s5_lam_im, s5_log_dt, s5_b_re, s5_b_im, s5_c_re, s5_c_im, s5_d, s5_w_glu, fox_b_f, w_out_even, w_in_odd, pool_w, pool_scale, sgu_ln_g, sgu_ln_b, sgu_w_s, sgu_b_s, w_out_odd, mlp_w1, mlp_w2, loss_target, m_mix_pre_g, m_mix_post_g, m_mlp_pre_g, m_mlp_post_g, m_w_in_even, m_s5_lam_re, m_s5_lam_im, m_s5_log_dt, m_s5_b_re, m_s5_b_im, m_s5_c_re, m_s5_c_im, m_s5_d, m_s5_w_glu, m_fox_b_f, m_w_out_even, m_w_in_odd, m_pool_w, m_pool_scale, m_sgu_ln_g, m_sgu_ln_b, m_sgu_w_s, m_sgu_b_s, m_w_out_odd, m_mlp_w1, m_mlp_w2, v_mix_pre_g, v_mix_post_g, v_mlp_pre_g, v_mlp_post_g, v_w_in_even, v_s5_lam_re, v_s5_lam_im, v_s5_log_dt, v_s5_b_re, v_s5_b_im, v_s5_c_re, v_s5_c_im, v_s5_d, v_s5_w_glu, v_fox_b_f, v_w_out_even, v_w_in_odd, v_pool_w, v_pool_scale, v_sgu_ln_g, v_sgu_ln_b, v_sgu_w_s, v_sgu_b_s, v_w_out_odd, v_mlp_w1, v_mlp_w2):
    given = dict(x=x, mix_pre_g=mix_pre_g, mix_post_g=mix_post_g, mlp_pre_g=mlp_pre_g, mlp_post_g=mlp_post_g, w_in_even=w_in_even, s5_lam_re=s5_lam_re, s5_lam_im=s5_lam_im, s5_log_dt=s5_log_dt, s5_b_re=s5_b_re, s5_b_im=s5_b_im, s5_c_re=s5_c_re, s5_c_im=s5_c_im, s5_d=s5_d, s5_w_glu=s5_w_glu, fox_b_f=fox_b_f, w_out_even=w_out_even, w_in_odd=w_in_odd, pool_w=pool_w, pool_scale=pool_scale, sgu_ln_g=sgu_ln_g, sgu_ln_b=sgu_ln_b, sgu_w_s=sgu_w_s, sgu_b_s=sgu_b_s, w_out_odd=w_out_odd, mlp_w1=mlp_w1, mlp_w2=mlp_w2, loss_target=loss_target, m_mix_pre_g=m_mix_pre_g, m_mix_post_g=m_mix_post_g, m_mlp_pre_g=m_mlp_pre_g, m_mlp_post_g=m_mlp_post_g, m_w_in_even=m_w_in_even, m_s5_lam_re=m_s5_lam_re, m_s5_lam_im=m_s5_lam_im, m_s5_log_dt=m_s5_log_dt, m_s5_b_re=m_s5_b_re, m_s5_b_im=m_s5_b_im, m_s5_c_re=m_s5_c_re, m_s5_c_im=m_s5_c_im, m_s5_d=m_s5_d, m_s5_w_glu=m_s5_w_glu, m_fox_b_f=m_fox_b_f, m_w_out_even=m_w_out_even, m_w_in_odd=m_w_in_odd, m_pool_w=m_pool_w, m_pool_scale=m_pool_scale, m_sgu_ln_g=m_sgu_ln_g, m_sgu_ln_b=m_sgu_ln_b, m_sgu_w_s=m_sgu_w_s, m_sgu_b_s=m_sgu_b_s, m_w_out_odd=m_w_out_odd, m_mlp_w1=m_mlp_w1, m_mlp_w2=m_mlp_w2, v_mix_pre_g=v_mix_pre_g, v_mix_post_g=v_mix_post_g, v_mlp_pre_g=v_mlp_pre_g, v_mlp_post_g=v_mlp_post_g, v_w_in_even=v_w_in_even, v_s5_lam_re=v_s5_lam_re, v_s5_lam_im=v_s5_lam_im, v_s5_log_dt=v_s5_log_dt, v_s5_b_re=v_s5_b_re, v_s5_b_im=v_s5_b_im, v_s5_c_re=v_s5_c_re, v_s5_c_im=v_s5_c_im, v_s5_d=v_s5_d, v_s5_w_glu=v_s5_w_glu, v_fox_b_f=v_fox_b_f, v_w_out_even=v_w_out_even, v_w_in_odd=v_w_in_odd, v_pool_w=v_pool_w, v_pool_scale=v_pool_scale, v_sgu_ln_g=v_sgu_ln_g, v_sgu_ln_b=v_sgu_ln_b, v_sgu_w_s=v_sgu_w_s, v_sgu_b_s=v_sgu_b_s, v_w_out_odd=v_w_out_odd, v_mlp_w1=v_mlp_w1, v_mlp_w2=v_mlp_w2)
    weights = {n: given[n] for n in TWIN_WEIGHTS}
    shared = {n: given[n] for n in SHARED_INPUTS}
    per_example = {n: given[n] for n in ['x']}
    grad_fn = _jax.value_and_grad(_loss, argnums=(0, 1))

    def one_microbatch(ex, loss_target):
        ex = dict(ex)
        diff = ex.pop(TWIN_DIFF_INPUT)
        return grad_fn(weights, diff, {**shared, **ex}, loss_target)

    if N_MICROBATCH == 1:
        loss, (grad_w, grad_x) = one_microbatch(per_example, given["loss_target"])
    else:
        def body(carry, xs):
            loss_sum, grad_sum = carry
            l_k, (gw_k, gx_k) = one_microbatch(xs[0], xs[1])
            with _jax.named_scope("update"):
                return (loss_sum + l_k, _jax.tree.map(_jnp.add, grad_sum, gw_k)), gx_k

        init = (_jnp.zeros((), _jnp.float32), _jax.tree.map(_jnp.zeros_like, weights))
        (loss, grad_w), grad_x = _jax.lax.scan(body, init, (per_example, given["loss_target"]))
    with _jax.named_scope("update"):
        delta_w, new_m, new_v = {}, {}, {}
        for n in TWIN_WEIGHTS:
            delta_w[n], new_m[n], new_v[n] = _adamw(weights[n], grad_w[n], given["m_" + n], given["v_" + n])
    return (loss, grad_x, *[grad_w[n] for n in TWIN_WEIGHTS], *[delta_w[n] for n in TWIN_WEIGHTS],
            *[new_m[n] for n in TWIN_WEIGHTS], *[new_v[n] for n in TWIN_WEIGHTS])
```

```python
import functools
import math

import jax
import jax.numpy as jnp
from jax import lax
from jax.experimental import pallas as pl
from jax.experimental.pallas import tpu as pltpu

F32 = jnp.float32
BF16 = jnp.bfloat16
MESH = pl.DeviceIdType.MESH
ANY = pl.BlockSpec(memory_space=pl.ANY)

N_DEV = 8
D_MODEL = 1024
EPS = 1e-6
S5_W = 512
S5_NS = 2048
FOX_W = 512
EVEN_IN = 2056
EVEN_PAD = 2176
ODD_IN = 1536
LANES = 128
VMEM_LIMIT = 56 * 1024 * 1024

ADAM_LR = 0.001
ADAM_B1 = 0.9
ADAM_B2 = 0.999
ADAM_EPS = 1e-08
ADAM_WD = 0.01
ADAM_STEP = 10

NT = (((1,), (1,)), ((), ()))
TN = (((0,), (0,)), ((), ()))
NN = (((1,), (0,)), ((), ()))


def _cp(*sem):
    return pltpu.CompilerParams(dimension_semantics=sem, vmem_limit_bytes=VMEM_LIMIT)


def _sds(shape, dtype=F32):
    return jax.ShapeDtypeStruct(tuple(shape), dtype)


def _gelu(x):
    t = jnp.tanh(0.7978845608028654 * (x + 0.044715 * x * x * x))
    return 0.5 * x * (1.0 + t)


def _gelu_grad(x):
    t = jnp.tanh(0.7978845608028654 * (x + 0.044715 * x * x * x))
    du = 0.7978845608028654 * (1.0 + 3.0 * 0.044715 * x * x)
    return 0.5 * (1.0 + t) + 0.5 * x * (1.0 - t * t) * du


def _sigmoid(x):
    return 1.0 / (1.0 + jnp.exp(-x))


def _dot(a, b, dn=NN):
    return lax.dot_general(a, b, dn, preferred_element_type=F32)


def _mm(a, b, *, name, ta=False, tb=False, b3=False, out3=False, out_dtypes=(F32,), epi=None, extra=(),
        bm=512, bn=1024, bk=1024):
    M = a.shape[1] if ta else a.shape[0]
    K = a.shape[0] if ta else a.shape[1]
    if b3:
        if tb:
            bk = b.shape[2]
            N = b.shape[1]
            assert b.shape[0] * bk == K
        else:
            bn = b.shape[2]
            N = b.shape[0] * bn
            assert b.shape[1] == K
    else:
        N = b.shape[0] if tb else b.shape[1]
    bm, bn, bk = min(bm, M), min(bn, N), min(bk, K)
    assert M % bm == 0 and N % bn == 0 and K % bk == 0, (name, M, N, K, bm, bn, bk)
    nk = K // bk
    n_extra = len(extra)
    n_out = len(out_dtypes)
    dn = (((0 if ta else 1,), (1 if tb else 0,)), ((), ()))

    def body(*refs):
        a_ref, b_ref = refs[0], refs[1]
        e_refs = refs[2:2 + n_extra]
        o_refs = refs[2 + n_extra:2 + n_extra + n_out]
        acc_ref = refs[-1]
        k = pl.program_id(2)

        @pl.when(k == 0)
        def _():
            acc_ref[...] = jnp.zeros_like(acc_ref)

        acc_ref[...] += lax.dot_general(a_ref[...].astype(BF16), b_ref[...].astype(BF16), dn,
                                        preferred_element_type=F32)

        @pl.when(k == nk - 1)
        def _():
            acc = acc_ref[...]
            outs = (acc,) if epi is None else epi(acc, *[e[...] for e in e_refs])
            for o_ref, o in zip(o_refs, outs):
                o_ref[...] = o.astype(o_ref.dtype)

    a_spec = pl.BlockSpec((bk, bm), lambda i, j, k: (k, i)) if ta else pl.BlockSpec((bm, bk), lambda i, j, k: (i, k))
    if b3:
        if tb:
            b_spec = pl.BlockSpec((None, bn, bk), lambda i, j, k: (k, j, 0))
        else:
            b_spec = pl.BlockSpec((None, bk, bn), lambda i, j, k: (j, k, 0))
    else:
        b_spec = pl.BlockSpec((bn, bk), lambda i, j, k: (j, k)) if tb else pl.BlockSpec((bk, bn), lambda i, j, k: (k, j))
    e_specs = [pl.BlockSpec((bm, bn), lambda i, j, k: (i, j)) for _ in extra]
    if out3:
        o_specs = [pl.BlockSpec((None, bm, bn), lambda i, j, k: (j, i, 0)) for _ in out_dtypes]
        o_shapes = [_sds((N // bn, M, bn), dt) for dt in out_dtypes]
    else:
        o_specs = [pl.BlockSpec((bm, bn), lambda i, j, k: (i, j)) for _ in out_dtypes]
        o_shapes = [_sds((M, N), dt) for dt in out_dtypes]
    outs = pl.pallas_call(
        body, name=name, grid=(M // bm, N // bn, nk),
        in_specs=[a_spec, b_spec] + e_specs, out_specs=o_specs, out_shape=o_shapes,
        scratch_shapes=[pltpu.VMEM((bm, bn), F32)],
        compiler_params=_cp("parallel", "parallel", "arbitrary"),
    )(a, b, *extra)
    return outs[0] if n_out == 1 else outs


def _epi_relu2(acc):
    r = jnp.maximum(acc, 0.0)
    return acc, r * r


def _epi_relu2_bwd(acc, p):
    return (acc * (2.0 * jnp.maximum(p.astype(F32), 0.0)),)


def _row_spec(rb, w=D_MODEL):
    return pl.BlockSpec((rb, w), lambda i: (i, 0))


def _vec_spec(w=D_MODEL):
    return pl.BlockSpec((1, w), lambda i: (0, 0))


def _rstd(v):
    return lax.rsqrt(jnp.mean(v * v, axis=-1, keepdims=True) + EPS)


def _rms_fwd(x, g, name):
    L = x.shape[0]
    rb = min(256, L)

    def body(x_ref, g_ref, h_ref, r_ref):
        xv = x_ref[...]
        r = _rstd(xv)
        h_ref[...] = (xv * r * g_ref[...]).astype(BF16)
        r_ref[...] = r

    return pl.pallas_call(
        body, name=name, grid=(L // rb,),
        in_specs=[_row_spec(rb), _vec_spec()],
        out_specs=[_row_spec(rb), _row_spec(rb, 1)],
        out_shape=[_sds((L, D_MODEL), BF16), _sds((L, 1))],
        compiler_params=_cp("parallel"),
    )(x, g)


def _post_pre_fwd(x_in, y, g_post, g_pre, name):
    L = x_in.shape[0]
    rb = min(256, L)

    def body(x_ref, y_ref, gp_ref, gn_ref, xo_ref, ry_ref, h_ref, rx_ref):
        yv = y_ref[...]
        ry = _rstd(yv)
        xo = x_ref[...] + yv * ry * gp_ref[...]
        rx = _rstd(xo)
        xo_ref[...] = xo
        ry_ref[...] = ry
        h_ref[...] = (xo * rx * gn_ref[...]).astype(BF16)
        rx_ref[...] = rx

    return pl.pallas_call(
        body, name=name, grid=(L // rb,),
        in_specs=[_row_spec(rb), _row_spec(rb), _vec_spec(), _vec_spec()],
        out_specs=[_row_spec(rb), _row_spec(rb, 1), _row_spec(rb), _row_spec(rb, 1)],
        out_shape=[_sds((L, D_MODEL)), _sds((L, 1)), _sds((L, D_MODEL), BF16), _sds((L, 1))],
        compiler_params=_cp("parallel"),
    )(x_in, y, g_post, g_pre)


def _post_loss_fwd(x_in, y, g_post, target, name):
    L = x_in.shape[0]
    rb = min(256, L)

    def body(x_ref, y_ref, gp_ref, t_ref, gx_ref, ry_ref, loss_ref):
        i = pl.program_id(0)
        yv = y_ref[...]
        ry = _rstd(yv)
        diff = x_ref[...] + yv * ry * gp_ref[...] - t_ref[...]
        gx_ref[...] = diff * (1.0 / D_MODEL)
        ry_ref[...] = ry

        @pl.when(i == 0)
        def _():
            loss_ref[...] = jnp.zeros_like(loss_ref)

        loss_ref[...] += jnp.sum(diff * diff, keepdims=True)

    return pl.pallas_call(
        body, name=name, grid=(L // rb,),
        in_specs=[_row_spec(rb), _row_spec(rb), _vec_spec(), _row_spec(rb)],
        out_specs=[_row_spec(rb), _row_spec(rb, 1), pl.BlockSpec((1, 1), lambda i: (0, 0))],
        out_shape=[_sds((L, D_MODEL)), _sds((L, 1)), _sds((1, 1))],
        compiler_params=_cp("arbitrary"),
    )(x_in, y, g_post, target)


def _rms_bwd_rows(dy, xv, r, g):
    n = xv * r
    dyg = dy * g
    return r * (dyg - n * jnp.mean(dyg * n, axis=-1, keepdims=True)), n


def _post_bwd(g_out, y, ry, g_post, name):
    L = y.shape[0]
    rb = min(256, L)

    def body(go_ref, y_ref, ry_ref, gp_ref, gy_ref, gg_ref):
        i = pl.program_id(0)
        go = go_ref[...]
        gy, n = _rms_bwd_rows(go, y_ref[...], ry_ref[...], gp_ref[...])
        gy_ref[...] = gy.astype(BF16)

        @pl.when(i == 0)
        def _():
            gg_ref[...] = jnp.zeros_like(gg_ref)

        gg_ref[...] += jnp.sum(go * n, axis=0, keepdims=True)

    return pl.pallas_call(
        body, name=name, grid=(L // rb,),
        in_specs=[_row_spec(rb), _row_spec(rb), _row_spec(rb, 1), _vec_spec()],
        out_specs=[_row_spec(rb), _vec_spec()],
        out_shape=[_sds((L, D_MODEL), BF16), _sds((1, D_MODEL))],
        compiler_params=_cp("arbitrary"),
    )(g_out, y, ry, g_post)


def _pre_post_bwd(g_h, x, rx, g_pre, g_out, y_prev, ry_prev, g_post_prev, name):
    L = x.shape[0]
    rb = min(256, L)

    def body(gh_ref, x_ref, rx_ref, gn_ref, go_ref, y_ref, ry_ref, gp_ref, gi_ref, ggn_ref, gy_ref, ggp_ref):
        i = pl.program_id(0)
        gh = gh_ref[...]
        gx, n = _rms_bwd_rows(gh, x_ref[...], rx_ref[...], gn_ref[...])
        gi = go_ref[...] + gx
        gi_ref[...] = gi
        gy, ny = _rms_bwd_rows(gi, y_ref[...], ry_ref[...], gp_ref[...])
        gy_ref[...] = gy.astype(BF16)

        @pl.when(i == 0)
        def _():
            ggn_ref[...] = jnp.zeros_like(ggn_ref)
            ggp_ref[...] = jnp.zeros_like(ggp_ref)

        ggn_ref[...] += jnp.sum(gh * n, axis=0, keepdims=True)
        ggp_ref[...] += jnp.sum(gi * ny, axis=0, keepdims=True)

    return pl.pallas_call(
        body, name=name, grid=(L // rb,),
        in_specs=[_row_spec(rb), _row_spec(rb), _row_spec(rb, 1), _vec_spec(), _row_spec(rb),
                  _row_spec(rb), _row_spec(rb, 1), _vec_spec()],
        out_specs=[_row_spec(rb), _vec_spec(), _row_spec(rb), _vec_spec()],
        out_shape=[_sds((L, D_MODEL)), _sds((1, D_MODEL)), _sds((L, D_MODEL), BF16), _sds((1, D_MODEL))],
        compiler_params=_cp("arbitrary"),
    )(g_h, x, rx, g_pre, g_out, y_prev, ry_prev, g_post_prev)


def _pre_bwd(g_h, x, rx, g_pre, g_out, name):
    L = x.shape[0]
    rb = min(256, L)

    def body(gh_ref, x_ref, rx_ref, gn_ref, go_ref, gi_ref, ggn_ref):
        i = pl.program_id(0)
        gh = gh_ref[...]
        gx, n = _rms_bwd_rows(gh, x_ref[...], rx_ref[...], gn_ref[...])
        gi_ref[...] = go_ref[...] + gx

        @pl.when(i == 0)
        def _():
            ggn_ref[...] = jnp.zeros_like(ggn_ref)

        ggn_ref[...] += jnp.sum(gh * n, axis=0, keepdims=True)

    return pl.pallas_call(
        body, name=name, grid=(L // rb,),
        in_specs=[_row_spec(rb), _row_spec(rb), _row_spec(rb, 1), _vec_spec(), _row_spec(rb)],
        out_specs=[_row_spec(rb), _vec_spec()],
        out_shape=[_sds((L, D_MODEL)), _sds((1, D_MODEL))],
        compiler_params=_cp("arbitrary"),
    )(g_h, x, rx, g_pre, g_out)


def _cmul(ar, ai, br, bi):
    return ar * br - ai * bi, ar * bi + ai * br


def _zoh_cols(lr, li, ldt):
    dt = jnp.exp(ldt)
    mag = jnp.exp(lr * dt)
    ar = mag * jnp.cos(li * dt)
    ai = mag * jnp.sin(li * dt)
    den = lr * lr + li * li
    nr = ar - 1.0
    qr = (nr * lr + ai * li) / den
    qi = (ai * lr - nr * li) / den
    return dt, ar, ai, qr, qi, den


def _b_mask():
    r = lax.broadcasted_iota(jnp.int32, (S5_NS, LANES), 0)
    c = lax.broadcasted_iota(jnp.int32, (S5_NS, LANES), 1)
    return ((r >> 6) & 7) == (c >> 4)


def _c_mask():
    r = lax.broadcasted_iota(jnp.int32, (S5_W, 512), 0)
    c = lax.broadcasted_iota(jnp.int32, (S5_W, 512), 1)
    return ((r >> 4) & 7) == (c >> 6)


def _s5_prep(lam_r, ldt_r, lam_c, ldt_c, b_t, c_t, name):
    def body(lam_r_ref, ldt_r_ref, lam_c_ref, ldt_c_ref, b_ref, c_ref, tab_ref, bset_ref, cset_ref):
        lr, li = lam_r_ref[0:1, :], lam_r_ref[1:2, :]
        dt = jnp.exp(ldt_r_ref[...])
        mag = jnp.exp(lr * dt)
        p1r, p1i = mag * jnp.cos(li * dt), mag * jnp.sin(li * dt)
        p2r, p2i = _cmul(p1r, p1i, p1r, p1i)
        p3r, p3i = _cmul(p2r, p2i, p1r, p1i)
        p4r, p4i = _cmul(p2r, p2i, p2r, p2i)
        p5r, p5i = _cmul(p4r, p4i, p1r, p1i)
        p6r, p6i = _cmul(p4r, p4i, p2r, p2i)
        p7r, p7i = _cmul(p4r, p4i, p3r, p3i)
        p8r, p8i = _cmul(p4r, p4i, p4r, p4i)
        pw_r = [p1r, p2r, p3r, p4r, p5r, p6r, p7r, p8r]
        pw_i = [p1i, p2i, p3i, p4i, p5i, p6i, p7i, p8i]
        row = lax.broadcasted_iota(jnp.int32, (8, S5_NS), 0)
        zero = jnp.zeros((8, S5_NS), F32)

        def bc(v):
            return jnp.broadcast_to(v, (8, S5_NS))

        for d in range(2):
            sgn = 1.0 if d == 0 else -1.0
            for t, s in enumerate((1, 2, 4)):
                live = (row >= s) if d == 0 else (row <= 7 - s)
                tab_ref[d, 2 * t] = jnp.where(live, bc(pw_r[s - 1]), zero)
                tab_ref[d, 2 * t + 1] = jnp.where(live, bc(sgn * pw_i[s - 1]), zero)
            cr, ci = zero, zero
            for i in range(8):
                e = i if d == 0 else 7 - i
                cr = jnp.where(row == i, bc(pw_r[e]), cr)
                ci = jnp.where(row == i, bc(sgn * pw_i[e]), ci)
            tab_ref[d, 6] = cr
            tab_ref[d, 7] = ci

        _, _, _, qr, qi, _ = _zoh_cols(lam_c_ref[:, 0:1], lam_c_ref[:, 1:2], ldt_c_ref[...])
        bm = _b_mask()
        br, bi = b_ref[0], b_ref[1]
        bset_ref[0] = jnp.where(bm, qr * br - qi * bi, 0.0).astype(BF16)
        bset_ref[1] = jnp.where(bm, qr * bi + qi * br, 0.0).astype(BF16)
        cm = _c_mask()
        cset_ref[0] = jnp.where(cm, c_ref[0], 0.0).astype(BF16)
        cset_ref[1] = jnp.where(cm, c_ref[1], 0.0).astype(BF16)

    vm = pl.BlockSpec(memory_space=pltpu.VMEM)
    return pl.pallas_call(
        body, name=name, in_specs=[vm] * 6, out_specs=[vm] * 3,
        out_shape=[_sds((2, 8, 8, S5_NS)), _sds((2, S5_NS, LANES), BF16), _sds((2, S5_W, 512), BF16)],
        compiler_params=pltpu.CompilerParams(vmem_limit_bytes=VMEM_LIMIT),
    )(lam_r, ldt_r, lam_c, ldt_c, b_t, c_t)


def _s5_bu(z, bset, name):
    L = z.shape[0]
    bl = min(512, L)

    def body(u_ref, b_ref, o_ref):
        u = u_ref[...].astype(BF16)
        o_ref[0] = _dot(u, b_ref[0], NT)
        o_ref[1] = _dot(u, b_ref[1], NT)

    return pl.pallas_call(
        body, name=name, grid=(L // bl, 4),
        in_specs=[pl.BlockSpec((bl, LANES), lambda i, j: (i, j)),
                  pl.BlockSpec((2, 512, LANES), lambda i, j: (0, j, 0))],
        out_specs=pl.BlockSpec((2, bl, 512), lambda i, j: (0, i, j)),
        out_shape=_sds((2, L, S5_NS)),
        compiler_params=_cp("parallel", "parallel"),
    )(z, bset)


def _s5_scan(bu, tabs, name, reverse=False, xs=None):
    L = bu.shape[1]
    nb = L // 8
    d = 1 if reverse else 0
    with_ga = xs is not None

    def body(*refs):
        if with_ga:
            bu_ref, tab_ref, xs_ref, x_ref, ga_ref = refs
        else:
            bu_ref, tab_ref, x_ref = refs
        tabs_v = [tab_ref[t] for t in range(8)]
        row = lax.broadcasted_iota(jnp.int32, (8, LANES), 0)
        zero = jnp.zeros((8, LANES), F32)

        def step(i, carry):
            cr, ci = carry[0], carry[1]
            b = (nb - 1 - i) if reverse else i
            off = pl.multiple_of(b * 8, 8)
            yr = bu_ref[0, pl.ds(off, 8), :]
            yi = bu_ref[1, pl.ds(off, 8), :]
            for t, s in enumerate((1, 2, 4)):
                sh = (8 - s) if reverse else s
                sr = pltpu.roll(yr, sh, 0)
                si = pltpu.roll(yi, sh, 0)
                mr, mi = tabs_v[2 * t], tabs_v[2 * t + 1]
                yr, yi = yr + mr * sr - mi * si, yi + mr * si + mi * sr
            pr, pi = tabs_v[6], tabs_v[7]
            yr, yi = yr + pr * cr - pi * ci, yi + pr * ci + pi * cr
            x_ref[0, pl.ds(off, 8), :] = yr
            x_ref[1, pl.ds(off, 8), :] = yi
            last = 0 if reverse else 7
            out = (jnp.broadcast_to(yr[last:last + 1, :], (8, LANES)),
                   jnp.broadcast_to(yi[last:last + 1, :], (8, LANES)))
            if with_ga:
                xr = xs_ref[0, pl.ds(off, 8), :]
                xi = xs_ref[1, pl.ds(off, 8), :]
                poff = pl.multiple_of(jnp.maximum(b - 1, 0) * 8, 8)
                live = (b > 0).astype(F32)
                pvr = jnp.broadcast_to(xs_ref[0, pl.ds(poff, 8), :][7:8, :], (8, LANES)) * live
                pvi = jnp.broadcast_to(xs_ref[1, pl.ds(poff, 8), :][7:8, :], (8, LANES)) * live
                xpr = jnp.where(row == 0, pvr, pltpu.roll(xr, 1, 0))
                xpi = jnp.where(row == 0, pvi, pltpu.roll(xi, 1, 0))
                out = out + (carry[2] + xpr * yr + xpi * yi, carry[3] + xpr * yi - xpi * yr)
            return out

        init = (zero, zero, zero, zero) if with_ga else (zero, zero)
        fin = lax.fori_loop(0, nb, step, init)
        if with_ga:
            ga_ref[0:1, :] = jnp.sum(fin[2], axis=0, keepdims=True)
            ga_ref[1:2, :] = jnp.sum(fin[3], axis=0, keepdims=True)

    seq = pl.BlockSpec((2, L, LANES), lambda g: (0, 0, g))
    in_specs = [seq, pl.BlockSpec((None, 8, 8, LANES), lambda g: (d, 0, 0, g))]
    out_specs = [seq]
    out_shape = [_sds((2, L, S5_NS))]
    args = [bu, tabs]
    if with_ga:
        in_specs.append(seq)
        args.append(xs)
        out_specs.append(pl.BlockSpec((2, LANES), lambda g: (0, g)))
        out_shape.append(_sds((2, S5_NS)))
    outs = pl.pallas_call(
        body, name=name, grid=(S5_NS // LANES,), in_specs=in_specs, out_specs=out_specs, out_shape=out_shape,
        compiler_params=_cp("parallel"),
    )(*args)
    return outs if with_ga else outs[0]


def _s5_out_fwd(xs, cset, z, dvec, wglu, name):
    L = z.shape[0]
    bl = min(256, L)

    def body(x_ref, c_ref, u_ref, d_ref, w_ref, ylin_ref, ya_ref):
        cols = []
        for j in range(4):
            xr = x_ref[0, :, 512 * j:512 * (j + 1)].astype(BF16)
            xi = x_ref[1, :, 512 * j:512 * (j + 1)].astype(BF16)
            cr = c_ref[0, LANES * j:LANES * (j + 1), :]
            ci = c_ref[1, LANES * j:LANES * (j + 1), :]
            cols.append(_dot(xr, cr, NT) - _dot(xi, ci, NT))
        ylin = jnp.concatenate(cols, axis=1) + d_ref[...] * u_ref[...]
        yg = _gelu(ylin)
        t = _dot(yg.astype(BF16), w_ref[...])
        ylin_ref[...] = ylin
        ya_ref[...] = (yg * _sigmoid(t)).astype(BF16)

    return pl.pallas_call(
        body, name=name, grid=(L // bl,),
        in_specs=[pl.BlockSpec((2, bl, S5_NS), lambda i: (0, i, 0)),
                  pl.BlockSpec((2, S5_W, 512), lambda i: (0, 0, 0)),
                  pl.BlockSpec((bl, S5_W), lambda i: (i, 0)),
                  pl.BlockSpec((1, S5_W), lambda i: (0, 0)),
                  pl.BlockSpec((S5_W, S5_W), lambda i: (0, 0))],
        out_specs=[pl.BlockSpec((bl, S5_W), lambda i: (i, 0))] * 2,
        out_shape=[_sds((L, S5_W)), _sds((L, S5_W), BF16)],
        compiler_params=_cp("parallel"),
    )(xs, cset, z, dvec, wglu)


def _s5_glu_bwd(g_m, ylin, z, dvec, wglu, name):
    L = z.shape[0]
    bl = min(256, L)

    def body(g_ref, ylin_ref, u_ref, d_ref, w_ref, gyl_ref, gud_ref, gw_ref, gd_ref):
        i = pl.program_id(0)
        ylin = ylin_ref[...]
        yg = _gelu(ylin)
        ygb = yg.astype(BF16)
        sg = _sigmoid(_dot(ygb, w_ref[...]))
        gya = g_ref[...]
        gt = gya * yg * sg * (1.0 - sg)
        gtb = gt.astype(BF16)
        gyg = gya * sg + _dot(gtb, w_ref[...], NT)
        gyl = gyg * _gelu_grad(ylin)
        gyl_ref[...] = gyl
        gud_ref[...] = gyl * d_ref[...]

        @pl.when(i == 0)
        def _():
            gw_ref[...] = jnp.zeros_like(gw_ref)
            gd_ref[...] = jnp.zeros_like(gd_ref)

        gw_ref[...] += _dot(ygb, gtb, TN)
        gd_ref[...] += jnp.sum(gyl * u_ref[...], axis=0, keepdims=True)

    blk = pl.BlockSpec((bl, S5_W), lambda i: (i, 0))
    return pl.pallas_call(
        body, name=name, grid=(L // bl,),
        in_specs=[blk, blk, blk, pl.BlockSpec((1, S5_W), lambda i: (0, 0)),
                  pl.BlockSpec((S5_W, S5_W), lambda i: (0, 0))],
        out_specs=[blk, blk, pl.BlockSpec((S5_W, S5_W), lambda i: (0, 0)), pl.BlockSpec((1, S5_W), lambda i: (0, 0))],
        out_shape=[_sds((L, S5_W)), _sds((L, S5_W)), _sds((S5_W, S5_W)), _sds((1, S5_W))],
        compiler_params=_cp("arbitrary"),
    )(g_m, ylin, z, dvec, wglu)


def _s5_c_bwd(gyl, xs, cset, name):
    L = gyl.shape[0]
    bl = min(256, L)

    def body(g_ref, x_ref, c_ref, gx_ref, gc_ref):
        i = pl.program_id(0)

        @pl.when(i == 0)
        def _():
            gc_ref[...] = jnp.zeros_like(gc_ref)

        for j in range(4):
            gj = g_ref[:, LANES * j:LANES * (j + 1)].astype(BF16)
            cr = c_ref[0, LANES * j:LANES * (j + 1), :]
            ci = c_ref[1, LANES * j:LANES * (j + 1), :]
            gx_ref[0, :, 512 * j:512 * (j + 1)] = _dot(gj, cr)
            gx_ref[1, :, 512 * j:512 * (j + 1)] = -_dot(gj, ci)
            xr = x_ref[0, :, 512 * j:512 * (j + 1)].astype(BF16)
            xi = x_ref[1, :, 512 * j:512 * (j + 1)].astype(BF16)
            gc_ref[0, LANES * j:LANES * (j + 1), :] += _dot(gj, xr, TN)
            gc_ref[1, LANES * j:LANES * (j + 1), :] -= _dot(gj, xi, TN)

    return pl.pallas_call(
        body, name=name, grid=(L // bl,),
        in_specs=[pl.BlockSpec((bl, S5_W), lambda i: (i, 0)),
                  pl.BlockSpec((2, bl, S5_NS), lambda i: (0, i, 0)),
                  pl.BlockSpec((2, S5_W, 512), lambda i: (0, 0, 0))],
        out_specs=[pl.BlockSpec((2, bl, S5_NS), lambda i: (0, i, 0)),
                   pl.BlockSpec((2, S5_W, 512), lambda i: (0, 0, 0))],
        out_shape=[_sds((2, L, S5_NS)), _sds((2, S5_W, 512))],
        compiler_params=_cp("arbitrary"),
    )(gyl, xs, cset)


def _s5_bu_bwd(gx, bset, z, gud, name):
    L = z.shape[0]
    bl = min(512, L)

    def body(gx_ref, b_ref, u_ref, gud_ref, gu_ref, gb_ref):
        i = pl.program_id(1)
        gr = gx_ref[0].astype(BF16)
        gi = gx_ref[1].astype(BF16)
        gu_ref[...] = gud_ref[...] + _dot(gr, b_ref[0]) + _dot(gi, b_ref[1])

        @pl.when(i == 0)
        def _():
            gb_ref[...] = jnp.zeros_like(gb_ref)

        u = u_ref[...].astype(BF16)
        gb_ref[0] += _dot(gr, u, TN)
        gb_ref[1] += _dot(gi, u, TN)

    return pl.pallas_call(
        body, name=name, grid=(4, L // bl),
        in_specs=[pl.BlockSpec((2, bl, 512), lambda j, i: (0, i, j)),
                  pl.BlockSpec((2, 512, LANES), lambda j, i: (0, j, 0)),
                  pl.BlockSpec((bl, LANES), lambda j, i: (i, j)),
                  pl.BlockSpec((bl, LANES), lambda j, i: (i, j))],
        out_specs=[pl.BlockSpec((bl, LANES), lambda j, i: (i, j)),
                   pl.BlockSpec((2, 512, LANES), lambda j, i: (0, j, 0))],
        out_shape=[_sds((L, S5_W)), _sds((2, S5_NS, LANES))],
        compiler_params=_cp("parallel", "arbitrary"),
    )(gx, bset, z, gud)


def _s5_param_bwd(lam_c, ldt_c, b_t, gb, ga_c, gc, name):
    def body(lam_ref, ldt_ref, b_ref, gb_ref, ga_ref, gc_ref, glam_ref, gldt_ref, gbo_ref, gco_ref):
        lr, li = lam_ref[:, 0:1], lam_ref[:, 1:2]
        dt, ar, ai, qr, qi, den = _zoh_cols(lr, li, ldt_ref[...])
        bm = _b_mask()
        gbr = jnp.where(bm, gb_ref[0], 0.0)
        gbi = jnp.where(bm, gb_ref[1], 0.0)
        br, bi = b_ref[0], b_ref[1]
        obr = gbr * qr + gbi * qi
        obi = gbi * qr - gbr * qi
        gqr = jnp.sum(gbr * br + gbi * bi, axis=1, keepdims=True)
        gqi = jnp.sum(gbi * br - gbr * bi, axis=1, keepdims=True)
        for s in (64, 32, 16):
            obr = obr + pltpu.roll(obr, s, 1)
            obi = obi + pltpu.roll(obi, s, 1)
        gbo_ref[0] = obr
        gbo_ref[1] = obi
        gar = ga_ref[:, 0:1] + (gqr * lr - gqi * li) / den
        gai = ga_ref[:, 1:2] + (gqr * li + gqi * lr) / den
        qlr = (qr * lr + qi * li) / den
        qli = (qi * lr - qr * li) / den
        glr = -(gqr * qlr + gqi * qli)
        gli = -(gqi * qlr - gqr * qli)
        glr = glr + dt * (gar * ar + gai * ai)
        gli = gli + dt * (gai * ar - gar * ai)
        wr, wi = _cmul(lr, li, ar, ai)
        gldt = (gar * wr + gai * wi) * dt
        glam_ref[:, 0:1] = glr
        glam_ref[:, 1:2] = gli
        r = lax.broadcasted_iota(jnp.int32, (S5_NS, 32), 0)
        c = lax.broadcasted_iota(jnp.int32, (S5_NS, 32), 1)
        gldt_ref[...] = jnp.sum(jnp.where((r >> 6) == c, gldt, 0.0), axis=0, keepdims=True)
        cm = _c_mask()
        for k in range(2):
            oc = jnp.where(cm, gc_ref[k], 0.0)
            for s in (256, 128, 64):
                oc = oc + pltpu.roll(oc, s, 1)
            gco_ref[k] = oc[:, 0:LANES]

    vm = pl.BlockSpec(memory_space=pltpu.VMEM)
    return pl.pallas_call(
        body, name=name, in_specs=[vm] * 6, out_specs=[vm] * 4,
        out_shape=[_sds((S5_NS, 2)), _sds((1, 32)), _sds((2, S5_NS, LANES)), _sds((2, S5_W, LANES))],
        compiler_params=pltpu.CompilerParams(vmem_limit_bytes=VMEM_LIMIT),
    )(lam_c, ldt_c, b_t, gb, ga_c, gc)


FL_BLK = EVEN_PAD // LANES - 1
Q_BLK, K_BLK, V_BLK = 4, 8, 12
NEG = -1e30


def _log_sigmoid(v):
    return jnp.minimum(v, 0.0) - jnp.log(1.0 + jnp.exp(-jnp.abs(v)))


def _fox_f_fwd(z, bf, name):
    L = z.shape[0]
    tl = min(256, L)

    def body(fl_ref, b_ref, f_ref, carry_ref):
        i = pl.program_id(0)

        @pl.when(i == 0)
        def _():
            carry_ref[...] = jnp.zeros_like(carry_ref)

        lf = _log_sigmoid(fl_ref[...] + b_ref[...])
        r = lax.broadcasted_iota(jnp.int32, (tl, tl), 0)
        c = lax.broadcasted_iota(jnp.int32, (tl, tl), 1)
        tri = (r >= c).astype(F32)
        cs = lax.dot_general(tri, lf, NN, precision=lax.Precision.HIGHEST, preferred_element_type=F32) + carry_ref[...]
        f_ref[...] = cs
        carry_ref[...] = cs[tl - 1:tl, :]

    return pl.pallas_call(
        body, name=name, grid=(L // tl,),
        in_specs=[pl.BlockSpec((tl, LANES), lambda i: (i, FL_BLK)), pl.BlockSpec((1, LANES), lambda i: (0, 0))],
        out_specs=pl.BlockSpec((tl, LANES), lambda i: (i, 0)),
        out_shape=_sds((L, LANES)),
        scratch_shapes=[pltpu.VMEM((1, LANES), F32)],
        compiler_params=_cp("arbitrary"),
    )(z, bf)


def _fox_f_bwd(dFk, dFq, z, bf, name):
    L = z.shape[0]
    tl = min(256, L)
    nb = L // tl

    def body(dfk_ref, dfq_ref, fl_ref, b_ref, dfl_ref, db_ref, carry_ref):
        i = pl.program_id(0)

        @pl.when(i == 0)
        def _():
            carry_ref[...] = jnp.zeros_like(carry_ref)
            db_ref[...] = jnp.zeros_like(db_ref)

        r = lax.broadcasted_iota(jnp.int32, (tl, tl), 0)
        c = lax.broadcasted_iota(jnp.int32, (tl, tl), 1)
        tri = (r <= c).astype(F32)
        cs = lax.dot_general(tri, dfk_ref[...] + dfq_ref[...], NN, precision=lax.Precision.HIGHEST,
                             preferred_element_type=F32) + carry_ref[...]
        carry_ref[...] = cs[0:1, :]
        dfl = cs * _sigmoid(-(fl_ref[...] + b_ref[...]))
        dfl_ref[...] = dfl
        db_ref[...] += jnp.sum(dfl, axis=0, keepdims=True)

    return pl.pallas_call(
        body, name=name, grid=(nb,),
        in_specs=[pl.BlockSpec((tl, LANES), lambda i: (nb - 1 - i, 0)),
                  pl.BlockSpec((tl, LANES), lambda i: (nb - 1 - i, 0)),
                  pl.BlockSpec((tl, LANES), lambda i: (nb - 1 - i, FL_BLK)),
                  pl.BlockSpec((1, LANES), lambda i: (0, 0))],
        out_specs=[pl.BlockSpec((tl, LANES), lambda i: (nb - 1 - i, 0)), pl.BlockSpec((1, LANES), lambda i: (0, 0))],
        out_shape=[_sds((L, LANES)), _sds((1, LANES))],
        scratch_shapes=[pltpu.VMEM((1, LANES), F32)],
        compiler_params=_cp("arbitrary"),
    )(dFk, dFq, z, bf)


def _head_mask(hh):
    lane = lax.broadcasted_iota(jnp.int32, (1, LANES), 1)
    return (lane >> 6) == hh


def _fox_scores(q_scaled, k, fq_ref, fr_ref, hh, causal):
    qh = jnp.where(_head_mask(hh), q_scaled, 0.0).astype(BF16)
    s = _dot(qh, k, NT) + (fq_ref[:, 64 * hh:64 * hh + 1] - fr_ref[hh:hh + 1, :])
    return jnp.where(causal, s, NEG)


def _causal(qi, ki, T):
    r = qi * T + lax.broadcasted_iota(jnp.int32, (T, T), 0)
    c = ki * T + lax.broadcasted_iota(jnp.int32, (T, T), 1)
    return c <= r


def _fox_fwd(z, fq, frow, name):
    L = z.shape[0]
    T = min(512, L)
    nq = L // T

    def body(q_ref, k_ref, v_ref, fq_ref, fr_ref, o_ref, lse_ref, m_ref, l_ref, acc_ref):
        qi, ki = pl.program_id(1), pl.program_id(2)

        @pl.when(ki == 0)
        def _():
            m_ref[...] = jnp.full_like(m_ref, NEG)
            l_ref[...] = jnp.zeros_like(l_ref)
            acc_ref[...] = jnp.zeros_like(acc_ref)

        @pl.when(ki <= qi)
        def _():
            q = q_ref[...] * 0.125
            k = k_ref[...].astype(BF16)
            v = v_ref[...].astype(BF16)
            causal = _causal(qi, ki, T)
            for hh in range(2):
                s = _fox_scores(q, k, fq_ref, fr_ref, hh, causal)
                m_old = m_ref[hh]
                m_new = jnp.maximum(m_old, jnp.max(s, axis=1, keepdims=True))
                alpha = jnp.exp(m_old - m_new)
                p = jnp.exp(s - m_new)
                l_ref[hh] = alpha * l_ref[hh] + jnp.sum(p, axis=1, keepdims=True)
                m_ref[hh] = m_new
                pv = _dot(p.astype(BF16), v)
                acc = acc_ref[...]
                acc_ref[...] = jnp.where(_head_mask(hh), alpha * acc + pv, acc)

        @pl.when(ki == nq - 1)
        def _():
            h0 = _head_mask(0)
            inv = jnp.where(h0, 1.0 / l_ref[0], 1.0 / l_ref[1])
            o_ref[...] = acc_ref[...] * inv
            lse_ref[...] = jnp.where(h0, m_ref[0] + jnp.log(l_ref[0]), m_ref[1] + jnp.log(l_ref[1]))

    def zspec(base, rowf):
        return pl.BlockSpec((T, LANES), lambda j, qi, ki: (rowf(qi, ki), base + j))

    kv_row = lambda qi, ki: jnp.minimum(ki, qi)
    q_row = lambda qi, ki: qi
    return pl.pallas_call(
        body, name=name, grid=(4, nq, nq),
        in_specs=[zspec(Q_BLK, q_row), zspec(K_BLK, kv_row), zspec(V_BLK, kv_row),
                  pl.BlockSpec((T, LANES), lambda j, qi, ki: (qi, j)),
                  pl.BlockSpec((None, 2, T), lambda j, qi, ki: (j, 0, jnp.minimum(ki, qi)))],
        out_specs=[pl.BlockSpec((T, LANES), lambda j, qi, ki: (qi, j))] * 2,
        out_shape=[_sds((L, FOX_W)), _sds((L, FOX_W))],
        scratch_shapes=[pltpu.VMEM((2, T, 1), F32), pltpu.VMEM((2, T, 1), F32), pltpu.VMEM((T, LANES), F32)],
        compiler_params=_cp("parallel", "parallel", "arbitrary"),
    )(z, z, z, fq, frow)


def _fox_delta(dob, o, hh):
    return jnp.sum(jnp.where(_head_mask(hh), dob.astype(F32) * o, 0.0), axis=1, keepdims=True)


def _fox_bwd_dq(z, fq, frow, o, lse, g_m, name):
    L = z.shape[0]
    T = min(512, L)
    nq = L // T

    def body(q_ref, k_ref, v_ref, fq_ref, fr_ref, o_ref, lse_ref, do_ref, dq_ref, df_ref, acc_ref, rs_ref):
        qi, ki = pl.program_id(1), pl.program_id(2)

        @pl.when(ki == 0)
        def _():
            acc_ref[...] = jnp.zeros_like(acc_ref)
            rs_ref[...] = jnp.zeros_like(rs_ref)

        @pl.when(ki <= qi)
        def _():
            q = q_ref[...] * 0.125
            k = k_ref[...].astype(BF16)
            v = v_ref[...].astype(BF16)
            dob = do_ref[...].astype(BF16)
            causal = _causal(qi, ki, T)
            for hh in range(2):
                s = _fox_scores(q, k, fq_ref, fr_ref, hh, causal)
                p = jnp.exp(s - lse_ref[:, 64 * hh:64 * hh + 1])
                dp = _dot(jnp.where(_head_mask(hh), dob, 0.0), v, NT)
                ds = p * (dp - _fox_delta(dob, o_ref[...], hh))
                rs_ref[hh] += jnp.sum(ds, axis=1, keepdims=True)
                acc = acc_ref[...]
                acc_ref[...] = jnp.where(_head_mask(hh), acc + _dot(ds.astype(BF16), k), acc)

        @pl.when(ki == nq - 1)
        def _():
            dq_ref[...] = acc_ref[...] * 0.125
            df_ref[...] = jnp.where(_head_mask(0), rs_ref[0], rs_ref[1])

    def zspec(base, rowf):
        return pl.BlockSpec((T, LANES), lambda j, qi, ki: (rowf(qi, ki), base + j))

    kv_row = lambda qi, ki: jnp.minimum(ki, qi)
    q_row = lambda qi, ki: qi
    qblk = pl.BlockSpec((T, LANES), lambda j, qi, ki: (qi, j))
    return pl.pallas_call(
        body, name=name, grid=(4, nq, nq),
        in_specs=[zspec(Q_BLK, q_row), zspec(K_BLK, kv_row), zspec(V_BLK, kv_row), qblk,
                  pl.BlockSpec((None, 2, T), lambda j, qi, ki: (j, 0, jnp.minimum(ki, qi))),
                  qblk, qblk, pl.BlockSpec((T, LANES), lambda j, qi, ki: (qi, 4 + j))],
        out_specs=[qblk, qblk],
        out_shape=[_sds((L, FOX_W)), _sds((L, FOX_W))],
        scratch_shapes=[pltpu.VMEM((T, LANES), F32), pltpu.VMEM((2, T, 1), F32)],
        compiler_params=_cp("parallel", "parallel", "arbitrary"),
    )(z, z, z, fq, frow, o, lse, g_m)


def _fox_bwd_dkv(z, fq, frow, o, lse, g_m, name):
    L = z.shape[0]
    T = min(512, L)
    nq = L // T

    def body(q_ref, k_ref, v_ref, fq_ref, fr_ref, o_ref, lse_ref, do_ref, dk_ref, dv_ref, df_ref,
             dk_acc, dv_acc, df_acc):
        ki, qi = pl.program_id(1), pl.program_id(2)

        @pl.when(qi == 0)
        def _():
            dk_acc[...] = jnp.zeros_like(dk_acc)
            dv_acc[...] = jnp.zeros_like(dv_acc)
            df_acc[...] = jnp.zeros_like(df_acc)

        @pl.when(qi >= ki)
        def _():
            q = q_ref[...] * 0.125
            qb = q.astype(BF16)
            k = k_ref[...].astype(BF16)
            v = v_ref[...].astype(BF16)
            dob = do_ref[...].astype(BF16)
            causal = _causal(qi, ki, T)
            for hh in range(2):
                hm = _head_mask(hh)
                s = _fox_scores(q, k, fq_ref, fr_ref, hh, causal)
                p = jnp.exp(s - lse_ref[:, 64 * hh:64 * hh + 1])
                dp = _dot(jnp.where(hm, dob, 0.0), v, NT)
                ds = p * (dp - _fox_delta(dob, o_ref[...], hh))
                dv = dv_acc[...]
                dv_acc[...] = jnp.where(hm, dv + _dot(p.astype(BF16), dob, TN), dv)
                dk = dk_acc[...]
                dk_acc[...] = jnp.where(hm, dk + _dot(ds.astype(BF16), qb, TN), dk)
                df_acc[hh:hh + 1, :] -= jnp.sum(ds, axis=0, keepdims=True)

        @pl.when(qi == nq - 1)
        def _():
            dk_ref[...] = dk_acc[...]
            dv_ref[...] = dv_acc[...]
            df_ref[...] = df_acc[...]

    q_row = lambda ki, qi: jnp.maximum(qi, ki)

    def qside(base):
        return pl.BlockSpec((T, LANES), lambda j, ki, qi: (q_row(ki, qi), base + j))

    def kside(base):
        return pl.BlockSpec((T, LANES), lambda j, ki, qi: (ki, base + j))

    kblk = pl.BlockSpec((T, LANES), lambda j, ki, qi: (ki, j))
    frow_spec = pl.BlockSpec((None, 2, T), lambda j, ki, qi: (j, 0, ki))
    return pl.pallas_call(
        body, name=name, grid=(4, nq, nq),
        in_specs=[qside(Q_BLK), kside(K_BLK), kside(V_BLK), qside(0), frow_spec, qside(0), qside(0), qside(4)],
        out_specs=[kblk, kblk, frow_spec],
        out_shape=[_sds((L, FOX_W)), _sds((L, FOX_W)), _sds((4, 2, L))],
        scratch_shapes=[pltpu.VMEM((T, LANES), F32), pltpu.VMEM((T, LANES), F32), pltpu.VMEM((2, T), F32)],
        compiler_params=_cp("parallel", "parallel", "arbitrary"),
    )(z, z, z, fq, frow, o, lse, g_m)


def _shift_rows(v, s, down, row):
    n = v.shape[0]
    if down:
        return jnp.where(row >= s, pltpu.roll(v, s, 0), 0.0)
    return jnp.where(row < n - s, pltpu.roll(v, n - s, 0), 0.0)


def _window_sum(v, g, down, row):
    out = jnp.zeros_like(v)
    s = v
    for k in range(4):
        s = s + _shift_rows(s, 1 << k, down, row)
        out = jnp.where(g == k, s, out)
    return out


def _pool_inv_cnt(g, row):
    w = jnp.left_shift(2, g).astype(F32)
    return 1.0 / jnp.minimum(row.astype(F32) + 1.0, w)


def _pool_fwd(z, pool_w, scale, name):
    L = z.shape[0]

    def body(x_ref, w_ref, s_ref, y_ref, p_ref):
        g = pl.program_id(0)
        row = lax.broadcasted_iota(jnp.int32, (L, LANES), 0)
        x = x_ref[...]
        pooled = (_window_sum(x, g, True, row) * _pool_inv_cnt(g, row) - x).astype(BF16)
        p_ref[...] = pooled
        y_ref[...] = (_dot(pooled, w_ref[...].astype(BF16)) * s_ref[...]).astype(BF16)

    col = pl.BlockSpec((L, LANES), lambda g: (0, g))
    return pl.pallas_call(
        body, name=name, grid=(4,),
        in_specs=[col, pl.BlockSpec((None, LANES, LANES), lambda g: (g, 0, 0)), pl.BlockSpec((1, LANES), lambda g: (0, g))],
        out_specs=[col, col],
        out_shape=[_sds((L, 512), BF16), _sds((L, 512), BF16)],
        compiler_params=_cp("parallel"),
    )(z, pool_w, scale)


def _pool_bwd(g_m, pooled, pool_w, scale, name):
    L = g_m.shape[0]

    def body(g_ref, p_ref, w_ref, s_ref, gx_ref, gw_ref, gs_ref):
        g = pl.program_id(0)
        row = lax.broadcasted_iota(jnp.int32, (L, LANES), 0)
        gy = g_ref[...]
        pooled = p_ref[...]
        wb = w_ref[...].astype(BF16)
        lin = _dot(pooled, wb)
        gs_ref[...] = jnp.sum(gy * lin, axis=0, keepdims=True)
        glin = (gy * s_ref[...]).astype(BF16)
        gw_ref[...] = _dot(pooled, glin, TN)
        gp = _dot(glin, wb, NT)
        gx_ref[...] = _window_sum(gp * _pool_inv_cnt(g, row), g, False, row) - gp

    col = pl.BlockSpec((L, LANES), lambda g: (0, g))
    wspec = pl.BlockSpec((None, LANES, LANES), lambda g: (g, 0, 0))
    vec = pl.BlockSpec((1, LANES), lambda g: (0, g))
    return pl.pallas_call(
        body, name=name, grid=(4,),
        in_specs=[col, col, wspec, vec],
        out_specs=[col, wspec, vec],
        out_shape=[_sds((L, 512)), _sds((4, LANES, LANES)), _sds((1, 512))],
        compiler_params=_cp("parallel"),
    )(g_m, pooled, pool_w, scale)


SGU_CHUNKS = 4


def _sgu_ln(v, gam, bet):
    gv = _gelu(v)
    mu = jnp.mean(gv, axis=-1, keepdims=True)
    xc = gv - mu
    rs = lax.rsqrt(jnp.mean(xc * xc, axis=-1, keepdims=True) + EPS)
    xh = xc * rs
    return xh, rs, xh * gam + bet


def _tril_ws(w_ref, g):
    r = lax.broadcasted_iota(jnp.int32, (LANES, LANES), 0)
    c = lax.broadcasted_iota(jnp.int32, (LANES, LANES), 1)
    return jnp.where(r >= c, w_ref[g], 0.0).astype(BF16)


def _sgu_fwd(z, ln_g, ln_b, w_s, b_st, name):
    L = z.shape[0]
    rb = min(SGU_CHUNKS * LANES, L)

    def body(u_ref, v_ref, g_ref, b_ref, w_ref, bs_ref, y_ref):
        _, _, vln = _sgu_ln(v_ref[...], g_ref[...], b_ref[...])
        gu = _gelu(u_ref[...])
        vb = vln.astype(BF16)
        for g in range(4):
            ws = _tril_ws(w_ref, g)
            for n in range(rb // LANES):
                rows = slice(n * LANES, (n + 1) * LANES)
                cols = slice(g * LANES, (g + 1) * LANES)
                mixed = _dot(ws, vb[rows, cols]) + bs_ref[:, g:g + 1]
                y_ref[rows, cols] = (gu[rows, cols] * mixed).astype(BF16)

    vm = lambda shape: pl.BlockSpec(shape, lambda i: tuple(0 for _ in shape))
    return pl.pallas_call(
        body, name=name, grid=(L // rb,),
        in_specs=[pl.BlockSpec((rb, 512), lambda i: (i, 1)), pl.BlockSpec((rb, 512), lambda i: (i, 2)),
                  vm((1, 512)), vm((1, 512)), vm((4, LANES, LANES)), vm((LANES, 4))],
        out_specs=pl.BlockSpec((rb, 512), lambda i: (i, 0)),
        out_shape=_sds((L, 512), BF16),
        compiler_params=_cp("parallel"),
    )(z, z, ln_g, ln_b, w_s, b_st)


def _sgu_bwd(g_m, z, ln_g, ln_b, w_s, b_st, name):
    L = z.shape[0]
    rb = min(SGU_CHUNKS * LANES, L)

    def body(gy_ref, u_ref, v_ref, g_ref, b_ref, w_ref, bs_ref, gu_ref, gv_ref, gw_ref, gbs_ref, gg_ref, gb_ref):
        i = pl.program_id(0)

        @pl.when(i == 0)
        def _():
            gw_ref[...] = jnp.zeros_like(gw_ref)
            gbs_ref[...] = jnp.zeros_like(gbs_ref)
            gg_ref[...] = jnp.zeros_like(gg_ref)
            gb_ref[...] = jnp.zeros_like(gb_ref)

        v = v_ref[...]
        u = u_ref[...]
        gy = gy_ref[...]
        xh, rs, vln = _sgu_ln(v, g_ref[...], b_ref[...])
        gel_u = _gelu(u)
        gmix = gy * gel_u
        vb = vln.astype(BF16)
        gmb = gmix.astype(BF16)
        r = lax.broadcasted_iota(jnp.int32, (LANES, LANES), 0)
        c = lax.broadcasted_iota(jnp.int32, (LANES, LANES), 1)
        gvln_cols = []
        for g in range(4):
            ws = _tril_ws(w_ref, g)
            cols = slice(g * LANES, (g + 1) * LANES)
            gw = jnp.zeros((LANES, LANES), F32)
            gbs = jnp.zeros((LANES, 1), F32)
            parts = []
            for n in range(rb // LANES):
                rows = slice(n * LANES, (n + 1) * LANES)
                mixed = _dot(ws, vb[rows, cols]) + bs_ref[:, g:g + 1]
                gu_ref[rows, cols] = gy[rows, cols] * mixed * _gelu_grad(u[rows, cols])
                parts.append(_dot(ws, gmb[rows, cols], TN))
                gw = gw + _dot(gmb[rows, cols], vb[rows, cols], NT)
                gbs = gbs + jnp.sum(gmix[rows, cols], axis=1, keepdims=True)
            gvln_cols.append(jnp.concatenate(parts, axis=0))
            gw_ref[g] += jnp.where(r >= c, gw, 0.0)
            gbs_ref[:, g:g + 1] += gbs
        gvln = jnp.concatenate(gvln_cols, axis=1)
        gg_ref[...] += jnp.sum(gvln * xh, axis=0, keepdims=True)
        gb_ref[...] += jnp.sum(gvln, axis=0, keepdims=True)
        gxh = gvln * g_ref[...]
        ggv = rs * (gxh - jnp.mean(gxh, axis=-1, keepdims=True) - xh * jnp.mean(gxh * xh, axis=-1, keepdims=True))
        gv_ref[...] = ggv * _gelu_grad(v)

    vm = lambda shape: pl.BlockSpec(shape, lambda i: tuple(0 for _ in shape))
    blk = pl.BlockSpec((rb, 512), lambda i: (i, 0))
    return pl.pallas_call(
        body, name=name, grid=(L // rb,),
        in_specs=[pl.BlockSpec((rb, 512), lambda i: (i, 1)), pl.BlockSpec((rb, 512), lambda i: (i, 1)),
                  pl.BlockSpec((rb, 512), lambda i: (i, 2)),
                  vm((1, 512)), vm((1, 512)), vm((4, LANES, LANES)), vm((LANES, 4))],
        out_specs=[blk, blk, vm((4, LANES, LANES)), vm((LANES, 4)), vm((1, 512)), vm((1, 512))],
        out_shape=[_sds((L, 512)), _sds((L, 512)), _sds((4, LANES, LANES)), _sds((LANES, 4)),
                   _sds((1, 512)), _sds((1, 512))],
        compiler_params=_cp("arbitrary"),
    )(g_m, z, z, ln_g, ln_b, w_s, b_st)


def _adamw_math(w, g, m, v):
    nm = ADAM_B1 * m + (1.0 - ADAM_B1) * g
    nv = ADAM_B2 * v + (1.0 - ADAM_B2) * (g * g)
    m_hat = nm / (1.0 - ADAM_B1 ** ADAM_STEP)
    v_hat = nv / (1.0 - ADAM_B2 ** ADAM_STEP)
    delta = -ADAM_LR * (m_hat / (jnp.sqrt(v_hat) + ADAM_EPS) + ADAM_WD * w)
    return delta, nm, nv


def _sum_adamw(parts, w, m, v, name):
    R, C = w.shape
    rb = min(128, R)

    def body(p_ref, w_ref, m_ref, v_ref, g_ref, d_ref, nm_ref, nv_ref):
        g = p_ref[0]
        for s in range(1, N_DEV):
            g = g + p_ref[s]
        d, nm, nv = _adamw_math(w_ref[...], g, m_ref[...], v_ref[...])
        g_ref[...] = g
        d_ref[...] = d
        nm_ref[...] = nm
        nv_ref[...] = nv

    blk = pl.BlockSpec((rb, C), lambda i: (i, 0))
    return pl.pallas_call(
        body, name=name, grid=(R // rb,),
        in_specs=[pl.BlockSpec((N_DEV, rb, C), lambda i: (0, i, 0)), blk, blk, blk],
        out_specs=[blk] * 4, out_shape=[_sds((R, C))] * 4,
        compiler_params=_cp("parallel"),
    )(parts, w, m, v)


def _sum_pieces(parts, name):
    _, R, C = parts.shape

    def body(p_ref, g_ref):
        g = p_ref[0]
        for s in range(1, N_DEV):
            g = g + p_ref[s]
        g_ref[...] = g

    vm = pl.BlockSpec(memory_space=pltpu.VMEM)
    return pl.pallas_call(body, name=name, in_specs=[vm], out_specs=vm, out_shape=_sds((R, C)))(parts)


def _adamw(w, g, m, v, name):
    vm = pl.BlockSpec(memory_space=pltpu.VMEM)

    def body(w_ref, g_ref, m_ref, v_ref, d_ref, nm_ref, nv_ref):
        d, nm, nv = _adamw_math(w_ref[...], g_ref[...], m_ref[...], v_ref[...])
        d_ref[...] = d
        nm_ref[...] = nm
        nv_ref[...] = nv

    return pl.pallas_call(body, name=name, in_specs=[vm] * 4, out_specs=[vm] * 3,
                          out_shape=[_sds(w.shape)] * 3)(w, g, m, v)


def _mesh_pos():
    return lax.axis_index("x"), lax.axis_index("y"), lax.axis_index("c")


def _dev_index(p):
    return 4 * p[0] + 2 * p[1] + p[2]


def _all_gather(xs, name):
    n = len(xs)

    def body(*refs):
        x_refs, o_refs = refs[:n], refs[n:2 * n]
        send_sems, recv_sems, local_sems = refs[2 * n:]
        x, y, c = _mesh_pos()
        me, sibling = (x, y, c), (x, y, 1 - c)
        chips = [(1 - x, y), (x, 1 - y), (1 - x, 1 - y)]

        def copy(i, k, block, to, src=None):
            dst = o_refs[i].at[_dev_index(block)]
            return pltpu.make_async_remote_copy(
                src_ref=dst if src is None else src, dst_ref=dst,
                send_sem=send_sems.at[i, k], recv_sem=recv_sems.at[i, k], device_id=to, device_id_type=MESH)

        mine = [pltpu.make_async_copy(x_refs[i], o_refs[i].at[_dev_index(me)], local_sems.at[i]) for i in range(n)]
        for cp in mine:
            cp.start()
        first = []
        for i in range(n):
            first.append(copy(i, 0, me, sibling, src=x_refs[i]))
            first += [copy(i, 1 + j, me, (*chip, c), src=x_refs[i]) for j, chip in enumerate(chips)]
        for cp in first:
            cp.start()
        passed = []
        for j, chip in enumerate(chips):
            for i in range(n):
                copy(i, 1 + j, (*chip, c), me).wait_recv()
                fwd = copy(i, 4 + j, (*chip, c), sibling)
                fwd.start()
                passed.append(fwd)
        for i in range(n):
            copy(i, 0, sibling, me).wait_recv()
            for j, chip in enumerate(chips):
                copy(i, 4 + j, (*chip, 1 - c), me).wait_recv()
        for cp in first + passed:
            cp.wait_send()
        for cp in mine:
            cp.wait()

    outs = pl.pallas_call(
        body, name=name,
        in_specs=[ANY] * n, out_specs=[ANY] * n,
        out_shape=[_sds((N_DEV,) + x.shape, x.dtype) for x in xs],
        scratch_shapes=[pltpu.SemaphoreType.DMA((n, 7)), pltpu.SemaphoreType.DMA((n, 7)),
                        pltpu.SemaphoreType.DMA((n,))],
    )(*xs)
    return list(outs)


def _exchange(gs, name):
    n = len(gs)

    def body(*refs):
        g_refs, o_refs = refs[:n], refs[n:2 * n]
        send_sems, recv_sems, local_sems = refs[2 * n:]
        x, y, c = _mesh_pos()
        me = (x, y, c)
        mi = _dev_index(me)
        peers = [(x ^ dx, y ^ dy, c ^ dc) for dx in range(2) for dy in range(2) for dc in range(2)][1:]

        def copy(i, k, peer):
            return pltpu.make_async_remote_copy(
                src_ref=g_refs[i].at[_dev_index(peer)], dst_ref=o_refs[i].at[mi],
                send_sem=send_sems.at[i, k], recv_sem=recv_sems.at[i, k], device_id=peer, device_id_type=MESH)

        mine = [pltpu.make_async_copy(g_refs[i].at[mi], o_refs[i].at[mi], local_sems.at[i]) for i in range(n)]
        for cp in mine:
            cp.start()
        sends = [copy(i, k, peer) for i in range(n) for k, peer in enumerate(peers)]
        for cp in sends:
            cp.start()
        for i in range(n):
            for k, peer in enumerate(peers):
                pltpu.make_async_remote_copy(
                    src_ref=g_refs[i].at[mi], dst_ref=o_refs[i].at[_dev_index(peer)],
                    send_sem=send_sems.at[i, k], recv_sem=recv_sems.at[i, k], device_id=peer,
                    device_id_type=MESH).wait_recv()
        for cp in sends:
            cp.wait_send()
        for cp in mine:
            cp.wait()

    outs = pl.pallas_call(
        body, name=name,
        in_specs=[ANY] * n, out_specs=[ANY] * n,
        out_shape=[_sds(g.shape, g.dtype) for g in gs],
        scratch_shapes=[pltpu.SemaphoreType.DMA((n, 7)), pltpu.SemaphoreType.DMA((n, 7)),
                        pltpu.SemaphoreType.DMA((n,))],
    )(*gs)
    return list(outs)


def _col_pieces(g, n_cols):
    R = g.shape[0]
    c = n_cols // N_DEV
    return jnp.transpose(g[:, :n_cols].reshape(R, N_DEV, c), (1, 0, 2))


def _from_col_pieces(p, pad_to=None):
    _, R, c = p.shape
    w = jnp.transpose(p, (1, 0, 2)).reshape(R, N_DEV * c)
    if pad_to is not None and pad_to > N_DEV * c:
        w = jnp.pad(w, ((0, 0), (0, pad_to - N_DEV * c)))
    return w


def _pack(arrs, rows):
    flat = jnp.concatenate([a.reshape(-1).astype(F32) for a in arrs])
    return jnp.pad(flat, (0, rows * LANES - flat.shape[0])).reshape(rows, LANES)


def _unpack(packed, shapes):
    flat = packed.reshape(-1)
    out, off = [], 0
    for s in shapes:
        n = math.prod(s)
        out.append(flat[off:off + n].reshape(s))
        off += n
    return out


def _packed_rows(shapes):
    n = sum(math.prod(s) for s in shapes)
    unit = N_DEV * 8 * LANES
    return -(-n // unit) * unit // LANES


def kernel(x, mix_pre_g, mix_post_g, mlp_pre_g, mlp_post_g, w_in_even, s5_lam_re, s5_lam_im, s5_log_dt, s5_b_re, s5_b_im, s5_c_re, s5_c_im, s5_d, s5_w_glu, fox_b_f, w_out_even, w_in_odd, pool_w, pool_scale, sgu_ln_g, sgu_ln_b, sgu_w_s, sgu_b_s, w_out_odd, mlp_w1, mlp_w2, loss_target, m_mix_pre_g, m_mix_post_g, m_mlp_pre_g, m_mlp_post_g, m_w_in_even, m_s5_lam_re, m_s5_lam_im, m_s5_log_dt, m_s5_b_re, m_s5_b_im, m_s5_c_re, m_s5_c_im, m_s5_d, m_s5_w_glu, m_fox_b_f, m_w_out_even, m_w_in_odd, m_pool_w, m_pool_scale, m_sgu_ln_g, m_sgu_ln_b, m_sgu_w_s, m_sgu_b_s, m_w_out_odd, m_mlp_w1, m_mlp_w2, v_mix_pre_g, v_mix_post_g, v_mlp_pre_g, v_mlp_post_g, v_w_in_even, v_s5_lam_re, v_s5_lam_im, v_s5_log_dt, v_s5_b_re, v_s5_b_im, v_s5_c_re, v_s5_c_im, v_s5_d, v_s5_w_glu, v_fox_b_f, v_w_out_even, v_w_in_odd, v_pool_w, v_pool_scale, v_sgu_ln_g, v_sgu_ln_b, v_sgu_w_s, v_sgu_b_s, v_w_out_odd, v_mlp_w1, v_mlp_w2):
    weights = dict(mix_pre_g=mix_pre_g, mix_post_g=mix_post_g, mlp_pre_g=mlp_pre_g, mlp_post_g=mlp_post_g, w_in_even=w_in_even, s5_lam_re=s5_lam_re, s5_lam_im=s5_lam_im, s5_log_dt=s5_log_dt, s5_b_re=s5_b_re, s5_b_im=s5_b_im, s5_c_re=s5_c_re, s5_c_im=s5_c_im, s5_d=s5_d, s5_w_glu=s5_w_glu, fox_b_f=fox_b_f, w_out_even=w_out_even, w_in_odd=w_in_odd, pool_w=pool_w, pool_scale=pool_scale, sgu_ln_g=sgu_ln_g, sgu_ln_b=sgu_ln_b, sgu_w_s=sgu_w_s, sgu_b_s=sgu_b_s, w_out_odd=w_out_odd, mlp_w1=mlp_w1, mlp_w2=mlp_w2)
    mom_m = dict(mix_pre_g=m_mix_pre_g, mix_post_g=m_mix_post_g, mlp_pre_g=m_mlp_pre_g, mlp_post_g=m_mlp_post_g, w_in_even=m_w_in_even, s5_lam_re=m_s5_lam_re, s5_lam_im=m_s5_lam_im, s5_log_dt=m_s5_log_dt, s5_b_re=m_s5_b_re, s5_b_im=m_s5_b_im, s5_c_re=m_s5_c_re, s5_c_im=m_s5_c_im, s5_d=m_s5_d, s5_w_glu=m_s5_w_glu, fox_b_f=m_fox_b_f, w_out_even=m_w_out_even, w_in_odd=m_w_in_odd, pool_w=m_pool_w, pool_scale=m_pool_scale, sgu_ln_g=m_sgu_ln_g, sgu_ln_b=m_sgu_ln_b, sgu_w_s=m_sgu_w_s, sgu_b_s=m_sgu_b_s, w_out_odd=m_w_out_odd, mlp_w1=m_mlp_w1, mlp_w2=m_mlp_w2)
    mom_v = dict(mix_pre_g=v_mix_pre_g, mix_post_g=v_mix_post_g, mlp_pre_g=v_mlp_pre_g, mlp_post_g=v_mlp_post_g, w_in_even=v_w_in_even, s5_lam_re=v_s5_lam_re, s5_lam_im=v_s5_lam_im, s5_log_dt=v_s5_log_dt, s5_b_re=v_s5_b_re, s5_b_im=v_s5_b_im, s5_c_re=v_s5_c_re, s5_c_im=v_s5_c_im, s5_d=v_s5_d, s5_w_glu=v_s5_w_glu, fox_b_f=v_fox_b_f, w_out_even=v_w_out_even, w_in_odd=v_w_in_odd, pool_w=v_pool_w, pool_scale=v_pool_scale, sgu_ln_g=v_sgu_ln_g, sgu_ln_b=v_sgu_ln_b, sgu_w_s=v_sgu_w_s, sgu_b_s=v_sgu_b_s, w_out_odd=v_w_out_odd, mlp_w1=v_mlp_w1, mlp_w2=v_mlp_w2)
    names = list(weights)
    L = x.shape[1]
    x0 = x[0]
    target = loss_target[0]
    my_index = 4 * lax.axis_index("x") + 2 * lax.axis_index("y") + lax.axis_index("c")

    small_vec = jnp.zeros((8, LANES), F32)
    small_vec = small_vec.at[0, :64].set(pool_scale[0]).at[1, :64].set(sgu_ln_g[0]).at[2, :64].set(sgu_ln_b[0])
    ag = _all_gather(
        [w_in_even[0].astype(BF16), s5_w_glu[0].astype(BF16), w_out_even[0].astype(BF16),
         w_in_odd[0].astype(BF16), w_out_odd[0].astype(BF16),
         mlp_w1[0].astype(BF16), mlp_w1[1].astype(BF16), mlp_w2[0].astype(BF16), mlp_w2[1].astype(BF16),
         small_vec], "ag_weights")
    win_e = _from_col_pieces(ag[0], EVEN_PAD)
    wglu = ag[1].reshape(S5_W, S5_W)
    wout_e = ag[2].reshape(D_MODEL, D_MODEL)
    win_o = _from_col_pieces(ag[3])
    wout_o = ag[4].reshape(D_MODEL, D_MODEL)
    w1 = [ag[5], ag[6]]
    w2 = [ag[7].reshape(4 * D_MODEL, D_MODEL), ag[8].reshape(4 * D_MODEL, D_MODEL)]
    pool_scale_f = ag[9][:, 0, :64].reshape(1, 512)
    ln_g_f = ag[9][:, 1, :64].reshape(1, 512)
    ln_b_f = ag[9][:, 2, :64].reshape(1, 512)

    lam_r = jnp.concatenate([s5_lam_re.reshape(1, S5_NS), s5_lam_im.reshape(1, S5_NS)], axis=0)
    ldt_r = jnp.repeat(s5_log_dt.reshape(32), 64).reshape(1, S5_NS)
    lam_c = jnp.transpose(lam_r)
    ldt_c = jnp.transpose(ldt_r)
    b_t = jnp.stack([jnp.tile(s5_b_re.reshape(S5_NS, 16), (1, 8)), jnp.tile(s5_b_im.reshape(S5_NS, 16), (1, 8))])
    c_t = jnp.stack([jnp.tile(s5_c_re.reshape(S5_W, 64), (1, 8)), jnp.tile(s5_c_im.reshape(S5_W, 64), (1, 8))])
    bf_pad = jnp.pad(fox_b_f, ((0, 0), (0, LANES - 8)))
    b_st = jnp.transpose(sgu_b_s[0])

    h0, rx0 = _rms_fwd(x0, mix_pre_g[0:1], "rms0")
    z0 = _mm(h0, win_e, name="win_even", bn=EVEN_PAD)
    tabs, bset, cset = _s5_prep(lam_r, ldt_r, lam_c, ldt_c, b_t, c_t, "s5_prep")
    bu = _s5_bu(z0, bset, "s5_bu")
    xs = _s5_scan(bu, tabs, "s5_scan")
    ylin, ya = _s5_out_fwd(xs, cset, z0, s5_d, wglu, "s5_out")
    fcum = _fox_f_fwd(z0, bf_pad, "fox_f")
    f8 = fcum[:, :8]
    fq = jnp.repeat(f8, 64, axis=1)
    frow = jnp.transpose(f8).reshape(4, 2, L)
    o_att, lse = _fox_fwd(z0, fq, frow, "fox_fwd")
    mix0 = jnp.concatenate([ya, o_att.astype(BF16)], axis=1)
    y0 = _mm(mix0, wout_e, name="wout_even")
    x1, ry0, h1, rx1 = _post_pre_fwd(x0, y0, mix_post_g[0:1], mlp_pre_g[0:1], "post0")
    p0, a0 = _mm(h1, w1[0], name="mlp0_w1", b3=True, out_dtypes=(BF16, BF16), epi=_epi_relu2)
    o0 = _mm(a0, w2[0], name="mlp0_w2")
    x2, ro0, h2, rx2 = _post_pre_fwd(x1, o0, mlp_post_g[0:1], mix_pre_g[1:2], "post1")
    z1 = _mm(h2, win_o, name="win_odd", bn=ODD_IN)
    yc, pooled = _pool_fwd(z1, pool_w[0], pool_scale_f, "pool_fwd")
    yd = _sgu_fwd(z1, ln_g_f, ln_b_f, sgu_w_s[0], b_st, "sgu_fwd")
    mix1 = jnp.concatenate([yc, yd], axis=1)
    y1 = _mm(mix1, wout_o, name="wout_odd")
    x3, ry1, h3, rx3 = _post_pre_fwd(x2, y1, mix_post_g[1:2], mlp_pre_g[1:2], "post2")
    p1, a1 = _mm(h3, w1[1], name="mlp1_w1", b3=True, out_dtypes=(BF16, BF16), epi=_epi_relu2)
    o1 = _mm(a1, w2[1], name="mlp1_w2")
    gx4, ro1, sq = _post_loss_fwd(x3, o1, mlp_post_g[1:2], target, "post3")
    loss = lax.psum(0.5 * sq[0, 0] / D_MODEL, ("x", "y", "c"))

    g_o1, gg_mlp_post1 = _post_bwd(gx4, o1, ro1, mlp_post_g[1:2], "bpost3")
    g_p1 = _mm(g_o1, w2[1], name="b_mlp1_a", tb=True, out_dtypes=(BF16,), epi=_epi_relu2_bwd, extra=(p1,))
    gw2_1 = _mm(a1, g_o1, name="b_mlp1_w2", ta=True)
    g_h3 = _mm(g_p1, w1[1], name="b_mlp1_h", tb=True, b3=True)
    gw1_1 = _mm(h3, g_p1, name="b_mlp1_w1", ta=True, out3=True, bn=512)
    g_x3, gg_mlp_pre1, g_y1, gg_mix_post1 = _pre_post_bwd(g_h3, x3, rx3, mlp_pre_g[1:2], gx4, y1, ry1, mix_post_g[1:2], "bpre3")
    g_mix1 = _mm(g_y1, wout_o, name="b_wout_odd_m", tb=True)
    gwout_o = _mm(mix1, g_y1, name="b_wout_odd_w", ta=True)
    g_xc, g_pool_w, g_pool_scale = _pool_bwd(g_mix1, pooled, pool_w[0], pool_scale_f, "pool_bwd")
    g_u1, g_v1, g_ws, g_bst, g_ln_g, g_ln_b = _sgu_bwd(g_mix1, z1, ln_g_f, ln_b_f, sgu_w_s[0], b_st, "sgu_bwd")
    g_z1 = jnp.concatenate([g_xc, g_u1, g_v1], axis=1).astype(BF16)
    g_h2 = _mm(g_z1, win_o, name="b_win_odd_h", tb=True, bk=ODD_IN)
    gwin_o = _mm(h2, g_z1, name="b_win_odd_w", ta=True, bn=ODD_IN)
    g_x2, gg_mix_pre1, g_o0, gg_mlp_post0 = _pre_post_bwd(g_h2, x2, rx2, mix_pre_g[1:2], g_x3, o0, ro0, mlp_post_g[0:1], "bpre2")
    g_p0 = _mm(g_o0, w2[0], name="b_mlp0_a", tb=True, out_dtypes=(BF16,), epi=_epi_relu2_bwd, extra=(p0,))
    gw2_0 = _mm(a0, g_o0, name="b_mlp0_w2", ta=True)
    g_h1 = _mm(g_p0, w1[0], name="b_mlp0_h", tb=True, b3=True)
    gw1_0 = _mm(h1, g_p0, name="b_mlp0_w1", ta=True, out3=True, bn=512)
    g_x1, gg_mlp_pre0, g_y0, gg_mix_post0 = _pre_post_bwd(g_h1, x1, rx1, mlp_pre_g[0:1], g_x2, y0, ry0, mix_post_g[0:1], "bpre1")
    g_mix0 = _mm(g_y0, wout_e, name="b_wout_even_m", tb=True)
    gwout_e = _mm(mix0, g_y0, name="b_wout_even_w", ta=True)
    gyl, gud, g_wglu, g_d = _s5_glu_bwd(g_mix0, ylin, z0, s5_d, wglu, "s5_glu_bwd")
    gxd, gc_raw = _s5_c_bwd(gyl, xs, cset, "s5_c_bwd")
    gxs, ga = _s5_scan(gxd, tabs, "s5_scan_bwd", reverse=True, xs=xs)
    g_u0, gb_raw = _s5_bu_bwd(gxs, bset, z0, gud, "s5_bu_bwd")
    g_lam, g_ldt, g_b, g_c = _s5_param_bwd(lam_c, ldt_c, b_t, gb_raw, jnp.transpose(ga), gc_raw, "s5_param_bwd")
    dq, dfq = _fox_bwd_dq(z0, fq, frow, o_att, lse, g_mix0, "fox_dq")
    dk, dv, dfrow = _fox_bwd_dkv(z0, fq, frow, o_att, lse, g_mix0, "fox_dkv")
    dFk = jnp.pad(jnp.transpose(dfrow.reshape(8, L)), ((0, 0), (0, LANES - 8)))
    dFq = jnp.pad(dfq[:, ::64], ((0, 0), (0, LANES - 8)))
    dfl, db_f = _fox_f_bwd(dFk, dFq, z0, bf_pad, "fox_f_bwd")
    g_z0 = jnp.concatenate([g_u0, dq, dk, dv, dfl], axis=1).astype(BF16)
    g_h0 = _mm(g_z0, win_e, name="b_win_even_h", tb=True, bk=EVEN_PAD)
    gwin_e = _mm(h0, g_z0, name="b_win_even_w", ta=True, bn=EVEN_PAD)
    grad_x, gg_mix_pre0 = _pre_bwd(g_h0, x0, rx0, mix_pre_g[0:1], g_x1, "bpre0")

    big = _exchange(
        [_col_pieces(gwin_e, EVEN_IN), g_wglu.reshape(N_DEV, 64, S5_W), gwout_e.reshape(N_DEV, 128, D_MODEL),
         _col_pieces(gwin_o, ODD_IN), gwout_o.reshape(N_DEV, 128, D_MODEL),
         gw1_0, gw1_1, gw2_0.reshape(N_DEV, 512, D_MODEL), gw2_1.reshape(N_DEV, 512, D_MODEL)], "exchange_big")

    small_grads = dict(
        mix_pre_g=jnp.concatenate([gg_mix_pre0, gg_mix_pre1]), mix_post_g=jnp.concatenate([gg_mix_post0, gg_mix_post1]),
        mlp_pre_g=jnp.concatenate([gg_mlp_pre0, gg_mlp_pre1]), mlp_post_g=jnp.concatenate([gg_mlp_post0, gg_mlp_post1]),
        s5_lam_re=g_lam[:, 0], s5_lam_im=g_lam[:, 1], s5_log_dt=g_ldt,
        s5_b_re=g_b[0, :, :16], s5_b_im=g_b[1, :, :16], s5_c_re=g_c[0, :, :64], s5_c_im=g_c[1, :, :64],
        s5_d=g_d, fox_b_f=db_f[:, :8], pool_w=g_pool_w, sgu_w_s=g_ws, sgu_b_s=jnp.transpose(g_bst),
        pool_scale=g_pool_scale, sgu_ln_g=g_ln_g, sgu_ln_b=g_ln_b)
    small_names = list(small_grads)
    full_shapes = [(512,) if nm in ("pool_scale", "sgu_ln_g", "sgu_ln_b") else weights[nm].shape for nm in small_names]
    rows = _packed_rows(full_shapes)
    packed = _pack([small_grads[nm] for nm in small_names], rows).reshape(N_DEV, rows // N_DEV, LANES)
    (recv_small,) = _exchange([packed], "exchange_small")
    piece = _sum_pieces(recv_small, "sum_small")
    (small_all,) = _all_gather([piece], "ag_small")
    small_full = _unpack(small_all.reshape(rows, LANES), full_shapes)
    small_g = {}
    for nm, g in zip(small_names, small_full):
        if nm in ("pool_scale", "sgu_ln_g", "sgu_ln_b"):
            g = lax.dynamic_slice(g, (my_index * 64,), (64,)).reshape(1, 64)
        small_g[nm] = g
    own_shapes = [weights[nm].shape for nm in small_names]
    rows2 = _packed_rows(own_shapes)
    pw = _pack([weights[nm] for nm in small_names], rows2)
    pg = _pack([small_g[nm] for nm in small_names], rows2)
    pm = _pack([mom_m[nm] for nm in small_names], rows2)
    pv = _pack([mom_v[nm] for nm in small_names], rows2)
    pd, pnm, pnv = _adamw(pw, pg, pm, pv, "adamw_small")
    res = {}
    for nm, d_, m_, v_ in zip(small_names, _unpack(pd, own_shapes), _unpack(pnm, own_shapes), _unpack(pnv, own_shapes)):
        res[nm] = (small_g[nm], d_, m_, v_)

    big_names = ["w_in_even", "s5_w_glu", "w_out_even", "w_in_odd", "w_out_odd"]
    for nm, parts in zip(big_names, big[:5]):
        shp = weights[nm].shape
        outs = _sum_adamw(parts, weights[nm][0], mom_m[nm][0], mom_v[nm][0], "adamw_" + nm)
        res[nm] = tuple(o.reshape(shp) for o in outs)
    for nm, parts in (("mlp_w1", big[5:7]), ("mlp_w2", big[7:9])):
        per_layer = [_sum_adamw(parts[l], weights[nm][l], mom_m[nm][l], mom_v[nm][l], "adamw_%s_%d" % (nm, l))
                     for l in range(2)]
        res[nm] = tuple(jnp.stack([per_layer[0][k], per_layer[1][k]]) for k in range(4))

    grads = [res[nm][0].reshape(weights[nm].shape) for nm in names]
    deltas = [res[nm][1].reshape(weights[nm].shape) for nm in names]
    new_m = [res[nm][2].reshape(weights[nm].shape) for nm in names]
    new_v = [res[nm][3].reshape(weights[nm].shape) for nm in names]
    return (loss, grad_x[None], *grads, *deltas, *new_m, *new_v)
```

```python
import functools
import math

import jax
import jax.numpy as jnp
from jax import lax
from jax.experimental import pallas as pl
from jax.experimental.pallas import tpu as pltpu

F32 = jnp.float32
BF16 = jnp.bfloat16
MESH = pl.DeviceIdType.MESH
ANY = pl.BlockSpec(memory_space=pl.ANY)

N_DEV = 8
D_MODEL = 1024
EPS = 1e-6
S5_W = 512
S5_NS = 2048
FOX_W = 512
EVEN_IN = 2056
EVEN_PAD = 2176
ODD_IN = 1536
LANES = 128
VMEM_LIMIT = 56 * 1024 * 1024

ADAM_LR = 0.001
ADAM_B1 = 0.9
ADAM_B2 = 0.999
ADAM_EPS = 1e-08
ADAM_WD = 0.01
ADAM_STEP = 10

NT = (((1,), (1,)), ((), ()))
TN = (((0,), (0,)), ((), ()))
NN = (((1,), (0,)), ((), ()))


def _cp(*sem):
    return pltpu.CompilerParams(dimension_semantics=sem, vmem_limit_bytes=VMEM_LIMIT)


def _sds(shape, dtype=F32):
    return jax.ShapeDtypeStruct(tuple(shape), dtype)


def _gelu(x):
    t = jnp.tanh(0.7978845608028654 * (x + 0.044715 * x * x * x))
    return 0.5 * x * (1.0 + t)


def _gelu_grad(x):
    t = jnp.tanh(0.7978845608028654 * (x + 0.044715 * x * x * x))
    du = 0.7978845608028654 * (1.0 + 3.0 * 0.044715 * x * x)
    return 0.5 * (1.0 + t) + 0.5 * x * (1.0 - t * t) * du


def _sigmoid(x):
    return 1.0 / (1.0 + jnp.exp(-x))


def _dot(a, b, dn=NN):
    return lax.dot_general(a, b, dn, preferred_element_type=F32)


def _mm(a, b, *, name, ta=False, tb=False, b3=False, out3=False, out_dtypes=(F32,), epi=None, extra=(),
        bm=512, bn=1024, bk=1024):
    M = a.shape[1] if ta else a.shape[0]
    K = a.shape[0] if ta else a.shape[1]
    if b3:
        if tb:
            bk = b.shape[2]
            N = b.shape[1]
            assert b.shape[0] * bk == K
        else:
            bn = b.shape[2]
            N = b.shape[0] * bn
            assert b.shape[1] == K
    else:
        N = b.shape[0] if tb else b.shape[1]
    bm, bn, bk = min(bm, M), min(bn, N), min(bk, K)
    assert M % bm == 0 and N % bn == 0 and K % bk == 0, (name, M, N, K, bm, bn, bk)
    nk = K // bk
    n_extra = len(extra)
    n_out = len(out_dtypes)
    dn = (((0 if ta else 1,), (1 if tb else 0,)), ((), ()))

    def body(*refs):
        a_ref, b_ref = refs[0], refs[1]
        e_refs = refs[2:2 + n_extra]
        o_refs = refs[2 + n_extra:2 + n_extra + n_out]
        acc_ref = refs[-1]
        k = pl.program_id(2)

        @pl.when(k == 0)
        def _():
            acc_ref[...] = jnp.zeros_like(acc_ref)

        acc_ref[...] += lax.dot_general(a_ref[...].astype(BF16), b_ref[...].astype(BF16), dn,
                                        preferred_element_type=F32)

        @pl.when(k == nk - 1)
        def _():
            acc = acc_ref[...]
            outs = (acc,) if epi is None else epi(acc, *[e[...] for e in e_refs])
            for o_ref, o in zip(o_refs, outs):
                o_ref[...] = o.astype(o_ref.dtype)

    a_spec = pl.BlockSpec((bk, bm), lambda i, j, k: (k, i)) if ta else pl.BlockSpec((bm, bk), lambda i, j, k: (i, k))
    if b3:
        if tb:
            b_spec = pl.BlockSpec((None, bn, bk), lambda i, j, k: (k, j, 0))
        else:
            b_spec = pl.BlockSpec((None, bk, bn), lambda i, j, k: (j, k, 0))
    else:
        b_spec = pl.BlockSpec((bn, bk), lambda i, j, k: (j, k)) if tb else pl.BlockSpec((bk, bn), lambda i, j, k: (k, j))
    e_specs = [pl.BlockSpec((bm, bn), lambda i, j, k: (i, j)) for _ in extra]
    if out3:
        o_specs = [pl.BlockSpec((None, bm, bn), lambda i, j, k: (j, i, 0)) for _ in out_dtypes]
        o_shapes = [_sds((N // bn, M, bn), dt) for dt in out_dtypes]
    else:
        o_specs = [pl.BlockSpec((bm, bn), lambda i, j, k: (i, j)) for _ in out_dtypes]
        o_shapes = [_sds((M, N), dt) for dt in out_dtypes]
    outs = pl.pallas_call(
        body, name=name, grid=(M // bm, N // bn, nk),
        in_specs=[a_spec, b_spec] + e_specs, out_specs=o_specs, out_shape=o_shapes,
        scratch_shapes=[pltpu.VMEM((bm, bn), F32)],
        compiler_params=_cp("parallel", "parallel", "arbitrary"),
    )(a, b, *extra)
    return outs[0] if n_out == 1 else outs


def _epi_relu2(acc):
    r = jnp.maximum(acc, 0.0)
    return acc, r * r


def _epi_relu2_bwd(acc, p):
    return (acc * (2.0 * jnp.maximum(p.astype(F32), 0.0)),)


def _row_spec(rb, w=D_MODEL):
    return pl.BlockSpec((rb, w), lambda i: (i, 0))


def _vec_spec(w=D_MODEL):
    return pl.BlockSpec((1, w), lambda i: (0, 0))


def _rstd(v):
    return lax.rsqrt(jnp.mean(v * v, axis=-1, keepdims=True) + EPS)


def _rms_fwd(x, g, name):
    L = x.shape[0]
    rb = min(256, L)

    def body(x_ref, g_ref, h_ref, r_ref):
        xv = x_ref[...]
        r = _rstd(xv)
        h_ref[...] = (xv * r * g_ref[...]).astype(BF16)
        r_ref[...] = r

    return pl.pallas_call(
        body, name=name, grid=(L // rb,),
        in_specs=[_row_spec(rb), _vec_spec()],
        out_specs=[_row_spec(rb), _row_spec(rb, 1)],
        out_shape=[_sds((L, D_MODEL), BF16), _sds((L, 1))],
        compiler_params=_cp("parallel"),
    )(x, g)


def _post_pre_fwd(x_in, y, g_post, g_pre, name):
    L = x_in.shape[0]
    rb = min(256, L)

    def body(x_ref, y_ref, gp_ref, gn_ref, xo_ref, ry_ref, h_ref, rx_ref):
        yv = y_ref[...]
        ry = _rstd(yv)
        xo = x_ref[...] + yv * ry * gp_ref[...]
        rx = _rstd(xo)
        xo_ref[...] = xo
        ry_ref[...] = ry
        h_ref[...] = (xo * rx * gn_ref[...]).astype(BF16)
        rx_ref[...] = rx

    return pl.pallas_call(
        body, name=name, grid=(L // rb,),
        in_specs=[_row_spec(rb), _row_spec(rb), _vec_spec(), _vec_spec()],
        out_specs=[_row_spec(rb), _row_spec(rb, 1), _row_spec(rb), _row_spec(rb, 1)],
        out_shape=[_sds((L, D_MODEL)), _sds((L, 1)), _sds((L, D_MODEL), BF16), _sds((L, 1))],
        compiler_params=_cp("parallel"),
    )(x_in, y, g_post, g_pre)


def _post_loss_fwd(x_in, y, g_post, target, name):
    L = x_in.shape[0]
    rb = min(256, L)

    def body(x_ref, y_ref, gp_ref, t_ref, gx_ref, ry_ref, loss_ref):
        i = pl.program_id(0)
        yv = y_ref[...]
        ry = _rstd(yv)
        diff = x_ref[...] + yv * ry * gp_ref[...] - t_ref[...]
        gx_ref[...] = diff * (1.0 / D_MODEL)
        ry_ref[...] = ry

        @pl.when(i == 0)
        def _():
            loss_ref[...] = jnp.zeros_like(loss_ref)

        loss_ref[...] += jnp.sum(diff * diff, keepdims=True)

    return pl.pallas_call(
        body, name=name, grid=(L // rb,),
        in_specs=[_row_spec(rb), _row_spec(rb), _vec_spec(), _row_spec(rb)],
        out_specs=[_row_spec(rb), _row_spec(rb, 1), pl.BlockSpec((1, 1), lambda i: (0, 0))],
        out_shape=[_sds((L, D_MODEL)), _sds((L, 1)), _sds((1, 1))],
        compiler_params=_cp("arbitrary"),
    )(x_in, y, g_post, target)


def _rms_bwd_rows(dy, xv, r, g):
    n = xv * r
    dyg = dy * g
    return r * (dyg - n * jnp.mean(dyg * n, axis=-1, keepdims=True)), n


def _post_bwd(g_out, y, ry, g_post, name):
    L = y.shape[0]
    rb = min(256, L)

    def body(go_ref, y_ref, ry_ref, gp_ref, gy_ref, gg_ref):
        i = pl.program_id(0)
        go = go_ref[...]
        gy, n = _rms_bwd_rows(go, y_ref[...], ry_ref[...], gp_ref[...])
        gy_ref[...] = gy.astype(BF16)

        @pl.when(i == 0)
        def _():
            gg_ref[...] = jnp.zeros_like(gg_ref)

        gg_ref[...] += jnp.sum(go * n, axis=0, keepdims=True)

    return pl.pallas_call(
        body, name=name, grid=(L // rb,),
        in_specs=[_row_spec(rb), _row_spec(rb), _row_spec(rb, 1), _vec_spec()],
        out_specs=[_row_spec(rb), _vec_spec()],
        out_shape=[_sds((L, D_MODEL), BF16), _sds((1, D_MODEL))],
        compiler_params=_cp("arbitrary"),
    )(g_out, y, ry, g_post)


def _pre_post_bwd(g_h, x, rx, g_pre, g_out, y_prev, ry_prev, g_post_prev, name):
    L = x.shape[0]
    rb = min(256, L)

    def body(gh_ref, x_ref, rx_ref, gn_ref, go_ref, y_ref, ry_ref, gp_ref, gi_ref, ggn_ref, gy_ref, ggp_ref):
        i = pl.program_id(0)
        gh = gh_ref[...]
        gx, n = _rms_bwd_rows(gh, x_ref[...], rx_ref[...], gn_ref[...])
        gi = go_ref[...] + gx
        gi_ref[...] = gi
        gy, ny = _rms_bwd_rows(gi, y_ref[...], ry_ref[...], gp_ref[...])
        gy_ref[...] = gy.astype(BF16)

        @pl.when(i == 0)
        def _():
            ggn_ref[...] = jnp.zeros_like(ggn_ref)
            ggp_ref[...] = jnp.zeros_like(ggp_ref)

        ggn_ref[...] += jnp.sum(gh * n, axis=0, keepdims=True)
        ggp_ref[...] += jnp.sum(gi * ny, axis=0, keepdims=True)

    return pl.pallas_call(
        body, name=name, grid=(L // rb,),
        in_specs=[_row_spec(rb), _row_spec(rb), _row_spec(rb, 1), _vec_spec(), _row_spec(rb),
                  _row_spec(rb), _row_spec(rb, 1), _vec_spec()],
        out_specs=[_row_spec(rb), _vec_spec(), _row_spec(rb), _vec_spec()],
        out_shape=[_sds((L, D_MODEL)), _sds((1, D_MODEL)), _sds((L, D_MODEL), BF16), _sds((1, D_MODEL))],
        compiler_params=_cp("arbitrary"),
    )(g_h, x, rx, g_pre, g_out, y_prev, ry_prev, g_post_prev)


def _pre_bwd(g_h, x, rx, g_pre, g_out, name):
    L = x.shape[0]
    rb = min(256, L)

    def body(gh_ref, x_ref, rx_ref, gn_ref, go_ref, gi_ref, ggn_ref):
        i = pl.program_id(0)
        gh = gh_ref[...]
        gx, n = _rms_bwd_rows(gh, x_ref[...], rx_ref[...], gn_ref[...])
        gi_ref[...] = go_ref[...] + gx

        @pl.when(i == 0)
        def _():
            ggn_ref[...] = jnp.zeros_like(ggn_ref)

        ggn_ref[...] += jnp.sum(gh * n, axis=0, keepdims=True)

    return pl.pallas_call(
        body, name=name, grid=(L // rb,),
        in_specs=[_row_spec(rb), _row_spec(rb), _row_spec(rb, 1), _vec_spec(), _row_spec(rb)],
        out_specs=[_row_spec(rb), _vec_spec()],
        out_shape=[_sds((L, D_MODEL)), _sds((1, D_MODEL))],
        compiler_params=_cp("arbitrary"),
    )(g_h, x, rx, g_pre, g_out)


def _cmul(ar, ai, br, bi):
    return ar * br - ai * bi, ar * bi + ai * br


def _zoh_cols(lr, li, ldt):
    dt = jnp.exp(ldt)
    mag = jnp.exp(lr * dt)
    ar = mag * jnp.cos(li * dt)
    ai = mag * jnp.sin(li * dt)
    den = lr * lr + li * li
    nr = ar - 1.0
    qr = (nr * lr + ai * li) / den
    qi = (ai * lr - nr * li) / den
    return dt, ar, ai, qr, qi, den


def _b_mask():
    r = lax.broadcasted_iota(jnp.int32, (S5_NS, LANES), 0)
    c = lax.broadcasted_iota(jnp.int32, (S5_NS, LANES), 1)
    return ((r >> 6) & 7) == (c >> 4)


def _c_mask():
    r = lax.broadcasted_iota(jnp.int32, (S5_W, 512), 0)
    c = lax.broadcasted_iota(jnp.int32, (S5_W, 512), 1)
    return ((r >> 4) & 7) == (c >> 6)


def _s5_prep(lam_r, ldt_r, lam_c, ldt_c, b_t, c_t, name):
    def body(lam_r_ref, ldt_r_ref, lam_c_ref, ldt_c_ref, b_ref, c_ref, tab_ref, bset_ref, cset_ref):
        lr, li = lam_r_ref[0:1, :], lam_r_ref[1:2, :]
        dt = jnp.exp(ldt_r_ref[...])
        mag = jnp.exp(lr * dt)
        p1r, p1i = mag * jnp.cos(li * dt), mag * jnp.sin(li * dt)
        p2r, p2i = _cmul(p1r, p1i, p1r, p1i)
        p3r, p3i = _cmul(p2r, p2i, p1r, p1i)
        p4r, p4i = _cmul(p2r, p2i, p2r, p2i)
        p5r, p5i = _cmul(p4r, p4i, p1r, p1i)
        p6r, p6i = _cmul(p4r, p4i, p2r, p2i)
        p7r, p7i = _cmul(p4r, p4i, p3r, p3i)
        p8r, p8i = _cmul(p4r, p4i, p4r, p4i)
        pw_r = [p1r, p2r, p3r, p4r, p5r, p6r, p7r, p8r]
        pw_i = [p1i, p2i, p3i, p4i, p5i, p6i, p7i, p8i]
        row = lax.broadcasted_iota(jnp.int32, (8, S5_NS), 0)
        zero = jnp.zeros((8, S5_NS), F32)

        def bc(v):
            return jnp.broadcast_to(v, (8, S5_NS))

        for d in range(2):
            sgn = 1.0 if d == 0 else -1.0
            for t, s in enumerate((1, 2, 4)):
                live = (row >= s) if d == 0 else (row <= 7 - s)
                tab_ref[d, 2 * t] = jnp.where(live, bc(pw_r[s - 1]), zero)
                tab_ref[d, 2 * t + 1] = jnp.where(live, bc(sgn * pw_i[s - 1]), zero)
            cr, ci = zero, zero
            for i in range(8):
                e = i if d == 0 else 7 - i
                cr = jnp.where(row == i, bc(pw_r[e]), cr)
                ci = jnp.where(row == i, bc(sgn * pw_i[e]), ci)
            tab_ref[d, 6] = cr
            tab_ref[d, 7] = ci

        _, _, _, qr, qi, _ = _zoh_cols(lam_c_ref[:, 0:1], lam_c_ref[:, 1:2], ldt_c_ref[...])
        bm = _b_mask()
        br, bi = b_ref[0], b_ref[1]
        bset_ref[0] = jnp.where(bm, qr * br - qi * bi, 0.0).astype(BF16)
        bset_ref[1] = jnp.where(bm, qr * bi + qi * br, 0.0).astype(BF16)
        cm = _c_mask()
        cset_ref[0] = jnp.where(cm, c_ref[0], 0.0).astype(BF16)
        cset_ref[1] = jnp.where(cm, c_ref[1], 0.0).astype(BF16)

    vm = pl.BlockSpec(memory_space=pltpu.VMEM)
    return pl.pallas_call(
        body, name=name, in_specs=[vm] * 6, out_specs=[vm] * 3,
        out_shape=[_sds((2, 8, 8, S5_NS)), _sds((2, S5_NS, LANES), BF16), _sds((2, S5_W, 512), BF16)],
        compiler_params=pltpu.CompilerParams(vmem_limit_bytes=VMEM_LIMIT),
    )(lam_r, ldt_r, lam_c, ldt_c, b_t, c_t)


def _s5_bu(z, bset, name):
    L = z.shape[0]
    bl = min(512, L)

    def body(u_ref, b_ref, o_ref):
        u = u_ref[...].astype(BF16)
        o_ref[0] = _dot(u, b_ref[0], NT)
        o_ref[1] = _dot(u, b_ref[1], NT)

    return pl.pallas_call(
        body, name=name, grid=(L // bl, 4),
        in_specs=[pl.BlockSpec((bl, LANES), lambda i, j: (i, j)),
                  pl.BlockSpec((2, 512, LANES), lambda i, j: (0, j, 0))],
        out_specs=pl.BlockSpec((2, bl, 512), lambda i, j: (0, i, j)),
        out_shape=_sds((2, L, S5_NS)),
        compiler_params=_cp("parallel", "parallel"),
    )(z, bset)


def _s5_scan(bu, tabs, name, reverse=False, xs=None):
    L = bu.shape[1]
    nb = L // 8
    d = 1 if reverse else 0
    with_ga = xs is not None

    def body(*refs):
        if with_ga:
            bu_ref, tab_ref, xs_ref, x_ref, ga_ref = refs
        else:
            bu_ref, tab_ref, x_ref = refs
        tabs_v = [tab_ref[t] for t in range(8)]
        row = lax.broadcasted_iota(jnp.int32, (8, LANES), 0)
        zero = jnp.zeros((8, LANES), F32)

        def step(i, carry):
            cr, ci = carry[0], carry[1]
            b = (nb - 1 - i) if reverse else i
            off = pl.multiple_of(b * 8, 8)
            yr = bu_ref[0, pl.ds(off, 8), :]
            yi = bu_ref[1, pl.ds(off, 8), :]
            for t, s in enumerate((1, 2, 4)):
                sh = (8 - s) if reverse else s
                sr = pltpu.roll(yr, sh, 0)
                si = pltpu.roll(yi, sh, 0)
                mr, mi = tabs_v[2 * t], tabs_v[2 * t + 1]
                yr, yi = yr + mr * sr - mi * si, yi + mr * si + mi * sr
            pr, pi = tabs_v[6], tabs_v[7]
            yr, yi = yr + pr * cr - pi * ci, yi + pr * ci + pi * cr
            x_ref[0, pl.ds(off, 8), :] = yr
            x_ref[1, pl.ds(off, 8), :] = yi
            last = 0 if reverse else 7
            out = (jnp.broadcast_to(yr[last:last + 1, :], (8, LANES)),
                   jnp.broadcast_to(yi[last:last + 1, :], (8, LANES)))
            if with_ga:
                xr = xs_ref[0, pl.ds(off, 8), :]
                xi = xs_ref[1, pl.ds(off, 8), :]
                poff = pl.multiple_of(jnp.maximum(b - 1, 0) * 8, 8)
                live = (b > 0).astype(F32)
                pvr = jnp.broadcast_to(xs_ref[0, pl.ds(poff, 8), :][7:8, :], (8, LANES)) * live
                pvi = jnp.broadcast_to(xs_ref[1, pl.ds(poff, 8), :][7:8, :], (8, LANES)) * live
                xpr = jnp.where(row == 0, pvr, pltpu.roll(xr, 1, 0))
                xpi = jnp.where(row == 0, pvi, pltpu.roll(xi, 1, 0))
                out = out + (carry[2] + xpr * yr + xpi * yi, carry[3] + xpr * yi - xpi * yr)
            return out

        init = (zero, zero, zero, zero) if with_ga else (zero, zero)
        fin = lax.fori_loop(0, nb, step, init)
        if with_ga:
            ga_ref[0:1, :] = jnp.sum(fin[2], axis=0, keepdims=True)
            ga_ref[1:2, :] = jnp.sum(fin[3], axis=0, keepdims=True)

    seq = pl.BlockSpec((2, L, LANES), lambda g: (0, 0, g))
    in_specs = [seq, pl.BlockSpec((None, 8, 8, LANES), lambda g: (d, 0, 0, g))]
    out_specs = [seq]
    out_shape = [_sds((2, L, S5_NS))]
    args = [bu, tabs]
    if with_ga:
        in_specs.append(seq)
        args.append(xs)
        out_specs.append(pl.BlockSpec((2, LANES), lambda g: (0, g)))
        out_shape.append(_sds((2, S5_NS)))
    outs = pl.pallas_call(
        body, name=name, grid=(S5_NS // LANES,), in_specs=in_specs, out_specs=out_specs, out_shape=out_shape,
        compiler_params=_cp("parallel"),
    )(*args)
    return outs if with_ga else outs[0]


def _s5_out_fwd(xs, cset, z, dvec, wglu, name):
    L = z.shape[0]
    bl = min(256, L)

    def body(x_ref, c_ref, u_ref, d_ref, w_ref, ylin_ref, ya_ref):
        cols = []
        for j in range(4):
            xr = x_ref[0, :, 512 * j:512 * (j + 1)].astype(BF16)
            xi = x_ref[1, :, 512 * j:512 * (j + 1)].astype(BF16)
            cr = c_ref[0, LANES * j:LANES * (j + 1), :]
            ci = c_ref[1, LANES * j:LANES * (j + 1), :]
            cols.append(_dot(xr, cr, NT) - _dot(xi, ci, NT))
        ylin = jnp.concatenate(cols, axis=1) + d_ref[...] * u_ref[...]
        yg = _gelu(ylin)
        t = _dot(yg.astype(BF16), w_ref[...])
        ylin_ref[...] = ylin
        ya_ref[...] = (yg * _sigmoid(t)).astype(BF16)

    return pl.pallas_call(
        body, name=name, grid=(L // bl,),
        in_specs=[pl.BlockSpec((2, bl, S5_NS), lambda i: (0, i, 0)),
                  pl.BlockSpec((2, S5_W, 512), lambda i: (0, 0, 0)),
                  pl.BlockSpec((bl, S5_W), lambda i: (i, 0)),
                  pl.BlockSpec((1, S5_W), lambda i: (0, 0)),
                  pl.BlockSpec((S5_W, S5_W), lambda i: (0, 0))],
        out_specs=[pl.BlockSpec((bl, S5_W), lambda i: (i, 0))] * 2,
        out_shape=[_sds((L, S5_W)), _sds((L, S5_W), BF16)],
        compiler_params=_cp("parallel"),
    )(xs, cset, z, dvec, wglu)


def _s5_glu_bwd(g_m, ylin, z, dvec, wglu, name):
    L = z.shape[0]
    bl = min(256, L)

    def body(g_ref, ylin_ref, u_ref, d_ref, w_ref, gyl_ref, gud_ref, gw_ref, gd_ref):
        i = pl.program_id(0)
        ylin = ylin_ref[...]
        yg = _gelu(ylin)
        ygb = yg.astype(BF16)
        sg = _sigmoid(_dot(ygb, w_ref[...]))
        gya = g_ref[...]
        gt = gya * yg * sg * (1.0 - sg)
        gtb = gt.astype(BF16)
        gyg = gya * sg + _dot(gtb, w_ref[...], NT)
        gyl = gyg * _gelu_grad(ylin)
        gyl_ref[...] = gyl
        gud_ref[...] = gyl * d_ref[...]

        @pl.when(i == 0)
        def _():
            gw_ref[...] = jnp.zeros_like(gw_ref)
            gd_ref[...] = jnp.zeros_like(gd_ref)

        gw_ref[...] += _dot(ygb, gtb, TN)
        gd_ref[...] += jnp.sum(gyl * u_ref[...], axis=0, keepdims=True)

    blk = pl.BlockSpec((bl, S5_W), lambda i: (i, 0))
    return pl.pallas_call(
        body, name=name, grid=(L // bl,),
        in_specs=[blk, blk, blk, pl.BlockSpec((1, S5_W), lambda i: (0, 0)),
                  pl.BlockSpec((S5_W, S5_W), lambda i: (0, 0))],
        out_specs=[blk, blk, pl.BlockSpec((S5_W, S5_W), lambda i: (0, 0)), pl.BlockSpec((1, S5_W), lambda i: (0, 0))],
        out_shape=[_sds((L, S5_W)), _sds((L, S5_W)), _sds((S5_W, S5_W)), _sds((1, S5_W))],
        compiler_params=_cp("arbitrary"),
    )(g_m, ylin, z, dvec, wglu)


def _s5_c_bwd(gyl, xs, cset, name):
    L = gyl.shape[0]
    bl = min(256, L)

    def body(g_ref, x_ref, c_ref, gx_ref, gc_ref):
        i = pl.program_id(0)

        @pl.when(i == 0)
        def _():
            gc_ref[...] = jnp.zeros_like(gc_ref)

        for j in range(4):
            gj = g_ref[:, LANES * j:LANES * (j + 1)].astype(BF16)
            cr = c_ref[0, LANES * j:LANES * (j + 1), :]
            ci = c_ref[1, LANES * j:LANES * (j + 1), :]
            gx_ref[0, :, 512 * j:512 * (j + 1)] = _dot(gj, cr)
            gx_ref[1, :, 512 * j:512 * (j + 1)] = -_dot(gj, ci)
            xr = x_ref[0, :, 512 * j:512 * (j + 1)].astype(BF16)
            xi = x_ref[1, :, 512 * j:512 * (j + 1)].astype(BF16)
            gc_ref[0, LANES * j:LANES * (j + 1), :] += _dot(gj, xr, TN)
            gc_ref[1, LANES * j:LANES * (j + 1), :] -= _dot(gj, xi, TN)

    return pl.pallas_call(
        body, name=name, grid=(L // bl,),
        in_specs=[pl.BlockSpec((bl, S5_W), lambda i: (i, 0)),
                  pl.BlockSpec((2, bl, S5_NS), lambda i: (0, i, 0)),
                  pl.BlockSpec((2, S5_W, 512), lambda i: (0, 0, 0))],
        out_specs=[pl.BlockSpec((2, bl, S5_NS), lambda i: (0, i, 0)),
                   pl.BlockSpec((2, S5_W, 512), lambda i: (0, 0, 0))],
        out_shape=[_sds((2, L, S5_NS)), _sds((2, S5_W, 512))],
        compiler_params=_cp("arbitrary"),
    )(gyl, xs, cset)


def _s5_bu_bwd(gx, bset, z, gud, name):
    L = z.shape[0]
    bl = min(512, L)

    def body(gx_ref, b_ref, u_ref, gud_ref, gu_ref, gb_ref):
        i = pl.program_id(1)
        gr = gx_ref[0].astype(BF16)
        gi = gx_ref[1].astype(BF16)
        gu_ref[...] = gud_ref[...] + _dot(gr, b_ref[0]) + _dot(gi, b_ref[1])

        @pl.when(i == 0)
        def _():
            gb_ref[...] = jnp.zeros_like(gb_ref)

        u = u_ref[...].astype(BF16)
        gb_ref[0] += _dot(gr, u, TN)
        gb_ref[1] += _dot(gi, u, TN)

    return pl.pallas_call(
        body, name=name, grid=(4, L // bl),
        in_specs=[pl.BlockSpec((2, bl, 512), lambda j, i: (0, i, j)),
                  pl.BlockSpec((2, 512, LANES), lambda j, i: (0, j, 0)),
                  pl.BlockSpec((bl, LANES), lambda j, i: (i, j)),
                  pl.BlockSpec((bl, LANES), lambda j, i: (i, j))],
        out_specs=[pl.BlockSpec((bl, LANES), lambda j, i: (i, j)),
                   pl.BlockSpec((2, 512, LANES), lambda j, i: (0, j, 0))],
        out_shape=[_sds((L, S5_W)), _sds((2, S5_NS, LANES))],
        compiler_params=_cp("parallel", "arbitrary"),
    )(gx, bset, z, gud)


def _s5_param_bwd(lam_c, ldt_c, b_t, gb, ga_c, gc, name):
    def body(lam_ref, ldt_ref, b_ref, gb_ref, ga_ref, gc_ref, glam_ref, gldt_ref, gbo_ref, gco_ref):
        lr, li = lam_ref[:, 0:1], lam_ref[:, 1:2]
        dt, ar, ai, qr, qi, den = _zoh_cols(lr, li, ldt_ref[...])
        bm = _b_mask()
        gbr = jnp.where(bm, gb_ref[0], 0.0)
        gbi = jnp.where(bm, gb_ref[1], 0.0)
        br, bi = b_ref[0], b_ref[1]
        obr = gbr * qr + gbi * qi
        obi = gbi * qr - gbr * qi
        gqr = jnp.sum(gbr * br + gbi * bi, axis=1, keepdims=True)
        gqi = jnp.sum(gbi * br - gbr * bi, axis=1, keepdims=True)
        for s in (64, 32, 16):
            obr = obr + pltpu.roll(obr, s, 1)
            obi = obi + pltpu.roll(obi, s, 1)
        gbo_ref[0] = obr
        gbo_ref[1] = obi
        gar = ga_ref[:, 0:1] + (gqr * lr - gqi * li) / den
        gai = ga_ref[:, 1:2] + (gqr * li + gqi * lr) / den
        qlr = (qr * lr + qi * li) / den
        qli = (qi * lr - qr * li) / den
        glr = -(gqr * qlr + gqi * qli)
        gli = -(gqi * qlr - gqr * qli)
        glr = glr + dt * (gar * ar + gai * ai)
        gli = gli + dt * (gai * ar - gar * ai)
        wr, wi = _cmul(lr, li, ar, ai)
        gldt = (gar * wr + gai * wi) * dt
        glam_ref[:, 0:1] = glr
        glam_ref[:, 1:2] = gli
        r = lax.broadcasted_iota(jnp.int32, (S5_NS, 32), 0)
        c = lax.broadcasted_iota(jnp.int32, (S5_NS, 32), 1)
        gldt_ref[...] = jnp.sum(jnp.where((r >> 6) == c, gldt, 0.0), axis=0, keepdims=True)
        cm = _c_mask()
        for k in range(2):
            oc = jnp.where(cm, gc_ref[k], 0.0)
            for s in (256, 128, 64):
                oc = oc + pltpu.roll(oc, s, 1)
            gco_ref[k] = oc[:, 0:LANES]

    vm = pl.BlockSpec(memory_space=pltpu.VMEM)
    return pl.pallas_call(
        body, name=name, in_specs=[vm] * 6, out_specs=[vm] * 4,
        out_shape=[_sds((S5_NS, 2)), _sds((1, 32)), _sds((2, S5_NS, LANES)), _sds((2, S5_W, LANES))],
        compiler_params=pltpu.CompilerParams(vmem_limit_bytes=VMEM_LIMIT),
    )(lam_c, ldt_c, b_t, gb, ga_c, gc)


FL_BLK = EVEN_PAD // LANES - 1
Q_BLK, K_BLK, V_BLK = 4, 8, 12
NEG = -1e30


def _log_sigmoid(v):
    return jnp.minimum(v, 0.0) - jnp.log(1.0 + jnp.exp(-jnp.abs(v)))


def _fox_f_fwd(z, bf, name):
    L = z.shape[0]
    tl = min(256, L)

    def body(fl_ref, b_ref, f_ref, carry_ref):
        i = pl.program_id(0)

        @pl.when(i == 0)
        def _():
            carry_ref[...] = jnp.zeros_like(carry_ref)

        lf = _log_sigmoid(fl_ref[...] + b_ref[...])
        r = lax.broadcasted_iota(jnp.int32, (tl, tl), 0)
        c = lax.broadcasted_iota(jnp.int32, (tl, tl), 1)
        tri = (r >= c).astype(F32)
        cs = lax.dot_general(tri, lf, NN, precision=lax.Precision.HIGHEST, preferred_element_type=F32) + carry_ref[...]
        f_ref[...] = cs
        carry_ref[...] = cs[tl - 1:tl, :]

    return pl.pallas_call(
        body, name=name, grid=(L // tl,),
        in_specs=[pl.BlockSpec((tl, LANES), lambda i: (i, FL_BLK)), pl.BlockSpec((1, LANES), lambda i: (0, 0))],
        out_specs=pl.BlockSpec((tl, LANES), lambda i: (i, 0)),
        out_shape=_sds((L, LANES)),
        scratch_shapes=[pltpu.VMEM((1, LANES), F32)],
        compiler_params=_cp("arbitrary"),
    )(z, bf)


def _fox_f_bwd(dFk, dFq, z, bf, name):
    L = z.shape[0]
    tl = min(256, L)
    nb = L // tl

    def body(dfk_ref, dfq_ref, fl_ref, b_ref, dfl_ref, db_ref, carry_ref):
        i = pl.program_id(0)

        @pl.when(i == 0)
        def _():
            carry_ref[...] = jnp.zeros_like(carry_ref)
            db_ref[...] = jnp.zeros_like(db_ref)

        r = lax.broadcasted_iota(jnp.int32, (tl, tl), 0)
        c = lax.broadcasted_iota(jnp.int32, (tl, tl), 1)
        tri = (r <= c).astype(F32)
        cs = lax.dot_general(tri, dfk_ref[...] + dfq_ref[...], NN, precision=lax.Precision.HIGHEST,
                             preferred_element_type=F32) + carry_ref[...]
        carry_ref[...] = cs[0:1, :]
        dfl = cs * _sigmoid(-(fl_ref[...] + b_ref[...]))
        dfl_ref[...] = dfl
        db_ref[...] += jnp.sum(dfl, axis=0, keepdims=True)

    return pl.pallas_call(
        body, name=name, grid=(nb,),
        in_specs=[pl.BlockSpec((tl, LANES), lambda i: (nb - 1 - i, 0)),
                  pl.BlockSpec((tl, LANES), lambda i: (nb - 1 - i, 0)),
                  pl.BlockSpec((tl, LANES), lambda i: (nb - 1 - i, FL_BLK)),
                  pl.BlockSpec((1, LANES), lambda i: (0, 0))],
        out_specs=[pl.BlockSpec((tl, LANES), lambda i: (nb - 1 - i, 0)), pl.BlockSpec((1, LANES), lambda i: (0, 0))],
        out_shape=[_sds((L, LANES)), _sds((1, LANES))],
        scratch_shapes=[pltpu.VMEM((1, LANES), F32)],
        compiler_params=_cp("arbitrary"),
    )(dFk, dFq, z, bf)


def _head_mask(hh):
    lane = lax.broadcasted_iota(jnp.int32, (1, LANES), 1)
    return (lane >> 6) == hh


def _fox_scores(q_scaled, k, fq_ref, fr_ref, hh, causal):
    qh = jnp.where(_head_mask(hh), q_scaled, 0.0).astype(BF16)
    s = _dot(qh, k, NT) + (fq_ref[:, 64 * hh:64 * hh + 1] - fr_ref[hh:hh + 1, :])
    return jnp.where(causal, s, NEG)


def _causal(qi, ki, T):
    r = qi * T + lax.broadcasted_iota(jnp.int32, (T, T), 0)
    c = ki * T + lax.broadcasted_iota(jnp.int32, (T, T), 1)
    return c <= r


def _fox_fwd(z, fq, frow, name):
    L = z.shape[0]
    T = min(512, L)
    nq = L // T

    def body(q_ref, k_ref, v_ref, fq_ref, fr_ref, o_ref, lse_ref, m_ref, l_ref, acc_ref):
        qi, ki = pl.program_id(1), pl.program_id(2)

        @pl.when(ki == 0)
        def _():
            m_ref[...] = jnp.full_like(m_ref, NEG)
            l_ref[...] = jnp.zeros_like(l_ref)
            acc_ref[...] = jnp.zeros_like(acc_ref)

        @pl.when(ki <= qi)
        def _():
            q = q_ref[...] * 0.125
            k = k_ref[...].astype(BF16)
            v = v_ref[...].astype(BF16)
            causal = _causal(qi, ki, T)
            for hh in range(2):
                s = _fox_scores(q, k, fq_ref, fr_ref, hh, causal)
                m_old = m_ref[hh]
                m_new = jnp.maximum(m_old, jnp.max(s, axis=1, keepdims=True))
                alpha = jnp.exp(m_old - m_new)
                p = jnp.exp(s - m_new)
                l_ref[hh] = alpha * l_ref[hh] + jnp.sum(p, axis=1, keepdims=True)
                m_ref[hh] = m_new
                pv = _dot(p.astype(BF16), v)
                acc = acc_ref[...]
                acc_ref[...] = jnp.where(_head_mask(hh), alpha * acc + pv, acc)

        @pl.when(ki == nq - 1)
        def _():
            h0 = _head_mask(0)
            inv = jnp.where(h0, 1.0 / l_ref[0], 1.0 / l_ref[1])
            o_ref[...] = acc_ref[...] * inv
            lse_ref[...] = jnp.where(h0, m_ref[0] + jnp.log(l_ref[0]), m_ref[1] + jnp.log(l_ref[1]))

    def zspec(base, rowf):
        return pl.BlockSpec((T, LANES), lambda j, qi, ki: (rowf(qi, ki), base + j))

    kv_row = lambda qi, ki: jnp.minimum(ki, qi)
    q_row = lambda qi, ki: qi
    return pl.pallas_call(
        body, name=name, grid=(4, nq, nq),
        in_specs=[zspec(Q_BLK, q_row), zspec(K_BLK, kv_row), zspec(V_BLK, kv_row),
                  pl.BlockSpec((T, LANES), lambda j, qi, ki: (qi, j)),
                  pl.BlockSpec((None, 2, T), lambda j, qi, ki: (j, 0, jnp.minimum(ki, qi)))],
        out_specs=[pl.BlockSpec((T, LANES), lambda j, qi, ki: (qi, j))] * 2,
        out_shape=[_sds((L, FOX_W)), _sds((L, FOX_W))],
        scratch_shapes=[pltpu.VMEM((2, T, 1), F32), pltpu.VMEM((2, T, 1), F32), pltpu.VMEM((T, LANES), F32)],
        compiler_params=_cp("parallel", "parallel", "arbitrary"),
    )(z, z, z, fq, frow)


def _fox_delta(dob, o, hh):
    return jnp.sum(jnp.where(_head_mask(hh), dob.astype(F32) * o, 0.0), axis=1, keepdims=True)


def _fox_bwd_dq(z, fq, frow, o, lse, g_m, name):
    L = z.shape[0]
    T = min(512, L)
    nq = L // T

    def body(q_ref, k_ref, v_ref, fq_ref, fr_ref, o_ref, lse_ref, do_ref, dq_ref, df_ref, acc_ref, rs_ref):
        qi, ki = pl.program_id(1), pl.program_id(2)

        @pl.when(ki == 0)
        def _():
            acc_ref[...] = jnp.zeros_like(acc_ref)
            rs_ref[...] = jnp.zeros_like(rs_ref)

        @pl.when(ki <= qi)
        def _():
            q = q_ref[...] * 0.125
            k = k_ref[...].astype(BF16)
            v = v_ref[...].astype(BF16)
            dob = do_ref[...].astype(BF16)
            causal = _causal(qi, ki, T)
            for hh in range(2):
                s = _fox_scores(q, k, fq_ref, fr_ref, hh, causal)
                p = jnp.exp(s - lse_ref[:, 64 * hh:64 * hh + 1])
                dp = _dot(jnp.where(_head_mask(hh), dob, 0.0), v, NT)
                ds = p * (dp - _fox_delta(dob, o_ref[...], hh))
                rs_ref[hh] += jnp.sum(ds, axis=1, keepdims=True)
                acc = acc_ref[...]
                acc_ref[...] = jnp.where(_head_mask(hh), acc + _dot(ds.astype(BF16), k), acc)

        @pl.when(ki == nq - 1)
        def _():
            dq_ref[...] = acc_ref[...] * 0.125
            df_ref[...] = jnp.where(_head_mask(0), rs_ref[0], rs_ref[1])

    def zspec(base, rowf):
        return pl.BlockSpec((T, LANES), lambda j, qi, ki: (rowf(qi, ki), base + j))

    kv_row = lambda qi, ki: jnp.minimum(ki, qi)
    q_row = lambda qi, ki: qi
    qblk = pl.BlockSpec((T, LANES), lambda j, qi, ki: (qi, j))
    return pl.pallas_call(
        body, name=name, grid=(4, nq, nq),
        in_specs=[zspec(Q_BLK, q_row), zspec(K_BLK, kv_row), zspec(V_BLK, kv_row), qblk,
                  pl.BlockSpec((None, 2, T), lambda j, qi, ki: (j, 0, jnp.minimum(ki, qi))),
                  qblk, qblk, pl.BlockSpec((T, LANES), lambda j, qi, ki: (qi, 4 + j))],
        out_specs=[qblk, qblk],
        out_shape=[_sds((L, FOX_W)), _sds((L, FOX_W))],
        scratch_shapes=[pltpu.VMEM((T, LANES), F32), pltpu.VMEM((2, T, 1), F32)],
        compiler_params=_cp("parallel", "parallel", "arbitrary"),
    )(z, z, z, fq, frow, o, lse, g_m)


def _fox_bwd_dkv(z, fq, frow, o, lse, g_m, name):
    L = z.shape[0]
    T = min(512, L)
    nq = L // T

    def body(q_ref, k_ref, v_ref, fq_ref, fr_ref, o_ref, lse_ref, do_ref, dk_ref, dv_ref, df_ref,
             dk_acc, dv_acc, df_acc):
        ki, qi = pl.program_id(1), pl.program_id(2)

        @pl.when(qi == 0)
        def _():
            dk_acc[...] = jnp.zeros_like(dk_acc)
            dv_acc[...] = jnp.zeros_like(dv_acc)
            df_acc[...] = jnp.zeros_like(df_acc)

        @pl.when(qi >= ki)
        def _():
            q = q_ref[...] * 0.125
            qb = q.astype(BF16)
            k = k_ref[...].astype(BF16)
            v = v_ref[...].astype(BF16)
            dob = do_ref[...].astype(BF16)
            causal = _causal(qi, ki, T)
            for hh in range(2):
                hm = _head_mask(hh)
                s = _fox_scores(q, k, fq_ref, fr_ref, hh, causal)
                p = jnp.exp(s - lse_ref[:, 64 * hh:64 * hh + 1])
                dp = _dot(jnp.where(hm, dob, 0.0), v, NT)
                ds = p * (dp - _fox_delta(dob, o_ref[...], hh))
                dv = dv_acc[...]
                dv_acc[...] = jnp.where(hm, dv + _dot(p.astype(BF16), dob, TN), dv)
                dk = dk_acc[...]
                dk_acc[...] = jnp.where(hm, dk + _dot(ds.astype(BF16), qb, TN), dk)
                df_acc[hh:hh + 1, :] -= jnp.sum(ds, axis=0, keepdims=True)

        @pl.when(qi == nq - 1)
        def _():
            dk_ref[...] = dk_acc[...]
            dv_ref[...] = dv_acc[...]
            df_ref[...] = df_acc[...]

    q_row = lambda ki, qi: jnp.maximum(qi, ki)

    def qside(base):
        return pl.BlockSpec((T, LANES), lambda j, ki, qi: (q_row(ki, qi), base + j))

    def kside(base):
        return pl.BlockSpec((T, LANES), lambda j, ki, qi: (ki, base + j))

    kblk = pl.BlockSpec((T, LANES), lambda j, ki, qi: (ki, j))
    frow_spec = pl.BlockSpec((None, 2, T), lambda j, ki, qi: (j, 0, ki))
    return pl.pallas_call(
        body, name=name, grid=(4, nq, nq),
        in_specs=[qside(Q_BLK), kside(K_BLK), kside(V_BLK), qside(0), frow_spec, qside(0), qside(0), qside(4)],
        out_specs=[kblk, kblk, frow_spec],
        out_shape=[_sds((L, FOX_W)), _sds((L, FOX_W)), _sds((4, 2, L))],
        scratch_shapes=[pltpu.VMEM((T, LANES), F32), pltpu.VMEM((T, LANES), F32), pltpu.VMEM((2, T), F32)],
        compiler_params=_cp("parallel", "parallel", "arbitrary"),
    )(z, z, z, fq, frow, o, lse, g_m)


def _shift_rows(v, s, down, row):
    n = v.shape[0]
    if down:
        return jnp.where(row >= s, pltpu.roll(v, s, 0), 0.0)
    return jnp.where(row < n - s, pltpu.roll(v, n - s, 0), 0.0)


def _window_sum(v, g, down, row):
    out = jnp.zeros_like(v)
    s = v
    for k in range(4):
        s = s + _shift_rows(s, 1 << k, down, row)
        out = jnp.where(g == k, s, out)
    return out


def _pool_inv_cnt(g, row):
    w = jnp.left_shift(2, g).astype(F32)
    return 1.0 / jnp.minimum(row.astype(F32) + 1.0, w)


def _pool_fwd(z, pool_w, scale, name):
    L = z.shape[0]

    def body(x_ref, w_ref, s_ref, y_ref, p_ref):
        g = pl.program_id(0)
        row = lax.broadcasted_iota(jnp.int32, (L, LANES), 0)
        x = x_ref[...]
        pooled = (_window_sum(x, g, True, row) * _pool_inv_cnt(g, row) - x).astype(BF16)
        p_ref[...] = pooled
        y_ref[...] = (_dot(pooled, w_ref[...].astype(BF16)) * s_ref[...]).astype(BF16)

    col = pl.BlockSpec((L, LANES), lambda g: (0, g))
    return pl.pallas_call(
        body, name=name, grid=(4,),
        in_specs=[col, pl.BlockSpec((None, LANES, LANES), lambda g: (g, 0, 0)), pl.BlockSpec((1, LANES), lambda g: (0, g))],
        out_specs=[col, col],
        out_shape=[_sds((L, 512), BF16), _sds((L, 512), BF16)],
        compiler_params=_cp("parallel"),
    )(z, pool_w, scale)


def _pool_bwd(g_m, pooled, pool_w, scale, name):
    L = g_m.shape[0]

    def body(g_ref, p_ref, w_ref, s_ref, gx_ref, gw_ref, gs_ref):
        g = pl.program_id(0)
        row = lax.broadcasted_iota(jnp.int32, (L, LANES), 0)
        gy = g_ref[...]
        pooled = p_ref[...]
        wb = w_ref[...].astype(BF16)
        lin = _dot(pooled, wb)
        gs_ref[...] = jnp.sum(gy * lin, axis=0, keepdims=True)
        glin = (gy * s_ref[...]).astype(BF16)
        gw_ref[...] = _dot(pooled, glin, TN)
        gp = _dot(glin, wb, NT)
        gx_ref[...] = _window_sum(gp * _pool_inv_cnt(g, row), g, False, row) - gp

    col = pl.BlockSpec((L, LANES), lambda g: (0, g))
    wspec = pl.BlockSpec((None, LANES, LANES), lambda g: (g, 0, 0))
    vec = pl.BlockSpec((1, LANES), lambda g: (0, g))
    return pl.pallas_call(
        body, name=name, grid=(4,),
        in_specs=[col, col, wspec, vec],
        out_specs=[col, wspec, vec],
        out_shape=[_sds((L, 512)), _sds((4, LANES, LANES)), _sds((1, 512))],
        compiler_params=_cp("parallel"),
    )(g_m, pooled, pool_w, scale)


SGU_CHUNKS = 4


def _sgu_ln(v, gam, bet):
    gv = _gelu(v)
    mu = jnp.mean(gv, axis=-1, keepdims=True)
    xc = gv - mu
    rs = lax.rsqrt(jnp.mean(xc * xc, axis=-1, keepdims=True) + EPS)
    xh = xc * rs
    return xh, rs, xh * gam + bet


def _tril_ws(w_ref, g):
    r = lax.broadcasted_iota(jnp.int32, (LANES, LANES), 0)
    c = lax.broadcasted_iota(jnp.int32, (LANES, LANES), 1)
    return jnp.where(r >= c, w_ref[g], 0.0).astype(BF16)


def _sgu_fwd(z, ln_g, ln_b, w_s, b_st, name):
    L = z.shape[0]
    rb = min(SGU_CHUNKS * LANES, L)

    def body(u_ref, v_ref, g_ref, b_ref, w_ref, bs_ref, y_ref):
        _, _, vln = _sgu_ln(v_ref[...], g_ref[...], b_ref[...])
        gu = _gelu(u_ref[...])
        vb = vln.astype(BF16)
        for g in range(4):
            ws = _tril_ws(w_ref, g)
            for n in range(rb // LANES):
                rows = slice(n * LANES, (n + 1) * LANES)
                cols = slice(g * LANES, (g + 1) * LANES)
                mixed = _dot(ws, vb[rows, cols]) + bs_ref[:, g:g + 1]
                y_ref[rows, cols] = (gu[rows, cols] * mixed).astype(BF16)

    vm = lambda shape: pl.BlockSpec(shape, lambda i: tuple(0 for _ in shape))
    return pl.pallas_call(
        body, name=name, grid=(L // rb,),
        in_specs=[pl.BlockSpec((rb, 512), lambda i: (i, 1)), pl.BlockSpec((rb, 512), lambda i: (i, 2)),
                  vm((1, 512)), vm((1, 512)), vm((4, LANES, LANES)), vm((LANES, 4))],
        out_specs=pl.BlockSpec((rb, 512), lambda i: (i, 0)),
        out_shape=_sds((L, 512), BF16),
        compiler_params=_cp("parallel"),
    )(z, z, ln_g, ln_b, w_s, b_st)


def _sgu_bwd(g_m, z, ln_g, ln_b, w_s, b_st, name):
    L = z.shape[0]
    rb = min(SGU_CHUNKS * LANES, L)

    def body(gy_ref, u_ref, v_ref, g_ref, b_ref, w_ref, bs_ref, gu_ref, gv_ref, gw_ref, gbs_ref, gg_ref, gb_ref):
        i = pl.program_id(0)

        @pl.when(i == 0)
        def _():
            gw_ref[...] = jnp.zeros_like(gw_ref)
            gbs_ref[...] = jnp.zeros_like(gbs_ref)
            gg_ref[...] = jnp.zeros_like(gg_ref)
            gb_ref[...] = jnp.zeros_like(gb_ref)

        v = v_ref[...]
        u = u_ref[...]
        gy = gy_ref[...]
        xh, rs, vln = _sgu_ln(v, g_ref[...], b_ref[...])
        gel_u = _gelu(u)
        gmix = gy * gel_u
        vb = vln.astype(BF16)
        gmb = gmix.astype(BF16)
        r = lax.broadcasted_iota(jnp.int32, (LANES, LANES), 0)
        c = lax.broadcasted_iota(jnp.int32, (LANES, LANES), 1)
        gvln_cols = []
        for g in range(4):
            ws = _tril_ws(w_ref, g)
            cols = slice(g * LANES, (g + 1) * LANES)
            gw = jnp.zeros((LANES, LANES), F32)
            gbs = jnp.zeros((LANES, 1), F32)
            parts = []
            for n in range(rb // LANES):
                rows = slice(n * LANES, (n + 1) * LANES)
                mixed = _dot(ws, vb[rows, cols]) + bs_ref[:, g:g + 1]
                gu_ref[rows, cols] = gy[rows, cols] * mixed * _gelu_grad(u[rows, cols])
                parts.append(_dot(ws, gmb[rows, cols], TN))
                gw = gw + _dot(gmb[rows, cols], vb[rows, cols], NT)
                gbs = gbs + jnp.sum(gmix[rows, cols], axis=1, keepdims=True)
            gvln_cols.append(jnp.concatenate(parts, axis=0))
            gw_ref[g] += jnp.where(r >= c, gw, 0.0)
            gbs_ref[:, g:g + 1] += gbs
        gvln = jnp.concatenate(gvln_cols, axis=1)
        gg_ref[...] += jnp.sum(gvln * xh, axis=0, keepdims=True)
        gb_ref[...] += jnp.sum(gvln, axis=0, keepdims=True)
        gxh = gvln * g_ref[...]
        ggv = rs * (gxh - jnp.mean(gxh, axis=-1, keepdims=True) - xh * jnp.mean(gxh * xh, axis=-1, keepdims=True))
        gv_ref[...] = ggv * _gelu_grad(v)

    vm = lambda shape: pl.BlockSpec(shape, lambda i: tuple(0 for _ in shape))
    blk = pl.BlockSpec((rb, 512), lambda i: (i, 0))
    return pl.pallas_call(
        body, name=name, grid=(L // rb,),
        in_specs=[pl.BlockSpec((rb, 512), lambda i: (i, 1)), pl.BlockSpec((rb, 512), lambda i: (i, 1)),
                  pl.BlockSpec((rb, 512), lambda i: (i, 2)),
                  vm((1, 512)), vm((1, 512)), vm((4, LANES, LANES)), vm((LANES, 4))],
        out_specs=[blk, blk, vm((4, LANES, LANES)), vm((LANES, 4)), vm((1, 512)), vm((1, 512))],
        out_shape=[_sds((L, 512)), _sds((L, 512)), _sds((4, LANES, LANES)), _sds((LANES, 4)),
                   _sds((1, 512)), _sds((1, 512))],
        compiler_params=_cp("arbitrary"),
    )(g_m, z, z, ln_g, ln_b, w_s, b_st)


def _adamw_math(w, g, m, v):
    nm = ADAM_B1 * m + (1.0 - ADAM_B1) * g
    nv = ADAM_B2 * v + (1.0 - ADAM_B2) * (g * g)
    m_hat = nm / (1.0 - ADAM_B1 ** ADAM_STEP)
    v_hat = nv / (1.0 - ADAM_B2 ** ADAM_STEP)
    delta = -ADAM_LR * (m_hat / (jnp.sqrt(v_hat) + ADAM_EPS) + ADAM_WD * w)
    return delta, nm, nv


def _sum_adamw(parts, w, m, v, name):
    R, C = w.shape
    rb = min(128, R)

    def body(p_ref, w_ref, m_ref, v_ref, g_ref, d_ref, nm_ref, nv_ref):
        g = p_ref[0]
        for s in range(1, N_DEV):
            g = g + p_ref[s]
        d, nm, nv = _adamw_math(w_ref[...], g, m_ref[...], v_ref[...])
        g_ref[...] = g
        d_ref[...] = d
        nm_ref[...] = nm
        nv_ref[...] = nv

    blk = pl.BlockSpec((rb, C), lambda i: (i, 0))
    return pl.pallas_call(
        body, name=name, grid=(R // rb,),
        in_specs=[pl.BlockSpec((N_DEV, rb, C), lambda i: (0, i, 0)), blk, blk, blk],
        out_specs=[blk] * 4, out_shape=[_sds((R, C))] * 4,
        compiler_params=_cp("parallel"),
    )(parts, w, m, v)


def _sum_pieces(parts, name):
    _, R, C = parts.shape

    def body(p_ref, g_ref):
        g = p_ref[0]
        for s in range(1, N_DEV):
            g = g + p_ref[s]
        g_ref[...] = g

    vm = pl.BlockSpec(memory_space=pltpu.VMEM)
    return pl.pallas_call(body, name=name, in_specs=[vm], out_specs=vm, out_shape=_sds((R, C)))(parts)


def _adamw(w, g, m, v, name):
    vm = pl.BlockSpec(memory_space=pltpu.VMEM)

    def body(w_ref, g_ref, m_ref, v_ref, d_ref, nm_ref, nv_ref):
        d, nm, nv = _adamw_math(w_ref[...], g_ref[...], m_ref[...], v_ref[...])
        d_ref[...] = d
        nm_ref[...] = nm
        nv_ref[...] = nv

    return pl.pallas_call(body, name=name, in_specs=[vm] * 4, out_specs=[vm] * 3,
                          out_shape=[_sds(w.shape)] * 3)(w, g, m, v)


def _mesh_pos():
    return lax.axis_index("x"), lax.axis_index("y"), lax.axis_index("c")


def _dev_index(p):
    return 4 * p[0] + 2 * p[1] + p[2]


def _all_gather(xs, name):
    n = len(xs)

    def body(*refs):
        x_refs, o_refs = refs[:n], refs[n:2 * n]
        send_sems, recv_sems, local_sems = refs[2 * n:]
        x, y, c = _mesh_pos()
        me, sibling = (x, y, c), (x, y, 1 - c)
        chips = [(1 - x, y), (x, 1 - y), (1 - x, 1 - y)]

        def copy(i, k, block, to, src=None):
            dst = o_refs[i].at[_dev_index(block)]
            return pltpu.make_async_remote_copy(
                src_ref=dst if src is None else src, dst_ref=dst,
                send_sem=send_sems.at[i, k], recv_sem=recv_sems.at[i, k], device_id=to, device_id_type=MESH)

        mine = [pltpu.make_async_copy(x_refs[i], o_refs[i].at[_dev_index(me)], local_sems.at[i]) for i in range(n)]
        for cp in mine:
            cp.start()
        first = []
        for i in range(n):
            first.append(copy(i, 0, me, sibling, src=x_refs[i]))
            first += [copy(i, 1 + j, me, (*chip, c), src=x_refs[i]) for j, chip in enumerate(chips)]
        for cp in first:
            cp.start()
        passed = []
        for j, chip in enumerate(chips):
            for i in range(n):
                copy(i, 1 + j, (*chip, c), me).wait_recv()
                fwd = copy(i, 4 + j, (*chip, c), sibling)
                fwd.start()
                passed.append(fwd)
        for i in range(n):
            copy(i, 0, sibling, me).wait_recv()
            for j, chip in enumerate(chips):
                copy(i, 4 + j, (*chip, 1 - c), me).wait_recv()
        for cp in first + passed:
            cp.wait_send()
        for cp in mine:
            cp.wait()

    outs = pl.pallas_call(
        body, name=name,
        in_specs=[ANY] * n, out_specs=[ANY] * n,
        out_shape=[_sds((N_DEV,) + x.shape, x.dtype) for x in xs],
        scratch_shapes=[pltpu.SemaphoreType.DMA((n, 7)), pltpu.SemaphoreType.DMA((n, 7)),
                        pltpu.SemaphoreType.DMA((n,))],
    )(*xs)
    return list(outs)


def _exchange(gs, name):
    n = len(gs)

    def body(*refs):
        g_refs, o_refs = refs[:n], refs[n:2 * n]
        send_sems, recv_sems, local_sems = refs[2 * n:]
        x, y, c = _mesh_pos()
        me = (x, y, c)
        mi = _dev_index(me)
        peers = [(x ^ dx, y ^ dy, c ^ dc) for dx in range(2) for dy in range(2) for dc in range(2)][1:]

        def copy(i, k, peer):
            return pltpu.make_async_remote_copy(
                src_ref=g_refs[i].at[_dev_index(peer)], dst_ref=o_refs[i].at[mi],
                send_sem=send_sems.at[i, k], recv_sem=recv_sems.at[i, k], device_id=peer, device_id_type=MESH)

        mine = [pltpu.make_async_copy(g_refs[i].at[mi], o_refs[i].at[mi], local_sems.at[i]) for i in range(n)]
        for cp in mine:
            cp.start()
        sends = [copy(i, k, peer) for i in range(n) for k, peer in enumerate(peers)]
        for cp in sends:
            cp.start()
        for i in range(n):
            for k, peer in enumerate(peers):
                pltpu.make_async_remote_copy(
                    src_ref=g_refs[i].at[mi], dst_ref=o_refs[i].at[_dev_index(peer)],
                    send_sem=send_sems.at[i, k], recv_sem=recv_sems.at[i, k], device_id=peer,
                    device_id_type=MESH).wait_recv()
        for cp in sends:
            cp.wait_send()
        for cp in mine:
            cp.wait()

    outs = pl.pallas_call(
        body, name=name,
        in_specs=[ANY] * n, out_specs=[ANY] * n,
        out_shape=[_sds(g.shape, g.dtype) for g in gs],
        scratch_shapes=[pltpu.SemaphoreType.DMA((n, 7)), pltpu.SemaphoreType.DMA((n, 7)),
                        pltpu.SemaphoreType.DMA((n,))],
    )(*gs)
    return list(outs)


HBM = pl.BlockSpec(memory_space=pltpu.HBM)
SEM = pl.BlockSpec(memory_space=pltpu.SEMAPHORE)
EFFECT = pltpu.SideEffectType.DATAFLOW_SIDE_EFFECTING


def _peer_list():
    x, y, c = _mesh_pos()
    peers = [(x ^ dx, y ^ dy, c ^ dc) for dx in range(2) for dy in range(2) for dc in range(2)][1:]
    return (x, y, c), peers


def _split_copy(src_ref, land_ref, send_sems, recv_sems, i, k, peer, slot, exchange):
    return pltpu.make_async_remote_copy(
        src_ref=src_ref.at[_dev_index(peer)] if exchange else src_ref, dst_ref=land_ref.at[slot],
        send_sem=send_sems.at[7 * i + k], recv_sem=recv_sems.at[7 * i + k], device_id=peer, device_id_type=MESH)


def _comm_start(groups, name, exchange):
    sizes = [len(g) for g in groups]
    n = sum(sizes)
    srcs = [a for g in groups for a in g]
    my_index = _dev_index(_mesh_pos())
    lands = []
    for a in srcs:
        if exchange:
            own = lax.dynamic_slice(a, (my_index, 0, 0), (1,) + a.shape[1:])
            shape = a.shape
        else:
            own = a[None]
            shape = (N_DEV,) + a.shape
        lands.append(lax.dynamic_update_slice(lax.empty(shape, a.dtype), own, (my_index, 0, 0)))

    def body(*refs):
        src_refs, land_refs = refs[:n], refs[n:2 * n]
        sem_refs = refs[2 * n:2 * n + 2 * len(sizes)]
        token_ref = refs[-1]
        me, peers = _peer_list()
        mi = _dev_index(me)
        i = 0
        for gi, sz in enumerate(sizes):
            for j in range(sz):
                for k, peer in enumerate(peers):
                    _split_copy(src_refs[i], land_refs[i], sem_refs[2 * gi], sem_refs[2 * gi + 1], j, k, peer, mi,
                                exchange).start()
                i += 1
        token_ref[...] = jnp.zeros_like(token_ref)

    sem_shapes = []
    for sz in sizes:
        sem_shapes += [pltpu.SemaphoreType.DMA((7 * sz,)), pltpu.SemaphoreType.DMA((7 * sz,))]
    thru = [pltpu.HBM(a.shape, a.dtype) for a in srcs + lands]
    n_sem = len(sem_shapes)
    outs = pl.pallas_call(
        body, name=name,
        out_shape=tuple(sem_shapes + thru + [_sds((8, LANES))]),
        in_specs=[HBM] * (2 * n),
        out_specs=tuple([SEM] * n_sem + [HBM] * (2 * n) + [pl.BlockSpec(memory_space=pltpu.VMEM)]),
        input_output_aliases={i: n_sem + i for i in range(2 * n)},
        compiler_params=pltpu.CompilerParams(has_side_effects=EFFECT),
    )(*[pltpu.with_memory_space_constraint(a, pltpu.HBM) for a in srcs + lands])
    sems, thru_src, thru_land, token = outs[:n_sem], outs[n_sem:n_sem + n], outs[n_sem + n:n_sem + 2 * n], outs[-1]
    result, off = [], 0
    for gi, sz in enumerate(sizes):
        result.append((sems[2 * gi], sems[2 * gi + 1], list(thru_src[off:off + sz]), list(thru_land[off:off + sz])))
        off += sz
    return result, token


def _comm_wait(group, after, name, exchange):
    send_sems, recv_sems, srcs, lands = group
    n = len(srcs)

    def body(*refs):
        src_refs, land_refs = refs[:n], refs[n:2 * n]
        ssem, rsem = refs[2 * n], refs[2 * n + 1]
        me, peers = _peer_list()
        for i in range(n):
            for k, peer in enumerate(peers):
                cp = _split_copy(src_refs[i], land_refs[i], ssem, rsem, i, k, peer, _dev_index(peer), exchange)
                cp.wait_send()
                cp.wait_recv()

    outs = pl.pallas_call(
        body, name=name,
        out_shape=tuple(pltpu.HBM(a.shape, a.dtype) for a in srcs + lands),
        in_specs=[HBM] * (2 * n) + [SEM, SEM, ANY],
        out_specs=tuple([HBM] * (2 * n)),
        input_output_aliases={i: i for i in range(2 * n)},
        compiler_params=pltpu.CompilerParams(has_side_effects=EFFECT),
    )(*srcs, *lands, send_sems, recv_sems, after)
    return list(outs[n:])


def _tie(a, token):
    return a + token[0, 0].astype(a.dtype)


def _col_pieces(g, n_cols):
    R = g.shape[0]
    c = n_cols // N_DEV
    return jnp.transpose(g[:, :n_cols].reshape(R, N_DEV, c), (1, 0, 2))


def _from_col_pieces(p, pad_to=None):
    _, R, c = p.shape
    w = jnp.transpose(p, (1, 0, 2)).reshape(R, N_DEV * c)
    if pad_to is not None and pad_to > N_DEV * c:
        w = jnp.pad(w, ((0, 0), (0, pad_to - N_DEV * c)))
    return w


def _pack(arrs, rows):
    flat = jnp.concatenate([a.reshape(-1).astype(F32) for a in arrs])
    return jnp.pad(flat, (0, rows * LANES - flat.shape[0])).reshape(rows, LANES)


def _unpack(packed, shapes):
    flat = packed.reshape(-1)
    out, off = [], 0
    for s in shapes:
        n = math.prod(s)
        out.append(flat[off:off + n].reshape(s))
        off += n
    return out


def _packed_rows(shapes):
    n = sum(math.prod(s) for s in shapes)
    unit = N_DEV * 8 * LANES
    return -(-n // unit) * unit // LANES


def kernel(x, mix_pre_g, mix_post_g, mlp_pre_g, mlp_post_g, w_in_even, s5_lam_re, s5_lam_im, s5_log_dt, s5_b_re, s5_b_im, s5_c_re, s5_c_im, s5_d, s5_w_glu, fox_b_f, w_out_even, w_in_odd, pool_w, pool_scale, sgu_ln_g, sgu_ln_b, sgu_w_s, sgu_b_s, w_out_odd, mlp_w1, mlp_w2, loss_target, m_mix_pre_g, m_mix_post_g, m_mlp_pre_g, m_mlp_post_g, m_w_in_even, m_s5_lam_re, m_s5_lam_im, m_s5_log_dt, m_s5_b_re, m_s5_b_im, m_s5_c_re, m_s5_c_im, m_s5_d, m_s5_w_glu, m_fox_b_f, m_w_out_even, m_w_in_odd, m_pool_w, m_pool_scale, m_sgu_ln_g, m_sgu_ln_b, m_sgu_w_s, m_sgu_b_s, m_w_out_odd, m_mlp_w1, m_mlp_w2, v_mix_pre_g, v_mix_post_g, v_mlp_pre_g, v_mlp_post_g, v_w_in_even, v_s5_lam_re, v_s5_lam_im, v_s5_log_dt, v_s5_b_re, v_s5_b_im, v_s5_c_re, v_s5_c_im, v_s5_d, v_s5_w_glu, v_fox_b_f, v_w_out_even, v_w_in_odd, v_pool_w, v_pool_scale, v_sgu_ln_g, v_sgu_ln_b, v_sgu_w_s, v_sgu_b_s, v_w_out_odd, v_mlp_w1, v_mlp_w2):
    weights = dict(mix_pre_g=mix_pre_g, mix_post_g=mix_post_g, mlp_pre_g=mlp_pre_g, mlp_post_g=mlp_post_g, w_in_even=w_in_even, s5_lam_re=s5_lam_re, s5_lam_im=s5_lam_im, s5_log_dt=s5_log_dt, s5_b_re=s5_b_re, s5_b_im=s5_b_im, s5_c_re=s5_c_re, s5_c_im=s5_c_im, s5_d=s5_d, s5_w_glu=s5_w_glu, fox_b_f=fox_b_f, w_out_even=w_out_even, w_in_odd=w_in_odd, pool_w=pool_w, pool_scale=pool_scale, sgu_ln_g=sgu_ln_g, sgu_ln_b=sgu_ln_b, sgu_w_s=sgu_w_s, sgu_b_s=sgu_b_s, w_out_odd=w_out_odd, mlp_w1=mlp_w1, mlp_w2=mlp_w2)
    mom_m = dict(mix_pre_g=m_mix_pre_g, mix_post_g=m_mix_post_g, mlp_pre_g=m_mlp_pre_g, mlp_post_g=m_mlp_post_g, w_in_even=m_w_in_even, s5_lam_re=m_s5_lam_re, s5_lam_im=m_s5_lam_im, s5_log_dt=m_s5_log_dt, s5_b_re=m_s5_b_re, s5_b_im=m_s5_b_im, s5_c_re=m_s5_c_re, s5_c_im=m_s5_c_im, s5_d=m_s5_d, s5_w_glu=m_s5_w_glu, fox_b_f=m_fox_b_f, w_out_even=m_w_out_even, w_in_odd=m_w_in_odd, pool_w=m_pool_w, pool_scale=m_pool_scale, sgu_ln_g=m_sgu_ln_g, sgu_ln_b=m_sgu_ln_b, sgu_w_s=m_sgu_w_s, sgu_b_s=m_sgu_b_s, w_out_odd=m_w_out_odd, mlp_w1=m_mlp_w1, mlp_w2=m_mlp_w2)
    mom_v = dict(mix_pre_g=v_mix_pre_g, mix_post_g=v_mix_post_g, mlp_pre_g=v_mlp_pre_g, mlp_post_g=v_mlp_post_g, w_in_even=v_w_in_even, s5_lam_re=v_s5_lam_re, s5_lam_im=v_s5_lam_im, s5_log_dt=v_s5_log_dt, s5_b_re=v_s5_b_re, s5_b_im=v_s5_b_im, s5_c_re=v_s5_c_re, s5_c_im=v_s5_c_im, s5_d=v_s5_d, s5_w_glu=v_s5_w_glu, fox_b_f=v_fox_b_f, w_out_even=v_w_out_even, w_in_odd=v_w_in_odd, pool_w=v_pool_w, pool_scale=v_pool_scale, sgu_ln_g=v_sgu_ln_g, sgu_ln_b=v_sgu_ln_b, sgu_w_s=v_sgu_w_s, sgu_b_s=v_sgu_b_s, w_out_odd=v_w_out_odd, mlp_w1=v_mlp_w1, mlp_w2=v_mlp_w2)
    names = list(weights)
    L = x.shape[1]
    x0 = x[0]
    target = loss_target[0]
    my_index = 4 * lax.axis_index("x") + 2 * lax.axis_index("y") + lax.axis_index("c")

    small_vec = jnp.zeros((8, LANES), F32)
    small_vec = small_vec.at[0, :64].set(pool_scale[0]).at[1, :64].set(sgu_ln_g[0]).at[2, :64].set(sgu_ln_b[0])
    ag_groups, ag_token = _comm_start(
        [[w_in_even[0].astype(BF16), small_vec],
         [s5_w_glu[0].astype(BF16), w_out_even[0].astype(BF16)],
         [mlp_w1[0].astype(BF16), mlp_w2[0].astype(BF16)],
         [w_in_odd[0].astype(BF16), w_out_odd[0].astype(BF16), mlp_w1[1].astype(BF16), mlp_w2[1].astype(BF16)]],
        "ag_start", exchange=False)

    lam_r = jnp.concatenate([s5_lam_re.reshape(1, S5_NS), s5_lam_im.reshape(1, S5_NS)], axis=0)
    ldt_r = jnp.repeat(s5_log_dt.reshape(32), 64).reshape(1, S5_NS)
    lam_c = jnp.transpose(lam_r)
    ldt_c = jnp.transpose(ldt_r)
    b_t = jnp.stack([jnp.tile(s5_b_re.reshape(S5_NS, 16), (1, 8)), jnp.tile(s5_b_im.reshape(S5_NS, 16), (1, 8))])
    c_t = jnp.stack([jnp.tile(s5_c_re.reshape(S5_W, 64), (1, 8)), jnp.tile(s5_c_im.reshape(S5_W, 64), (1, 8))])
    bf_pad = jnp.pad(fox_b_f, ((0, 0), (0, LANES - 8)))
    b_st = jnp.transpose(sgu_b_s[0])

    h0, rx0 = _rms_fwd(x0, _tie(mix_pre_g[0:1], ag_token), "rms0")
    tabs, bset, cset = _s5_prep(lam_r, ldt_r, lam_c, ldt_c, b_t, c_t, "s5_prep")
    ag0 = _comm_wait(ag_groups[0], tabs, "ag_wait0", exchange=False)
    win_e = _from_col_pieces(ag0[0], EVEN_PAD)
    pool_scale_f = ag0[1][:, 0, :64].reshape(1, 512)
    ln_g_f = ag0[1][:, 1, :64].reshape(1, 512)
    ln_b_f = ag0[1][:, 2, :64].reshape(1, 512)
    z0 = _mm(h0, win_e, name="win_even", bn=EVEN_PAD)
    bu = _s5_bu(z0, bset, "s5_bu")
    xs = _s5_scan(bu, tabs, "s5_scan")
    ag1 = _comm_wait(ag_groups[1], xs, "ag_wait1", exchange=False)
    wglu = ag1[0].reshape(S5_W, S5_W)
    wout_e = ag1[1].reshape(D_MODEL, D_MODEL)
    ylin, ya = _s5_out_fwd(xs, cset, z0, s5_d, wglu, "s5_out")
    fcum = _fox_f_fwd(z0, bf_pad, "fox_f")
    f8 = fcum[:, :8]
    fq = jnp.repeat(f8, 64, axis=1)
    frow = jnp.transpose(f8).reshape(4, 2, L)
    o_att, lse = _fox_fwd(z0, fq, frow, "fox_fwd")
    mix0 = jnp.concatenate([ya, o_att.astype(BF16)], axis=1)
    y0 = _mm(mix0, wout_e, name="wout_even")
    x1, ry0, h1, rx1 = _post_pre_fwd(x0, y0, mix_post_g[0:1], mlp_pre_g[0:1], "post0")
    ag2 = _comm_wait(ag_groups[2], rx1, "ag_wait2", exchange=False)
    w1 = [ag2[0], None]
    w2 = [ag2[1].reshape(4 * D_MODEL, D_MODEL), None]
    p0, a0 = _mm(h1, w1[0], name="mlp0_w1", b3=True, out_dtypes=(BF16, BF16), epi=_epi_relu2)
    o0 = _mm(a0, w2[0], name="mlp0_w2")
    x2, ro0, h2, rx2 = _post_pre_fwd(x1, o0, mlp_post_g[0:1], mix_pre_g[1:2], "post1")
    ag3 = _comm_wait(ag_groups[3], rx2, "ag_wait3", exchange=False)
    win_o = _from_col_pieces(ag3[0])
    wout_o = ag3[1].reshape(D_MODEL, D_MODEL)
    w1[1] = ag3[2]
    w2[1] = ag3[3].reshape(4 * D_MODEL, D_MODEL)
    z1 = _mm(h2, win_o, name="win_odd", bn=ODD_IN)
    yc, pooled = _pool_fwd(z1, pool_w[0], pool_scale_f, "pool_fwd")
    yd = _sgu_fwd(z1, ln_g_f, ln_b_f, sgu_w_s[0], b_st, "sgu_fwd")
    mix1 = jnp.concatenate([yc, yd], axis=1)
    y1 = _mm(mix1, wout_o, name="wout_odd")
    x3, ry1, h3, rx3 = _post_pre_fwd(x2, y1, mix_post_g[1:2], mlp_pre_g[1:2], "post2")
    p1, a1 = _mm(h3, w1[1], name="mlp1_w1", b3=True, out_dtypes=(BF16, BF16), epi=_epi_relu2)
    o1 = _mm(a1, w2[1], name="mlp1_w2")
    gx4, ro1, sq = _post_loss_fwd(x3, o1, mlp_post_g[1:2], target, "post3")
    loss = lax.psum(0.5 * sq[0, 0] / D_MODEL, ("x", "y", "c"))

    g_o1, gg_mlp_post1 = _post_bwd(gx4, o1, ro1, mlp_post_g[1:2], "bpost3")
    g_p1 = _mm(g_o1, w2[1], name="b_mlp1_a", tb=True, out_dtypes=(BF16,), epi=_epi_relu2_bwd, extra=(p1,))
    gw2_1 = _mm(a1, g_o1, name="b_mlp1_w2", ta=True)
    g_h3 = _mm(g_p1, w1[1], name="b_mlp1_h", tb=True, b3=True)
    gw1_1 = _mm(h3, g_p1, name="b_mlp1_w1", ta=True, out3=True, bn=512)
    (ex1,), tok1 = _comm_start([[gw1_1, gw2_1.reshape(N_DEV, 512, D_MODEL)]], "ex_start1", exchange=True)
    g_x3, gg_mlp_pre1, g_y1, gg_mix_post1 = _pre_post_bwd(g_h3, x3, rx3, _tie(mlp_pre_g[1:2], tok1), gx4, y1, ry1, mix_post_g[1:2], "bpre3")
    g_mix1 = _mm(g_y1, wout_o, name="b_wout_odd_m", tb=True)
    gwout_o = _mm(mix1, g_y1, name="b_wout_odd_w", ta=True)
    g_xc, g_pool_w, g_pool_scale = _pool_bwd(g_mix1, pooled, pool_w[0], pool_scale_f, "pool_bwd")
    g_u1, g_v1, g_ws, g_bst, g_ln_g, g_ln_b = _sgu_bwd(g_mix1, z1, ln_g_f, ln_b_f, sgu_w_s[0], b_st, "sgu_bwd")
    g_z1 = jnp.concatenate([g_xc, g_u1, g_v1], axis=1).astype(BF16)
    g_h2 = _mm(g_z1, win_o, name="b_win_odd_h", tb=True, bk=ODD_IN)
    gwin_o = _mm(h2, g_z1, name="b_win_odd_w", ta=True, bn=ODD_IN)
    (ex2,), tok2 = _comm_start([[gwout_o.reshape(N_DEV, 128, D_MODEL), _col_pieces(gwin_o, ODD_IN)]], "ex_start2", exchange=True)
    g_x2, gg_mix_pre1, g_o0, gg_mlp_post0 = _pre_post_bwd(g_h2, x2, rx2, _tie(mix_pre_g[1:2], tok2), g_x3, o0, ro0, mlp_post_g[0:1], "bpre2")
    g_p0 = _mm(g_o0, w2[0], name="b_mlp0_a", tb=True, out_dtypes=(BF16,), epi=_epi_relu2_bwd, extra=(p0,))
    gw2_0 = _mm(a0, g_o0, name="b_mlp0_w2", ta=True)
    g_h1 = _mm(g_p0, w1[0], name="b_mlp0_h", tb=True, b3=True)
    gw1_0 = _mm(h1, g_p0, name="b_mlp0_w1", ta=True, out3=True, bn=512)
    (ex3,), tok3 = _comm_start([[gw1_0, gw2_0.reshape(N_DEV, 512, D_MODEL)]], "ex_start3", exchange=True)
    g_x1, gg_mlp_pre0, g_y0, gg_mix_post0 = _pre_post_bwd(g_h1, x1, rx1, _tie(mlp_pre_g[0:1], tok3), g_x2, y0, ry0, mix_post_g[0:1], "bpre1")
    g_mix0 = _mm(g_y0, wout_e, name="b_wout_even_m", tb=True)
    gwout_e = _mm(mix0, g_y0, name="b_wout_even_w", ta=True)
    gyl, gud, g_wglu, g_d = _s5_glu_bwd(g_mix0, ylin, z0, s5_d, wglu, "s5_glu_bwd")
    (ex4,), tok4 = _comm_start([[gwout_e.reshape(N_DEV, 128, D_MODEL), g_wglu.reshape(N_DEV, 64, S5_W)]], "ex_start4", exchange=True)
    gxd, gc_raw = _s5_c_bwd(gyl, xs, _tie(cset, tok4), "s5_c_bwd")
    gxs, ga = _s5_scan(gxd, tabs, "s5_scan_bwd", reverse=True, xs=xs)
    g_u0, gb_raw = _s5_bu_bwd(gxs, bset, z0, gud, "s5_bu_bwd")
    g_lam, g_ldt, g_b, g_c = _s5_param_bwd(lam_c, ldt_c, b_t, gb_raw, jnp.transpose(ga), gc_raw, "s5_param_bwd")
    dq, dfq = _fox_bwd_dq(z0, fq, frow, o_att, lse, g_mix0, "fox_dq")
    dk, dv, dfrow = _fox_bwd_dkv(z0, fq, frow, o_att, lse, g_mix0, "fox_dkv")
    dFk = jnp.pad(jnp.transpose(dfrow.reshape(8, L)), ((0, 0), (0, LANES - 8)))
    dFq = jnp.pad(dfq[:, ::64], ((0, 0), (0, LANES - 8)))
    dfl, db_f = _fox_f_bwd(dFk, dFq, z0, bf_pad, "fox_f_bwd")
    g_z0 = jnp.concatenate([g_u0, dq, dk, dv, dfl], axis=1).astype(BF16)
    gwin_e = _mm(h0, g_z0, name="b_win_even_w", ta=True, bn=EVEN_PAD)
    (ex5,), tok5 = _comm_start([[_col_pieces(gwin_e, EVEN_IN)]], "ex_start5", exchange=True)
    g_h0 = _mm(g_z0, win_e, name="b_win_even_h", tb=True, bk=EVEN_PAD)
    grad_x, gg_mix_pre0 = _pre_bwd(g_h0, x0, rx0, _tie(mix_pre_g[0:1], tok5), g_x1, "bpre0")

    r_w1_1, r_w2_1 = _comm_wait(ex1, grad_x, "ex_wait1", exchange=True)
    r_wout_o, r_win_o = _comm_wait(ex2, grad_x, "ex_wait2", exchange=True)
    r_w1_0, r_w2_0 = _comm_wait(ex3, grad_x, "ex_wait3", exchange=True)
    r_wout_e, r_wglu = _comm_wait(ex4, grad_x, "ex_wait4", exchange=True)

    small_grads = dict(
        mix_pre_g=jnp.concatenate([gg_mix_pre0, gg_mix_pre1]), mix_post_g=jnp.concatenate([gg_mix_post0, gg_mix_post1]),
        mlp_pre_g=jnp.concatenate([gg_mlp_pre0, gg_mlp_pre1]), mlp_post_g=jnp.concatenate([gg_mlp_post0, gg_mlp_post1]),
        s5_lam_re=g_lam[:, 0], s5_lam_im=g_lam[:, 1], s5_log_dt=g_ldt,
        s5_b_re=g_b[0, :, :16], s5_b_im=g_b[1, :, :16], s5_c_re=g_c[0, :, :64], s5_c_im=g_c[1, :, :64],
        s5_d=g_d, fox_b_f=db_f[:, :8], pool_w=g_pool_w, sgu_w_s=g_ws, sgu_b_s=jnp.transpose(g_bst),
        pool_scale=g_pool_scale, sgu_ln_g=g_ln_g, sgu_ln_b=g_ln_b)
    small_names = list(small_grads)
    full_shapes = [(512,) if nm in ("pool_scale", "sgu_ln_g", "sgu_ln_b") else weights[nm].shape for nm in small_names]
    rows = _packed_rows(full_shapes)
    packed = _pack([small_grads[nm] for nm in small_names], rows).reshape(N_DEV, rows // N_DEV, LANES)
    (recv_small,) = _exchange([packed], "exchange_small")
    piece = _sum_pieces(recv_small, "sum_small")
    (small_all,) = _all_gather([piece], "ag_small")
    small_full = _unpack(small_all.reshape(rows, LANES), full_shapes)
    small_g = {}
    for nm, g in zip(small_names, small_full):
        if nm in ("pool_scale", "sgu_ln_g", "sgu_ln_b"):
            g = lax.dynamic_slice(g, (my_index * 64,), (64,)).reshape(1, 64)
        small_g[nm] = g
    own_shapes = [weights[nm].shape for nm in small_names]
    rows2 = _packed_rows(own_shapes)
    pw = _pack([weights[nm] for nm in small_names], rows2)
    pg = _pack([small_g[nm] for nm in small_names], rows2)
    pm = _pack([mom_m[nm] for nm in small_names], rows2)
    pv = _pack([mom_v[nm] for nm in small_names], rows2)
    pd, pnm, pnv = _adamw(pw, pg, pm, pv, "adamw_small")
    res = {}
    for nm, d_, m_, v_ in zip(small_names, _unpack(pd, own_shapes), _unpack(pnm, own_shapes), _unpack(pnv, own_shapes)):
        res[nm] = (small_g[nm], d_, m_, v_)

    for nm, parts in (("mlp_w1", (r_w1_0, r_w1_1)), ("mlp_w2", (r_w2_0, r_w2_1))):
        per_layer = [_sum_adamw(parts[l], weights[nm][l], mom_m[nm][l], mom_v[nm][l], "adamw_%s_%d" % (nm, l))
                     for l in range(2)]
        res[nm] = tuple(jnp.stack([per_layer[0][k], per_layer[1][k]]) for k in range(4))
    big_parts = dict(s5_w_glu=r_wglu, w_out_even=r_wout_e, w_in_odd=r_win_o, w_out_odd=r_wout_o)
    for nm, parts in big_parts.items():
        outs = _sum_adamw(parts, weights[nm][0], mom_m[nm][0], mom_v[nm][0], "adamw_" + nm)
        res[nm] = tuple(o.reshape(weights[nm].shape) for o in outs)
    (r_win_e,) = _comm_wait(ex5, res["w_out_odd"][1], "ex_wait5", exchange=True)
    outs = _sum_adamw(r_win_e, w_in_even[0], m_w_in_even[0], v_w_in_even[0], "adamw_w_in_even")
    res["w_in_even"] = tuple(o.reshape(w_in_even.shape) for o in outs)

    grads = [res[nm][0].reshape(weights[nm].shape) for nm in names]
    deltas = [res[nm][1].reshape(weights[nm].shape) for nm in names]
    new_m = [res[nm][2].reshape(weights[nm].shape) for nm in names]
    new_v = [res[nm][3].reshape(weights[nm].shape) for nm in names]
    return (loss, grad_x[None], *grads, *deltas, *new_m, *new_v)
```

```python
import functools
import math

import jax
import jax.numpy as jnp
from jax import lax
from jax.experimental import pallas as pl
from jax.experimental.pallas import tpu as pltpu

F32 = jnp.float32
BF16 = jnp.bfloat16
MESH = pl.DeviceIdType.MESH
ANY = pl.BlockSpec(memory_space=pl.ANY)

N_DEV = 8
D_MODEL = 1024
EPS = 1e-6
S5_W = 512
S5_NS = 2048
SCAN_GROUPS = 4
SCAN_CHUNK = 1024
FOX_W = 512
EVEN_IN = 2056
EVEN_PAD = 2176
ODD_IN = 1536
LANES = 128
VMEM_LIMIT = 56 * 1024 * 1024

ADAM_LR = 0.001
ADAM_B1 = 0.9
ADAM_B2 = 0.999
ADAM_EPS = 1e-08
ADAM_WD = 0.01
ADAM_STEP = 10

NT = (((1,), (1,)), ((), ()))
TN = (((0,), (0,)), ((), ()))
NN = (((1,), (0,)), ((), ()))


def _cp(*sem):
    return pltpu.CompilerParams(dimension_semantics=sem, vmem_limit_bytes=VMEM_LIMIT)


def _sds(shape, dtype=F32):
    return jax.ShapeDtypeStruct(tuple(shape), dtype)


def _gelu(x):
    t = jnp.tanh(0.7978845608028654 * (x + 0.044715 * x * x * x))
    return 0.5 * x * (1.0 + t)


def _gelu_grad(x):
    t = jnp.tanh(0.7978845608028654 * (x + 0.044715 * x * x * x))
    du = 0.7978845608028654 * (1.0 + 3.0 * 0.044715 * x * x)
    return 0.5 * (1.0 + t) + 0.5 * x * (1.0 - t * t) * du


def _sigmoid(x):
    return 1.0 / (1.0 + jnp.exp(-x))


def _dot(a, b, dn=NN):
    return lax.dot_general(a, b, dn, preferred_element_type=F32)


def _mm(a, b, *, name, ta=False, tb=False, b3=False, out3=False, out_dtypes=(F32,), epi=None, extra=(),
        bm=512, bn=1024, bk=1024):
    M = a.shape[1] if ta else a.shape[0]
    K = a.shape[0] if ta else a.shape[1]
    if b3:
        if tb:
            bk = b.shape[2]
            N = b.shape[1]
            assert b.shape[0] * bk == K
        else:
            bn = b.shape[2]
            N = b.shape[0] * bn
            assert b.shape[1] == K
    else:
        N = b.shape[0] if tb else b.shape[1]
    bm, bn, bk = min(bm, M), min(bn, N), min(bk, K)
    assert M % bm == 0 and N % bn == 0 and K % bk == 0, (name, M, N, K, bm, bn, bk)
    nk = K // bk
    n_extra = len(extra)
    n_out = len(out_dtypes)
    dn = (((0 if ta else 1,), (1 if tb else 0,)), ((), ()))

    def body(*refs):
        a_ref, b_ref = refs[0], refs[1]
        e_refs = refs[2:2 + n_extra]
        o_refs = refs[2 + n_extra:2 + n_extra + n_out]
        acc_ref = refs[-1]
        k = pl.program_id(2)

        @pl.when(k == 0)
        def _():
            acc_ref[...] = jnp.zeros_like(acc_ref)

        acc_ref[...] += lax.dot_general(a_ref[...].astype(BF16), b_ref[...].astype(BF16), dn,
                                        preferred_element_type=F32)

        @pl.when(k == nk - 1)
        def _():
            acc = acc_ref[...]
            outs = (acc,) if epi is None else epi(acc, *[e[...] for e in e_refs])
            for o_ref, o in zip(o_refs, outs):
                o_ref[...] = o.astype(o_ref.dtype)

    a_spec = pl.BlockSpec((bk, bm), lambda i, j, k: (k, i)) if ta else pl.BlockSpec((bm, bk), lambda i, j, k: (i, k))
    if b3:
        if tb:
            b_spec = pl.BlockSpec((None, bn, bk), lambda i, j, k: (k, j, 0))
        else:
            b_spec = pl.BlockSpec((None, bk, bn), lambda i, j, k: (j, k, 0))
    else:
        b_spec = pl.BlockSpec((bn, bk), lambda i, j, k: (j, k)) if tb else pl.BlockSpec((bk, bn), lambda i, j, k: (k, j))
    e_specs = [pl.BlockSpec((bm, bn), lambda i, j, k: (i, j)) for _ in extra]
    if out3:
        o_specs = [pl.BlockSpec((None, bm, bn), lambda i, j, k: (j, i, 0)) for _ in out_dtypes]
        o_shapes = [_sds((N // bn, M, bn), dt) for dt in out_dtypes]
    else:
        o_specs = [pl.BlockSpec((bm, bn), lambda i, j, k: (i, j)) for _ in out_dtypes]
        o_shapes = [_sds((M, N), dt) for dt in out_dtypes]
    outs = pl.pallas_call(
        body, name=name, grid=(M // bm, N // bn, nk),
        in_specs=[a_spec, b_spec] + e_specs, out_specs=o_specs, out_shape=o_shapes,
        scratch_shapes=[pltpu.VMEM((bm, bn), F32)],
        compiler_params=_cp("parallel", "parallel", "arbitrary"),
    )(a, b, *extra)
    return outs[0] if n_out == 1 else outs


def _epi_relu2(acc):
    r = jnp.maximum(acc, 0.0)
    return acc, r * r


def _epi_relu2_bwd(acc, p):
    return (acc * (2.0 * jnp.maximum(p.astype(F32), 0.0)),)


def _row_spec(rb, w=D_MODEL):
    return pl.BlockSpec((rb, w), lambda i: (i, 0))


def _vec_spec(w=D_MODEL):
    return pl.BlockSpec((1, w), lambda i: (0, 0))


def _rstd(v):
    return lax.rsqrt(jnp.mean(v * v, axis=-1, keepdims=True) + EPS)


def _rms_fwd(x, g, name):
    L = x.shape[0]
    rb = min(256, L)

    def body(x_ref, g_ref, h_ref, r_ref):
        xv = x_ref[...]
        r = _rstd(xv)
        h_ref[...] = (xv * r * g_ref[...]).astype(BF16)
        r_ref[...] = r

    return pl.pallas_call(
        body, name=name, grid=(L // rb,),
        in_specs=[_row_spec(rb), _vec_spec()],
        out_specs=[_row_spec(rb), _row_spec(rb, 1)],
        out_shape=[_sds((L, D_MODEL), BF16), _sds((L, 1))],
        compiler_params=_cp("parallel"),
    )(x, g)


def _post_pre_fwd(x_in, y, g_post, g_pre, name):
    L = x_in.shape[0]
    rb = min(256, L)

    def body(x_ref, y_ref, gp_ref, gn_ref, xo_ref, ry_ref, h_ref, rx_ref):
        yv = y_ref[...]
        ry = _rstd(yv)
        xo = x_ref[...] + yv * ry * gp_ref[...]
        rx = _rstd(xo)
        xo_ref[...] = xo
        ry_ref[...] = ry
        h_ref[...] = (xo * rx * gn_ref[...]).astype(BF16)
        rx_ref[...] = rx

    return pl.pallas_call(
        body, name=name, grid=(L // rb,),
        in_specs=[_row_spec(rb), _row_spec(rb), _vec_spec(), _vec_spec()],
        out_specs=[_row_spec(rb), _row_spec(rb, 1), _row_spec(rb), _row_spec(rb, 1)],
        out_shape=[_sds((L, D_MODEL)), _sds((L, 1)), _sds((L, D_MODEL), BF16), _sds((L, 1))],
        compiler_params=_cp("parallel"),
    )(x_in, y, g_post, g_pre)


def _post_loss_fwd(x_in, y, g_post, target, name):
    L = x_in.shape[0]
    rb = min(256, L)

    def body(x_ref, y_ref, gp_ref, t_ref, gx_ref, ry_ref, loss_ref):
        i = pl.program_id(0)
        yv = y_ref[...]
        ry = _rstd(yv)
        diff = x_ref[...] + yv * ry * gp_ref[...] - t_ref[...]
        gx_ref[...] = diff * (1.0 / D_MODEL)
        ry_ref[...] = ry

        @pl.when(i == 0)
        def _():
            loss_ref[...] = jnp.zeros_like(loss_ref)

        loss_ref[...] += jnp.sum(diff * diff, keepdims=True)

    return pl.pallas_call(
        body, name=name, grid=(L // rb,),
        in_specs=[_row_spec(rb), _row_spec(rb), _vec_spec(), _row_spec(rb)],
        out_specs=[_row_spec(rb), _row_spec(rb, 1), pl.BlockSpec((1, 1), lambda i: (0, 0))],
        out_shape=[_sds((L, D_MODEL)), _sds((L, 1)), _sds((1, 1))],
        compiler_params=_cp("arbitrary"),
    )(x_in, y, g_post, target)


def _rms_bwd_rows(dy, xv, r, g):
    n = xv * r
    dyg = dy * g
    return r * (dyg - n * jnp.mean(dyg * n, axis=-1, keepdims=True)), n


def _post_bwd(g_out, y, ry, g_post, name):
    L = y.shape[0]
    rb = min(256, L)

    def body(go_ref, y_ref, ry_ref, gp_ref, gy_ref, gg_ref):
        i = pl.program_id(0)
        go = go_ref[...]
        gy, n = _rms_bwd_rows(go, y_ref[...], ry_ref[...], gp_ref[...])
        gy_ref[...] = gy.astype(BF16)

        @pl.when(i == 0)
        def _():
            gg_ref[...] = jnp.zeros_like(gg_ref)

        gg_ref[...] += jnp.sum(go * n, axis=0, keepdims=True)

    return pl.pallas_call(
        body, name=name, grid=(L // rb,),
        in_specs=[_row_spec(rb), _row_spec(rb), _row_spec(rb, 1), _vec_spec()],
        out_specs=[_row_spec(rb), _vec_spec()],
        out_shape=[_sds((L, D_MODEL), BF16), _sds((1, D_MODEL))],
        compiler_params=_cp("arbitrary"),
    )(g_out, y, ry, g_post)


def _pre_post_bwd(g_h, x, rx, g_pre, g_out, y_prev, ry_prev, g_post_prev, name):
    L = x.shape[0]
    rb = min(256, L)

    def body(gh_ref, x_ref, rx_ref, gn_ref, go_ref, y_ref, ry_ref, gp_ref, gi_ref, ggn_ref, gy_ref, ggp_ref):
        i = pl.program_id(0)
        gh = gh_ref[...]
        gx, n = _rms_bwd_rows(gh, x_ref[...], rx_ref[...], gn_ref[...])
        gi = go_ref[...] + gx
        gi_ref[...] = gi
        gy, ny = _rms_bwd_rows(gi, y_ref[...], ry_ref[...], gp_ref[...])
        gy_ref[...] = gy.astype(BF16)

        @pl.when(i == 0)
        def _():
            ggn_ref[...] = jnp.zeros_like(ggn_ref)
            ggp_ref[...] = jnp.zeros_like(ggp_ref)

        ggn_ref[...] += jnp.sum(gh * n, axis=0, keepdims=True)
        ggp_ref[...] += jnp.sum(gi * ny, axis=0, keepdims=True)

    return pl.pallas_call(
        body, name=name, grid=(L // rb,),
        in_specs=[_row_spec(rb), _row_spec(rb), _row_spec(rb, 1), _vec_spec(), _row_spec(rb),
                  _row_spec(rb), _row_spec(rb, 1), _vec_spec()],
        out_specs=[_row_spec(rb), _vec_spec(), _row_spec(rb), _vec_spec()],
        out_shape=[_sds((L, D_MODEL)), _sds((1, D_MODEL)), _sds((L, D_MODEL), BF16), _sds((1, D_MODEL))],
        compiler_params=_cp("arbitrary"),
    )(g_h, x, rx, g_pre, g_out, y_prev, ry_prev, g_post_prev)


def _pre_bwd(g_h, x, rx, g_pre, g_out, name):
    L = x.shape[0]
    rb = min(256, L)

    def body(gh_ref, x_ref, rx_ref, gn_ref, go_ref, gi_ref, ggn_ref):
        i = pl.program_id(0)
        gh = gh_ref[...]
        gx, n = _rms_bwd_rows(gh, x_ref[...], rx_ref[...], gn_ref[...])
        gi_ref[...] = go_ref[...] + gx

        @pl.when(i == 0)
        def _():
            ggn_ref[...] = jnp.zeros_like(ggn_ref)

        ggn_ref[...] += jnp.sum(gh * n, axis=0, keepdims=True)

    return pl.pallas_call(
        body, name=name, grid=(L // rb,),
        in_specs=[_row_spec(rb), _row_spec(rb), _row_spec(rb, 1), _vec_spec(), _row_spec(rb)],
        out_specs=[_row_spec(rb), _vec_spec()],
        out_shape=[_sds((L, D_MODEL)), _sds((1, D_MODEL))],
        compiler_params=_cp("arbitrary"),
    )(g_h, x, rx, g_pre, g_out)


def _cmul(ar, ai, br, bi):
    return ar * br - ai * bi, ar * bi + ai * br


def _zoh_cols(lr, li, ldt):
    dt = jnp.exp(ldt)
    mag = jnp.exp(lr * dt)
    ar = mag * jnp.cos(li * dt)
    ai = mag * jnp.sin(li * dt)
    den = lr * lr + li * li
    nr = ar - 1.0
    qr = (nr * lr + ai * li) / den
    qi = (ai * lr - nr * li) / den
    return dt, ar, ai, qr, qi, den


def _b_mask():
    r = lax.broadcasted_iota(jnp.int32, (S5_NS, LANES), 0)
    c = lax.broadcasted_iota(jnp.int32, (S5_NS, LANES), 1)
    return ((r >> 6) & 7) == (c >> 4)


def _c_mask():
    r = lax.broadcasted_iota(jnp.int32, (S5_W, 512), 0)
    c = lax.broadcasted_iota(jnp.int32, (S5_W, 512), 1)
    return ((r >> 4) & 7) == (c >> 6)


def _s5_prep(lam_r, ldt_r, lam_c, ldt_c, b_t, c_t, name):
    def body(lam_r_ref, ldt_r_ref, lam_c_ref, ldt_c_ref, b_ref, c_ref, tab_ref, bset_ref, cset_ref):
        lr, li = lam_r_ref[0:1, :], lam_r_ref[1:2, :]
        dt = jnp.exp(ldt_r_ref[...])
        mag = jnp.exp(lr * dt)
        p1r, p1i = mag * jnp.cos(li * dt), mag * jnp.sin(li * dt)
        p2r, p2i = _cmul(p1r, p1i, p1r, p1i)
        p3r, p3i = _cmul(p2r, p2i, p1r, p1i)
        p4r, p4i = _cmul(p2r, p2i, p2r, p2i)
        p5r, p5i = _cmul(p4r, p4i, p1r, p1i)
        p6r, p6i = _cmul(p4r, p4i, p2r, p2i)
        p7r, p7i = _cmul(p4r, p4i, p3r, p3i)
        p8r, p8i = _cmul(p4r, p4i, p4r, p4i)
        pw_r = [p1r, p2r, p3r, p4r, p5r, p6r, p7r, p8r]
        pw_i = [p1i, p2i, p3i, p4i, p5i, p6i, p7i, p8i]
        row = lax.broadcasted_iota(jnp.int32, (8, S5_NS), 0)
        zero = jnp.zeros((8, S5_NS), F32)

        def bc(v):
            return jnp.broadcast_to(v, (8, S5_NS))

        for d in range(2):
            sgn = 1.0 if d == 0 else -1.0
            for t, s in enumerate((1, 2, 4)):
                live = (row >= s) if d == 0 else (row <= 7 - s)
                tab_ref[d, 2 * t] = jnp.where(live, bc(pw_r[s - 1]), zero)
                tab_ref[d, 2 * t + 1] = jnp.where(live, bc(sgn * pw_i[s - 1]), zero)
            cr, ci = zero, zero
            for i in range(8):
                e = i if d == 0 else 7 - i
                cr = jnp.where(row == i, bc(pw_r[e]), cr)
                ci = jnp.where(row == i, bc(sgn * pw_i[e]), ci)
            tab_ref[d, 6] = cr
            tab_ref[d, 7] = ci

        _, _, _, qr, qi, _ = _zoh_cols(lam_c_ref[:, 0:1], lam_c_ref[:, 1:2], ldt_c_ref[...])
        bm = _b_mask()
        br, bi = b_ref[0], b_ref[1]
        bset_ref[0] = jnp.where(bm, qr * br - qi * bi, 0.0).astype(BF16)
        bset_ref[1] = jnp.where(bm, qr * bi + qi * br, 0.0).astype(BF16)
        cm = _c_mask()
        cset_ref[0] = jnp.where(cm, c_ref[0], 0.0).astype(BF16)
        cset_ref[1] = jnp.where(cm, c_ref[1], 0.0).astype(BF16)

    vm = pl.BlockSpec(memory_space=pltpu.VMEM)
    return pl.pallas_call(
        body, name=name, in_specs=[vm] * 6, out_specs=[vm] * 3,
        out_shape=[_sds((2, 8, 8, S5_NS)), _sds((2, S5_NS, LANES), BF16), _sds((2, S5_W, 512), BF16)],
        compiler_params=pltpu.CompilerParams(vmem_limit_bytes=VMEM_LIMIT),
    )(lam_r, ldt_r, lam_c, ldt_c, b_t, c_t)


def _s5_bu(z, bset, name):
    L = z.shape[0]
    bl = min(512, L)

    def body(u_ref, b_ref, o_ref):
        u = u_ref[...].astype(BF16)
        o_ref[0] = _dot(u, b_ref[0], NT)
        o_ref[1] = _dot(u, b_ref[1], NT)

    return pl.pallas_call(
        body, name=name, grid=(L // bl, 4),
        in_specs=[pl.BlockSpec((bl, LANES), lambda i, j: (i, j)),
                  pl.BlockSpec((2, 512, LANES), lambda i, j: (0, j, 0))],
        out_specs=pl.BlockSpec((2, bl, 512), lambda i, j: (0, i, j)),
        out_shape=_sds((2, L, S5_NS)),
        compiler_params=_cp("parallel", "parallel"),
    )(z, bset)


def _s5_scan(bu, tabs, name, reverse=False, xs=None):
    L = bu.shape[1]
    tl = min(SCAN_CHUNK, L)
    nc = L // tl
    nb = tl // 8
    W = SCAN_GROUPS * LANES
    d = 1 if reverse else 0
    with_ga = xs is not None
    assert not with_ga or reverse

    def body(*refs):
        if with_ga:
            bu_ref, tab_ref, xs_ref, x_ref, ga_ref, carry_ref, acc_ref = refs
        else:
            bu_ref, tab_ref, x_ref, carry_ref = refs
        c = pl.program_id(1)
        row = lax.broadcasted_iota(jnp.int32, (8, LANES), 0)

        @pl.when(c == 0)
        def _():
            carry_ref[...] = jnp.zeros_like(carry_ref)
            if with_ga:
                acc_ref[...] = jnp.zeros_like(acc_ref)

        def step(i, carry):
            b = (nb - 1 - i) if reverse else i
            off = pl.multiple_of(b * 8, 8)
            out = []
            for g in range(SCAN_GROUPS):
                lanes = pl.ds(g * LANES, LANES)
                cr, ci = carry[2 * g], carry[2 * g + 1]
                yr = bu_ref[0, pl.ds(off, 8), lanes]
                yi = bu_ref[1, pl.ds(off, 8), lanes]
                for t, s in enumerate((1, 2, 4)):
                    sh = (8 - s) if reverse else s
                    sr = pltpu.roll(yr, sh, 0)
                    si = pltpu.roll(yi, sh, 0)
                    mr, mi = tab_ref[2 * t, :, lanes], tab_ref[2 * t + 1, :, lanes]
                    yr, yi = yr + mr * sr - mi * si, yi + mr * si + mi * sr
                pr, pi = tab_ref[6, :, lanes], tab_ref[7, :, lanes]
                yr, yi = yr + pr * cr - pi * ci, yi + pr * ci + pi * cr
                x_ref[0, pl.ds(off, 8), lanes] = yr
                x_ref[1, pl.ds(off, 8), lanes] = yi
                if with_ga:
                    nr = jnp.where(row == 7, cr, pltpu.roll(yr, 7, 0))
                    ni = jnp.where(row == 7, ci, pltpu.roll(yi, 7, 0))
                    xr = xs_ref[0, pl.ds(off, 8), lanes]
                    xi = xs_ref[1, pl.ds(off, 8), lanes]
                    acc_ref[0, :, lanes] += xr * nr + xi * ni
                    acc_ref[1, :, lanes] += xr * ni - xi * nr
                last = 0 if reverse else 7
                out += [jnp.broadcast_to(yr[last:last + 1, :], (8, LANES)),
                        jnp.broadcast_to(yi[last:last + 1, :], (8, LANES))]
            return tuple(out)

        init = []
        for g in range(SCAN_GROUPS):
            init += [carry_ref[0, :, pl.ds(g * LANES, LANES)], carry_ref[1, :, pl.ds(g * LANES, LANES)]]
        fin = lax.fori_loop(0, nb, step, tuple(init))
        for g in range(SCAN_GROUPS):
            carry_ref[0, :, pl.ds(g * LANES, LANES)] = fin[2 * g]
            carry_ref[1, :, pl.ds(g * LANES, LANES)] = fin[2 * g + 1]
        if with_ga:
            @pl.when(c == nc - 1)
            def _():
                ga_ref[0:1, :] = jnp.sum(acc_ref[0], axis=0, keepdims=True)
                ga_ref[1:2, :] = jnp.sum(acc_ref[1], axis=0, keepdims=True)

    chunk = (lambda g, c: (0, nc - 1 - c, g)) if reverse else (lambda g, c: (0, c, g))
    seq = pl.BlockSpec((2, tl, W), chunk)
    in_specs = [seq, pl.BlockSpec((None, 8, 8, W), lambda g, c: (d, 0, 0, g))]
    out_specs = [seq]
    out_shape = [_sds((2, L, S5_NS))]
    scratch = [pltpu.VMEM((2, 8, W), F32)]
    args = [bu, tabs]
    if with_ga:
        in_specs.append(seq)
        args.append(xs)
        out_specs.append(pl.BlockSpec((2, W), lambda g, c: (0, g)))
        out_shape.append(_sds((2, S5_NS)))
        scratch.append(pltpu.VMEM((2, 8, W), F32))
    outs = pl.pallas_call(
        body, name=name, grid=(S5_NS // W, nc), in_specs=in_specs, out_specs=out_specs, out_shape=out_shape,
        scratch_shapes=scratch, compiler_params=_cp("parallel", "arbitrary"),
    )(*args)
    return outs if with_ga else outs[0]


def _s5_out_fwd(xs, cset, z, dvec, wglu, name):
    L = z.shape[0]
    bl = min(256, L)

    def body(x_ref, c_ref, u_ref, d_ref, w_ref, ylin_ref, ya_ref):
        cols = []
        for j in range(4):
            xr = x_ref[0, :, 512 * j:512 * (j + 1)].astype(BF16)
            xi = x_ref[1, :, 512 * j:512 * (j + 1)].astype(BF16)
            cr = c_ref[0, LANES * j:LANES * (j + 1), :]
            ci = c_ref[1, LANES * j:LANES * (j + 1), :]
            cols.append(_dot(xr, cr, NT) - _dot(xi, ci, NT))
        ylin = jnp.concatenate(cols, axis=1) + d_ref[...] * u_ref[...]
        yg = _gelu(ylin)
        t = _dot(yg.astype(BF16), w_ref[...])
        ylin_ref[...] = ylin
        ya_ref[...] = (yg * _sigmoid(t)).astype(BF16)

    return pl.pallas_call(
        body, name=name, grid=(L // bl,),
        in_specs=[pl.BlockSpec((2, bl, S5_NS), lambda i: (0, i, 0)),
                  pl.BlockSpec((2, S5_W, 512), lambda i: (0, 0, 0)),
                  pl.BlockSpec((bl, S5_W), lambda i: (i, 0)),
                  pl.BlockSpec((1, S5_W), lambda i: (0, 0)),
                  pl.BlockSpec((S5_W, S5_W), lambda i: (0, 0))],
        out_specs=[pl.BlockSpec((bl, S5_W), lambda i: (i, 0))] * 2,
        out_shape=[_sds((L, S5_W)), _sds((L, S5_W), BF16)],
        compiler_params=_cp("parallel"),
    )(xs, cset, z, dvec, wglu)


def _s5_glu_bwd(g_m, ylin, z, dvec, wglu, name):
    L = z.shape[0]
    bl = min(256, L)

    def body(g_ref, ylin_ref, u_ref, d_ref, w_ref, gyl_ref, gud_ref, gw_ref, gd_ref):
        i = pl.program_id(0)
        ylin = ylin_ref[...]
        yg = _gelu(ylin)
        ygb = yg.astype(BF16)
        sg = _sigmoid(_dot(ygb, w_ref[...]))
        gya = g_ref[...]
        gt = gya * yg * sg * (1.0 - sg)
        gtb = gt.astype(BF16)
        gyg = gya * sg + _dot(gtb, w_ref[...], NT)
        gyl = gyg * _gelu_grad(ylin)
        gyl_ref[...] = gyl
        gud_ref[...] = gyl * d_ref[...]

        @pl.when(i == 0)
        def _():
            gw_ref[...] = jnp.zeros_like(gw_ref)
            gd_ref[...] = jnp.zeros_like(gd_ref)

        gw_ref[...] += _dot(ygb, gtb, TN)
        gd_ref[...] += jnp.sum(gyl * u_ref[...], axis=0, keepdims=True)

    blk = pl.BlockSpec((bl, S5_W), lambda i: (i, 0))
    return pl.pallas_call(
        body, name=name, grid=(L // bl,),
        in_specs=[blk, blk, blk, pl.BlockSpec((1, S5_W), lambda i: (0, 0)),
                  pl.BlockSpec((S5_W, S5_W), lambda i: (0, 0))],
        out_specs=[blk, blk, pl.BlockSpec((S5_W, S5_W), lambda i: (0, 0)), pl.BlockSpec((1, S5_W), lambda i: (0, 0))],
        out_shape=[_sds((L, S5_W)), _sds((L, S5_W)), _sds((S5_W, S5_W)), _sds((1, S5_W))],
        compiler_params=_cp("arbitrary"),
    )(g_m, ylin, z, dvec, wglu)


def _s5_c_bwd(gyl, xs, cset, name):
    L = gyl.shape[0]
    bl = min(256, L)

    def body(g_ref, x_ref, c_ref, gx_ref, gc_ref):
        i = pl.program_id(0)

        @pl.when(i == 0)
        def _():
            gc_ref[...] = jnp.zeros_like(gc_ref)

        for j in range(4):
            gj = g_ref[:, LANES * j:LANES * (j + 1)].astype(BF16)
            cr = c_ref[0, LANES * j:LANES * (j + 1), :]
            ci = c_ref[1, LANES * j:LANES * (j + 1), :]
            gx_ref[0, :, 512 * j:512 * (j + 1)] = _dot(gj, cr)
            gx_ref[1, :, 512 * j:512 * (j + 1)] = -_dot(gj, ci)
            xr = x_ref[0, :, 512 * j:512 * (j + 1)].astype(BF16)
            xi = x_ref[1, :, 512 * j:512 * (j + 1)].astype(BF16)
            gc_ref[0, LANES * j:LANES * (j + 1), :] += _dot(gj, xr, TN)
            gc_ref[1, LANES * j:LANES * (j + 1), :] -= _dot(gj, xi, TN)

    return pl.pallas_call(
        body, name=name, grid=(L // bl,),
        in_specs=[pl.BlockSpec((bl, S5_W), lambda i: (i, 0)),
                  pl.BlockSpec((2, bl, S5_NS), lambda i: (0, i, 0)),
                  pl.BlockSpec((2, S5_W, 512), lambda i: (0, 0, 0))],
        out_specs=[pl.BlockSpec((2, bl, S5_NS), lambda i: (0, i, 0)),
                   pl.BlockSpec((2, S5_W, 512), lambda i: (0, 0, 0))],
        out_shape=[_sds((2, L, S5_NS)), _sds((2, S5_W, 512))],
        compiler_params=_cp("arbitrary"),
    )(gyl, xs, cset)


def _s5_bu_bwd(gx, bset, z, gud, name):
    L = z.shape[0]
    bl = min(512, L)

    def body(gx_ref, b_ref, u_ref, gud_ref, gu_ref, gb_ref):
        i = pl.program_id(1)
        gr = gx_ref[0].astype(BF16)
        gi = gx_ref[1].astype(BF16)
        gu_ref[...] = gud_ref[...] + _dot(gr, b_ref[0]) + _dot(gi, b_ref[1])

        @pl.when(i == 0)
        def _():
            gb_ref[...] = jnp.zeros_like(gb_ref)

        u = u_ref[...].astype(BF16)
        gb_ref[0] += _dot(gr, u, TN)
        gb_ref[1] += _dot(gi, u, TN)

    return pl.pallas_call(
        body, name=name, grid=(4, L // bl),
        in_specs=[pl.BlockSpec((2, bl, 512), lambda j, i: (0, i, j)),
                  pl.BlockSpec((2, 512, LANES), lambda j, i: (0, j, 0)),
                  pl.BlockSpec((bl, LANES), lambda j, i: (i, j)),
                  pl.BlockSpec((bl, LANES), lambda j, i: (i, j))],
        out_specs=[pl.BlockSpec((bl, LANES), lambda j, i: (i, j)),
                   pl.BlockSpec((2, 512, LANES), lambda j, i: (0, j, 0))],
        out_shape=[_sds((L, S5_W)), _sds((2, S5_NS, LANES))],
        compiler_params=_cp("parallel", "arbitrary"),
    )(gx, bset, z, gud)


def _s5_param_bwd(lam_c, ldt_c, b_t, gb, ga_c, gc, name):
    def body(lam_ref, ldt_ref, b_ref, gb_ref, ga_ref, gc_ref, glam_ref, gldt_ref, gbo_ref, gco_ref):
        lr, li = lam_ref[:, 0:1], lam_ref[:, 1:2]
        dt, ar, ai, qr, qi, den = _zoh_cols(lr, li, ldt_ref[...])
        bm = _b_mask()
        gbr = jnp.where(bm, gb_ref[0], 0.0)
        gbi = jnp.where(bm, gb_ref[1], 0.0)
        br, bi = b_ref[0], b_ref[1]
        obr = gbr * qr + gbi * qi
        obi = gbi * qr - gbr * qi
        gqr = jnp.sum(gbr * br + gbi * bi, axis=1, keepdims=True)
        gqi = jnp.sum(gbi * br - gbr * bi, axis=1, keepdims=True)
        for s in (64, 32, 16):
            obr = obr + pltpu.roll(obr, s, 1)
            obi = obi + pltpu.roll(obi, s, 1)
        gbo_ref[0] = obr
        gbo_ref[1] = obi
        gar = ga_ref[:, 0:1] + (gqr * lr - gqi * li) / den
        gai = ga_ref[:, 1:2] + (gqr * li + gqi * lr) / den
        qlr = (qr * lr + qi * li) / den
        qli = (qi * lr - qr * li) / den
        glr = -(gqr * qlr + gqi * qli)
        gli = -(gqi * qlr - gqr * qli)
        glr = glr + dt * (gar * ar + gai * ai)
        gli = gli + dt * (gai * ar - gar * ai)
        wr, wi = _cmul(lr, li, ar, ai)
        gldt = (gar * wr + gai * wi) * dt
        glam_ref[:, 0:1] = glr
        glam_ref[:, 1:2] = gli
        r = lax.broadcasted_iota(jnp.int32, (S5_NS, 32), 0)
        c = lax.broadcasted_iota(jnp.int32, (S5_NS, 32), 1)
        gldt_ref[...] = jnp.sum(jnp.where((r >> 6) == c, gldt, 0.0), axis=0, keepdims=True)
        cm = _c_mask()
        for k in range(2):
            oc = jnp.where(cm, gc_ref[k], 0.0)
            for s in (256, 128, 64):
                oc = oc + pltpu.roll(oc, s, 1)
            gco_ref[k] = oc[:, 0:LANES]

    vm = pl.BlockSpec(memory_space=pltpu.VMEM)
    return pl.pallas_call(
        body, name=name, in_specs=[vm] * 6, out_specs=[vm] * 4,
        out_shape=[_sds((S5_NS, 2)), _sds((1, 32)), _sds((2, S5_NS, LANES)), _sds((2, S5_W, LANES))],
        compiler_params=pltpu.CompilerParams(vmem_limit_bytes=VMEM_LIMIT),
    )(lam_c, ldt_c, b_t, gb, ga_c, gc)


FL_BLK = EVEN_PAD // LANES - 1
Q_BLK, K_BLK, V_BLK = 4, 8, 12
NEG = -1e30


def _log_sigmoid(v):
    return jnp.minimum(v, 0.0) - jnp.log(1.0 + jnp.exp(-jnp.abs(v)))


def _fox_f_fwd(z, bf, name):
    L = z.shape[0]
    tl = min(256, L)

    def body(fl_ref, b_ref, f_ref, carry_ref):
        i = pl.program_id(0)

        @pl.when(i == 0)
        def _():
            carry_ref[...] = jnp.zeros_like(carry_ref)

        lf = _log_sigmoid(fl_ref[...] + b_ref[...])
        r = lax.broadcasted_iota(jnp.int32, (tl, tl), 0)
        c = lax.broadcasted_iota(jnp.int32, (tl, tl), 1)
        tri = (r >= c).astype(F32)
        cs = lax.dot_general(tri, lf, NN, precision=lax.Precision.HIGHEST, preferred_element_type=F32) + carry_ref[...]
        f_ref[...] = cs
        carry_ref[...] = cs[tl - 1:tl, :]

    return pl.pallas_call(
        body, name=name, grid=(L // tl,),
        in_specs=[pl.BlockSpec((tl, LANES), lambda i: (i, FL_BLK)), pl.BlockSpec((1, LANES), lambda i: (0, 0))],
        out_specs=pl.BlockSpec((tl, LANES), lambda i: (i, 0)),
        out_shape=_sds((L, LANES)),
        scratch_shapes=[pltpu.VMEM((1, LANES), F32)],
        compiler_params=_cp("arbitrary"),
    )(z, bf)


def _fox_f_bwd(dFk, dFq, z, bf, name):
    L = z.shape[0]
    tl = min(256, L)
    nb = L // tl

    def body(dfk_ref, dfq_ref, fl_ref, b_ref, dfl_ref, db_ref, carry_ref):
        i = pl.program_id(0)

        @pl.when(i == 0)
        def _():
            carry_ref[...] = jnp.zeros_like(carry_ref)
            db_ref[...] = jnp.zeros_like(db_ref)

        r = lax.broadcasted_iota(jnp.int32, (tl, tl), 0)
        c = lax.broadcasted_iota(jnp.int32, (tl, tl), 1)
        tri = (r <= c).astype(F32)
        cs = lax.dot_general(tri, dfk_ref[...] + dfq_ref[...], NN, precision=lax.Precision.HIGHEST,
                             preferred_element_type=F32) + carry_ref[...]
        carry_ref[...] = cs[0:1, :]
        dfl = cs * _sigmoid(-(fl_ref[...] + b_ref[...]))
        dfl_ref[...] = dfl
        db_ref[...] += jnp.sum(dfl, axis=0, keepdims=True)

    return pl.pallas_call(
        body, name=name, grid=(nb,),
        in_specs=[pl.BlockSpec((tl, LANES), lambda i: (nb - 1 - i, 0)),
                  pl.BlockSpec((tl, LANES), lambda i: (nb - 1 - i, 0)),
                  pl.BlockSpec((tl, LANES), lambda i: (nb - 1 - i, FL_BLK)),
                  pl.BlockSpec((1, LANES), lambda i: (0, 0))],
        out_specs=[pl.BlockSpec((tl, LANES), lambda i: (nb - 1 - i, 0)), pl.BlockSpec((1, LANES), lambda i: (0, 0))],
        out_shape=[_sds((L, LANES)), _sds((1, LANES))],
        scratch_shapes=[pltpu.VMEM((1, LANES), F32)],
        compiler_params=_cp("arbitrary"),
    )(dFk, dFq, z, bf)


def _head_mask(hh):
    lane = lax.broadcasted_iota(jnp.int32, (1, LANES), 1)
    return (lane >> 6) == hh


def _fox_scores(q_scaled, k, fq_ref, fr_ref, hh, causal):
    qh = jnp.where(_head_mask(hh), q_scaled, 0.0).astype(BF16)
    s = _dot(qh, k, NT) + (fq_ref[:, 64 * hh:64 * hh + 1] - fr_ref[hh:hh + 1, :])
    return s if causal is None else jnp.where(causal, s, NEG)


def _causal(qi, ki, T):
    r = qi * T + lax.broadcasted_iota(jnp.int32, (T, T), 0)
    c = ki * T + lax.broadcasted_iota(jnp.int32, (T, T), 1)
    return c <= r


def _fox_fwd(z, fq, frow, name):
    L = z.shape[0]
    T = min(512, L)
    nq = L // T

    def body(q_ref, k_ref, v_ref, fq_ref, fr_ref, o_ref, lse_ref, m_ref, l_ref, acc_ref):
        qi, ki = pl.program_id(1), pl.program_id(2)

        @pl.when(ki == 0)
        def _():
            m_ref[...] = jnp.full_like(m_ref, NEG)
            l_ref[...] = jnp.zeros_like(l_ref)
            acc_ref[...] = jnp.zeros_like(acc_ref)

        def step(diagonal):
            q = q_ref[...] * 0.125
            k = k_ref[...].astype(BF16)
            v = v_ref[...].astype(BF16)
            causal = _causal(qi, ki, T) if diagonal else None
            alphas, pvs = [], []
            for hh in range(2):
                s = _fox_scores(q, k, fq_ref, fr_ref, hh, causal)
                m_old = m_ref[hh]
                m_new = jnp.maximum(m_old, jnp.max(s, axis=1, keepdims=True))
                alpha = jnp.exp(m_old - m_new)
                p = jnp.exp(s - m_new)
                l_ref[hh] = alpha * l_ref[hh] + jnp.sum(p, axis=1, keepdims=True)
                m_ref[hh] = m_new
                alphas.append(alpha)
                pvs.append(_dot(p.astype(BF16), v))
            h0 = _head_mask(0)
            acc_ref[...] = jnp.where(h0, alphas[0], alphas[1]) * acc_ref[...] + jnp.where(h0, pvs[0], pvs[1])

        @pl.when(ki < qi)
        def _():
            step(False)

        @pl.when(ki == qi)
        def _():
            step(True)

        @pl.when(ki == nq - 1)
        def _():
            h0 = _head_mask(0)
            inv = jnp.where(h0, 1.0 / l_ref[0], 1.0 / l_ref[1])
            o_ref[...] = acc_ref[...] * inv
            lse_ref[...] = jnp.where(h0, m_ref[0] + jnp.log(l_ref[0]), m_ref[1] + jnp.log(l_ref[1]))

    def zspec(base, rowf):
        return pl.BlockSpec((T, LANES), lambda j, qi, ki: (rowf(qi, ki), base + j))

    kv_row = lambda qi, ki: jnp.minimum(ki, qi)
    q_row = lambda qi, ki: qi
    return pl.pallas_call(
        body, name=name, grid=(4, nq, nq),
        in_specs=[zspec(Q_BLK, q_row), zspec(K_BLK, kv_row), zspec(V_BLK, kv_row),
                  pl.BlockSpec((T, LANES), lambda j, qi, ki: (qi, j)),
                  pl.BlockSpec((None, 2, T), lambda j, qi, ki: (j, 0, jnp.minimum(ki, qi)))],
        out_specs=[pl.BlockSpec((T, LANES), lambda j, qi, ki: (qi, j))] * 2,
        out_shape=[_sds((L, FOX_W)), _sds((L, FOX_W))],
        scratch_shapes=[pltpu.VMEM((2, T, 1), F32), pltpu.VMEM((2, T, 1), F32), pltpu.VMEM((T, LANES), F32)],
        compiler_params=_cp("parallel", "parallel", "arbitrary"),
    )(z, z, z, fq, frow)


def _fox_delta(dob, o, hh):
    return jnp.sum(jnp.where(_head_mask(hh), dob.astype(F32) * o, 0.0), axis=1, keepdims=True)


def _fox_bwd_dq(z, fq, frow, o, lse, g_m, name):
    L = z.shape[0]
    T = min(512, L)
    nq = L // T

    def body(q_ref, k_ref, v_ref, fq_ref, fr_ref, o_ref, lse_ref, do_ref, dq_ref, df_ref, acc_ref, rs_ref):
        qi, ki = pl.program_id(1), pl.program_id(2)

        @pl.when(ki == 0)
        def _():
            acc_ref[...] = jnp.zeros_like(acc_ref)
            rs_ref[...] = jnp.zeros_like(rs_ref)

        def step(diagonal):
            q = q_ref[...] * 0.125
            k = k_ref[...].astype(BF16)
            v = v_ref[...].astype(BF16)
            dob = do_ref[...].astype(BF16)
            causal = _causal(qi, ki, T) if diagonal else None
            dqs = []
            for hh in range(2):
                s = _fox_scores(q, k, fq_ref, fr_ref, hh, causal)
                p = jnp.exp(s - lse_ref[:, 64 * hh:64 * hh + 1])
                dp = _dot(jnp.where(_head_mask(hh), dob, 0.0), v, NT)
                ds = p * (dp - _fox_delta(dob, o_ref[...], hh))
                rs_ref[hh] += jnp.sum(ds, axis=1, keepdims=True)
                dqs.append(_dot(ds.astype(BF16), k))
            acc_ref[...] += jnp.where(_head_mask(0), dqs[0], dqs[1])

        @pl.when(ki < qi)
        def _():
            step(False)

        @pl.when(ki == qi)
        def _():
            step(True)

        @pl.when(ki == nq - 1)
        def _():
            dq_ref[...] = acc_ref[...] * 0.125
            df_ref[...] = jnp.where(_head_mask(0), rs_ref[0], rs_ref[1])

    def zspec(base, rowf):
        return pl.BlockSpec((T, LANES), lambda j, qi, ki: (rowf(qi, ki), base + j))

    kv_row = lambda qi, ki: jnp.minimum(ki, qi)
    q_row = lambda qi, ki: qi
    qblk = pl.BlockSpec((T, LANES), lambda j, qi, ki: (qi, j))
    return pl.pallas_call(
        body, name=name, grid=(4, nq, nq),
        in_specs=[zspec(Q_BLK, q_row), zspec(K_BLK, kv_row), zspec(V_BLK, kv_row), qblk,
                  pl.BlockSpec((None, 2, T), lambda j, qi, ki: (j, 0, jnp.minimum(ki, qi))),
                  qblk, qblk, pl.BlockSpec((T, LANES), lambda j, qi, ki: (qi, 4 + j))],
        out_specs=[qblk, qblk],
        out_shape=[_sds((L, FOX_W)), _sds((L, FOX_W))],
        scratch_shapes=[pltpu.VMEM((T, LANES), F32), pltpu.VMEM((2, T, 1), F32)],
        compiler_params=_cp("parallel", "parallel", "arbitrary"),
    )(z, z, z, fq, frow, o, lse, g_m)


def _fox_bwd_dkv(z, fq, frow, o, lse, g_m, name):
    L = z.shape[0]
    T = min(512, L)
    nq = L // T

    def body(q_ref, k_ref, v_ref, fq_ref, fr_ref, o_ref, lse_ref, do_ref, dk_ref, dv_ref, df_ref,
             dk_acc, dv_acc, df_acc):
        ki, qi = pl.program_id(1), pl.program_id(2)

        @pl.when(qi == 0)
        def _():
            dk_acc[...] = jnp.zeros_like(dk_acc)
            dv_acc[...] = jnp.zeros_like(dv_acc)
            df_acc[...] = jnp.zeros_like(df_acc)

        def step(diagonal):
            q = q_ref[...] * 0.125
            qb = q.astype(BF16)
            k = k_ref[...].astype(BF16)
            v = v_ref[...].astype(BF16)
            dob = do_ref[...].astype(BF16)
            causal = _causal(qi, ki, T) if diagonal else None
            dvs, dks = [], []
            for hh in range(2):
                s = _fox_scores(q, k, fq_ref, fr_ref, hh, causal)
                p = jnp.exp(s - lse_ref[:, 64 * hh:64 * hh + 1])
                dp = _dot(jnp.where(_head_mask(hh), dob, 0.0), v, NT)
                ds = p * (dp - _fox_delta(dob, o_ref[...], hh))
                dvs.append(_dot(p.astype(BF16), dob, TN))
                dks.append(_dot(ds.astype(BF16), qb, TN))
                df_acc[hh:hh + 1, :] -= jnp.sum(ds, axis=0, keepdims=True)
            h0 = _head_mask(0)
            dv_acc[...] += jnp.where(h0, dvs[0], dvs[1])
            dk_acc[...] += jnp.where(h0, dks[0], dks[1])

        @pl.when(qi > ki)
        def _():
            step(False)

        @pl.when(qi == ki)
        def _():
            step(True)

        @pl.when(qi == nq - 1)
        def _():
            dk_ref[...] = dk_acc[...]
            dv_ref[...] = dv_acc[...]
            df_ref[...] = df_acc[...]

    q_row = lambda ki, qi: jnp.maximum(qi, ki)

    def qside(base):
        return pl.BlockSpec((T, LANES), lambda j, ki, qi: (q_row(ki, qi), base + j))

    def kside(base):
        return pl.BlockSpec((T, LANES), lambda j, ki, qi: (ki, base + j))

    kblk = pl.BlockSpec((T, LANES), lambda j, ki, qi: (ki, j))
    frow_spec = pl.BlockSpec((None, 2, T), lambda j, ki, qi: (j, 0, ki))
    return pl.pallas_call(
        body, name=name, grid=(4, nq, nq),
        in_specs=[qside(Q_BLK), kside(K_BLK), kside(V_BLK), qside(0), frow_spec, qside(0), qside(0), qside(4)],
        out_specs=[kblk, kblk, frow_spec],
        out_shape=[_sds((L, FOX_W)), _sds((L, FOX_W)), _sds((4, 2, L))],
        scratch_shapes=[pltpu.VMEM((T, LANES), F32), pltpu.VMEM((T, LANES), F32), pltpu.VMEM((2, T), F32)],
        compiler_params=_cp("parallel", "parallel", "arbitrary"),
    )(z, z, z, fq, frow, o, lse, g_m)


def _shift_rows(v, s, down, row):
    n = v.shape[0]
    if down:
        return jnp.where(row >= s, pltpu.roll(v, s, 0), 0.0)
    return jnp.where(row < n - s, pltpu.roll(v, n - s, 0), 0.0)


def _window_sum(v, g, down, row):
    out = jnp.zeros_like(v)
    s = v
    for k in range(4):
        s = s + _shift_rows(s, 1 << k, down, row)
        out = jnp.where(g == k, s, out)
    return out


def _pool_inv_cnt(g, row):
    w = jnp.left_shift(2, g).astype(F32)
    return 1.0 / jnp.minimum(row.astype(F32) + 1.0, w)


def _pool_fwd(z, pool_w, scale, name):
    L = z.shape[0]

    def body(x_ref, w_ref, s_ref, y_ref, p_ref):
        g = pl.program_id(0)
        row = lax.broadcasted_iota(jnp.int32, (L, LANES), 0)
        x = x_ref[...]
        pooled = (_window_sum(x, g, True, row) * _pool_inv_cnt(g, row) - x).astype(BF16)
        p_ref[...] = pooled
        y_ref[...] = (_dot(pooled, w_ref[...].astype(BF16)) * s_ref[...]).astype(BF16)

    col = pl.BlockSpec((L, LANES), lambda g: (0, g))
    return pl.pallas_call(
        body, name=name, grid=(4,),
        in_specs=[col, pl.BlockSpec((None, LANES, LANES), lambda g: (g, 0, 0)), pl.BlockSpec((1, LANES), lambda g: (0, g))],
        out_specs=[col, col],
        out_shape=[_sds((L, 512), BF16), _sds((L, 512), BF16)],
        compiler_params=_cp("parallel"),
    )(z, pool_w, scale)


def _pool_bwd(g_m, pooled, pool_w, scale, name):
    L = g_m.shape[0]

    def body(g_ref, p_ref, w_ref, s_ref, gx_ref, gw_ref, gs_ref):
        g = pl.program_id(0)
        row = lax.broadcasted_iota(jnp.int32, (L, LANES), 0)
        gy = g_ref[...]
        pooled = p_ref[...]
        wb = w_ref[...].astype(BF16)
        lin = _dot(pooled, wb)
        gs_ref[...] = jnp.sum(gy * lin, axis=0, keepdims=True)
        glin = (gy * s_ref[...]).astype(BF16)
        gw_ref[...] = _dot(pooled, glin, TN)
        gp = _dot(glin, wb, NT)
        gx_ref[...] = _window_sum(gp * _pool_inv_cnt(g, row), g, False, row) - gp

    col = pl.BlockSpec((L, LANES), lambda g: (0, g))
    wspec = pl.BlockSpec((None, LANES, LANES), lambda g: (g, 0, 0))
    vec = pl.BlockSpec((1, LANES), lambda g: (0, g))
    return pl.pallas_call(
        body, name=name, grid=(4,),
        in_specs=[col, col, wspec, vec],
        out_specs=[col, wspec, vec],
        out_shape=[_sds((L, 512)), _sds((4, LANES, LANES)), _sds((1, 512))],
        compiler_params=_cp("parallel"),
    )(g_m, pooled, pool_w, scale)


SGU_CHUNKS = 4


def _sgu_ln(v, gam, bet):
    gv = _gelu(v)
    mu = jnp.mean(gv, axis=-1, keepdims=True)
    xc = gv - mu
    rs = lax.rsqrt(jnp.mean(xc * xc, axis=-1, keepdims=True) + EPS)
    xh = xc * rs
    return xh, rs, xh * gam + bet


def _tril_ws(w_ref, g):
    r = lax.broadcasted_iota(jnp.int32, (LANES, LANES), 0)
    c = lax.broadcasted_iota(jnp.int32, (LANES, LANES), 1)
    return jnp.where(r >= c, w_ref[g], 0.0).astype(BF16)


def _sgu_fwd(z, ln_g, ln_b, w_s, b_st, name):
    L = z.shape[0]
    rb = min(SGU_CHUNKS * LANES, L)

    def body(u_ref, v_ref, g_ref, b_ref, w_ref, bs_ref, y_ref):
        _, _, vln = _sgu_ln(v_ref[...], g_ref[...], b_ref[...])
        gu = _gelu(u_ref[...])
        vb = vln.astype(BF16)
        for g in range(4):
            ws = _tril_ws(w_ref, g)
            for n in range(rb // LANES):
                rows = slice(n * LANES, (n + 1) * LANES)
                cols = slice(g * LANES, (g + 1) * LANES)
                mixed = _dot(ws, vb[rows, cols]) + bs_ref[:, g:g + 1]
                y_ref[rows, cols] = (gu[rows, cols] * mixed).astype(BF16)

    vm = lambda shape: pl.BlockSpec(shape, lambda i: tuple(0 for _ in shape))
    return pl.pallas_call(
        body, name=name, grid=(L // rb,),
        in_specs=[pl.BlockSpec((rb, 512), lambda i: (i, 1)), pl.BlockSpec((rb, 512), lambda i: (i, 2)),
                  vm((1, 512)), vm((1, 512)), vm((4, LANES, LANES)), vm((LANES, 4))],
        out_specs=pl.BlockSpec((rb, 512), lambda i: (i, 0)),
        out_shape=_sds((L, 512), BF16),
        compiler_params=_cp("parallel"),
    )(z, z, ln_g, ln_b, w_s, b_st)


def _sgu_bwd(g_m, z, ln_g, ln_b, w_s, b_st, name):
    L = z.shape[0]
    rb = min(SGU_CHUNKS * LANES, L)

    def body(gy_ref, u_ref, v_ref, g_ref, b_ref, w_ref, bs_ref, gu_ref, gv_ref, gw_ref, gbs_ref, gg_ref, gb_ref):
        i = pl.program_id(0)

        @pl.when(i == 0)
        def _():
            gw_ref[...] = jnp.zeros_like(gw_ref)
            gbs_ref[...] = jnp.zeros_like(gbs_ref)
            gg_ref[...] = jnp.zeros_like(gg_ref)
            gb_ref[...] = jnp.zeros_like(gb_ref)

        v = v_ref[...]
        u = u_ref[...]
        gy = gy_ref[...]
        xh, rs, vln = _sgu_ln(v, g_ref[...], b_ref[...])
        gel_u = _gelu(u)
        gmix = gy * gel_u
        vb = vln.astype(BF16)
        gmb = gmix.astype(BF16)
        r = lax.broadcasted_iota(jnp.int32, (LANES, LANES), 0)
        c = lax.broadcasted_iota(jnp.int32, (LANES, LANES), 1)
        gvln_cols = []
        for g in range(4):
            ws = _tril_ws(w_ref, g)
            cols = slice(g * LANES, (g + 1) * LANES)
            gw = jnp.zeros((LANES, LANES), F32)
            gbs = jnp.zeros((LANES, 1), F32)
            parts = []
            for n in range(rb // LANES):
                rows = slice(n * LANES, (n + 1) * LANES)
                mixed = _dot(ws, vb[rows, cols]) + bs_ref[:, g:g + 1]
                gu_ref[rows, cols] = gy[rows, cols] * mixed * _gelu_grad(u[rows, cols])
                parts.append(_dot(ws, gmb[rows, cols], TN))
                gw = gw + _dot(gmb[rows, cols], vb[rows, cols], NT)
                gbs = gbs + jnp.sum(gmix[rows, cols], axis=1, keepdims=True)
            gvln_cols.append(jnp.concatenate(parts, axis=0))
            gw_ref[g] += jnp.where(r >= c, gw, 0.0)
            gbs_ref[:, g:g + 1] += gbs
        gvln = jnp.concatenate(gvln_cols, axis=1)
        gg_ref[...] += jnp.sum(gvln * xh, axis=0, keepdims=True)
        gb_ref[...] += jnp.sum(gvln, axis=0, keepdims=True)
        gxh = gvln * g_ref[...]
        ggv = rs * (gxh - jnp.mean(gxh, axis=-1, keepdims=True) - xh * jnp.mean(gxh * xh, axis=-1, keepdims=True))
        gv_ref[...] = ggv * _gelu_grad(v)

    vm = lambda shape: pl.BlockSpec(shape, lambda i: tuple(0 for _ in shape))
    blk = pl.BlockSpec((rb, 512), lambda i: (i, 0))
    return pl.pallas_call(
        body, name=name, grid=(L // rb,),
        in_specs=[pl.BlockSpec((rb, 512), lambda i: (i, 1)), pl.BlockSpec((rb, 512), lambda i: (i, 1)),
                  pl.BlockSpec((rb, 512), lambda i: (i, 2)),
                  vm((1, 512)), vm((1, 512)), vm((4, LANES, LANES)), vm((LANES, 4))],
        out_specs=[blk, blk, vm((4, LANES, LANES)), vm((LANES, 4)), vm((1, 512)), vm((1, 512))],
        out_shape=[_sds((L, 512)), _sds((L, 512)), _sds((4, LANES, LANES)), _sds((LANES, 4)),
                   _sds((1, 512)), _sds((1, 512))],
        compiler_params=_cp("arbitrary"),
    )(g_m, z, z, ln_g, ln_b, w_s, b_st)


def _adamw_math(w, g, m, v):
    nm = ADAM_B1 * m + (1.0 - ADAM_B1) * g
    nv = ADAM_B2 * v + (1.0 - ADAM_B2) * (g * g)
    m_hat = nm / (1.0 - ADAM_B1 ** ADAM_STEP)
    v_hat = nv / (1.0 - ADAM_B2 ** ADAM_STEP)
    delta = -ADAM_LR * (m_hat / (jnp.sqrt(v_hat) + ADAM_EPS) + ADAM_WD * w)
    return delta, nm, nv


def _sum_adamw(parts, w, m, v, name):
    R, C = w.shape
    rb = min(128, R)

    def body(p_ref, w_ref, m_ref, v_ref, g_ref, d_ref, nm_ref, nv_ref):
        g = p_ref[0]
        for s in range(1, N_DEV):
            g = g + p_ref[s]
        d, nm, nv = _adamw_math(w_ref[...], g, m_ref[...], v_ref[...])
        g_ref[...] = g
        d_ref[...] = d
        nm_ref[...] = nm
        nv_ref[...] = nv

    blk = pl.BlockSpec((rb, C), lambda i: (i, 0))
    return pl.pallas_call(
        body, name=name, grid=(R // rb,),
        in_specs=[pl.BlockSpec((N_DEV, rb, C), lambda i: (0, i, 0)), blk, blk, blk],
        out_specs=[blk] * 4, out_shape=[_sds((R, C))] * 4,
        compiler_params=_cp("parallel"),
    )(parts, w, m, v)


def _sum_pieces(parts, name):
    _, R, C = parts.shape

    def body(p_ref, g_ref):
        g = p_ref[0]
        for s in range(1, N_DEV):
            g = g + p_ref[s]
        g_ref[...] = g

    vm = pl.BlockSpec(memory_space=pltpu.VMEM)
    return pl.pallas_call(body, name=name, in_specs=[vm], out_specs=vm, out_shape=_sds((R, C)))(parts)


def _adamw(w, g, m, v, name):
    vm = pl.BlockSpec(memory_space=pltpu.VMEM)

    def body(w_ref, g_ref, m_ref, v_ref, d_ref, nm_ref, nv_ref):
        d, nm, nv = _adamw_math(w_ref[...], g_ref[...], m_ref[...], v_ref[...])
        d_ref[...] = d
        nm_ref[...] = nm
        nv_ref[...] = nv

    return pl.pallas_call(body, name=name, in_specs=[vm] * 4, out_specs=[vm] * 3,
                          out_shape=[_sds(w.shape)] * 3)(w, g, m, v)


def _mesh_pos():
    return lax.axis_index("x"), lax.axis_index("y"), lax.axis_index("c")


def _dev_index(p):
    return 4 * p[0] + 2 * p[1] + p[2]


def _all_gather(xs, name):
    n = len(xs)

    def body(*refs):
        x_refs, o_refs = refs[:n], refs[n:2 * n]
        send_sems, recv_sems, local_sems = refs[2 * n:]
        x, y, c = _mesh_pos()
        me, sibling = (x, y, c), (x, y, 1 - c)
        chips = [(1 - x, y), (x, 1 - y), (1 - x, 1 - y)]

        def copy(i, k, block, to, src=None):
            dst = o_refs[i].at[_dev_index(block)]
            return pltpu.make_async_remote_copy(
                src_ref=dst if src is None else src, dst_ref=dst,
                send_sem=send_sems.at[i, k], recv_sem=recv_sems.at[i, k], device_id=to, device_id_type=MESH)

        mine = [pltpu.make_async_copy(x_refs[i], o_refs[i].at[_dev_index(me)], local_sems.at[i]) for i in range(n)]
        for cp in mine:
            cp.start()
        first = []
        for i in range(n):
            first.append(copy(i, 0, me, sibling, src=x_refs[i]))
            first += [copy(i, 1 + j, me, (*chip, c), src=x_refs[i]) for j, chip in enumerate(chips)]
        for cp in first:
            cp.start()
        passed = []
        for j, chip in enumerate(chips):
            for i in range(n):
                copy(i, 1 + j, (*chip, c), me).wait_recv()
                fwd = copy(i, 4 + j, (*chip, c), sibling)
                fwd.start()
                passed.append(fwd)
        for i in range(n):
            copy(i, 0, sibling, me).wait_recv()
            for j, chip in enumerate(chips):
                copy(i, 4 + j, (*chip, 1 - c), me).wait_recv()
        for cp in first + passed:
            cp.wait_send()
        for cp in mine:
            cp.wait()

    outs = pl.pallas_call(
        body, name=name,
        in_specs=[ANY] * n, out_specs=[ANY] * n,
        out_shape=[_sds((N_DEV,) + x.shape, x.dtype) for x in xs],
        scratch_shapes=[pltpu.SemaphoreType.DMA((n, 7)), pltpu.SemaphoreType.DMA((n, 7)),
                        pltpu.SemaphoreType.DMA((n,))],
    )(*xs)
    return list(outs)


def _exchange(gs, name):
    n = len(gs)

    def body(*refs):
        g_refs, o_refs = refs[:n], refs[n:2 * n]
        send_sems, recv_sems, local_sems = refs[2 * n:]
        x, y, c = _mesh_pos()
        me = (x, y, c)
        mi = _dev_index(me)
        peers = [(x ^ dx, y ^ dy, c ^ dc) for dx in range(2) for dy in range(2) for dc in range(2)][1:]

        def copy(i, k, peer):
            return pltpu.make_async_remote_copy(
                src_ref=g_refs[i].at[_dev_index(peer)], dst_ref=o_refs[i].at[mi],
                send_sem=send_sems.at[i, k], recv_sem=recv_sems.at[i, k], device_id=peer, device_id_type=MESH)

        mine = [pltpu.make_async_copy(g_refs[i].at[mi], o_refs[i].at[mi], local_sems.at[i]) for i in range(n)]
        for cp in mine:
            cp.start()
        sends = [copy(i, k, peer) for i in range(n) for k, peer in enumerate(peers)]
        for cp in sends:
            cp.start()
        for i in range(n):
            for k, peer in enumerate(peers):
                pltpu.make_async_remote_copy(
                    src_ref=g_refs[i].at[mi], dst_ref=o_refs[i].at[_dev_index(peer)],
                    send_sem=send_sems.at[i, k], recv_sem=recv_sems.at[i, k], device_id=peer,
                    device_id_type=MESH).wait_recv()
        for cp in sends:
            cp.wait_send()
        for cp in mine:
            cp.wait()

    outs = pl.pallas_call(
        body, name=name,
        in_specs=[ANY] * n, out_specs=[ANY] * n,
        out_shape=[_sds(g.shape, g.dtype) for g in gs],
        scratch_shapes=[pltpu.SemaphoreType.DMA((n, 7)), pltpu.SemaphoreType.DMA((n, 7)),
                        pltpu.SemaphoreType.DMA((n,))],
    )(*gs)
    return list(outs)


HBM = pl.BlockSpec(memory_space=pltpu.HBM)
SEM = pl.BlockSpec(memory_space=pltpu.SEMAPHORE)
EFFECT = pltpu.SideEffectType.DATAFLOW_SIDE_EFFECTING


def _peer_list():
    x, y, c = _mesh_pos()
    peers = [(x ^ dx, y ^ dy, c ^ dc) for dx in range(2) for dy in range(2) for dc in range(2)][1:]
    return (x, y, c), peers


def _split_copy(src_ref, land_ref, send_sems, recv_sems, i, k, peer, slot, exchange):
    return pltpu.make_async_remote_copy(
        src_ref=src_ref.at[_dev_index(peer)] if exchange else src_ref, dst_ref=land_ref.at[slot],
        send_sem=send_sems.at[7 * i + k], recv_sem=recv_sems.at[7 * i + k], device_id=peer, device_id_type=MESH)


def _comm_start(groups, name, exchange, dep=None):
    sizes = [len(g) for g in groups]
    n = sum(sizes)
    srcs = [a for g in groups for a in g]
    my_index = _dev_index(_mesh_pos())
    lands = []
    for a in srcs:
        if exchange:
            own = lax.dynamic_slice(a, (my_index, 0, 0), (1,) + a.shape[1:])
            shape = a.shape
        else:
            own = a[None]
            shape = (N_DEV,) + a.shape
        lands.append(lax.dynamic_update_slice(lax.empty(shape, a.dtype), own, (my_index, 0, 0)))

    n_dep = 0 if dep is None else 1

    def body(*refs):
        src_refs, land_refs = refs[:n], refs[n:2 * n]
        sem_refs = refs[2 * n + n_dep:2 * n + n_dep + 2 * len(sizes)]
        token_ref = refs[-1]
        me, peers = _peer_list()
        mi = _dev_index(me)
        i = 0
        for gi, sz in enumerate(sizes):
            for j in range(sz):
                for k, peer in enumerate(peers):
                    _split_copy(src_refs[i], land_refs[i], sem_refs[2 * gi], sem_refs[2 * gi + 1], j, k, peer, mi,
                                exchange).start()
                i += 1
        token_ref[...] = jnp.zeros_like(token_ref)

    sem_shapes = []
    for sz in sizes:
        sem_shapes += [pltpu.SemaphoreType.DMA((7 * sz,)), pltpu.SemaphoreType.DMA((7 * sz,))]
    thru = [pltpu.HBM(a.shape, a.dtype) for a in srcs + lands]
    n_sem = len(sem_shapes)
    outs = pl.pallas_call(
        body, name=name,
        out_shape=tuple(sem_shapes + thru + [_sds((8, LANES))]),
        in_specs=[HBM] * (2 * n) + [ANY] * n_dep,
        out_specs=tuple([SEM] * n_sem + [HBM] * (2 * n) + [pl.BlockSpec(memory_space=pltpu.VMEM)]),
        input_output_aliases={i: n_sem + i for i in range(2 * n)},
        compiler_params=pltpu.CompilerParams(has_side_effects=EFFECT),
    )(*[pltpu.with_memory_space_constraint(a, pltpu.HBM) for a in srcs + lands], *([] if dep is None else [dep]))
    sems, thru_src, thru_land, token = outs[:n_sem], outs[n_sem:n_sem + n], outs[n_sem + n:n_sem + 2 * n], outs[-1]
    result, off = [], 0
    for gi, sz in enumerate(sizes):
        result.append((sems[2 * gi], sems[2 * gi + 1], list(thru_src[off:off + sz]), list(thru_land[off:off + sz])))
        off += sz
    return result, token


def _comm_wait(group, after, name, exchange):
    send_sems, recv_sems, srcs, lands = group
    n = len(srcs)

    def body(*refs):
        src_refs, land_refs = refs[:n], refs[n:2 * n]
        ssem, rsem = refs[2 * n], refs[2 * n + 1]
        me, peers = _peer_list()
        for i in range(n):
            for k, peer in enumerate(peers):
                cp = _split_copy(src_refs[i], land_refs[i], ssem, rsem, i, k, peer, _dev_index(peer), exchange)
                cp.wait_send()
                cp.wait_recv()

    outs = pl.pallas_call(
        body, name=name,
        out_shape=tuple(pltpu.HBM(a.shape, a.dtype) for a in srcs + lands),
        in_specs=[HBM] * (2 * n) + [SEM, SEM, ANY],
        out_specs=tuple([HBM] * (2 * n)),
        input_output_aliases={i: i for i in range(2 * n)},
        compiler_params=pltpu.CompilerParams(has_side_effects=EFFECT),
    )(*srcs, *lands, send_sems, recv_sems, after)
    return list(outs[n:])


def _tie(a, token):
    return a + token[0, 0].astype(a.dtype)


def _col_pieces(g, n_cols):
    R = g.shape[0]
    c = n_cols // N_DEV
    return jnp.transpose(g[:, :n_cols].reshape(R, N_DEV, c), (1, 0, 2))


def _from_col_pieces(p, pad_to=None):
    _, R, c = p.shape
    w = jnp.transpose(p, (1, 0, 2)).reshape(R, N_DEV * c)
    if pad_to is not None and pad_to > N_DEV * c:
        w = jnp.pad(w, ((0, 0), (0, pad_to - N_DEV * c)))
    return w


def _pack(arrs, rows):
    flat = jnp.concatenate([a.reshape(-1).astype(F32) for a in arrs])
    return jnp.pad(flat, (0, rows * LANES - flat.shape[0])).reshape(rows, LANES)


def _unpack(packed, shapes):
    flat = packed.reshape(-1)
    out, off = [], 0
    for s in shapes:
        n = math.prod(s)
        out.append(flat[off:off + n].reshape(s))
        off += n
    return out


def _packed_rows(shapes):
    n = sum(math.prod(s) for s in shapes)
    unit = N_DEV * 8 * LANES
    return -(-n // unit) * unit // LANES


def kernel(x, mix_pre_g, mix_post_g, mlp_pre_g, mlp_post_g, w_in_even, s5_lam_re, s5_lam_im, s5_log_dt, s5_b_re, s5_b_im, s5_c_re, s5_c_im, s5_d, s5_w_glu, fox_b_f, w_out_even, w_in_odd, pool_w, pool_scale, sgu_ln_g, sgu_ln_b, sgu_w_s, sgu_b_s, w_out_odd, mlp_w1, mlp_w2, loss_target, m_mix_pre_g, m_mix_post_g, m_mlp_pre_g, m_mlp_post_g, m_w_in_even, m_s5_lam_re, m_s5_lam_im, m_s5_log_dt, m_s5_b_re, m_s5_b_im, m_s5_c_re, m_s5_c_im, m_s5_d, m_s5_w_glu, m_fox_b_f, m_w_out_even, m_w_in_odd, m_pool_w, m_pool_scale, m_sgu_ln_g, m_sgu_ln_b, m_sgu_w_s, m_sgu_b_s, m_w_out_odd, m_mlp_w1, m_mlp_w2, v_mix_pre_g, v_mix_post_g, v_mlp_pre_g, v_mlp_post_g, v_w_in_even, v_s5_lam_re, v_s5_lam_im, v_s5_log_dt, v_s5_b_re, v_s5_b_im, v_s5_c_re, v_s5_c_im, v_s5_d, v_s5_w_glu, v_fox_b_f, v_w_out_even, v_w_in_odd, v_pool_w, v_pool_scale, v_sgu_ln_g, v_sgu_ln_b, v_sgu_w_s, v_sgu_b_s, v_w_out_odd, v_mlp_w1, v_mlp_w2):
    weights = dict(mix_pre_g=mix_pre_g, mix_post_g=mix_post_g, mlp_pre_g=mlp_pre_g, mlp_post_g=mlp_post_g, w_in_even=w_in_even, s5_lam_re=s5_lam_re, s5_lam_im=s5_lam_im, s5_log_dt=s5_log_dt, s5_b_re=s5_b_re, s5_b_im=s5_b_im, s5_c_re=s5_c_re, s5_c_im=s5_c_im, s5_d=s5_d, s5_w_glu=s5_w_glu, fox_b_f=fox_b_f, w_out_even=w_out_even, w_in_odd=w_in_odd, pool_w=pool_w, pool_scale=pool_scale, sgu_ln_g=sgu_ln_g, sgu_ln_b=sgu_ln_b, sgu_w_s=sgu_w_s, sgu_b_s=sgu_b_s, w_out_odd=w_out_odd, mlp_w1=mlp_w1, mlp_w2=mlp_w2)
    mom_m = dict(mix_pre_g=m_mix_pre_g, mix_post_g=m_mix_post_g, mlp_pre_g=m_mlp_pre_g, mlp_post_g=m_mlp_post_g, w_in_even=m_w_in_even, s5_lam_re=m_s5_lam_re, s5_lam_im=m_s5_lam_im, s5_log_dt=m_s5_log_dt, s5_b_re=m_s5_b_re, s5_b_im=m_s5_b_im, s5_c_re=m_s5_c_re, s5_c_im=m_s5_c_im, s5_d=m_s5_d, s5_w_glu=m_s5_w_glu, fox_b_f=m_fox_b_f, w_out_even=m_w_out_even, w_in_odd=m_w_in_odd, pool_w=m_pool_w, pool_scale=m_pool_scale, sgu_ln_g=m_sgu_ln_g, sgu_ln_b=m_sgu_ln_b, sgu_w_s=m_sgu_w_s, sgu_b_s=m_sgu_b_s, w_out_odd=m_w_out_odd, mlp_w1=m_mlp_w1, mlp_w2=m_mlp_w2)
    mom_v = dict(mix_pre_g=v_mix_pre_g, mix_post_g=v_mix_post_g, mlp_pre_g=v_mlp_pre_g, mlp_post_g=v_mlp_post_g, w_in_even=v_w_in_even, s5_lam_re=v_s5_lam_re, s5_lam_im=v_s5_lam_im, s5_log_dt=v_s5_log_dt, s5_b_re=v_s5_b_re, s5_b_im=v_s5_b_im, s5_c_re=v_s5_c_re, s5_c_im=v_s5_c_im, s5_d=v_s5_d, s5_w_glu=v_s5_w_glu, fox_b_f=v_fox_b_f, w_out_even=v_w_out_even, w_in_odd=v_w_in_odd, pool_w=v_pool_w, pool_scale=v_pool_scale, sgu_ln_g=v_sgu_ln_g, sgu_ln_b=v_sgu_ln_b, sgu_w_s=v_sgu_w_s, sgu_b_s=v_sgu_b_s, w_out_odd=v_w_out_odd, mlp_w1=v_mlp_w1, mlp_w2=v_mlp_w2)
    names = list(weights)
    L = x.shape[1]
    x0 = x[0]
    target = loss_target[0]
    my_index = 4 * lax.axis_index("x") + 2 * lax.axis_index("y") + lax.axis_index("c")

    small_vec = jnp.zeros((8, LANES), F32)
    small_vec = small_vec.at[0, :64].set(pool_scale[0]).at[1, :64].set(sgu_ln_g[0]).at[2, :64].set(sgu_ln_b[0])
    ag_groups, ag_token = _comm_start(
        [[w_in_even[0].astype(BF16), small_vec],
         [s5_w_glu[0].astype(BF16), w_out_even[0].astype(BF16)],
         [mlp_w1[0].astype(BF16), mlp_w2[0].astype(BF16)],
         [w_in_odd[0].astype(BF16), w_out_odd[0].astype(BF16), mlp_w1[1].astype(BF16), mlp_w2[1].astype(BF16)]],
        "ag_start", exchange=False)

    lam_r = jnp.concatenate([s5_lam_re.reshape(1, S5_NS), s5_lam_im.reshape(1, S5_NS)], axis=0)
    ldt_r = jnp.repeat(s5_log_dt.reshape(32), 64).reshape(1, S5_NS)
    lam_c = jnp.transpose(lam_r)
    ldt_c = jnp.transpose(ldt_r)
    b_t = jnp.stack([jnp.tile(s5_b_re.reshape(S5_NS, 16), (1, 8)), jnp.tile(s5_b_im.reshape(S5_NS, 16), (1, 8))])
    c_t = jnp.stack([jnp.tile(s5_c_re.reshape(S5_W, 64), (1, 8)), jnp.tile(s5_c_im.reshape(S5_W, 64), (1, 8))])
    bf_pad = jnp.pad(fox_b_f, ((0, 0), (0, LANES - 8)))
    b_st = jnp.transpose(sgu_b_s[0])

    h0, rx0 = _rms_fwd(x0, _tie(mix_pre_g[0:1], ag_token), "rms0")
    tabs, bset, cset = _s5_prep(lam_r, ldt_r, lam_c, ldt_c, b_t, c_t, "s5_prep")
    ag0 = _comm_wait(ag_groups[0], tabs, "ag_wait0", exchange=False)
    win_e = _from_col_pieces(ag0[0], EVEN_PAD)
    pool_scale_f = ag0[1][:, 0, :64].reshape(1, 512)
    ln_g_f = ag0[1][:, 1, :64].reshape(1, 512)
    ln_b_f = ag0[1][:, 2, :64].reshape(1, 512)
    z0 = _mm(h0, win_e, name="win_even", bn=EVEN_PAD)
    bu = _s5_bu(z0, bset, "s5_bu")
    xs = _s5_scan(bu, tabs, "s5_scan")
    ag1 = _comm_wait(ag_groups[1], xs, "ag_wait1", exchange=False)
    wglu = ag1[0].reshape(S5_W, S5_W)
    wout_e = ag1[1].reshape(D_MODEL, D_MODEL)
    ylin, ya = _s5_out_fwd(xs, cset, z0, s5_d, wglu, "s5_out")
    fcum = _fox_f_fwd(z0, bf_pad, "fox_f")
    f8 = fcum[:, :8]
    fq = jnp.repeat(f8, 64, axis=1)
    frow = jnp.transpose(f8).reshape(4, 2, L)
    o_att, lse = _fox_fwd(z0, fq, frow, "fox_fwd")
    mix0 = jnp.concatenate([ya, o_att.astype(BF16)], axis=1)
    y0 = _mm(mix0, wout_e, name="wout_even")
    x1, ry0, h1, rx1 = _post_pre_fwd(x0, y0, mix_post_g[0:1], mlp_pre_g[0:1], "post0")
    ag2 = _comm_wait(ag_groups[2], rx1, "ag_wait2", exchange=False)
    w1 = [ag2[0], None]
    w2 = [ag2[1].reshape(4 * D_MODEL, D_MODEL), None]
    p0, a0 = _mm(h1, w1[0], name="mlp0_w1", b3=True, out_dtypes=(BF16, BF16), epi=_epi_relu2)
    o0 = _mm(a0, w2[0], name="mlp0_w2")
    x2, ro0, h2, rx2 = _post_pre_fwd(x1, o0, mlp_post_g[0:1], mix_pre_g[1:2], "post1")
    ag3 = _comm_wait(ag_groups[3], rx2, "ag_wait3", exchange=False)
    win_o = _from_col_pieces(ag3[0])
    wout_o = ag3[1].reshape(D_MODEL, D_MODEL)
    w1[1] = ag3[2]
    w2[1] = ag3[3].reshape(4 * D_MODEL, D_MODEL)
    z1 = _mm(h2, win_o, name="win_odd", bn=ODD_IN)
    yc, pooled = _pool_fwd(z1, pool_w[0], pool_scale_f, "pool_fwd")
    yd = _sgu_fwd(z1, ln_g_f, ln_b_f, sgu_w_s[0], b_st, "sgu_fwd")
    mix1 = jnp.concatenate([yc, yd], axis=1)
    y1 = _mm(mix1, wout_o, name="wout_odd")
    x3, ry1, h3, rx3 = _post_pre_fwd(x2, y1, mix_post_g[1:2], mlp_pre_g[1:2], "post2")
    p1, a1 = _mm(h3, w1[1], name="mlp1_w1", b3=True, out_dtypes=(BF16, BF16), epi=_epi_relu2)
    o1 = _mm(a1, w2[1], name="mlp1_w2")
    gx4, ro1, sq = _post_loss_fwd(x3, o1, mlp_post_g[1:2], target, "post3")

    g_o1, gg_mlp_post1 = _post_bwd(gx4, o1, ro1, mlp_post_g[1:2], "bpost3")
    g_p1 = _mm(g_o1, w2[1], name="b_mlp1_a", tb=True, out_dtypes=(BF16,), epi=_epi_relu2_bwd, extra=(p1,))
    gw2_1 = _mm(a1, g_o1, name="b_mlp1_w2", ta=True)
    g_h3 = _mm(g_p1, w1[1], name="b_mlp1_h", tb=True, b3=True)
    gw1_1 = _mm(h3, g_p1, name="b_mlp1_w1", ta=True, out3=True, bn=512)
    (ex1,), tok1 = _comm_start([[gw1_1, gw2_1.reshape(N_DEV, 512, D_MODEL)]], "ex_start1", exchange=True)
    g_x3, gg_mlp_pre1, g_y1, gg_mix_post1 = _pre_post_bwd(g_h3, x3, rx3, _tie(mlp_pre_g[1:2], tok1), gx4, y1, ry1, mix_post_g[1:2], "bpre3")
    g_mix1 = _mm(g_y1, wout_o, name="b_wout_odd_m", tb=True)
    gwout_o = _mm(mix1, g_y1, name="b_wout_odd_w", ta=True)
    g_xc, g_pool_w, g_pool_scale = _pool_bwd(g_mix1, pooled, pool_w[0], pool_scale_f, "pool_bwd")
    g_u1, g_v1, g_ws, g_bst, g_ln_g, g_ln_b = _sgu_bwd(g_mix1, z1, ln_g_f, ln_b_f, sgu_w_s[0], b_st, "sgu_bwd")
    g_z1 = jnp.concatenate([g_xc, g_u1, g_v1], axis=1).astype(BF16)
    g_h2 = _mm(g_z1, win_o, name="b_win_odd_h", tb=True, bk=ODD_IN)
    gwin_o = _mm(h2, g_z1, name="b_win_odd_w", ta=True, bn=ODD_IN)
    (ex2,), tok2 = _comm_start([[gwout_o.reshape(N_DEV, 128, D_MODEL), _col_pieces(gwin_o, ODD_IN)]], "ex_start2", exchange=True)
    g_x2, gg_mix_pre1, g_o0, gg_mlp_post0 = _pre_post_bwd(g_h2, x2, rx2, _tie(mix_pre_g[1:2], tok2), g_x3, o0, ro0, mlp_post_g[0:1], "bpre2")
    g_p0 = _mm(g_o0, w2[0], name="b_mlp0_a", tb=True, out_dtypes=(BF16,), epi=_epi_relu2_bwd, extra=(p0,))
    gw2_0 = _mm(a0, g_o0, name="b_mlp0_w2", ta=True)
    g_h1 = _mm(g_p0, w1[0], name="b_mlp0_h", tb=True, b3=True)
    gw1_0 = _mm(h1, g_p0, name="b_mlp0_w1", ta=True, out3=True, bn=512)
    (ex3,), tok3 = _comm_start([[gw1_0, gw2_0.reshape(N_DEV, 512, D_MODEL)]], "ex_start3", exchange=True)
    g_x1, gg_mlp_pre0, g_y0, gg_mix_post0 = _pre_post_bwd(g_h1, x1, rx1, _tie(mlp_pre_g[0:1], tok3), g_x2, y0, ry0, mix_post_g[0:1], "bpre1")
    g_mix0 = _mm(g_y0, wout_e, name="b_wout_even_m", tb=True)
    gwout_e = _mm(mix0, g_y0, name="b_wout_even_w", ta=True)
    gyl, gud, g_wglu, g_d = _s5_glu_bwd(g_mix0, ylin, z0, s5_d, wglu, "s5_glu_bwd")
    (ex4,), tok4 = _comm_start([[gwout_e.reshape(N_DEV, 128, D_MODEL), g_wglu.reshape(N_DEV, 64, S5_W)]], "ex_start4", exchange=True)
    gxd, gc_raw = _s5_c_bwd(gyl, xs, _tie(cset, tok4), "s5_c_bwd")
    gxs, ga = _s5_scan(gxd, tabs, "s5_scan_bwd", reverse=True, xs=xs)
    g_u0, gb_raw = _s5_bu_bwd(gxs, bset, z0, gud, "s5_bu_bwd")
    g_lam, g_ldt, g_b, g_c = _s5_param_bwd(lam_c, ldt_c, b_t, gb_raw, jnp.transpose(ga), gc_raw, "s5_param_bwd")
    dq, dfq = _fox_bwd_dq(z0, fq, frow, o_att, lse, g_mix0, "fox_dq")
    dk, dv, dfrow = _fox_bwd_dkv(z0, fq, frow, o_att, lse, g_mix0, "fox_dkv")
    dFk = jnp.pad(jnp.transpose(dfrow.reshape(8, L)), ((0, 0), (0, LANES - 8)))
    dFq = jnp.pad(dfq[:, ::64], ((0, 0), (0, LANES - 8)))
    dfl, db_f = _fox_f_bwd(dFk, dFq, z0, bf_pad, "fox_f_bwd")
    g_z0 = jnp.concatenate([g_u0, dq, dk, dv, dfl], axis=1).astype(BF16)
    g_h0 = _mm(g_z0, win_e, name="b_win_even_h", tb=True, bk=EVEN_PAD)
    grad_x, gg_mix_pre0 = _pre_bwd(g_h0, x0, rx0, mix_pre_g[0:1], g_x1, "bpre0")
    gwin_e = _mm(h0, g_z0, name="b_win_even_w", ta=True, bn=EVEN_PAD)

    small_grads = dict(
        mix_pre_g=jnp.concatenate([gg_mix_pre0, gg_mix_pre1]), mix_post_g=jnp.concatenate([gg_mix_post0, gg_mix_post1]),
        mlp_pre_g=jnp.concatenate([gg_mlp_pre0, gg_mlp_pre1]), mlp_post_g=jnp.concatenate([gg_mlp_post0, gg_mlp_post1]),
        s5_lam_re=g_lam[:, 0], s5_lam_im=g_lam[:, 1], s5_log_dt=g_ldt,
        s5_b_re=g_b[0, :, :16], s5_b_im=g_b[1, :, :16], s5_c_re=g_c[0, :, :64], s5_c_im=g_c[1, :, :64],
        s5_d=g_d, fox_b_f=db_f[:, :8], pool_w=g_pool_w, sgu_w_s=g_ws, sgu_b_s=jnp.transpose(g_bst),
        pool_scale=g_pool_scale, sgu_ln_g=g_ln_g, sgu_ln_b=g_ln_b)
    small_names = list(small_grads)
    full_shapes = [(512,) if nm in ("pool_scale", "sgu_ln_g", "sgu_ln_b") else weights[nm].shape for nm in small_names]
    full_shapes.append((1, 1))
    rows = _packed_rows(full_shapes)
    packed = _pack([small_grads[nm] for nm in small_names] + [sq], rows).reshape(N_DEV, rows // N_DEV, LANES)
    (recv_small,) = _exchange([packed], "exchange_small")
    piece = _sum_pieces(recv_small, "sum_small")
    (small_all,) = _all_gather([piece], "ag_small")
    small_full = _unpack(small_all.reshape(rows, LANES), full_shapes)
    loss = 0.5 * small_full.pop()[0, 0] / D_MODEL

    (ex5,), tok5 = _comm_start([[_col_pieces(gwin_e, EVEN_IN)]], "ex_start5", exchange=True, dep=small_all)
    r_w1_1, r_w2_1 = _comm_wait(ex1, tok5, "ex_wait1", exchange=True)
    r_wout_o, r_win_o = _comm_wait(ex2, tok5, "ex_wait2", exchange=True)
    r_w1_0, r_w2_0 = _comm_wait(ex3, tok5, "ex_wait3", exchange=True)
    r_wout_e, r_wglu = _comm_wait(ex4, tok5, "ex_wait4", exchange=True)
    small_g = {}
    for nm, g in zip(small_names, small_full):
        if nm in ("pool_scale", "sgu_ln_g", "sgu_ln_b"):
            g = lax.dynamic_slice(g, (my_index * 64,), (64,)).reshape(1, 64)
        small_g[nm] = g
    own_shapes = [weights[nm].shape for nm in small_names]
    rows2 = _packed_rows(own_shapes)
    pw = _pack([weights[nm] for nm in small_names], rows2)
    pg = _pack([small_g[nm] for nm in small_names], rows2)
    pm = _pack([mom_m[nm] for nm in small_names], rows2)
    pv = _pack([mom_v[nm] for nm in small_names], rows2)
    pd, pnm, pnv = _adamw(pw, pg, pm, pv, "adamw_small")
    res = {}
    for nm, d_, m_, v_ in zip(small_names, _unpack(pd, own_shapes), _unpack(pnm, own_shapes), _unpack(pnv, own_shapes)):
        res[nm] = (small_g[nm], d_, m_, v_)

    for nm, parts in (("mlp_w1", (r_w1_0, r_w1_1)), ("mlp_w2", (r_w2_0, r_w2_1))):
        per_layer = [_sum_adamw(parts[l], weights[nm][l], mom_m[nm][l], mom_v[nm][l], "adamw_%s_%d" % (nm, l))
                     for l in range(2)]
        res[nm] = tuple(jnp.stack([per_layer[0][k], per_layer[1][k]]) for k in range(4))
    big_parts = dict(s5_w_glu=r_wglu, w_out_even=r_wout_e, w_in_odd=r_win_o, w_out_odd=r_wout_o)
    for nm, parts in big_parts.items():
        outs = _sum_adamw(parts, weights[nm][0], mom_m[nm][0], mom_v[nm][0], "adamw_" + nm)
        res[nm] = tuple(o.reshape(weights[nm].shape) for o in outs)
    (r_win_e,) = _comm_wait(ex5, res["w_out_odd"][1], "ex_wait5", exchange=True)
    outs = _sum_adamw(r_win_e, w_in_even[0], m_w_in_even[0], v_w_in_even[0], "adamw_w_in_even")
    res["w_in_even"] = tuple(o.reshape(w_in_even.shape) for o in outs)

    grads = [res[nm][0].reshape(weights[nm].shape) for nm in names]
    deltas = [res[nm][1].reshape(weights[nm].shape) for nm in names]
    new_m = [res[nm][2].reshape(weights[nm].shape) for nm in names]
    new_v = [res[nm][3].reshape(weights[nm].shape) for nm in names]
    return (loss, grad_x[None], *grads, *deltas, *new_m, *new_v)
```

```python
import functools
import math

import jax
import jax.numpy as jnp
from jax import lax
from jax.experimental import pallas as pl
from jax.experimental.pallas import tpu as pltpu

F32 = jnp.float32
BF16 = jnp.bfloat16
MESH = pl.DeviceIdType.MESH
ANY = pl.BlockSpec(memory_space=pl.ANY)

N_DEV = 8
D_MODEL = 1024
EPS = 1e-6
S5_W = 512
S5_NS = 2048
SCAN_GROUPS = 4
SCAN_CHUNK = 1024
FOX_W = 512
EVEN_IN = 2056
EVEN_PAD = 2176
ODD_IN = 1536
LANES = 128
PIECE = 4 * D_MODEL // N_DEV
VMEM_LIMIT = 56 * 1024 * 1024

ADAM_LR = 0.001
ADAM_B1 = 0.9
ADAM_B2 = 0.999
ADAM_EPS = 1e-08
ADAM_WD = 0.01
ADAM_STEP = 10

NT = (((1,), (1,)), ((), ()))
TN = (((0,), (0,)), ((), ()))
NN = (((1,), (0,)), ((), ()))


def _cp(*sem):
    return pltpu.CompilerParams(dimension_semantics=sem, vmem_limit_bytes=VMEM_LIMIT)


def _sds(shape, dtype=F32):
    return jax.ShapeDtypeStruct(tuple(shape), dtype)


def _gelu(x):
    t = jnp.tanh(0.7978845608028654 * (x + 0.044715 * x * x * x))
    return 0.5 * x * (1.0 + t)


def _gelu_grad(x):
    t = jnp.tanh(0.7978845608028654 * (x + 0.044715 * x * x * x))
    du = 0.7978845608028654 * (1.0 + 3.0 * 0.044715 * x * x)
    return 0.5 * (1.0 + t) + 0.5 * x * (1.0 - t * t) * du


def _sigmoid(x):
    return 1.0 / (1.0 + jnp.exp(-x))


def _dot(a, b, dn=NN):
    return lax.dot_general(a, b, dn, preferred_element_type=F32)


def _mm(a, b, *, name, ta=False, tb=False, b3=False, out3=False, out_dtypes=(F32,), epi=None, extra=(),
        bm=1024, bn=1024, bk=1024):
    M = a.shape[1] if ta else a.shape[0]
    K = a.shape[0] if ta else a.shape[1]
    pw = b.shape[2] if b3 else PIECE
    if b3:
        N = b.shape[1] if tb else b.shape[0] * pw
        assert (b.shape[0] * pw if tb else b.shape[1]) == K
    else:
        N = b.shape[0] if tb else b.shape[1]
    bm, bn, bk = min(bm, M), min(bn, N), min(bk, K)
    assert M % bm == 0 and N % bn == 0 and K % bk == 0, (name, M, N, K, bm, bn, bk)
    assert not (b3 or out3) or ((bk if tb else bn) % pw == 0 and bn % PIECE == 0)
    nk = K // bk
    n_extra = len(extra)
    n_out = len(out_dtypes)
    dn = (((0 if ta else 1,), (1 if tb else 0,)), ((), ()))

    def body(*refs):
        a_ref, b_ref = refs[0], refs[1]
        e_refs = refs[2:2 + n_extra]
        o_refs = refs[2 + n_extra:2 + n_extra + n_out]
        acc_ref = refs[-1]
        k = pl.program_id(2)

        @pl.when(k == 0)
        def _():
            acc_ref[...] = jnp.zeros_like(acc_ref)

        if not b3:
            acc_ref[...] += lax.dot_general(a_ref[...].astype(BF16), b_ref[...].astype(BF16), dn,
                                            preferred_element_type=F32)
        elif tb:
            for t in range(bk // pw):
                a_t = a_ref[pl.ds(t * pw, pw), :] if ta else a_ref[:, pl.ds(t * pw, pw)]
                acc_ref[...] += lax.dot_general(a_t.astype(BF16), b_ref[t].astype(BF16), dn,
                                                preferred_element_type=F32)
        else:
            a_v = a_ref[...].astype(BF16)
            for t in range(bn // pw):
                acc_ref[:, pl.ds(t * pw, pw)] += lax.dot_general(a_v, b_ref[t].astype(BF16), dn,
                                                                 preferred_element_type=F32)

        @pl.when(k == nk - 1)
        def _():
            acc = acc_ref[...]
            outs = (acc,) if epi is None else epi(acc, *[e[...] for e in e_refs])
            for o_ref, o in zip(o_refs, outs):
                if out3:
                    for t in range(bn // PIECE):
                        o_ref[t] = o[:, t * PIECE:(t + 1) * PIECE].astype(o_ref.dtype)
                else:
                    o_ref[...] = o.astype(o_ref.dtype)

    a_spec = pl.BlockSpec((bk, bm), lambda i, j, k: (k, i)) if ta else pl.BlockSpec((bm, bk), lambda i, j, k: (i, k))
    if b3:
        if tb:
            b_spec = pl.BlockSpec((bk // pw, bn, pw), lambda i, j, k: (k, j, 0))
        else:
            b_spec = pl.BlockSpec((bn // pw, bk, pw), lambda i, j, k: (j, k, 0))
    else:
        b_spec = pl.BlockSpec((bn, bk), lambda i, j, k: (j, k)) if tb else pl.BlockSpec((bk, bn), lambda i, j, k: (k, j))
    e_specs = [pl.BlockSpec((bm, bn), lambda i, j, k: (i, j)) for _ in extra]
    if out3:
        o_specs = [pl.BlockSpec((bn // PIECE, bm, PIECE), lambda i, j, k: (j, i, 0)) for _ in out_dtypes]
        o_shapes = [_sds((N // PIECE, M, PIECE), dt) for dt in out_dtypes]
    else:
        o_specs = [pl.BlockSpec((bm, bn), lambda i, j, k: (i, j)) for _ in out_dtypes]
        o_shapes = [_sds((M, N), dt) for dt in out_dtypes]
    outs = pl.pallas_call(
        body, name=name, grid=(M // bm, N // bn, nk),
        in_specs=[a_spec, b_spec] + e_specs, out_specs=o_specs, out_shape=o_shapes,
        scratch_shapes=[pltpu.VMEM((bm, bn), F32)],
        compiler_params=_cp("parallel", "parallel", "arbitrary"),
    )(a, b, *extra)
    return outs[0] if n_out == 1 else outs


def _epi_relu2(acc):
    r = jnp.maximum(acc, 0.0)
    return acc, r * r


def _epi_relu2_bwd(acc, p):
    return (acc * (2.0 * jnp.maximum(p.astype(F32), 0.0)),)


def _row_spec(rb, w=D_MODEL):
    return pl.BlockSpec((rb, w), lambda i: (i, 0))


def _vec_spec(w=D_MODEL):
    return pl.BlockSpec((1, w), lambda i: (0, 0))


def _rstd(v):
    return lax.rsqrt(jnp.mean(v * v, axis=-1, keepdims=True) + EPS)


def _rms_fwd(x, g, name):
    L = x.shape[0]
    rb = min(256, L)

    def body(x_ref, g_ref, h_ref, r_ref):
        xv = x_ref[...]
        r = _rstd(xv)
        h_ref[...] = (xv * r * g_ref[...]).astype(BF16)
        r_ref[...] = r

    return pl.pallas_call(
        body, name=name, grid=(L // rb,),
        in_specs=[_row_spec(rb), _vec_spec()],
        out_specs=[_row_spec(rb), _row_spec(rb, 1)],
        out_shape=[_sds((L, D_MODEL), BF16), _sds((L, 1))],
        compiler_params=_cp("parallel"),
    )(x, g)


def _post_pre_fwd(x_in, y, g_post, g_pre, name):
    L = x_in.shape[0]
    rb = min(256, L)

    def body(x_ref, y_ref, gp_ref, gn_ref, xo_ref, ry_ref, h_ref, rx_ref):
        yv = y_ref[...]
        ry = _rstd(yv)
        xo = x_ref[...] + yv * ry * gp_ref[...]
        rx = _rstd(xo)
        xo_ref[...] = xo
        ry_ref[...] = ry
        h_ref[...] = (xo * rx * gn_ref[...]).astype(BF16)
        rx_ref[...] = rx

    return pl.pallas_call(
        body, name=name, grid=(L // rb,),
        in_specs=[_row_spec(rb), _row_spec(rb), _vec_spec(), _vec_spec()],
        out_specs=[_row_spec(rb), _row_spec(rb, 1), _row_spec(rb), _row_spec(rb, 1)],
        out_shape=[_sds((L, D_MODEL)), _sds((L, 1)), _sds((L, D_MODEL), BF16), _sds((L, 1))],
        compiler_params=_cp("parallel"),
    )(x_in, y, g_post, g_pre)


def _post_loss_fwd(x_in, y, g_post, target, name):
    L = x_in.shape[0]
    rb = min(256, L)

    def body(x_ref, y_ref, gp_ref, t_ref, gx_ref, ry_ref, loss_ref):
        i = pl.program_id(0)
        yv = y_ref[...]
        ry = _rstd(yv)
        diff = x_ref[...] + yv * ry * gp_ref[...] - t_ref[...]
        gx_ref[...] = diff * (1.0 / D_MODEL)
        ry_ref[...] = ry

        @pl.when(i == 0)
        def _():
            loss_ref[...] = jnp.zeros_like(loss_ref)

        loss_ref[...] += jnp.sum(diff * diff, keepdims=True)

    return pl.pallas_call(
        body, name=name, grid=(L // rb,),
        in_specs=[_row_spec(rb), _row_spec(rb), _vec_spec(), _row_spec(rb)],
        out_specs=[_row_spec(rb), _row_spec(rb, 1), pl.BlockSpec((1, 1), lambda i: (0, 0))],
        out_shape=[_sds((L, D_MODEL)), _sds((L, 1)), _sds((1, 1))],
        compiler_params=_cp("arbitrary"),
    )(x_in, y, g_post, target)


def _rms_bwd_rows(dy, xv, r, g):
    n = xv * r
    dyg = dy * g
    return r * (dyg - n * jnp.mean(dyg * n, axis=-1, keepdims=True)), n


def _post_bwd(g_out, y, ry, g_post, name):
    L = y.shape[0]
    rb = min(256, L)

    def body(go_ref, y_ref, ry_ref, gp_ref, gy_ref, gg_ref):
        i = pl.program_id(0)
        go = go_ref[...]
        gy, n = _rms_bwd_rows(go, y_ref[...], ry_ref[...], gp_ref[...])
        gy_ref[...] = gy.astype(BF16)

        @pl.when(i == 0)
        def _():
            gg_ref[...] = jnp.zeros_like(gg_ref)

        gg_ref[...] += jnp.sum(go * n, axis=0, keepdims=True)

    return pl.pallas_call(
        body, name=name, grid=(L // rb,),
        in_specs=[_row_spec(rb), _row_spec(rb), _row_spec(rb, 1), _vec_spec()],
        out_specs=[_row_spec(rb), _vec_spec()],
        out_shape=[_sds((L, D_MODEL), BF16), _sds((1, D_MODEL))],
        compiler_params=_cp("arbitrary"),
    )(g_out, y, ry, g_post)


def _pre_post_bwd(g_h, x, rx, g_pre, g_out, y_prev, ry_prev, g_post_prev, name):
    L = x.shape[0]
    rb = min(256, L)

    def body(gh_ref, x_ref, rx_ref, gn_ref, go_ref, y_ref, ry_ref, gp_ref, gi_ref, ggn_ref, gy_ref, ggp_ref):
        i = pl.program_id(0)
        gh = gh_ref[...]
        gx, n = _rms_bwd_rows(gh, x_ref[...], rx_ref[...], gn_ref[...])
        gi = go_ref[...] + gx
        gi_ref[...] = gi
        gy, ny = _rms_bwd_rows(gi, y_ref[...], ry_ref[...], gp_ref[...])
        gy_ref[...] = gy.astype(BF16)

        @pl.when(i == 0)
        def _():
            ggn_ref[...] = jnp.zeros_like(ggn_ref)
            ggp_ref[...] = jnp.zeros_like(ggp_ref)

        ggn_ref[...] += jnp.sum(gh * n, axis=0, keepdims=True)
        ggp_ref[...] += jnp.sum(gi * ny, axis=0, keepdims=True)

    return pl.pallas_call(
        body, name=name, grid=(L // rb,),
        in_specs=[_row_spec(rb), _row_spec(rb), _row_spec(rb, 1), _vec_spec(), _row_spec(rb),
                  _row_spec(rb), _row_spec(rb, 1), _vec_spec()],
        out_specs=[_row_spec(rb), _vec_spec(), _row_spec(rb), _vec_spec()],
        out_shape=[_sds((L, D_MODEL)), _sds((1, D_MODEL)), _sds((L, D_MODEL), BF16), _sds((1, D_MODEL))],
        compiler_params=_cp("arbitrary"),
    )(g_h, x, rx, g_pre, g_out, y_prev, ry_prev, g_post_prev)


def _pre_bwd(g_h, x, rx, g_pre, g_out, name):
    L = x.shape[0]
    rb = min(256, L)

    def body(gh_ref, x_ref, rx_ref, gn_ref, go_ref, gi_ref, ggn_ref):
        i = pl.program_id(0)
        gh = gh_ref[...]
        gx, n = _rms_bwd_rows(gh, x_ref[...], rx_ref[...], gn_ref[...])
        gi_ref[...] = go_ref[...] + gx

        @pl.when(i == 0)
        def _():
            ggn_ref[...] = jnp.zeros_like(ggn_ref)

        ggn_ref[...] += jnp.sum(gh * n, axis=0, keepdims=True)

    return pl.pallas_call(
        body, name=name, grid=(L // rb,),
        in_specs=[_row_spec(rb), _row_spec(rb), _row_spec(rb, 1), _vec_spec(), _row_spec(rb)],
        out_specs=[_row_spec(rb), _vec_spec()],
        out_shape=[_sds((L, D_MODEL)), _sds((1, D_MODEL))],
        compiler_params=_cp("arbitrary"),
    )(g_h, x, rx, g_pre, g_out)


def _cmul(ar, ai, br, bi):
    return ar * br - ai * bi, ar * bi + ai * br


def _zoh_cols(lr, li, ldt):
    dt = jnp.exp(ldt)
    mag = jnp.exp(lr * dt)
    ar = mag * jnp.cos(li * dt)
    ai = mag * jnp.sin(li * dt)
    den = lr * lr + li * li
    nr = ar - 1.0
    qr = (nr * lr + ai * li) / den
    qi = (ai * lr - nr * li) / den
    return dt, ar, ai, qr, qi, den


def _b_mask():
    r = lax.broadcasted_iota(jnp.int32, (S5_NS, LANES), 0)
    c = lax.broadcasted_iota(jnp.int32, (S5_NS, LANES), 1)
    return ((r >> 6) & 7) == (c >> 4)


def _c_mask():
    r = lax.broadcasted_iota(jnp.int32, (S5_W, 512), 0)
    c = lax.broadcasted_iota(jnp.int32, (S5_W, 512), 1)
    return ((r >> 4) & 7) == (c >> 6)


def _s5_prep(lam_r, ldt_r, lam_c, ldt_c, b_t, c_t, name):
    def body(lam_r_ref, ldt_r_ref, lam_c_ref, ldt_c_ref, b_ref, c_ref, tab_ref, bset_ref, cset_ref):
        lr, li = lam_r_ref[0:1, :], lam_r_ref[1:2, :]
        dt = jnp.exp(ldt_r_ref[...])
        mag = jnp.exp(lr * dt)
        p1r, p1i = mag * jnp.cos(li * dt), mag * jnp.sin(li * dt)
        p2r, p2i = _cmul(p1r, p1i, p1r, p1i)
        p3r, p3i = _cmul(p2r, p2i, p1r, p1i)
        p4r, p4i = _cmul(p2r, p2i, p2r, p2i)
        p5r, p5i = _cmul(p4r, p4i, p1r, p1i)
        p6r, p6i = _cmul(p4r, p4i, p2r, p2i)
        p7r, p7i = _cmul(p4r, p4i, p3r, p3i)
        p8r, p8i = _cmul(p4r, p4i, p4r, p4i)
        pw_r = [p1r, p2r, p3r, p4r, p5r, p6r, p7r, p8r]
        pw_i = [p1i, p2i, p3i, p4i, p5i, p6i, p7i, p8i]
        row = lax.broadcasted_iota(jnp.int32, (8, S5_NS), 0)
        zero = jnp.zeros((8, S5_NS), F32)

        def bc(v):
            return jnp.broadcast_to(v, (8, S5_NS))

        for d in range(2):
            sgn = 1.0 if d == 0 else -1.0
            for t, s in enumerate((1, 2, 4)):
                live = (row >= s) if d == 0 else (row <= 7 - s)
                tab_ref[d, 2 * t] = jnp.where(live, bc(pw_r[s - 1]), zero)
                tab_ref[d, 2 * t + 1] = jnp.where(live, bc(sgn * pw_i[s - 1]), zero)
            cr, ci = zero, zero
            for i in range(8):
                e = i if d == 0 else 7 - i
                cr = jnp.where(row == i, bc(pw_r[e]), cr)
                ci = jnp.where(row == i, bc(sgn * pw_i[e]), ci)
            tab_ref[d, 6] = cr
            tab_ref[d, 7] = ci

        _, _, _, qr, qi, _ = _zoh_cols(lam_c_ref[:, 0:1], lam_c_ref[:, 1:2], ldt_c_ref[...])
        bm = _b_mask()
        br, bi = b_ref[0], b_ref[1]
        bset_ref[0] = jnp.where(bm, qr * br - qi * bi, 0.0).astype(BF16)
        bset_ref[1] = jnp.where(bm, qr * bi + qi * br, 0.0).astype(BF16)
        cm = _c_mask()
        cset_ref[0] = jnp.where(cm, c_ref[0], 0.0).astype(BF16)
        cset_ref[1] = jnp.where(cm, c_ref[1], 0.0).astype(BF16)

    vm = pl.BlockSpec(memory_space=pltpu.VMEM)
    return pl.pallas_call(
        body, name=name, in_specs=[vm] * 6, out_specs=[vm] * 3,
        out_shape=[_sds((2, 8, 8, S5_NS)), _sds((2, S5_NS, LANES), BF16), _sds((2, S5_W, 512), BF16)],
        compiler_params=pltpu.CompilerParams(vmem_limit_bytes=VMEM_LIMIT),
    )(lam_r, ldt_r, lam_c, ldt_c, b_t, c_t)


def _s5_bu(z, bset, name):
    L = z.shape[0]
    bl = min(512, L)

    def body(u_ref, b_ref, o_ref):
        u = u_ref[...].astype(BF16)
        o_ref[0] = _dot(u, b_ref[0], NT)
        o_ref[1] = _dot(u, b_ref[1], NT)

    return pl.pallas_call(
        body, name=name, grid=(L // bl, 4),
        in_specs=[pl.BlockSpec((bl, LANES), lambda i, j: (i, j)),
                  pl.BlockSpec((2, 512, LANES), lambda i, j: (0, j, 0))],
        out_specs=pl.BlockSpec((2, bl, 512), lambda i, j: (0, i, j)),
        out_shape=_sds((2, L, S5_NS)),
        compiler_params=_cp("parallel", "parallel"),
    )(z, bset)


def _s5_scan(bu, tabs, name, reverse=False, xs=None):
    L = bu.shape[1]
    tl = min(SCAN_CHUNK, L)
    nc = L // tl
    nb = tl // 8
    W = SCAN_GROUPS * LANES
    d = 1 if reverse else 0
    with_ga = xs is not None
    assert not with_ga or reverse

    def body(*refs):
        if with_ga:
            bu_ref, tab_ref, xs_ref, x_ref, ga_ref, carry_ref, acc_ref = refs
        else:
            bu_ref, tab_ref, x_ref, carry_ref = refs
        c = pl.program_id(1)
        row = lax.broadcasted_iota(jnp.int32, (8, LANES), 0)

        @pl.when(c == 0)
        def _():
            carry_ref[...] = jnp.zeros_like(carry_ref)
            if with_ga:
                acc_ref[...] = jnp.zeros_like(acc_ref)

        def step(i, carry):
            b = (nb - 1 - i) if reverse else i
            off = pl.multiple_of(b * 8, 8)
            out = []
            for g in range(SCAN_GROUPS):
                lanes = pl.ds(g * LANES, LANES)
                cr, ci = carry[2 * g], carry[2 * g + 1]
                yr = bu_ref[0, pl.ds(off, 8), lanes]
                yi = bu_ref[1, pl.ds(off, 8), lanes]
                for t, s in enumerate((1, 2, 4)):
                    sh = (8 - s) if reverse else s
                    sr = pltpu.roll(yr, sh, 0)
                    si = pltpu.roll(yi, sh, 0)
                    mr, mi = tab_ref[2 * t, :, lanes], tab_ref[2 * t + 1, :, lanes]
                    yr, yi = yr + mr * sr - mi * si, yi + mr * si + mi * sr
                pr, pi = tab_ref[6, :, lanes], tab_ref[7, :, lanes]
                yr, yi = yr + pr * cr - pi * ci, yi + pr * ci + pi * cr
                x_ref[0, pl.ds(off, 8), lanes] = yr
                x_ref[1, pl.ds(off, 8), lanes] = yi
                if with_ga:
                    nr = jnp.where(row == 7, cr, pltpu.roll(yr, 7, 0))
                    ni = jnp.where(row == 7, ci, pltpu.roll(yi, 7, 0))
                    xr = xs_ref[0, pl.ds(off, 8), lanes]
                    xi = xs_ref[1, pl.ds(off, 8), lanes]
                    acc_ref[0, :, lanes] += xr * nr + xi * ni
                    acc_ref[1, :, lanes] += xr * ni - xi * nr
                last = 0 if reverse else 7
                out += [jnp.broadcast_to(yr[last:last + 1, :], (8, LANES)),
                        jnp.broadcast_to(yi[last:last + 1, :], (8, LANES))]
            return tuple(out)

        init = []
        for g in range(SCAN_GROUPS):
            init += [carry_ref[0, :, pl.ds(g * LANES, LANES)], carry_ref[1, :, pl.ds(g * LANES, LANES)]]
        fin = lax.fori_loop(0, nb, step, tuple(init))
        for g in range(SCAN_GROUPS):
            carry_ref[0, :, pl.ds(g * LANES, LANES)] = fin[2 * g]
            carry_ref[1, :, pl.ds(g * LANES, LANES)] = fin[2 * g + 1]
        if with_ga:
            @pl.when(c == nc - 1)
            def _():
                ga_ref[0:1, :] = jnp.sum(acc_ref[0], axis=0, keepdims=True)
                ga_ref[1:2, :] = jnp.sum(acc_ref[1], axis=0, keepdims=True)

    chunk = (lambda g, c: (0, nc - 1 - c, g)) if reverse else (lambda g, c: (0, c, g))
    seq = pl.BlockSpec((2, tl, W), chunk)
    in_specs = [seq, pl.BlockSpec((None, 8, 8, W), lambda g, c: (d, 0, 0, g))]
    out_specs = [seq]
    out_shape = [_sds((2, L, S5_NS))]
    scratch = [pltpu.VMEM((2, 8, W), F32)]
    args = [bu, tabs]
    if with_ga:
        in_specs.append(seq)
        args.append(xs)
        out_specs.append(pl.BlockSpec((2, W), lambda g, c: (0, g)))
        out_shape.append(_sds((2, S5_NS)))
        scratch.append(pltpu.VMEM((2, 8, W), F32))
    outs = pl.pallas_call(
        body, name=name, grid=(S5_NS // W, nc), in_specs=in_specs, out_specs=out_specs, out_shape=out_shape,
        scratch_shapes=scratch, compiler_params=_cp("parallel", "arbitrary"),
    )(*args)
    return outs if with_ga else outs[0]


def _s5_out_fwd(xs, cset, z, dvec, wglu, name):
    L = z.shape[0]
    bl = min(256, L)

    def body(x_ref, c_ref, u_ref, d_ref, w_ref, ylin_ref, ya_ref):
        cols = []
        for j in range(4):
            xr = x_ref[0, :, 512 * j:512 * (j + 1)].astype(BF16)
            xi = x_ref[1, :, 512 * j:512 * (j + 1)].astype(BF16)
            cr = c_ref[0, LANES * j:LANES * (j + 1), :]
            ci = c_ref[1, LANES * j:LANES * (j + 1), :]
            cols.append(_dot(xr, cr, NT) - _dot(xi, ci, NT))
        ylin = jnp.concatenate(cols, axis=1) + d_ref[...] * u_ref[...]
        yg = _gelu(ylin)
        t = _dot(yg.astype(BF16), w_ref[...])
        ylin_ref[...] = ylin
        ya_ref[...] = (yg * _sigmoid(t)).astype(BF16)

    return pl.pallas_call(
        body, name=name, grid=(L // bl,),
        in_specs=[pl.BlockSpec((2, bl, S5_NS), lambda i: (0, i, 0)),
                  pl.BlockSpec((2, S5_W, 512), lambda i: (0, 0, 0)),
                  pl.BlockSpec((bl, S5_W), lambda i: (i, 0)),
                  pl.BlockSpec((1, S5_W), lambda i: (0, 0)),
                  pl.BlockSpec((S5_W, S5_W), lambda i: (0, 0))],
        out_specs=[pl.BlockSpec((bl, S5_W), lambda i: (i, 0))] * 2,
        out_shape=[_sds((L, S5_W)), _sds((L, S5_W), BF16)],
        compiler_params=_cp("parallel"),
    )(xs, cset, z, dvec, wglu)


def _s5_glu_bwd(g_m, ylin, z, dvec, wglu, name):
    L = z.shape[0]
    bl = min(256, L)

    def body(g_ref, ylin_ref, u_ref, d_ref, w_ref, gyl_ref, gud_ref, gw_ref, gd_ref):
        i = pl.program_id(0)
        ylin = ylin_ref[...]
        yg = _gelu(ylin)
        ygb = yg.astype(BF16)
        sg = _sigmoid(_dot(ygb, w_ref[...]))
        gya = g_ref[...]
        gt = gya * yg * sg * (1.0 - sg)
        gtb = gt.astype(BF16)
        gyg = gya * sg + _dot(gtb, w_ref[...], NT)
        gyl = gyg * _gelu_grad(ylin)
        gyl_ref[...] = gyl
        gud_ref[...] = gyl * d_ref[...]

        @pl.when(i == 0)
        def _():
            gw_ref[...] = jnp.zeros_like(gw_ref)
            gd_ref[...] = jnp.zeros_like(gd_ref)

        gw_ref[...] += _dot(ygb, gtb, TN)
        gd_ref[...] += jnp.sum(gyl * u_ref[...], axis=0, keepdims=True)

    blk = pl.BlockSpec((bl, S5_W), lambda i: (i, 0))
    return pl.pallas_call(
        body, name=name, grid=(L // bl,),
        in_specs=[blk, blk, blk, pl.BlockSpec((1, S5_W), lambda i: (0, 0)),
                  pl.BlockSpec((S5_W, S5_W), lambda i: (0, 0))],
        out_specs=[blk, blk, pl.BlockSpec((S5_W, S5_W), lambda i: (0, 0)), pl.BlockSpec((1, S5_W), lambda i: (0, 0))],
        out_shape=[_sds((L, S5_W)), _sds((L, S5_W)), _sds((S5_W, S5_W)), _sds((1, S5_W))],
        compiler_params=_cp("arbitrary"),
    )(g_m, ylin, z, dvec, wglu)


def _s5_c_bwd(gyl, xs, cset, name):
    L = gyl.shape[0]
    bl = min(256, L)

    def body(g_ref, x_ref, c_ref, gx_ref, gc_ref):
        i = pl.program_id(0)

        @pl.when(i == 0)
        def _():
            gc_ref[...] = jnp.zeros_like(gc_ref)

        for j in range(4):
            gj = g_ref[:, LANES * j:LANES * (j + 1)].astype(BF16)
            cr = c_ref[0, LANES * j:LANES * (j + 1), :]
            ci = c_ref[1, LANES * j:LANES * (j + 1), :]
            gx_ref[0, :, 512 * j:512 * (j + 1)] = _dot(gj, cr)
            gx_ref[1, :, 512 * j:512 * (j + 1)] = -_dot(gj, ci)
            xr = x_ref[0, :, 512 * j:512 * (j + 1)].astype(BF16)
            xi = x_ref[1, :, 512 * j:512 * (j + 1)].astype(BF16)
            gc_ref[0, LANES * j:LANES * (j + 1), :] += _dot(gj, xr, TN)
            gc_ref[1, LANES * j:LANES * (j + 1), :] -= _dot(gj, xi, TN)

    return pl.pallas_call(
        body, name=name, grid=(L // bl,),
        in_specs=[pl.BlockSpec((bl, S5_W), lambda i: (i, 0)),
                  pl.BlockSpec((2, bl, S5_NS), lambda i: (0, i, 0)),
                  pl.BlockSpec((2, S5_W, 512), lambda i: (0, 0, 0))],
        out_specs=[pl.BlockSpec((2, bl, S5_NS), lambda i: (0, i, 0)),
                   pl.BlockSpec((2, S5_W, 512), lambda i: (0, 0, 0))],
        out_shape=[_sds((2, L, S5_NS)), _sds((2, S5_W, 512))],
        compiler_params=_cp("arbitrary"),
    )(gyl, xs, cset)


def _s5_bu_bwd(gx, bset, z, gud, name):
    L = z.shape[0]
    bl = min(512, L)

    def body(gx_ref, b_ref, u_ref, gud_ref, gu_ref, gb_ref):
        i = pl.program_id(1)
        gr = gx_ref[0].astype(BF16)
        gi = gx_ref[1].astype(BF16)
        gu_ref[...] = gud_ref[...] + _dot(gr, b_ref[0]) + _dot(gi, b_ref[1])

        @pl.when(i == 0)
        def _():
            gb_ref[...] = jnp.zeros_like(gb_ref)

        u = u_ref[...].astype(BF16)
        gb_ref[0] += _dot(gr, u, TN)
        gb_ref[1] += _dot(gi, u, TN)

    return pl.pallas_call(
        body, name=name, grid=(4, L // bl),
        in_specs=[pl.BlockSpec((2, bl, 512), lambda j, i: (0, i, j)),
                  pl.BlockSpec((2, 512, LANES), lambda j, i: (0, j, 0)),
                  pl.BlockSpec((bl, LANES), lambda j, i: (i, j)),
                  pl.BlockSpec((bl, LANES), lambda j, i: (i, j))],
        out_specs=[pl.BlockSpec((bl, LANES), lambda j, i: (i, j)),
                   pl.BlockSpec((2, 512, LANES), lambda j, i: (0, j, 0))],
        out_shape=[_sds((L, S5_W)), _sds((2, S5_NS, LANES))],
        compiler_params=_cp("parallel", "arbitrary"),
    )(gx, bset, z, gud)


def _s5_param_bwd(lam_c, ldt_c, b_t, gb, ga_c, gc, name):
    def body(lam_ref, ldt_ref, b_ref, gb_ref, ga_ref, gc_ref, glam_ref, gldt_ref, gbo_ref, gco_ref):
        lr, li = lam_ref[:, 0:1], lam_ref[:, 1:2]
        dt, ar, ai, qr, qi, den = _zoh_cols(lr, li, ldt_ref[...])
        bm = _b_mask()
        gbr = jnp.where(bm, gb_ref[0], 0.0)
        gbi = jnp.where(bm, gb_ref[1], 0.0)
        br, bi = b_ref[0], b_ref[1]
        obr = gbr * qr + gbi * qi
        obi = gbi * qr - gbr * qi
        gqr = jnp.sum(gbr * br + gbi * bi, axis=1, keepdims=True)
        gqi = jnp.sum(gbi * br - gbr * bi, axis=1, keepdims=True)
        for s in (64, 32, 16):
            obr = obr + pltpu.roll(obr, s, 1)
            obi = obi + pltpu.roll(obi, s, 1)
        gbo_ref[0] = obr
        gbo_ref[1] = obi
        gar = ga_ref[:, 0:1] + (gqr * lr - gqi * li) / den
        gai = ga_ref[:, 1:2] + (gqr * li + gqi * lr) / den
        qlr = (qr * lr + qi * li) / den
        qli = (qi * lr - qr * li) / den
        glr = -(gqr * qlr + gqi * qli)
        gli = -(gqi * qlr - gqr * qli)
        glr = glr + dt * (gar * ar + gai * ai)
        gli = gli + dt * (gai * ar - gar * ai)
        wr, wi = _cmul(lr, li, ar, ai)
        gldt = (gar * wr + gai * wi) * dt
        glam_ref[:, 0:1] = glr
        glam_ref[:, 1:2] = gli
        r = lax.broadcasted_iota(jnp.int32, (S5_NS, 32), 0)
        c = lax.broadcasted_iota(jnp.int32, (S5_NS, 32), 1)
        gldt_ref[...] = jnp.sum(jnp.where((r >> 6) == c, gldt, 0.0), axis=0, keepdims=True)
        cm = _c_mask()
        for k in range(2):
            oc = jnp.where(cm, gc_ref[k], 0.0)
            for s in (256, 128, 64):
                oc = oc + pltpu.roll(oc, s, 1)
            gco_ref[k] = oc[:, 0:LANES]

    vm = pl.BlockSpec(memory_space=pltpu.VMEM)
    return pl.pallas_call(
        body, name=name, in_specs=[vm] * 6, out_specs=[vm] * 4,
        out_shape=[_sds((S5_NS, 2)), _sds((1, 32)), _sds((2, S5_NS, LANES)), _sds((2, S5_W, LANES))],
        compiler_params=pltpu.CompilerParams(vmem_limit_bytes=VMEM_LIMIT),
    )(lam_c, ldt_c, b_t, gb, ga_c, gc)


FL_BLK = EVEN_PAD // LANES - 1
Q_BLK, K_BLK, V_BLK = 4, 8, 12
NEG = -1e30


def _log_sigmoid(v):
    return jnp.minimum(v, 0.0) - jnp.log(1.0 + jnp.exp(-jnp.abs(v)))


def _fox_f_fwd(z, bf, name):
    L = z.shape[0]
    tl = min(256, L)

    def body(fl_ref, b_ref, f_ref, carry_ref):
        i = pl.program_id(0)

        @pl.when(i == 0)
        def _():
            carry_ref[...] = jnp.zeros_like(carry_ref)

        lf = _log_sigmoid(fl_ref[...] + b_ref[...])
        r = lax.broadcasted_iota(jnp.int32, (tl, tl), 0)
        c = lax.broadcasted_iota(jnp.int32, (tl, tl), 1)
        tri = (r >= c).astype(F32)
        cs = lax.dot_general(tri, lf, NN, precision=lax.Precision.HIGHEST, preferred_element_type=F32) + carry_ref[...]
        f_ref[...] = cs
        carry_ref[...] = cs[tl - 1:tl, :]

    return pl.pallas_call(
        body, name=name, grid=(L // tl,),
        in_specs=[pl.BlockSpec((tl, LANES), lambda i: (i, FL_BLK)), pl.BlockSpec((1, LANES), lambda i: (0, 0))],
        out_specs=pl.BlockSpec((tl, LANES), lambda i: (i, 0)),
        out_shape=_sds((L, LANES)),
        scratch_shapes=[pltpu.VMEM((1, LANES), F32)],
        compiler_params=_cp("arbitrary"),
    )(z, bf)


def _fox_f_bwd(dFk, dFq, z, bf, name):
    L = z.shape[0]
    tl = min(256, L)
    nb = L // tl

    def body(dfk_ref, dfq_ref, fl_ref, b_ref, dfl_ref, db_ref, carry_ref):
        i = pl.program_id(0)

        @pl.when(i == 0)
        def _():
            carry_ref[...] = jnp.zeros_like(carry_ref)
            db_ref[...] = jnp.zeros_like(db_ref)

        r = lax.broadcasted_iota(jnp.int32, (tl, tl), 0)
        c = lax.broadcasted_iota(jnp.int32, (tl, tl), 1)
        tri = (r <= c).astype(F32)
        cs = lax.dot_general(tri, dfk_ref[...] + dfq_ref[...], NN, precision=lax.Precision.HIGHEST,
                             preferred_element_type=F32) + carry_ref[...]
        carry_ref[...] = cs[0:1, :]
        dfl = cs * _sigmoid(-(fl_ref[...] + b_ref[...]))
        dfl_ref[...] = dfl
        db_ref[...] += jnp.sum(dfl, axis=0, keepdims=True)

    return pl.pallas_call(
        body, name=name, grid=(nb,),
        in_specs=[pl.BlockSpec((tl, LANES), lambda i: (nb - 1 - i, 0)),
                  pl.BlockSpec((tl, LANES), lambda i: (nb - 1 - i, 0)),
                  pl.BlockSpec((tl, LANES), lambda i: (nb - 1 - i, FL_BLK)),
                  pl.BlockSpec((1, LANES), lambda i: (0, 0))],
        out_specs=[pl.BlockSpec((tl, LANES), lambda i: (nb - 1 - i, 0)), pl.BlockSpec((1, LANES), lambda i: (0, 0))],
        out_shape=[_sds((L, LANES)), _sds((1, LANES))],
        scratch_shapes=[pltpu.VMEM((1, LANES), F32)],
        compiler_params=_cp("arbitrary"),
    )(dFk, dFq, z, bf)


def _head_mask(hh):
    lane = lax.broadcasted_iota(jnp.int32, (1, LANES), 1)
    return (lane >> 6) == hh


def _fox_scores(q_scaled, k, fq_ref, fr_ref, hh, causal):
    qh = jnp.where(_head_mask(hh), q_scaled, 0.0).astype(BF16)
    s = _dot(qh, k, NT) + (fq_ref[:, 64 * hh:64 * hh + 1] - fr_ref[hh:hh + 1, :])
    return s if causal is None else jnp.where(causal, s, NEG)


def _causal(qi, ki, T):
    r = qi * T + lax.broadcasted_iota(jnp.int32, (T, T), 0)
    c = ki * T + lax.broadcasted_iota(jnp.int32, (T, T), 1)
    return c <= r


def _fox_fwd(z, fq, frow, name):
    L = z.shape[0]
    T = min(512, L)
    nq = L // T

    def body(q_ref, k_ref, v_ref, fq_ref, fr_ref, o_ref, lse_ref, m_ref, l_ref, acc_ref):
        qi, ki = pl.program_id(1), pl.program_id(2)

        @pl.when(ki == 0)
        def _():
            m_ref[...] = jnp.full_like(m_ref, NEG)
            l_ref[...] = jnp.zeros_like(l_ref)
            acc_ref[...] = jnp.zeros_like(acc_ref)

        def step(diagonal):
            q = q_ref[...] * 0.125
            k = k_ref[...].astype(BF16)
            v = v_ref[...].astype(BF16)
            causal = _causal(qi, ki, T) if diagonal else None
            alphas, pvs = [], []
            for hh in range(2):
                s = _fox_scores(q, k, fq_ref, fr_ref, hh, causal)
                m_old = m_ref[hh]
                m_new = jnp.maximum(m_old, jnp.max(s, axis=1, keepdims=True))
                alpha = jnp.exp(m_old - m_new)
                p = jnp.exp(s - m_new)
                l_ref[hh] = alpha * l_ref[hh] + jnp.sum(p, axis=1, keepdims=True)
                m_ref[hh] = m_new
                alphas.append(alpha)
                pvs.append(_dot(p.astype(BF16), v))
            h0 = _head_mask(0)
            acc_ref[...] = jnp.where(h0, alphas[0], alphas[1]) * acc_ref[...] + jnp.where(h0, pvs[0], pvs[1])

        @pl.when(ki < qi)
        def _():
            step(False)

        @pl.when(ki == qi)
        def _():
            step(True)

        @pl.when(ki == nq - 1)
        def _():
            h0 = _head_mask(0)
            inv = jnp.where(h0, 1.0 / l_ref[0], 1.0 / l_ref[1])
            o_ref[...] = acc_ref[...] * inv
            lse_ref[...] = jnp.where(h0, m_ref[0] + jnp.log(l_ref[0]), m_ref[1] + jnp.log(l_ref[1]))

    def zspec(base, rowf):
        return pl.BlockSpec((T, LANES), lambda j, qi, ki: (rowf(qi, ki), base + j))

    kv_row = lambda qi, ki: jnp.minimum(ki, qi)
    q_row = lambda qi, ki: qi
    return pl.pallas_call(
        body, name=name, grid=(4, nq, nq),
        in_specs=[zspec(Q_BLK, q_row), zspec(K_BLK, kv_row), zspec(V_BLK, kv_row),
                  pl.BlockSpec((T, LANES), lambda j, qi, ki: (qi, j)),
                  pl.BlockSpec((None, 2, T), lambda j, qi, ki: (j, 0, jnp.minimum(ki, qi)))],
        out_specs=[pl.BlockSpec((T, LANES), lambda j, qi, ki: (qi, j))] * 2,
        out_shape=[_sds((L, FOX_W)), _sds((L, FOX_W))],
        scratch_shapes=[pltpu.VMEM((2, T, 1), F32), pltpu.VMEM((2, T, 1), F32), pltpu.VMEM((T, LANES), F32)],
        compiler_params=_cp("parallel", "parallel", "arbitrary"),
    )(z, z, z, fq, frow)


def _fox_delta(dob, o, hh):
    return jnp.sum(jnp.where(_head_mask(hh), dob.astype(F32) * o, 0.0), axis=1, keepdims=True)


def _fox_bwd_dq(z, fq, frow, o, lse, g_m, name):
    L = z.shape[0]
    T = min(512, L)
    nq = L // T

    def body(q_ref, k_ref, v_ref, fq_ref, fr_ref, o_ref, lse_ref, do_ref, dq_ref, df_ref, acc_ref, rs_ref):
        qi, ki = pl.program_id(1), pl.program_id(2)

        @pl.when(ki == 0)
        def _():
            acc_ref[...] = jnp.zeros_like(acc_ref)
            rs_ref[...] = jnp.zeros_like(rs_ref)

        def step(diagonal):
            q = q_ref[...] * 0.125
            k = k_ref[...].astype(BF16)
            v = v_ref[...].astype(BF16)
            dob = do_ref[...].astype(BF16)
            causal = _causal(qi, ki, T) if diagonal else None
            dqs = []
            for hh in range(2):
                s = _fox_scores(q, k, fq_ref, fr_ref, hh, causal)
                p = jnp.exp(s - lse_ref[:, 64 * hh:64 * hh + 1])
                dp = _dot(jnp.where(_head_mask(hh), dob, 0.0), v, NT)
                ds = p * (dp - _fox_delta(dob, o_ref[...], hh))
                rs_ref[hh] += jnp.sum(ds, axis=1, keepdims=True)
                dqs.append(_dot(ds.astype(BF16), k))
            acc_ref[...] += jnp.where(_head_mask(0), dqs[0], dqs[1])

        @pl.when(ki < qi)
        def _():
            step(False)

        @pl.when(ki == qi)
        def _():
            step(True)

        @pl.when(ki == nq - 1)
        def _():
            dq_ref[...] = acc_ref[...] * 0.125
            df_ref[...] = jnp.where(_head_mask(0), rs_ref[0], rs_ref[1])

    def zspec(base, rowf):
        return pl.BlockSpec((T, LANES), lambda j, qi, ki: (rowf(qi, ki), base + j))

    kv_row = lambda qi, ki: jnp.minimum(ki, qi)
    q_row = lambda qi, ki: qi
    qblk = pl.BlockSpec((T, LANES), lambda j, qi, ki: (qi, j))
    return pl.pallas_call(
        body, name=name, grid=(4, nq, nq),
        in_specs=[zspec(Q_BLK, q_row), zspec(K_BLK, kv_row), zspec(V_BLK, kv_row), qblk,
                  pl.BlockSpec((None, 2, T), lambda j, qi, ki: (j, 0, jnp.minimum(ki, qi))),
                  qblk, qblk, pl.BlockSpec((T, LANES), lambda j, qi, ki: (qi, 4 + j))],
        out_specs=[qblk, qblk],
        out_shape=[_sds((L, FOX_W)), _sds((L, FOX_W))],
        scratch_shapes=[pltpu.VMEM((T, LANES), F32), pltpu.VMEM((2, T, 1), F32)],
        compiler_params=_cp("parallel", "parallel", "arbitrary"),
    )(z, z, z, fq, frow, o, lse, g_m)


def _fox_bwd_dkv(z, fq, frow, o, lse, g_m, name):
    L = z.shape[0]
    T = min(512, L)
    nq = L // T

    def body(q_ref, k_ref, v_ref, fq_ref, fr_ref, o_ref, lse_ref, do_ref, dk_ref, dv_ref, df_ref,
             dk_acc, dv_acc, df_acc):
        ki, qi = pl.program_id(1), pl.program_id(2)

        @pl.when(qi == 0)
        def _():
            dk_acc[...] = jnp.zeros_like(dk_acc)
            dv_acc[...] = jnp.zeros_like(dv_acc)
            df_acc[...] = jnp.zeros_like(df_acc)

        def step(diagonal):
            q = q_ref[...] * 0.125
            qb = q.astype(BF16)
            k = k_ref[...].astype(BF16)
            v = v_ref[...].astype(BF16)
            dob = do_ref[...].astype(BF16)
            causal = _causal(qi, ki, T) if diagonal else None
            dvs, dks = [], []
            for hh in range(2):
                s = _fox_scores(q, k, fq_ref, fr_ref, hh, causal)
                p = jnp.exp(s - lse_ref[:, 64 * hh:64 * hh + 1])
                dp = _dot(jnp.where(_head_mask(hh), dob, 0.0), v, NT)
                ds = p * (dp - _fox_delta(dob, o_ref[...], hh))
                dvs.append(_dot(p.astype(BF16), dob, TN))
                dks.append(_dot(ds.astype(BF16), qb, TN))
                df_acc[hh:hh + 1, :] -= jnp.sum(ds, axis=0, keepdims=True)
            h0 = _head_mask(0)
            dv_acc[...] += jnp.where(h0, dvs[0], dvs[1])
            dk_acc[...] += jnp.where(h0, dks[0], dks[1])

        @pl.when(qi > ki)
        def _():
            step(False)

        @pl.when(qi == ki)
        def _():
            step(True)

        @pl.when(qi == nq - 1)
        def _():
            dk_ref[...] = dk_acc[...]
            dv_ref[...] = dv_acc[...]
            df_ref[...] = df_acc[...]

    q_row = lambda ki, qi: jnp.maximum(qi, ki)

    def qside(base):
        return pl.BlockSpec((T, LANES), lambda j, ki, qi: (q_row(ki, qi), base + j))

    def kside(base):
        return pl.BlockSpec((T, LANES), lambda j, ki, qi: (ki, base + j))

    kblk = pl.BlockSpec((T, LANES), lambda j, ki, qi: (ki, j))
    frow_spec = pl.BlockSpec((None, 2, T), lambda j, ki, qi: (j, 0, ki))
    return pl.pallas_call(
        body, name=name, grid=(4, nq, nq),
        in_specs=[qside(Q_BLK), kside(K_BLK), kside(V_BLK), qside(0), frow_spec, qside(0), qside(0), qside(4)],
        out_specs=[kblk, kblk, frow_spec],
        out_shape=[_sds((L, FOX_W)), _sds((L, FOX_W)), _sds((4, 2, L))],
        scratch_shapes=[pltpu.VMEM((T, LANES), F32), pltpu.VMEM((T, LANES), F32), pltpu.VMEM((2, T), F32)],
        compiler_params=_cp("parallel", "parallel", "arbitrary"),
    )(z, z, z, fq, frow, o, lse, g_m)


def _shift_rows(v, s, down, row):
    n = v.shape[0]
    if down:
        return jnp.where(row >= s, pltpu.roll(v, s, 0), 0.0)
    return jnp.where(row < n - s, pltpu.roll(v, n - s, 0), 0.0)


def _window_sum(v, g, down, row):
    out = jnp.zeros_like(v)
    s = v
    for k in range(4):
        s = s + _shift_rows(s, 1 << k, down, row)
        out = jnp.where(g == k, s, out)
    return out


def _pool_inv_cnt(g, row):
    w = jnp.left_shift(2, g).astype(F32)
    return 1.0 / jnp.minimum(row.astype(F32) + 1.0, w)


def _pool_fwd(z, pool_w, scale, name):
    L = z.shape[0]

    def body(x_ref, w_ref, s_ref, y_ref, p_ref):
        g = pl.program_id(0)
        row = lax.broadcasted_iota(jnp.int32, (L, LANES), 0)
        x = x_ref[...]
        pooled = (_window_sum(x, g, True, row) * _pool_inv_cnt(g, row) - x).astype(BF16)
        p_ref[...] = pooled
        y_ref[...] = (_dot(pooled, w_ref[...].astype(BF16)) * s_ref[...]).astype(BF16)

    col = pl.BlockSpec((L, LANES), lambda g: (0, g))
    return pl.pallas_call(
        body, name=name, grid=(4,),
        in_specs=[col, pl.BlockSpec((None, LANES, LANES), lambda g: (g, 0, 0)), pl.BlockSpec((1, LANES), lambda g: (0, g))],
        out_specs=[col, col],
        out_shape=[_sds((L, 512), BF16), _sds((L, 512), BF16)],
        compiler_params=_cp("parallel"),
    )(z, pool_w, scale)


def _pool_bwd(g_m, pooled, pool_w, scale, name):
    L = g_m.shape[0]

    def body(g_ref, p_ref, w_ref, s_ref, gx_ref, gw_ref, gs_ref):
        g = pl.program_id(0)
        row = lax.broadcasted_iota(jnp.int32, (L, LANES), 0)
        gy = g_ref[...]
        pooled = p_ref[...]
        wb = w_ref[...].astype(BF16)
        lin = _dot(pooled, wb)
        gs_ref[...] = jnp.sum(gy * lin, axis=0, keepdims=True)
        glin = (gy * s_ref[...]).astype(BF16)
        gw_ref[...] = _dot(pooled, glin, TN)
        gp = _dot(glin, wb, NT)
        gx_ref[...] = _window_sum(gp * _pool_inv_cnt(g, row), g, False, row) - gp

    col = pl.BlockSpec((L, LANES), lambda g: (0, g))
    wspec = pl.BlockSpec((None, LANES, LANES), lambda g: (g, 0, 0))
    vec = pl.BlockSpec((1, LANES), lambda g: (0, g))
    return pl.pallas_call(
        body, name=name, grid=(4,),
        in_specs=[col, col, wspec, vec],
        out_specs=[col, wspec, vec],
        out_shape=[_sds((L, 512)), _sds((4, LANES, LANES)), _sds((1, 512))],
        compiler_params=_cp("parallel"),
    )(g_m, pooled, pool_w, scale)


SGU_CHUNKS = 4


def _sgu_ln(v, gam, bet):
    gv = _gelu(v)
    mu = jnp.mean(gv, axis=-1, keepdims=True)
    xc = gv - mu
    rs = lax.rsqrt(jnp.mean(xc * xc, axis=-1, keepdims=True) + EPS)
    xh = xc * rs
    return xh, rs, xh * gam + bet


def _tril_ws(w_ref, g):
    r = lax.broadcasted_iota(jnp.int32, (LANES, LANES), 0)
    c = lax.broadcasted_iota(jnp.int32, (LANES, LANES), 1)
    return jnp.where(r >= c, w_ref[g], 0.0).astype(BF16)


def _sgu_fwd(z, ln_g, ln_b, w_s, b_st, name):
    L = z.shape[0]
    rb = min(SGU_CHUNKS * LANES, L)

    def body(u_ref, v_ref, g_ref, b_ref, w_ref, bs_ref, y_ref):
        _, _, vln = _sgu_ln(v_ref[...], g_ref[...], b_ref[...])
        gu = _gelu(u_ref[...])
        vb = vln.astype(BF16)
        for g in range(4):
            ws = _tril_ws(w_ref, g)
            for n in range(rb // LANES):
                rows = slice(n * LANES, (n + 1) * LANES)
                cols = slice(g * LANES, (g + 1) * LANES)
                mixed = _dot(ws, vb[rows, cols]) + bs_ref[:, g:g + 1]
                y_ref[rows, cols] = (gu[rows, cols] * mixed).astype(BF16)

    vm = lambda shape: pl.BlockSpec(shape, lambda i: tuple(0 for _ in shape))
    return pl.pallas_call(
        body, name=name, grid=(L // rb,),
        in_specs=[pl.BlockSpec((rb, 512), lambda i: (i, 1)), pl.BlockSpec((rb, 512), lambda i: (i, 2)),
                  vm((1, 512)), vm((1, 512)), vm((4, LANES, LANES)), vm((LANES, 4))],
        out_specs=pl.BlockSpec((rb, 512), lambda i: (i, 0)),
        out_shape=_sds((L, 512), BF16),
        compiler_params=_cp("parallel"),
    )(z, z, ln_g, ln_b, w_s, b_st)


def _sgu_bwd(g_m, z, ln_g, ln_b, w_s, b_st, name):
    L = z.shape[0]
    rb = min(SGU_CHUNKS * LANES, L)

    def body(gy_ref, u_ref, v_ref, g_ref, b_ref, w_ref, bs_ref, gu_ref, gv_ref, gw_ref, gbs_ref, gg_ref, gb_ref):
        i = pl.program_id(0)

        @pl.when(i == 0)
        def _():
            gw_ref[...] = jnp.zeros_like(gw_ref)
            gbs_ref[...] = jnp.zeros_like(gbs_ref)
            gg_ref[...] = jnp.zeros_like(gg_ref)
            gb_ref[...] = jnp.zeros_like(gb_ref)

        v = v_ref[...]
        u = u_ref[...]
        gy = gy_ref[...]
        xh, rs, vln = _sgu_ln(v, g_ref[...], b_ref[...])
        gel_u = _gelu(u)
        gmix = gy * gel_u
        vb = vln.astype(BF16)
        gmb = gmix.astype(BF16)
        r = lax.broadcasted_iota(jnp.int32, (LANES, LANES), 0)
        c = lax.broadcasted_iota(jnp.int32, (LANES, LANES), 1)
        gvln_cols = []
        for g in range(4):
            ws = _tril_ws(w_ref, g)
            cols = slice(g * LANES, (g + 1) * LANES)
            gw = jnp.zeros((LANES, LANES), F32)
            gbs = jnp.zeros((LANES, 1), F32)
            parts = []
            for n in range(rb // LANES):
                rows = slice(n * LANES, (n + 1) * LANES)
                mixed = _dot(ws, vb[rows, cols]) + bs_ref[:, g:g + 1]
                gu_ref[rows, cols] = gy[rows, cols] * mixed * _gelu_grad(u[rows, cols])
                parts.append(_dot(ws, gmb[rows, cols], TN))
                gw = gw + _dot(gmb[rows, cols], vb[rows, cols], NT)
                gbs = gbs + jnp.sum(gmix[rows, cols], axis=1, keepdims=True)
            gvln_cols.append(jnp.concatenate(parts, axis=0))
            gw_ref[g] += jnp.where(r >= c, gw, 0.0)
            gbs_ref[:, g:g + 1] += gbs
        gvln = jnp.concatenate(gvln_cols, axis=1)
        gg_ref[...] += jnp.sum(gvln * xh, axis=0, keepdims=True)
        gb_ref[...] += jnp.sum(gvln, axis=0, keepdims=True)
        gxh = gvln * g_ref[...]
        ggv = rs * (gxh - jnp.mean(gxh, axis=-1, keepdims=True) - xh * jnp.mean(gxh * xh, axis=-1, keepdims=True))
        gv_ref[...] = ggv * _gelu_grad(v)

    vm = lambda shape: pl.BlockSpec(shape, lambda i: tuple(0 for _ in shape))
    blk = pl.BlockSpec((rb, 512), lambda i: (i, 0))
    return pl.pallas_call(
        body, name=name, grid=(L // rb,),
        in_specs=[pl.BlockSpec((rb, 512), lambda i: (i, 1)), pl.BlockSpec((rb, 512), lambda i: (i, 1)),
                  pl.BlockSpec((rb, 512), lambda i: (i, 2)),
                  vm((1, 512)), vm((1, 512)), vm((4, LANES, LANES)), vm((LANES, 4))],
        out_specs=[blk, blk, vm((4, LANES, LANES)), vm((LANES, 4)), vm((1, 512)), vm((1, 512))],
        out_shape=[_sds((L, 512)), _sds((L, 512)), _sds((4, LANES, LANES)), _sds((LANES, 4)),
                   _sds((1, 512)), _sds((1, 512))],
        compiler_params=_cp("arbitrary"),
    )(g_m, z, z, ln_g, ln_b, w_s, b_st)


def _adamw_math(w, g, m, v):
    nm = ADAM_B1 * m + (1.0 - ADAM_B1) * g
    nv = ADAM_B2 * v + (1.0 - ADAM_B2) * (g * g)
    m_hat = nm / (1.0 - ADAM_B1 ** ADAM_STEP)
    v_hat = nv / (1.0 - ADAM_B2 ** ADAM_STEP)
    delta = -ADAM_LR * (m_hat / (jnp.sqrt(v_hat) + ADAM_EPS) + ADAM_WD * w)
    return delta, nm, nv


def _sum_adamw(parts, w, m, v, name):
    R, C = w.shape
    rb = 128 if R % 128 == 0 else R

    def body(p_ref, w_ref, m_ref, v_ref, g_ref, d_ref, nm_ref, nv_ref):
        g = p_ref[0]
        for s in range(1, N_DEV):
            g = g + p_ref[s]
        d, nm, nv = _adamw_math(w_ref[...], g, m_ref[...], v_ref[...])
        g_ref[...] = g
        d_ref[...] = d
        nm_ref[...] = nm
        nv_ref[...] = nv

    blk = pl.BlockSpec((rb, C), lambda i: (i, 0))
    return pl.pallas_call(
        body, name=name, grid=(R // rb,),
        in_specs=[pl.BlockSpec((N_DEV, rb, C), lambda i: (0, i, 0)), blk, blk, blk],
        out_specs=[blk] * 4, out_shape=[_sds((R, C))] * 4,
        compiler_params=_cp("parallel"),
    )(parts, w, m, v)


def _sum_pieces(parts, name):
    _, R, C = parts.shape

    def body(p_ref, g_ref):
        g = p_ref[0]
        for s in range(1, N_DEV):
            g = g + p_ref[s]
        g_ref[...] = g

    vm = pl.BlockSpec(memory_space=pltpu.VMEM)
    return pl.pallas_call(body, name=name, in_specs=[vm], out_specs=vm, out_shape=_sds((R, C)))(parts)


def _adamw(w, g, m, v, name):
    vm = pl.BlockSpec(memory_space=pltpu.VMEM)

    def body(w_ref, g_ref, m_ref, v_ref, d_ref, nm_ref, nv_ref):
        d, nm, nv = _adamw_math(w_ref[...], g_ref[...], m_ref[...], v_ref[...])
        d_ref[...] = d
        nm_ref[...] = nm
        nv_ref[...] = nv

    return pl.pallas_call(body, name=name, in_specs=[vm] * 4, out_specs=[vm] * 3,
                          out_shape=[_sds(w.shape)] * 3)(w, g, m, v)


def _mesh_pos():
    return lax.axis_index("x"), lax.axis_index("y"), lax.axis_index("c")


def _dev_index(p):
    return 4 * p[0] + 2 * p[1] + p[2]


def _all_gather(xs, name):
    n = len(xs)

    def body(*refs):
        x_refs, o_refs = refs[:n], refs[n:2 * n]
        send_sems, recv_sems, local_sems = refs[2 * n:]
        x, y, c = _mesh_pos()
        me, sibling = (x, y, c), (x, y, 1 - c)
        chips = [(1 - x, y), (x, 1 - y), (1 - x, 1 - y)]

        def copy(i, k, block, to, src=None):
            dst = o_refs[i].at[_dev_index(block)]
            return pltpu.make_async_remote_copy(
                src_ref=dst if src is None else src, dst_ref=dst,
                send_sem=send_sems.at[i, k], recv_sem=recv_sems.at[i, k], device_id=to, device_id_type=MESH)

        mine = [pltpu.make_async_copy(x_refs[i], o_refs[i].at[_dev_index(me)], local_sems.at[i]) for i in range(n)]
        for cp in mine:
            cp.start()
        first = []
        for i in range(n):
            first.append(copy(i, 0, me, sibling, src=x_refs[i]))
            first += [copy(i, 1 + j, me, (*chip, c), src=x_refs[i]) for j, chip in enumerate(chips)]
        for cp in first:
            cp.start()
        passed = []
        for j, chip in enumerate(chips):
            for i in range(n):
                copy(i, 1 + j, (*chip, c), me).wait_recv()
                fwd = copy(i, 4 + j, (*chip, c), sibling)
                fwd.start()
                passed.append(fwd)
        for i in range(n):
            copy(i, 0, sibling, me).wait_recv()
            for j, chip in enumerate(chips):
                copy(i, 4 + j, (*chip, 1 - c), me).wait_recv()
        for cp in first + passed:
            cp.wait_send()
        for cp in mine:
            cp.wait()

    outs = pl.pallas_call(
        body, name=name,
        in_specs=[ANY] * n, out_specs=[ANY] * n,
        out_shape=[_sds((N_DEV,) + x.shape, x.dtype) for x in xs],
        scratch_shapes=[pltpu.SemaphoreType.DMA((n, 7)), pltpu.SemaphoreType.DMA((n, 7)),
                        pltpu.SemaphoreType.DMA((n,))],
    )(*xs)
    return list(outs)


def _exchange(gs, name):
    n = len(gs)

    def body(*refs):
        g_refs, o_refs = refs[:n], refs[n:2 * n]
        send_sems, recv_sems, local_sems = refs[2 * n:]
        x, y, c = _mesh_pos()
        me = (x, y, c)
        mi = _dev_index(me)
        peers = [(x ^ dx, y ^ dy, c ^ dc) for dx in range(2) for dy in range(2) for dc in range(2)][1:]

        def copy(i, k, peer):
            return pltpu.make_async_remote_copy(
                src_ref=g_refs[i].at[_dev_index(peer)], dst_ref=o_refs[i].at[mi],
                send_sem=send_sems.at[i, k], recv_sem=recv_sems.at[i, k], device_id=peer, device_id_type=MESH)

        mine = [pltpu.make_async_copy(g_refs[i].at[mi], o_refs[i].at[mi], local_sems.at[i]) for i in range(n)]
        for cp in mine:
            cp.start()
        sends = [copy(i, k, peer) for i in range(n) for k, peer in enumerate(peers)]
        for cp in sends:
            cp.start()
        for i in range(n):
            for k, peer in enumerate(peers):
                pltpu.make_async_remote_copy(
                    src_ref=g_refs[i].at[mi], dst_ref=o_refs[i].at[_dev_index(peer)],
                    send_sem=send_sems.at[i, k], recv_sem=recv_sems.at[i, k], device_id=peer,
                    device_id_type=MESH).wait_recv()
        for cp in sends:
            cp.wait_send()
        for cp in mine:
            cp.wait()

    outs = pl.pallas_call(
        body, name=name,
        in_specs=[ANY] * n, out_specs=[ANY] * n,
        out_shape=[_sds(g.shape, g.dtype) for g in gs],
        scratch_shapes=[pltpu.SemaphoreType.DMA((n, 7)), pltpu.SemaphoreType.DMA((n, 7)),
                        pltpu.SemaphoreType.DMA((n,))],
    )(*gs)
    return list(outs)


HBM = pl.BlockSpec(memory_space=pltpu.HBM)
SEM = pl.BlockSpec(memory_space=pltpu.SEMAPHORE)
EFFECT = pltpu.SideEffectType.DATAFLOW_SIDE_EFFECTING


def _peer_list():
    x, y, c = _mesh_pos()
    peers = [(x ^ dx, y ^ dy, c ^ dc) for dx in range(2) for dy in range(2) for dc in range(2)][1:]
    return (x, y, c), peers


def _split_copy(src_ref, land_ref, send_sems, recv_sems, i, k, peer, slot, exchange):
    return pltpu.make_async_remote_copy(
        src_ref=src_ref.at[_dev_index(peer)] if exchange else src_ref, dst_ref=land_ref.at[slot],
        send_sem=send_sems.at[7 * i + k], recv_sem=recv_sems.at[7 * i + k], device_id=peer, device_id_type=MESH)


def _comm_start(groups, name, exchange, dep=None):
    sizes = [len(g) for g in groups]
    n = sum(sizes)
    srcs = [a for g in groups for a in g]
    my_index = _dev_index(_mesh_pos())
    lands = []
    for a in srcs:
        if exchange:
            own = lax.dynamic_slice(a, (my_index, 0, 0), (1,) + a.shape[1:])
            shape = a.shape
        else:
            own = a[None]
            shape = (N_DEV,) + a.shape
        lands.append(lax.dynamic_update_slice(lax.empty(shape, a.dtype), own, (my_index, 0, 0)))

    n_dep = 0 if dep is None else 1

    def body(*refs):
        src_refs, land_refs = refs[:n], refs[n:2 * n]
        sem_refs = refs[2 * n + n_dep:2 * n + n_dep + 2 * len(sizes)]
        token_ref = refs[-1]
        me, peers = _peer_list()
        mi = _dev_index(me)
        i = 0
        for gi, sz in enumerate(sizes):
            for j in range(sz):
                for k, peer in enumerate(peers):
                    _split_copy(src_refs[i], land_refs[i], sem_refs[2 * gi], sem_refs[2 * gi + 1], j, k, peer, mi,
                                exchange).start()
                i += 1
        token_ref[...] = jnp.zeros_like(token_ref)

    sem_shapes = []
    for sz in sizes:
        sem_shapes += [pltpu.SemaphoreType.DMA((7 * sz,)), pltpu.SemaphoreType.DMA((7 * sz,))]
    thru = [pltpu.HBM(a.shape, a.dtype) for a in srcs + lands]
    n_sem = len(sem_shapes)
    outs = pl.pallas_call(
        body, name=name,
        out_shape=tuple(sem_shapes + thru + [_sds((8, LANES))]),
        in_specs=[HBM] * (2 * n) + [ANY] * n_dep,
        out_specs=tuple([SEM] * n_sem + [HBM] * (2 * n) + [pl.BlockSpec(memory_space=pltpu.VMEM)]),
        input_output_aliases={i: n_sem + i for i in range(2 * n)},
        compiler_params=pltpu.CompilerParams(has_side_effects=EFFECT),
    )(*[pltpu.with_memory_space_constraint(a, pltpu.HBM) for a in srcs + lands], *([] if dep is None else [dep]))
    sems, thru_src, thru_land, token = outs[:n_sem], outs[n_sem:n_sem + n], outs[n_sem + n:n_sem + 2 * n], outs[-1]
    result, off = [], 0
    for gi, sz in enumerate(sizes):
        result.append((sems[2 * gi], sems[2 * gi + 1], list(thru_src[off:off + sz]), list(thru_land[off:off + sz])))
        off += sz
    return result, token


def _comm_wait(group, after, name, exchange):
    send_sems, recv_sems, srcs, lands = group
    n = len(srcs)
    after = list(after) if isinstance(after, (list, tuple)) else [after]

    def body(*refs):
        src_refs, land_refs = refs[:n], refs[n:2 * n]
        ssem, rsem = refs[2 * n], refs[2 * n + 1]
        me, peers = _peer_list()
        for i in range(n):
            for k, peer in enumerate(peers):
                cp = _split_copy(src_refs[i], land_refs[i], ssem, rsem, i, k, peer, _dev_index(peer), exchange)
                cp.wait_send()
                cp.wait_recv()

    outs = pl.pallas_call(
        body, name=name,
        out_shape=tuple(pltpu.HBM(a.shape, a.dtype) for a in srcs + lands),
        in_specs=[HBM] * (2 * n) + [SEM, SEM] + [ANY] * len(after),
        out_specs=tuple([HBM] * (2 * n)),
        input_output_aliases={i: i for i in range(2 * n)},
        compiler_params=pltpu.CompilerParams(has_side_effects=EFFECT),
    )(*srcs, *lands, send_sems, recv_sems, *after)
    return list(outs[n:])


def _tie(a, token):
    return a + token[0, 0].astype(a.dtype)


def _pack(arrs, rows):
    flat = jnp.concatenate([a.reshape(-1).astype(F32) for a in arrs])
    return jnp.pad(flat, (0, rows * LANES - flat.shape[0])).reshape(rows, LANES)


def _unpack(packed, shapes):
    flat = packed.reshape(-1)
    out, off = [], 0
    for s in shapes:
        n = math.prod(s)
        out.append(flat[off:off + n].reshape(s))
        off += n
    return out


def _packed_rows(shapes):
    n = sum(math.prod(s) for s in shapes)
    unit = N_DEV * 8 * LANES
    return -(-n // unit) * unit // LANES


def kernel(x, mix_pre_g, mix_post_g, mlp_pre_g, mlp_post_g, w_in_even, s5_lam_re, s5_lam_im, s5_log_dt, s5_b_re, s5_b_im, s5_c_re, s5_c_im, s5_d, s5_w_glu, fox_b_f, w_out_even, w_in_odd, pool_w, pool_scale, sgu_ln_g, sgu_ln_b, sgu_w_s, sgu_b_s, w_out_odd, mlp_w1, mlp_w2, loss_target, m_mix_pre_g, m_mix_post_g, m_mlp_pre_g, m_mlp_post_g, m_w_in_even, m_s5_lam_re, m_s5_lam_im, m_s5_log_dt, m_s5_b_re, m_s5_b_im, m_s5_c_re, m_s5_c_im, m_s5_d, m_s5_w_glu, m_fox_b_f, m_w_out_even, m_w_in_odd, m_pool_w, m_pool_scale, m_sgu_ln_g, m_sgu_ln_b, m_sgu_w_s, m_sgu_b_s, m_w_out_odd, m_mlp_w1, m_mlp_w2, v_mix_pre_g, v_mix_post_g, v_mlp_pre_g, v_mlp_post_g, v_w_in_even, v_s5_lam_re, v_s5_lam_im, v_s5_log_dt, v_s5_b_re, v_s5_b_im, v_s5_c_re, v_s5_c_im, v_s5_d, v_s5_w_glu, v_fox_b_f, v_w_out_even, v_w_in_odd, v_pool_w, v_pool_scale, v_sgu_ln_g, v_sgu_ln_b, v_sgu_w_s, v_sgu_b_s, v_w_out_odd, v_mlp_w1, v_mlp_w2):
    weights = dict(mix_pre_g=mix_pre_g, mix_post_g=mix_post_g, mlp_pre_g=mlp_pre_g, mlp_post_g=mlp_post_g, w_in_even=w_in_even, s5_lam_re=s5_lam_re, s5_lam_im=s5_lam_im, s5_log_dt=s5_log_dt, s5_b_re=s5_b_re, s5_b_im=s5_b_im, s5_c_re=s5_c_re, s5_c_im=s5_c_im, s5_d=s5_d, s5_w_glu=s5_w_glu, fox_b_f=fox_b_f, w_out_even=w_out_even, w_in_odd=w_in_odd, pool_w=pool_w, pool_scale=pool_scale, sgu_ln_g=sgu_ln_g, sgu_ln_b=sgu_ln_b, sgu_w_s=sgu_w_s, sgu_b_s=sgu_b_s, w_out_odd=w_out_odd, mlp_w1=mlp_w1, mlp_w2=mlp_w2)
    mom_m = dict(mix_pre_g=m_mix_pre_g, mix_post_g=m_mix_post_g, mlp_pre_g=m_mlp_pre_g, mlp_post_g=m_mlp_post_g, w_in_even=m_w_in_even, s5_lam_re=m_s5_lam_re, s5_lam_im=m_s5_lam_im, s5_log_dt=m_s5_log_dt, s5_b_re=m_s5_b_re, s5_b_im=m_s5_b_im, s5_c_re=m_s5_c_re, s5_c_im=m_s5_c_im, s5_d=m_s5_d, s5_w_glu=m_s5_w_glu, fox_b_f=m_fox_b_f, w_out_even=m_w_out_even, w_in_odd=m_w_in_odd, pool_w=m_pool_w, pool_scale=m_pool_scale, sgu_ln_g=m_sgu_ln_g, sgu_ln_b=m_sgu_ln_b, sgu_w_s=m_sgu_w_s, sgu_b_s=m_sgu_b_s, w_out_odd=m_w_out_odd, mlp_w1=m_mlp_w1, mlp_w2=m_mlp_w2)
    mom_v = dict(mix_pre_g=v_mix_pre_g, mix_post_g=v_mix_post_g, mlp_pre_g=v_mlp_pre_g, mlp_post_g=v_mlp_post_g, w_in_even=v_w_in_even, s5_lam_re=v_s5_lam_re, s5_lam_im=v_s5_lam_im, s5_log_dt=v_s5_log_dt, s5_b_re=v_s5_b_re, s5_b_im=v_s5_b_im, s5_c_re=v_s5_c_re, s5_c_im=v_s5_c_im, s5_d=v_s5_d, s5_w_glu=v_s5_w_glu, fox_b_f=v_fox_b_f, w_out_even=v_w_out_even, w_in_odd=v_w_in_odd, pool_w=v_pool_w, pool_scale=v_pool_scale, sgu_ln_g=v_sgu_ln_g, sgu_ln_b=v_sgu_ln_b, sgu_w_s=v_sgu_w_s, sgu_b_s=v_sgu_b_s, w_out_odd=v_w_out_odd, mlp_w1=v_mlp_w1, mlp_w2=v_mlp_w2)
    names = list(weights)
    L = x.shape[1]
    x0 = x[0]
    target = loss_target[0]
    my_index = 4 * lax.axis_index("x") + 2 * lax.axis_index("y") + lax.axis_index("c")

    small_vec = jnp.zeros((8, LANES), F32)
    small_vec = small_vec.at[0, :64].set(pool_scale[0]).at[1, :64].set(sgu_ln_g[0]).at[2, :64].set(sgu_ln_b[0])
    ag_groups, ag_token = _comm_start(
        [[jnp.transpose(w_in_even[0]).astype(BF16), small_vec],
         [s5_w_glu[0].astype(BF16), w_out_even[0].astype(BF16)],
         [mlp_w1[0].astype(BF16), mlp_w2[0].astype(BF16)],
         [jnp.transpose(w_in_odd[0]).astype(BF16), w_out_odd[0].astype(BF16), mlp_w1[1].astype(BF16), mlp_w2[1].astype(BF16)]],
        "ag_start", exchange=False)

    lam_r = jnp.concatenate([s5_lam_re.reshape(1, S5_NS), s5_lam_im.reshape(1, S5_NS)], axis=0)
    ldt_r = jnp.repeat(s5_log_dt.reshape(32), 64).reshape(1, S5_NS)
    lam_c = jnp.transpose(lam_r)
    ldt_c = jnp.transpose(ldt_r)
    b_t = jnp.stack([jnp.tile(s5_b_re.reshape(S5_NS, 16), (1, 8)), jnp.tile(s5_b_im.reshape(S5_NS, 16), (1, 8))])
    c_t = jnp.stack([jnp.tile(s5_c_re.reshape(S5_W, 64), (1, 8)), jnp.tile(s5_c_im.reshape(S5_W, 64), (1, 8))])
    bf_pad = jnp.pad(fox_b_f, ((0, 0), (0, LANES - 8)))
    b_st = jnp.transpose(sgu_b_s[0])

    h0, rx0 = _rms_fwd(x0, _tie(mix_pre_g[0:1], ag_token), "rms0")
    tabs, bset, cset = _s5_prep(lam_r, ldt_r, lam_c, ldt_c, b_t, c_t, "s5_prep")
    ag0 = _comm_wait(ag_groups[0], tabs, "ag_wait0", exchange=False)
    winT_e = jnp.pad(ag0[0].reshape(EVEN_IN, D_MODEL), ((0, EVEN_PAD - EVEN_IN), (0, 0)))
    pool_scale_f = ag0[1][:, 0, :64].reshape(1, 512)
    ln_g_f = ag0[1][:, 1, :64].reshape(1, 512)
    ln_b_f = ag0[1][:, 2, :64].reshape(1, 512)
    z0 = _mm(h0, winT_e, name="win_even", tb=True, bm=512, bn=EVEN_PAD)
    bu = _s5_bu(z0, bset, "s5_bu")
    xs = _s5_scan(bu, tabs, "s5_scan")
    ag1 = _comm_wait(ag_groups[1], xs, "ag_wait1", exchange=False)
    wglu = ag1[0].reshape(S5_W, S5_W)
    wout_e = ag1[1].reshape(D_MODEL, D_MODEL)
    ylin, ya = _s5_out_fwd(xs, cset, z0, s5_d, wglu, "s5_out")
    fcum = _fox_f_fwd(z0, bf_pad, "fox_f")
    f8 = fcum[:, :8]
    fq = jnp.repeat(f8, 64, axis=1)
    frow = jnp.transpose(f8).reshape(4, 2, L)
    o_att, lse = _fox_fwd(z0, fq, frow, "fox_fwd")
    mix0 = jnp.concatenate([ya, o_att.astype(BF16)], axis=1)
    y0 = _mm(mix0, wout_e, name="wout_even")
    x1, ry0, h1, rx1 = _post_pre_fwd(x0, y0, mix_post_g[0:1], mlp_pre_g[0:1], "post0")
    ag2 = _comm_wait(ag_groups[2], rx1, "ag_wait2", exchange=False)
    w1 = [ag2[0], None]
    w2 = [ag2[1].reshape(4 * D_MODEL, D_MODEL), None]
    p0, a0 = _mm(h1, w1[0], name="mlp0_w1", b3=True, out_dtypes=(BF16, BF16), epi=_epi_relu2)
    o0 = _mm(a0, w2[0], name="mlp0_w2")
    x2, ro0, h2, rx2 = _post_pre_fwd(x1, o0, mlp_post_g[0:1], mix_pre_g[1:2], "post1")
    ag3 = _comm_wait(ag_groups[3], rx2, "ag_wait3", exchange=False)
    winT_o = ag3[0].reshape(ODD_IN, D_MODEL)
    wout_o = ag3[1].reshape(D_MODEL, D_MODEL)
    w1[1] = ag3[2]
    w2[1] = ag3[3].reshape(4 * D_MODEL, D_MODEL)
    z1 = _mm(h2, winT_o, name="win_odd", tb=True, bn=ODD_IN)
    yc, pooled = _pool_fwd(z1, pool_w[0], pool_scale_f, "pool_fwd")
    yd = _sgu_fwd(z1, ln_g_f, ln_b_f, sgu_w_s[0], b_st, "sgu_fwd")
    mix1 = jnp.concatenate([yc, yd], axis=1)
    y1 = _mm(mix1, wout_o, name="wout_odd")
    x3, ry1, h3, rx3 = _post_pre_fwd(x2, y1, mix_post_g[1:2], mlp_pre_g[1:2], "post2")
    p1, a1 = _mm(h3, w1[1], name="mlp1_w1", b3=True, out_dtypes=(BF16, BF16), epi=_epi_relu2)
    o1 = _mm(a1, w2[1], name="mlp1_w2")
    gx4, ro1, sq = _post_loss_fwd(x3, o1, mlp_post_g[1:2], target, "post3")

    g_o1, gg_mlp_post1 = _post_bwd(gx4, o1, ro1, mlp_post_g[1:2], "bpost3")
    g_p1 = _mm(g_o1, w2[1], name="b_mlp1_a", tb=True, out_dtypes=(BF16,), epi=_epi_relu2_bwd, extra=(p1,))
    gw2_1 = _mm(a1, g_o1, name="b_mlp1_w2", ta=True)
    g_h3 = _mm(g_p1, w1[1], name="b_mlp1_h", tb=True, b3=True)
    gw1_1 = _mm(h3, g_p1, name="b_mlp1_w1", ta=True, out3=True)
    (ex1,), tok1 = _comm_start([[gw1_1, gw2_1.reshape(N_DEV, 512, D_MODEL)]], "ex_start1", exchange=True)
    g_x3, gg_mlp_pre1, g_y1, gg_mix_post1 = _pre_post_bwd(g_h3, x3, rx3, _tie(mlp_pre_g[1:2], tok1), gx4, y1, ry1, mix_post_g[1:2], "bpre3")
    g_mix1 = _mm(g_y1, wout_o, name="b_wout_odd_m", tb=True)
    gwout_o = _mm(mix1, g_y1, name="b_wout_odd_w", ta=True)
    g_xc, g_pool_w, g_pool_scale = _pool_bwd(g_mix1, pooled, pool_w[0], pool_scale_f, "pool_bwd")
    g_u1, g_v1, g_ws, g_bst, g_ln_g, g_ln_b = _sgu_bwd(g_mix1, z1, ln_g_f, ln_b_f, sgu_w_s[0], b_st, "sgu_bwd")
    g_z1 = jnp.concatenate([g_xc, g_u1, g_v1], axis=1).astype(BF16)
    g_h2 = _mm(g_z1, winT_o, name="b_win_odd_h", bk=ODD_IN)
    gwinT_o = _mm(g_z1, h2, name="b_win_odd_w", ta=True, bm=512)
    (ex2,), tok2 = _comm_start([[gwout_o.reshape(N_DEV, 128, D_MODEL), gwinT_o.reshape(N_DEV, ODD_IN // N_DEV, D_MODEL)]], "ex_start2", exchange=True)
    g_x2, gg_mix_pre1, g_o0, gg_mlp_post0 = _pre_post_bwd(g_h2, x2, rx2, _tie(mix_pre_g[1:2], tok2), g_x3, o0, ro0, mlp_post_g[0:1], "bpre2")
    g_p0 = _mm(g_o0, w2[0], name="b_mlp0_a", tb=True, out_dtypes=(BF16,), epi=_epi_relu2_bwd, extra=(p0,))
    gw2_0 = _mm(a0, g_o0, name="b_mlp0_w2", ta=True)
    g_h1 = _mm(g_p0, w1[0], name="b_mlp0_h", tb=True, b3=True)
    gw1_0 = _mm(h1, g_p0, name="b_mlp0_w1", ta=True, out3=True)
    (ex3,), tok3 = _comm_start([[gw1_0, gw2_0.reshape(N_DEV, 512, D_MODEL)]], "ex_start3", exchange=True)
    g_x1, gg_mlp_pre0, g_y0, gg_mix_post0 = _pre_post_bwd(g_h1, x1, rx1, _tie(mlp_pre_g[0:1], tok3), g_x2, y0, ry0, mix_post_g[0:1], "bpre1")
    g_mix0 = _mm(g_y0, wout_e, name="b_wout_even_m", tb=True)
    gwout_e = _mm(mix0, g_y0, name="b_wout_even_w", ta=True)
    gyl, gud, g_wglu, g_d = _s5_glu_bwd(g_mix0, ylin, z0, s5_d, wglu, "s5_glu_bwd")
    (ex4,), tok4 = _comm_start([[gwout_e.reshape(N_DEV, 128, D_MODEL), g_wglu.reshape(N_DEV, 64, S5_W)]], "ex_start4", exchange=True)
    gxd, gc_raw = _s5_c_bwd(gyl, xs, _tie(cset, tok4), "s5_c_bwd")
    gxs, ga = _s5_scan(gxd, tabs, "s5_scan_bwd", reverse=True, xs=xs)
    g_u0, gb_raw = _s5_bu_bwd(gxs, bset, z0, gud, "s5_bu_bwd")
    g_lam, g_ldt, g_b, g_c = _s5_param_bwd(lam_c, ldt_c, b_t, gb_raw, jnp.transpose(ga), gc_raw, "s5_param_bwd")
    dq, dfq = _fox_bwd_dq(z0, fq, frow, o_att, lse, g_mix0, "fox_dq")
    dk, dv, dfrow = _fox_bwd_dkv(z0, fq, frow, o_att, lse, g_mix0, "fox_dkv")
    dFk = jnp.pad(jnp.transpose(dfrow.reshape(8, L)), ((0, 0), (0, LANES - 8)))
    dFq = jnp.pad(dfq[:, ::64], ((0, 0), (0, LANES - 8)))
    dfl, db_f = _fox_f_bwd(dFk, dFq, z0, bf_pad, "fox_f_bwd")
    g_z0 = jnp.concatenate([g_u0, dq, dk, dv, dfl], axis=1).astype(BF16)
    g_h0 = _mm(g_z0, winT_e, name="b_win_even_h", bk=EVEN_PAD)
    grad_x, gg_mix_pre0 = _pre_bwd(g_h0, x0, rx0, mix_pre_g[0:1], g_x1, "bpre0")
    gwinT_e = _mm(g_z0, h0, name="b_win_even_w", ta=True, bm=EVEN_PAD, bk=512)

    small_grads = dict(
        mix_pre_g=jnp.concatenate([gg_mix_pre0, gg_mix_pre1]), mix_post_g=jnp.concatenate([gg_mix_post0, gg_mix_post1]),
        mlp_pre_g=jnp.concatenate([gg_mlp_pre0, gg_mlp_pre1]), mlp_post_g=jnp.concatenate([gg_mlp_post0, gg_mlp_post1]),
        s5_lam_re=g_lam[:, 0], s5_lam_im=g_lam[:, 1], s5_log_dt=g_ldt,
        s5_b_re=g_b[0, :, :16], s5_b_im=g_b[1, :, :16], s5_c_re=g_c[0, :, :64], s5_c_im=g_c[1, :, :64],
        s5_d=g_d, fox_b_f=db_f[:, :8], pool_w=g_pool_w, sgu_w_s=g_ws, sgu_b_s=jnp.transpose(g_bst),
        pool_scale=g_pool_scale, sgu_ln_g=g_ln_g, sgu_ln_b=g_ln_b)
    small_names = list(small_grads)
    full_shapes = [(512,) if nm in ("pool_scale", "sgu_ln_g", "sgu_ln_b") else weights[nm].shape for nm in small_names]
    full_shapes.append((1, 1))
    rows = _packed_rows(full_shapes)
    packed = _pack([small_grads[nm] for nm in small_names] + [sq], rows).reshape(N_DEV, rows // N_DEV, LANES)
    (recv_small,) = _exchange([packed], "exchange_small")
    piece = _sum_pieces(recv_small, "sum_small")
    (small_all,) = _all_gather([piece], "ag_small")
    small_full = _unpack(small_all.reshape(rows, LANES), full_shapes)
    loss = 0.5 * small_full.pop()[0, 0] / D_MODEL

    gwinT_e_pieces = gwinT_e[:EVEN_IN].reshape(N_DEV, EVEN_IN // N_DEV, D_MODEL)
    (ex5,), tok5 = _comm_start([[gwinT_e_pieces]], "ex_start5", exchange=True, dep=small_all)
    r_w1_1, r_w2_1 = _comm_wait(ex1, tok5, "ex_wait1", exchange=True)
    r_wout_o, r_win_o = _comm_wait(ex2, tok5, "ex_wait2", exchange=True)
    r_w1_0, r_w2_0 = _comm_wait(ex3, tok5, "ex_wait3", exchange=True)
    r_wout_e, r_wglu = _comm_wait(ex4, tok5, "ex_wait4", exchange=True)
    small_g = {}
    for nm, g in zip(small_names, small_full):
        if nm in ("pool_scale", "sgu_ln_g", "sgu_ln_b"):
            g = lax.dynamic_slice(g, (my_index * 64,), (64,)).reshape(1, 64)
        small_g[nm] = g
    own_shapes = [weights[nm].shape for nm in small_names]
    rows2 = _packed_rows(own_shapes)
    pw = _pack([weights[nm] for nm in small_names], rows2)
    pg = _pack([small_g[nm] for nm in small_names], rows2)
    pm = _pack([mom_m[nm] for nm in small_names], rows2)
    pv = _pack([mom_v[nm] for nm in small_names], rows2)
    pd, pnm, pnv = _adamw(pw, pg, pm, pv, "adamw_small")
    res = {}
    for nm, d_, m_, v_ in zip(small_names, _unpack(pd, own_shapes), _unpack(pnm, own_shapes), _unpack(pnv, own_shapes)):
        res[nm] = (small_g[nm], d_, m_, v_)

    for nm, parts in (("mlp_w1", (r_w1_0, r_w1_1)), ("mlp_w2", (r_w2_0, r_w2_1))):
        per_layer = [_sum_adamw(parts[l], weights[nm][l], mom_m[nm][l], mom_v[nm][l], "adamw_%s_%d" % (nm, l))
                     for l in range(2)]
        res[nm] = tuple(jnp.stack([per_layer[0][k], per_layer[1][k]]) for k in range(4))
    big_parts = dict(s5_w_glu=r_wglu, w_out_even=r_wout_e, w_out_odd=r_wout_o)
    for nm, parts in big_parts.items():
        outs = _sum_adamw(parts, weights[nm][0], mom_m[nm][0], mom_v[nm][0], "adamw_" + nm)
        res[nm] = tuple(o.reshape(weights[nm].shape) for o in outs)
    done = [res[nm][1] for nm in ("mlp_w1", "mlp_w2", "s5_w_glu", "w_out_even", "w_out_odd")]
    for nm, parts in (("w_in_odd", r_win_o), ("w_in_even", None)):
        if parts is None:
            (parts,) = _comm_wait(ex5, done, "ex_wait5", exchange=True)
        outs = _sum_adamw(parts, jnp.transpose(weights[nm][0]), jnp.transpose(mom_m[nm][0]),
                          jnp.transpose(mom_v[nm][0]), "adamw_" + nm)
        res[nm] = tuple(jnp.transpose(o)[None] for o in outs)
        done.append(res[nm][1])

    grads = [res[nm][0].reshape(weights[nm].shape) for nm in names]
    deltas = [res[nm][1].reshape(weights[nm].shape) for nm in names]
    new_m = [res[nm][2].reshape(weights[nm].shape) for nm in names]
    new_v = [res[nm][3].reshape(weights[nm].shape) for nm in names]
    return (loss, grad_x[None], *grads, *deltas, *new_m, *new_v)
```

```python
import functools
import math

import jax
import jax.numpy as jnp
from jax import lax
from jax.experimental import pallas as pl
from jax.experimental.pallas import tpu as pltpu

F32 = jnp.float32
BF16 = jnp.bfloat16
MESH = pl.DeviceIdType.MESH
ANY = pl.BlockSpec(memory_space=pl.ANY)

N_DEV = 8
D_MODEL = 1024
EPS = 1e-6
S5_W = 512
S5_NS = 2048
SCAN_GROUPS = 4
SCAN_CHUNK = 1024
FOX_W = 512
EVEN_IN = 2056
EVEN_PAD = 2176
ODD_IN = 1536
LANES = 128
PIECE = 4 * D_MODEL // N_DEV
VMEM_LIMIT = 56 * 1024 * 1024

ADAM_LR = 0.001
ADAM_B1 = 0.9
ADAM_B2 = 0.999
ADAM_EPS = 1e-08
ADAM_WD = 0.01
ADAM_STEP = 10

NT = (((1,), (1,)), ((), ()))
TN = (((0,), (0,)), ((), ()))
NN = (((1,), (0,)), ((), ()))


def _cp(*sem):
    return pltpu.CompilerParams(dimension_semantics=sem, vmem_limit_bytes=VMEM_LIMIT)


def _sds(shape, dtype=F32):
    return jax.ShapeDtypeStruct(tuple(shape), dtype)


def _gelu(x):
    t = jnp.tanh(0.7978845608028654 * (x + 0.044715 * x * x * x))
    return 0.5 * x * (1.0 + t)


def _gelu_grad(x):
    t = jnp.tanh(0.7978845608028654 * (x + 0.044715 * x * x * x))
    du = 0.7978845608028654 * (1.0 + 3.0 * 0.044715 * x * x)
    return 0.5 * (1.0 + t) + 0.5 * x * (1.0 - t * t) * du


def _sigmoid(x):
    return 1.0 / (1.0 + jnp.exp(-x))


def _dot(a, b, dn=NN):
    return lax.dot_general(a, b, dn, preferred_element_type=F32)


def _mm(a, b, *, name, ta=False, tb=False, b3=False, out3=False, out_dtypes=(F32,), epi=None, extra=(),
        bm=1024, bn=1024, bk=1024):
    M = a.shape[1] if ta else a.shape[0]
    K = a.shape[0] if ta else a.shape[1]
    pw = b.shape[2] if b3 else PIECE
    if b3:
        N = b.shape[1] if tb else b.shape[0] * pw
        assert (b.shape[0] * pw if tb else b.shape[1]) == K
    else:
        N = b.shape[0] if tb else b.shape[1]
    bm, bn, bk = min(bm, M), min(bn, N), min(bk, K)
    assert M % bm == 0 and N % bn == 0 and K % bk == 0, (name, M, N, K, bm, bn, bk)
    assert not (b3 or out3) or ((bk if tb else bn) % pw == 0 and bn % PIECE == 0)
    nk = K // bk
    n_extra = len(extra)
    n_out = len(out_dtypes)
    dn = (((0 if ta else 1,), (1 if tb else 0,)), ((), ()))

    def body(*refs):
        a_ref, b_ref = refs[0], refs[1]
        e_refs = refs[2:2 + n_extra]
        o_refs = refs[2 + n_extra:2 + n_extra + n_out]
        acc_ref = refs[-1]
        k = pl.program_id(2)

        @pl.when(k == 0)
        def _():
            acc_ref[...] = jnp.zeros_like(acc_ref)

        if not b3:
            acc_ref[...] += lax.dot_general(a_ref[...].astype(BF16), b_ref[...].astype(BF16), dn,
                                            preferred_element_type=F32)
        elif tb:
            for t in range(bk // pw):
                a_t = a_ref[pl.ds(t * pw, pw), :] if ta else a_ref[:, pl.ds(t * pw, pw)]
                acc_ref[...] += lax.dot_general(a_t.astype(BF16), b_ref[t].astype(BF16), dn,
                                                preferred_element_type=F32)
        else:
            a_v = a_ref[...].astype(BF16)
            for t in range(bn // pw):
                acc_ref[:, pl.ds(t * pw, pw)] += lax.dot_general(a_v, b_ref[t].astype(BF16), dn,
                                                                 preferred_element_type=F32)

        @pl.when(k == nk - 1)
        def _():
            acc = acc_ref[...]
            outs = (acc,) if epi is None else epi(acc, *[e[...] for e in e_refs])
            for o_ref, o in zip(o_refs, outs):
                if out3:
                    for t in range(bn // PIECE):
                        o_ref[t] = o[:, t * PIECE:(t + 1) * PIECE].astype(o_ref.dtype)
                else:
                    o_ref[...] = o.astype(o_ref.dtype)

    a_spec = pl.BlockSpec((bk, bm), lambda i, j, k: (k, i)) if ta else pl.BlockSpec((bm, bk), lambda i, j, k: (i, k))
    if b3:
        if tb:
            b_spec = pl.BlockSpec((bk // pw, bn, pw), lambda i, j, k: (k, j, 0))
        else:
            b_spec = pl.BlockSpec((bn // pw, bk, pw), lambda i, j, k: (j, k, 0))
    else:
        b_spec = pl.BlockSpec((bn, bk), lambda i, j, k: (j, k)) if tb else pl.BlockSpec((bk, bn), lambda i, j, k: (k, j))
    e_specs = [pl.BlockSpec((bm, bn), lambda i, j, k: (i, j)) for _ in extra]
    if out3:
        o_specs = [pl.BlockSpec((bn // PIECE, bm, PIECE), lambda i, j, k: (j, i, 0)) for _ in out_dtypes]
        o_shapes = [_sds((N // PIECE, M, PIECE), dt) for dt in out_dtypes]
    else:
        o_specs = [pl.BlockSpec((bm, bn), lambda i, j, k: (i, j)) for _ in out_dtypes]
        o_shapes = [_sds((M, N), dt) for dt in out_dtypes]
    outs = pl.pallas_call(
        body, name=name, grid=(M // bm, N // bn, nk),
        in_specs=[a_spec, b_spec] + e_specs, out_specs=o_specs, out_shape=o_shapes,
        scratch_shapes=[pltpu.VMEM((bm, bn), F32)],
        compiler_params=_cp("parallel", "parallel", "arbitrary"),
    )(a, b, *extra)
    return outs[0] if n_out == 1 else outs


def _epi_relu2(acc):
    r = jnp.maximum(acc, 0.0)
    return acc, r * r


def _epi_relu2_bwd(acc, p):
    return (acc * (2.0 * jnp.maximum(p.astype(F32), 0.0)),)


def _row_spec(rb, w=D_MODEL):
    return pl.BlockSpec((rb, w), lambda i: (i, 0))


def _vec_spec(w=D_MODEL):
    return pl.BlockSpec((1, w), lambda i: (0, 0))


def _rstd(v):
    return lax.rsqrt(jnp.mean(v * v, axis=-1, keepdims=True) + EPS)


def _rms_fwd(x, g, name):
    L = x.shape[0]
    rb = min(256, L)

    def body(x_ref, g_ref, h_ref, r_ref):
        xv = x_ref[...]
        r = _rstd(xv)
        h_ref[...] = (xv * r * g_ref[...]).astype(BF16)
        r_ref[...] = r

    return pl.pallas_call(
        body, name=name, grid=(L // rb,),
        in_specs=[_row_spec(rb), _vec_spec()],
        out_specs=[_row_spec(rb), _row_spec(rb, 1)],
        out_shape=[_sds((L, D_MODEL), BF16), _sds((L, 1))],
        compiler_params=_cp("parallel"),
    )(x, g)


def _post_pre_fwd(x_in, y, g_post, g_pre, name):
    L = x_in.shape[0]
    rb = min(256, L)

    def body(x_ref, y_ref, gp_ref, gn_ref, xo_ref, ry_ref, h_ref, rx_ref):
        yv = y_ref[...]
        ry = _rstd(yv)
        xo = x_ref[...] + yv * ry * gp_ref[...]
        rx = _rstd(xo)
        xo_ref[...] = xo
        ry_ref[...] = ry
        h_ref[...] = (xo * rx * gn_ref[...]).astype(BF16)
        rx_ref[...] = rx

    return pl.pallas_call(
        body, name=name, grid=(L // rb,),
        in_specs=[_row_spec(rb), _row_spec(rb), _vec_spec(), _vec_spec()],
        out_specs=[_row_spec(rb), _row_spec(rb, 1), _row_spec(rb), _row_spec(rb, 1)],
        out_shape=[_sds((L, D_MODEL)), _sds((L, 1)), _sds((L, D_MODEL), BF16), _sds((L, 1))],
        compiler_params=_cp("parallel"),
    )(x_in, y, g_post, g_pre)


def _post_loss_fwd(x_in, y, g_post, target, name):
    L = x_in.shape[0]
    rb = min(256, L)

    def body(x_ref, y_ref, gp_ref, t_ref, gx_ref, ry_ref, loss_ref):
        i = pl.program_id(0)
        yv = y_ref[...]
        ry = _rstd(yv)
        diff = x_ref[...] + yv * ry * gp_ref[...] - t_ref[...]
        gx_ref[...] = diff * (1.0 / D_MODEL)
        ry_ref[...] = ry

        @pl.when(i == 0)
        def _():
            loss_ref[...] = jnp.zeros_like(loss_ref)

        loss_ref[...] += jnp.sum(diff * diff, keepdims=True)

    return pl.pallas_call(
        body, name=name, grid=(L // rb,),
        in_specs=[_row_spec(rb), _row_spec(rb), _vec_spec(), _row_spec(rb)],
        out_specs=[_row_spec(rb), _row_spec(rb, 1), pl.BlockSpec((1, 1), lambda i: (0, 0))],
        out_shape=[_sds((L, D_MODEL)), _sds((L, 1)), _sds((1, 1))],
        compiler_params=_cp("arbitrary"),
    )(x_in, y, g_post, target)


def _rms_bwd_rows(dy, xv, r, g):
    n = xv * r
    dyg = dy * g
    return r * (dyg - n * jnp.mean(dyg * n, axis=-1, keepdims=True)), n


def _post_bwd(g_out, y, ry, g_post, name):
    L = y.shape[0]
    rb = min(256, L)

    def body(go_ref, y_ref, ry_ref, gp_ref, gy_ref, gg_ref):
        i = pl.program_id(0)
        go = go_ref[...]
        gy, n = _rms_bwd_rows(go, y_ref[...], ry_ref[...], gp_ref[...])
        gy_ref[...] = gy.astype(BF16)

        @pl.when(i == 0)
        def _():
            gg_ref[...] = jnp.zeros_like(gg_ref)

        gg_ref[...] += jnp.sum(go * n, axis=0, keepdims=True)

    return pl.pallas_call(
        body, name=name, grid=(L // rb,),
        in_specs=[_row_spec(rb), _row_spec(rb), _row_spec(rb, 1), _vec_spec()],
        out_specs=[_row_spec(rb), _vec_spec()],
        out_shape=[_sds((L, D_MODEL), BF16), _sds((1, D_MODEL))],
        compiler_params=_cp("arbitrary"),
    )(g_out, y, ry, g_post)


def _pre_post_bwd(g_h, x, rx, g_pre, g_out, y_prev, ry_prev, g_post_prev, name):
    L = x.shape[0]
    rb = min(256, L)

    def body(gh_ref, x_ref, rx_ref, gn_ref, go_ref, y_ref, ry_ref, gp_ref, gi_ref, ggn_ref, gy_ref, ggp_ref):
        i = pl.program_id(0)
        gh = gh_ref[...]
        gx, n = _rms_bwd_rows(gh, x_ref[...], rx_ref[...], gn_ref[...])
        gi = go_ref[...] + gx
        gi_ref[...] = gi
        gy, ny = _rms_bwd_rows(gi, y_ref[...], ry_ref[...], gp_ref[...])
        gy_ref[...] = gy.astype(BF16)

        @pl.when(i == 0)
        def _():
            ggn_ref[...] = jnp.zeros_like(ggn_ref)
            ggp_ref[...] = jnp.zeros_like(ggp_ref)

        ggn_ref[...] += jnp.sum(gh * n, axis=0, keepdims=True)
        ggp_ref[...] += jnp.sum(gi * ny, axis=0, keepdims=True)

    return pl.pallas_call(
        body, name=name, grid=(L // rb,),
        in_specs=[_row_spec(rb), _row_spec(rb), _row_spec(rb, 1), _vec_spec(), _row_spec(rb),
                  _row_spec(rb), _row_spec(rb, 1), _vec_spec()],
        out_specs=[_row_spec(rb), _vec_spec(), _row_spec(rb), _vec_spec()],
        out_shape=[_sds((L, D_MODEL)), _sds((1, D_MODEL)), _sds((L, D_MODEL), BF16), _sds((1, D_MODEL))],
        compiler_params=_cp("arbitrary"),
    )(g_h, x, rx, g_pre, g_out, y_prev, ry_prev, g_post_prev)


def _pre_bwd(g_h, x, rx, g_pre, g_out, name):
    L = x.shape[0]
    rb = min(256, L)

    def body(gh_ref, x_ref, rx_ref, gn_ref, go_ref, gi_ref, ggn_ref):
        i = pl.program_id(0)
        gh = gh_ref[...]
        gx, n = _rms_bwd_rows(gh, x_ref[...], rx_ref[...], gn_ref[...])
        gi_ref[...] = go_ref[...] + gx

        @pl.when(i == 0)
        def _():
            ggn_ref[...] = jnp.zeros_like(ggn_ref)

        ggn_ref[...] += jnp.sum(gh * n, axis=0, keepdims=True)

    return pl.pallas_call(
        body, name=name, grid=(L // rb,),
        in_specs=[_row_spec(rb), _row_spec(rb), _row_spec(rb, 1), _vec_spec(), _row_spec(rb)],
        out_specs=[_row_spec(rb), _vec_spec()],
        out_shape=[_sds((L, D_MODEL)), _sds((1, D_MODEL))],
        compiler_params=_cp("arbitrary"),
    )(g_h, x, rx, g_pre, g_out)


def _cmul(ar, ai, br, bi):
    return ar * br - ai * bi, ar * bi + ai * br


def _zoh_cols(lr, li, ldt):
    dt = jnp.exp(ldt)
    mag = jnp.exp(lr * dt)
    ar = mag * jnp.cos(li * dt)
    ai = mag * jnp.sin(li * dt)
    den = lr * lr + li * li
    nr = ar - 1.0
    qr = (nr * lr + ai * li) / den
    qi = (ai * lr - nr * li) / den
    return dt, ar, ai, qr, qi, den


def _b_mask():
    r = lax.broadcasted_iota(jnp.int32, (S5_NS, LANES), 0)
    c = lax.broadcasted_iota(jnp.int32, (S5_NS, LANES), 1)
    return ((r >> 6) & 7) == (c >> 4)


def _c_mask():
    r = lax.broadcasted_iota(jnp.int32, (S5_W, 512), 0)
    c = lax.broadcasted_iota(jnp.int32, (S5_W, 512), 1)
    return ((r >> 4) & 7) == (c >> 6)


def _s5_prep(lam_r, ldt_r, lam_c, ldt_c, b_t, c_t, name):
    def body(lam_r_ref, ldt_r_ref, lam_c_ref, ldt_c_ref, b_ref, c_ref, tab_ref, bset_ref, cset_ref):
        lr, li = lam_r_ref[0:1, :], lam_r_ref[1:2, :]
        dt = jnp.exp(ldt_r_ref[...])
        mag = jnp.exp(lr * dt)
        p1r, p1i = mag * jnp.cos(li * dt), mag * jnp.sin(li * dt)
        p2r, p2i = _cmul(p1r, p1i, p1r, p1i)
        p3r, p3i = _cmul(p2r, p2i, p1r, p1i)
        p4r, p4i = _cmul(p2r, p2i, p2r, p2i)
        p5r, p5i = _cmul(p4r, p4i, p1r, p1i)
        p6r, p6i = _cmul(p4r, p4i, p2r, p2i)
        p7r, p7i = _cmul(p4r, p4i, p3r, p3i)
        p8r, p8i = _cmul(p4r, p4i, p4r, p4i)
        pw_r = [p1r, p2r, p3r, p4r, p5r, p6r, p7r, p8r]
        pw_i = [p1i, p2i, p3i, p4i, p5i, p6i, p7i, p8i]
        row = lax.broadcasted_iota(jnp.int32, (8, S5_NS), 0)
        zero = jnp.zeros((8, S5_NS), F32)

        def bc(v):
            return jnp.broadcast_to(v, (8, S5_NS))

        for d in range(2):
            sgn = 1.0 if d == 0 else -1.0
            for t, s in enumerate((1, 2, 4)):
                live = (row >= s) if d == 0 else (row <= 7 - s)
                tab_ref[d, 2 * t] = jnp.where(live, bc(pw_r[s - 1]), zero)
                tab_ref[d, 2 * t + 1] = jnp.where(live, bc(sgn * pw_i[s - 1]), zero)
            cr, ci = zero, zero
            for i in range(8):
                e = i if d == 0 else 7 - i
                cr = jnp.where(row == i, bc(pw_r[e]), cr)
                ci = jnp.where(row == i, bc(sgn * pw_i[e]), ci)
            tab_ref[d, 6] = cr
            tab_ref[d, 7] = ci

        _, _, _, qr, qi, _ = _zoh_cols(lam_c_ref[:, 0:1], lam_c_ref[:, 1:2], ldt_c_ref[...])
        bm = _b_mask()
        br, bi = b_ref[0], b_ref[1]
        bset_ref[0] = jnp.where(bm, qr * br - qi * bi, 0.0).astype(BF16)
        bset_ref[1] = jnp.where(bm, qr * bi + qi * br, 0.0).astype(BF16)
        cm = _c_mask()
        cset_ref[0] = jnp.where(cm, c_ref[0], 0.0).astype(BF16)
        cset_ref[1] = jnp.where(cm, c_ref[1], 0.0).astype(BF16)

    vm = pl.BlockSpec(memory_space=pltpu.VMEM)
    return pl.pallas_call(
        body, name=name, in_specs=[vm] * 6, out_specs=[vm] * 3,
        out_shape=[_sds((2, 8, 8, S5_NS)), _sds((2, S5_NS, LANES), BF16), _sds((2, S5_W, 512), BF16)],
        compiler_params=pltpu.CompilerParams(vmem_limit_bytes=VMEM_LIMIT),
    )(lam_r, ldt_r, lam_c, ldt_c, b_t, c_t)


def _s5_bu(z, bset, name):
    L = z.shape[0]
    bl = min(512, L)

    def body(u_ref, b_ref, o_ref):
        u = u_ref[...].astype(BF16)
        o_ref[0] = _dot(u, b_ref[0], NT)
        o_ref[1] = _dot(u, b_ref[1], NT)

    return pl.pallas_call(
        body, name=name, grid=(L // bl, 4),
        in_specs=[pl.BlockSpec((bl, LANES), lambda i, j: (i, j)),
                  pl.BlockSpec((2, 512, LANES), lambda i, j: (0, j, 0))],
        out_specs=pl.BlockSpec((2, bl, 512), lambda i, j: (0, i, j)),
        out_shape=_sds((2, L, S5_NS)),
        compiler_params=_cp("parallel", "parallel"),
    )(z, bset)


def _s5_scan(bu, tabs, name, reverse=False, xs=None):
    L = bu.shape[1]
    tl = min(SCAN_CHUNK, L)
    nc = L // tl
    nb = tl // 8
    W = SCAN_GROUPS * LANES
    d = 1 if reverse else 0
    with_ga = xs is not None
    assert not with_ga or reverse

    def body(*refs):
        if with_ga:
            bu_ref, tab_ref, xs_ref, x_ref, ga_ref, carry_ref, acc_ref = refs
        else:
            bu_ref, tab_ref, x_ref, carry_ref = refs
        c = pl.program_id(1)
        row = lax.broadcasted_iota(jnp.int32, (8, LANES), 0)

        @pl.when(c == 0)
        def _():
            carry_ref[...] = jnp.zeros_like(carry_ref)
            if with_ga:
                acc_ref[...] = jnp.zeros_like(acc_ref)

        def step(i, carry):
            b = (nb - 1 - i) if reverse else i
            off = pl.multiple_of(b * 8, 8)
            out = []
            for g in range(SCAN_GROUPS):
                lanes = pl.ds(g * LANES, LANES)
                cr, ci = carry[2 * g], carry[2 * g + 1]
                yr = bu_ref[0, pl.ds(off, 8), lanes]
                yi = bu_ref[1, pl.ds(off, 8), lanes]
                for t, s in enumerate((1, 2, 4)):
                    sh = (8 - s) if reverse else s
                    sr = pltpu.roll(yr, sh, 0)
                    si = pltpu.roll(yi, sh, 0)
                    mr, mi = tab_ref[2 * t, :, lanes], tab_ref[2 * t + 1, :, lanes]
                    yr, yi = yr + mr * sr - mi * si, yi + mr * si + mi * sr
                pr, pi = tab_ref[6, :, lanes], tab_ref[7, :, lanes]
                yr, yi = yr + pr * cr - pi * ci, yi + pr * ci + pi * cr
                x_ref[0, pl.ds(off, 8), lanes] = yr
                x_ref[1, pl.ds(off, 8), lanes] = yi
                if with_ga:
                    nr = jnp.where(row == 7, cr, pltpu.roll(yr, 7, 0))
                    ni = jnp.where(row == 7, ci, pltpu.roll(yi, 7, 0))
                    xr = xs_ref[0, pl.ds(off, 8), lanes]
                    xi = xs_ref[1, pl.ds(off, 8), lanes]
                    acc_ref[0, :, lanes] += xr * nr + xi * ni
                    acc_ref[1, :, lanes] += xr * ni - xi * nr
                last = 0 if reverse else 7
                out += [jnp.broadcast_to(yr[last:last + 1, :], (8, LANES)),
                        jnp.broadcast_to(yi[last:last + 1, :], (8, LANES))]
            return tuple(out)

        init = []
        for g in range(SCAN_GROUPS):
            init += [carry_ref[0, :, pl.ds(g * LANES, LANES)], carry_ref[1, :, pl.ds(g * LANES, LANES)]]
        fin = lax.fori_loop(0, nb, step, tuple(init))
        for g in range(SCAN_GROUPS):
            carry_ref[0, :, pl.ds(g * LANES, LANES)] = fin[2 * g]
            carry_ref[1, :, pl.ds(g * LANES, LANES)] = fin[2 * g + 1]
        if with_ga:
            @pl.when(c == nc - 1)
            def _():
                ga_ref[0:1, :] = jnp.sum(acc_ref[0], axis=0, keepdims=True)
                ga_ref[1:2, :] = jnp.sum(acc_ref[1], axis=0, keepdims=True)

    chunk = (lambda g, c: (0, nc - 1 - c, g)) if reverse else (lambda g, c: (0, c, g))
    seq = pl.BlockSpec((2, tl, W), chunk)
    in_specs = [seq, pl.BlockSpec((None, 8, 8, W), lambda g, c: (d, 0, 0, g))]
    out_specs = [seq]
    out_shape = [_sds((2, L, S5_NS))]
    scratch = [pltpu.VMEM((2, 8, W), F32)]
    args = [bu, tabs]
    if with_ga:
        in_specs.append(seq)
        args.append(xs)
        out_specs.append(pl.BlockSpec((2, W), lambda g, c: (0, g)))
        out_shape.append(_sds((2, S5_NS)))
        scratch.append(pltpu.VMEM((2, 8, W), F32))
    outs = pl.pallas_call(
        body, name=name, grid=(S5_NS // W, nc), in_specs=in_specs, out_specs=out_specs, out_shape=out_shape,
        scratch_shapes=scratch, compiler_params=_cp("parallel", "arbitrary"),
    )(*args)
    return outs if with_ga else outs[0]


def _s5_out_fwd(xs, cset, z, dvec, wglu, name):
    L = z.shape[0]
    bl = min(256, L)

    def body(x_ref, c_ref, u_ref, d_ref, w_ref, ylin_ref, ya_ref):
        cols = []
        for j in range(4):
            xr = x_ref[0, :, 512 * j:512 * (j + 1)].astype(BF16)
            xi = x_ref[1, :, 512 * j:512 * (j + 1)].astype(BF16)
            cr = c_ref[0, LANES * j:LANES * (j + 1), :]
            ci = c_ref[1, LANES * j:LANES * (j + 1), :]
            cols.append(_dot(xr, cr, NT) - _dot(xi, ci, NT))
        ylin = jnp.concatenate(cols, axis=1) + d_ref[...] * u_ref[...]
        yg = _gelu(ylin)
        t = _dot(yg.astype(BF16), w_ref[...])
        ylin_ref[...] = ylin
        ya_ref[...] = (yg * _sigmoid(t)).astype(BF16)

    return pl.pallas_call(
        body, name=name, grid=(L // bl,),
        in_specs=[pl.BlockSpec((2, bl, S5_NS), lambda i: (0, i, 0)),
                  pl.BlockSpec((2, S5_W, 512), lambda i: (0, 0, 0)),
                  pl.BlockSpec((bl, S5_W), lambda i: (i, 0)),
                  pl.BlockSpec((1, S5_W), lambda i: (0, 0)),
                  pl.BlockSpec((S5_W, S5_W), lambda i: (0, 0))],
        out_specs=[pl.BlockSpec((bl, S5_W), lambda i: (i, 0))] * 2,
        out_shape=[_sds((L, S5_W)), _sds((L, S5_W), BF16)],
        compiler_params=_cp("parallel"),
    )(xs, cset, z, dvec, wglu)


def _s5_glu_bwd(g_m, ylin, z, dvec, wglu, name):
    L = z.shape[0]
    bl = min(256, L)

    def body(g_ref, ylin_ref, u_ref, d_ref, w_ref, gyl_ref, gud_ref, gw_ref, gd_ref):
        i = pl.program_id(0)
        ylin = ylin_ref[...]
        yg = _gelu(ylin)
        ygb = yg.astype(BF16)
        sg = _sigmoid(_dot(ygb, w_ref[...]))
        gya = g_ref[...]
        gt = gya * yg * sg * (1.0 - sg)
        gtb = gt.astype(BF16)
        gyg = gya * sg + _dot(gtb, w_ref[...], NT)
        gyl = gyg * _gelu_grad(ylin)
        gyl_ref[...] = gyl
        gud_ref[...] = gyl * d_ref[...]

        @pl.when(i == 0)
        def _():
            gw_ref[...] = jnp.zeros_like(gw_ref)
            gd_ref[...] = jnp.zeros_like(gd_ref)

        gw_ref[...] += _dot(ygb, gtb, TN)
        gd_ref[...] += jnp.sum(gyl * u_ref[...], axis=0, keepdims=True)

    blk = pl.BlockSpec((bl, S5_W), lambda i: (i, 0))
    return pl.pallas_call(
        body, name=name, grid=(L // bl,),
        in_specs=[blk, blk, blk, pl.BlockSpec((1, S5_W), lambda i: (0, 0)),
                  pl.BlockSpec((S5_W, S5_W), lambda i: (0, 0))],
        out_specs=[blk, blk, pl.BlockSpec((S5_W, S5_W), lambda i: (0, 0)), pl.BlockSpec((1, S5_W), lambda i: (0, 0))],
        out_shape=[_sds((L, S5_W)), _sds((L, S5_W)), _sds((S5_W, S5_W)), _sds((1, S5_W))],
        compiler_params=_cp("arbitrary"),
    )(g_m, ylin, z, dvec, wglu)


def _s5_c_bwd(gyl, xs, cset, name):
    L = gyl.shape[0]
    bl = min(256, L)

    def body(g_ref, x_ref, c_ref, gx_ref, gc_ref):
        i = pl.program_id(0)

        @pl.when(i == 0)
        def _():
            gc_ref[...] = jnp.zeros_like(gc_ref)

        for j in range(4):
            gj = g_ref[:, LANES * j:LANES * (j + 1)].astype(BF16)
            cr = c_ref[0, LANES * j:LANES * (j + 1), :]
            ci = c_ref[1, LANES * j:LANES * (j + 1), :]
            gx_ref[0, :, 512 * j:512 * (j + 1)] = _dot(gj, cr)
            gx_ref[1, :, 512 * j:512 * (j + 1)] = -_dot(gj, ci)
            xr = x_ref[0, :, 512 * j:512 * (j + 1)].astype(BF16)
            xi = x_ref[1, :, 512 * j:512 * (j + 1)].astype(BF16)
            gc_ref[0, LANES * j:LANES * (j + 1), :] += _dot(gj, xr, TN)
            gc_ref[1, LANES * j:LANES * (j + 1), :] -= _dot(gj, xi, TN)

    return pl.pallas_call(
        body, name=name, grid=(L // bl,),
        in_specs=[pl.BlockSpec((bl, S5_W), lambda i: (i, 0)),
                  pl.BlockSpec((2, bl, S5_NS), lambda i: (0, i, 0)),
                  pl.BlockSpec((2, S5_W, 512), lambda i: (0, 0, 0))],
        out_specs=[pl.BlockSpec((2, bl, S5_NS), lambda i: (0, i, 0)),
                   pl.BlockSpec((2, S5_W, 512), lambda i: (0, 0, 0))],
        out_shape=[_sds((2, L, S5_NS)), _sds((2, S5_W, 512))],
        compiler_params=_cp("arbitrary"),
    )(gyl, xs, cset)


def _s5_bu_bwd(gx, bset, z, gud, name):
    L = z.shape[0]
    bl = min(512, L)

    def body(gx_ref, b_ref, u_ref, gud_ref, gu_ref, gb_ref):
        i = pl.program_id(1)
        gr = gx_ref[0].astype(BF16)
        gi = gx_ref[1].astype(BF16)
        gu_ref[...] = gud_ref[...] + _dot(gr, b_ref[0]) + _dot(gi, b_ref[1])

        @pl.when(i == 0)
        def _():
            gb_ref[...] = jnp.zeros_like(gb_ref)

        u = u_ref[...].astype(BF16)
        gb_ref[0] += _dot(gr, u, TN)
        gb_ref[1] += _dot(gi, u, TN)

    return pl.pallas_call(
        body, name=name, grid=(4, L // bl),
        in_specs=[pl.BlockSpec((2, bl, 512), lambda j, i: (0, i, j)),
                  pl.BlockSpec((2, 512, LANES), lambda j, i: (0, j, 0)),
                  pl.BlockSpec((bl, LANES), lambda j, i: (i, j)),
                  pl.BlockSpec((bl, LANES), lambda j, i: (i, j))],
        out_specs=[pl.BlockSpec((bl, LANES), lambda j, i: (i, j)),
                   pl.BlockSpec((2, 512, LANES), lambda j, i: (0, j, 0))],
        out_shape=[_sds((L, S5_W)), _sds((2, S5_NS, LANES))],
        compiler_params=_cp("parallel", "arbitrary"),
    )(gx, bset, z, gud)


def _s5_param_bwd(lam_c, ldt_c, b_t, gb, ga_c, gc, name):
    def body(lam_ref, ldt_ref, b_ref, gb_ref, ga_ref, gc_ref, glam_ref, gldt_ref, gbo_ref, gco_ref):
        lr, li = lam_ref[:, 0:1], lam_ref[:, 1:2]
        dt, ar, ai, qr, qi, den = _zoh_cols(lr, li, ldt_ref[...])
        bm = _b_mask()
        gbr = jnp.where(bm, gb_ref[0], 0.0)
        gbi = jnp.where(bm, gb_ref[1], 0.0)
        br, bi = b_ref[0], b_ref[1]
        obr = gbr * qr + gbi * qi
        obi = gbi * qr - gbr * qi
        gqr = jnp.sum(gbr * br + gbi * bi, axis=1, keepdims=True)
        gqi = jnp.sum(gbi * br - gbr * bi, axis=1, keepdims=True)
        for s in (64, 32, 16):
            obr = obr + pltpu.roll(obr, s, 1)
            obi = obi + pltpu.roll(obi, s, 1)
        gbo_ref[0] = obr
        gbo_ref[1] = obi
        gar = ga_ref[:, 0:1] + (gqr * lr - gqi * li) / den
        gai = ga_ref[:, 1:2] + (gqr * li + gqi * lr) / den
        qlr = (qr * lr + qi * li) / den
        qli = (qi * lr - qr * li) / den
        glr = -(gqr * qlr + gqi * qli)
        gli = -(gqi * qlr - gqr * qli)
        glr = glr + dt * (gar * ar + gai * ai)
        gli = gli + dt * (gai * ar - gar * ai)
        wr, wi = _cmul(lr, li, ar, ai)
        gldt = (gar * wr + gai * wi) * dt
        glam_ref[:, 0:1] = glr
        glam_ref[:, 1:2] = gli
        r = lax.broadcasted_iota(jnp.int32, (S5_NS, 32), 0)
        c = lax.broadcasted_iota(jnp.int32, (S5_NS, 32), 1)
        gldt_ref[...] = jnp.sum(jnp.where((r >> 6) == c, gldt, 0.0), axis=0, keepdims=True)
        cm = _c_mask()
        for k in range(2):
            oc = jnp.where(cm, gc_ref[k], 0.0)
            for s in (256, 128, 64):
                oc = oc + pltpu.roll(oc, s, 1)
            gco_ref[k] = oc[:, 0:LANES]

    vm = pl.BlockSpec(memory_space=pltpu.VMEM)
    return pl.pallas_call(
        body, name=name, in_specs=[vm] * 6, out_specs=[vm] * 4,
        out_shape=[_sds((S5_NS, 2)), _sds((1, 32)), _sds((2, S5_NS, LANES)), _sds((2, S5_W, LANES))],
        compiler_params=pltpu.CompilerParams(vmem_limit_bytes=VMEM_LIMIT),
    )(lam_c, ldt_c, b_t, gb, ga_c, gc)


FL_BLK = EVEN_PAD // LANES - 1
Q_BLK, K_BLK, V_BLK = 4, 8, 12
NEG = -1e30


def _log_sigmoid(v):
    return jnp.minimum(v, 0.0) - jnp.log(1.0 + jnp.exp(-jnp.abs(v)))


def _fox_f_fwd(z, bf, name):
    L = z.shape[0]
    tl = min(256, L)

    def body(fl_ref, b_ref, f_ref, fq_ref, carry_ref):
        i = pl.program_id(0)

        @pl.when(i == 0)
        def _():
            carry_ref[...] = jnp.zeros_like(carry_ref)

        lf = _log_sigmoid(fl_ref[...] + b_ref[...])
        r = lax.broadcasted_iota(jnp.int32, (tl, tl), 0)
        c = lax.broadcasted_iota(jnp.int32, (tl, tl), 1)
        tri = (r >= c).astype(F32)
        cs = lax.dot_general(tri, lf, NN, precision=lax.Precision.HIGHEST, preferred_element_type=F32) + carry_ref[...]
        f_ref[...] = cs
        carry_ref[...] = cs[tl - 1:tl, :]
        expand = (lax.broadcasted_iota(jnp.int32, (LANES, FOX_W), 0)
                  == (lax.broadcasted_iota(jnp.int32, (LANES, FOX_W), 1) >> 6)).astype(F32)
        fq_ref[...] = lax.dot_general(cs, expand, NN, precision=lax.Precision.HIGHEST, preferred_element_type=F32)

    return pl.pallas_call(
        body, name=name, grid=(L // tl,),
        in_specs=[pl.BlockSpec((tl, LANES), lambda i: (i, FL_BLK)), pl.BlockSpec((1, LANES), lambda i: (0, 0))],
        out_specs=[pl.BlockSpec((tl, LANES), lambda i: (i, 0)), pl.BlockSpec((tl, FOX_W), lambda i: (i, 0))],
        out_shape=[_sds((L, LANES)), _sds((L, FOX_W))],
        scratch_shapes=[pltpu.VMEM((1, LANES), F32)],
        compiler_params=_cp("arbitrary"),
    )(z, bf)


def _fox_f_bwd(dFk, dfq, z, bf, name):
    L = z.shape[0]
    tl = min(256, L)
    nb = L // tl

    def body(dfk_ref, dfq_ref, fl_ref, b_ref, dfl_ref, db_ref, carry_ref):
        i = pl.program_id(0)

        @pl.when(i == 0)
        def _():
            carry_ref[...] = jnp.zeros_like(carry_ref)
            db_ref[...] = jnp.zeros_like(db_ref)

        sel = (lax.broadcasted_iota(jnp.int32, (FOX_W, LANES), 0)
               == 64 * lax.broadcasted_iota(jnp.int32, (FOX_W, LANES), 1)).astype(F32)
        dfq_h = lax.dot_general(dfq_ref[...], sel, NN, precision=lax.Precision.HIGHEST, preferred_element_type=F32)
        r = lax.broadcasted_iota(jnp.int32, (tl, tl), 0)
        c = lax.broadcasted_iota(jnp.int32, (tl, tl), 1)
        tri = (r <= c).astype(F32)
        cs = lax.dot_general(tri, dfk_ref[...] + dfq_h, NN, precision=lax.Precision.HIGHEST,
                             preferred_element_type=F32) + carry_ref[...]
        carry_ref[...] = cs[0:1, :]
        dfl = cs * _sigmoid(-(fl_ref[...] + b_ref[...]))
        dfl_ref[...] = dfl
        db_ref[...] += jnp.sum(dfl, axis=0, keepdims=True)

    return pl.pallas_call(
        body, name=name, grid=(nb,),
        in_specs=[pl.BlockSpec((tl, LANES), lambda i: (nb - 1 - i, 0)),
                  pl.BlockSpec((tl, FOX_W), lambda i: (nb - 1 - i, 0)),
                  pl.BlockSpec((tl, LANES), lambda i: (nb - 1 - i, FL_BLK)),
                  pl.BlockSpec((1, LANES), lambda i: (0, 0))],
        out_specs=[pl.BlockSpec((tl, LANES), lambda i: (nb - 1 - i, 0)), pl.BlockSpec((1, LANES), lambda i: (0, 0))],
        out_shape=[_sds((L, LANES)), _sds((1, LANES))],
        scratch_shapes=[pltpu.VMEM((1, LANES), F32)],
        compiler_params=_cp("arbitrary"),
    )(dFk, dfq, z, bf)


def _head_mask(hh):
    lane = lax.broadcasted_iota(jnp.int32, (1, LANES), 1)
    return (lane >> 6) == hh


FOX_T = 512


def _fox_head(x, hh):
    return jnp.where(_head_mask(hh), x, 0.0).astype(BF16)


def _fox_scores(q_scaled, k, fq_ref, fr_ref, hh, causal):
    s = _dot(_fox_head(q_scaled, hh), k, NT) + (fq_ref[:, 64 * hh:64 * hh + 1] - fr_ref[hh:hh + 1, :])
    return s if causal is None else jnp.where(causal, s, NEG)


def _causal(T):
    return lax.broadcasted_iota(jnp.int32, (T, T), 1) <= lax.broadcasted_iota(jnp.int32, (T, T), 0)


def _fox_fwd(z, fq, frow, name):
    L = z.shape[0]
    T = min(FOX_T, L)
    nq = L // T

    def body(q_ref, k_ref, v_ref, fq_ref, fr_ref, o_ref, lse_ref, m_ref, l_ref, acc_ref):
        qi, ki = pl.program_id(1), pl.program_id(2)

        @pl.when(ki == 0)
        def _():
            m_ref[...] = jnp.full_like(m_ref, NEG)
            l_ref[...] = jnp.zeros_like(l_ref)
            acc_ref[...] = jnp.zeros_like(acc_ref)

        def step(diagonal):
            q = q_ref[...] * 0.125
            k = k_ref[...].astype(BF16)
            v = v_ref[...].astype(BF16)
            causal = _causal(T) if diagonal else None
            alphas, pvs = [], []
            for hh in range(2):
                s = _fox_scores(q, k, fq_ref, fr_ref, hh, causal)
                m_old = m_ref[hh]
                m_new = jnp.maximum(m_old, jnp.max(s, axis=1, keepdims=True))
                alpha = jnp.exp(m_old - m_new)
                p = jnp.exp(s - m_new)
                l_ref[hh] = alpha * l_ref[hh] + jnp.sum(p, axis=1, keepdims=True)
                m_ref[hh] = m_new
                alphas.append(alpha)
                pvs.append(_dot(p.astype(BF16), v))
            h0 = _head_mask(0)
            acc_ref[...] = jnp.where(h0, alphas[0], alphas[1]) * acc_ref[...] + jnp.where(h0, pvs[0], pvs[1])

        @pl.when(ki < qi)
        def _():
            step(False)

        @pl.when(ki == qi)
        def _():
            step(True)

        @pl.when(ki == nq - 1)
        def _():
            h0 = _head_mask(0)
            inv = jnp.where(h0, 1.0 / l_ref[0], 1.0 / l_ref[1])
            o_ref[...] = acc_ref[...] * inv
            lse_ref[...] = jnp.where(h0, m_ref[0] + jnp.log(l_ref[0]), m_ref[1] + jnp.log(l_ref[1]))

    def zspec(base, rowf):
        return pl.BlockSpec((T, LANES), lambda j, qi, ki: (rowf(qi, ki), base + j))

    kv_row = lambda qi, ki: jnp.minimum(ki, qi)
    q_row = lambda qi, ki: qi
    return pl.pallas_call(
        body, name=name, grid=(4, nq, nq),
        in_specs=[zspec(Q_BLK, q_row), zspec(K_BLK, kv_row), zspec(V_BLK, kv_row),
                  pl.BlockSpec((T, LANES), lambda j, qi, ki: (qi, j)),
                  pl.BlockSpec((None, 2, T), lambda j, qi, ki: (j, 0, jnp.minimum(ki, qi)))],
        out_specs=[pl.BlockSpec((T, LANES), lambda j, qi, ki: (qi, j))] * 2,
        out_shape=[_sds((L, FOX_W)), _sds((L, FOX_W))],
        scratch_shapes=[pltpu.VMEM((2, T, 1), F32), pltpu.VMEM((2, T, 1), F32), pltpu.VMEM((T, LANES), F32)],
        compiler_params=_cp("parallel", "parallel", "arbitrary"),
    )(z, z, z, fq, frow)


def _fox_bwd(z, fq, frow, o, lse, g_m, name):
    L = z.shape[0]
    T = min(FOX_T, L)
    nq = L // T

    def body(q_ref, k_ref, v_ref, fq_ref, fr_ref, o_ref, lse_ref, do_ref,
             dq_ref, dk_ref, dv_ref, dfq_ref, dfk_ref, dk_acc, dv_acc, df_acc):
        ki, qi = pl.program_id(1), pl.program_id(2)

        @pl.when((ki == 0) & (qi == 0))
        def _():
            dq_ref[...] = jnp.zeros_like(dq_ref)
            dfq_ref[...] = jnp.zeros_like(dfq_ref)

        @pl.when(qi == 0)
        def _():
            dk_acc[...] = jnp.zeros_like(dk_acc)
            dv_acc[...] = jnp.zeros_like(dv_acc)
            df_acc[...] = jnp.zeros_like(df_acc)

        def step(diagonal):
            q = q_ref[...] * 0.125
            qb = q.astype(BF16)
            k = k_ref[...].astype(BF16)
            v = v_ref[...].astype(BF16)
            do = do_ref[...]
            dob = do.astype(BF16)
            do_o = dob.astype(F32) * o_ref[...]
            causal = _causal(T) if diagonal else None
            dvs, dks, dqs, rss = [], [], [], []
            for hh in range(2):
                s = _fox_scores(q, k, fq_ref, fr_ref, hh, causal)
                p = jnp.exp(s - lse_ref[:, 64 * hh:64 * hh + 1])
                dp = _dot(_fox_head(do, hh), v, NT)
                delta = jnp.sum(jnp.where(_head_mask(hh), do_o, 0.0), axis=1, keepdims=True)
                ds = p * (dp - delta)
                dsb = ds.astype(BF16)
                dvs.append(_dot(p.astype(BF16), dob, TN))
                dks.append(_dot(dsb, qb, TN))
                dqs.append(_dot(dsb, k))
                rss.append(jnp.sum(ds, axis=1, keepdims=True))
                df_acc[hh:hh + 1, :] -= jnp.sum(ds, axis=0, keepdims=True)
            h0 = _head_mask(0)
            dv_acc[...] += jnp.where(h0, dvs[0], dvs[1])
            dk_acc[...] += jnp.where(h0, dks[0], dks[1])
            rows = pl.ds(pl.multiple_of(qi * T, T), T)
            dq_ref[rows, :] += jnp.where(h0, dqs[0], dqs[1])
            dfq_ref[rows, :] += jnp.where(h0, rss[0], rss[1])

        @pl.when(qi > ki)
        def _():
            step(False)

        @pl.when(qi == ki)
        def _():
            step(True)

        @pl.when(qi == nq - 1)
        def _():
            dk_ref[...] = dk_acc[...]
            dv_ref[...] = dv_acc[...]
            dfk_ref[...] = df_acc[...]

        @pl.when((ki == nq - 1) & (qi == nq - 1))
        def _():
            dq_ref[...] = dq_ref[...] * 0.125

    q_row = lambda ki, qi: jnp.maximum(qi, ki)

    def qside(base):
        return pl.BlockSpec((T, LANES), lambda j, ki, qi: (q_row(ki, qi), base + j))

    def kside(base):
        return pl.BlockSpec((T, LANES), lambda j, ki, qi: (ki, base + j))

    kblk = pl.BlockSpec((T, LANES), lambda j, ki, qi: (ki, j))
    pair = pl.BlockSpec((L, LANES), lambda j, ki, qi: (0, j))
    frow_spec = pl.BlockSpec((None, 2, T), lambda j, ki, qi: (j, 0, ki))
    return pl.pallas_call(
        body, name=name, grid=(4, nq, nq),
        in_specs=[qside(Q_BLK), kside(K_BLK), kside(V_BLK), qside(0), frow_spec, qside(0), qside(0), qside(4)],
        out_specs=[pair, kblk, kblk, pair, frow_spec],
        out_shape=[_sds((L, FOX_W)), _sds((L, FOX_W)), _sds((L, FOX_W)), _sds((L, FOX_W)), _sds((4, 2, L))],
        scratch_shapes=[pltpu.VMEM((T, LANES), F32), pltpu.VMEM((T, LANES), F32), pltpu.VMEM((2, T), F32)],
        compiler_params=_cp("parallel", "arbitrary", "arbitrary"),
    )(z, z, z, fq, frow, o, lse, g_m)


def _shift_rows(v, s, down, row):
    n = v.shape[0]
    if down:
        return jnp.where(row >= s, pltpu.roll(v, s, 0), 0.0)
    return jnp.where(row < n - s, pltpu.roll(v, n - s, 0), 0.0)


def _window_sum(v, g, down, row):
    out = jnp.zeros_like(v)
    s = v
    for k in range(4):
        s = s + _shift_rows(s, 1 << k, down, row)
        out = jnp.where(g == k, s, out)
    return out


def _pool_inv_cnt(g, row):
    w = jnp.left_shift(2, g).astype(F32)
    return 1.0 / jnp.minimum(row.astype(F32) + 1.0, w)


def _pool_fwd(z, pool_w, scale, name):
    L = z.shape[0]

    def body(x_ref, w_ref, s_ref, y_ref, p_ref):
        g = pl.program_id(0)
        row = lax.broadcasted_iota(jnp.int32, (L, LANES), 0)
        x = x_ref[...]
        pooled = (_window_sum(x, g, True, row) * _pool_inv_cnt(g, row) - x).astype(BF16)
        p_ref[...] = pooled
        y_ref[...] = (_dot(pooled, w_ref[...].astype(BF16)) * s_ref[...]).astype(BF16)

    col = pl.BlockSpec((L, LANES), lambda g: (0, g))
    return pl.pallas_call(
        body, name=name, grid=(4,),
        in_specs=[col, pl.BlockSpec((None, LANES, LANES), lambda g: (g, 0, 0)), pl.BlockSpec((1, LANES), lambda g: (0, g))],
        out_specs=[col, col],
        out_shape=[_sds((L, 512), BF16), _sds((L, 512), BF16)],
        compiler_params=_cp("parallel"),
    )(z, pool_w, scale)


def _pool_bwd(g_m, pooled, pool_w, scale, name):
    L = g_m.shape[0]

    def body(g_ref, p_ref, w_ref, s_ref, gx_ref, gw_ref, gs_ref):
        g = pl.program_id(0)
        row = lax.broadcasted_iota(jnp.int32, (L, LANES), 0)
        gy = g_ref[...]
        pooled = p_ref[...]
        wb = w_ref[...].astype(BF16)
        lin = _dot(pooled, wb)
        gs_ref[...] = jnp.sum(gy * lin, axis=0, keepdims=True)
        glin = (gy * s_ref[...]).astype(BF16)
        gw_ref[...] = _dot(pooled, glin, TN)
        gp = _dot(glin, wb, NT)
        gx_ref[...] = _window_sum(gp * _pool_inv_cnt(g, row), g, False, row) - gp

    col = pl.BlockSpec((L, LANES), lambda g: (0, g))
    wspec = pl.BlockSpec((None, LANES, LANES), lambda g: (g, 0, 0))
    vec = pl.BlockSpec((1, LANES), lambda g: (0, g))
    return pl.pallas_call(
        body, name=name, grid=(4,),
        in_specs=[col, col, wspec, vec],
        out_specs=[col, wspec, vec],
        out_shape=[_sds((L, 512)), _sds((4, LANES, LANES)), _sds((1, 512))],
        compiler_params=_cp("parallel"),
    )(g_m, pooled, pool_w, scale)


SGU_CHUNKS = 4


def _sgu_ln(v, gam, bet):
    gv = _gelu(v)
    mu = jnp.mean(gv, axis=-1, keepdims=True)
    xc = gv - mu
    rs = lax.rsqrt(jnp.mean(xc * xc, axis=-1, keepdims=True) + EPS)
    xh = xc * rs
    return xh, rs, xh * gam + bet


def _tril_ws(w_ref, g):
    r = lax.broadcasted_iota(jnp.int32, (LANES, LANES), 0)
    c = lax.broadcasted_iota(jnp.int32, (LANES, LANES), 1)
    return jnp.where(r >= c, w_ref[g], 0.0).astype(BF16)


def _sgu_fwd(z, ln_g, ln_b, w_s, b_st, name):
    L = z.shape[0]
    rb = min(SGU_CHUNKS * LANES, L)

    def body(u_ref, v_ref, g_ref, b_ref, w_ref, bs_ref, y_ref):
        _, _, vln = _sgu_ln(v_ref[...], g_ref[...], b_ref[...])
        gu = _gelu(u_ref[...])
        vb = vln.astype(BF16)
        for g in range(4):
            ws = _tril_ws(w_ref, g)
            for n in range(rb // LANES):
                rows = slice(n * LANES, (n + 1) * LANES)
                cols = slice(g * LANES, (g + 1) * LANES)
                mixed = _dot(ws, vb[rows, cols]) + bs_ref[:, g:g + 1]
                y_ref[rows, cols] = (gu[rows, cols] * mixed).astype(BF16)

    vm = lambda shape: pl.BlockSpec(shape, lambda i: tuple(0 for _ in shape))
    return pl.pallas_call(
        body, name=name, grid=(L // rb,),
        in_specs=[pl.BlockSpec((rb, 512), lambda i: (i, 1)), pl.BlockSpec((rb, 512), lambda i: (i, 2)),
                  vm((1, 512)), vm((1, 512)), vm((4, LANES, LANES)), vm((LANES, 4))],
        out_specs=pl.BlockSpec((rb, 512), lambda i: (i, 0)),
        out_shape=_sds((L, 512), BF16),
        compiler_params=_cp("parallel"),
    )(z, z, ln_g, ln_b, w_s, b_st)


def _sgu_bwd(g_m, z, ln_g, ln_b, w_s, b_st, name):
    L = z.shape[0]
    rb = min(SGU_CHUNKS * LANES, L)

    def body(gy_ref, u_ref, v_ref, g_ref, b_ref, w_ref, bs_ref, gu_ref, gv_ref, gw_ref, gbs_ref, gg_ref, gb_ref):
        i = pl.program_id(0)

        @pl.when(i == 0)
        def _():
            gw_ref[...] = jnp.zeros_like(gw_ref)
            gbs_ref[...] = jnp.zeros_like(gbs_ref)
            gg_ref[...] = jnp.zeros_like(gg_ref)
            gb_ref[...] = jnp.zeros_like(gb_ref)

        v = v_ref[...]
        u = u_ref[...]
        gy = gy_ref[...]
        xh, rs, vln = _sgu_ln(v, g_ref[...], b_ref[...])
        gel_u = _gelu(u)
        gmix = gy * gel_u
        vb = vln.astype(BF16)
        gmb = gmix.astype(BF16)
        r = lax.broadcasted_iota(jnp.int32, (LANES, LANES), 0)
        c = lax.broadcasted_iota(jnp.int32, (LANES, LANES), 1)
        gvln_cols = []
        for g in range(4):
            ws = _tril_ws(w_ref, g)
            cols = slice(g * LANES, (g + 1) * LANES)
            gw = jnp.zeros((LANES, LANES), F32)
            gbs = jnp.zeros((LANES, 1), F32)
            parts = []
            for n in range(rb // LANES):
                rows = slice(n * LANES, (n + 1) * LANES)
                mixed = _dot(ws, vb[rows, cols]) + bs_ref[:, g:g + 1]
                gu_ref[rows, cols] = gy[rows, cols] * mixed * _gelu_grad(u[rows, cols])
                parts.append(_dot(ws, gmb[rows, cols], TN))
                gw = gw + _dot(gmb[rows, cols], vb[rows, cols], NT)
                gbs = gbs + jnp.sum(gmix[rows, cols], axis=1, keepdims=True)
            gvln_cols.append(jnp.concatenate(parts, axis=0))
            gw_ref[g] += jnp.where(r >= c, gw, 0.0)
            gbs_ref[:, g:g + 1] += gbs
        gvln = jnp.concatenate(gvln_cols, axis=1)
        gg_ref[...] += jnp.sum(gvln * xh, axis=0, keepdims=True)
        gb_ref[...] += jnp.sum(gvln, axis=0, keepdims=True)
        gxh = gvln * g_ref[...]
        ggv = rs * (gxh - jnp.mean(gxh, axis=-1, keepdims=True) - xh * jnp.mean(gxh * xh, axis=-1, keepdims=True))
        gv_ref[...] = ggv * _gelu_grad(v)

    vm = lambda shape: pl.BlockSpec(shape, lambda i: tuple(0 for _ in shape))
    blk = pl.BlockSpec((rb, 512), lambda i: (i, 0))
    return pl.pallas_call(
        body, name=name, grid=(L // rb,),
        in_specs=[pl.BlockSpec((rb, 512), lambda i: (i, 1)), pl.BlockSpec((rb, 512), lambda i: (i, 1)),
                  pl.BlockSpec((rb, 512), lambda i: (i, 2)),
                  vm((1, 512)), vm((1, 512)), vm((4, LANES, LANES)), vm((LANES, 4))],
        out_specs=[blk, blk, vm((4, LANES, LANES)), vm((LANES, 4)), vm((1, 512)), vm((1, 512))],
        out_shape=[_sds((L, 512)), _sds((L, 512)), _sds((4, LANES, LANES)), _sds((LANES, 4)),
                   _sds((1, 512)), _sds((1, 512))],
        compiler_params=_cp("arbitrary"),
    )(g_m, z, z, ln_g, ln_b, w_s, b_st)


def _adamw_math(w, g, m, v):
    nm = ADAM_B1 * m + (1.0 - ADAM_B1) * g
    nv = ADAM_B2 * v + (1.0 - ADAM_B2) * (g * g)
    m_hat = nm / (1.0 - ADAM_B1 ** ADAM_STEP)
    v_hat = nv / (1.0 - ADAM_B2 ** ADAM_STEP)
    delta = -ADAM_LR * (m_hat / (jnp.sqrt(v_hat) + ADAM_EPS) + ADAM_WD * w)
    return delta, nm, nv


def _sum_adamw(parts, w, m, v, name, layer=0, prev=None):
    n_layers, R, C = w.shape
    rb = 128 if R % 128 == 0 else R

    def body(p_ref, w_ref, m_ref, v_ref, *rest):
        g_ref, d_ref, nm_ref, nv_ref = rest[-4:]
        g = p_ref[0]
        for s in range(1, N_DEV):
            g = g + p_ref[s]
        d, nm, nv = _adamw_math(w_ref[...], g, m_ref[...], v_ref[...])
        g_ref[...] = g
        d_ref[...] = d
        nm_ref[...] = nm
        nv_ref[...] = nv

    blk = pl.BlockSpec((None, rb, C), lambda i: (layer, i, 0))
    prev = [] if prev is None else list(prev)
    return pl.pallas_call(
        body, name=name, grid=(R // rb,),
        in_specs=[pl.BlockSpec((N_DEV, rb, C), lambda i: (0, i, 0)), blk, blk, blk] + [ANY] * len(prev),
        out_specs=[blk] * 4, out_shape=[_sds((n_layers, R, C))] * 4,
        input_output_aliases={4 + k: k for k in range(len(prev))},
        compiler_params=_cp("parallel"),
    )(parts, w, m, v, *prev)


def _sum_pieces(parts, name):
    _, R, C = parts.shape

    def body(p_ref, g_ref):
        g = p_ref[0]
        for s in range(1, N_DEV):
            g = g + p_ref[s]
        g_ref[...] = g

    vm = pl.BlockSpec(memory_space=pltpu.VMEM)
    return pl.pallas_call(body, name=name, in_specs=[vm], out_specs=vm, out_shape=_sds((R, C)))(parts)


def _adamw(w, g, m, v, name):
    vm = pl.BlockSpec(memory_space=pltpu.VMEM)

    def body(w_ref, g_ref, m_ref, v_ref, d_ref, nm_ref, nv_ref):
        d, nm, nv = _adamw_math(w_ref[...], g_ref[...], m_ref[...], v_ref[...])
        d_ref[...] = d
        nm_ref[...] = nm
        nv_ref[...] = nv

    return pl.pallas_call(body, name=name, in_specs=[vm] * 4, out_specs=[vm] * 3,
                          out_shape=[_sds(w.shape)] * 3)(w, g, m, v)


def _mesh_pos():
    return lax.axis_index("x"), lax.axis_index("y"), lax.axis_index("c")


def _dev_index(p):
    return 4 * p[0] + 2 * p[1] + p[2]


def _all_gather(xs, name):
    n = len(xs)

    def body(*refs):
        x_refs, o_refs = refs[:n], refs[n:2 * n]
        send_sems, recv_sems, local_sems = refs[2 * n:]
        x, y, c = _mesh_pos()
        me, sibling = (x, y, c), (x, y, 1 - c)
        chips = [(1 - x, y), (x, 1 - y), (1 - x, 1 - y)]

        def copy(i, k, block, to, src=None):
            dst = o_refs[i].at[_dev_index(block)]
            return pltpu.make_async_remote_copy(
                src_ref=dst if src is None else src, dst_ref=dst,
                send_sem=send_sems.at[i, k], recv_sem=recv_sems.at[i, k], device_id=to, device_id_type=MESH)

        mine = [pltpu.make_async_copy(x_refs[i], o_refs[i].at[_dev_index(me)], local_sems.at[i]) for i in range(n)]
        for cp in mine:
            cp.start()
        first = []
        for i in range(n):
            first.append(copy(i, 0, me, sibling, src=x_refs[i]))
            first += [copy(i, 1 + j, me, (*chip, c), src=x_refs[i]) for j, chip in enumerate(chips)]
        for cp in first:
            cp.start()
        passed = []
        for j, chip in enumerate(chips):
            for i in range(n):
                copy(i, 1 + j, (*chip, c), me).wait_recv()
                fwd = copy(i, 4 + j, (*chip, c), sibling)
                fwd.start()
                passed.append(fwd)
        for i in range(n):
            copy(i, 0, sibling, me).wait_recv()
            for j, chip in enumerate(chips):
                copy(i, 4 + j, (*chip, 1 - c), me).wait_recv()
        for cp in first + passed:
            cp.wait_send()
        for cp in mine:
            cp.wait()

    outs = pl.pallas_call(
        body, name=name,
        in_specs=[ANY] * n, out_specs=[ANY] * n,
        out_shape=[_sds((N_DEV,) + x.shape, x.dtype) for x in xs],
        scratch_shapes=[pltpu.SemaphoreType.DMA((n, 7)), pltpu.SemaphoreType.DMA((n, 7)),
                        pltpu.SemaphoreType.DMA((n,))],
    )(*xs)
    return list(outs)


def _exchange(gs, name):
    n = len(gs)

    def body(*refs):
        g_refs, o_refs = refs[:n], refs[n:2 * n]
        send_sems, recv_sems, local_sems = refs[2 * n:]
        x, y, c = _mesh_pos()
        me = (x, y, c)
        mi = _dev_index(me)
        peers = [(x ^ dx, y ^ dy, c ^ dc) for dx in range(2) for dy in range(2) for dc in range(2)][1:]

        def copy(i, k, peer):
            return pltpu.make_async_remote_copy(
                src_ref=g_refs[i].at[_dev_index(peer)], dst_ref=o_refs[i].at[mi],
                send_sem=send_sems.at[i, k], recv_sem=recv_sems.at[i, k], device_id=peer, device_id_type=MESH)

        mine = [pltpu.make_async_copy(g_refs[i].at[mi], o_refs[i].at[mi], local_sems.at[i]) for i in range(n)]
        for cp in mine:
            cp.start()
        sends = [copy(i, k, peer) for i in range(n) for k, peer in enumerate(peers)]
        for cp in sends:
            cp.start()
        for i in range(n):
            for k, peer in enumerate(peers):
                pltpu.make_async_remote_copy(
                    src_ref=g_refs[i].at[mi], dst_ref=o_refs[i].at[_dev_index(peer)],
                    send_sem=send_sems.at[i, k], recv_sem=recv_sems.at[i, k], device_id=peer,
                    device_id_type=MESH).wait_recv()
        for cp in sends:
            cp.wait_send()
        for cp in mine:
            cp.wait()

    outs = pl.pallas_call(
        body, name=name,
        in_specs=[ANY] * n, out_specs=[ANY] * n,
        out_shape=[_sds(g.shape, g.dtype) for g in gs],
        scratch_shapes=[pltpu.SemaphoreType.DMA((n, 7)), pltpu.SemaphoreType.DMA((n, 7)),
                        pltpu.SemaphoreType.DMA((n,))],
    )(*gs)
    return list(outs)


HBM = pl.BlockSpec(memory_space=pltpu.HBM)
SEM = pl.BlockSpec(memory_space=pltpu.SEMAPHORE)
EFFECT = pltpu.SideEffectType.DATAFLOW_SIDE_EFFECTING


def _peer_list():
    x, y, c = _mesh_pos()
    peers = [(x ^ dx, y ^ dy, c ^ dc) for dx in range(2) for dy in range(2) for dc in range(2)][1:]
    return (x, y, c), peers


def _split_copy(src_ref, land_ref, send_sems, recv_sems, i, k, peer, slot, exchange):
    return pltpu.make_async_remote_copy(
        src_ref=src_ref.at[_dev_index(peer)] if exchange else src_ref, dst_ref=land_ref.at[slot],
        send_sem=send_sems.at[7 * i + k], recv_sem=recv_sems.at[7 * i + k], device_id=peer, device_id_type=MESH)


def _comm_start(groups, name, exchange, dep=None):
    sizes = [len(g) for g in groups]
    n = sum(sizes)
    srcs = [a for g in groups for a in g]
    my_index = _dev_index(_mesh_pos())
    lands = []
    for a in srcs:
        if exchange:
            own = lax.dynamic_slice(a, (my_index, 0, 0), (1,) + a.shape[1:])
            shape = a.shape
        else:
            own = a[None]
            shape = (N_DEV,) + a.shape
        lands.append(lax.dynamic_update_slice(lax.empty(shape, a.dtype), own, (my_index, 0, 0)))

    n_dep = 0 if dep is None else 1

    def body(*refs):
        src_refs, land_refs = refs[:n], refs[n:2 * n]
        sem_refs = refs[2 * n + n_dep:2 * n + n_dep + 2 * len(sizes)]
        token_ref = refs[-1]
        me, peers = _peer_list()
        mi = _dev_index(me)
        i = 0
        for gi, sz in enumerate(sizes):
            for j in range(sz):
                for k, peer in enumerate(peers):
                    _split_copy(src_refs[i], land_refs[i], sem_refs[2 * gi], sem_refs[2 * gi + 1], j, k, peer, mi,
                                exchange).start()
                i += 1
        token_ref[...] = jnp.zeros_like(token_ref)

    sem_shapes = []
    for sz in sizes:
        sem_shapes += [pltpu.SemaphoreType.DMA((7 * sz,)), pltpu.SemaphoreType.DMA((7 * sz,))]
    thru = [pltpu.HBM(a.shape, a.dtype) for a in srcs + lands]
    n_sem = len(sem_shapes)
    outs = pl.pallas_call(
        body, name=name,
        out_shape=tuple(sem_shapes + thru + [_sds((8, LANES))]),
        in_specs=[HBM] * (2 * n) + [ANY] * n_dep,
        out_specs=tuple([SEM] * n_sem + [HBM] * (2 * n) + [pl.BlockSpec(memory_space=pltpu.VMEM)]),
        input_output_aliases={i: n_sem + i for i in range(2 * n)},
        compiler_params=pltpu.CompilerParams(has_side_effects=EFFECT),
    )(*[pltpu.with_memory_space_constraint(a, pltpu.HBM) for a in srcs + lands], *([] if dep is None else [dep]))
    sems, thru_src, thru_land, token = outs[:n_sem], outs[n_sem:n_sem + n], outs[n_sem + n:n_sem + 2 * n], outs[-1]
    result, off = [], 0
    for gi, sz in enumerate(sizes):
        result.append((sems[2 * gi], sems[2 * gi + 1], list(thru_src[off:off + sz]), list(thru_land[off:off + sz])))
        off += sz
    return result, token


def _comm_wait(group, after, name, exchange):
    send_sems, recv_sems, srcs, lands = group
    n = len(srcs)
    after = list(after) if isinstance(after, (list, tuple)) else [after]

    def body(*refs):
        src_refs, land_refs = refs[:n], refs[n:2 * n]
        ssem, rsem = refs[2 * n], refs[2 * n + 1]
        me, peers = _peer_list()
        for i in range(n):
            for k, peer in enumerate(peers):
                cp = _split_copy(src_refs[i], land_refs[i], ssem, rsem, i, k, peer, _dev_index(peer), exchange)
                cp.wait_send()
                cp.wait_recv()

    outs = pl.pallas_call(
        body, name=name,
        out_shape=tuple(pltpu.HBM(a.shape, a.dtype) for a in srcs + lands),
        in_specs=[HBM] * (2 * n) + [SEM, SEM] + [ANY] * len(after),
        out_specs=tuple([HBM] * (2 * n)),
        input_output_aliases={i: i for i in range(2 * n)},
        compiler_params=pltpu.CompilerParams(has_side_effects=EFFECT),
    )(*srcs, *lands, send_sems, recv_sems, *after)
    return list(outs[n:])


def _tie(a, token):
    return a + token[0, 0].astype(a.dtype)


def _pack(arrs, rows):
    flat = jnp.concatenate([a.reshape(-1).astype(F32) for a in arrs])
    return jnp.pad(flat, (0, rows * LANES - flat.shape[0])).reshape(rows, LANES)


def _unpack(packed, shapes):
    flat = packed.reshape(-1)
    out, off = [], 0
    for s in shapes:
        n = math.prod(s)
        out.append(flat[off:off + n].reshape(s))
        off += n
    return out


def _packed_rows(shapes):
    n = sum(math.prod(s) for s in shapes)
    unit = N_DEV * 8 * LANES
    return -(-n // unit) * unit // LANES


def kernel(x, mix_pre_g, mix_post_g, mlp_pre_g, mlp_post_g, w_in_even, s5_lam_re, s5_lam_im, s5_log_dt, s5_b_re, s5_b_im, s5_c_re, s5_c_im, s5_d, s5_w_glu, fox_b_f, w_out_even, w_in_odd, pool_w, pool_scale, sgu_ln_g, sgu_ln_b, sgu_w_s, sgu_b_s, w_out_odd, mlp_w1, mlp_w2, loss_target, m_mix_pre_g, m_mix_post_g, m_mlp_pre_g, m_mlp_post_g, m_w_in_even, m_s5_lam_re, m_s5_lam_im, m_s5_log_dt, m_s5_b_re, m_s5_b_im, m_s5_c_re, m_s5_c_im, m_s5_d, m_s5_w_glu, m_fox_b_f, m_w_out_even, m_w_in_odd, m_pool_w, m_pool_scale, m_sgu_ln_g, m_sgu_ln_b, m_sgu_w_s, m_sgu_b_s, m_w_out_odd, m_mlp_w1, m_mlp_w2, v_mix_pre_g, v_mix_post_g, v_mlp_pre_g, v_mlp_post_g, v_w_in_even, v_s5_lam_re, v_s5_lam_im, v_s5_log_dt, v_s5_b_re, v_s5_b_im, v_s5_c_re, v_s5_c_im, v_s5_d, v_s5_w_glu, v_fox_b_f, v_w_out_even, v_w_in_odd, v_pool_w, v_pool_scale, v_sgu_ln_g, v_sgu_ln_b, v_sgu_w_s, v_sgu_b_s, v_w_out_odd, v_mlp_w1, v_mlp_w2):
    weights = dict(mix_pre_g=mix_pre_g, mix_post_g=mix_post_g, mlp_pre_g=mlp_pre_g, mlp_post_g=mlp_post_g, w_in_even=w_in_even, s5_lam_re=s5_lam_re, s5_lam_im=s5_lam_im, s5_log_dt=s5_log_dt, s5_b_re=s5_b_re, s5_b_im=s5_b_im, s5_c_re=s5_c_re, s5_c_im=s5_c_im, s5_d=s5_d, s5_w_glu=s5_w_glu, fox_b_f=fox_b_f, w_out_even=w_out_even, w_in_odd=w_in_odd, pool_w=pool_w, pool_scale=pool_scale, sgu_ln_g=sgu_ln_g, sgu_ln_b=sgu_ln_b, sgu_w_s=sgu_w_s, sgu_b_s=sgu_b_s, w_out_odd=w_out_odd, mlp_w1=mlp_w1, mlp_w2=mlp_w2)
    mom_m = dict(mix_pre_g=m_mix_pre_g, mix_post_g=m_mix_post_g, mlp_pre_g=m_mlp_pre_g, mlp_post_g=m_mlp_post_g, w_in_even=m_w_in_even, s5_lam_re=m_s5_lam_re, s5_lam_im=m_s5_lam_im, s5_log_dt=m_s5_log_dt, s5_b_re=m_s5_b_re, s5_b_im=m_s5_b_im, s5_c_re=m_s5_c_re, s5_c_im=m_s5_c_im, s5_d=m_s5_d, s5_w_glu=m_s5_w_glu, fox_b_f=m_fox_b_f, w_out_even=m_w_out_even, w_in_odd=m_w_in_odd, pool_w=m_pool_w, pool_scale=m_pool_scale, sgu_ln_g=m_sgu_ln_g, sgu_ln_b=m_sgu_ln_b, sgu_w_s=m_sgu_w_s, sgu_b_s=m_sgu_b_s, w_out_odd=m_w_out_odd, mlp_w1=m_mlp_w1, mlp_w2=m_mlp_w2)
    mom_v = dict(mix_pre_g=v_mix_pre_g, mix_post_g=v_mix_post_g, mlp_pre_g=v_mlp_pre_g, mlp_post_g=v_mlp_post_g, w_in_even=v_w_in_even, s5_lam_re=v_s5_lam_re, s5_lam_im=v_s5_lam_im, s5_log_dt=v_s5_log_dt, s5_b_re=v_s5_b_re, s5_b_im=v_s5_b_im, s5_c_re=v_s5_c_re, s5_c_im=v_s5_c_im, s5_d=v_s5_d, s5_w_glu=v_s5_w_glu, fox_b_f=v_fox_b_f, w_out_even=v_w_out_even, w_in_odd=v_w_in_odd, pool_w=v_pool_w, pool_scale=v_pool_scale, sgu_ln_g=v_sgu_ln_g, sgu_ln_b=v_sgu_ln_b, sgu_w_s=v_sgu_w_s, sgu_b_s=v_sgu_b_s, w_out_odd=v_w_out_odd, mlp_w1=v_mlp_w1, mlp_w2=v_mlp_w2)
    names = list(weights)
    L = x.shape[1]
    x0 = x[0]
    target = loss_target[0]
    my_index = 4 * lax.axis_index("x") + 2 * lax.axis_index("y") + lax.axis_index("c")

    small_vec = jnp.zeros((8, LANES), F32)
    small_vec = small_vec.at[0, :64].set(pool_scale[0]).at[1, :64].set(sgu_ln_g[0]).at[2, :64].set(sgu_ln_b[0])
    ag_groups, ag_token = _comm_start(
        [[jnp.transpose(w_in_even[0]).astype(BF16), small_vec],
         [s5_w_glu[0].astype(BF16), w_out_even[0].astype(BF16)],
         [mlp_w1[0].astype(BF16), mlp_w2[0].astype(BF16)],
         [jnp.transpose(w_in_odd[0]).astype(BF16), w_out_odd[0].astype(BF16), mlp_w1[1].astype(BF16), mlp_w2[1].astype(BF16)]],
        "ag_start", exchange=False)

    lam_r = jnp.concatenate([s5_lam_re.reshape(1, S5_NS), s5_lam_im.reshape(1, S5_NS)], axis=0)
    ldt_r = jnp.repeat(s5_log_dt.reshape(32), 64).reshape(1, S5_NS)
    lam_c = jnp.transpose(lam_r)
    ldt_c = jnp.transpose(ldt_r)
    b_t = jnp.stack([jnp.tile(s5_b_re.reshape(S5_NS, 16), (1, 8)), jnp.tile(s5_b_im.reshape(S5_NS, 16), (1, 8))])
    c_t = jnp.stack([jnp.tile(s5_c_re.reshape(S5_W, 64), (1, 8)), jnp.tile(s5_c_im.reshape(S5_W, 64), (1, 8))])
    bf_pad = jnp.pad(fox_b_f, ((0, 0), (0, LANES - 8)))
    b_st = jnp.transpose(sgu_b_s[0])

    h0, rx0 = _rms_fwd(x0, _tie(mix_pre_g[0:1], ag_token), "rms0")
    tabs, bset, cset = _s5_prep(lam_r, ldt_r, lam_c, ldt_c, b_t, c_t, "s5_prep")
    ag0 = _comm_wait(ag_groups[0], tabs, "ag_wait0", exchange=False)
    winT_e = jnp.pad(ag0[0].reshape(EVEN_IN, D_MODEL), ((0, EVEN_PAD - EVEN_IN), (0, 0)))
    pool_scale_f = ag0[1][:, 0, :64].reshape(1, 512)
    ln_g_f = ag0[1][:, 1, :64].reshape(1, 512)
    ln_b_f = ag0[1][:, 2, :64].reshape(1, 512)
    z0 = _mm(h0, winT_e, name="win_even", tb=True, bm=512, bn=EVEN_PAD)
    bu = _s5_bu(z0, bset, "s5_bu")
    xs = _s5_scan(bu, tabs, "s5_scan")
    ag1 = _comm_wait(ag_groups[1], xs, "ag_wait1", exchange=False)
    wglu = ag1[0].reshape(S5_W, S5_W)
    wout_e = ag1[1].reshape(D_MODEL, D_MODEL)
    ylin, ya = _s5_out_fwd(xs, cset, z0, s5_d, wglu, "s5_out")
    fcum, fq = _fox_f_fwd(z0, bf_pad, "fox_f")
    frow = jnp.transpose(fcum[:, :8]).reshape(4, 2, L)
    o_att, lse = _fox_fwd(z0, fq, frow, "fox_fwd")
    mix0 = jnp.concatenate([ya, o_att.astype(BF16)], axis=1)
    y0 = _mm(mix0, wout_e, name="wout_even")
    x1, ry0, h1, rx1 = _post_pre_fwd(x0, y0, mix_post_g[0:1], mlp_pre_g[0:1], "post0")
    ag2 = _comm_wait(ag_groups[2], rx1, "ag_wait2", exchange=False)
    w1 = [ag2[0], None]
    w2 = [ag2[1].reshape(4 * D_MODEL, D_MODEL), None]
    p0, a0 = _mm(h1, w1[0], name="mlp0_w1", b3=True, out_dtypes=(BF16, BF16), epi=_epi_relu2)
    o0 = _mm(a0, w2[0], name="mlp0_w2")
    x2, ro0, h2, rx2 = _post_pre_fwd(x1, o0, mlp_post_g[0:1], mix_pre_g[1:2], "post1")
    ag3 = _comm_wait(ag_groups[3], rx2, "ag_wait3", exchange=False)
    winT_o = ag3[0].reshape(ODD_IN, D_MODEL)
    wout_o = ag3[1].reshape(D_MODEL, D_MODEL)
    w1[1] = ag3[2]
    w2[1] = ag3[3].reshape(4 * D_MODEL, D_MODEL)
    z1 = _mm(h2, winT_o, name="win_odd", tb=True, bn=ODD_IN)
    yc, pooled = _pool_fwd(z1, pool_w[0], pool_scale_f, "pool_fwd")
    yd = _sgu_fwd(z1, ln_g_f, ln_b_f, sgu_w_s[0], b_st, "sgu_fwd")
    mix1 = jnp.concatenate([yc, yd], axis=1)
    y1 = _mm(mix1, wout_o, name="wout_odd")
    x3, ry1, h3, rx3 = _post_pre_fwd(x2, y1, mix_post_g[1:2], mlp_pre_g[1:2], "post2")
    p1, a1 = _mm(h3, w1[1], name="mlp1_w1", b3=True, out_dtypes=(BF16, BF16), epi=_epi_relu2)
    o1 = _mm(a1, w2[1], name="mlp1_w2")
    gx4, ro1, sq = _post_loss_fwd(x3, o1, mlp_post_g[1:2], target, "post3")

    g_o1, gg_mlp_post1 = _post_bwd(gx4, o1, ro1, mlp_post_g[1:2], "bpost3")
    g_p1 = _mm(g_o1, w2[1], name="b_mlp1_a", tb=True, out_dtypes=(BF16,), epi=_epi_relu2_bwd, extra=(p1,))
    gw2_1 = _mm(a1, g_o1, name="b_mlp1_w2", ta=True)
    g_h3 = _mm(g_p1, w1[1], name="b_mlp1_h", tb=True, b3=True)
    gw1_1 = _mm(h3, g_p1, name="b_mlp1_w1", ta=True, out3=True)
    (ex1,), tok1 = _comm_start([[gw1_1, gw2_1.reshape(N_DEV, 512, D_MODEL)]], "ex_start1", exchange=True)
    g_x3, gg_mlp_pre1, g_y1, gg_mix_post1 = _pre_post_bwd(g_h3, x3, rx3, _tie(mlp_pre_g[1:2], tok1), gx4, y1, ry1, mix_post_g[1:2], "bpre3")
    g_mix1 = _mm(g_y1, wout_o, name="b_wout_odd_m", tb=True)
    gwout_o = _mm(mix1, g_y1, name="b_wout_odd_w", ta=True)
    g_xc, g_pool_w, g_pool_scale = _pool_bwd(g_mix1, pooled, pool_w[0], pool_scale_f, "pool_bwd")
    g_u1, g_v1, g_ws, g_bst, g_ln_g, g_ln_b = _sgu_bwd(g_mix1, z1, ln_g_f, ln_b_f, sgu_w_s[0], b_st, "sgu_bwd")
    g_z1 = jnp.concatenate([g_xc, g_u1, g_v1], axis=1).astype(BF16)
    g_h2 = _mm(g_z1, winT_o, name="b_win_odd_h", bk=ODD_IN)
    gwinT_o = _mm(g_z1, h2, name="b_win_odd_w", ta=True, bm=512)
    (ex2,), tok2 = _comm_start([[gwout_o.reshape(N_DEV, 128, D_MODEL), gwinT_o.reshape(N_DEV, ODD_IN // N_DEV, D_MODEL)]], "ex_start2", exchange=True)
    g_x2, gg_mix_pre1, g_o0, gg_mlp_post0 = _pre_post_bwd(g_h2, x2, rx2, _tie(mix_pre_g[1:2], tok2), g_x3, o0, ro0, mlp_post_g[0:1], "bpre2")
    g_p0 = _mm(g_o0, w2[0], name="b_mlp0_a", tb=True, out_dtypes=(BF16,), epi=_epi_relu2_bwd, extra=(p0,))
    gw2_0 = _mm(a0, g_o0, name="b_mlp0_w2", ta=True)
    g_h1 = _mm(g_p0, w1[0], name="b_mlp0_h", tb=True, b3=True)
    gw1_0 = _mm(h1, g_p0, name="b_mlp0_w1", ta=True, out3=True)
    (ex3,), tok3 = _comm_start([[gw1_0, gw2_0.reshape(N_DEV, 512, D_MODEL)]], "ex_start3", exchange=True)
    g_x1, gg_mlp_pre0, g_y0, gg_mix_post0 = _pre_post_bwd(g_h1, x1, rx1, _tie(mlp_pre_g[0:1], tok3), g_x2, y0, ry0, mix_post_g[0:1], "bpre1")
    g_mix0 = _mm(g_y0, wout_e, name="b_wout_even_m", tb=True)
    gwout_e = _mm(mix0, g_y0, name="b_wout_even_w", ta=True)
    gyl, gud, g_wglu, g_d = _s5_glu_bwd(g_mix0, ylin, z0, s5_d, wglu, "s5_glu_bwd")
    (ex4,), tok4 = _comm_start([[gwout_e.reshape(N_DEV, 128, D_MODEL), g_wglu.reshape(N_DEV, 64, S5_W)]], "ex_start4", exchange=True)
    gxd, gc_raw = _s5_c_bwd(gyl, xs, _tie(cset, tok4), "s5_c_bwd")
    gxs, ga = _s5_scan(gxd, tabs, "s5_scan_bwd", reverse=True, xs=xs)
    g_u0, gb_raw = _s5_bu_bwd(gxs, bset, z0, gud, "s5_bu_bwd")
    g_lam, g_ldt, g_b, g_c = _s5_param_bwd(lam_c, ldt_c, b_t, gb_raw, jnp.transpose(ga), gc_raw, "s5_param_bwd")
    dq, dk, dv, dfq, dfrow = _fox_bwd(z0, fq, frow, o_att, lse, g_mix0, "fox_bwd")
    dFk = jnp.pad(jnp.transpose(dfrow.reshape(8, L)), ((0, 0), (0, LANES - 8)))
    dfl, db_f = _fox_f_bwd(dFk, dfq, z0, bf_pad, "fox_f_bwd")
    g_z0 = jnp.concatenate([g_u0, dq, dk, dv, dfl], axis=1).astype(BF16)
    g_h0 = _mm(g_z0, winT_e, name="b_win_even_h", bk=EVEN_PAD)
    grad_x, gg_mix_pre0 = _pre_bwd(g_h0, x0, rx0, mix_pre_g[0:1], g_x1, "bpre0")
    gwinT_e = _mm(g_z0, h0, name="b_win_even_w", ta=True, bm=EVEN_PAD, bk=512)

    small_grads = dict(
        mix_pre_g=jnp.concatenate([gg_mix_pre0, gg_mix_pre1]), mix_post_g=jnp.concatenate([gg_mix_post0, gg_mix_post1]),
        mlp_pre_g=jnp.concatenate([gg_mlp_pre0, gg_mlp_pre1]), mlp_post_g=jnp.concatenate([gg_mlp_post0, gg_mlp_post1]),
        s5_lam_re=g_lam[:, 0], s5_lam_im=g_lam[:, 1], s5_log_dt=g_ldt,
        s5_b_re=g_b[0, :, :16], s5_b_im=g_b[1, :, :16], s5_c_re=g_c[0, :, :64], s5_c_im=g_c[1, :, :64],
        s5_d=g_d, fox_b_f=db_f[:, :8], pool_w=g_pool_w, sgu_w_s=g_ws, sgu_b_s=jnp.transpose(g_bst),
        pool_scale=g_pool_scale, sgu_ln_g=g_ln_g, sgu_ln_b=g_ln_b)
    small_names = list(small_grads)
    full_shapes = [(512,) if nm in ("pool_scale", "sgu_ln_g", "sgu_ln_b") else weights[nm].shape for nm in small_names]
    full_shapes.append((1, 1))
    rows = _packed_rows(full_shapes)
    packed = _pack([small_grads[nm] for nm in small_names] + [sq], rows).reshape(N_DEV, rows // N_DEV, LANES)
    (recv_small,) = _exchange([packed], "exchange_small")
    piece = _sum_pieces(recv_small, "sum_small")
    (small_all,) = _all_gather([piece], "ag_small")
    small_full = _unpack(small_all.reshape(rows, LANES), full_shapes)
    loss = 0.5 * small_full.pop()[0, 0] / D_MODEL

    gwinT_e_pieces = gwinT_e[:EVEN_IN].reshape(N_DEV, EVEN_IN // N_DEV, D_MODEL)
    (ex5,), tok5 = _comm_start([[gwinT_e_pieces]], "ex_start5", exchange=True, dep=small_all)
    r_w1_1, r_w2_1 = _comm_wait(ex1, tok5, "ex_wait1", exchange=True)
    r_wout_o, r_win_o = _comm_wait(ex2, tok5, "ex_wait2", exchange=True)
    r_w1_0, r_w2_0 = _comm_wait(ex3, tok5, "ex_wait3", exchange=True)
    r_wout_e, r_wglu = _comm_wait(ex4, tok5, "ex_wait4", exchange=True)
    small_g = {}
    for nm, g in zip(small_names, small_full):
        if nm in ("pool_scale", "sgu_ln_g", "sgu_ln_b"):
            g = lax.dynamic_slice(g, (my_index * 64,), (64,)).reshape(1, 64)
        small_g[nm] = g
    own_shapes = [weights[nm].shape for nm in small_names]
    rows2 = _packed_rows(own_shapes)
    pw = _pack([weights[nm] for nm in small_names], rows2)
    pg = _pack([small_g[nm] for nm in small_names], rows2)
    pm = _pack([mom_m[nm] for nm in small_names], rows2)
    pv = _pack([mom_v[nm] for nm in small_names], rows2)
    pd, pnm, pnv = _adamw(pw, pg, pm, pv, "adamw_small")
    res = {}
    for nm, d_, m_, v_ in zip(small_names, _unpack(pd, own_shapes), _unpack(pnm, own_shapes), _unpack(pnv, own_shapes)):
        res[nm] = (small_g[nm], d_, m_, v_)

    for nm, parts in (("mlp_w1", (r_w1_0, r_w1_1)), ("mlp_w2", (r_w2_0, r_w2_1))):
        first = _sum_adamw(parts[0], weights[nm], mom_m[nm], mom_v[nm], "adamw_%s_0" % nm, layer=0)
        res[nm] = tuple(_sum_adamw(parts[1], weights[nm], mom_m[nm], mom_v[nm], "adamw_%s_1" % nm, layer=1, prev=first))
    big_parts = dict(s5_w_glu=r_wglu, w_out_even=r_wout_e, w_out_odd=r_wout_o)
    for nm, parts in big_parts.items():
        res[nm] = tuple(_sum_adamw(parts, weights[nm], mom_m[nm], mom_v[nm], "adamw_" + nm))
    done = [res[nm][1] for nm in ("mlp_w1", "mlp_w2", "s5_w_glu", "w_out_even", "w_out_odd")]
    for nm, parts in (("w_in_odd", r_win_o), ("w_in_even", None)):
        if parts is None:
            (parts,) = _comm_wait(ex5, done, "ex_wait5", exchange=True)
        outs = _sum_adamw(parts, jnp.transpose(weights[nm], (0, 2, 1)), jnp.transpose(mom_m[nm], (0, 2, 1)),
                          jnp.transpose(mom_v[nm], (0, 2, 1)), "adamw_" + nm)
        res[nm] = tuple(jnp.transpose(o, (0, 2, 1)) for o in outs)
        done.append(res[nm][1])

    grads = [res[nm][0].reshape(weights[nm].shape) for nm in names]
    deltas = [res[nm][1].reshape(weights[nm].shape) for nm in names]
    new_m = [res[nm][2].reshape(weights[nm].shape) for nm in names]
    new_v = [res[nm][3].reshape(weights[nm].shape) for nm in names]
    return (loss, grad_x[None], *grads, *deltas, *new_m, *new_v)
```

```python
import functools
import math

import jax
import jax.numpy as jnp
from jax import lax
from jax.experimental import pallas as pl
from jax.experimental.pallas import tpu as pltpu

F32 = jnp.float32
BF16 = jnp.bfloat16
MESH = pl.DeviceIdType.MESH
ANY = pl.BlockSpec(memory_space=pl.ANY)

N_DEV = 8
D_MODEL = 1024
EPS = 1e-6
S5_W = 512
S5_NS = 2048
SCAN_GROUPS = 4
SCAN_CHUNK = 1024
FOX_W = 512
EVEN_IN = 2056
EVEN_PAD = 2176
ODD_IN = 1536
LANES = 128
PIECE = 4 * D_MODEL // N_DEV
VMEM_LIMIT = 56 * 1024 * 1024

ADAM_LR = 0.001
ADAM_B1 = 0.9
ADAM_B2 = 0.999
ADAM_EPS = 1e-08
ADAM_WD = 0.01
ADAM_STEP = 10

NT = (((1,), (1,)), ((), ()))
TN = (((0,), (0,)), ((), ()))
NN = (((1,), (0,)), ((), ()))


def _cp(*sem):
    return pltpu.CompilerParams(dimension_semantics=sem, vmem_limit_bytes=VMEM_LIMIT)


def _sds(shape, dtype=F32):
    return jax.ShapeDtypeStruct(tuple(shape), dtype)


def _gelu(x):
    t = jnp.tanh(0.7978845608028654 * (x + 0.044715 * x * x * x))
    return 0.5 * x * (1.0 + t)


def _gelu_grad(x):
    t = jnp.tanh(0.7978845608028654 * (x + 0.044715 * x * x * x))
    du = 0.7978845608028654 * (1.0 + 3.0 * 0.044715 * x * x)
    return 0.5 * (1.0 + t) + 0.5 * x * (1.0 - t * t) * du


def _sigmoid(x):
    return 1.0 / (1.0 + jnp.exp(-x))


def _dot(a, b, dn=NN):
    return lax.dot_general(a, b, dn, preferred_element_type=F32)


def _mm(a, b, *, name, ta=False, tb=False, b3=False, out3=False, out_dtypes=(F32,), epi=None, extra=(),
        bm=1024, bn=1024, bk=1024):
    M = a.shape[1] if ta else a.shape[0]
    K = a.shape[0] if ta else a.shape[1]
    pw = b.shape[2] if b3 else PIECE
    if b3:
        N = b.shape[1] if tb else b.shape[0] * pw
        assert (b.shape[0] * pw if tb else b.shape[1]) == K
    else:
        N = b.shape[0] if tb else b.shape[1]
    bm, bn, bk = min(bm, M), min(bn, N), min(bk, K)
    assert M % bm == 0 and N % bn == 0 and K % bk == 0, (name, M, N, K, bm, bn, bk)
    assert not (b3 or out3) or ((bk if tb else bn) % pw == 0 and bn % PIECE == 0)
    nk = K // bk
    n_extra = len(extra)
    n_out = len(out_dtypes)
    dn = (((0 if ta else 1,), (1 if tb else 0,)), ((), ()))

    def body(*refs):
        a_ref, b_ref = refs[0], refs[1]
        e_refs = refs[2:2 + n_extra]
        o_refs = refs[2 + n_extra:2 + n_extra + n_out]
        acc_ref = refs[-1]
        k = pl.program_id(2)

        @pl.when(k == 0)
        def _():
            acc_ref[...] = jnp.zeros_like(acc_ref)

        if not b3:
            acc_ref[...] += lax.dot_general(a_ref[...].astype(BF16), b_ref[...].astype(BF16), dn,
                                            preferred_element_type=F32)
        elif tb:
            for t in range(bk // pw):
                a_t = a_ref[pl.ds(t * pw, pw), :] if ta else a_ref[:, pl.ds(t * pw, pw)]
                acc_ref[...] += lax.dot_general(a_t.astype(BF16), b_ref[t].astype(BF16), dn,
                                                preferred_element_type=F32)
        else:
            a_v = a_ref[...].astype(BF16)
            for t in range(bn // pw):
                acc_ref[:, pl.ds(t * pw, pw)] += lax.dot_general(a_v, b_ref[t].astype(BF16), dn,
                                                                 preferred_element_type=F32)

        @pl.when(k == nk - 1)
        def _():
            acc = acc_ref[...]
            outs = (acc,) if epi is None else epi(acc, *[e[...] for e in e_refs])
            for o_ref, o in zip(o_refs, outs):
                if out3:
                    for t in range(bn // PIECE):
                        o_ref[t] = o[:, t * PIECE:(t + 1) * PIECE].astype(o_ref.dtype)
                else:
                    o_ref[...] = o.astype(o_ref.dtype)

    a_spec = pl.BlockSpec((bk, bm), lambda i, j, k: (k, i)) if ta else pl.BlockSpec((bm, bk), lambda i, j, k: (i, k))
    if b3:
        if tb:
            b_spec = pl.BlockSpec((bk // pw, bn, pw), lambda i, j, k: (k, j, 0))
        else:
            b_spec = pl.BlockSpec((bn // pw, bk, pw), lambda i, j, k: (j, k, 0))
    else:
        b_spec = pl.BlockSpec((bn, bk), lambda i, j, k: (j, k)) if tb else pl.BlockSpec((bk, bn), lambda i, j, k: (k, j))
    e_specs = [pl.BlockSpec((bm, bn), lambda i, j, k: (i, j)) for _ in extra]
    if out3:
        o_specs = [pl.BlockSpec((bn // PIECE, bm, PIECE), lambda i, j, k: (j, i, 0)) for _ in out_dtypes]
        o_shapes = [_sds((N // PIECE, M, PIECE), dt) for dt in out_dtypes]
    else:
        o_specs = [pl.BlockSpec((bm, bn), lambda i, j, k: (i, j)) for _ in out_dtypes]
        o_shapes = [_sds((M, N), dt) for dt in out_dtypes]
    outs = pl.pallas_call(
        body, name=name, grid=(M // bm, N // bn, nk),
        in_specs=[a_spec, b_spec] + e_specs, out_specs=o_specs, out_shape=o_shapes,
        scratch_shapes=[pltpu.VMEM((bm, bn), F32)],
        compiler_params=_cp("parallel", "parallel", "arbitrary"),
    )(a, b, *extra)
    return outs[0] if n_out == 1 else outs


def _epi_relu2(acc):
    r = jnp.maximum(acc, 0.0)
    return acc, r * r


def _epi_relu2_bwd(acc, p):
    return (acc * (2.0 * jnp.maximum(p.astype(F32), 0.0)),)


def _row_spec(rb, w=D_MODEL):
    return pl.BlockSpec((rb, w), lambda i: (i, 0))


def _vec_spec(w=D_MODEL):
    return pl.BlockSpec((1, w), lambda i: (0, 0))


def _rstd(v):
    return lax.rsqrt(jnp.mean(v * v, axis=-1, keepdims=True) + EPS)


def _rms_fwd(x, g, name):
    L = x.shape[0]
    rb = min(256, L)

    def body(x_ref, g_ref, h_ref, r_ref):
        xv = x_ref[...]
        r = _rstd(xv)
        h_ref[...] = (xv * r * g_ref[...]).astype(BF16)
        r_ref[...] = r

    return pl.pallas_call(
        body, name=name, grid=(L // rb,),
        in_specs=[_row_spec(rb), _vec_spec()],
        out_specs=[_row_spec(rb), _row_spec(rb, 1)],
        out_shape=[_sds((L, D_MODEL), BF16), _sds((L, 1))],
        compiler_params=_cp("parallel"),
    )(x, g)


def _post_pre_fwd(x_in, y, g_post, g_pre, name):
    L = x_in.shape[0]
    rb = min(256, L)

    def body(x_ref, y_ref, gp_ref, gn_ref, xo_ref, ry_ref, h_ref, rx_ref):
        yv = y_ref[...]
        ry = _rstd(yv)
        xo = x_ref[...] + yv * ry * gp_ref[...]
        rx = _rstd(xo)
        xo_ref[...] = xo
        ry_ref[...] = ry
        h_ref[...] = (xo * rx * gn_ref[...]).astype(BF16)
        rx_ref[...] = rx

    return pl.pallas_call(
        body, name=name, grid=(L // rb,),
        in_specs=[_row_spec(rb), _row_spec(rb), _vec_spec(), _vec_spec()],
        out_specs=[_row_spec(rb), _row_spec(rb, 1), _row_spec(rb), _row_spec(rb, 1)],
        out_shape=[_sds((L, D_MODEL)), _sds((L, 1)), _sds((L, D_MODEL), BF16), _sds((L, 1))],
        compiler_params=_cp("parallel"),
    )(x_in, y, g_post, g_pre)


def _post_loss_fwd(x_in, y, g_post, target, name):
    L = x_in.shape[0]
    rb = min(256, L)

    def body(x_ref, y_ref, gp_ref, t_ref, gx_ref, ry_ref, loss_ref):
        i = pl.program_id(0)
        yv = y_ref[...]
        ry = _rstd(yv)
        diff = x_ref[...] + yv * ry * gp_ref[...] - t_ref[...]
        gx_ref[...] = diff * (1.0 / D_MODEL)
        ry_ref[...] = ry

        @pl.when(i == 0)
        def _():
            loss_ref[...] = jnp.zeros_like(loss_ref)

        loss_ref[...] += jnp.sum(diff * diff, keepdims=True)

    return pl.pallas_call(
        body, name=name, grid=(L // rb,),
        in_specs=[_row_spec(rb), _row_spec(rb), _vec_spec(), _row_spec(rb)],
        out_specs=[_row_spec(rb), _row_spec(rb, 1), pl.BlockSpec((1, 1), lambda i: (0, 0))],
        out_shape=[_sds((L, D_MODEL)), _sds((L, 1)), _sds((1, 1))],
        compiler_params=_cp("arbitrary"),
    )(x_in, y, g_post, target)


def _rms_bwd_rows(dy, xv, r, g):
    n = xv * r
    dyg = dy * g
    return r * (dyg - n * jnp.mean(dyg * n, axis=-1, keepdims=True)), n


def _post_bwd(g_out, y, ry, g_post, name):
    L = y.shape[0]
    rb = min(256, L)

    def body(go_ref, y_ref, ry_ref, gp_ref, gy_ref, gg_ref):
        i = pl.program_id(0)
        go = go_ref[...]
        gy, n = _rms_bwd_rows(go, y_ref[...], ry_ref[...], gp_ref[...])
        gy_ref[...] = gy.astype(BF16)

        @pl.when(i == 0)
        def _():
            gg_ref[...] = jnp.zeros_like(gg_ref)

        gg_ref[...] += jnp.sum(go * n, axis=0, keepdims=True)

    return pl.pallas_call(
        body, name=name, grid=(L // rb,),
        in_specs=[_row_spec(rb), _row_spec(rb), _row_spec(rb, 1), _vec_spec()],
        out_specs=[_row_spec(rb), _vec_spec()],
        out_shape=[_sds((L, D_MODEL), BF16), _sds((1, D_MODEL))],
        compiler_params=_cp("arbitrary"),
    )(g_out, y, ry, g_post)


def _pre_post_bwd(g_h, x, rx, g_pre, g_out, y_prev, ry_prev, g_post_prev, name):
    L = x.shape[0]
    rb = min(256, L)

    def body(gh_ref, x_ref, rx_ref, gn_ref, go_ref, y_ref, ry_ref, gp_ref, gi_ref, ggn_ref, gy_ref, ggp_ref):
        i = pl.program_id(0)
        gh = gh_ref[...]
        gx, n = _rms_bwd_rows(gh, x_ref[...], rx_ref[...], gn_ref[...])
        gi = go_ref[...] + gx
        gi_ref[...] = gi
        gy, ny = _rms_bwd_rows(gi, y_ref[...], ry_ref[...], gp_ref[...])
        gy_ref[...] = gy.astype(BF16)

        @pl.when(i == 0)
        def _():
            ggn_ref[...] = jnp.zeros_like(ggn_ref)
            ggp_ref[...] = jnp.zeros_like(ggp_ref)

        ggn_ref[...] += jnp.sum(gh * n, axis=0, keepdims=True)
        ggp_ref[...] += jnp.sum(gi * ny, axis=0, keepdims=True)

    return pl.pallas_call(
        body, name=name, grid=(L // rb,),
        in_specs=[_row_spec(rb), _row_spec(rb), _row_spec(rb, 1), _vec_spec(), _row_spec(rb),
                  _row_spec(rb), _row_spec(rb, 1), _vec_spec()],
        out_specs=[_row_spec(rb), _vec_spec(), _row_spec(rb), _vec_spec()],
        out_shape=[_sds((L, D_MODEL)), _sds((1, D_MODEL)), _sds((L, D_MODEL), BF16), _sds((1, D_MODEL))],
        compiler_params=_cp("arbitrary"),
    )(g_h, x, rx, g_pre, g_out, y_prev, ry_prev, g_post_prev)


def _pre_bwd(g_h, x, rx, g_pre, g_out, name):
    L = x.shape[0]
    rb = min(256, L)

    def body(gh_ref, x_ref, rx_ref, gn_ref, go_ref, gi_ref, ggn_ref):
        i = pl.program_id(0)
        gh = gh_ref[...]
        gx, n = _rms_bwd_rows(gh, x_ref[...], rx_ref[...], gn_ref[...])
        gi_ref[...] = go_ref[...] + gx

        @pl.when(i == 0)
        def _():
            ggn_ref[...] = jnp.zeros_like(ggn_ref)

        ggn_ref[...] += jnp.sum(gh * n, axis=0, keepdims=True)

    return pl.pallas_call(
        body, name=name, grid=(L // rb,),
        in_specs=[_row_spec(rb), _row_spec(rb), _row_spec(rb, 1), _vec_spec(), _row_spec(rb)],
        out_specs=[_row_spec(rb), _vec_spec()],
        out_shape=[_sds((L, D_MODEL)), _sds((1, D_MODEL))],
        compiler_params=_cp("arbitrary"),
    )(g_h, x, rx, g_pre, g_out)


def _cmul(ar, ai, br, bi):
    return ar * br - ai * bi, ar * bi + ai * br


def _zoh_cols(lr, li, ldt):
    dt = jnp.exp(ldt)
    mag = jnp.exp(lr * dt)
    ar = mag * jnp.cos(li * dt)
    ai = mag * jnp.sin(li * dt)
    den = lr * lr + li * li
    nr = ar - 1.0
    qr = (nr * lr + ai * li) / den
    qi = (ai * lr - nr * li) / den
    return dt, ar, ai, qr, qi, den


def _b_mask():
    r = lax.broadcasted_iota(jnp.int32, (S5_NS, LANES), 0)
    c = lax.broadcasted_iota(jnp.int32, (S5_NS, LANES), 1)
    return ((r >> 6) & 7) == (c >> 4)


def _c_mask():
    r = lax.broadcasted_iota(jnp.int32, (S5_W, 512), 0)
    c = lax.broadcasted_iota(jnp.int32, (S5_W, 512), 1)
    return ((r >> 4) & 7) == (c >> 6)


def _s5_prep(lam_r, ldt_r, lam_c, ldt_c, b_t, c_t, name):
    def body(lam_r_ref, ldt_r_ref, lam_c_ref, ldt_c_ref, b_ref, c_ref, tab_ref, bset_ref, cset_ref):
        lr, li = lam_r_ref[0:1, :], lam_r_ref[1:2, :]
        dt = jnp.exp(ldt_r_ref[...])
        mag = jnp.exp(lr * dt)
        p1r, p1i = mag * jnp.cos(li * dt), mag * jnp.sin(li * dt)
        p2r, p2i = _cmul(p1r, p1i, p1r, p1i)
        p3r, p3i = _cmul(p2r, p2i, p1r, p1i)
        p4r, p4i = _cmul(p2r, p2i, p2r, p2i)
        p5r, p5i = _cmul(p4r, p4i, p1r, p1i)
        p6r, p6i = _cmul(p4r, p4i, p2r, p2i)
        p7r, p7i = _cmul(p4r, p4i, p3r, p3i)
        p8r, p8i = _cmul(p4r, p4i, p4r, p4i)
        pw_r = [p1r, p2r, p3r, p4r, p5r, p6r, p7r, p8r]
        pw_i = [p1i, p2i, p3i, p4i, p5i, p6i, p7i, p8i]
        row = lax.broadcasted_iota(jnp.int32, (8, S5_NS), 0)
        zero = jnp.zeros((8, S5_NS), F32)

        def bc(v):
            return jnp.broadcast_to(v, (8, S5_NS))

        for d in range(2):
            sgn = 1.0 if d == 0 else -1.0
            for t, s in enumerate((1, 2, 4)):
                live = (row >= s) if d == 0 else (row <= 7 - s)
                tab_ref[d, 2 * t] = jnp.where(live, bc(pw_r[s - 1]), zero)
                tab_ref[d, 2 * t + 1] = jnp.where(live, bc(sgn * pw_i[s - 1]), zero)
            cr, ci = zero, zero
            for i in range(8):
                e = i if d == 0 else 7 - i
                cr = jnp.where(row == i, bc(pw_r[e]), cr)
                ci = jnp.where(row == i, bc(sgn * pw_i[e]), ci)
            tab_ref[d, 6] = cr
            tab_ref[d, 7] = ci

        _, _, _, qr, qi, _ = _zoh_cols(lam_c_ref[:, 0:1], lam_c_ref[:, 1:2], ldt_c_ref[...])
        bm = _b_mask()
        br, bi = b_ref[0], b_ref[1]
        bset_ref[0] = jnp.where(bm, qr * br - qi * bi, 0.0).astype(BF16)
        bset_ref[1] = jnp.where(bm, qr * bi + qi * br, 0.0).astype(BF16)
        cm = _c_mask()
        cset_ref[0] = jnp.where(cm, c_ref[0], 0.0).astype(BF16)
        cset_ref[1] = jnp.where(cm, c_ref[1], 0.0).astype(BF16)

    vm = pl.BlockSpec(memory_space=pltpu.VMEM)
    return pl.pallas_call(
        body, name=name, in_specs=[vm] * 6, out_specs=[vm] * 3,
        out_shape=[_sds((2, 8, 8, S5_NS)), _sds((2, S5_NS, LANES), BF16), _sds((2, S5_W, 512), BF16)],
        compiler_params=pltpu.CompilerParams(vmem_limit_bytes=VMEM_LIMIT),
    )(lam_r, ldt_r, lam_c, ldt_c, b_t, c_t)


SCAN_W = SCAN_GROUPS * LANES


def _scan_chunk(src_ref, dst_ref, tab_ref, carry_ref, nb, reverse, xs_ref=None, acc_ref=None):
    row = lax.broadcasted_iota(jnp.int32, (8, LANES), 0)

    def step(i, carry):
        b = (nb - 1 - i) if reverse else i
        off = pl.multiple_of(b * 8, 8)
        out = []
        for g in range(SCAN_GROUPS):
            lanes = pl.ds(g * LANES, LANES)
            cr, ci = carry[2 * g], carry[2 * g + 1]
            yr = src_ref[0, pl.ds(off, 8), lanes]
            yi = src_ref[1, pl.ds(off, 8), lanes]
            for t, s in enumerate((1, 2, 4)):
                sh = (8 - s) if reverse else s
                sr = pltpu.roll(yr, sh, 0)
                si = pltpu.roll(yi, sh, 0)
                mr, mi = tab_ref[2 * t, :, lanes], tab_ref[2 * t + 1, :, lanes]
                yr, yi = yr + mr * sr - mi * si, yi + mr * si + mi * sr
            pr, pi = tab_ref[6, :, lanes], tab_ref[7, :, lanes]
            yr, yi = yr + pr * cr - pi * ci, yi + pr * ci + pi * cr
            dst_ref[0, pl.ds(off, 8), lanes] = yr
            dst_ref[1, pl.ds(off, 8), lanes] = yi
            if xs_ref is not None:
                nr = jnp.where(row == 7, cr, pltpu.roll(yr, 7, 0))
                ni = jnp.where(row == 7, ci, pltpu.roll(yi, 7, 0))
                xr = xs_ref[0, pl.ds(off, 8), lanes]
                xi = xs_ref[1, pl.ds(off, 8), lanes]
                acc_ref[0, :, lanes] += xr * nr + xi * ni
                acc_ref[1, :, lanes] += xr * ni - xi * nr
            last = 0 if reverse else 7
            out += [jnp.broadcast_to(yr[last:last + 1, :], (8, LANES)),
                    jnp.broadcast_to(yi[last:last + 1, :], (8, LANES))]
        return tuple(out)

    init = []
    for g in range(SCAN_GROUPS):
        init += [carry_ref[0, :, pl.ds(g * LANES, LANES)], carry_ref[1, :, pl.ds(g * LANES, LANES)]]
    fin = lax.fori_loop(0, nb, step, tuple(init))
    for g in range(SCAN_GROUPS):
        carry_ref[0, :, pl.ds(g * LANES, LANES)] = fin[2 * g]
        carry_ref[1, :, pl.ds(g * LANES, LANES)] = fin[2 * g + 1]


def _s5_scan_fwd(z, bset, tabs, name):
    L = z.shape[0]
    tl = min(SCAN_CHUNK, L)
    nc = L // tl

    def body(u_ref, b_ref, tab_ref, x_ref, carry_ref):
        @pl.when(pl.program_id(1) == 0)
        def _():
            carry_ref[...] = jnp.zeros_like(carry_ref)

        u = u_ref[...].astype(BF16)
        x_ref[0] = _dot(u, b_ref[0], NT)
        x_ref[1] = _dot(u, b_ref[1], NT)
        _scan_chunk(x_ref, x_ref, tab_ref, carry_ref, tl // 8, False)

    return pl.pallas_call(
        body, name=name, grid=(S5_NS // SCAN_W, nc),
        in_specs=[pl.BlockSpec((tl, LANES), lambda j, c: (c, j)),
                  pl.BlockSpec((2, SCAN_W, LANES), lambda j, c: (0, j, 0)),
                  pl.BlockSpec((None, 8, 8, SCAN_W), lambda j, c: (0, 0, 0, j))],
        out_specs=pl.BlockSpec((2, tl, SCAN_W), lambda j, c: (0, c, j)),
        out_shape=_sds((2, L, S5_NS)),
        scratch_shapes=[pltpu.VMEM((2, 8, SCAN_W), F32)],
        compiler_params=_cp("parallel", "arbitrary"),
    )(z, bset, tabs)


def _s5_scan_bwd(gyl, cset, xs, z, bset, gud, tabs, name):
    L = z.shape[0]
    tl = min(SCAN_CHUNK, L)
    nc = L // tl

    def body(g_ref, c_ref, xs_ref, u_ref, b_ref, gud_ref, tab_ref, gu_ref, ga_ref, gb_ref, gc_ref,
             gx_ref, carry_ref, acc_ref):
        c = pl.program_id(1)

        @pl.when(c == 0)
        def _():
            carry_ref[...] = jnp.zeros_like(carry_ref)
            acc_ref[...] = jnp.zeros_like(acc_ref)
            gb_ref[...] = jnp.zeros_like(gb_ref)
            gc_ref[...] = jnp.zeros_like(gc_ref)

        gy = g_ref[...].astype(BF16)
        gx_ref[0] = _dot(gy, c_ref[0])
        gx_ref[1] = -_dot(gy, c_ref[1])
        gc_ref[0] += _dot(gy, xs_ref[0].astype(BF16), TN)
        gc_ref[1] -= _dot(gy, xs_ref[1].astype(BF16), TN)
        _scan_chunk(gx_ref, gx_ref, tab_ref, carry_ref, tl // 8, True, xs_ref, acc_ref)
        gr = gx_ref[0].astype(BF16)
        gi = gx_ref[1].astype(BF16)
        gu_ref[...] = gud_ref[...] + _dot(gr, b_ref[0]) + _dot(gi, b_ref[1])
        u = u_ref[...].astype(BF16)
        gb_ref[0] += _dot(gr, u, TN)
        gb_ref[1] += _dot(gi, u, TN)

        @pl.when(c == nc - 1)
        def _():
            ga_ref[0:1, :] = jnp.sum(acc_ref[0], axis=0, keepdims=True)
            ga_ref[1:2, :] = jnp.sum(acc_ref[1], axis=0, keepdims=True)

    rev = lambda j, c: (nc - 1 - c, j)
    col = pl.BlockSpec((tl, LANES), rev)
    return pl.pallas_call(
        body, name=name, grid=(S5_NS // SCAN_W, nc),
        in_specs=[col, pl.BlockSpec((2, LANES, SCAN_W), lambda j, c: (0, j, 0)),
                  pl.BlockSpec((2, tl, SCAN_W), lambda j, c: (0, nc - 1 - c, j)), col,
                  pl.BlockSpec((2, SCAN_W, LANES), lambda j, c: (0, j, 0)), col,
                  pl.BlockSpec((None, 8, 8, SCAN_W), lambda j, c: (1, 0, 0, j))],
        out_specs=[col, pl.BlockSpec((2, SCAN_W), lambda j, c: (0, j)),
                   pl.BlockSpec((2, SCAN_W, LANES), lambda j, c: (0, j, 0)),
                   pl.BlockSpec((2, LANES, SCAN_W), lambda j, c: (0, j, 0))],
        out_shape=[_sds((L, S5_W)), _sds((2, S5_NS)), _sds((2, S5_NS, LANES)), _sds((2, S5_W, 512))],
        scratch_shapes=[pltpu.VMEM((2, tl, SCAN_W), F32), pltpu.VMEM((2, 8, SCAN_W), F32),
                        pltpu.VMEM((2, 8, SCAN_W), F32)],
        compiler_params=_cp("parallel", "arbitrary"),
    )(gyl, cset, xs, z, bset, gud, tabs)


def _s5_out_fwd(xs, cset, z, dvec, wglu, name):
    L = z.shape[0]
    bl = min(256, L)

    def body(x_ref, c_ref, u_ref, d_ref, w_ref, ylin_ref, ya_ref):
        cols = []
        for j in range(4):
            xr = x_ref[0, :, 512 * j:512 * (j + 1)].astype(BF16)
            xi = x_ref[1, :, 512 * j:512 * (j + 1)].astype(BF16)
            cr = c_ref[0, LANES * j:LANES * (j + 1), :]
            ci = c_ref[1, LANES * j:LANES * (j + 1), :]
            cols.append(_dot(xr, cr, NT) - _dot(xi, ci, NT))
        ylin = jnp.concatenate(cols, axis=1) + d_ref[...] * u_ref[...]
        yg = _gelu(ylin)
        t = _dot(yg.astype(BF16), w_ref[...])
        ylin_ref[...] = ylin
        ya_ref[...] = (yg * _sigmoid(t)).astype(BF16)

    return pl.pallas_call(
        body, name=name, grid=(L // bl,),
        in_specs=[pl.BlockSpec((2, bl, S5_NS), lambda i: (0, i, 0)),
                  pl.BlockSpec((2, S5_W, 512), lambda i: (0, 0, 0)),
                  pl.BlockSpec((bl, S5_W), lambda i: (i, 0)),
                  pl.BlockSpec((1, S5_W), lambda i: (0, 0)),
                  pl.BlockSpec((S5_W, S5_W), lambda i: (0, 0))],
        out_specs=[pl.BlockSpec((bl, S5_W), lambda i: (i, 0))] * 2,
        out_shape=[_sds((L, S5_W)), _sds((L, S5_W), BF16)],
        compiler_params=_cp("parallel"),
    )(xs, cset, z, dvec, wglu)


def _s5_glu_bwd(g_m, ylin, z, dvec, wglu, name):
    L = z.shape[0]
    bl = min(256, L)

    def body(g_ref, ylin_ref, u_ref, d_ref, w_ref, gyl_ref, gud_ref, gw_ref, gd_ref):
        i = pl.program_id(0)
        ylin = ylin_ref[...]
        yg = _gelu(ylin)
        ygb = yg.astype(BF16)
        sg = _sigmoid(_dot(ygb, w_ref[...]))
        gya = g_ref[...]
        gt = gya * yg * sg * (1.0 - sg)
        gtb = gt.astype(BF16)
        gyg = gya * sg + _dot(gtb, w_ref[...], NT)
        gyl = gyg * _gelu_grad(ylin)
        gyl_ref[...] = gyl
        gud_ref[...] = gyl * d_ref[...]

        @pl.when(i == 0)
        def _():
            gw_ref[...] = jnp.zeros_like(gw_ref)
            gd_ref[...] = jnp.zeros_like(gd_ref)

        gw_ref[...] += _dot(ygb, gtb, TN)
        gd_ref[...] += jnp.sum(gyl * u_ref[...], axis=0, keepdims=True)

    blk = pl.BlockSpec((bl, S5_W), lambda i: (i, 0))
    return pl.pallas_call(
        body, name=name, grid=(L // bl,),
        in_specs=[blk, blk, blk, pl.BlockSpec((1, S5_W), lambda i: (0, 0)),
                  pl.BlockSpec((S5_W, S5_W), lambda i: (0, 0))],
        out_specs=[blk, blk, pl.BlockSpec((S5_W, S5_W), lambda i: (0, 0)), pl.BlockSpec((1, S5_W), lambda i: (0, 0))],
        out_shape=[_sds((L, S5_W)), _sds((L, S5_W)), _sds((S5_W, S5_W)), _sds((1, S5_W))],
        compiler_params=_cp("arbitrary"),
    )(g_m, ylin, z, dvec, wglu)


def _s5_param_bwd(lam_c, ldt_c, b_t, gb, ga_c, gc, name):
    def body(lam_ref, ldt_ref, b_ref, gb_ref, ga_ref, gc_ref, glam_ref, gldt_ref, gbo_ref, gco_ref):
        lr, li = lam_ref[:, 0:1], lam_ref[:, 1:2]
        dt, ar, ai, qr, qi, den = _zoh_cols(lr, li, ldt_ref[...])
        bm = _b_mask()
        gbr = jnp.where(bm, gb_ref[0], 0.0)
        gbi = jnp.where(bm, gb_ref[1], 0.0)
        br, bi = b_ref[0], b_ref[1]
        obr = gbr * qr + gbi * qi
        obi = gbi * qr - gbr * qi
        gqr = jnp.sum(gbr * br + gbi * bi, axis=1, keepdims=True)
        gqi = jnp.sum(gbi * br - gbr * bi, axis=1, keepdims=True)
        for s in (64, 32, 16):
            obr = obr + pltpu.roll(obr, s, 1)
            obi = obi + pltpu.roll(obi, s, 1)
        gbo_ref[0] = obr
        gbo_ref[1] = obi
        gar = ga_ref[:, 0:1] + (gqr * lr - gqi * li) / den
        gai = ga_ref[:, 1:2] + (gqr * li + gqi * lr) / den
        qlr = (qr * lr + qi * li) / den
        qli = (qi * lr - qr * li) / den
        glr = -(gqr * qlr + gqi * qli)
        gli = -(gqi * qlr - gqr * qli)
        glr = glr + dt * (gar * ar + gai * ai)
        gli = gli + dt * (gai * ar - gar * ai)
        wr, wi = _cmul(lr, li, ar, ai)
        gldt = (gar * wr + gai * wi) * dt
        glam_ref[:, 0:1] = glr
        glam_ref[:, 1:2] = gli
        r = lax.broadcasted_iota(jnp.int32, (S5_NS, 32), 0)
        c = lax.broadcasted_iota(jnp.int32, (S5_NS, 32), 1)
        gldt_ref[...] = jnp.sum(jnp.where((r >> 6) == c, gldt, 0.0), axis=0, keepdims=True)
        cm = _c_mask()
        for k in range(2):
            oc = jnp.where(cm, gc_ref[k], 0.0)
            for s in (256, 128, 64):
                oc = oc + pltpu.roll(oc, s, 1)
            gco_ref[k] = oc[:, 0:LANES]

    vm = pl.BlockSpec(memory_space=pltpu.VMEM)
    return pl.pallas_call(
        body, name=name, in_specs=[vm] * 6, out_specs=[vm] * 4,
        out_shape=[_sds((S5_NS, 2)), _sds((1, 32)), _sds((2, S5_NS, LANES)), _sds((2, S5_W, LANES))],
        compiler_params=pltpu.CompilerParams(vmem_limit_bytes=VMEM_LIMIT),
    )(lam_c, ldt_c, b_t, gb, ga_c, gc)


FL_BLK = EVEN_PAD // LANES - 1
Q_BLK, K_BLK, V_BLK = 4, 8, 12
NEG = -1e30


def _log_sigmoid(v):
    return jnp.minimum(v, 0.0) - jnp.log(1.0 + jnp.exp(-jnp.abs(v)))


def _fox_f_fwd(z, bf, name):
    L = z.shape[0]
    tl = min(256, L)

    def body(fl_ref, b_ref, f_ref, fq_ref, carry_ref):
        i = pl.program_id(0)

        @pl.when(i == 0)
        def _():
            carry_ref[...] = jnp.zeros_like(carry_ref)

        lf = _log_sigmoid(fl_ref[...] + b_ref[...])
        r = lax.broadcasted_iota(jnp.int32, (tl, tl), 0)
        c = lax.broadcasted_iota(jnp.int32, (tl, tl), 1)
        tri = (r >= c).astype(F32)
        cs = lax.dot_general(tri, lf, NN, precision=lax.Precision.HIGHEST, preferred_element_type=F32) + carry_ref[...]
        f_ref[...] = cs
        carry_ref[...] = cs[tl - 1:tl, :]
        expand = (lax.broadcasted_iota(jnp.int32, (LANES, FOX_W), 0)
                  == (lax.broadcasted_iota(jnp.int32, (LANES, FOX_W), 1) >> 6)).astype(F32)
        fq_ref[...] = lax.dot_general(cs, expand, NN, precision=lax.Precision.HIGHEST, preferred_element_type=F32)

    return pl.pallas_call(
        body, name=name, grid=(L // tl,),
        in_specs=[pl.BlockSpec((tl, LANES), lambda i: (i, FL_BLK)), pl.BlockSpec((1, LANES), lambda i: (0, 0))],
        out_specs=[pl.BlockSpec((tl, LANES), lambda i: (i, 0)), pl.BlockSpec((tl, FOX_W), lambda i: (i, 0))],
        out_shape=[_sds((L, LANES)), _sds((L, FOX_W))],
        scratch_shapes=[pltpu.VMEM((1, LANES), F32)],
        compiler_params=_cp("arbitrary"),
    )(z, bf)


def _fox_f_bwd(dFk, dfq, z, bf, name):
    L = z.shape[0]
    tl = min(256, L)
    nb = L // tl

    def body(dfk_ref, dfq_ref, fl_ref, b_ref, dfl_ref, db_ref, carry_ref):
        i = pl.program_id(0)

        @pl.when(i == 0)
        def _():
            carry_ref[...] = jnp.zeros_like(carry_ref)
            db_ref[...] = jnp.zeros_like(db_ref)

        sel = (lax.broadcasted_iota(jnp.int32, (FOX_W, LANES), 0)
               == 64 * lax.broadcasted_iota(jnp.int32, (FOX_W, LANES), 1)).astype(F32)
        dfq_h = lax.dot_general(dfq_ref[...], sel, NN, precision=lax.Precision.HIGHEST, preferred_element_type=F32)
        r = lax.broadcasted_iota(jnp.int32, (tl, tl), 0)
        c = lax.broadcasted_iota(jnp.int32, (tl, tl), 1)
        tri = (r <= c).astype(F32)
        cs = lax.dot_general(tri, dfk_ref[...] + dfq_h, NN, precision=lax.Precision.HIGHEST,
                             preferred_element_type=F32) + carry_ref[...]
        carry_ref[...] = cs[0:1, :]
        dfl = cs * _sigmoid(-(fl_ref[...] + b_ref[...]))
        dfl_ref[...] = dfl
        db_ref[...] += jnp.sum(dfl, axis=0, keepdims=True)

    return pl.pallas_call(
        body, name=name, grid=(nb,),
        in_specs=[pl.BlockSpec((tl, LANES), lambda i: (nb - 1 - i, 0)),
                  pl.BlockSpec((tl, FOX_W), lambda i: (nb - 1 - i, 0)),
                  pl.BlockSpec((tl, LANES), lambda i: (nb - 1 - i, FL_BLK)),
                  pl.BlockSpec((1, LANES), lambda i: (0, 0))],
        out_specs=[pl.BlockSpec((tl, LANES), lambda i: (nb - 1 - i, 0)), pl.BlockSpec((1, LANES), lambda i: (0, 0))],
        out_shape=[_sds((L, LANES)), _sds((1, LANES))],
        scratch_shapes=[pltpu.VMEM((1, LANES), F32)],
        compiler_params=_cp("arbitrary"),
    )(dFk, dfq, z, bf)


def _head_mask(hh):
    lane = lax.broadcasted_iota(jnp.int32, (1, LANES), 1)
    return (lane >> 6) == hh


FOX_T = 512


def _fox_head(x, hh):
    return jnp.where(_head_mask(hh), x, 0.0).astype(BF16)


def _fox_scores(qh, k, fq_ref, fr_ref, hh, causal):
    s = _dot(qh, k, NT) + (fq_ref[:, 64 * hh:64 * hh + 1] - fr_ref[hh:hh + 1, :])
    return s if causal is None else jnp.where(causal, s, NEG)


def _causal(T):
    return lax.broadcasted_iota(jnp.int32, (T, T), 1) <= lax.broadcasted_iota(jnp.int32, (T, T), 0)


def _fox_fwd(z, fq, frow, name):
    L = z.shape[0]
    T = min(FOX_T, L)
    nq = L // T

    def body(q_ref, k_ref, v_ref, fq_ref, fr_ref, o_ref, lse_ref, m_ref, l_ref, acc_ref):
        qi, ki = pl.program_id(1), pl.program_id(2)

        @pl.when(ki == 0)
        def _():
            m_ref[...] = jnp.full_like(m_ref, NEG)
            l_ref[...] = jnp.zeros_like(l_ref)
            acc_ref[...] = jnp.zeros_like(acc_ref)

        def step(diagonal):
            q = q_ref[...] * 0.125
            k = k_ref[...].astype(BF16)
            v = v_ref[...].astype(BF16)
            causal = _causal(T) if diagonal else None
            s = jnp.concatenate([_fox_scores(_fox_head(q, hh), k, fq_ref, fr_ref, hh, causal) for hh in range(2)],
                                axis=0)
            m_old = m_ref[...]
            m_new = jnp.maximum(m_old, jnp.max(s, axis=1, keepdims=True))
            alpha = jnp.exp(m_old - m_new)
            p = jnp.exp(s - m_new)
            l_ref[...] = alpha * l_ref[...] + jnp.sum(p, axis=1, keepdims=True)
            m_ref[...] = m_new
            acc_ref[...] = alpha * acc_ref[...] + _dot(p.astype(BF16), v)

        @pl.when(ki < qi)
        def _():
            step(False)

        @pl.when(ki == qi)
        def _():
            step(True)

        @pl.when(ki == nq - 1)
        def _():
            h0 = _head_mask(0)
            l = l_ref[...]
            o_h = acc_ref[...] / l
            lse_h = m_ref[...] + jnp.log(l)
            o_ref[...] = jnp.where(h0, o_h[:T], o_h[T:])
            lse_ref[...] = jnp.where(h0, lse_h[:T], lse_h[T:])

    def zspec(base, rowf):
        return pl.BlockSpec((T, LANES), lambda j, qi, ki: (rowf(qi, ki), base + j))

    kv_row = lambda qi, ki: jnp.minimum(ki, qi)
    q_row = lambda qi, ki: qi
    return pl.pallas_call(
        body, name=name, grid=(4, nq, nq),
        in_specs=[zspec(Q_BLK, q_row), zspec(K_BLK, kv_row), zspec(V_BLK, kv_row),
                  pl.BlockSpec((T, LANES), lambda j, qi, ki: (qi, j)),
                  pl.BlockSpec((None, 2, T), lambda j, qi, ki: (j, 0, jnp.minimum(ki, qi)))],
        out_specs=[pl.BlockSpec((T, LANES), lambda j, qi, ki: (qi, j))] * 2,
        out_shape=[_sds((L, FOX_W)), _sds((L, FOX_W))],
        scratch_shapes=[pltpu.VMEM((2 * T, 1), F32), pltpu.VMEM((2 * T, 1), F32), pltpu.VMEM((2 * T, LANES), F32)],
        compiler_params=_cp("parallel", "parallel", "arbitrary"),
    )(z, z, z, fq, frow)


def _fox_bwd(z, fq, frow, o, lse, g_m, name):
    L = z.shape[0]
    T = min(FOX_T, L)
    nq = L // T

    def body(q_ref, k_ref, v_ref, fq_ref, fr_ref, o_ref, lse_ref, do_ref,
             dq_ref, dk_ref, dv_ref, dfq_ref, dfk_ref, dk_acc, dv_acc, df_acc):
        ki, qi = pl.program_id(1), pl.program_id(2)

        @pl.when((ki == 0) & (qi == 0))
        def _():
            dq_ref[...] = jnp.zeros_like(dq_ref)
            dfq_ref[...] = jnp.zeros_like(dfq_ref)

        @pl.when(qi == 0)
        def _():
            dk_acc[...] = jnp.zeros_like(dk_acc)
            dv_acc[...] = jnp.zeros_like(dv_acc)
            df_acc[...] = jnp.zeros_like(df_acc)

        def step(diagonal):
            q = q_ref[...] * 0.125
            qb = q.astype(BF16)
            k = k_ref[...].astype(BF16)
            v = v_ref[...].astype(BF16)
            do = do_ref[...]
            dob = do.astype(BF16)
            do_o = dob.astype(F32) * o_ref[...]
            causal = _causal(T) if diagonal else None
            dvs, dks, dqs, rss = [], [], [], []
            for hh in range(2):
                s = _fox_scores(_fox_head(q, hh), k, fq_ref, fr_ref, hh, causal)
                p = jnp.exp(s - lse_ref[:, 64 * hh:64 * hh + 1])
                dp = _dot(_fox_head(do, hh), v, NT)
                delta = jnp.sum(jnp.where(_head_mask(hh), do_o, 0.0), axis=1, keepdims=True)
                ds = p * (dp - delta)
                dsb = ds.astype(BF16)
                dvs.append(_dot(p.astype(BF16), dob, TN))
                dks.append(_dot(dsb, qb, TN))
                dqs.append(_dot(dsb, k))
                rss.append(jnp.sum(ds, axis=1, keepdims=True))
                df_acc[hh:hh + 1, :] -= jnp.sum(ds, axis=0, keepdims=True)
            h0 = _head_mask(0)
            dv_acc[...] += jnp.where(h0, dvs[0], dvs[1])
            dk_acc[...] += jnp.where(h0, dks[0], dks[1])
            rows = pl.ds(pl.multiple_of(qi * T, T), T)
            dq_ref[rows, :] += jnp.where(h0, dqs[0], dqs[1])
            dfq_ref[rows, :] += jnp.where(h0, rss[0], rss[1])

        @pl.when(qi > ki)
        def _():
            step(False)

        @pl.when(qi == ki)
        def _():
            step(True)

        @pl.when(qi == nq - 1)
        def _():
            dk_ref[...] = dk_acc[...]
            dv_ref[...] = dv_acc[...]
            dfk_ref[...] = df_acc[...]

        @pl.when((ki == nq - 1) & (qi == nq - 1))
        def _():
            dq_ref[...] = dq_ref[...] * 0.125

    q_row = lambda ki, qi: jnp.maximum(qi, ki)

    def qside(base):
        return pl.BlockSpec((T, LANES), lambda j, ki, qi: (q_row(ki, qi), base + j))

    def kside(base):
        return pl.BlockSpec((T, LANES), lambda j, ki, qi: (ki, base + j))

    kblk = pl.BlockSpec((T, LANES), lambda j, ki, qi: (ki, j))
    pair = pl.BlockSpec((L, LANES), lambda j, ki, qi: (0, j))
    frow_spec = pl.BlockSpec((None, 2, T), lambda j, ki, qi: (j, 0, ki))
    return pl.pallas_call(
        body, name=name, grid=(4, nq, nq),
        in_specs=[qside(Q_BLK), kside(K_BLK), kside(V_BLK), qside(0), frow_spec, qside(0), qside(0), qside(4)],
        out_specs=[pair, kblk, kblk, pair, frow_spec],
        out_shape=[_sds((L, FOX_W)), _sds((L, FOX_W)), _sds((L, FOX_W)), _sds((L, FOX_W)), _sds((4, 2, L))],
        scratch_shapes=[pltpu.VMEM((T, LANES), F32), pltpu.VMEM((T, LANES), F32), pltpu.VMEM((2, T), F32)],
        compiler_params=_cp("parallel", "arbitrary", "arbitrary"),
    )(z, z, z, fq, frow, o, lse, g_m)


def _shift_rows(v, s, down, row):
    n = v.shape[0]
    if down:
        return jnp.where(row >= s, pltpu.roll(v, s, 0), 0.0)
    return jnp.where(row < n - s, pltpu.roll(v, n - s, 0), 0.0)


def _window_sum(v, g, down, row):
    out = jnp.zeros_like(v)
    s = v
    for k in range(4):
        s = s + _shift_rows(s, 1 << k, down, row)
        out = jnp.where(g == k, s, out)
    return out


def _pool_inv_cnt(g, row):
    w = jnp.left_shift(2, g).astype(F32)
    return 1.0 / jnp.minimum(row.astype(F32) + 1.0, w)


def _pool_fwd(z, pool_w, scale, name):
    L = z.shape[0]

    def body(x_ref, w_ref, s_ref, y_ref, p_ref):
        g = pl.program_id(0)
        row = lax.broadcasted_iota(jnp.int32, (L, LANES), 0)
        x = x_ref[...]
        pooled = (_window_sum(x, g, True, row) * _pool_inv_cnt(g, row) - x).astype(BF16)
        p_ref[...] = pooled
        y_ref[...] = (_dot(pooled, w_ref[...].astype(BF16)) * s_ref[...]).astype(BF16)

    col = pl.BlockSpec((L, LANES), lambda g: (0, g))
    return pl.pallas_call(
        body, name=name, grid=(4,),
        in_specs=[col, pl.BlockSpec((None, LANES, LANES), lambda g: (g, 0, 0)), pl.BlockSpec((1, LANES), lambda g: (0, g))],
        out_specs=[col, col],
        out_shape=[_sds((L, 512), BF16), _sds((L, 512), BF16)],
        compiler_params=_cp("parallel"),
    )(z, pool_w, scale)


def _pool_bwd(g_m, pooled, pool_w, scale, name):
    L = g_m.shape[0]

    def body(g_ref, p_ref, w_ref, s_ref, gx_ref, gw_ref, gs_ref):
        g = pl.program_id(0)
        row = lax.broadcasted_iota(jnp.int32, (L, LANES), 0)
        gy = g_ref[...]
        pooled = p_ref[...]
        wb = w_ref[...].astype(BF16)
        lin = _dot(pooled, wb)
        gs_ref[...] = jnp.sum(gy * lin, axis=0, keepdims=True)
        glin = (gy * s_ref[...]).astype(BF16)
        gw_ref[...] = _dot(pooled, glin, TN)
        gp = _dot(glin, wb, NT)
        gx_ref[...] = _window_sum(gp * _pool_inv_cnt(g, row), g, False, row) - gp

    col = pl.BlockSpec((L, LANES), lambda g: (0, g))
    wspec = pl.BlockSpec((None, LANES, LANES), lambda g: (g, 0, 0))
    vec = pl.BlockSpec((1, LANES), lambda g: (0, g))
    return pl.pallas_call(
        body, name=name, grid=(4,),
        in_specs=[col, col, wspec, vec],
        out_specs=[col, wspec, vec],
        out_shape=[_sds((L, 512)), _sds((4, LANES, LANES)), _sds((1, 512))],
        compiler_params=_cp("parallel"),
    )(g_m, pooled, pool_w, scale)


SGU_CHUNKS = 4


def _sgu_ln(v, gam, bet):
    gv = _gelu(v)
    mu = jnp.mean(gv, axis=-1, keepdims=True)
    xc = gv - mu
    rs = lax.rsqrt(jnp.mean(xc * xc, axis=-1, keepdims=True) + EPS)
    xh = xc * rs
    return xh, rs, xh * gam + bet


def _tril_ws(w_ref, g):
    r = lax.broadcasted_iota(jnp.int32, (LANES, LANES), 0)
    c = lax.broadcasted_iota(jnp.int32, (LANES, LANES), 1)
    return jnp.where(r >= c, w_ref[g], 0.0).astype(BF16)


def _sgu_fwd(z, ln_g, ln_b, w_s, b_st, name):
    L = z.shape[0]
    rb = min(SGU_CHUNKS * LANES, L)

    def body(u_ref, v_ref, g_ref, b_ref, w_ref, bs_ref, y_ref):
        _, _, vln = _sgu_ln(v_ref[...], g_ref[...], b_ref[...])
        gu = _gelu(u_ref[...])
        vb = vln.astype(BF16)
        for g in range(4):
            ws = _tril_ws(w_ref, g)
            for n in range(rb // LANES):
                rows = slice(n * LANES, (n + 1) * LANES)
                cols = slice(g * LANES, (g + 1) * LANES)
                mixed = _dot(ws, vb[rows, cols]) + bs_ref[:, g:g + 1]
                y_ref[rows, cols] = (gu[rows, cols] * mixed).astype(BF16)

    vm = lambda shape: pl.BlockSpec(shape, lambda i: tuple(0 for _ in shape))
    return pl.pallas_call(
        body, name=name, grid=(L // rb,),
        in_specs=[pl.BlockSpec((rb, 512), lambda i: (i, 1)), pl.BlockSpec((rb, 512), lambda i: (i, 2)),
                  vm((1, 512)), vm((1, 512)), vm((4, LANES, LANES)), vm((LANES, 4))],
        out_specs=pl.BlockSpec((rb, 512), lambda i: (i, 0)),
        out_shape=_sds((L, 512), BF16),
        compiler_params=_cp("parallel"),
    )(z, z, ln_g, ln_b, w_s, b_st)


def _sgu_bwd(g_m, z, ln_g, ln_b, w_s, b_st, name):
    L = z.shape[0]
    rb = min(SGU_CHUNKS * LANES, L)

    def body(gy_ref, u_ref, v_ref, g_ref, b_ref, w_ref, bs_ref, gu_ref, gv_ref, gw_ref, gbs_ref, gg_ref, gb_ref):
        i = pl.program_id(0)

        @pl.when(i == 0)
        def _():
            gw_ref[...] = jnp.zeros_like(gw_ref)
            gbs_ref[...] = jnp.zeros_like(gbs_ref)
            gg_ref[...] = jnp.zeros_like(gg_ref)
            gb_ref[...] = jnp.zeros_like(gb_ref)

        v = v_ref[...]
        u = u_ref[...]
        gy = gy_ref[...]
        xh, rs, vln = _sgu_ln(v, g_ref[...], b_ref[...])
        gel_u = _gelu(u)
        gmix = gy * gel_u
        vb = vln.astype(BF16)
        gmb = gmix.astype(BF16)
        r = lax.broadcasted_iota(jnp.int32, (LANES, LANES), 0)
        c = lax.broadcasted_iota(jnp.int32, (LANES, LANES), 1)
        gvln_cols = []
        for g in range(4):
            ws = _tril_ws(w_ref, g)
            cols = slice(g * LANES, (g + 1) * LANES)
            gw = jnp.zeros((LANES, LANES), F32)
            gbs = jnp.zeros((LANES, 1), F32)
            parts = []
            for n in range(rb // LANES):
                rows = slice(n * LANES, (n + 1) * LANES)
                mixed = _dot(ws, vb[rows, cols]) + bs_ref[:, g:g + 1]
                gu_ref[rows, cols] = gy[rows, cols] * mixed * _gelu_grad(u[rows, cols])
                parts.append(_dot(ws, gmb[rows, cols], TN))
                gw = gw + _dot(gmb[rows, cols], vb[rows, cols], NT)
                gbs = gbs + jnp.sum(gmix[rows, cols], axis=1, keepdims=True)
            gvln_cols.append(jnp.concatenate(parts, axis=0))
            gw_ref[g] += jnp.where(r >= c, gw, 0.0)
            gbs_ref[:, g:g + 1] += gbs
        gvln = jnp.concatenate(gvln_cols, axis=1)
        gg_ref[...] += jnp.sum(gvln * xh, axis=0, keepdims=True)
        gb_ref[...] += jnp.sum(gvln, axis=0, keepdims=True)
        gxh = gvln * g_ref[...]
        ggv = rs * (gxh - jnp.mean(gxh, axis=-1, keepdims=True) - xh * jnp.mean(gxh * xh, axis=-1, keepdims=True))
        gv_ref[...] = ggv * _gelu_grad(v)

    vm = lambda shape: pl.BlockSpec(shape, lambda i: tuple(0 for _ in shape))
    blk = pl.BlockSpec((rb, 512), lambda i: (i, 0))
    return pl.pallas_call(
        body, name=name, grid=(L // rb,),
        in_specs=[pl.BlockSpec((rb, 512), lambda i: (i, 1)), pl.BlockSpec((rb, 512), lambda i: (i, 1)),
                  pl.BlockSpec((rb, 512), lambda i: (i, 2)),
                  vm((1, 512)), vm((1, 512)), vm((4, LANES, LANES)), vm((LANES, 4))],
        out_specs=[blk, blk, vm((4, LANES, LANES)), vm((LANES, 4)), vm((1, 512)), vm((1, 512))],
        out_shape=[_sds((L, 512)), _sds((L, 512)), _sds((4, LANES, LANES)), _sds((LANES, 4)),
                   _sds((1, 512)), _sds((1, 512))],
        compiler_params=_cp("arbitrary"),
    )(g_m, z, z, ln_g, ln_b, w_s, b_st)


def _adamw_math(w, g, m, v):
    nm = ADAM_B1 * m + (1.0 - ADAM_B1) * g
    nv = ADAM_B2 * v + (1.0 - ADAM_B2) * (g * g)
    m_hat = nm / (1.0 - ADAM_B1 ** ADAM_STEP)
    v_hat = nv / (1.0 - ADAM_B2 ** ADAM_STEP)
    delta = -ADAM_LR * (m_hat / (jnp.sqrt(v_hat) + ADAM_EPS) + ADAM_WD * w)
    return delta, nm, nv


def _sum_adamw(parts, w, m, v, name, layer=0, prev=None):
    n_layers, R, C = w.shape
    rb = 128 if R % 128 == 0 else R

    def body(p_ref, w_ref, m_ref, v_ref, *rest):
        g_ref, d_ref, nm_ref, nv_ref = rest[-4:]
        g = p_ref[0]
        for s in range(1, N_DEV):
            g = g + p_ref[s]
        d, nm, nv = _adamw_math(w_ref[...], g, m_ref[...], v_ref[...])
        g_ref[...] = g
        d_ref[...] = d
        nm_ref[...] = nm
        nv_ref[...] = nv

    blk = pl.BlockSpec((None, rb, C), lambda i: (layer, i, 0))
    prev = [] if prev is None else list(prev)
    return pl.pallas_call(
        body, name=name, grid=(R // rb,),
        in_specs=[pl.BlockSpec((N_DEV, rb, C), lambda i: (0, i, 0)), blk, blk, blk] + [ANY] * len(prev),
        out_specs=[blk] * 4, out_shape=[_sds((n_layers, R, C))] * 4,
        input_output_aliases={4 + k: k for k in range(len(prev))},
        compiler_params=_cp("parallel"),
    )(parts, w, m, v, *prev)


def _sum_pieces(parts, name):
    _, R, C = parts.shape

    def body(p_ref, g_ref):
        g = p_ref[0]
        for s in range(1, N_DEV):
            g = g + p_ref[s]
        g_ref[...] = g

    vm = pl.BlockSpec(memory_space=pltpu.VMEM)
    return pl.pallas_call(body, name=name, in_specs=[vm], out_specs=vm, out_shape=_sds((R, C)))(parts)


def _adamw(w, g, m, v, name):
    vm = pl.BlockSpec(memory_space=pltpu.VMEM)

    def body(w_ref, g_ref, m_ref, v_ref, d_ref, nm_ref, nv_ref):
        d, nm, nv = _adamw_math(w_ref[...], g_ref[...], m_ref[...], v_ref[...])
        d_ref[...] = d
        nm_ref[...] = nm
        nv_ref[...] = nv

    return pl.pallas_call(body, name=name, in_specs=[vm] * 4, out_specs=[vm] * 3,
                          out_shape=[_sds(w.shape)] * 3)(w, g, m, v)


def _mesh_pos():
    return lax.axis_index("x"), lax.axis_index("y"), lax.axis_index("c")


def _dev_index(p):
    return 4 * p[0] + 2 * p[1] + p[2]


def _all_gather(xs, name):
    n = len(xs)

    def body(*refs):
        x_refs, o_refs = refs[:n], refs[n:2 * n]
        send_sems, recv_sems, local_sems = refs[2 * n:]
        x, y, c = _mesh_pos()
        me, sibling = (x, y, c), (x, y, 1 - c)
        chips = [(1 - x, y), (x, 1 - y), (1 - x, 1 - y)]

        def copy(i, k, block, to, src=None):
            dst = o_refs[i].at[_dev_index(block)]
            return pltpu.make_async_remote_copy(
                src_ref=dst if src is None else src, dst_ref=dst,
                send_sem=send_sems.at[i, k], recv_sem=recv_sems.at[i, k], device_id=to, device_id_type=MESH)

        mine = [pltpu.make_async_copy(x_refs[i], o_refs[i].at[_dev_index(me)], local_sems.at[i]) for i in range(n)]
        for cp in mine:
            cp.start()
        first = []
        for i in range(n):
            first.append(copy(i, 0, me, sibling, src=x_refs[i]))
            first += [copy(i, 1 + j, me, (*chip, c), src=x_refs[i]) for j, chip in enumerate(chips)]
        for cp in first:
            cp.start()
        passed = []
        for j, chip in enumerate(chips):
            for i in range(n):
                copy(i, 1 + j, (*chip, c), me).wait_recv()
                fwd = copy(i, 4 + j, (*chip, c), sibling)
                fwd.start()
                passed.append(fwd)
        for i in range(n):
            copy(i, 0, sibling, me).wait_recv()
            for j, chip in enumerate(chips):
                copy(i, 4 + j, (*chip, 1 - c), me).wait_recv()
        for cp in first + passed:
            cp.wait_send()
        for cp in mine:
            cp.wait()

    outs = pl.pallas_call(
        body, name=name,
        in_specs=[ANY] * n, out_specs=[ANY] * n,
        out_shape=[_sds((N_DEV,) + x.shape, x.dtype) for x in xs],
        scratch_shapes=[pltpu.SemaphoreType.DMA((n, 7)), pltpu.SemaphoreType.DMA((n, 7)),
                        pltpu.SemaphoreType.DMA((n,))],
    )(*xs)
    return list(outs)


def _exchange(gs, name):
    n = len(gs)

    def body(*refs):
        g_refs, o_refs = refs[:n], refs[n:2 * n]
        send_sems, recv_sems, local_sems = refs[2 * n:]
        x, y, c = _mesh_pos()
        me = (x, y, c)
        mi = _dev_index(me)
        peers = [(x ^ dx, y ^ dy, c ^ dc) for dx in range(2) for dy in range(2) for dc in range(2)][1:]

        def copy(i, k, peer):
            return pltpu.make_async_remote_copy(
                src_ref=g_refs[i].at[_dev_index(peer)], dst_ref=o_refs[i].at[mi],
                send_sem=send_sems.at[i, k], recv_sem=recv_sems.at[i, k], device_id=peer, device_id_type=MESH)

        mine = [pltpu.make_async_copy(g_refs[i].at[mi], o_refs[i].at[mi], local_sems.at[i]) for i in range(n)]
        for cp in mine:
            cp.start()
        sends = [copy(i, k, peer) for i in range(n) for k, peer in enumerate(peers)]
        for cp in sends:
            cp.start()
        for i in range(n):
            for k, peer in enumerate(peers):
                pltpu.make_async_remote_copy(
                    src_ref=g_refs[i].at[mi], dst_ref=o_refs[i].at[_dev_index(peer)],
                    send_sem=send_sems.at[i, k], recv_sem=recv_sems.at[i, k], device_id=peer,
                    device_id_type=MESH).wait_recv()
        for cp in sends:
            cp.wait_send()
        for cp in mine:
            cp.wait()

    outs = pl.pallas_call(
        body, name=name,
        in_specs=[ANY] * n, out_specs=[ANY] * n,
        out_shape=[_sds(g.shape, g.dtype) for g in gs],
        scratch_shapes=[pltpu.SemaphoreType.DMA((n, 7)), pltpu.SemaphoreType.DMA((n, 7)),
                        pltpu.SemaphoreType.DMA((n,))],
    )(*gs)
    return list(outs)


HBM = pl.BlockSpec(memory_space=pltpu.HBM)
SEM = pl.BlockSpec(memory_space=pltpu.SEMAPHORE)
EFFECT = pltpu.SideEffectType.DATAFLOW_SIDE_EFFECTING


def _peer_list():
    x, y, c = _mesh_pos()
    peers = [(x ^ dx, y ^ dy, c ^ dc) for dx in range(2) for dy in range(2) for dc in range(2)][1:]
    return (x, y, c), peers


def _split_copy(src_ref, land_ref, send_sems, recv_sems, i, k, peer, slot, exchange):
    return pltpu.make_async_remote_copy(
        src_ref=src_ref.at[_dev_index(peer)] if exchange else src_ref, dst_ref=land_ref.at[slot],
        send_sem=send_sems.at[7 * i + k], recv_sem=recv_sems.at[7 * i + k], device_id=peer, device_id_type=MESH)


def _comm_start(groups, name, exchange, dep=None):
    sizes = [len(g) for g in groups]
    n = sum(sizes)
    srcs = [a for g in groups for a in g]
    my_index = _dev_index(_mesh_pos())
    lands = []
    for a in srcs:
        if exchange:
            own = lax.dynamic_slice(a, (my_index, 0, 0), (1,) + a.shape[1:])
            shape = a.shape
        else:
            own = a[None]
            shape = (N_DEV,) + a.shape
        lands.append(lax.dynamic_update_slice(lax.empty(shape, a.dtype), own, (my_index, 0, 0)))

    n_dep = 0 if dep is None else 1

    def body(*refs):
        src_refs, land_refs = refs[:n], refs[n:2 * n]
        sem_refs = refs[2 * n + n_dep:2 * n + n_dep + 2 * len(sizes)]
        token_ref = refs[-1]
        me, peers = _peer_list()
        mi = _dev_index(me)
        i = 0
        for gi, sz in enumerate(sizes):
            for j in range(sz):
                for k, peer in enumerate(peers):
                    _split_copy(src_refs[i], land_refs[i], sem_refs[2 * gi], sem_refs[2 * gi + 1], j, k, peer, mi,
                                exchange).start()
                i += 1
        token_ref[...] = jnp.zeros_like(token_ref)

    sem_shapes = []
    for sz in sizes:
        sem_shapes += [pltpu.SemaphoreType.DMA((7 * sz,)), pltpu.SemaphoreType.DMA((7 * sz,))]
    thru = [pltpu.HBM(a.shape, a.dtype) for a in srcs + lands]
    n_sem = len(sem_shapes)
    outs = pl.pallas_call(
        body, name=name,
        out_shape=tuple(sem_shapes + thru + [_sds((8, LANES))]),
        in_specs=[HBM] * (2 * n) + [ANY] * n_dep,
        out_specs=tuple([SEM] * n_sem + [HBM] * (2 * n) + [pl.BlockSpec(memory_space=pltpu.VMEM)]),
        input_output_aliases={i: n_sem + i for i in range(2 * n)},
        compiler_params=pltpu.CompilerParams(has_side_effects=EFFECT),
    )(*[pltpu.with_memory_space_constraint(a, pltpu.HBM) for a in srcs + lands], *([] if dep is None else [dep]))
    sems, thru_src, thru_land, token = outs[:n_sem], outs[n_sem:n_sem + n], outs[n_sem + n:n_sem + 2 * n], outs[-1]
    result, off = [], 0
    for gi, sz in enumerate(sizes):
        result.append((sems[2 * gi], sems[2 * gi + 1], list(thru_src[off:off + sz]), list(thru_land[off:off + sz])))
        off += sz
    return result, token


def _comm_wait(group, after, name, exchange):
    send_sems, recv_sems, srcs, lands = group
    n = len(srcs)
    after = list(after) if isinstance(after, (list, tuple)) else [after]

    def body(*refs):
        src_refs, land_refs = refs[:n], refs[n:2 * n]
        ssem, rsem = refs[2 * n], refs[2 * n + 1]
        me, peers = _peer_list()
        for i in range(n):
            for k, peer in enumerate(peers):
                cp = _split_copy(src_refs[i], land_refs[i], ssem, rsem, i, k, peer, _dev_index(peer), exchange)
                cp.wait_send()
                cp.wait_recv()

    outs = pl.pallas_call(
        body, name=name,
        out_shape=tuple(pltpu.HBM(a.shape, a.dtype) for a in srcs + lands),
        in_specs=[HBM] * (2 * n) + [SEM, SEM] + [ANY] * len(after),
        out_specs=tuple([HBM] * (2 * n)),
        input_output_aliases={i: i for i in range(2 * n)},
        compiler_params=pltpu.CompilerParams(has_side_effects=EFFECT),
    )(*srcs, *lands, send_sems, recv_sems, *after)
    return list(outs[n:])


def _tie(a, token):
    return a + token[0, 0].astype(a.dtype)


def _pack(arrs, rows):
    flat = jnp.concatenate([a.reshape(-1).astype(F32) for a in arrs])
    return jnp.pad(flat, (0, rows * LANES - flat.shape[0])).reshape(rows, LANES)


def _unpack(packed, shapes):
    flat = packed.reshape(-1)
    out, off = [], 0
    for s in shapes:
        n = math.prod(s)
        out.append(flat[off:off + n].reshape(s))
        off += n
    return out


def _packed_rows(shapes):
    n = sum(math.prod(s) for s in shapes)
    unit = N_DEV * 8 * LANES
    return -(-n // unit) * unit // LANES


def kernel(x, mix_pre_g, mix_post_g, mlp_pre_g, mlp_post_g, w_in_even, s5_lam_re, s5_lam_im, s5_log_dt, s5_b_re, s5_b_im, s5_c_re, s5_c_im, s5_d, s5_w_glu, fox_b_f, w_out_even, w_in_odd, pool_w, pool_scale, sgu_ln_g, sgu_ln_b, sgu_w_s, sgu_b_s, w_out_odd, mlp_w1, mlp_w2, loss_target, m_mix_pre_g, m_mix_post_g, m_mlp_pre_g, m_mlp_post_g, m_w_in_even, m_s5_lam_re, m_s5_lam_im, m_s5_log_dt, m_s5_b_re, m_s5_b_im, m_s5_c_re, m_s5_c_im, m_s5_d, m_s5_w_glu, m_fox_b_f, m_w_out_even, m_w_in_odd, m_pool_w, m_pool_scale, m_sgu_ln_g, m_sgu_ln_b, m_sgu_w_s, m_sgu_b_s, m_w_out_odd, m_mlp_w1, m_mlp_w2, v_mix_pre_g, v_mix_post_g, v_mlp_pre_g, v_mlp_post_g, v_w_in_even, v_s5_lam_re, v_s5_lam_im, v_s5_log_dt, v_s5_b_re, v_s5_b_im, v_s5_c_re, v_s5_c_im, v_s5_d, v_s5_w_glu, v_fox_b_f, v_w_out_even, v_w_in_odd, v_pool_w, v_pool_scale, v_sgu_ln_g, v_sgu_ln_b, v_sgu_w_s, v_sgu_b_s, v_w_out_odd, v_mlp_w1, v_mlp_w2):
    weights = dict(mix_pre_g=mix_pre_g, mix_post_g=mix_post_g, mlp_pre_g=mlp_pre_g, mlp_post_g=mlp_post_g, w_in_even=w_in_even, s5_lam_re=s5_lam_re, s5_lam_im=s5_lam_im, s5_log_dt=s5_log_dt, s5_b_re=s5_b_re, s5_b_im=s5_b_im, s5_c_re=s5_c_re, s5_c_im=s5_c_im, s5_d=s5_d, s5_w_glu=s5_w_glu, fox_b_f=fox_b_f, w_out_even=w_out_even, w_in_odd=w_in_odd, pool_w=pool_w, pool_scale=pool_scale, sgu_ln_g=sgu_ln_g, sgu_ln_b=sgu_ln_b, sgu_w_s=sgu_w_s, sgu_b_s=sgu_b_s, w_out_odd=w_out_odd, mlp_w1=mlp_w1, mlp_w2=mlp_w2)
    mom_m = dict(mix_pre_g=m_mix_pre_g, mix_post_g=m_mix_post_g, mlp_pre_g=m_mlp_pre_g, mlp_post_g=m_mlp_post_g, w_in_even=m_w_in_even, s5_lam_re=m_s5_lam_re, s5_lam_im=m_s5_lam_im, s5_log_dt=m_s5_log_dt, s5_b_re=m_s5_b_re, s5_b_im=m_s5_b_im, s5_c_re=m_s5_c_re, s5_c_im=m_s5_c_im, s5_d=m_s5_d, s5_w_glu=m_s5_w_glu, fox_b_f=m_fox_b_f, w_out_even=m_w_out_even, w_in_odd=m_w_in_odd, pool_w=m_pool_w, pool_scale=m_pool_scale, sgu_ln_g=m_sgu_ln_g, sgu_ln_b=m_sgu_ln_b, sgu_w_s=m_sgu_w_s, sgu_b_s=m_sgu_b_s, w_out_odd=m_w_out_odd, mlp_w1=m_mlp_w1, mlp_w2=m_mlp_w2)
    mom_v = dict(mix_pre_g=v_mix_pre_g, mix_post_g=v_mix_post_g, mlp_pre_g=v_mlp_pre_g, mlp_post_g=v_mlp_post_g, w_in_even=v_w_in_even, s5_lam_re=v_s5_lam_re, s5_lam_im=v_s5_lam_im, s5_log_dt=v_s5_log_dt, s5_b_re=v_s5_b_re, s5_b_im=v_s5_b_im, s5_c_re=v_s5_c_re, s5_c_im=v_s5_c_im, s5_d=v_s5_d, s5_w_glu=v_s5_w_glu, fox_b_f=v_fox_b_f, w_out_even=v_w_out_even, w_in_odd=v_w_in_odd, pool_w=v_pool_w, pool_scale=v_pool_scale, sgu_ln_g=v_sgu_ln_g, sgu_ln_b=v_sgu_ln_b, sgu_w_s=v_sgu_w_s, sgu_b_s=v_sgu_b_s, w_out_odd=v_w_out_odd, mlp_w1=v_mlp_w1, mlp_w2=v_mlp_w2)
    names = list(weights)
    L = x.shape[1]
    x0 = x[0]
    target = loss_target[0]
    my_index = 4 * lax.axis_index("x") + 2 * lax.axis_index("y") + lax.axis_index("c")

    small_vec = jnp.zeros((8, LANES), F32)
    small_vec = small_vec.at[0, :64].set(pool_scale[0]).at[1, :64].set(sgu_ln_g[0]).at[2, :64].set(sgu_ln_b[0])
    ag_groups, ag_token = _comm_start(
        [[jnp.transpose(w_in_even[0]).astype(BF16), small_vec],
         [s5_w_glu[0].astype(BF16), w_out_even[0].astype(BF16)],
         [mlp_w1[0].astype(BF16), mlp_w2[0].astype(BF16)],
         [jnp.transpose(w_in_odd[0]).astype(BF16), w_out_odd[0].astype(BF16), mlp_w1[1].astype(BF16), mlp_w2[1].astype(BF16)]],
        "ag_start", exchange=False)

    lam_r = jnp.concatenate([s5_lam_re.reshape(1, S5_NS), s5_lam_im.reshape(1, S5_NS)], axis=0)
    ldt_r = jnp.repeat(s5_log_dt.reshape(32), 64).reshape(1, S5_NS)
    lam_c = jnp.transpose(lam_r)
    ldt_c = jnp.transpose(ldt_r)
    b_t = jnp.stack([jnp.tile(s5_b_re.reshape(S5_NS, 16), (1, 8)), jnp.tile(s5_b_im.reshape(S5_NS, 16), (1, 8))])
    c_t = jnp.stack([jnp.tile(s5_c_re.reshape(S5_W, 64), (1, 8)), jnp.tile(s5_c_im.reshape(S5_W, 64), (1, 8))])
    bf_pad = jnp.pad(fox_b_f, ((0, 0), (0, LANES - 8)))
    b_st = jnp.transpose(sgu_b_s[0])

    h0, rx0 = _rms_fwd(x0, _tie(mix_pre_g[0:1], ag_token), "rms0")
    tabs, bset, cset = _s5_prep(lam_r, ldt_r, lam_c, ldt_c, b_t, c_t, "s5_prep")
    ag0 = _comm_wait(ag_groups[0], tabs, "ag_wait0", exchange=False)
    winT_e = jnp.pad(ag0[0].reshape(EVEN_IN, D_MODEL), ((0, EVEN_PAD - EVEN_IN), (0, 0)))
    pool_scale_f = ag0[1][:, 0, :64].reshape(1, 512)
    ln_g_f = ag0[1][:, 1, :64].reshape(1, 512)
    ln_b_f = ag0[1][:, 2, :64].reshape(1, 512)
    z0 = _mm(h0, winT_e, name="win_even", tb=True, bm=512, bn=EVEN_PAD)
    xs = _s5_scan_fwd(z0, bset, tabs, "s5_scan")
    ag1 = _comm_wait(ag_groups[1], xs, "ag_wait1", exchange=False)
    wglu = ag1[0].reshape(S5_W, S5_W)
    wout_e = ag1[1].reshape(D_MODEL, D_MODEL)
    ylin, ya = _s5_out_fwd(xs, cset, z0, s5_d, wglu, "s5_out")
    fcum, fq = _fox_f_fwd(z0, bf_pad, "fox_f")
    frow = jnp.transpose(fcum[:, :8]).reshape(4, 2, L)
    o_att, lse = _fox_fwd(z0, fq, frow, "fox_fwd")
    mix0 = jnp.concatenate([ya, o_att.astype(BF16)], axis=1)
    y0 = _mm(mix0, wout_e, name="wout_even")
    x1, ry0, h1, rx1 = _post_pre_fwd(x0, y0, mix_post_g[0:1], mlp_pre_g[0:1], "post0")
    ag2 = _comm_wait(ag_groups[2], rx1, "ag_wait2", exchange=False)
    w1 = [ag2[0], None]
    w2 = [ag2[1].reshape(4 * D_MODEL, D_MODEL), None]
    p0, a0 = _mm(h1, w1[0], name="mlp0_w1", b3=True, out_dtypes=(BF16, BF16), epi=_epi_relu2)
    o0 = _mm(a0, w2[0], name="mlp0_w2")
    x2, ro0, h2, rx2 = _post_pre_fwd(x1, o0, mlp_post_g[0:1], mix_pre_g[1:2], "post1")
    ag3 = _comm_wait(ag_groups[3], rx2, "ag_wait3", exchange=False)
    winT_o = ag3[0].reshape(ODD_IN, D_MODEL)
    wout_o = ag3[1].reshape(D_MODEL, D_MODEL)
    w1[1] = ag3[2]
    w2[1] = ag3[3].reshape(4 * D_MODEL, D_MODEL)
    z1 = _mm(h2, winT_o, name="win_odd", tb=True, bn=ODD_IN)
    yc, pooled = _pool_fwd(z1, pool_w[0], pool_scale_f, "pool_fwd")
    yd = _sgu_fwd(z1, ln_g_f, ln_b_f, sgu_w_s[0], b_st, "sgu_fwd")
    mix1 = jnp.concatenate([yc, yd], axis=1)
    y1 = _mm(mix1, wout_o, name="wout_odd")
    x3, ry1, h3, rx3 = _post_pre_fwd(x2, y1, mix_post_g[1:2], mlp_pre_g[1:2], "post2")
    p1, a1 = _mm(h3, w1[1], name="mlp1_w1", b3=True, out_dtypes=(BF16, BF16), epi=_epi_relu2)
    o1 = _mm(a1, w2[1], name="mlp1_w2")
    gx4, ro1, sq = _post_loss_fwd(x3, o1, mlp_post_g[1:2], target, "post3")

    g_o1, gg_mlp_post1 = _post_bwd(gx4, o1, ro1, mlp_post_g[1:2], "bpost3")
    g_p1 = _mm(g_o1, w2[1], name="b_mlp1_a", tb=True, out_dtypes=(BF16,), epi=_epi_relu2_bwd, extra=(p1,))
    gw2_1 = _mm(a1, g_o1, name="b_mlp1_w2", ta=True)
    g_h3 = _mm(g_p1, w1[1], name="b_mlp1_h", tb=True, b3=True)
    gw1_1 = _mm(h3, g_p1, name="b_mlp1_w1", ta=True, out3=True)
    (ex1,), tok1 = _comm_start([[gw1_1, gw2_1.reshape(N_DEV, 512, D_MODEL)]], "ex_start1", exchange=True)
    g_x3, gg_mlp_pre1, g_y1, gg_mix_post1 = _pre_post_bwd(g_h3, x3, rx3, _tie(mlp_pre_g[1:2], tok1), gx4, y1, ry1, mix_post_g[1:2], "bpre3")
    g_mix1 = _mm(g_y1, wout_o, name="b_wout_odd_m", tb=True)
    gwout_o = _mm(mix1, g_y1, name="b_wout_odd_w", ta=True)
    g_xc, g_pool_w, g_pool_scale = _pool_bwd(g_mix1, pooled, pool_w[0], pool_scale_f, "pool_bwd")
    g_u1, g_v1, g_ws, g_bst, g_ln_g, g_ln_b = _sgu_bwd(g_mix1, z1, ln_g_f, ln_b_f, sgu_w_s[0], b_st, "sgu_bwd")
    g_z1 = jnp.concatenate([g_xc, g_u1, g_v1], axis=1).astype(BF16)
    g_h2 = _mm(g_z1, winT_o, name="b_win_odd_h", bk=ODD_IN)
    gwinT_o = _mm(g_z1, h2, name="b_win_odd_w", ta=True, bm=512)
    (ex2,), tok2 = _comm_start([[gwout_o.reshape(N_DEV, 128, D_MODEL), gwinT_o.reshape(N_DEV, ODD_IN // N_DEV, D_MODEL)]], "ex_start2", exchange=True)
    g_x2, gg_mix_pre1, g_o0, gg_mlp_post0 = _pre_post_bwd(g_h2, x2, rx2, _tie(mix_pre_g[1:2], tok2), g_x3, o0, ro0, mlp_post_g[0:1], "bpre2")
    g_p0 = _mm(g_o0, w2[0], name="b_mlp0_a", tb=True, out_dtypes=(BF16,), epi=_epi_relu2_bwd, extra=(p0,))
    gw2_0 = _mm(a0, g_o0, name="b_mlp0_w2", ta=True)
    g_h1 = _mm(g_p0, w1[0], name="b_mlp0_h", tb=True, b3=True)
    gw1_0 = _mm(h1, g_p0, name="b_mlp0_w1", ta=True, out3=True)
    (ex3,), tok3 = _comm_start([[gw1_0, gw2_0.reshape(N_DEV, 512, D_MODEL)]], "ex_start3", exchange=True)
    g_x1, gg_mlp_pre0, g_y0, gg_mix_post0 = _pre_post_bwd(g_h1, x1, rx1, _tie(mlp_pre_g[0:1], tok3), g_x2, y0, ry0, mix_post_g[0:1], "bpre1")
    g_mix0 = _mm(g_y0, wout_e, name="b_wout_even_m", tb=True)
    gwout_e = _mm(mix0, g_y0, name="b_wout_even_w", ta=True)
    gyl, gud, g_wglu, g_d = _s5_glu_bwd(g_mix0, ylin, z0, s5_d, wglu, "s5_glu_bwd")
    (ex4,), tok4 = _comm_start([[gwout_e.reshape(N_DEV, 128, D_MODEL), g_wglu.reshape(N_DEV, 64, S5_W)]], "ex_start4", exchange=True)
    g_u0, ga, gb_raw, gc_raw = _s5_scan_bwd(gyl, _tie(cset, tok4), xs, z0, bset, gud, tabs, "s5_scan_bwd")
    g_lam, g_ldt, g_b, g_c = _s5_param_bwd(lam_c, ldt_c, b_t, gb_raw, jnp.transpose(ga), gc_raw, "s5_param_bwd")
    dq, dk, dv, dfq, dfrow = _fox_bwd(z0, fq, frow, o_att, lse, g_mix0, "fox_bwd")
    dFk = jnp.pad(jnp.transpose(dfrow.reshape(8, L)), ((0, 0), (0, LANES - 8)))
    dfl, db_f = _fox_f_bwd(dFk, dfq, z0, bf_pad, "fox_f_bwd")
    g_z0 = jnp.concatenate([g_u0, dq, dk, dv, dfl], axis=1).astype(BF16)
    g_h0 = _mm(g_z0, winT_e, name="b_win_even_h", bk=EVEN_PAD)
    grad_x, gg_mix_pre0 = _pre_bwd(g_h0, x0, rx0, mix_pre_g[0:1], g_x1, "bpre0")
    gwinT_e = _mm(g_z0, h0, name="b_win_even_w", ta=True, bm=EVEN_PAD, bk=512)

    small_grads = dict(
        mix_pre_g=jnp.concatenate([gg_mix_pre0, gg_mix_pre1]), mix_post_g=jnp.concatenate([gg_mix_post0, gg_mix_post1]),
        mlp_pre_g=jnp.concatenate([gg_mlp_pre0, gg_mlp_pre1]), mlp_post_g=jnp.concatenate([gg_mlp_post0, gg_mlp_post1]),
        s5_lam_re=g_lam[:, 0], s5_lam_im=g_lam[:, 1], s5_log_dt=g_ldt,
        s5_b_re=g_b[0, :, :16], s5_b_im=g_b[1, :, :16], s5_c_re=g_c[0, :, :64], s5_c_im=g_c[1, :, :64],
        s5_d=g_d, fox_b_f=db_f[:, :8], pool_w=g_pool_w, sgu_w_s=g_ws, sgu_b_s=jnp.transpose(g_bst),
        pool_scale=g_pool_scale, sgu_ln_g=g_ln_g, sgu_ln_b=g_ln_b)
    small_names = list(small_grads)
    full_shapes = [(512,) if nm in ("pool_scale", "sgu_ln_g", "sgu_ln_b") else weights[nm].shape for nm in small_names]
    full_shapes.append((1, 1))
    rows = _packed_rows(full_shapes)
    packed = _pack([small_grads[nm] for nm in small_names] + [sq], rows).reshape(N_DEV, rows // N_DEV, LANES)
    (recv_small,) = _exchange([packed], "exchange_small")
    piece = _sum_pieces(recv_small, "sum_small")
    (small_all,) = _all_gather([piece], "ag_small")
    small_full = _unpack(small_all.reshape(rows, LANES), full_shapes)
    loss = 0.5 * small_full.pop()[0, 0] / D_MODEL

    gwinT_e_pieces = gwinT_e[:EVEN_IN].reshape(N_DEV, EVEN_IN // N_DEV, D_MODEL)
    (ex5,), tok5 = _comm_start([[gwinT_e_pieces]], "ex_start5", exchange=True, dep=small_all)
    r_w1_1, r_w2_1 = _comm_wait(ex1, tok5, "ex_wait1", exchange=True)
    r_wout_o, r_win_o = _comm_wait(ex2, tok5, "ex_wait2", exchange=True)
    r_w1_0, r_w2_0 = _comm_wait(ex3, tok5, "ex_wait3", exchange=True)
    r_wout_e, r_wglu = _comm_wait(ex4, tok5, "ex_wait4", exchange=True)
    small_g = {}
    for nm, g in zip(small_names, small_full):
        if nm in ("pool_scale", "sgu_ln_g", "sgu_ln_b"):
            g = lax.dynamic_slice(g, (my_index * 64,), (64,)).reshape(1, 64)
        small_g[nm] = g
    own_shapes = [weights[nm].shape for nm in small_names]
    rows2 = _packed_rows(own_shapes)
    pw = _pack([weights[nm] for nm in small_names], rows2)
    pg = _pack([small_g[nm] for nm in small_names], rows2)
    pm = _pack([mom_m[nm] for nm in small_names], rows2)
    pv = _pack([mom_v[nm] for nm in small_names], rows2)
    pd, pnm, pnv = _adamw(pw, pg, pm, pv, "adamw_small")
    res = {}
    for nm, d_, m_, v_ in zip(small_names, _unpack(pd, own_shapes), _unpack(pnm, own_shapes), _unpack(pnv, own_shapes)):
        res[nm] = (small_g[nm], d_, m_, v_)

    for nm, parts in (("mlp_w1", (r_w1_0, r_w1_1)), ("mlp_w2", (r_w2_0, r_w2_1))):
        first = _sum_adamw(parts[0], weights[nm], mom_m[nm], mom_v[nm], "adamw_%s_0" % nm, layer=0)
        res[nm] = tuple(_sum_adamw(parts[1], weights[nm], mom_m[nm], mom_v[nm], "adamw_%s_1" % nm, layer=1, prev=first))
    big_parts = dict(s5_w_glu=r_wglu, w_out_even=r_wout_e, w_out_odd=r_wout_o)
    for nm, parts in big_parts.items():
        res[nm] = tuple(_sum_adamw(parts, weights[nm], mom_m[nm], mom_v[nm], "adamw_" + nm))
    done = [res[nm][1] for nm in ("mlp_w1", "mlp_w2", "s5_w_glu", "w_out_even", "w_out_odd")]
    for nm, parts in (("w_in_odd", r_win_o), ("w_in_even", None)):
        if parts is None:
            (parts,) = _comm_wait(ex5, done, "ex_wait5", exchange=True)
        outs = _sum_adamw(parts, jnp.transpose(weights[nm], (0, 2, 1)), jnp.transpose(mom_m[nm], (0, 2, 1)),
                          jnp.transpose(mom_v[nm], (0, 2, 1)), "adamw_" + nm)
        res[nm] = tuple(jnp.transpose(o, (0, 2, 1)) for o in outs)
        done.append(res[nm][1])

    grads = [res[nm][0].reshape(weights[nm].shape) for nm in names]
    deltas = [res[nm][1].reshape(weights[nm].shape) for nm in names]
    new_m = [res[nm][2].reshape(weights[nm].shape) for nm in names]
    new_v = [res[nm][3].reshape(weights[nm].shape) for nm in names]
    return (loss, grad_x[None], *grads, *deltas, *new_m, *new_v)
```

```python
import functools
import math

import jax
import jax.numpy as jnp
from jax import lax
from jax.experimental import pallas as pl
from jax.experimental.pallas import tpu as pltpu

F32 = jnp.float32
BF16 = jnp.bfloat16
MESH = pl.DeviceIdType.MESH
ANY = pl.BlockSpec(memory_space=pl.ANY)

N_DEV = 8
D_MODEL = 1024
EPS = 1e-6
S5_W = 512
S5_NS = 2048
SCAN_GROUPS = 4
SCAN_CHUNK = 1024
FOX_W = 512
EVEN_IN = 2056
EVEN_PAD = 2176
ODD_IN = 1536
LANES = 128
PIECE = 4 * D_MODEL // N_DEV
VMEM_LIMIT = 56 * 1024 * 1024

ADAM_LR = 0.001
ADAM_B1 = 0.9
ADAM_B2 = 0.999
ADAM_EPS = 1e-08
ADAM_WD = 0.01
ADAM_STEP = 10

NT = (((1,), (1,)), ((), ()))
TN = (((0,), (0,)), ((), ()))
NN = (((1,), (0,)), ((), ()))


def _cp(*sem):
    return pltpu.CompilerParams(dimension_semantics=sem, vmem_limit_bytes=VMEM_LIMIT)


def _sds(shape, dtype=F32):
    return jax.ShapeDtypeStruct(tuple(shape), dtype)


def _gelu(x):
    t = jnp.tanh(0.7978845608028654 * (x + 0.044715 * x * x * x))
    return 0.5 * x * (1.0 + t)


def _gelu_grad(x):
    t = jnp.tanh(0.7978845608028654 * (x + 0.044715 * x * x * x))
    du = 0.7978845608028654 * (1.0 + 3.0 * 0.044715 * x * x)
    return 0.5 * (1.0 + t) + 0.5 * x * (1.0 - t * t) * du


def _sigmoid(x):
    return 1.0 / (1.0 + jnp.exp(-x))


def _dot(a, b, dn=NN):
    return lax.dot_general(a, b, dn, preferred_element_type=F32)


def _mm(a, b, *, name, ta=False, tb=False, b3=False, out3=False, out_dtypes=(F32,), epi=None, extra=(),
        bm=1024, bn=1024, bk=1024):
    M = a.shape[1] if ta else a.shape[0]
    K = a.shape[0] if ta else a.shape[1]
    pw = b.shape[2] if b3 else PIECE
    if b3:
        N = b.shape[1] if tb else b.shape[0] * pw
        assert (b.shape[0] * pw if tb else b.shape[1]) == K
    else:
        N = b.shape[0] if tb else b.shape[1]
    bm, bn, bk = min(bm, M), min(bn, N), min(bk, K)
    assert M % bm == 0 and N % bn == 0 and K % bk == 0, (name, M, N, K, bm, bn, bk)
    assert not (b3 or out3) or ((bk if tb else bn) % pw == 0 and bn % PIECE == 0)
    nk = K // bk
    n_extra = len(extra)
    n_out = len(out_dtypes)
    dn = (((0 if ta else 1,), (1 if tb else 0,)), ((), ()))

    def body(*refs):
        a_ref, b_ref = refs[0], refs[1]
        e_refs = refs[2:2 + n_extra]
        o_refs = refs[2 + n_extra:2 + n_extra + n_out]
        acc_ref = refs[-1]
        k = pl.program_id(2)

        @pl.when(k == 0)
        def _():
            acc_ref[...] = jnp.zeros_like(acc_ref)

        if not b3:
            acc_ref[...] += lax.dot_general(a_ref[...].astype(BF16), b_ref[...].astype(BF16), dn,
                                            preferred_element_type=F32)
        elif tb:
            for t in range(bk // pw):
                a_t = a_ref[pl.ds(t * pw, pw), :] if ta else a_ref[:, pl.ds(t * pw, pw)]
                acc_ref[...] += lax.dot_general(a_t.astype(BF16), b_ref[t].astype(BF16), dn,
                                                preferred_element_type=F32)
        else:
            a_v = a_ref[...].astype(BF16)
            for t in range(bn // pw):
                acc_ref[:, pl.ds(t * pw, pw)] += lax.dot_general(a_v, b_ref[t].astype(BF16), dn,
                                                                 preferred_element_type=F32)

        @pl.when(k == nk - 1)
        def _():
            acc = acc_ref[...]
            outs = (acc,) if epi is None else epi(acc, *[e[...] for e in e_refs])
            for o_ref, o in zip(o_refs, outs):
                if out3:
                    for t in range(bn // PIECE):
                        o_ref[t] = o[:, t * PIECE:(t + 1) * PIECE].astype(o_ref.dtype)
                else:
                    o_ref[...] = o.astype(o_ref.dtype)

    a_spec = pl.BlockSpec((bk, bm), lambda i, j, k: (k, i)) if ta else pl.BlockSpec((bm, bk), lambda i, j, k: (i, k))
    if b3:
        if tb:
            b_spec = pl.BlockSpec((bk // pw, bn, pw), lambda i, j, k: (k, j, 0))
        else:
            b_spec = pl.BlockSpec((bn // pw, bk, pw), lambda i, j, k: (j, k, 0))
    else:
        b_spec = pl.BlockSpec((bn, bk), lambda i, j, k: (j, k)) if tb else pl.BlockSpec((bk, bn), lambda i, j, k: (k, j))
    e_specs = [pl.BlockSpec((bm, bn), lambda i, j, k: (i, j)) for _ in extra]
    if out3:
        o_specs = [pl.BlockSpec((bn // PIECE, bm, PIECE), lambda i, j, k: (j, i, 0)) for _ in out_dtypes]
        o_shapes = [_sds((N // PIECE, M, PIECE), dt) for dt in out_dtypes]
    else:
        o_specs = [pl.BlockSpec((bm, bn), lambda i, j, k: (i, j)) for _ in out_dtypes]
        o_shapes = [_sds((M, N), dt) for dt in out_dtypes]
    outs = pl.pallas_call(
        body, name=name, grid=(M // bm, N // bn, nk),
        in_specs=[a_spec, b_spec] + e_specs, out_specs=o_specs, out_shape=o_shapes,
        scratch_shapes=[pltpu.VMEM((bm, bn), F32)],
        compiler_params=_cp("parallel", "parallel", "arbitrary"),
    )(a, b, *extra)
    return outs[0] if n_out == 1 else outs


def _epi_relu2(acc):
    r = jnp.maximum(acc, 0.0)
    return acc, r * r


def _epi_relu2_bwd(acc, p):
    return (acc * (2.0 * jnp.maximum(p.astype(F32), 0.0)),)


def _row_spec(rb, w=D_MODEL):
    return pl.BlockSpec((rb, w), lambda i: (i, 0))


def _vec_spec(w=D_MODEL):
    return pl.BlockSpec((1, w), lambda i: (0, 0))


def _rstd(v):
    return lax.rsqrt(jnp.mean(v * v, axis=-1, keepdims=True) + EPS)


def _rms_fwd(x, g, name):
    L = x.shape[0]
    rb = min(256, L)

    def body(x_ref, g_ref, h_ref, r_ref):
        xv = x_ref[...]
        r = _rstd(xv)
        h_ref[...] = (xv * r * g_ref[...]).astype(BF16)
        r_ref[...] = r

    return pl.pallas_call(
        body, name=name, grid=(L // rb,),
        in_specs=[_row_spec(rb), _vec_spec()],
        out_specs=[_row_spec(rb), _row_spec(rb, 1)],
        out_shape=[_sds((L, D_MODEL), BF16), _sds((L, 1))],
        compiler_params=_cp("parallel"),
    )(x, g)


def _post_pre_fwd(x_in, y, g_post, g_pre, name):
    L = x_in.shape[0]
    rb = min(256, L)

    def body(x_ref, y_ref, gp_ref, gn_ref, xo_ref, ry_ref, h_ref, rx_ref):
        yv = y_ref[...]
        ry = _rstd(yv)
        xo = x_ref[...] + yv * ry * gp_ref[...]
        rx = _rstd(xo)
        xo_ref[...] = xo
        ry_ref[...] = ry
        h_ref[...] = (xo * rx * gn_ref[...]).astype(BF16)
        rx_ref[...] = rx

    return pl.pallas_call(
        body, name=name, grid=(L // rb,),
        in_specs=[_row_spec(rb), _row_spec(rb), _vec_spec(), _vec_spec()],
        out_specs=[_row_spec(rb), _row_spec(rb, 1), _row_spec(rb), _row_spec(rb, 1)],
        out_shape=[_sds((L, D_MODEL)), _sds((L, 1)), _sds((L, D_MODEL), BF16), _sds((L, 1))],
        compiler_params=_cp("parallel"),
    )(x_in, y, g_post, g_pre)


def _post_loss_fwd(x_in, y, g_post, target, name):
    L = x_in.shape[0]
    rb = min(256, L)

    def body(x_ref, y_ref, gp_ref, t_ref, gx_ref, ry_ref, loss_ref):
        i = pl.program_id(0)
        yv = y_ref[...]
        ry = _rstd(yv)
        diff = x_ref[...] + yv * ry * gp_ref[...] - t_ref[...]
        gx_ref[...] = diff * (1.0 / D_MODEL)
        ry_ref[...] = ry

        @pl.when(i == 0)
        def _():
            loss_ref[...] = jnp.zeros_like(loss_ref)

        loss_ref[...] += jnp.sum(diff * diff, keepdims=True)

    return pl.pallas_call(
        body, name=name, grid=(L // rb,),
        in_specs=[_row_spec(rb), _row_spec(rb), _vec_spec(), _row_spec(rb)],
        out_specs=[_row_spec(rb), _row_spec(rb, 1), pl.BlockSpec((1, 1), lambda i: (0, 0))],
        out_shape=[_sds((L, D_MODEL)), _sds((L, 1)), _sds((1, 1))],
        compiler_params=_cp("arbitrary"),
    )(x_in, y, g_post, target)


def _rms_bwd_rows(dy, xv, r, g):
    n = xv * r
    dyg = dy * g
    return r * (dyg - n * jnp.mean(dyg * n, axis=-1, keepdims=True)), n


def _post_bwd(g_out, y, ry, g_post, name):
    L = y.shape[0]
    rb = min(256, L)

    def body(go_ref, y_ref, ry_ref, gp_ref, gy_ref, gg_ref):
        i = pl.program_id(0)
        go = go_ref[...]
        gy, n = _rms_bwd_rows(go, y_ref[...], ry_ref[...], gp_ref[...])
        gy_ref[...] = gy.astype(BF16)

        @pl.when(i == 0)
        def _():
            gg_ref[...] = jnp.zeros_like(gg_ref)

        gg_ref[...] += jnp.sum(go * n, axis=0, keepdims=True)

    return pl.pallas_call(
        body, name=name, grid=(L // rb,),
        in_specs=[_row_spec(rb), _row_spec(rb), _row_spec(rb, 1), _vec_spec()],
        out_specs=[_row_spec(rb), _vec_spec()],
        out_shape=[_sds((L, D_MODEL), BF16), _sds((1, D_MODEL))],
        compiler_params=_cp("arbitrary"),
    )(g_out, y, ry, g_post)


def _pre_post_bwd(g_h, x, rx, g_pre, g_out, y_prev, ry_prev, g_post_prev, name):
    L = x.shape[0]
    rb = min(256, L)

    def body(gh_ref, x_ref, rx_ref, gn_ref, go_ref, y_ref, ry_ref, gp_ref, gi_ref, ggn_ref, gy_ref, ggp_ref):
        i = pl.program_id(0)
        gh = gh_ref[...]
        gx, n = _rms_bwd_rows(gh, x_ref[...], rx_ref[...], gn_ref[...])
        gi = go_ref[...] + gx
        gi_ref[...] = gi
        gy, ny = _rms_bwd_rows(gi, y_ref[...], ry_ref[...], gp_ref[...])
        gy_ref[...] = gy.astype(BF16)

        @pl.when(i == 0)
        def _():
            ggn_ref[...] = jnp.zeros_like(ggn_ref)
            ggp_ref[...] = jnp.zeros_like(ggp_ref)

        ggn_ref[...] += jnp.sum(gh * n, axis=0, keepdims=True)
        ggp_ref[...] += jnp.sum(gi * ny, axis=0, keepdims=True)

    return pl.pallas_call(
        body, name=name, grid=(L // rb,),
        in_specs=[_row_spec(rb), _row_spec(rb), _row_spec(rb, 1), _vec_spec(), _row_spec(rb),
                  _row_spec(rb), _row_spec(rb, 1), _vec_spec()],
        out_specs=[_row_spec(rb), _vec_spec(), _row_spec(rb), _vec_spec()],
        out_shape=[_sds((L, D_MODEL)), _sds((1, D_MODEL)), _sds((L, D_MODEL), BF16), _sds((1, D_MODEL))],
        compiler_params=_cp("arbitrary"),
    )(g_h, x, rx, g_pre, g_out, y_prev, ry_prev, g_post_prev)


def _pre_bwd(g_h, x, rx, g_pre, g_out, name):
    L = x.shape[0]
    rb = min(256, L)

    def body(gh_ref, x_ref, rx_ref, gn_ref, go_ref, gi_ref, ggn_ref):
        i = pl.program_id(0)
        gh = gh_ref[...]
        gx, n = _rms_bwd_rows(gh, x_ref[...], rx_ref[...], gn_ref[...])
        gi_ref[...] = go_ref[...] + gx

        @pl.when(i == 0)
        def _():
            ggn_ref[...] = jnp.zeros_like(ggn_ref)

        ggn_ref[...] += jnp.sum(gh * n, axis=0, keepdims=True)

    return pl.pallas_call(
        body, name=name, grid=(L // rb,),
        in_specs=[_row_spec(rb), _row_spec(rb), _row_spec(rb, 1), _vec_spec(), _row_spec(rb)],
        out_specs=[_row_spec(rb), _vec_spec()],
        out_shape=[_sds((L, D_MODEL)), _sds((1, D_MODEL))],
        compiler_params=_cp("arbitrary"),
    )(g_h, x, rx, g_pre, g_out)


def _cmul(ar, ai, br, bi):
    return ar * br - ai * bi, ar * bi + ai * br


def _zoh_cols(lr, li, ldt):
    dt = jnp.exp(ldt)
    mag = jnp.exp(lr * dt)
    ar = mag * jnp.cos(li * dt)
    ai = mag * jnp.sin(li * dt)
    den = lr * lr + li * li
    nr = ar - 1.0
    qr = (nr * lr + ai * li) / den
    qi = (ai * lr - nr * li) / den
    return dt, ar, ai, qr, qi, den


def _b_mask():
    r = lax.broadcasted_iota(jnp.int32, (S5_NS, LANES), 0)
    c = lax.broadcasted_iota(jnp.int32, (S5_NS, LANES), 1)
    return ((r >> 6) & 7) == (c >> 4)


def _c_mask():
    r = lax.broadcasted_iota(jnp.int32, (S5_W, 512), 0)
    c = lax.broadcasted_iota(jnp.int32, (S5_W, 512), 1)
    return ((r >> 4) & 7) == (c >> 6)


def _s5_prep(lam_r, ldt_r, lam_c, ldt_c, b_t, c_t, name):
    def body(lam_r_ref, ldt_r_ref, lam_c_ref, ldt_c_ref, b_ref, c_ref, tab_ref, bset_ref, cset_ref):
        lr, li = lam_r_ref[0:1, :], lam_r_ref[1:2, :]
        dt = jnp.exp(ldt_r_ref[...])
        mag = jnp.exp(lr * dt)
        p1r, p1i = mag * jnp.cos(li * dt), mag * jnp.sin(li * dt)
        p2r, p2i = _cmul(p1r, p1i, p1r, p1i)
        p3r, p3i = _cmul(p2r, p2i, p1r, p1i)
        p4r, p4i = _cmul(p2r, p2i, p2r, p2i)
        p5r, p5i = _cmul(p4r, p4i, p1r, p1i)
        p6r, p6i = _cmul(p4r, p4i, p2r, p2i)
        p7r, p7i = _cmul(p4r, p4i, p3r, p3i)
        p8r, p8i = _cmul(p4r, p4i, p4r, p4i)
        pw_r = [p1r, p2r, p3r, p4r, p5r, p6r, p7r, p8r]
        pw_i = [p1i, p2i, p3i, p4i, p5i, p6i, p7i, p8i]
        row = lax.broadcasted_iota(jnp.int32, (8, S5_NS), 0)
        zero = jnp.zeros((8, S5_NS), F32)

        def bc(v):
            return jnp.broadcast_to(v, (8, S5_NS))

        for d in range(2):
            sgn = 1.0 if d == 0 else -1.0
            for t, s in enumerate((1, 2, 4)):
                live = (row >= s) if d == 0 else (row <= 7 - s)
                tab_ref[d, 2 * t] = jnp.where(live, bc(pw_r[s - 1]), zero)
                tab_ref[d, 2 * t + 1] = jnp.where(live, bc(sgn * pw_i[s - 1]), zero)
            cr, ci = zero, zero
            for i in range(8):
                e = i if d == 0 else 7 - i
                cr = jnp.where(row == i, bc(pw_r[e]), cr)
                ci = jnp.where(row == i, bc(sgn * pw_i[e]), ci)
            tab_ref[d, 6] = cr
            tab_ref[d, 7] = ci

        _, _, _, qr, qi, _ = _zoh_cols(lam_c_ref[:, 0:1], lam_c_ref[:, 1:2], ldt_c_ref[...])
        bm = _b_mask()
        br, bi = b_ref[0], b_ref[1]
        bset_ref[0] = jnp.where(bm, qr * br - qi * bi, 0.0).astype(BF16)
        bset_ref[1] = jnp.where(bm, qr * bi + qi * br, 0.0).astype(BF16)
        cm = _c_mask()
        cset_ref[0] = jnp.where(cm, c_ref[0], 0.0).astype(BF16)
        cset_ref[1] = jnp.where(cm, c_ref[1], 0.0).astype(BF16)

    vm = pl.BlockSpec(memory_space=pltpu.VMEM)
    return pl.pallas_call(
        body, name=name, in_specs=[vm] * 6, out_specs=[vm] * 3,
        out_shape=[_sds((2, 8, 8, S5_NS)), _sds((2, S5_NS, LANES), BF16), _sds((2, S5_W, 512), BF16)],
        compiler_params=pltpu.CompilerParams(vmem_limit_bytes=VMEM_LIMIT),
    )(lam_r, ldt_r, lam_c, ldt_c, b_t, c_t)


SCAN_W = SCAN_GROUPS * LANES


def _scan_chunk(src_ref, dst_ref, tab_ref, carry_ref, nb, reverse, xs_ref=None, acc_ref=None):
    row = lax.broadcasted_iota(jnp.int32, (8, LANES), 0)

    def step(i, carry):
        b = (nb - 1 - i) if reverse else i
        off = pl.multiple_of(b * 8, 8)
        out = []
        for g in range(SCAN_GROUPS):
            lanes = pl.ds(g * LANES, LANES)
            cr, ci = carry[2 * g], carry[2 * g + 1]
            yr = src_ref[0, pl.ds(off, 8), lanes]
            yi = src_ref[1, pl.ds(off, 8), lanes]
            for t, s in enumerate((1, 2, 4)):
                sh = (8 - s) if reverse else s
                sr = pltpu.roll(yr, sh, 0)
                si = pltpu.roll(yi, sh, 0)
                mr, mi = tab_ref[2 * t, :, lanes], tab_ref[2 * t + 1, :, lanes]
                yr, yi = yr + mr * sr - mi * si, yi + mr * si + mi * sr
            pr, pi = tab_ref[6, :, lanes], tab_ref[7, :, lanes]
            yr, yi = yr + pr * cr - pi * ci, yi + pr * ci + pi * cr
            dst_ref[0, pl.ds(off, 8), lanes] = yr
            dst_ref[1, pl.ds(off, 8), lanes] = yi
            if xs_ref is not None:
                nr = jnp.where(row == 7, cr, pltpu.roll(yr, 7, 0))
                ni = jnp.where(row == 7, ci, pltpu.roll(yi, 7, 0))
                xr = xs_ref[0, pl.ds(off, 8), lanes]
                xi = xs_ref[1, pl.ds(off, 8), lanes]
                acc_ref[0, :, lanes] += xr * nr + xi * ni
                acc_ref[1, :, lanes] += xr * ni - xi * nr
            last = 0 if reverse else 7
            out += [jnp.broadcast_to(yr[last:last + 1, :], (8, LANES)),
                    jnp.broadcast_to(yi[last:last + 1, :], (8, LANES))]
        return tuple(out)

    init = []
    for g in range(SCAN_GROUPS):
        init += [carry_ref[0, :, pl.ds(g * LANES, LANES)], carry_ref[1, :, pl.ds(g * LANES, LANES)]]
    fin = lax.fori_loop(0, nb, step, tuple(init))
    for g in range(SCAN_GROUPS):
        carry_ref[0, :, pl.ds(g * LANES, LANES)] = fin[2 * g]
        carry_ref[1, :, pl.ds(g * LANES, LANES)] = fin[2 * g + 1]


def _s5_scan_fwd(z, bset, tabs, name):
    L = z.shape[0]
    tl = min(SCAN_CHUNK, L)
    nc = L // tl

    def body(u_ref, b_ref, tab_ref, x_ref, carry_ref):
        @pl.when(pl.program_id(1) == 0)
        def _():
            carry_ref[...] = jnp.zeros_like(carry_ref)

        u = u_ref[...].astype(BF16)
        x_ref[0] = _dot(u, b_ref[0], NT)
        x_ref[1] = _dot(u, b_ref[1], NT)
        _scan_chunk(x_ref, x_ref, tab_ref, carry_ref, tl // 8, False)

    return pl.pallas_call(
        body, name=name, grid=(S5_NS // SCAN_W, nc),
        in_specs=[pl.BlockSpec((tl, LANES), lambda j, c: (c, j)),
                  pl.BlockSpec((2, SCAN_W, LANES), lambda j, c: (0, j, 0)),
                  pl.BlockSpec((None, 8, 8, SCAN_W), lambda j, c: (0, 0, 0, j))],
        out_specs=pl.BlockSpec((2, tl, SCAN_W), lambda j, c: (0, c, j)),
        out_shape=_sds((2, L, S5_NS)),
        scratch_shapes=[pltpu.VMEM((2, 8, SCAN_W), F32)],
        compiler_params=_cp("parallel", "arbitrary"),
    )(z, bset, tabs)


def _s5_scan_bwd(gyl, cset, xs, z, bset, gud, tabs, name):
    L = z.shape[0]
    tl = min(SCAN_CHUNK, L)
    nc = L // tl

    def body(g_ref, c_ref, xs_ref, u_ref, b_ref, gud_ref, tab_ref, gu_ref, ga_ref, gb_ref, gc_ref,
             gx_ref, carry_ref, acc_ref):
        c = pl.program_id(1)

        @pl.when(c == 0)
        def _():
            carry_ref[...] = jnp.zeros_like(carry_ref)
            acc_ref[...] = jnp.zeros_like(acc_ref)
            gb_ref[...] = jnp.zeros_like(gb_ref)
            gc_ref[...] = jnp.zeros_like(gc_ref)

        gy = g_ref[...].astype(BF16)
        gx_ref[0] = _dot(gy, c_ref[0])
        gx_ref[1] = -_dot(gy, c_ref[1])
        gc_ref[0] += _dot(gy, xs_ref[0].astype(BF16), TN)
        gc_ref[1] -= _dot(gy, xs_ref[1].astype(BF16), TN)
        _scan_chunk(gx_ref, gx_ref, tab_ref, carry_ref, tl // 8, True, xs_ref, acc_ref)
        gr = gx_ref[0].astype(BF16)
        gi = gx_ref[1].astype(BF16)
        gu_ref[...] = gud_ref[...] + _dot(gr, b_ref[0]) + _dot(gi, b_ref[1])
        u = u_ref[...].astype(BF16)
        gb_ref[0] += _dot(gr, u, TN)
        gb_ref[1] += _dot(gi, u, TN)

        @pl.when(c == nc - 1)
        def _():
            ga_ref[0:1, :] = jnp.sum(acc_ref[0], axis=0, keepdims=True)
            ga_ref[1:2, :] = jnp.sum(acc_ref[1], axis=0, keepdims=True)

    rev = lambda j, c: (nc - 1 - c, j)
    col = pl.BlockSpec((tl, LANES), rev)
    return pl.pallas_call(
        body, name=name, grid=(S5_NS // SCAN_W, nc),
        in_specs=[col, pl.BlockSpec((2, LANES, SCAN_W), lambda j, c: (0, j, 0)),
                  pl.BlockSpec((2, tl, SCAN_W), lambda j, c: (0, nc - 1 - c, j)), col,
                  pl.BlockSpec((2, SCAN_W, LANES), lambda j, c: (0, j, 0)), col,
                  pl.BlockSpec((None, 8, 8, SCAN_W), lambda j, c: (1, 0, 0, j))],
        out_specs=[col, pl.BlockSpec((2, SCAN_W), lambda j, c: (0, j)),
                   pl.BlockSpec((2, SCAN_W, LANES), lambda j, c: (0, j, 0)),
                   pl.BlockSpec((2, LANES, SCAN_W), lambda j, c: (0, j, 0))],
        out_shape=[_sds((L, S5_W)), _sds((2, S5_NS)), _sds((2, S5_NS, LANES)), _sds((2, S5_W, 512))],
        scratch_shapes=[pltpu.VMEM((2, tl, SCAN_W), F32), pltpu.VMEM((2, 8, SCAN_W), F32),
                        pltpu.VMEM((2, 8, SCAN_W), F32)],
        compiler_params=_cp("parallel", "arbitrary"),
    )(gyl, cset, xs, z, bset, gud, tabs)


def _s5_out_fwd(xs, cset, z, dvec, wglu, name):
    L = z.shape[0]
    bl = min(256, L)

    def body(x_ref, c_ref, u_ref, d_ref, w_ref, ylin_ref, ya_ref):
        cols = []
        for j in range(4):
            xr = x_ref[0, :, 512 * j:512 * (j + 1)].astype(BF16)
            xi = x_ref[1, :, 512 * j:512 * (j + 1)].astype(BF16)
            cr = c_ref[0, LANES * j:LANES * (j + 1), :]
            ci = c_ref[1, LANES * j:LANES * (j + 1), :]
            cols.append(_dot(xr, cr, NT) - _dot(xi, ci, NT))
        ylin = jnp.concatenate(cols, axis=1) + d_ref[...] * u_ref[...]
        yg = _gelu(ylin)
        t = _dot(yg.astype(BF16), w_ref[...])
        ylin_ref[...] = ylin
        ya_ref[...] = (yg * _sigmoid(t)).astype(BF16)

    return pl.pallas_call(
        body, name=name, grid=(L // bl,),
        in_specs=[pl.BlockSpec((2, bl, S5_NS), lambda i: (0, i, 0)),
                  pl.BlockSpec((2, S5_W, 512), lambda i: (0, 0, 0)),
                  pl.BlockSpec((bl, S5_W), lambda i: (i, 0)),
                  pl.BlockSpec((1, S5_W), lambda i: (0, 0)),
                  pl.BlockSpec((S5_W, S5_W), lambda i: (0, 0))],
        out_specs=[pl.BlockSpec((bl, S5_W), lambda i: (i, 0))] * 2,
        out_shape=[_sds((L, S5_W)), _sds((L, S5_W), BF16)],
        compiler_params=_cp("parallel"),
    )(xs, cset, z, dvec, wglu)


def _s5_glu_bwd(g_m, ylin, z, dvec, wglu, name):
    L = z.shape[0]
    bl = min(256, L)

    def body(g_ref, ylin_ref, u_ref, d_ref, w_ref, gyl_ref, gud_ref, gw_ref, gd_ref):
        i = pl.program_id(0)
        ylin = ylin_ref[...]
        yg = _gelu(ylin)
        ygb = yg.astype(BF16)
        sg = _sigmoid(_dot(ygb, w_ref[...]))
        gya = g_ref[...]
        gt = gya * yg * sg * (1.0 - sg)
        gtb = gt.astype(BF16)
        gyg = gya * sg + _dot(gtb, w_ref[...], NT)
        gyl = gyg * _gelu_grad(ylin)
        gyl_ref[...] = gyl
        gud_ref[...] = gyl * d_ref[...]

        @pl.when(i == 0)
        def _():
            gw_ref[...] = jnp.zeros_like(gw_ref)
            gd_ref[...] = jnp.zeros_like(gd_ref)

        gw_ref[...] += _dot(ygb, gtb, TN)
        gd_ref[...] += jnp.sum(gyl * u_ref[...], axis=0, keepdims=True)

    blk = pl.BlockSpec((bl, S5_W), lambda i: (i, 0))
    return pl.pallas_call(
        body, name=name, grid=(L // bl,),
        in_specs=[blk, blk, blk, pl.BlockSpec((1, S5_W), lambda i: (0, 0)),
                  pl.BlockSpec((S5_W, S5_W), lambda i: (0, 0))],
        out_specs=[blk, blk, pl.BlockSpec((S5_W, S5_W), lambda i: (0, 0)), pl.BlockSpec((1, S5_W), lambda i: (0, 0))],
        out_shape=[_sds((L, S5_W)), _sds((L, S5_W)), _sds((S5_W, S5_W)), _sds((1, S5_W))],
        compiler_params=_cp("arbitrary"),
    )(g_m, ylin, z, dvec, wglu)


def _s5_param_bwd(lam_c, ldt_c, b_t, gb, ga_c, gc, name):
    def body(lam_ref, ldt_ref, b_ref, gb_ref, ga_ref, gc_ref, glam_ref, gldt_ref, gbo_ref, gco_ref):
        lr, li = lam_ref[:, 0:1], lam_ref[:, 1:2]
        dt, ar, ai, qr, qi, den = _zoh_cols(lr, li, ldt_ref[...])
        bm = _b_mask()
        gbr = jnp.where(bm, gb_ref[0], 0.0)
        gbi = jnp.where(bm, gb_ref[1], 0.0)
        br, bi = b_ref[0], b_ref[1]
        obr = gbr * qr + gbi * qi
        obi = gbi * qr - gbr * qi
        gqr = jnp.sum(gbr * br + gbi * bi, axis=1, keepdims=True)
        gqi = jnp.sum(gbi * br - gbr * bi, axis=1, keepdims=True)
        for s in (64, 32, 16):
            obr = obr + pltpu.roll(obr, s, 1)
            obi = obi + pltpu.roll(obi, s, 1)
        gbo_ref[0] = obr
        gbo_ref[1] = obi
        gar = ga_ref[:, 0:1] + (gqr * lr - gqi * li) / den
        gai = ga_ref[:, 1:2] + (gqr * li + gqi * lr) / den
        qlr = (qr * lr + qi * li) / den
        qli = (qi * lr - qr * li) / den
        glr = -(gqr * qlr + gqi * qli)
        gli = -(gqi * qlr - gqr * qli)
        glr = glr + dt * (gar * ar + gai * ai)
        gli = gli + dt * (gai * ar - gar * ai)
        wr, wi = _cmul(lr, li, ar, ai)
        gldt = (gar * wr + gai * wi) * dt
        glam_ref[:, 0:1] = glr
        glam_ref[:, 1:2] = gli
        r = lax.broadcasted_iota(jnp.int32, (S5_NS, 32), 0)
        c = lax.broadcasted_iota(jnp.int32, (S5_NS, 32), 1)
        gldt_ref[...] = jnp.sum(jnp.where((r >> 6) == c, gldt, 0.0), axis=0, keepdims=True)
        cm = _c_mask()
        for k in range(2):
            oc = jnp.where(cm, gc_ref[k], 0.0)
            for s in (256, 128, 64):
                oc = oc + pltpu.roll(oc, s, 1)
            gco_ref[k] = oc[:, 0:LANES]

    vm = pl.BlockSpec(memory_space=pltpu.VMEM)
    return pl.pallas_call(
        body, name=name, in_specs=[vm] * 6, out_specs=[vm] * 4,
        out_shape=[_sds((S5_NS, 2)), _sds((1, 32)), _sds((2, S5_NS, LANES)), _sds((2, S5_W, LANES))],
        compiler_params=pltpu.CompilerParams(vmem_limit_bytes=VMEM_LIMIT),
    )(lam_c, ldt_c, b_t, gb, ga_c, gc)


FL_BLK = EVEN_PAD // LANES - 1
Q_BLK, K_BLK, V_BLK = 4, 8, 12
NEG = -1e30


def _log_sigmoid(v):
    return jnp.minimum(v, 0.0) - jnp.log(1.0 + jnp.exp(-jnp.abs(v)))


def _fox_f_fwd(z, bf, name):
    L = z.shape[0]
    tl = min(256, L)

    def body(fl_ref, b_ref, f_ref, fq_ref, carry_ref):
        i = pl.program_id(0)

        @pl.when(i == 0)
        def _():
            carry_ref[...] = jnp.zeros_like(carry_ref)

        lf = _log_sigmoid(fl_ref[...] + b_ref[...])
        r = lax.broadcasted_iota(jnp.int32, (tl, tl), 0)
        c = lax.broadcasted_iota(jnp.int32, (tl, tl), 1)
        tri = (r >= c).astype(F32)
        cs = lax.dot_general(tri, lf, NN, precision=lax.Precision.HIGHEST, preferred_element_type=F32) + carry_ref[...]
        f_ref[...] = cs
        carry_ref[...] = cs[tl - 1:tl, :]
        expand = (lax.broadcasted_iota(jnp.int32, (LANES, FOX_W), 0)
                  == (lax.broadcasted_iota(jnp.int32, (LANES, FOX_W), 1) >> 6)).astype(F32)
        fq_ref[...] = lax.dot_general(cs, expand, NN, precision=lax.Precision.HIGHEST, preferred_element_type=F32)

    return pl.pallas_call(
        body, name=name, grid=(L // tl,),
        in_specs=[pl.BlockSpec((tl, LANES), lambda i: (i, FL_BLK)), pl.BlockSpec((1, LANES), lambda i: (0, 0))],
        out_specs=[pl.BlockSpec((tl, LANES), lambda i: (i, 0)), pl.BlockSpec((tl, FOX_W), lambda i: (i, 0))],
        out_shape=[_sds((L, LANES)), _sds((L, FOX_W))],
        scratch_shapes=[pltpu.VMEM((1, LANES), F32)],
        compiler_params=_cp("arbitrary"),
    )(z, bf)


def _fox_f_bwd(dFk, dfq, z, bf, name):
    L = z.shape[0]
    tl = min(256, L)
    nb = L // tl

    def body(dfk_ref, dfq_ref, fl_ref, b_ref, dfl_ref, db_ref, carry_ref):
        i = pl.program_id(0)

        @pl.when(i == 0)
        def _():
            carry_ref[...] = jnp.zeros_like(carry_ref)
            db_ref[...] = jnp.zeros_like(db_ref)

        sel = (lax.broadcasted_iota(jnp.int32, (FOX_W, LANES), 0)
               == 64 * lax.broadcasted_iota(jnp.int32, (FOX_W, LANES), 1)).astype(F32)
        dfq_h = lax.dot_general(dfq_ref[...], sel, NN, precision=lax.Precision.HIGHEST, preferred_element_type=F32)
        r = lax.broadcasted_iota(jnp.int32, (tl, tl), 0)
        c = lax.broadcasted_iota(jnp.int32, (tl, tl), 1)
        tri = (r <= c).astype(F32)
        cs = lax.dot_general(tri, dfk_ref[...] + dfq_h, NN, precision=lax.Precision.HIGHEST,
                             preferred_element_type=F32) + carry_ref[...]
        carry_ref[...] = cs[0:1, :]
        dfl = cs * _sigmoid(-(fl_ref[...] + b_ref[...]))
        dfl_ref[...] = dfl
        db_ref[...] += jnp.sum(dfl, axis=0, keepdims=True)

    return pl.pallas_call(
        body, name=name, grid=(nb,),
        in_specs=[pl.BlockSpec((tl, LANES), lambda i: (nb - 1 - i, 0)),
                  pl.BlockSpec((tl, FOX_W), lambda i: (nb - 1 - i, 0)),
                  pl.BlockSpec((tl, LANES), lambda i: (nb - 1 - i, FL_BLK)),
                  pl.BlockSpec((1, LANES), lambda i: (0, 0))],
        out_specs=[pl.BlockSpec((tl, LANES), lambda i: (nb - 1 - i, 0)), pl.BlockSpec((1, LANES), lambda i: (0, 0))],
        out_shape=[_sds((L, LANES)), _sds((1, LANES))],
        scratch_shapes=[pltpu.VMEM((1, LANES), F32)],
        compiler_params=_cp("arbitrary"),
    )(dFk, dfq, z, bf)


def _head_mask(hh):
    lane = lax.broadcasted_iota(jnp.int32, (1, LANES), 1)
    return (lane >> 6) == hh


FOX_T = 512


def _fox_head(x, hh):
    return jnp.where(_head_mask(hh), x, 0.0).astype(BF16)


def _fox_scores(qh, k, fq_ref, fr_ref, hh, causal):
    s = _dot(qh, k, NT) + (fq_ref[:, 64 * hh:64 * hh + 1] - fr_ref[hh:hh + 1, :])
    return s if causal is None else jnp.where(causal, s, NEG)


def _causal(T):
    return lax.broadcasted_iota(jnp.int32, (T, T), 1) <= lax.broadcasted_iota(jnp.int32, (T, T), 0)


def _fox_fwd(z, fq, frow, name):
    L = z.shape[0]
    T = min(FOX_T, L)
    nq = L // T

    def body(qt_ref, kt_ref, q_ref, k_ref, v_ref, fq_ref, fr_ref, o_ref, lse_ref, m_ref, l_ref, acc_ref):
        t = pl.program_id(1)
        qi, ki = qt_ref[t], kt_ref[t]

        @pl.when(ki == 0)
        def _():
            m_ref[...] = jnp.full_like(m_ref, NEG)
            l_ref[...] = jnp.zeros_like(l_ref)
            acc_ref[...] = jnp.zeros_like(acc_ref)

        def step(diagonal):
            q = q_ref[...] * 0.125
            k = k_ref[...].astype(BF16)
            v = v_ref[...].astype(BF16)
            causal = _causal(T) if diagonal else None
            s = jnp.concatenate([_fox_scores(_fox_head(q, hh), k, fq_ref, fr_ref, hh, causal) for hh in range(2)],
                                axis=0)
            m_old = m_ref[...]
            m_new = jnp.maximum(m_old, jnp.max(s, axis=1, keepdims=True))
            alpha = jnp.exp(m_old - m_new)
            p = jnp.exp(s - m_new)
            l_ref[...] = alpha * l_ref[...] + jnp.sum(p, axis=1, keepdims=True)
            m_ref[...] = m_new
            acc_ref[...] = alpha * acc_ref[...] + _dot(p.astype(BF16), v)

        @pl.when(ki < qi)
        def _():
            step(False)

        @pl.when(ki == qi)
        def _():
            step(True)
            h0 = _head_mask(0)
            l = l_ref[...]
            o_h = acc_ref[...] / l
            lse_h = m_ref[...] + jnp.log(l)
            o_ref[...] = jnp.where(h0, o_h[:T], o_h[T:])
            lse_ref[...] = jnp.where(h0, lse_h[:T], lse_h[T:])

    pairs = [(qi, ki) for qi in range(nq) for ki in range(qi + 1)]
    qt = jnp.asarray([p[0] for p in pairs], jnp.int32)
    kt = jnp.asarray([p[1] for p in pairs], jnp.int32)

    def qspec(base):
        return pl.BlockSpec((T, LANES), lambda j, t, qt, kt: (qt[t], base + j))

    def kspec(base):
        return pl.BlockSpec((T, LANES), lambda j, t, qt, kt: (kt[t], base + j))

    return pl.pallas_call(
        body, name=name,
        grid_spec=pltpu.PrefetchScalarGridSpec(
            num_scalar_prefetch=2, grid=(4, len(pairs)),
            in_specs=[qspec(Q_BLK), kspec(K_BLK), kspec(V_BLK), qspec(0),
                      pl.BlockSpec((None, 2, T), lambda j, t, qt, kt: (j, 0, kt[t]))],
            out_specs=[qspec(0), qspec(0)],
            scratch_shapes=[pltpu.VMEM((2 * T, 1), F32), pltpu.VMEM((2 * T, 1), F32),
                            pltpu.VMEM((2 * T, LANES), F32)]),
        out_shape=[_sds((L, FOX_W)), _sds((L, FOX_W))],
        compiler_params=_cp("parallel", "arbitrary"),
    )(qt, kt, z, z, z, fq, frow)


def _fox_bwd(z, fq, frow, o, lse, g_m, name):
    L = z.shape[0]
    T = min(FOX_T, L)
    nq = L // T

    pairs = [(qi, ki) for ki in range(nq) for qi in range(ki, nq)]
    qt = jnp.asarray([p[0] for p in pairs], jnp.int32)
    kt = jnp.asarray([p[1] for p in pairs], jnp.int32)

    def body(qt_ref, kt_ref, q_ref, k_ref, v_ref, fq_ref, fr_ref, o_ref, lse_ref, do_ref,
             dq_ref, dk_ref, dv_ref, dfq_ref, dfk_ref, dk_acc, dv_acc, df_acc):
        t = pl.program_id(1)
        qi, ki = qt_ref[t], kt_ref[t]

        @pl.when(t == 0)
        def _():
            dq_ref[...] = jnp.zeros_like(dq_ref)
            dfq_ref[...] = jnp.zeros_like(dfq_ref)

        @pl.when(qi == ki)
        def _():
            dk_acc[...] = jnp.zeros_like(dk_acc)
            dv_acc[...] = jnp.zeros_like(dv_acc)
            df_acc[...] = jnp.zeros_like(df_acc)

        def step(diagonal):
            q = q_ref[...] * 0.125
            qb = q.astype(BF16)
            k = k_ref[...].astype(BF16)
            v = v_ref[...].astype(BF16)
            do = do_ref[...]
            dob = do.astype(BF16)
            do_o = dob.astype(F32) * o_ref[...]
            causal = _causal(T) if diagonal else None
            dvs, dks, dqs, rss = [], [], [], []
            for hh in range(2):
                s = _fox_scores(_fox_head(q, hh), k, fq_ref, fr_ref, hh, causal)
                p = jnp.exp(s - lse_ref[:, 64 * hh:64 * hh + 1])
                dp = _dot(_fox_head(do, hh), v, NT)
                delta = jnp.sum(jnp.where(_head_mask(hh), do_o, 0.0), axis=1, keepdims=True)
                ds = p * (dp - delta)
                dsb = ds.astype(BF16)
                dvs.append(_dot(p.astype(BF16), dob, TN))
                dks.append(_dot(dsb, qb, TN))
                dqs.append(_dot(dsb, k))
                rss.append(jnp.sum(ds, axis=1, keepdims=True))
                df_acc[hh:hh + 1, :] -= jnp.sum(ds, axis=0, keepdims=True)
            h0 = _head_mask(0)
            dv_acc[...] += jnp.where(h0, dvs[0], dvs[1])
            dk_acc[...] += jnp.where(h0, dks[0], dks[1])
            rows = pl.ds(pl.multiple_of(qi * T, T), T)
            dq_ref[rows, :] += jnp.where(h0, dqs[0], dqs[1])
            dfq_ref[rows, :] += jnp.where(h0, rss[0], rss[1])

        @pl.when(qi > ki)
        def _():
            step(False)

        @pl.when(qi == ki)
        def _():
            step(True)

        @pl.when(qi == nq - 1)
        def _():
            dk_ref[...] = dk_acc[...]
            dv_ref[...] = dv_acc[...]
            dfk_ref[...] = df_acc[...]

        @pl.when(t == len(pairs) - 1)
        def _():
            dq_ref[...] = dq_ref[...] * 0.125

    def qside(base):
        return pl.BlockSpec((T, LANES), lambda j, t, qt, kt: (qt[t], base + j))

    def kside(base):
        return pl.BlockSpec((T, LANES), lambda j, t, qt, kt: (kt[t], base + j))

    pair = pl.BlockSpec((L, LANES), lambda j, t, qt, kt: (0, j))
    frow_spec = pl.BlockSpec((None, 2, T), lambda j, t, qt, kt: (j, 0, kt[t]))
    return pl.pallas_call(
        body, name=name,
        grid_spec=pltpu.PrefetchScalarGridSpec(
            num_scalar_prefetch=2, grid=(4, len(pairs)),
            in_specs=[qside(Q_BLK), kside(K_BLK), kside(V_BLK), qside(0), frow_spec, qside(0), qside(0), qside(4)],
            out_specs=[pair, kside(0), kside(0), pair, frow_spec],
            scratch_shapes=[pltpu.VMEM((T, LANES), F32), pltpu.VMEM((T, LANES), F32), pltpu.VMEM((2, T), F32)]),
        out_shape=[_sds((L, FOX_W)), _sds((L, FOX_W)), _sds((L, FOX_W)), _sds((L, FOX_W)), _sds((4, 2, L))],
        compiler_params=_cp("parallel", "arbitrary"),
    )(qt, kt, z, z, z, fq, frow, o, lse, g_m)


def _shift_rows(v, s, down, row):
    n = v.shape[0]
    if down:
        return jnp.where(row >= s, pltpu.roll(v, s, 0), 0.0)
    return jnp.where(row < n - s, pltpu.roll(v, n - s, 0), 0.0)


def _window_sum(v, g, down, row):
    out = jnp.zeros_like(v)
    s = v
    for k in range(4):
        s = s + _shift_rows(s, 1 << k, down, row)
        out = jnp.where(g == k, s, out)
    return out


def _pool_inv_cnt(g, row):
    w = jnp.left_shift(2, g).astype(F32)
    return 1.0 / jnp.minimum(row.astype(F32) + 1.0, w)


def _pool_fwd(z, pool_w, scale, name):
    L = z.shape[0]

    def body(x_ref, w_ref, s_ref, y_ref, p_ref):
        g = pl.program_id(0)
        row = lax.broadcasted_iota(jnp.int32, (L, LANES), 0)
        x = x_ref[...]
        pooled = (_window_sum(x, g, True, row) * _pool_inv_cnt(g, row) - x).astype(BF16)
        p_ref[...] = pooled
        y_ref[...] = (_dot(pooled, w_ref[...].astype(BF16)) * s_ref[...]).astype(BF16)

    col = pl.BlockSpec((L, LANES), lambda g: (0, g))
    return pl.pallas_call(
        body, name=name, grid=(4,),
        in_specs=[col, pl.BlockSpec((None, LANES, LANES), lambda g: (g, 0, 0)), pl.BlockSpec((1, LANES), lambda g: (0, g))],
        out_specs=[col, col],
        out_shape=[_sds((L, 512), BF16), _sds((L, 512), BF16)],
        compiler_params=_cp("parallel"),
    )(z, pool_w, scale)


def _pool_bwd(g_m, pooled, pool_w, scale, name):
    L = g_m.shape[0]

    def body(g_ref, p_ref, w_ref, s_ref, gx_ref, gw_ref, gs_ref):
        g = pl.program_id(0)
        row = lax.broadcasted_iota(jnp.int32, (L, LANES), 0)
        gy = g_ref[...]
        pooled = p_ref[...]
        wb = w_ref[...].astype(BF16)
        lin = _dot(pooled, wb)
        gs_ref[...] = jnp.sum(gy * lin, axis=0, keepdims=True)
        glin = (gy * s_ref[...]).astype(BF16)
        gw_ref[...] = _dot(pooled, glin, TN)
        gp = _dot(glin, wb, NT)
        gx_ref[...] = _window_sum(gp * _pool_inv_cnt(g, row), g, False, row) - gp

    col = pl.BlockSpec((L, LANES), lambda g: (0, g))
    wspec = pl.BlockSpec((None, LANES, LANES), lambda g: (g, 0, 0))
    vec = pl.BlockSpec((1, LANES), lambda g: (0, g))
    return pl.pallas_call(
        body, name=name, grid=(4,),
        in_specs=[col, col, wspec, vec],
        out_specs=[col, wspec, vec],
        out_shape=[_sds((L, 512)), _sds((4, LANES, LANES)), _sds((1, 512))],
        compiler_params=_cp("parallel"),
    )(g_m, pooled, pool_w, scale)


SGU_CHUNKS = 4


def _sgu_ln(v, gam, bet):
    gv = _gelu(v)
    mu = jnp.mean(gv, axis=-1, keepdims=True)
    xc = gv - mu
    rs = lax.rsqrt(jnp.mean(xc * xc, axis=-1, keepdims=True) + EPS)
    xh = xc * rs
    return xh, rs, xh * gam + bet


def _tril_ws(w_ref, g):
    r = lax.broadcasted_iota(jnp.int32, (LANES, LANES), 0)
    c = lax.broadcasted_iota(jnp.int32, (LANES, LANES), 1)
    return jnp.where(r >= c, w_ref[g], 0.0).astype(BF16)


def _sgu_fwd(z, ln_g, ln_b, w_s, b_st, name):
    L = z.shape[0]
    rb = min(SGU_CHUNKS * LANES, L)

    def body(u_ref, v_ref, g_ref, b_ref, w_ref, bs_ref, y_ref):
        _, _, vln = _sgu_ln(v_ref[...], g_ref[...], b_ref[...])
        gu = _gelu(u_ref[...])
        vb = vln.astype(BF16)
        for g in range(4):
            ws = _tril_ws(w_ref, g)
            for n in range(rb // LANES):
                rows = slice(n * LANES, (n + 1) * LANES)
                cols = slice(g * LANES, (g + 1) * LANES)
                mixed = _dot(ws, vb[rows, cols]) + bs_ref[:, g:g + 1]
                y_ref[rows, cols] = (gu[rows, cols] * mixed).astype(BF16)

    vm = lambda shape: pl.BlockSpec(shape, lambda i: tuple(0 for _ in shape))
    return pl.pallas_call(
        body, name=name, grid=(L // rb,),
        in_specs=[pl.BlockSpec((rb, 512), lambda i: (i, 1)), pl.BlockSpec((rb, 512), lambda i: (i, 2)),
                  vm((1, 512)), vm((1, 512)), vm((4, LANES, LANES)), vm((LANES, 4))],
        out_specs=pl.BlockSpec((rb, 512), lambda i: (i, 0)),
        out_shape=_sds((L, 512), BF16),
        compiler_params=_cp("parallel"),
    )(z, z, ln_g, ln_b, w_s, b_st)


def _sgu_bwd(g_m, z, ln_g, ln_b, w_s, b_st, name):
    L = z.shape[0]
    rb = min(SGU_CHUNKS * LANES, L)

    def body(gy_ref, u_ref, v_ref, g_ref, b_ref, w_ref, bs_ref, gu_ref, gv_ref, gw_ref, gbs_ref, gg_ref, gb_ref):
        i = pl.program_id(0)

        @pl.when(i == 0)
        def _():
            gw_ref[...] = jnp.zeros_like(gw_ref)
            gbs_ref[...] = jnp.zeros_like(gbs_ref)
            gg_ref[...] = jnp.zeros_like(gg_ref)
            gb_ref[...] = jnp.zeros_like(gb_ref)

        v = v_ref[...]
        u = u_ref[...]
        gy = gy_ref[...]
        xh, rs, vln = _sgu_ln(v, g_ref[...], b_ref[...])
        gel_u = _gelu(u)
        gmix = gy * gel_u
        vb = vln.astype(BF16)
        gmb = gmix.astype(BF16)
        r = lax.broadcasted_iota(jnp.int32, (LANES, LANES), 0)
        c = lax.broadcasted_iota(jnp.int32, (LANES, LANES), 1)
        gvln_cols = []
        for g in range(4):
            ws = _tril_ws(w_ref, g)
            cols = slice(g * LANES, (g + 1) * LANES)
            gw = jnp.zeros((LANES, LANES), F32)
            gbs = jnp.zeros((LANES, 1), F32)
            parts = []
            for n in range(rb // LANES):
                rows = slice(n * LANES, (n + 1) * LANES)
                mixed = _dot(ws, vb[rows, cols]) + bs_ref[:, g:g + 1]
                gu_ref[rows, cols] = gy[rows, cols] * mixed * _gelu_grad(u[rows, cols])
                parts.append(_dot(ws, gmb[rows, cols], TN))
                gw = gw + _dot(gmb[rows, cols], vb[rows, cols], NT)
                gbs = gbs + jnp.sum(gmix[rows, cols], axis=1, keepdims=True)
            gvln_cols.append(jnp.concatenate(parts, axis=0))
            gw_ref[g] += jnp.where(r >= c, gw, 0.0)
            gbs_ref[:, g:g + 1] += gbs
        gvln = jnp.concatenate(gvln_cols, axis=1)
        gg_ref[...] += jnp.sum(gvln * xh, axis=0, keepdims=True)
        gb_ref[...] += jnp.sum(gvln, axis=0, keepdims=True)
        gxh = gvln * g_ref[...]
        ggv = rs * (gxh - jnp.mean(gxh, axis=-1, keepdims=True) - xh * jnp.mean(gxh * xh, axis=-1, keepdims=True))
        gv_ref[...] = ggv * _gelu_grad(v)

    vm = lambda shape: pl.BlockSpec(shape, lambda i: tuple(0 for _ in shape))
    blk = pl.BlockSpec((rb, 512), lambda i: (i, 0))
    return pl.pallas_call(
        body, name=name, grid=(L // rb,),
        in_specs=[pl.BlockSpec((rb, 512), lambda i: (i, 1)), pl.BlockSpec((rb, 512), lambda i: (i, 1)),
                  pl.BlockSpec((rb, 512), lambda i: (i, 2)),
                  vm((1, 512)), vm((1, 512)), vm((4, LANES, LANES)), vm((LANES, 4))],
        out_specs=[blk, blk, vm((4, LANES, LANES)), vm((LANES, 4)), vm((1, 512)), vm((1, 512))],
        out_shape=[_sds((L, 512)), _sds((L, 512)), _sds((4, LANES, LANES)), _sds((LANES, 4)),
                   _sds((1, 512)), _sds((1, 512))],
        compiler_params=_cp("arbitrary"),
    )(g_m, z, z, ln_g, ln_b, w_s, b_st)


def _adamw_math(w, g, m, v):
    nm = ADAM_B1 * m + (1.0 - ADAM_B1) * g
    nv = ADAM_B2 * v + (1.0 - ADAM_B2) * (g * g)
    m_hat = nm / (1.0 - ADAM_B1 ** ADAM_STEP)
    v_hat = nv / (1.0 - ADAM_B2 ** ADAM_STEP)
    delta = -ADAM_LR * (m_hat / (jnp.sqrt(v_hat) + ADAM_EPS) + ADAM_WD * w)
    return delta, nm, nv


def _sum_adamw(parts, w, m, v, name, layer=0, prev=None):
    n_layers, R, C = w.shape
    rb = 128 if R % 128 == 0 else R

    def body(p_ref, w_ref, m_ref, v_ref, *rest):
        g_ref, d_ref, nm_ref, nv_ref = rest[-4:]
        g = p_ref[0]
        for s in range(1, N_DEV):
            g = g + p_ref[s]
        d, nm, nv = _adamw_math(w_ref[...], g, m_ref[...], v_ref[...])
        g_ref[...] = g
        d_ref[...] = d
        nm_ref[...] = nm
        nv_ref[...] = nv

    blk = pl.BlockSpec((None, rb, C), lambda i: (layer, i, 0))
    prev = [] if prev is None else list(prev)
    return pl.pallas_call(
        body, name=name, grid=(R // rb,),
        in_specs=[pl.BlockSpec((N_DEV, rb, C), lambda i: (0, i, 0)), blk, blk, blk] + [ANY] * len(prev),
        out_specs=[blk] * 4, out_shape=[_sds((n_layers, R, C))] * 4,
        input_output_aliases={4 + k: k for k in range(len(prev))},
        compiler_params=_cp("parallel"),
    )(parts, w, m, v, *prev)


def _sum_pieces(parts, name):
    _, R, C = parts.shape

    def body(p_ref, g_ref):
        g = p_ref[0]
        for s in range(1, N_DEV):
            g = g + p_ref[s]
        g_ref[...] = g

    vm = pl.BlockSpec(memory_space=pltpu.VMEM)
    return pl.pallas_call(body, name=name, in_specs=[vm], out_specs=vm, out_shape=_sds((R, C)))(parts)


def _adamw(w, g, m, v, name):
    vm = pl.BlockSpec(memory_space=pltpu.VMEM)

    def body(w_ref, g_ref, m_ref, v_ref, d_ref, nm_ref, nv_ref):
        d, nm, nv = _adamw_math(w_ref[...], g_ref[...], m_ref[...], v_ref[...])
        d_ref[...] = d
        nm_ref[...] = nm
        nv_ref[...] = nv

    return pl.pallas_call(body, name=name, in_specs=[vm] * 4, out_specs=[vm] * 3,
                          out_shape=[_sds(w.shape)] * 3)(w, g, m, v)


def _mesh_pos():
    return lax.axis_index("x"), lax.axis_index("y"), lax.axis_index("c")


def _dev_index(p):
    return 4 * p[0] + 2 * p[1] + p[2]


def _all_gather(xs, name):
    n = len(xs)

    def body(*refs):
        x_refs, o_refs = refs[:n], refs[n:2 * n]
        send_sems, recv_sems, local_sems = refs[2 * n:]
        x, y, c = _mesh_pos()
        me, sibling = (x, y, c), (x, y, 1 - c)
        chips = [(1 - x, y), (x, 1 - y), (1 - x, 1 - y)]

        def copy(i, k, block, to, src=None):
            dst = o_refs[i].at[_dev_index(block)]
            return pltpu.make_async_remote_copy(
                src_ref=dst if src is None else src, dst_ref=dst,
                send_sem=send_sems.at[i, k], recv_sem=recv_sems.at[i, k], device_id=to, device_id_type=MESH)

        mine = [pltpu.make_async_copy(x_refs[i], o_refs[i].at[_dev_index(me)], local_sems.at[i]) for i in range(n)]
        for cp in mine:
            cp.start()
        first = []
        for i in range(n):
            first.append(copy(i, 0, me, sibling, src=x_refs[i]))
            first += [copy(i, 1 + j, me, (*chip, c), src=x_refs[i]) for j, chip in enumerate(chips)]
        for cp in first:
            cp.start()
        passed = []
        for j, chip in enumerate(chips):
            for i in range(n):
                copy(i, 1 + j, (*chip, c), me).wait_recv()
                fwd = copy(i, 4 + j, (*chip, c), sibling)
                fwd.start()
                passed.append(fwd)
        for i in range(n):
            copy(i, 0, sibling, me).wait_recv()
            for j, chip in enumerate(chips):
                copy(i, 4 + j, (*chip, 1 - c), me).wait_recv()
        for cp in first + passed:
            cp.wait_send()
        for cp in mine:
            cp.wait()

    outs = pl.pallas_call(
        body, name=name,
        in_specs=[ANY] * n, out_specs=[ANY] * n,
        out_shape=[_sds((N_DEV,) + x.shape, x.dtype) for x in xs],
        scratch_shapes=[pltpu.SemaphoreType.DMA((n, 7)), pltpu.SemaphoreType.DMA((n, 7)),
                        pltpu.SemaphoreType.DMA((n,))],
    )(*xs)
    return list(outs)


def _exchange(gs, name):
    n = len(gs)

    def body(*refs):
        g_refs, o_refs = refs[:n], refs[n:2 * n]
        send_sems, recv_sems, local_sems = refs[2 * n:]
        x, y, c = _mesh_pos()
        me = (x, y, c)
        mi = _dev_index(me)
        peers = [(x ^ dx, y ^ dy, c ^ dc) for dx in range(2) for dy in range(2) for dc in range(2)][1:]

        def copy(i, k, peer):
            return pltpu.make_async_remote_copy(
                src_ref=g_refs[i].at[_dev_index(peer)], dst_ref=o_refs[i].at[mi],
                send_sem=send_sems.at[i, k], recv_sem=recv_sems.at[i, k], device_id=peer, device_id_type=MESH)

        mine = [pltpu.make_async_copy(g_refs[i].at[mi], o_refs[i].at[mi], local_sems.at[i]) for i in range(n)]
        for cp in mine:
            cp.start()
        sends = [copy(i, k, peer) for i in range(n) for k, peer in enumerate(peers)]
        for cp in sends:
            cp.start()
        for i in range(n):
            for k, peer in enumerate(peers):
                pltpu.make_async_remote_copy(
                    src_ref=g_refs[i].at[mi], dst_ref=o_refs[i].at[_dev_index(peer)],
                    send_sem=send_sems.at[i, k], recv_sem=recv_sems.at[i, k], device_id=peer,
                    device_id_type=MESH).wait_recv()
        for cp in sends:
            cp.wait_send()
        for cp in mine:
            cp.wait()

    outs = pl.pallas_call(
        body, name=name,
        in_specs=[ANY] * n, out_specs=[ANY] * n,
        out_shape=[_sds(g.shape, g.dtype) for g in gs],
        scratch_shapes=[pltpu.SemaphoreType.DMA((n, 7)), pltpu.SemaphoreType.DMA((n, 7)),
                        pltpu.SemaphoreType.DMA((n,))],
    )(*gs)
    return list(outs)


HBM = pl.BlockSpec(memory_space=pltpu.HBM)
SEM = pl.BlockSpec(memory_space=pltpu.SEMAPHORE)
EFFECT = pltpu.SideEffectType.DATAFLOW_SIDE_EFFECTING


def _peer_list():
    x, y, c = _mesh_pos()
    peers = [(x ^ dx, y ^ dy, c ^ dc) for dx in range(2) for dy in range(2) for dc in range(2)][1:]
    return (x, y, c), peers


def _split_copy(src_ref, land_ref, send_sems, recv_sems, i, k, peer, slot, exchange):
    return pltpu.make_async_remote_copy(
        src_ref=src_ref.at[_dev_index(peer)] if exchange else src_ref, dst_ref=land_ref.at[slot],
        send_sem=send_sems.at[7 * i + k], recv_sem=recv_sems.at[7 * i + k], device_id=peer, device_id_type=MESH)


def _comm_start(groups, name, exchange, dep=None):
    sizes = [len(g) for g in groups]
    n = sum(sizes)
    srcs = [a for g in groups for a in g]
    my_index = _dev_index(_mesh_pos())
    lands = []
    for a in srcs:
        if exchange:
            own = lax.dynamic_slice(a, (my_index, 0, 0), (1,) + a.shape[1:])
            shape = a.shape
        else:
            own = a[None]
            shape = (N_DEV,) + a.shape
        lands.append(lax.dynamic_update_slice(lax.empty(shape, a.dtype), own, (my_index, 0, 0)))

    n_dep = 0 if dep is None else 1

    def body(*refs):
        src_refs, land_refs = refs[:n], refs[n:2 * n]
        sem_refs = refs[2 * n + n_dep:2 * n + n_dep + 2 * len(sizes)]
        token_ref = refs[-1]
        me, peers = _peer_list()
        mi = _dev_index(me)
        i = 0
        for gi, sz in enumerate(sizes):
            for j in range(sz):
                for k, peer in enumerate(peers):
                    _split_copy(src_refs[i], land_refs[i], sem_refs[2 * gi], sem_refs[2 * gi + 1], j, k, peer, mi,
                                exchange).start()
                i += 1
        token_ref[...] = jnp.zeros_like(token_ref)

    sem_shapes = []
    for sz in sizes:
        sem_shapes += [pltpu.SemaphoreType.DMA((7 * sz,)), pltpu.SemaphoreType.DMA((7 * sz,))]
    thru = [pltpu.HBM(a.shape, a.dtype) for a in srcs + lands]
    n_sem = len(sem_shapes)
    outs = pl.pallas_call(
        body, name=name,
        out_shape=tuple(sem_shapes + thru + [_sds((8, LANES))]),
        in_specs=[HBM] * (2 * n) + [ANY] * n_dep,
        out_specs=tuple([SEM] * n_sem + [HBM] * (2 * n) + [pl.BlockSpec(memory_space=pltpu.VMEM)]),
        input_output_aliases={i: n_sem + i for i in range(2 * n)},
        compiler_params=pltpu.CompilerParams(has_side_effects=EFFECT),
    )(*[pltpu.with_memory_space_constraint(a, pltpu.HBM) for a in srcs + lands], *([] if dep is None else [dep]))
    sems, thru_src, thru_land, token = outs[:n_sem], outs[n_sem:n_sem + n], outs[n_sem + n:n_sem + 2 * n], outs[-1]
    result, off = [], 0
    for gi, sz in enumerate(sizes):
        result.append((sems[2 * gi], sems[2 * gi + 1], list(thru_src[off:off + sz]), list(thru_land[off:off + sz])))
        off += sz
    return result, token


def _comm_wait(group, after, name, exchange):
    send_sems, recv_sems, srcs, lands = group
    n = len(srcs)
    after = list(after) if isinstance(after, (list, tuple)) else [after]

    def body(*refs):
        src_refs, land_refs = refs[:n], refs[n:2 * n]
        ssem, rsem = refs[2 * n], refs[2 * n + 1]
        me, peers = _peer_list()
        for i in range(n):
            for k, peer in enumerate(peers):
                cp = _split_copy(src_refs[i], land_refs[i], ssem, rsem, i, k, peer, _dev_index(peer), exchange)
                cp.wait_send()
                cp.wait_recv()

    outs = pl.pallas_call(
        body, name=name,
        out_shape=tuple(pltpu.HBM(a.shape, a.dtype) for a in srcs + lands),
        in_specs=[HBM] * (2 * n) + [SEM, SEM] + [ANY] * len(after),
        out_specs=tuple([HBM] * (2 * n)),
        input_output_aliases={i: i for i in range(2 * n)},
        compiler_params=pltpu.CompilerParams(has_side_effects=EFFECT),
    )(*srcs, *lands, send_sems, recv_sems, *after)
    return list(outs[n:])


def _tie(a, token):
    return a + token[0, 0].astype(a.dtype)


def _pack(arrs, rows):
    flat = jnp.concatenate([a.reshape(-1).astype(F32) for a in arrs])
    return jnp.pad(flat, (0, rows * LANES - flat.shape[0])).reshape(rows, LANES)


def _unpack(packed, shapes):
    flat = packed.reshape(-1)
    out, off = [], 0
    for s in shapes:
        n = math.prod(s)
        out.append(flat[off:off + n].reshape(s))
        off += n
    return out


def _packed_rows(shapes):
    n = sum(math.prod(s) for s in shapes)
    unit = N_DEV * 8 * LANES
    return -(-n // unit) * unit // LANES


def kernel(x, mix_pre_g, mix_post_g, mlp_pre_g, mlp_post_g, w_in_even, s5_lam_re, s5_lam_im, s5_log_dt, s5_b_re, s5_b_im, s5_c_re, s5_c_im, s5_d, s5_w_glu, fox_b_f, w_out_even, w_in_odd, pool_w, pool_scale, sgu_ln_g, sgu_ln_b, sgu_w_s, sgu_b_s, w_out_odd, mlp_w1, mlp_w2, loss_target, m_mix_pre_g, m_mix_post_g, m_mlp_pre_g, m_mlp_post_g, m_w_in_even, m_s5_lam_re, m_s5_lam_im, m_s5_log_dt, m_s5_b_re, m_s5_b_im, m_s5_c_re, m_s5_c_im, m_s5_d, m_s5_w_glu, m_fox_b_f, m_w_out_even, m_w_in_odd, m_pool_w, m_pool_scale, m_sgu_ln_g, m_sgu_ln_b, m_sgu_w_s, m_sgu_b_s, m_w_out_odd, m_mlp_w1, m_mlp_w2, v_mix_pre_g, v_mix_post_g, v_mlp_pre_g, v_mlp_post_g, v_w_in_even, v_s5_lam_re, v_s5_lam_im, v_s5_log_dt, v_s5_b_re, v_s5_b_im, v_s5_c_re, v_s5_c_im, v_s5_d, v_s5_w_glu, v_fox_b_f, v_w_out_even, v_w_in_odd, v_pool_w, v_pool_scale, v_sgu_ln_g, v_sgu_ln_b, v_sgu_w_s, v_sgu_b_s, v_w_out_odd, v_mlp_w1, v_mlp_w2):
    weights = dict(mix_pre_g=mix_pre_g, mix_post_g=mix_post_g, mlp_pre_g=mlp_pre_g, mlp_post_g=mlp_post_g, w_in_even=w_in_even, s5_lam_re=s5_lam_re, s5_lam_im=s5_lam_im, s5_log_dt=s5_log_dt, s5_b_re=s5_b_re, s5_b_im=s5_b_im, s5_c_re=s5_c_re, s5_c_im=s5_c_im, s5_d=s5_d, s5_w_glu=s5_w_glu, fox_b_f=fox_b_f, w_out_even=w_out_even, w_in_odd=w_in_odd, pool_w=pool_w, pool_scale=pool_scale, sgu_ln_g=sgu_ln_g, sgu_ln_b=sgu_ln_b, sgu_w_s=sgu_w_s, sgu_b_s=sgu_b_s, w_out_odd=w_out_odd, mlp_w1=mlp_w1, mlp_w2=mlp_w2)
    mom_m = dict(mix_pre_g=m_mix_pre_g, mix_post_g=m_mix_post_g, mlp_pre_g=m_mlp_pre_g, mlp_post_g=m_mlp_post_g, w_in_even=m_w_in_even, s5_lam_re=m_s5_lam_re, s5_lam_im=m_s5_lam_im, s5_log_dt=m_s5_log_dt, s5_b_re=m_s5_b_re, s5_b_im=m_s5_b_im, s5_c_re=m_s5_c_re, s5_c_im=m_s5_c_im, s5_d=m_s5_d, s5_w_glu=m_s5_w_glu, fox_b_f=m_fox_b_f, w_out_even=m_w_out_even, w_in_odd=m_w_in_odd, pool_w=m_pool_w, pool_scale=m_pool_scale, sgu_ln_g=m_sgu_ln_g, sgu_ln_b=m_sgu_ln_b, sgu_w_s=m_sgu_w_s, sgu_b_s=m_sgu_b_s, w_out_odd=m_w_out_odd, mlp_w1=m_mlp_w1, mlp_w2=m_mlp_w2)
    mom_v = dict(mix_pre_g=v_mix_pre_g, mix_post_g=v_mix_post_g, mlp_pre_g=v_mlp_pre_g, mlp_post_g=v_mlp_post_g, w_in_even=v_w_in_even, s5_lam_re=v_s5_lam_re, s5_lam_im=v_s5_lam_im, s5_log_dt=v_s5_log_dt, s5_b_re=v_s5_b_re, s5_b_im=v_s5_b_im, s5_c_re=v_s5_c_re, s5_c_im=v_s5_c_im, s5_d=v_s5_d, s5_w_glu=v_s5_w_glu, fox_b_f=v_fox_b_f, w_out_even=v_w_out_even, w_in_odd=v_w_in_odd, pool_w=v_pool_w, pool_scale=v_pool_scale, sgu_ln_g=v_sgu_ln_g, sgu_ln_b=v_sgu_ln_b, sgu_w_s=v_sgu_w_s, sgu_b_s=v_sgu_b_s, w_out_odd=v_w_out_odd, mlp_w1=v_mlp_w1, mlp_w2=v_mlp_w2)
    names = list(weights)
    L = x.shape[1]
    x0 = x[0]
    target = loss_target[0]
    my_index = 4 * lax.axis_index("x") + 2 * lax.axis_index("y") + lax.axis_index("c")

    small_vec = jnp.zeros((8, LANES), F32)
    small_vec = small_vec.at[0, :64].set(pool_scale[0]).at[1, :64].set(sgu_ln_g[0]).at[2, :64].set(sgu_ln_b[0])
    ag_groups, ag_token = _comm_start(
        [[jnp.transpose(w_in_even[0]).astype(BF16), small_vec],
         [s5_w_glu[0].astype(BF16), w_out_even[0].astype(BF16)],
         [mlp_w1[0].astype(BF16), mlp_w2[0].astype(BF16)],
         [jnp.transpose(w_in_odd[0]).astype(BF16), w_out_odd[0].astype(BF16), mlp_w1[1].astype(BF16), mlp_w2[1].astype(BF16)]],
        "ag_start", exchange=False)

    lam_r = jnp.concatenate([s5_lam_re.reshape(1, S5_NS), s5_lam_im.reshape(1, S5_NS)], axis=0)
    ldt_r = jnp.repeat(s5_log_dt.reshape(32), 64).reshape(1, S5_NS)
    lam_c = jnp.transpose(lam_r)
    ldt_c = jnp.transpose(ldt_r)
    b_t = jnp.stack([jnp.tile(s5_b_re.reshape(S5_NS, 16), (1, 8)), jnp.tile(s5_b_im.reshape(S5_NS, 16), (1, 8))])
    c_t = jnp.stack([jnp.tile(s5_c_re.reshape(S5_W, 64), (1, 8)), jnp.tile(s5_c_im.reshape(S5_W, 64), (1, 8))])
    bf_pad = jnp.pad(fox_b_f, ((0, 0), (0, LANES - 8)))
    b_st = jnp.transpose(sgu_b_s[0])

    h0, rx0 = _rms_fwd(x0, _tie(mix_pre_g[0:1], ag_token), "rms0")
    tabs, bset, cset = _s5_prep(lam_r, ldt_r, lam_c, ldt_c, b_t, c_t, "s5_prep")
    ag0 = _comm_wait(ag_groups[0], tabs, "ag_wait0", exchange=False)
    winT_e = jnp.pad(ag0[0].reshape(EVEN_IN, D_MODEL), ((0, EVEN_PAD - EVEN_IN), (0, 0)))
    pool_scale_f = ag0[1][:, 0, :64].reshape(1, 512)
    ln_g_f = ag0[1][:, 1, :64].reshape(1, 512)
    ln_b_f = ag0[1][:, 2, :64].reshape(1, 512)
    z0 = _mm(h0, winT_e, name="win_even", tb=True, bm=512, bn=EVEN_PAD)
    xs = _s5_scan_fwd(z0, bset, tabs, "s5_scan")
    ag1 = _comm_wait(ag_groups[1], xs, "ag_wait1", exchange=False)
    wglu = ag1[0].reshape(S5_W, S5_W)
    wout_e = ag1[1].reshape(D_MODEL, D_MODEL)
    ylin, ya = _s5_out_fwd(xs, cset, z0, s5_d, wglu, "s5_out")
    fcum, fq = _fox_f_fwd(z0, bf_pad, "fox_f")
    frow = jnp.transpose(fcum[:, :8]).reshape(4, 2, L)
    o_att, lse = _fox_fwd(z0, fq, frow, "fox_fwd")
    mix0 = jnp.concatenate([ya, o_att.astype(BF16)], axis=1)
    y0 = _mm(mix0, wout_e, name="wout_even")
    x1, ry0, h1, rx1 = _post_pre_fwd(x0, y0, mix_post_g[0:1], mlp_pre_g[0:1], "post0")
    ag2 = _comm_wait(ag_groups[2], rx1, "ag_wait2", exchange=False)
    w1 = [ag2[0], None]
    w2 = [ag2[1].reshape(4 * D_MODEL, D_MODEL), None]
    p0, a0 = _mm(h1, w1[0], name="mlp0_w1", b3=True, out_dtypes=(BF16, BF16), epi=_epi_relu2)
    o0 = _mm(a0, w2[0], name="mlp0_w2")
    x2, ro0, h2, rx2 = _post_pre_fwd(x1, o0, mlp_post_g[0:1], mix_pre_g[1:2], "post1")
    ag3 = _comm_wait(ag_groups[3], rx2, "ag_wait3", exchange=False)
    winT_o = ag3[0].reshape(ODD_IN, D_MODEL)
    wout_o = ag3[1].reshape(D_MODEL, D_MODEL)
    w1[1] = ag3[2]
    w2[1] = ag3[3].reshape(4 * D_MODEL, D_MODEL)
    z1 = _mm(h2, winT_o, name="win_odd", tb=True, bn=ODD_IN)
    yc, pooled = _pool_fwd(z1, pool_w[0], pool_scale_f, "pool_fwd")
    yd = _sgu_fwd(z1, ln_g_f, ln_b_f, sgu_w_s[0], b_st, "sgu_fwd")
    mix1 = jnp.concatenate([yc, yd], axis=1)
    y1 = _mm(mix1, wout_o, name="wout_odd")
    x3, ry1, h3, rx3 = _post_pre_fwd(x2, y1, mix_post_g[1:2], mlp_pre_g[1:2], "post2")
    p1, a1 = _mm(h3, w1[1], name="mlp1_w1", b3=True, out_dtypes=(BF16, BF16), epi=_epi_relu2)
    o1 = _mm(a1, w2[1], name="mlp1_w2")
    gx4, ro1, sq = _post_loss_fwd(x3, o1, mlp_post_g[1:2], target, "post3")

    g_o1, gg_mlp_post1 = _post_bwd(gx4, o1, ro1, mlp_post_g[1:2], "bpost3")
    g_p1 = _mm(g_o1, w2[1], name="b_mlp1_a", tb=True, out_dtypes=(BF16,), epi=_epi_relu2_bwd, extra=(p1,))
    gw2_1 = _mm(a1, g_o1, name="b_mlp1_w2", ta=True)
    g_h3 = _mm(g_p1, w1[1], name="b_mlp1_h", tb=True, b3=True)
    gw1_1 = _mm(h3, g_p1, name="b_mlp1_w1", ta=True, out3=True)
    (ex1,), tok1 = _comm_start([[gw1_1, gw2_1.reshape(N_DEV, 512, D_MODEL)]], "ex_start1", exchange=True)
    g_x3, gg_mlp_pre1, g_y1, gg_mix_post1 = _pre_post_bwd(g_h3, x3, rx3, _tie(mlp_pre_g[1:2], tok1), gx4, y1, ry1, mix_post_g[1:2], "bpre3")
    g_mix1 = _mm(g_y1, wout_o, name="b_wout_odd_m", tb=True)
    gwout_o = _mm(mix1, g_y1, name="b_wout_odd_w", ta=True)
    g_xc, g_pool_w, g_pool_scale = _pool_bwd(g_mix1, pooled, pool_w[0], pool_scale_f, "pool_bwd")
    g_u1, g_v1, g_ws, g_bst, g_ln_g, g_ln_b = _sgu_bwd(g_mix1, z1, ln_g_f, ln_b_f, sgu_w_s[0], b_st, "sgu_bwd")
    g_z1 = jnp.concatenate([g_xc, g_u1, g_v1], axis=1).astype(BF16)
    g_h2 = _mm(g_z1, winT_o, name="b_win_odd_h", bk=ODD_IN)
    gwinT_o = _mm(g_z1, h2, name="b_win_odd_w", ta=True, bm=512)
    (ex2,), tok2 = _comm_start([[gwout_o.reshape(N_DEV, 128, D_MODEL), gwinT_o.reshape(N_DEV, ODD_IN // N_DEV, D_MODEL)]], "ex_start2", exchange=True)
    g_x2, gg_mix_pre1, g_o0, gg_mlp_post0 = _pre_post_bwd(g_h2, x2, rx2, _tie(mix_pre_g[1:2], tok2), g_x3, o0, ro0, mlp_post_g[0:1], "bpre2")
    g_p0 = _mm(g_o0, w2[0], name="b_mlp0_a", tb=True, out_dtypes=(BF16,), epi=_epi_relu2_bwd, extra=(p0,))
    gw2_0 = _mm(a0, g_o0, name="b_mlp0_w2", ta=True)
    g_h1 = _mm(g_p0, w1[0], name="b_mlp0_h", tb=True, b3=True)
    gw1_0 = _mm(h1, g_p0, name="b_mlp0_w1", ta=True, out3=True)
    (ex3,), tok3 = _comm_start([[gw1_0, gw2_0.reshape(N_DEV, 512, D_MODEL)]], "ex_start3", exchange=True)
    g_x1, gg_mlp_pre0, g_y0, gg_mix_post0 = _pre_post_bwd(g_h1, x1, rx1, _tie(mlp_pre_g[0:1], tok3), g_x2, y0, ry0, mix_post_g[0:1], "bpre1")
    g_mix0 = _mm(g_y0, wout_e, name="b_wout_even_m", tb=True)
    gwout_e = _mm(mix0, g_y0, name="b_wout_even_w", ta=True)
    gyl, gud, g_wglu, g_d = _s5_glu_bwd(g_mix0, ylin, z0, s5_d, wglu, "s5_glu_bwd")
    (ex4,), tok4 = _comm_start([[gwout_e.reshape(N_DEV, 128, D_MODEL), g_wglu.reshape(N_DEV, 64, S5_W)]], "ex_start4", exchange=True)
    g_u0, ga, gb_raw, gc_raw = _s5_scan_bwd(gyl, _tie(cset, tok4), xs, z0, bset, gud, tabs, "s5_scan_bwd")
    g_lam, g_ldt, g_b, g_c = _s5_param_bwd(lam_c, ldt_c, b_t, gb_raw, jnp.transpose(ga), gc_raw, "s5_param_bwd")
    dq, dk, dv, dfq, dfrow = _fox_bwd(z0, fq, frow, o_att, lse, g_mix0, "fox_bwd")
    dFk = jnp.pad(jnp.transpose(dfrow.reshape(8, L)), ((0, 0), (0, LANES - 8)))
    dfl, db_f = _fox_f_bwd(dFk, dfq, z0, bf_pad, "fox_f_bwd")
    g_z0 = jnp.concatenate([g_u0, dq, dk, dv, dfl], axis=1).astype(BF16)
    g_h0 = _mm(g_z0, winT_e, name="b_win_even_h", bk=EVEN_PAD)
    grad_x, gg_mix_pre0 = _pre_bwd(g_h0, x0, rx0, mix_pre_g[0:1], g_x1, "bpre0")
    gwinT_e = _mm(g_z0, h0, name="b_win_even_w", ta=True, bm=EVEN_PAD, bk=512)

    small_grads = dict(
        mix_pre_g=jnp.concatenate([gg_mix_pre0, gg_mix_pre1]), mix_post_g=jnp.concatenate([gg_mix_post0, gg_mix_post1]),
        mlp_pre_g=jnp.concatenate([gg_mlp_pre0, gg_mlp_pre1]), mlp_post_g=jnp.concatenate([gg_mlp_post0, gg_mlp_post1]),
        s5_lam_re=g_lam[:, 0], s5_lam_im=g_lam[:, 1], s5_log_dt=g_ldt,
        s5_b_re=g_b[0, :, :16], s5_b_im=g_b[1, :, :16], s5_c_re=g_c[0, :, :64], s5_c_im=g_c[1, :, :64],
        s5_d=g_d, fox_b_f=db_f[:, :8], pool_w=g_pool_w, sgu_w_s=g_ws, sgu_b_s=jnp.transpose(g_bst),
        pool_scale=g_pool_scale, sgu_ln_g=g_ln_g, sgu_ln_b=g_ln_b)
    small_names = list(small_grads)
    full_shapes = [(512,) if nm in ("pool_scale", "sgu_ln_g", "sgu_ln_b") else weights[nm].shape for nm in small_names]
    full_shapes.append((1, 1))
    rows = _packed_rows(full_shapes)
    packed = _pack([small_grads[nm] for nm in small_names] + [sq], rows).reshape(N_DEV, rows // N_DEV, LANES)
    (recv_small,) = _exchange([packed], "exchange_small")
    piece = _sum_pieces(recv_small, "sum_small")
    (small_all,) = _all_gather([piece], "ag_small")
    small_full = _unpack(small_all.reshape(rows, LANES), full_shapes)
    loss = 0.5 * small_full.pop()[0, 0] / D_MODEL

    gwinT_e_pieces = gwinT_e[:EVEN_IN].reshape(N_DEV, EVEN_IN // N_DEV, D_MODEL)
    (ex5,), tok5 = _comm_start([[gwinT_e_pieces]], "ex_start5", exchange=True, dep=small_all)
    r_w1_1, r_w2_1 = _comm_wait(ex1, tok5, "ex_wait1", exchange=True)
    r_wout_o, r_win_o = _comm_wait(ex2, tok5, "ex_wait2", exchange=True)
    r_w1_0, r_w2_0 = _comm_wait(ex3, tok5, "ex_wait3", exchange=True)
    r_wout_e, r_wglu = _comm_wait(ex4, tok5, "ex_wait4", exchange=True)
    small_g = {}
    for nm, g in zip(small_names, small_full):
        if nm in ("pool_scale", "sgu_ln_g", "sgu_ln_b"):
            g = lax.dynamic_slice(g, (my_index * 64,), (64,)).reshape(1, 64)
        small_g[nm] = g
    own_shapes = [weights[nm].shape for nm in small_names]
    rows2 = _packed_rows(own_shapes)
    pw = _pack([weights[nm] for nm in small_names], rows2)
    pg = _pack([small_g[nm] for nm in small_names], rows2)
    pm = _pack([mom_m[nm] for nm in small_names], rows2)
    pv = _pack([mom_v[nm] for nm in small_names], rows2)
    pd, pnm, pnv = _adamw(pw, pg, pm, pv, "adamw_small")
    res = {}
    for nm, d_, m_, v_ in zip(small_names, _unpack(pd, own_shapes), _unpack(pnm, own_shapes), _unpack(pnv, own_shapes)):
        res[nm] = (small_g[nm], d_, m_, v_)

    for nm, parts in (("mlp_w1", (r_w1_0, r_w1_1)), ("mlp_w2", (r_w2_0, r_w2_1))):
        first = _sum_adamw(parts[0], weights[nm], mom_m[nm], mom_v[nm], "adamw_%s_0" % nm, layer=0)
        res[nm] = tuple(_sum_adamw(parts[1], weights[nm], mom_m[nm], mom_v[nm], "adamw_%s_1" % nm, layer=1, prev=first))
    big_parts = dict(s5_w_glu=r_wglu, w_out_even=r_wout_e, w_out_odd=r_wout_o)
    for nm, parts in big_parts.items():
        res[nm] = tuple(_sum_adamw(parts, weights[nm], mom_m[nm], mom_v[nm], "adamw_" + nm))
    done = [res[nm][1] for nm in ("mlp_w1", "mlp_w2", "s5_w_glu", "w_out_even", "w_out_odd")]
    for nm, parts in (("w_in_odd", r_win_o), ("w_in_even", None)):
        if parts is None:
            (parts,) = _comm_wait(ex5, done, "ex_wait5", exchange=True)
        outs = _sum_adamw(parts, jnp.transpose(weights[nm], (0, 2, 1)), jnp.transpose(mom_m[nm], (0, 2, 1)),
                          jnp.transpose(mom_v[nm], (0, 2, 1)), "adamw_" + nm)
        res[nm] = tuple(jnp.transpose(o, (0, 2, 1)) for o in outs)
        done.append(res[nm][1])

    grads = [res[nm][0].reshape(weights[nm].shape) for nm in names]
    deltas = [res[nm][1].reshape(weights[nm].shape) for nm in names]
    new_m = [res[nm][2].reshape(weights[nm].shape) for nm in names]
    new_v = [res[nm][3].reshape(weights[nm].shape) for nm in names]
    return (loss, grad_x[None], *grads, *deltas, *new_m, *new_v)
```

```python
import functools
import math

import jax
import jax.numpy as jnp
from jax import lax
from jax.experimental import pallas as pl
from jax.experimental.pallas import tpu as pltpu

F32 = jnp.float32
BF16 = jnp.bfloat16
MESH = pl.DeviceIdType.MESH
ANY = pl.BlockSpec(memory_space=pl.ANY)

N_DEV = 8
D_MODEL = 1024
EPS = 1e-6
NORM_ROWS = 512
S5_W = 512
S5_NS = 2048
SCAN_GROUPS = 4
SCAN_CHUNK = 1024
FOX_W = 512
EVEN_IN = 2056
EVEN_PAD = 2176
ODD_IN = 1536
LANES = 128
PIECE = 4 * D_MODEL // N_DEV
VMEM_LIMIT = 56 * 1024 * 1024

ADAM_LR = 0.001
ADAM_B1 = 0.9
ADAM_B2 = 0.999
ADAM_EPS = 1e-08
ADAM_WD = 0.01
ADAM_STEP = 10

NT = (((1,), (1,)), ((), ()))
TN = (((0,), (0,)), ((), ()))
NN = (((1,), (0,)), ((), ()))


def _cp(*sem):
    return pltpu.CompilerParams(dimension_semantics=sem, vmem_limit_bytes=VMEM_LIMIT)


def _sds(shape, dtype=F32):
    return jax.ShapeDtypeStruct(tuple(shape), dtype)


def _gelu(x):
    t = jnp.tanh(0.7978845608028654 * (x + 0.044715 * x * x * x))
    return 0.5 * x * (1.0 + t)


def _gelu_grad(x):
    t = jnp.tanh(0.7978845608028654 * (x + 0.044715 * x * x * x))
    du = 0.7978845608028654 * (1.0 + 3.0 * 0.044715 * x * x)
    return 0.5 * (1.0 + t) + 0.5 * x * (1.0 - t * t) * du


def _sigmoid(x):
    return 1.0 / (1.0 + jnp.exp(-x))


def _dot(a, b, dn=NN):
    return lax.dot_general(a, b, dn, preferred_element_type=F32)


def _mm(a, b, *, name, ta=False, tb=False, b3=False, out3=False, out_dtypes=(F32,), epi=None, extra=(),
        bm=1024, bn=1024, bk=1024):
    a_list = list(a) if isinstance(a, (list, tuple)) else [a]
    widths = [p.shape[1] for p in a_list]
    offs = [sum(widths[:i]) for i in range(len(widths))]
    na = len(a_list)
    M = sum(widths) if ta else a_list[0].shape[0]
    K = a_list[0].shape[0] if ta else sum(widths)
    if na > 1:
        assert not b3 and not tb
        bm, bk = (M, bk) if ta else (bm, K)
    pw = b.shape[2] if b3 else PIECE
    if b3:
        N = b.shape[1] if tb else b.shape[0] * pw
        assert (b.shape[0] * pw if tb else b.shape[1]) == K
    else:
        N = b.shape[0] if tb else b.shape[1]
    bm, bn, bk = min(bm, M), min(bn, N), min(bk, K)
    assert M % bm == 0 and N % bn == 0 and K % bk == 0, (name, M, N, K, bm, bn, bk)
    assert not (b3 or out3) or ((bk if tb else bn) % pw == 0 and bn % PIECE == 0)
    nk = K // bk
    n_extra = len(extra)
    n_out = len(out_dtypes)
    dn = (((0 if ta else 1,), (1 if tb else 0,)), ((), ()))

    in_place = nk > 1 and epi is None and not out3 and tuple(out_dtypes) == (F32,)
    use_acc = nk > 1 and not in_place

    def body(*refs):
        a_refs, b_ref = refs[:na], refs[na]
        a_ref = a_refs[0]
        e_refs = refs[na + 1:na + 1 + n_extra]
        o_refs = refs[na + 1 + n_extra:na + 1 + n_extra + n_out]
        acc_ref = refs[-1] if use_acc else o_refs[0]
        k = pl.program_id(2)

        def dot(a_v, b_v):
            return lax.dot_general(a_v.astype(BF16), b_v.astype(BF16), dn, preferred_element_type=F32)

        if na > 1 and ta:
            prod = jnp.concatenate([dot(r[...], b_ref[...]) for r in a_refs], axis=0)
        elif na > 1:
            prod = None
            for r, off, w in zip(a_refs, offs, widths):
                term = dot(r[...], b_ref[pl.ds(off, w), :])
                prod = term if prod is None else prod + term
        elif not b3:
            prod = dot(a_ref[...], b_ref[...])
        elif tb:
            prod = None
            for t in range(bk // pw):
                a_t = a_ref[pl.ds(t * pw, pw), :] if ta else a_ref[:, pl.ds(t * pw, pw)]
                term = dot(a_t, b_ref[t])
                prod = term if prod is None else prod + term
        else:
            a_v = a_ref[...].astype(BF16)
            prod = jnp.concatenate([dot(a_v, b_ref[t]) for t in range(bn // pw)], axis=1)

        def finish(acc):
            outs = (acc,) if epi is None else epi(acc, *[e[...] for e in e_refs])
            for o_ref, o in zip(o_refs, outs):
                if out3:
                    for t in range(bn // PIECE):
                        o_ref[t] = o[:, t * PIECE:(t + 1) * PIECE].astype(o_ref.dtype)
                else:
                    o_ref[...] = o.astype(o_ref.dtype)

        if nk == 1:
            finish(prod)
            return

        @pl.when(k == 0)
        def _():
            acc_ref[...] = prod

        @pl.when(k > 0)
        def _():
            acc_ref[...] += prod

        if use_acc:
            @pl.when(k == nk - 1)
            def _():
                finish(acc_ref[...])

    if na > 1:
        a_specs = [pl.BlockSpec((bk, w), lambda i, j, k: (k, 0)) if ta else pl.BlockSpec((bm, w), lambda i, j, k: (i, 0))
                   for w in widths]
    else:
        a_specs = [pl.BlockSpec((bk, bm), lambda i, j, k: (k, i)) if ta else
                   pl.BlockSpec((bm, bk), lambda i, j, k: (i, k))]
    if b3:
        if tb:
            b_spec = pl.BlockSpec((bk // pw, bn, pw), lambda i, j, k: (k, j, 0))
        else:
            b_spec = pl.BlockSpec((bn // pw, bk, pw), lambda i, j, k: (j, k, 0))
    else:
        b_spec = pl.BlockSpec((bn, bk), lambda i, j, k: (j, k)) if tb else pl.BlockSpec((bk, bn), lambda i, j, k: (k, j))
    e_specs = [pl.BlockSpec((bm, bn), lambda i, j, k: (i, j)) for _ in extra]
    if out3:
        o_specs = [pl.BlockSpec((bn // PIECE, bm, PIECE), lambda i, j, k: (j, i, 0)) for _ in out_dtypes]
        o_shapes = [_sds((N // PIECE, M, PIECE), dt) for dt in out_dtypes]
    else:
        o_specs = [pl.BlockSpec((bm, bn), lambda i, j, k: (i, j)) for _ in out_dtypes]
        o_shapes = [_sds((M, N), dt) for dt in out_dtypes]
    outs = pl.pallas_call(
        body, name=name, grid=(M // bm, N // bn, nk),
        in_specs=a_specs + [b_spec] + e_specs, out_specs=o_specs, out_shape=o_shapes,
        scratch_shapes=[pltpu.VMEM((bm, bn), F32)] if use_acc else [],
        compiler_params=_cp("parallel", "parallel", "arbitrary"),
    )(*a_list, b, *extra)
    return outs[0] if n_out == 1 else outs


def _epi_relu2(acc):
    r = jnp.maximum(acc, 0.0)
    return acc, r * r


def _epi_relu2_bwd(acc, p):
    return (acc * (2.0 * jnp.maximum(p.astype(F32), 0.0)),)


def _row_spec(rb, w=D_MODEL):
    return pl.BlockSpec((rb, w), lambda i: (i, 0))


def _vec_spec(w=D_MODEL):
    return pl.BlockSpec((1, w), lambda i: (0, 0))


def _rstd(v):
    return lax.rsqrt(jnp.mean(v * v, axis=-1, keepdims=True) + EPS)


def _rms_fwd(x, g, name):
    L = x.shape[0]
    rb = min(NORM_ROWS, L)

    def body(x_ref, g_ref, h_ref, r_ref):
        xv = x_ref[...]
        r = _rstd(xv)
        h_ref[...] = (xv * r * g_ref[...]).astype(BF16)
        r_ref[...] = r

    return pl.pallas_call(
        body, name=name, grid=(L // rb,),
        in_specs=[_row_spec(rb), _vec_spec()],
        out_specs=[_row_spec(rb), _row_spec(rb, 1)],
        out_shape=[_sds((L, D_MODEL), BF16), _sds((L, 1))],
        compiler_params=_cp("parallel"),
    )(x, g)


def _post_pre_fwd(x_in, y, g_post, g_pre, name):
    L = x_in.shape[0]
    rb = min(NORM_ROWS, L)

    def body(x_ref, y_ref, gp_ref, gn_ref, xo_ref, ry_ref, h_ref, rx_ref):
        yv = y_ref[...]
        ry = _rstd(yv)
        xo = x_ref[...] + yv * ry * gp_ref[...]
        rx = _rstd(xo)
        xo_ref[...] = xo
        ry_ref[...] = ry
        h_ref[...] = (xo * rx * gn_ref[...]).astype(BF16)
        rx_ref[...] = rx

    return pl.pallas_call(
        body, name=name, grid=(L // rb,),
        in_specs=[_row_spec(rb), _row_spec(rb), _vec_spec(), _vec_spec()],
        out_specs=[_row_spec(rb), _row_spec(rb, 1), _row_spec(rb), _row_spec(rb, 1)],
        out_shape=[_sds((L, D_MODEL)), _sds((L, 1)), _sds((L, D_MODEL), BF16), _sds((L, 1))],
        compiler_params=_cp("parallel"),
    )(x_in, y, g_post, g_pre)


def _post_loss_fwd(x_in, y, g_post, target, name):
    L = x_in.shape[0]
    rb = min(NORM_ROWS, L)

    def body(x_ref, y_ref, gp_ref, t_ref, gx_ref, ry_ref, loss_ref):
        i = pl.program_id(0)
        yv = y_ref[...]
        ry = _rstd(yv)
        diff = x_ref[...] + yv * ry * gp_ref[...] - t_ref[...]
        gx_ref[...] = diff * (1.0 / D_MODEL)
        ry_ref[...] = ry

        @pl.when(i == 0)
        def _():
            loss_ref[...] = jnp.zeros_like(loss_ref)

        loss_ref[...] += jnp.sum(diff * diff, keepdims=True)

    return pl.pallas_call(
        body, name=name, grid=(L // rb,),
        in_specs=[_row_spec(rb), _row_spec(rb), _vec_spec(), _row_spec(rb)],
        out_specs=[_row_spec(rb), _row_spec(rb, 1), pl.BlockSpec((1, 1), lambda i: (0, 0))],
        out_shape=[_sds((L, D_MODEL)), _sds((L, 1)), _sds((1, 1))],
        compiler_params=_cp("arbitrary"),
    )(x_in, y, g_post, target)


def _rms_bwd_rows(dy, xv, r, g):
    n = xv * r
    dyg = dy * g
    return r * (dyg - n * jnp.mean(dyg * n, axis=-1, keepdims=True)), n


def _post_bwd(g_out, y, ry, g_post, name):
    L = y.shape[0]
    rb = min(NORM_ROWS, L)

    def body(go_ref, y_ref, ry_ref, gp_ref, gy_ref, gg_ref):
        i = pl.program_id(0)
        go = go_ref[...]
        gy, n = _rms_bwd_rows(go, y_ref[...], ry_ref[...], gp_ref[...])
        gy_ref[...] = gy.astype(BF16)

        @pl.when(i == 0)
        def _():
            gg_ref[...] = jnp.zeros_like(gg_ref)

        gg_ref[...] += jnp.sum(go * n, axis=0, keepdims=True)

    return pl.pallas_call(
        body, name=name, grid=(L // rb,),
        in_specs=[_row_spec(rb), _row_spec(rb), _row_spec(rb, 1), _vec_spec()],
        out_specs=[_row_spec(rb), _vec_spec()],
        out_shape=[_sds((L, D_MODEL), BF16), _sds((1, D_MODEL))],
        compiler_params=_cp("arbitrary"),
    )(g_out, y, ry, g_post)


def _pre_post_bwd(g_h, x, rx, g_pre, g_out, y_prev, ry_prev, g_post_prev, name):
    L = x.shape[0]
    rb = min(NORM_ROWS, L)

    def body(gh_ref, x_ref, rx_ref, gn_ref, go_ref, y_ref, ry_ref, gp_ref, gi_ref, ggn_ref, gy_ref, ggp_ref):
        i = pl.program_id(0)
        gh = gh_ref[...]
        gx, n = _rms_bwd_rows(gh, x_ref[...], rx_ref[...], gn_ref[...])
        gi = go_ref[...] + gx
        gi_ref[...] = gi
        gy, ny = _rms_bwd_rows(gi, y_ref[...], ry_ref[...], gp_ref[...])
        gy_ref[...] = gy.astype(BF16)

        @pl.when(i == 0)
        def _():
            ggn_ref[...] = jnp.zeros_like(ggn_ref)
            ggp_ref[...] = jnp.zeros_like(ggp_ref)

        ggn_ref[...] += jnp.sum(gh * n, axis=0, keepdims=True)
        ggp_ref[...] += jnp.sum(gi * ny, axis=0, keepdims=True)

    return pl.pallas_call(
        body, name=name, grid=(L // rb,),
        in_specs=[_row_spec(rb), _row_spec(rb), _row_spec(rb, 1), _vec_spec(), _row_spec(rb),
                  _row_spec(rb), _row_spec(rb, 1), _vec_spec()],
        out_specs=[_row_spec(rb), _vec_spec(), _row_spec(rb), _vec_spec()],
        out_shape=[_sds((L, D_MODEL)), _sds((1, D_MODEL)), _sds((L, D_MODEL), BF16), _sds((1, D_MODEL))],
        compiler_params=_cp("arbitrary"),
    )(g_h, x, rx, g_pre, g_out, y_prev, ry_prev, g_post_prev)


def _pre_bwd(g_h, x, rx, g_pre, g_out, name):
    L = x.shape[0]
    rb = min(NORM_ROWS, L)

    def body(gh_ref, x_ref, rx_ref, gn_ref, go_ref, gi_ref, ggn_ref):
        i = pl.program_id(0)
        gh = gh_ref[...]
        gx, n = _rms_bwd_rows(gh, x_ref[...], rx_ref[...], gn_ref[...])
        gi_ref[...] = go_ref[...] + gx

        @pl.when(i == 0)
        def _():
            ggn_ref[...] = jnp.zeros_like(ggn_ref)

        ggn_ref[...] += jnp.sum(gh * n, axis=0, keepdims=True)

    return pl.pallas_call(
        body, name=name, grid=(L // rb,),
        in_specs=[_row_spec(rb), _row_spec(rb), _row_spec(rb, 1), _vec_spec(), _row_spec(rb)],
        out_specs=[_row_spec(rb), _vec_spec()],
        out_shape=[_sds((L, D_MODEL)), _sds((1, D_MODEL))],
        compiler_params=_cp("arbitrary"),
    )(g_h, x, rx, g_pre, g_out)


def _cmul(ar, ai, br, bi):
    return ar * br - ai * bi, ar * bi + ai * br


def _zoh_cols(lr, li, ldt):
    dt = jnp.exp(ldt)
    mag = jnp.exp(lr * dt)
    ar = mag * jnp.cos(li * dt)
    ai = mag * jnp.sin(li * dt)
    den = lr * lr + li * li
    nr = ar - 1.0
    qr = (nr * lr + ai * li) / den
    qi = (ai * lr - nr * li) / den
    return dt, ar, ai, qr, qi, den


def _b_mask():
    r = lax.broadcasted_iota(jnp.int32, (S5_NS, LANES), 0)
    c = lax.broadcasted_iota(jnp.int32, (S5_NS, LANES), 1)
    return ((r >> 6) & 7) == (c >> 4)


def _c_mask():
    r = lax.broadcasted_iota(jnp.int32, (S5_W, 512), 0)
    c = lax.broadcasted_iota(jnp.int32, (S5_W, 512), 1)
    return ((r >> 4) & 7) == (c >> 6)


def _s5_prep(lam_r, ldt_r, lam_c, ldt_c, b_t, c_t, name):
    def body(lam_r_ref, ldt_r_ref, lam_c_ref, ldt_c_ref, b_ref, c_ref, tab_ref, bset_ref, cset_ref):
        lr, li = lam_r_ref[0:1, :], lam_r_ref[1:2, :]
        dt = jnp.exp(ldt_r_ref[...])
        mag = jnp.exp(lr * dt)
        p1r, p1i = mag * jnp.cos(li * dt), mag * jnp.sin(li * dt)
        p2r, p2i = _cmul(p1r, p1i, p1r, p1i)
        p3r, p3i = _cmul(p2r, p2i, p1r, p1i)
        p4r, p4i = _cmul(p2r, p2i, p2r, p2i)
        p5r, p5i = _cmul(p4r, p4i, p1r, p1i)
        p6r, p6i = _cmul(p4r, p4i, p2r, p2i)
        p7r, p7i = _cmul(p4r, p4i, p3r, p3i)
        p8r, p8i = _cmul(p4r, p4i, p4r, p4i)
        pw_r = [p1r, p2r, p3r, p4r, p5r, p6r, p7r, p8r]
        pw_i = [p1i, p2i, p3i, p4i, p5i, p6i, p7i, p8i]
        row = lax.broadcasted_iota(jnp.int32, (8, S5_NS), 0)
        zero = jnp.zeros((8, S5_NS), F32)

        def bc(v):
            return jnp.broadcast_to(v, (8, S5_NS))

        for d in range(2):
            sgn = 1.0 if d == 0 else -1.0
            for t, s in enumerate((1, 2, 4)):
                live = (row >= s) if d == 0 else (row <= 7 - s)
                tab_ref[d, 2 * t] = jnp.where(live, bc(pw_r[s - 1]), zero)
                tab_ref[d, 2 * t + 1] = jnp.where(live, bc(sgn * pw_i[s - 1]), zero)
            cr, ci = zero, zero
            for i in range(8):
                e = i if d == 0 else 7 - i
                cr = jnp.where(row == i, bc(pw_r[e]), cr)
                ci = jnp.where(row == i, bc(sgn * pw_i[e]), ci)
            tab_ref[d, 6] = cr
            tab_ref[d, 7] = ci

        _, _, _, qr, qi, _ = _zoh_cols(lam_c_ref[:, 0:1], lam_c_ref[:, 1:2], ldt_c_ref[...])
        bm = _b_mask()
        br, bi = b_ref[0], b_ref[1]
        bset_ref[0] = jnp.where(bm, qr * br - qi * bi, 0.0).astype(BF16)
        bset_ref[1] = jnp.where(bm, qr * bi + qi * br, 0.0).astype(BF16)
        cm = _c_mask()
        cset_ref[0] = jnp.where(cm, c_ref[0], 0.0).astype(BF16)
        cset_ref[1] = jnp.where(cm, c_ref[1], 0.0).astype(BF16)

    vm = pl.BlockSpec(memory_space=pltpu.VMEM)
    return pl.pallas_call(
        body, name=name, in_specs=[vm] * 6, out_specs=[vm] * 3,
        out_shape=[_sds((2, 8, 8, S5_NS)), _sds((2, S5_NS, LANES), BF16), _sds((2, S5_W, 512), BF16)],
        compiler_params=pltpu.CompilerParams(vmem_limit_bytes=VMEM_LIMIT),
    )(lam_r, ldt_r, lam_c, ldt_c, b_t, c_t)


SCAN_W = SCAN_GROUPS * LANES


def _scan_chunk(src_ref, dst_ref, tab_ref, carry_ref, nb, reverse, xs_ref=None, acc_ref=None):
    row = lax.broadcasted_iota(jnp.int32, (8, LANES), 0)

    def step(i, carry):
        b = (nb - 1 - i) if reverse else i
        off = pl.multiple_of(b * 8, 8)
        out = []
        for g in range(SCAN_GROUPS):
            lanes = pl.ds(g * LANES, LANES)
            cr, ci = carry[2 * g], carry[2 * g + 1]
            yr = src_ref[0, pl.ds(off, 8), lanes]
            yi = src_ref[1, pl.ds(off, 8), lanes]
            for t, s in enumerate((1, 2, 4)):
                sh = (8 - s) if reverse else s
                sr = pltpu.roll(yr, sh, 0)
                si = pltpu.roll(yi, sh, 0)
                mr, mi = tab_ref[2 * t, :, lanes], tab_ref[2 * t + 1, :, lanes]
                yr, yi = yr + mr * sr - mi * si, yi + mr * si + mi * sr
            pr, pi = tab_ref[6, :, lanes], tab_ref[7, :, lanes]
            yr, yi = yr + pr * cr - pi * ci, yi + pr * ci + pi * cr
            dst_ref[0, pl.ds(off, 8), lanes] = yr
            dst_ref[1, pl.ds(off, 8), lanes] = yi
            if xs_ref is not None:
                nr = jnp.where(row == 7, cr, pltpu.roll(yr, 7, 0))
                ni = jnp.where(row == 7, ci, pltpu.roll(yi, 7, 0))
                xr = xs_ref[0, pl.ds(off, 8), lanes]
                xi = xs_ref[1, pl.ds(off, 8), lanes]
                acc_ref[0, :, lanes] += xr * nr + xi * ni
                acc_ref[1, :, lanes] += xr * ni - xi * nr
            last = 0 if reverse else 7
            out += [jnp.broadcast_to(yr[last:last + 1, :], (8, LANES)),
                    jnp.broadcast_to(yi[last:last + 1, :], (8, LANES))]
        return tuple(out)

    init = []
    for g in range(SCAN_GROUPS):
        init += [carry_ref[0, :, pl.ds(g * LANES, LANES)], carry_ref[1, :, pl.ds(g * LANES, LANES)]]
    fin = lax.fori_loop(0, nb, step, tuple(init))
    for g in range(SCAN_GROUPS):
        carry_ref[0, :, pl.ds(g * LANES, LANES)] = fin[2 * g]
        carry_ref[1, :, pl.ds(g * LANES, LANES)] = fin[2 * g + 1]


def _s5_scan_fwd(z, bset, tabs, name):
    L = z.shape[0]
    tl = min(SCAN_CHUNK, L)
    nc = L // tl

    def body(u_ref, b_ref, tab_ref, x_ref, carry_ref):
        @pl.when(pl.program_id(1) == 0)
        def _():
            carry_ref[...] = jnp.zeros_like(carry_ref)

        u = u_ref[...].astype(BF16)
        x_ref[0] = _dot(u, b_ref[0], NT)
        x_ref[1] = _dot(u, b_ref[1], NT)
        _scan_chunk(x_ref, x_ref, tab_ref, carry_ref, tl // 8, False)

    return pl.pallas_call(
        body, name=name, grid=(S5_NS // SCAN_W, nc),
        in_specs=[pl.BlockSpec((tl, LANES), lambda j, c: (c, j)),
                  pl.BlockSpec((2, SCAN_W, LANES), lambda j, c: (0, j, 0)),
                  pl.BlockSpec((None, 8, 8, SCAN_W), lambda j, c: (0, 0, 0, j))],
        out_specs=pl.BlockSpec((2, tl, SCAN_W), lambda j, c: (0, c, j)),
        out_shape=_sds((2, L, S5_NS)),
        scratch_shapes=[pltpu.VMEM((2, 8, SCAN_W), F32)],
        compiler_params=_cp("parallel", "arbitrary"),
    )(z, bset, tabs)


def _s5_scan_bwd(gyl, cset, xs, z, bset, gud, tabs, name):
    L = z.shape[0]
    tl = min(SCAN_CHUNK, L)
    nc = L // tl

    def body(g_ref, c_ref, xs_ref, u_ref, b_ref, gud_ref, tab_ref, gu_ref, ga_ref, gb_ref, gc_ref,
             gx_ref, carry_ref, acc_ref):
        c = pl.program_id(1)

        @pl.when(c == 0)
        def _():
            carry_ref[...] = jnp.zeros_like(carry_ref)
            acc_ref[...] = jnp.zeros_like(acc_ref)
            gb_ref[...] = jnp.zeros_like(gb_ref)
            gc_ref[...] = jnp.zeros_like(gc_ref)

        gy = g_ref[...].astype(BF16)
        gx_ref[0] = _dot(gy, c_ref[0])
        gx_ref[1] = -_dot(gy, c_ref[1])
        gc_ref[0] += _dot(gy, xs_ref[0].astype(BF16), TN)
        gc_ref[1] -= _dot(gy, xs_ref[1].astype(BF16), TN)
        _scan_chunk(gx_ref, gx_ref, tab_ref, carry_ref, tl // 8, True, xs_ref, acc_ref)
        gr = gx_ref[0].astype(BF16)
        gi = gx_ref[1].astype(BF16)
        gu_ref[...] = gud_ref[...] + _dot(gr, b_ref[0]) + _dot(gi, b_ref[1])
        u = u_ref[...].astype(BF16)
        gb_ref[0] += _dot(gr, u, TN)
        gb_ref[1] += _dot(gi, u, TN)

        @pl.when(c == nc - 1)
        def _():
            ga_ref[0:1, :] = jnp.sum(acc_ref[0], axis=0, keepdims=True)
            ga_ref[1:2, :] = jnp.sum(acc_ref[1], axis=0, keepdims=True)

    rev = lambda j, c: (nc - 1 - c, j)
    col = pl.BlockSpec((tl, LANES), rev)
    return pl.pallas_call(
        body, name=name, grid=(S5_NS // SCAN_W, nc),
        in_specs=[col, pl.BlockSpec((2, LANES, SCAN_W), lambda j, c: (0, j, 0)),
                  pl.BlockSpec((2, tl, SCAN_W), lambda j, c: (0, nc - 1 - c, j)), col,
                  pl.BlockSpec((2, SCAN_W, LANES), lambda j, c: (0, j, 0)), col,
                  pl.BlockSpec((None, 8, 8, SCAN_W), lambda j, c: (1, 0, 0, j))],
        out_specs=[col, pl.BlockSpec((2, SCAN_W), lambda j, c: (0, j)),
                   pl.BlockSpec((2, SCAN_W, LANES), lambda j, c: (0, j, 0)),
                   pl.BlockSpec((2, LANES, SCAN_W), lambda j, c: (0, j, 0))],
        out_shape=[_sds((L, S5_W)), _sds((2, S5_NS)), _sds((2, S5_NS, LANES)), _sds((2, S5_W, 512))],
        scratch_shapes=[pltpu.VMEM((2, tl, SCAN_W), F32), pltpu.VMEM((2, 8, SCAN_W), F32),
                        pltpu.VMEM((2, 8, SCAN_W), F32)],
        compiler_params=_cp("parallel", "arbitrary"),
    )(gyl, cset, xs, z, bset, gud, tabs)


def _s5_out_fwd(xs, cset, z, dvec, wglu, name):
    L = z.shape[0]
    bl = min(256, L)

    def body(x_ref, c_ref, u_ref, d_ref, w_ref, ylin_ref, ya_ref):
        cols = []
        for j in range(4):
            xr = x_ref[0, :, 512 * j:512 * (j + 1)].astype(BF16)
            xi = x_ref[1, :, 512 * j:512 * (j + 1)].astype(BF16)
            cr = c_ref[0, LANES * j:LANES * (j + 1), :]
            ci = c_ref[1, LANES * j:LANES * (j + 1), :]
            cols.append(_dot(xr, cr, NT) - _dot(xi, ci, NT))
        ylin = jnp.concatenate(cols, axis=1) + d_ref[...] * u_ref[...]
        yg = _gelu(ylin)
        t = _dot(yg.astype(BF16), w_ref[...])
        ylin_ref[...] = ylin
        ya_ref[...] = (yg * _sigmoid(t)).astype(BF16)

    return pl.pallas_call(
        body, name=name, grid=(L // bl,),
        in_specs=[pl.BlockSpec((2, bl, S5_NS), lambda i: (0, i, 0)),
                  pl.BlockSpec((2, S5_W, 512), lambda i: (0, 0, 0)),
                  pl.BlockSpec((bl, S5_W), lambda i: (i, 0)),
                  pl.BlockSpec((1, S5_W), lambda i: (0, 0)),
                  pl.BlockSpec((S5_W, S5_W), lambda i: (0, 0))],
        out_specs=[pl.BlockSpec((bl, S5_W), lambda i: (i, 0))] * 2,
        out_shape=[_sds((L, S5_W)), _sds((L, S5_W), BF16)],
        compiler_params=_cp("parallel"),
    )(xs, cset, z, dvec, wglu)


def _s5_glu_bwd(g_m, ylin, z, dvec, wglu, name):
    L = z.shape[0]
    bl = min(256, L)

    def body(g_ref, ylin_ref, u_ref, d_ref, w_ref, gyl_ref, gud_ref, gw_ref, gd_ref):
        i = pl.program_id(0)
        ylin = ylin_ref[...]
        yg = _gelu(ylin)
        ygb = yg.astype(BF16)
        sg = _sigmoid(_dot(ygb, w_ref[...]))
        gya = g_ref[...]
        gt = gya * yg * sg * (1.0 - sg)
        gtb = gt.astype(BF16)
        gyg = gya * sg + _dot(gtb, w_ref[...], NT)
        gyl = gyg * _gelu_grad(ylin)
        gyl_ref[...] = gyl
        gud_ref[...] = gyl * d_ref[...]

        @pl.when(i == 0)
        def _():
            gw_ref[...] = jnp.zeros_like(gw_ref)
            gd_ref[...] = jnp.zeros_like(gd_ref)

        gw_ref[...] += _dot(ygb, gtb, TN)
        gd_ref[...] += jnp.sum(gyl * u_ref[...], axis=0, keepdims=True)

    blk = pl.BlockSpec((bl, S5_W), lambda i: (i, 0))
    return pl.pallas_call(
        body, name=name, grid=(L // bl,),
        in_specs=[blk, blk, blk, pl.BlockSpec((1, S5_W), lambda i: (0, 0)),
                  pl.BlockSpec((S5_W, S5_W), lambda i: (0, 0))],
        out_specs=[blk, blk, pl.BlockSpec((S5_W, S5_W), lambda i: (0, 0)), pl.BlockSpec((1, S5_W), lambda i: (0, 0))],
        out_shape=[_sds((L, S5_W)), _sds((L, S5_W)), _sds((S5_W, S5_W)), _sds((1, S5_W))],
        compiler_params=_cp("arbitrary"),
    )(g_m, ylin, z, dvec, wglu)


def _s5_param_bwd(lam_c, ldt_c, b_t, gb, ga_c, gc, name):
    def body(lam_ref, ldt_ref, b_ref, gb_ref, ga_ref, gc_ref, glam_ref, gldt_ref, gbo_ref, gco_ref):
        lr, li = lam_ref[:, 0:1], lam_ref[:, 1:2]
        dt, ar, ai, qr, qi, den = _zoh_cols(lr, li, ldt_ref[...])
        bm = _b_mask()
        gbr = jnp.where(bm, gb_ref[0], 0.0)
        gbi = jnp.where(bm, gb_ref[1], 0.0)
        br, bi = b_ref[0], b_ref[1]
        obr = gbr * qr + gbi * qi
        obi = gbi * qr - gbr * qi
        gqr = jnp.sum(gbr * br + gbi * bi, axis=1, keepdims=True)
        gqi = jnp.sum(gbi * br - gbr * bi, axis=1, keepdims=True)
        for s in (64, 32, 16):
            obr = obr + pltpu.roll(obr, s, 1)
            obi = obi + pltpu.roll(obi, s, 1)
        gbo_ref[0] = obr
        gbo_ref[1] = obi
        gar = ga_ref[:, 0:1] + (gqr * lr - gqi * li) / den
        gai = ga_ref[:, 1:2] + (gqr * li + gqi * lr) / den
        qlr = (qr * lr + qi * li) / den
        qli = (qi * lr - qr * li) / den
        glr = -(gqr * qlr + gqi * qli)
        gli = -(gqi * qlr - gqr * qli)
        glr = glr + dt * (gar * ar + gai * ai)
        gli = gli + dt * (gai * ar - gar * ai)
        wr, wi = _cmul(lr, li, ar, ai)
        gldt = (gar * wr + gai * wi) * dt
        glam_ref[:, 0:1] = glr
        glam_ref[:, 1:2] = gli
        r = lax.broadcasted_iota(jnp.int32, (S5_NS, 32), 0)
        c = lax.broadcasted_iota(jnp.int32, (S5_NS, 32), 1)
        gldt_ref[...] = jnp.sum(jnp.where((r >> 6) == c, gldt, 0.0), axis=0, keepdims=True)
        cm = _c_mask()
        for k in range(2):
            oc = jnp.where(cm, gc_ref[k], 0.0)
            for s in (256, 128, 64):
                oc = oc + pltpu.roll(oc, s, 1)
            gco_ref[k] = oc[:, 0:LANES]

    vm = pl.BlockSpec(memory_space=pltpu.VMEM)
    return pl.pallas_call(
        body, name=name, in_specs=[vm] * 6, out_specs=[vm] * 4,
        out_shape=[_sds((S5_NS, 2)), _sds((1, 32)), _sds((2, S5_NS, LANES)), _sds((2, S5_W, LANES))],
        compiler_params=pltpu.CompilerParams(vmem_limit_bytes=VMEM_LIMIT),
    )(lam_c, ldt_c, b_t, gb, ga_c, gc)


FL_BLK = EVEN_PAD // LANES - 1
Q_BLK, K_BLK, V_BLK = 4, 8, 12
NEG = -1e30


def _log_sigmoid(v):
    return jnp.minimum(v, 0.0) - jnp.log(1.0 + jnp.exp(-jnp.abs(v)))


def _fox_f_fwd(z, bf, name):
    L = z.shape[0]
    tl = min(256, L)

    def body(fl_ref, b_ref, f_ref, fq_ref, carry_ref):
        i = pl.program_id(0)

        @pl.when(i == 0)
        def _():
            carry_ref[...] = jnp.zeros_like(carry_ref)

        lf = _log_sigmoid(fl_ref[...] + b_ref[...])
        r = lax.broadcasted_iota(jnp.int32, (tl, tl), 0)
        c = lax.broadcasted_iota(jnp.int32, (tl, tl), 1)
        tri = (r >= c).astype(F32)
        cs = lax.dot_general(tri, lf, NN, precision=lax.Precision.HIGHEST, preferred_element_type=F32) + carry_ref[...]
        f_ref[...] = cs
        carry_ref[...] = cs[tl - 1:tl, :]
        expand = (lax.broadcasted_iota(jnp.int32, (LANES, FOX_W), 0)
                  == (lax.broadcasted_iota(jnp.int32, (LANES, FOX_W), 1) >> 6)).astype(F32)
        fq_ref[...] = lax.dot_general(cs, expand, NN, precision=lax.Precision.HIGHEST, preferred_element_type=F32)

    return pl.pallas_call(
        body, name=name, grid=(L // tl,),
        in_specs=[pl.BlockSpec((tl, LANES), lambda i: (i, FL_BLK)), pl.BlockSpec((1, LANES), lambda i: (0, 0))],
        out_specs=[pl.BlockSpec((tl, LANES), lambda i: (i, 0)), pl.BlockSpec((tl, FOX_W), lambda i: (i, 0))],
        out_shape=[_sds((L, LANES)), _sds((L, FOX_W))],
        scratch_shapes=[pltpu.VMEM((1, LANES), F32)],
        compiler_params=_cp("arbitrary"),
    )(z, bf)


def _fox_f_bwd(dFk, dfq, z, bf, name):
    L = z.shape[0]
    tl = min(256, L)
    nb = L // tl

    def body(dfk_ref, dfq_ref, fl_ref, b_ref, dfl_ref, db_ref, carry_ref):
        i = pl.program_id(0)

        @pl.when(i == 0)
        def _():
            carry_ref[...] = jnp.zeros_like(carry_ref)
            db_ref[...] = jnp.zeros_like(db_ref)

        sel = (lax.broadcasted_iota(jnp.int32, (FOX_W, LANES), 0)
               == 64 * lax.broadcasted_iota(jnp.int32, (FOX_W, LANES), 1)).astype(F32)
        dfq_h = lax.dot_general(dfq_ref[...], sel, NN, precision=lax.Precision.HIGHEST, preferred_element_type=F32)
        r = lax.broadcasted_iota(jnp.int32, (tl, tl), 0)
        c = lax.broadcasted_iota(jnp.int32, (tl, tl), 1)
        tri = (r <= c).astype(F32)
        cs = lax.dot_general(tri, dfk_ref[...] + dfq_h, NN, precision=lax.Precision.HIGHEST,
                             preferred_element_type=F32) + carry_ref[...]
        carry_ref[...] = cs[0:1, :]
        dfl = cs * _sigmoid(-(fl_ref[...] + b_ref[...]))
        dfl_ref[...] = dfl
        db_ref[...] += jnp.sum(dfl, axis=0, keepdims=True)

    return pl.pallas_call(
        body, name=name, grid=(nb,),
        in_specs=[pl.BlockSpec((tl, LANES), lambda i: (nb - 1 - i, 0)),
                  pl.BlockSpec((tl, FOX_W), lambda i: (nb - 1 - i, 0)),
                  pl.BlockSpec((tl, LANES), lambda i: (nb - 1 - i, FL_BLK)),
                  pl.BlockSpec((1, LANES), lambda i: (0, 0))],
        out_specs=[pl.BlockSpec((tl, LANES), lambda i: (nb - 1 - i, 0)), pl.BlockSpec((1, LANES), lambda i: (0, 0))],
        out_shape=[_sds((L, LANES)), _sds((1, LANES))],
        scratch_shapes=[pltpu.VMEM((1, LANES), F32)],
        compiler_params=_cp("arbitrary"),
    )(dFk, dfq, z, bf)


def _head_mask(hh):
    lane = lax.broadcasted_iota(jnp.int32, (1, LANES), 1)
    return (lane >> 6) == hh


FOX_T = 512


def _fox_head(x, hh):
    return jnp.where(_head_mask(hh), x, 0.0).astype(BF16)


def _fox_scores(qh, k, fq_ref, fr_ref, hh, causal):
    s = _dot(qh, k, NT) + (fq_ref[:, 64 * hh:64 * hh + 1] - fr_ref[hh:hh + 1, :])
    return s if causal is None else jnp.where(causal, s, NEG)


def _causal(T):
    return lax.broadcasted_iota(jnp.int32, (T, T), 1) <= lax.broadcasted_iota(jnp.int32, (T, T), 0)


def _fox_fwd(z, fq, frow, name):
    L = z.shape[0]
    T = min(FOX_T, L)
    nq = L // T

    def body(qt_ref, kt_ref, q_ref, k_ref, v_ref, fq_ref, fr_ref, o_ref, lse_ref, m_ref, l_ref, acc_ref):
        t = pl.program_id(1)
        qi, ki = qt_ref[t], kt_ref[t]

        @pl.when(ki == 0)
        def _():
            m_ref[...] = jnp.full_like(m_ref, NEG)
            l_ref[...] = jnp.zeros_like(l_ref)
            acc_ref[...] = jnp.zeros_like(acc_ref)

        def step(diagonal):
            q = q_ref[...] * 0.125
            k = k_ref[...].astype(BF16)
            v = v_ref[...].astype(BF16)
            causal = _causal(T) if diagonal else None
            s = jnp.concatenate([_fox_scores(_fox_head(q, hh), k, fq_ref, fr_ref, hh, causal) for hh in range(2)],
                                axis=0)
            m_old = m_ref[...]
            m_new = jnp.maximum(m_old, jnp.max(s, axis=1, keepdims=True))
            alpha = jnp.exp(m_old - m_new)
            p = jnp.exp(s - m_new)
            l_ref[...] = alpha * l_ref[...] + jnp.sum(p, axis=1, keepdims=True)
            m_ref[...] = m_new
            acc_ref[...] = alpha * acc_ref[...] + _dot(p.astype(BF16), v)

        @pl.when(ki < qi)
        def _():
            step(False)

        @pl.when(ki == qi)
        def _():
            step(True)
            h0 = _head_mask(0)
            l = l_ref[...]
            o_h = acc_ref[...] / l
            lse_h = m_ref[...] + jnp.log(l)
            o_ref[...] = jnp.where(h0, o_h[:T], o_h[T:])
            lse_ref[...] = jnp.where(h0, lse_h[:T], lse_h[T:])

    pairs = [(qi, ki) for qi in range(nq) for ki in range(qi + 1)]
    qt = jnp.asarray([p[0] for p in pairs], jnp.int32)
    kt = jnp.asarray([p[1] for p in pairs], jnp.int32)

    def qspec(base):
        return pl.BlockSpec((T, LANES), lambda j, t, qt, kt: (qt[t], base + j))

    def kspec(base):
        return pl.BlockSpec((T, LANES), lambda j, t, qt, kt: (kt[t], base + j))

    return pl.pallas_call(
        body, name=name,
        grid_spec=pltpu.PrefetchScalarGridSpec(
            num_scalar_prefetch=2, grid=(4, len(pairs)),
            in_specs=[qspec(Q_BLK), kspec(K_BLK), kspec(V_BLK), qspec(0),
                      pl.BlockSpec((None, 2, T), lambda j, t, qt, kt: (j, 0, kt[t]))],
            out_specs=[qspec(0), qspec(0)],
            scratch_shapes=[pltpu.VMEM((2 * T, 1), F32), pltpu.VMEM((2 * T, 1), F32),
                            pltpu.VMEM((2 * T, LANES), F32)]),
        out_shape=[_sds((L, FOX_W)), _sds((L, FOX_W))],
        compiler_params=_cp("parallel", "arbitrary"),
    )(qt, kt, z, z, z, fq, frow)


def _fox_bwd(z, fq, frow, o, lse, g_m, name):
    L = z.shape[0]
    T = min(FOX_T, L)
    nq = L // T

    pairs = [(qi, ki) for ki in range(nq) for qi in range(ki, nq)]
    qt = jnp.asarray([p[0] for p in pairs], jnp.int32)
    kt = jnp.asarray([p[1] for p in pairs], jnp.int32)

    def body(qt_ref, kt_ref, q_ref, k_ref, v_ref, fq_ref, fr_ref, o_ref, lse_ref, do_ref,
             dq_ref, dk_ref, dv_ref, dfq_ref, dfk_ref, dk_acc, dv_acc, df_acc):
        t = pl.program_id(1)
        qi, ki = qt_ref[t], kt_ref[t]

        @pl.when(t == 0)
        def _():
            dq_ref[...] = jnp.zeros_like(dq_ref)
            dfq_ref[...] = jnp.zeros_like(dfq_ref)

        @pl.when(qi == ki)
        def _():
            dk_acc[...] = jnp.zeros_like(dk_acc)
            dv_acc[...] = jnp.zeros_like(dv_acc)
            df_acc[...] = jnp.zeros_like(df_acc)

        def step(diagonal):
            q = q_ref[...] * 0.125
            qb = q.astype(BF16)
            k = k_ref[...].astype(BF16)
            v = v_ref[...].astype(BF16)
            do = do_ref[...]
            dob = do.astype(BF16)
            do_o = dob.astype(F32) * o_ref[...]
            causal = _causal(T) if diagonal else None
            dvs, dks, dqs, rss = [], [], [], []
            for hh in range(2):
                s = _fox_scores(_fox_head(q, hh), k, fq_ref, fr_ref, hh, causal)
                p = jnp.exp(s - lse_ref[:, 64 * hh:64 * hh + 1])
                dp = _dot(_fox_head(do, hh), v, NT)
                delta = jnp.sum(jnp.where(_head_mask(hh), do_o, 0.0), axis=1, keepdims=True)
                ds = p * (dp - delta)
                dsb = ds.astype(BF16)
                dvs.append(_dot(p.astype(BF16), dob, TN))
                dks.append(_dot(dsb, qb, TN))
                dqs.append(_dot(dsb, k))
                rss.append(jnp.sum(ds, axis=1, keepdims=True))
                df_acc[hh:hh + 1, :] -= jnp.sum(ds, axis=0, keepdims=True)
            h0 = _head_mask(0)
            dv_acc[...] += jnp.where(h0, dvs[0], dvs[1])
            dk_acc[...] += jnp.where(h0, dks[0], dks[1])
            rows = pl.ds(pl.multiple_of(qi * T, T), T)
            dq_ref[rows, :] += jnp.where(h0, dqs[0], dqs[1])
            dfq_ref[rows, :] += jnp.where(h0, rss[0], rss[1])

        @pl.when(qi > ki)
        def _():
            step(False)

        @pl.when(qi == ki)
        def _():
            step(True)

        @pl.when(qi == nq - 1)
        def _():
            dk_ref[...] = dk_acc[...]
            dv_ref[...] = dv_acc[...]
            dfk_ref[...] = df_acc[...]

        @pl.when(t == len(pairs) - 1)
        def _():
            dq_ref[...] = dq_ref[...] * 0.125

    def qside(base):
        return pl.BlockSpec((T, LANES), lambda j, t, qt, kt: (qt[t], base + j))

    def kside(base):
        return pl.BlockSpec((T, LANES), lambda j, t, qt, kt: (kt[t], base + j))

    pair = pl.BlockSpec((L, LANES), lambda j, t, qt, kt: (0, j))
    frow_spec = pl.BlockSpec((None, 2, T), lambda j, t, qt, kt: (j, 0, kt[t]))
    return pl.pallas_call(
        body, name=name,
        grid_spec=pltpu.PrefetchScalarGridSpec(
            num_scalar_prefetch=2, grid=(4, len(pairs)),
            in_specs=[qside(Q_BLK), kside(K_BLK), kside(V_BLK), qside(0), frow_spec, qside(0), qside(0), qside(4)],
            out_specs=[pair, kside(0), kside(0), pair, frow_spec],
            scratch_shapes=[pltpu.VMEM((T, LANES), F32), pltpu.VMEM((T, LANES), F32), pltpu.VMEM((2, T), F32)]),
        out_shape=[_sds((L, FOX_W)), _sds((L, FOX_W)), _sds((L, FOX_W)), _sds((L, FOX_W)), _sds((4, 2, L))],
        compiler_params=_cp("parallel", "arbitrary"),
    )(qt, kt, z, z, z, fq, frow, o, lse, g_m)


def _shift_rows(v, s, down, row):
    n = v.shape[0]
    if down:
        return jnp.where(row >= s, pltpu.roll(v, s, 0), 0.0)
    return jnp.where(row < n - s, pltpu.roll(v, n - s, 0), 0.0)


def _window_sum(v, g, down, row):
    out = jnp.zeros_like(v)
    s = v
    for k in range(4):
        s = s + _shift_rows(s, 1 << k, down, row)
        out = jnp.where(g == k, s, out)
    return out


def _pool_inv_cnt(g, row):
    w = jnp.left_shift(2, g).astype(F32)
    return 1.0 / jnp.minimum(row.astype(F32) + 1.0, w)


def _pool_fwd(z, pool_w, scale, name):
    L = z.shape[0]

    def body(x_ref, w_ref, s_ref, y_ref, p_ref):
        g = pl.program_id(0)
        row = lax.broadcasted_iota(jnp.int32, (L, LANES), 0)
        x = x_ref[...]
        pooled = (_window_sum(x, g, True, row) * _pool_inv_cnt(g, row) - x).astype(BF16)
        p_ref[...] = pooled
        y_ref[...] = (_dot(pooled, w_ref[...].astype(BF16)) * s_ref[...]).astype(BF16)

    col = pl.BlockSpec((L, LANES), lambda g: (0, g))
    return pl.pallas_call(
        body, name=name, grid=(4,),
        in_specs=[col, pl.BlockSpec((None, LANES, LANES), lambda g: (g, 0, 0)), pl.BlockSpec((1, LANES), lambda g: (0, g))],
        out_specs=[col, col],
        out_shape=[_sds((L, 512), BF16), _sds((L, 512), BF16)],
        compiler_params=_cp("parallel"),
    )(z, pool_w, scale)


def _pool_bwd(g_m, pooled, pool_w, scale, name):
    L = g_m.shape[0]

    def body(g_ref, p_ref, w_ref, s_ref, gx_ref, gw_ref, gs_ref):
        g = pl.program_id(0)
        row = lax.broadcasted_iota(jnp.int32, (L, LANES), 0)
        gy = g_ref[...]
        pooled = p_ref[...]
        wb = w_ref[...].astype(BF16)
        lin = _dot(pooled, wb)
        gs_ref[...] = jnp.sum(gy * lin, axis=0, keepdims=True)
        glin = (gy * s_ref[...]).astype(BF16)
        gw_ref[...] = _dot(pooled, glin, TN)
        gp = _dot(glin, wb, NT)
        gx_ref[...] = _window_sum(gp * _pool_inv_cnt(g, row), g, False, row) - gp

    col = pl.BlockSpec((L, LANES), lambda g: (0, g))
    wspec = pl.BlockSpec((None, LANES, LANES), lambda g: (g, 0, 0))
    vec = pl.BlockSpec((1, LANES), lambda g: (0, g))
    return pl.pallas_call(
        body, name=name, grid=(4,),
        in_specs=[col, col, wspec, vec],
        out_specs=[col, wspec, vec],
        out_shape=[_sds((L, 512)), _sds((4, LANES, LANES)), _sds((1, 512))],
        compiler_params=_cp("parallel"),
    )(g_m, pooled, pool_w, scale)


SGU_CHUNKS = 4


def _sgu_ln(v, gam, bet):
    gv = _gelu(v)
    mu = jnp.mean(gv, axis=-1, keepdims=True)
    xc = gv - mu
    rs = lax.rsqrt(jnp.mean(xc * xc, axis=-1, keepdims=True) + EPS)
    xh = xc * rs
    return xh, rs, xh * gam + bet


def _tril_ws(w_ref, g):
    r = lax.broadcasted_iota(jnp.int32, (LANES, LANES), 0)
    c = lax.broadcasted_iota(jnp.int32, (LANES, LANES), 1)
    return jnp.where(r >= c, w_ref[g], 0.0).astype(BF16)


def _sgu_fwd(z, ln_g, ln_b, w_s, b_st, name):
    L = z.shape[0]
    rb = min(SGU_CHUNKS * LANES, L)

    def body(u_ref, v_ref, g_ref, b_ref, w_ref, bs_ref, y_ref):
        _, _, vln = _sgu_ln(v_ref[...], g_ref[...], b_ref[...])
        gu = _gelu(u_ref[...])
        vb = vln.astype(BF16)
        for g in range(4):
            ws = _tril_ws(w_ref, g)
            for n in range(rb // LANES):
                rows = slice(n * LANES, (n + 1) * LANES)
                cols = slice(g * LANES, (g + 1) * LANES)
                mixed = _dot(ws, vb[rows, cols]) + bs_ref[:, g:g + 1]
                y_ref[rows, cols] = (gu[rows, cols] * mixed).astype(BF16)

    vm = lambda shape: pl.BlockSpec(shape, lambda i: tuple(0 for _ in shape))
    return pl.pallas_call(
        body, name=name, grid=(L // rb,),
        in_specs=[pl.BlockSpec((rb, 512), lambda i: (i, 1)), pl.BlockSpec((rb, 512), lambda i: (i, 2)),
                  vm((1, 512)), vm((1, 512)), vm((4, LANES, LANES)), vm((LANES, 4))],
        out_specs=pl.BlockSpec((rb, 512), lambda i: (i, 0)),
        out_shape=_sds((L, 512), BF16),
        compiler_params=_cp("parallel"),
    )(z, z, ln_g, ln_b, w_s, b_st)


def _sgu_bwd(g_m, z, ln_g, ln_b, w_s, b_st, name):
    L = z.shape[0]
    rb = min(SGU_CHUNKS * LANES, L)

    def body(gy_ref, u_ref, v_ref, g_ref, b_ref, w_ref, bs_ref, gu_ref, gv_ref, gw_ref, gbs_ref, gg_ref, gb_ref):
        i = pl.program_id(0)

        @pl.when(i == 0)
        def _():
            gw_ref[...] = jnp.zeros_like(gw_ref)
            gbs_ref[...] = jnp.zeros_like(gbs_ref)
            gg_ref[...] = jnp.zeros_like(gg_ref)
            gb_ref[...] = jnp.zeros_like(gb_ref)

        v = v_ref[...]
        u = u_ref[...]
        gy = gy_ref[...]
        xh, rs, vln = _sgu_ln(v, g_ref[...], b_ref[...])
        gel_u = _gelu(u)
        gmix = gy * gel_u
        vb = vln.astype(BF16)
        gmb = gmix.astype(BF16)
        r = lax.broadcasted_iota(jnp.int32, (LANES, LANES), 0)
        c = lax.broadcasted_iota(jnp.int32, (LANES, LANES), 1)
        gvln_cols = []
        for g in range(4):
            ws = _tril_ws(w_ref, g)
            cols = slice(g * LANES, (g + 1) * LANES)
            gw = jnp.zeros((LANES, LANES), F32)
            gbs = jnp.zeros((LANES, 1), F32)
            parts = []
            for n in range(rb // LANES):
                rows = slice(n * LANES, (n + 1) * LANES)
                mixed = _dot(ws, vb[rows, cols]) + bs_ref[:, g:g + 1]
                gu_ref[rows, cols] = gy[rows, cols] * mixed * _gelu_grad(u[rows, cols])
                parts.append(_dot(ws, gmb[rows, cols], TN))
                gw = gw + _dot(gmb[rows, cols], vb[rows, cols], NT)
                gbs = gbs + jnp.sum(gmix[rows, cols], axis=1, keepdims=True)
            gvln_cols.append(jnp.concatenate(parts, axis=0))
            gw_ref[g] += jnp.where(r >= c, gw, 0.0)
            gbs_ref[:, g:g + 1] += gbs
        gvln = jnp.concatenate(gvln_cols, axis=1)
        gg_ref[...] += jnp.sum(gvln * xh, axis=0, keepdims=True)
        gb_ref[...] += jnp.sum(gvln, axis=0, keepdims=True)
        gxh = gvln * g_ref[...]
        ggv = rs * (gxh - jnp.mean(gxh, axis=-1, keepdims=True) - xh * jnp.mean(gxh * xh, axis=-1, keepdims=True))
        gv_ref[...] = ggv * _gelu_grad(v)

    vm = lambda shape: pl.BlockSpec(shape, lambda i: tuple(0 for _ in shape))
    blk = pl.BlockSpec((rb, 512), lambda i: (i, 0))
    return pl.pallas_call(
        body, name=name, grid=(L // rb,),
        in_specs=[pl.BlockSpec((rb, 512), lambda i: (i, 1)), pl.BlockSpec((rb, 512), lambda i: (i, 1)),
                  pl.BlockSpec((rb, 512), lambda i: (i, 2)),
                  vm((1, 512)), vm((1, 512)), vm((4, LANES, LANES)), vm((LANES, 4))],
        out_specs=[blk, blk, vm((4, LANES, LANES)), vm((LANES, 4)), vm((1, 512)), vm((1, 512))],
        out_shape=[_sds((L, 512)), _sds((L, 512)), _sds((4, LANES, LANES)), _sds((LANES, 4)),
                   _sds((1, 512)), _sds((1, 512))],
        compiler_params=_cp("arbitrary"),
    )(g_m, z, z, ln_g, ln_b, w_s, b_st)


def _adamw_math(w, g, m, v):
    nm = ADAM_B1 * m + (1.0 - ADAM_B1) * g
    nv = ADAM_B2 * v + (1.0 - ADAM_B2) * (g * g)
    m_hat = nm / (1.0 - ADAM_B1 ** ADAM_STEP)
    v_hat = nv / (1.0 - ADAM_B2 ** ADAM_STEP)
    delta = -ADAM_LR * (m_hat / (jnp.sqrt(v_hat) + ADAM_EPS) + ADAM_WD * w)
    return delta, nm, nv


def _sum_adamw(parts, w, m, v, name, layer=0, prev=None):
    n_layers, R, C = w.shape
    rb = 128 if R % 128 == 0 else R

    def body(p_ref, w_ref, m_ref, v_ref, *rest):
        g_ref, d_ref, nm_ref, nv_ref = rest[-4:]
        g = p_ref[0]
        for s in range(1, N_DEV):
            g = g + p_ref[s]
        d, nm, nv = _adamw_math(w_ref[...], g, m_ref[...], v_ref[...])
        g_ref[...] = g
        d_ref[...] = d
        nm_ref[...] = nm
        nv_ref[...] = nv

    blk = pl.BlockSpec((None, rb, C), lambda i: (layer, i, 0))
    prev = [] if prev is None else list(prev)
    return pl.pallas_call(
        body, name=name, grid=(R // rb,),
        in_specs=[pl.BlockSpec((N_DEV, rb, C), lambda i: (0, i, 0)), blk, blk, blk] + [ANY] * len(prev),
        out_specs=[blk] * 4, out_shape=[_sds((n_layers, R, C))] * 4,
        input_output_aliases={4 + k: k for k in range(len(prev))},
        compiler_params=_cp("parallel"),
    )(parts, w, m, v, *prev)


def _sum_pieces(parts, name):
    _, R, C = parts.shape

    def body(p_ref, g_ref):
        g = p_ref[0]
        for s in range(1, N_DEV):
            g = g + p_ref[s]
        g_ref[...] = g

    vm = pl.BlockSpec(memory_space=pltpu.VMEM)
    return pl.pallas_call(body, name=name, in_specs=[vm], out_specs=vm, out_shape=_sds((R, C)))(parts)


def _adamw(w, g, m, v, name):
    vm = pl.BlockSpec(memory_space=pltpu.VMEM)

    def body(w_ref, g_ref, m_ref, v_ref, d_ref, nm_ref, nv_ref):
        d, nm, nv = _adamw_math(w_ref[...], g_ref[...], m_ref[...], v_ref[...])
        d_ref[...] = d
        nm_ref[...] = nm
        nv_ref[...] = nv

    return pl.pallas_call(body, name=name, in_specs=[vm] * 4, out_specs=[vm] * 3,
                          out_shape=[_sds(w.shape)] * 3)(w, g, m, v)


def _mesh_pos():
    return lax.axis_index("x"), lax.axis_index("y"), lax.axis_index("c")


def _dev_index(p):
    return 4 * p[0] + 2 * p[1] + p[2]


def _all_gather(xs, name):
    n = len(xs)

    def body(*refs):
        x_refs, o_refs = refs[:n], refs[n:2 * n]
        send_sems, recv_sems, local_sems = refs[2 * n:]
        x, y, c = _mesh_pos()
        me, sibling = (x, y, c), (x, y, 1 - c)
        chips = [(1 - x, y), (x, 1 - y), (1 - x, 1 - y)]

        def copy(i, k, block, to, src=None):
            dst = o_refs[i].at[_dev_index(block)]
            return pltpu.make_async_remote_copy(
                src_ref=dst if src is None else src, dst_ref=dst,
                send_sem=send_sems.at[i, k], recv_sem=recv_sems.at[i, k], device_id=to, device_id_type=MESH)

        mine = [pltpu.make_async_copy(x_refs[i], o_refs[i].at[_dev_index(me)], local_sems.at[i]) for i in range(n)]
        for cp in mine:
            cp.start()
        first = []
        for i in range(n):
            first.append(copy(i, 0, me, sibling, src=x_refs[i]))
            first += [copy(i, 1 + j, me, (*chip, c), src=x_refs[i]) for j, chip in enumerate(chips)]
        for cp in first:
            cp.start()
        passed = []
        for j, chip in enumerate(chips):
            for i in range(n):
                copy(i, 1 + j, (*chip, c), me).wait_recv()
                fwd = copy(i, 4 + j, (*chip, c), sibling)
                fwd.start()
                passed.append(fwd)
        for i in range(n):
            copy(i, 0, sibling, me).wait_recv()
            for j, chip in enumerate(chips):
                copy(i, 4 + j, (*chip, 1 - c), me).wait_recv()
        for cp in first + passed:
            cp.wait_send()
        for cp in mine:
            cp.wait()

    outs = pl.pallas_call(
        body, name=name,
        in_specs=[ANY] * n, out_specs=[ANY] * n,
        out_shape=[_sds((N_DEV,) + x.shape, x.dtype) for x in xs],
        scratch_shapes=[pltpu.SemaphoreType.DMA((n, 7)), pltpu.SemaphoreType.DMA((n, 7)),
                        pltpu.SemaphoreType.DMA((n,))],
    )(*xs)
    return list(outs)


def _exchange(gs, name):
    n = len(gs)

    def body(*refs):
        g_refs, o_refs = refs[:n], refs[n:2 * n]
        send_sems, recv_sems, local_sems = refs[2 * n:]
        x, y, c = _mesh_pos()
        me = (x, y, c)
        mi = _dev_index(me)
        peers = [(x ^ dx, y ^ dy, c ^ dc) for dx in range(2) for dy in range(2) for dc in range(2)][1:]

        def copy(i, k, peer):
            return pltpu.make_async_remote_copy(
                src_ref=g_refs[i].at[_dev_index(peer)], dst_ref=o_refs[i].at[mi],
                send_sem=send_sems.at[i, k], recv_sem=recv_sems.at[i, k], device_id=peer, device_id_type=MESH)

        mine = [pltpu.make_async_copy(g_refs[i].at[mi], o_refs[i].at[mi], local_sems.at[i]) for i in range(n)]
        for cp in mine:
            cp.start()
        sends = [copy(i, k, peer) for i in range(n) for k, peer in enumerate(peers)]
        for cp in sends:
            cp.start()
        for i in range(n):
            for k, peer in enumerate(peers):
                pltpu.make_async_remote_copy(
                    src_ref=g_refs[i].at[mi], dst_ref=o_refs[i].at[_dev_index(peer)],
                    send_sem=send_sems.at[i, k], recv_sem=recv_sems.at[i, k], device_id=peer,
                    device_id_type=MESH).wait_recv()
        for cp in sends:
            cp.wait_send()
        for cp in mine:
            cp.wait()

    outs = pl.pallas_call(
        body, name=name,
        in_specs=[ANY] * n, out_specs=[ANY] * n,
        out_shape=[_sds(g.shape, g.dtype) for g in gs],
        scratch_shapes=[pltpu.SemaphoreType.DMA((n, 7)), pltpu.SemaphoreType.DMA((n, 7)),
                        pltpu.SemaphoreType.DMA((n,))],
    )(*gs)
    return list(outs)


HBM = pl.BlockSpec(memory_space=pltpu.HBM)
SEM = pl.BlockSpec(memory_space=pltpu.SEMAPHORE)
EFFECT = pltpu.SideEffectType.DATAFLOW_SIDE_EFFECTING


def _peer_list():
    x, y, c = _mesh_pos()
    peers = [(x ^ dx, y ^ dy, c ^ dc) for dx in range(2) for dy in range(2) for dc in range(2)][1:]
    return (x, y, c), peers


def _split_copy(src_ref, land_ref, send_sems, recv_sems, i, k, peer, slot, exchange):
    return pltpu.make_async_remote_copy(
        src_ref=src_ref.at[_dev_index(peer)] if exchange else src_ref, dst_ref=land_ref.at[slot],
        send_sem=send_sems.at[7 * i + k], recv_sem=recv_sems.at[7 * i + k], device_id=peer, device_id_type=MESH)


def _comm_start(groups, name, exchange, dep=None):
    sizes = [len(g) for g in groups]
    n = sum(sizes)
    srcs = [a for g in groups for a in g]
    my_index = _dev_index(_mesh_pos())
    lands = []
    for a in srcs:
        if exchange:
            own = lax.dynamic_slice(a, (my_index, 0, 0), (1,) + a.shape[1:])
            shape = a.shape
        else:
            own = a[None]
            shape = (N_DEV,) + a.shape
        lands.append(lax.dynamic_update_slice(lax.empty(shape, a.dtype), own, (my_index, 0, 0)))

    n_dep = 0 if dep is None else 1

    def body(*refs):
        src_refs, land_refs = refs[:n], refs[n:2 * n]
        sem_refs = refs[2 * n + n_dep:2 * n + n_dep + 2 * len(sizes)]
        token_ref = refs[-1]
        me, peers = _peer_list()
        mi = _dev_index(me)
        i = 0
        for gi, sz in enumerate(sizes):
            for j in range(sz):
                for k, peer in enumerate(peers):
                    _split_copy(src_refs[i], land_refs[i], sem_refs[2 * gi], sem_refs[2 * gi + 1], j, k, peer, mi,
                                exchange).start()
                i += 1
        token_ref[...] = jnp.zeros_like(token_ref)

    sem_shapes = []
    for sz in sizes:
        sem_shapes += [pltpu.SemaphoreType.DMA((7 * sz,)), pltpu.SemaphoreType.DMA((7 * sz,))]
    thru = [pltpu.HBM(a.shape, a.dtype) for a in srcs + lands]
    n_sem = len(sem_shapes)
    outs = pl.pallas_call(
        body, name=name,
        out_shape=tuple(sem_shapes + thru + [_sds((8, LANES))]),
        in_specs=[HBM] * (2 * n) + [ANY] * n_dep,
        out_specs=tuple([SEM] * n_sem + [HBM] * (2 * n) + [pl.BlockSpec(memory_space=pltpu.VMEM)]),
        input_output_aliases={i: n_sem + i for i in range(2 * n)},
        compiler_params=pltpu.CompilerParams(has_side_effects=EFFECT),
    )(*[pltpu.with_memory_space_constraint(a, pltpu.HBM) for a in srcs + lands], *([] if dep is None else [dep]))
    sems, thru_src, thru_land, token = outs[:n_sem], outs[n_sem:n_sem + n], outs[n_sem + n:n_sem + 2 * n], outs[-1]
    result, off = [], 0
    for gi, sz in enumerate(sizes):
        result.append((sems[2 * gi], sems[2 * gi + 1], list(thru_src[off:off + sz]), list(thru_land[off:off + sz])))
        off += sz
    return result, token


def _comm_wait(group, after, name, exchange):
    send_sems, recv_sems, srcs, lands = group
    n = len(srcs)
    after = list(after) if isinstance(after, (list, tuple)) else [after]

    def body(*refs):
        src_refs, land_refs = refs[:n], refs[n:2 * n]
        ssem, rsem = refs[2 * n], refs[2 * n + 1]
        me, peers = _peer_list()
        for i in range(n):
            for k, peer in enumerate(peers):
                cp = _split_copy(src_refs[i], land_refs[i], ssem, rsem, i, k, peer, _dev_index(peer), exchange)
                cp.wait_send()
                cp.wait_recv()

    outs = pl.pallas_call(
        body, name=name,
        out_shape=tuple(pltpu.HBM(a.shape, a.dtype) for a in srcs + lands),
        in_specs=[HBM] * (2 * n) + [SEM, SEM] + [ANY] * len(after),
        out_specs=tuple([HBM] * (2 * n)),
        input_output_aliases={i: i for i in range(2 * n)},
        compiler_params=pltpu.CompilerParams(has_side_effects=EFFECT),
    )(*srcs, *lands, send_sems, recv_sems, *after)
    return list(outs[n:])


def _tie(a, token):
    return a + token[0, 0].astype(a.dtype)


def _pack(arrs, rows):
    flat = jnp.concatenate([a.reshape(-1).astype(F32) for a in arrs])
    return jnp.pad(flat, (0, rows * LANES - flat.shape[0])).reshape(rows, LANES)


def _unpack(packed, shapes):
    flat = packed.reshape(-1)
    out, off = [], 0
    for s in shapes:
        n = math.prod(s)
        out.append(flat[off:off + n].reshape(s))
        off += n
    return out


def _packed_rows(shapes):
    n = sum(math.prod(s) for s in shapes)
    unit = N_DEV * 8 * LANES
    return -(-n // unit) * unit // LANES


def kernel(x, mix_pre_g, mix_post_g, mlp_pre_g, mlp_post_g, w_in_even, s5_lam_re, s5_lam_im, s5_log_dt, s5_b_re, s5_b_im, s5_c_re, s5_c_im, s5_d, s5_w_glu, fox_b_f, w_out_even, w_in_odd, pool_w, pool_scale, sgu_ln_g, sgu_ln_b, sgu_w_s, sgu_b_s, w_out_odd, mlp_w1, mlp_w2, loss_target, m_mix_pre_g, m_mix_post_g, m_mlp_pre_g, m_mlp_post_g, m_w_in_even, m_s5_lam_re, m_s5_lam_im, m_s5_log_dt, m_s5_b_re, m_s5_b_im, m_s5_c_re, m_s5_c_im, m_s5_d, m_s5_w_glu, m_fox_b_f, m_w_out_even, m_w_in_odd, m_pool_w, m_pool_scale, m_sgu_ln_g, m_sgu_ln_b, m_sgu_w_s, m_sgu_b_s, m_w_out_odd, m_mlp_w1, m_mlp_w2, v_mix_pre_g, v_mix_post_g, v_mlp_pre_g, v_mlp_post_g, v_w_in_even, v_s5_lam_re, v_s5_lam_im, v_s5_log_dt, v_s5_b_re, v_s5_b_im, v_s5_c_re, v_s5_c_im, v_s5_d, v_s5_w_glu, v_fox_b_f, v_w_out_even, v_w_in_odd, v_pool_w, v_pool_scale, v_sgu_ln_g, v_sgu_ln_b, v_sgu_w_s, v_sgu_b_s, v_w_out_odd, v_mlp_w1, v_mlp_w2):
    weights = dict(mix_pre_g=mix_pre_g, mix_post_g=mix_post_g, mlp_pre_g=mlp_pre_g, mlp_post_g=mlp_post_g, w_in_even=w_in_even, s5_lam_re=s5_lam_re, s5_lam_im=s5_lam_im, s5_log_dt=s5_log_dt, s5_b_re=s5_b_re, s5_b_im=s5_b_im, s5_c_re=s5_c_re, s5_c_im=s5_c_im, s5_d=s5_d, s5_w_glu=s5_w_glu, fox_b_f=fox_b_f, w_out_even=w_out_even, w_in_odd=w_in_odd, pool_w=pool_w, pool_scale=pool_scale, sgu_ln_g=sgu_ln_g, sgu_ln_b=sgu_ln_b, sgu_w_s=sgu_w_s, sgu_b_s=sgu_b_s, w_out_odd=w_out_odd, mlp_w1=mlp_w1, mlp_w2=mlp_w2)
    mom_m = dict(mix_pre_g=m_mix_pre_g, mix_post_g=m_mix_post_g, mlp_pre_g=m_mlp_pre_g, mlp_post_g=m_mlp_post_g, w_in_even=m_w_in_even, s5_lam_re=m_s5_lam_re, s5_lam_im=m_s5_lam_im, s5_log_dt=m_s5_log_dt, s5_b_re=m_s5_b_re, s5_b_im=m_s5_b_im, s5_c_re=m_s5_c_re, s5_c_im=m_s5_c_im, s5_d=m_s5_d, s5_w_glu=m_s5_w_glu, fox_b_f=m_fox_b_f, w_out_even=m_w_out_even, w_in_odd=m_w_in_odd, pool_w=m_pool_w, pool_scale=m_pool_scale, sgu_ln_g=m_sgu_ln_g, sgu_ln_b=m_sgu_ln_b, sgu_w_s=m_sgu_w_s, sgu_b_s=m_sgu_b_s, w_out_odd=m_w_out_odd, mlp_w1=m_mlp_w1, mlp_w2=m_mlp_w2)
    mom_v = dict(mix_pre_g=v_mix_pre_g, mix_post_g=v_mix_post_g, mlp_pre_g=v_mlp_pre_g, mlp_post_g=v_mlp_post_g, w_in_even=v_w_in_even, s5_lam_re=v_s5_lam_re, s5_lam_im=v_s5_lam_im, s5_log_dt=v_s5_log_dt, s5_b_re=v_s5_b_re, s5_b_im=v_s5_b_im, s5_c_re=v_s5_c_re, s5_c_im=v_s5_c_im, s5_d=v_s5_d, s5_w_glu=v_s5_w_glu, fox_b_f=v_fox_b_f, w_out_even=v_w_out_even, w_in_odd=v_w_in_odd, pool_w=v_pool_w, pool_scale=v_pool_scale, sgu_ln_g=v_sgu_ln_g, sgu_ln_b=v_sgu_ln_b, sgu_w_s=v_sgu_w_s, sgu_b_s=v_sgu_b_s, w_out_odd=v_w_out_odd, mlp_w1=v_mlp_w1, mlp_w2=v_mlp_w2)
    names = list(weights)
    L = x.shape[1]
    x0 = x[0]
    target = loss_target[0]
    my_index = 4 * lax.axis_index("x") + 2 * lax.axis_index("y") + lax.axis_index("c")

    small_vec = jnp.zeros((8, LANES), F32)
    small_vec = small_vec.at[0, :64].set(pool_scale[0]).at[1, :64].set(sgu_ln_g[0]).at[2, :64].set(sgu_ln_b[0])
    ag_groups, ag_token = _comm_start(
        [[jnp.transpose(w_in_even[0]).astype(BF16), small_vec],
         [s5_w_glu[0].astype(BF16), w_out_even[0].astype(BF16)],
         [mlp_w1[0].astype(BF16), mlp_w2[0].astype(BF16)],
         [jnp.transpose(w_in_odd[0]).astype(BF16), w_out_odd[0].astype(BF16), mlp_w1[1].astype(BF16), mlp_w2[1].astype(BF16)]],
        "ag_start", exchange=False)

    lam_r = jnp.concatenate([s5_lam_re.reshape(1, S5_NS), s5_lam_im.reshape(1, S5_NS)], axis=0)
    ldt_r = jnp.repeat(s5_log_dt.reshape(32), 64).reshape(1, S5_NS)
    lam_c = jnp.transpose(lam_r)
    ldt_c = jnp.transpose(ldt_r)
    b_t = jnp.stack([jnp.tile(s5_b_re.reshape(S5_NS, 16), (1, 8)), jnp.tile(s5_b_im.reshape(S5_NS, 16), (1, 8))])
    c_t = jnp.stack([jnp.tile(s5_c_re.reshape(S5_W, 64), (1, 8)), jnp.tile(s5_c_im.reshape(S5_W, 64), (1, 8))])
    bf_pad = jnp.pad(fox_b_f, ((0, 0), (0, LANES - 8)))
    b_st = jnp.transpose(sgu_b_s[0])

    h0, rx0 = _rms_fwd(x0, _tie(mix_pre_g[0:1], ag_token), "rms0")
    tabs, bset, cset = _s5_prep(lam_r, ldt_r, lam_c, ldt_c, b_t, c_t, "s5_prep")
    ag0 = _comm_wait(ag_groups[0], tabs, "ag_wait0", exchange=False)
    winT_e = jnp.pad(ag0[0].reshape(EVEN_IN, D_MODEL), ((0, EVEN_PAD - EVEN_IN), (0, 0)))
    pool_scale_f = ag0[1][:, 0, :64].reshape(1, 512)
    ln_g_f = ag0[1][:, 1, :64].reshape(1, 512)
    ln_b_f = ag0[1][:, 2, :64].reshape(1, 512)
    z0 = _mm(h0, winT_e, name="win_even", tb=True, bm=512, bn=EVEN_PAD)
    xs = _s5_scan_fwd(z0, bset, tabs, "s5_scan")
    ag1 = _comm_wait(ag_groups[1], xs, "ag_wait1", exchange=False)
    wglu = ag1[0].reshape(S5_W, S5_W)
    wout_e = ag1[1].reshape(D_MODEL, D_MODEL)
    ylin, ya = _s5_out_fwd(xs, cset, z0, s5_d, wglu, "s5_out")
    fcum, fq = _fox_f_fwd(z0, bf_pad, "fox_f")
    frow = jnp.transpose(fcum[:, :8]).reshape(4, 2, L)
    o_att, lse = _fox_fwd(z0, fq, frow, "fox_fwd")
    mix0 = [ya, o_att]
    y0 = _mm(mix0, wout_e, name="wout_even")
    x1, ry0, h1, rx1 = _post_pre_fwd(x0, y0, mix_post_g[0:1], mlp_pre_g[0:1], "post0")
    ag2 = _comm_wait(ag_groups[2], rx1, "ag_wait2", exchange=False)
    w1 = [ag2[0], None]
    w2 = [ag2[1].reshape(4 * D_MODEL, D_MODEL), None]
    p0, a0 = _mm(h1, w1[0], name="mlp0_w1", b3=True, out_dtypes=(BF16, BF16), epi=_epi_relu2)
    o0 = _mm(a0, w2[0], name="mlp0_w2")
    x2, ro0, h2, rx2 = _post_pre_fwd(x1, o0, mlp_post_g[0:1], mix_pre_g[1:2], "post1")
    ag3 = _comm_wait(ag_groups[3], rx2, "ag_wait3", exchange=False)
    winT_o = ag3[0].reshape(ODD_IN, D_MODEL)
    wout_o = ag3[1].reshape(D_MODEL, D_MODEL)
    w1[1] = ag3[2]
    w2[1] = ag3[3].reshape(4 * D_MODEL, D_MODEL)
    z1 = _mm(h2, winT_o, name="win_odd", tb=True, bn=ODD_IN)
    yc, pooled = _pool_fwd(z1, pool_w[0], pool_scale_f, "pool_fwd")
    yd = _sgu_fwd(z1, ln_g_f, ln_b_f, sgu_w_s[0], b_st, "sgu_fwd")
    mix1 = [yc, yd]
    y1 = _mm(mix1, wout_o, name="wout_odd")
    x3, ry1, h3, rx3 = _post_pre_fwd(x2, y1, mix_post_g[1:2], mlp_pre_g[1:2], "post2")
    p1, a1 = _mm(h3, w1[1], name="mlp1_w1", b3=True, out_dtypes=(BF16, BF16), epi=_epi_relu2)
    o1 = _mm(a1, w2[1], name="mlp1_w2")
    gx4, ro1, sq = _post_loss_fwd(x3, o1, mlp_post_g[1:2], target, "post3")

    g_o1, gg_mlp_post1 = _post_bwd(gx4, o1, ro1, mlp_post_g[1:2], "bpost3")
    g_p1 = _mm(g_o1, w2[1], name="b_mlp1_a", tb=True, out_dtypes=(BF16,), epi=_epi_relu2_bwd, extra=(p1,))
    gw2_1 = _mm(a1, g_o1, name="b_mlp1_w2", ta=True)
    g_h3 = _mm(g_p1, w1[1], name="b_mlp1_h", tb=True, b3=True)
    gw1_1 = _mm(h3, g_p1, name="b_mlp1_w1", ta=True, out3=True)
    (ex1,), tok1 = _comm_start([[gw1_1, gw2_1.reshape(N_DEV, 512, D_MODEL)]], "ex_start1", exchange=True)
    g_x3, gg_mlp_pre1, g_y1, gg_mix_post1 = _pre_post_bwd(g_h3, x3, rx3, _tie(mlp_pre_g[1:2], tok1), gx4, y1, ry1, mix_post_g[1:2], "bpre3")
    g_mix1 = _mm(g_y1, wout_o, name="b_wout_odd_m", tb=True)
    gwout_o = _mm(mix1, g_y1, name="b_wout_odd_w", ta=True)
    g_xc, g_pool_w, g_pool_scale = _pool_bwd(g_mix1, pooled, pool_w[0], pool_scale_f, "pool_bwd")
    g_u1, g_v1, g_ws, g_bst, g_ln_g, g_ln_b = _sgu_bwd(g_mix1, z1, ln_g_f, ln_b_f, sgu_w_s[0], b_st, "sgu_bwd")
    g_z1 = [g_xc, g_u1, g_v1]
    g_h2 = _mm(g_z1, winT_o, name="b_win_odd_h")
    gwinT_o = _mm(g_z1, h2, name="b_win_odd_w", ta=True)
    (ex2,), tok2 = _comm_start([[gwout_o.reshape(N_DEV, 128, D_MODEL), gwinT_o.reshape(N_DEV, ODD_IN // N_DEV, D_MODEL)]], "ex_start2", exchange=True)
    g_x2, gg_mix_pre1, g_o0, gg_mlp_post0 = _pre_post_bwd(g_h2, x2, rx2, _tie(mix_pre_g[1:2], tok2), g_x3, o0, ro0, mlp_post_g[0:1], "bpre2")
    g_p0 = _mm(g_o0, w2[0], name="b_mlp0_a", tb=True, out_dtypes=(BF16,), epi=_epi_relu2_bwd, extra=(p0,))
    gw2_0 = _mm(a0, g_o0, name="b_mlp0_w2", ta=True)
    g_h1 = _mm(g_p0, w1[0], name="b_mlp0_h", tb=True, b3=True)
    gw1_0 = _mm(h1, g_p0, name="b_mlp0_w1", ta=True, out3=True)
    (ex3,), tok3 = _comm_start([[gw1_0, gw2_0.reshape(N_DEV, 512, D_MODEL)]], "ex_start3", exchange=True)
    g_x1, gg_mlp_pre0, g_y0, gg_mix_post0 = _pre_post_bwd(g_h1, x1, rx1, _tie(mlp_pre_g[0:1], tok3), g_x2, y0, ry0, mix_post_g[0:1], "bpre1")
    g_mix0 = _mm(g_y0, wout_e, name="b_wout_even_m", tb=True)
    gwout_e = _mm(mix0, g_y0, name="b_wout_even_w", ta=True)
    gyl, gud, g_wglu, g_d = _s5_glu_bwd(g_mix0, ylin, z0, s5_d, wglu, "s5_glu_bwd")
    (ex4,), tok4 = _comm_start([[gwout_e.reshape(N_DEV, 128, D_MODEL), g_wglu.reshape(N_DEV, 64, S5_W)]], "ex_start4", exchange=True)
    g_u0, ga, gb_raw, gc_raw = _s5_scan_bwd(gyl, _tie(cset, tok4), xs, z0, bset, gud, tabs, "s5_scan_bwd")
    g_lam, g_ldt, g_b, g_c = _s5_param_bwd(lam_c, ldt_c, b_t, gb_raw, jnp.transpose(ga), gc_raw, "s5_param_bwd")
    dq, dk, dv, dfq, dfrow = _fox_bwd(z0, fq, frow, o_att, lse, g_mix0, "fox_bwd")
    dFk = jnp.pad(jnp.transpose(dfrow.reshape(8, L)), ((0, 0), (0, LANES - 8)))
    dfl, db_f = _fox_f_bwd(dFk, dfq, z0, bf_pad, "fox_f_bwd")
    g_z0 = [g_u0, dq, dk, dv, dfl]
    g_h0 = _mm(g_z0, winT_e, name="b_win_even_h")
    grad_x, gg_mix_pre0 = _pre_bwd(g_h0, x0, rx0, mix_pre_g[0:1], g_x1, "bpre0")
    gwinT_e = _mm(g_z0, h0, name="b_win_even_w", ta=True, bk=512)

    small_grads = dict(
        mix_pre_g=jnp.concatenate([gg_mix_pre0, gg_mix_pre1]), mix_post_g=jnp.concatenate([gg_mix_post0, gg_mix_post1]),
        mlp_pre_g=jnp.concatenate([gg_mlp_pre0, gg_mlp_pre1]), mlp_post_g=jnp.concatenate([gg_mlp_post0, gg_mlp_post1]),
        s5_lam_re=g_lam[:, 0], s5_lam_im=g_lam[:, 1], s5_log_dt=g_ldt,
        s5_b_re=g_b[0, :, :16], s5_b_im=g_b[1, :, :16], s5_c_re=g_c[0, :, :64], s5_c_im=g_c[1, :, :64],
        s5_d=g_d, fox_b_f=db_f[:, :8], pool_w=g_pool_w, sgu_w_s=g_ws, sgu_b_s=jnp.transpose(g_bst),
        pool_scale=g_pool_scale, sgu_ln_g=g_ln_g, sgu_ln_b=g_ln_b)
    small_names = list(small_grads)
    full_shapes = [(512,) if nm in ("pool_scale", "sgu_ln_g", "sgu_ln_b") else weights[nm].shape for nm in small_names]
    full_shapes.append((1, 1))
    rows = _packed_rows(full_shapes)
    packed = _pack([small_grads[nm] for nm in small_names] + [sq], rows).reshape(N_DEV, rows // N_DEV, LANES)
    (recv_small,) = _exchange([packed], "exchange_small")
    piece = _sum_pieces(recv_small, "sum_small")
    (small_all,) = _all_gather([piece], "ag_small")
    small_full = _unpack(small_all.reshape(rows, LANES), full_shapes)
    loss = 0.5 * small_full.pop()[0, 0] / D_MODEL

    gwinT_e_pieces = gwinT_e[:EVEN_IN].reshape(N_DEV, EVEN_IN // N_DEV, D_MODEL)
    (ex5,), tok5 = _comm_start([[gwinT_e_pieces]], "ex_start5", exchange=True, dep=small_all)
    r_w1_1, r_w2_1 = _comm_wait(ex1, tok5, "ex_wait1", exchange=True)
    r_wout_o, r_win_o = _comm_wait(ex2, tok5, "ex_wait2", exchange=True)
    r_w1_0, r_w2_0 = _comm_wait(ex3, tok5, "ex_wait3", exchange=True)
    r_wout_e, r_wglu = _comm_wait(ex4, tok5, "ex_wait4", exchange=True)
    small_g = {}
    for nm, g in zip(small_names, small_full):
        if nm in ("pool_scale", "sgu_ln_g", "sgu_ln_b"):
            g = lax.dynamic_slice(g, (my_index * 64,), (64,)).reshape(1, 64)
        small_g[nm] = g
    own_shapes = [weights[nm].shape for nm in small_names]
    rows2 = _packed_rows(own_shapes)
    pw = _pack([weights[nm] for nm in small_names], rows2)
    pg = _pack([small_g[nm] for nm in small_names], rows2)
    pm = _pack([mom_m[nm] for nm in small_names], rows2)
    pv = _pack([mom_v[nm] for nm in small_names], rows2)
    pd, pnm, pnv = _adamw(pw, pg, pm, pv, "adamw_small")
    res = {}
    for nm, d_, m_, v_ in zip(small_names, _unpack(pd, own_shapes), _unpack(pnm, own_shapes), _unpack(pnv, own_shapes)):
        res[nm] = (small_g[nm], d_, m_, v_)

    for nm, parts in (("mlp_w1", (r_w1_0, r_w1_1)), ("mlp_w2", (r_w2_0, r_w2_1))):
        first = _sum_adamw(parts[0], weights[nm], mom_m[nm], mom_v[nm], "adamw_%s_0" % nm, layer=0)
        res[nm] = tuple(_sum_adamw(parts[1], weights[nm], mom_m[nm], mom_v[nm], "adamw_%s_1" % nm, layer=1, prev=first))
    big_parts = dict(s5_w_glu=r_wglu, w_out_even=r_wout_e, w_out_odd=r_wout_o)
    for nm, parts in big_parts.items():
        res[nm] = tuple(_sum_adamw(parts, weights[nm], mom_m[nm], mom_v[nm], "adamw_" + nm))
    done = [res[nm][1] for nm in ("mlp_w1", "mlp_w2", "s5_w_glu", "w_out_even", "w_out_odd")]
    for nm, parts in (("w_in_odd", r_win_o), ("w_in_even", None)):
        if parts is None:
            (parts,) = _comm_wait(ex5, done, "ex_wait5", exchange=True)
        outs = _sum_adamw(parts, jnp.transpose(weights[nm], (0, 2, 1)), jnp.transpose(mom_m[nm], (0, 2, 1)),
                          jnp.transpose(mom_v[nm], (0, 2, 1)), "adamw_" + nm)
        res[nm] = tuple(jnp.transpose(o, (0, 2, 1)) for o in outs)
        done.append(res[nm][1])

    grads = [res[nm][0].reshape(weights[nm].shape) for nm in names]
    deltas = [res[nm][1].reshape(weights[nm].shape) for nm in names]
    new_m = [res[nm][2].reshape(weights[nm].shape) for nm in names]
    new_v = [res[nm][3].reshape(weights[nm].shape) for nm in names]
    return (loss, grad_x[None], *grads, *deltas, *new_m, *new_v)
```

```python
import functools
import math

import jax
import jax.numpy as jnp
from jax import lax
from jax.experimental import pallas as pl
from jax.experimental.pallas import tpu as pltpu

F32 = jnp.float32
BF16 = jnp.bfloat16
MESH = pl.DeviceIdType.MESH
ANY = pl.BlockSpec(memory_space=pl.ANY)

N_DEV = 8
D_MODEL = 1024
EPS = 1e-6
NORM_ROWS = 512
S5_W = 512
S5_NS = 2048
SCAN_GROUPS = 4
SCAN_CHUNK = 1024
FOX_W = 512
EVEN_IN = 2056
EVEN_PAD = 2176
ODD_IN = 1536
LANES = 128
PIECE = 4 * D_MODEL // N_DEV
VMEM_LIMIT = 56 * 1024 * 1024

ADAM_LR = 0.001
ADAM_B1 = 0.9
ADAM_B2 = 0.999
ADAM_EPS = 1e-08
ADAM_WD = 0.01
ADAM_STEP = 10

NT = (((1,), (1,)), ((), ()))
TN = (((0,), (0,)), ((), ()))
NN = (((1,), (0,)), ((), ()))


def _cp(*sem):
    return pltpu.CompilerParams(dimension_semantics=sem, vmem_limit_bytes=VMEM_LIMIT)


def _sds(shape, dtype=F32):
    return jax.ShapeDtypeStruct(tuple(shape), dtype)


def _gelu(x):
    t = jnp.tanh(0.7978845608028654 * (x + 0.044715 * x * x * x))
    return 0.5 * x * (1.0 + t)


def _gelu_grad(x):
    t = jnp.tanh(0.7978845608028654 * (x + 0.044715 * x * x * x))
    du = 0.7978845608028654 * (1.0 + 3.0 * 0.044715 * x * x)
    return 0.5 * (1.0 + t) + 0.5 * x * (1.0 - t * t) * du


def _sigmoid(x):
    return 1.0 / (1.0 + jnp.exp(-x))


def _dot(a, b, dn=NN):
    return lax.dot_general(a, b, dn, preferred_element_type=F32)


def _mm(a, b, *, name, ta=False, tb=False, b3=False, out3=False, out_dtypes=(F32,), epi=None, extra=(),
        bm=1024, bn=1024, bk=1024):
    a_list = list(a) if isinstance(a, (list, tuple)) else [a]
    widths = [p.shape[1] for p in a_list]
    offs = [sum(widths[:i]) for i in range(len(widths))]
    na = len(a_list)
    M = sum(widths) if ta else a_list[0].shape[0]
    K = a_list[0].shape[0] if ta else sum(widths)
    if na > 1:
        assert not b3 and not tb
        bm, bk = (M, bk) if ta else (bm, K)
    pw = b.shape[2] if b3 else PIECE
    if b3:
        N = b.shape[1] if tb else b.shape[0] * pw
        assert (b.shape[0] * pw if tb else b.shape[1]) == K
    else:
        N = b.shape[0] if tb else b.shape[1]
    bm, bn, bk = min(bm, M), min(bn, N), min(bk, K)
    assert M % bm == 0 and N % bn == 0 and K % bk == 0, (name, M, N, K, bm, bn, bk)
    assert not (b3 or out3) or ((bk if tb else bn) % pw == 0 and bn % PIECE == 0)
    nk = K // bk
    n_extra = len(extra)
    n_out = len(out_dtypes)
    dn = (((0 if ta else 1,), (1 if tb else 0,)), ((), ()))

    use_acc = nk > 1

    def body(*refs):
        a_refs, b_ref = refs[:na], refs[na]
        a_ref = a_refs[0]
        e_refs = refs[na + 1:na + 1 + n_extra]
        o_refs = refs[na + 1 + n_extra:na + 1 + n_extra + n_out]
        acc_ref = refs[-1] if use_acc else o_refs[0]
        k = pl.program_id(2)

        def dot(a_v, b_v):
            return lax.dot_general(a_v.astype(BF16), b_v.astype(BF16), dn, preferred_element_type=F32)

        if na > 1 and ta:
            prod = jnp.concatenate([dot(r[...], b_ref[...]) for r in a_refs], axis=0)
        elif na > 1:
            prod = None
            for r, off, w in zip(a_refs, offs, widths):
                term = dot(r[...], b_ref[pl.ds(off, w), :])
                prod = term if prod is None else prod + term
        elif not b3:
            prod = dot(a_ref[...], b_ref[...])
        elif tb:
            prod = None
            for t in range(bk // pw):
                a_t = a_ref[pl.ds(t * pw, pw), :] if ta else a_ref[:, pl.ds(t * pw, pw)]
                term = dot(a_t, b_ref[t])
                prod = term if prod is None else prod + term
        else:
            a_v = a_ref[...].astype(BF16)
            prod = jnp.concatenate([dot(a_v, b_ref[t]) for t in range(bn // pw)], axis=1)

        def finish(acc):
            outs = (acc,) if epi is None else epi(acc, *[e[...] for e in e_refs])
            for o_ref, o in zip(o_refs, outs):
                if out3:
                    for t in range(bn // PIECE):
                        o_ref[t] = o[:, t * PIECE:(t + 1) * PIECE].astype(o_ref.dtype)
                else:
                    o_ref[...] = o.astype(o_ref.dtype)

        if nk == 1:
            finish(prod)
            return

        @pl.when(k == 0)
        def _():
            acc_ref[...] = prod

        @pl.when(k > 0)
        def _():
            acc_ref[...] += prod

        if use_acc:
            @pl.when(k == nk - 1)
            def _():
                finish(acc_ref[...])

    if na > 1:
        a_specs = [pl.BlockSpec((bk, w), lambda i, j, k: (k, 0)) if ta else pl.BlockSpec((bm, w), lambda i, j, k: (i, 0))
                   for w in widths]
    else:
        a_specs = [pl.BlockSpec((bk, bm), lambda i, j, k: (k, i)) if ta else
                   pl.BlockSpec((bm, bk), lambda i, j, k: (i, k))]
    if b3:
        if tb:
            b_spec = pl.BlockSpec((bk // pw, bn, pw), lambda i, j, k: (k, j, 0))
        else:
            b_spec = pl.BlockSpec((bn // pw, bk, pw), lambda i, j, k: (j, k, 0))
    else:
        b_spec = pl.BlockSpec((bn, bk), lambda i, j, k: (j, k)) if tb else pl.BlockSpec((bk, bn), lambda i, j, k: (k, j))
    e_specs = [pl.BlockSpec((bm, bn), lambda i, j, k: (i, j)) for _ in extra]
    if out3:
        o_specs = [pl.BlockSpec((bn // PIECE, bm, PIECE), lambda i, j, k: (j, i, 0)) for _ in out_dtypes]
        o_shapes = [_sds((N // PIECE, M, PIECE), dt) for dt in out_dtypes]
    else:
        o_specs = [pl.BlockSpec((bm, bn), lambda i, j, k: (i, j)) for _ in out_dtypes]
        o_shapes = [_sds((M, N), dt) for dt in out_dtypes]
    outs = pl.pallas_call(
        body, name=name, grid=(M // bm, N // bn, nk),
        in_specs=a_specs + [b_spec] + e_specs, out_specs=o_specs, out_shape=o_shapes,
        scratch_shapes=[pltpu.VMEM((bm, bn), F32)] if use_acc else [],
        compiler_params=_cp("parallel", "parallel", "arbitrary"),
    )(*a_list, b, *extra)
    return outs[0] if n_out == 1 else outs


def _epi_relu2(acc):
    r = jnp.maximum(acc, 0.0)
    return acc, r * r


def _epi_relu2_bwd(acc, p):
    return (acc * (2.0 * jnp.maximum(p.astype(F32), 0.0)),)


def _row_spec(rb, w=D_MODEL):
    return pl.BlockSpec((rb, w), lambda i: (i, 0))


def _vec_spec(w=D_MODEL):
    return pl.BlockSpec((1, w), lambda i: (0, 0))


def _rstd(v):
    return lax.rsqrt(jnp.mean(v * v, axis=-1, keepdims=True) + EPS)


def _rms_fwd(x, g, name):
    L = x.shape[0]
    rb = min(NORM_ROWS, L)

    def body(x_ref, g_ref, h_ref, r_ref):
        xv = x_ref[...]
        r = _rstd(xv)
        h_ref[...] = (xv * r * g_ref[...]).astype(BF16)
        r_ref[...] = r

    return pl.pallas_call(
        body, name=name, grid=(L // rb,),
        in_specs=[_row_spec(rb), _vec_spec()],
        out_specs=[_row_spec(rb), _row_spec(rb, 1)],
        out_shape=[_sds((L, D_MODEL), BF16), _sds((L, 1))],
        compiler_params=_cp("parallel"),
    )(x, g)


def _post_pre_fwd(x_in, y, g_post, g_pre, name):
    L = x_in.shape[0]
    rb = min(NORM_ROWS, L)

    def body(x_ref, y_ref, gp_ref, gn_ref, xo_ref, ry_ref, h_ref, rx_ref):
        yv = y_ref[...]
        ry = _rstd(yv)
        xo = x_ref[...] + yv * ry * gp_ref[...]
        rx = _rstd(xo)
        xo_ref[...] = xo
        ry_ref[...] = ry
        h_ref[...] = (xo * rx * gn_ref[...]).astype(BF16)
        rx_ref[...] = rx

    return pl.pallas_call(
        body, name=name, grid=(L // rb,),
        in_specs=[_row_spec(rb), _row_spec(rb), _vec_spec(), _vec_spec()],
        out_specs=[_row_spec(rb), _row_spec(rb, 1), _row_spec(rb), _row_spec(rb, 1)],
        out_shape=[_sds((L, D_MODEL)), _sds((L, 1)), _sds((L, D_MODEL), BF16), _sds((L, 1))],
        compiler_params=_cp("parallel"),
    )(x_in, y, g_post, g_pre)


def _post_loss_fwd(x_in, y, g_post, target, name):
    L = x_in.shape[0]
    rb = min(NORM_ROWS, L)

    def body(x_ref, y_ref, gp_ref, t_ref, gx_ref, ry_ref, loss_ref):
        i = pl.program_id(0)
        yv = y_ref[...]
        ry = _rstd(yv)
        diff = x_ref[...] + yv * ry * gp_ref[...] - t_ref[...]
        gx_ref[...] = diff * (1.0 / D_MODEL)
        ry_ref[...] = ry

        @pl.when(i == 0)
        def _():
            loss_ref[...] = jnp.zeros_like(loss_ref)

        loss_ref[...] += jnp.sum(diff * diff, keepdims=True)

    return pl.pallas_call(
        body, name=name, grid=(L // rb,),
        in_specs=[_row_spec(rb), _row_spec(rb), _vec_spec(), _row_spec(rb)],
        out_specs=[_row_spec(rb), _row_spec(rb, 1), pl.BlockSpec((1, 1), lambda i: (0, 0))],
        out_shape=[_sds((L, D_MODEL)), _sds((L, 1)), _sds((1, 1))],
        compiler_params=_cp("arbitrary"),
    )(x_in, y, g_post, target)


def _rms_bwd_rows(dy, xv, r, g):
    n = xv * r
    dyg = dy * g
    return r * (dyg - n * jnp.mean(dyg * n, axis=-1, keepdims=True)), n


def _post_bwd(g_out, y, ry, g_post, name):
    L = y.shape[0]
    rb = min(NORM_ROWS, L)

    def body(go_ref, y_ref, ry_ref, gp_ref, gy_ref, gg_ref):
        i = pl.program_id(0)
        go = go_ref[...]
        gy, n = _rms_bwd_rows(go, y_ref[...], ry_ref[...], gp_ref[...])
        gy_ref[...] = gy.astype(BF16)

        @pl.when(i == 0)
        def _():
            gg_ref[...] = jnp.zeros_like(gg_ref)

        gg_ref[...] += jnp.sum(go * n, axis=0, keepdims=True)

    return pl.pallas_call(
        body, name=name, grid=(L // rb,),
        in_specs=[_row_spec(rb), _row_spec(rb), _row_spec(rb, 1), _vec_spec()],
        out_specs=[_row_spec(rb), _vec_spec()],
        out_shape=[_sds((L, D_MODEL), BF16), _sds((1, D_MODEL))],
        compiler_params=_cp("arbitrary"),
    )(g_out, y, ry, g_post)


def _pre_post_bwd(g_h, x, rx, g_pre, g_out, y_prev, ry_prev, g_post_prev, name):
    L = x.shape[0]
    rb = min(NORM_ROWS, L)

    def body(gh_ref, x_ref, rx_ref, gn_ref, go_ref, y_ref, ry_ref, gp_ref, gi_ref, ggn_ref, gy_ref, ggp_ref):
        i = pl.program_id(0)
        gh = gh_ref[...]
        gx, n = _rms_bwd_rows(gh, x_ref[...], rx_ref[...], gn_ref[...])
        gi = go_ref[...] + gx
        gi_ref[...] = gi
        gy, ny = _rms_bwd_rows(gi, y_ref[...], ry_ref[...], gp_ref[...])
        gy_ref[...] = gy.astype(BF16)

        @pl.when(i == 0)
        def _():
            ggn_ref[...] = jnp.zeros_like(ggn_ref)
            ggp_ref[...] = jnp.zeros_like(ggp_ref)

        ggn_ref[...] += jnp.sum(gh * n, axis=0, keepdims=True)
        ggp_ref[...] += jnp.sum(gi * ny, axis=0, keepdims=True)

    return pl.pallas_call(
        body, name=name, grid=(L // rb,),
        in_specs=[_row_spec(rb), _row_spec(rb), _row_spec(rb, 1), _vec_spec(), _row_spec(rb),
                  _row_spec(rb), _row_spec(rb, 1), _vec_spec()],
        out_specs=[_row_spec(rb), _vec_spec(), _row_spec(rb), _vec_spec()],
        out_shape=[_sds((L, D_MODEL)), _sds((1, D_MODEL)), _sds((L, D_MODEL), BF16), _sds((1, D_MODEL))],
        compiler_params=_cp("arbitrary"),
    )(g_h, x, rx, g_pre, g_out, y_prev, ry_prev, g_post_prev)


def _pre_bwd(g_h, x, rx, g_pre, g_out, name):
    L = x.shape[0]
    rb = min(NORM_ROWS, L)

    def body(gh_ref, x_ref, rx_ref, gn_ref, go_ref, gi_ref, ggn_ref):
        i = pl.program_id(0)
        gh = gh_ref[...]
        gx, n = _rms_bwd_rows(gh, x_ref[...], rx_ref[...], gn_ref[...])
        gi_ref[...] = go_ref[...] + gx

        @pl.when(i == 0)
        def _():
            ggn_ref[...] = jnp.zeros_like(ggn_ref)

        ggn_ref[...] += jnp.sum(gh * n, axis=0, keepdims=True)

    return pl.pallas_call(
        body, name=name, grid=(L // rb,),
        in_specs=[_row_spec(rb), _row_spec(rb), _row_spec(rb, 1), _vec_spec(), _row_spec(rb)],
        out_specs=[_row_spec(rb), _vec_spec()],
        out_shape=[_sds((L, D_MODEL)), _sds((1, D_MODEL))],
        compiler_params=_cp("arbitrary"),
    )(g_h, x, rx, g_pre, g_out)


def _cmul(ar, ai, br, bi):
    return ar * br - ai * bi, ar * bi + ai * br


def _zoh_cols(lr, li, ldt):
    dt = jnp.exp(ldt)
    mag = jnp.exp(lr * dt)
    ar = mag * jnp.cos(li * dt)
    ai = mag * jnp.sin(li * dt)
    den = lr * lr + li * li
    nr = ar - 1.0
    qr = (nr * lr + ai * li) / den
    qi = (ai * lr - nr * li) / den
    return dt, ar, ai, qr, qi, den


def _b_mask():
    r = lax.broadcasted_iota(jnp.int32, (S5_NS, LANES), 0)
    c = lax.broadcasted_iota(jnp.int32, (S5_NS, LANES), 1)
    return ((r >> 6) & 7) == (c >> 4)


def _c_mask():
    r = lax.broadcasted_iota(jnp.int32, (S5_W, 512), 0)
    c = lax.broadcasted_iota(jnp.int32, (S5_W, 512), 1)
    return ((r >> 4) & 7) == (c >> 6)


def _s5_prep(lam_r, ldt_r, lam_c, ldt_c, b_t, c_t, name):
    def body(lam_r_ref, ldt_r_ref, lam_c_ref, ldt_c_ref, b_ref, c_ref, tab_ref, bset_ref, cset_ref):
        lr, li = lam_r_ref[0:1, :], lam_r_ref[1:2, :]
        dt = jnp.exp(ldt_r_ref[...])
        mag = jnp.exp(lr * dt)
        p1r, p1i = mag * jnp.cos(li * dt), mag * jnp.sin(li * dt)
        p2r, p2i = _cmul(p1r, p1i, p1r, p1i)
        p3r, p3i = _cmul(p2r, p2i, p1r, p1i)
        p4r, p4i = _cmul(p2r, p2i, p2r, p2i)
        p5r, p5i = _cmul(p4r, p4i, p1r, p1i)
        p6r, p6i = _cmul(p4r, p4i, p2r, p2i)
        p7r, p7i = _cmul(p4r, p4i, p3r, p3i)
        p8r, p8i = _cmul(p4r, p4i, p4r, p4i)
        pw_r = [p1r, p2r, p3r, p4r, p5r, p6r, p7r, p8r]
        pw_i = [p1i, p2i, p3i, p4i, p5i, p6i, p7i, p8i]
        row = lax.broadcasted_iota(jnp.int32, (8, S5_NS), 0)
        zero = jnp.zeros((8, S5_NS), F32)

        def bc(v):
            return jnp.broadcast_to(v, (8, S5_NS))

        for d in range(2):
            sgn = 1.0 if d == 0 else -1.0
            for t, s in enumerate((1, 2, 4)):
                live = (row >= s) if d == 0 else (row <= 7 - s)
                tab_ref[d, 2 * t] = jnp.where(live, bc(pw_r[s - 1]), zero)
                tab_ref[d, 2 * t + 1] = jnp.where(live, bc(sgn * pw_i[s - 1]), zero)
            cr, ci = zero, zero
            for i in range(8):
                e = i if d == 0 else 7 - i
                cr = jnp.where(row == i, bc(pw_r[e]), cr)
                ci = jnp.where(row == i, bc(sgn * pw_i[e]), ci)
            tab_ref[d, 6] = cr
            tab_ref[d, 7] = ci

        _, _, _, qr, qi, _ = _zoh_cols(lam_c_ref[:, 0:1], lam_c_ref[:, 1:2], ldt_c_ref[...])
        bm = _b_mask()
        br, bi = b_ref[0], b_ref[1]
        bset_ref[0] = jnp.where(bm, qr * br - qi * bi, 0.0).astype(BF16)
        bset_ref[1] = jnp.where(bm, qr * bi + qi * br, 0.0).astype(BF16)
        cm = _c_mask()
        cset_ref[0] = jnp.where(cm, c_ref[0], 0.0).astype(BF16)
        cset_ref[1] = jnp.where(cm, c_ref[1], 0.0).astype(BF16)

    vm = pl.BlockSpec(memory_space=pltpu.VMEM)
    return pl.pallas_call(
        body, name=name, in_specs=[vm] * 6, out_specs=[vm] * 3,
        out_shape=[_sds((2, 8, 8, S5_NS)), _sds((2, S5_NS, LANES), BF16), _sds((2, S5_W, 512), BF16)],
        compiler_params=pltpu.CompilerParams(vmem_limit_bytes=VMEM_LIMIT),
    )(lam_r, ldt_r, lam_c, ldt_c, b_t, c_t)


SCAN_W = SCAN_GROUPS * LANES


def _scan_chunk(src_ref, dst_ref, tab_ref, carry_ref, nb, reverse, xs_ref=None, acc_ref=None):
    row = lax.broadcasted_iota(jnp.int32, (8, LANES), 0)

    def step(i, carry):
        b = (nb - 1 - i) if reverse else i
        off = pl.multiple_of(b * 8, 8)
        out = []
        for g in range(SCAN_GROUPS):
            lanes = pl.ds(g * LANES, LANES)
            cr, ci = carry[2 * g], carry[2 * g + 1]
            yr = src_ref[0, pl.ds(off, 8), lanes]
            yi = src_ref[1, pl.ds(off, 8), lanes]
            for t, s in enumerate((1, 2, 4)):
                sh = (8 - s) if reverse else s
                sr = pltpu.roll(yr, sh, 0)
                si = pltpu.roll(yi, sh, 0)
                mr, mi = tab_ref[2 * t, :, lanes], tab_ref[2 * t + 1, :, lanes]
                yr, yi = yr + mr * sr - mi * si, yi + mr * si + mi * sr
            pr, pi = tab_ref[6, :, lanes], tab_ref[7, :, lanes]
            yr, yi = yr + pr * cr - pi * ci, yi + pr * ci + pi * cr
            dst_ref[0, pl.ds(off, 8), lanes] = yr
            dst_ref[1, pl.ds(off, 8), lanes] = yi
            if xs_ref is not None:
                nr = jnp.where(row == 7, cr, pltpu.roll(yr, 7, 0))
                ni = jnp.where(row == 7, ci, pltpu.roll(yi, 7, 0))
                xr = xs_ref[0, pl.ds(off, 8), lanes]
                xi = xs_ref[1, pl.ds(off, 8), lanes]
                acc_ref[0, :, lanes] += xr * nr + xi * ni
                acc_ref[1, :, lanes] += xr * ni - xi * nr
            last = 0 if reverse else 7
            out += [jnp.broadcast_to(yr[last:last + 1, :], (8, LANES)),
                    jnp.broadcast_to(yi[last:last + 1, :], (8, LANES))]
        return tuple(out)

    init = []
    for g in range(SCAN_GROUPS):
        init += [carry_ref[0, :, pl.ds(g * LANES, LANES)], carry_ref[1, :, pl.ds(g * LANES, LANES)]]
    fin = lax.fori_loop(0, nb, step, tuple(init))
    for g in range(SCAN_GROUPS):
        carry_ref[0, :, pl.ds(g * LANES, LANES)] = fin[2 * g]
        carry_ref[1, :, pl.ds(g * LANES, LANES)] = fin[2 * g + 1]


def _s5_scan_fwd(z, bset, tabs, name):
    L = z.shape[0]
    tl = min(SCAN_CHUNK, L)
    nc = L // tl

    def body(u_ref, b_ref, tab_ref, x_ref, carry_ref):
        @pl.when(pl.program_id(1) == 0)
        def _():
            carry_ref[...] = jnp.zeros_like(carry_ref)

        u = u_ref[...].astype(BF16)
        x_ref[0] = _dot(u, b_ref[0], NT)
        x_ref[1] = _dot(u, b_ref[1], NT)
        _scan_chunk(x_ref, x_ref, tab_ref, carry_ref, tl // 8, False)

    return pl.pallas_call(
        body, name=name, grid=(S5_NS // SCAN_W, nc),
        in_specs=[pl.BlockSpec((tl, LANES), lambda j, c: (c, j)),
                  pl.BlockSpec((2, SCAN_W, LANES), lambda j, c: (0, j, 0)),
                  pl.BlockSpec((None, 8, 8, SCAN_W), lambda j, c: (0, 0, 0, j))],
        out_specs=pl.BlockSpec((2, tl, SCAN_W), lambda j, c: (0, c, j)),
        out_shape=_sds((2, L, S5_NS)),
        scratch_shapes=[pltpu.VMEM((2, 8, SCAN_W), F32)],
        compiler_params=_cp("parallel", "arbitrary"),
    )(z, bset, tabs)


def _s5_scan_bwd(gyl, cset, xs, z, bset, gud, tabs, name):
    L = z.shape[0]
    tl = min(SCAN_CHUNK, L)
    nc = L // tl

    def body(g_ref, c_ref, xs_ref, u_ref, b_ref, gud_ref, tab_ref, gu_ref, ga_ref, gb_ref, gc_ref,
             gx_ref, carry_ref, acc_ref):
        c = pl.program_id(1)

        @pl.when(c == 0)
        def _():
            carry_ref[...] = jnp.zeros_like(carry_ref)
            acc_ref[...] = jnp.zeros_like(acc_ref)
            gb_ref[...] = jnp.zeros_like(gb_ref)
            gc_ref[...] = jnp.zeros_like(gc_ref)

        gy = g_ref[...].astype(BF16)
        gx_ref[0] = _dot(gy, c_ref[0])
        gx_ref[1] = -_dot(gy, c_ref[1])
        gc_ref[0] += _dot(gy, xs_ref[0].astype(BF16), TN)
        gc_ref[1] -= _dot(gy, xs_ref[1].astype(BF16), TN)
        _scan_chunk(gx_ref, gx_ref, tab_ref, carry_ref, tl // 8, True, xs_ref, acc_ref)
        gr = gx_ref[0].astype(BF16)
        gi = gx_ref[1].astype(BF16)
        gu_ref[...] = gud_ref[...] + _dot(gr, b_ref[0]) + _dot(gi, b_ref[1])
        u = u_ref[...].astype(BF16)
        gb_ref[0] += _dot(gr, u, TN)
        gb_ref[1] += _dot(gi, u, TN)

        @pl.when(c == nc - 1)
        def _():
            ga_ref[0:1, :] = jnp.sum(acc_ref[0], axis=0, keepdims=True)
            ga_ref[1:2, :] = jnp.sum(acc_ref[1], axis=0, keepdims=True)

    rev = lambda j, c: (nc - 1 - c, j)
    col = pl.BlockSpec((tl, LANES), rev)
    return pl.pallas_call(
        body, name=name, grid=(S5_NS // SCAN_W, nc),
        in_specs=[col, pl.BlockSpec((2, LANES, SCAN_W), lambda j, c: (0, j, 0)),
                  pl.BlockSpec((2, tl, SCAN_W), lambda j, c: (0, nc - 1 - c, j)), col,
                  pl.BlockSpec((2, SCAN_W, LANES), lambda j, c: (0, j, 0)), col,
                  pl.BlockSpec((None, 8, 8, SCAN_W), lambda j, c: (1, 0, 0, j))],
        out_specs=[col, pl.BlockSpec((2, SCAN_W), lambda j, c: (0, j)),
                   pl.BlockSpec((2, SCAN_W, LANES), lambda j, c: (0, j, 0)),
                   pl.BlockSpec((2, LANES, SCAN_W), lambda j, c: (0, j, 0))],
        out_shape=[_sds((L, S5_W)), _sds((2, S5_NS)), _sds((2, S5_NS, LANES)), _sds((2, S5_W, 512))],
        scratch_shapes=[pltpu.VMEM((2, tl, SCAN_W), F32), pltpu.VMEM((2, 8, SCAN_W), F32),
                        pltpu.VMEM((2, 8, SCAN_W), F32)],
        compiler_params=_cp("parallel", "arbitrary"),
    )(gyl, cset, xs, z, bset, gud, tabs)


def _s5_out_fwd(xs, cset, z, dvec, wglu, name):
    L = z.shape[0]
    bl = min(256, L)

    def body(x_ref, c_ref, u_ref, d_ref, w_ref, ylin_ref, ya_ref):
        cols = []
        for j in range(4):
            xr = x_ref[0, :, 512 * j:512 * (j + 1)].astype(BF16)
            xi = x_ref[1, :, 512 * j:512 * (j + 1)].astype(BF16)
            cr = c_ref[0, LANES * j:LANES * (j + 1), :]
            ci = c_ref[1, LANES * j:LANES * (j + 1), :]
            cols.append(_dot(xr, cr, NT) - _dot(xi, ci, NT))
        ylin = jnp.concatenate(cols, axis=1) + d_ref[...] * u_ref[...]
        yg = _gelu(ylin)
        t = _dot(yg.astype(BF16), w_ref[...])
        ylin_ref[...] = ylin
        ya_ref[...] = (yg * _sigmoid(t)).astype(BF16)

    return pl.pallas_call(
        body, name=name, grid=(L // bl,),
        in_specs=[pl.BlockSpec((2, bl, S5_NS), lambda i: (0, i, 0)),
                  pl.BlockSpec((2, S5_W, 512), lambda i: (0, 0, 0)),
                  pl.BlockSpec((bl, S5_W), lambda i: (i, 0)),
                  pl.BlockSpec((1, S5_W), lambda i: (0, 0)),
                  pl.BlockSpec((S5_W, S5_W), lambda i: (0, 0))],
        out_specs=[pl.BlockSpec((bl, S5_W), lambda i: (i, 0))] * 2,
        out_shape=[_sds((L, S5_W)), _sds((L, S5_W), BF16)],
        compiler_params=_cp("parallel"),
    )(xs, cset, z, dvec, wglu)


def _s5_glu_bwd(g_m, ylin, z, dvec, wglu, name):
    L = z.shape[0]
    bl = min(256, L)

    def body(g_ref, ylin_ref, u_ref, d_ref, w_ref, gyl_ref, gud_ref, gw_ref, gd_ref):
        i = pl.program_id(0)
        ylin = ylin_ref[...]
        yg = _gelu(ylin)
        ygb = yg.astype(BF16)
        sg = _sigmoid(_dot(ygb, w_ref[...]))
        gya = g_ref[...]
        gt = gya * yg * sg * (1.0 - sg)
        gtb = gt.astype(BF16)
        gyg = gya * sg + _dot(gtb, w_ref[...], NT)
        gyl = gyg * _gelu_grad(ylin)
        gyl_ref[...] = gyl
        gud_ref[...] = gyl * d_ref[...]

        @pl.when(i == 0)
        def _():
            gw_ref[...] = jnp.zeros_like(gw_ref)
            gd_ref[...] = jnp.zeros_like(gd_ref)

        gw_ref[...] += _dot(ygb, gtb, TN)
        gd_ref[...] += jnp.sum(gyl * u_ref[...], axis=0, keepdims=True)

    blk = pl.BlockSpec((bl, S5_W), lambda i: (i, 0))
    return pl.pallas_call(
        body, name=name, grid=(L // bl,),
        in_specs=[blk, blk, blk, pl.BlockSpec((1, S5_W), lambda i: (0, 0)),
                  pl.BlockSpec((S5_W, S5_W), lambda i: (0, 0))],
        out_specs=[blk, blk, pl.BlockSpec((S5_W, S5_W), lambda i: (0, 0)), pl.BlockSpec((1, S5_W), lambda i: (0, 0))],
        out_shape=[_sds((L, S5_W)), _sds((L, S5_W)), _sds((S5_W, S5_W)), _sds((1, S5_W))],
        compiler_params=_cp("arbitrary"),
    )(g_m, ylin, z, dvec, wglu)


def _s5_param_bwd(lam_c, ldt_c, b_t, gb, ga_c, gc, name):
    def body(lam_ref, ldt_ref, b_ref, gb_ref, ga_ref, gc_ref, glam_ref, gldt_ref, gbo_ref, gco_ref):
        lr, li = lam_ref[:, 0:1], lam_ref[:, 1:2]
        dt, ar, ai, qr, qi, den = _zoh_cols(lr, li, ldt_ref[...])
        bm = _b_mask()
        gbr = jnp.where(bm, gb_ref[0], 0.0)
        gbi = jnp.where(bm, gb_ref[1], 0.0)
        br, bi = b_ref[0], b_ref[1]
        obr = gbr * qr + gbi * qi
        obi = gbi * qr - gbr * qi
        gqr = jnp.sum(gbr * br + gbi * bi, axis=1, keepdims=True)
        gqi = jnp.sum(gbi * br - gbr * bi, axis=1, keepdims=True)
        for s in (64, 32, 16):
            obr = obr + pltpu.roll(obr, s, 1)
            obi = obi + pltpu.roll(obi, s, 1)
        gbo_ref[0] = obr
        gbo_ref[1] = obi
        gar = ga_ref[:, 0:1] + (gqr * lr - gqi * li) / den
        gai = ga_ref[:, 1:2] + (gqr * li + gqi * lr) / den
        qlr = (qr * lr + qi * li) / den
        qli = (qi * lr - qr * li) / den
        glr = -(gqr * qlr + gqi * qli)
        gli = -(gqi * qlr - gqr * qli)
        glr = glr + dt * (gar * ar + gai * ai)
        gli = gli + dt * (gai * ar - gar * ai)
        wr, wi = _cmul(lr, li, ar, ai)
        gldt = (gar * wr + gai * wi) * dt
        glam_ref[:, 0:1] = glr
        glam_ref[:, 1:2] = gli
        r = lax.broadcasted_iota(jnp.int32, (S5_NS, 32), 0)
        c = lax.broadcasted_iota(jnp.int32, (S5_NS, 32), 1)
        gldt_ref[...] = jnp.sum(jnp.where((r >> 6) == c, gldt, 0.0), axis=0, keepdims=True)
        cm = _c_mask()
        for k in range(2):
            oc = jnp.where(cm, gc_ref[k], 0.0)
            for s in (256, 128, 64):
                oc = oc + pltpu.roll(oc, s, 1)
            gco_ref[k] = oc[:, 0:LANES]

    vm = pl.BlockSpec(memory_space=pltpu.VMEM)
    return pl.pallas_call(
        body, name=name, in_specs=[vm] * 6, out_specs=[vm] * 4,
        out_shape=[_sds((S5_NS, 2)), _sds((1, 32)), _sds((2, S5_NS, LANES)), _sds((2, S5_W, LANES))],
        compiler_params=pltpu.CompilerParams(vmem_limit_bytes=VMEM_LIMIT),
    )(lam_c, ldt_c, b_t, gb, ga_c, gc)


FL_BLK = EVEN_PAD // LANES - 1
Q_BLK, K_BLK, V_BLK = 4, 8, 12
NEG = -1e30


def _log_sigmoid(v):
    return jnp.minimum(v, 0.0) - jnp.log(1.0 + jnp.exp(-jnp.abs(v)))


def _fox_f_fwd(z, bf, name):
    L = z.shape[0]
    tl = min(256, L)

    def body(fl_ref, b_ref, f_ref, fq_ref, carry_ref):
        i = pl.program_id(0)

        @pl.when(i == 0)
        def _():
            carry_ref[...] = jnp.zeros_like(carry_ref)

        lf = _log_sigmoid(fl_ref[...] + b_ref[...])
        r = lax.broadcasted_iota(jnp.int32, (tl, tl), 0)
        c = lax.broadcasted_iota(jnp.int32, (tl, tl), 1)
        tri = (r >= c).astype(F32)
        cs = lax.dot_general(tri, lf, NN, precision=lax.Precision.HIGHEST, preferred_element_type=F32) + carry_ref[...]
        f_ref[...] = cs
        carry_ref[...] = cs[tl - 1:tl, :]
        expand = (lax.broadcasted_iota(jnp.int32, (LANES, FOX_W), 0)
                  == (lax.broadcasted_iota(jnp.int32, (LANES, FOX_W), 1) >> 6)).astype(F32)
        fq_ref[...] = lax.dot_general(cs, expand, NN, precision=lax.Precision.HIGHEST, preferred_element_type=F32)

    return pl.pallas_call(
        body, name=name, grid=(L // tl,),
        in_specs=[pl.BlockSpec((tl, LANES), lambda i: (i, FL_BLK)), pl.BlockSpec((1, LANES), lambda i: (0, 0))],
        out_specs=[pl.BlockSpec((tl, LANES), lambda i: (i, 0)), pl.BlockSpec((tl, FOX_W), lambda i: (i, 0))],
        out_shape=[_sds((L, LANES)), _sds((L, FOX_W))],
        scratch_shapes=[pltpu.VMEM((1, LANES), F32)],
        compiler_params=_cp("arbitrary"),
    )(z, bf)


def _fox_f_bwd(dFk, dfq, z, bf, name):
    L = z.shape[0]
    tl = min(256, L)
    nb = L // tl

    def body(dfk_ref, dfq_ref, fl_ref, b_ref, dfl_ref, db_ref, carry_ref):
        i = pl.program_id(0)

        @pl.when(i == 0)
        def _():
            carry_ref[...] = jnp.zeros_like(carry_ref)
            db_ref[...] = jnp.zeros_like(db_ref)

        sel = (lax.broadcasted_iota(jnp.int32, (FOX_W, LANES), 0)
               == 64 * lax.broadcasted_iota(jnp.int32, (FOX_W, LANES), 1)).astype(F32)
        dfq_h = lax.dot_general(dfq_ref[...], sel, NN, precision=lax.Precision.HIGHEST, preferred_element_type=F32)
        r = lax.broadcasted_iota(jnp.int32, (tl, tl), 0)
        c = lax.broadcasted_iota(jnp.int32, (tl, tl), 1)
        tri = (r <= c).astype(F32)
        cs = lax.dot_general(tri, dfk_ref[...] + dfq_h, NN, precision=lax.Precision.HIGHEST,
                             preferred_element_type=F32) + carry_ref[...]
        carry_ref[...] = cs[0:1, :]
        dfl = cs * _sigmoid(-(fl_ref[...] + b_ref[...]))
        dfl_ref[...] = dfl
        db_ref[...] += jnp.sum(dfl, axis=0, keepdims=True)

    return pl.pallas_call(
        body, name=name, grid=(nb,),
        in_specs=[pl.BlockSpec((tl, LANES), lambda i: (nb - 1 - i, 0)),
                  pl.BlockSpec((tl, FOX_W), lambda i: (nb - 1 - i, 0)),
                  pl.BlockSpec((tl, LANES), lambda i: (nb - 1 - i, FL_BLK)),
                  pl.BlockSpec((1, LANES), lambda i: (0, 0))],
        out_specs=[pl.BlockSpec((tl, LANES), lambda i: (nb - 1 - i, 0)), pl.BlockSpec((1, LANES), lambda i: (0, 0))],
        out_shape=[_sds((L, LANES)), _sds((1, LANES))],
        scratch_shapes=[pltpu.VMEM((1, LANES), F32)],
        compiler_params=_cp("arbitrary"),
    )(dFk, dfq, z, bf)


def _head_mask(hh):
    lane = lax.broadcasted_iota(jnp.int32, (1, LANES), 1)
    return (lane >> 6) == hh


FOX_T = 512


def _fox_head(x, hh):
    return jnp.where(_head_mask(hh), x, 0.0).astype(BF16)


def _fox_scores(qh, k, fq_ref, fr_ref, hh, causal):
    s = _dot(qh, k, NT) + (fq_ref[:, 64 * hh:64 * hh + 1] - fr_ref[hh:hh + 1, :])
    return s if causal is None else jnp.where(causal, s, NEG)


def _causal(T):
    return lax.broadcasted_iota(jnp.int32, (T, T), 1) <= lax.broadcasted_iota(jnp.int32, (T, T), 0)


def _fox_fwd(z, fq, frow, name):
    L = z.shape[0]
    T = min(FOX_T, L)
    nq = L // T

    def body(qt_ref, kt_ref, q_ref, k_ref, v_ref, fq_ref, fr_ref, o_ref, lse_ref, m_ref, l_ref, acc_ref):
        t = pl.program_id(1)
        qi, ki = qt_ref[t], kt_ref[t]

        @pl.when(ki == 0)
        def _():
            m_ref[...] = jnp.full_like(m_ref, NEG)
            l_ref[...] = jnp.zeros_like(l_ref)
            acc_ref[...] = jnp.zeros_like(acc_ref)

        def step(diagonal):
            q = q_ref[...] * 0.125
            k = k_ref[...].astype(BF16)
            v = v_ref[...].astype(BF16)
            causal = _causal(T) if diagonal else None
            s = jnp.concatenate([_fox_scores(_fox_head(q, hh), k, fq_ref, fr_ref, hh, causal) for hh in range(2)],
                                axis=0)
            m_old = m_ref[...]
            m_new = jnp.maximum(m_old, jnp.max(s, axis=1, keepdims=True))
            alpha = jnp.exp(m_old - m_new)
            p = jnp.exp(s - m_new)
            l_ref[...] = alpha * l_ref[...] + jnp.sum(p, axis=1, keepdims=True)
            m_ref[...] = m_new
            acc_ref[...] = alpha * acc_ref[...] + _dot(p.astype(BF16), v)

        @pl.when(ki < qi)
        def _():
            step(False)

        @pl.when(ki == qi)
        def _():
            step(True)
            h0 = _head_mask(0)
            l = l_ref[...]
            o_h = acc_ref[...] / l
            lse_h = m_ref[...] + jnp.log(l)
            o_ref[...] = jnp.where(h0, o_h[:T], o_h[T:])
            lse_ref[...] = jnp.where(h0, lse_h[:T], lse_h[T:])

    pairs = [(qi, ki) for qi in range(nq) for ki in range(qi + 1)]
    qt = jnp.asarray([p[0] for p in pairs], jnp.int32)
    kt = jnp.asarray([p[1] for p in pairs], jnp.int32)

    def qspec(base):
        return pl.BlockSpec((T, LANES), lambda j, t, qt, kt: (qt[t], base + j))

    def kspec(base):
        return pl.BlockSpec((T, LANES), lambda j, t, qt, kt: (kt[t], base + j))

    return pl.pallas_call(
        body, name=name,
        grid_spec=pltpu.PrefetchScalarGridSpec(
            num_scalar_prefetch=2, grid=(4, len(pairs)),
            in_specs=[qspec(Q_BLK), kspec(K_BLK), kspec(V_BLK), qspec(0),
                      pl.BlockSpec((None, 2, T), lambda j, t, qt, kt: (j, 0, kt[t]))],
            out_specs=[qspec(0), qspec(0)],
            scratch_shapes=[pltpu.VMEM((2 * T, 1), F32), pltpu.VMEM((2 * T, 1), F32),
                            pltpu.VMEM((2 * T, LANES), F32)]),
        out_shape=[_sds((L, FOX_W)), _sds((L, FOX_W))],
        compiler_params=_cp("parallel", "arbitrary"),
    )(qt, kt, z, z, z, fq, frow)


def _fox_bwd(z, fq, frow, o, lse, g_m, name):
    L = z.shape[0]
    T = min(FOX_T, L)
    nq = L // T

    pairs = [(qi, ki) for ki in range(nq) for qi in range(ki, nq)]
    qt = jnp.asarray([p[0] for p in pairs], jnp.int32)
    kt = jnp.asarray([p[1] for p in pairs], jnp.int32)

    def body(qt_ref, kt_ref, q_ref, k_ref, v_ref, fq_ref, fr_ref, o_ref, lse_ref, do_ref,
             dq_ref, dk_ref, dv_ref, dfq_ref, dfk_ref, dk_acc, dv_acc, df_acc):
        t = pl.program_id(1)
        qi, ki = qt_ref[t], kt_ref[t]

        @pl.when(t == 0)
        def _():
            dq_ref[...] = jnp.zeros_like(dq_ref)
            dfq_ref[...] = jnp.zeros_like(dfq_ref)

        @pl.when(qi == ki)
        def _():
            dk_acc[...] = jnp.zeros_like(dk_acc)
            dv_acc[...] = jnp.zeros_like(dv_acc)
            df_acc[...] = jnp.zeros_like(df_acc)

        def step(diagonal):
            q = q_ref[...] * 0.125
            qb = q.astype(BF16)
            k = k_ref[...].astype(BF16)
            v = v_ref[...].astype(BF16)
            do = do_ref[...]
            dob = do.astype(BF16)
            do_o = dob.astype(F32) * o_ref[...]
            causal = _causal(T) if diagonal else None
            dvs, dks, dqs, rss = [], [], [], []
            for hh in range(2):
                s = _fox_scores(_fox_head(q, hh), k, fq_ref, fr_ref, hh, causal)
                p = jnp.exp(s - lse_ref[:, 64 * hh:64 * hh + 1])
                dp = _dot(_fox_head(do, hh), v, NT)
                delta = jnp.sum(jnp.where(_head_mask(hh), do_o, 0.0), axis=1, keepdims=True)
                ds = p * (dp - delta)
                dsb = ds.astype(BF16)
                dvs.append(_dot(p.astype(BF16), dob, TN))
                dks.append(_dot(dsb, qb, TN))
                dqs.append(_dot(dsb, k))
                rss.append(jnp.sum(ds, axis=1, keepdims=True))
                df_acc[hh:hh + 1, :] -= jnp.sum(ds, axis=0, keepdims=True)
            h0 = _head_mask(0)
            dv_acc[...] += jnp.where(h0, dvs[0], dvs[1])
            dk_acc[...] += jnp.where(h0, dks[0], dks[1])
            rows = pl.ds(pl.multiple_of(qi * T, T), T)
            dq_ref[rows, :] += jnp.where(h0, dqs[0], dqs[1])
            dfq_ref[rows, :] += jnp.where(h0, rss[0], rss[1])

        @pl.when(qi > ki)
        def _():
            step(False)

        @pl.when(qi == ki)
        def _():
            step(True)

        @pl.when(qi == nq - 1)
        def _():
            dk_ref[...] = dk_acc[...]
            dv_ref[...] = dv_acc[...]
            dfk_ref[...] = df_acc[...]

        @pl.when(t == len(pairs) - 1)
        def _():
            dq_ref[...] = dq_ref[...] * 0.125

    def qside(base):
        return pl.BlockSpec((T, LANES), lambda j, t, qt, kt: (qt[t], base + j))

    def kside(base):
        return pl.BlockSpec((T, LANES), lambda j, t, qt, kt: (kt[t], base + j))

    pair = pl.BlockSpec((L, LANES), lambda j, t, qt, kt: (0, j))
    frow_spec = pl.BlockSpec((None, 2, T), lambda j, t, qt, kt: (j, 0, kt[t]))
    return pl.pallas_call(
        body, name=name,
        grid_spec=pltpu.PrefetchScalarGridSpec(
            num_scalar_prefetch=2, grid=(4, len(pairs)),
            in_specs=[qside(Q_BLK), kside(K_BLK), kside(V_BLK), qside(0), frow_spec, qside(0), qside(0), qside(4)],
            out_specs=[pair, kside(0), kside(0), pair, frow_spec],
            scratch_shapes=[pltpu.VMEM((T, LANES), F32), pltpu.VMEM((T, LANES), F32), pltpu.VMEM((2, T), F32)]),
        out_shape=[_sds((L, FOX_W)), _sds((L, FOX_W)), _sds((L, FOX_W)), _sds((L, FOX_W)), _sds((4, 2, L))],
        compiler_params=_cp("parallel", "arbitrary"),
    )(qt, kt, z, z, z, fq, frow, o, lse, g_m)


def _shift_rows(v, s, down, row):
    n = v.shape[0]
    if down:
        return jnp.where(row >= s, pltpu.roll(v, s, 0), 0.0)
    return jnp.where(row < n - s, pltpu.roll(v, n - s, 0), 0.0)


def _window_sum(v, g, down, row):
    out = jnp.zeros_like(v)
    s = v
    for k in range(4):
        s = s + _shift_rows(s, 1 << k, down, row)
        out = jnp.where(g == k, s, out)
    return out


def _pool_inv_cnt(g, row):
    w = jnp.left_shift(2, g).astype(F32)
    return 1.0 / jnp.minimum(row.astype(F32) + 1.0, w)


def _pool_fwd(z, pool_w, scale, name):
    L = z.shape[0]

    def body(x_ref, w_ref, s_ref, y_ref, p_ref):
        g = pl.program_id(0)
        row = lax.broadcasted_iota(jnp.int32, (L, LANES), 0)
        x = x_ref[...]
        pooled = (_window_sum(x, g, True, row) * _pool_inv_cnt(g, row) - x).astype(BF16)
        p_ref[...] = pooled
        y_ref[...] = (_dot(pooled, w_ref[...].astype(BF16)) * s_ref[...]).astype(BF16)

    col = pl.BlockSpec((L, LANES), lambda g: (0, g))
    return pl.pallas_call(
        body, name=name, grid=(4,),
        in_specs=[col, pl.BlockSpec((None, LANES, LANES), lambda g: (g, 0, 0)), pl.BlockSpec((1, LANES), lambda g: (0, g))],
        out_specs=[col, col],
        out_shape=[_sds((L, 512), BF16), _sds((L, 512), BF16)],
        compiler_params=_cp("parallel"),
    )(z, pool_w, scale)


def _pool_bwd(g_m, pooled, pool_w, scale, name):
    L = g_m.shape[0]

    def body(g_ref, p_ref, w_ref, s_ref, gx_ref, gw_ref, gs_ref):
        g = pl.program_id(0)
        row = lax.broadcasted_iota(jnp.int32, (L, LANES), 0)
        gy = g_ref[...]
        pooled = p_ref[...]
        wb = w_ref[...].astype(BF16)
        lin = _dot(pooled, wb)
        gs_ref[...] = jnp.sum(gy * lin, axis=0, keepdims=True)
        glin = (gy * s_ref[...]).astype(BF16)
        gw_ref[...] = _dot(pooled, glin, TN)
        gp = _dot(glin, wb, NT)
        gx_ref[...] = _window_sum(gp * _pool_inv_cnt(g, row), g, False, row) - gp

    col = pl.BlockSpec((L, LANES), lambda g: (0, g))
    wspec = pl.BlockSpec((None, LANES, LANES), lambda g: (g, 0, 0))
    vec = pl.BlockSpec((1, LANES), lambda g: (0, g))
    return pl.pallas_call(
        body, name=name, grid=(4,),
        in_specs=[col, col, wspec, vec],
        out_specs=[col, wspec, vec],
        out_shape=[_sds((L, 512)), _sds((4, LANES, LANES)), _sds((1, 512))],
        compiler_params=_cp("parallel"),
    )(g_m, pooled, pool_w, scale)


SGU_CHUNKS = 4


def _sgu_ln(v, gam, bet):
    gv = _gelu(v)
    mu = jnp.mean(gv, axis=-1, keepdims=True)
    xc = gv - mu
    rs = lax.rsqrt(jnp.mean(xc * xc, axis=-1, keepdims=True) + EPS)
    xh = xc * rs
    return xh, rs, xh * gam + bet


def _tril_ws(w_ref, g):
    r = lax.broadcasted_iota(jnp.int32, (LANES, LANES), 0)
    c = lax.broadcasted_iota(jnp.int32, (LANES, LANES), 1)
    return jnp.where(r >= c, w_ref[g], 0.0).astype(BF16)


def _sgu_fwd(z, ln_g, ln_b, w_s, b_st, name):
    L = z.shape[0]
    rb = min(SGU_CHUNKS * LANES, L)

    def body(u_ref, v_ref, g_ref, b_ref, w_ref, bs_ref, y_ref):
        _, _, vln = _sgu_ln(v_ref[...], g_ref[...], b_ref[...])
        gu = _gelu(u_ref[...])
        vb = vln.astype(BF16)
        for g in range(4):
            ws = _tril_ws(w_ref, g)
            for n in range(rb // LANES):
                rows = slice(n * LANES, (n + 1) * LANES)
                cols = slice(g * LANES, (g + 1) * LANES)
                mixed = _dot(ws, vb[rows, cols]) + bs_ref[:, g:g + 1]
                y_ref[rows, cols] = (gu[rows, cols] * mixed).astype(BF16)

    vm = lambda shape: pl.BlockSpec(shape, lambda i: tuple(0 for _ in shape))
    return pl.pallas_call(
        body, name=name, grid=(L // rb,),
        in_specs=[pl.BlockSpec((rb, 512), lambda i: (i, 1)), pl.BlockSpec((rb, 512), lambda i: (i, 2)),
                  vm((1, 512)), vm((1, 512)), vm((4, LANES, LANES)), vm((LANES, 4))],
        out_specs=pl.BlockSpec((rb, 512), lambda i: (i, 0)),
        out_shape=_sds((L, 512), BF16),
        compiler_params=_cp("parallel"),
    )(z, z, ln_g, ln_b, w_s, b_st)


def _sgu_bwd(g_m, z, ln_g, ln_b, w_s, b_st, name):
    L = z.shape[0]
    rb = min(SGU_CHUNKS * LANES, L)

    def body(gy_ref, u_ref, v_ref, g_ref, b_ref, w_ref, bs_ref, gu_ref, gv_ref, gw_ref, gbs_ref, gg_ref, gb_ref):
        i = pl.program_id(0)

        @pl.when(i == 0)
        def _():
            gw_ref[...] = jnp.zeros_like(gw_ref)
            gbs_ref[...] = jnp.zeros_like(gbs_ref)
            gg_ref[...] = jnp.zeros_like(gg_ref)
            gb_ref[...] = jnp.zeros_like(gb_ref)

        v = v_ref[...]
        u = u_ref[...]
        gy = gy_ref[...]
        xh, rs, vln = _sgu_ln(v, g_ref[...], b_ref[...])
        gel_u = _gelu(u)
        gmix = gy * gel_u
        vb = vln.astype(BF16)
        gmb = gmix.astype(BF16)
        r = lax.broadcasted_iota(jnp.int32, (LANES, LANES), 0)
        c = lax.broadcasted_iota(jnp.int32, (LANES, LANES), 1)
        gvln_cols = []
        for g in range(4):
            ws = _tril_ws(w_ref, g)
            cols = slice(g * LANES, (g + 1) * LANES)
            gw = jnp.zeros((LANES, LANES), F32)
            gbs = jnp.zeros((LANES, 1), F32)
            parts = []
            for n in range(rb // LANES):
                rows = slice(n * LANES, (n + 1) * LANES)
                mixed = _dot(ws, vb[rows, cols]) + bs_ref[:, g:g + 1]
                gu_ref[rows, cols] = gy[rows, cols] * mixed * _gelu_grad(u[rows, cols])
                parts.append(_dot(ws, gmb[rows, cols], TN))
                gw = gw + _dot(gmb[rows, cols], vb[rows, cols], NT)
                gbs = gbs + jnp.sum(gmix[rows, cols], axis=1, keepdims=True)
            gvln_cols.append(jnp.concatenate(parts, axis=0))
            gw_ref[g] += jnp.where(r >= c, gw, 0.0)
            gbs_ref[:, g:g + 1] += gbs
        gvln = jnp.concatenate(gvln_cols, axis=1)
        gg_ref[...] += jnp.sum(gvln * xh, axis=0, keepdims=True)
        gb_ref[...] += jnp.sum(gvln, axis=0, keepdims=True)
        gxh = gvln * g_ref[...]
        ggv = rs * (gxh - jnp.mean(gxh, axis=-1, keepdims=True) - xh * jnp.mean(gxh * xh, axis=-1, keepdims=True))
        gv_ref[...] = ggv * _gelu_grad(v)

    vm = lambda shape: pl.BlockSpec(shape, lambda i: tuple(0 for _ in shape))
    blk = pl.BlockSpec((rb, 512), lambda i: (i, 0))
    return pl.pallas_call(
        body, name=name, grid=(L // rb,),
        in_specs=[pl.BlockSpec((rb, 512), lambda i: (i, 1)), pl.BlockSpec((rb, 512), lambda i: (i, 1)),
                  pl.BlockSpec((rb, 512), lambda i: (i, 2)),
                  vm((1, 512)), vm((1, 512)), vm((4, LANES, LANES)), vm((LANES, 4))],
        out_specs=[blk, blk, vm((4, LANES, LANES)), vm((LANES, 4)), vm((1, 512)), vm((1, 512))],
        out_shape=[_sds((L, 512)), _sds((L, 512)), _sds((4, LANES, LANES)), _sds((LANES, 4)),
                   _sds((1, 512)), _sds((1, 512))],
        compiler_params=_cp("arbitrary"),
    )(g_m, z, z, ln_g, ln_b, w_s, b_st)


def _adamw_math(w, g, m, v):
    nm = ADAM_B1 * m + (1.0 - ADAM_B1) * g
    nv = ADAM_B2 * v + (1.0 - ADAM_B2) * (g * g)
    m_hat = nm / (1.0 - ADAM_B1 ** ADAM_STEP)
    v_hat = nv / (1.0 - ADAM_B2 ** ADAM_STEP)
    delta = -ADAM_LR * (m_hat / (jnp.sqrt(v_hat) + ADAM_EPS) + ADAM_WD * w)
    return delta, nm, nv


def _sum_adamw(parts, w, m, v, name, layer=0, prev=None):
    n_layers, R, C = w.shape
    rb = 128 if R % 128 == 0 else R

    def body(p_ref, w_ref, m_ref, v_ref, *rest):
        g_ref, d_ref, nm_ref, nv_ref = rest[-4:]
        g = p_ref[0].astype(F32)
        for s in range(1, N_DEV):
            g = g + p_ref[s].astype(F32)
        d, nm, nv = _adamw_math(w_ref[...], g, m_ref[...], v_ref[...])
        g_ref[...] = g
        d_ref[...] = d
        nm_ref[...] = nm
        nv_ref[...] = nv

    blk = pl.BlockSpec((None, rb, C), lambda i: (layer, i, 0))
    prev = [] if prev is None else list(prev)
    return pl.pallas_call(
        body, name=name, grid=(R // rb,),
        in_specs=[pl.BlockSpec((N_DEV, rb, C), lambda i: (0, i, 0)), blk, blk, blk] + [ANY] * len(prev),
        out_specs=[blk] * 4, out_shape=[_sds((n_layers, R, C))] * 4,
        input_output_aliases={4 + k: k for k in range(len(prev))},
        compiler_params=_cp("parallel"),
    )(parts, w, m, v, *prev)


def _sum_pieces(parts, name):
    _, R, C = parts.shape

    def body(p_ref, g_ref):
        g = p_ref[0]
        for s in range(1, N_DEV):
            g = g + p_ref[s]
        g_ref[...] = g

    vm = pl.BlockSpec(memory_space=pltpu.VMEM)
    return pl.pallas_call(body, name=name, in_specs=[vm], out_specs=vm, out_shape=_sds((R, C)))(parts)


def _adamw(w, g, m, v, name):
    vm = pl.BlockSpec(memory_space=pltpu.VMEM)

    def body(w_ref, g_ref, m_ref, v_ref, d_ref, nm_ref, nv_ref):
        d, nm, nv = _adamw_math(w_ref[...], g_ref[...], m_ref[...], v_ref[...])
        d_ref[...] = d
        nm_ref[...] = nm
        nv_ref[...] = nv

    return pl.pallas_call(body, name=name, in_specs=[vm] * 4, out_specs=[vm] * 3,
                          out_shape=[_sds(w.shape)] * 3)(w, g, m, v)


def _mesh_pos():
    return lax.axis_index("x"), lax.axis_index("y"), lax.axis_index("c")


def _dev_index(p):
    return 4 * p[0] + 2 * p[1] + p[2]


def _all_gather(xs, name):
    n = len(xs)

    def body(*refs):
        x_refs, o_refs = refs[:n], refs[n:2 * n]
        send_sems, recv_sems, local_sems = refs[2 * n:]
        x, y, c = _mesh_pos()
        me, sibling = (x, y, c), (x, y, 1 - c)
        chips = [(1 - x, y), (x, 1 - y), (1 - x, 1 - y)]

        def copy(i, k, block, to, src=None):
            dst = o_refs[i].at[_dev_index(block)]
            return pltpu.make_async_remote_copy(
                src_ref=dst if src is None else src, dst_ref=dst,
                send_sem=send_sems.at[i, k], recv_sem=recv_sems.at[i, k], device_id=to, device_id_type=MESH)

        mine = [pltpu.make_async_copy(x_refs[i], o_refs[i].at[_dev_index(me)], local_sems.at[i]) for i in range(n)]
        for cp in mine:
            cp.start()
        first = []
        for i in range(n):
            first.append(copy(i, 0, me, sibling, src=x_refs[i]))
            first += [copy(i, 1 + j, me, (*chip, c), src=x_refs[i]) for j, chip in enumerate(chips)]
        for cp in first:
            cp.start()
        passed = []
        for j, chip in enumerate(chips):
            for i in range(n):
                copy(i, 1 + j, (*chip, c), me).wait_recv()
                fwd = copy(i, 4 + j, (*chip, c), sibling)
                fwd.start()
                passed.append(fwd)
        for i in range(n):
            copy(i, 0, sibling, me).wait_recv()
            for j, chip in enumerate(chips):
                copy(i, 4 + j, (*chip, 1 - c), me).wait_recv()
        for cp in first + passed:
            cp.wait_send()
        for cp in mine:
            cp.wait()

    outs = pl.pallas_call(
        body, name=name,
        in_specs=[ANY] * n, out_specs=[ANY] * n,
        out_shape=[_sds((N_DEV,) + x.shape, x.dtype) for x in xs],
        scratch_shapes=[pltpu.SemaphoreType.DMA((n, 7)), pltpu.SemaphoreType.DMA((n, 7)),
                        pltpu.SemaphoreType.DMA((n,))],
    )(*xs)
    return list(outs)


def _exchange(gs, name):
    n = len(gs)

    def body(*refs):
        g_refs, o_refs = refs[:n], refs[n:2 * n]
        send_sems, recv_sems, local_sems = refs[2 * n:]
        x, y, c = _mesh_pos()
        me = (x, y, c)
        mi = _dev_index(me)
        peers = [(x ^ dx, y ^ dy, c ^ dc) for dx in range(2) for dy in range(2) for dc in range(2)][1:]

        def copy(i, k, peer):
            return pltpu.make_async_remote_copy(
                src_ref=g_refs[i].at[_dev_index(peer)], dst_ref=o_refs[i].at[mi],
                send_sem=send_sems.at[i, k], recv_sem=recv_sems.at[i, k], device_id=peer, device_id_type=MESH)

        mine = [pltpu.make_async_copy(g_refs[i].at[mi], o_refs[i].at[mi], local_sems.at[i]) for i in range(n)]
        for cp in mine:
            cp.start()
        sends = [copy(i, k, peer) for i in range(n) for k, peer in enumerate(peers)]
        for cp in sends:
            cp.start()
        for i in range(n):
            for k, peer in enumerate(peers):
                pltpu.make_async_remote_copy(
                    src_ref=g_refs[i].at[mi], dst_ref=o_refs[i].at[_dev_index(peer)],
                    send_sem=send_sems.at[i, k], recv_sem=recv_sems.at[i, k], device_id=peer,
                    device_id_type=MESH).wait_recv()
        for cp in sends:
            cp.wait_send()
        for cp in mine:
            cp.wait()

    outs = pl.pallas_call(
        body, name=name,
        in_specs=[ANY] * n, out_specs=[ANY] * n,
        out_shape=[_sds(g.shape, g.dtype) for g in gs],
        scratch_shapes=[pltpu.SemaphoreType.DMA((n, 7)), pltpu.SemaphoreType.DMA((n, 7)),
                        pltpu.SemaphoreType.DMA((n,))],
    )(*gs)
    return list(outs)


HBM = pl.BlockSpec(memory_space=pltpu.HBM)
SEM = pl.BlockSpec(memory_space=pltpu.SEMAPHORE)
EFFECT = pltpu.SideEffectType.DATAFLOW_SIDE_EFFECTING


def _peer_list():
    x, y, c = _mesh_pos()
    peers = [(x ^ dx, y ^ dy, c ^ dc) for dx in range(2) for dy in range(2) for dc in range(2)][1:]
    return (x, y, c), peers


def _split_copy(src_ref, land_ref, send_sems, recv_sems, i, k, peer, slot, exchange):
    return pltpu.make_async_remote_copy(
        src_ref=src_ref.at[_dev_index(peer)] if exchange else src_ref, dst_ref=land_ref.at[slot],
        send_sem=send_sems.at[7 * i + k], recv_sem=recv_sems.at[7 * i + k], device_id=peer, device_id_type=MESH)


def _comm_start(groups, name, exchange, dep=None):
    sizes = [len(g) for g in groups]
    n = sum(sizes)
    srcs = [a for g in groups for a in g]
    my_index = _dev_index(_mesh_pos())
    lands = []
    for a in srcs:
        if exchange:
            own = lax.dynamic_slice(a, (my_index, 0, 0), (1,) + a.shape[1:])
            shape = a.shape
        else:
            own = a[None]
            shape = (N_DEV,) + a.shape
        lands.append(lax.dynamic_update_slice(lax.empty(shape, a.dtype), own, (my_index, 0, 0)))

    n_dep = 0 if dep is None else 1

    def body(*refs):
        src_refs, land_refs = refs[:n], refs[n:2 * n]
        sem_refs = refs[2 * n + n_dep:2 * n + n_dep + 2 * len(sizes)]
        token_ref = refs[-1]
        me, peers = _peer_list()
        mi = _dev_index(me)
        i = 0
        for gi, sz in enumerate(sizes):
            for j in range(sz):
                for k, peer in enumerate(peers):
                    _split_copy(src_refs[i], land_refs[i], sem_refs[2 * gi], sem_refs[2 * gi + 1], j, k, peer, mi,
                                exchange).start()
                i += 1
        token_ref[...] = jnp.zeros_like(token_ref)

    sem_shapes = []
    for sz in sizes:
        sem_shapes += [pltpu.SemaphoreType.DMA((7 * sz,)), pltpu.SemaphoreType.DMA((7 * sz,))]
    thru = [pltpu.HBM(a.shape, a.dtype) for a in srcs + lands]
    n_sem = len(sem_shapes)
    outs = pl.pallas_call(
        body, name=name,
        out_shape=tuple(sem_shapes + thru + [_sds((8, LANES))]),
        in_specs=[HBM] * (2 * n) + [ANY] * n_dep,
        out_specs=tuple([SEM] * n_sem + [HBM] * (2 * n) + [pl.BlockSpec(memory_space=pltpu.VMEM)]),
        input_output_aliases={i: n_sem + i for i in range(2 * n)},
        compiler_params=pltpu.CompilerParams(has_side_effects=EFFECT),
    )(*[pltpu.with_memory_space_constraint(a, pltpu.HBM) for a in srcs + lands], *([] if dep is None else [dep]))
    sems, thru_src, thru_land, token = outs[:n_sem], outs[n_sem:n_sem + n], outs[n_sem + n:n_sem + 2 * n], outs[-1]
    result, off = [], 0
    for gi, sz in enumerate(sizes):
        result.append((sems[2 * gi], sems[2 * gi + 1], list(thru_src[off:off + sz]), list(thru_land[off:off + sz])))
        off += sz
    return result, token


def _comm_wait(group, after, name, exchange):
    send_sems, recv_sems, srcs, lands = group
    n = len(srcs)
    after = list(after) if isinstance(after, (list, tuple)) else [after]

    def body(*refs):
        src_refs, land_refs = refs[:n], refs[n:2 * n]
        ssem, rsem = refs[2 * n], refs[2 * n + 1]
        me, peers = _peer_list()
        for i in range(n):
            for k, peer in enumerate(peers):
                cp = _split_copy(src_refs[i], land_refs[i], ssem, rsem, i, k, peer, _dev_index(peer), exchange)
                cp.wait_send()
                cp.wait_recv()

    outs = pl.pallas_call(
        body, name=name,
        out_shape=tuple(pltpu.HBM(a.shape, a.dtype) for a in srcs + lands),
        in_specs=[HBM] * (2 * n) + [SEM, SEM] + [ANY] * len(after),
        out_specs=tuple([HBM] * (2 * n)),
        input_output_aliases={i: i for i in range(2 * n)},
        compiler_params=pltpu.CompilerParams(has_side_effects=EFFECT),
    )(*srcs, *lands, send_sems, recv_sems, *after)
    return list(outs[n:])


def _tie(a, token):
    return a + token[0, 0].astype(a.dtype)


def _pack(arrs, rows):
    flat = jnp.concatenate([a.reshape(-1).astype(F32) for a in arrs])
    return jnp.pad(flat, (0, rows * LANES - flat.shape[0])).reshape(rows, LANES)


def _unpack(packed, shapes):
    flat = packed.reshape(-1)
    out, off = [], 0
    for s in shapes:
        n = math.prod(s)
        out.append(flat[off:off + n].reshape(s))
        off += n
    return out


def _packed_rows(shapes):
    n = sum(math.prod(s) for s in shapes)
    unit = N_DEV * 8 * LANES
    return -(-n // unit) * unit // LANES


def kernel(x, mix_pre_g, mix_post_g, mlp_pre_g, mlp_post_g, w_in_even, s5_lam_re, s5_lam_im, s5_log_dt, s5_b_re, s5_b_im, s5_c_re, s5_c_im, s5_d, s5_w_glu, fox_b_f, w_out_even, w_in_odd, pool_w, pool_scale, sgu_ln_g, sgu_ln_b, sgu_w_s, sgu_b_s, w_out_odd, mlp_w1, mlp_w2, loss_target, m_mix_pre_g, m_mix_post_g, m_mlp_pre_g, m_mlp_post_g, m_w_in_even, m_s5_lam_re, m_s5_lam_im, m_s5_log_dt, m_s5_b_re, m_s5_b_im, m_s5_c_re, m_s5_c_im, m_s5_d, m_s5_w_glu, m_fox_b_f, m_w_out_even, m_w_in_odd, m_pool_w, m_pool_scale, m_sgu_ln_g, m_sgu_ln_b, m_sgu_w_s, m_sgu_b_s, m_w_out_odd, m_mlp_w1, m_mlp_w2, v_mix_pre_g, v_mix_post_g, v_mlp_pre_g, v_mlp_post_g, v_w_in_even, v_s5_lam_re, v_s5_lam_im, v_s5_log_dt, v_s5_b_re, v_s5_b_im, v_s5_c_re, v_s5_c_im, v_s5_d, v_s5_w_glu, v_fox_b_f, v_w_out_even, v_w_in_odd, v_pool_w, v_pool_scale, v_sgu_ln_g, v_sgu_ln_b, v_sgu_w_s, v_sgu_b_s, v_w_out_odd, v_mlp_w1, v_mlp_w2):
    weights = dict(mix_pre_g=mix_pre_g, mix_post_g=mix_post_g, mlp_pre_g=mlp_pre_g, mlp_post_g=mlp_post_g, w_in_even=w_in_even, s5_lam_re=s5_lam_re, s5_lam_im=s5_lam_im, s5_log_dt=s5_log_dt, s5_b_re=s5_b_re, s5_b_im=s5_b_im, s5_c_re=s5_c_re, s5_c_im=s5_c_im, s5_d=s5_d, s5_w_glu=s5_w_glu, fox_b_f=fox_b_f, w_out_even=w_out_even, w_in_odd=w_in_odd, pool_w=pool_w, pool_scale=pool_scale, sgu_ln_g=sgu_ln_g, sgu_ln_b=sgu_ln_b, sgu_w_s=sgu_w_s, sgu_b_s=sgu_b_s, w_out_odd=w_out_odd, mlp_w1=mlp_w1, mlp_w2=mlp_w2)
    mom_m = dict(mix_pre_g=m_mix_pre_g, mix_post_g=m_mix_post_g, mlp_pre_g=m_mlp_pre_g, mlp_post_g=m_mlp_post_g, w_in_even=m_w_in_even, s5_lam_re=m_s5_lam_re, s5_lam_im=m_s5_lam_im, s5_log_dt=m_s5_log_dt, s5_b_re=m_s5_b_re, s5_b_im=m_s5_b_im, s5_c_re=m_s5_c_re, s5_c_im=m_s5_c_im, s5_d=m_s5_d, s5_w_glu=m_s5_w_glu, fox_b_f=m_fox_b_f, w_out_even=m_w_out_even, w_in_odd=m_w_in_odd, pool_w=m_pool_w, pool_scale=m_pool_scale, sgu_ln_g=m_sgu_ln_g, sgu_ln_b=m_sgu_ln_b, sgu_w_s=m_sgu_w_s, sgu_b_s=m_sgu_b_s, w_out_odd=m_w_out_odd, mlp_w1=m_mlp_w1, mlp_w2=m_mlp_w2)
    mom_v = dict(mix_pre_g=v_mix_pre_g, mix_post_g=v_mix_post_g, mlp_pre_g=v_mlp_pre_g, mlp_post_g=v_mlp_post_g, w_in_even=v_w_in_even, s5_lam_re=v_s5_lam_re, s5_lam_im=v_s5_lam_im, s5_log_dt=v_s5_log_dt, s5_b_re=v_s5_b_re, s5_b_im=v_s5_b_im, s5_c_re=v_s5_c_re, s5_c_im=v_s5_c_im, s5_d=v_s5_d, s5_w_glu=v_s5_w_glu, fox_b_f=v_fox_b_f, w_out_even=v_w_out_even, w_in_odd=v_w_in_odd, pool_w=v_pool_w, pool_scale=v_pool_scale, sgu_ln_g=v_sgu_ln_g, sgu_ln_b=v_sgu_ln_b, sgu_w_s=v_sgu_w_s, sgu_b_s=v_sgu_b_s, w_out_odd=v_w_out_odd, mlp_w1=v_mlp_w1, mlp_w2=v_mlp_w2)
    names = list(weights)
    L = x.shape[1]
    x0 = x[0]
    target = loss_target[0]
    my_index = 4 * lax.axis_index("x") + 2 * lax.axis_index("y") + lax.axis_index("c")

    small_vec = jnp.zeros((8, LANES), F32)
    small_vec = small_vec.at[0, :64].set(pool_scale[0]).at[1, :64].set(sgu_ln_g[0]).at[2, :64].set(sgu_ln_b[0])
    ag_groups, ag_token = _comm_start(
        [[jnp.transpose(w_in_even[0]).astype(BF16), small_vec],
         [s5_w_glu[0].astype(BF16), w_out_even[0].astype(BF16)],
         [mlp_w1[0].astype(BF16), mlp_w2[0].astype(BF16)],
         [jnp.transpose(w_in_odd[0]).astype(BF16), w_out_odd[0].astype(BF16), mlp_w1[1].astype(BF16), mlp_w2[1].astype(BF16)]],
        "ag_start", exchange=False)

    lam_r = jnp.concatenate([s5_lam_re.reshape(1, S5_NS), s5_lam_im.reshape(1, S5_NS)], axis=0)
    ldt_r = jnp.repeat(s5_log_dt.reshape(32), 64).reshape(1, S5_NS)
    lam_c = jnp.transpose(lam_r)
    ldt_c = jnp.transpose(ldt_r)
    b_t = jnp.stack([jnp.tile(s5_b_re.reshape(S5_NS, 16), (1, 8)), jnp.tile(s5_b_im.reshape(S5_NS, 16), (1, 8))])
    c_t = jnp.stack([jnp.tile(s5_c_re.reshape(S5_W, 64), (1, 8)), jnp.tile(s5_c_im.reshape(S5_W, 64), (1, 8))])
    bf_pad = jnp.pad(fox_b_f, ((0, 0), (0, LANES - 8)))
    b_st = jnp.transpose(sgu_b_s[0])

    h0, rx0 = _rms_fwd(x0, _tie(mix_pre_g[0:1], ag_token), "rms0")
    tabs, bset, cset = _s5_prep(lam_r, ldt_r, lam_c, ldt_c, b_t, c_t, "s5_prep")
    ag0 = _comm_wait(ag_groups[0], tabs, "ag_wait0", exchange=False)
    winT_e = jnp.pad(ag0[0].reshape(EVEN_IN, D_MODEL), ((0, EVEN_PAD - EVEN_IN), (0, 0)))
    pool_scale_f = ag0[1][:, 0, :64].reshape(1, 512)
    ln_g_f = ag0[1][:, 1, :64].reshape(1, 512)
    ln_b_f = ag0[1][:, 2, :64].reshape(1, 512)
    z0 = _mm(h0, winT_e, name="win_even", tb=True, bm=512, bn=EVEN_PAD)
    xs = _s5_scan_fwd(z0, bset, tabs, "s5_scan")
    ag1 = _comm_wait(ag_groups[1], xs, "ag_wait1", exchange=False)
    wglu = ag1[0].reshape(S5_W, S5_W)
    wout_e = ag1[1].reshape(D_MODEL, D_MODEL)
    ylin, ya = _s5_out_fwd(xs, cset, z0, s5_d, wglu, "s5_out")
    fcum, fq = _fox_f_fwd(z0, bf_pad, "fox_f")
    frow = jnp.transpose(fcum[:, :8]).reshape(4, 2, L)
    o_att, lse = _fox_fwd(z0, fq, frow, "fox_fwd")
    mix0 = [ya, o_att]
    y0 = _mm(mix0, wout_e, name="wout_even")
    x1, ry0, h1, rx1 = _post_pre_fwd(x0, y0, mix_post_g[0:1], mlp_pre_g[0:1], "post0")
    ag2 = _comm_wait(ag_groups[2], rx1, "ag_wait2", exchange=False)
    w1 = [ag2[0], None]
    w2 = [ag2[1].reshape(4 * D_MODEL, D_MODEL), None]
    p0, a0 = _mm(h1, w1[0], name="mlp0_w1", b3=True, out_dtypes=(BF16, BF16), epi=_epi_relu2)
    o0 = _mm(a0, w2[0], name="mlp0_w2")
    x2, ro0, h2, rx2 = _post_pre_fwd(x1, o0, mlp_post_g[0:1], mix_pre_g[1:2], "post1")
    ag3 = _comm_wait(ag_groups[3], rx2, "ag_wait3", exchange=False)
    winT_o = ag3[0].reshape(ODD_IN, D_MODEL)
    wout_o = ag3[1].reshape(D_MODEL, D_MODEL)
    w1[1] = ag3[2]
    w2[1] = ag3[3].reshape(4 * D_MODEL, D_MODEL)
    z1 = _mm(h2, winT_o, name="win_odd", tb=True, bn=ODD_IN)
    yc, pooled = _pool_fwd(z1, pool_w[0], pool_scale_f, "pool_fwd")
    yd = _sgu_fwd(z1, ln_g_f, ln_b_f, sgu_w_s[0], b_st, "sgu_fwd")
    mix1 = [yc, yd]
    y1 = _mm(mix1, wout_o, name="wout_odd")
    x3, ry1, h3, rx3 = _post_pre_fwd(x2, y1, mix_post_g[1:2], mlp_pre_g[1:2], "post2")
    p1, a1 = _mm(h3, w1[1], name="mlp1_w1", b3=True, out_dtypes=(BF16, BF16), epi=_epi_relu2, bm=2048)
    o1 = _mm(a1, w2[1], name="mlp1_w2", bm=2048)
    gx4, ro1, sq = _post_loss_fwd(x3, o1, mlp_post_g[1:2], target, "post3")

    g_o1, gg_mlp_post1 = _post_bwd(gx4, o1, ro1, mlp_post_g[1:2], "bpost3")
    g_p1 = _mm(g_o1, w2[1], name="b_mlp1_a", tb=True, out_dtypes=(BF16,), epi=_epi_relu2_bwd, extra=(p1,), bm=2048)
    gw2_1 = _mm(a1, g_o1, name="b_mlp1_w2", ta=True, bm=2048)
    g_h3 = _mm(g_p1, w1[1], name="b_mlp1_h", tb=True, b3=True, bm=2048)
    gw1_1 = _mm(h3, g_p1, name="b_mlp1_w1", ta=True, out3=True, bn=2048)
    (ex1,), tok1 = _comm_start([[gw1_1, gw2_1.reshape(N_DEV, 512, D_MODEL)]], "ex_start1", exchange=True)
    g_x3, gg_mlp_pre1, g_y1, gg_mix_post1 = _pre_post_bwd(g_h3, x3, rx3, _tie(mlp_pre_g[1:2], tok1), gx4, y1, ry1, mix_post_g[1:2], "bpre3")
    g_mix1 = _mm(g_y1, wout_o, name="b_wout_odd_m", tb=True)
    gwout_o = _mm(mix1, g_y1, name="b_wout_odd_w", ta=True)
    g_xc, g_pool_w, g_pool_scale = _pool_bwd(g_mix1, pooled, pool_w[0], pool_scale_f, "pool_bwd")
    g_u1, g_v1, g_ws, g_bst, g_ln_g, g_ln_b = _sgu_bwd(g_mix1, z1, ln_g_f, ln_b_f, sgu_w_s[0], b_st, "sgu_bwd")
    g_z1 = [g_xc, g_u1, g_v1]
    g_h2 = _mm(g_z1, winT_o, name="b_win_odd_h")
    gwinT_o = _mm(g_z1, h2, name="b_win_odd_w", ta=True)
    (ex2,), tok2 = _comm_start([[gwout_o.reshape(N_DEV, 128, D_MODEL), gwinT_o.reshape(N_DEV, ODD_IN // N_DEV, D_MODEL)]], "ex_start2", exchange=True)
    g_x2, gg_mix_pre1, g_o0, gg_mlp_post0 = _pre_post_bwd(g_h2, x2, rx2, _tie(mix_pre_g[1:2], tok2), g_x3, o0, ro0, mlp_post_g[0:1], "bpre2")
    g_p0 = _mm(g_o0, w2[0], name="b_mlp0_a", tb=True, out_dtypes=(BF16,), epi=_epi_relu2_bwd, extra=(p0,))
    gw2_0 = _mm(a0, g_o0, name="b_mlp0_w2", ta=True)
    g_h1 = _mm(g_p0, w1[0], name="b_mlp0_h", tb=True, b3=True)
    gw1_0 = _mm(h1, g_p0, name="b_mlp0_w1", ta=True, out3=True)
    (ex3,), tok3 = _comm_start([[gw1_0, gw2_0.reshape(N_DEV, 512, D_MODEL)]], "ex_start3", exchange=True)
    g_x1, gg_mlp_pre0, g_y0, gg_mix_post0 = _pre_post_bwd(g_h1, x1, rx1, _tie(mlp_pre_g[0:1], tok3), g_x2, y0, ry0, mix_post_g[0:1], "bpre1")
    g_mix0 = _mm(g_y0, wout_e, name="b_wout_even_m", tb=True)
    gwout_e = _mm(mix0, g_y0, name="b_wout_even_w", ta=True)
    gyl, gud, g_wglu, g_d = _s5_glu_bwd(g_mix0, ylin, z0, s5_d, wglu, "s5_glu_bwd")
    (ex4,), tok4 = _comm_start([[gwout_e.reshape(N_DEV, 128, D_MODEL), g_wglu.reshape(N_DEV, 64, S5_W)]], "ex_start4", exchange=True)
    g_u0, ga, gb_raw, gc_raw = _s5_scan_bwd(gyl, _tie(cset, tok4), xs, z0, bset, gud, tabs, "s5_scan_bwd")
    g_lam, g_ldt, g_b, g_c = _s5_param_bwd(lam_c, ldt_c, b_t, gb_raw, jnp.transpose(ga), gc_raw, "s5_param_bwd")
    dq, dk, dv, dfq, dfrow = _fox_bwd(z0, fq, frow, o_att, lse, g_mix0, "fox_bwd")
    dFk = jnp.pad(jnp.transpose(dfrow.reshape(8, L)), ((0, 0), (0, LANES - 8)))
    dfl, db_f = _fox_f_bwd(dFk, dfq, z0, bf_pad, "fox_f_bwd")
    g_z0 = [g_u0, dq, dk, dv, dfl]
    g_h0 = _mm(g_z0, winT_e, name="b_win_even_h")
    grad_x, gg_mix_pre0 = _pre_bwd(g_h0, x0, rx0, mix_pre_g[0:1], g_x1, "bpre0")
    gwinT_e = _mm(g_z0, h0, name="b_win_even_w", ta=True, bk=512)

    small_grads = dict(
        mix_pre_g=jnp.concatenate([gg_mix_pre0, gg_mix_pre1]), mix_post_g=jnp.concatenate([gg_mix_post0, gg_mix_post1]),
        mlp_pre_g=jnp.concatenate([gg_mlp_pre0, gg_mlp_pre1]), mlp_post_g=jnp.concatenate([gg_mlp_post0, gg_mlp_post1]),
        s5_lam_re=g_lam[:, 0], s5_lam_im=g_lam[:, 1], s5_log_dt=g_ldt,
        s5_b_re=g_b[0, :, :16], s5_b_im=g_b[1, :, :16], s5_c_re=g_c[0, :, :64], s5_c_im=g_c[1, :, :64],
        s5_d=g_d, fox_b_f=db_f[:, :8], pool_w=g_pool_w, sgu_w_s=g_ws, sgu_b_s=jnp.transpose(g_bst),
        pool_scale=g_pool_scale, sgu_ln_g=g_ln_g, sgu_ln_b=g_ln_b)
    small_names = list(small_grads)
    full_shapes = [(512,) if nm in ("pool_scale", "sgu_ln_g", "sgu_ln_b") else weights[nm].shape for nm in small_names]
    full_shapes.append((1, 1))
    rows = _packed_rows(full_shapes)
    packed = _pack([small_grads[nm] for nm in small_names] + [sq], rows).reshape(N_DEV, rows // N_DEV, LANES)
    (recv_small,) = _exchange([packed], "exchange_small")
    piece = _sum_pieces(recv_small, "sum_small")
    (small_all,) = _all_gather([piece], "ag_small")
    small_full = _unpack(small_all.reshape(rows, LANES), full_shapes)
    loss = 0.5 * small_full.pop()[0, 0] / D_MODEL

    gwinT_e_pieces = gwinT_e[:EVEN_IN].reshape(N_DEV, EVEN_IN // N_DEV, D_MODEL).astype(BF16)
    (ex5,), tok5 = _comm_start([[gwinT_e_pieces]], "ex_start5", exchange=True, dep=small_all)
    r_w1_1, r_w2_1 = _comm_wait(ex1, tok5, "ex_wait1", exchange=True)
    r_wout_o, r_win_o = _comm_wait(ex2, tok5, "ex_wait2", exchange=True)
    r_w1_0, r_w2_0 = _comm_wait(ex3, tok5, "ex_wait3", exchange=True)
    r_wout_e, r_wglu = _comm_wait(ex4, tok5, "ex_wait4", exchange=True)
    small_g = {}
    for nm, g in zip(small_names, small_full):
        if nm in ("pool_scale", "sgu_ln_g", "sgu_ln_b"):
            g = lax.dynamic_slice(g, (my_index * 64,), (64,)).reshape(1, 64)
        small_g[nm] = g
    own_shapes = [weights[nm].shape for nm in small_names]
    rows2 = _packed_rows(own_shapes)
    pw = _pack([weights[nm] for nm in small_names], rows2)
    pg = _pack([small_g[nm] for nm in small_names], rows2)
    pm = _pack([mom_m[nm] for nm in small_names], rows2)
    pv = _pack([mom_v[nm] for nm in small_names], rows2)
    pd, pnm, pnv = _adamw(pw, pg, pm, pv, "adamw_small")
    res = {}
    for nm, d_, m_, v_ in zip(small_names, _unpack(pd, own_shapes), _unpack(pnm, own_shapes), _unpack(pnv, own_shapes)):
        res[nm] = (small_g[nm], d_, m_, v_)

    for nm, parts in (("mlp_w1", (r_w1_0, r_w1_1)), ("mlp_w2", (r_w2_0, r_w2_1))):
        first = _sum_adamw(parts[0], weights[nm], mom_m[nm], mom_v[nm], "adamw_%s_0" % nm, layer=0)
        res[nm] = tuple(_sum_adamw(parts[1], weights[nm], mom_m[nm], mom_v[nm], "adamw_%s_1" % nm, layer=1, prev=first))
    big_parts = dict(s5_w_glu=r_wglu, w_out_even=r_wout_e, w_out_odd=r_wout_o)
    for nm, parts in big_parts.items():
        res[nm] = tuple(_sum_adamw(parts, weights[nm], mom_m[nm], mom_v[nm], "adamw_" + nm))
    done = [res[nm][1] for nm in ("mlp_w1", "mlp_w2", "s5_w_glu", "w_out_even", "w_out_odd")]
    for nm, parts in (("w_in_odd", r_win_o), ("w_in_even", None)):
        if parts is None:
            (parts,) = _comm_wait(ex5, done, "ex_wait5", exchange=True)
        outs = _sum_adamw(parts, jnp.transpose(weights[nm], (0, 2, 1)), jnp.transpose(mom_m[nm], (0, 2, 1)),
                          jnp.transpose(mom_v[nm], (0, 2, 1)), "adamw_" + nm)
        res[nm] = tuple(jnp.transpose(o, (0, 2, 1)) for o in outs)
        done.append(res[nm][1])

    grads = [res[nm][0].reshape(weights[nm].shape) for nm in names]
    deltas = [res[nm][1].reshape(weights[nm].shape) for nm in names]
    new_m = [res[nm][2].reshape(weights[nm].shape) for nm in names]
    new_v = [res[nm][3].reshape(weights[nm].shape) for nm in names]
    return (loss, grad_x[None], *grads, *deltas, *new_m, *new_v)
```

```python
import functools
import math

import jax
import jax.numpy as jnp
from jax import lax
from jax.experimental import pallas as pl
from jax.experimental.pallas import tpu as pltpu

F32 = jnp.float32
BF16 = jnp.bfloat16
MESH = pl.DeviceIdType.MESH
ANY = pl.BlockSpec(memory_space=pl.ANY)

N_DEV = 8
D_MODEL = 1024
EPS = 1e-6
NORM_ROWS = 512
S5_W = 512
S5_NS = 2048
SCAN_GROUPS = 4
SCAN_CHUNK = 1024
FOX_W = 512
EVEN_IN = 2056
EVEN_PAD = 2176
ODD_IN = 1536
LANES = 128
PIECE = 4 * D_MODEL // N_DEV
VMEM_LIMIT = 56 * 1024 * 1024

ADAM_LR = 0.001
ADAM_B1 = 0.9
ADAM_B2 = 0.999
ADAM_EPS = 1e-08
ADAM_WD = 0.01
ADAM_STEP = 10

NT = (((1,), (1,)), ((), ()))
TN = (((0,), (0,)), ((), ()))
NN = (((1,), (0,)), ((), ()))


def _cp(*sem):
    return pltpu.CompilerParams(dimension_semantics=sem, vmem_limit_bytes=VMEM_LIMIT)


def _sds(shape, dtype=F32):
    return jax.ShapeDtypeStruct(tuple(shape), dtype)


def _gelu(x):
    t = jnp.tanh(0.7978845608028654 * (x + 0.044715 * x * x * x))
    return 0.5 * x * (1.0 + t)


def _gelu_grad(x):
    t = jnp.tanh(0.7978845608028654 * (x + 0.044715 * x * x * x))
    du = 0.7978845608028654 * (1.0 + 3.0 * 0.044715 * x * x)
    return 0.5 * (1.0 + t) + 0.5 * x * (1.0 - t * t) * du


def _sigmoid(x):
    return 1.0 / (1.0 + jnp.exp(-x))


def _dot(a, b, dn=NN):
    return lax.dot_general(a, b, dn, preferred_element_type=F32)


def _mm(a, b, *, name, ta=False, tb=False, b3=False, out3=False, out_dtypes=(F32,), epi=None, extra=(),
        bm=1024, bn=1024, bk=1024):
    a_list = list(a) if isinstance(a, (list, tuple)) else [a]
    widths = [p.shape[1] for p in a_list]
    offs = [sum(widths[:i]) for i in range(len(widths))]
    na = len(a_list)
    M = sum(widths) if ta else a_list[0].shape[0]
    K = a_list[0].shape[0] if ta else sum(widths)
    if na > 1:
        assert not b3 and not tb
        bm, bk = (M, bk) if ta else (bm, K)
    pw = b.shape[2] if b3 else PIECE
    if b3:
        N = b.shape[1] if tb else b.shape[0] * pw
        assert (b.shape[0] * pw if tb else b.shape[1]) == K
    else:
        N = b.shape[0] if tb else b.shape[1]
    bm, bn, bk = min(bm, M), min(bn, N), min(bk, K)
    assert M % bm == 0 and N % bn == 0 and K % bk == 0, (name, M, N, K, bm, bn, bk)
    assert not (b3 or out3) or ((bk if tb else bn) % pw == 0 and bn % PIECE == 0)
    nk = K // bk
    n_extra = len(extra)
    n_out = len(out_dtypes)
    dn = (((0 if ta else 1,), (1 if tb else 0,)), ((), ()))

    use_acc = nk > 1

    def body(*refs):
        a_refs, b_ref = refs[:na], refs[na]
        a_ref = a_refs[0]
        e_refs = refs[na + 1:na + 1 + n_extra]
        o_refs = refs[na + 1 + n_extra:na + 1 + n_extra + n_out]
        acc_ref = refs[-1] if use_acc else o_refs[0]
        k = pl.program_id(2)

        def dot(a_v, b_v):
            return lax.dot_general(a_v.astype(BF16), b_v.astype(BF16), dn, preferred_element_type=F32)

        everything = slice(None)
        if na > 1 and ta:
            terms = [(pl.ds(off, w), everything, r, b_ref) for r, off, w in zip(a_refs, offs, widths)]
        elif na > 1:
            terms = [(everything, everything, r, b_ref.at[pl.ds(off, w), :]) for r, off, w in zip(a_refs, offs, widths)]
        elif not b3:
            terms = [(everything, everything, a_ref, b_ref)]
        elif tb:
            terms = [(everything, everything,
                      a_ref.at[pl.ds(t * pw, pw), :] if ta else a_ref.at[:, pl.ds(t * pw, pw)], b_ref.at[t])
                     for t in range(bk // pw)]
        else:
            terms = [(everything, pl.ds(t * pw, pw), a_ref, b_ref.at[t]) for t in range(bn // pw)]

        def finish(acc):
            outs = (acc,) if epi is None else epi(acc, *[e[...] for e in e_refs])
            for o_ref, o in zip(o_refs, outs):
                if out3:
                    for t in range(bn // PIECE):
                        o_ref[t] = o[:, t * PIECE:(t + 1) * PIECE].astype(o_ref.dtype)
                else:
                    o_ref[...] = o.astype(o_ref.dtype)

        if nk == 1:
            bands = {}
            for rows, cols, a_r, b_r in terms:
                key = (getattr(rows, "start", None), getattr(cols, "start", None))
                val = dot(a_r[...], b_r[...])
                bands[key] = val if key not in bands else bands[key] + val
            vals = list(bands.values())
            if len(vals) == 1:
                finish(vals[0])
            else:
                finish(jnp.concatenate(vals, axis=0 if (na > 1 and ta) else 1))
            return

        @pl.when(k == 0)
        def _():
            acc_ref[...] = jnp.zeros_like(acc_ref)

        for rows, cols, a_r, b_r in terms:
            acc_ref[rows, cols] += dot(a_r[...], b_r[...])

        @pl.when(k == nk - 1)
        def _():
            finish(acc_ref[...])

    if na > 1:
        a_specs = [pl.BlockSpec((bk, w), lambda i, j, k: (k, 0)) if ta else pl.BlockSpec((bm, w), lambda i, j, k: (i, 0))
                   for w in widths]
    else:
        a_specs = [pl.BlockSpec((bk, bm), lambda i, j, k: (k, i)) if ta else
                   pl.BlockSpec((bm, bk), lambda i, j, k: (i, k))]
    if b3:
        if tb:
            b_spec = pl.BlockSpec((bk // pw, bn, pw), lambda i, j, k: (k, j, 0))
        else:
            b_spec = pl.BlockSpec((bn // pw, bk, pw), lambda i, j, k: (j, k, 0))
    else:
        b_spec = pl.BlockSpec((bn, bk), lambda i, j, k: (j, k)) if tb else pl.BlockSpec((bk, bn), lambda i, j, k: (k, j))
    e_specs = [pl.BlockSpec((bm, bn), lambda i, j, k: (i, j)) for _ in extra]
    if out3:
        o_specs = [pl.BlockSpec((bn // PIECE, bm, PIECE), lambda i, j, k: (j, i, 0)) for _ in out_dtypes]
        o_shapes = [_sds((N // PIECE, M, PIECE), dt) for dt in out_dtypes]
    else:
        o_specs = [pl.BlockSpec((bm, bn), lambda i, j, k: (i, j)) for _ in out_dtypes]
        o_shapes = [_sds((M, N), dt) for dt in out_dtypes]
    outs = pl.pallas_call(
        body, name=name, grid=(M // bm, N // bn, nk),
        in_specs=a_specs + [b_spec] + e_specs, out_specs=o_specs, out_shape=o_shapes,
        scratch_shapes=[pltpu.VMEM((bm, bn), F32)] if use_acc else [],
        compiler_params=_cp("parallel", "parallel", "arbitrary"),
    )(*a_list, b, *extra)
    return outs[0] if n_out == 1 else outs


def _epi_relu2(acc):
    r = jnp.maximum(acc, 0.0)
    return acc, r * r


def _epi_relu2_bwd(acc, p):
    return (acc * (2.0 * jnp.maximum(p.astype(F32), 0.0)),)


def _row_spec(rb, w=D_MODEL):
    return pl.BlockSpec((rb, w), lambda i: (i, 0))


def _vec_spec(w=D_MODEL):
    return pl.BlockSpec((1, w), lambda i: (0, 0))


def _rstd(v):
    return lax.rsqrt(jnp.mean(v * v, axis=-1, keepdims=True) + EPS)


def _rms_fwd(x, g, name):
    L = x.shape[0]
    rb = min(NORM_ROWS, L)

    def body(x_ref, g_ref, h_ref, r_ref):
        xv = x_ref[...]
        r = _rstd(xv)
        h_ref[...] = (xv * r * g_ref[...]).astype(BF16)
        r_ref[...] = r

    return pl.pallas_call(
        body, name=name, grid=(L // rb,),
        in_specs=[_row_spec(rb), _vec_spec()],
        out_specs=[_row_spec(rb), _row_spec(rb, 1)],
        out_shape=[_sds((L, D_MODEL), BF16), _sds((L, 1))],
        compiler_params=_cp("parallel"),
    )(x, g)


def _post_pre_fwd(x_in, y, g_post, g_pre, name):
    L = x_in.shape[0]
    rb = min(NORM_ROWS, L)

    def body(x_ref, y_ref, gp_ref, gn_ref, xo_ref, ry_ref, h_ref, rx_ref):
        yv = y_ref[...]
        ry = _rstd(yv)
        xo = x_ref[...] + yv * ry * gp_ref[...]
        rx = _rstd(xo)
        xo_ref[...] = xo
        ry_ref[...] = ry
        h_ref[...] = (xo * rx * gn_ref[...]).astype(BF16)
        rx_ref[...] = rx

    return pl.pallas_call(
        body, name=name, grid=(L // rb,),
        in_specs=[_row_spec(rb), _row_spec(rb), _vec_spec(), _vec_spec()],
        out_specs=[_row_spec(rb), _row_spec(rb, 1), _row_spec(rb), _row_spec(rb, 1)],
        out_shape=[_sds((L, D_MODEL)), _sds((L, 1)), _sds((L, D_MODEL), BF16), _sds((L, 1))],
        compiler_params=_cp("parallel"),
    )(x_in, y, g_post, g_pre)


def _post_loss_fwd(x_in, y, g_post, target, name):
    L = x_in.shape[0]
    rb = min(NORM_ROWS, L)

    def body(x_ref, y_ref, gp_ref, t_ref, gx_ref, ry_ref, loss_ref):
        i = pl.program_id(0)
        yv = y_ref[...]
        ry = _rstd(yv)
        diff = x_ref[...] + yv * ry * gp_ref[...] - t_ref[...]
        gx_ref[...] = diff * (1.0 / D_MODEL)
        ry_ref[...] = ry

        @pl.when(i == 0)
        def _():
            loss_ref[...] = jnp.zeros_like(loss_ref)

        loss_ref[...] += jnp.sum(diff * diff, keepdims=True)

    return pl.pallas_call(
        body, name=name, grid=(L // rb,),
        in_specs=[_row_spec(rb), _row_spec(rb), _vec_spec(), _row_spec(rb)],
        out_specs=[_row_spec(rb), _row_spec(rb, 1), pl.BlockSpec((1, 1), lambda i: (0, 0))],
        out_shape=[_sds((L, D_MODEL)), _sds((L, 1)), _sds((1, 1))],
        compiler_params=_cp("arbitrary"),
    )(x_in, y, g_post, target)


def _rms_bwd_rows(dy, xv, r, g):
    n = xv * r
    dyg = dy * g
    return r * (dyg - n * jnp.mean(dyg * n, axis=-1, keepdims=True)), n


def _post_bwd(g_out, y, ry, g_post, name):
    L = y.shape[0]
    rb = min(NORM_ROWS, L)

    def body(go_ref, y_ref, ry_ref, gp_ref, gy_ref, gg_ref):
        i = pl.program_id(0)
        go = go_ref[...]
        gy, n = _rms_bwd_rows(go, y_ref[...], ry_ref[...], gp_ref[...])
        gy_ref[...] = gy.astype(BF16)

        @pl.when(i == 0)
        def _():
            gg_ref[...] = jnp.zeros_like(gg_ref)

        gg_ref[...] += jnp.sum(go * n, axis=0, keepdims=True)

    return pl.pallas_call(
        body, name=name, grid=(L // rb,),
        in_specs=[_row_spec(rb), _row_spec(rb), _row_spec(rb, 1), _vec_spec()],
        out_specs=[_row_spec(rb), _vec_spec()],
        out_shape=[_sds((L, D_MODEL), BF16), _sds((1, D_MODEL))],
        compiler_params=_cp("arbitrary"),
    )(g_out, y, ry, g_post)


def _pre_post_bwd(g_h, x, rx, g_pre, g_out, y_prev, ry_prev, g_post_prev, name):
    L = x.shape[0]
    rb = min(NORM_ROWS, L)

    def body(gh_ref, x_ref, rx_ref, gn_ref, go_ref, y_ref, ry_ref, gp_ref, gi_ref, ggn_ref, gy_ref, ggp_ref):
        i = pl.program_id(0)
        gh = gh_ref[...]
        gx, n = _rms_bwd_rows(gh, x_ref[...], rx_ref[...], gn_ref[...])
        gi = go_ref[...] + gx
        gi_ref[...] = gi
        gy, ny = _rms_bwd_rows(gi, y_ref[...], ry_ref[...], gp_ref[...])
        gy_ref[...] = gy.astype(BF16)

        @pl.when(i == 0)
        def _():
            ggn_ref[...] = jnp.zeros_like(ggn_ref)
            ggp_ref[...] = jnp.zeros_like(ggp_ref)

        ggn_ref[...] += jnp.sum(gh * n, axis=0, keepdims=True)
        ggp_ref[...] += jnp.sum(gi * ny, axis=0, keepdims=True)

    return pl.pallas_call(
        body, name=name, grid=(L // rb,),
        in_specs=[_row_spec(rb), _row_spec(rb), _row_spec(rb, 1), _vec_spec(), _row_spec(rb),
                  _row_spec(rb), _row_spec(rb, 1), _vec_spec()],
        out_specs=[_row_spec(rb), _vec_spec(), _row_spec(rb), _vec_spec()],
        out_shape=[_sds((L, D_MODEL)), _sds((1, D_MODEL)), _sds((L, D_MODEL), BF16), _sds((1, D_MODEL))],
        compiler_params=_cp("arbitrary"),
    )(g_h, x, rx, g_pre, g_out, y_prev, ry_prev, g_post_prev)


def _pre_bwd(g_h, x, rx, g_pre, g_out, name):
    L = x.shape[0]
    rb = min(NORM_ROWS, L)

    def body(gh_ref, x_ref, rx_ref, gn_ref, go_ref, gi_ref, ggn_ref):
        i = pl.program_id(0)
        gh = gh_ref[...]
        gx, n = _rms_bwd_rows(gh, x_ref[...], rx_ref[...], gn_ref[...])
        gi_ref[...] = go_ref[...] + gx

        @pl.when(i == 0)
        def _():
            ggn_ref[...] = jnp.zeros_like(ggn_ref)

        ggn_ref[...] += jnp.sum(gh * n, axis=0, keepdims=True)

    return pl.pallas_call(
        body, name=name, grid=(L // rb,),
        in_specs=[_row_spec(rb), _row_spec(rb), _row_spec(rb, 1), _vec_spec(), _row_spec(rb)],
        out_specs=[_row_spec(rb), _vec_spec()],
        out_shape=[_sds((L, D_MODEL)), _sds((1, D_MODEL))],
        compiler_params=_cp("arbitrary"),
    )(g_h, x, rx, g_pre, g_out)


def _cmul(ar, ai, br, bi):
    return ar * br - ai * bi, ar * bi + ai * br


def _zoh_cols(lr, li, ldt):
    dt = jnp.exp(ldt)
    mag = jnp.exp(lr * dt)
    ar = mag * jnp.cos(li * dt)
    ai = mag * jnp.sin(li * dt)
    den = lr * lr + li * li
    nr = ar - 1.0
    qr = (nr * lr + ai * li) / den
    qi = (ai * lr - nr * li) / den
    return dt, ar, ai, qr, qi, den


def _b_mask():
    r = lax.broadcasted_iota(jnp.int32, (S5_NS, LANES), 0)
    c = lax.broadcasted_iota(jnp.int32, (S5_NS, LANES), 1)
    return ((r >> 6) & 7) == (c >> 4)


def _c_mask():
    r = lax.broadcasted_iota(jnp.int32, (S5_W, 512), 0)
    c = lax.broadcasted_iota(jnp.int32, (S5_W, 512), 1)
    return ((r >> 4) & 7) == (c >> 6)


def _s5_prep(lam_r, ldt_r, lam_c, ldt_c, b_t, c_t, name):
    def body(lam_r_ref, ldt_r_ref, lam_c_ref, ldt_c_ref, b_ref, c_ref, tab_ref, bset_ref, cset_ref):
        lr, li = lam_r_ref[0:1, :], lam_r_ref[1:2, :]
        dt = jnp.exp(ldt_r_ref[...])
        mag = jnp.exp(lr * dt)
        p1r, p1i = mag * jnp.cos(li * dt), mag * jnp.sin(li * dt)
        p2r, p2i = _cmul(p1r, p1i, p1r, p1i)
        p3r, p3i = _cmul(p2r, p2i, p1r, p1i)
        p4r, p4i = _cmul(p2r, p2i, p2r, p2i)
        p5r, p5i = _cmul(p4r, p4i, p1r, p1i)
        p6r, p6i = _cmul(p4r, p4i, p2r, p2i)
        p7r, p7i = _cmul(p4r, p4i, p3r, p3i)
        p8r, p8i = _cmul(p4r, p4i, p4r, p4i)
        pw_r = [p1r, p2r, p3r, p4r, p5r, p6r, p7r, p8r]
        pw_i = [p1i, p2i, p3i, p4i, p5i, p6i, p7i, p8i]
        row = lax.broadcasted_iota(jnp.int32, (8, S5_NS), 0)
        zero = jnp.zeros((8, S5_NS), F32)

        def bc(v):
            return jnp.broadcast_to(v, (8, S5_NS))

        for d in range(2):
            sgn = 1.0 if d == 0 else -1.0
            for t, s in enumerate((1, 2, 4)):
                live = (row >= s) if d == 0 else (row <= 7 - s)
                tab_ref[d, 2 * t] = jnp.where(live, bc(pw_r[s - 1]), zero)
                tab_ref[d, 2 * t + 1] = jnp.where(live, bc(sgn * pw_i[s - 1]), zero)
            cr, ci = zero, zero
            for i in range(8):
                e = i if d == 0 else 7 - i
                cr = jnp.where(row == i, bc(pw_r[e]), cr)
                ci = jnp.where(row == i, bc(sgn * pw_i[e]), ci)
            tab_ref[d, 6] = cr
            tab_ref[d, 7] = ci

        _, _, _, qr, qi, _ = _zoh_cols(lam_c_ref[:, 0:1], lam_c_ref[:, 1:2], ldt_c_ref[...])
        bm = _b_mask()
        br, bi = b_ref[0], b_ref[1]
        bset_ref[0] = jnp.where(bm, qr * br - qi * bi, 0.0).astype(BF16)
        bset_ref[1] = jnp.where(bm, qr * bi + qi * br, 0.0).astype(BF16)
        cm = _c_mask()
        cset_ref[0] = jnp.where(cm, c_ref[0], 0.0).astype(BF16)
        cset_ref[1] = jnp.where(cm, c_ref[1], 0.0).astype(BF16)

    vm = pl.BlockSpec(memory_space=pltpu.VMEM)
    return pl.pallas_call(
        body, name=name, in_specs=[vm] * 6, out_specs=[vm] * 3,
        out_shape=[_sds((2, 8, 8, S5_NS)), _sds((2, S5_NS, LANES), BF16), _sds((2, S5_W, 512), BF16)],
        compiler_params=pltpu.CompilerParams(vmem_limit_bytes=VMEM_LIMIT),
    )(lam_r, ldt_r, lam_c, ldt_c, b_t, c_t)


SCAN_W = SCAN_GROUPS * LANES


def _scan_chunk(src_ref, dst_ref, tab_ref, carry_ref, nb, reverse, xs_ref=None, acc_ref=None):
    row = lax.broadcasted_iota(jnp.int32, (8, LANES), 0)

    def step(i, carry):
        b = (nb - 1 - i) if reverse else i
        off = pl.multiple_of(b * 8, 8)
        out = []
        for g in range(SCAN_GROUPS):
            lanes = pl.ds(g * LANES, LANES)
            cr, ci = carry[2 * g], carry[2 * g + 1]
            yr = src_ref[0, pl.ds(off, 8), lanes]
            yi = src_ref[1, pl.ds(off, 8), lanes]
            for t, s in enumerate((1, 2, 4)):
                sh = (8 - s) if reverse else s
                sr = pltpu.roll(yr, sh, 0)
                si = pltpu.roll(yi, sh, 0)
                mr, mi = tab_ref[2 * t, :, lanes], tab_ref[2 * t + 1, :, lanes]
                yr, yi = yr + mr * sr - mi * si, yi + mr * si + mi * sr
            pr, pi = tab_ref[6, :, lanes], tab_ref[7, :, lanes]
            yr, yi = yr + pr * cr - pi * ci, yi + pr * ci + pi * cr
            dst_ref[0, pl.ds(off, 8), lanes] = yr
            dst_ref[1, pl.ds(off, 8), lanes] = yi
            if xs_ref is not None:
                nr = jnp.where(row == 7, cr, pltpu.roll(yr, 7, 0))
                ni = jnp.where(row == 7, ci, pltpu.roll(yi, 7, 0))
                xr = xs_ref[0, pl.ds(off, 8), lanes]
                xi = xs_ref[1, pl.ds(off, 8), lanes]
                acc_ref[0, :, lanes] += xr * nr + xi * ni
                acc_ref[1, :, lanes] += xr * ni - xi * nr
            last = 0 if reverse else 7
            out += [jnp.broadcast_to(yr[last:last + 1, :], (8, LANES)),
                    jnp.broadcast_to(yi[last:last + 1, :], (8, LANES))]
        return tuple(out)

    init = []
    for g in range(SCAN_GROUPS):
        init += [carry_ref[0, :, pl.ds(g * LANES, LANES)], carry_ref[1, :, pl.ds(g * LANES, LANES)]]
    fin = lax.fori_loop(0, nb, step, tuple(init))
    for g in range(SCAN_GROUPS):
        carry_ref[0, :, pl.ds(g * LANES, LANES)] = fin[2 * g]
        carry_ref[1, :, pl.ds(g * LANES, LANES)] = fin[2 * g + 1]


def _s5_scan_fwd(z, bset, tabs, name):
    L = z.shape[0]
    tl = min(SCAN_CHUNK, L)
    nc = L // tl

    def body(u_ref, b_ref, tab_ref, x_ref, carry_ref):
        @pl.when(pl.program_id(1) == 0)
        def _():
            carry_ref[...] = jnp.zeros_like(carry_ref)

        u = u_ref[...].astype(BF16)
        x_ref[0] = _dot(u, b_ref[0], NT)
        x_ref[1] = _dot(u, b_ref[1], NT)
        _scan_chunk(x_ref, x_ref, tab_ref, carry_ref, tl // 8, False)

    return pl.pallas_call(
        body, name=name, grid=(S5_NS // SCAN_W, nc),
        in_specs=[pl.BlockSpec((tl, LANES), lambda j, c: (c, j)),
                  pl.BlockSpec((2, SCAN_W, LANES), lambda j, c: (0, j, 0)),
                  pl.BlockSpec((None, 8, 8, SCAN_W), lambda j, c: (0, 0, 0, j))],
        out_specs=pl.BlockSpec((2, tl, SCAN_W), lambda j, c: (0, c, j)),
        out_shape=_sds((2, L, S5_NS)),
        scratch_shapes=[pltpu.VMEM((2, 8, SCAN_W), F32)],
        compiler_params=_cp("parallel", "arbitrary"),
    )(z, bset, tabs)


def _s5_scan_bwd(gyl, cset, xs, z, bset, gud, tabs, name):
    L = z.shape[0]
    tl = min(SCAN_CHUNK, L)
    nc = L // tl

    def body(g_ref, c_ref, xs_ref, u_ref, b_ref, gud_ref, tab_ref, gu_ref, ga_ref, gb_ref, gc_ref,
             gx_ref, carry_ref, acc_ref):
        c = pl.program_id(1)

        @pl.when(c == 0)
        def _():
            carry_ref[...] = jnp.zeros_like(carry_ref)
            acc_ref[...] = jnp.zeros_like(acc_ref)
            gb_ref[...] = jnp.zeros_like(gb_ref)
            gc_ref[...] = jnp.zeros_like(gc_ref)

        gy = g_ref[...].astype(BF16)
        gx_ref[0] = _dot(gy, c_ref[0])
        gx_ref[1] = -_dot(gy, c_ref[1])
        gc_ref[0] += _dot(gy, xs_ref[0].astype(BF16), TN)
        gc_ref[1] -= _dot(gy, xs_ref[1].astype(BF16), TN)
        _scan_chunk(gx_ref, gx_ref, tab_ref, carry_ref, tl // 8, True, xs_ref, acc_ref)
        gr = gx_ref[0].astype(BF16)
        gi = gx_ref[1].astype(BF16)
        gu_ref[...] = gud_ref[...] + _dot(gr, b_ref[0]) + _dot(gi, b_ref[1])
        u = u_ref[...].astype(BF16)
        gb_ref[0] += _dot(gr, u, TN)
        gb_ref[1] += _dot(gi, u, TN)

        @pl.when(c == nc - 1)
        def _():
            ga_ref[0:1, :] = jnp.sum(acc_ref[0], axis=0, keepdims=True)
            ga_ref[1:2, :] = jnp.sum(acc_ref[1], axis=0, keepdims=True)

    rev = lambda j, c: (nc - 1 - c, j)
    col = pl.BlockSpec((tl, LANES), rev)
    return pl.pallas_call(
        body, name=name, grid=(S5_NS // SCAN_W, nc),
        in_specs=[col, pl.BlockSpec((2, LANES, SCAN_W), lambda j, c: (0, j, 0)),
                  pl.BlockSpec((2, tl, SCAN_W), lambda j, c: (0, nc - 1 - c, j)), col,
                  pl.BlockSpec((2, SCAN_W, LANES), lambda j, c: (0, j, 0)), col,
                  pl.BlockSpec((None, 8, 8, SCAN_W), lambda j, c: (1, 0, 0, j))],
        out_specs=[col, pl.BlockSpec((2, SCAN_W), lambda j, c: (0, j)),
                   pl.BlockSpec((2, SCAN_W, LANES), lambda j, c: (0, j, 0)),
                   pl.BlockSpec((2, LANES, SCAN_W), lambda j, c: (0, j, 0))],
        out_shape=[_sds((L, S5_W)), _sds((2, S5_NS)), _sds((2, S5_NS, LANES)), _sds((2, S5_W, 512))],
        scratch_shapes=[pltpu.VMEM((2, tl, SCAN_W), F32), pltpu.VMEM((2, 8, SCAN_W), F32),
                        pltpu.VMEM((2, 8, SCAN_W), F32)],
        compiler_params=_cp("parallel", "arbitrary"),
    )(gyl, cset, xs, z, bset, gud, tabs)


def _s5_out_fwd(xs, cset, z, dvec, wglu, name):
    L = z.shape[0]
    bl = min(256, L)

    def body(x_ref, c_ref, u_ref, d_ref, w_ref, ylin_ref, ya_ref):
        cols = []
        for j in range(4):
            xr = x_ref[0, :, 512 * j:512 * (j + 1)].astype(BF16)
            xi = x_ref[1, :, 512 * j:512 * (j + 1)].astype(BF16)
            cr = c_ref[0, LANES * j:LANES * (j + 1), :]
            ci = c_ref[1, LANES * j:LANES * (j + 1), :]
            cols.append(_dot(xr, cr, NT) - _dot(xi, ci, NT))
        ylin = jnp.concatenate(cols, axis=1) + d_ref[...] * u_ref[...]
        yg = _gelu(ylin)
        t = _dot(yg.astype(BF16), w_ref[...])
        ylin_ref[...] = ylin
        ya_ref[...] = (yg * _sigmoid(t)).astype(BF16)

    return pl.pallas_call(
        body, name=name, grid=(L // bl,),
        in_specs=[pl.BlockSpec((2, bl, S5_NS), lambda i: (0, i, 0)),
                  pl.BlockSpec((2, S5_W, 512), lambda i: (0, 0, 0)),
                  pl.BlockSpec((bl, S5_W), lambda i: (i, 0)),
                  pl.BlockSpec((1, S5_W), lambda i: (0, 0)),
                  pl.BlockSpec((S5_W, S5_W), lambda i: (0, 0))],
        out_specs=[pl.BlockSpec((bl, S5_W), lambda i: (i, 0))] * 2,
        out_shape=[_sds((L, S5_W)), _sds((L, S5_W), BF16)],
        compiler_params=_cp("parallel"),
    )(xs, cset, z, dvec, wglu)


def _s5_glu_bwd(g_m, ylin, z, dvec, wglu, name):
    L = z.shape[0]
    bl = min(256, L)

    def body(g_ref, ylin_ref, u_ref, d_ref, w_ref, gyl_ref, gud_ref, gw_ref, gd_ref):
        i = pl.program_id(0)
        ylin = ylin_ref[...]
        yg = _gelu(ylin)
        ygb = yg.astype(BF16)
        sg = _sigmoid(_dot(ygb, w_ref[...]))
        gya = g_ref[...]
        gt = gya * yg * sg * (1.0 - sg)
        gtb = gt.astype(BF16)
        gyg = gya * sg + _dot(gtb, w_ref[...], NT)
        gyl = gyg * _gelu_grad(ylin)
        gyl_ref[...] = gyl
        gud_ref[...] = gyl * d_ref[...]

        @pl.when(i == 0)
        def _():
            gw_ref[...] = jnp.zeros_like(gw_ref)
            gd_ref[...] = jnp.zeros_like(gd_ref)

        gw_ref[...] += _dot(ygb, gtb, TN)
        gd_ref[...] += jnp.sum(gyl * u_ref[...], axis=0, keepdims=True)

    blk = pl.BlockSpec((bl, S5_W), lambda i: (i, 0))
    return pl.pallas_call(
        body, name=name, grid=(L // bl,),
        in_specs=[blk, blk, blk, pl.BlockSpec((1, S5_W), lambda i: (0, 0)),
                  pl.BlockSpec((S5_W, S5_W), lambda i: (0, 0))],
        out_specs=[blk, blk, pl.BlockSpec((S5_W, S5_W), lambda i: (0, 0)), pl.BlockSpec((1, S5_W), lambda i: (0, 0))],
        out_shape=[_sds((L, S5_W)), _sds((L, S5_W)), _sds((S5_W, S5_W)), _sds((1, S5_W))],
        compiler_params=_cp("arbitrary"),
    )(g_m, ylin, z, dvec, wglu)


def _s5_param_bwd(lam_c, ldt_c, b_t, gb, ga_c, gc, name):
    def body(lam_ref, ldt_ref, b_ref, gb_ref, ga_ref, gc_ref, glam_ref, gldt_ref, gbo_ref, gco_ref):
        lr, li = lam_ref[:, 0:1], lam_ref[:, 1:2]
        dt, ar, ai, qr, qi, den = _zoh_cols(lr, li, ldt_ref[...])
        bm = _b_mask()
        gbr = jnp.where(bm, gb_ref[0], 0.0)
        gbi = jnp.where(bm, gb_ref[1], 0.0)
        br, bi = b_ref[0], b_ref[1]
        obr = gbr * qr + gbi * qi
        obi = gbi * qr - gbr * qi
        gqr = jnp.sum(gbr * br + gbi * bi, axis=1, keepdims=True)
        gqi = jnp.sum(gbi * br - gbr * bi, axis=1, keepdims=True)
        for s in (64, 32, 16):
            obr = obr + pltpu.roll(obr, s, 1)
            obi = obi + pltpu.roll(obi, s, 1)
        gbo_ref[0] = obr
        gbo_ref[1] = obi
        gar = ga_ref[:, 0:1] + (gqr * lr - gqi * li) / den
        gai = ga_ref[:, 1:2] + (gqr * li + gqi * lr) / den
        qlr = (qr * lr + qi * li) / den
        qli = (qi * lr - qr * li) / den
        glr = -(gqr * qlr + gqi * qli)
        gli = -(gqi * qlr - gqr * qli)
        glr = glr + dt * (gar * ar + gai * ai)
        gli = gli + dt * (gai * ar - gar * ai)
        wr, wi = _cmul(lr, li, ar, ai)
        gldt = (gar * wr + gai * wi) * dt
        glam_ref[:, 0:1] = glr
        glam_ref[:, 1:2] = gli
        r = lax.broadcasted_iota(jnp.int32, (S5_NS, 32), 0)
        c = lax.broadcasted_iota(jnp.int32, (S5_NS, 32), 1)
        gldt_ref[...] = jnp.sum(jnp.where((r >> 6) == c, gldt, 0.0), axis=0, keepdims=True)
        cm = _c_mask()
        for k in range(2):
            oc = jnp.where(cm, gc_ref[k], 0.0)
            for s in (256, 128, 64):
                oc = oc + pltpu.roll(oc, s, 1)
            gco_ref[k] = oc[:, 0:LANES]

    vm = pl.BlockSpec(memory_space=pltpu.VMEM)
    return pl.pallas_call(
        body, name=name, in_specs=[vm] * 6, out_specs=[vm] * 4,
        out_shape=[_sds((S5_NS, 2)), _sds((1, 32)), _sds((2, S5_NS, LANES)), _sds((2, S5_W, LANES))],
        compiler_params=pltpu.CompilerParams(vmem_limit_bytes=VMEM_LIMIT),
    )(lam_c, ldt_c, b_t, gb, ga_c, gc)


FL_BLK = EVEN_PAD // LANES - 1
Q_BLK, K_BLK, V_BLK = 4, 8, 12
NEG = -1e30


def _log_sigmoid(v):
    return jnp.minimum(v, 0.0) - jnp.log(1.0 + jnp.exp(-jnp.abs(v)))


def _fox_f_fwd(z, bf, name):
    L = z.shape[0]
    tl = min(256, L)

    def body(fl_ref, b_ref, f_ref, fq_ref, carry_ref):
        i = pl.program_id(0)

        @pl.when(i == 0)
        def _():
            carry_ref[...] = jnp.zeros_like(carry_ref)

        lf = _log_sigmoid(fl_ref[...] + b_ref[...])
        r = lax.broadcasted_iota(jnp.int32, (tl, tl), 0)
        c = lax.broadcasted_iota(jnp.int32, (tl, tl), 1)
        tri = (r >= c).astype(F32)
        cs = lax.dot_general(tri, lf, NN, precision=lax.Precision.HIGHEST, preferred_element_type=F32) + carry_ref[...]
        f_ref[...] = cs
        carry_ref[...] = cs[tl - 1:tl, :]
        expand = (lax.broadcasted_iota(jnp.int32, (LANES, FOX_W), 0)
                  == (lax.broadcasted_iota(jnp.int32, (LANES, FOX_W), 1) >> 6)).astype(F32)
        fq_ref[...] = lax.dot_general(cs, expand, NN, precision=lax.Precision.HIGHEST, preferred_element_type=F32)

    return pl.pallas_call(
        body, name=name, grid=(L // tl,),
        in_specs=[pl.BlockSpec((tl, LANES), lambda i: (i, FL_BLK)), pl.BlockSpec((1, LANES), lambda i: (0, 0))],
        out_specs=[pl.BlockSpec((tl, LANES), lambda i: (i, 0)), pl.BlockSpec((tl, FOX_W), lambda i: (i, 0))],
        out_shape=[_sds((L, LANES)), _sds((L, FOX_W))],
        scratch_shapes=[pltpu.VMEM((1, LANES), F32)],
        compiler_params=_cp("arbitrary"),
    )(z, bf)


def _fox_f_bwd(dFk, dfq, z, bf, name):
    L = z.shape[0]
    tl = min(256, L)
    nb = L // tl

    def body(dfk_ref, dfq_ref, fl_ref, b_ref, dfl_ref, db_ref, carry_ref):
        i = pl.program_id(0)

        @pl.when(i == 0)
        def _():
            carry_ref[...] = jnp.zeros_like(carry_ref)
            db_ref[...] = jnp.zeros_like(db_ref)

        sel = (lax.broadcasted_iota(jnp.int32, (FOX_W, LANES), 0)
               == 64 * lax.broadcasted_iota(jnp.int32, (FOX_W, LANES), 1)).astype(F32)
        dfq_h = lax.dot_general(dfq_ref[...], sel, NN, precision=lax.Precision.HIGHEST, preferred_element_type=F32)
        r = lax.broadcasted_iota(jnp.int32, (tl, tl), 0)
        c = lax.broadcasted_iota(jnp.int32, (tl, tl), 1)
        tri = (r <= c).astype(F32)
        cs = lax.dot_general(tri, dfk_ref[...] + dfq_h, NN, precision=lax.Precision.HIGHEST,
                             preferred_element_type=F32) + carry_ref[...]
        carry_ref[...] = cs[0:1, :]
        dfl = cs * _sigmoid(-(fl_ref[...] + b_ref[...]))
        dfl_ref[...] = dfl
        db_ref[...] += jnp.sum(dfl, axis=0, keepdims=True)

    return pl.pallas_call(
        body, name=name, grid=(nb,),
        in_specs=[pl.BlockSpec((tl, LANES), lambda i: (nb - 1 - i, 0)),
                  pl.BlockSpec((tl, FOX_W), lambda i: (nb - 1 - i, 0)),
                  pl.BlockSpec((tl, LANES), lambda i: (nb - 1 - i, FL_BLK)),
                  pl.BlockSpec((1, LANES), lambda i: (0, 0))],
        out_specs=[pl.BlockSpec((tl, LANES), lambda i: (nb - 1 - i, 0)), pl.BlockSpec((1, LANES), lambda i: (0, 0))],
        out_shape=[_sds((L, LANES)), _sds((1, LANES))],
        scratch_shapes=[pltpu.VMEM((1, LANES), F32)],
        compiler_params=_cp("arbitrary"),
    )(dFk, dfq, z, bf)


def _head_mask(hh):
    lane = lax.broadcasted_iota(jnp.int32, (1, LANES), 1)
    return (lane >> 6) == hh


FOX_T = 512


def _fox_head(x, hh):
    return jnp.where(_head_mask(hh), x, 0.0).astype(BF16)


def _fox_scores(qh, k, fq_ref, fr_ref, hh, causal):
    s = _dot(qh, k, NT) + (fq_ref[:, 64 * hh:64 * hh + 1] - fr_ref[hh:hh + 1, :])
    return s if causal is None else jnp.where(causal, s, NEG)


def _causal(T):
    return lax.broadcasted_iota(jnp.int32, (T, T), 1) <= lax.broadcasted_iota(jnp.int32, (T, T), 0)


def _fox_fwd(z, fq, frow, name):
    L = z.shape[0]
    T = min(FOX_T, L)
    nq = L // T

    def body(qt_ref, kt_ref, q_ref, k_ref, v_ref, fq_ref, fr_ref, o_ref, lse_ref, m_ref, l_ref, acc_ref):
        t = pl.program_id(1)
        qi, ki = qt_ref[t], kt_ref[t]

        @pl.when(ki == 0)
        def _():
            m_ref[...] = jnp.full_like(m_ref, NEG)
            l_ref[...] = jnp.zeros_like(l_ref)
            acc_ref[...] = jnp.zeros_like(acc_ref)

        def step(diagonal):
            q = q_ref[...] * 0.125
            k = k_ref[...].astype(BF16)
            v = v_ref[...].astype(BF16)
            causal = _causal(T) if diagonal else None
            s = jnp.concatenate([_fox_scores(_fox_head(q, hh), k, fq_ref, fr_ref, hh, causal) for hh in range(2)],
                                axis=0)
            m_old = m_ref[...]
            m_new = jnp.maximum(m_old, jnp.max(s, axis=1, keepdims=True))
            alpha = jnp.exp(m_old - m_new)
            p = jnp.exp(s - m_new)
            l_ref[...] = alpha * l_ref[...] + jnp.sum(p, axis=1, keepdims=True)
            m_ref[...] = m_new
            acc_ref[...] = alpha * acc_ref[...] + _dot(p.astype(BF16), v)

        @pl.when(ki < qi)
        def _():
            step(False)

        @pl.when(ki == qi)
        def _():
            step(True)
            h0 = _head_mask(0)
            l = l_ref[...]
            o_h = acc_ref[...] / l
            lse_h = m_ref[...] + jnp.log(l)
            o_ref[...] = jnp.where(h0, o_h[:T], o_h[T:])
            lse_ref[...] = jnp.where(h0, lse_h[:T], lse_h[T:])

    pairs = [(qi, ki) for qi in range(nq) for ki in range(qi + 1)]
    qt = jnp.asarray([p[0] for p in pairs], jnp.int32)
    kt = jnp.asarray([p[1] for p in pairs], jnp.int32)

    def qspec(base):
        return pl.BlockSpec((T, LANES), lambda j, t, qt, kt: (qt[t], base + j))

    def kspec(base):
        return pl.BlockSpec((T, LANES), lambda j, t, qt, kt: (kt[t], base + j))

    return pl.pallas_call(
        body, name=name,
        grid_spec=pltpu.PrefetchScalarGridSpec(
            num_scalar_prefetch=2, grid=(4, len(pairs)),
            in_specs=[qspec(Q_BLK), kspec(K_BLK), kspec(V_BLK), qspec(0),
                      pl.BlockSpec((None, 2, T), lambda j, t, qt, kt: (j, 0, kt[t]))],
            out_specs=[qspec(0), qspec(0)],
            scratch_shapes=[pltpu.VMEM((2 * T, 1), F32), pltpu.VMEM((2 * T, 1), F32),
                            pltpu.VMEM((2 * T, LANES), F32)]),
        out_shape=[_sds((L, FOX_W)), _sds((L, FOX_W))],
        compiler_params=_cp("parallel", "arbitrary"),
    )(qt, kt, z, z, z, fq, frow)


def _fox_bwd(z, fq, frow, o, lse, g_m, name):
    L = z.shape[0]
    T = min(FOX_T, L)
    nq = L // T

    pairs = [(qi, ki) for ki in range(nq) for qi in range(ki, nq)]
    qt = jnp.asarray([p[0] for p in pairs], jnp.int32)
    kt = jnp.asarray([p[1] for p in pairs], jnp.int32)

    def body(qt_ref, kt_ref, q_ref, k_ref, v_ref, fq_ref, fr_ref, o_ref, lse_ref, do_ref,
             dq_ref, dk_ref, dv_ref, dfq_ref, dfk_ref, dk_acc, dv_acc, df_acc):
        t = pl.program_id(1)
        qi, ki = qt_ref[t], kt_ref[t]

        @pl.when(t == 0)
        def _():
            dq_ref[...] = jnp.zeros_like(dq_ref)
            dfq_ref[...] = jnp.zeros_like(dfq_ref)

        @pl.when(qi == ki)
        def _():
            dk_acc[...] = jnp.zeros_like(dk_acc)
            dv_acc[...] = jnp.zeros_like(dv_acc)
            df_acc[...] = jnp.zeros_like(df_acc)

        def step(diagonal):
            q = q_ref[...] * 0.125
            qb = q.astype(BF16)
            k = k_ref[...].astype(BF16)
            v = v_ref[...].astype(BF16)
            do = do_ref[...]
            dob = do.astype(BF16)
            do_o = dob.astype(F32) * o_ref[...]
            causal = _causal(T) if diagonal else None
            dvs, dks, dqs, rss = [], [], [], []
            for hh in range(2):
                s = _fox_scores(_fox_head(q, hh), k, fq_ref, fr_ref, hh, causal)
                p = jnp.exp(s - lse_ref[:, 64 * hh:64 * hh + 1])
                dp = _dot(_fox_head(do, hh), v, NT)
                delta = jnp.sum(jnp.where(_head_mask(hh), do_o, 0.0), axis=1, keepdims=True)
                ds = p * (dp - delta)
                dsb = ds.astype(BF16)
                dvs.append(_dot(p.astype(BF16), dob, TN))
                dks.append(_dot(dsb, qb, TN))
                dqs.append(_dot(dsb, k))
                rss.append(jnp.sum(ds, axis=1, keepdims=True))
                df_acc[hh:hh + 1, :] -= jnp.sum(ds, axis=0, keepdims=True)
            h0 = _head_mask(0)
            dv_acc[...] += jnp.where(h0, dvs[0], dvs[1])
            dk_acc[...] += jnp.where(h0, dks[0], dks[1])
            rows = pl.ds(pl.multiple_of(qi * T, T), T)
            dq_ref[rows, :] += jnp.where(h0, dqs[0], dqs[1])
            dfq_ref[rows, :] += jnp.where(h0, rss[0], rss[1])

        @pl.when(qi > ki)
        def _():
            step(False)

        @pl.when(qi == ki)
        def _():
            step(True)

        @pl.when(qi == nq - 1)
        def _():
            dk_ref[...] = dk_acc[...]
            dv_ref[...] = dv_acc[...]
            dfk_ref[...] = df_acc[...]

        @pl.when(t == len(pairs) - 1)
        def _():
            dq_ref[...] = dq_ref[...] * 0.125

    def qside(base):
        return pl.BlockSpec((T, LANES), lambda j, t, qt, kt: (qt[t], base + j))

    def kside(base):
        return pl.BlockSpec((T, LANES), lambda j, t, qt, kt: (kt[t], base + j))

    pair = pl.BlockSpec((L, LANES), lambda j, t, qt, kt: (0, j))
    frow_spec = pl.BlockSpec((None, 2, T), lambda j, t, qt, kt: (j, 0, kt[t]))
    return pl.pallas_call(
        body, name=name,
        grid_spec=pltpu.PrefetchScalarGridSpec(
            num_scalar_prefetch=2, grid=(4, len(pairs)),
            in_specs=[qside(Q_BLK), kside(K_BLK), kside(V_BLK), qside(0), frow_spec, qside(0), qside(0), qside(4)],
            out_specs=[pair, kside(0), kside(0), pair, frow_spec],
            scratch_shapes=[pltpu.VMEM((T, LANES), F32), pltpu.VMEM((T, LANES), F32), pltpu.VMEM((2, T), F32)]),
        out_shape=[_sds((L, FOX_W)), _sds((L, FOX_W)), _sds((L, FOX_W)), _sds((L, FOX_W)), _sds((4, 2, L))],
        compiler_params=_cp("parallel", "arbitrary"),
    )(qt, kt, z, z, z, fq, frow, o, lse, g_m)


def _shift_rows(v, s, down, row):
    n = v.shape[0]
    if down:
        return jnp.where(row >= s, pltpu.roll(v, s, 0), 0.0)
    return jnp.where(row < n - s, pltpu.roll(v, n - s, 0), 0.0)


def _window_sum(v, g, down, row):
    out = jnp.zeros_like(v)
    s = v
    for k in range(4):
        s = s + _shift_rows(s, 1 << k, down, row)
        out = jnp.where(g == k, s, out)
    return out


def _pool_inv_cnt(g, row):
    w = jnp.left_shift(2, g).astype(F32)
    return 1.0 / jnp.minimum(row.astype(F32) + 1.0, w)


def _pool_fwd(z, pool_w, scale, name):
    L = z.shape[0]

    def body(x_ref, w_ref, s_ref, y_ref, p_ref):
        g = pl.program_id(0)
        row = lax.broadcasted_iota(jnp.int32, (L, LANES), 0)
        x = x_ref[...]
        pooled = (_window_sum(x, g, True, row) * _pool_inv_cnt(g, row) - x).astype(BF16)
        p_ref[...] = pooled
        y_ref[...] = (_dot(pooled, w_ref[...].astype(BF16)) * s_ref[...]).astype(BF16)

    col = pl.BlockSpec((L, LANES), lambda g: (0, g))
    return pl.pallas_call(
        body, name=name, grid=(4,),
        in_specs=[col, pl.BlockSpec((None, LANES, LANES), lambda g: (g, 0, 0)), pl.BlockSpec((1, LANES), lambda g: (0, g))],
        out_specs=[col, col],
        out_shape=[_sds((L, 512), BF16), _sds((L, 512), BF16)],
        compiler_params=_cp("parallel"),
    )(z, pool_w, scale)


def _pool_bwd(g_m, pooled, pool_w, scale, name):
    L = g_m.shape[0]

    def body(g_ref, p_ref, w_ref, s_ref, gx_ref, gw_ref, gs_ref):
        g = pl.program_id(0)
        row = lax.broadcasted_iota(jnp.int32, (L, LANES), 0)
        gy = g_ref[...]
        pooled = p_ref[...]
        wb = w_ref[...].astype(BF16)
        lin = _dot(pooled, wb)
        gs_ref[...] = jnp.sum(gy * lin, axis=0, keepdims=True)
        glin = (gy * s_ref[...]).astype(BF16)
        gw_ref[...] = _dot(pooled, glin, TN)
        gp = _dot(glin, wb, NT)
        gx_ref[...] = _window_sum(gp * _pool_inv_cnt(g, row), g, False, row) - gp

    col = pl.BlockSpec((L, LANES), lambda g: (0, g))
    wspec = pl.BlockSpec((None, LANES, LANES), lambda g: (g, 0, 0))
    vec = pl.BlockSpec((1, LANES), lambda g: (0, g))
    return pl.pallas_call(
        body, name=name, grid=(4,),
        in_specs=[col, col, wspec, vec],
        out_specs=[col, wspec, vec],
        out_shape=[_sds((L, 512)), _sds((4, LANES, LANES)), _sds((1, 512))],
        compiler_params=_cp("parallel"),
    )(g_m, pooled, pool_w, scale)


SGU_CHUNKS = 4


def _sgu_ln(v, gam, bet):
    gv = _gelu(v)
    mu = jnp.mean(gv, axis=-1, keepdims=True)
    xc = gv - mu
    rs = lax.rsqrt(jnp.mean(xc * xc, axis=-1, keepdims=True) + EPS)
    xh = xc * rs
    return xh, rs, xh * gam + bet


def _tril_ws(w_ref, g):
    r = lax.broadcasted_iota(jnp.int32, (LANES, LANES), 0)
    c = lax.broadcasted_iota(jnp.int32, (LANES, LANES), 1)
    return jnp.where(r >= c, w_ref[g], 0.0).astype(BF16)


def _sgu_fwd(z, ln_g, ln_b, w_s, b_st, name):
    L = z.shape[0]
    rb = min(SGU_CHUNKS * LANES, L)

    def body(u_ref, v_ref, g_ref, b_ref, w_ref, bs_ref, y_ref):
        _, _, vln = _sgu_ln(v_ref[...], g_ref[...], b_ref[...])
        gu = _gelu(u_ref[...])
        vb = vln.astype(BF16)
        for g in range(4):
            ws = _tril_ws(w_ref, g)
            for n in range(rb // LANES):
                rows = slice(n * LANES, (n + 1) * LANES)
                cols = slice(g * LANES, (g + 1) * LANES)
                mixed = _dot(ws, vb[rows, cols]) + bs_ref[:, g:g + 1]
                y_ref[rows, cols] = (gu[rows, cols] * mixed).astype(BF16)

    vm = lambda shape: pl.BlockSpec(shape, lambda i: tuple(0 for _ in shape))
    return pl.pallas_call(
        body, name=name, grid=(L // rb,),
        in_specs=[pl.BlockSpec((rb, 512), lambda i: (i, 1)), pl.BlockSpec((rb, 512), lambda i: (i, 2)),
                  vm((1, 512)), vm((1, 512)), vm((4, LANES, LANES)), vm((LANES, 4))],
        out_specs=pl.BlockSpec((rb, 512), lambda i: (i, 0)),
        out_shape=_sds((L, 512), BF16),
        compiler_params=_cp("parallel"),
    )(z, z, ln_g, ln_b, w_s, b_st)


def _sgu_bwd(g_m, z, ln_g, ln_b, w_s, b_st, name):
    L = z.shape[0]
    rb = min(SGU_CHUNKS * LANES, L)

    def body(gy_ref, u_ref, v_ref, g_ref, b_ref, w_ref, bs_ref, gu_ref, gv_ref, gw_ref, gbs_ref, gg_ref, gb_ref):
        i = pl.program_id(0)

        @pl.when(i == 0)
        def _():
            gw_ref[...] = jnp.zeros_like(gw_ref)
            gbs_ref[...] = jnp.zeros_like(gbs_ref)
            gg_ref[...] = jnp.zeros_like(gg_ref)
            gb_ref[...] = jnp.zeros_like(gb_ref)

        v = v_ref[...]
        u = u_ref[...]
        gy = gy_ref[...]
        xh, rs, vln = _sgu_ln(v, g_ref[...], b_ref[...])
        gel_u = _gelu(u)
        gmix = gy * gel_u
        vb = vln.astype(BF16)
        gmb = gmix.astype(BF16)
        r = lax.broadcasted_iota(jnp.int32, (LANES, LANES), 0)
        c = lax.broadcasted_iota(jnp.int32, (LANES, LANES), 1)
        gvln_cols = []
        for g in range(4):
            ws = _tril_ws(w_ref, g)
            cols = slice(g * LANES, (g + 1) * LANES)
            gw = jnp.zeros((LANES, LANES), F32)
            gbs = jnp.zeros((LANES, 1), F32)
            parts = []
            for n in range(rb // LANES):
                rows = slice(n * LANES, (n + 1) * LANES)
                mixed = _dot(ws, vb[rows, cols]) + bs_ref[:, g:g + 1]
                gu_ref[rows, cols] = gy[rows, cols] * mixed * _gelu_grad(u[rows, cols])
                parts.append(_dot(ws, gmb[rows, cols], TN))
                gw = gw + _dot(gmb[rows, cols], vb[rows, cols], NT)
                gbs = gbs + jnp.sum(gmix[rows, cols], axis=1, keepdims=True)
            gvln_cols.append(jnp.concatenate(parts, axis=0))
            gw_ref[g] += jnp.where(r >= c, gw, 0.0)
            gbs_ref[:, g:g + 1] += gbs
        gvln = jnp.concatenate(gvln_cols, axis=1)
        gg_ref[...] += jnp.sum(gvln * xh, axis=0, keepdims=True)
        gb_ref[...] += jnp.sum(gvln, axis=0, keepdims=True)
        gxh = gvln * g_ref[...]
        ggv = rs * (gxh - jnp.mean(gxh, axis=-1, keepdims=True) - xh * jnp.mean(gxh * xh, axis=-1, keepdims=True))
        gv_ref[...] = ggv * _gelu_grad(v)

    vm = lambda shape: pl.BlockSpec(shape, lambda i: tuple(0 for _ in shape))
    blk = pl.BlockSpec((rb, 512), lambda i: (i, 0))
    return pl.pallas_call(
        body, name=name, grid=(L // rb,),
        in_specs=[pl.BlockSpec((rb, 512), lambda i: (i, 1)), pl.BlockSpec((rb, 512), lambda i: (i, 1)),
                  pl.BlockSpec((rb, 512), lambda i: (i, 2)),
                  vm((1, 512)), vm((1, 512)), vm((4, LANES, LANES)), vm((LANES, 4))],
        out_specs=[blk, blk, vm((4, LANES, LANES)), vm((LANES, 4)), vm((1, 512)), vm((1, 512))],
        out_shape=[_sds((L, 512)), _sds((L, 512)), _sds((4, LANES, LANES)), _sds((LANES, 4)),
                   _sds((1, 512)), _sds((1, 512))],
        compiler_params=_cp("arbitrary"),
    )(g_m, z, z, ln_g, ln_b, w_s, b_st)


def _adamw_math(w, g, m, v):
    nm = ADAM_B1 * m + (1.0 - ADAM_B1) * g
    nv = ADAM_B2 * v + (1.0 - ADAM_B2) * (g * g)
    m_hat = nm / (1.0 - ADAM_B1 ** ADAM_STEP)
    v_hat = nv / (1.0 - ADAM_B2 ** ADAM_STEP)
    delta = -ADAM_LR * (m_hat / (jnp.sqrt(v_hat) + ADAM_EPS) + ADAM_WD * w)
    return delta, nm, nv


def _sum_adamw(parts, w, m, v, name, layer=0, prev=None):
    n_layers, R, C = w.shape
    rb = 128 if R % 128 == 0 else R

    def body(p_ref, w_ref, m_ref, v_ref, *rest):
        g_ref, d_ref, nm_ref, nv_ref = rest[-4:]
        g = p_ref[0].astype(F32)
        for s in range(1, N_DEV):
            g = g + p_ref[s].astype(F32)
        d, nm, nv = _adamw_math(w_ref[...], g, m_ref[...], v_ref[...])
        g_ref[...] = g
        d_ref[...] = d
        nm_ref[...] = nm
        nv_ref[...] = nv

    blk = pl.BlockSpec((None, rb, C), lambda i: (layer, i, 0))
    prev = [] if prev is None else list(prev)
    return pl.pallas_call(
        body, name=name, grid=(R // rb,),
        in_specs=[pl.BlockSpec((N_DEV, rb, C), lambda i: (0, i, 0)), blk, blk, blk] + [ANY] * len(prev),
        out_specs=[blk] * 4, out_shape=[_sds((n_layers, R, C))] * 4,
        input_output_aliases={4 + k: k for k in range(len(prev))},
        compiler_params=_cp("parallel"),
    )(parts, w, m, v, *prev)


def _sum_pieces(parts, name):
    _, R, C = parts.shape

    def body(p_ref, g_ref):
        g = p_ref[0]
        for s in range(1, N_DEV):
            g = g + p_ref[s]
        g_ref[...] = g

    vm = pl.BlockSpec(memory_space=pltpu.VMEM)
    return pl.pallas_call(body, name=name, in_specs=[vm], out_specs=vm, out_shape=_sds((R, C)))(parts)


def _adamw(w, g, m, v, name):
    vm = pl.BlockSpec(memory_space=pltpu.VMEM)

    def body(w_ref, g_ref, m_ref, v_ref, d_ref, nm_ref, nv_ref):
        d, nm, nv = _adamw_math(w_ref[...], g_ref[...], m_ref[...], v_ref[...])
        d_ref[...] = d
        nm_ref[...] = nm
        nv_ref[...] = nv

    return pl.pallas_call(body, name=name, in_specs=[vm] * 4, out_specs=[vm] * 3,
                          out_shape=[_sds(w.shape)] * 3)(w, g, m, v)


def _mesh_pos():
    return lax.axis_index("x"), lax.axis_index("y"), lax.axis_index("c")


def _dev_index(p):
    return 4 * p[0] + 2 * p[1] + p[2]


def _all_gather(xs, name):
    n = len(xs)

    def body(*refs):
        x_refs, o_refs = refs[:n], refs[n:2 * n]
        send_sems, recv_sems, local_sems = refs[2 * n:]
        x, y, c = _mesh_pos()
        me, sibling = (x, y, c), (x, y, 1 - c)
        chips = [(1 - x, y), (x, 1 - y), (1 - x, 1 - y)]

        def copy(i, k, block, to, src=None):
            dst = o_refs[i].at[_dev_index(block)]
            return pltpu.make_async_remote_copy(
                src_ref=dst if src is None else src, dst_ref=dst,
                send_sem=send_sems.at[i, k], recv_sem=recv_sems.at[i, k], device_id=to, device_id_type=MESH)

        mine = [pltpu.make_async_copy(x_refs[i], o_refs[i].at[_dev_index(me)], local_sems.at[i]) for i in range(n)]
        for cp in mine:
            cp.start()
        first = []
        for i in range(n):
            first.append(copy(i, 0, me, sibling, src=x_refs[i]))
            first += [copy(i, 1 + j, me, (*chip, c), src=x_refs[i]) for j, chip in enumerate(chips)]
        for cp in first:
            cp.start()
        passed = []
        for j, chip in enumerate(chips):
            for i in range(n):
                copy(i, 1 + j, (*chip, c), me).wait_recv()
                fwd = copy(i, 4 + j, (*chip, c), sibling)
                fwd.start()
                passed.append(fwd)
        for i in range(n):
            copy(i, 0, sibling, me).wait_recv()
            for j, chip in enumerate(chips):
                copy(i, 4 + j, (*chip, 1 - c), me).wait_recv()
        for cp in first + passed:
            cp.wait_send()
        for cp in mine:
            cp.wait()

    outs = pl.pallas_call(
        body, name=name,
        in_specs=[ANY] * n, out_specs=[ANY] * n,
        out_shape=[_sds((N_DEV,) + x.shape, x.dtype) for x in xs],
        scratch_shapes=[pltpu.SemaphoreType.DMA((n, 7)), pltpu.SemaphoreType.DMA((n, 7)),
                        pltpu.SemaphoreType.DMA((n,))],
    )(*xs)
    return list(outs)


def _exchange(gs, name):
    n = len(gs)

    def body(*refs):
        g_refs, o_refs = refs[:n], refs[n:2 * n]
        send_sems, recv_sems, local_sems = refs[2 * n:]
        x, y, c = _mesh_pos()
        me = (x, y, c)
        mi = _dev_index(me)
        peers = [(x ^ dx, y ^ dy, c ^ dc) for dx in range(2) for dy in range(2) for dc in range(2)][1:]

        def copy(i, k, peer):
            return pltpu.make_async_remote_copy(
                src_ref=g_refs[i].at[_dev_index(peer)], dst_ref=o_refs[i].at[mi],
                send_sem=send_sems.at[i, k], recv_sem=recv_sems.at[i, k], device_id=peer, device_id_type=MESH)

        mine = [pltpu.make_async_copy(g_refs[i].at[mi], o_refs[i].at[mi], local_sems.at[i]) for i in range(n)]
        for cp in mine:
            cp.start()
        sends = [copy(i, k, peer) for i in range(n) for k, peer in enumerate(peers)]
        for cp in sends:
            cp.start()
        for i in range(n):
            for k, peer in enumerate(peers):
                pltpu.make_async_remote_copy(
                    src_ref=g_refs[i].at[mi], dst_ref=o_refs[i].at[_dev_index(peer)],
                    send_sem=send_sems.at[i, k], recv_sem=recv_sems.at[i, k], device_id=peer,
                    device_id_type=MESH).wait_recv()
        for cp in sends:
            cp.wait_send()
        for cp in mine:
            cp.wait()

    outs = pl.pallas_call(
        body, name=name,
        in_specs=[ANY] * n, out_specs=[ANY] * n,
        out_shape=[_sds(g.shape, g.dtype) for g in gs],
        scratch_shapes=[pltpu.SemaphoreType.DMA((n, 7)), pltpu.SemaphoreType.DMA((n, 7)),
                        pltpu.SemaphoreType.DMA((n,))],
    )(*gs)
    return list(outs)


HBM = pl.BlockSpec(memory_space=pltpu.HBM)
SEM = pl.BlockSpec(memory_space=pltpu.SEMAPHORE)
EFFECT = pltpu.SideEffectType.DATAFLOW_SIDE_EFFECTING


def _peer_list():
    x, y, c = _mesh_pos()
    peers = [(x ^ dx, y ^ dy, c ^ dc) for dx in range(2) for dy in range(2) for dc in range(2)][1:]
    return (x, y, c), peers


def _split_copy(src_ref, land_ref, send_sems, recv_sems, i, k, peer, slot, exchange):
    return pltpu.make_async_remote_copy(
        src_ref=src_ref.at[_dev_index(peer)] if exchange else src_ref, dst_ref=land_ref.at[slot],
        send_sem=send_sems.at[7 * i + k], recv_sem=recv_sems.at[7 * i + k], device_id=peer, device_id_type=MESH)


def _comm_start(groups, name, exchange, dep=None):
    sizes = [len(g) for g in groups]
    n = sum(sizes)
    srcs = [a for g in groups for a in g]
    my_index = _dev_index(_mesh_pos())
    lands = []
    for a in srcs:
        if exchange:
            own = lax.dynamic_slice(a, (my_index, 0, 0), (1,) + a.shape[1:])
            shape = a.shape
        else:
            own = a[None]
            shape = (N_DEV,) + a.shape
        lands.append(lax.dynamic_update_slice(lax.empty(shape, a.dtype), own, (my_index, 0, 0)))

    n_dep = 0 if dep is None else 1

    def body(*refs):
        src_refs, land_refs = refs[:n], refs[n:2 * n]
        sem_refs = refs[2 * n + n_dep:2 * n + n_dep + 2 * len(sizes)]
        token_ref = refs[-1]
        me, peers = _peer_list()
        mi = _dev_index(me)
        i = 0
        for gi, sz in enumerate(sizes):
            for j in range(sz):
                for k, peer in enumerate(peers):
                    _split_copy(src_refs[i], land_refs[i], sem_refs[2 * gi], sem_refs[2 * gi + 1], j, k, peer, mi,
                                exchange).start()
                i += 1
        token_ref[...] = jnp.zeros_like(token_ref)

    sem_shapes = []
    for sz in sizes:
        sem_shapes += [pltpu.SemaphoreType.DMA((7 * sz,)), pltpu.SemaphoreType.DMA((7 * sz,))]
    thru = [pltpu.HBM(a.shape, a.dtype) for a in srcs + lands]
    n_sem = len(sem_shapes)
    outs = pl.pallas_call(
        body, name=name,
        out_shape=tuple(sem_shapes + thru + [_sds((8, LANES))]),
        in_specs=[HBM] * (2 * n) + [ANY] * n_dep,
        out_specs=tuple([SEM] * n_sem + [HBM] * (2 * n) + [pl.BlockSpec(memory_space=pltpu.VMEM)]),
        input_output_aliases={i: n_sem + i for i in range(2 * n)},
        compiler_params=pltpu.CompilerParams(has_side_effects=EFFECT),
    )(*[pltpu.with_memory_space_constraint(a, pltpu.HBM) for a in srcs + lands], *([] if dep is None else [dep]))
    sems, thru_src, thru_land, token = outs[:n_sem], outs[n_sem:n_sem + n], outs[n_sem + n:n_sem + 2 * n], outs[-1]
    result, off = [], 0
    for gi, sz in enumerate(sizes):
        result.append((sems[2 * gi], sems[2 * gi + 1], list(thru_src[off:off + sz]), list(thru_land[off:off + sz])))
        off += sz
    return result, token


def _comm_wait(group, after, name, exchange):
    send_sems, recv_sems, srcs, lands = group
    n = len(srcs)
    after = list(after) if isinstance(after, (list, tuple)) else [after]

    def body(*refs):
        src_refs, land_refs = refs[:n], refs[n:2 * n]
        ssem, rsem = refs[2 * n], refs[2 * n + 1]
        me, peers = _peer_list()
        for i in range(n):
            for k, peer in enumerate(peers):
                cp = _split_copy(src_refs[i], land_refs[i], ssem, rsem, i, k, peer, _dev_index(peer), exchange)
                cp.wait_send()
                cp.wait_recv()

    outs = pl.pallas_call(
        body, name=name,
        out_shape=tuple(pltpu.HBM(a.shape, a.dtype) for a in srcs + lands),
        in_specs=[HBM] * (2 * n) + [SEM, SEM] + [ANY] * len(after),
        out_specs=tuple([HBM] * (2 * n)),
        input_output_aliases={i: i for i in range(2 * n)},
        compiler_params=pltpu.CompilerParams(has_side_effects=EFFECT),
    )(*srcs, *lands, send_sems, recv_sems, *after)
    return list(outs[n:])


def _tie(a, token):
    return a + token[0, 0].astype(a.dtype)


def _pack(arrs, rows):
    flat = jnp.concatenate([a.reshape(-1).astype(F32) for a in arrs])
    return jnp.pad(flat, (0, rows * LANES - flat.shape[0])).reshape(rows, LANES)


def _unpack(packed, shapes):
    flat = packed.reshape(-1)
    out, off = [], 0
    for s in shapes:
        n = math.prod(s)
        out.append(flat[off:off + n].reshape(s))
        off += n
    return out


def _packed_rows(shapes):
    n = sum(math.prod(s) for s in shapes)
    unit = N_DEV * 8 * LANES
    return -(-n // unit) * unit // LANES


def kernel(x, mix_pre_g, mix_post_g, mlp_pre_g, mlp_post_g, w_in_even, s5_lam_re, s5_lam_im, s5_log_dt, s5_b_re, s5_b_im, s5_c_re, s5_c_im, s5_d, s5_w_glu, fox_b_f, w_out_even, w_in_odd, pool_w, pool_scale, sgu_ln_g, sgu_ln_b, sgu_w_s, sgu_b_s, w_out_odd, mlp_w1, mlp_w2, loss_target, m_mix_pre_g, m_mix_post_g, m_mlp_pre_g, m_mlp_post_g, m_w_in_even, m_s5_lam_re, m_s5_lam_im, m_s5_log_dt, m_s5_b_re, m_s5_b_im, m_s5_c_re, m_s5_c_im, m_s5_d, m_s5_w_glu, m_fox_b_f, m_w_out_even, m_w_in_odd, m_pool_w, m_pool_scale, m_sgu_ln_g, m_sgu_ln_b, m_sgu_w_s, m_sgu_b_s, m_w_out_odd, m_mlp_w1, m_mlp_w2, v_mix_pre_g, v_mix_post_g, v_mlp_pre_g, v_mlp_post_g, v_w_in_even, v_s5_lam_re, v_s5_lam_im, v_s5_log_dt, v_s5_b_re, v_s5_b_im, v_s5_c_re, v_s5_c_im, v_s5_d, v_s5_w_glu, v_fox_b_f, v_w_out_even, v_w_in_odd, v_pool_w, v_pool_scale, v_sgu_ln_g, v_sgu_ln_b, v_sgu_w_s, v_sgu_b_s, v_w_out_odd, v_mlp_w1, v_mlp_w2):
    weights = dict(mix_pre_g=mix_pre_g, mix_post_g=mix_post_g, mlp_pre_g=mlp_pre_g, mlp_post_g=mlp_post_g, w_in_even=w_in_even, s5_lam_re=s5_lam_re, s5_lam_im=s5_lam_im, s5_log_dt=s5_log_dt, s5_b_re=s5_b_re, s5_b_im=s5_b_im, s5_c_re=s5_c_re, s5_c_im=s5_c_im, s5_d=s5_d, s5_w_glu=s5_w_glu, fox_b_f=fox_b_f, w_out_even=w_out_even, w_in_odd=w_in_odd, pool_w=pool_w, pool_scale=pool_scale, sgu_ln_g=sgu_ln_g, sgu_ln_b=sgu_ln_b, sgu_w_s=sgu_w_s, sgu_b_s=sgu_b_s, w_out_odd=w_out_odd, mlp_w1=mlp_w1, mlp_w2=mlp_w2)
    mom_m = dict(mix_pre_g=m_mix_pre_g, mix_post_g=m_mix_post_g, mlp_pre_g=m_mlp_pre_g, mlp_post_g=m_mlp_post_g, w_in_even=m_w_in_even, s5_lam_re=m_s5_lam_re, s5_lam_im=m_s5_lam_im, s5_log_dt=m_s5_log_dt, s5_b_re=m_s5_b_re, s5_b_im=m_s5_b_im, s5_c_re=m_s5_c_re, s5_c_im=m_s5_c_im, s5_d=m_s5_d, s5_w_glu=m_s5_w_glu, fox_b_f=m_fox_b_f, w_out_even=m_w_out_even, w_in_odd=m_w_in_odd, pool_w=m_pool_w, pool_scale=m_pool_scale, sgu_ln_g=m_sgu_ln_g, sgu_ln_b=m_sgu_ln_b, sgu_w_s=m_sgu_w_s, sgu_b_s=m_sgu_b_s, w_out_odd=m_w_out_odd, mlp_w1=m_mlp_w1, mlp_w2=m_mlp_w2)
    mom_v = dict(mix_pre_g=v_mix_pre_g, mix_post_g=v_mix_post_g, mlp_pre_g=v_mlp_pre_g, mlp_post_g=v_mlp_post_g, w_in_even=v_w_in_even, s5_lam_re=v_s5_lam_re, s5_lam_im=v_s5_lam_im, s5_log_dt=v_s5_log_dt, s5_b_re=v_s5_b_re, s5_b_im=v_s5_b_im, s5_c_re=v_s5_c_re, s5_c_im=v_s5_c_im, s5_d=v_s5_d, s5_w_glu=v_s5_w_glu, fox_b_f=v_fox_b_f, w_out_even=v_w_out_even, w_in_odd=v_w_in_odd, pool_w=v_pool_w, pool_scale=v_pool_scale, sgu_ln_g=v_sgu_ln_g, sgu_ln_b=v_sgu_ln_b, sgu_w_s=v_sgu_w_s, sgu_b_s=v_sgu_b_s, w_out_odd=v_w_out_odd, mlp_w1=v_mlp_w1, mlp_w2=v_mlp_w2)
    names = list(weights)
    L = x.shape[1]
    x0 = x[0]
    target = loss_target[0]
    my_index = 4 * lax.axis_index("x") + 2 * lax.axis_index("y") + lax.axis_index("c")

    small_vec = jnp.zeros((8, LANES), F32)
    small_vec = small_vec.at[0, :64].set(pool_scale[0]).at[1, :64].set(sgu_ln_g[0]).at[2, :64].set(sgu_ln_b[0])
    ag_groups, ag_token = _comm_start(
        [[jnp.transpose(w_in_even[0]).astype(BF16), small_vec],
         [s5_w_glu[0].astype(BF16), w_out_even[0].astype(BF16)],
         [mlp_w1[0].astype(BF16), mlp_w2[0].astype(BF16)],
         [jnp.transpose(w_in_odd[0]).astype(BF16), w_out_odd[0].astype(BF16), mlp_w1[1].astype(BF16), mlp_w2[1].astype(BF16)]],
        "ag_start", exchange=False)

    lam_r = jnp.concatenate([s5_lam_re.reshape(1, S5_NS), s5_lam_im.reshape(1, S5_NS)], axis=0)
    ldt_r = jnp.repeat(s5_log_dt.reshape(32), 64).reshape(1, S5_NS)
    lam_c = jnp.transpose(lam_r)
    ldt_c = jnp.transpose(ldt_r)
    b_t = jnp.stack([jnp.tile(s5_b_re.reshape(S5_NS, 16), (1, 8)), jnp.tile(s5_b_im.reshape(S5_NS, 16), (1, 8))])
    c_t = jnp.stack([jnp.tile(s5_c_re.reshape(S5_W, 64), (1, 8)), jnp.tile(s5_c_im.reshape(S5_W, 64), (1, 8))])
    bf_pad = jnp.pad(fox_b_f, ((0, 0), (0, LANES - 8)))
    b_st = jnp.transpose(sgu_b_s[0])

    h0, rx0 = _rms_fwd(x0, _tie(mix_pre_g[0:1], ag_token), "rms0")
    tabs, bset, cset = _s5_prep(lam_r, ldt_r, lam_c, ldt_c, b_t, c_t, "s5_prep")
    ag0 = _comm_wait(ag_groups[0], tabs, "ag_wait0", exchange=False)
    winT_e = jnp.pad(ag0[0].reshape(EVEN_IN, D_MODEL), ((0, EVEN_PAD - EVEN_IN), (0, 0)))
    pool_scale_f = ag0[1][:, 0, :64].reshape(1, 512)
    ln_g_f = ag0[1][:, 1, :64].reshape(1, 512)
    ln_b_f = ag0[1][:, 2, :64].reshape(1, 512)
    z0 = _mm(h0, winT_e, name="win_even", tb=True, bm=512, bn=EVEN_PAD)
    xs = _s5_scan_fwd(z0, bset, tabs, "s5_scan")
    ag1 = _comm_wait(ag_groups[1], xs, "ag_wait1", exchange=False)
    wglu = ag1[0].reshape(S5_W, S5_W)
    wout_e = ag1[1].reshape(D_MODEL, D_MODEL)
    ylin, ya = _s5_out_fwd(xs, cset, z0, s5_d, wglu, "s5_out")
    fcum, fq = _fox_f_fwd(z0, bf_pad, "fox_f")
    frow = jnp.transpose(fcum[:, :8]).reshape(4, 2, L)
    o_att, lse = _fox_fwd(z0, fq, frow, "fox_fwd")
    mix0 = [ya, o_att]
    y0 = _mm(mix0, wout_e, name="wout_even")
    x1, ry0, h1, rx1 = _post_pre_fwd(x0, y0, mix_post_g[0:1], mlp_pre_g[0:1], "post0")
    ag2 = _comm_wait(ag_groups[2], rx1, "ag_wait2", exchange=False)
    w1 = [ag2[0], None]
    w2 = [ag2[1].reshape(4 * D_MODEL, D_MODEL), None]
    p0, a0 = _mm(h1, w1[0], name="mlp0_w1", b3=True, out_dtypes=(BF16, BF16), epi=_epi_relu2, bm=2048)
    o0 = _mm(a0, w2[0], name="mlp0_w2", bm=2048)
    x2, ro0, h2, rx2 = _post_pre_fwd(x1, o0, mlp_post_g[0:1], mix_pre_g[1:2], "post1")
    ag3 = _comm_wait(ag_groups[3], rx2, "ag_wait3", exchange=False)
    winT_o = ag3[0].reshape(ODD_IN, D_MODEL)
    wout_o = ag3[1].reshape(D_MODEL, D_MODEL)
    w1[1] = ag3[2]
    w2[1] = ag3[3].reshape(4 * D_MODEL, D_MODEL)
    z1 = _mm(h2, winT_o, name="win_odd", tb=True, bn=ODD_IN)
    yc, pooled = _pool_fwd(z1, pool_w[0], pool_scale_f, "pool_fwd")
    yd = _sgu_fwd(z1, ln_g_f, ln_b_f, sgu_w_s[0], b_st, "sgu_fwd")
    mix1 = [yc, yd]
    y1 = _mm(mix1, wout_o, name="wout_odd")
    x3, ry1, h3, rx3 = _post_pre_fwd(x2, y1, mix_post_g[1:2], mlp_pre_g[1:2], "post2")
    p1, a1 = _mm(h3, w1[1], name="mlp1_w1", b3=True, out_dtypes=(BF16, BF16), epi=_epi_relu2, bm=2048)
    o1 = _mm(a1, w2[1], name="mlp1_w2", bm=2048)
    gx4, ro1, sq = _post_loss_fwd(x3, o1, mlp_post_g[1:2], target, "post3")

    g_o1, gg_mlp_post1 = _post_bwd(gx4, o1, ro1, mlp_post_g[1:2], "bpost3")
    g_p1 = _mm(g_o1, w2[1], name="b_mlp1_a", tb=True, out_dtypes=(BF16,), epi=_epi_relu2_bwd, extra=(p1,), bm=2048)
    gw2_1 = _mm(a1, g_o1, name="b_mlp1_w2", ta=True, bm=2048)
    g_h3 = _mm(g_p1, w1[1], name="b_mlp1_h", tb=True, b3=True, bm=2048)
    gw1_1 = _mm(h3, g_p1, name="b_mlp1_w1", ta=True, out3=True, bn=2048)
    (ex1,), tok1 = _comm_start([[gw1_1, gw2_1.reshape(N_DEV, 512, D_MODEL)]], "ex_start1", exchange=True)
    g_x3, gg_mlp_pre1, g_y1, gg_mix_post1 = _pre_post_bwd(g_h3, x3, rx3, _tie(mlp_pre_g[1:2], tok1), gx4, y1, ry1, mix_post_g[1:2], "bpre3")
    g_mix1 = _mm(g_y1, wout_o, name="b_wout_odd_m", tb=True)
    gwout_o = _mm(mix1, g_y1, name="b_wout_odd_w", ta=True)
    g_xc, g_pool_w, g_pool_scale = _pool_bwd(g_mix1, pooled, pool_w[0], pool_scale_f, "pool_bwd")
    g_u1, g_v1, g_ws, g_bst, g_ln_g, g_ln_b = _sgu_bwd(g_mix1, z1, ln_g_f, ln_b_f, sgu_w_s[0], b_st, "sgu_bwd")
    g_z1 = [g_xc, g_u1, g_v1]
    g_h2 = _mm(g_z1, winT_o, name="b_win_odd_h")
    gwinT_o = _mm(g_z1, h2, name="b_win_odd_w", ta=True)
    (ex2,), tok2 = _comm_start([[gwout_o.reshape(N_DEV, 128, D_MODEL), gwinT_o.reshape(N_DEV, ODD_IN // N_DEV, D_MODEL)]], "ex_start2", exchange=True)
    g_x2, gg_mix_pre1, g_o0, gg_mlp_post0 = _pre_post_bwd(g_h2, x2, rx2, _tie(mix_pre_g[1:2], tok2), g_x3, o0, ro0, mlp_post_g[0:1], "bpre2")
    g_p0 = _mm(g_o0, w2[0], name="b_mlp0_a", tb=True, out_dtypes=(BF16,), epi=_epi_relu2_bwd, extra=(p0,), bm=2048)
    gw2_0 = _mm(a0, g_o0, name="b_mlp0_w2", ta=True, bm=2048)
    g_h1 = _mm(g_p0, w1[0], name="b_mlp0_h", tb=True, b3=True, bm=2048)
    gw1_0 = _mm(h1, g_p0, name="b_mlp0_w1", ta=True, out3=True, bn=2048)
    (ex3,), tok3 = _comm_start([[gw1_0, gw2_0.reshape(N_DEV, 512, D_MODEL)]], "ex_start3", exchange=True)
    g_x1, gg_mlp_pre0, g_y0, gg_mix_post0 = _pre_post_bwd(g_h1, x1, rx1, _tie(mlp_pre_g[0:1], tok3), g_x2, y0, ry0, mix_post_g[0:1], "bpre1")
    g_mix0 = _mm(g_y0, wout_e, name="b_wout_even_m", tb=True)
    gwout_e = _mm(mix0, g_y0, name="b_wout_even_w", ta=True)
    gyl, gud, g_wglu, g_d = _s5_glu_bwd(g_mix0, ylin, z0, s5_d, wglu, "s5_glu_bwd")
    (ex4,), tok4 = _comm_start([[gwout_e.reshape(N_DEV, 128, D_MODEL), g_wglu.reshape(N_DEV, 64, S5_W)]], "ex_start4", exchange=True)
    g_u0, ga, gb_raw, gc_raw = _s5_scan_bwd(gyl, _tie(cset, tok4), xs, z0, bset, gud, tabs, "s5_scan_bwd")
    g_lam, g_ldt, g_b, g_c = _s5_param_bwd(lam_c, ldt_c, b_t, gb_raw, jnp.transpose(ga), gc_raw, "s5_param_bwd")
    dq, dk, dv, dfq, dfrow = _fox_bwd(z0, fq, frow, o_att, lse, g_mix0, "fox_bwd")
    dFk = jnp.pad(jnp.transpose(dfrow.reshape(8, L)), ((0, 0), (0, LANES - 8)))
    dfl, db_f = _fox_f_bwd(dFk, dfq, z0, bf_pad, "fox_f_bwd")
    g_z0 = [g_u0, dq, dk, dv, dfl]
    g_h0 = _mm(g_z0, winT_e, name="b_win_even_h")
    grad_x, gg_mix_pre0 = _pre_bwd(g_h0, x0, rx0, mix_pre_g[0:1], g_x1, "bpre0")
    gwinT_e = _mm(g_z0, h0, name="b_win_even_w", ta=True, bk=512)

    small_grads = dict(
        mix_pre_g=jnp.concatenate([gg_mix_pre0, gg_mix_pre1]), mix_post_g=jnp.concatenate([gg_mix_post0, gg_mix_post1]),
        mlp_pre_g=jnp.concatenate([gg_mlp_pre0, gg_mlp_pre1]), mlp_post_g=jnp.concatenate([gg_mlp_post0, gg_mlp_post1]),
        s5_lam_re=g_lam[:, 0], s5_lam_im=g_lam[:, 1], s5_log_dt=g_ldt,
        s5_b_re=g_b[0, :, :16], s5_b_im=g_b[1, :, :16], s5_c_re=g_c[0, :, :64], s5_c_im=g_c[1, :, :64],
        s5_d=g_d, fox_b_f=db_f[:, :8], pool_w=g_pool_w, sgu_w_s=g_ws, sgu_b_s=jnp.transpose(g_bst),
        pool_scale=g_pool_scale, sgu_ln_g=g_ln_g, sgu_ln_b=g_ln_b)
    small_names = list(small_grads)
    full_shapes = [(512,) if nm in ("pool_scale", "sgu_ln_g", "sgu_ln_b") else weights[nm].shape for nm in small_names]
    full_shapes.append((1, 1))
    rows = _packed_rows(full_shapes)
    packed = _pack([small_grads[nm] for nm in small_names] + [sq], rows).reshape(N_DEV, rows // N_DEV, LANES)
    (recv_small,) = _exchange([packed], "exchange_small")
    piece = _sum_pieces(recv_small, "sum_small")
    (small_all,) = _all_gather([piece], "ag_small")
    small_full = _unpack(small_all.reshape(rows, LANES), full_shapes)
    loss = 0.5 * small_full.pop()[0, 0] / D_MODEL

    gwinT_e_pieces = gwinT_e[:EVEN_IN].reshape(N_DEV, EVEN_IN // N_DEV, D_MODEL).astype(BF16)
    (ex5,), tok5 = _comm_start([[gwinT_e_pieces]], "ex_start5", exchange=True, dep=small_all)
    r_w1_1, r_w2_1 = _comm_wait(ex1, tok5, "ex_wait1", exchange=True)
    r_wout_o, r_win_o = _comm_wait(ex2, tok5, "ex_wait2", exchange=True)
    r_w1_0, r_w2_0 = _comm_wait(ex3, tok5, "ex_wait3", exchange=True)
    r_wout_e, r_wglu = _comm_wait(ex4, tok5, "ex_wait4", exchange=True)
    small_g = {}
    for nm, g in zip(small_names, small_full):
        if nm in ("pool_scale", "sgu_ln_g", "sgu_ln_b"):
            g = lax.dynamic_slice(g, (my_index * 64,), (64,)).reshape(1, 64)
        small_g[nm] = g
    own_shapes = [weights[nm].shape for nm in small_names]
    rows2 = _packed_rows(own_shapes)
    pw = _pack([weights[nm] for nm in small_names], rows2)
    pg = _pack([small_g[nm] for nm in small_names], rows2)
    pm = _pack([mom_m[nm] for nm in small_names], rows2)
    pv = _pack([mom_v[nm] for nm in small_names], rows2)
    pd, pnm, pnv = _adamw(pw, pg, pm, pv, "adamw_small")
    res = {}
    for nm, d_, m_, v_ in zip(small_names, _unpack(pd, own_shapes), _unpack(pnm, own_shapes), _unpack(pnv, own_shapes)):
        res[nm] = (small_g[nm], d_, m_, v_)

    for nm, parts in (("mlp_w1", (r_w1_0, r_w1_1)), ("mlp_w2", (r_w2_0, r_w2_1))):
        first = _sum_adamw(parts[0], weights[nm], mom_m[nm], mom_v[nm], "adamw_%s_0" % nm, layer=0)
        res[nm] = tuple(_sum_adamw(parts[1], weights[nm], mom_m[nm], mom_v[nm], "adamw_%s_1" % nm, layer=1, prev=first))
    big_parts = dict(s5_w_glu=r_wglu, w_out_even=r_wout_e, w_out_odd=r_wout_o)
    for nm, parts in big_parts.items():
        res[nm] = tuple(_sum_adamw(parts, weights[nm], mom_m[nm], mom_v[nm], "adamw_" + nm))
    done = [res[nm][1] for nm in ("mlp_w1", "mlp_w2", "s5_w_glu", "w_out_even", "w_out_odd")]
    for nm, parts in (("w_in_odd", r_win_o), ("w_in_even", None)):
        if parts is None:
            (parts,) = _comm_wait(ex5, done, "ex_wait5", exchange=True)
        outs = _sum_adamw(parts, jnp.transpose(weights[nm], (0, 2, 1)), jnp.transpose(mom_m[nm], (0, 2, 1)),
                          jnp.transpose(mom_v[nm], (0, 2, 1)), "adamw_" + nm)
        res[nm] = tuple(jnp.transpose(o, (0, 2, 1)) for o in outs)
        done.append(res[nm][1])

    grads = [res[nm][0].reshape(weights[nm].shape) for nm in names]
    deltas = [res[nm][1].reshape(weights[nm].shape) for nm in names]
    new_m = [res[nm][2].reshape(weights[nm].shape) for nm in names]
    new_v = [res[nm][3].reshape(weights[nm].shape) for nm in names]
    return (loss, grad_x[None], *grads, *deltas, *new_m, *new_v)
```

```python
import functools
import math

import jax
import jax.numpy as jnp
from jax import lax
from jax.experimental import pallas as pl
from jax.experimental.pallas import tpu as pltpu

F32 = jnp.float32
BF16 = jnp.bfloat16
MESH = pl.DeviceIdType.MESH
ANY = pl.BlockSpec(memory_space=pl.ANY)

N_DEV = 8
D_MODEL = 1024
EPS = 1e-6
NORM_ROWS = 512
S5_W = 512
S5_NS = 2048
SCAN_GROUPS = 4
SCAN_CHUNK = 1024
FOX_W = 512
EVEN_IN = 2056
EVEN_PAD = 2176
ODD_IN = 1536
LANES = 128
PIECE = 4 * D_MODEL // N_DEV
VMEM_LIMIT = 56 * 1024 * 1024

ADAM_LR = 0.001
ADAM_B1 = 0.9
ADAM_B2 = 0.999
ADAM_EPS = 1e-08
ADAM_WD = 0.01
ADAM_STEP = 10

NT = (((1,), (1,)), ((), ()))
TN = (((0,), (0,)), ((), ()))
NN = (((1,), (0,)), ((), ()))


def _cp(*sem):
    return pltpu.CompilerParams(dimension_semantics=sem, vmem_limit_bytes=VMEM_LIMIT)


def _sds(shape, dtype=F32):
    return jax.ShapeDtypeStruct(tuple(shape), dtype)


def _gelu(x):
    t = jnp.tanh(0.7978845608028654 * (x + 0.044715 * x * x * x))
    return 0.5 * x * (1.0 + t)


def _gelu_grad(x):
    t = jnp.tanh(0.7978845608028654 * (x + 0.044715 * x * x * x))
    du = 0.7978845608028654 * (1.0 + 3.0 * 0.044715 * x * x)
    return 0.5 * (1.0 + t) + 0.5 * x * (1.0 - t * t) * du


def _sigmoid(x):
    return 1.0 / (1.0 + jnp.exp(-x))


def _dot(a, b, dn=NN):
    return lax.dot_general(a, b, dn, preferred_element_type=F32)


def _mm(a, b, *, name, ta=False, tb=False, b3=False, out3=False, out_dtypes=(F32,), epi=None, extra=(),
        bm=1024, bn=1024, bk=1024, dep=None):
    a_list = list(a) if isinstance(a, (list, tuple)) else [a]
    widths = [p.shape[1] for p in a_list]
    offs = [sum(widths[:i]) for i in range(len(widths))]
    na = len(a_list)
    M = sum(widths) if ta else a_list[0].shape[0]
    K = a_list[0].shape[0] if ta else sum(widths)
    if na > 1:
        assert not b3 and not tb
        bm, bk = (M, bk) if ta else (bm, K)
    pw = b.shape[2] if b3 else PIECE
    if b3:
        N = b.shape[1] if tb else b.shape[0] * pw
        assert (b.shape[0] * pw if tb else b.shape[1]) == K
    else:
        N = b.shape[0] if tb else b.shape[1]
    bm, bn, bk = min(bm, M), min(bn, N), min(bk, K)
    assert M % bm == 0 and N % bn == 0 and K % bk == 0, (name, M, N, K, bm, bn, bk)
    assert not (b3 or out3) or ((bk if tb else bn) % pw == 0 and bn % PIECE == 0)
    nk = K // bk
    n_extra = len(extra)
    n_out = len(out_dtypes)
    dn = (((0 if ta else 1,), (1 if tb else 0,)), ((), ()))

    use_acc = nk > 1

    def body(*refs):
        a_refs, b_ref = refs[:na], refs[na]
        a_ref = a_refs[0]
        e_refs = refs[na + 1:na + 1 + n_extra]
        first_out = na + 1 + n_extra + (0 if dep is None else 1)
        o_refs = refs[first_out:first_out + n_out]
        acc_ref = refs[-1] if use_acc else o_refs[0]
        k = pl.program_id(2)

        def dot(a_v, b_v):
            return lax.dot_general(a_v.astype(BF16), b_v.astype(BF16), dn, preferred_element_type=F32)

        everything = slice(None)
        if na > 1 and ta:
            terms = [(pl.ds(off, w), everything, r, b_ref) for r, off, w in zip(a_refs, offs, widths)]
        elif na > 1:
            terms = [(everything, everything, r, b_ref.at[pl.ds(off, w), :]) for r, off, w in zip(a_refs, offs, widths)]
        elif not b3:
            terms = [(everything, everything, a_ref, b_ref)]
        elif tb:
            terms = [(everything, everything,
                      a_ref.at[pl.ds(t * pw, pw), :] if ta else a_ref.at[:, pl.ds(t * pw, pw)], b_ref.at[t])
                     for t in range(bk // pw)]
        else:
            terms = [(everything, pl.ds(t * pw, pw), a_ref, b_ref.at[t]) for t in range(bn // pw)]

        def finish(acc):
            outs = (acc,) if epi is None else epi(acc, *[e[...] for e in e_refs])
            for o_ref, o in zip(o_refs, outs):
                if out3:
                    for t in range(bn // PIECE):
                        o_ref[t] = o[:, t * PIECE:(t + 1) * PIECE].astype(o_ref.dtype)
                else:
                    o_ref[...] = o.astype(o_ref.dtype)

        if nk == 1:
            bands = {}
            for rows, cols, a_r, b_r in terms:
                key = (getattr(rows, "start", None), getattr(cols, "start", None))
                val = dot(a_r[...], b_r[...])
                bands[key] = val if key not in bands else bands[key] + val
            vals = list(bands.values())
            if len(vals) == 1:
                finish(vals[0])
            else:
                finish(jnp.concatenate(vals, axis=0 if (na > 1 and ta) else 1))
            return

        @pl.when(k == 0)
        def _():
            acc_ref[...] = jnp.zeros_like(acc_ref)

        for rows, cols, a_r, b_r in terms:
            acc_ref[rows, cols] += dot(a_r[...], b_r[...])

        @pl.when(k == nk - 1)
        def _():
            finish(acc_ref[...])

    if na > 1:
        a_specs = [pl.BlockSpec((bk, w), lambda i, j, k: (k, 0)) if ta else pl.BlockSpec((bm, w), lambda i, j, k: (i, 0))
                   for w in widths]
    else:
        a_specs = [pl.BlockSpec((bk, bm), lambda i, j, k: (k, i)) if ta else
                   pl.BlockSpec((bm, bk), lambda i, j, k: (i, k))]
    if b3:
        if tb:
            b_spec = pl.BlockSpec((bk // pw, bn, pw), lambda i, j, k: (k, j, 0))
        else:
            b_spec = pl.BlockSpec((bn // pw, bk, pw), lambda i, j, k: (j, k, 0))
    else:
        b_spec = pl.BlockSpec((bn, bk), lambda i, j, k: (j, k)) if tb else pl.BlockSpec((bk, bn), lambda i, j, k: (k, j))
    e_specs = [pl.BlockSpec((bm, bn), lambda i, j, k: (i, j)) for _ in extra]
    if out3:
        o_specs = [pl.BlockSpec((bn // PIECE, bm, PIECE), lambda i, j, k: (j, i, 0)) for _ in out_dtypes]
        o_shapes = [_sds((N // PIECE, M, PIECE), dt) for dt in out_dtypes]
    else:
        o_specs = [pl.BlockSpec((bm, bn), lambda i, j, k: (i, j)) for _ in out_dtypes]
        o_shapes = [_sds((M, N), dt) for dt in out_dtypes]
    outs = pl.pallas_call(
        body, name=name, grid=(M // bm, N // bn, nk),
        in_specs=a_specs + [b_spec] + e_specs + ([] if dep is None else [ANY]),
        out_specs=o_specs, out_shape=o_shapes,
        scratch_shapes=[pltpu.VMEM((bm, bn), F32)] if use_acc else [],
        compiler_params=_cp("parallel", "parallel", "arbitrary"),
    )(*a_list, b, *extra, *([] if dep is None else [dep]))
    return outs[0] if n_out == 1 else outs


def _epi_relu2(acc):
    r = jnp.maximum(acc, 0.0)
    return acc, r * r


def _epi_relu2_bwd(acc, p):
    return (acc * (2.0 * jnp.maximum(p.astype(F32), 0.0)),)


def _row_spec(rb, w=D_MODEL):
    return pl.BlockSpec((rb, w), lambda i: (i, 0))


def _vec_spec(w=D_MODEL):
    return pl.BlockSpec((1, w), lambda i: (0, 0))


def _rstd(v):
    return lax.rsqrt(jnp.mean(v * v, axis=-1, keepdims=True) + EPS)


def _rms_fwd(x, g, name):
    L = x.shape[0]
    rb = min(NORM_ROWS, L)

    def body(x_ref, g_ref, h_ref, r_ref):
        xv = x_ref[...]
        r = _rstd(xv)
        h_ref[...] = (xv * r * g_ref[...]).astype(BF16)
        r_ref[...] = r

    return pl.pallas_call(
        body, name=name, grid=(L // rb,),
        in_specs=[_row_spec(rb), _vec_spec()],
        out_specs=[_row_spec(rb), _row_spec(rb, 1)],
        out_shape=[_sds((L, D_MODEL), BF16), _sds((L, 1))],
        compiler_params=_cp("parallel"),
    )(x, g)


def _post_pre_fwd(x_in, y, g_post, g_pre, name):
    L = x_in.shape[0]
    rb = min(NORM_ROWS, L)

    def body(x_ref, y_ref, gp_ref, gn_ref, xo_ref, ry_ref, h_ref, rx_ref):
        yv = y_ref[...]
        ry = _rstd(yv)
        xo = x_ref[...] + yv * ry * gp_ref[...]
        rx = _rstd(xo)
        xo_ref[...] = xo
        ry_ref[...] = ry
        h_ref[...] = (xo * rx * gn_ref[...]).astype(BF16)
        rx_ref[...] = rx

    return pl.pallas_call(
        body, name=name, grid=(L // rb,),
        in_specs=[_row_spec(rb), _row_spec(rb), _vec_spec(), _vec_spec()],
        out_specs=[_row_spec(rb), _row_spec(rb, 1), _row_spec(rb), _row_spec(rb, 1)],
        out_shape=[_sds((L, D_MODEL)), _sds((L, 1)), _sds((L, D_MODEL), BF16), _sds((L, 1))],
        compiler_params=_cp("parallel"),
    )(x_in, y, g_post, g_pre)


def _post_loss_fwd(x_in, y, g_post, target, name):
    L = x_in.shape[0]
    rb = min(NORM_ROWS, L)

    def body(x_ref, y_ref, gp_ref, t_ref, gx_ref, ry_ref, loss_ref):
        i = pl.program_id(0)
        yv = y_ref[...]
        ry = _rstd(yv)
        diff = x_ref[...] + yv * ry * gp_ref[...] - t_ref[...]
        gx_ref[...] = diff * (1.0 / D_MODEL)
        ry_ref[...] = ry

        @pl.when(i == 0)
        def _():
            loss_ref[...] = jnp.zeros_like(loss_ref)

        loss_ref[...] += jnp.sum(diff * diff, keepdims=True)

    return pl.pallas_call(
        body, name=name, grid=(L // rb,),
        in_specs=[_row_spec(rb), _row_spec(rb), _vec_spec(), _row_spec(rb)],
        out_specs=[_row_spec(rb), _row_spec(rb, 1), pl.BlockSpec((1, 1), lambda i: (0, 0))],
        out_shape=[_sds((L, D_MODEL)), _sds((L, 1)), _sds((1, 1))],
        compiler_params=_cp("arbitrary"),
    )(x_in, y, g_post, target)


def _rms_bwd_rows(dy, xv, r, g):
    n = xv * r
    dyg = dy * g
    return r * (dyg - n * jnp.mean(dyg * n, axis=-1, keepdims=True)), n


def _post_bwd(g_out, y, ry, g_post, name):
    L = y.shape[0]
    rb = min(NORM_ROWS, L)

    def body(go_ref, y_ref, ry_ref, gp_ref, gy_ref, gg_ref):
        i = pl.program_id(0)
        go = go_ref[...]
        gy, n = _rms_bwd_rows(go, y_ref[...], ry_ref[...], gp_ref[...])
        gy_ref[...] = gy.astype(BF16)

        @pl.when(i == 0)
        def _():
            gg_ref[...] = jnp.zeros_like(gg_ref)

        gg_ref[...] += jnp.sum(go * n, axis=0, keepdims=True)

    return pl.pallas_call(
        body, name=name, grid=(L // rb,),
        in_specs=[_row_spec(rb), _row_spec(rb), _row_spec(rb, 1), _vec_spec()],
        out_specs=[_row_spec(rb), _vec_spec()],
        out_shape=[_sds((L, D_MODEL), BF16), _sds((1, D_MODEL))],
        compiler_params=_cp("arbitrary"),
    )(g_out, y, ry, g_post)


def _pre_post_bwd(g_h, x, rx, g_pre, g_out, y_prev, ry_prev, g_post_prev, name):
    L = x.shape[0]
    rb = min(NORM_ROWS, L)

    def body(gh_ref, x_ref, rx_ref, gn_ref, go_ref, y_ref, ry_ref, gp_ref, gi_ref, ggn_ref, gy_ref, ggp_ref):
        i = pl.program_id(0)
        gh = gh_ref[...]
        gx, n = _rms_bwd_rows(gh, x_ref[...], rx_ref[...], gn_ref[...])
        gi = go_ref[...] + gx
        gi_ref[...] = gi
        gy, ny = _rms_bwd_rows(gi, y_ref[...], ry_ref[...], gp_ref[...])
        gy_ref[...] = gy.astype(BF16)

        @pl.when(i == 0)
        def _():
            ggn_ref[...] = jnp.zeros_like(ggn_ref)
            ggp_ref[...] = jnp.zeros_like(ggp_ref)

        ggn_ref[...] += jnp.sum(gh * n, axis=0, keepdims=True)
        ggp_ref[...] += jnp.sum(gi * ny, axis=0, keepdims=True)

    return pl.pallas_call(
        body, name=name, grid=(L // rb,),
        in_specs=[_row_spec(rb), _row_spec(rb), _row_spec(rb, 1), _vec_spec(), _row_spec(rb),
                  _row_spec(rb), _row_spec(rb, 1), _vec_spec()],
        out_specs=[_row_spec(rb), _vec_spec(), _row_spec(rb), _vec_spec()],
        out_shape=[_sds((L, D_MODEL)), _sds((1, D_MODEL)), _sds((L, D_MODEL), BF16), _sds((1, D_MODEL))],
        compiler_params=_cp("arbitrary"),
    )(g_h, x, rx, g_pre, g_out, y_prev, ry_prev, g_post_prev)


def _pre_bwd(g_h, x, rx, g_pre, g_out, name):
    L = x.shape[0]
    rb = min(NORM_ROWS, L)

    def body(gh_ref, x_ref, rx_ref, gn_ref, go_ref, gi_ref, ggn_ref):
        i = pl.program_id(0)
        gh = gh_ref[...]
        gx, n = _rms_bwd_rows(gh, x_ref[...], rx_ref[...], gn_ref[...])
        gi_ref[...] = go_ref[...] + gx

        @pl.when(i == 0)
        def _():
            ggn_ref[...] = jnp.zeros_like(ggn_ref)

        ggn_ref[...] += jnp.sum(gh * n, axis=0, keepdims=True)

    return pl.pallas_call(
        body, name=name, grid=(L // rb,),
        in_specs=[_row_spec(rb), _row_spec(rb), _row_spec(rb, 1), _vec_spec(), _row_spec(rb)],
        out_specs=[_row_spec(rb), _vec_spec()],
        out_shape=[_sds((L, D_MODEL)), _sds((1, D_MODEL))],
        compiler_params=_cp("arbitrary"),
    )(g_h, x, rx, g_pre, g_out)


def _cmul(ar, ai, br, bi):
    return ar * br - ai * bi, ar * bi + ai * br


def _zoh_cols(lr, li, ldt):
    dt = jnp.exp(ldt)
    mag = jnp.exp(lr * dt)
    ar = mag * jnp.cos(li * dt)
    ai = mag * jnp.sin(li * dt)
    den = lr * lr + li * li
    nr = ar - 1.0
    qr = (nr * lr + ai * li) / den
    qi = (ai * lr - nr * li) / den
    return dt, ar, ai, qr, qi, den


def _b_mask():
    r = lax.broadcasted_iota(jnp.int32, (S5_NS, LANES), 0)
    c = lax.broadcasted_iota(jnp.int32, (S5_NS, LANES), 1)
    return ((r >> 6) & 7) == (c >> 4)


def _c_mask():
    r = lax.broadcasted_iota(jnp.int32, (S5_W, 512), 0)
    c = lax.broadcasted_iota(jnp.int32, (S5_W, 512), 1)
    return ((r >> 4) & 7) == (c >> 6)


def _s5_prep(lam_r, ldt_r, lam_c, ldt_c, b_t, c_t, name):
    def body(lam_r_ref, ldt_r_ref, lam_c_ref, ldt_c_ref, b_ref, c_ref, tab_ref, bset_ref, cset_ref):
        lr, li = lam_r_ref[0:1, :], lam_r_ref[1:2, :]
        dt = jnp.exp(ldt_r_ref[...])
        mag = jnp.exp(lr * dt)
        p1r, p1i = mag * jnp.cos(li * dt), mag * jnp.sin(li * dt)
        p2r, p2i = _cmul(p1r, p1i, p1r, p1i)
        p3r, p3i = _cmul(p2r, p2i, p1r, p1i)
        p4r, p4i = _cmul(p2r, p2i, p2r, p2i)
        p5r, p5i = _cmul(p4r, p4i, p1r, p1i)
        p6r, p6i = _cmul(p4r, p4i, p2r, p2i)
        p7r, p7i = _cmul(p4r, p4i, p3r, p3i)
        p8r, p8i = _cmul(p4r, p4i, p4r, p4i)
        pw_r = [p1r, p2r, p3r, p4r, p5r, p6r, p7r, p8r]
        pw_i = [p1i, p2i, p3i, p4i, p5i, p6i, p7i, p8i]
        row = lax.broadcasted_iota(jnp.int32, (8, S5_NS), 0)
        zero = jnp.zeros((8, S5_NS), F32)

        def bc(v):
            return jnp.broadcast_to(v, (8, S5_NS))

        for d in range(2):
            sgn = 1.0 if d == 0 else -1.0
            for t, s in enumerate((1, 2, 4)):
                live = (row >= s) if d == 0 else (row <= 7 - s)
                tab_ref[d, 2 * t] = jnp.where(live, bc(pw_r[s - 1]), zero)
                tab_ref[d, 2 * t + 1] = jnp.where(live, bc(sgn * pw_i[s - 1]), zero)
            cr, ci = zero, zero
            for i in range(8):
                e = i if d == 0 else 7 - i
                cr = jnp.where(row == i, bc(pw_r[e]), cr)
                ci = jnp.where(row == i, bc(sgn * pw_i[e]), ci)
            tab_ref[d, 6] = cr
            tab_ref[d, 7] = ci

        _, _, _, qr, qi, _ = _zoh_cols(lam_c_ref[:, 0:1], lam_c_ref[:, 1:2], ldt_c_ref[...])
        bm = _b_mask()
        br, bi = b_ref[0], b_ref[1]
        bset_ref[0] = jnp.where(bm, qr * br - qi * bi, 0.0).astype(BF16)
        bset_ref[1] = jnp.where(bm, qr * bi + qi * br, 0.0).astype(BF16)
        cm = _c_mask()
        cset_ref[0] = jnp.where(cm, c_ref[0], 0.0).astype(BF16)
        cset_ref[1] = jnp.where(cm, c_ref[1], 0.0).astype(BF16)

    vm = pl.BlockSpec(memory_space=pltpu.VMEM)
    return pl.pallas_call(
        body, name=name, in_specs=[vm] * 6, out_specs=[vm] * 3,
        out_shape=[_sds((2, 8, 8, S5_NS)), _sds((2, S5_NS, LANES), BF16), _sds((2, S5_W, 512), BF16)],
        compiler_params=pltpu.CompilerParams(vmem_limit_bytes=VMEM_LIMIT),
    )(lam_r, ldt_r, lam_c, ldt_c, b_t, c_t)


SCAN_W = SCAN_GROUPS * LANES


def _scan_chunk(src_ref, dst_ref, tab_ref, carry_ref, nb, reverse, xs_ref=None, acc_ref=None):
    row = lax.broadcasted_iota(jnp.int32, (8, LANES), 0)

    def step(i, carry):
        b = (nb - 1 - i) if reverse else i
        off = pl.multiple_of(b * 8, 8)
        out = []
        for g in range(SCAN_GROUPS):
            lanes = pl.ds(g * LANES, LANES)
            cr, ci = carry[2 * g], carry[2 * g + 1]
            yr = src_ref[0, pl.ds(off, 8), lanes]
            yi = src_ref[1, pl.ds(off, 8), lanes]
            for t, s in enumerate((1, 2, 4)):
                sh = (8 - s) if reverse else s
                sr = pltpu.roll(yr, sh, 0)
                si = pltpu.roll(yi, sh, 0)
                mr, mi = tab_ref[2 * t, :, lanes], tab_ref[2 * t + 1, :, lanes]
                yr, yi = yr + mr * sr - mi * si, yi + mr * si + mi * sr
            pr, pi = tab_ref[6, :, lanes], tab_ref[7, :, lanes]
            yr, yi = yr + pr * cr - pi * ci, yi + pr * ci + pi * cr
            dst_ref[0, pl.ds(off, 8), lanes] = yr
            dst_ref[1, pl.ds(off, 8), lanes] = yi
            if xs_ref is not None:
                nr = jnp.where(row == 7, cr, pltpu.roll(yr, 7, 0))
                ni = jnp.where(row == 7, ci, pltpu.roll(yi, 7, 0))
                xr = xs_ref[0, pl.ds(off, 8), lanes]
                xi = xs_ref[1, pl.ds(off, 8), lanes]
                acc_ref[0, :, lanes] += xr * nr + xi * ni
                acc_ref[1, :, lanes] += xr * ni - xi * nr
            last = 0 if reverse else 7
            out += [jnp.broadcast_to(yr[last:last + 1, :], (8, LANES)),
                    jnp.broadcast_to(yi[last:last + 1, :], (8, LANES))]
        return tuple(out)

    init = []
    for g in range(SCAN_GROUPS):
        init += [carry_ref[0, :, pl.ds(g * LANES, LANES)], carry_ref[1, :, pl.ds(g * LANES, LANES)]]
    fin = lax.fori_loop(0, nb, step, tuple(init))
    for g in range(SCAN_GROUPS):
        carry_ref[0, :, pl.ds(g * LANES, LANES)] = fin[2 * g]
        carry_ref[1, :, pl.ds(g * LANES, LANES)] = fin[2 * g + 1]


def _s5_scan_fwd(z, bset, tabs, name):
    L = z.shape[0]
    tl = min(SCAN_CHUNK, L)
    nc = L // tl

    def body(u_ref, b_ref, tab_ref, x_ref, carry_ref):
        @pl.when(pl.program_id(1) == 0)
        def _():
            carry_ref[...] = jnp.zeros_like(carry_ref)

        u = u_ref[...].astype(BF16)
        x_ref[0] = _dot(u, b_ref[0], NT)
        x_ref[1] = _dot(u, b_ref[1], NT)
        _scan_chunk(x_ref, x_ref, tab_ref, carry_ref, tl // 8, False)

    return pl.pallas_call(
        body, name=name, grid=(S5_NS // SCAN_W, nc),
        in_specs=[pl.BlockSpec((tl, LANES), lambda j, c: (c, j)),
                  pl.BlockSpec((2, SCAN_W, LANES), lambda j, c: (0, j, 0)),
                  pl.BlockSpec((None, 8, 8, SCAN_W), lambda j, c: (0, 0, 0, j))],
        out_specs=pl.BlockSpec((2, tl, SCAN_W), lambda j, c: (0, c, j)),
        out_shape=_sds((2, L, S5_NS)),
        scratch_shapes=[pltpu.VMEM((2, 8, SCAN_W), F32)],
        compiler_params=_cp("parallel", "arbitrary"),
    )(z, bset, tabs)


def _s5_scan_bwd(gyl, cset, xs, z, bset, gud, tabs, name):
    L = z.shape[0]
    tl = min(SCAN_CHUNK, L)
    nc = L // tl

    def body(g_ref, c_ref, xs_ref, u_ref, b_ref, gud_ref, tab_ref, gu_ref, ga_ref, gb_ref, gc_ref,
             gx_ref, carry_ref, acc_ref):
        c = pl.program_id(1)

        @pl.when(c == 0)
        def _():
            carry_ref[...] = jnp.zeros_like(carry_ref)
            acc_ref[...] = jnp.zeros_like(acc_ref)
            gb_ref[...] = jnp.zeros_like(gb_ref)
            gc_ref[...] = jnp.zeros_like(gc_ref)

        gy = g_ref[...].astype(BF16)
        gx_ref[0] = _dot(gy, c_ref[0])
        gx_ref[1] = -_dot(gy, c_ref[1])
        gc_ref[0] += _dot(gy, xs_ref[0].astype(BF16), TN)
        gc_ref[1] -= _dot(gy, xs_ref[1].astype(BF16), TN)
        _scan_chunk(gx_ref, gx_ref, tab_ref, carry_ref, tl // 8, True, xs_ref, acc_ref)
        gr = gx_ref[0].astype(BF16)
        gi = gx_ref[1].astype(BF16)
        gu_ref[...] = gud_ref[...] + _dot(gr, b_ref[0]) + _dot(gi, b_ref[1])
        u = u_ref[...].astype(BF16)
        gb_ref[0] += _dot(gr, u, TN)
        gb_ref[1] += _dot(gi, u, TN)

        @pl.when(c == nc - 1)
        def _():
            ga_ref[0:1, :] = jnp.sum(acc_ref[0], axis=0, keepdims=True)
            ga_ref[1:2, :] = jnp.sum(acc_ref[1], axis=0, keepdims=True)

    rev = lambda j, c: (nc - 1 - c, j)
    col = pl.BlockSpec((tl, LANES), rev)
    return pl.pallas_call(
        body, name=name, grid=(S5_NS // SCAN_W, nc),
        in_specs=[col, pl.BlockSpec((2, LANES, SCAN_W), lambda j, c: (0, j, 0)),
                  pl.BlockSpec((2, tl, SCAN_W), lambda j, c: (0, nc - 1 - c, j)), col,
                  pl.BlockSpec((2, SCAN_W, LANES), lambda j, c: (0, j, 0)), col,
                  pl.BlockSpec((None, 8, 8, SCAN_W), lambda j, c: (1, 0, 0, j))],
        out_specs=[col, pl.BlockSpec((2, SCAN_W), lambda j, c: (0, j)),
                   pl.BlockSpec((2, SCAN_W, LANES), lambda j, c: (0, j, 0)),
                   pl.BlockSpec((2, LANES, SCAN_W), lambda j, c: (0, j, 0))],
        out_shape=[_sds((L, S5_W)), _sds((2, S5_NS)), _sds((2, S5_NS, LANES)), _sds((2, S5_W, 512))],
        scratch_shapes=[pltpu.VMEM((2, tl, SCAN_W), F32), pltpu.VMEM((2, 8, SCAN_W), F32),
                        pltpu.VMEM((2, 8, SCAN_W), F32)],
        compiler_params=_cp("parallel", "arbitrary"),
    )(gyl, cset, xs, z, bset, gud, tabs)


def _s5_out_fwd(xs, cset, z, dvec, wglu, name):
    L = z.shape[0]
    bl = min(256, L)

    def body(x_ref, c_ref, u_ref, d_ref, w_ref, ylin_ref, ya_ref):
        cols = []
        for j in range(4):
            xr = x_ref[0, :, 512 * j:512 * (j + 1)].astype(BF16)
            xi = x_ref[1, :, 512 * j:512 * (j + 1)].astype(BF16)
            cr = c_ref[0, LANES * j:LANES * (j + 1), :]
            ci = c_ref[1, LANES * j:LANES * (j + 1), :]
            cols.append(_dot(xr, cr, NT) - _dot(xi, ci, NT))
        ylin = jnp.concatenate(cols, axis=1) + d_ref[...] * u_ref[...]
        yg = _gelu(ylin)
        t = _dot(yg.astype(BF16), w_ref[...])
        ylin_ref[...] = ylin
        ya_ref[...] = (yg * _sigmoid(t)).astype(BF16)

    return pl.pallas_call(
        body, name=name, grid=(L // bl,),
        in_specs=[pl.BlockSpec((2, bl, S5_NS), lambda i: (0, i, 0)),
                  pl.BlockSpec((2, S5_W, 512), lambda i: (0, 0, 0)),
                  pl.BlockSpec((bl, S5_W), lambda i: (i, 0)),
                  pl.BlockSpec((1, S5_W), lambda i: (0, 0)),
                  pl.BlockSpec((S5_W, S5_W), lambda i: (0, 0))],
        out_specs=[pl.BlockSpec((bl, S5_W), lambda i: (i, 0))] * 2,
        out_shape=[_sds((L, S5_W)), _sds((L, S5_W), BF16)],
        compiler_params=_cp("parallel"),
    )(xs, cset, z, dvec, wglu)


def _s5_glu_bwd(g_m, ylin, z, dvec, wglu, name):
    L = z.shape[0]
    bl = min(256, L)

    def body(g_ref, ylin_ref, u_ref, d_ref, w_ref, gyl_ref, gud_ref, gw_ref, gd_ref):
        i = pl.program_id(0)
        ylin = ylin_ref[...]
        yg = _gelu(ylin)
        ygb = yg.astype(BF16)
        sg = _sigmoid(_dot(ygb, w_ref[...]))
        gya = g_ref[...]
        gt = gya * yg * sg * (1.0 - sg)
        gtb = gt.astype(BF16)
        gyg = gya * sg + _dot(gtb, w_ref[...], NT)
        gyl = gyg * _gelu_grad(ylin)
        gyl_ref[...] = gyl
        gud_ref[...] = gyl * d_ref[...]

        @pl.when(i == 0)
        def _():
            gw_ref[...] = jnp.zeros_like(gw_ref)
            gd_ref[...] = jnp.zeros_like(gd_ref)

        gw_ref[...] += _dot(ygb, gtb, TN)
        gd_ref[...] += jnp.sum(gyl * u_ref[...], axis=0, keepdims=True)

    blk = pl.BlockSpec((bl, S5_W), lambda i: (i, 0))
    return pl.pallas_call(
        body, name=name, grid=(L // bl,),
        in_specs=[blk, blk, blk, pl.BlockSpec((1, S5_W), lambda i: (0, 0)),
                  pl.BlockSpec((S5_W, S5_W), lambda i: (0, 0))],
        out_specs=[blk, blk, pl.BlockSpec((S5_W, S5_W), lambda i: (0, 0)), pl.BlockSpec((1, S5_W), lambda i: (0, 0))],
        out_shape=[_sds((L, S5_W)), _sds((L, S5_W)), _sds((S5_W, S5_W)), _sds((1, S5_W))],
        compiler_params=_cp("arbitrary"),
    )(g_m, ylin, z, dvec, wglu)


def _s5_param_bwd(lam_c, ldt_c, b_t, gb, ga_c, gc, name):
    def body(lam_ref, ldt_ref, b_ref, gb_ref, ga_ref, gc_ref, glam_ref, gldt_ref, gbo_ref, gco_ref):
        lr, li = lam_ref[:, 0:1], lam_ref[:, 1:2]
        dt, ar, ai, qr, qi, den = _zoh_cols(lr, li, ldt_ref[...])
        bm = _b_mask()
        gbr = jnp.where(bm, gb_ref[0], 0.0)
        gbi = jnp.where(bm, gb_ref[1], 0.0)
        br, bi = b_ref[0], b_ref[1]
        obr = gbr * qr + gbi * qi
        obi = gbi * qr - gbr * qi
        gqr = jnp.sum(gbr * br + gbi * bi, axis=1, keepdims=True)
        gqi = jnp.sum(gbi * br - gbr * bi, axis=1, keepdims=True)
        for s in (64, 32, 16):
            obr = obr + pltpu.roll(obr, s, 1)
            obi = obi + pltpu.roll(obi, s, 1)
        gbo_ref[0] = obr
        gbo_ref[1] = obi
        gar = ga_ref[:, 0:1] + (gqr * lr - gqi * li) / den
        gai = ga_ref[:, 1:2] + (gqr * li + gqi * lr) / den
        qlr = (qr * lr + qi * li) / den
        qli = (qi * lr - qr * li) / den
        glr = -(gqr * qlr + gqi * qli)
        gli = -(gqi * qlr - gqr * qli)
        glr = glr + dt * (gar * ar + gai * ai)
        gli = gli + dt * (gai * ar - gar * ai)
        wr, wi = _cmul(lr, li, ar, ai)
        gldt = (gar * wr + gai * wi) * dt
        glam_ref[:, 0:1] = glr
        glam_ref[:, 1:2] = gli
        r = lax.broadcasted_iota(jnp.int32, (S5_NS, 32), 0)
        c = lax.broadcasted_iota(jnp.int32, (S5_NS, 32), 1)
        gldt_ref[...] = jnp.sum(jnp.where((r >> 6) == c, gldt, 0.0), axis=0, keepdims=True)
        cm = _c_mask()
        for k in range(2):
            oc = jnp.where(cm, gc_ref[k], 0.0)
            for s in (256, 128, 64):
                oc = oc + pltpu.roll(oc, s, 1)
            gco_ref[k] = oc[:, 0:LANES]

    vm = pl.BlockSpec(memory_space=pltpu.VMEM)
    return pl.pallas_call(
        body, name=name, in_specs=[vm] * 6, out_specs=[vm] * 4,
        out_shape=[_sds((S5_NS, 2)), _sds((1, 32)), _sds((2, S5_NS, LANES)), _sds((2, S5_W, LANES))],
        compiler_params=pltpu.CompilerParams(vmem_limit_bytes=VMEM_LIMIT),
    )(lam_c, ldt_c, b_t, gb, ga_c, gc)


FL_BLK = EVEN_PAD // LANES - 1
Q_BLK, K_BLK, V_BLK = 4, 8, 12
NEG = -1e30


def _log_sigmoid(v):
    return jnp.minimum(v, 0.0) - jnp.log(1.0 + jnp.exp(-jnp.abs(v)))


def _fox_f_fwd(z, bf, name):
    L = z.shape[0]
    tl = min(256, L)

    def body(fl_ref, b_ref, f_ref, fq_ref, carry_ref):
        i = pl.program_id(0)

        @pl.when(i == 0)
        def _():
            carry_ref[...] = jnp.zeros_like(carry_ref)

        lf = _log_sigmoid(fl_ref[...] + b_ref[...])
        r = lax.broadcasted_iota(jnp.int32, (tl, tl), 0)
        c = lax.broadcasted_iota(jnp.int32, (tl, tl), 1)
        tri = (r >= c).astype(F32)
        cs = lax.dot_general(tri, lf, NN, precision=lax.Precision.HIGHEST, preferred_element_type=F32) + carry_ref[...]
        f_ref[...] = cs
        carry_ref[...] = cs[tl - 1:tl, :]
        expand = (lax.broadcasted_iota(jnp.int32, (LANES, FOX_W), 0)
                  == (lax.broadcasted_iota(jnp.int32, (LANES, FOX_W), 1) >> 6)).astype(F32)
        fq_ref[...] = lax.dot_general(cs, expand, NN, precision=lax.Precision.HIGHEST, preferred_element_type=F32)

    return pl.pallas_call(
        body, name=name, grid=(L // tl,),
        in_specs=[pl.BlockSpec((tl, LANES), lambda i: (i, FL_BLK)), pl.BlockSpec((1, LANES), lambda i: (0, 0))],
        out_specs=[pl.BlockSpec((tl, LANES), lambda i: (i, 0)), pl.BlockSpec((tl, FOX_W), lambda i: (i, 0))],
        out_shape=[_sds((L, LANES)), _sds((L, FOX_W))],
        scratch_shapes=[pltpu.VMEM((1, LANES), F32)],
        compiler_params=_cp("arbitrary"),
    )(z, bf)


def _fox_f_bwd(dFk, dfq, z, bf, name):
    L = z.shape[0]
    tl = min(256, L)
    nb = L // tl

    def body(dfk_ref, dfq_ref, fl_ref, b_ref, dfl_ref, db_ref, carry_ref):
        i = pl.program_id(0)

        @pl.when(i == 0)
        def _():
            carry_ref[...] = jnp.zeros_like(carry_ref)
            db_ref[...] = jnp.zeros_like(db_ref)

        sel = (lax.broadcasted_iota(jnp.int32, (FOX_W, LANES), 0)
               == 64 * lax.broadcasted_iota(jnp.int32, (FOX_W, LANES), 1)).astype(F32)
        dfq_h = lax.dot_general(dfq_ref[...], sel, NN, precision=lax.Precision.HIGHEST, preferred_element_type=F32)
        r = lax.broadcasted_iota(jnp.int32, (tl, tl), 0)
        c = lax.broadcasted_iota(jnp.int32, (tl, tl), 1)
        tri = (r <= c).astype(F32)
        cs = lax.dot_general(tri, dfk_ref[...] + dfq_h, NN, precision=lax.Precision.HIGHEST,
                             preferred_element_type=F32) + carry_ref[...]
        carry_ref[...] = cs[0:1, :]
        dfl = cs * _sigmoid(-(fl_ref[...] + b_ref[...]))
        dfl_ref[...] = dfl
        db_ref[...] += jnp.sum(dfl, axis=0, keepdims=True)

    return pl.pallas_call(
        body, name=name, grid=(nb,),
        in_specs=[pl.BlockSpec((tl, LANES), lambda i: (nb - 1 - i, 0)),
                  pl.BlockSpec((tl, FOX_W), lambda i: (nb - 1 - i, 0)),
                  pl.BlockSpec((tl, LANES), lambda i: (nb - 1 - i, FL_BLK)),
                  pl.BlockSpec((1, LANES), lambda i: (0, 0))],
        out_specs=[pl.BlockSpec((tl, LANES), lambda i: (nb - 1 - i, 0)), pl.BlockSpec((1, LANES), lambda i: (0, 0))],
        out_shape=[_sds((L, LANES)), _sds((1, LANES))],
        scratch_shapes=[pltpu.VMEM((1, LANES), F32)],
        compiler_params=_cp("arbitrary"),
    )(dFk, dfq, z, bf)


def _head_mask(hh):
    lane = lax.broadcasted_iota(jnp.int32, (1, LANES), 1)
    return (lane >> 6) == hh


FOX_T = 512


def _fox_head(x, hh):
    return jnp.where(_head_mask(hh), x, 0.0).astype(BF16)


def _fox_scores(qh, k, fq_ref, fr_ref, hh, causal):
    s = _dot(qh, k, NT) + (fq_ref[:, 64 * hh:64 * hh + 1] - fr_ref[hh:hh + 1, :])
    return s if causal is None else jnp.where(causal, s, NEG)


def _causal(T):
    return lax.broadcasted_iota(jnp.int32, (T, T), 1) <= lax.broadcasted_iota(jnp.int32, (T, T), 0)


def _fox_fwd(z, fq, frow, name):
    L = z.shape[0]
    T = min(FOX_T, L)
    nq = L // T

    def body(qt_ref, kt_ref, q_ref, k_ref, v_ref, fq_ref, fr_ref, o_ref, lse_ref, m_ref, l_ref, acc_ref):
        t = pl.program_id(1)
        qi, ki = qt_ref[t], kt_ref[t]

        @pl.when(ki == 0)
        def _():
            m_ref[...] = jnp.full_like(m_ref, NEG)
            l_ref[...] = jnp.zeros_like(l_ref)
            acc_ref[...] = jnp.zeros_like(acc_ref)

        def step(diagonal):
            q = q_ref[...] * 0.125
            k = k_ref[...].astype(BF16)
            v = v_ref[...].astype(BF16)
            causal = _causal(T) if diagonal else None
            s = jnp.concatenate([_fox_scores(_fox_head(q, hh), k, fq_ref, fr_ref, hh, causal) for hh in range(2)],
                                axis=0)
            m_old = m_ref[...]
            m_new = jnp.maximum(m_old, jnp.max(s, axis=1, keepdims=True))
            alpha = jnp.exp(m_old - m_new)
            p = jnp.exp(s - m_new)
            l_ref[...] = alpha * l_ref[...] + jnp.sum(p, axis=1, keepdims=True)
            m_ref[...] = m_new
            acc_ref[...] = alpha * acc_ref[...] + _dot(p.astype(BF16), v)

        @pl.when(ki < qi)
        def _():
            step(False)

        @pl.when(ki == qi)
        def _():
            step(True)
            h0 = _head_mask(0)
            l = l_ref[...]
            o_h = acc_ref[...] / l
            lse_h = m_ref[...] + jnp.log(l)
            o_ref[...] = jnp.where(h0, o_h[:T], o_h[T:])
            lse_ref[...] = jnp.where(h0, lse_h[:T], lse_h[T:])

    pairs = [(qi, ki) for qi in range(nq) for ki in range(qi + 1)]
    qt = jnp.asarray([p[0] for p in pairs], jnp.int32)
    kt = jnp.asarray([p[1] for p in pairs], jnp.int32)

    def qspec(base):
        return pl.BlockSpec((T, LANES), lambda j, t, qt, kt: (qt[t], base + j))

    def kspec(base):
        return pl.BlockSpec((T, LANES), lambda j, t, qt, kt: (kt[t], base + j))

    return pl.pallas_call(
        body, name=name,
        grid_spec=pltpu.PrefetchScalarGridSpec(
            num_scalar_prefetch=2, grid=(4, len(pairs)),
            in_specs=[qspec(Q_BLK), kspec(K_BLK), kspec(V_BLK), qspec(0),
                      pl.BlockSpec((None, 2, T), lambda j, t, qt, kt: (j, 0, kt[t]))],
            out_specs=[qspec(0), qspec(0)],
            scratch_shapes=[pltpu.VMEM((2 * T, 1), F32), pltpu.VMEM((2 * T, 1), F32),
                            pltpu.VMEM((2 * T, LANES), F32)]),
        out_shape=[_sds((L, FOX_W)), _sds((L, FOX_W))],
        compiler_params=_cp("parallel", "arbitrary"),
    )(qt, kt, z, z, z, fq, frow)


def _fox_bwd(z, fq, frow, o, lse, g_m, name):
    L = z.shape[0]
    T = min(FOX_T, L)
    nq = L // T

    pairs = [(qi, ki) for ki in range(nq) for qi in range(ki, nq)]
    qt = jnp.asarray([p[0] for p in pairs], jnp.int32)
    kt = jnp.asarray([p[1] for p in pairs], jnp.int32)

    def body(qt_ref, kt_ref, q_ref, k_ref, v_ref, fq_ref, fr_ref, o_ref, lse_ref, do_ref,
             dq_ref, dk_ref, dv_ref, dfq_ref, dfk_ref, dk_acc, dv_acc, df_acc):
        t = pl.program_id(1)
        qi, ki = qt_ref[t], kt_ref[t]

        @pl.when(t == 0)
        def _():
            dq_ref[...] = jnp.zeros_like(dq_ref)
            dfq_ref[...] = jnp.zeros_like(dfq_ref)

        @pl.when(qi == ki)
        def _():
            dk_acc[...] = jnp.zeros_like(dk_acc)
            dv_acc[...] = jnp.zeros_like(dv_acc)
            df_acc[...] = jnp.zeros_like(df_acc)

        def step(diagonal):
            q = q_ref[...] * 0.125
            qb = q.astype(BF16)
            k = k_ref[...].astype(BF16)
            v = v_ref[...].astype(BF16)
            do = do_ref[...]
            dob = do.astype(BF16)
            do_o = dob.astype(F32) * o_ref[...]
            causal = _causal(T) if diagonal else None
            dvs, dks, dqs, rss = [], [], [], []
            for hh in range(2):
                s = _fox_scores(_fox_head(q, hh), k, fq_ref, fr_ref, hh, causal)
                p = jnp.exp(s - lse_ref[:, 64 * hh:64 * hh + 1])
                dp = _dot(_fox_head(do, hh), v, NT)
                delta = jnp.sum(jnp.where(_head_mask(hh), do_o, 0.0), axis=1, keepdims=True)
                ds = p * (dp - delta)
                dsb = ds.astype(BF16)
                dvs.append(_dot(p.astype(BF16), dob, TN))
                dks.append(_dot(dsb, qb, TN))
                dqs.append(_dot(dsb, k))
                rss.append(jnp.sum(ds, axis=1, keepdims=True))
                df_acc[hh:hh + 1, :] -= jnp.sum(ds, axis=0, keepdims=True)
            h0 = _head_mask(0)
            dv_acc[...] += jnp.where(h0, dvs[0], dvs[1])
            dk_acc[...] += jnp.where(h0, dks[0], dks[1])
            rows = pl.ds(pl.multiple_of(qi * T, T), T)
            dq_ref[rows, :] += jnp.where(h0, dqs[0], dqs[1])
            dfq_ref[rows, :] += jnp.where(h0, rss[0], rss[1])

        @pl.when(qi > ki)
        def _():
            step(False)

        @pl.when(qi == ki)
        def _():
            step(True)

        @pl.when(qi == nq - 1)
        def _():
            dk_ref[...] = dk_acc[...]
            dv_ref[...] = dv_acc[...]
            dfk_ref[...] = df_acc[...]

        @pl.when(t == len(pairs) - 1)
        def _():
            dq_ref[...] = dq_ref[...] * 0.125

    def qside(base):
        return pl.BlockSpec((T, LANES), lambda j, t, qt, kt: (qt[t], base + j))

    def kside(base):
        return pl.BlockSpec((T, LANES), lambda j, t, qt, kt: (kt[t], base + j))

    pair = pl.BlockSpec((L, LANES), lambda j, t, qt, kt: (0, j))
    frow_spec = pl.BlockSpec((None, 2, T), lambda j, t, qt, kt: (j, 0, kt[t]))
    return pl.pallas_call(
        body, name=name,
        grid_spec=pltpu.PrefetchScalarGridSpec(
            num_scalar_prefetch=2, grid=(4, len(pairs)),
            in_specs=[qside(Q_BLK), kside(K_BLK), kside(V_BLK), qside(0), frow_spec, qside(0), qside(0), qside(4)],
            out_specs=[pair, kside(0), kside(0), pair, frow_spec],
            scratch_shapes=[pltpu.VMEM((T, LANES), F32), pltpu.VMEM((T, LANES), F32), pltpu.VMEM((2, T), F32)]),
        out_shape=[_sds((L, FOX_W)), _sds((L, FOX_W)), _sds((L, FOX_W)), _sds((L, FOX_W)), _sds((4, 2, L))],
        compiler_params=_cp("parallel", "arbitrary"),
    )(qt, kt, z, z, z, fq, frow, o, lse, g_m)


def _shift_rows(v, s, down, row):
    n = v.shape[0]
    if down:
        return jnp.where(row >= s, pltpu.roll(v, s, 0), 0.0)
    return jnp.where(row < n - s, pltpu.roll(v, n - s, 0), 0.0)


def _window_sum(v, g, down, row):
    out = jnp.zeros_like(v)
    s = v
    for k in range(4):
        s = s + _shift_rows(s, 1 << k, down, row)
        out = jnp.where(g == k, s, out)
    return out


def _pool_inv_cnt(g, row):
    w = jnp.left_shift(2, g).astype(F32)
    return 1.0 / jnp.minimum(row.astype(F32) + 1.0, w)


def _pool_fwd(z, pool_w, scale, name):
    L = z.shape[0]

    def body(x_ref, w_ref, s_ref, y_ref, p_ref):
        g = pl.program_id(0)
        row = lax.broadcasted_iota(jnp.int32, (L, LANES), 0)
        x = x_ref[...]
        pooled = (_window_sum(x, g, True, row) * _pool_inv_cnt(g, row) - x).astype(BF16)
        p_ref[...] = pooled
        y_ref[...] = (_dot(pooled, w_ref[...].astype(BF16)) * s_ref[...]).astype(BF16)

    col = pl.BlockSpec((L, LANES), lambda g: (0, g))
    return pl.pallas_call(
        body, name=name, grid=(4,),
        in_specs=[col, pl.BlockSpec((None, LANES, LANES), lambda g: (g, 0, 0)), pl.BlockSpec((1, LANES), lambda g: (0, g))],
        out_specs=[col, col],
        out_shape=[_sds((L, 512), BF16), _sds((L, 512), BF16)],
        compiler_params=_cp("parallel"),
    )(z, pool_w, scale)


def _pool_bwd(g_m, pooled, pool_w, scale, name):
    L = g_m.shape[0]

    def body(g_ref, p_ref, w_ref, s_ref, gx_ref, gw_ref, gs_ref):
        g = pl.program_id(0)
        row = lax.broadcasted_iota(jnp.int32, (L, LANES), 0)
        gy = g_ref[...]
        pooled = p_ref[...]
        wb = w_ref[...].astype(BF16)
        lin = _dot(pooled, wb)
        gs_ref[...] = jnp.sum(gy * lin, axis=0, keepdims=True)
        glin = (gy * s_ref[...]).astype(BF16)
        gw_ref[...] = _dot(pooled, glin, TN)
        gp = _dot(glin, wb, NT)
        gx_ref[...] = _window_sum(gp * _pool_inv_cnt(g, row), g, False, row) - gp

    col = pl.BlockSpec((L, LANES), lambda g: (0, g))
    wspec = pl.BlockSpec((None, LANES, LANES), lambda g: (g, 0, 0))
    vec = pl.BlockSpec((1, LANES), lambda g: (0, g))
    return pl.pallas_call(
        body, name=name, grid=(4,),
        in_specs=[col, col, wspec, vec],
        out_specs=[col, wspec, vec],
        out_shape=[_sds((L, 512)), _sds((4, LANES, LANES)), _sds((1, 512))],
        compiler_params=_cp("parallel"),
    )(g_m, pooled, pool_w, scale)


SGU_CHUNKS = 4


def _sgu_ln(v, gam, bet):
    gv = _gelu(v)
    mu = jnp.mean(gv, axis=-1, keepdims=True)
    xc = gv - mu
    rs = lax.rsqrt(jnp.mean(xc * xc, axis=-1, keepdims=True) + EPS)
    xh = xc * rs
    return xh, rs, xh * gam + bet


def _tril_ws(w_ref, g):
    r = lax.broadcasted_iota(jnp.int32, (LANES, LANES), 0)
    c = lax.broadcasted_iota(jnp.int32, (LANES, LANES), 1)
    return jnp.where(r >= c, w_ref[g], 0.0).astype(BF16)


def _sgu_fwd(z, ln_g, ln_b, w_s, b_st, name):
    L = z.shape[0]
    rb = min(SGU_CHUNKS * LANES, L)

    def body(u_ref, v_ref, g_ref, b_ref, w_ref, bs_ref, y_ref):
        _, _, vln = _sgu_ln(v_ref[...], g_ref[...], b_ref[...])
        gu = _gelu(u_ref[...])
        vb = vln.astype(BF16)
        for g in range(4):
            ws = _tril_ws(w_ref, g)
            for n in range(rb // LANES):
                rows = slice(n * LANES, (n + 1) * LANES)
                cols = slice(g * LANES, (g + 1) * LANES)
                mixed = _dot(ws, vb[rows, cols]) + bs_ref[:, g:g + 1]
                y_ref[rows, cols] = (gu[rows, cols] * mixed).astype(BF16)

    vm = lambda shape: pl.BlockSpec(shape, lambda i: tuple(0 for _ in shape))
    return pl.pallas_call(
        body, name=name, grid=(L // rb,),
        in_specs=[pl.BlockSpec((rb, 512), lambda i: (i, 1)), pl.BlockSpec((rb, 512), lambda i: (i, 2)),
                  vm((1, 512)), vm((1, 512)), vm((4, LANES, LANES)), vm((LANES, 4))],
        out_specs=pl.BlockSpec((rb, 512), lambda i: (i, 0)),
        out_shape=_sds((L, 512), BF16),
        compiler_params=_cp("parallel"),
    )(z, z, ln_g, ln_b, w_s, b_st)


def _sgu_bwd(g_m, z, ln_g, ln_b, w_s, b_st, name):
    L = z.shape[0]
    rb = min(SGU_CHUNKS * LANES, L)

    def body(gy_ref, u_ref, v_ref, g_ref, b_ref, w_ref, bs_ref, gu_ref, gv_ref, gw_ref, gbs_ref, gg_ref, gb_ref):
        i = pl.program_id(0)

        @pl.when(i == 0)
        def _():
            gw_ref[...] = jnp.zeros_like(gw_ref)
            gbs_ref[...] = jnp.zeros_like(gbs_ref)
            gg_ref[...] = jnp.zeros_like(gg_ref)
            gb_ref[...] = jnp.zeros_like(gb_ref)

        v = v_ref[...]
        u = u_ref[...]
        gy = gy_ref[...]
        xh, rs, vln = _sgu_ln(v, g_ref[...], b_ref[...])
        gel_u = _gelu(u)
        gmix = gy * gel_u
        vb = vln.astype(BF16)
        gmb = gmix.astype(BF16)
        r = lax.broadcasted_iota(jnp.int32, (LANES, LANES), 0)
        c = lax.broadcasted_iota(jnp.int32, (LANES, LANES), 1)
        gvln_cols = []
        for g in range(4):
            ws = _tril_ws(w_ref, g)
            cols = slice(g * LANES, (g + 1) * LANES)
            gw = jnp.zeros((LANES, LANES), F32)
            gbs = jnp.zeros((LANES, 1), F32)
            parts = []
            for n in range(rb // LANES):
                rows = slice(n * LANES, (n + 1) * LANES)
                mixed = _dot(ws, vb[rows, cols]) + bs_ref[:, g:g + 1]
                gu_ref[rows, cols] = gy[rows, cols] * mixed * _gelu_grad(u[rows, cols])
                parts.append(_dot(ws, gmb[rows, cols], TN))
                gw = gw + _dot(gmb[rows, cols], vb[rows, cols], NT)
                gbs = gbs + jnp.sum(gmix[rows, cols], axis=1, keepdims=True)
            gvln_cols.append(jnp.concatenate(parts, axis=0))
            gw_ref[g] += jnp.where(r >= c, gw, 0.0)
            gbs_ref[:, g:g + 1] += gbs
        gvln = jnp.concatenate(gvln_cols, axis=1)
        gg_ref[...] += jnp.sum(gvln * xh, axis=0, keepdims=True)
        gb_ref[...] += jnp.sum(gvln, axis=0, keepdims=True)
        gxh = gvln * g_ref[...]
        ggv = rs * (gxh - jnp.mean(gxh, axis=-1, keepdims=True) - xh * jnp.mean(gxh * xh, axis=-1, keepdims=True))
        gv_ref[...] = ggv * _gelu_grad(v)

    vm = lambda shape: pl.BlockSpec(shape, lambda i: tuple(0 for _ in shape))
    blk = pl.BlockSpec((rb, 512), lambda i: (i, 0))
    return pl.pallas_call(
        body, name=name, grid=(L // rb,),
        in_specs=[pl.BlockSpec((rb, 512), lambda i: (i, 1)), pl.BlockSpec((rb, 512), lambda i: (i, 1)),
                  pl.BlockSpec((rb, 512), lambda i: (i, 2)),
                  vm((1, 512)), vm((1, 512)), vm((4, LANES, LANES)), vm((LANES, 4))],
        out_specs=[blk, blk, vm((4, LANES, LANES)), vm((LANES, 4)), vm((1, 512)), vm((1, 512))],
        out_shape=[_sds((L, 512)), _sds((L, 512)), _sds((4, LANES, LANES)), _sds((LANES, 4)),
                   _sds((1, 512)), _sds((1, 512))],
        compiler_params=_cp("arbitrary"),
    )(g_m, z, z, ln_g, ln_b, w_s, b_st)


def _adamw_math(w, g, m, v):
    nm = ADAM_B1 * m + (1.0 - ADAM_B1) * g
    nv = ADAM_B2 * v + (1.0 - ADAM_B2) * (g * g)
    m_hat = nm / (1.0 - ADAM_B1 ** ADAM_STEP)
    v_hat = nv / (1.0 - ADAM_B2 ** ADAM_STEP)
    delta = -ADAM_LR * (m_hat / (jnp.sqrt(v_hat) + ADAM_EPS) + ADAM_WD * w)
    return delta, nm, nv


def _sum_adamw(parts, w, m, v, name, layer=0, prev=None):
    n_layers, R, C = w.shape
    rb = 128 if R % 128 == 0 else R

    def body(p_ref, w_ref, m_ref, v_ref, *rest):
        g_ref, d_ref, nm_ref, nv_ref = rest[-4:]
        g = p_ref[0].astype(F32)
        for s in range(1, N_DEV):
            g = g + p_ref[s].astype(F32)
        d, nm, nv = _adamw_math(w_ref[...], g, m_ref[...], v_ref[...])
        g_ref[...] = g
        d_ref[...] = d
        nm_ref[...] = nm
        nv_ref[...] = nv

    blk = pl.BlockSpec((None, rb, C), lambda i: (layer, i, 0))
    prev = [] if prev is None else list(prev)
    return pl.pallas_call(
        body, name=name, grid=(R // rb,),
        in_specs=[pl.BlockSpec((N_DEV, rb, C), lambda i: (0, i, 0)), blk, blk, blk] + [ANY] * len(prev),
        out_specs=[blk] * 4, out_shape=[_sds((n_layers, R, C))] * 4,
        input_output_aliases={4 + k: k for k in range(len(prev))},
        compiler_params=_cp("parallel"),
    )(parts, w, m, v, *prev)


def _sum_pieces(parts, name):
    _, R, C = parts.shape

    def body(p_ref, g_ref):
        g = p_ref[0]
        for s in range(1, N_DEV):
            g = g + p_ref[s]
        g_ref[...] = g

    vm = pl.BlockSpec(memory_space=pltpu.VMEM)
    return pl.pallas_call(body, name=name, in_specs=[vm], out_specs=vm, out_shape=_sds((R, C)))(parts)


def _adamw_many(ws, gs, ms, vs, name):
    n = len(ws)
    vm = pl.BlockSpec(memory_space=pltpu.VMEM)

    def body(*refs):
        w_refs, g_refs, m_refs, v_refs = refs[:n], refs[n:2 * n], refs[2 * n:3 * n], refs[3 * n:4 * n]
        d_refs, nm_refs, nv_refs = refs[4 * n:5 * n], refs[5 * n:6 * n], refs[6 * n:7 * n]
        for i in range(n):
            d, nm, nv = _adamw_math(w_refs[i][...], g_refs[i][...], m_refs[i][...], v_refs[i][...])
            d_refs[i][...] = d
            nm_refs[i][...] = nm
            nv_refs[i][...] = nv

    shapes = [_sds(w.shape) for w in ws]
    outs = pl.pallas_call(
        body, name=name, in_specs=[vm] * (4 * n), out_specs=[vm] * (3 * n), out_shape=shapes * 3,
        compiler_params=pltpu.CompilerParams(vmem_limit_bytes=VMEM_LIMIT),
    )(*ws, *gs, *ms, *vs)
    return list(outs[:n]), list(outs[n:2 * n]), list(outs[2 * n:])


def _mesh_pos():
    return lax.axis_index("x"), lax.axis_index("y"), lax.axis_index("c")


def _dev_index(p):
    return 4 * p[0] + 2 * p[1] + p[2]


HBM = pl.BlockSpec(memory_space=pltpu.HBM)
SEM = pl.BlockSpec(memory_space=pltpu.SEMAPHORE)
EFFECT = pltpu.SideEffectType.DATAFLOW_SIDE_EFFECTING


def _peer_list():
    x, y, c = _mesh_pos()
    peers = [(x ^ dx, y ^ dy, c ^ dc) for dx in range(2) for dy in range(2) for dc in range(2)][1:]
    return (x, y, c), peers


def _split_copy(src_ref, land_ref, send_sems, recv_sems, i, k, peer, slot, exchange):
    return pltpu.make_async_remote_copy(
        src_ref=src_ref.at[_dev_index(peer)] if exchange else src_ref, dst_ref=land_ref.at[slot],
        send_sem=send_sems.at[7 * i + k], recv_sem=recv_sems.at[7 * i + k], device_id=peer, device_id_type=MESH)


def _comm_start(groups, name, exchange, dep=None):
    sizes = [len(g) for g in groups]
    n = sum(sizes)
    srcs = [a for g in groups for a in g]
    my_index = _dev_index(_mesh_pos())
    lands = []
    for a in srcs:
        if exchange:
            own = lax.dynamic_slice(a, (my_index, 0, 0), (1,) + a.shape[1:])
            shape = a.shape
        else:
            own = a[None]
            shape = (N_DEV,) + a.shape
        lands.append(lax.dynamic_update_slice(lax.empty(shape, a.dtype), own, (my_index, 0, 0)))

    n_dep = 0 if dep is None else 1

    def body(*refs):
        src_refs, land_refs = refs[:n], refs[n:2 * n]
        sem_refs = refs[2 * n + n_dep:2 * n + n_dep + 2 * len(sizes)]
        token_ref = refs[-1]
        me, peers = _peer_list()
        mi = _dev_index(me)
        i = 0
        for gi, sz in enumerate(sizes):
            for j in range(sz):
                for k, peer in enumerate(peers):
                    _split_copy(src_refs[i], land_refs[i], sem_refs[2 * gi], sem_refs[2 * gi + 1], j, k, peer, mi,
                                exchange).start()
                i += 1
        token_ref[...] = jnp.zeros_like(token_ref)

    sem_shapes = []
    for sz in sizes:
        sem_shapes += [pltpu.SemaphoreType.DMA((7 * sz,)), pltpu.SemaphoreType.DMA((7 * sz,))]
    thru = [pltpu.HBM(a.shape, a.dtype) for a in srcs + lands]
    n_sem = len(sem_shapes)
    outs = pl.pallas_call(
        body, name=name,
        out_shape=tuple(sem_shapes + thru + [_sds((8, LANES))]),
        in_specs=[HBM] * (2 * n) + [ANY] * n_dep,
        out_specs=tuple([SEM] * n_sem + [HBM] * (2 * n) + [pl.BlockSpec(memory_space=pltpu.VMEM)]),
        input_output_aliases={i: n_sem + i for i in range(2 * n)},
        compiler_params=pltpu.CompilerParams(has_side_effects=EFFECT),
    )(*[pltpu.with_memory_space_constraint(a, pltpu.HBM) for a in srcs + lands], *([] if dep is None else [dep]))
    sems, thru_src, thru_land, token = outs[:n_sem], outs[n_sem:n_sem + n], outs[n_sem + n:n_sem + 2 * n], outs[-1]
    result, off = [], 0
    for gi, sz in enumerate(sizes):
        result.append((sems[2 * gi], sems[2 * gi + 1], list(thru_src[off:off + sz]), list(thru_land[off:off + sz])))
        off += sz
    return result, token


def _comm_wait(group, after, name, exchange):
    send_sems, recv_sems, srcs, lands = group
    n = len(srcs)
    after = list(after) if isinstance(after, (list, tuple)) else [after]

    def body(*refs):
        src_refs, land_refs = refs[:n], refs[n:2 * n]
        ssem, rsem = refs[2 * n], refs[2 * n + 1]
        me, peers = _peer_list()
        for i in range(n):
            for k, peer in enumerate(peers):
                cp = _split_copy(src_refs[i], land_refs[i], ssem, rsem, i, k, peer, _dev_index(peer), exchange)
                cp.wait_send()
                cp.wait_recv()

    outs = pl.pallas_call(
        body, name=name,
        out_shape=tuple(pltpu.HBM(a.shape, a.dtype) for a in srcs + lands),
        in_specs=[HBM] * (2 * n) + [SEM, SEM] + [ANY] * len(after),
        out_specs=tuple([HBM] * (2 * n)),
        input_output_aliases={i: i for i in range(2 * n)},
        compiler_params=pltpu.CompilerParams(has_side_effects=EFFECT),
    )(*srcs, *lands, send_sems, recv_sems, *after)
    return list(outs[n:])


def _tie(a, token):
    return a + token[0, 0].astype(a.dtype)


def _pack(arrs, rows):
    flat = jnp.concatenate([a.reshape(-1).astype(F32) for a in arrs])
    return jnp.pad(flat, (0, rows * LANES - flat.shape[0])).reshape(rows, LANES)


def _unpack(packed, shapes):
    flat = packed.reshape(-1)
    out, off = [], 0
    for s in shapes:
        n = math.prod(s)
        out.append(flat[off:off + n].reshape(s))
        off += n
    return out


def _packed_rows(shapes):
    n = sum(math.prod(s) for s in shapes)
    unit = N_DEV * 8 * LANES
    return -(-n // unit) * unit // LANES


def kernel(x, mix_pre_g, mix_post_g, mlp_pre_g, mlp_post_g, w_in_even, s5_lam_re, s5_lam_im, s5_log_dt, s5_b_re, s5_b_im, s5_c_re, s5_c_im, s5_d, s5_w_glu, fox_b_f, w_out_even, w_in_odd, pool_w, pool_scale, sgu_ln_g, sgu_ln_b, sgu_w_s, sgu_b_s, w_out_odd, mlp_w1, mlp_w2, loss_target, m_mix_pre_g, m_mix_post_g, m_mlp_pre_g, m_mlp_post_g, m_w_in_even, m_s5_lam_re, m_s5_lam_im, m_s5_log_dt, m_s5_b_re, m_s5_b_im, m_s5_c_re, m_s5_c_im, m_s5_d, m_s5_w_glu, m_fox_b_f, m_w_out_even, m_w_in_odd, m_pool_w, m_pool_scale, m_sgu_ln_g, m_sgu_ln_b, m_sgu_w_s, m_sgu_b_s, m_w_out_odd, m_mlp_w1, m_mlp_w2, v_mix_pre_g, v_mix_post_g, v_mlp_pre_g, v_mlp_post_g, v_w_in_even, v_s5_lam_re, v_s5_lam_im, v_s5_log_dt, v_s5_b_re, v_s5_b_im, v_s5_c_re, v_s5_c_im, v_s5_d, v_s5_w_glu, v_fox_b_f, v_w_out_even, v_w_in_odd, v_pool_w, v_pool_scale, v_sgu_ln_g, v_sgu_ln_b, v_sgu_w_s, v_sgu_b_s, v_w_out_odd, v_mlp_w1, v_mlp_w2):
    weights = dict(mix_pre_g=mix_pre_g, mix_post_g=mix_post_g, mlp_pre_g=mlp_pre_g, mlp_post_g=mlp_post_g, w_in_even=w_in_even, s5_lam_re=s5_lam_re, s5_lam_im=s5_lam_im, s5_log_dt=s5_log_dt, s5_b_re=s5_b_re, s5_b_im=s5_b_im, s5_c_re=s5_c_re, s5_c_im=s5_c_im, s5_d=s5_d, s5_w_glu=s5_w_glu, fox_b_f=fox_b_f, w_out_even=w_out_even, w_in_odd=w_in_odd, pool_w=pool_w, pool_scale=pool_scale, sgu_ln_g=sgu_ln_g, sgu_ln_b=sgu_ln_b, sgu_w_s=sgu_w_s, sgu_b_s=sgu_b_s, w_out_odd=w_out_odd, mlp_w1=mlp_w1, mlp_w2=mlp_w2)
    mom_m = dict(mix_pre_g=m_mix_pre_g, mix_post_g=m_mix_post_g, mlp_pre_g=m_mlp_pre_g, mlp_post_g=m_mlp_post_g, w_in_even=m_w_in_even, s5_lam_re=m_s5_lam_re, s5_lam_im=m_s5_lam_im, s5_log_dt=m_s5_log_dt, s5_b_re=m_s5_b_re, s5_b_im=m_s5_b_im, s5_c_re=m_s5_c_re, s5_c_im=m_s5_c_im, s5_d=m_s5_d, s5_w_glu=m_s5_w_glu, fox_b_f=m_fox_b_f, w_out_even=m_w_out_even, w_in_odd=m_w_in_odd, pool_w=m_pool_w, pool_scale=m_pool_scale, sgu_ln_g=m_sgu_ln_g, sgu_ln_b=m_sgu_ln_b, sgu_w_s=m_sgu_w_s, sgu_b_s=m_sgu_b_s, w_out_odd=m_w_out_odd, mlp_w1=m_mlp_w1, mlp_w2=m_mlp_w2)
    mom_v = dict(mix_pre_g=v_mix_pre_g, mix_post_g=v_mix_post_g, mlp_pre_g=v_mlp_pre_g, mlp_post_g=v_mlp_post_g, w_in_even=v_w_in_even, s5_lam_re=v_s5_lam_re, s5_lam_im=v_s5_lam_im, s5_log_dt=v_s5_log_dt, s5_b_re=v_s5_b_re, s5_b_im=v_s5_b_im, s5_c_re=v_s5_c_re, s5_c_im=v_s5_c_im, s5_d=v_s5_d, s5_w_glu=v_s5_w_glu, fox_b_f=v_fox_b_f, w_out_even=v_w_out_even, w_in_odd=v_w_in_odd, pool_w=v_pool_w, pool_scale=v_pool_scale, sgu_ln_g=v_sgu_ln_g, sgu_ln_b=v_sgu_ln_b, sgu_w_s=v_sgu_w_s, sgu_b_s=v_sgu_b_s, w_out_odd=v_w_out_odd, mlp_w1=v_mlp_w1, mlp_w2=v_mlp_w2)
    names = list(weights)
    L = x.shape[1]
    x0 = x[0]
    target = loss_target[0]
    my_index = 4 * lax.axis_index("x") + 2 * lax.axis_index("y") + lax.axis_index("c")

    small_vec = jnp.zeros((8, LANES), F32)
    small_vec = small_vec.at[0, :64].set(pool_scale[0]).at[1, :64].set(sgu_ln_g[0]).at[2, :64].set(sgu_ln_b[0])
    ag_groups, ag_token = _comm_start(
        [[jnp.transpose(w_in_even[0]).astype(BF16), small_vec],
         [s5_w_glu[0].astype(BF16), w_out_even[0].astype(BF16)],
         [mlp_w1[0].astype(BF16), mlp_w2[0].astype(BF16)],
         [jnp.transpose(w_in_odd[0]).astype(BF16), w_out_odd[0].astype(BF16), mlp_w1[1].astype(BF16), mlp_w2[1].astype(BF16)]],
        "ag_start", exchange=False)

    lam_r = jnp.concatenate([s5_lam_re.reshape(1, S5_NS), s5_lam_im.reshape(1, S5_NS)], axis=0)
    ldt_r = jnp.repeat(s5_log_dt.reshape(32), 64).reshape(1, S5_NS)
    lam_c = jnp.transpose(lam_r)
    ldt_c = jnp.transpose(ldt_r)
    b_t = jnp.stack([jnp.tile(s5_b_re.reshape(S5_NS, 16), (1, 8)), jnp.tile(s5_b_im.reshape(S5_NS, 16), (1, 8))])
    c_t = jnp.stack([jnp.tile(s5_c_re.reshape(S5_W, 64), (1, 8)), jnp.tile(s5_c_im.reshape(S5_W, 64), (1, 8))])
    bf_pad = jnp.pad(fox_b_f, ((0, 0), (0, LANES - 8)))
    b_st = jnp.transpose(sgu_b_s[0])

    h0, rx0 = _rms_fwd(x0, _tie(mix_pre_g[0:1], ag_token), "rms0")
    tabs, bset, cset = _s5_prep(lam_r, ldt_r, lam_c, ldt_c, b_t, c_t, "s5_prep")
    ag0 = _comm_wait(ag_groups[0], tabs, "ag_wait0", exchange=False)
    winT_e = jnp.pad(ag0[0].reshape(EVEN_IN, D_MODEL), ((0, EVEN_PAD - EVEN_IN), (0, 0)))
    pool_scale_f = ag0[1][:, 0, :64].reshape(1, 512)
    ln_g_f = ag0[1][:, 1, :64].reshape(1, 512)
    ln_b_f = ag0[1][:, 2, :64].reshape(1, 512)
    z0 = _mm(h0, winT_e, name="win_even", tb=True, bm=512, bn=EVEN_PAD)
    xs = _s5_scan_fwd(z0, bset, tabs, "s5_scan")
    ag1 = _comm_wait(ag_groups[1], xs, "ag_wait1", exchange=False)
    wglu = ag1[0].reshape(S5_W, S5_W)
    wout_e = ag1[1].reshape(D_MODEL, D_MODEL)
    ylin, ya = _s5_out_fwd(xs, cset, z0, s5_d, wglu, "s5_out")
    fcum, fq = _fox_f_fwd(z0, bf_pad, "fox_f")
    frow = jnp.transpose(fcum[:, :8]).reshape(4, 2, L)
    o_att, lse = _fox_fwd(z0, fq, frow, "fox_fwd")
    mix0 = [ya, o_att]
    y0 = _mm(mix0, wout_e, name="wout_even")
    x1, ry0, h1, rx1 = _post_pre_fwd(x0, y0, mix_post_g[0:1], mlp_pre_g[0:1], "post0")
    ag2 = _comm_wait(ag_groups[2], rx1, "ag_wait2", exchange=False)
    w1 = [ag2[0], None]
    w2 = [ag2[1].reshape(4 * D_MODEL, D_MODEL), None]
    p0, a0 = _mm(h1, w1[0], name="mlp0_w1", b3=True, out_dtypes=(BF16, BF16), epi=_epi_relu2, bm=2048)
    o0 = _mm(a0, w2[0], name="mlp0_w2", bm=2048)
    x2, ro0, h2, rx2 = _post_pre_fwd(x1, o0, mlp_post_g[0:1], mix_pre_g[1:2], "post1")
    ag3 = _comm_wait(ag_groups[3], rx2, "ag_wait3", exchange=False)
    winT_o = ag3[0].reshape(ODD_IN, D_MODEL)
    wout_o = ag3[1].reshape(D_MODEL, D_MODEL)
    w1[1] = ag3[2]
    w2[1] = ag3[3].reshape(4 * D_MODEL, D_MODEL)
    z1 = _mm(h2, winT_o, name="win_odd", tb=True, bn=ODD_IN)
    yc, pooled = _pool_fwd(z1, pool_w[0], pool_scale_f, "pool_fwd")
    yd = _sgu_fwd(z1, ln_g_f, ln_b_f, sgu_w_s[0], b_st, "sgu_fwd")
    mix1 = [yc, yd]
    y1 = _mm(mix1, wout_o, name="wout_odd")
    x3, ry1, h3, rx3 = _post_pre_fwd(x2, y1, mix_post_g[1:2], mlp_pre_g[1:2], "post2")
    p1, a1 = _mm(h3, w1[1], name="mlp1_w1", b3=True, out_dtypes=(BF16, BF16), epi=_epi_relu2, bm=2048)
    o1 = _mm(a1, w2[1], name="mlp1_w2", bm=2048)
    gx4, ro1, sq = _post_loss_fwd(x3, o1, mlp_post_g[1:2], target, "post3")

    g_o1, gg_mlp_post1 = _post_bwd(gx4, o1, ro1, mlp_post_g[1:2], "bpost3")
    g_p1 = _mm(g_o1, w2[1], name="b_mlp1_a", tb=True, out_dtypes=(BF16,), epi=_epi_relu2_bwd, extra=(p1,), bm=2048)
    gw2_1 = _mm(a1, g_o1, name="b_mlp1_w2", ta=True, bm=2048)
    g_h3 = _mm(g_p1, w1[1], name="b_mlp1_h", tb=True, b3=True, bm=2048)
    gw1_1 = _mm(h3, g_p1, name="b_mlp1_w1", ta=True, out3=True, bn=2048)
    (ex1,), tok1 = _comm_start([[gw1_1, gw2_1.reshape(N_DEV, 512, D_MODEL)]], "ex_start1", exchange=True)
    g_x3, gg_mlp_pre1, g_y1, gg_mix_post1 = _pre_post_bwd(g_h3, x3, rx3, _tie(mlp_pre_g[1:2], tok1), gx4, y1, ry1, mix_post_g[1:2], "bpre3")
    g_mix1 = _mm(g_y1, wout_o, name="b_wout_odd_m", tb=True)
    gwout_o = _mm(mix1, g_y1, name="b_wout_odd_w", ta=True)
    g_xc, g_pool_w, g_pool_scale = _pool_bwd(g_mix1, pooled, pool_w[0], pool_scale_f, "pool_bwd")
    g_u1, g_v1, g_ws, g_bst, g_ln_g, g_ln_b = _sgu_bwd(g_mix1, z1, ln_g_f, ln_b_f, sgu_w_s[0], b_st, "sgu_bwd")
    g_z1 = [g_xc, g_u1, g_v1]
    g_h2 = _mm(g_z1, winT_o, name="b_win_odd_h")
    gwinT_o = _mm(g_z1, h2, name="b_win_odd_w", ta=True)
    (ex2,), tok2 = _comm_start([[gwout_o.reshape(N_DEV, 128, D_MODEL), gwinT_o.reshape(N_DEV, ODD_IN // N_DEV, D_MODEL)]], "ex_start2", exchange=True)
    g_x2, gg_mix_pre1, g_o0, gg_mlp_post0 = _pre_post_bwd(g_h2, x2, rx2, _tie(mix_pre_g[1:2], tok2), g_x3, o0, ro0, mlp_post_g[0:1], "bpre2")
    g_p0 = _mm(g_o0, w2[0], name="b_mlp0_a", tb=True, out_dtypes=(BF16,), epi=_epi_relu2_bwd, extra=(p0,), bm=2048)
    gw2_0 = _mm(a0, g_o0, name="b_mlp0_w2", ta=True, bm=2048)
    g_h1 = _mm(g_p0, w1[0], name="b_mlp0_h", tb=True, b3=True, bm=2048)
    gw1_0 = _mm(h1, g_p0, name="b_mlp0_w1", ta=True, out3=True, bn=2048)
    (ex3,), tok3 = _comm_start([[gw1_0, gw2_0.reshape(N_DEV, 512, D_MODEL)]], "ex_start3", exchange=True)
    g_x1, gg_mlp_pre0, g_y0, gg_mix_post0 = _pre_post_bwd(g_h1, x1, rx1, _tie(mlp_pre_g[0:1], tok3), g_x2, y0, ry0, mix_post_g[0:1], "bpre1")
    g_mix0 = _mm(g_y0, wout_e, name="b_wout_even_m", tb=True)
    gwout_e = _mm(mix0, g_y0, name="b_wout_even_w", ta=True)
    gyl, gud, g_wglu, g_d = _s5_glu_bwd(g_mix0, ylin, z0, s5_d, wglu, "s5_glu_bwd")
    (ex4,), tok4 = _comm_start([[gwout_e.reshape(N_DEV, 128, D_MODEL), g_wglu.reshape(N_DEV, 64, S5_W)]], "ex_start4", exchange=True)
    g_u0, ga, gb_raw, gc_raw = _s5_scan_bwd(gyl, _tie(cset, tok4), xs, z0, bset, gud, tabs, "s5_scan_bwd")
    g_lam, g_ldt, g_b, g_c = _s5_param_bwd(lam_c, ldt_c, b_t, gb_raw, jnp.transpose(ga), gc_raw, "s5_param_bwd")
    dq, dk, dv, dfq, dfrow = _fox_bwd(z0, fq, frow, o_att, lse, g_mix0, "fox_bwd")
    dFk = jnp.pad(jnp.transpose(dfrow.reshape(8, L)), ((0, 0), (0, LANES - 8)))
    dfl, db_f = _fox_f_bwd(dFk, dfq, z0, bf_pad, "fox_f_bwd")
    g_z0 = [g_u0, dq, dk, dv, dfl]
    g_h0 = _mm(g_z0, winT_e, name="b_win_even_h")
    grad_x, gg_mix_pre0 = _pre_bwd(g_h0, x0, rx0, mix_pre_g[0:1], g_x1, "bpre0")

    small_grads = dict(
        mix_pre_g=jnp.concatenate([gg_mix_pre0, gg_mix_pre1]), mix_post_g=jnp.concatenate([gg_mix_post0, gg_mix_post1]),
        mlp_pre_g=jnp.concatenate([gg_mlp_pre0, gg_mlp_pre1]), mlp_post_g=jnp.concatenate([gg_mlp_post0, gg_mlp_post1]),
        s5_lam_re=g_lam[:, 0], s5_lam_im=g_lam[:, 1], s5_log_dt=g_ldt,
        s5_b_re=g_b[0, :, :16], s5_b_im=g_b[1, :, :16], s5_c_re=g_c[0, :, :64], s5_c_im=g_c[1, :, :64],
        s5_d=g_d, fox_b_f=db_f[:, :8], pool_w=g_pool_w, sgu_w_s=g_ws, sgu_b_s=jnp.transpose(g_bst),
        pool_scale=g_pool_scale, sgu_ln_g=g_ln_g, sgu_ln_b=g_ln_b)
    small_names = list(small_grads)
    full_shapes = [(512,) if nm in ("pool_scale", "sgu_ln_g", "sgu_ln_b") else weights[nm].shape for nm in small_names]
    full_shapes.append((1, 1))
    rows = _packed_rows(full_shapes)
    packed = _pack([small_grads[nm] for nm in small_names] + [sq], rows).reshape(N_DEV, rows // N_DEV, LANES)
    (exs,), tok_s = _comm_start([[packed]], "exs_start", exchange=True)
    gwinT_e = _mm(g_z0, h0, name="b_win_even_w", ta=True, bk=512, dep=tok_s)
    (recv_small,) = _comm_wait(exs, gwinT_e, "exs_wait", exchange=True)
    piece = _sum_pieces(recv_small, "sum_small")
    (ags,), tok_a = _comm_start([[piece]], "ags_start", exchange=False)

    gwinT_e_pieces = gwinT_e[:EVEN_IN].reshape(N_DEV, EVEN_IN // N_DEV, D_MODEL).astype(BF16)
    (ex5,), tok5 = _comm_start([[gwinT_e_pieces]], "ex_start5", exchange=True, dep=tok_a)
    r_w1_1, r_w2_1 = _comm_wait(ex1, tok5, "ex_wait1", exchange=True)
    r_wout_o, r_win_o = _comm_wait(ex2, tok5, "ex_wait2", exchange=True)
    r_w1_0, r_w2_0 = _comm_wait(ex3, tok5, "ex_wait3", exchange=True)
    r_wout_e, r_wglu = _comm_wait(ex4, tok5, "ex_wait4", exchange=True)

    res = {}
    for nm, parts in (("mlp_w1", (r_w1_0, r_w1_1)), ("mlp_w2", (r_w2_0, r_w2_1))):
        first = _sum_adamw(parts[0], weights[nm], mom_m[nm], mom_v[nm], "adamw_%s_0" % nm, layer=0)
        res[nm] = tuple(_sum_adamw(parts[1], weights[nm], mom_m[nm], mom_v[nm], "adamw_%s_1" % nm, layer=1, prev=first))
    big_parts = dict(s5_w_glu=r_wglu, w_out_even=r_wout_e, w_out_odd=r_wout_o)
    for nm, parts in big_parts.items():
        res[nm] = tuple(_sum_adamw(parts, weights[nm], mom_m[nm], mom_v[nm], "adamw_" + nm))
    done = [res[nm][1] for nm in ("mlp_w1", "mlp_w2", "s5_w_glu", "w_out_even", "w_out_odd")]

    (small_all,) = _comm_wait(ags, done, "ags_wait", exchange=False)
    small_full = _unpack(small_all.reshape(rows, LANES), full_shapes)
    loss = 0.5 * small_full.pop()[0, 0] / D_MODEL
    small_g = []
    for nm, g in zip(small_names, small_full):
        if nm in ("pool_scale", "sgu_ln_g", "sgu_ln_b"):
            g = lax.dynamic_slice(g, (my_index * 64,), (64,)).reshape(1, 64)
        small_g.append(g)
    sd, sm, sv = _adamw_many([weights[nm] for nm in small_names], small_g, [mom_m[nm] for nm in small_names],
                             [mom_v[nm] for nm in small_names], "adamw_small")
    for nm, g_, d_, m_, v_ in zip(small_names, small_g, sd, sm, sv):
        res[nm] = (g_, d_, m_, v_)
    done.append(sd[0])

    for nm, parts in (("w_in_odd", r_win_o), ("w_in_even", None)):
        if parts is None:
            (parts,) = _comm_wait(ex5, done, "ex_wait5", exchange=True)
        outs = _sum_adamw(parts, jnp.transpose(weights[nm], (0, 2, 1)), jnp.transpose(mom_m[nm], (0, 2, 1)),
                          jnp.transpose(mom_v[nm], (0, 2, 1)), "adamw_" + nm)
        res[nm] = tuple(jnp.transpose(o, (0, 2, 1)) for o in outs)
        done.append(res[nm][1])

    grads = [res[nm][0].reshape(weights[nm].shape) for nm in names]
    deltas = [res[nm][1].reshape(weights[nm].shape) for nm in names]
    new_m = [res[nm][2].reshape(weights[nm].shape) for nm in names]
    new_v = [res[nm][3].reshape(weights[nm].shape) for nm in names]
    return (loss, grad_x[None], *grads, *deltas, *new_m, *new_v)
```

```python
import functools
import math

import jax
import jax.numpy as jnp
from jax import lax
from jax.experimental import pallas as pl
from jax.experimental.pallas import tpu as pltpu

F32 = jnp.float32
BF16 = jnp.bfloat16
MESH = pl.DeviceIdType.MESH
ANY = pl.BlockSpec(memory_space=pl.ANY)

N_DEV = 8
D_MODEL = 1024
EPS = 1e-6
NORM_ROWS = 512
S5_W = 512
S5_NS = 2048
SCAN_GROUPS = 4
SCAN_CHUNK = 1024
FOX_W = 512
EVEN_IN = 2056
EVEN_PAD = 2176
ODD_IN = 1536
LANES = 128
PIECE = 4 * D_MODEL // N_DEV
VMEM_LIMIT = 56 * 1024 * 1024

ADAM_LR = 0.001
ADAM_B1 = 0.9
ADAM_B2 = 0.999
ADAM_EPS = 1e-08
ADAM_WD = 0.01
ADAM_STEP = 10

NT = (((1,), (1,)), ((), ()))
TN = (((0,), (0,)), ((), ()))
NN = (((1,), (0,)), ((), ()))


def _cp(*sem):
    return pltpu.CompilerParams(dimension_semantics=sem, vmem_limit_bytes=VMEM_LIMIT)


def _sds(shape, dtype=F32):
    return jax.ShapeDtypeStruct(tuple(shape), dtype)


def _gelu(x):
    t = jnp.tanh(0.7978845608028654 * (x + 0.044715 * x * x * x))
    return 0.5 * x * (1.0 + t)


def _gelu_grad(x):
    t = jnp.tanh(0.7978845608028654 * (x + 0.044715 * x * x * x))
    du = 0.7978845608028654 * (1.0 + 3.0 * 0.044715 * x * x)
    return 0.5 * (1.0 + t) + 0.5 * x * (1.0 - t * t) * du


def _sigmoid(x):
    return 1.0 / (1.0 + jnp.exp(-x))


def _dot(a, b, dn=NN):
    return lax.dot_general(a, b, dn, preferred_element_type=F32)


def _mm(a, b, *, name, ta=False, tb=False, b3=False, out3=False, out_dtypes=(F32,), epi=None, extra=(),
        bm=1024, bn=1024, bk=1024, dep=None):
    a_list = list(a) if isinstance(a, (list, tuple)) else [a]
    widths = [p.shape[1] for p in a_list]
    offs = [sum(widths[:i]) for i in range(len(widths))]
    na = len(a_list)
    M = sum(widths) if ta else a_list[0].shape[0]
    K = a_list[0].shape[0] if ta else sum(widths)
    if na > 1:
        assert not b3 and not tb
        bm, bk = (M, bk) if ta else (bm, K)
    pw = b.shape[2] if b3 else PIECE
    if b3:
        N = b.shape[1] if tb else b.shape[0] * pw
        assert (b.shape[0] * pw if tb else b.shape[1]) == K
    else:
        N = b.shape[0] if tb else b.shape[1]
    bm, bn, bk = min(bm, M), min(bn, N), min(bk, K)
    assert M % bm == 0 and N % bn == 0 and K % bk == 0, (name, M, N, K, bm, bn, bk)
    assert not (b3 or out3) or ((bk if tb else bn) % pw == 0 and bn % PIECE == 0)
    nk = K // bk
    n_extra = len(extra)
    n_out = len(out_dtypes)
    dn = (((0 if ta else 1,), (1 if tb else 0,)), ((), ()))

    use_acc = nk > 1

    def body(*refs):
        a_refs, b_ref = refs[:na], refs[na]
        a_ref = a_refs[0]
        e_refs = refs[na + 1:na + 1 + n_extra]
        first_out = na + 1 + n_extra + (0 if dep is None else 1)
        o_refs = refs[first_out:first_out + n_out]
        acc_ref = refs[-1] if use_acc else o_refs[0]
        k = pl.program_id(2)

        def dot(a_v, b_v):
            return lax.dot_general(a_v.astype(BF16), b_v.astype(BF16), dn, preferred_element_type=F32)

        everything = slice(None)
        if na > 1 and ta:
            terms = [(pl.ds(off, w), everything, r, b_ref) for r, off, w in zip(a_refs, offs, widths)]
        elif na > 1:
            terms = [(everything, everything, r, b_ref.at[pl.ds(off, w), :]) for r, off, w in zip(a_refs, offs, widths)]
        elif not b3:
            terms = [(everything, everything, a_ref, b_ref)]
        elif tb:
            terms = [(everything, everything,
                      a_ref.at[pl.ds(t * pw, pw), :] if ta else a_ref.at[:, pl.ds(t * pw, pw)], b_ref.at[t])
                     for t in range(bk // pw)]
        else:
            terms = [(everything, pl.ds(t * pw, pw), a_ref, b_ref.at[t]) for t in range(bn // pw)]

        def finish(acc):
            outs = (acc,) if epi is None else epi(acc, *[e[...] for e in e_refs])
            for o_ref, o in zip(o_refs, outs):
                if out3:
                    for t in range(bn // PIECE):
                        o_ref[t] = o[:, t * PIECE:(t + 1) * PIECE].astype(o_ref.dtype)
                else:
                    o_ref[...] = o.astype(o_ref.dtype)

        if nk == 1:
            bands = {}
            for rows, cols, a_r, b_r in terms:
                key = (getattr(rows, "start", None), getattr(cols, "start", None))
                val = dot(a_r[...], b_r[...])
                bands[key] = val if key not in bands else bands[key] + val
            vals = list(bands.values())
            if len(vals) == 1:
                finish(vals[0])
            else:
                finish(jnp.concatenate(vals, axis=0 if (na > 1 and ta) else 1))
            return

        @pl.when(k == 0)
        def _():
            acc_ref[...] = jnp.zeros_like(acc_ref)

        for rows, cols, a_r, b_r in terms:
            acc_ref[rows, cols] += dot(a_r[...], b_r[...])

        @pl.when(k == nk - 1)
        def _():
            finish(acc_ref[...])

    if na > 1:
        a_specs = [pl.BlockSpec((bk, w), lambda i, j, k: (k, 0)) if ta else pl.BlockSpec((bm, w), lambda i, j, k: (i, 0))
                   for w in widths]
    else:
        a_specs = [pl.BlockSpec((bk, bm), lambda i, j, k: (k, i)) if ta else
                   pl.BlockSpec((bm, bk), lambda i, j, k: (i, k))]
    if b3:
        if tb:
            b_spec = pl.BlockSpec((bk // pw, bn, pw), lambda i, j, k: (k, j, 0))
        else:
            b_spec = pl.BlockSpec((bn // pw, bk, pw), lambda i, j, k: (j, k, 0))
    else:
        b_spec = pl.BlockSpec((bn, bk), lambda i, j, k: (j, k)) if tb else pl.BlockSpec((bk, bn), lambda i, j, k: (k, j))
    e_specs = [pl.BlockSpec((bm, bn), lambda i, j, k: (i, j)) for _ in extra]
    if out3:
        o_specs = [pl.BlockSpec((bn // PIECE, bm, PIECE), lambda i, j, k: (j, i, 0)) for _ in out_dtypes]
        o_shapes = [_sds((N // PIECE, M, PIECE), dt) for dt in out_dtypes]
    else:
        o_specs = [pl.BlockSpec((bm, bn), lambda i, j, k: (i, j)) for _ in out_dtypes]
        o_shapes = [_sds((M, N), dt) for dt in out_dtypes]
    outs = pl.pallas_call(
        body, name=name, grid=(M // bm, N // bn, nk),
        in_specs=a_specs + [b_spec] + e_specs + ([] if dep is None else [ANY]),
        out_specs=o_specs, out_shape=o_shapes,
        scratch_shapes=[pltpu.VMEM((bm, bn), F32)] if use_acc else [],
        compiler_params=_cp("parallel", "parallel", "arbitrary"),
    )(*a_list, b, *extra, *([] if dep is None else [dep]))
    return outs[0] if n_out == 1 else outs


def _epi_relu2(acc):
    r = jnp.maximum(acc, 0.0)
    return acc, r * r


def _epi_relu2_bwd(acc, p):
    return (acc * (2.0 * jnp.maximum(p.astype(F32), 0.0)),)


def _row_spec(rb, w=D_MODEL):
    return pl.BlockSpec((rb, w), lambda i: (i, 0))


def _vec_spec(w=D_MODEL):
    return pl.BlockSpec((1, w), lambda i: (0, 0))


def _rstd(v):
    return lax.rsqrt(jnp.mean(v * v, axis=-1, keepdims=True) + EPS)


def _rms_fwd(x, g, name):
    L = x.shape[0]
    rb = min(NORM_ROWS, L)

    def body(x_ref, g_ref, h_ref, r_ref):
        xv = x_ref[...]
        r = _rstd(xv)
        h_ref[...] = (xv * r * g_ref[...]).astype(BF16)
        r_ref[...] = r

    return pl.pallas_call(
        body, name=name, grid=(L // rb,),
        in_specs=[_row_spec(rb), _vec_spec()],
        out_specs=[_row_spec(rb), _row_spec(rb, 1)],
        out_shape=[_sds((L, D_MODEL), BF16), _sds((L, 1))],
        compiler_params=_cp("parallel"),
    )(x, g)


def _post_pre_fwd(x_in, y, g_post, g_pre, name):
    L = x_in.shape[0]
    rb = min(NORM_ROWS, L)

    def body(x_ref, y_ref, gp_ref, gn_ref, xo_ref, ry_ref, h_ref, rx_ref):
        yv = y_ref[...]
        ry = _rstd(yv)
        xo = x_ref[...] + yv * ry * gp_ref[...]
        rx = _rstd(xo)
        xo_ref[...] = xo
        ry_ref[...] = ry
        h_ref[...] = (xo * rx * gn_ref[...]).astype(BF16)
        rx_ref[...] = rx

    return pl.pallas_call(
        body, name=name, grid=(L // rb,),
        in_specs=[_row_spec(rb), _row_spec(rb), _vec_spec(), _vec_spec()],
        out_specs=[_row_spec(rb), _row_spec(rb, 1), _row_spec(rb), _row_spec(rb, 1)],
        out_shape=[_sds((L, D_MODEL)), _sds((L, 1)), _sds((L, D_MODEL), BF16), _sds((L, 1))],
        compiler_params=_cp("parallel"),
    )(x_in, y, g_post, g_pre)


def _post_loss_fwd(x_in, y, g_post, target, name):
    L = x_in.shape[0]
    rb = min(NORM_ROWS, L)

    def body(x_ref, y_ref, gp_ref, t_ref, gx_ref, ry_ref, loss_ref):
        i = pl.program_id(0)
        yv = y_ref[...]
        ry = _rstd(yv)
        diff = x_ref[...] + yv * ry * gp_ref[...] - t_ref[...]
        gx_ref[...] = diff * (1.0 / D_MODEL)
        ry_ref[...] = ry

        @pl.when(i == 0)
        def _():
            loss_ref[...] = jnp.zeros_like(loss_ref)

        loss_ref[...] += jnp.sum(diff * diff, keepdims=True)

    return pl.pallas_call(
        body, name=name, grid=(L // rb,),
        in_specs=[_row_spec(rb), _row_spec(rb), _vec_spec(), _row_spec(rb)],
        out_specs=[_row_spec(rb), _row_spec(rb, 1), pl.BlockSpec((1, 1), lambda i: (0, 0))],
        out_shape=[_sds((L, D_MODEL)), _sds((L, 1)), _sds((1, 1))],
        compiler_params=_cp("arbitrary"),
    )(x_in, y, g_post, target)


def _rms_bwd_rows(dy, xv, r, g):
    n = xv * r
    dyg = dy * g
    return r * (dyg - n * jnp.mean(dyg * n, axis=-1, keepdims=True)), n


def _post_bwd(g_out, y, ry, g_post, name):
    L = y.shape[0]
    rb = min(NORM_ROWS, L)

    def body(go_ref, y_ref, ry_ref, gp_ref, gy_ref, gg_ref):
        i = pl.program_id(0)
        go = go_ref[...]
        gy, n = _rms_bwd_rows(go, y_ref[...], ry_ref[...], gp_ref[...])
        gy_ref[...] = gy.astype(BF16)

        @pl.when(i == 0)
        def _():
            gg_ref[...] = jnp.zeros_like(gg_ref)

        gg_ref[...] += jnp.sum(go * n, axis=0, keepdims=True)

    return pl.pallas_call(
        body, name=name, grid=(L // rb,),
        in_specs=[_row_spec(rb), _row_spec(rb), _row_spec(rb, 1), _vec_spec()],
        out_specs=[_row_spec(rb), _vec_spec()],
        out_shape=[_sds((L, D_MODEL), BF16), _sds((1, D_MODEL))],
        compiler_params=_cp("arbitrary"),
    )(g_out, y, ry, g_post)


def _pre_post_bwd(g_h, x, rx, g_pre, g_out, y_prev, ry_prev, g_post_prev, name):
    L = x.shape[0]
    rb = min(NORM_ROWS, L)

    def body(gh_ref, x_ref, rx_ref, gn_ref, go_ref, y_ref, ry_ref, gp_ref, gi_ref, ggn_ref, gy_ref, ggp_ref):
        i = pl.program_id(0)
        gh = gh_ref[...]
        gx, n = _rms_bwd_rows(gh, x_ref[...], rx_ref[...], gn_ref[...])
        gi = go_ref[...] + gx
        gi_ref[...] = gi
        gy, ny = _rms_bwd_rows(gi, y_ref[...], ry_ref[...], gp_ref[...])
        gy_ref[...] = gy.astype(BF16)

        @pl.when(i == 0)
        def _():
            ggn_ref[...] = jnp.zeros_like(ggn_ref)
            ggp_ref[...] = jnp.zeros_like(ggp_ref)

        ggn_ref[...] += jnp.sum(gh * n, axis=0, keepdims=True)
        ggp_ref[...] += jnp.sum(gi * ny, axis=0, keepdims=True)

    return pl.pallas_call(
        body, name=name, grid=(L // rb,),
        in_specs=[_row_spec(rb), _row_spec(rb), _row_spec(rb, 1), _vec_spec(), _row_spec(rb),
                  _row_spec(rb), _row_spec(rb, 1), _vec_spec()],
        out_specs=[_row_spec(rb), _vec_spec(), _row_spec(rb), _vec_spec()],
        out_shape=[_sds((L, D_MODEL)), _sds((1, D_MODEL)), _sds((L, D_MODEL), BF16), _sds((1, D_MODEL))],
        compiler_params=_cp("arbitrary"),
    )(g_h, x, rx, g_pre, g_out, y_prev, ry_prev, g_post_prev)


def _pre_bwd(g_h, x, rx, g_pre, g_out, name):
    L = x.shape[0]
    rb = min(NORM_ROWS, L)

    def body(gh_ref, x_ref, rx_ref, gn_ref, go_ref, gi_ref, ggn_ref):
        i = pl.program_id(0)
        gh = gh_ref[...]
        gx, n = _rms_bwd_rows(gh, x_ref[...], rx_ref[...], gn_ref[...])
        gi_ref[...] = go_ref[...] + gx

        @pl.when(i == 0)
        def _():
            ggn_ref[...] = jnp.zeros_like(ggn_ref)

        ggn_ref[...] += jnp.sum(gh * n, axis=0, keepdims=True)

    return pl.pallas_call(
        body, name=name, grid=(L // rb,),
        in_specs=[_row_spec(rb), _row_spec(rb), _row_spec(rb, 1), _vec_spec(), _row_spec(rb)],
        out_specs=[_row_spec(rb), _vec_spec()],
        out_shape=[_sds((L, D_MODEL)), _sds((1, D_MODEL))],
        compiler_params=_cp("arbitrary"),
    )(g_h, x, rx, g_pre, g_out)


def _cmul(ar, ai, br, bi):
    return ar * br - ai * bi, ar * bi + ai * br


def _zoh_cols(lr, li, ldt):
    dt = jnp.exp(ldt)
    mag = jnp.exp(lr * dt)
    ar = mag * jnp.cos(li * dt)
    ai = mag * jnp.sin(li * dt)
    den = lr * lr + li * li
    nr = ar - 1.0
    qr = (nr * lr + ai * li) / den
    qi = (ai * lr - nr * li) / den
    return dt, ar, ai, qr, qi, den


def _b_mask():
    r = lax.broadcasted_iota(jnp.int32, (S5_NS, LANES), 0)
    c = lax.broadcasted_iota(jnp.int32, (S5_NS, LANES), 1)
    return ((r >> 6) & 7) == (c >> 4)


def _c_mask():
    r = lax.broadcasted_iota(jnp.int32, (S5_W, 512), 0)
    c = lax.broadcasted_iota(jnp.int32, (S5_W, 512), 1)
    return ((r >> 4) & 7) == (c >> 6)


def _s5_prep(lam_r, ldt_r, lam_c, ldt_c, b_t, c_t, name):
    def body(lam_r_ref, ldt_r_ref, lam_c_ref, ldt_c_ref, b_ref, c_ref, tab_ref, bset_ref, cset_ref):
        lr, li = lam_r_ref[0:1, :], lam_r_ref[1:2, :]
        dt = jnp.exp(ldt_r_ref[...])
        mag = jnp.exp(lr * dt)
        p1r, p1i = mag * jnp.cos(li * dt), mag * jnp.sin(li * dt)
        p2r, p2i = _cmul(p1r, p1i, p1r, p1i)
        p3r, p3i = _cmul(p2r, p2i, p1r, p1i)
        p4r, p4i = _cmul(p2r, p2i, p2r, p2i)
        p5r, p5i = _cmul(p4r, p4i, p1r, p1i)
        p6r, p6i = _cmul(p4r, p4i, p2r, p2i)
        p7r, p7i = _cmul(p4r, p4i, p3r, p3i)
        p8r, p8i = _cmul(p4r, p4i, p4r, p4i)
        pw_r = [p1r, p2r, p3r, p4r, p5r, p6r, p7r, p8r]
        pw_i = [p1i, p2i, p3i, p4i, p5i, p6i, p7i, p8i]
        row = lax.broadcasted_iota(jnp.int32, (8, S5_NS), 0)
        zero = jnp.zeros((8, S5_NS), F32)

        def bc(v):
            return jnp.broadcast_to(v, (8, S5_NS))

        for d in range(2):
            sgn = 1.0 if d == 0 else -1.0
            for t, s in enumerate((1, 2, 4)):
                live = (row >= s) if d == 0 else (row <= 7 - s)
                tab_ref[d, 2 * t] = jnp.where(live, bc(pw_r[s - 1]), zero)
                tab_ref[d, 2 * t + 1] = jnp.where(live, bc(sgn * pw_i[s - 1]), zero)
            cr, ci = zero, zero
            for i in range(8):
                e = i if d == 0 else 7 - i
                cr = jnp.where(row == i, bc(pw_r[e]), cr)
                ci = jnp.where(row == i, bc(sgn * pw_i[e]), ci)
            tab_ref[d, 6] = cr
            tab_ref[d, 7] = ci

        _, _, _, qr, qi, _ = _zoh_cols(lam_c_ref[:, 0:1], lam_c_ref[:, 1:2], ldt_c_ref[...])
        bm = _b_mask()
        br, bi = b_ref[0], b_ref[1]
        bset_ref[0] = jnp.where(bm, qr * br - qi * bi, 0.0).astype(BF16)
        bset_ref[1] = jnp.where(bm, qr * bi + qi * br, 0.0).astype(BF16)
        cm = _c_mask()
        cset_ref[0] = jnp.where(cm, c_ref[0], 0.0).astype(BF16)
        cset_ref[1] = jnp.where(cm, c_ref[1], 0.0).astype(BF16)

    vm = pl.BlockSpec(memory_space=pltpu.VMEM)
    return pl.pallas_call(
        body, name=name, in_specs=[vm] * 6, out_specs=[vm] * 3,
        out_shape=[_sds((2, 8, 8, S5_NS)), _sds((2, S5_NS, LANES), BF16), _sds((2, S5_W, 512), BF16)],
        compiler_params=pltpu.CompilerParams(vmem_limit_bytes=VMEM_LIMIT),
    )(lam_r, ldt_r, lam_c, ldt_c, b_t, c_t)


SCAN_W = SCAN_GROUPS * LANES


def _scan_chunk(src_ref, dst_ref, tab_ref, carry_ref, nb, reverse, xs_ref=None, acc_ref=None):
    row = lax.broadcasted_iota(jnp.int32, (8, LANES), 0)

    def step(i, carry):
        b = (nb - 1 - i) if reverse else i
        off = pl.multiple_of(b * 8, 8)
        out = []
        for g in range(SCAN_GROUPS):
            lanes = pl.ds(g * LANES, LANES)
            cr, ci = carry[2 * g], carry[2 * g + 1]
            yr = src_ref[0, pl.ds(off, 8), lanes]
            yi = src_ref[1, pl.ds(off, 8), lanes]
            for t, s in enumerate((1, 2, 4)):
                sh = (8 - s) if reverse else s
                sr = pltpu.roll(yr, sh, 0)
                si = pltpu.roll(yi, sh, 0)
                mr, mi = tab_ref[2 * t, :, lanes], tab_ref[2 * t + 1, :, lanes]
                yr, yi = yr + mr * sr - mi * si, yi + mr * si + mi * sr
            pr, pi = tab_ref[6, :, lanes], tab_ref[7, :, lanes]
            yr, yi = yr + pr * cr - pi * ci, yi + pr * ci + pi * cr
            dst_ref[0, pl.ds(off, 8), lanes] = yr
            dst_ref[1, pl.ds(off, 8), lanes] = yi
            if xs_ref is not None:
                nr = jnp.where(row == 7, cr, pltpu.roll(yr, 7, 0))
                ni = jnp.where(row == 7, ci, pltpu.roll(yi, 7, 0))
                xr = xs_ref[0, pl.ds(off, 8), lanes]
                xi = xs_ref[1, pl.ds(off, 8), lanes]
                acc_ref[0, :, lanes] += xr * nr + xi * ni
                acc_ref[1, :, lanes] += xr * ni - xi * nr
            last = 0 if reverse else 7
            out += [jnp.broadcast_to(yr[last:last + 1, :], (8, LANES)),
                    jnp.broadcast_to(yi[last:last + 1, :], (8, LANES))]
        return tuple(out)

    init = []
    for g in range(SCAN_GROUPS):
        init += [carry_ref[0, :, pl.ds(g * LANES, LANES)], carry_ref[1, :, pl.ds(g * LANES, LANES)]]
    fin = lax.fori_loop(0, nb, step, tuple(init))
    for g in range(SCAN_GROUPS):
        carry_ref[0, :, pl.ds(g * LANES, LANES)] = fin[2 * g]
        carry_ref[1, :, pl.ds(g * LANES, LANES)] = fin[2 * g + 1]


def _s5_scan_fwd(z, bset, tabs, name):
    L = z.shape[0]
    tl = min(SCAN_CHUNK, L)
    nc = L // tl

    def body(u_ref, b_ref, tab_ref, x_ref, carry_ref):
        @pl.when(pl.program_id(1) == 0)
        def _():
            carry_ref[...] = jnp.zeros_like(carry_ref)

        u = u_ref[...].astype(BF16)
        x_ref[0] = _dot(u, b_ref[0], NT)
        x_ref[1] = _dot(u, b_ref[1], NT)
        _scan_chunk(x_ref, x_ref, tab_ref, carry_ref, tl // 8, False)

    return pl.pallas_call(
        body, name=name, grid=(S5_NS // SCAN_W, nc),
        in_specs=[pl.BlockSpec((tl, LANES), lambda j, c: (c, j)),
                  pl.BlockSpec((2, SCAN_W, LANES), lambda j, c: (0, j, 0)),
                  pl.BlockSpec((None, 8, 8, SCAN_W), lambda j, c: (0, 0, 0, j))],
        out_specs=pl.BlockSpec((2, tl, SCAN_W), lambda j, c: (0, c, j)),
        out_shape=_sds((2, L, S5_NS)),
        scratch_shapes=[pltpu.VMEM((2, 8, SCAN_W), F32)],
        compiler_params=_cp("parallel", "arbitrary"),
    )(z, bset, tabs)


def _s5_scan_bwd(gyl, cset, xs, z, bset, gud, tabs, name):
    L = z.shape[0]
    tl = min(SCAN_CHUNK, L)
    nc = L // tl

    def body(g_ref, c_ref, xs_ref, u_ref, b_ref, gud_ref, tab_ref, gu_ref, ga_ref, gb_ref, gc_ref,
             gx_ref, carry_ref, acc_ref):
        c = pl.program_id(1)

        @pl.when(c == 0)
        def _():
            carry_ref[...] = jnp.zeros_like(carry_ref)
            acc_ref[...] = jnp.zeros_like(acc_ref)
            gb_ref[...] = jnp.zeros_like(gb_ref)
            gc_ref[...] = jnp.zeros_like(gc_ref)

        gy = g_ref[...].astype(BF16)
        gx_ref[0] = _dot(gy, c_ref[0])
        gx_ref[1] = -_dot(gy, c_ref[1])
        gc_ref[0] += _dot(gy, xs_ref[0].astype(BF16), TN)
        gc_ref[1] -= _dot(gy, xs_ref[1].astype(BF16), TN)
        _scan_chunk(gx_ref, gx_ref, tab_ref, carry_ref, tl // 8, True, xs_ref, acc_ref)
        gr = gx_ref[0].astype(BF16)
        gi = gx_ref[1].astype(BF16)
        gu_ref[...] = gud_ref[...] + _dot(gr, b_ref[0]) + _dot(gi, b_ref[1])
        u = u_ref[...].astype(BF16)
        gb_ref[0] += _dot(gr, u, TN)
        gb_ref[1] += _dot(gi, u, TN)

        @pl.when(c == nc - 1)
        def _():
            ga_ref[0:1, :] = jnp.sum(acc_ref[0], axis=0, keepdims=True)
            ga_ref[1:2, :] = jnp.sum(acc_ref[1], axis=0, keepdims=True)

    rev = lambda j, c: (nc - 1 - c, j)
    col = pl.BlockSpec((tl, LANES), rev)
    return pl.pallas_call(
        body, name=name, grid=(S5_NS // SCAN_W, nc),
        in_specs=[col, pl.BlockSpec((2, LANES, SCAN_W), lambda j, c: (0, j, 0)),
                  pl.BlockSpec((2, tl, SCAN_W), lambda j, c: (0, nc - 1 - c, j)), col,
                  pl.BlockSpec((2, SCAN_W, LANES), lambda j, c: (0, j, 0)), col,
                  pl.BlockSpec((None, 8, 8, SCAN_W), lambda j, c: (1, 0, 0, j))],
        out_specs=[col, pl.BlockSpec((2, SCAN_W), lambda j, c: (0, j)),
                   pl.BlockSpec((2, SCAN_W, LANES), lambda j, c: (0, j, 0)),
                   pl.BlockSpec((2, LANES, SCAN_W), lambda j, c: (0, j, 0))],
        out_shape=[_sds((L, S5_W)), _sds((2, S5_NS)), _sds((2, S5_NS, LANES)), _sds((2, S5_W, 512))],
        scratch_shapes=[pltpu.VMEM((2, tl, SCAN_W), F32), pltpu.VMEM((2, 8, SCAN_W), F32),
                        pltpu.VMEM((2, 8, SCAN_W), F32)],
        compiler_params=_cp("parallel", "arbitrary"),
    )(gyl, cset, xs, z, bset, gud, tabs)


def _s5_out_fwd(xs, cset, z, dvec, wglu, name):
    L = z.shape[0]
    bl = min(256, L)

    def body(x_ref, c_ref, u_ref, d_ref, w_ref, ylin_ref, ya_ref):
        cols = []
        for j in range(4):
            xr = x_ref[0, :, 512 * j:512 * (j + 1)].astype(BF16)
            xi = x_ref[1, :, 512 * j:512 * (j + 1)].astype(BF16)
            cr = c_ref[0, LANES * j:LANES * (j + 1), :]
            ci = c_ref[1, LANES * j:LANES * (j + 1), :]
            cols.append(_dot(xr, cr, NT) - _dot(xi, ci, NT))
        ylin = jnp.concatenate(cols, axis=1) + d_ref[...] * u_ref[...]
        yg = _gelu(ylin)
        t = _dot(yg.astype(BF16), w_ref[...])
        ylin_ref[...] = ylin
        ya_ref[...] = (yg * _sigmoid(t)).astype(BF16)

    return pl.pallas_call(
        body, name=name, grid=(L // bl,),
        in_specs=[pl.BlockSpec((2, bl, S5_NS), lambda i: (0, i, 0)),
                  pl.BlockSpec((2, S5_W, 512), lambda i: (0, 0, 0)),
                  pl.BlockSpec((bl, S5_W), lambda i: (i, 0)),
                  pl.BlockSpec((1, S5_W), lambda i: (0, 0)),
                  pl.BlockSpec((S5_W, S5_W), lambda i: (0, 0))],
        out_specs=[pl.BlockSpec((bl, S5_W), lambda i: (i, 0))] * 2,
        out_shape=[_sds((L, S5_W)), _sds((L, S5_W), BF16)],
        compiler_params=_cp("parallel"),
    )(xs, cset, z, dvec, wglu)


def _s5_glu_bwd(g_m, ylin, z, dvec, wglu, name):
    L = z.shape[0]
    bl = min(256, L)

    def body(g_ref, ylin_ref, u_ref, d_ref, w_ref, gyl_ref, gud_ref, gw_ref, gd_ref):
        i = pl.program_id(0)
        ylin = ylin_ref[...]
        yg = _gelu(ylin)
        ygb = yg.astype(BF16)
        sg = _sigmoid(_dot(ygb, w_ref[...]))
        gya = g_ref[...]
        gt = gya * yg * sg * (1.0 - sg)
        gtb = gt.astype(BF16)
        gyg = gya * sg + _dot(gtb, w_ref[...], NT)
        gyl = gyg * _gelu_grad(ylin)
        gyl_ref[...] = gyl
        gud_ref[...] = gyl * d_ref[...]

        @pl.when(i == 0)
        def _():
            gw_ref[...] = jnp.zeros_like(gw_ref)
            gd_ref[...] = jnp.zeros_like(gd_ref)

        gw_ref[...] += _dot(ygb, gtb, TN)
        gd_ref[...] += jnp.sum(gyl * u_ref[...], axis=0, keepdims=True)

    blk = pl.BlockSpec((bl, S5_W), lambda i: (i, 0))
    return pl.pallas_call(
        body, name=name, grid=(L // bl,),
        in_specs=[blk, blk, blk, pl.BlockSpec((1, S5_W), lambda i: (0, 0)),
                  pl.BlockSpec((S5_W, S5_W), lambda i: (0, 0))],
        out_specs=[blk, blk, pl.BlockSpec((S5_W, S5_W), lambda i: (0, 0)), pl.BlockSpec((1, S5_W), lambda i: (0, 0))],
        out_shape=[_sds((L, S5_W)), _sds((L, S5_W)), _sds((S5_W, S5_W)), _sds((1, S5_W))],
        compiler_params=_cp("arbitrary"),
    )(g_m, ylin, z, dvec, wglu)


def _s5_param_bwd(lam_c, ldt_c, b_t, gb, ga_c, gc, name):
    def body(lam_ref, ldt_ref, b_ref, gb_ref, ga_ref, gc_ref, glam_ref, gldt_ref, gbo_ref, gco_ref):
        lr, li = lam_ref[:, 0:1], lam_ref[:, 1:2]
        dt, ar, ai, qr, qi, den = _zoh_cols(lr, li, ldt_ref[...])
        bm = _b_mask()
        gbr = jnp.where(bm, gb_ref[0], 0.0)
        gbi = jnp.where(bm, gb_ref[1], 0.0)
        br, bi = b_ref[0], b_ref[1]
        obr = gbr * qr + gbi * qi
        obi = gbi * qr - gbr * qi
        gqr = jnp.sum(gbr * br + gbi * bi, axis=1, keepdims=True)
        gqi = jnp.sum(gbi * br - gbr * bi, axis=1, keepdims=True)
        for s in (64, 32, 16):
            obr = obr + pltpu.roll(obr, s, 1)
            obi = obi + pltpu.roll(obi, s, 1)
        gbo_ref[0] = obr
        gbo_ref[1] = obi
        gar = ga_ref[:, 0:1] + (gqr * lr - gqi * li) / den
        gai = ga_ref[:, 1:2] + (gqr * li + gqi * lr) / den
        qlr = (qr * lr + qi * li) / den
        qli = (qi * lr - qr * li) / den
        glr = -(gqr * qlr + gqi * qli)
        gli = -(gqi * qlr - gqr * qli)
        glr = glr + dt * (gar * ar + gai * ai)
        gli = gli + dt * (gai * ar - gar * ai)
        wr, wi = _cmul(lr, li, ar, ai)
        gldt = (gar * wr + gai * wi) * dt
        glam_ref[:, 0:1] = glr
        glam_ref[:, 1:2] = gli
        r = lax.broadcasted_iota(jnp.int32, (S5_NS, 32), 0)
        c = lax.broadcasted_iota(jnp.int32, (S5_NS, 32), 1)
        gldt_ref[...] = jnp.sum(jnp.where((r >> 6) == c, gldt, 0.0), axis=0, keepdims=True)
        cm = _c_mask()
        for k in range(2):
            oc = jnp.where(cm, gc_ref[k], 0.0)
            for s in (256, 128, 64):
                oc = oc + pltpu.roll(oc, s, 1)
            gco_ref[k] = oc[:, 0:LANES]

    vm = pl.BlockSpec(memory_space=pltpu.VMEM)
    return pl.pallas_call(
        body, name=name, in_specs=[vm] * 6, out_specs=[vm] * 4,
        out_shape=[_sds((S5_NS, 2)), _sds((1, 32)), _sds((2, S5_NS, LANES)), _sds((2, S5_W, LANES))],
        compiler_params=pltpu.CompilerParams(vmem_limit_bytes=VMEM_LIMIT),
    )(lam_c, ldt_c, b_t, gb, ga_c, gc)


FL_BLK = EVEN_PAD // LANES - 1
Q_BLK, K_BLK, V_BLK = 4, 8, 12
NEG = -1e30


def _log_sigmoid(v):
    return jnp.minimum(v, 0.0) - jnp.log(1.0 + jnp.exp(-jnp.abs(v)))


def _fox_f_fwd(z, bf, name):
    L = z.shape[0]
    tl = min(256, L)

    def body(fl_ref, b_ref, f_ref, fq_ref, carry_ref):
        i = pl.program_id(0)

        @pl.when(i == 0)
        def _():
            carry_ref[...] = jnp.zeros_like(carry_ref)

        lf = _log_sigmoid(fl_ref[...] + b_ref[...])
        r = lax.broadcasted_iota(jnp.int32, (tl, tl), 0)
        c = lax.broadcasted_iota(jnp.int32, (tl, tl), 1)
        tri = (r >= c).astype(F32)
        cs = lax.dot_general(tri, lf, NN, precision=lax.Precision.HIGHEST, preferred_element_type=F32) + carry_ref[...]
        f_ref[...] = cs
        carry_ref[...] = cs[tl - 1:tl, :]
        expand = (lax.broadcasted_iota(jnp.int32, (LANES, FOX_W), 0)
                  == (lax.broadcasted_iota(jnp.int32, (LANES, FOX_W), 1) >> 6)).astype(F32)
        fq_ref[...] = lax.dot_general(cs, expand, NN, precision=lax.Precision.HIGHEST, preferred_element_type=F32)

    return pl.pallas_call(
        body, name=name, grid=(L // tl,),
        in_specs=[pl.BlockSpec((tl, LANES), lambda i: (i, FL_BLK)), pl.BlockSpec((1, LANES), lambda i: (0, 0))],
        out_specs=[pl.BlockSpec((tl, LANES), lambda i: (i, 0)), pl.BlockSpec((tl, FOX_W), lambda i: (i, 0))],
        out_shape=[_sds((L, LANES)), _sds((L, FOX_W))],
        scratch_shapes=[pltpu.VMEM((1, LANES), F32)],
        compiler_params=_cp("arbitrary"),
    )(z, bf)


def _fox_f_bwd(dFk, dfq, z, bf, name):
    L = z.shape[0]
    tl = min(256, L)
    nb = L // tl

    def body(dfk_ref, dfq_ref, fl_ref, b_ref, dfl_ref, db_ref, carry_ref):
        i = pl.program_id(0)

        @pl.when(i == 0)
        def _():
            carry_ref[...] = jnp.zeros_like(carry_ref)
            db_ref[...] = jnp.zeros_like(db_ref)

        sel = (lax.broadcasted_iota(jnp.int32, (FOX_W, LANES), 0)
               == 64 * lax.broadcasted_iota(jnp.int32, (FOX_W, LANES), 1)).astype(F32)
        dfq_h = lax.dot_general(dfq_ref[...], sel, NN, precision=lax.Precision.HIGHEST, preferred_element_type=F32)
        r = lax.broadcasted_iota(jnp.int32, (tl, tl), 0)
        c = lax.broadcasted_iota(jnp.int32, (tl, tl), 1)
        tri = (r <= c).astype(F32)
        cs = lax.dot_general(tri, dfk_ref[...] + dfq_h, NN, precision=lax.Precision.HIGHEST,
                             preferred_element_type=F32) + carry_ref[...]
        carry_ref[...] = cs[0:1, :]
        dfl = cs * _sigmoid(-(fl_ref[...] + b_ref[...]))
        dfl_ref[...] = dfl
        db_ref[...] += jnp.sum(dfl, axis=0, keepdims=True)

    return pl.pallas_call(
        body, name=name, grid=(nb,),
        in_specs=[pl.BlockSpec((tl, LANES), lambda i: (nb - 1 - i, 0)),
                  pl.BlockSpec((tl, FOX_W), lambda i: (nb - 1 - i, 0)),
                  pl.BlockSpec((tl, LANES), lambda i: (nb - 1 - i, FL_BLK)),
                  pl.BlockSpec((1, LANES), lambda i: (0, 0))],
        out_specs=[pl.BlockSpec((tl, LANES), lambda i: (nb - 1 - i, 0)), pl.BlockSpec((1, LANES), lambda i: (0, 0))],
        out_shape=[_sds((L, LANES)), _sds((1, LANES))],
        scratch_shapes=[pltpu.VMEM((1, LANES), F32)],
        compiler_params=_cp("arbitrary"),
    )(dFk, dfq, z, bf)


def _head_mask(hh):
    lane = lax.broadcasted_iota(jnp.int32, (1, LANES), 1)
    return (lane >> 6) == hh


FOX_T = 512


def _fox_head(x, hh):
    return jnp.where(_head_mask(hh), x, 0.0).astype(BF16)


def _fox_scores(qh, k, fq_ref, fr_ref, hh, causal):
    s = _dot(qh, k, NT) + (fq_ref[:, 64 * hh:64 * hh + 1] - fr_ref[hh:hh + 1, :])
    return s if causal is None else jnp.where(causal, s, NEG)


def _causal(T):
    return lax.broadcasted_iota(jnp.int32, (T, T), 1) <= lax.broadcasted_iota(jnp.int32, (T, T), 0)


def _fox_fwd(z, fq, frow, name):
    L = z.shape[0]
    T = min(FOX_T, L)
    nq = L // T

    def body(qt_ref, kt_ref, q_ref, k_ref, v_ref, fq_ref, fr_ref, o_ref, lse_ref, m_ref, l_ref, acc_ref):
        t = pl.program_id(1)
        qi, ki = qt_ref[t], kt_ref[t]

        @pl.when(ki == 0)
        def _():
            m_ref[...] = jnp.full_like(m_ref, NEG)
            l_ref[...] = jnp.zeros_like(l_ref)
            acc_ref[...] = jnp.zeros_like(acc_ref)

        def step(diagonal):
            q = q_ref[...] * 0.125
            k = k_ref[...].astype(BF16)
            v = v_ref[...].astype(BF16)
            causal = _causal(T) if diagonal else None
            s = jnp.concatenate([_fox_scores(_fox_head(q, hh), k, fq_ref, fr_ref, hh, causal) for hh in range(2)],
                                axis=0)
            m_old = m_ref[...]
            m_new = jnp.maximum(m_old, jnp.max(s, axis=1, keepdims=True))
            alpha = jnp.exp(m_old - m_new)
            p = jnp.exp(s - m_new)
            l_ref[...] = alpha * l_ref[...] + jnp.sum(p, axis=1, keepdims=True)
            m_ref[...] = m_new
            acc_ref[...] = alpha * acc_ref[...] + _dot(p.astype(BF16), v)

        @pl.when(ki < qi)
        def _():
            step(False)

        @pl.when(ki == qi)
        def _():
            step(True)
            h0 = _head_mask(0)
            l = l_ref[...]
            o_h = acc_ref[...] / l
            lse_h = m_ref[...] + jnp.log(l)
            o_ref[...] = jnp.where(h0, o_h[:T], o_h[T:])
            lse_ref[...] = jnp.where(h0, lse_h[:T], lse_h[T:])

    pairs = [(qi, ki) for qi in range(nq) for ki in range(qi + 1)]
    qt = jnp.asarray([p[0] for p in pairs], jnp.int32)
    kt = jnp.asarray([p[1] for p in pairs], jnp.int32)

    def qspec(base):
        return pl.BlockSpec((T, LANES), lambda j, t, qt, kt: (qt[t], base + j))

    def kspec(base):
        return pl.BlockSpec((T, LANES), lambda j, t, qt, kt: (kt[t], base + j))

    return pl.pallas_call(
        body, name=name,
        grid_spec=pltpu.PrefetchScalarGridSpec(
            num_scalar_prefetch=2, grid=(4, len(pairs)),
            in_specs=[qspec(Q_BLK), kspec(K_BLK), kspec(V_BLK), qspec(0),
                      pl.BlockSpec((None, 2, T), lambda j, t, qt, kt: (j, 0, kt[t]))],
            out_specs=[qspec(0), qspec(0)],
            scratch_shapes=[pltpu.VMEM((2 * T, 1), F32), pltpu.VMEM((2 * T, 1), F32),
                            pltpu.VMEM((2 * T, LANES), F32)]),
        out_shape=[_sds((L, FOX_W)), _sds((L, FOX_W))],
        compiler_params=_cp("parallel", "arbitrary"),
    )(qt, kt, z, z, z, fq, frow)


def _fox_bwd(z, fq, frow, o, lse, g_m, name):
    L = z.shape[0]
    T = min(FOX_T, L)
    nq = L // T

    pairs = [(qi, ki) for ki in range(nq) for qi in range(ki, nq)]
    qt = jnp.asarray([p[0] for p in pairs], jnp.int32)
    kt = jnp.asarray([p[1] for p in pairs], jnp.int32)

    def body(qt_ref, kt_ref, q_ref, k_ref, v_ref, fq_ref, fr_ref, o_ref, lse_ref, do_ref,
             dq_ref, dk_ref, dv_ref, dfq_ref, dfk_ref, dk_acc, dv_acc, df_acc):
        t = pl.program_id(1)
        qi, ki = qt_ref[t], kt_ref[t]

        @pl.when(t == 0)
        def _():
            dq_ref[...] = jnp.zeros_like(dq_ref)
            dfq_ref[...] = jnp.zeros_like(dfq_ref)

        @pl.when(qi == ki)
        def _():
            dk_acc[...] = jnp.zeros_like(dk_acc)
            dv_acc[...] = jnp.zeros_like(dv_acc)
            df_acc[...] = jnp.zeros_like(df_acc)

        def step(diagonal):
            q = q_ref[...] * 0.125
            qb = q.astype(BF16)
            k = k_ref[...].astype(BF16)
            v = v_ref[...].astype(BF16)
            do = do_ref[...]
            dob = do.astype(BF16)
            do_o = dob.astype(F32) * o_ref[...]
            causal = _causal(T) if diagonal else None
            dvs, dks, dqs, rss = [], [], [], []
            for hh in range(2):
                s = _fox_scores(_fox_head(q, hh), k, fq_ref, fr_ref, hh, causal)
                p = jnp.exp(s - lse_ref[:, 64 * hh:64 * hh + 1])
                dp = _dot(_fox_head(do, hh), v, NT)
                delta = jnp.sum(jnp.where(_head_mask(hh), do_o, 0.0), axis=1, keepdims=True)
                ds = p * (dp - delta)
                dsb = ds.astype(BF16)
                dvs.append(_dot(p.astype(BF16), dob, TN))
                dks.append(_dot(dsb, qb, TN))
                dqs.append(_dot(dsb, k))
                rss.append(jnp.sum(ds, axis=1, keepdims=True))
                df_acc[hh:hh + 1, :] -= jnp.sum(ds, axis=0, keepdims=True)
            h0 = _head_mask(0)
            dv_acc[...] += jnp.where(h0, dvs[0], dvs[1])
            dk_acc[...] += jnp.where(h0, dks[0], dks[1])
            rows = pl.ds(pl.multiple_of(qi * T, T), T)
            dq_ref[rows, :] += jnp.where(h0, dqs[0], dqs[1])
            dfq_ref[rows, :] += jnp.where(h0, rss[0], rss[1])

        @pl.when(qi > ki)
        def _():
            step(False)

        @pl.when(qi == ki)
        def _():
            step(True)

        @pl.when(qi == nq - 1)
        def _():
            dk_ref[...] = dk_acc[...]
            dv_ref[...] = dv_acc[...]
            dfk_ref[...] = df_acc[...]

        @pl.when(t == len(pairs) - 1)
        def _():
            dq_ref[...] = dq_ref[...] * 0.125

    def qside(base):
        return pl.BlockSpec((T, LANES), lambda j, t, qt, kt: (qt[t], base + j))

    def kside(base):
        return pl.BlockSpec((T, LANES), lambda j, t, qt, kt: (kt[t], base + j))

    pair = pl.BlockSpec((L, LANES), lambda j, t, qt, kt: (0, j))
    frow_spec = pl.BlockSpec((None, 2, T), lambda j, t, qt, kt: (j, 0, kt[t]))
    return pl.pallas_call(
        body, name=name,
        grid_spec=pltpu.PrefetchScalarGridSpec(
            num_scalar_prefetch=2, grid=(4, len(pairs)),
            in_specs=[qside(Q_BLK), kside(K_BLK), kside(V_BLK), qside(0), frow_spec, qside(0), qside(0), qside(4)],
            out_specs=[pair, kside(0), kside(0), pair, frow_spec],
            scratch_shapes=[pltpu.VMEM((T, LANES), F32), pltpu.VMEM((T, LANES), F32), pltpu.VMEM((2, T), F32)]),
        out_shape=[_sds((L, FOX_W)), _sds((L, FOX_W)), _sds((L, FOX_W)), _sds((L, FOX_W)), _sds((4, 2, L))],
        compiler_params=_cp("parallel", "arbitrary"),
    )(qt, kt, z, z, z, fq, frow, o, lse, g_m)


def _shift_rows(v, s, down, row):
    n = v.shape[0]
    if down:
        return jnp.where(row >= s, pltpu.roll(v, s, 0), 0.0)
    return jnp.where(row < n - s, pltpu.roll(v, n - s, 0), 0.0)


def _window_sum(v, g, down, row):
    out = jnp.zeros_like(v)
    s = v
    for k in range(4):
        s = s + _shift_rows(s, 1 << k, down, row)
        out = jnp.where(g == k, s, out)
    return out


def _pool_inv_cnt(g, row):
    w = jnp.left_shift(2, g).astype(F32)
    return 1.0 / jnp.minimum(row.astype(F32) + 1.0, w)


def _pool_fwd(z, pool_w, scale, name):
    L = z.shape[0]

    def body(x_ref, w_ref, s_ref, y_ref, p_ref):
        g = pl.program_id(0)
        row = lax.broadcasted_iota(jnp.int32, (L, LANES), 0)
        x = x_ref[...]
        pooled = (_window_sum(x, g, True, row) * _pool_inv_cnt(g, row) - x).astype(BF16)
        p_ref[...] = pooled
        y_ref[...] = (_dot(pooled, w_ref[...].astype(BF16)) * s_ref[...]).astype(BF16)

    col = pl.BlockSpec((L, LANES), lambda g: (0, g))
    return pl.pallas_call(
        body, name=name, grid=(4,),
        in_specs=[col, pl.BlockSpec((None, LANES, LANES), lambda g: (g, 0, 0)), pl.BlockSpec((1, LANES), lambda g: (0, g))],
        out_specs=[col, col],
        out_shape=[_sds((L, 512), BF16), _sds((L, 512), BF16)],
        compiler_params=_cp("parallel"),
    )(z, pool_w, scale)


def _pool_bwd(g_m, pooled, pool_w, scale, name):
    L = g_m.shape[0]

    def body(g_ref, p_ref, w_ref, s_ref, gx_ref, gw_ref, gs_ref):
        g = pl.program_id(0)
        row = lax.broadcasted_iota(jnp.int32, (L, LANES), 0)
        gy = g_ref[...]
        pooled = p_ref[...]
        wb = w_ref[...].astype(BF16)
        lin = _dot(pooled, wb)
        gs_ref[...] = jnp.sum(gy * lin, axis=0, keepdims=True)
        glin = (gy * s_ref[...]).astype(BF16)
        gw_ref[...] = _dot(pooled, glin, TN)
        gp = _dot(glin, wb, NT)
        gx_ref[...] = _window_sum(gp * _pool_inv_cnt(g, row), g, False, row) - gp

    col = pl.BlockSpec((L, LANES), lambda g: (0, g))
    wspec = pl.BlockSpec((None, LANES, LANES), lambda g: (g, 0, 0))
    vec = pl.BlockSpec((1, LANES), lambda g: (0, g))
    return pl.pallas_call(
        body, name=name, grid=(4,),
        in_specs=[col, col, wspec, vec],
        out_specs=[col, wspec, vec],
        out_shape=[_sds((L, 512)), _sds((4, LANES, LANES)), _sds((1, 512))],
        compiler_params=_cp("parallel"),
    )(g_m, pooled, pool_w, scale)


SGU_CHUNKS = 4


def _sgu_ln(v, gam, bet):
    gv = _gelu(v)
    mu = jnp.mean(gv, axis=-1, keepdims=True)
    xc = gv - mu
    rs = lax.rsqrt(jnp.mean(xc * xc, axis=-1, keepdims=True) + EPS)
    xh = xc * rs
    return xh, rs, xh * gam + bet


def _tril_ws(w_ref, g):
    r = lax.broadcasted_iota(jnp.int32, (LANES, LANES), 0)
    c = lax.broadcasted_iota(jnp.int32, (LANES, LANES), 1)
    return jnp.where(r >= c, w_ref[g], 0.0).astype(BF16)


def _sgu_fwd(z, ln_g, ln_b, w_s, b_st, name):
    L = z.shape[0]
    rb = min(SGU_CHUNKS * LANES, L)

    def body(u_ref, v_ref, g_ref, b_ref, w_ref, bs_ref, y_ref):
        _, _, vln = _sgu_ln(v_ref[...], g_ref[...], b_ref[...])
        gu = _gelu(u_ref[...])
        vb = vln.astype(BF16)
        for g in range(4):
            ws = _tril_ws(w_ref, g)
            for n in range(rb // LANES):
                rows = slice(n * LANES, (n + 1) * LANES)
                cols = slice(g * LANES, (g + 1) * LANES)
                mixed = _dot(ws, vb[rows, cols]) + bs_ref[:, g:g + 1]
                y_ref[rows, cols] = (gu[rows, cols] * mixed).astype(BF16)

    vm = lambda shape: pl.BlockSpec(shape, lambda i: tuple(0 for _ in shape))
    return pl.pallas_call(
        body, name=name, grid=(L // rb,),
        in_specs=[pl.BlockSpec((rb, 512), lambda i: (i, 1)), pl.BlockSpec((rb, 512), lambda i: (i, 2)),
                  vm((1, 512)), vm((1, 512)), vm((4, LANES, LANES)), vm((LANES, 4))],
        out_specs=pl.BlockSpec((rb, 512), lambda i: (i, 0)),
        out_shape=_sds((L, 512), BF16),
        compiler_params=_cp("parallel"),
    )(z, z, ln_g, ln_b, w_s, b_st)


def _sgu_bwd(g_m, z, ln_g, ln_b, w_s, b_st, name):
    L = z.shape[0]
    rb = min(SGU_CHUNKS * LANES, L)

    def body(gy_ref, u_ref, v_ref, g_ref, b_ref, w_ref, bs_ref, gu_ref, gv_ref, gw_ref, gbs_ref, gg_ref, gb_ref):
        i = pl.program_id(0)

        @pl.when(i == 0)
        def _():
            gw_ref[...] = jnp.zeros_like(gw_ref)
            gbs_ref[...] = jnp.zeros_like(gbs_ref)
            gg_ref[...] = jnp.zeros_like(gg_ref)
            gb_ref[...] = jnp.zeros_like(gb_ref)

        v = v_ref[...]
        u = u_ref[...]
        gy = gy_ref[...]
        xh, rs, vln = _sgu_ln(v, g_ref[...], b_ref[...])
        gel_u = _gelu(u)
        gmix = gy * gel_u
        vb = vln.astype(BF16)
        gmb = gmix.astype(BF16)
        r = lax.broadcasted_iota(jnp.int32, (LANES, LANES), 0)
        c = lax.broadcasted_iota(jnp.int32, (LANES, LANES), 1)
        gvln_cols = []
        for g in range(4):
            ws = _tril_ws(w_ref, g)
            cols = slice(g * LANES, (g + 1) * LANES)
            gw = jnp.zeros((LANES, LANES), F32)
            gbs = jnp.zeros((LANES, 1), F32)
            parts = []
            for n in range(rb // LANES):
                rows = slice(n * LANES, (n + 1) * LANES)
                mixed = _dot(ws, vb[rows, cols]) + bs_ref[:, g:g + 1]
                gu_ref[rows, cols] = gy[rows, cols] * mixed * _gelu_grad(u[rows, cols])
                parts.append(_dot(ws, gmb[rows, cols], TN))
                gw = gw + _dot(gmb[rows, cols], vb[rows, cols], NT)
                gbs = gbs + jnp.sum(gmix[rows, cols], axis=1, keepdims=True)
            gvln_cols.append(jnp.concatenate(parts, axis=0))
            gw_ref[g] += jnp.where(r >= c, gw, 0.0)
            gbs_ref[:, g:g + 1] += gbs
        gvln = jnp.concatenate(gvln_cols, axis=1)
        gg_ref[...] += jnp.sum(gvln * xh, axis=0, keepdims=True)
        gb_ref[...] += jnp.sum(gvln, axis=0, keepdims=True)
        gxh = gvln * g_ref[...]
        ggv = rs * (gxh - jnp.mean(gxh, axis=-1, keepdims=True) - xh * jnp.mean(gxh * xh, axis=-1, keepdims=True))
        gv_ref[...] = ggv * _gelu_grad(v)

    vm = lambda shape: pl.BlockSpec(shape, lambda i: tuple(0 for _ in shape))
    blk = pl.BlockSpec((rb, 512), lambda i: (i, 0))
    return pl.pallas_call(
        body, name=name, grid=(L // rb,),
        in_specs=[pl.BlockSpec((rb, 512), lambda i: (i, 1)), pl.BlockSpec((rb, 512), lambda i: (i, 1)),
                  pl.BlockSpec((rb, 512), lambda i: (i, 2)),
                  vm((1, 512)), vm((1, 512)), vm((4, LANES, LANES)), vm((LANES, 4))],
        out_specs=[blk, blk, vm((4, LANES, LANES)), vm((LANES, 4)), vm((1, 512)), vm((1, 512))],
        out_shape=[_sds((L, 512)), _sds((L, 512)), _sds((4, LANES, LANES)), _sds((LANES, 4)),
                   _sds((1, 512)), _sds((1, 512))],
        compiler_params=_cp("arbitrary"),
    )(g_m, z, z, ln_g, ln_b, w_s, b_st)


def _adamw_math(w, g, m, v):
    nm = ADAM_B1 * m + (1.0 - ADAM_B1) * g
    nv = ADAM_B2 * v + (1.0 - ADAM_B2) * (g * g)
    m_hat = nm / (1.0 - ADAM_B1 ** ADAM_STEP)
    v_hat = nv / (1.0 - ADAM_B2 ** ADAM_STEP)
    delta = -ADAM_LR * (m_hat / (jnp.sqrt(v_hat) + ADAM_EPS) + ADAM_WD * w)
    return delta, nm, nv


def _sum_adamw(parts, w, m, v, name, layer=0, prev=None):
    n_layers, R, C = w.shape
    rb = 128 if R % 128 == 0 else R

    def body(p_ref, w_ref, m_ref, v_ref, *rest):
        g_ref, d_ref, nm_ref, nv_ref = rest[-4:]
        g = p_ref[0].astype(F32)
        for s in range(1, N_DEV):
            g = g + p_ref[s].astype(F32)
        d, nm, nv = _adamw_math(w_ref[...], g, m_ref[...], v_ref[...])
        g_ref[...] = g
        d_ref[...] = d
        nm_ref[...] = nm
        nv_ref[...] = nv

    blk = pl.BlockSpec((None, rb, C), lambda i: (layer, i, 0))
    prev = [] if prev is None else list(prev)
    return pl.pallas_call(
        body, name=name, grid=(R // rb,),
        in_specs=[pl.BlockSpec((N_DEV, rb, C), lambda i: (0, i, 0)), blk, blk, blk] + [ANY] * len(prev),
        out_specs=[blk] * 4, out_shape=[_sds((n_layers, R, C))] * 4,
        input_output_aliases={4 + k: k for k in range(len(prev))},
        compiler_params=_cp("parallel"),
    )(parts, w, m, v, *prev)


def _sum_pieces(parts, name):
    _, R, C = parts.shape

    def body(p_ref, g_ref):
        g = p_ref[0].astype(F32)
        for s in range(1, N_DEV):
            g = g + p_ref[s].astype(F32)
        g_ref[...] = g

    vm = pl.BlockSpec(memory_space=pltpu.VMEM)
    return pl.pallas_call(body, name=name, in_specs=[vm], out_specs=vm, out_shape=_sds((R, C)),
                          compiler_params=pltpu.CompilerParams(vmem_limit_bytes=VMEM_LIMIT))(parts)


def _adamw_many(ws, gs, ms, vs, name):
    n = len(ws)
    vm = pl.BlockSpec(memory_space=pltpu.VMEM)

    def body(*refs):
        w_refs, g_refs, m_refs, v_refs = refs[:n], refs[n:2 * n], refs[2 * n:3 * n], refs[3 * n:4 * n]
        d_refs, nm_refs, nv_refs = refs[4 * n:5 * n], refs[5 * n:6 * n], refs[6 * n:7 * n]
        for i in range(n):
            d, nm, nv = _adamw_math(w_refs[i][...], g_refs[i][...], m_refs[i][...], v_refs[i][...])
            d_refs[i][...] = d
            nm_refs[i][...] = nm
            nv_refs[i][...] = nv

    shapes = [_sds(w.shape) for w in ws]
    outs = pl.pallas_call(
        body, name=name, in_specs=[vm] * (4 * n), out_specs=[vm] * (3 * n), out_shape=shapes * 3,
        compiler_params=pltpu.CompilerParams(vmem_limit_bytes=VMEM_LIMIT),
    )(*ws, *gs, *ms, *vs)
    return list(outs[:n]), list(outs[n:2 * n]), list(outs[2 * n:])


def _mesh_pos():
    return lax.axis_index("x"), lax.axis_index("y"), lax.axis_index("c")


def _dev_index(p):
    return 4 * p[0] + 2 * p[1] + p[2]


HBM = pl.BlockSpec(memory_space=pltpu.HBM)
SEM = pl.BlockSpec(memory_space=pltpu.SEMAPHORE)
EFFECT = pltpu.SideEffectType.DATAFLOW_SIDE_EFFECTING


def _peer_list():
    x, y, c = _mesh_pos()
    peers = [(x ^ dx, y ^ dy, c ^ dc) for dx in range(2) for dy in range(2) for dc in range(2)][1:]
    return (x, y, c), peers


def _split_copy(src_ref, land_ref, send_sems, recv_sems, i, k, peer, slot, exchange):
    return pltpu.make_async_remote_copy(
        src_ref=src_ref.at[_dev_index(peer)] if exchange else src_ref, dst_ref=land_ref.at[slot],
        send_sem=send_sems.at[7 * i + k], recv_sem=recv_sems.at[7 * i + k], device_id=peer, device_id_type=MESH)


def _comm_start(groups, name, exchange, dep=None):
    sizes = [len(g) for g in groups]
    n = sum(sizes)
    srcs = [a for g in groups for a in g]
    my_index = _dev_index(_mesh_pos())
    lands = []
    for a in srcs:
        if exchange:
            own = lax.dynamic_slice(a, (my_index, 0, 0), (1,) + a.shape[1:])
            shape = a.shape
        else:
            own = a[None]
            shape = (N_DEV,) + a.shape
        lands.append(lax.dynamic_update_slice(lax.empty(shape, a.dtype), own, (my_index, 0, 0)))

    n_dep = 0 if dep is None else 1

    def body(*refs):
        src_refs, land_refs = refs[:n], refs[n:2 * n]
        sem_refs = refs[2 * n + n_dep:2 * n + n_dep + 2 * len(sizes)]
        token_ref = refs[-1]
        me, peers = _peer_list()
        mi = _dev_index(me)
        i = 0
        for gi, sz in enumerate(sizes):
            for j in range(sz):
                for k, peer in enumerate(peers):
                    _split_copy(src_refs[i], land_refs[i], sem_refs[2 * gi], sem_refs[2 * gi + 1], j, k, peer, mi,
                                exchange).start()
                i += 1
        token_ref[...] = jnp.zeros_like(token_ref)

    sem_shapes = []
    for sz in sizes:
        sem_shapes += [pltpu.SemaphoreType.DMA((7 * sz,)), pltpu.SemaphoreType.DMA((7 * sz,))]
    thru = [pltpu.HBM(a.shape, a.dtype) for a in srcs + lands]
    n_sem = len(sem_shapes)
    outs = pl.pallas_call(
        body, name=name,
        out_shape=tuple(sem_shapes + thru + [_sds((8, LANES))]),
        in_specs=[HBM] * (2 * n) + [ANY] * n_dep,
        out_specs=tuple([SEM] * n_sem + [HBM] * (2 * n) + [pl.BlockSpec(memory_space=pltpu.VMEM)]),
        input_output_aliases={i: n_sem + i for i in range(2 * n)},
        compiler_params=pltpu.CompilerParams(has_side_effects=EFFECT),
    )(*[pltpu.with_memory_space_constraint(a, pltpu.HBM) for a in srcs + lands], *([] if dep is None else [dep]))
    sems, thru_src, thru_land, token = outs[:n_sem], outs[n_sem:n_sem + n], outs[n_sem + n:n_sem + 2 * n], outs[-1]
    result, off = [], 0
    for gi, sz in enumerate(sizes):
        result.append((sems[2 * gi], sems[2 * gi + 1], list(thru_src[off:off + sz]), list(thru_land[off:off + sz])))
        off += sz
    return result, token


def _comm_wait(group, after, name, exchange):
    send_sems, recv_sems, srcs, lands = group
    n = len(srcs)
    after = list(after) if isinstance(after, (list, tuple)) else [after]

    def body(*refs):
        src_refs, land_refs = refs[:n], refs[n:2 * n]
        ssem, rsem = refs[2 * n], refs[2 * n + 1]
        me, peers = _peer_list()
        for i in range(n):
            for k, peer in enumerate(peers):
                cp = _split_copy(src_refs[i], land_refs[i], ssem, rsem, i, k, peer, _dev_index(peer), exchange)
                cp.wait_send()
                cp.wait_recv()

    outs = pl.pallas_call(
        body, name=name,
        out_shape=tuple(pltpu.HBM(a.shape, a.dtype) for a in srcs + lands),
        in_specs=[HBM] * (2 * n) + [SEM, SEM] + [ANY] * len(after),
        out_specs=tuple([HBM] * (2 * n)),
        input_output_aliases={i: i for i in range(2 * n)},
        compiler_params=pltpu.CompilerParams(has_side_effects=EFFECT),
    )(*srcs, *lands, send_sems, recv_sems, *after)
    return list(outs[n:])


def _tie(a, token):
    return a + token[0, 0].astype(a.dtype)


def _pack(arrs, rows):
    flat = jnp.concatenate([a.reshape(-1).astype(F32) for a in arrs])
    return jnp.pad(flat, (0, rows * LANES - flat.shape[0])).reshape(rows, LANES)


def _unpack(packed, shapes):
    flat = packed.reshape(-1)
    out, off = [], 0
    for s in shapes:
        n = math.prod(s)
        out.append(flat[off:off + n].reshape(s))
        off += n
    return out


def _packed_rows(shapes):
    n = sum(math.prod(s) for s in shapes)
    unit = N_DEV * 8 * LANES
    return -(-n // unit) * unit // LANES


def kernel(x, mix_pre_g, mix_post_g, mlp_pre_g, mlp_post_g, w_in_even, s5_lam_re, s5_lam_im, s5_log_dt, s5_b_re, s5_b_im, s5_c_re, s5_c_im, s5_d, s5_w_glu, fox_b_f, w_out_even, w_in_odd, pool_w, pool_scale, sgu_ln_g, sgu_ln_b, sgu_w_s, sgu_b_s, w_out_odd, mlp_w1, mlp_w2, loss_target, m_mix_pre_g, m_mix_post_g, m_mlp_pre_g, m_mlp_post_g, m_w_in_even, m_s5_lam_re, m_s5_lam_im, m_s5_log_dt, m_s5_b_re, m_s5_b_im, m_s5_c_re, m_s5_c_im, m_s5_d, m_s5_w_glu, m_fox_b_f, m_w_out_even, m_w_in_odd, m_pool_w, m_pool_scale, m_sgu_ln_g, m_sgu_ln_b, m_sgu_w_s, m_sgu_b_s, m_w_out_odd, m_mlp_w1, m_mlp_w2, v_mix_pre_g, v_mix_post_g, v_mlp_pre_g, v_mlp_post_g, v_w_in_even, v_s5_lam_re, v_s5_lam_im, v_s5_log_dt, v_s5_b_re, v_s5_b_im, v_s5_c_re, v_s5_c_im, v_s5_d, v_s5_w_glu, v_fox_b_f, v_w_out_even, v_w_in_odd, v_pool_w, v_pool_scale, v_sgu_ln_g, v_sgu_ln_b, v_sgu_w_s, v_sgu_b_s, v_w_out_odd, v_mlp_w1, v_mlp_w2):
    weights = dict(mix_pre_g=mix_pre_g, mix_post_g=mix_post_g, mlp_pre_g=mlp_pre_g, mlp_post_g=mlp_post_g, w_in_even=w_in_even, s5_lam_re=s5_lam_re, s5_lam_im=s5_lam_im, s5_log_dt=s5_log_dt, s5_b_re=s5_b_re, s5_b_im=s5_b_im, s5_c_re=s5_c_re, s5_c_im=s5_c_im, s5_d=s5_d, s5_w_glu=s5_w_glu, fox_b_f=fox_b_f, w_out_even=w_out_even, w_in_odd=w_in_odd, pool_w=pool_w, pool_scale=pool_scale, sgu_ln_g=sgu_ln_g, sgu_ln_b=sgu_ln_b, sgu_w_s=sgu_w_s, sgu_b_s=sgu_b_s, w_out_odd=w_out_odd, mlp_w1=mlp_w1, mlp_w2=mlp_w2)
    mom_m = dict(mix_pre_g=m_mix_pre_g, mix_post_g=m_mix_post_g, mlp_pre_g=m_mlp_pre_g, mlp_post_g=m_mlp_post_g, w_in_even=m_w_in_even, s5_lam_re=m_s5_lam_re, s5_lam_im=m_s5_lam_im, s5_log_dt=m_s5_log_dt, s5_b_re=m_s5_b_re, s5_b_im=m_s5_b_im, s5_c_re=m_s5_c_re, s5_c_im=m_s5_c_im, s5_d=m_s5_d, s5_w_glu=m_s5_w_glu, fox_b_f=m_fox_b_f, w_out_even=m_w_out_even, w_in_odd=m_w_in_odd, pool_w=m_pool_w, pool_scale=m_pool_scale, sgu_ln_g=m_sgu_ln_g, sgu_ln_b=m_sgu_ln_b, sgu_w_s=m_sgu_w_s, sgu_b_s=m_sgu_b_s, w_out_odd=m_w_out_odd, mlp_w1=m_mlp_w1, mlp_w2=m_mlp_w2)
    mom_v = dict(mix_pre_g=v_mix_pre_g, mix_post_g=v_mix_post_g, mlp_pre_g=v_mlp_pre_g, mlp_post_g=v_mlp_post_g, w_in_even=v_w_in_even, s5_lam_re=v_s5_lam_re, s5_lam_im=v_s5_lam_im, s5_log_dt=v_s5_log_dt, s5_b_re=v_s5_b_re, s5_b_im=v_s5_b_im, s5_c_re=v_s5_c_re, s5_c_im=v_s5_c_im, s5_d=v_s5_d, s5_w_glu=v_s5_w_glu, fox_b_f=v_fox_b_f, w_out_even=v_w_out_even, w_in_odd=v_w_in_odd, pool_w=v_pool_w, pool_scale=v_pool_scale, sgu_ln_g=v_sgu_ln_g, sgu_ln_b=v_sgu_ln_b, sgu_w_s=v_sgu_w_s, sgu_b_s=v_sgu_b_s, w_out_odd=v_w_out_odd, mlp_w1=v_mlp_w1, mlp_w2=v_mlp_w2)
    names = list(weights)
    L = x.shape[1]
    x0 = x[0]
    target = loss_target[0]
    my_index = 4 * lax.axis_index("x") + 2 * lax.axis_index("y") + lax.axis_index("c")

    small_vec = jnp.zeros((8, LANES), F32)
    small_vec = small_vec.at[0, :64].set(pool_scale[0]).at[1, :64].set(sgu_ln_g[0]).at[2, :64].set(sgu_ln_b[0])
    ag_first, ag_token0 = _comm_start([[jnp.transpose(w_in_even[0]).astype(BF16), small_vec]], "ag_start0",
                                      exchange=False)
    later = lambda w: _tie(w, ag_token0).astype(BF16)
    ag_rest, ag_token = _comm_start(
        [[later(s5_w_glu[0]), later(w_out_even[0])],
         [later(mlp_w1[0]), later(mlp_w2[0])],
         [later(jnp.transpose(w_in_odd[0])), later(w_out_odd[0]), later(mlp_w1[1]), later(mlp_w2[1])]],
        "ag_start", exchange=False)
    ag_groups = ag_first + ag_rest

    lam_r = jnp.concatenate([s5_lam_re.reshape(1, S5_NS), s5_lam_im.reshape(1, S5_NS)], axis=0)
    ldt_r = jnp.repeat(s5_log_dt.reshape(32), 64).reshape(1, S5_NS)
    lam_c = jnp.transpose(lam_r)
    ldt_c = jnp.transpose(ldt_r)
    b_t = jnp.stack([jnp.tile(s5_b_re.reshape(S5_NS, 16), (1, 8)), jnp.tile(s5_b_im.reshape(S5_NS, 16), (1, 8))])
    c_t = jnp.stack([jnp.tile(s5_c_re.reshape(S5_W, 64), (1, 8)), jnp.tile(s5_c_im.reshape(S5_W, 64), (1, 8))])
    bf_pad = jnp.pad(fox_b_f, ((0, 0), (0, LANES - 8)))
    b_st = jnp.transpose(sgu_b_s[0])

    h0, rx0 = _rms_fwd(x0, _tie(mix_pre_g[0:1], ag_token), "rms0")
    tabs, bset, cset = _s5_prep(lam_r, ldt_r, lam_c, ldt_c, b_t, c_t, "s5_prep")
    ag0 = _comm_wait(ag_groups[0], tabs, "ag_wait0", exchange=False)
    winT_e = jnp.pad(ag0[0].reshape(EVEN_IN, D_MODEL), ((0, EVEN_PAD - EVEN_IN), (0, 0)))
    pool_scale_f = ag0[1][:, 0, :64].reshape(1, 512)
    ln_g_f = ag0[1][:, 1, :64].reshape(1, 512)
    ln_b_f = ag0[1][:, 2, :64].reshape(1, 512)
    z0 = _mm(h0, winT_e, name="win_even", tb=True, bm=512, bn=EVEN_PAD)
    xs = _s5_scan_fwd(z0, bset, tabs, "s5_scan")
    ag1 = _comm_wait(ag_groups[1], xs, "ag_wait1", exchange=False)
    wglu = ag1[0].reshape(S5_W, S5_W)
    wout_e = ag1[1].reshape(D_MODEL, D_MODEL)
    ylin, ya = _s5_out_fwd(xs, cset, z0, s5_d, wglu, "s5_out")
    fcum, fq = _fox_f_fwd(z0, bf_pad, "fox_f")
    frow = jnp.transpose(fcum[:, :8]).reshape(4, 2, L)
    o_att, lse = _fox_fwd(z0, fq, frow, "fox_fwd")
    mix0 = [ya, o_att]
    y0 = _mm(mix0, wout_e, name="wout_even")
    x1, ry0, h1, rx1 = _post_pre_fwd(x0, y0, mix_post_g[0:1], mlp_pre_g[0:1], "post0")
    ag2 = _comm_wait(ag_groups[2], rx1, "ag_wait2", exchange=False)
    w1 = [ag2[0], None]
    w2 = [ag2[1].reshape(4 * D_MODEL, D_MODEL), None]
    p0, a0 = _mm(h1, w1[0], name="mlp0_w1", b3=True, out_dtypes=(BF16, BF16), epi=_epi_relu2, bm=2048)
    o0 = _mm(a0, w2[0], name="mlp0_w2", bm=2048)
    x2, ro0, h2, rx2 = _post_pre_fwd(x1, o0, mlp_post_g[0:1], mix_pre_g[1:2], "post1")
    ag3 = _comm_wait(ag_groups[3], rx2, "ag_wait3", exchange=False)
    winT_o = ag3[0].reshape(ODD_IN, D_MODEL)
    wout_o = ag3[1].reshape(D_MODEL, D_MODEL)
    w1[1] = ag3[2]
    w2[1] = ag3[3].reshape(4 * D_MODEL, D_MODEL)
    z1 = _mm(h2, winT_o, name="win_odd", tb=True, bn=ODD_IN)
    yc, pooled = _pool_fwd(z1, pool_w[0], pool_scale_f, "pool_fwd")
    yd = _sgu_fwd(z1, ln_g_f, ln_b_f, sgu_w_s[0], b_st, "sgu_fwd")
    mix1 = [yc, yd]
    y1 = _mm(mix1, wout_o, name="wout_odd")
    x3, ry1, h3, rx3 = _post_pre_fwd(x2, y1, mix_post_g[1:2], mlp_pre_g[1:2], "post2")
    p1, a1 = _mm(h3, w1[1], name="mlp1_w1", b3=True, out_dtypes=(BF16, BF16), epi=_epi_relu2, bm=2048)
    o1 = _mm(a1, w2[1], name="mlp1_w2", bm=2048)
    gx4, ro1, sq = _post_loss_fwd(x3, o1, mlp_post_g[1:2], target, "post3")

    g_o1, gg_mlp_post1 = _post_bwd(gx4, o1, ro1, mlp_post_g[1:2], "bpost3")
    g_p1 = _mm(g_o1, w2[1], name="b_mlp1_a", tb=True, out_dtypes=(BF16,), epi=_epi_relu2_bwd, extra=(p1,), bm=2048)
    gw2_1 = _mm(a1, g_o1, name="b_mlp1_w2", ta=True, bm=2048)
    g_h3 = _mm(g_p1, w1[1], name="b_mlp1_h", tb=True, b3=True, bm=2048)
    gw1_1 = _mm(h3, g_p1, name="b_mlp1_w1", ta=True, out3=True, bn=2048)
    (ex1,), tok1 = _comm_start([[gw1_1, gw2_1.reshape(N_DEV, 512, D_MODEL)]], "ex_start1", exchange=True)
    g_x3, gg_mlp_pre1, g_y1, gg_mix_post1 = _pre_post_bwd(g_h3, x3, rx3, _tie(mlp_pre_g[1:2], tok1), gx4, y1, ry1, mix_post_g[1:2], "bpre3")
    g_mix1 = _mm(g_y1, wout_o, name="b_wout_odd_m", tb=True)
    gwout_o = _mm(mix1, g_y1, name="b_wout_odd_w", ta=True)
    g_xc, g_pool_w, g_pool_scale = _pool_bwd(g_mix1, pooled, pool_w[0], pool_scale_f, "pool_bwd")
    g_u1, g_v1, g_ws, g_bst, g_ln_g, g_ln_b = _sgu_bwd(g_mix1, z1, ln_g_f, ln_b_f, sgu_w_s[0], b_st, "sgu_bwd")
    g_z1 = [g_xc, g_u1, g_v1]
    g_h2 = _mm(g_z1, winT_o, name="b_win_odd_h")
    gwinT_o = _mm(g_z1, h2, name="b_win_odd_w", ta=True)
    (ex2,), tok2 = _comm_start([[gwout_o.reshape(N_DEV, 128, D_MODEL), gwinT_o.reshape(N_DEV, ODD_IN // N_DEV, D_MODEL)]], "ex_start2", exchange=True)
    g_x2, gg_mix_pre1, g_o0, gg_mlp_post0 = _pre_post_bwd(g_h2, x2, rx2, _tie(mix_pre_g[1:2], tok2), g_x3, o0, ro0, mlp_post_g[0:1], "bpre2")
    g_p0 = _mm(g_o0, w2[0], name="b_mlp0_a", tb=True, out_dtypes=(BF16,), epi=_epi_relu2_bwd, extra=(p0,), bm=2048)
    gw2_0 = _mm(a0, g_o0, name="b_mlp0_w2", ta=True, bm=2048)
    g_h1 = _mm(g_p0, w1[0], name="b_mlp0_h", tb=True, b3=True, bm=2048)
    gw1_0 = _mm(h1, g_p0, name="b_mlp0_w1", ta=True, out3=True, bn=2048)
    (ex3,), tok3 = _comm_start([[gw1_0, gw2_0.reshape(N_DEV, 512, D_MODEL)]], "ex_start3", exchange=True)
    g_x1, gg_mlp_pre0, g_y0, gg_mix_post0 = _pre_post_bwd(g_h1, x1, rx1, _tie(mlp_pre_g[0:1], tok3), g_x2, y0, ry0, mix_post_g[0:1], "bpre1")
    g_mix0 = _mm(g_y0, wout_e, name="b_wout_even_m", tb=True)
    gwout_e = _mm(mix0, g_y0, name="b_wout_even_w", ta=True)
    gyl, gud, g_wglu, g_d = _s5_glu_bwd(g_mix0, ylin, z0, s5_d, wglu, "s5_glu_bwd")
    (ex4,), tok4 = _comm_start([[gwout_e.reshape(N_DEV, 128, D_MODEL), g_wglu.reshape(N_DEV, 64, S5_W)]], "ex_start4", exchange=True)
    g_u0, ga, gb_raw, gc_raw = _s5_scan_bwd(gyl, _tie(cset, tok4), xs, z0, bset, gud, tabs, "s5_scan_bwd")
    g_lam, g_ldt, g_b, g_c = _s5_param_bwd(lam_c, ldt_c, b_t, gb_raw, jnp.transpose(ga), gc_raw, "s5_param_bwd")
    dq, dk, dv, dfq, dfrow = _fox_bwd(z0, fq, frow, o_att, lse, g_mix0, "fox_bwd")
    dFk = jnp.pad(jnp.transpose(dfrow.reshape(8, L)), ((0, 0), (0, LANES - 8)))
    dfl, db_f = _fox_f_bwd(dFk, dfq, z0, bf_pad, "fox_f_bwd")
    g_z0 = [g_u0, dq, dk, dv, dfl]
    g_h0 = _mm(g_z0, winT_e, name="b_win_even_h")
    grad_x, gg_mix_pre0 = _pre_bwd(g_h0, x0, rx0, mix_pre_g[0:1], g_x1, "bpre0")

    small_grads = dict(
        mix_pre_g=jnp.concatenate([gg_mix_pre0, gg_mix_pre1]), mix_post_g=jnp.concatenate([gg_mix_post0, gg_mix_post1]),
        mlp_pre_g=jnp.concatenate([gg_mlp_pre0, gg_mlp_pre1]), mlp_post_g=jnp.concatenate([gg_mlp_post0, gg_mlp_post1]),
        s5_lam_re=g_lam[:, 0], s5_lam_im=g_lam[:, 1], s5_log_dt=g_ldt,
        s5_b_re=g_b[0, :, :16], s5_b_im=g_b[1, :, :16], s5_c_re=g_c[0, :, :64], s5_c_im=g_c[1, :, :64],
        s5_d=g_d, fox_b_f=db_f[:, :8], pool_w=g_pool_w, sgu_w_s=g_ws, sgu_b_s=jnp.transpose(g_bst),
        pool_scale=g_pool_scale, sgu_ln_g=g_ln_g, sgu_ln_b=g_ln_b)
    small_names = list(small_grads)
    full_shapes = [(512,) if nm in ("pool_scale", "sgu_ln_g", "sgu_ln_b") else weights[nm].shape for nm in small_names]
    full_shapes.append((1, 1))
    rows = _packed_rows(full_shapes)
    packed = _pack([small_grads[nm] for nm in small_names] + [sq], rows).reshape(N_DEV, rows // N_DEV, LANES)
    (exs,), tok_s = _comm_start([[packed]], "exs_start", exchange=True)
    gwinT_e = _mm(g_z0, h0, name="b_win_even_w", ta=True, bk=512, out_dtypes=(BF16,), dep=tok_s)
    (recv_small,) = _comm_wait(exs, gwinT_e, "exs_wait", exchange=True)
    piece = _sum_pieces(recv_small, "sum_small")
    (ags,), tok_a = _comm_start([[piece]], "ags_start", exchange=False)

    gwinT_e_pieces = gwinT_e[:EVEN_IN].reshape(N_DEV, EVEN_IN // N_DEV, D_MODEL)
    (ex5,), tok5 = _comm_start([[gwinT_e_pieces]], "ex_start5", exchange=True, dep=tok_a)
    r_w1_1, r_w2_1 = _comm_wait(ex1, tok5, "ex_wait1", exchange=True)
    r_wout_o, r_win_o = _comm_wait(ex2, tok5, "ex_wait2", exchange=True)
    r_w1_0, r_w2_0 = _comm_wait(ex3, tok5, "ex_wait3", exchange=True)
    r_wout_e, r_wglu = _comm_wait(ex4, tok5, "ex_wait4", exchange=True)

    res = {}
    for nm, parts in (("mlp_w1", (r_w1_0, r_w1_1)), ("mlp_w2", (r_w2_0, r_w2_1))):
        first = _sum_adamw(parts[0], weights[nm], mom_m[nm], mom_v[nm], "adamw_%s_0" % nm, layer=0)
        res[nm] = tuple(_sum_adamw(parts[1], weights[nm], mom_m[nm], mom_v[nm], "adamw_%s_1" % nm, layer=1, prev=first))
    big_parts = dict(s5_w_glu=r_wglu, w_out_even=r_wout_e, w_out_odd=r_wout_o)
    for nm, parts in big_parts.items():
        res[nm] = tuple(_sum_adamw(parts, weights[nm], mom_m[nm], mom_v[nm], "adamw_" + nm))
    done = [res[nm][1] for nm in ("mlp_w1", "mlp_w2", "s5_w_glu", "w_out_even", "w_out_odd")]

    (small_all,) = _comm_wait(ags, done, "ags_wait", exchange=False)
    small_full = _unpack(small_all.reshape(rows, LANES), full_shapes)
    loss = 0.5 * small_full.pop()[0, 0] / D_MODEL
    small_g = []
    for nm, g in zip(small_names, small_full):
        if nm in ("pool_scale", "sgu_ln_g", "sgu_ln_b"):
            g = lax.dynamic_slice(g, (my_index * 64,), (64,)).reshape(1, 64)
        small_g.append(g)
    (r_win_e,) = _comm_wait(ex5, done, "ex_wait5", exchange=True)
    last_names = small_names + ["w_in_odd", "w_in_even"]
    last_g = small_g + [jnp.transpose(_sum_pieces(r_win_o, "sum_w_in_odd"))[None],
                        jnp.transpose(_sum_pieces(r_win_e, "sum_w_in_even"))[None]]
    sd, sm, sv = _adamw_many([weights[nm] for nm in last_names], last_g, [mom_m[nm] for nm in last_names],
                             [mom_v[nm] for nm in last_names], "adamw_small")
    for nm, g_, d_, m_, v_ in zip(last_names, last_g, sd, sm, sv):
        res[nm] = (g_, d_, m_, v_)

    grads = [res[nm][0].reshape(weights[nm].shape) for nm in names]
    deltas = [res[nm][1].reshape(weights[nm].shape) for nm in names]
    new_m = [res[nm][2].reshape(weights[nm].shape) for nm in names]
    new_v = [res[nm][3].reshape(weights[nm].shape) for nm in names]
    return (loss, grad_x[None], *grads, *deltas, *new_m, *new_v)
```

```python
import functools
import math

import jax
import jax.numpy as jnp
from jax import lax
from jax.experimental import pallas as pl
from jax.experimental.pallas import tpu as pltpu

F32 = jnp.float32
BF16 = jnp.bfloat16
MESH = pl.DeviceIdType.MESH
ANY = pl.BlockSpec(memory_space=pl.ANY)

N_DEV = 8
D_MODEL = 1024
EPS = 1e-6
NORM_ROWS = 512
FUSED_ROWS = 512
S5_W = 512
S5_NS = 2048
SCAN_GROUPS = 4
SCAN_CHUNK = 1024
FOX_W = 512
EVEN_IN = 2056
EVEN_PAD = 2176
ODD_IN = 1536
LANES = 128
PIECE = 4 * D_MODEL // N_DEV
VMEM_LIMIT = 56 * 1024 * 1024

ADAM_LR = 0.001
ADAM_B1 = 0.9
ADAM_B2 = 0.999
ADAM_EPS = 1e-08
ADAM_WD = 0.01
ADAM_STEP = 10

NT = (((1,), (1,)), ((), ()))
TN = (((0,), (0,)), ((), ()))
NN = (((1,), (0,)), ((), ()))


def _cp(*sem):
    return pltpu.CompilerParams(dimension_semantics=sem, vmem_limit_bytes=VMEM_LIMIT)


def _sds(shape, dtype=F32):
    return jax.ShapeDtypeStruct(tuple(shape), dtype)


def _gelu(x):
    t = jnp.tanh(0.7978845608028654 * (x + 0.044715 * x * x * x))
    return 0.5 * x * (1.0 + t)


def _gelu_grad(x):
    t = jnp.tanh(0.7978845608028654 * (x + 0.044715 * x * x * x))
    du = 0.7978845608028654 * (1.0 + 3.0 * 0.044715 * x * x)
    return 0.5 * (1.0 + t) + 0.5 * x * (1.0 - t * t) * du


def _sigmoid(x):
    return 1.0 / (1.0 + jnp.exp(-x))


def _dot(a, b, dn=NN):
    return lax.dot_general(a, b, dn, preferred_element_type=F32)


def _mm(a, b, *, name, ta=False, tb=False, b3=False, out3=False, out_dtypes=(F32,), epi=None, extra=(),
        cols=(), vecs=(), out_kinds=None, bm=1024, bn=1024, bk=1024, dep=None):
    a_list = list(a) if isinstance(a, (list, tuple)) else [a]
    widths = [p.shape[1] for p in a_list]
    offs = [sum(widths[:i]) for i in range(len(widths))]
    na = len(a_list)
    M = sum(widths) if ta else a_list[0].shape[0]
    K = a_list[0].shape[0] if ta else sum(widths)
    if na > 1:
        assert not b3 and not tb
        bm, bk = (M, bk) if ta else (bm, K)
    pw = b.shape[2] if b3 else PIECE
    if b3:
        N = b.shape[1] if tb else b.shape[0] * pw
        assert (b.shape[0] * pw if tb else b.shape[1]) == K
    else:
        N = b.shape[0] if tb else b.shape[1]
    bm, bn, bk = min(bm, M), min(bn, N), min(bk, K)
    assert M % bm == 0 and N % bn == 0 and K % bk == 0, (name, M, N, K, bm, bn, bk)
    assert not (b3 or out3) or ((bk if tb else bn) % pw == 0 and bn % PIECE == 0)
    nk = K // bk
    n_extra = len(extra) + len(cols) + len(vecs)
    n_out = len(out_dtypes)
    out_kinds = tuple(out_kinds) if out_kinds is not None else ("full",) * n_out
    dn = (((0 if ta else 1,), (1 if tb else 0,)), ((), ()))

    use_acc = nk > 1

    def body(*refs):
        a_refs, b_ref = refs[:na], refs[na]
        a_ref = a_refs[0]
        e_refs = refs[na + 1:na + 1 + n_extra]
        first_out = na + 1 + n_extra + (0 if dep is None else 1)
        o_refs = refs[first_out:first_out + n_out]
        acc_ref = refs[-1] if use_acc else o_refs[0]
        i, k = pl.program_id(0), pl.program_id(2)

        def dot(a_v, b_v):
            return lax.dot_general(a_v.astype(BF16), b_v.astype(BF16), dn, preferred_element_type=F32)

        everything = slice(None)
        if na > 1 and ta:
            terms = [(pl.ds(off, w), everything, r, b_ref) for r, off, w in zip(a_refs, offs, widths)]
        elif na > 1:
            terms = [(everything, everything, r, b_ref.at[pl.ds(off, w), :]) for r, off, w in zip(a_refs, offs, widths)]
        elif not b3:
            terms = [(everything, everything, a_ref, b_ref)]
        elif tb:
            terms = [(everything, everything,
                      a_ref.at[pl.ds(t * pw, pw), :] if ta else a_ref.at[:, pl.ds(t * pw, pw)], b_ref.at[t])
                     for t in range(bk // pw)]
        else:
            terms = [(everything, pl.ds(t * pw, pw), a_ref, b_ref.at[t]) for t in range(bn // pw)]

        def finish(acc):
            outs = (acc,) if epi is None else epi(acc, *[e[...] for e in e_refs])
            for o_ref, o, kind in zip(o_refs, outs, out_kinds):
                if kind == "vsum":
                    @pl.when(i == 0)
                    def _(o_ref=o_ref, o=o):
                        o_ref[...] = o

                    @pl.when(i > 0)
                    def _(o_ref=o_ref, o=o):
                        o_ref[...] += o
                elif out3:
                    for t in range(bn // PIECE):
                        o_ref[t] = o[:, t * PIECE:(t + 1) * PIECE].astype(o_ref.dtype)
                else:
                    o_ref[...] = o.astype(o_ref.dtype)

        if nk == 1:
            bands = {}
            for rows, cols, a_r, b_r in terms:
                key = (getattr(rows, "start", None), getattr(cols, "start", None))
                val = dot(a_r[...], b_r[...])
                bands[key] = val if key not in bands else bands[key] + val
            vals = list(bands.values())
            if len(vals) == 1:
                finish(vals[0])
            else:
                finish(jnp.concatenate(vals, axis=0 if (na > 1 and ta) else 1))
            return

        @pl.when(k == 0)
        def _():
            acc_ref[...] = jnp.zeros_like(acc_ref)

        for rows, cols, a_r, b_r in terms:
            acc_ref[rows, cols] += dot(a_r[...], b_r[...])

        @pl.when(k == nk - 1)
        def _():
            finish(acc_ref[...])

    if na > 1:
        a_specs = [pl.BlockSpec((bk, w), lambda i, j, k: (k, 0)) if ta else pl.BlockSpec((bm, w), lambda i, j, k: (i, 0))
                   for w in widths]
    else:
        a_specs = [pl.BlockSpec((bk, bm), lambda i, j, k: (k, i)) if ta else
                   pl.BlockSpec((bm, bk), lambda i, j, k: (i, k))]
    if b3:
        if tb:
            b_spec = pl.BlockSpec((bk // pw, bn, pw), lambda i, j, k: (k, j, 0))
        else:
            b_spec = pl.BlockSpec((bn // pw, bk, pw), lambda i, j, k: (j, k, 0))
    else:
        b_spec = pl.BlockSpec((bn, bk), lambda i, j, k: (j, k)) if tb else pl.BlockSpec((bk, bn), lambda i, j, k: (k, j))
    e_specs = ([pl.BlockSpec((bm, bn), lambda i, j, k: (i, j)) for _ in extra]
               + [pl.BlockSpec((bm, 1), lambda i, j, k: (i, 0)) for _ in cols]
               + [pl.BlockSpec((1, bn), lambda i, j, k: (0, j)) for _ in vecs])
    if out3:
        o_specs = [pl.BlockSpec((bn // PIECE, bm, PIECE), lambda i, j, k: (j, i, 0)) for _ in out_dtypes]
        o_shapes = [_sds((N // PIECE, M, PIECE), dt) for dt in out_dtypes]
    else:
        spec_of = {"full": pl.BlockSpec((bm, bn), lambda i, j, k: (i, j)),
                   "col": pl.BlockSpec((bm, 1), lambda i, j, k: (i, 0)),
                   "vsum": pl.BlockSpec((1, bn), lambda i, j, k: (0, j))}
        shape_of = {"full": (M, N), "col": (M, 1), "vsum": (1, N)}
        o_specs = [spec_of[kind] for kind in out_kinds]
        o_shapes = [_sds(shape_of[kind], dt) for kind, dt in zip(out_kinds, out_dtypes)]
    assert "col" not in out_kinds or bn == N
    outs = pl.pallas_call(
        body, name=name, grid=(M // bm, N // bn, nk),
        in_specs=a_specs + [b_spec] + e_specs + ([] if dep is None else [ANY]),
        out_specs=o_specs, out_shape=o_shapes,
        scratch_shapes=[pltpu.VMEM((bm, bn), F32)] if use_acc else [],
        compiler_params=_cp("arbitrary" if "vsum" in out_kinds else "parallel", "parallel", "arbitrary"),
    )(*a_list, b, *extra, *cols, *vecs, *([] if dep is None else [dep]))
    return outs[0] if n_out == 1 else outs


def _epi_relu2(acc):
    r = jnp.maximum(acc, 0.0)
    return acc, r * r


def _epi_relu2_bwd(acc, p):
    return (acc * (2.0 * jnp.maximum(p.astype(F32), 0.0)),)


def _row_spec(rb, w=D_MODEL):
    return pl.BlockSpec((rb, w), lambda i: (i, 0))


def _vec_spec(w=D_MODEL):
    return pl.BlockSpec((1, w), lambda i: (0, 0))


def _rstd(v):
    return lax.rsqrt(jnp.mean(v * v, axis=-1, keepdims=True) + EPS)


def _rms_fwd(x, g, name):
    L = x.shape[0]
    rb = min(NORM_ROWS, L)

    def body(x_ref, g_ref, h_ref, r_ref):
        xv = x_ref[...]
        r = _rstd(xv)
        h_ref[...] = (xv * r * g_ref[...]).astype(BF16)
        r_ref[...] = r

    return pl.pallas_call(
        body, name=name, grid=(L // rb,),
        in_specs=[_row_spec(rb), _vec_spec()],
        out_specs=[_row_spec(rb), _row_spec(rb, 1)],
        out_shape=[_sds((L, D_MODEL), BF16), _sds((L, 1))],
        compiler_params=_cp("parallel"),
    )(x, g)


def _post_loss_fwd(x_in, y, g_post, target, name):
    L = x_in.shape[0]
    rb = min(NORM_ROWS, L)

    def body(x_ref, y_ref, gp_ref, t_ref, gx_ref, ry_ref, loss_ref):
        i = pl.program_id(0)
        yv = y_ref[...]
        ry = _rstd(yv)
        diff = x_ref[...] + yv * ry * gp_ref[...] - t_ref[...]
        gx_ref[...] = diff * (1.0 / D_MODEL)
        ry_ref[...] = ry

        @pl.when(i == 0)
        def _():
            loss_ref[...] = jnp.zeros_like(loss_ref)

        loss_ref[...] += jnp.sum(diff * diff, keepdims=True)

    return pl.pallas_call(
        body, name=name, grid=(L // rb,),
        in_specs=[_row_spec(rb), _row_spec(rb), _vec_spec(), _row_spec(rb)],
        out_specs=[_row_spec(rb), _row_spec(rb, 1), pl.BlockSpec((1, 1), lambda i: (0, 0))],
        out_shape=[_sds((L, D_MODEL)), _sds((L, 1)), _sds((1, 1))],
        compiler_params=_cp("arbitrary"),
    )(x_in, y, g_post, target)


def _rms_bwd_rows(dy, xv, r, g):
    n = xv * r
    dyg = dy * g
    return r * (dyg - n * jnp.mean(dyg * n, axis=-1, keepdims=True)), n


POST_PRE_DTYPES = (F32, F32, BF16, F32, F32)
POST_PRE_KINDS = ("full", "col", "full", "col", "full")
PRE_POST_BWD_DTYPES = (F32, F32, BF16, F32)
PRE_POST_BWD_KINDS = ("full", "vsum", "full", "vsum")


def _epi_post_pre(y, x_in, g_post, g_pre):
    ry = _rstd(y)
    xo = x_in + y * ry * g_post
    rx = _rstd(xo)
    return xo, ry, xo * rx * g_pre, rx, y


def _epi_pre_post_bwd(gh, x, g_out, y_prev, rx, ry_prev, g_pre, g_post_prev):
    gx, n = _rms_bwd_rows(gh, x, rx, g_pre)
    gi = g_out + gx
    gy, ny = _rms_bwd_rows(gi, y_prev, ry_prev, g_post_prev)
    return gi, jnp.sum(gh * n, axis=0, keepdims=True), gy, jnp.sum(gi * ny, axis=0, keepdims=True)


def _epi_pre_bwd(gh, x, g_out, rx, g_pre):
    gx, n = _rms_bwd_rows(gh, x, rx, g_pre)
    return g_out + gx, jnp.sum(gh * n, axis=0, keepdims=True)


def _post_bwd(g_out, y, ry, g_post, name):
    L = y.shape[0]
    rb = min(NORM_ROWS, L)

    def body(go_ref, y_ref, ry_ref, gp_ref, gy_ref, gg_ref):
        i = pl.program_id(0)
        go = go_ref[...]
        gy, n = _rms_bwd_rows(go, y_ref[...], ry_ref[...], gp_ref[...])
        gy_ref[...] = gy.astype(BF16)

        @pl.when(i == 0)
        def _():
            gg_ref[...] = jnp.zeros_like(gg_ref)

        gg_ref[...] += jnp.sum(go * n, axis=0, keepdims=True)

    return pl.pallas_call(
        body, name=name, grid=(L // rb,),
        in_specs=[_row_spec(rb), _row_spec(rb), _row_spec(rb, 1), _vec_spec()],
        out_specs=[_row_spec(rb), _vec_spec()],
        out_shape=[_sds((L, D_MODEL), BF16), _sds((1, D_MODEL))],
        compiler_params=_cp("arbitrary"),
    )(g_out, y, ry, g_post)


def _cmul(ar, ai, br, bi):
    return ar * br - ai * bi, ar * bi + ai * br


def _zoh_cols(lr, li, ldt):
    dt = jnp.exp(ldt)
    mag = jnp.exp(lr * dt)
    ar = mag * jnp.cos(li * dt)
    ai = mag * jnp.sin(li * dt)
    den = lr * lr + li * li
    nr = ar - 1.0
    qr = (nr * lr + ai * li) / den
    qi = (ai * lr - nr * li) / den
    return dt, ar, ai, qr, qi, den


def _b_mask():
    r = lax.broadcasted_iota(jnp.int32, (S5_NS, LANES), 0)
    c = lax.broadcasted_iota(jnp.int32, (S5_NS, LANES), 1)
    return ((r >> 6) & 7) == (c >> 4)


def _c_mask():
    r = lax.broadcasted_iota(jnp.int32, (S5_W, 512), 0)
    c = lax.broadcasted_iota(jnp.int32, (S5_W, 512), 1)
    return ((r >> 4) & 7) == (c >> 6)


def _s5_prep(lam_r, ldt_r, lam_c, ldt_c, b_t, c_t, name):
    def body(lam_r_ref, ldt_r_ref, lam_c_ref, ldt_c_ref, b_ref, c_ref, tab_ref, bset_ref, cset_ref):
        lr, li = lam_r_ref[0:1, :], lam_r_ref[1:2, :]
        dt = jnp.exp(ldt_r_ref[...])
        mag = jnp.exp(lr * dt)
        p1r, p1i = mag * jnp.cos(li * dt), mag * jnp.sin(li * dt)
        p2r, p2i = _cmul(p1r, p1i, p1r, p1i)
        p3r, p3i = _cmul(p2r, p2i, p1r, p1i)
        p4r, p4i = _cmul(p2r, p2i, p2r, p2i)
        p5r, p5i = _cmul(p4r, p4i, p1r, p1i)
        p6r, p6i = _cmul(p4r, p4i, p2r, p2i)
        p7r, p7i = _cmul(p4r, p4i, p3r, p3i)
        p8r, p8i = _cmul(p4r, p4i, p4r, p4i)
        pw_r = [p1r, p2r, p3r, p4r, p5r, p6r, p7r, p8r]
        pw_i = [p1i, p2i, p3i, p4i, p5i, p6i, p7i, p8i]
        row = lax.broadcasted_iota(jnp.int32, (8, S5_NS), 0)
        zero = jnp.zeros((8, S5_NS), F32)

        def bc(v):
            return jnp.broadcast_to(v, (8, S5_NS))

        for d in range(2):
            sgn = 1.0 if d == 0 else -1.0
            for t, s in enumerate((1, 2, 4)):
                live = (row >= s) if d == 0 else (row <= 7 - s)
                tab_ref[d, 2 * t] = jnp.where(live, bc(pw_r[s - 1]), zero)
                tab_ref[d, 2 * t + 1] = jnp.where(live, bc(sgn * pw_i[s - 1]), zero)
            cr, ci = zero, zero
            for i in range(8):
                e = i if d == 0 else 7 - i
                cr = jnp.where(row == i, bc(pw_r[e]), cr)
                ci = jnp.where(row == i, bc(sgn * pw_i[e]), ci)
            tab_ref[d, 6] = cr
            tab_ref[d, 7] = ci

        _, _, _, qr, qi, _ = _zoh_cols(lam_c_ref[:, 0:1], lam_c_ref[:, 1:2], ldt_c_ref[...])
        bm = _b_mask()
        br, bi = b_ref[0], b_ref[1]
        bset_ref[0] = jnp.where(bm, qr * br - qi * bi, 0.0).astype(BF16)
        bset_ref[1] = jnp.where(bm, qr * bi + qi * br, 0.0).astype(BF16)
        cm = _c_mask()
        cset_ref[0] = jnp.where(cm, c_ref[0], 0.0).astype(BF16)
        cset_ref[1] = jnp.where(cm, c_ref[1], 0.0).astype(BF16)

    vm = pl.BlockSpec(memory_space=pltpu.VMEM)
    return pl.pallas_call(
        body, name=name, in_specs=[vm] * 6, out_specs=[vm] * 3,
        out_shape=[_sds((2, 8, 8, S5_NS)), _sds((2, S5_NS, LANES), BF16), _sds((2, S5_W, 512), BF16)],
        compiler_params=pltpu.CompilerParams(vmem_limit_bytes=VMEM_LIMIT),
    )(lam_r, ldt_r, lam_c, ldt_c, b_t, c_t)


SCAN_W = SCAN_GROUPS * LANES


def _scan_chunk(src_ref, dst_ref, tab_ref, carry_ref, nb, reverse, xs_ref=None, acc_ref=None):
    row = lax.broadcasted_iota(jnp.int32, (8, LANES), 0)

    def step(i, carry):
        b = (nb - 1 - i) if reverse else i
        off = pl.multiple_of(b * 8, 8)
        out = []
        for g in range(SCAN_GROUPS):
            lanes = pl.ds(g * LANES, LANES)
            cr, ci = carry[2 * g], carry[2 * g + 1]
            yr = src_ref[0, pl.ds(off, 8), lanes]
            yi = src_ref[1, pl.ds(off, 8), lanes]
            for t, s in enumerate((1, 2, 4)):
                sh = (8 - s) if reverse else s
                sr = pltpu.roll(yr, sh, 0)
                si = pltpu.roll(yi, sh, 0)
                mr, mi = tab_ref[2 * t, :, lanes], tab_ref[2 * t + 1, :, lanes]
                yr, yi = yr + mr * sr - mi * si, yi + mr * si + mi * sr
            pr, pi = tab_ref[6, :, lanes], tab_ref[7, :, lanes]
            yr, yi = yr + pr * cr - pi * ci, yi + pr * ci + pi * cr
            dst_ref[0, pl.ds(off, 8), lanes] = yr
            dst_ref[1, pl.ds(off, 8), lanes] = yi
            if xs_ref is not None:
                nr = jnp.where(row == 7, cr, pltpu.roll(yr, 7, 0))
                ni = jnp.where(row == 7, ci, pltpu.roll(yi, 7, 0))
                xr = xs_ref[0, pl.ds(off, 8), lanes]
                xi = xs_ref[1, pl.ds(off, 8), lanes]
                acc_ref[0, :, lanes] += xr * nr + xi * ni
                acc_ref[1, :, lanes] += xr * ni - xi * nr
            last = 0 if reverse else 7
            out += [jnp.broadcast_to(yr[last:last + 1, :], (8, LANES)),
                    jnp.broadcast_to(yi[last:last + 1, :], (8, LANES))]
        return tuple(out)

    init = []
    for g in range(SCAN_GROUPS):
        init += [carry_ref[0, :, pl.ds(g * LANES, LANES)], carry_ref[1, :, pl.ds(g * LANES, LANES)]]
    fin = lax.fori_loop(0, nb, step, tuple(init))
    for g in range(SCAN_GROUPS):
        carry_ref[0, :, pl.ds(g * LANES, LANES)] = fin[2 * g]
        carry_ref[1, :, pl.ds(g * LANES, LANES)] = fin[2 * g + 1]


def _s5_scan_fwd(z, bset, tabs, name):
    L = z.shape[0]
    tl = min(SCAN_CHUNK, L)
    nc = L // tl

    def body(u_ref, b_ref, tab_ref, x_ref, carry_ref):
        @pl.when(pl.program_id(1) == 0)
        def _():
            carry_ref[...] = jnp.zeros_like(carry_ref)

        u = u_ref[...].astype(BF16)
        x_ref[0] = _dot(u, b_ref[0], NT)
        x_ref[1] = _dot(u, b_ref[1], NT)
        _scan_chunk(x_ref, x_ref, tab_ref, carry_ref, tl // 8, False)

    return pl.pallas_call(
        body, name=name, grid=(S5_NS // SCAN_W, nc),
        in_specs=[pl.BlockSpec((tl, LANES), lambda j, c: (c, j)),
                  pl.BlockSpec((2, SCAN_W, LANES), lambda j, c: (0, j, 0)),
                  pl.BlockSpec((None, 8, 8, SCAN_W), lambda j, c: (0, 0, 0, j))],
        out_specs=pl.BlockSpec((2, tl, SCAN_W), lambda j, c: (0, c, j)),
        out_shape=_sds((2, L, S5_NS)),
        scratch_shapes=[pltpu.VMEM((2, 8, SCAN_W), F32)],
        compiler_params=_cp("parallel", "arbitrary"),
    )(z, bset, tabs)


def _s5_scan_bwd(gyl, cset, xs, z, bset, gud, tabs, name):
    L = z.shape[0]
    tl = min(SCAN_CHUNK, L)
    nc = L // tl

    def body(g_ref, c_ref, xs_ref, u_ref, b_ref, gud_ref, tab_ref, gu_ref, ga_ref, gb_ref, gc_ref,
             gx_ref, carry_ref, acc_ref):
        c = pl.program_id(1)

        @pl.when(c == 0)
        def _():
            carry_ref[...] = jnp.zeros_like(carry_ref)
            acc_ref[...] = jnp.zeros_like(acc_ref)
            gb_ref[...] = jnp.zeros_like(gb_ref)
            gc_ref[...] = jnp.zeros_like(gc_ref)

        gy = g_ref[...].astype(BF16)
        gx_ref[0] = _dot(gy, c_ref[0])
        gx_ref[1] = -_dot(gy, c_ref[1])
        gc_ref[0] += _dot(gy, xs_ref[0].astype(BF16), TN)
        gc_ref[1] -= _dot(gy, xs_ref[1].astype(BF16), TN)
        _scan_chunk(gx_ref, gx_ref, tab_ref, carry_ref, tl // 8, True, xs_ref, acc_ref)
        gr = gx_ref[0].astype(BF16)
        gi = gx_ref[1].astype(BF16)
        gu_ref[...] = gud_ref[...] + _dot(gr, b_ref[0]) + _dot(gi, b_ref[1])
        u = u_ref[...].astype(BF16)
        gb_ref[0] += _dot(gr, u, TN)
        gb_ref[1] += _dot(gi, u, TN)

        @pl.when(c == nc - 1)
        def _():
            ga_ref[0:1, :] = jnp.sum(acc_ref[0], axis=0, keepdims=True)
            ga_ref[1:2, :] = jnp.sum(acc_ref[1], axis=0, keepdims=True)

    rev = lambda j, c: (nc - 1 - c, j)
    col = pl.BlockSpec((tl, LANES), rev)
    return pl.pallas_call(
        body, name=name, grid=(S5_NS // SCAN_W, nc),
        in_specs=[col, pl.BlockSpec((2, LANES, SCAN_W), lambda j, c: (0, j, 0)),
                  pl.BlockSpec((2, tl, SCAN_W), lambda j, c: (0, nc - 1 - c, j)), col,
                  pl.BlockSpec((2, SCAN_W, LANES), lambda j, c: (0, j, 0)), col,
                  pl.BlockSpec((None, 8, 8, SCAN_W), lambda j, c: (1, 0, 0, j))],
        out_specs=[col, pl.BlockSpec((2, SCAN_W), lambda j, c: (0, j)),
                   pl.BlockSpec((2, SCAN_W, LANES), lambda j, c: (0, j, 0)),
                   pl.BlockSpec((2, LANES, SCAN_W), lambda j, c: (0, j, 0))],
        out_shape=[_sds((L, S5_W)), _sds((2, S5_NS)), _sds((2, S5_NS, LANES)), _sds((2, S5_W, 512))],
        scratch_shapes=[pltpu.VMEM((2, tl, SCAN_W), F32), pltpu.VMEM((2, 8, SCAN_W), F32),
                        pltpu.VMEM((2, 8, SCAN_W), F32)],
        compiler_params=_cp("parallel", "arbitrary"),
    )(gyl, cset, xs, z, bset, gud, tabs)


def _s5_out_fwd(xs, cset, z, dvec, wglu, name):
    L = z.shape[0]
    bl = min(256, L)

    def body(x_ref, c_ref, u_ref, d_ref, w_ref, ylin_ref, ya_ref):
        cols = []
        for j in range(4):
            xr = x_ref[0, :, 512 * j:512 * (j + 1)].astype(BF16)
            xi = x_ref[1, :, 512 * j:512 * (j + 1)].astype(BF16)
            cr = c_ref[0, LANES * j:LANES * (j + 1), :]
            ci = c_ref[1, LANES * j:LANES * (j + 1), :]
            cols.append(_dot(xr, cr, NT) - _dot(xi, ci, NT))
        ylin = jnp.concatenate(cols, axis=1) + d_ref[...] * u_ref[...]
        yg = _gelu(ylin)
        t = _dot(yg.astype(BF16), w_ref[...])
        ylin_ref[...] = ylin
        ya_ref[...] = (yg * _sigmoid(t)).astype(BF16)

    return pl.pallas_call(
        body, name=name, grid=(L // bl,),
        in_specs=[pl.BlockSpec((2, bl, S5_NS), lambda i: (0, i, 0)),
                  pl.BlockSpec((2, S5_W, 512), lambda i: (0, 0, 0)),
                  pl.BlockSpec((bl, S5_W), lambda i: (i, 0)),
                  pl.BlockSpec((1, S5_W), lambda i: (0, 0)),
                  pl.BlockSpec((S5_W, S5_W), lambda i: (0, 0))],
        out_specs=[pl.BlockSpec((bl, S5_W), lambda i: (i, 0))] * 2,
        out_shape=[_sds((L, S5_W)), _sds((L, S5_W), BF16)],
        compiler_params=_cp("parallel"),
    )(xs, cset, z, dvec, wglu)


def _s5_glu_bwd(g_m, ylin, z, dvec, wglu, name):
    L = z.shape[0]
    bl = min(256, L)

    def body(g_ref, ylin_ref, u_ref, d_ref, w_ref, gyl_ref, gud_ref, gw_ref, gd_ref):
        i = pl.program_id(0)
        ylin = ylin_ref[...]
        yg = _gelu(ylin)
        ygb = yg.astype(BF16)
        sg = _sigmoid(_dot(ygb, w_ref[...]))
        gya = g_ref[...]
        gt = gya * yg * sg * (1.0 - sg)
        gtb = gt.astype(BF16)
        gyg = gya * sg + _dot(gtb, w_ref[...], NT)
        gyl = gyg * _gelu_grad(ylin)
        gyl_ref[...] = gyl
        gud_ref[...] = gyl * d_ref[...]

        @pl.when(i == 0)
        def _():
            gw_ref[...] = jnp.zeros_like(gw_ref)
            gd_ref[...] = jnp.zeros_like(gd_ref)

        gw_ref[...] += _dot(ygb, gtb, TN)
        gd_ref[...] += jnp.sum(gyl * u_ref[...], axis=0, keepdims=True)

    blk = pl.BlockSpec((bl, S5_W), lambda i: (i, 0))
    return pl.pallas_call(
        body, name=name, grid=(L // bl,),
        in_specs=[blk, blk, blk, pl.BlockSpec((1, S5_W), lambda i: (0, 0)),
                  pl.BlockSpec((S5_W, S5_W), lambda i: (0, 0))],
        out_specs=[blk, blk, pl.BlockSpec((S5_W, S5_W), lambda i: (0, 0)), pl.BlockSpec((1, S5_W), lambda i: (0, 0))],
        out_shape=[_sds((L, S5_W)), _sds((L, S5_W)), _sds((S5_W, S5_W)), _sds((1, S5_W))],
        compiler_params=_cp("arbitrary"),
    )(g_m, ylin, z, dvec, wglu)


def _s5_param_bwd(lam_c, ldt_c, b_t, gb, ga_c, gc, name):
    def body(lam_ref, ldt_ref, b_ref, gb_ref, ga_ref, gc_ref, glam_ref, gldt_ref, gbo_ref, gco_ref):
        lr, li = lam_ref[:, 0:1], lam_ref[:, 1:2]
        dt, ar, ai, qr, qi, den = _zoh_cols(lr, li, ldt_ref[...])
        bm = _b_mask()
        gbr = jnp.where(bm, gb_ref[0], 0.0)
        gbi = jnp.where(bm, gb_ref[1], 0.0)
        br, bi = b_ref[0], b_ref[1]
        obr = gbr * qr + gbi * qi
        obi = gbi * qr - gbr * qi
        gqr = jnp.sum(gbr * br + gbi * bi, axis=1, keepdims=True)
        gqi = jnp.sum(gbi * br - gbr * bi, axis=1, keepdims=True)
        for s in (64, 32, 16):
            obr = obr + pltpu.roll(obr, s, 1)
            obi = obi + pltpu.roll(obi, s, 1)
        gbo_ref[0] = obr
        gbo_ref[1] = obi
        gar = ga_ref[:, 0:1] + (gqr * lr - gqi * li) / den
        gai = ga_ref[:, 1:2] + (gqr * li + gqi * lr) / den
        qlr = (qr * lr + qi * li) / den
        qli = (qi * lr - qr * li) / den
        glr = -(gqr * qlr + gqi * qli)
        gli = -(gqi * qlr - gqr * qli)
        glr = glr + dt * (gar * ar + gai * ai)
        gli = gli + dt * (gai * ar - gar * ai)
        wr, wi = _cmul(lr, li, ar, ai)
        gldt = (gar * wr + gai * wi) * dt
        glam_ref[:, 0:1] = glr
        glam_ref[:, 1:2] = gli
        r = lax.broadcasted_iota(jnp.int32, (S5_NS, 32), 0)
        c = lax.broadcasted_iota(jnp.int32, (S5_NS, 32), 1)
        gldt_ref[...] = jnp.sum(jnp.where((r >> 6) == c, gldt, 0.0), axis=0, keepdims=True)
        cm = _c_mask()
        for k in range(2):
            oc = jnp.where(cm, gc_ref[k], 0.0)
            for s in (256, 128, 64):
                oc = oc + pltpu.roll(oc, s, 1)
            gco_ref[k] = oc[:, 0:LANES]

    vm = pl.BlockSpec(memory_space=pltpu.VMEM)
    return pl.pallas_call(
        body, name=name, in_specs=[vm] * 6, out_specs=[vm] * 4,
        out_shape=[_sds((S5_NS, 2)), _sds((1, 32)), _sds((2, S5_NS, LANES)), _sds((2, S5_W, LANES))],
        compiler_params=pltpu.CompilerParams(vmem_limit_bytes=VMEM_LIMIT),
    )(lam_c, ldt_c, b_t, gb, ga_c, gc)


FL_BLK = EVEN_PAD // LANES - 1
Q_BLK, K_BLK, V_BLK = 4, 8, 12
NEG = -1e30


def _log_sigmoid(v):
    return jnp.minimum(v, 0.0) - jnp.log(1.0 + jnp.exp(-jnp.abs(v)))


def _fox_f_fwd(z, bf, name):
    L = z.shape[0]
    tl = min(256, L)

    def body(fl_ref, b_ref, f_ref, fq_ref, carry_ref):
        i = pl.program_id(0)

        @pl.when(i == 0)
        def _():
            carry_ref[...] = jnp.zeros_like(carry_ref)

        lf = _log_sigmoid(fl_ref[...] + b_ref[...])
        r = lax.broadcasted_iota(jnp.int32, (tl, tl), 0)
        c = lax.broadcasted_iota(jnp.int32, (tl, tl), 1)
        tri = (r >= c).astype(F32)
        cs = lax.dot_general(tri, lf, NN, precision=lax.Precision.HIGHEST, preferred_element_type=F32) + carry_ref[...]
        f_ref[...] = cs
        carry_ref[...] = cs[tl - 1:tl, :]
        expand = (lax.broadcasted_iota(jnp.int32, (LANES, FOX_W), 0)
                  == (lax.broadcasted_iota(jnp.int32, (LANES, FOX_W), 1) >> 6)).astype(F32)
        fq_ref[...] = lax.dot_general(cs, expand, NN, precision=lax.Precision.HIGHEST, preferred_element_type=F32)

    return pl.pallas_call(
        body, name=name, grid=(L // tl,),
        in_specs=[pl.BlockSpec((tl, LANES), lambda i: (i, FL_BLK)), pl.BlockSpec((1, LANES), lambda i: (0, 0))],
        out_specs=[pl.BlockSpec((tl, LANES), lambda i: (i, 0)), pl.BlockSpec((tl, FOX_W), lambda i: (i, 0))],
        out_shape=[_sds((L, LANES)), _sds((L, FOX_W))],
        scratch_shapes=[pltpu.VMEM((1, LANES), F32)],
        compiler_params=_cp("arbitrary"),
    )(z, bf)


def _fox_f_bwd(dFk, dfq, z, bf, name):
    L = z.shape[0]
    tl = min(256, L)
    nb = L // tl

    def body(dfk_ref, dfq_ref, fl_ref, b_ref, dfl_ref, db_ref, carry_ref):
        i = pl.program_id(0)

        @pl.when(i == 0)
        def _():
            carry_ref[...] = jnp.zeros_like(carry_ref)
            db_ref[...] = jnp.zeros_like(db_ref)

        sel = (lax.broadcasted_iota(jnp.int32, (FOX_W, LANES), 0)
               == 64 * lax.broadcasted_iota(jnp.int32, (FOX_W, LANES), 1)).astype(F32)
        dfq_h = lax.dot_general(dfq_ref[...], sel, NN, precision=lax.Precision.HIGHEST, preferred_element_type=F32)
        r = lax.broadcasted_iota(jnp.int32, (tl, tl), 0)
        c = lax.broadcasted_iota(jnp.int32, (tl, tl), 1)
        tri = (r <= c).astype(F32)
        cs = lax.dot_general(tri, dfk_ref[...] + dfq_h, NN, precision=lax.Precision.HIGHEST,
                             preferred_element_type=F32) + carry_ref[...]
        carry_ref[...] = cs[0:1, :]
        dfl = cs * _sigmoid(-(fl_ref[...] + b_ref[...]))
        dfl_ref[...] = dfl
        db_ref[...] += jnp.sum(dfl, axis=0, keepdims=True)

    return pl.pallas_call(
        body, name=name, grid=(nb,),
        in_specs=[pl.BlockSpec((tl, LANES), lambda i: (nb - 1 - i, 0)),
                  pl.BlockSpec((tl, FOX_W), lambda i: (nb - 1 - i, 0)),
                  pl.BlockSpec((tl, LANES), lambda i: (nb - 1 - i, FL_BLK)),
                  pl.BlockSpec((1, LANES), lambda i: (0, 0))],
        out_specs=[pl.BlockSpec((tl, LANES), lambda i: (nb - 1 - i, 0)), pl.BlockSpec((1, LANES), lambda i: (0, 0))],
        out_shape=[_sds((L, LANES)), _sds((1, LANES))],
        scratch_shapes=[pltpu.VMEM((1, LANES), F32)],
        compiler_params=_cp("arbitrary"),
    )(dFk, dfq, z, bf)


def _head_mask(hh):
    lane = lax.broadcasted_iota(jnp.int32, (1, LANES), 1)
    return (lane >> 6) == hh


FOX_T = 512


def _fox_head(x, hh):
    return jnp.where(_head_mask(hh), x, 0.0).astype(BF16)


def _fox_scores(qh, k, fq_ref, fr_ref, hh, causal):
    s = _dot(qh, k, NT) + (fq_ref[:, 64 * hh:64 * hh + 1] - fr_ref[hh:hh + 1, :])
    return s if causal is None else jnp.where(causal, s, NEG)


def _causal(T):
    return lax.broadcasted_iota(jnp.int32, (T, T), 1) <= lax.broadcasted_iota(jnp.int32, (T, T), 0)


def _fox_fwd(z, fq, frow, name):
    L = z.shape[0]
    T = min(FOX_T, L)
    nq = L // T

    def body(qt_ref, kt_ref, q_ref, k_ref, v_ref, fq_ref, fr_ref, o_ref, lse_ref, m_ref, l_ref, acc_ref):
        t = pl.program_id(1)
        qi, ki = qt_ref[t], kt_ref[t]

        @pl.when(ki == 0)
        def _():
            m_ref[...] = jnp.full_like(m_ref, NEG)
            l_ref[...] = jnp.zeros_like(l_ref)
            acc_ref[...] = jnp.zeros_like(acc_ref)

        def step(diagonal):
            q = q_ref[...] * 0.125
            k = k_ref[...].astype(BF16)
            v = v_ref[...].astype(BF16)
            causal = _causal(T) if diagonal else None
            s = jnp.concatenate([_fox_scores(_fox_head(q, hh), k, fq_ref, fr_ref, hh, causal) for hh in range(2)],
                                axis=0)
            m_old = m_ref[...]
            m_new = jnp.maximum(m_old, jnp.max(s, axis=1, keepdims=True))
            alpha = jnp.exp(m_old - m_new)
            p = jnp.exp(s - m_new)
            l_ref[...] = alpha * l_ref[...] + jnp.sum(p, axis=1, keepdims=True)
            m_ref[...] = m_new
            acc_ref[...] = alpha * acc_ref[...] + _dot(p.astype(BF16), v)

        @pl.when(ki < qi)
        def _():
            step(False)

        @pl.when(ki == qi)
        def _():
            step(True)
            h0 = _head_mask(0)
            l = l_ref[...]
            o_h = acc_ref[...] / l
            lse_h = m_ref[...] + jnp.log(l)
            o_ref[...] = jnp.where(h0, o_h[:T], o_h[T:])
            lse_ref[...] = jnp.where(h0, lse_h[:T], lse_h[T:])

    pairs = [(qi, ki) for qi in range(nq) for ki in range(qi + 1)]
    qt = jnp.asarray([p[0] for p in pairs], jnp.int32)
    kt = jnp.asarray([p[1] for p in pairs], jnp.int32)

    def qspec(base):
        return pl.BlockSpec((T, LANES), lambda j, t, qt, kt: (qt[t], base + j))

    def kspec(base):
        return pl.BlockSpec((T, LANES), lambda j, t, qt, kt: (kt[t], base + j))

    return pl.pallas_call(
        body, name=name,
        grid_spec=pltpu.PrefetchScalarGridSpec(
            num_scalar_prefetch=2, grid=(4, len(pairs)),
            in_specs=[qspec(Q_BLK), kspec(K_BLK), kspec(V_BLK), qspec(0),
                      pl.BlockSpec((None, 2, T), lambda j, t, qt, kt: (j, 0, kt[t]))],
            out_specs=[qspec(0), qspec(0)],
            scratch_shapes=[pltpu.VMEM((2 * T, 1), F32), pltpu.VMEM((2 * T, 1), F32),
                            pltpu.VMEM((2 * T, LANES), F32)]),
        out_shape=[_sds((L, FOX_W)), _sds((L, FOX_W))],
        compiler_params=_cp("parallel", "arbitrary"),
    )(qt, kt, z, z, z, fq, frow)


def _fox_bwd(z, fq, frow, o, lse, g_m, name):
    L = z.shape[0]
    T = min(FOX_T, L)
    nq = L // T

    pairs = [(qi, ki) for ki in range(nq) for qi in range(ki, nq)]
    qt = jnp.asarray([p[0] for p in pairs], jnp.int32)
    kt = jnp.asarray([p[1] for p in pairs], jnp.int32)

    def body(qt_ref, kt_ref, q_ref, k_ref, v_ref, fq_ref, fr_ref, o_ref, lse_ref, do_ref,
             dq_ref, dk_ref, dv_ref, dfq_ref, dfk_ref, dk_acc, dv_acc, df_acc):
        t = pl.program_id(1)
        qi, ki = qt_ref[t], kt_ref[t]

        @pl.when(t == 0)
        def _():
            dq_ref[...] = jnp.zeros_like(dq_ref)
            dfq_ref[...] = jnp.zeros_like(dfq_ref)

        @pl.when(qi == ki)
        def _():
            dk_acc[...] = jnp.zeros_like(dk_acc)
            dv_acc[...] = jnp.zeros_like(dv_acc)
            df_acc[...] = jnp.zeros_like(df_acc)

        def step(diagonal):
            q = q_ref[...] * 0.125
            qb = q.astype(BF16)
            k = k_ref[...].astype(BF16)
            v = v_ref[...].astype(BF16)
            do = do_ref[...]
            dob = do.astype(BF16)
            do_o = dob.astype(F32) * o_ref[...]
            causal = _causal(T) if diagonal else None
            dvs, dks, dqs, rss = [], [], [], []
            for hh in range(2):
                s = _fox_scores(_fox_head(q, hh), k, fq_ref, fr_ref, hh, causal)
                p = jnp.exp(s - lse_ref[:, 64 * hh:64 * hh + 1])
                dp = _dot(_fox_head(do, hh), v, NT)
                delta = jnp.sum(jnp.where(_head_mask(hh), do_o, 0.0), axis=1, keepdims=True)
                ds = p * (dp - delta)
                dsb = ds.astype(BF16)
                dvs.append(_dot(p.astype(BF16), dob, TN))
                dks.append(_dot(dsb, qb, TN))
                dqs.append(_dot(dsb, k))
                rss.append(jnp.sum(ds, axis=1, keepdims=True))
                df_acc[hh:hh + 1, :] -= jnp.sum(ds, axis=0, keepdims=True)
            h0 = _head_mask(0)
            dv_acc[...] += jnp.where(h0, dvs[0], dvs[1])
            dk_acc[...] += jnp.where(h0, dks[0], dks[1])
            rows = pl.ds(pl.multiple_of(qi * T, T), T)
            dq_ref[rows, :] += jnp.where(h0, dqs[0], dqs[1])
            dfq_ref[rows, :] += jnp.where(h0, rss[0], rss[1])

        @pl.when(qi > ki)
        def _():
            step(False)

        @pl.when(qi == ki)
        def _():
            step(True)

        @pl.when(qi == nq - 1)
        def _():
            dk_ref[...] = dk_acc[...]
            dv_ref[...] = dv_acc[...]
            dfk_ref[...] = df_acc[...]

        @pl.when(t == len(pairs) - 1)
        def _():
            dq_ref[...] = dq_ref[...] * 0.125

    def qside(base):
        return pl.BlockSpec((T, LANES), lambda j, t, qt, kt: (qt[t], base + j))

    def kside(base):
        return pl.BlockSpec((T, LANES), lambda j, t, qt, kt: (kt[t], base + j))

    pair = pl.BlockSpec((L, LANES), lambda j, t, qt, kt: (0, j))
    frow_spec = pl.BlockSpec((None, 2, T), lambda j, t, qt, kt: (j, 0, kt[t]))
    return pl.pallas_call(
        body, name=name,
        grid_spec=pltpu.PrefetchScalarGridSpec(
            num_scalar_prefetch=2, grid=(4, len(pairs)),
            in_specs=[qside(Q_BLK), kside(K_BLK), kside(V_BLK), qside(0), frow_spec, qside(0), qside(0), qside(4)],
            out_specs=[pair, kside(0), kside(0), pair, frow_spec],
            scratch_shapes=[pltpu.VMEM((T, LANES), F32), pltpu.VMEM((T, LANES), F32), pltpu.VMEM((2, T), F32)]),
        out_shape=[_sds((L, FOX_W)), _sds((L, FOX_W)), _sds((L, FOX_W)), _sds((L, FOX_W)), _sds((4, 2, L))],
        compiler_params=_cp("parallel", "arbitrary"),
    )(qt, kt, z, z, z, fq, frow, o, lse, g_m)


def _shift_rows(v, s, down, row):
    n = v.shape[0]
    if down:
        return jnp.where(row >= s, pltpu.roll(v, s, 0), 0.0)
    return jnp.where(row < n - s, pltpu.roll(v, n - s, 0), 0.0)


def _window_sum(v, g, down, row):
    out = jnp.zeros_like(v)
    s = v
    for k in range(4):
        s = s + _shift_rows(s, 1 << k, down, row)
        out = jnp.where(g == k, s, out)
    return out


def _pool_inv_cnt(g, row):
    w = jnp.left_shift(2, g).astype(F32)
    return 1.0 / jnp.minimum(row.astype(F32) + 1.0, w)


def _pool_fwd(z, pool_w, scale, name):
    L = z.shape[0]

    def body(x_ref, w_ref, s_ref, y_ref, p_ref):
        g = pl.program_id(0)
        row = lax.broadcasted_iota(jnp.int32, (L, LANES), 0)
        x = x_ref[...]
        pooled = (_window_sum(x, g, True, row) * _pool_inv_cnt(g, row) - x).astype(BF16)
        p_ref[...] = pooled
        y_ref[...] = (_dot(pooled, w_ref[...].astype(BF16)) * s_ref[...]).astype(BF16)

    col = pl.BlockSpec((L, LANES), lambda g: (0, g))
    return pl.pallas_call(
        body, name=name, grid=(4,),
        in_specs=[col, pl.BlockSpec((None, LANES, LANES), lambda g: (g, 0, 0)), pl.BlockSpec((1, LANES), lambda g: (0, g))],
        out_specs=[col, col],
        out_shape=[_sds((L, 512), BF16), _sds((L, 512), BF16)],
        compiler_params=_cp("parallel"),
    )(z, pool_w, scale)


def _pool_bwd(g_m, pooled, pool_w, scale, name):
    L = g_m.shape[0]

    def body(g_ref, p_ref, w_ref, s_ref, gx_ref, gw_ref, gs_ref):
        g = pl.program_id(0)
        row = lax.broadcasted_iota(jnp.int32, (L, LANES), 0)
        gy = g_ref[...]
        pooled = p_ref[...]
        wb = w_ref[...].astype(BF16)
        lin = _dot(pooled, wb)
        gs_ref[...] = jnp.sum(gy * lin, axis=0, keepdims=True)
        glin = (gy * s_ref[...]).astype(BF16)
        gw_ref[...] = _dot(pooled, glin, TN)
        gp = _dot(glin, wb, NT)
        gx_ref[...] = _window_sum(gp * _pool_inv_cnt(g, row), g, False, row) - gp

    col = pl.BlockSpec((L, LANES), lambda g: (0, g))
    wspec = pl.BlockSpec((None, LANES, LANES), lambda g: (g, 0, 0))
    vec = pl.BlockSpec((1, LANES), lambda g: (0, g))
    return pl.pallas_call(
        body, name=name, grid=(4,),
        in_specs=[col, col, wspec, vec],
        out_specs=[col, wspec, vec],
        out_shape=[_sds((L, 512)), _sds((4, LANES, LANES)), _sds((1, 512))],
        compiler_params=_cp("parallel"),
    )(g_m, pooled, pool_w, scale)


SGU_CHUNKS = 4


def _sgu_ln(v, gam, bet):
    gv = _gelu(v)
    mu = jnp.mean(gv, axis=-1, keepdims=True)
    xc = gv - mu
    rs = lax.rsqrt(jnp.mean(xc * xc, axis=-1, keepdims=True) + EPS)
    xh = xc * rs
    return xh, rs, xh * gam + bet


def _tril_ws(w_ref, g):
    r = lax.broadcasted_iota(jnp.int32, (LANES, LANES), 0)
    c = lax.broadcasted_iota(jnp.int32, (LANES, LANES), 1)
    return jnp.where(r >= c, w_ref[g], 0.0).astype(BF16)


def _sgu_fwd(z, ln_g, ln_b, w_s, b_st, name):
    L = z.shape[0]
    rb = min(SGU_CHUNKS * LANES, L)

    def body(u_ref, v_ref, g_ref, b_ref, w_ref, bs_ref, y_ref):
        _, _, vln = _sgu_ln(v_ref[...], g_ref[...], b_ref[...])
        gu = _gelu(u_ref[...])
        vb = vln.astype(BF16)
        for g in range(4):
            ws = _tril_ws(w_ref, g)
            for n in range(rb // LANES):
                rows = slice(n * LANES, (n + 1) * LANES)
                cols = slice(g * LANES, (g + 1) * LANES)
                mixed = _dot(ws, vb[rows, cols]) + bs_ref[:, g:g + 1]
                y_ref[rows, cols] = (gu[rows, cols] * mixed).astype(BF16)

    vm = lambda shape: pl.BlockSpec(shape, lambda i: tuple(0 for _ in shape))
    return pl.pallas_call(
        body, name=name, grid=(L // rb,),
        in_specs=[pl.BlockSpec((rb, 512), lambda i: (i, 1)), pl.BlockSpec((rb, 512), lambda i: (i, 2)),
                  vm((1, 512)), vm((1, 512)), vm((4, LANES, LANES)), vm((LANES, 4))],
        out_specs=pl.BlockSpec((rb, 512), lambda i: (i, 0)),
        out_shape=_sds((L, 512), BF16),
        compiler_params=_cp("parallel"),
    )(z, z, ln_g, ln_b, w_s, b_st)


def _sgu_bwd(g_m, z, ln_g, ln_b, w_s, b_st, name):
    L = z.shape[0]
    rb = min(SGU_CHUNKS * LANES, L)

    def body(gy_ref, u_ref, v_ref, g_ref, b_ref, w_ref, bs_ref, gu_ref, gv_ref, gw_ref, gbs_ref, gg_ref, gb_ref):
        i = pl.program_id(0)

        @pl.when(i == 0)
        def _():
            gw_ref[...] = jnp.zeros_like(gw_ref)
            gbs_ref[...] = jnp.zeros_like(gbs_ref)
            gg_ref[...] = jnp.zeros_like(gg_ref)
            gb_ref[...] = jnp.zeros_like(gb_ref)

        v = v_ref[...]
        u = u_ref[...]
        gy = gy_ref[...]
        xh, rs, vln = _sgu_ln(v, g_ref[...], b_ref[...])
        gel_u = _gelu(u)
        gmix = gy * gel_u
        vb = vln.astype(BF16)
        gmb = gmix.astype(BF16)
        r = lax.broadcasted_iota(jnp.int32, (LANES, LANES), 0)
        c = lax.broadcasted_iota(jnp.int32, (LANES, LANES), 1)
        gvln_cols = []
        for g in range(4):
            ws = _tril_ws(w_ref, g)
            cols = slice(g * LANES, (g + 1) * LANES)
            gw = jnp.zeros((LANES, LANES), F32)
            gbs = jnp.zeros((LANES, 1), F32)
            parts = []
            for n in range(rb // LANES):
                rows = slice(n * LANES, (n + 1) * LANES)
                mixed = _dot(ws, vb[rows, cols]) + bs_ref[:, g:g + 1]
                gu_ref[rows, cols] = gy[rows, cols] * mixed * _gelu_grad(u[rows, cols])
                parts.append(_dot(ws, gmb[rows, cols], TN))
                gw = gw + _dot(gmb[rows, cols], vb[rows, cols], NT)
                gbs = gbs + jnp.sum(gmix[rows, cols], axis=1, keepdims=True)
            gvln_cols.append(jnp.concatenate(parts, axis=0))
            gw_ref[g] += jnp.where(r >= c, gw, 0.0)
            gbs_ref[:, g:g + 1] += gbs
        gvln = jnp.concatenate(gvln_cols, axis=1)
        gg_ref[...] += jnp.sum(gvln * xh, axis=0, keepdims=True)
        gb_ref[...] += jnp.sum(gvln, axis=0, keepdims=True)
        gxh = gvln * g_ref[...]
        ggv = rs * (gxh - jnp.mean(gxh, axis=-1, keepdims=True) - xh * jnp.mean(gxh * xh, axis=-1, keepdims=True))
        gv_ref[...] = ggv * _gelu_grad(v)

    vm = lambda shape: pl.BlockSpec(shape, lambda i: tuple(0 for _ in shape))
    blk = pl.BlockSpec((rb, 512), lambda i: (i, 0))
    return pl.pallas_call(
        body, name=name, grid=(L // rb,),
        in_specs=[pl.BlockSpec((rb, 512), lambda i: (i, 1)), pl.BlockSpec((rb, 512), lambda i: (i, 1)),
                  pl.BlockSpec((rb, 512), lambda i: (i, 2)),
                  vm((1, 512)), vm((1, 512)), vm((4, LANES, LANES)), vm((LANES, 4))],
        out_specs=[blk, blk, vm((4, LANES, LANES)), vm((LANES, 4)), vm((1, 512)), vm((1, 512))],
        out_shape=[_sds((L, 512)), _sds((L, 512)), _sds((4, LANES, LANES)), _sds((LANES, 4)),
                   _sds((1, 512)), _sds((1, 512))],
        compiler_params=_cp("arbitrary"),
    )(g_m, z, z, ln_g, ln_b, w_s, b_st)


def _adamw_math(w, g, m, v):
    nm = ADAM_B1 * m + (1.0 - ADAM_B1) * g
    nv = ADAM_B2 * v + (1.0 - ADAM_B2) * (g * g)
    m_hat = nm / (1.0 - ADAM_B1 ** ADAM_STEP)
    v_hat = nv / (1.0 - ADAM_B2 ** ADAM_STEP)
    delta = -ADAM_LR * (m_hat / (jnp.sqrt(v_hat) + ADAM_EPS) + ADAM_WD * w)
    return delta, nm, nv


def _sum_adamw(parts, w, m, v, name, layer=0, prev=None):
    n_layers, R, C = w.shape
    rb = 128 if R % 128 == 0 else R

    def body(p_ref, w_ref, m_ref, v_ref, *rest):
        g_ref, d_ref, nm_ref, nv_ref = rest[-4:]
        g = p_ref[0].astype(F32)
        for s in range(1, N_DEV):
            g = g + p_ref[s].astype(F32)
        d, nm, nv = _adamw_math(w_ref[...], g, m_ref[...], v_ref[...])
        g_ref[...] = g
        d_ref[...] = d
        nm_ref[...] = nm
        nv_ref[...] = nv

    blk = pl.BlockSpec((None, rb, C), lambda i: (layer, i, 0))
    prev = [] if prev is None else list(prev)
    return pl.pallas_call(
        body, name=name, grid=(R // rb,),
        in_specs=[pl.BlockSpec((N_DEV, rb, C), lambda i: (0, i, 0)), blk, blk, blk] + [ANY] * len(prev),
        out_specs=[blk] * 4, out_shape=[_sds((n_layers, R, C))] * 4,
        input_output_aliases={4 + k: k for k in range(len(prev))},
        compiler_params=_cp("parallel"),
    )(parts, w, m, v, *prev)


def _sum_pieces(parts, name):
    _, R, C = parts.shape

    def body(p_ref, g_ref):
        g = p_ref[0].astype(F32)
        for s in range(1, N_DEV):
            g = g + p_ref[s].astype(F32)
        g_ref[...] = g

    vm = pl.BlockSpec(memory_space=pltpu.VMEM)
    return pl.pallas_call(body, name=name, in_specs=[vm], out_specs=vm, out_shape=_sds((R, C)),
                          compiler_params=pltpu.CompilerParams(vmem_limit_bytes=VMEM_LIMIT))(parts)


def _adamw_many(ws, gs, ms, vs, name):
    n = len(ws)
    vm = pl.BlockSpec(memory_space=pltpu.VMEM)

    def body(*refs):
        w_refs, g_refs, m_refs, v_refs = refs[:n], refs[n:2 * n], refs[2 * n:3 * n], refs[3 * n:4 * n]
        d_refs, nm_refs, nv_refs = refs[4 * n:5 * n], refs[5 * n:6 * n], refs[6 * n:7 * n]
        for i in range(n):
            d, nm, nv = _adamw_math(w_refs[i][...], g_refs[i][...], m_refs[i][...], v_refs[i][...])
            d_refs[i][...] = d
            nm_refs[i][...] = nm
            nv_refs[i][...] = nv

    shapes = [_sds(w.shape) for w in ws]
    outs = pl.pallas_call(
        body, name=name, in_specs=[vm] * (4 * n), out_specs=[vm] * (3 * n), out_shape=shapes * 3,
        compiler_params=pltpu.CompilerParams(vmem_limit_bytes=VMEM_LIMIT),
    )(*ws, *gs, *ms, *vs)
    return list(outs[:n]), list(outs[n:2 * n]), list(outs[2 * n:])


def _mesh_pos():
    return lax.axis_index("x"), lax.axis_index("y"), lax.axis_index("c")


def _dev_index(p):
    return 4 * p[0] + 2 * p[1] + p[2]


HBM = pl.BlockSpec(memory_space=pltpu.HBM)
SEM = pl.BlockSpec(memory_space=pltpu.SEMAPHORE)
EFFECT = pltpu.SideEffectType.DATAFLOW_SIDE_EFFECTING


def _peer_list():
    x, y, c = _mesh_pos()
    peers = [(x ^ dx, y ^ dy, c ^ dc) for dx in range(2) for dy in range(2) for dc in range(2)][1:]
    return (x, y, c), peers


def _split_copy(src_ref, land_ref, send_sems, recv_sems, i, k, peer, slot, exchange):
    return pltpu.make_async_remote_copy(
        src_ref=src_ref.at[_dev_index(peer)] if exchange else src_ref, dst_ref=land_ref.at[slot],
        send_sem=send_sems.at[7 * i + k], recv_sem=recv_sems.at[7 * i + k], device_id=peer, device_id_type=MESH)


def _comm_start(groups, name, exchange, dep=None):
    sizes = [len(g) for g in groups]
    n = sum(sizes)
    srcs = [a for g in groups for a in g]
    my_index = _dev_index(_mesh_pos())
    lands = []
    for a in srcs:
        if exchange:
            own = lax.dynamic_slice(a, (my_index, 0, 0), (1,) + a.shape[1:])
            shape = a.shape
        else:
            own = a[None]
            shape = (N_DEV,) + a.shape
        lands.append(lax.dynamic_update_slice(lax.empty(shape, a.dtype), own, (my_index, 0, 0)))

    n_dep = 0 if dep is None else 1

    def body(*refs):
        src_refs, land_refs = refs[:n], refs[n:2 * n]
        sem_refs = refs[2 * n + n_dep:2 * n + n_dep + 2 * len(sizes)]
        token_ref = refs[-1]
        me, peers = _peer_list()
        mi = _dev_index(me)
        i = 0
        for gi, sz in enumerate(sizes):
            for j in range(sz):
                for k, peer in enumerate(peers):
                    _split_copy(src_refs[i], land_refs[i], sem_refs[2 * gi], sem_refs[2 * gi + 1], j, k, peer, mi,
                                exchange).start()
                i += 1
        token_ref[...] = jnp.zeros_like(token_ref)

    sem_shapes = []
    for sz in sizes:
        sem_shapes += [pltpu.SemaphoreType.DMA((7 * sz,)), pltpu.SemaphoreType.DMA((7 * sz,))]
    thru = [pltpu.HBM(a.shape, a.dtype) for a in srcs + lands]
    n_sem = len(sem_shapes)
    outs = pl.pallas_call(
        body, name=name,
        out_shape=tuple(sem_shapes + thru + [_sds((8, LANES))]),
        in_specs=[HBM] * (2 * n) + [ANY] * n_dep,
        out_specs=tuple([SEM] * n_sem + [HBM] * (2 * n) + [pl.BlockSpec(memory_space=pltpu.VMEM)]),
        input_output_aliases={i: n_sem + i for i in range(2 * n)},
        compiler_params=pltpu.CompilerParams(has_side_effects=EFFECT),
    )(*[pltpu.with_memory_space_constraint(a, pltpu.HBM) for a in srcs + lands], *([] if dep is None else [dep]))
    sems, thru_src, thru_land, token = outs[:n_sem], outs[n_sem:n_sem + n], outs[n_sem + n:n_sem + 2 * n], outs[-1]
    result, off = [], 0
    for gi, sz in enumerate(sizes):
        result.append((sems[2 * gi], sems[2 * gi + 1], list(thru_src[off:off + sz]), list(thru_land[off:off + sz])))
        off += sz
    return result, token


def _comm_wait(group, after, name, exchange):
    send_sems, recv_sems, srcs, lands = group
    n = len(srcs)
    after = list(after) if isinstance(after, (list, tuple)) else [after]

    def body(*refs):
        src_refs, land_refs = refs[:n], refs[n:2 * n]
        ssem, rsem = refs[2 * n], refs[2 * n + 1]
        me, peers = _peer_list()
        for i in range(n):
            for k, peer in enumerate(peers):
                cp = _split_copy(src_refs[i], land_refs[i], ssem, rsem, i, k, peer, _dev_index(peer), exchange)
                cp.wait_send()
                cp.wait_recv()

    outs = pl.pallas_call(
        body, name=name,
        out_shape=tuple(pltpu.HBM(a.shape, a.dtype) for a in srcs + lands),
        in_specs=[HBM] * (2 * n) + [SEM, SEM] + [ANY] * len(after),
        out_specs=tuple([HBM] * (2 * n)),
        input_output_aliases={i: i for i in range(2 * n)},
        compiler_params=pltpu.CompilerParams(has_side_effects=EFFECT),
    )(*srcs, *lands, send_sems, recv_sems, *after)
    return list(outs[n:])


def _tie(a, token):
    return a + token[0, 0].astype(a.dtype)


def _pack(arrs, rows):
    flat = jnp.concatenate([a.reshape(-1).astype(F32) for a in arrs])
    return jnp.pad(flat, (0, rows * LANES - flat.shape[0])).reshape(rows, LANES)


def _unpack(packed, shapes):
    flat = packed.reshape(-1)
    out, off = [], 0
    for s in shapes:
        n = math.prod(s)
        out.append(flat[off:off + n].reshape(s))
        off += n
    return out


def _packed_rows(shapes):
    n = sum(math.prod(s) for s in shapes)
    unit = N_DEV * 8 * LANES
    return -(-n // unit) * unit // LANES


def kernel(x, mix_pre_g, mix_post_g, mlp_pre_g, mlp_post_g, w_in_even, s5_lam_re, s5_lam_im, s5_log_dt, s5_b_re, s5_b_im, s5_c_re, s5_c_im, s5_d, s5_w_glu, fox_b_f, w_out_even, w_in_odd, pool_w, pool_scale, sgu_ln_g, sgu_ln_b, sgu_w_s, sgu_b_s, w_out_odd, mlp_w1, mlp_w2, loss_target, m_mix_pre_g, m_mix_post_g, m_mlp_pre_g, m_mlp_post_g, m_w_in_even, m_s5_lam_re, m_s5_lam_im, m_s5_log_dt, m_s5_b_re, m_s5_b_im, m_s5_c_re, m_s5_c_im, m_s5_d, m_s5_w_glu, m_fox_b_f, m_w_out_even, m_w_in_odd, m_pool_w, m_pool_scale, m_sgu_ln_g, m_sgu_ln_b, m_sgu_w_s, m_sgu_b_s, m_w_out_odd, m_mlp_w1, m_mlp_w2, v_mix_pre_g, v_mix_post_g, v_mlp_pre_g, v_mlp_post_g, v_w_in_even, v_s5_lam_re, v_s5_lam_im, v_s5_log_dt, v_s5_b_re, v_s5_b_im, v_s5_c_re, v_s5_c_im, v_s5_d, v_s5_w_glu, v_fox_b_f, v_w_out_even, v_w_in_odd, v_pool_w, v_pool_scale, v_sgu_ln_g, v_sgu_ln_b, v_sgu_w_s, v_sgu_b_s, v_w_out_odd, v_mlp_w1, v_mlp_w2):
    weights = dict(mix_pre_g=mix_pre_g, mix_post_g=mix_post_g, mlp_pre_g=mlp_pre_g, mlp_post_g=mlp_post_g, w_in_even=w_in_even, s5_lam_re=s5_lam_re, s5_lam_im=s5_lam_im, s5_log_dt=s5_log_dt, s5_b_re=s5_b_re, s5_b_im=s5_b_im, s5_c_re=s5_c_re, s5_c_im=s5_c_im, s5_d=s5_d, s5_w_glu=s5_w_glu, fox_b_f=fox_b_f, w_out_even=w_out_even, w_in_odd=w_in_odd, pool_w=pool_w, pool_scale=pool_scale, sgu_ln_g=sgu_ln_g, sgu_ln_b=sgu_ln_b, sgu_w_s=sgu_w_s, sgu_b_s=sgu_b_s, w_out_odd=w_out_odd, mlp_w1=mlp_w1, mlp_w2=mlp_w2)
    mom_m = dict(mix_pre_g=m_mix_pre_g, mix_post_g=m_mix_post_g, mlp_pre_g=m_mlp_pre_g, mlp_post_g=m_mlp_post_g, w_in_even=m_w_in_even, s5_lam_re=m_s5_lam_re, s5_lam_im=m_s5_lam_im, s5_log_dt=m_s5_log_dt, s5_b_re=m_s5_b_re, s5_b_im=m_s5_b_im, s5_c_re=m_s5_c_re, s5_c_im=m_s5_c_im, s5_d=m_s5_d, s5_w_glu=m_s5_w_glu, fox_b_f=m_fox_b_f, w_out_even=m_w_out_even, w_in_odd=m_w_in_odd, pool_w=m_pool_w, pool_scale=m_pool_scale, sgu_ln_g=m_sgu_ln_g, sgu_ln_b=m_sgu_ln_b, sgu_w_s=m_sgu_w_s, sgu_b_s=m_sgu_b_s, w_out_odd=m_w_out_odd, mlp_w1=m_mlp_w1, mlp_w2=m_mlp_w2)
    mom_v = dict(mix_pre_g=v_mix_pre_g, mix_post_g=v_mix_post_g, mlp_pre_g=v_mlp_pre_g, mlp_post_g=v_mlp_post_g, w_in_even=v_w_in_even, s5_lam_re=v_s5_lam_re, s5_lam_im=v_s5_lam_im, s5_log_dt=v_s5_log_dt, s5_b_re=v_s5_b_re, s5_b_im=v_s5_b_im, s5_c_re=v_s5_c_re, s5_c_im=v_s5_c_im, s5_d=v_s5_d, s5_w_glu=v_s5_w_glu, fox_b_f=v_fox_b_f, w_out_even=v_w_out_even, w_in_odd=v_w_in_odd, pool_w=v_pool_w, pool_scale=v_pool_scale, sgu_ln_g=v_sgu_ln_g, sgu_ln_b=v_sgu_ln_b, sgu_w_s=v_sgu_w_s, sgu_b_s=v_sgu_b_s, w_out_odd=v_w_out_odd, mlp_w1=v_mlp_w1, mlp_w2=v_mlp_w2)
    names = list(weights)
    L = x.shape[1]
    x0 = x[0]
    target = loss_target[0]
    my_index = 4 * lax.axis_index("x") + 2 * lax.axis_index("y") + lax.axis_index("c")

    small_vec = jnp.zeros((8, LANES), F32)
    small_vec = small_vec.at[0, :64].set(pool_scale[0]).at[1, :64].set(sgu_ln_g[0]).at[2, :64].set(sgu_ln_b[0])
    ag_groups, ag_token = _comm_start(
        [[jnp.transpose(w_in_even[0]).astype(BF16), small_vec],
         [s5_w_glu[0].astype(BF16), w_out_even[0].astype(BF16)],
         [mlp_w1[0].astype(BF16), mlp_w2[0].astype(BF16)],
         [jnp.transpose(w_in_odd[0]).astype(BF16), w_out_odd[0].astype(BF16), mlp_w1[1].astype(BF16), mlp_w2[1].astype(BF16)]],
        "ag_start", exchange=False)

    lam_r = jnp.concatenate([s5_lam_re.reshape(1, S5_NS), s5_lam_im.reshape(1, S5_NS)], axis=0)
    ldt_r = jnp.repeat(s5_log_dt.reshape(32), 64).reshape(1, S5_NS)
    lam_c = jnp.transpose(lam_r)
    ldt_c = jnp.transpose(ldt_r)
    b_t = jnp.stack([jnp.tile(s5_b_re.reshape(S5_NS, 16), (1, 8)), jnp.tile(s5_b_im.reshape(S5_NS, 16), (1, 8))])
    c_t = jnp.stack([jnp.tile(s5_c_re.reshape(S5_W, 64), (1, 8)), jnp.tile(s5_c_im.reshape(S5_W, 64), (1, 8))])
    bf_pad = jnp.pad(fox_b_f, ((0, 0), (0, LANES - 8)))
    b_st = jnp.transpose(sgu_b_s[0])

    h0, rx0 = _rms_fwd(x0, _tie(mix_pre_g[0:1], ag_token), "rms0")
    tabs, bset, cset = _s5_prep(lam_r, ldt_r, lam_c, ldt_c, b_t, c_t, "s5_prep")
    ag0 = _comm_wait(ag_groups[0], tabs, "ag_wait0", exchange=False)
    winT_e = jnp.pad(ag0[0].reshape(EVEN_IN, D_MODEL), ((0, EVEN_PAD - EVEN_IN), (0, 0)))
    pool_scale_f = ag0[1][:, 0, :64].reshape(1, 512)
    ln_g_f = ag0[1][:, 1, :64].reshape(1, 512)
    ln_b_f = ag0[1][:, 2, :64].reshape(1, 512)
    z0 = _mm(h0, winT_e, name="win_even", tb=True, bm=512, bn=EVEN_PAD)
    xs = _s5_scan_fwd(z0, bset, tabs, "s5_scan")
    ag1 = _comm_wait(ag_groups[1], xs, "ag_wait1", exchange=False)
    wglu = ag1[0].reshape(S5_W, S5_W)
    wout_e = ag1[1].reshape(D_MODEL, D_MODEL)
    ylin, ya = _s5_out_fwd(xs, cset, z0, s5_d, wglu, "s5_out")
    fcum, fq = _fox_f_fwd(z0, bf_pad, "fox_f")
    frow = jnp.transpose(fcum[:, :8]).reshape(4, 2, L)
    o_att, lse = _fox_fwd(z0, fq, frow, "fox_fwd")
    mix0 = [ya, o_att]
    x1, ry0, h1, rx1, y0 = _mm(mix0, wout_e, name="wout_even", epi=_epi_post_pre, extra=(x0,),
                               vecs=(mix_post_g[0:1], mlp_pre_g[0:1]), out_dtypes=POST_PRE_DTYPES,
                               out_kinds=POST_PRE_KINDS, bm=FUSED_ROWS)
    ag2 = _comm_wait(ag_groups[2], rx1, "ag_wait2", exchange=False)
    w1 = [ag2[0], None]
    w2 = [ag2[1].reshape(4 * D_MODEL, D_MODEL), None]
    p0, a0 = _mm(h1, w1[0], name="mlp0_w1", b3=True, out_dtypes=(BF16, BF16), epi=_epi_relu2, bm=2048)
    x2, ro0, h2, rx2, o0 = _mm(a0, w2[0], name="mlp0_w2", epi=_epi_post_pre, extra=(x1,),
                               vecs=(mlp_post_g[0:1], mix_pre_g[1:2]), out_dtypes=POST_PRE_DTYPES,
                               out_kinds=POST_PRE_KINDS, bm=FUSED_ROWS)
    ag3 = _comm_wait(ag_groups[3], rx2, "ag_wait3", exchange=False)
    winT_o = ag3[0].reshape(ODD_IN, D_MODEL)
    wout_o = ag3[1].reshape(D_MODEL, D_MODEL)
    w1[1] = ag3[2]
    w2[1] = ag3[3].reshape(4 * D_MODEL, D_MODEL)
    z1 = _mm(h2, winT_o, name="win_odd", tb=True, bn=ODD_IN)
    yc, pooled = _pool_fwd(z1, pool_w[0], pool_scale_f, "pool_fwd")
    yd = _sgu_fwd(z1, ln_g_f, ln_b_f, sgu_w_s[0], b_st, "sgu_fwd")
    mix1 = [yc, yd]
    x3, ry1, h3, rx3, y1 = _mm(mix1, wout_o, name="wout_odd", epi=_epi_post_pre, extra=(x2,),
                               vecs=(mix_post_g[1:2], mlp_pre_g[1:2]), out_dtypes=POST_PRE_DTYPES,
                               out_kinds=POST_PRE_KINDS, bm=FUSED_ROWS)
    p1, a1 = _mm(h3, w1[1], name="mlp1_w1", b3=True, out_dtypes=(BF16, BF16), epi=_epi_relu2, bm=2048)
    o1 = _mm(a1, w2[1], name="mlp1_w2", bm=2048)
    gx4, ro1, sq = _post_loss_fwd(x3, o1, mlp_post_g[1:2], target, "post3")

    g_o1, gg_mlp_post1 = _post_bwd(gx4, o1, ro1, mlp_post_g[1:2], "bpost3")
    g_p1 = _mm(g_o1, w2[1], name="b_mlp1_a", tb=True, out_dtypes=(BF16,), epi=_epi_relu2_bwd, extra=(p1,), bm=2048)
    gw2_1 = _mm(a1, g_o1, name="b_mlp1_w2", ta=True, bm=2048)
    gw1_1 = _mm(h3, g_p1, name="b_mlp1_w1", ta=True, out3=True, bn=2048)
    (ex1,), tok1 = _comm_start([[gw1_1, gw2_1.reshape(N_DEV, 512, D_MODEL)]], "ex_start1", exchange=True)
    g_x3, gg_mlp_pre1, g_y1, gg_mix_post1 = _mm(
        g_p1, w1[1], name="b_mlp1_h", tb=True, b3=True, epi=_epi_pre_post_bwd, extra=(x3, gx4, y1), cols=(rx3, ry1),
        vecs=(_tie(mlp_pre_g[1:2], tok1), mix_post_g[1:2]), out_dtypes=PRE_POST_BWD_DTYPES,
        out_kinds=PRE_POST_BWD_KINDS, bm=FUSED_ROWS)
    g_mix1 = _mm(g_y1, wout_o, name="b_wout_odd_m", tb=True)
    gwout_o = _mm(mix1, g_y1, name="b_wout_odd_w", ta=True)
    g_xc, g_pool_w, g_pool_scale = _pool_bwd(g_mix1, pooled, pool_w[0], pool_scale_f, "pool_bwd")
    g_u1, g_v1, g_ws, g_bst, g_ln_g, g_ln_b = _sgu_bwd(g_mix1, z1, ln_g_f, ln_b_f, sgu_w_s[0], b_st, "sgu_bwd")
    g_z1 = [g_xc, g_u1, g_v1]
    gwinT_o = _mm(g_z1, h2, name="b_win_odd_w", ta=True)
    (ex2,), tok2 = _comm_start([[gwout_o.reshape(N_DEV, 128, D_MODEL), gwinT_o.reshape(N_DEV, ODD_IN // N_DEV, D_MODEL)]], "ex_start2", exchange=True)
    g_x2, gg_mix_pre1, g_o0, gg_mlp_post0 = _mm(
        g_z1, winT_o, name="b_win_odd_h", epi=_epi_pre_post_bwd, extra=(x2, g_x3, o0), cols=(rx2, ro0),
        vecs=(_tie(mix_pre_g[1:2], tok2), mlp_post_g[0:1]), out_dtypes=PRE_POST_BWD_DTYPES,
        out_kinds=PRE_POST_BWD_KINDS, bm=FUSED_ROWS)
    g_p0 = _mm(g_o0, w2[0], name="b_mlp0_a", tb=True, out_dtypes=(BF16,), epi=_epi_relu2_bwd, extra=(p0,), bm=2048)
    gw2_0 = _mm(a0, g_o0, name="b_mlp0_w2", ta=True, bm=2048)
    gw1_0 = _mm(h1, g_p0, name="b_mlp0_w1", ta=True, out3=True, bn=2048)
    (ex3,), tok3 = _comm_start([[gw1_0, gw2_0.reshape(N_DEV, 512, D_MODEL)]], "ex_start3", exchange=True)
    g_x1, gg_mlp_pre0, g_y0, gg_mix_post0 = _mm(
        g_p0, w1[0], name="b_mlp0_h", tb=True, b3=True, epi=_epi_pre_post_bwd, extra=(x1, g_x2, y0), cols=(rx1, ry0),
        vecs=(_tie(mlp_pre_g[0:1], tok3), mix_post_g[0:1]), out_dtypes=PRE_POST_BWD_DTYPES,
        out_kinds=PRE_POST_BWD_KINDS, bm=FUSED_ROWS)
    g_mix0 = _mm(g_y0, wout_e, name="b_wout_even_m", tb=True)
    gwout_e = _mm(mix0, g_y0, name="b_wout_even_w", ta=True)
    gyl, gud, g_wglu, g_d = _s5_glu_bwd(g_mix0, ylin, z0, s5_d, wglu, "s5_glu_bwd")
    (ex4,), tok4 = _comm_start([[gwout_e.reshape(N_DEV, 128, D_MODEL), g_wglu.reshape(N_DEV, 64, S5_W)]], "ex_start4", exchange=True)
    g_u0, ga, gb_raw, gc_raw = _s5_scan_bwd(gyl, _tie(cset, tok4), xs, z0, bset, gud, tabs, "s5_scan_bwd")
    g_lam, g_ldt, g_b, g_c = _s5_param_bwd(lam_c, ldt_c, b_t, gb_raw, jnp.transpose(ga), gc_raw, "s5_param_bwd")
    dq, dk, dv, dfq, dfrow = _fox_bwd(z0, fq, frow, o_att, lse, g_mix0, "fox_bwd")
    dFk = jnp.pad(jnp.transpose(dfrow.reshape(8, L)), ((0, 0), (0, LANES - 8)))
    dfl, db_f = _fox_f_bwd(dFk, dfq, z0, bf_pad, "fox_f_bwd")
    g_z0 = [g_u0, dq, dk, dv, dfl]
    grad_x, gg_mix_pre0 = _mm(g_z0, winT_e, name="b_win_even_h", epi=_epi_pre_bwd, extra=(x0, g_x1), cols=(rx0,),
                              vecs=(mix_pre_g[0:1],), out_dtypes=(F32, F32), out_kinds=("full", "vsum"),
                              bm=FUSED_ROWS)

    small_grads = dict(
        mix_pre_g=jnp.concatenate([gg_mix_pre0, gg_mix_pre1]), mix_post_g=jnp.concatenate([gg_mix_post0, gg_mix_post1]),
        mlp_pre_g=jnp.concatenate([gg_mlp_pre0, gg_mlp_pre1]), mlp_post_g=jnp.concatenate([gg_mlp_post0, gg_mlp_post1]),
        s5_lam_re=g_lam[:, 0], s5_lam_im=g_lam[:, 1], s5_log_dt=g_ldt,
        s5_b_re=g_b[0, :, :16], s5_b_im=g_b[1, :, :16], s5_c_re=g_c[0, :, :64], s5_c_im=g_c[1, :, :64],
        s5_d=g_d, fox_b_f=db_f[:, :8], pool_w=g_pool_w, sgu_w_s=g_ws, sgu_b_s=jnp.transpose(g_bst),
        pool_scale=g_pool_scale, sgu_ln_g=g_ln_g, sgu_ln_b=g_ln_b)
    small_names = list(small_grads)
    full_shapes = [(512,) if nm in ("pool_scale", "sgu_ln_g", "sgu_ln_b") else weights[nm].shape for nm in small_names]
    full_shapes.append((1, 1))
    rows = _packed_rows(full_shapes)
    packed = _pack([small_grads[nm] for nm in small_names] + [sq], rows).reshape(N_DEV, rows // N_DEV, LANES)
    (exs,), tok_s = _comm_start([[packed]], "exs_start", exchange=True)
    gwinT_e = _mm(g_z0, h0, name="b_win_even_w", ta=True, bk=512, out_dtypes=(BF16,), dep=tok_s)
    (recv_small,) = _comm_wait(exs, gwinT_e, "exs_wait", exchange=True)
    piece = _sum_pieces(recv_small, "sum_small")
    (ags,), tok_a = _comm_start([[piece]], "ags_start", exchange=False)

    gwinT_e_pieces = gwinT_e[:EVEN_IN].reshape(N_DEV, EVEN_IN // N_DEV, D_MODEL)
    (ex5,), tok5 = _comm_start([[gwinT_e_pieces]], "ex_start5", exchange=True, dep=tok_a)
    r_w1_1, r_w2_1 = _comm_wait(ex1, tok5, "ex_wait1", exchange=True)
    r_wout_o, r_win_o = _comm_wait(ex2, tok5, "ex_wait2", exchange=True)
    r_w1_0, r_w2_0 = _comm_wait(ex3, tok5, "ex_wait3", exchange=True)
    r_wout_e, r_wglu = _comm_wait(ex4, tok5, "ex_wait4", exchange=True)

    res = {}
    for nm, parts in (("mlp_w1", (r_w1_0, r_w1_1)), ("mlp_w2", (r_w2_0, r_w2_1))):
        first = _sum_adamw(parts[0], weights[nm], mom_m[nm], mom_v[nm], "adamw_%s_0" % nm, layer=0)
        res[nm] = tuple(_sum_adamw(parts[1], weights[nm], mom_m[nm], mom_v[nm], "adamw_%s_1" % nm, layer=1, prev=first))
    big_parts = dict(s5_w_glu=r_wglu, w_out_even=r_wout_e, w_out_odd=r_wout_o)
    for nm, parts in big_parts.items():
        res[nm] = tuple(_sum_adamw(parts, weights[nm], mom_m[nm], mom_v[nm], "adamw_" + nm))
    done = [res[nm][1] for nm in ("mlp_w1", "mlp_w2", "s5_w_glu", "w_out_even", "w_out_odd")]

    (small_all,) = _comm_wait(ags, done, "ags_wait", exchange=False)
    small_full = _unpack(small_all.reshape(rows, LANES), full_shapes)
    loss = 0.5 * small_full.pop()[0, 0] / D_MODEL
    small_g = []
    for nm, g in zip(small_names, small_full):
        if nm in ("pool_scale", "sgu_ln_g", "sgu_ln_b"):
            g = lax.dynamic_slice(g, (my_index * 64,), (64,)).reshape(1, 64)
        small_g.append(g)
    sd, sm, sv = _adamw_many([weights[nm] for nm in small_names], small_g, [mom_m[nm] for nm in small_names],
                             [mom_v[nm] for nm in small_names], "adamw_small")
    for nm, g_, d_, m_, v_ in zip(small_names, small_g, sd, sm, sv):
        res[nm] = (g_, d_, m_, v_)
    done.append(sd[0])

    for nm, parts in (("w_in_odd", r_win_o), ("w_in_even", None)):
        if parts is None:
            (parts,) = _comm_wait(ex5, done, "ex_wait5", exchange=True)
        outs = _sum_adamw(parts, jnp.transpose(weights[nm], (0, 2, 1)), jnp.transpose(mom_m[nm], (0, 2, 1)),
                          jnp.transpose(mom_v[nm], (0, 2, 1)), "adamw_" + nm)
        res[nm] = tuple(jnp.transpose(o, (0, 2, 1)) for o in outs)
        done.append(res[nm][1])

    grads = [res[nm][0].reshape(weights[nm].shape) for nm in names]
    deltas = [res[nm][1].reshape(weights[nm].shape) for nm in names]
    new_m = [res[nm][2].reshape(weights[nm].shape) for nm in names]
    new_v = [res[nm][3].reshape(weights[nm].shape) for nm in names]
    return (loss, grad_x[None], *grads, *deltas, *new_m, *new_v)
```

```python
import functools
import math

import jax
import jax.numpy as jnp
from jax import lax
from jax.experimental import pallas as pl
from jax.experimental.pallas import tpu as pltpu

F32 = jnp.float32
BF16 = jnp.bfloat16
MESH = pl.DeviceIdType.MESH
ANY = pl.BlockSpec(memory_space=pl.ANY)

N_DEV = 8
D_MODEL = 1024
EPS = 1e-6
NORM_ROWS = 512
FUSED_ROWS = 512
S5_W = 512
S5_NS = 2048
SCAN_GROUPS = 4
SCAN_CHUNK = 1024
FOX_W = 512
EVEN_IN = 2056
EVEN_PAD = 2176
ODD_IN = 1536
LANES = 128
PIECE = 4 * D_MODEL // N_DEV
VMEM_LIMIT = 56 * 1024 * 1024

ADAM_LR = 0.001
ADAM_B1 = 0.9
ADAM_B2 = 0.999
ADAM_EPS = 1e-08
ADAM_WD = 0.01
ADAM_STEP = 10

NT = (((1,), (1,)), ((), ()))
TN = (((0,), (0,)), ((), ()))
NN = (((1,), (0,)), ((), ()))


def _cp(*sem):
    return pltpu.CompilerParams(dimension_semantics=sem, vmem_limit_bytes=VMEM_LIMIT)


def _sds(shape, dtype=F32):
    return jax.ShapeDtypeStruct(tuple(shape), dtype)


def _gelu(x):
    t = jnp.tanh(0.7978845608028654 * (x + 0.044715 * x * x * x))
    return 0.5 * x * (1.0 + t)


def _gelu_grad(x):
    t = jnp.tanh(0.7978845608028654 * (x + 0.044715 * x * x * x))
    du = 0.7978845608028654 * (1.0 + 3.0 * 0.044715 * x * x)
    return 0.5 * (1.0 + t) + 0.5 * x * (1.0 - t * t) * du


def _sigmoid(x):
    return 1.0 / (1.0 + jnp.exp(-x))


def _dot(a, b, dn=NN):
    return lax.dot_general(a, b, dn, preferred_element_type=F32)


def _mm(a, b, *, name, ta=False, tb=False, b3=False, out3=False, out_dtypes=(F32,), epi=None, extra=(),
        cols=(), vecs=(), out_kinds=None, bm=1024, bn=1024, bk=1024, dep=None):
    a_list = list(a) if isinstance(a, (list, tuple)) else [a]
    widths = [p.shape[1] for p in a_list]
    offs = [sum(widths[:i]) for i in range(len(widths))]
    na = len(a_list)
    M = sum(widths) if ta else a_list[0].shape[0]
    K = a_list[0].shape[0] if ta else sum(widths)
    if na > 1:
        assert not b3 and not tb
        bm, bk = (M, bk) if ta else (bm, K)
    pw = b.shape[2] if b3 else PIECE
    if b3:
        N = b.shape[1] if tb else b.shape[0] * pw
        assert (b.shape[0] * pw if tb else b.shape[1]) == K
    else:
        N = b.shape[0] if tb else b.shape[1]
    bm, bn, bk = min(bm, M), min(bn, N), min(bk, K)
    assert M % bm == 0 and N % bn == 0 and K % bk == 0, (name, M, N, K, bm, bn, bk)
    assert not (b3 or out3) or ((bk if tb else bn) % pw == 0 and bn % PIECE == 0)
    nk = K // bk
    n_extra = len(extra) + len(cols) + len(vecs)
    n_out = len(out_dtypes)
    out_kinds = tuple(out_kinds) if out_kinds is not None else ("full",) * n_out
    dn = (((0 if ta else 1,), (1 if tb else 0,)), ((), ()))

    use_acc = nk > 1

    def body(*refs):
        a_refs, b_ref = refs[:na], refs[na]
        a_ref = a_refs[0]
        e_refs = refs[na + 1:na + 1 + n_extra]
        first_out = na + 1 + n_extra + (0 if dep is None else 1)
        o_refs = refs[first_out:first_out + n_out]
        acc_ref = refs[-1] if use_acc else o_refs[0]
        i, k = pl.program_id(0), pl.program_id(2)

        def dot(a_v, b_v):
            return lax.dot_general(a_v.astype(BF16), b_v.astype(BF16), dn, preferred_element_type=F32)

        everything = slice(None)
        if na > 1 and ta:
            terms = [(pl.ds(off, w), everything, r, b_ref) for r, off, w in zip(a_refs, offs, widths)]
        elif na > 1:
            terms = [(everything, everything, r, b_ref.at[pl.ds(off, w), :]) for r, off, w in zip(a_refs, offs, widths)]
        elif not b3:
            terms = [(everything, everything, a_ref, b_ref)]
        elif tb:
            terms = [(everything, everything,
                      a_ref.at[pl.ds(t * pw, pw), :] if ta else a_ref.at[:, pl.ds(t * pw, pw)], b_ref.at[t])
                     for t in range(bk // pw)]
        else:
            terms = [(everything, pl.ds(t * pw, pw), a_ref, b_ref.at[t]) for t in range(bn // pw)]

        def finish(acc):
            outs = (acc,) if epi is None else epi(acc, *[e[...] for e in e_refs])
            for o_ref, o, kind in zip(o_refs, outs, out_kinds):
                if kind == "vsum":
                    @pl.when(i == 0)
                    def _(o_ref=o_ref, o=o):
                        o_ref[...] = o

                    @pl.when(i > 0)
                    def _(o_ref=o_ref, o=o):
                        o_ref[...] += o
                elif out3:
                    for t in range(bn // PIECE):
                        o_ref[t] = o[:, t * PIECE:(t + 1) * PIECE].astype(o_ref.dtype)
                else:
                    o_ref[...] = o.astype(o_ref.dtype)

        if nk == 1:
            bands = {}
            for rows, cols, a_r, b_r in terms:
                key = (getattr(rows, "start", None), getattr(cols, "start", None))
                val = dot(a_r[...], b_r[...])
                bands[key] = val if key not in bands else bands[key] + val
            vals = list(bands.values())
            if len(vals) == 1:
                finish(vals[0])
            else:
                finish(jnp.concatenate(vals, axis=0 if (na > 1 and ta) else 1))
            return

        @pl.when(k == 0)
        def _():
            acc_ref[...] = jnp.zeros_like(acc_ref)

        for rows, cols, a_r, b_r in terms:
            acc_ref[rows, cols] += dot(a_r[...], b_r[...])

        @pl.when(k == nk - 1)
        def _():
            finish(acc_ref[...])

    if na > 1:
        a_specs = [pl.BlockSpec((bk, w), lambda i, j, k: (k, 0)) if ta else pl.BlockSpec((bm, w), lambda i, j, k: (i, 0))
                   for w in widths]
    else:
        a_specs = [pl.BlockSpec((bk, bm), lambda i, j, k: (k, i)) if ta else
                   pl.BlockSpec((bm, bk), lambda i, j, k: (i, k))]
    if b3:
        if tb:
            b_spec = pl.BlockSpec((bk // pw, bn, pw), lambda i, j, k: (k, j, 0))
        else:
            b_spec = pl.BlockSpec((bn // pw, bk, pw), lambda i, j, k: (j, k, 0))
    else:
        b_spec = pl.BlockSpec((bn, bk), lambda i, j, k: (j, k)) if tb else pl.BlockSpec((bk, bn), lambda i, j, k: (k, j))
    e_specs = ([pl.BlockSpec((bm, bn), lambda i, j, k: (i, j)) for _ in extra]
               + [pl.BlockSpec((bm, 1), lambda i, j, k: (i, 0)) for _ in cols]
               + [pl.BlockSpec((1, bn), lambda i, j, k: (0, j)) for _ in vecs])
    if out3:
        o_specs = [pl.BlockSpec((bn // PIECE, bm, PIECE), lambda i, j, k: (j, i, 0)) for _ in out_dtypes]
        o_shapes = [_sds((N // PIECE, M, PIECE), dt) for dt in out_dtypes]
    else:
        spec_of = {"full": pl.BlockSpec((bm, bn), lambda i, j, k: (i, j)),
                   "col": pl.BlockSpec((bm, 1), lambda i, j, k: (i, 0)),
                   "vsum": pl.BlockSpec((1, bn), lambda i, j, k: (0, j))}
        shape_of = {"full": (M, N), "col": (M, 1), "vsum": (1, N)}
        o_specs = [spec_of[kind] for kind in out_kinds]
        o_shapes = [_sds(shape_of[kind], dt) for kind, dt in zip(out_kinds, out_dtypes)]
    assert "col" not in out_kinds or bn == N
    outs = pl.pallas_call(
        body, name=name, grid=(M // bm, N // bn, nk),
        in_specs=a_specs + [b_spec] + e_specs + ([] if dep is None else [ANY]),
        out_specs=o_specs, out_shape=o_shapes,
        scratch_shapes=[pltpu.VMEM((bm, bn), F32)] if use_acc else [],
        compiler_params=_cp("arbitrary" if "vsum" in out_kinds else "parallel", "parallel", "arbitrary"),
    )(*a_list, b, *extra, *cols, *vecs, *([] if dep is None else [dep]))
    return outs[0] if n_out == 1 else outs


def _epi_relu2(acc):
    r = jnp.maximum(acc, 0.0)
    return acc, r * r


def _epi_relu2_bwd(acc, p):
    return (acc * (2.0 * jnp.maximum(p.astype(F32), 0.0)),)


def _row_spec(rb, w=D_MODEL):
    return pl.BlockSpec((rb, w), lambda i: (i, 0))


def _vec_spec(w=D_MODEL):
    return pl.BlockSpec((1, w), lambda i: (0, 0))


def _rstd(v):
    return lax.rsqrt(jnp.mean(v * v, axis=-1, keepdims=True) + EPS)


def _rms_fwd(x, g, name):
    L = x.shape[0]
    rb = min(NORM_ROWS, L)

    def body(x_ref, g_ref, h_ref, r_ref):
        xv = x_ref[...]
        r = _rstd(xv)
        h_ref[...] = (xv * r * g_ref[...]).astype(BF16)
        r_ref[...] = r

    return pl.pallas_call(
        body, name=name, grid=(L // rb,),
        in_specs=[_row_spec(rb), _vec_spec()],
        out_specs=[_row_spec(rb), _row_spec(rb, 1)],
        out_shape=[_sds((L, D_MODEL), BF16), _sds((L, 1))],
        compiler_params=_cp("parallel"),
    )(x, g)


def _post_loss_fwd(x_in, y, g_post, target, name):
    L = x_in.shape[0]
    rb = min(NORM_ROWS, L)

    def body(x_ref, y_ref, gp_ref, t_ref, gx_ref, ry_ref, loss_ref):
        i = pl.program_id(0)
        yv = y_ref[...]
        ry = _rstd(yv)
        diff = x_ref[...] + yv * ry * gp_ref[...] - t_ref[...]
        gx_ref[...] = diff * (1.0 / D_MODEL)
        ry_ref[...] = ry

        @pl.when(i == 0)
        def _():
            loss_ref[...] = jnp.zeros_like(loss_ref)

        loss_ref[...] += jnp.sum(diff * diff, keepdims=True)

    return pl.pallas_call(
        body, name=name, grid=(L // rb,),
        in_specs=[_row_spec(rb), _row_spec(rb), _vec_spec(), _row_spec(rb)],
        out_specs=[_row_spec(rb), _row_spec(rb, 1), pl.BlockSpec((1, 1), lambda i: (0, 0))],
        out_shape=[_sds((L, D_MODEL)), _sds((L, 1)), _sds((1, 1))],
        compiler_params=_cp("arbitrary"),
    )(x_in, y, g_post, target)


def _rms_bwd_rows(dy, xv, r, g):
    n = xv * r
    dyg = dy * g
    return r * (dyg - n * jnp.mean(dyg * n, axis=-1, keepdims=True)), n


POST_PRE_DTYPES = (F32, F32, BF16, F32, F32)
POST_PRE_KINDS = ("full", "col", "full", "col", "full")
PRE_POST_BWD_DTYPES = (F32, F32, BF16, F32)
PRE_POST_BWD_KINDS = ("full", "vsum", "full", "vsum")


def _epi_post_pre(y, x_in, g_post, g_pre):
    ry = _rstd(y)
    xo = x_in + y * ry * g_post
    rx = _rstd(xo)
    return xo, ry, xo * rx * g_pre, rx, y


def _epi_pre_post_bwd(gh, x, g_out, y_prev, rx, ry_prev, g_pre, g_post_prev):
    gx, n = _rms_bwd_rows(gh, x, rx, g_pre)
    gi = g_out + gx
    gy, ny = _rms_bwd_rows(gi, y_prev, ry_prev, g_post_prev)
    return gi, jnp.sum(gh * n, axis=0, keepdims=True), gy, jnp.sum(gi * ny, axis=0, keepdims=True)


def _epi_pre_bwd(gh, x, g_out, rx, g_pre):
    gx, n = _rms_bwd_rows(gh, x, rx, g_pre)
    return g_out + gx, jnp.sum(gh * n, axis=0, keepdims=True)


def _post_bwd(g_out, y, ry, g_post, name):
    L = y.shape[0]
    rb = min(NORM_ROWS, L)

    def body(go_ref, y_ref, ry_ref, gp_ref, gy_ref, gg_ref):
        i = pl.program_id(0)
        go = go_ref[...]
        gy, n = _rms_bwd_rows(go, y_ref[...], ry_ref[...], gp_ref[...])
        gy_ref[...] = gy.astype(BF16)

        @pl.when(i == 0)
        def _():
            gg_ref[...] = jnp.zeros_like(gg_ref)

        gg_ref[...] += jnp.sum(go * n, axis=0, keepdims=True)

    return pl.pallas_call(
        body, name=name, grid=(L // rb,),
        in_specs=[_row_spec(rb), _row_spec(rb), _row_spec(rb, 1), _vec_spec()],
        out_specs=[_row_spec(rb), _vec_spec()],
        out_shape=[_sds((L, D_MODEL), BF16), _sds((1, D_MODEL))],
        compiler_params=_cp("arbitrary"),
    )(g_out, y, ry, g_post)


def _cmul(ar, ai, br, bi):
    return ar * br - ai * bi, ar * bi + ai * br


def _zoh_cols(lr, li, ldt):
    dt = jnp.exp(ldt)
    mag = jnp.exp(lr * dt)
    ar = mag * jnp.cos(li * dt)
    ai = mag * jnp.sin(li * dt)
    den = lr * lr + li * li
    nr = ar - 1.0
    qr = (nr * lr + ai * li) / den
    qi = (ai * lr - nr * li) / den
    return dt, ar, ai, qr, qi, den


def _b_mask():
    r = lax.broadcasted_iota(jnp.int32, (S5_NS, LANES), 0)
    c = lax.broadcasted_iota(jnp.int32, (S5_NS, LANES), 1)
    return ((r >> 6) & 7) == (c >> 4)


def _c_mask():
    r = lax.broadcasted_iota(jnp.int32, (S5_W, 512), 0)
    c = lax.broadcasted_iota(jnp.int32, (S5_W, 512), 1)
    return ((r >> 4) & 7) == (c >> 6)


def _s5_prep(lam_r, ldt_r, lam_c, ldt_c, b_t, c_t, name):
    def body(lam_r_ref, ldt_r_ref, lam_c_ref, ldt_c_ref, b_ref, c_ref, tab_ref, bset_ref, cset_ref):
        lr, li = lam_r_ref[0:1, :], lam_r_ref[1:2, :]
        dt = jnp.exp(ldt_r_ref[...])
        mag = jnp.exp(lr * dt)
        p1r, p1i = mag * jnp.cos(li * dt), mag * jnp.sin(li * dt)
        p2r, p2i = _cmul(p1r, p1i, p1r, p1i)
        p3r, p3i = _cmul(p2r, p2i, p1r, p1i)
        p4r, p4i = _cmul(p2r, p2i, p2r, p2i)
        p5r, p5i = _cmul(p4r, p4i, p1r, p1i)
        p6r, p6i = _cmul(p4r, p4i, p2r, p2i)
        p7r, p7i = _cmul(p4r, p4i, p3r, p3i)
        p8r, p8i = _cmul(p4r, p4i, p4r, p4i)
        pw_r = [p1r, p2r, p3r, p4r, p5r, p6r, p7r, p8r]
        pw_i = [p1i, p2i, p3i, p4i, p5i, p6i, p7i, p8i]
        row = lax.broadcasted_iota(jnp.int32, (8, S5_NS), 0)
        zero = jnp.zeros((8, S5_NS), F32)

        def bc(v):
            return jnp.broadcast_to(v, (8, S5_NS))

        for d in range(2):
            sgn = 1.0 if d == 0 else -1.0
            for t, s in enumerate((1, 2, 4)):
                live = (row >= s) if d == 0 else (row <= 7 - s)
                tab_ref[d, 2 * t] = jnp.where(live, bc(pw_r[s - 1]), zero)
                tab_ref[d, 2 * t + 1] = jnp.where(live, bc(sgn * pw_i[s - 1]), zero)
            cr, ci = zero, zero
            for i in range(8):
                e = i if d == 0 else 7 - i
                cr = jnp.where(row == i, bc(pw_r[e]), cr)
                ci = jnp.where(row == i, bc(sgn * pw_i[e]), ci)
            tab_ref[d, 6] = cr
            tab_ref[d, 7] = ci

        _, _, _, qr, qi, _ = _zoh_cols(lam_c_ref[:, 0:1], lam_c_ref[:, 1:2], ldt_c_ref[...])
        bm = _b_mask()
        br, bi = b_ref[0], b_ref[1]
        bset_ref[0] = jnp.where(bm, qr * br - qi * bi, 0.0).astype(BF16)
        bset_ref[1] = jnp.where(bm, qr * bi + qi * br, 0.0).astype(BF16)
        cm = _c_mask()
        cset_ref[0] = jnp.where(cm, c_ref[0], 0.0).astype(BF16)
        cset_ref[1] = jnp.where(cm, c_ref[1], 0.0).astype(BF16)

    vm = pl.BlockSpec(memory_space=pltpu.VMEM)
    return pl.pallas_call(
        body, name=name, in_specs=[vm] * 6, out_specs=[vm] * 3,
        out_shape=[_sds((2, 8, 8, S5_NS)), _sds((2, S5_NS, LANES), BF16), _sds((2, S5_W, 512), BF16)],
        compiler_params=pltpu.CompilerParams(vmem_limit_bytes=VMEM_LIMIT),
    )(lam_r, ldt_r, lam_c, ldt_c, b_t, c_t)


SCAN_W = SCAN_GROUPS * LANES


def _scan_chunk(src_ref, dst_ref, tab_ref, carry_ref, nb, reverse, xs_ref=None, acc_ref=None):
    row = lax.broadcasted_iota(jnp.int32, (8, LANES), 0)

    def step(i, carry):
        b = (nb - 1 - i) if reverse else i
        off = pl.multiple_of(b * 8, 8)
        out = []
        for g in range(SCAN_GROUPS):
            lanes = pl.ds(g * LANES, LANES)
            cr, ci = carry[2 * g], carry[2 * g + 1]
            yr = src_ref[0, pl.ds(off, 8), lanes]
            yi = src_ref[1, pl.ds(off, 8), lanes]
            for t, s in enumerate((1, 2, 4)):
                sh = (8 - s) if reverse else s
                sr = pltpu.roll(yr, sh, 0)
                si = pltpu.roll(yi, sh, 0)
                mr, mi = tab_ref[2 * t, :, lanes], tab_ref[2 * t + 1, :, lanes]
                yr, yi = yr + mr * sr - mi * si, yi + mr * si + mi * sr
            pr, pi = tab_ref[6, :, lanes], tab_ref[7, :, lanes]
            yr, yi = yr + pr * cr - pi * ci, yi + pr * ci + pi * cr
            dst_ref[0, pl.ds(off, 8), lanes] = yr
            dst_ref[1, pl.ds(off, 8), lanes] = yi
            if xs_ref is not None:
                nr = jnp.where(row == 7, cr, pltpu.roll(yr, 7, 0))
                ni = jnp.where(row == 7, ci, pltpu.roll(yi, 7, 0))
                xr = xs_ref[0, pl.ds(off, 8), lanes]
                xi = xs_ref[1, pl.ds(off, 8), lanes]
                acc_ref[0, :, lanes] += xr * nr + xi * ni
                acc_ref[1, :, lanes] += xr * ni - xi * nr
            last = 0 if reverse else 7
            out += [jnp.broadcast_to(yr[last:last + 1, :], (8, LANES)),
                    jnp.broadcast_to(yi[last:last + 1, :], (8, LANES))]
        return tuple(out)

    init = []
    for g in range(SCAN_GROUPS):
        init += [carry_ref[0, :, pl.ds(g * LANES, LANES)], carry_ref[1, :, pl.ds(g * LANES, LANES)]]
    fin = lax.fori_loop(0, nb, step, tuple(init))
    for g in range(SCAN_GROUPS):
        carry_ref[0, :, pl.ds(g * LANES, LANES)] = fin[2 * g]
        carry_ref[1, :, pl.ds(g * LANES, LANES)] = fin[2 * g + 1]


def _s5_scan_fwd(z, bset, tabs, name):
    L = z.shape[0]
    tl = min(SCAN_CHUNK, L)
    nc = L // tl

    def body(u_ref, b_ref, tab_ref, x_ref, carry_ref):
        @pl.when(pl.program_id(1) == 0)
        def _():
            carry_ref[...] = jnp.zeros_like(carry_ref)

        u = u_ref[...].astype(BF16)
        x_ref[0] = _dot(u, b_ref[0], NT)
        x_ref[1] = _dot(u, b_ref[1], NT)
        _scan_chunk(x_ref, x_ref, tab_ref, carry_ref, tl // 8, False)

    return pl.pallas_call(
        body, name=name, grid=(S5_NS // SCAN_W, nc),
        in_specs=[pl.BlockSpec((tl, LANES), lambda j, c: (c, j)),
                  pl.BlockSpec((2, SCAN_W, LANES), lambda j, c: (0, j, 0)),
                  pl.BlockSpec((None, 8, 8, SCAN_W), lambda j, c: (0, 0, 0, j))],
        out_specs=pl.BlockSpec((2, tl, SCAN_W), lambda j, c: (0, c, j)),
        out_shape=_sds((2, L, S5_NS)),
        scratch_shapes=[pltpu.VMEM((2, 8, SCAN_W), F32)],
        compiler_params=_cp("parallel", "arbitrary"),
    )(z, bset, tabs)


def _s5_scan_bwd(gyl, cset, xs, z, bset, gud, tabs, name):
    L = z.shape[0]
    tl = min(SCAN_CHUNK, L)
    nc = L // tl

    def body(g_ref, c_ref, xs_ref, u_ref, b_ref, gud_ref, tab_ref, gu_ref, ga_ref, gb_ref, gc_ref,
             gx_ref, carry_ref, acc_ref):
        c = pl.program_id(1)

        @pl.when(c == 0)
        def _():
            carry_ref[...] = jnp.zeros_like(carry_ref)
            acc_ref[...] = jnp.zeros_like(acc_ref)
            gb_ref[...] = jnp.zeros_like(gb_ref)
            gc_ref[...] = jnp.zeros_like(gc_ref)

        gy = g_ref[...].astype(BF16)
        gx_ref[0] = _dot(gy, c_ref[0])
        gx_ref[1] = -_dot(gy, c_ref[1])
        gc_ref[0] += _dot(gy, xs_ref[0].astype(BF16), TN)
        gc_ref[1] -= _dot(gy, xs_ref[1].astype(BF16), TN)
        _scan_chunk(gx_ref, gx_ref, tab_ref, carry_ref, tl // 8, True, xs_ref, acc_ref)
        gr = gx_ref[0].astype(BF16)
        gi = gx_ref[1].astype(BF16)
        gu_ref[...] = gud_ref[...] + _dot(gr, b_ref[0]) + _dot(gi, b_ref[1])
        u = u_ref[...].astype(BF16)
        gb_ref[0] += _dot(gr, u, TN)
        gb_ref[1] += _dot(gi, u, TN)

        @pl.when(c == nc - 1)
        def _():
            ga_ref[0:1, :] = jnp.sum(acc_ref[0], axis=0, keepdims=True)
            ga_ref[1:2, :] = jnp.sum(acc_ref[1], axis=0, keepdims=True)

    rev = lambda j, c: (nc - 1 - c, j)
    col = pl.BlockSpec((tl, LANES), rev)
    return pl.pallas_call(
        body, name=name, grid=(S5_NS // SCAN_W, nc),
        in_specs=[col, pl.BlockSpec((2, LANES, SCAN_W), lambda j, c: (0, j, 0)),
                  pl.BlockSpec((2, tl, SCAN_W), lambda j, c: (0, nc - 1 - c, j)), col,
                  pl.BlockSpec((2, SCAN_W, LANES), lambda j, c: (0, j, 0)), col,
                  pl.BlockSpec((None, 8, 8, SCAN_W), lambda j, c: (1, 0, 0, j))],
        out_specs=[col, pl.BlockSpec((2, SCAN_W), lambda j, c: (0, j)),
                   pl.BlockSpec((2, SCAN_W, LANES), lambda j, c: (0, j, 0)),
                   pl.BlockSpec((2, LANES, SCAN_W), lambda j, c: (0, j, 0))],
        out_shape=[_sds((L, S5_W)), _sds((2, S5_NS)), _sds((2, S5_NS, LANES)), _sds((2, S5_W, 512))],
        scratch_shapes=[pltpu.VMEM((2, tl, SCAN_W), F32), pltpu.VMEM((2, 8, SCAN_W), F32),
                        pltpu.VMEM((2, 8, SCAN_W), F32)],
        compiler_params=_cp("parallel", "arbitrary"),
    )(gyl, cset, xs, z, bset, gud, tabs)


def _s5_out_fwd(xs, cset, z, dvec, wglu, name):
    L = z.shape[0]
    bl = min(256, L)

    def body(x_ref, c_ref, u_ref, d_ref, w_ref, ylin_ref, ya_ref):
        cols = []
        for j in range(4):
            xr = x_ref[0, :, 512 * j:512 * (j + 1)].astype(BF16)
            xi = x_ref[1, :, 512 * j:512 * (j + 1)].astype(BF16)
            cr = c_ref[0, LANES * j:LANES * (j + 1), :]
            ci = c_ref[1, LANES * j:LANES * (j + 1), :]
            cols.append(_dot(xr, cr, NT) - _dot(xi, ci, NT))
        ylin = jnp.concatenate(cols, axis=1) + d_ref[...] * u_ref[...]
        yg = _gelu(ylin)
        t = _dot(yg.astype(BF16), w_ref[...])
        ylin_ref[...] = ylin
        ya_ref[...] = (yg * _sigmoid(t)).astype(BF16)

    return pl.pallas_call(
        body, name=name, grid=(L // bl,),
        in_specs=[pl.BlockSpec((2, bl, S5_NS), lambda i: (0, i, 0)),
                  pl.BlockSpec((2, S5_W, 512), lambda i: (0, 0, 0)),
                  pl.BlockSpec((bl, S5_W), lambda i: (i, 0)),
                  pl.BlockSpec((1, S5_W), lambda i: (0, 0)),
                  pl.BlockSpec((S5_W, S5_W), lambda i: (0, 0))],
        out_specs=[pl.BlockSpec((bl, S5_W), lambda i: (i, 0))] * 2,
        out_shape=[_sds((L, S5_W)), _sds((L, S5_W), BF16)],
        compiler_params=_cp("parallel"),
    )(xs, cset, z, dvec, wglu)


def _s5_glu_bwd(g_m, ylin, z, dvec, wglu, name):
    L = z.shape[0]
    bl = min(256, L)

    def body(g_ref, ylin_ref, u_ref, d_ref, w_ref, gyl_ref, gud_ref, gw_ref, gd_ref):
        i = pl.program_id(0)
        ylin = ylin_ref[...]
        yg = _gelu(ylin)
        ygb = yg.astype(BF16)
        sg = _sigmoid(_dot(ygb, w_ref[...]))
        gya = g_ref[...]
        gt = gya * yg * sg * (1.0 - sg)
        gtb = gt.astype(BF16)
        gyg = gya * sg + _dot(gtb, w_ref[...], NT)
        gyl = gyg * _gelu_grad(ylin)
        gyl_ref[...] = gyl
        gud_ref[...] = gyl * d_ref[...]

        @pl.when(i == 0)
        def _():
            gw_ref[...] = jnp.zeros_like(gw_ref)
            gd_ref[...] = jnp.zeros_like(gd_ref)

        gw_ref[...] += _dot(ygb, gtb, TN)
        gd_ref[...] += jnp.sum(gyl * u_ref[...], axis=0, keepdims=True)

    blk = pl.BlockSpec((bl, S5_W), lambda i: (i, 0))
    return pl.pallas_call(
        body, name=name, grid=(L // bl,),
        in_specs=[blk, blk, blk, pl.BlockSpec((1, S5_W), lambda i: (0, 0)),
                  pl.BlockSpec((S5_W, S5_W), lambda i: (0, 0))],
        out_specs=[blk, blk, pl.BlockSpec((S5_W, S5_W), lambda i: (0, 0)), pl.BlockSpec((1, S5_W), lambda i: (0, 0))],
        out_shape=[_sds((L, S5_W)), _sds((L, S5_W)), _sds((S5_W, S5_W)), _sds((1, S5_W))],
        compiler_params=_cp("arbitrary"),
    )(g_m, ylin, z, dvec, wglu)


def _s5_param_bwd(lam_c, ldt_c, b_t, gb, ga_c, gc, name):
    def body(lam_ref, ldt_ref, b_ref, gb_ref, ga_ref, gc_ref, glam_ref, gldt_ref, gbo_ref, gco_ref):
        lr, li = lam_ref[:, 0:1], lam_ref[:, 1:2]
        dt, ar, ai, qr, qi, den = _zoh_cols(lr, li, ldt_ref[...])
        bm = _b_mask()
        gbr = jnp.where(bm, gb_ref[0], 0.0)
        gbi = jnp.where(bm, gb_ref[1], 0.0)
        br, bi = b_ref[0], b_ref[1]
        obr = gbr * qr + gbi * qi
        obi = gbi * qr - gbr * qi
        gqr = jnp.sum(gbr * br + gbi * bi, axis=1, keepdims=True)
        gqi = jnp.sum(gbi * br - gbr * bi, axis=1, keepdims=True)
        for s in (64, 32, 16):
            obr = obr + pltpu.roll(obr, s, 1)
            obi = obi + pltpu.roll(obi, s, 1)
        gbo_ref[0] = obr
        gbo_ref[1] = obi
        gar = ga_ref[:, 0:1] + (gqr * lr - gqi * li) / den
        gai = ga_ref[:, 1:2] + (gqr * li + gqi * lr) / den
        qlr = (qr * lr + qi * li) / den
        qli = (qi * lr - qr * li) / den
        glr = -(gqr * qlr + gqi * qli)
        gli = -(gqi * qlr - gqr * qli)
        glr = glr + dt * (gar * ar + gai * ai)
        gli = gli + dt * (gai * ar - gar * ai)
        wr, wi = _cmul(lr, li, ar, ai)
        gldt = (gar * wr + gai * wi) * dt
        glam_ref[:, 0:1] = glr
        glam_ref[:, 1:2] = gli
        r = lax.broadcasted_iota(jnp.int32, (S5_NS, 32), 0)
        c = lax.broadcasted_iota(jnp.int32, (S5_NS, 32), 1)
        gldt_ref[...] = jnp.sum(jnp.where((r >> 6) == c, gldt, 0.0), axis=0, keepdims=True)
        cm = _c_mask()
        for k in range(2):
            oc = jnp.where(cm, gc_ref[k], 0.0)
            for s in (256, 128, 64):
                oc = oc + pltpu.roll(oc, s, 1)
            gco_ref[k] = oc[:, 0:LANES]

    vm = pl.BlockSpec(memory_space=pltpu.VMEM)
    return pl.pallas_call(
        body, name=name, in_specs=[vm] * 6, out_specs=[vm] * 4,
        out_shape=[_sds((S5_NS, 2)), _sds((1, 32)), _sds((2, S5_NS, LANES)), _sds((2, S5_W, LANES))],
        compiler_params=pltpu.CompilerParams(vmem_limit_bytes=VMEM_LIMIT),
    )(lam_c, ldt_c, b_t, gb, ga_c, gc)


FL_BLK = EVEN_PAD // LANES - 1
Q_BLK, K_BLK, V_BLK = 4, 8, 12
NEG = -1e30


def _log_sigmoid(v):
    return jnp.minimum(v, 0.0) - jnp.log(1.0 + jnp.exp(-jnp.abs(v)))


def _fox_f_fwd(z, bf, name):
    L = z.shape[0]
    tl = min(256, L)

    def body(fl_ref, b_ref, f_ref, fq_ref, carry_ref):
        i = pl.program_id(0)

        @pl.when(i == 0)
        def _():
            carry_ref[...] = jnp.zeros_like(carry_ref)

        lf = _log_sigmoid(fl_ref[...] + b_ref[...])
        r = lax.broadcasted_iota(jnp.int32, (tl, tl), 0)
        c = lax.broadcasted_iota(jnp.int32, (tl, tl), 1)
        tri = (r >= c).astype(F32)
        cs = lax.dot_general(tri, lf, NN, precision=lax.Precision.HIGHEST, preferred_element_type=F32) + carry_ref[...]
        f_ref[...] = cs
        carry_ref[...] = cs[tl - 1:tl, :]
        expand = (lax.broadcasted_iota(jnp.int32, (LANES, FOX_W), 0)
                  == (lax.broadcasted_iota(jnp.int32, (LANES, FOX_W), 1) >> 6)).astype(F32)
        fq_ref[...] = lax.dot_general(cs, expand, NN, precision=lax.Precision.HIGHEST, preferred_element_type=F32)

    return pl.pallas_call(
        body, name=name, grid=(L // tl,),
        in_specs=[pl.BlockSpec((tl, LANES), lambda i: (i, FL_BLK)), pl.BlockSpec((1, LANES), lambda i: (0, 0))],
        out_specs=[pl.BlockSpec((tl, LANES), lambda i: (i, 0)), pl.BlockSpec((tl, FOX_W), lambda i: (i, 0))],
        out_shape=[_sds((L, LANES)), _sds((L, FOX_W))],
        scratch_shapes=[pltpu.VMEM((1, LANES), F32)],
        compiler_params=_cp("arbitrary"),
    )(z, bf)


def _fox_f_bwd(dFk, dfq, z, bf, name):
    L = z.shape[0]
    tl = min(256, L)
    nb = L // tl

    def body(dfk_ref, dfq_ref, fl_ref, b_ref, dfl_ref, db_ref, carry_ref):
        i = pl.program_id(0)

        @pl.when(i == 0)
        def _():
            carry_ref[...] = jnp.zeros_like(carry_ref)
            db_ref[...] = jnp.zeros_like(db_ref)

        sel = (lax.broadcasted_iota(jnp.int32, (FOX_W, LANES), 0)
               == 64 * lax.broadcasted_iota(jnp.int32, (FOX_W, LANES), 1)).astype(F32)
        dfq_h = lax.dot_general(dfq_ref[...], sel, NN, precision=lax.Precision.HIGHEST, preferred_element_type=F32)
        r = lax.broadcasted_iota(jnp.int32, (tl, tl), 0)
        c = lax.broadcasted_iota(jnp.int32, (tl, tl), 1)
        tri = (r <= c).astype(F32)
        cs = lax.dot_general(tri, dfk_ref[...] + dfq_h, NN, precision=lax.Precision.HIGHEST,
                             preferred_element_type=F32) + carry_ref[...]
        carry_ref[...] = cs[0:1, :]
        dfl = cs * _sigmoid(-(fl_ref[...] + b_ref[...]))
        dfl_ref[...] = dfl
        db_ref[...] += jnp.sum(dfl, axis=0, keepdims=True)

    return pl.pallas_call(
        body, name=name, grid=(nb,),
        in_specs=[pl.BlockSpec((tl, LANES), lambda i: (nb - 1 - i, 0)),
                  pl.BlockSpec((tl, FOX_W), lambda i: (nb - 1 - i, 0)),
                  pl.BlockSpec((tl, LANES), lambda i: (nb - 1 - i, FL_BLK)),
                  pl.BlockSpec((1, LANES), lambda i: (0, 0))],
        out_specs=[pl.BlockSpec((tl, LANES), lambda i: (nb - 1 - i, 0)), pl.BlockSpec((1, LANES), lambda i: (0, 0))],
        out_shape=[_sds((L, LANES)), _sds((1, LANES))],
        scratch_shapes=[pltpu.VMEM((1, LANES), F32)],
        compiler_params=_cp("arbitrary"),
    )(dFk, dfq, z, bf)


def _head_mask(hh):
    lane = lax.broadcasted_iota(jnp.int32, (1, LANES), 1)
    return (lane >> 6) == hh


FOX_T = 512


def _fox_head(x, hh):
    return jnp.where(_head_mask(hh), x, 0.0).astype(BF16)


def _fox_scores(qh, k, fq_ref, fr_ref, hh, causal):
    s = _dot(qh, k, NT) + (fq_ref[:, 64 * hh:64 * hh + 1] - fr_ref[hh:hh + 1, :])
    return s if causal is None else jnp.where(causal, s, NEG)


def _causal(T):
    return lax.broadcasted_iota(jnp.int32, (T, T), 1) <= lax.broadcasted_iota(jnp.int32, (T, T), 0)


def _fox_fwd(z, fq, frow, name):
    L = z.shape[0]
    T = min(FOX_T, L)
    nq = L // T

    def body(qt_ref, kt_ref, q_ref, k_ref, v_ref, fq_ref, fr_ref, o_ref, lse_ref, m_ref, l_ref, acc_ref):
        t = pl.program_id(1)
        qi, ki = qt_ref[t], kt_ref[t]

        @pl.when(ki == 0)
        def _():
            m_ref[...] = jnp.full_like(m_ref, NEG)
            l_ref[...] = jnp.zeros_like(l_ref)
            acc_ref[...] = jnp.zeros_like(acc_ref)

        def step(diagonal):
            q = q_ref[...] * 0.125
            k = k_ref[...].astype(BF16)
            v = v_ref[...].astype(BF16)
            causal = _causal(T) if diagonal else None
            s = jnp.concatenate([_fox_scores(_fox_head(q, hh), k, fq_ref, fr_ref, hh, causal) for hh in range(2)],
                                axis=0)
            m_old = m_ref[...]
            m_new = jnp.maximum(m_old, jnp.max(s, axis=1, keepdims=True))
            alpha = jnp.exp(m_old - m_new)
            p = jnp.exp(s - m_new)
            l_ref[...] = alpha * l_ref[...] + jnp.sum(p, axis=1, keepdims=True)
            m_ref[...] = m_new
            acc_ref[...] = alpha * acc_ref[...] + _dot(p.astype(BF16), v)

        @pl.when(ki < qi)
        def _():
            step(False)

        @pl.when(ki == qi)
        def _():
            step(True)
            h0 = _head_mask(0)
            l = l_ref[...]
            o_h = acc_ref[...] / l
            lse_h = m_ref[...] + jnp.log(l)
            o_ref[...] = jnp.where(h0, o_h[:T], o_h[T:])
            lse_ref[...] = jnp.where(h0, lse_h[:T], lse_h[T:])

    pairs = [(qi, ki) for qi in range(nq) for ki in range(qi + 1)]
    qt = jnp.asarray([p[0] for p in pairs], jnp.int32)
    kt = jnp.asarray([p[1] for p in pairs], jnp.int32)

    def qspec(base):
        return pl.BlockSpec((T, LANES), lambda j, t, qt, kt: (qt[t], base + j))

    def kspec(base):
        return pl.BlockSpec((T, LANES), lambda j, t, qt, kt: (kt[t], base + j))

    return pl.pallas_call(
        body, name=name,
        grid_spec=pltpu.PrefetchScalarGridSpec(
            num_scalar_prefetch=2, grid=(4, len(pairs)),
            in_specs=[qspec(Q_BLK), kspec(K_BLK), kspec(V_BLK), qspec(0),
                      pl.BlockSpec((None, 2, T), lambda j, t, qt, kt: (j, 0, kt[t]))],
            out_specs=[qspec(0), qspec(0)],
            scratch_shapes=[pltpu.VMEM((2 * T, 1), F32), pltpu.VMEM((2 * T, 1), F32),
                            pltpu.VMEM((2 * T, LANES), F32)]),
        out_shape=[_sds((L, FOX_W)), _sds((L, FOX_W))],
        compiler_params=_cp("parallel", "arbitrary"),
    )(qt, kt, z, z, z, fq, frow)


def _fox_bwd(z, fq, frow, o, lse, g_m, name):
    L = z.shape[0]
    T = min(FOX_T, L)
    nq = L // T

    pairs = [(qi, ki) for ki in range(nq) for qi in range(ki, nq)]
    qt = jnp.asarray([p[0] for p in pairs], jnp.int32)
    kt = jnp.asarray([p[1] for p in pairs], jnp.int32)

    def body(qt_ref, kt_ref, q_ref, k_ref, v_ref, fq_ref, fr_ref, o_ref, lse_ref, do_ref,
             dq_ref, dk_ref, dv_ref, dfq_ref, dfk_ref, dk_acc, dv_acc, df_acc):
        t = pl.program_id(1)
        qi, ki = qt_ref[t], kt_ref[t]

        @pl.when(t == 0)
        def _():
            dq_ref[...] = jnp.zeros_like(dq_ref)
            dfq_ref[...] = jnp.zeros_like(dfq_ref)

        @pl.when(qi == ki)
        def _():
            dk_acc[...] = jnp.zeros_like(dk_acc)
            dv_acc[...] = jnp.zeros_like(dv_acc)
            df_acc[...] = jnp.zeros_like(df_acc)

        def step(diagonal):
            q = q_ref[...] * 0.125
            qb = q.astype(BF16)
            k = k_ref[...].astype(BF16)
            v = v_ref[...].astype(BF16)
            do = do_ref[...]
            dob = do.astype(BF16)
            do_o = dob.astype(F32) * o_ref[...]
            causal = _causal(T) if diagonal else None
            dvs, dks, dqs, rss = [], [], [], []
            for hh in range(2):
                s = _fox_scores(_fox_head(q, hh), k, fq_ref, fr_ref, hh, causal)
                p = jnp.exp(s - lse_ref[:, 64 * hh:64 * hh + 1])
                dp = _dot(_fox_head(do, hh), v, NT)
                delta = jnp.sum(jnp.where(_head_mask(hh), do_o, 0.0), axis=1, keepdims=True)
                ds = p * (dp - delta)
                dsb = ds.astype(BF16)
                dvs.append(_dot(p.astype(BF16), dob, TN))
                dks.append(_dot(dsb, qb, TN))
                dqs.append(_dot(dsb, k))
                rss.append(jnp.sum(ds, axis=1, keepdims=True))
                df_acc[hh:hh + 1, :] -= jnp.sum(ds, axis=0, keepdims=True)
            h0 = _head_mask(0)
            dv_acc[...] += jnp.where(h0, dvs[0], dvs[1])
            dk_acc[...] += jnp.where(h0, dks[0], dks[1])
            rows = pl.ds(pl.multiple_of(qi * T, T), T)
            dq_ref[rows, :] += jnp.where(h0, dqs[0], dqs[1])
            dfq_ref[rows, :] += jnp.where(h0, rss[0], rss[1])

        @pl.when(qi > ki)
        def _():
            step(False)

        @pl.when(qi == ki)
        def _():
            step(True)

        @pl.when(qi == nq - 1)
        def _():
            dk_ref[...] = dk_acc[...]
            dv_ref[...] = dv_acc[...]
            dfk_ref[...] = df_acc[...]

        @pl.when(t == len(pairs) - 1)
        def _():
            dq_ref[...] = dq_ref[...] * 0.125

    def qside(base):
        return pl.BlockSpec((T, LANES), lambda j, t, qt, kt: (qt[t], base + j))

    def kside(base):
        return pl.BlockSpec((T, LANES), lambda j, t, qt, kt: (kt[t], base + j))

    pair = pl.BlockSpec((L, LANES), lambda j, t, qt, kt: (0, j))
    frow_spec = pl.BlockSpec((None, 2, T), lambda j, t, qt, kt: (j, 0, kt[t]))
    return pl.pallas_call(
        body, name=name,
        grid_spec=pltpu.PrefetchScalarGridSpec(
            num_scalar_prefetch=2, grid=(4, len(pairs)),
            in_specs=[qside(Q_BLK), kside(K_BLK), kside(V_BLK), qside(0), frow_spec, qside(0), qside(0), qside(4)],
            out_specs=[pair, kside(0), kside(0), pair, frow_spec],
            scratch_shapes=[pltpu.VMEM((T, LANES), F32), pltpu.VMEM((T, LANES), F32), pltpu.VMEM((2, T), F32)]),
        out_shape=[_sds((L, FOX_W)), _sds((L, FOX_W)), _sds((L, FOX_W)), _sds((L, FOX_W)), _sds((4, 2, L))],
        compiler_params=_cp("parallel", "arbitrary"),
    )(qt, kt, z, z, z, fq, frow, o, lse, g_m)


def _shift_rows(v, s, down, row):
    n = v.shape[0]
    if down:
        return jnp.where(row >= s, pltpu.roll(v, s, 0), 0.0)
    return jnp.where(row < n - s, pltpu.roll(v, n - s, 0), 0.0)


def _window_sum(v, g, down, row):
    out = jnp.zeros_like(v)
    s = v
    for k in range(4):
        s = s + _shift_rows(s, 1 << k, down, row)
        out = jnp.where(g == k, s, out)
    return out


def _pool_inv_cnt(g, row):
    w = jnp.left_shift(2, g).astype(F32)
    return 1.0 / jnp.minimum(row.astype(F32) + 1.0, w)


def _pool_fwd(z, pool_w, scale, name):
    L = z.shape[0]

    def body(x_ref, w_ref, s_ref, y_ref, p_ref):
        g = pl.program_id(0)
        row = lax.broadcasted_iota(jnp.int32, (L, LANES), 0)
        x = x_ref[...]
        pooled = (_window_sum(x, g, True, row) * _pool_inv_cnt(g, row) - x).astype(BF16)
        p_ref[...] = pooled
        y_ref[...] = (_dot(pooled, w_ref[...].astype(BF16)) * s_ref[...]).astype(BF16)

    col = pl.BlockSpec((L, LANES), lambda g: (0, g))
    return pl.pallas_call(
        body, name=name, grid=(4,),
        in_specs=[col, pl.BlockSpec((None, LANES, LANES), lambda g: (g, 0, 0)), pl.BlockSpec((1, LANES), lambda g: (0, g))],
        out_specs=[col, col],
        out_shape=[_sds((L, 512), BF16), _sds((L, 512), BF16)],
        compiler_params=_cp("parallel"),
    )(z, pool_w, scale)


def _pool_bwd(g_m, pooled, pool_w, scale, name):
    L = g_m.shape[0]

    def body(g_ref, p_ref, w_ref, s_ref, gx_ref, gw_ref, gs_ref):
        g = pl.program_id(0)
        row = lax.broadcasted_iota(jnp.int32, (L, LANES), 0)
        gy = g_ref[...]
        pooled = p_ref[...]
        wb = w_ref[...].astype(BF16)
        lin = _dot(pooled, wb)
        gs_ref[...] = jnp.sum(gy * lin, axis=0, keepdims=True)
        glin = (gy * s_ref[...]).astype(BF16)
        gw_ref[...] = _dot(pooled, glin, TN)
        gp = _dot(glin, wb, NT)
        gx_ref[...] = _window_sum(gp * _pool_inv_cnt(g, row), g, False, row) - gp

    col = pl.BlockSpec((L, LANES), lambda g: (0, g))
    wspec = pl.BlockSpec((None, LANES, LANES), lambda g: (g, 0, 0))
    vec = pl.BlockSpec((1, LANES), lambda g: (0, g))
    return pl.pallas_call(
        body, name=name, grid=(4,),
        in_specs=[col, col, wspec, vec],
        out_specs=[col, wspec, vec],
        out_shape=[_sds((L, 512)), _sds((4, LANES, LANES)), _sds((1, 512))],
        compiler_params=_cp("parallel"),
    )(g_m, pooled, pool_w, scale)


SGU_CHUNKS = 4


def _sgu_ln(v, gam, bet):
    gv = _gelu(v)
    mu = jnp.mean(gv, axis=-1, keepdims=True)
    xc = gv - mu
    rs = lax.rsqrt(jnp.mean(xc * xc, axis=-1, keepdims=True) + EPS)
    xh = xc * rs
    return xh, rs, xh * gam + bet


def _tril_ws(w_ref, g):
    r = lax.broadcasted_iota(jnp.int32, (LANES, LANES), 0)
    c = lax.broadcasted_iota(jnp.int32, (LANES, LANES), 1)
    return jnp.where(r >= c, w_ref[g], 0.0).astype(BF16)


def _sgu_fwd(z, ln_g, ln_b, w_s, b_st, name):
    L = z.shape[0]
    rb = min(SGU_CHUNKS * LANES, L)

    def body(u_ref, v_ref, g_ref, b_ref, w_ref, bs_ref, y_ref):
        _, _, vln = _sgu_ln(v_ref[...], g_ref[...], b_ref[...])
        gu = _gelu(u_ref[...])
        vb = vln.astype(BF16)
        for g in range(4):
            ws = _tril_ws(w_ref, g)
            for n in range(rb // LANES):
                rows = slice(n * LANES, (n + 1) * LANES)
                cols = slice(g * LANES, (g + 1) * LANES)
                mixed = _dot(ws, vb[rows, cols]) + bs_ref[:, g:g + 1]
                y_ref[rows, cols] = (gu[rows, cols] * mixed).astype(BF16)

    vm = lambda shape: pl.BlockSpec(shape, lambda i: tuple(0 for _ in shape))
    return pl.pallas_call(
        body, name=name, grid=(L // rb,),
        in_specs=[pl.BlockSpec((rb, 512), lambda i: (i, 1)), pl.BlockSpec((rb, 512), lambda i: (i, 2)),
                  vm((1, 512)), vm((1, 512)), vm((4, LANES, LANES)), vm((LANES, 4))],
        out_specs=pl.BlockSpec((rb, 512), lambda i: (i, 0)),
        out_shape=_sds((L, 512), BF16),
        compiler_params=_cp("parallel"),
    )(z, z, ln_g, ln_b, w_s, b_st)


def _sgu_bwd(g_m, z, ln_g, ln_b, w_s, b_st, name):
    L = z.shape[0]
    rb = min(SGU_CHUNKS * LANES, L)

    def body(gy_ref, u_ref, v_ref, g_ref, b_ref, w_ref, bs_ref, gu_ref, gv_ref, gw_ref, gbs_ref, gg_ref, gb_ref):
        i = pl.program_id(0)

        @pl.when(i == 0)
        def _():
            gw_ref[...] = jnp.zeros_like(gw_ref)
            gbs_ref[...] = jnp.zeros_like(gbs_ref)
            gg_ref[...] = jnp.zeros_like(gg_ref)
            gb_ref[...] = jnp.zeros_like(gb_ref)

        v = v_ref[...]
        u = u_ref[...]
        gy = gy_ref[...]
        xh, rs, vln = _sgu_ln(v, g_ref[...], b_ref[...])
        gel_u = _gelu(u)
        gmix = gy * gel_u
        vb = vln.astype(BF16)
        gmb = gmix.astype(BF16)
        r = lax.broadcasted_iota(jnp.int32, (LANES, LANES), 0)
        c = lax.broadcasted_iota(jnp.int32, (LANES, LANES), 1)
        gvln_cols = []
        for g in range(4):
            ws = _tril_ws(w_ref, g)
            cols = slice(g * LANES, (g + 1) * LANES)
            gw = jnp.zeros((LANES, LANES), F32)
            gbs = jnp.zeros((LANES, 1), F32)
            parts = []
            for n in range(rb // LANES):
                rows = slice(n * LANES, (n + 1) * LANES)
                mixed = _dot(ws, vb[rows, cols]) + bs_ref[:, g:g + 1]
                gu_ref[rows, cols] = gy[rows, cols] * mixed * _gelu_grad(u[rows, cols])
                parts.append(_dot(ws, gmb[rows, cols], TN))
                gw = gw + _dot(gmb[rows, cols], vb[rows, cols], NT)
                gbs = gbs + jnp.sum(gmix[rows, cols], axis=1, keepdims=True)
            gvln_cols.append(jnp.concatenate(parts, axis=0))
            gw_ref[g] += jnp.where(r >= c, gw, 0.0)
            gbs_ref[:, g:g + 1] += gbs
        gvln = jnp.concatenate(gvln_cols, axis=1)
        gg_ref[...] += jnp.sum(gvln * xh, axis=0, keepdims=True)
        gb_ref[...] += jnp.sum(gvln, axis=0, keepdims=True)
        gxh = gvln * g_ref[...]
        ggv = rs * (gxh - jnp.mean(gxh, axis=-1, keepdims=True) - xh * jnp.mean(gxh * xh, axis=-1, keepdims=True))
        gv_ref[...] = ggv * _gelu_grad(v)

    vm = lambda shape: pl.BlockSpec(shape, lambda i: tuple(0 for _ in shape))
    blk = pl.BlockSpec((rb, 512), lambda i: (i, 0))
    return pl.pallas_call(
        body, name=name, grid=(L // rb,),
        in_specs=[pl.BlockSpec((rb, 512), lambda i: (i, 1)), pl.BlockSpec((rb, 512), lambda i: (i, 1)),
                  pl.BlockSpec((rb, 512), lambda i: (i, 2)),
                  vm((1, 512)), vm((1, 512)), vm((4, LANES, LANES)), vm((LANES, 4))],
        out_specs=[blk, blk, vm((4, LANES, LANES)), vm((LANES, 4)), vm((1, 512)), vm((1, 512))],
        out_shape=[_sds((L, 512)), _sds((L, 512)), _sds((4, LANES, LANES)), _sds((LANES, 4)),
                   _sds((1, 512)), _sds((1, 512))],
        compiler_params=_cp("arbitrary"),
    )(g_m, z, z, ln_g, ln_b, w_s, b_st)


def _adamw_math(w, g, m, v):
    nm = ADAM_B1 * m + (1.0 - ADAM_B1) * g
    nv = ADAM_B2 * v + (1.0 - ADAM_B2) * (g * g)
    m_hat = nm / (1.0 - ADAM_B1 ** ADAM_STEP)
    v_hat = nv / (1.0 - ADAM_B2 ** ADAM_STEP)
    delta = -ADAM_LR * (m_hat / (jnp.sqrt(v_hat) + ADAM_EPS) + ADAM_WD * w)
    return delta, nm, nv


def _sum_adamw(parts, w, m, v, name, layer=0, prev=None):
    n_layers, R, C = w.shape
    rb = 128 if R % 128 == 0 else R

    def body(p_ref, w_ref, m_ref, v_ref, *rest):
        g_ref, d_ref, nm_ref, nv_ref = rest[-4:]
        g = p_ref[0].astype(F32)
        for s in range(1, N_DEV):
            g = g + p_ref[s].astype(F32)
        d, nm, nv = _adamw_math(w_ref[...], g, m_ref[...], v_ref[...])
        g_ref[...] = g
        d_ref[...] = d
        nm_ref[...] = nm
        nv_ref[...] = nv

    blk = pl.BlockSpec((None, rb, C), lambda i: (layer, i, 0))
    prev = [] if prev is None else list(prev)
    return pl.pallas_call(
        body, name=name, grid=(R // rb,),
        in_specs=[pl.BlockSpec((N_DEV, rb, C), lambda i: (0, i, 0)), blk, blk, blk] + [ANY] * len(prev),
        out_specs=[blk] * 4, out_shape=[_sds((n_layers, R, C))] * 4,
        input_output_aliases={4 + k: k for k in range(len(prev))},
        compiler_params=_cp("parallel"),
    )(parts, w, m, v, *prev)


def _sum_pieces(parts, name):
    _, R, C = parts.shape

    def body(p_ref, g_ref):
        g = p_ref[0].astype(F32)
        for s in range(1, N_DEV):
            g = g + p_ref[s].astype(F32)
        g_ref[...] = g

    vm = pl.BlockSpec(memory_space=pltpu.VMEM)
    return pl.pallas_call(body, name=name, in_specs=[vm], out_specs=vm, out_shape=_sds((R, C)),
                          compiler_params=pltpu.CompilerParams(vmem_limit_bytes=VMEM_LIMIT))(parts)


def _adamw_many(ws, gs, ms, vs, name):
    n = len(ws)
    vm = pl.BlockSpec(memory_space=pltpu.VMEM)

    def body(*refs):
        w_refs, g_refs, m_refs, v_refs = refs[:n], refs[n:2 * n], refs[2 * n:3 * n], refs[3 * n:4 * n]
        d_refs, nm_refs, nv_refs = refs[4 * n:5 * n], refs[5 * n:6 * n], refs[6 * n:7 * n]
        for i in range(n):
            d, nm, nv = _adamw_math(w_refs[i][...], g_refs[i][...], m_refs[i][...], v_refs[i][...])
            d_refs[i][...] = d
            nm_refs[i][...] = nm
            nv_refs[i][...] = nv

    shapes = [_sds(w.shape) for w in ws]
    outs = pl.pallas_call(
        body, name=name, in_specs=[vm] * (4 * n), out_specs=[vm] * (3 * n), out_shape=shapes * 3,
        compiler_params=pltpu.CompilerParams(vmem_limit_bytes=VMEM_LIMIT),
    )(*ws, *gs, *ms, *vs)
    return list(outs[:n]), list(outs[n:2 * n]), list(outs[2 * n:])


def _mesh_pos():
    return lax.axis_index("x"), lax.axis_index("y"), lax.axis_index("c")


def _dev_index(p):
    return 4 * p[0] + 2 * p[1] + p[2]


HBM = pl.BlockSpec(memory_space=pltpu.HBM)
SEM = pl.BlockSpec(memory_space=pltpu.SEMAPHORE)
EFFECT = pltpu.SideEffectType.DATAFLOW_SIDE_EFFECTING


def _peer_list():
    x, y, c = _mesh_pos()
    peers = [(x ^ dx, y ^ dy, c ^ dc) for dx in range(2) for dy in range(2) for dc in range(2)][1:]
    return (x, y, c), peers


def _split_copy(src_ref, land_ref, send_sems, recv_sems, i, k, peer, slot, exchange):
    return pltpu.make_async_remote_copy(
        src_ref=src_ref.at[_dev_index(peer)] if exchange else src_ref, dst_ref=land_ref.at[slot],
        send_sem=send_sems.at[7 * i + k], recv_sem=recv_sems.at[7 * i + k], device_id=peer, device_id_type=MESH)


def _comm_start(groups, name, exchange, dep=None):
    sizes = [len(g) for g in groups]
    n = sum(sizes)
    srcs = [a for g in groups for a in g]
    my_index = _dev_index(_mesh_pos())
    lands = []
    for a in srcs:
        if exchange:
            own = lax.dynamic_slice(a, (my_index, 0, 0), (1,) + a.shape[1:])
            shape = a.shape
        else:
            own = a[None]
            shape = (N_DEV,) + a.shape
        lands.append(lax.dynamic_update_slice(lax.empty(shape, a.dtype), own, (my_index, 0, 0)))

    n_dep = 0 if dep is None else 1

    def body(*refs):
        src_refs, land_refs = refs[:n], refs[n:2 * n]
        sem_refs = refs[2 * n + n_dep:2 * n + n_dep + 2 * len(sizes)]
        token_ref = refs[-1]
        me, peers = _peer_list()
        mi = _dev_index(me)
        i = 0
        for gi, sz in enumerate(sizes):
            for j in range(sz):
                for k, peer in enumerate(peers):
                    _split_copy(src_refs[i], land_refs[i], sem_refs[2 * gi], sem_refs[2 * gi + 1], j, k, peer, mi,
                                exchange).start()
                i += 1
        token_ref[...] = jnp.zeros_like(token_ref)

    sem_shapes = []
    for sz in sizes:
        sem_shapes += [pltpu.SemaphoreType.DMA((7 * sz,)), pltpu.SemaphoreType.DMA((7 * sz,))]
    thru = [pltpu.HBM(a.shape, a.dtype) for a in srcs + lands]
    n_sem = len(sem_shapes)
    outs = pl.pallas_call(
        body, name=name,
        out_shape=tuple(sem_shapes + thru + [_sds((8, LANES))]),
        in_specs=[HBM] * (2 * n) + [ANY] * n_dep,
        out_specs=tuple([SEM] * n_sem + [HBM] * (2 * n) + [pl.BlockSpec(memory_space=pltpu.VMEM)]),
        input_output_aliases={i: n_sem + i for i in range(2 * n)},
        compiler_params=pltpu.CompilerParams(has_side_effects=EFFECT),
    )(*[pltpu.with_memory_space_constraint(a, pltpu.HBM) for a in srcs + lands], *([] if dep is None else [dep]))
    sems, thru_src, thru_land, token = outs[:n_sem], outs[n_sem:n_sem + n], outs[n_sem + n:n_sem + 2 * n], outs[-1]
    result, off = [], 0
    for gi, sz in enumerate(sizes):
        result.append((sems[2 * gi], sems[2 * gi + 1], list(thru_src[off:off + sz]), list(thru_land[off:off + sz])))
        off += sz
    return result, token


def _comm_wait(group, after, name, exchange):
    send_sems, recv_sems, srcs, lands = group
    n = len(srcs)
    after = list(after) if isinstance(after, (list, tuple)) else [after]

    def body(*refs):
        src_refs, land_refs = refs[:n], refs[n:2 * n]
        ssem, rsem = refs[2 * n], refs[2 * n + 1]
        me, peers = _peer_list()
        for i in range(n):
            for k, peer in enumerate(peers):
                cp = _split_copy(src_refs[i], land_refs[i], ssem, rsem, i, k, peer, _dev_index(peer), exchange)
                cp.wait_send()
                cp.wait_recv()

    outs = pl.pallas_call(
        body, name=name,
        out_shape=tuple(pltpu.HBM(a.shape, a.dtype) for a in srcs + lands),
        in_specs=[HBM] * (2 * n) + [SEM, SEM] + [ANY] * len(after),
        out_specs=tuple([HBM] * (2 * n)),
        input_output_aliases={i: i for i in range(2 * n)},
        compiler_params=pltpu.CompilerParams(has_side_effects=EFFECT),
    )(*srcs, *lands, send_sems, recv_sems, *after)
    return list(outs[n:])


def _tie(a, token):
    return a + token[0, 0].astype(a.dtype)


def _pack(arrs, rows):
    flat = jnp.concatenate([a.reshape(-1).astype(F32) for a in arrs])
    return jnp.pad(flat, (0, rows * LANES - flat.shape[0])).reshape(rows, LANES)


def _unpack(packed, shapes):
    flat = packed.reshape(-1)
    out, off = [], 0
    for s in shapes:
        n = math.prod(s)
        out.append(flat[off:off + n].reshape(s))
        off += n
    return out


def _packed_rows(shapes):
    n = sum(math.prod(s) for s in shapes)
    unit = N_DEV * 8 * LANES
    return -(-n // unit) * unit // LANES


def kernel(x, mix_pre_g, mix_post_g, mlp_pre_g, mlp_post_g, w_in_even, s5_lam_re, s5_lam_im, s5_log_dt, s5_b_re, s5_b_im, s5_c_re, s5_c_im, s5_d, s5_w_glu, fox_b_f, w_out_even, w_in_odd, pool_w, pool_scale, sgu_ln_g, sgu_ln_b, sgu_w_s, sgu_b_s, w_out_odd, mlp_w1, mlp_w2, loss_target, m_mix_pre_g, m_mix_post_g, m_mlp_pre_g, m_mlp_post_g, m_w_in_even, m_s5_lam_re, m_s5_lam_im, m_s5_log_dt, m_s5_b_re, m_s5_b_im, m_s5_c_re, m_s5_c_im, m_s5_d, m_s5_w_glu, m_fox_b_f, m_w_out_even, m_w_in_odd, m_pool_w, m_pool_scale, m_sgu_ln_g, m_sgu_ln_b, m_sgu_w_s, m_sgu_b_s, m_w_out_odd, m_mlp_w1, m_mlp_w2, v_mix_pre_g, v_mix_post_g, v_mlp_pre_g, v_mlp_post_g, v_w_in_even, v_s5_lam_re, v_s5_lam_im, v_s5_log_dt, v_s5_b_re, v_s5_b_im, v_s5_c_re, v_s5_c_im, v_s5_d, v_s5_w_glu, v_fox_b_f, v_w_out_even, v_w_in_odd, v_pool_w, v_pool_scale, v_sgu_ln_g, v_sgu_ln_b, v_sgu_w_s, v_sgu_b_s, v_w_out_odd, v_mlp_w1, v_mlp_w2):
    weights = dict(mix_pre_g=mix_pre_g, mix_post_g=mix_post_g, mlp_pre_g=mlp_pre_g, mlp_post_g=mlp_post_g, w_in_even=w_in_even, s5_lam_re=s5_lam_re, s5_lam_im=s5_lam_im, s5_log_dt=s5_log_dt, s5_b_re=s5_b_re, s5_b_im=s5_b_im, s5_c_re=s5_c_re, s5_c_im=s5_c_im, s5_d=s5_d, s5_w_glu=s5_w_glu, fox_b_f=fox_b_f, w_out_even=w_out_even, w_in_odd=w_in_odd, pool_w=pool_w, pool_scale=pool_scale, sgu_ln_g=sgu_ln_g, sgu_ln_b=sgu_ln_b, sgu_w_s=sgu_w_s, sgu_b_s=sgu_b_s, w_out_odd=w_out_odd, mlp_w1=mlp_w1, mlp_w2=mlp_w2)
    mom_m = dict(mix_pre_g=m_mix_pre_g, mix_post_g=m_mix_post_g, mlp_pre_g=m_mlp_pre_g, mlp_post_g=m_mlp_post_g, w_in_even=m_w_in_even, s5_lam_re=m_s5_lam_re, s5_lam_im=m_s5_lam_im, s5_log_dt=m_s5_log_dt, s5_b_re=m_s5_b_re, s5_b_im=m_s5_b_im, s5_c_re=m_s5_c_re, s5_c_im=m_s5_c_im, s5_d=m_s5_d, s5_w_glu=m_s5_w_glu, fox_b_f=m_fox_b_f, w_out_even=m_w_out_even, w_in_odd=m_w_in_odd, pool_w=m_pool_w, pool_scale=m_pool_scale, sgu_ln_g=m_sgu_ln_g, sgu_ln_b=m_sgu_ln_b, sgu_w_s=m_sgu_w_s, sgu_b_s=m_sgu_b_s, w_out_odd=m_w_out_odd, mlp_w1=m_mlp_w1, mlp_w2=m_mlp_w2)
    mom_v = dict(mix_pre_g=v_mix_pre_g, mix_post_g=v_mix_post_g, mlp_pre_g=v_mlp_pre_g, mlp_post_g=v_mlp_post_g, w_in_even=v_w_in_even, s5_lam_re=v_s5_lam_re, s5_lam_im=v_s5_lam_im, s5_log_dt=v_s5_log_dt, s5_b_re=v_s5_b_re, s5_b_im=v_s5_b_im, s5_c_re=v_s5_c_re, s5_c_im=v_s5_c_im, s5_d=v_s5_d, s5_w_glu=v_s5_w_glu, fox_b_f=v_fox_b_f, w_out_even=v_w_out_even, w_in_odd=v_w_in_odd, pool_w=v_pool_w, pool_scale=v_pool_scale, sgu_ln_g=v_sgu_ln_g, sgu_ln_b=v_sgu_ln_b, sgu_w_s=v_sgu_w_s, sgu_b_s=v_sgu_b_s, w_out_odd=v_w_out_odd, mlp_w1=v_mlp_w1, mlp_w2=v_mlp_w2)
    names = list(weights)
    L = x.shape[1]
    x0 = x[0]
    target = loss_target[0]
    my_index = 4 * lax.axis_index("x") + 2 * lax.axis_index("y") + lax.axis_index("c")

    small_vec = jnp.zeros((8, LANES), F32)
    small_vec = small_vec.at[0, :64].set(pool_scale[0]).at[1, :64].set(sgu_ln_g[0]).at[2, :64].set(sgu_ln_b[0])
    ag_groups, ag_token = _comm_start(
        [[jnp.transpose(w_in_even[0]).astype(BF16), small_vec],
         [s5_w_glu[0].astype(BF16), w_out_even[0].astype(BF16)],
         [mlp_w1[0].astype(BF16), mlp_w2[0].astype(BF16)],
         [jnp.transpose(w_in_odd[0]).astype(BF16), w_out_odd[0].astype(BF16), mlp_w1[1].astype(BF16), mlp_w2[1].astype(BF16)]],
        "ag_start", exchange=False)

    lam_r = jnp.concatenate([s5_lam_re.reshape(1, S5_NS), s5_lam_im.reshape(1, S5_NS)], axis=0)
    ldt_r = jnp.repeat(s5_log_dt.reshape(32), 64).reshape(1, S5_NS)
    lam_c = jnp.transpose(lam_r)
    ldt_c = jnp.transpose(ldt_r)
    b_t = jnp.stack([jnp.tile(s5_b_re.reshape(S5_NS, 16), (1, 8)), jnp.tile(s5_b_im.reshape(S5_NS, 16), (1, 8))])
    c_t = jnp.stack([jnp.tile(s5_c_re.reshape(S5_W, 64), (1, 8)), jnp.tile(s5_c_im.reshape(S5_W, 64), (1, 8))])
    bf_pad = jnp.pad(fox_b_f, ((0, 0), (0, LANES - 8)))
    b_st = jnp.transpose(sgu_b_s[0])

    h0, rx0 = _rms_fwd(x0, _tie(mix_pre_g[0:1], ag_token), "rms0")
    tabs, bset, cset = _s5_prep(lam_r, ldt_r, lam_c, ldt_c, b_t, c_t, "s5_prep")
    ag0 = _comm_wait(ag_groups[0], tabs, "ag_wait0", exchange=False)
    winT_e = jnp.pad(ag0[0].reshape(EVEN_IN, D_MODEL), ((0, EVEN_PAD - EVEN_IN), (0, 0)))
    pool_scale_f = ag0[1][:, 0, :64].reshape(1, 512)
    ln_g_f = ag0[1][:, 1, :64].reshape(1, 512)
    ln_b_f = ag0[1][:, 2, :64].reshape(1, 512)
    z0 = _mm(h0, winT_e, name="win_even", tb=True, bm=512, bn=EVEN_PAD)
    xs = _s5_scan_fwd(z0, bset, tabs, "s5_scan")
    ag1 = _comm_wait(ag_groups[1], xs, "ag_wait1", exchange=False)
    wglu = ag1[0].reshape(S5_W, S5_W)
    wout_e = ag1[1].reshape(D_MODEL, D_MODEL)
    ylin, ya = _s5_out_fwd(xs, cset, z0, s5_d, wglu, "s5_out")
    fcum, fq = _fox_f_fwd(z0, bf_pad, "fox_f")
    frow = jnp.transpose(fcum[:, :8]).reshape(4, 2, L)
    o_att, lse = _fox_fwd(z0, fq, frow, "fox_fwd")
    mix0 = [ya, o_att]
    x1, ry0, h1, rx1, y0 = _mm(mix0, wout_e, name="wout_even", epi=_epi_post_pre, extra=(x0,),
                               vecs=(mix_post_g[0:1], mlp_pre_g[0:1]), out_dtypes=POST_PRE_DTYPES,
                               out_kinds=POST_PRE_KINDS, bm=FUSED_ROWS)
    ag2 = _comm_wait(ag_groups[2], rx1, "ag_wait2", exchange=False)
    w1 = [ag2[0], None]
    w2 = [ag2[1].reshape(4 * D_MODEL, D_MODEL), None]
    p0, a0 = _mm(h1, w1[0], name="mlp0_w1", b3=True, out_dtypes=(BF16, BF16), epi=_epi_relu2, bm=2048)
    x2, ro0, h2, rx2, o0 = _mm(a0, w2[0], name="mlp0_w2", epi=_epi_post_pre, extra=(x1,),
                               vecs=(mlp_post_g[0:1], mix_pre_g[1:2]), out_dtypes=POST_PRE_DTYPES,
                               out_kinds=POST_PRE_KINDS, bm=FUSED_ROWS, bk=4 * D_MODEL)
    ag3 = _comm_wait(ag_groups[3], rx2, "ag_wait3", exchange=False)
    winT_o = ag3[0].reshape(ODD_IN, D_MODEL)
    wout_o = ag3[1].reshape(D_MODEL, D_MODEL)
    w1[1] = ag3[2]
    w2[1] = ag3[3].reshape(4 * D_MODEL, D_MODEL)
    z1 = _mm(h2, winT_o, name="win_odd", tb=True, bn=ODD_IN)
    yc, pooled = _pool_fwd(z1, pool_w[0], pool_scale_f, "pool_fwd")
    yd = _sgu_fwd(z1, ln_g_f, ln_b_f, sgu_w_s[0], b_st, "sgu_fwd")
    mix1 = [yc, yd]
    x3, ry1, h3, rx3, y1 = _mm(mix1, wout_o, name="wout_odd", epi=_epi_post_pre, extra=(x2,),
                               vecs=(mix_post_g[1:2], mlp_pre_g[1:2]), out_dtypes=POST_PRE_DTYPES,
                               out_kinds=POST_PRE_KINDS, bm=FUSED_ROWS)
    p1, a1 = _mm(h3, w1[1], name="mlp1_w1", b3=True, out_dtypes=(BF16, BF16), epi=_epi_relu2, bm=2048)
    o1 = _mm(a1, w2[1], name="mlp1_w2", bm=2048)
    gx4, ro1, sq = _post_loss_fwd(x3, o1, mlp_post_g[1:2], target, "post3")

    g_o1, gg_mlp_post1 = _post_bwd(gx4, o1, ro1, mlp_post_g[1:2], "bpost3")
    g_p1 = _mm(g_o1, w2[1], name="b_mlp1_a", tb=True, out_dtypes=(BF16,), epi=_epi_relu2_bwd, extra=(p1,), bm=2048)
    gw2_1 = _mm(a1, g_o1, name="b_mlp1_w2", ta=True, bm=2048)
    gw1_1 = _mm(h3, g_p1, name="b_mlp1_w1", ta=True, out3=True, bn=2048)
    (ex1,), tok1 = _comm_start([[gw1_1, gw2_1.reshape(N_DEV, 512, D_MODEL)]], "ex_start1", exchange=True)
    g_x3, gg_mlp_pre1, g_y1, gg_mix_post1 = _mm(
        g_p1, w1[1], name="b_mlp1_h", tb=True, b3=True, epi=_epi_pre_post_bwd, extra=(x3, gx4, y1), cols=(rx3, ry1),
        vecs=(_tie(mlp_pre_g[1:2], tok1), mix_post_g[1:2]), out_dtypes=PRE_POST_BWD_DTYPES,
        out_kinds=PRE_POST_BWD_KINDS, bm=FUSED_ROWS, bk=4 * D_MODEL)
    g_mix1 = _mm(g_y1, wout_o, name="b_wout_odd_m", tb=True)
    gwout_o = _mm(mix1, g_y1, name="b_wout_odd_w", ta=True)
    g_xc, g_pool_w, g_pool_scale = _pool_bwd(g_mix1, pooled, pool_w[0], pool_scale_f, "pool_bwd")
    g_u1, g_v1, g_ws, g_bst, g_ln_g, g_ln_b = _sgu_bwd(g_mix1, z1, ln_g_f, ln_b_f, sgu_w_s[0], b_st, "sgu_bwd")
    g_z1 = [g_xc, g_u1, g_v1]
    gwinT_o = _mm(g_z1, h2, name="b_win_odd_w", ta=True)
    (ex2,), tok2 = _comm_start([[gwout_o.reshape(N_DEV, 128, D_MODEL), gwinT_o.reshape(N_DEV, ODD_IN // N_DEV, D_MODEL)]], "ex_start2", exchange=True)
    g_x2, gg_mix_pre1, g_o0, gg_mlp_post0 = _mm(
        g_z1, winT_o, name="b_win_odd_h", epi=_epi_pre_post_bwd, extra=(x2, g_x3, o0), cols=(rx2, ro0),
        vecs=(_tie(mix_pre_g[1:2], tok2), mlp_post_g[0:1]), out_dtypes=PRE_POST_BWD_DTYPES,
        out_kinds=PRE_POST_BWD_KINDS, bm=FUSED_ROWS)
    g_p0 = _mm(g_o0, w2[0], name="b_mlp0_a", tb=True, out_dtypes=(BF16,), epi=_epi_relu2_bwd, extra=(p0,), bm=2048)
    gw2_0 = _mm(a0, g_o0, name="b_mlp0_w2", ta=True, bm=2048)
    gw1_0 = _mm(h1, g_p0, name="b_mlp0_w1", ta=True, out3=True, bn=2048)
    (ex3,), tok3 = _comm_start([[gw1_0, gw2_0.reshape(N_DEV, 512, D_MODEL)]], "ex_start3", exchange=True)
    g_x1, gg_mlp_pre0, g_y0, gg_mix_post0 = _mm(
        g_p0, w1[0], name="b_mlp0_h", tb=True, b3=True, epi=_epi_pre_post_bwd, extra=(x1, g_x2, y0), cols=(rx1, ry0),
        vecs=(_tie(mlp_pre_g[0:1], tok3), mix_post_g[0:1]), out_dtypes=PRE_POST_BWD_DTYPES,
        out_kinds=PRE_POST_BWD_KINDS, bm=FUSED_ROWS, bk=4 * D_MODEL)
    g_mix0 = _mm(g_y0, wout_e, name="b_wout_even_m", tb=True)
    gwout_e = _mm(mix0, g_y0, name="b_wout_even_w", ta=True)
    gyl, gud, g_wglu, g_d = _s5_glu_bwd(g_mix0, ylin, z0, s5_d, wglu, "s5_glu_bwd")
    (ex4,), tok4 = _comm_start([[gwout_e.reshape(N_DEV, 128, D_MODEL), g_wglu.reshape(N_DEV, 64, S5_W)]], "ex_start4", exchange=True)
    g_u0, ga, gb_raw, gc_raw = _s5_scan_bwd(gyl, _tie(cset, tok4), xs, z0, bset, gud, tabs, "s5_scan_bwd")
    g_lam, g_ldt, g_b, g_c = _s5_param_bwd(lam_c, ldt_c, b_t, gb_raw, jnp.transpose(ga), gc_raw, "s5_param_bwd")
    dq, dk, dv, dfq, dfrow = _fox_bwd(z0, fq, frow, o_att, lse, g_mix0, "fox_bwd")
    dFk = jnp.pad(jnp.transpose(dfrow.reshape(8, L)), ((0, 0), (0, LANES - 8)))
    dfl, db_f = _fox_f_bwd(dFk, dfq, z0, bf_pad, "fox_f_bwd")
    g_z0 = [g_u0, dq, dk, dv, dfl]
    grad_x, gg_mix_pre0 = _mm(g_z0, winT_e, name="b_win_even_h", epi=_epi_pre_bwd, extra=(x0, g_x1), cols=(rx0,),
                              vecs=(mix_pre_g[0:1],), out_dtypes=(F32, F32), out_kinds=("full", "vsum"),
                              bm=FUSED_ROWS)

    small_grads = dict(
        mix_pre_g=jnp.concatenate([gg_mix_pre0, gg_mix_pre1]), mix_post_g=jnp.concatenate([gg_mix_post0, gg_mix_post1]),
        mlp_pre_g=jnp.concatenate([gg_mlp_pre0, gg_mlp_pre1]), mlp_post_g=jnp.concatenate([gg_mlp_post0, gg_mlp_post1]),
        s5_lam_re=g_lam[:, 0], s5_lam_im=g_lam[:, 1], s5_log_dt=g_ldt,
        s5_b_re=g_b[0, :, :16], s5_b_im=g_b[1, :, :16], s5_c_re=g_c[0, :, :64], s5_c_im=g_c[1, :, :64],
        s5_d=g_d, fox_b_f=db_f[:, :8], pool_w=g_pool_w, sgu_w_s=g_ws, sgu_b_s=jnp.transpose(g_bst),
        pool_scale=g_pool_scale, sgu_ln_g=g_ln_g, sgu_ln_b=g_ln_b)
    small_names = list(small_grads)
    full_shapes = [(512,) if nm in ("pool_scale", "sgu_ln_g", "sgu_ln_b") else weights[nm].shape for nm in small_names]
    full_shapes.append((1, 1))
    rows = _packed_rows(full_shapes)
    packed = _pack([small_grads[nm] for nm in small_names] + [sq], rows).reshape(N_DEV, rows // N_DEV, LANES)
    (exs,), tok_s = _comm_start([[packed]], "exs_start", exchange=True)
    gwinT_e = _mm(g_z0, h0, name="b_win_even_w", ta=True, bk=512, out_dtypes=(BF16,), dep=tok_s)
    (recv_small,) = _comm_wait(exs, gwinT_e, "exs_wait", exchange=True)
    piece = _sum_pieces(recv_small, "sum_small")
    (ags,), tok_a = _comm_start([[piece]], "ags_start", exchange=False)

    gwinT_e_pieces = gwinT_e[:EVEN_IN].reshape(N_DEV, EVEN_IN // N_DEV, D_MODEL)
    (ex5,), tok5 = _comm_start([[gwinT_e_pieces]], "ex_start5", exchange=True, dep=tok_a)
    r_w1_1, r_w2_1 = _comm_wait(ex1, tok5, "ex_wait1", exchange=True)
    r_wout_o, r_win_o = _comm_wait(ex2, tok5, "ex_wait2", exchange=True)
    r_w1_0, r_w2_0 = _comm_wait(ex3, tok5, "ex_wait3", exchange=True)
    r_wout_e, r_wglu = _comm_wait(ex4, tok5, "ex_wait4", exchange=True)

    res = {}
    for nm, parts in (("mlp_w1", (r_w1_0, r_w1_1)), ("mlp_w2", (r_w2_0, r_w2_1))):
        first = _sum_adamw(parts[0], weights[nm], mom_m[nm], mom_v[nm], "adamw_%s_0" % nm, layer=0)
        res[nm] = tuple(_sum_adamw(parts[1], weights[nm], mom_m[nm], mom_v[nm], "adamw_%s_1" % nm, layer=1, prev=first))
    big_parts = dict(s5_w_glu=r_wglu, w_out_even=r_wout_e, w_out_odd=r_wout_o)
    for nm, parts in big_parts.items():
        res[nm] = tuple(_sum_adamw(parts, weights[nm], mom_m[nm], mom_v[nm], "adamw_" + nm))
    done = [res[nm][1] for nm in ("mlp_w1", "mlp_w2", "s5_w_glu", "w_out_even", "w_out_odd")]

    (small_all,) = _comm_wait(ags, done, "ags_wait", exchange=False)
    small_full = _unpack(small_all.reshape(rows, LANES), full_shapes)
    loss = 0.5 * small_full.pop()[0, 0] / D_MODEL
    small_g = []
    for nm, g in zip(small_names, small_full):
        if nm in ("pool_scale", "sgu_ln_g", "sgu_ln_b"):
            g = lax.dynamic_slice(g, (my_index * 64,), (64,)).reshape(1, 64)
        small_g.append(g)
    sd, sm, sv = _adamw_many([weights[nm] for nm in small_names], small_g, [mom_m[nm] for nm in small_names],
                             [mom_v[nm] for nm in small_names], "adamw_small")
    for nm, g_, d_, m_, v_ in zip(small_names, small_g, sd, sm, sv):
        res[nm] = (g_, d_, m_, v_)
    done.append(sd[0])

    for nm, parts in (("w_in_odd", r_win_o), ("w_in_even", None)):
        if parts is None:
            (parts,) = _comm_wait(ex5, done, "ex_wait5", exchange=True)
        outs = _sum_adamw(parts, jnp.transpose(weights[nm], (0, 2, 1)), jnp.transpose(mom_m[nm], (0, 2, 1)),
                          jnp.transpose(mom_v[nm], (0, 2, 1)), "adamw_" + nm)
        res[nm] = tuple(jnp.transpose(o, (0, 2, 1)) for o in outs)
        done.append(res[nm][1])

    grads = [res[nm][0].reshape(weights[nm].shape) for nm in names]
    deltas = [res[nm][1].reshape(weights[nm].shape) for nm in names]
    new_m = [res[nm][2].reshape(weights[nm].shape) for nm in names]
    new_v = [res[nm][3].reshape(weights[nm].shape) for nm in names]
    return (loss, grad_x[None], *grads, *deltas, *new_m, *new_v)
```

```python
import functools
import math

import jax
import jax.numpy as jnp
from jax import lax
from jax.experimental import pallas as pl
from jax.experimental.pallas import tpu as pltpu

F32 = jnp.float32
BF16 = jnp.bfloat16
MESH = pl.DeviceIdType.MESH
ANY = pl.BlockSpec(memory_space=pl.ANY)

N_DEV = 8
D_MODEL = 1024
EPS = 1e-6
NORM_ROWS = 512
FUSED_ROWS = 512
S5_W = 512
S5_NS = 2048
SCAN_GROUPS = 4
SCAN_CHUNK = 1024
FOX_W = 512
EVEN_IN = 2056
EVEN_PAD = 2176
ODD_IN = 1536
LANES = 128
PIECE = 4 * D_MODEL // N_DEV
VMEM_LIMIT = 56 * 1024 * 1024

ADAM_LR = 0.001
ADAM_B1 = 0.9
ADAM_B2 = 0.999
ADAM_EPS = 1e-08
ADAM_WD = 0.01
ADAM_STEP = 10

NT = (((1,), (1,)), ((), ()))
TN = (((0,), (0,)), ((), ()))
NN = (((1,), (0,)), ((), ()))


def _cp(*sem):
    return pltpu.CompilerParams(dimension_semantics=sem, vmem_limit_bytes=VMEM_LIMIT)


def _sds(shape, dtype=F32):
    return jax.ShapeDtypeStruct(tuple(shape), dtype)


def _gelu(x):
    t = jnp.tanh(0.7978845608028654 * (x + 0.044715 * x * x * x))
    return 0.5 * x * (1.0 + t)


def _gelu_grad(x):
    t = jnp.tanh(0.7978845608028654 * (x + 0.044715 * x * x * x))
    du = 0.7978845608028654 * (1.0 + 3.0 * 0.044715 * x * x)
    return 0.5 * (1.0 + t) + 0.5 * x * (1.0 - t * t) * du


def _sigmoid(x):
    return 1.0 / (1.0 + jnp.exp(-x))


def _dot(a, b, dn=NN):
    return lax.dot_general(a, b, dn, preferred_element_type=F32)


def _mm(a, b, *, name, ta=False, tb=False, b3=False, out3=False, out_dtypes=(F32,), epi=None, extra=(),
        cols=(), vecs=(), out_kinds=None, bm=1024, bn=1024, bk=1024, dep=None):
    a_list = list(a) if isinstance(a, (list, tuple)) else [a]
    widths = [p.shape[1] for p in a_list]
    offs = [sum(widths[:i]) for i in range(len(widths))]
    na = len(a_list)
    M = sum(widths) if ta else a_list[0].shape[0]
    K = a_list[0].shape[0] if ta else sum(widths)
    if na > 1:
        assert not b3 and not tb
        bm, bk = (M, bk) if ta else (bm, K)
    pw = b.shape[2] if b3 else PIECE
    if b3:
        N = b.shape[1] if tb else b.shape[0] * pw
        assert (b.shape[0] * pw if tb else b.shape[1]) == K
    else:
        N = b.shape[0] if tb else b.shape[1]
    bm, bn, bk = min(bm, M), min(bn, N), min(bk, K)
    assert M % bm == 0 and N % bn == 0 and K % bk == 0, (name, M, N, K, bm, bn, bk)
    assert not (b3 or out3) or ((bk if tb else bn) % pw == 0 and bn % PIECE == 0)
    nk = K // bk
    n_extra = len(extra) + len(cols) + len(vecs)
    n_out = len(out_dtypes)
    out_kinds = tuple(out_kinds) if out_kinds is not None else ("full",) * n_out
    dn = (((0 if ta else 1,), (1 if tb else 0,)), ((), ()))

    use_acc = nk > 1

    def body(*refs):
        a_refs, b_ref = refs[:na], refs[na]
        a_ref = a_refs[0]
        e_refs = refs[na + 1:na + 1 + n_extra]
        first_out = na + 1 + n_extra + (0 if dep is None else 1)
        o_refs = refs[first_out:first_out + n_out]
        acc_ref = refs[-1] if use_acc else o_refs[0]
        i, k = pl.program_id(0), pl.program_id(2)

        def dot(a_v, b_v):
            return lax.dot_general(a_v.astype(BF16), b_v.astype(BF16), dn, preferred_element_type=F32)

        everything = slice(None)
        if na > 1 and ta:
            terms = [(pl.ds(off, w), everything, r, b_ref) for r, off, w in zip(a_refs, offs, widths)]
        elif na > 1:
            terms = [(everything, everything, r, b_ref.at[pl.ds(off, w), :]) for r, off, w in zip(a_refs, offs, widths)]
        elif not b3:
            terms = [(everything, everything, a_ref, b_ref)]
        elif tb:
            terms = [(everything, everything,
                      a_ref.at[pl.ds(t * pw, pw), :] if ta else a_ref.at[:, pl.ds(t * pw, pw)], b_ref.at[t])
                     for t in range(bk // pw)]
        else:
            terms = [(everything, pl.ds(t * pw, pw), a_ref, b_ref.at[t]) for t in range(bn // pw)]

        def finish(acc):
            outs = (acc,) if epi is None else epi(acc, *[e[...] for e in e_refs])
            for o_ref, o, kind in zip(o_refs, outs, out_kinds):
                if kind == "vsum":
                    @pl.when(i == 0)
                    def _(o_ref=o_ref, o=o):
                        o_ref[...] = o

                    @pl.when(i > 0)
                    def _(o_ref=o_ref, o=o):
                        o_ref[...] += o
                elif out3:
                    for t in range(bn // PIECE):
                        o_ref[t] = o[:, t * PIECE:(t + 1) * PIECE].astype(o_ref.dtype)
                else:
                    o_ref[...] = o.astype(o_ref.dtype)

        if nk == 1:
            bands = {}
            for rows, cols, a_r, b_r in terms:
                key = (getattr(rows, "start", None), getattr(cols, "start", None))
                val = dot(a_r[...], b_r[...])
                bands[key] = val if key not in bands else bands[key] + val
            vals = list(bands.values())
            if len(vals) == 1:
                finish(vals[0])
            else:
                finish(jnp.concatenate(vals, axis=0 if (na > 1 and ta) else 1))
            return

        @pl.when(k == 0)
        def _():
            acc_ref[...] = jnp.zeros_like(acc_ref)

        for rows, cols, a_r, b_r in terms:
            acc_ref[rows, cols] += dot(a_r[...], b_r[...])

        @pl.when(k == nk - 1)
        def _():
            finish(acc_ref[...])

    if na > 1:
        a_specs = [pl.BlockSpec((bk, w), lambda i, j, k: (k, 0)) if ta else pl.BlockSpec((bm, w), lambda i, j, k: (i, 0))
                   for w in widths]
    else:
        a_specs = [pl.BlockSpec((bk, bm), lambda i, j, k: (k, i)) if ta else
                   pl.BlockSpec((bm, bk), lambda i, j, k: (i, k))]
    if b3:
        if tb:
            b_spec = pl.BlockSpec((bk // pw, bn, pw), lambda i, j, k: (k, j, 0))
        else:
            b_spec = pl.BlockSpec((bn // pw, bk, pw), lambda i, j, k: (j, k, 0))
    else:
        b_spec = pl.BlockSpec((bn, bk), lambda i, j, k: (j, k)) if tb else pl.BlockSpec((bk, bn), lambda i, j, k: (k, j))
    e_specs = ([pl.BlockSpec((bm, bn), lambda i, j, k: (i, j)) for _ in extra]
               + [pl.BlockSpec((bm, 1), lambda i, j, k: (i, 0)) for _ in cols]
               + [pl.BlockSpec((1, bn), lambda i, j, k: (0, j)) for _ in vecs])
    if out3:
        o_specs = [pl.BlockSpec((bn // PIECE, bm, PIECE), lambda i, j, k: (j, i, 0)) for _ in out_dtypes]
        o_shapes = [_sds((N // PIECE, M, PIECE), dt) for dt in out_dtypes]
    else:
        spec_of = {"full": pl.BlockSpec((bm, bn), lambda i, j, k: (i, j)),
                   "col": pl.BlockSpec((bm, 1), lambda i, j, k: (i, 0)),
                   "vsum": pl.BlockSpec((1, bn), lambda i, j, k: (0, j))}
        shape_of = {"full": (M, N), "col": (M, 1), "vsum": (1, N)}
        o_specs = [spec_of[kind] for kind in out_kinds]
        o_shapes = [_sds(shape_of[kind], dt) for kind, dt in zip(out_kinds, out_dtypes)]
    assert "col" not in out_kinds or bn == N
    outs = pl.pallas_call(
        body, name=name, grid=(M // bm, N // bn, nk),
        in_specs=a_specs + [b_spec] + e_specs + ([] if dep is None else [ANY]),
        out_specs=o_specs, out_shape=o_shapes,
        scratch_shapes=[pltpu.VMEM((bm, bn), F32)] if use_acc else [],
        compiler_params=_cp("arbitrary" if "vsum" in out_kinds else "parallel", "parallel", "arbitrary"),
    )(*a_list, b, *extra, *cols, *vecs, *([] if dep is None else [dep]))
    return outs[0] if n_out == 1 else outs


def _epi_relu2(acc):
    r = jnp.maximum(acc, 0.0)
    return acc, r * r


def _epi_relu2_bwd(acc, p):
    return (acc * (2.0 * jnp.maximum(p.astype(F32), 0.0)),)


def _row_spec(rb, w=D_MODEL):
    return pl.BlockSpec((rb, w), lambda i: (i, 0))


def _vec_spec(w=D_MODEL):
    return pl.BlockSpec((1, w), lambda i: (0, 0))


def _rstd(v):
    return lax.rsqrt(jnp.mean(v * v, axis=-1, keepdims=True) + EPS)


def _rms_fwd(x, g, name):
    L = x.shape[0]
    rb = min(NORM_ROWS, L)

    def body(x_ref, g_ref, h_ref, r_ref):
        xv = x_ref[...]
        r = _rstd(xv)
        h_ref[...] = (xv * r * g_ref[...]).astype(BF16)
        r_ref[...] = r

    return pl.pallas_call(
        body, name=name, grid=(L // rb,),
        in_specs=[_row_spec(rb), _vec_spec()],
        out_specs=[_row_spec(rb), _row_spec(rb, 1)],
        out_shape=[_sds((L, D_MODEL), BF16), _sds((L, 1))],
        compiler_params=_cp("parallel"),
    )(x, g)


def _post_loss_fwd(x_in, y, g_post, target, name):
    L = x_in.shape[0]
    rb = min(NORM_ROWS, L)

    def body(x_ref, y_ref, gp_ref, t_ref, gx_ref, ry_ref, loss_ref):
        i = pl.program_id(0)
        yv = y_ref[...]
        ry = _rstd(yv)
        diff = x_ref[...] + yv * ry * gp_ref[...] - t_ref[...]
        gx_ref[...] = diff * (1.0 / D_MODEL)
        ry_ref[...] = ry

        @pl.when(i == 0)
        def _():
            loss_ref[...] = jnp.zeros_like(loss_ref)

        loss_ref[...] += jnp.sum(diff * diff, keepdims=True)

    return pl.pallas_call(
        body, name=name, grid=(L // rb,),
        in_specs=[_row_spec(rb), _row_spec(rb), _vec_spec(), _row_spec(rb)],
        out_specs=[_row_spec(rb), _row_spec(rb, 1), pl.BlockSpec((1, 1), lambda i: (0, 0))],
        out_shape=[_sds((L, D_MODEL)), _sds((L, 1)), _sds((1, 1))],
        compiler_params=_cp("arbitrary"),
    )(x_in, y, g_post, target)


def _rms_bwd_rows(dy, xv, r, g):
    n = xv * r
    dyg = dy * g
    return r * (dyg - n * jnp.mean(dyg * n, axis=-1, keepdims=True)), n


POST_PRE_DTYPES = (F32, F32, BF16, F32, F32)
POST_PRE_KINDS = ("full", "col", "full", "col", "full")
PRE_POST_BWD_DTYPES = (F32, F32, BF16, F32)
PRE_POST_BWD_KINDS = ("full", "vsum", "full", "vsum")


def _epi_post_pre(y, x_in, g_post, g_pre):
    ry = _rstd(y)
    xo = x_in + y * ry * g_post
    rx = _rstd(xo)
    return xo, ry, xo * rx * g_pre, rx, y


def _epi_pre_post_bwd(gh, x, g_out, y_prev, rx, ry_prev, g_pre, g_post_prev):
    gx, n = _rms_bwd_rows(gh, x, rx, g_pre)
    gi = g_out + gx
    gy, ny = _rms_bwd_rows(gi, y_prev, ry_prev, g_post_prev)
    return gi, jnp.sum(gh * n, axis=0, keepdims=True), gy, jnp.sum(gi * ny, axis=0, keepdims=True)


def _epi_pre_bwd(gh, x, g_out, rx, g_pre):
    gx, n = _rms_bwd_rows(gh, x, rx, g_pre)
    return g_out + gx, jnp.sum(gh * n, axis=0, keepdims=True)


def _post_bwd(g_out, y, ry, g_post, name):
    L = y.shape[0]
    rb = min(NORM_ROWS, L)

    def body(go_ref, y_ref, ry_ref, gp_ref, gy_ref, gg_ref):
        i = pl.program_id(0)
        go = go_ref[...]
        gy, n = _rms_bwd_rows(go, y_ref[...], ry_ref[...], gp_ref[...])
        gy_ref[...] = gy.astype(BF16)

        @pl.when(i == 0)
        def _():
            gg_ref[...] = jnp.zeros_like(gg_ref)

        gg_ref[...] += jnp.sum(go * n, axis=0, keepdims=True)

    return pl.pallas_call(
        body, name=name, grid=(L // rb,),
        in_specs=[_row_spec(rb), _row_spec(rb), _row_spec(rb, 1), _vec_spec()],
        out_specs=[_row_spec(rb), _vec_spec()],
        out_shape=[_sds((L, D_MODEL), BF16), _sds((1, D_MODEL))],
        compiler_params=_cp("arbitrary"),
    )(g_out, y, ry, g_post)


def _cmul(ar, ai, br, bi):
    return ar * br - ai * bi, ar * bi + ai * br


def _zoh_cols(lr, li, ldt):
    dt = jnp.exp(ldt)
    mag = jnp.exp(lr * dt)
    ar = mag * jnp.cos(li * dt)
    ai = mag * jnp.sin(li * dt)
    den = lr * lr + li * li
    nr = ar - 1.0
    qr = (nr * lr + ai * li) / den
    qi = (ai * lr - nr * li) / den
    return dt, ar, ai, qr, qi, den


def _b_mask():
    r = lax.broadcasted_iota(jnp.int32, (S5_NS, LANES), 0)
    c = lax.broadcasted_iota(jnp.int32, (S5_NS, LANES), 1)
    return ((r >> 6) & 7) == (c >> 4)


def _c_mask():
    r = lax.broadcasted_iota(jnp.int32, (S5_W, 512), 0)
    c = lax.broadcasted_iota(jnp.int32, (S5_W, 512), 1)
    return ((r >> 4) & 7) == (c >> 6)


def _s5_prep(lam_r, ldt_r, lam_c, ldt_c, b_t, c_t, name):
    def body(lam_r_ref, ldt_r_ref, lam_c_ref, ldt_c_ref, b_ref, c_ref, tab_ref, bset_ref, cset_ref):
        lr, li = lam_r_ref[0:1, :], lam_r_ref[1:2, :]
        dt = jnp.exp(ldt_r_ref[...])
        mag = jnp.exp(lr * dt)
        p1r, p1i = mag * jnp.cos(li * dt), mag * jnp.sin(li * dt)
        p2r, p2i = _cmul(p1r, p1i, p1r, p1i)
        p3r, p3i = _cmul(p2r, p2i, p1r, p1i)
        p4r, p4i = _cmul(p2r, p2i, p2r, p2i)
        p5r, p5i = _cmul(p4r, p4i, p1r, p1i)
        p6r, p6i = _cmul(p4r, p4i, p2r, p2i)
        p7r, p7i = _cmul(p4r, p4i, p3r, p3i)
        p8r, p8i = _cmul(p4r, p4i, p4r, p4i)
        pw_r = [p1r, p2r, p3r, p4r, p5r, p6r, p7r, p8r]
        pw_i = [p1i, p2i, p3i, p4i, p5i, p6i, p7i, p8i]
        row = lax.broadcasted_iota(jnp.int32, (8, S5_NS), 0)
        zero = jnp.zeros((8, S5_NS), F32)

        def bc(v):
            return jnp.broadcast_to(v, (8, S5_NS))

        for d in range(2):
            sgn = 1.0 if d == 0 else -1.0
            for t, s in enumerate((1, 2, 4)):
                live = (row >= s) if d == 0 else (row <= 7 - s)
                tab_ref[d, 2 * t] = jnp.where(live, bc(pw_r[s - 1]), zero)
                tab_ref[d, 2 * t + 1] = jnp.where(live, bc(sgn * pw_i[s - 1]), zero)
            cr, ci = zero, zero
            for i in range(8):
                e = i if d == 0 else 7 - i
                cr = jnp.where(row == i, bc(pw_r[e]), cr)
                ci = jnp.where(row == i, bc(sgn * pw_i[e]), ci)
            tab_ref[d, 6] = cr
            tab_ref[d, 7] = ci

        _, _, _, qr, qi, _ = _zoh_cols(lam_c_ref[:, 0:1], lam_c_ref[:, 1:2], ldt_c_ref[...])
        bm = _b_mask()
        br, bi = b_ref[0], b_ref[1]
        bset_ref[0] = jnp.where(bm, qr * br - qi * bi, 0.0).astype(BF16)
        bset_ref[1] = jnp.where(bm, qr * bi + qi * br, 0.0).astype(BF16)
        cm = _c_mask()
        cset_ref[0] = jnp.where(cm, c_ref[0], 0.0).astype(BF16)
        cset_ref[1] = jnp.where(cm, c_ref[1], 0.0).astype(BF16)

    vm = pl.BlockSpec(memory_space=pltpu.VMEM)
    return pl.pallas_call(
        body, name=name, in_specs=[vm] * 6, out_specs=[vm] * 3,
        out_shape=[_sds((2, 8, 8, S5_NS)), _sds((2, S5_NS, LANES), BF16), _sds((2, S5_W, 512), BF16)],
        compiler_params=pltpu.CompilerParams(vmem_limit_bytes=VMEM_LIMIT),
    )(lam_r, ldt_r, lam_c, ldt_c, b_t, c_t)


SCAN_W = SCAN_GROUPS * LANES


def _scan_chunk(src_ref, dst_ref, tab_ref, carry_ref, nb, reverse, xs_ref=None, acc_ref=None):
    row = lax.broadcasted_iota(jnp.int32, (8, LANES), 0)

    def step(i, carry):
        b = (nb - 1 - i) if reverse else i
        off = pl.multiple_of(b * 8, 8)
        out = []
        for g in range(SCAN_GROUPS):
            lanes = pl.ds(g * LANES, LANES)
            cr, ci = carry[2 * g], carry[2 * g + 1]
            yr = src_ref[0, pl.ds(off, 8), lanes]
            yi = src_ref[1, pl.ds(off, 8), lanes]
            for t, s in enumerate((1, 2, 4)):
                sh = (8 - s) if reverse else s
                sr = pltpu.roll(yr, sh, 0)
                si = pltpu.roll(yi, sh, 0)
                mr, mi = tab_ref[2 * t, :, lanes], tab_ref[2 * t + 1, :, lanes]
                yr, yi = yr + mr * sr - mi * si, yi + mr * si + mi * sr
            pr, pi = tab_ref[6, :, lanes], tab_ref[7, :, lanes]
            yr, yi = yr + pr * cr - pi * ci, yi + pr * ci + pi * cr
            dst_ref[0, pl.ds(off, 8), lanes] = yr
            dst_ref[1, pl.ds(off, 8), lanes] = yi
            if xs_ref is not None:
                nr = jnp.where(row == 7, cr, pltpu.roll(yr, 7, 0))
                ni = jnp.where(row == 7, ci, pltpu.roll(yi, 7, 0))
                xr = xs_ref[0, pl.ds(off, 8), lanes]
                xi = xs_ref[1, pl.ds(off, 8), lanes]
                acc_ref[0, :, lanes] += xr * nr + xi * ni
                acc_ref[1, :, lanes] += xr * ni - xi * nr
            last = 0 if reverse else 7
            out += [jnp.broadcast_to(yr[last:last + 1, :], (8, LANES)),
                    jnp.broadcast_to(yi[last:last + 1, :], (8, LANES))]
        return tuple(out)

    init = []
    for g in range(SCAN_GROUPS):
        init += [carry_ref[0, :, pl.ds(g * LANES, LANES)], carry_ref[1, :, pl.ds(g * LANES, LANES)]]
    fin = lax.fori_loop(0, nb, step, tuple(init))
    for g in range(SCAN_GROUPS):
        carry_ref[0, :, pl.ds(g * LANES, LANES)] = fin[2 * g]
        carry_ref[1, :, pl.ds(g * LANES, LANES)] = fin[2 * g + 1]


def _s5_scan_fwd(z, bset, tabs, name):
    L = z.shape[0]
    tl = min(SCAN_CHUNK, L)
    nc = L // tl

    def body(u_ref, b_ref, tab_ref, x_ref, carry_ref):
        @pl.when(pl.program_id(1) == 0)
        def _():
            carry_ref[...] = jnp.zeros_like(carry_ref)

        u = u_ref[...].astype(BF16)
        x_ref[0] = _dot(u, b_ref[0], NT)
        x_ref[1] = _dot(u, b_ref[1], NT)
        _scan_chunk(x_ref, x_ref, tab_ref, carry_ref, tl // 8, False)

    return pl.pallas_call(
        body, name=name, grid=(S5_NS // SCAN_W, nc),
        in_specs=[pl.BlockSpec((tl, LANES), lambda j, c: (c, j)),
                  pl.BlockSpec((2, SCAN_W, LANES), lambda j, c: (0, j, 0)),
                  pl.BlockSpec((None, 8, 8, SCAN_W), lambda j, c: (0, 0, 0, j))],
        out_specs=pl.BlockSpec((2, tl, SCAN_W), lambda j, c: (0, c, j)),
        out_shape=_sds((2, L, S5_NS)),
        scratch_shapes=[pltpu.VMEM((2, 8, SCAN_W), F32)],
        compiler_params=_cp("parallel", "arbitrary"),
    )(z, bset, tabs)


def _s5_scan_bwd(gyl, cset, xs, z, bset, gud, tabs, name):
    L = z.shape[0]
    tl = min(SCAN_CHUNK, L)
    nc = L // tl

    def body(g_ref, c_ref, xs_ref, u_ref, b_ref, gud_ref, tab_ref, gu_ref, ga_ref, gb_ref, gc_ref,
             gx_ref, carry_ref, acc_ref):
        c = pl.program_id(1)

        @pl.when(c == 0)
        def _():
            carry_ref[...] = jnp.zeros_like(carry_ref)
            acc_ref[...] = jnp.zeros_like(acc_ref)
            gb_ref[...] = jnp.zeros_like(gb_ref)
            gc_ref[...] = jnp.zeros_like(gc_ref)

        gy = g_ref[...].astype(BF16)
        gx_ref[0] = _dot(gy, c_ref[0])
        gx_ref[1] = -_dot(gy, c_ref[1])
        gc_ref[0] += _dot(gy, xs_ref[0].astype(BF16), TN)
        gc_ref[1] -= _dot(gy, xs_ref[1].astype(BF16), TN)
        _scan_chunk(gx_ref, gx_ref, tab_ref, carry_ref, tl // 8, True, xs_ref, acc_ref)
        gr = gx_ref[0].astype(BF16)
        gi = gx_ref[1].astype(BF16)
        gu_ref[...] = gud_ref[...] + _dot(gr, b_ref[0]) + _dot(gi, b_ref[1])
        u = u_ref[...].astype(BF16)
        gb_ref[0] += _dot(gr, u, TN)
        gb_ref[1] += _dot(gi, u, TN)

        @pl.when(c == nc - 1)
        def _():
            ga_ref[0:1, :] = jnp.sum(acc_ref[0], axis=0, keepdims=True)
            ga_ref[1:2, :] = jnp.sum(acc_ref[1], axis=0, keepdims=True)

    rev = lambda j, c: (nc - 1 - c, j)
    col = pl.BlockSpec((tl, LANES), rev)
    return pl.pallas_call(
        body, name=name, grid=(S5_NS // SCAN_W, nc),
        in_specs=[col, pl.BlockSpec((2, LANES, SCAN_W), lambda j, c: (0, j, 0)),
                  pl.BlockSpec((2, tl, SCAN_W), lambda j, c: (0, nc - 1 - c, j)), col,
                  pl.BlockSpec((2, SCAN_W, LANES), lambda j, c: (0, j, 0)), col,
                  pl.BlockSpec((None, 8, 8, SCAN_W), lambda j, c: (1, 0, 0, j))],
        out_specs=[col, pl.BlockSpec((2, SCAN_W), lambda j, c: (0, j)),
                   pl.BlockSpec((2, SCAN_W, LANES), lambda j, c: (0, j, 0)),
                   pl.BlockSpec((2, LANES, SCAN_W), lambda j, c: (0, j, 0))],
        out_shape=[_sds((L, S5_W)), _sds((2, S5_NS)), _sds((2, S5_NS, LANES)), _sds((2, S5_W, 512))],
        scratch_shapes=[pltpu.VMEM((2, tl, SCAN_W), F32), pltpu.VMEM((2, 8, SCAN_W), F32),
                        pltpu.VMEM((2, 8, SCAN_W), F32)],
        compiler_params=_cp("parallel", "arbitrary"),
    )(gyl, cset, xs, z, bset, gud, tabs)


def _s5_out_fwd(xs, cset, z, dvec, wglu, name):
    L = z.shape[0]
    bl = min(256, L)

    def body(x_ref, c_ref, u_ref, d_ref, w_ref, ylin_ref, ya_ref):
        cols = []
        for j in range(4):
            xr = x_ref[0, :, 512 * j:512 * (j + 1)].astype(BF16)
            xi = x_ref[1, :, 512 * j:512 * (j + 1)].astype(BF16)
            cr = c_ref[0, LANES * j:LANES * (j + 1), :]
            ci = c_ref[1, LANES * j:LANES * (j + 1), :]
            cols.append(_dot(xr, cr, NT) - _dot(xi, ci, NT))
        ylin = jnp.concatenate(cols, axis=1) + d_ref[...] * u_ref[...]
        yg = _gelu(ylin)
        t = _dot(yg.astype(BF16), w_ref[...])
        ylin_ref[...] = ylin
        ya_ref[...] = (yg * _sigmoid(t)).astype(BF16)

    return pl.pallas_call(
        body, name=name, grid=(L // bl,),
        in_specs=[pl.BlockSpec((2, bl, S5_NS), lambda i: (0, i, 0)),
                  pl.BlockSpec((2, S5_W, 512), lambda i: (0, 0, 0)),
                  pl.BlockSpec((bl, S5_W), lambda i: (i, 0)),
                  pl.BlockSpec((1, S5_W), lambda i: (0, 0)),
                  pl.BlockSpec((S5_W, S5_W), lambda i: (0, 0))],
        out_specs=[pl.BlockSpec((bl, S5_W), lambda i: (i, 0))] * 2,
        out_shape=[_sds((L, S5_W)), _sds((L, S5_W), BF16)],
        compiler_params=_cp("parallel"),
    )(xs, cset, z, dvec, wglu)


def _s5_glu_bwd(g_m, ylin, z, dvec, wglu, name):
    L = z.shape[0]
    bl = min(256, L)

    def body(g_ref, ylin_ref, u_ref, d_ref, w_ref, gyl_ref, gud_ref, gw_ref, gd_ref):
        i = pl.program_id(0)
        ylin = ylin_ref[...]
        yg = _gelu(ylin)
        ygb = yg.astype(BF16)
        sg = _sigmoid(_dot(ygb, w_ref[...]))
        gya = g_ref[...]
        gt = gya * yg * sg * (1.0 - sg)
        gtb = gt.astype(BF16)
        gyg = gya * sg + _dot(gtb, w_ref[...], NT)
        gyl = gyg * _gelu_grad(ylin)
        gyl_ref[...] = gyl
        gud_ref[...] = gyl * d_ref[...]

        @pl.when(i == 0)
        def _():
            gw_ref[...] = jnp.zeros_like(gw_ref)
            gd_ref[...] = jnp.zeros_like(gd_ref)

        gw_ref[...] += _dot(ygb, gtb, TN)
        gd_ref[...] += jnp.sum(gyl * u_ref[...], axis=0, keepdims=True)

    blk = pl.BlockSpec((bl, S5_W), lambda i: (i, 0))
    return pl.pallas_call(
        body, name=name, grid=(L // bl,),
        in_specs=[blk, blk, blk, pl.BlockSpec((1, S5_W), lambda i: (0, 0)),
                  pl.BlockSpec((S5_W, S5_W), lambda i: (0, 0))],
        out_specs=[blk, blk, pl.BlockSpec((S5_W, S5_W), lambda i: (0, 0)), pl.BlockSpec((1, S5_W), lambda i: (0, 0))],
        out_shape=[_sds((L, S5_W)), _sds((L, S5_W)), _sds((S5_W, S5_W)), _sds((1, S5_W))],
        compiler_params=_cp("arbitrary"),
    )(g_m, ylin, z, dvec, wglu)


def _s5_param_bwd(lam_c, ldt_c, b_t, gb, ga_c, gc, name):
    def body(lam_ref, ldt_ref, b_ref, gb_ref, ga_ref, gc_ref, glam_ref, gldt_ref, gbo_ref, gco_ref):
        lr, li = lam_ref[:, 0:1], lam_ref[:, 1:2]
        dt, ar, ai, qr, qi, den = _zoh_cols(lr, li, ldt_ref[...])
        bm = _b_mask()
        gbr = jnp.where(bm, gb_ref[0], 0.0)
        gbi = jnp.where(bm, gb_ref[1], 0.0)
        br, bi = b_ref[0], b_ref[1]
        obr = gbr * qr + gbi * qi
        obi = gbi * qr - gbr * qi
        gqr = jnp.sum(gbr * br + gbi * bi, axis=1, keepdims=True)
        gqi = jnp.sum(gbi * br - gbr * bi, axis=1, keepdims=True)
        for s in (64, 32, 16):
            obr = obr + pltpu.roll(obr, s, 1)
            obi = obi + pltpu.roll(obi, s, 1)
        gbo_ref[0] = obr
        gbo_ref[1] = obi
        gar = ga_ref[:, 0:1] + (gqr * lr - gqi * li) / den
        gai = ga_ref[:, 1:2] + (gqr * li + gqi * lr) / den
        qlr = (qr * lr + qi * li) / den
        qli = (qi * lr - qr * li) / den
        glr = -(gqr * qlr + gqi * qli)
        gli = -(gqi * qlr - gqr * qli)
        glr = glr + dt * (gar * ar + gai * ai)
        gli = gli + dt * (gai * ar - gar * ai)
        wr, wi = _cmul(lr, li, ar, ai)
        gldt = (gar * wr + gai * wi) * dt
        glam_ref[:, 0:1] = glr
        glam_ref[:, 1:2] = gli
        r = lax.broadcasted_iota(jnp.int32, (S5_NS, 32), 0)
        c = lax.broadcasted_iota(jnp.int32, (S5_NS, 32), 1)
        gldt_ref[...] = jnp.sum(jnp.where((r >> 6) == c, gldt, 0.0), axis=0, keepdims=True)
        cm = _c_mask()
        for k in range(2):
            oc = jnp.where(cm, gc_ref[k], 0.0)
            for s in (256, 128, 64):
                oc = oc + pltpu.roll(oc, s, 1)
            gco_ref[k] = oc[:, 0:LANES]

    vm = pl.BlockSpec(memory_space=pltpu.VMEM)
    return pl.pallas_call(
        body, name=name, in_specs=[vm] * 6, out_specs=[vm] * 4,
        out_shape=[_sds((S5_NS, 2)), _sds((1, 32)), _sds((2, S5_NS, LANES)), _sds((2, S5_W, LANES))],
        compiler_params=pltpu.CompilerParams(vmem_limit_bytes=VMEM_LIMIT),
    )(lam_c, ldt_c, b_t, gb, ga_c, gc)


FL_BLK = EVEN_PAD // LANES - 1
Q_BLK, K_BLK, V_BLK = 4, 8, 12
NEG = -1e30


def _log_sigmoid(v):
    return jnp.minimum(v, 0.0) - jnp.log(1.0 + jnp.exp(-jnp.abs(v)))


def _fox_f_fwd(z, bf, name):
    L = z.shape[0]
    tl = min(256, L)

    def body(fl_ref, b_ref, f_ref, fq_ref, carry_ref):
        i = pl.program_id(0)

        @pl.when(i == 0)
        def _():
            carry_ref[...] = jnp.zeros_like(carry_ref)

        lf = _log_sigmoid(fl_ref[...] + b_ref[...])
        r = lax.broadcasted_iota(jnp.int32, (tl, tl), 0)
        c = lax.broadcasted_iota(jnp.int32, (tl, tl), 1)
        tri = (r >= c).astype(F32)
        cs = lax.dot_general(tri, lf, NN, precision=lax.Precision.HIGHEST, preferred_element_type=F32) + carry_ref[...]
        f_ref[...] = cs
        carry_ref[...] = cs[tl - 1:tl, :]
        expand = (lax.broadcasted_iota(jnp.int32, (LANES, FOX_W), 0)
                  == (lax.broadcasted_iota(jnp.int32, (LANES, FOX_W), 1) >> 6)).astype(F32)
        fq_ref[...] = lax.dot_general(cs, expand, NN, precision=lax.Precision.HIGHEST, preferred_element_type=F32)

    return pl.pallas_call(
        body, name=name, grid=(L // tl,),
        in_specs=[pl.BlockSpec((tl, LANES), lambda i: (i, FL_BLK)), pl.BlockSpec((1, LANES), lambda i: (0, 0))],
        out_specs=[pl.BlockSpec((tl, LANES), lambda i: (i, 0)), pl.BlockSpec((tl, FOX_W), lambda i: (i, 0))],
        out_shape=[_sds((L, LANES)), _sds((L, FOX_W))],
        scratch_shapes=[pltpu.VMEM((1, LANES), F32)],
        compiler_params=_cp("arbitrary"),
    )(z, bf)


def _fox_f_bwd(dFk, dfq, z, bf, name):
    L = z.shape[0]
    tl = min(256, L)
    nb = L // tl

    def body(dfk_ref, dfq_ref, fl_ref, b_ref, dfl_ref, db_ref, carry_ref):
        i = pl.program_id(0)

        @pl.when(i == 0)
        def _():
            carry_ref[...] = jnp.zeros_like(carry_ref)
            db_ref[...] = jnp.zeros_like(db_ref)

        sel = (lax.broadcasted_iota(jnp.int32, (FOX_W, LANES), 0)
               == 64 * lax.broadcasted_iota(jnp.int32, (FOX_W, LANES), 1)).astype(F32)
        dfq_h = lax.dot_general(dfq_ref[...], sel, NN, precision=lax.Precision.HIGHEST, preferred_element_type=F32)
        r = lax.broadcasted_iota(jnp.int32, (tl, tl), 0)
        c = lax.broadcasted_iota(jnp.int32, (tl, tl), 1)
        tri = (r <= c).astype(F32)
        cs = lax.dot_general(tri, dfk_ref[...] + dfq_h, NN, precision=lax.Precision.HIGHEST,
                             preferred_element_type=F32) + carry_ref[...]
        carry_ref[...] = cs[0:1, :]
        dfl = cs * _sigmoid(-(fl_ref[...] + b_ref[...]))
        dfl_ref[...] = dfl
        db_ref[...] += jnp.sum(dfl, axis=0, keepdims=True)

    return pl.pallas_call(
        body, name=name, grid=(nb,),
        in_specs=[pl.BlockSpec((tl, LANES), lambda i: (nb - 1 - i, 0)),
                  pl.BlockSpec((tl, FOX_W), lambda i: (nb - 1 - i, 0)),
                  pl.BlockSpec((tl, LANES), lambda i: (nb - 1 - i, FL_BLK)),
                  pl.BlockSpec((1, LANES), lambda i: (0, 0))],
        out_specs=[pl.BlockSpec((tl, LANES), lambda i: (nb - 1 - i, 0)), pl.BlockSpec((1, LANES), lambda i: (0, 0))],
        out_shape=[_sds((L, LANES)), _sds((1, LANES))],
        scratch_shapes=[pltpu.VMEM((1, LANES), F32)],
        compiler_params=_cp("arbitrary"),
    )(dFk, dfq, z, bf)


def _head_mask(hh):
    lane = lax.broadcasted_iota(jnp.int32, (1, LANES), 1)
    return (lane >> 6) == hh


FOX_T = 512


def _fox_head(x, hh):
    return jnp.where(_head_mask(hh), x, 0.0).astype(BF16)


def _fox_scores(qh, k, fq_ref, fr_ref, hh, causal):
    s = _dot(qh, k, NT) + (fq_ref[:, 64 * hh:64 * hh + 1] - fr_ref[hh:hh + 1, :])
    return s if causal is None else jnp.where(causal, s, NEG)


def _causal(T):
    return lax.broadcasted_iota(jnp.int32, (T, T), 1) <= lax.broadcasted_iota(jnp.int32, (T, T), 0)


def _fox_fwd(z, fq, frow, name):
    L = z.shape[0]
    T = min(FOX_T, L)
    nq = L // T

    def body(qt_ref, kt_ref, q_ref, k_ref, v_ref, fq_ref, fr_ref, o_ref, lse_ref, m_ref, l_ref, acc_ref):
        t = pl.program_id(1)
        qi, ki = qt_ref[t], kt_ref[t]

        @pl.when(ki == 0)
        def _():
            m_ref[...] = jnp.full_like(m_ref, NEG)
            l_ref[...] = jnp.zeros_like(l_ref)
            acc_ref[...] = jnp.zeros_like(acc_ref)

        def step(diagonal):
            q = q_ref[...] * 0.125
            k = k_ref[...].astype(BF16)
            v = v_ref[...].astype(BF16)
            causal = _causal(T) if diagonal else None
            s = jnp.concatenate([_fox_scores(_fox_head(q, hh), k, fq_ref, fr_ref, hh, causal) for hh in range(2)],
                                axis=0)
            m_old = m_ref[...]
            m_new = jnp.maximum(m_old, jnp.max(s, axis=1, keepdims=True))
            alpha = jnp.exp(m_old - m_new)
            p = jnp.exp(s - m_new)
            l_ref[...] = alpha * l_ref[...] + jnp.sum(p, axis=1, keepdims=True)
            m_ref[...] = m_new
            acc_ref[...] = alpha * acc_ref[...] + _dot(p.astype(BF16), v)

        @pl.when(ki < qi)
        def _():
            step(False)

        @pl.when(ki == qi)
        def _():
            step(True)
            h0 = _head_mask(0)
            l = l_ref[...]
            o_h = acc_ref[...] / l
            lse_h = m_ref[...] + jnp.log(l)
            o_ref[...] = jnp.where(h0, o_h[:T], o_h[T:])
            lse_ref[...] = jnp.where(h0, lse_h[:T], lse_h[T:])

    pairs = [(qi, ki) for qi in range(nq) for ki in range(qi + 1)]
    qt = jnp.asarray([p[0] for p in pairs], jnp.int32)
    kt = jnp.asarray([p[1] for p in pairs], jnp.int32)

    def qspec(base):
        return pl.BlockSpec((T, LANES), lambda j, t, qt, kt: (qt[t], base + j))

    def kspec(base):
        return pl.BlockSpec((T, LANES), lambda j, t, qt, kt: (kt[t], base + j))

    return pl.pallas_call(
        body, name=name,
        grid_spec=pltpu.PrefetchScalarGridSpec(
            num_scalar_prefetch=2, grid=(4, len(pairs)),
            in_specs=[qspec(Q_BLK), kspec(K_BLK), kspec(V_BLK), qspec(0),
                      pl.BlockSpec((None, 2, T), lambda j, t, qt, kt: (j, 0, kt[t]))],
            out_specs=[qspec(0), qspec(0)],
            scratch_shapes=[pltpu.VMEM((2 * T, 1), F32), pltpu.VMEM((2 * T, 1), F32),
                            pltpu.VMEM((2 * T, LANES), F32)]),
        out_shape=[_sds((L, FOX_W)), _sds((L, FOX_W))],
        compiler_params=_cp("parallel", "arbitrary"),
    )(qt, kt, z, z, z, fq, frow)


def _fox_bwd(z, fq, frow, o, lse, g_m, name):
    L = z.shape[0]
    T = min(FOX_T, L)
    nq = L // T

    pairs = [(qi, ki) for ki in range(nq) for qi in range(ki, nq)]
    qt = jnp.asarray([p[0] for p in pairs], jnp.int32)
    kt = jnp.asarray([p[1] for p in pairs], jnp.int32)

    def body(qt_ref, kt_ref, q_ref, k_ref, v_ref, fq_ref, fr_ref, o_ref, lse_ref, do_ref,
             dq_ref, dk_ref, dv_ref, dfq_ref, dfk_ref, dk_acc, dv_acc, df_acc):
        t = pl.program_id(1)
        qi, ki = qt_ref[t], kt_ref[t]

        @pl.when(t == 0)
        def _():
            dq_ref[...] = jnp.zeros_like(dq_ref)
            dfq_ref[...] = jnp.zeros_like(dfq_ref)

        @pl.when(qi == ki)
        def _():
            dk_acc[...] = jnp.zeros_like(dk_acc)
            dv_acc[...] = jnp.zeros_like(dv_acc)
            df_acc[...] = jnp.zeros_like(df_acc)

        def step(diagonal):
            q = q_ref[...] * 0.125
            qb = q.astype(BF16)
            k = k_ref[...].astype(BF16)
            v = v_ref[...].astype(BF16)
            do = do_ref[...]
            dob = do.astype(BF16)
            do_o = dob.astype(F32) * o_ref[...]
            causal = _causal(T) if diagonal else None
            dvs, dks, dqs, rss = [], [], [], []
            for hh in range(2):
                s = _fox_scores(_fox_head(q, hh), k, fq_ref, fr_ref, hh, causal)
                p = jnp.exp(s - lse_ref[:, 64 * hh:64 * hh + 1])
                dp = _dot(_fox_head(do, hh), v, NT)
                delta = jnp.sum(jnp.where(_head_mask(hh), do_o, 0.0), axis=1, keepdims=True)
                ds = p * (dp - delta)
                dsb = ds.astype(BF16)
                dvs.append(_dot(p.astype(BF16), dob, TN))
                dks.append(_dot(dsb, qb, TN))
                dqs.append(_dot(dsb, k))
                rss.append(jnp.sum(ds, axis=1, keepdims=True))
                df_acc[hh:hh + 1, :] -= jnp.sum(ds, axis=0, keepdims=True)
            h0 = _head_mask(0)
            dv_acc[...] += jnp.where(h0, dvs[0], dvs[1])
            dk_acc[...] += jnp.where(h0, dks[0], dks[1])
            rows = pl.ds(pl.multiple_of(qi * T, T), T)
            dq_ref[rows, :] += jnp.where(h0, dqs[0], dqs[1])
            dfq_ref[rows, :] += jnp.where(h0, rss[0], rss[1])

        @pl.when(qi > ki)
        def _():
            step(False)

        @pl.when(qi == ki)
        def _():
            step(True)

        @pl.when(qi == nq - 1)
        def _():
            dk_ref[...] = dk_acc[...]
            dv_ref[...] = dv_acc[...]
            dfk_ref[...] = df_acc[...]

        @pl.when(t == len(pairs) - 1)
        def _():
            dq_ref[...] = dq_ref[...] * 0.125

    def qside(base):
        return pl.BlockSpec((T, LANES), lambda j, t, qt, kt: (qt[t], base + j))

    def kside(base):
        return pl.BlockSpec((T, LANES), lambda j, t, qt, kt: (kt[t], base + j))

    pair = pl.BlockSpec((L, LANES), lambda j, t, qt, kt: (0, j))
    frow_spec = pl.BlockSpec((None, 2, T), lambda j, t, qt, kt: (j, 0, kt[t]))
    return pl.pallas_call(
        body, name=name,
        grid_spec=pltpu.PrefetchScalarGridSpec(
            num_scalar_prefetch=2, grid=(4, len(pairs)),
            in_specs=[qside(Q_BLK), kside(K_BLK), kside(V_BLK), qside(0), frow_spec, qside(0), qside(0), qside(4)],
            out_specs=[pair, kside(0), kside(0), pair, frow_spec],
            scratch_shapes=[pltpu.VMEM((T, LANES), F32), pltpu.VMEM((T, LANES), F32), pltpu.VMEM((2, T), F32)]),
        out_shape=[_sds((L, FOX_W)), _sds((L, FOX_W)), _sds((L, FOX_W)), _sds((L, FOX_W)), _sds((4, 2, L))],
        compiler_params=_cp("parallel", "arbitrary"),
    )(qt, kt, z, z, z, fq, frow, o, lse, g_m)


def _shift_rows(v, s, down, row):
    n = v.shape[0]
    if down:
        return jnp.where(row >= s, pltpu.roll(v, s, 0), 0.0)
    return jnp.where(row < n - s, pltpu.roll(v, n - s, 0), 0.0)


def _window_sum(v, g, down, row):
    out = jnp.zeros_like(v)
    s = v
    for k in range(4):
        s = s + _shift_rows(s, 1 << k, down, row)
        out = jnp.where(g == k, s, out)
    return out


def _pool_inv_cnt(g, row):
    w = jnp.left_shift(2, g).astype(F32)
    return 1.0 / jnp.minimum(row.astype(F32) + 1.0, w)


def _pool_fwd(z, pool_w, scale, name):
    L = z.shape[0]

    def body(x_ref, w_ref, s_ref, y_ref, p_ref):
        g = pl.program_id(0)
        row = lax.broadcasted_iota(jnp.int32, (L, LANES), 0)
        x = x_ref[...]
        pooled = (_window_sum(x, g, True, row) * _pool_inv_cnt(g, row) - x).astype(BF16)
        p_ref[...] = pooled
        y_ref[...] = (_dot(pooled, w_ref[...].astype(BF16)) * s_ref[...]).astype(BF16)

    col = pl.BlockSpec((L, LANES), lambda g: (0, g))
    return pl.pallas_call(
        body, name=name, grid=(4,),
        in_specs=[col, pl.BlockSpec((None, LANES, LANES), lambda g: (g, 0, 0)), pl.BlockSpec((1, LANES), lambda g: (0, g))],
        out_specs=[col, col],
        out_shape=[_sds((L, 512), BF16), _sds((L, 512), BF16)],
        compiler_params=_cp("parallel"),
    )(z, pool_w, scale)


def _pool_bwd(g_m, pooled, pool_w, scale, name):
    L = g_m.shape[0]

    def body(g_ref, p_ref, w_ref, s_ref, gx_ref, gw_ref, gs_ref):
        g = pl.program_id(0)
        row = lax.broadcasted_iota(jnp.int32, (L, LANES), 0)
        gy = g_ref[...]
        pooled = p_ref[...]
        wb = w_ref[...].astype(BF16)
        lin = _dot(pooled, wb)
        gs_ref[...] = jnp.sum(gy * lin, axis=0, keepdims=True)
        glin = (gy * s_ref[...]).astype(BF16)
        gw_ref[...] = _dot(pooled, glin, TN)
        gp = _dot(glin, wb, NT)
        gx_ref[...] = _window_sum(gp * _pool_inv_cnt(g, row), g, False, row) - gp

    col = pl.BlockSpec((L, LANES), lambda g: (0, g))
    wspec = pl.BlockSpec((None, LANES, LANES), lambda g: (g, 0, 0))
    vec = pl.BlockSpec((1, LANES), lambda g: (0, g))
    return pl.pallas_call(
        body, name=name, grid=(4,),
        in_specs=[col, col, wspec, vec],
        out_specs=[col, wspec, vec],
        out_shape=[_sds((L, 512)), _sds((4, LANES, LANES)), _sds((1, 512))],
        compiler_params=_cp("parallel"),
    )(g_m, pooled, pool_w, scale)


SGU_CHUNKS = 4


def _sgu_ln(v, gam, bet):
    gv = _gelu(v)
    mu = jnp.mean(gv, axis=-1, keepdims=True)
    xc = gv - mu
    rs = lax.rsqrt(jnp.mean(xc * xc, axis=-1, keepdims=True) + EPS)
    xh = xc * rs
    return xh, rs, xh * gam + bet


def _tril_ws(w_ref, g):
    r = lax.broadcasted_iota(jnp.int32, (LANES, LANES), 0)
    c = lax.broadcasted_iota(jnp.int32, (LANES, LANES), 1)
    return jnp.where(r >= c, w_ref[g], 0.0).astype(BF16)


def _sgu_fwd(z, ln_g, ln_b, w_s, b_st, name):
    L = z.shape[0]
    rb = min(SGU_CHUNKS * LANES, L)

    def body(u_ref, v_ref, g_ref, b_ref, w_ref, bs_ref, y_ref):
        _, _, vln = _sgu_ln(v_ref[...], g_ref[...], b_ref[...])
        gu = _gelu(u_ref[...])
        vb = vln.astype(BF16)
        for g in range(4):
            ws = _tril_ws(w_ref, g)
            for n in range(rb // LANES):
                rows = slice(n * LANES, (n + 1) * LANES)
                cols = slice(g * LANES, (g + 1) * LANES)
                mixed = _dot(ws, vb[rows, cols]) + bs_ref[:, g:g + 1]
                y_ref[rows, cols] = (gu[rows, cols] * mixed).astype(BF16)

    vm = lambda shape: pl.BlockSpec(shape, lambda i: tuple(0 for _ in shape))
    return pl.pallas_call(
        body, name=name, grid=(L // rb,),
        in_specs=[pl.BlockSpec((rb, 512), lambda i: (i, 1)), pl.BlockSpec((rb, 512), lambda i: (i, 2)),
                  vm((1, 512)), vm((1, 512)), vm((4, LANES, LANES)), vm((LANES, 4))],
        out_specs=pl.BlockSpec((rb, 512), lambda i: (i, 0)),
        out_shape=_sds((L, 512), BF16),
        compiler_params=_cp("parallel"),
    )(z, z, ln_g, ln_b, w_s, b_st)


def _sgu_bwd(g_m, z, ln_g, ln_b, w_s, b_st, name):
    L = z.shape[0]
    rb = min(SGU_CHUNKS * LANES, L)

    def body(gy_ref, u_ref, v_ref, g_ref, b_ref, w_ref, bs_ref, gu_ref, gv_ref, gw_ref, gbs_ref, gg_ref, gb_ref):
        i = pl.program_id(0)

        @pl.when(i == 0)
        def _():
            gw_ref[...] = jnp.zeros_like(gw_ref)
            gbs_ref[...] = jnp.zeros_like(gbs_ref)
            gg_ref[...] = jnp.zeros_like(gg_ref)
            gb_ref[...] = jnp.zeros_like(gb_ref)

        v = v_ref[...]
        u = u_ref[...]
        gy = gy_ref[...]
        xh, rs, vln = _sgu_ln(v, g_ref[...], b_ref[...])
        gel_u = _gelu(u)
        gmix = gy * gel_u
        vb = vln.astype(BF16)
        gmb = gmix.astype(BF16)
        r = lax.broadcasted_iota(jnp.int32, (LANES, LANES), 0)
        c = lax.broadcasted_iota(jnp.int32, (LANES, LANES), 1)
        gvln_cols = []
        for g in range(4):
            ws = _tril_ws(w_ref, g)
            cols = slice(g * LANES, (g + 1) * LANES)
            gw = jnp.zeros((LANES, LANES), F32)
            gbs = jnp.zeros((LANES, 1), F32)
            parts = []
            for n in range(rb // LANES):
                rows = slice(n * LANES, (n + 1) * LANES)
                mixed = _dot(ws, vb[rows, cols]) + bs_ref[:, g:g + 1]
                gu_ref[rows, cols] = gy[rows, cols] * mixed * _gelu_grad(u[rows, cols])
                parts.append(_dot(ws, gmb[rows, cols], TN))
                gw = gw + _dot(gmb[rows, cols], vb[rows, cols], NT)
                gbs = gbs + jnp.sum(gmix[rows, cols], axis=1, keepdims=True)
            gvln_cols.append(jnp.concatenate(parts, axis=0))
            gw_ref[g] += jnp.where(r >= c, gw, 0.0)
            gbs_ref[:, g:g + 1] += gbs
        gvln = jnp.concatenate(gvln_cols, axis=1)
        gg_ref[...] += jnp.sum(gvln * xh, axis=0, keepdims=True)
        gb_ref[...] += jnp.sum(gvln, axis=0, keepdims=True)
        gxh = gvln * g_ref[...]
        ggv = rs * (gxh - jnp.mean(gxh, axis=-1, keepdims=True) - xh * jnp.mean(gxh * xh, axis=-1, keepdims=True))
        gv_ref[...] = ggv * _gelu_grad(v)

    vm = lambda shape: pl.BlockSpec(shape, lambda i: tuple(0 for _ in shape))
    blk = pl.BlockSpec((rb, 512), lambda i: (i, 0))
    return pl.pallas_call(
        body, name=name, grid=(L // rb,),
        in_specs=[pl.BlockSpec((rb, 512), lambda i: (i, 1)), pl.BlockSpec((rb, 512), lambda i: (i, 1)),
                  pl.BlockSpec((rb, 512), lambda i: (i, 2)),
                  vm((1, 512)), vm((1, 512)), vm((4, LANES, LANES)), vm((LANES, 4))],
        out_specs=[blk, blk, vm((4, LANES, LANES)), vm((LANES, 4)), vm((1, 512)), vm((1, 512))],
        out_shape=[_sds((L, 512)), _sds((L, 512)), _sds((4, LANES, LANES)), _sds((LANES, 4)),
                   _sds((1, 512)), _sds((1, 512))],
        compiler_params=_cp("arbitrary"),
    )(g_m, z, z, ln_g, ln_b, w_s, b_st)


def _adamw_math(w, g, m, v):
    nm = ADAM_B1 * m + (1.0 - ADAM_B1) * g
    nv = ADAM_B2 * v + (1.0 - ADAM_B2) * (g * g)
    m_hat = nm / (1.0 - ADAM_B1 ** ADAM_STEP)
    v_hat = nv / (1.0 - ADAM_B2 ** ADAM_STEP)
    delta = -ADAM_LR * (m_hat / (jnp.sqrt(v_hat) + ADAM_EPS) + ADAM_WD * w)
    return delta, nm, nv


def _sum_adamw(parts, w, m, v, name, layer=0, prev=None):
    n_layers, R, C = w.shape
    rb = 128 if R % 128 == 0 else R

    def body(p_ref, w_ref, m_ref, v_ref, *rest):
        g_ref, d_ref, nm_ref, nv_ref = rest[-4:]
        g = p_ref[0].astype(F32)
        for s in range(1, N_DEV):
            g = g + p_ref[s].astype(F32)
        d, nm, nv = _adamw_math(w_ref[...], g, m_ref[...], v_ref[...])
        g_ref[...] = g
        d_ref[...] = d
        nm_ref[...] = nm
        nv_ref[...] = nv

    blk = pl.BlockSpec((None, rb, C), lambda i: (layer, i, 0))
    prev = [] if prev is None else list(prev)
    return pl.pallas_call(
        body, name=name, grid=(R // rb,),
        in_specs=[pl.BlockSpec((N_DEV, rb, C), lambda i: (0, i, 0)), blk, blk, blk] + [ANY] * len(prev),
        out_specs=[blk] * 4, out_shape=[_sds((n_layers, R, C))] * 4,
        input_output_aliases={4 + k: k for k in range(len(prev))},
        compiler_params=_cp("parallel"),
    )(parts, w, m, v, *prev)


def _sum_pieces(parts, name):
    _, R, C = parts.shape

    def body(p_ref, g_ref):
        g = p_ref[0].astype(F32)
        for s in range(1, N_DEV):
            g = g + p_ref[s].astype(F32)
        g_ref[...] = g

    vm = pl.BlockSpec(memory_space=pltpu.VMEM)
    return pl.pallas_call(body, name=name, in_specs=[vm], out_specs=vm, out_shape=_sds((R, C)),
                          compiler_params=pltpu.CompilerParams(vmem_limit_bytes=VMEM_LIMIT))(parts)


def _adamw_many(ws, gs, ms, vs, name):
    n = len(ws)
    vm = pl.BlockSpec(memory_space=pltpu.VMEM)

    def body(*refs):
        w_refs, g_refs, m_refs, v_refs = refs[:n], refs[n:2 * n], refs[2 * n:3 * n], refs[3 * n:4 * n]
        d_refs, nm_refs, nv_refs = refs[4 * n:5 * n], refs[5 * n:6 * n], refs[6 * n:7 * n]
        for i in range(n):
            d, nm, nv = _adamw_math(w_refs[i][...], g_refs[i][...], m_refs[i][...], v_refs[i][...])
            d_refs[i][...] = d
            nm_refs[i][...] = nm
            nv_refs[i][...] = nv

    shapes = [_sds(w.shape) for w in ws]
    outs = pl.pallas_call(
        body, name=name, in_specs=[vm] * (4 * n), out_specs=[vm] * (3 * n), out_shape=shapes * 3,
        compiler_params=pltpu.CompilerParams(vmem_limit_bytes=VMEM_LIMIT),
    )(*ws, *gs, *ms, *vs)
    return list(outs[:n]), list(outs[n:2 * n]), list(outs[2 * n:])


def _mesh_pos():
    return lax.axis_index("x"), lax.axis_index("y"), lax.axis_index("c")


def _dev_index(p):
    return 4 * p[0] + 2 * p[1] + p[2]


HBM = pl.BlockSpec(memory_space=pltpu.HBM)
SEM = pl.BlockSpec(memory_space=pltpu.SEMAPHORE)
EFFECT = pltpu.SideEffectType.DATAFLOW_SIDE_EFFECTING


def _peer_list():
    x, y, c = _mesh_pos()
    peers = [(x ^ dx, y ^ dy, c ^ dc) for dx in range(2) for dy in range(2) for dc in range(2)][1:]
    return (x, y, c), peers


def _split_copy(src_ref, land_ref, send_sems, recv_sems, i, k, peer, slot, exchange):
    return pltpu.make_async_remote_copy(
        src_ref=src_ref.at[_dev_index(peer)] if exchange else src_ref, dst_ref=land_ref.at[slot],
        send_sem=send_sems.at[7 * i + k], recv_sem=recv_sems.at[7 * i + k], device_id=peer, device_id_type=MESH)


def _comm_start(groups, name, exchange, dep=None):
    sizes = [len(g) for g in groups]
    n = sum(sizes)
    srcs = [a for g in groups for a in g]
    my_index = _dev_index(_mesh_pos())
    lands = []
    for a in srcs:
        if exchange:
            own = lax.dynamic_slice(a, (my_index, 0, 0), (1,) + a.shape[1:])
            shape = a.shape
        else:
            own = a[None]
            shape = (N_DEV,) + a.shape
        lands.append(lax.dynamic_update_slice(lax.empty(shape, a.dtype), own, (my_index, 0, 0)))

    n_dep = 0 if dep is None else 1

    def body(*refs):
        src_refs, land_refs = refs[:n], refs[n:2 * n]
        sem_refs = refs[2 * n + n_dep:2 * n + n_dep + 2 * len(sizes)]
        token_ref = refs[-1]
        me, peers = _peer_list()
        mi = _dev_index(me)
        i = 0
        for gi, sz in enumerate(sizes):
            for j in range(sz):
                for k, peer in enumerate(peers):
                    _split_copy(src_refs[i], land_refs[i], sem_refs[2 * gi], sem_refs[2 * gi + 1], j, k, peer, mi,
                                exchange).start()
                i += 1
        token_ref[...] = jnp.zeros_like(token_ref)

    sem_shapes = []
    for sz in sizes:
        sem_shapes += [pltpu.SemaphoreType.DMA((7 * sz,)), pltpu.SemaphoreType.DMA((7 * sz,))]
    thru = [pltpu.HBM(a.shape, a.dtype) for a in srcs + lands]
    n_sem = len(sem_shapes)
    outs = pl.pallas_call(
        body, name=name,
        out_shape=tuple(sem_shapes + thru + [_sds((8, LANES))]),
        in_specs=[HBM] * (2 * n) + [ANY] * n_dep,
        out_specs=tuple([SEM] * n_sem + [HBM] * (2 * n) + [pl.BlockSpec(memory_space=pltpu.VMEM)]),
        input_output_aliases={i: n_sem + i for i in range(2 * n)},
        compiler_params=pltpu.CompilerParams(has_side_effects=EFFECT),
    )(*[pltpu.with_memory_space_constraint(a, pltpu.HBM) for a in srcs + lands], *([] if dep is None else [dep]))
    sems, thru_src, thru_land, token = outs[:n_sem], outs[n_sem:n_sem + n], outs[n_sem + n:n_sem + 2 * n], outs[-1]
    result, off = [], 0
    for gi, sz in enumerate(sizes):
        result.append((sems[2 * gi], sems[2 * gi + 1], list(thru_src[off:off + sz]), list(thru_land[off:off + sz])))
        off += sz
    return result, token


def _comm_wait(group, after, name, exchange):
    send_sems, recv_sems, srcs, lands = group
    n = len(srcs)
    after = list(after) if isinstance(after, (list, tuple)) else [after]

    def body(*refs):
        src_refs, land_refs = refs[:n], refs[n:2 * n]
        ssem, rsem = refs[2 * n], refs[2 * n + 1]
        me, peers = _peer_list()
        for i in range(n):
            for k, peer in enumerate(peers):
                cp = _split_copy(src_refs[i], land_refs[i], ssem, rsem, i, k, peer, _dev_index(peer), exchange)
                cp.wait_send()
                cp.wait_recv()

    outs = pl.pallas_call(
        body, name=name,
        out_shape=tuple(pltpu.HBM(a.shape, a.dtype) for a in srcs + lands),
        in_specs=[HBM] * (2 * n) + [SEM, SEM] + [ANY] * len(after),
        out_specs=tuple([HBM] * (2 * n)),
        input_output_aliases={i: i for i in range(2 * n)},
        compiler_params=pltpu.CompilerParams(has_side_effects=EFFECT),
    )(*srcs, *lands, send_sems, recv_sems, *after)
    return list(outs[n:])


def _tie(a, token):
    return a + token[0, 0].astype(a.dtype)


def _pack(arrs, rows):
    flat = jnp.concatenate([a.reshape(-1).astype(F32) for a in arrs])
    return jnp.pad(flat, (0, rows * LANES - flat.shape[0])).reshape(rows, LANES)


def _unpack(packed, shapes):
    flat = packed.reshape(-1)
    out, off = [], 0
    for s in shapes:
        n = math.prod(s)
        out.append(flat[off:off + n].reshape(s))
        off += n
    return out


def _packed_rows(shapes):
    n = sum(math.prod(s) for s in shapes)
    unit = N_DEV * 8 * LANES
    return -(-n // unit) * unit // LANES


def kernel(x, mix_pre_g, mix_post_g, mlp_pre_g, mlp_post_g, w_in_even, s5_lam_re, s5_lam_im, s5_log_dt, s5_b_re, s5_b_im, s5_c_re, s5_c_im, s5_d, s5_w_glu, fox_b_f, w_out_even, w_in_odd, pool_w, pool_scale, sgu_ln_g, sgu_ln_b, sgu_w_s, sgu_b_s, w_out_odd, mlp_w1, mlp_w2, loss_target, m_mix_pre_g, m_mix_post_g, m_mlp_pre_g, m_mlp_post_g, m_w_in_even, m_s5_lam_re, m_s5_lam_im, m_s5_log_dt, m_s5_b_re, m_s5_b_im, m_s5_c_re, m_s5_c_im, m_s5_d, m_s5_w_glu, m_fox_b_f, m_w_out_even, m_w_in_odd, m_pool_w, m_pool_scale, m_sgu_ln_g, m_sgu_ln_b, m_sgu_w_s, m_sgu_b_s, m_w_out_odd, m_mlp_w1, m_mlp_w2, v_mix_pre_g, v_mix_post_g, v_mlp_pre_g, v_mlp_post_g, v_w_in_even, v_s5_lam_re, v_s5_lam_im, v_s5_log_dt, v_s5_b_re, v_s5_b_im, v_s5_c_re, v_s5_c_im, v_s5_d, v_s5_w_glu, v_fox_b_f, v_w_out_even, v_w_in_odd, v_pool_w, v_pool_scale, v_sgu_ln_g, v_sgu_ln_b, v_sgu_w_s, v_sgu_b_s, v_w_out_odd, v_mlp_w1, v_mlp_w2):
    weights = dict(mix_pre_g=mix_pre_g, mix_post_g=mix_post_g, mlp_pre_g=mlp_pre_g, mlp_post_g=mlp_post_g, w_in_even=w_in_even, s5_lam_re=s5_lam_re, s5_lam_im=s5_lam_im, s5_log_dt=s5_log_dt, s5_b_re=s5_b_re, s5_b_im=s5_b_im, s5_c_re=s5_c_re, s5_c_im=s5_c_im, s5_d=s5_d, s5_w_glu=s5_w_glu, fox_b_f=fox_b_f, w_out_even=w_out_even, w_in_odd=w_in_odd, pool_w=pool_w, pool_scale=pool_scale, sgu_ln_g=sgu_ln_g, sgu_ln_b=sgu_ln_b, sgu_w_s=sgu_w_s, sgu_b_s=sgu_b_s, w_out_odd=w_out_odd, mlp_w1=mlp_w1, mlp_w2=mlp_w2)
    mom_m = dict(mix_pre_g=m_mix_pre_g, mix_post_g=m_mix_post_g, mlp_pre_g=m_mlp_pre_g, mlp_post_g=m_mlp_post_g, w_in_even=m_w_in_even, s5_lam_re=m_s5_lam_re, s5_lam_im=m_s5_lam_im, s5_log_dt=m_s5_log_dt, s5_b_re=m_s5_b_re, s5_b_im=m_s5_b_im, s5_c_re=m_s5_c_re, s5_c_im=m_s5_c_im, s5_d=m_s5_d, s5_w_glu=m_s5_w_glu, fox_b_f=m_fox_b_f, w_out_even=m_w_out_even, w_in_odd=m_w_in_odd, pool_w=m_pool_w, pool_scale=m_pool_scale, sgu_ln_g=m_sgu_ln_g, sgu_ln_b=m_sgu_ln_b, sgu_w_s=m_sgu_w_s, sgu_b_s=m_sgu_b_s, w_out_odd=m_w_out_odd, mlp_w1=m_mlp_w1, mlp_w2=m_mlp_w2)
    mom_v = dict(mix_pre_g=v_mix_pre_g, mix_post_g=v_mix_post_g, mlp_pre_g=v_mlp_pre_g, mlp_post_g=v_mlp_post_g, w_in_even=v_w_in_even, s5_lam_re=v_s5_lam_re, s5_lam_im=v_s5_lam_im, s5_log_dt=v_s5_log_dt, s5_b_re=v_s5_b_re, s5_b_im=v_s5_b_im, s5_c_re=v_s5_c_re, s5_c_im=v_s5_c_im, s5_d=v_s5_d, s5_w_glu=v_s5_w_glu, fox_b_f=v_fox_b_f, w_out_even=v_w_out_even, w_in_odd=v_w_in_odd, pool_w=v_pool_w, pool_scale=v_pool_scale, sgu_ln_g=v_sgu_ln_g, sgu_ln_b=v_sgu_ln_b, sgu_w_s=v_sgu_w_s, sgu_b_s=v_sgu_b_s, w_out_odd=v_w_out_odd, mlp_w1=v_mlp_w1, mlp_w2=v_mlp_w2)
    names = list(weights)
    L = x.shape[1]
    x0 = x[0]
    target = loss_target[0]
    my_index = 4 * lax.axis_index("x") + 2 * lax.axis_index("y") + lax.axis_index("c")

    small_vec = jnp.zeros((8, LANES), F32)
    small_vec = small_vec.at[0, :64].set(pool_scale[0]).at[1, :64].set(sgu_ln_g[0]).at[2, :64].set(sgu_ln_b[0])
    ag_groups, ag_token = _comm_start(
        [[jnp.transpose(w_in_even[0]).astype(BF16), small_vec],
         [s5_w_glu[0].astype(BF16), w_out_even[0].astype(BF16)],
         [mlp_w1[0].astype(BF16), mlp_w2[0].astype(BF16)],
         [jnp.transpose(w_in_odd[0]).astype(BF16), w_out_odd[0].astype(BF16), mlp_w1[1].astype(BF16), mlp_w2[1].astype(BF16)]],
        "ag_start", exchange=False)

    lam_r = jnp.concatenate([s5_lam_re.reshape(1, S5_NS), s5_lam_im.reshape(1, S5_NS)], axis=0)
    ldt_r = jnp.repeat(s5_log_dt.reshape(32), 64).reshape(1, S5_NS)
    lam_c = jnp.transpose(lam_r)
    ldt_c = jnp.transpose(ldt_r)
    b_t = jnp.stack([jnp.tile(s5_b_re.reshape(S5_NS, 16), (1, 8)), jnp.tile(s5_b_im.reshape(S5_NS, 16), (1, 8))])
    c_t = jnp.stack([jnp.tile(s5_c_re.reshape(S5_W, 64), (1, 8)), jnp.tile(s5_c_im.reshape(S5_W, 64), (1, 8))])
    bf_pad = jnp.pad(fox_b_f, ((0, 0), (0, LANES - 8)))
    b_st = jnp.transpose(sgu_b_s[0])

    h0, rx0 = _rms_fwd(x0, _tie(mix_pre_g[0:1], ag_token), "rms0")
    tabs, bset, cset = _s5_prep(lam_r, ldt_r, lam_c, ldt_c, b_t, c_t, "s5_prep")
    ag0 = _comm_wait(ag_groups[0], tabs, "ag_wait0", exchange=False)
    winT_e = jnp.pad(ag0[0].reshape(EVEN_IN, D_MODEL), ((0, EVEN_PAD - EVEN_IN), (0, 0)))
    pool_scale_f = ag0[1][:, 0, :64].reshape(1, 512)
    ln_g_f = ag0[1][:, 1, :64].reshape(1, 512)
    ln_b_f = ag0[1][:, 2, :64].reshape(1, 512)
    z0 = _mm(h0, winT_e, name="win_even", tb=True, bm=512, bn=EVEN_PAD)
    xs = _s5_scan_fwd(z0, bset, tabs, "s5_scan")
    ag1 = _comm_wait(ag_groups[1], xs, "ag_wait1", exchange=False)
    wglu = ag1[0].reshape(S5_W, S5_W)
    wout_e = ag1[1].reshape(D_MODEL, D_MODEL)
    ylin, ya = _s5_out_fwd(xs, cset, z0, s5_d, wglu, "s5_out")
    fcum, fq = _fox_f_fwd(z0, bf_pad, "fox_f")
    frow = jnp.transpose(fcum[:, :8]).reshape(4, 2, L)
    o_att, lse = _fox_fwd(z0, fq, frow, "fox_fwd")
    mix0 = [ya, o_att]
    x1, ry0, h1, rx1, y0 = _mm(mix0, wout_e, name="wout_even", epi=_epi_post_pre, extra=(x0,),
                               vecs=(mix_post_g[0:1], mlp_pre_g[0:1]), out_dtypes=POST_PRE_DTYPES,
                               out_kinds=POST_PRE_KINDS, bm=FUSED_ROWS)
    ag2 = _comm_wait(ag_groups[2], rx1, "ag_wait2", exchange=False)
    w1 = [ag2[0], None]
    w2 = [ag2[1].reshape(4 * D_MODEL, D_MODEL), None]
    p0, a0 = _mm(h1, w1[0], name="mlp0_w1", b3=True, out_dtypes=(BF16, BF16), epi=_epi_relu2, bm=2048)
    x2, ro0, h2, rx2, o0 = _mm(a0, w2[0], name="mlp0_w2", epi=_epi_post_pre, extra=(x1,),
                               vecs=(mlp_post_g[0:1], mix_pre_g[1:2]), out_dtypes=POST_PRE_DTYPES,
                               out_kinds=POST_PRE_KINDS, bm=FUSED_ROWS, bk=4 * D_MODEL)
    ag3 = _comm_wait(ag_groups[3], rx2, "ag_wait3", exchange=False)
    winT_o = ag3[0].reshape(ODD_IN, D_MODEL)
    wout_o = ag3[1].reshape(D_MODEL, D_MODEL)
    w1[1] = ag3[2]
    w2[1] = ag3[3].reshape(4 * D_MODEL, D_MODEL)
    z1 = _mm(h2, winT_o, name="win_odd", tb=True, bn=ODD_IN)
    yc, pooled = _pool_fwd(z1, pool_w[0], pool_scale_f, "pool_fwd")
    yd = _sgu_fwd(z1, ln_g_f, ln_b_f, sgu_w_s[0], b_st, "sgu_fwd")
    mix1 = [yc, yd]
    x3, ry1, h3, rx3, y1 = _mm(mix1, wout_o, name="wout_odd", epi=_epi_post_pre, extra=(x2,),
                               vecs=(mix_post_g[1:2], mlp_pre_g[1:2]), out_dtypes=POST_PRE_DTYPES,
                               out_kinds=POST_PRE_KINDS, bm=FUSED_ROWS)
    p1, a1 = _mm(h3, w1[1], name="mlp1_w1", b3=True, out_dtypes=(BF16, BF16), epi=_epi_relu2, bm=512, bn=4 * D_MODEL)
    o1 = _mm(a1, w2[1], name="mlp1_w2", bm=512, bk=4 * D_MODEL)
    gx4, ro1, sq = _post_loss_fwd(x3, o1, mlp_post_g[1:2], target, "post3")

    g_o1, gg_mlp_post1 = _post_bwd(gx4, o1, ro1, mlp_post_g[1:2], "bpost3")
    g_p1 = _mm(g_o1, w2[1], name="b_mlp1_a", tb=True, out_dtypes=(BF16,), epi=_epi_relu2_bwd, extra=(p1,),
               bm=512, bn=4 * D_MODEL)
    gw2_1 = _mm(a1, g_o1, name="b_mlp1_w2", ta=True, bm=512, bk=L)
    gw1_1 = _mm(h3, g_p1, name="b_mlp1_w1", ta=True, out3=True, bn=512, bk=L)
    (ex1,), tok1 = _comm_start([[gw1_1, gw2_1.reshape(N_DEV, 512, D_MODEL)]], "ex_start1", exchange=True)
    g_x3, gg_mlp_pre1, g_y1, gg_mix_post1 = _mm(
        g_p1, w1[1], name="b_mlp1_h", tb=True, b3=True, epi=_epi_pre_post_bwd, extra=(x3, gx4, y1), cols=(rx3, ry1),
        vecs=(_tie(mlp_pre_g[1:2], tok1), mix_post_g[1:2]), out_dtypes=PRE_POST_BWD_DTYPES,
        out_kinds=PRE_POST_BWD_KINDS, bm=FUSED_ROWS, bk=4 * D_MODEL)
    g_mix1 = _mm(g_y1, wout_o, name="b_wout_odd_m", tb=True)
    gwout_o = _mm(mix1, g_y1, name="b_wout_odd_w", ta=True)
    g_xc, g_pool_w, g_pool_scale = _pool_bwd(g_mix1, pooled, pool_w[0], pool_scale_f, "pool_bwd")
    g_u1, g_v1, g_ws, g_bst, g_ln_g, g_ln_b = _sgu_bwd(g_mix1, z1, ln_g_f, ln_b_f, sgu_w_s[0], b_st, "sgu_bwd")
    g_z1 = [g_xc, g_u1, g_v1]
    gwinT_o = _mm(g_z1, h2, name="b_win_odd_w", ta=True)
    (ex2,), tok2 = _comm_start([[gwout_o.reshape(N_DEV, 128, D_MODEL), gwinT_o.reshape(N_DEV, ODD_IN // N_DEV, D_MODEL)]], "ex_start2", exchange=True)
    g_x2, gg_mix_pre1, g_o0, gg_mlp_post0 = _mm(
        g_z1, winT_o, name="b_win_odd_h", epi=_epi_pre_post_bwd, extra=(x2, g_x3, o0), cols=(rx2, ro0),
        vecs=(_tie(mix_pre_g[1:2], tok2), mlp_post_g[0:1]), out_dtypes=PRE_POST_BWD_DTYPES,
        out_kinds=PRE_POST_BWD_KINDS, bm=FUSED_ROWS)
    g_p0 = _mm(g_o0, w2[0], name="b_mlp0_a", tb=True, out_dtypes=(BF16,), epi=_epi_relu2_bwd, extra=(p0,), bm=2048)
    gw2_0 = _mm(a0, g_o0, name="b_mlp0_w2", ta=True, bm=2048)
    gw1_0 = _mm(h1, g_p0, name="b_mlp0_w1", ta=True, out3=True, bn=2048)
    (ex3,), tok3 = _comm_start([[gw1_0, gw2_0.reshape(N_DEV, 512, D_MODEL)]], "ex_start3", exchange=True)
    g_x1, gg_mlp_pre0, g_y0, gg_mix_post0 = _mm(
        g_p0, w1[0], name="b_mlp0_h", tb=True, b3=True, epi=_epi_pre_post_bwd, extra=(x1, g_x2, y0), cols=(rx1, ry0),
        vecs=(_tie(mlp_pre_g[0:1], tok3), mix_post_g[0:1]), out_dtypes=PRE_POST_BWD_DTYPES,
        out_kinds=PRE_POST_BWD_KINDS, bm=FUSED_ROWS, bk=4 * D_MODEL)
    g_mix0 = _mm(g_y0, wout_e, name="b_wout_even_m", tb=True)
    gwout_e = _mm(mix0, g_y0, name="b_wout_even_w", ta=True)
    gyl, gud, g_wglu, g_d = _s5_glu_bwd(g_mix0, ylin, z0, s5_d, wglu, "s5_glu_bwd")
    (ex4,), tok4 = _comm_start([[gwout_e.reshape(N_DEV, 128, D_MODEL), g_wglu.reshape(N_DEV, 64, S5_W)]], "ex_start4", exchange=True)
    g_u0, ga, gb_raw, gc_raw = _s5_scan_bwd(gyl, _tie(cset, tok4), xs, z0, bset, gud, tabs, "s5_scan_bwd")
    g_lam, g_ldt, g_b, g_c = _s5_param_bwd(lam_c, ldt_c, b_t, gb_raw, jnp.transpose(ga), gc_raw, "s5_param_bwd")
    dq, dk, dv, dfq, dfrow = _fox_bwd(z0, fq, frow, o_att, lse, g_mix0, "fox_bwd")
    dFk = jnp.pad(jnp.transpose(dfrow.reshape(8, L)), ((0, 0), (0, LANES - 8)))
    dfl, db_f = _fox_f_bwd(dFk, dfq, z0, bf_pad, "fox_f_bwd")
    g_z0 = [g_u0, dq, dk, dv, dfl]
    grad_x, gg_mix_pre0 = _mm(g_z0, winT_e, name="b_win_even_h", epi=_epi_pre_bwd, extra=(x0, g_x1), cols=(rx0,),
                              vecs=(mix_pre_g[0:1],), out_dtypes=(F32, F32), out_kinds=("full", "vsum"),
                              bm=FUSED_ROWS)

    small_grads = dict(
        mix_pre_g=jnp.concatenate([gg_mix_pre0, gg_mix_pre1]), mix_post_g=jnp.concatenate([gg_mix_post0, gg_mix_post1]),
        mlp_pre_g=jnp.concatenate([gg_mlp_pre0, gg_mlp_pre1]), mlp_post_g=jnp.concatenate([gg_mlp_post0, gg_mlp_post1]),
        s5_lam_re=g_lam[:, 0], s5_lam_im=g_lam[:, 1], s5_log_dt=g_ldt,
        s5_b_re=g_b[0, :, :16], s5_b_im=g_b[1, :, :16], s5_c_re=g_c[0, :, :64], s5_c_im=g_c[1, :, :64],
        s5_d=g_d, fox_b_f=db_f[:, :8], pool_w=g_pool_w, sgu_w_s=g_ws, sgu_b_s=jnp.transpose(g_bst),
        pool_scale=g_pool_scale, sgu_ln_g=g_ln_g, sgu_ln_b=g_ln_b)
    small_names = list(small_grads)
    full_shapes = [(512,) if nm in ("pool_scale", "sgu_ln_g", "sgu_ln_b") else weights[nm].shape for nm in small_names]
    full_shapes.append((1, 1))
    rows = _packed_rows(full_shapes)
    packed = _pack([small_grads[nm] for nm in small_names] + [sq], rows).reshape(N_DEV, rows // N_DEV, LANES)
    (exs,), tok_s = _comm_start([[packed]], "exs_start", exchange=True)
    gwinT_e = _mm(g_z0, h0, name="b_win_even_w", ta=True, bk=512, out_dtypes=(BF16,), dep=tok_s)
    (recv_small,) = _comm_wait(exs, gwinT_e, "exs_wait", exchange=True)
    piece = _sum_pieces(recv_small, "sum_small")
    (ags,), tok_a = _comm_start([[piece]], "ags_start", exchange=False)

    gwinT_e_pieces = gwinT_e[:EVEN_IN].reshape(N_DEV, EVEN_IN // N_DEV, D_MODEL)
    (ex5,), tok5 = _comm_start([[gwinT_e_pieces]], "ex_start5", exchange=True, dep=tok_a)
    r_w1_1, r_w2_1 = _comm_wait(ex1, tok5, "ex_wait1", exchange=True)
    r_wout_o, r_win_o = _comm_wait(ex2, tok5, "ex_wait2", exchange=True)
    r_w1_0, r_w2_0 = _comm_wait(ex3, tok5, "ex_wait3", exchange=True)
    r_wout_e, r_wglu = _comm_wait(ex4, tok5, "ex_wait4", exchange=True)

    res = {}
    for nm, parts in (("mlp_w1", (r_w1_0, r_w1_1)), ("mlp_w2", (r_w2_0, r_w2_1))):
        first = _sum_adamw(parts[0], weights[nm], mom_m[nm], mom_v[nm], "adamw_%s_0" % nm, layer=0)
        res[nm] = tuple(_sum_adamw(parts[1], weights[nm], mom_m[nm], mom_v[nm], "adamw_%s_1" % nm, layer=1, prev=first))
    big_parts = dict(s5_w_glu=r_wglu, w_out_even=r_wout_e, w_out_odd=r_wout_o)
    for nm, parts in big_parts.items():
        res[nm] = tuple(_sum_adamw(parts, weights[nm], mom_m[nm], mom_v[nm], "adamw_" + nm))
    done = [res[nm][1] for nm in ("mlp_w1", "mlp_w2", "s5_w_glu", "w_out_even", "w_out_odd")]

    (small_all,) = _comm_wait(ags, done, "ags_wait", exchange=False)
    small_full = _unpack(small_all.reshape(rows, LANES), full_shapes)
    loss = 0.5 * small_full.pop()[0, 0] / D_MODEL
    small_g = []
    for nm, g in zip(small_names, small_full):
        if nm in ("pool_scale", "sgu_ln_g", "sgu_ln_b"):
            g = lax.dynamic_slice(g, (my_index * 64,), (64,)).reshape(1, 64)
        small_g.append(g)
    sd, sm, sv = _adamw_many([weights[nm] for nm in small_names], small_g, [mom_m[nm] for nm in small_names],
                             [mom_v[nm] for nm in small_names], "adamw_small")
    for nm, g_, d_, m_, v_ in zip(small_names, small_g, sd, sm, sv):
        res[nm] = (g_, d_, m_, v_)
    done.append(sd[0])

    for nm, parts in (("w_in_odd", r_win_o), ("w_in_even", None)):
        if parts is None:
            (parts,) = _comm_wait(ex5, done, "ex_wait5", exchange=True)
        outs = _sum_adamw(parts, jnp.transpose(weights[nm], (0, 2, 1)), jnp.transpose(mom_m[nm], (0, 2, 1)),
                          jnp.transpose(mom_v[nm], (0, 2, 1)), "adamw_" + nm)
        res[nm] = tuple(jnp.transpose(o, (0, 2, 1)) for o in outs)
        done.append(res[nm][1])

    grads = [res[nm][0].reshape(weights[nm].shape) for nm in names]
    deltas = [res[nm][1].reshape(weights[nm].shape) for nm in names]
    new_m = [res[nm][2].reshape(weights[nm].shape) for nm in names]
    new_v = [res[nm][3].reshape(weights[nm].shape) for nm in names]
    return (loss, grad_x[None], *grads, *deltas, *new_m, *new_v)
```

```python
import functools
import math

import jax
import jax.numpy as jnp
from jax import lax
from jax.experimental import pallas as pl
from jax.experimental.pallas import tpu as pltpu

F32 = jnp.float32
BF16 = jnp.bfloat16
MESH = pl.DeviceIdType.MESH
ANY = pl.BlockSpec(memory_space=pl.ANY)

N_DEV = 8
D_MODEL = 1024
EPS = 1e-6
NORM_ROWS = 512
FUSED_ROWS = 512
S5_W = 512
S5_NS = 2048
SCAN_GROUPS = 4
SCAN_CHUNK = 1024
FOX_W = 512
EVEN_IN = 2056
EVEN_PAD = 2176
ODD_IN = 1536
LANES = 128
PIECE = 4 * D_MODEL // N_DEV
VMEM_LIMIT = 56 * 1024 * 1024

ADAM_LR = 0.001
ADAM_B1 = 0.9
ADAM_B2 = 0.999
ADAM_EPS = 1e-08
ADAM_WD = 0.01
ADAM_STEP = 10

NT = (((1,), (1,)), ((), ()))
TN = (((0,), (0,)), ((), ()))
NN = (((1,), (0,)), ((), ()))


def _cp(*sem):
    return pltpu.CompilerParams(dimension_semantics=sem, vmem_limit_bytes=VMEM_LIMIT)


def _sds(shape, dtype=F32):
    return jax.ShapeDtypeStruct(tuple(shape), dtype)


def _gelu(x):
    t = jnp.tanh(0.7978845608028654 * (x + 0.044715 * x * x * x))
    return 0.5 * x * (1.0 + t)


def _gelu_grad(x):
    t = jnp.tanh(0.7978845608028654 * (x + 0.044715 * x * x * x))
    du = 0.7978845608028654 * (1.0 + 3.0 * 0.044715 * x * x)
    return 0.5 * (1.0 + t) + 0.5 * x * (1.0 - t * t) * du


def _sigmoid(x):
    return 1.0 / (1.0 + jnp.exp(-x))


def _dot(a, b, dn=NN):
    return lax.dot_general(a, b, dn, preferred_element_type=F32)


def _mm(a, b, *, name, ta=False, tb=False, b3=False, out3=False, out_dtypes=(F32,), epi=None, extra=(),
        cols=(), vecs=(), out_kinds=None, bm=1024, bn=1024, bk=1024, dep=None):
    a_list = list(a) if isinstance(a, (list, tuple)) else [a]
    widths = [p.shape[1] for p in a_list]
    offs = [sum(widths[:i]) for i in range(len(widths))]
    na = len(a_list)
    M = sum(widths) if ta else a_list[0].shape[0]
    K = a_list[0].shape[0] if ta else sum(widths)
    if na > 1:
        assert not b3 and not tb
        bm, bk = (M, bk) if ta else (bm, K)
    pw = b.shape[2] if b3 else PIECE
    if b3:
        N = b.shape[1] if tb else b.shape[0] * pw
        assert (b.shape[0] * pw if tb else b.shape[1]) == K
    else:
        N = b.shape[0] if tb else b.shape[1]
    bm, bn, bk = min(bm, M), min(bn, N), min(bk, K)
    assert M % bm == 0 and N % bn == 0 and K % bk == 0, (name, M, N, K, bm, bn, bk)
    assert not (b3 or out3) or ((bk if tb else bn) % pw == 0 and bn % PIECE == 0)
    nk = K // bk
    n_extra = len(extra) + len(cols) + len(vecs)
    n_out = len(out_dtypes)
    out_kinds = tuple(out_kinds) if out_kinds is not None else ("full",) * n_out
    dn = (((0 if ta else 1,), (1 if tb else 0,)), ((), ()))

    use_acc = nk > 1

    def body(*refs):
        a_refs, b_ref = refs[:na], refs[na]
        a_ref = a_refs[0]
        e_refs = refs[na + 1:na + 1 + n_extra]
        first_out = na + 1 + n_extra + (0 if dep is None else 1)
        o_refs = refs[first_out:first_out + n_out]
        acc_ref = refs[-1] if use_acc else o_refs[0]
        i, k = pl.program_id(0), pl.program_id(2)

        def dot(a_v, b_v):
            return lax.dot_general(a_v.astype(BF16), b_v.astype(BF16), dn, preferred_element_type=F32)

        everything = slice(None)
        if na > 1 and ta:
            terms = [(pl.ds(off, w), everything, r, b_ref) for r, off, w in zip(a_refs, offs, widths)]
        elif na > 1:
            terms = [(everything, everything, r, b_ref.at[pl.ds(off, w), :]) for r, off, w in zip(a_refs, offs, widths)]
        elif not b3:
            terms = [(everything, everything, a_ref, b_ref)]
        elif tb:
            terms = [(everything, everything,
                      a_ref.at[pl.ds(t * pw, pw), :] if ta else a_ref.at[:, pl.ds(t * pw, pw)], b_ref.at[t])
                     for t in range(bk // pw)]
        else:
            terms = [(everything, pl.ds(t * pw, pw), a_ref, b_ref.at[t]) for t in range(bn // pw)]

        def finish(acc):
            outs = (acc,) if epi is None else epi(acc, *[e[...] for e in e_refs])
            for o_ref, o, kind in zip(o_refs, outs, out_kinds):
                if kind == "vsum":
                    @pl.when(i == 0)
                    def _(o_ref=o_ref, o=o):
                        o_ref[...] = o

                    @pl.when(i > 0)
                    def _(o_ref=o_ref, o=o):
                        o_ref[...] += o
                elif out3:
                    for t in range(bn // PIECE):
                        o_ref[t] = o[:, t * PIECE:(t + 1) * PIECE].astype(o_ref.dtype)
                else:
                    o_ref[...] = o.astype(o_ref.dtype)

        if nk == 1:
            bands = {}
            for rows, cols, a_r, b_r in terms:
                key = (getattr(rows, "start", None), getattr(cols, "start", None))
                val = dot(a_r[...], b_r[...])
                bands[key] = val if key not in bands else bands[key] + val
            vals = list(bands.values())
            if len(vals) == 1:
                finish(vals[0])
            else:
                finish(jnp.concatenate(vals, axis=0 if (na > 1 and ta) else 1))
            return

        @pl.when(k == 0)
        def _():
            acc_ref[...] = jnp.zeros_like(acc_ref)

        for rows, cols, a_r, b_r in terms:
            acc_ref[rows, cols] += dot(a_r[...], b_r[...])

        @pl.when(k == nk - 1)
        def _():
            finish(acc_ref[...])

    if na > 1:
        a_specs = [pl.BlockSpec((bk, w), lambda i, j, k: (k, 0)) if ta else pl.BlockSpec((bm, w), lambda i, j, k: (i, 0))
                   for w in widths]
    else:
        a_specs = [pl.BlockSpec((bk, bm), lambda i, j, k: (k, i)) if ta else
                   pl.BlockSpec((bm, bk), lambda i, j, k: (i, k))]
    if b3:
        if tb:
            b_spec = pl.BlockSpec((bk // pw, bn, pw), lambda i, j, k: (k, j, 0))
        else:
            b_spec = pl.BlockSpec((bn // pw, bk, pw), lambda i, j, k: (j, k, 0))
    else:
        b_spec = pl.BlockSpec((bn, bk), lambda i, j, k: (j, k)) if tb else pl.BlockSpec((bk, bn), lambda i, j, k: (k, j))
    e_specs = ([pl.BlockSpec((bm, bn), lambda i, j, k: (i, j)) for _ in extra]
               + [pl.BlockSpec((bm, 1), lambda i, j, k: (i, 0)) for _ in cols]
               + [pl.BlockSpec((1, bn), lambda i, j, k: (0, j)) for _ in vecs])
    if out3:
        o_specs = [pl.BlockSpec((bn // PIECE, bm, PIECE), lambda i, j, k: (j, i, 0)) for _ in out_dtypes]
        o_shapes = [_sds((N // PIECE, M, PIECE), dt) for dt in out_dtypes]
    else:
        spec_of = {"full": pl.BlockSpec((bm, bn), lambda i, j, k: (i, j)),
                   "col": pl.BlockSpec((bm, 1), lambda i, j, k: (i, 0)),
                   "vsum": pl.BlockSpec((1, bn), lambda i, j, k: (0, j))}
        shape_of = {"full": (M, N), "col": (M, 1), "vsum": (1, N)}
        o_specs = [spec_of[kind] for kind in out_kinds]
        o_shapes = [_sds(shape_of[kind], dt) for kind, dt in zip(out_kinds, out_dtypes)]
    assert "col" not in out_kinds or bn == N
    outs = pl.pallas_call(
        body, name=name, grid=(M // bm, N // bn, nk),
        in_specs=a_specs + [b_spec] + e_specs + ([] if dep is None else [ANY]),
        out_specs=o_specs, out_shape=o_shapes,
        scratch_shapes=[pltpu.VMEM((bm, bn), F32)] if use_acc else [],
        compiler_params=_cp("arbitrary" if "vsum" in out_kinds else "parallel", "parallel", "arbitrary"),
    )(*a_list, b, *extra, *cols, *vecs, *([] if dep is None else [dep]))
    return outs[0] if n_out == 1 else outs


def _epi_relu2(acc):
    r = jnp.maximum(acc, 0.0)
    return acc, r * r


def _epi_relu2_bwd(acc, p):
    return (acc * (2.0 * jnp.maximum(p.astype(F32), 0.0)),)


def _row_spec(rb, w=D_MODEL):
    return pl.BlockSpec((rb, w), lambda i: (i, 0))


def _vec_spec(w=D_MODEL):
    return pl.BlockSpec((1, w), lambda i: (0, 0))


def _rstd(v):
    return lax.rsqrt(jnp.mean(v * v, axis=-1, keepdims=True) + EPS)


def _rms_fwd(x, g, name):
    L = x.shape[0]
    rb = min(NORM_ROWS, L)

    def body(x_ref, g_ref, h_ref, r_ref):
        xv = x_ref[...]
        r = _rstd(xv)
        h_ref[...] = (xv * r * g_ref[...]).astype(BF16)
        r_ref[...] = r

    return pl.pallas_call(
        body, name=name, grid=(L // rb,),
        in_specs=[_row_spec(rb), _vec_spec()],
        out_specs=[_row_spec(rb), _row_spec(rb, 1)],
        out_shape=[_sds((L, D_MODEL), BF16), _sds((L, 1))],
        compiler_params=_cp("parallel"),
    )(x, g)


def _rms_bwd_rows(dy, xv, r, g):
    n = xv * r
    dyg = dy * g
    return r * (dyg - n * jnp.mean(dyg * n, axis=-1, keepdims=True)), n


POST_PRE_DTYPES = (F32, F32, BF16, F32, F32)
POST_PRE_KINDS = ("full", "col", "full", "col", "full")
PRE_POST_BWD_DTYPES = (F32, F32, BF16, F32)
PRE_POST_BWD_KINDS = ("full", "vsum", "full", "vsum")


def _epi_post_pre(y, x_in, g_post, g_pre):
    ry = _rstd(y)
    xo = x_in + y * ry * g_post
    rx = _rstd(xo)
    return xo, ry, xo * rx * g_pre, rx, y


def _epi_pre_post_bwd(gh, x, g_out, y_prev, rx, ry_prev, g_pre, g_post_prev):
    gx, n = _rms_bwd_rows(gh, x, rx, g_pre)
    gi = g_out + gx
    gy, ny = _rms_bwd_rows(gi, y_prev, ry_prev, g_post_prev)
    return gi, jnp.sum(gh * n, axis=0, keepdims=True), gy, jnp.sum(gi * ny, axis=0, keepdims=True)


def _epi_pre_bwd(gh, x, g_out, rx, g_pre):
    gx, n = _rms_bwd_rows(gh, x, rx, g_pre)
    return g_out + gx, jnp.sum(gh * n, axis=0, keepdims=True)


def _epi_post_loss(y, x_in, target, g_post):
    ry = _rstd(y)
    diff = x_in + y * ry * g_post - target
    gx = diff * (1.0 / D_MODEL)
    gy, n = _rms_bwd_rows(gx, y, ry, g_post)
    sq = jnp.broadcast_to(jnp.sum(diff * diff, keepdims=True), (1, y.shape[1]))
    return gx, gy, jnp.sum(gx * n, axis=0, keepdims=True), sq


def _cmul(ar, ai, br, bi):
    return ar * br - ai * bi, ar * bi + ai * br


def _zoh_cols(lr, li, ldt):
    dt = jnp.exp(ldt)
    mag = jnp.exp(lr * dt)
    ar = mag * jnp.cos(li * dt)
    ai = mag * jnp.sin(li * dt)
    den = lr * lr + li * li
    nr = ar - 1.0
    qr = (nr * lr + ai * li) / den
    qi = (ai * lr - nr * li) / den
    return dt, ar, ai, qr, qi, den


def _b_mask():
    r = lax.broadcasted_iota(jnp.int32, (S5_NS, LANES), 0)
    c = lax.broadcasted_iota(jnp.int32, (S5_NS, LANES), 1)
    return ((r >> 6) & 7) == (c >> 4)


def _c_mask():
    r = lax.broadcasted_iota(jnp.int32, (S5_W, 512), 0)
    c = lax.broadcasted_iota(jnp.int32, (S5_W, 512), 1)
    return ((r >> 4) & 7) == (c >> 6)


def _s5_prep(lam_r, ldt_r, lam_c, ldt_c, b_t, c_t, name):
    def body(lam_r_ref, ldt_r_ref, lam_c_ref, ldt_c_ref, b_ref, c_ref, tab_ref, bset_ref, cset_ref):
        lr, li = lam_r_ref[0:1, :], lam_r_ref[1:2, :]
        dt = jnp.exp(ldt_r_ref[...])
        mag = jnp.exp(lr * dt)
        p1r, p1i = mag * jnp.cos(li * dt), mag * jnp.sin(li * dt)
        p2r, p2i = _cmul(p1r, p1i, p1r, p1i)
        p3r, p3i = _cmul(p2r, p2i, p1r, p1i)
        p4r, p4i = _cmul(p2r, p2i, p2r, p2i)
        p5r, p5i = _cmul(p4r, p4i, p1r, p1i)
        p6r, p6i = _cmul(p4r, p4i, p2r, p2i)
        p7r, p7i = _cmul(p4r, p4i, p3r, p3i)
        p8r, p8i = _cmul(p4r, p4i, p4r, p4i)
        pw_r = [p1r, p2r, p3r, p4r, p5r, p6r, p7r, p8r]
        pw_i = [p1i, p2i, p3i, p4i, p5i, p6i, p7i, p8i]
        row = lax.broadcasted_iota(jnp.int32, (8, S5_NS), 0)
        zero = jnp.zeros((8, S5_NS), F32)

        def bc(v):
            return jnp.broadcast_to(v, (8, S5_NS))

        for d in range(2):
            sgn = 1.0 if d == 0 else -1.0
            for t, s in enumerate((1, 2, 4)):
                live = (row >= s) if d == 0 else (row <= 7 - s)
                tab_ref[d, 2 * t] = jnp.where(live, bc(pw_r[s - 1]), zero)
                tab_ref[d, 2 * t + 1] = jnp.where(live, bc(sgn * pw_i[s - 1]), zero)
            cr, ci = zero, zero
            for i in range(8):
                e = i if d == 0 else 7 - i
                cr = jnp.where(row == i, bc(pw_r[e]), cr)
                ci = jnp.where(row == i, bc(sgn * pw_i[e]), ci)
            tab_ref[d, 6] = cr
            tab_ref[d, 7] = ci

        _, _, _, qr, qi, _ = _zoh_cols(lam_c_ref[:, 0:1], lam_c_ref[:, 1:2], ldt_c_ref[...])
        bm = _b_mask()
        br, bi = b_ref[0], b_ref[1]
        bset_ref[0] = jnp.where(bm, qr * br - qi * bi, 0.0).astype(BF16)
        bset_ref[1] = jnp.where(bm, qr * bi + qi * br, 0.0).astype(BF16)
        cm = _c_mask()
        cset_ref[0] = jnp.where(cm, c_ref[0], 0.0).astype(BF16)
        cset_ref[1] = jnp.where(cm, c_ref[1], 0.0).astype(BF16)

    vm = pl.BlockSpec(memory_space=pltpu.VMEM)
    return pl.pallas_call(
        body, name=name, in_specs=[vm] * 6, out_specs=[vm] * 3,
        out_shape=[_sds((2, 8, 8, S5_NS)), _sds((2, S5_NS, LANES), BF16), _sds((2, S5_W, 512), BF16)],
        compiler_params=pltpu.CompilerParams(vmem_limit_bytes=VMEM_LIMIT),
    )(lam_r, ldt_r, lam_c, ldt_c, b_t, c_t)


SCAN_W = SCAN_GROUPS * LANES


def _scan_chunk(src_ref, dst_ref, tab_ref, carry_ref, nb, reverse, xs_ref=None, acc_ref=None):
    row = lax.broadcasted_iota(jnp.int32, (8, LANES), 0)

    def step(i, carry):
        b = (nb - 1 - i) if reverse else i
        off = pl.multiple_of(b * 8, 8)
        out = []
        for g in range(SCAN_GROUPS):
            lanes = pl.ds(g * LANES, LANES)
            cr, ci = carry[2 * g], carry[2 * g + 1]
            yr = src_ref[0, pl.ds(off, 8), lanes]
            yi = src_ref[1, pl.ds(off, 8), lanes]
            for t, s in enumerate((1, 2, 4)):
                sh = (8 - s) if reverse else s
                sr = pltpu.roll(yr, sh, 0)
                si = pltpu.roll(yi, sh, 0)
                mr, mi = tab_ref[2 * t, :, lanes], tab_ref[2 * t + 1, :, lanes]
                yr, yi = yr + mr * sr - mi * si, yi + mr * si + mi * sr
            pr, pi = tab_ref[6, :, lanes], tab_ref[7, :, lanes]
            yr, yi = yr + pr * cr - pi * ci, yi + pr * ci + pi * cr
            dst_ref[0, pl.ds(off, 8), lanes] = yr
            dst_ref[1, pl.ds(off, 8), lanes] = yi
            if xs_ref is not None:
                nr = jnp.where(row == 7, cr, pltpu.roll(yr, 7, 0))
                ni = jnp.where(row == 7, ci, pltpu.roll(yi, 7, 0))
                xr = xs_ref[0, pl.ds(off, 8), lanes]
                xi = xs_ref[1, pl.ds(off, 8), lanes]
                acc_ref[0, :, lanes] += xr * nr + xi * ni
                acc_ref[1, :, lanes] += xr * ni - xi * nr
            last = 0 if reverse else 7
            out += [jnp.broadcast_to(yr[last:last + 1, :], (8, LANES)),
                    jnp.broadcast_to(yi[last:last + 1, :], (8, LANES))]
        return tuple(out)

    init = []
    for g in range(SCAN_GROUPS):
        init += [carry_ref[0, :, pl.ds(g * LANES, LANES)], carry_ref[1, :, pl.ds(g * LANES, LANES)]]
    fin = lax.fori_loop(0, nb, step, tuple(init))
    for g in range(SCAN_GROUPS):
        carry_ref[0, :, pl.ds(g * LANES, LANES)] = fin[2 * g]
        carry_ref[1, :, pl.ds(g * LANES, LANES)] = fin[2 * g + 1]


def _s5_scan_fwd(z, bset, tabs, name):
    L = z.shape[0]
    tl = min(SCAN_CHUNK, L)
    nc = L // tl

    def body(u_ref, b_ref, tab_ref, x_ref, carry_ref):
        @pl.when(pl.program_id(1) == 0)
        def _():
            carry_ref[...] = jnp.zeros_like(carry_ref)

        u = u_ref[...].astype(BF16)
        x_ref[0] = _dot(u, b_ref[0], NT)
        x_ref[1] = _dot(u, b_ref[1], NT)
        _scan_chunk(x_ref, x_ref, tab_ref, carry_ref, tl // 8, False)

    return pl.pallas_call(
        body, name=name, grid=(S5_NS // SCAN_W, nc),
        in_specs=[pl.BlockSpec((tl, LANES), lambda j, c: (c, j)),
                  pl.BlockSpec((2, SCAN_W, LANES), lambda j, c: (0, j, 0)),
                  pl.BlockSpec((None, 8, 8, SCAN_W), lambda j, c: (0, 0, 0, j))],
        out_specs=pl.BlockSpec((2, tl, SCAN_W), lambda j, c: (0, c, j)),
        out_shape=_sds((2, L, S5_NS)),
        scratch_shapes=[pltpu.VMEM((2, 8, SCAN_W), F32)],
        compiler_params=_cp("parallel", "arbitrary"),
    )(z, bset, tabs)


def _s5_scan_bwd(gyl, cset, xs, z, bset, gud, tabs, name):
    L = z.shape[0]
    tl = min(SCAN_CHUNK, L)
    nc = L // tl

    def body(g_ref, c_ref, xs_ref, u_ref, b_ref, gud_ref, tab_ref, gu_ref, ga_ref, gb_ref, gc_ref,
             gx_ref, carry_ref, acc_ref):
        c = pl.program_id(1)

        @pl.when(c == 0)
        def _():
            carry_ref[...] = jnp.zeros_like(carry_ref)
            acc_ref[...] = jnp.zeros_like(acc_ref)
            gb_ref[...] = jnp.zeros_like(gb_ref)
            gc_ref[...] = jnp.zeros_like(gc_ref)

        gy = g_ref[...].astype(BF16)
        gx_ref[0] = _dot(gy, c_ref[0])
        gx_ref[1] = -_dot(gy, c_ref[1])
        gc_ref[0] += _dot(gy, xs_ref[0].astype(BF16), TN)
        gc_ref[1] -= _dot(gy, xs_ref[1].astype(BF16), TN)
        _scan_chunk(gx_ref, gx_ref, tab_ref, carry_ref, tl // 8, True, xs_ref, acc_ref)
        gr = gx_ref[0].astype(BF16)
        gi = gx_ref[1].astype(BF16)
        gu_ref[...] = gud_ref[...] + _dot(gr, b_ref[0]) + _dot(gi, b_ref[1])
        u = u_ref[...].astype(BF16)
        gb_ref[0] += _dot(gr, u, TN)
        gb_ref[1] += _dot(gi, u, TN)

        @pl.when(c == nc - 1)
        def _():
            ga_ref[0:1, :] = jnp.sum(acc_ref[0], axis=0, keepdims=True)
            ga_ref[1:2, :] = jnp.sum(acc_ref[1], axis=0, keepdims=True)

    rev = lambda j, c: (nc - 1 - c, j)
    col = pl.BlockSpec((tl, LANES), rev)
    return pl.pallas_call(
        body, name=name, grid=(S5_NS // SCAN_W, nc),
        in_specs=[col, pl.BlockSpec((2, LANES, SCAN_W), lambda j, c: (0, j, 0)),
                  pl.BlockSpec((2, tl, SCAN_W), lambda j, c: (0, nc - 1 - c, j)), col,
                  pl.BlockSpec((2, SCAN_W, LANES), lambda j, c: (0, j, 0)), col,
                  pl.BlockSpec((None, 8, 8, SCAN_W), lambda j, c: (1, 0, 0, j))],
        out_specs=[col, pl.BlockSpec((2, SCAN_W), lambda j, c: (0, j)),
                   pl.BlockSpec((2, SCAN_W, LANES), lambda j, c: (0, j, 0)),
                   pl.BlockSpec((2, LANES, SCAN_W), lambda j, c: (0, j, 0))],
        out_shape=[_sds((L, S5_W)), _sds((2, S5_NS)), _sds((2, S5_NS, LANES)), _sds((2, S5_W, 512))],
        scratch_shapes=[pltpu.VMEM((2, tl, SCAN_W), F32), pltpu.VMEM((2, 8, SCAN_W), F32),
                        pltpu.VMEM((2, 8, SCAN_W), F32)],
        compiler_params=_cp("parallel", "arbitrary"),
    )(gyl, cset, xs, z, bset, gud, tabs)


def _s5_out_fwd(xs, cset, z, dvec, wglu, name):
    L = z.shape[0]
    bl = min(256, L)

    def body(x_ref, c_ref, u_ref, d_ref, w_ref, ylin_ref, ya_ref):
        cols = []
        for j in range(4):
            xr = x_ref[0, :, 512 * j:512 * (j + 1)].astype(BF16)
            xi = x_ref[1, :, 512 * j:512 * (j + 1)].astype(BF16)
            cr = c_ref[0, LANES * j:LANES * (j + 1), :]
            ci = c_ref[1, LANES * j:LANES * (j + 1), :]
            cols.append(_dot(xr, cr, NT) - _dot(xi, ci, NT))
        ylin = jnp.concatenate(cols, axis=1) + d_ref[...] * u_ref[...]
        yg = _gelu(ylin)
        t = _dot(yg.astype(BF16), w_ref[...])
        ylin_ref[...] = ylin
        ya_ref[...] = (yg * _sigmoid(t)).astype(BF16)

    return pl.pallas_call(
        body, name=name, grid=(L // bl,),
        in_specs=[pl.BlockSpec((2, bl, S5_NS), lambda i: (0, i, 0)),
                  pl.BlockSpec((2, S5_W, 512), lambda i: (0, 0, 0)),
                  pl.BlockSpec((bl, S5_W), lambda i: (i, 0)),
                  pl.BlockSpec((1, S5_W), lambda i: (0, 0)),
                  pl.BlockSpec((S5_W, S5_W), lambda i: (0, 0))],
        out_specs=[pl.BlockSpec((bl, S5_W), lambda i: (i, 0))] * 2,
        out_shape=[_sds((L, S5_W)), _sds((L, S5_W), BF16)],
        compiler_params=_cp("parallel"),
    )(xs, cset, z, dvec, wglu)


def _s5_glu_bwd(g_m, ylin, z, dvec, wglu, name):
    L = z.shape[0]
    bl = min(256, L)

    def body(g_ref, ylin_ref, u_ref, d_ref, w_ref, gyl_ref, gud_ref, gw_ref, gd_ref):
        i = pl.program_id(0)
        ylin = ylin_ref[...]
        yg = _gelu(ylin)
        ygb = yg.astype(BF16)
        sg = _sigmoid(_dot(ygb, w_ref[...]))
        gya = g_ref[...]
        gt = gya * yg * sg * (1.0 - sg)
        gtb = gt.astype(BF16)
        gyg = gya * sg + _dot(gtb, w_ref[...], NT)
        gyl = gyg * _gelu_grad(ylin)
        gyl_ref[...] = gyl
        gud_ref[...] = gyl * d_ref[...]

        @pl.when(i == 0)
        def _():
            gw_ref[...] = jnp.zeros_like(gw_ref)
            gd_ref[...] = jnp.zeros_like(gd_ref)

        gw_ref[...] += _dot(ygb, gtb, TN)
        gd_ref[...] += jnp.sum(gyl * u_ref[...], axis=0, keepdims=True)

    blk = pl.BlockSpec((bl, S5_W), lambda i: (i, 0))
    return pl.pallas_call(
        body, name=name, grid=(L // bl,),
        in_specs=[blk, blk, blk, pl.BlockSpec((1, S5_W), lambda i: (0, 0)),
                  pl.BlockSpec((S5_W, S5_W), lambda i: (0, 0))],
        out_specs=[blk, blk, pl.BlockSpec((S5_W, S5_W), lambda i: (0, 0)), pl.BlockSpec((1, S5_W), lambda i: (0, 0))],
        out_shape=[_sds((L, S5_W)), _sds((L, S5_W)), _sds((S5_W, S5_W)), _sds((1, S5_W))],
        compiler_params=_cp("arbitrary"),
    )(g_m, ylin, z, dvec, wglu)


def _s5_param_bwd(lam_c, ldt_c, b_t, gb, ga_c, gc, name):
    def body(lam_ref, ldt_ref, b_ref, gb_ref, ga_ref, gc_ref, glam_ref, gldt_ref, gbo_ref, gco_ref):
        lr, li = lam_ref[:, 0:1], lam_ref[:, 1:2]
        dt, ar, ai, qr, qi, den = _zoh_cols(lr, li, ldt_ref[...])
        bm = _b_mask()
        gbr = jnp.where(bm, gb_ref[0], 0.0)
        gbi = jnp.where(bm, gb_ref[1], 0.0)
        br, bi = b_ref[0], b_ref[1]
        obr = gbr * qr + gbi * qi
        obi = gbi * qr - gbr * qi
        gqr = jnp.sum(gbr * br + gbi * bi, axis=1, keepdims=True)
        gqi = jnp.sum(gbi * br - gbr * bi, axis=1, keepdims=True)
        for s in (64, 32, 16):
            obr = obr + pltpu.roll(obr, s, 1)
            obi = obi + pltpu.roll(obi, s, 1)
        gbo_ref[0] = obr
        gbo_ref[1] = obi
        gar = ga_ref[:, 0:1] + (gqr * lr - gqi * li) / den
        gai = ga_ref[:, 1:2] + (gqr * li + gqi * lr) / den
        qlr = (qr * lr + qi * li) / den
        qli = (qi * lr - qr * li) / den
        glr = -(gqr * qlr + gqi * qli)
        gli = -(gqi * qlr - gqr * qli)
        glr = glr + dt * (gar * ar + gai * ai)
        gli = gli + dt * (gai * ar - gar * ai)
        wr, wi = _cmul(lr, li, ar, ai)
        gldt = (gar * wr + gai * wi) * dt
        glam_ref[:, 0:1] = glr
        glam_ref[:, 1:2] = gli
        r = lax.broadcasted_iota(jnp.int32, (S5_NS, 32), 0)
        c = lax.broadcasted_iota(jnp.int32, (S5_NS, 32), 1)
        gldt_ref[...] = jnp.sum(jnp.where((r >> 6) == c, gldt, 0.0), axis=0, keepdims=True)
        cm = _c_mask()
        for k in range(2):
            oc = jnp.where(cm, gc_ref[k], 0.0)
            for s in (256, 128, 64):
                oc = oc + pltpu.roll(oc, s, 1)
            gco_ref[k] = oc[:, 0:LANES]

    vm = pl.BlockSpec(memory_space=pltpu.VMEM)
    return pl.pallas_call(
        body, name=name, in_specs=[vm] * 6, out_specs=[vm] * 4,
        out_shape=[_sds((S5_NS, 2)), _sds((1, 32)), _sds((2, S5_NS, LANES)), _sds((2, S5_W, LANES))],
        compiler_params=pltpu.CompilerParams(vmem_limit_bytes=VMEM_LIMIT),
    )(lam_c, ldt_c, b_t, gb, ga_c, gc)


FL_BLK = EVEN_PAD // LANES - 1
Q_BLK, K_BLK, V_BLK = 4, 8, 12
NEG = -1e30


def _log_sigmoid(v):
    return jnp.minimum(v, 0.0) - jnp.log(1.0 + jnp.exp(-jnp.abs(v)))


def _fox_f_fwd(z, bf, name):
    L = z.shape[0]
    tl = min(256, L)

    def body(fl_ref, b_ref, f_ref, fq_ref, carry_ref):
        i = pl.program_id(0)

        @pl.when(i == 0)
        def _():
            carry_ref[...] = jnp.zeros_like(carry_ref)

        lf = _log_sigmoid(fl_ref[...] + b_ref[...])
        r = lax.broadcasted_iota(jnp.int32, (tl, tl), 0)
        c = lax.broadcasted_iota(jnp.int32, (tl, tl), 1)
        tri = (r >= c).astype(F32)
        cs = lax.dot_general(tri, lf, NN, precision=lax.Precision.HIGHEST, preferred_element_type=F32) + carry_ref[...]
        f_ref[...] = cs
        carry_ref[...] = cs[tl - 1:tl, :]
        expand = (lax.broadcasted_iota(jnp.int32, (LANES, FOX_W), 0)
                  == (lax.broadcasted_iota(jnp.int32, (LANES, FOX_W), 1) >> 6)).astype(F32)
        fq_ref[...] = lax.dot_general(cs, expand, NN, precision=lax.Precision.HIGHEST, preferred_element_type=F32)

    return pl.pallas_call(
        body, name=name, grid=(L // tl,),
        in_specs=[pl.BlockSpec((tl, LANES), lambda i: (i, FL_BLK)), pl.BlockSpec((1, LANES), lambda i: (0, 0))],
        out_specs=[pl.BlockSpec((tl, LANES), lambda i: (i, 0)), pl.BlockSpec((tl, FOX_W), lambda i: (i, 0))],
        out_shape=[_sds((L, LANES)), _sds((L, FOX_W))],
        scratch_shapes=[pltpu.VMEM((1, LANES), F32)],
        compiler_params=_cp("arbitrary"),
    )(z, bf)


def _fox_f_bwd(dFk, dfq, z, bf, name):
    L = z.shape[0]
    tl = min(256, L)
    nb = L // tl

    def body(dfk_ref, dfq_ref, fl_ref, b_ref, dfl_ref, db_ref, carry_ref):
        i = pl.program_id(0)

        @pl.when(i == 0)
        def _():
            carry_ref[...] = jnp.zeros_like(carry_ref)
            db_ref[...] = jnp.zeros_like(db_ref)

        sel = (lax.broadcasted_iota(jnp.int32, (FOX_W, LANES), 0)
               == 64 * lax.broadcasted_iota(jnp.int32, (FOX_W, LANES), 1)).astype(F32)
        dfq_h = lax.dot_general(dfq_ref[...], sel, NN, precision=lax.Precision.HIGHEST, preferred_element_type=F32)
        r = lax.broadcasted_iota(jnp.int32, (tl, tl), 0)
        c = lax.broadcasted_iota(jnp.int32, (tl, tl), 1)
        tri = (r <= c).astype(F32)
        cs = lax.dot_general(tri, dfk_ref[...] + dfq_h, NN, precision=lax.Precision.HIGHEST,
                             preferred_element_type=F32) + carry_ref[...]
        carry_ref[...] = cs[0:1, :]
        dfl = cs * _sigmoid(-(fl_ref[...] + b_ref[...]))
        dfl_ref[...] = dfl
        db_ref[...] += jnp.sum(dfl, axis=0, keepdims=True)

    return pl.pallas_call(
        body, name=name, grid=(nb,),
        in_specs=[pl.BlockSpec((tl, LANES), lambda i: (nb - 1 - i, 0)),
                  pl.BlockSpec((tl, FOX_W), lambda i: (nb - 1 - i, 0)),
                  pl.BlockSpec((tl, LANES), lambda i: (nb - 1 - i, FL_BLK)),
                  pl.BlockSpec((1, LANES), lambda i: (0, 0))],
        out_specs=[pl.BlockSpec((tl, LANES), lambda i: (nb - 1 - i, 0)), pl.BlockSpec((1, LANES), lambda i: (0, 0))],
        out_shape=[_sds((L, LANES)), _sds((1, LANES))],
        scratch_shapes=[pltpu.VMEM((1, LANES), F32)],
        compiler_params=_cp("arbitrary"),
    )(dFk, dfq, z, bf)


def _head_mask(hh):
    lane = lax.broadcasted_iota(jnp.int32, (1, LANES), 1)
    return (lane >> 6) == hh


FOX_T = 512


def _fox_head(x, hh):
    return jnp.where(_head_mask(hh), x, 0.0).astype(BF16)


def _fox_scores(qh, k, fq_ref, fr_ref, hh, causal):
    s = _dot(qh, k, NT) + (fq_ref[:, 64 * hh:64 * hh + 1] - fr_ref[hh:hh + 1, :])
    return s if causal is None else jnp.where(causal, s, NEG)


def _causal(T):
    return lax.broadcasted_iota(jnp.int32, (T, T), 1) <= lax.broadcasted_iota(jnp.int32, (T, T), 0)


def _fox_fwd(z, fq, frow, name):
    L = z.shape[0]
    T = min(FOX_T, L)
    nq = L // T

    def body(qt_ref, kt_ref, q_ref, k_ref, v_ref, fq_ref, fr_ref, o_ref, lse_ref, m_ref, l_ref, acc_ref):
        t = pl.program_id(1)
        qi, ki = qt_ref[t], kt_ref[t]

        @pl.when(ki == 0)
        def _():
            m_ref[...] = jnp.full_like(m_ref, NEG)
            l_ref[...] = jnp.zeros_like(l_ref)
            acc_ref[...] = jnp.zeros_like(acc_ref)

        def step(diagonal):
            q = q_ref[...] * 0.125
            k = k_ref[...].astype(BF16)
            v = v_ref[...].astype(BF16)
            causal = _causal(T) if diagonal else None
            s = jnp.concatenate([_fox_scores(_fox_head(q, hh), k, fq_ref, fr_ref, hh, causal) for hh in range(2)],
                                axis=0)
            m_old = m_ref[...]
            m_new = jnp.maximum(m_old, jnp.max(s, axis=1, keepdims=True))
            alpha = jnp.exp(m_old - m_new)
            p = jnp.exp(s - m_new)
            l_ref[...] = alpha * l_ref[...] + jnp.sum(p, axis=1, keepdims=True)
            m_ref[...] = m_new
            acc_ref[...] = alpha * acc_ref[...] + _dot(p.astype(BF16), v)

        @pl.when(ki < qi)
        def _():
            step(False)

        @pl.when(ki == qi)
        def _():
            step(True)
            h0 = _head_mask(0)
            l = l_ref[...]
            o_h = acc_ref[...] / l
            lse_h = m_ref[...] + jnp.log(l)
            o_ref[...] = jnp.where(h0, o_h[:T], o_h[T:])
            lse_ref[...] = jnp.where(h0, lse_h[:T], lse_h[T:])

    pairs = [(qi, ki) for qi in range(nq) for ki in range(qi + 1)]
    qt = jnp.asarray([p[0] for p in pairs], jnp.int32)
    kt = jnp.asarray([p[1] for p in pairs], jnp.int32)

    def qspec(base):
        return pl.BlockSpec((T, LANES), lambda j, t, qt, kt: (qt[t], base + j))

    def kspec(base):
        return pl.BlockSpec((T, LANES), lambda j, t, qt, kt: (kt[t], base + j))

    return pl.pallas_call(
        body, name=name,
        grid_spec=pltpu.PrefetchScalarGridSpec(
            num_scalar_prefetch=2, grid=(4, len(pairs)),
            in_specs=[qspec(Q_BLK), kspec(K_BLK), kspec(V_BLK), qspec(0),
                      pl.BlockSpec((None, 2, T), lambda j, t, qt, kt: (j, 0, kt[t]))],
            out_specs=[qspec(0), qspec(0)],
            scratch_shapes=[pltpu.VMEM((2 * T, 1), F32), pltpu.VMEM((2 * T, 1), F32),
                            pltpu.VMEM((2 * T, LANES), F32)]),
        out_shape=[_sds((L, FOX_W)), _sds((L, FOX_W))],
        compiler_params=_cp("parallel", "arbitrary"),
    )(qt, kt, z, z, z, fq, frow)


def _fox_bwd(z, fq, frow, o, lse, g_m, name):
    L = z.shape[0]
    T = min(FOX_T, L)
    nq = L // T

    pairs = [(qi, ki) for ki in range(nq) for qi in range(ki, nq)]
    qt = jnp.asarray([p[0] for p in pairs], jnp.int32)
    kt = jnp.asarray([p[1] for p in pairs], jnp.int32)

    def body(qt_ref, kt_ref, q_ref, k_ref, v_ref, fq_ref, fr_ref, o_ref, lse_ref, do_ref,
             dq_ref, dk_ref, dv_ref, dfq_ref, dfk_ref, dk_acc, dv_acc, df_acc):
        t = pl.program_id(1)
        qi, ki = qt_ref[t], kt_ref[t]

        @pl.when(t == 0)
        def _():
            dq_ref[...] = jnp.zeros_like(dq_ref)
            dfq_ref[...] = jnp.zeros_like(dfq_ref)

        @pl.when(qi == ki)
        def _():
            dk_acc[...] = jnp.zeros_like(dk_acc)
            dv_acc[...] = jnp.zeros_like(dv_acc)
            df_acc[...] = jnp.zeros_like(df_acc)

        def step(diagonal):
            q = q_ref[...] * 0.125
            qb = q.astype(BF16)
            k = k_ref[...].astype(BF16)
            v = v_ref[...].astype(BF16)
            do = do_ref[...]
            dob = do.astype(BF16)
            do_o = dob.astype(F32) * o_ref[...]
            causal = _causal(T) if diagonal else None
            dvs, dks, dqs, rss = [], [], [], []
            for hh in range(2):
                s = _fox_scores(_fox_head(q, hh), k, fq_ref, fr_ref, hh, causal)
                p = jnp.exp(s - lse_ref[:, 64 * hh:64 * hh + 1])
                dp = _dot(_fox_head(do, hh), v, NT)
                delta = jnp.sum(jnp.where(_head_mask(hh), do_o, 0.0), axis=1, keepdims=True)
                ds = p * (dp - delta)
                dsb = ds.astype(BF16)
                dvs.append(_dot(p.astype(BF16), dob, TN))
                dks.append(_dot(dsb, qb, TN))
                dqs.append(_dot(dsb, k))
                rss.append(jnp.sum(ds, axis=1, keepdims=True))
                df_acc[hh:hh + 1, :] -= jnp.sum(ds, axis=0, keepdims=True)
            h0 = _head_mask(0)
            dv_acc[...] += jnp.where(h0, dvs[0], dvs[1])
            dk_acc[...] += jnp.where(h0, dks[0], dks[1])
            rows = pl.ds(pl.multiple_of(qi * T, T), T)
            dq_ref[rows, :] += jnp.where(h0, dqs[0], dqs[1])
            dfq_ref[rows, :] += jnp.where(h0, rss[0], rss[1])

        @pl.when(qi > ki)
        def _():
            step(False)

        @pl.when(qi == ki)
        def _():
            step(True)

        @pl.when(qi == nq - 1)
        def _():
            dk_ref[...] = dk_acc[...]
            dv_ref[...] = dv_acc[...]
            dfk_ref[...] = df_acc[...]

        @pl.when(t == len(pairs) - 1)
        def _():
            dq_ref[...] = dq_ref[...] * 0.125

    def qside(base):
        return pl.BlockSpec((T, LANES), lambda j, t, qt, kt: (qt[t], base + j))

    def kside(base):
        return pl.BlockSpec((T, LANES), lambda j, t, qt, kt: (kt[t], base + j))

    pair = pl.BlockSpec((L, LANES), lambda j, t, qt, kt: (0, j))
    frow_spec = pl.BlockSpec((None, 2, T), lambda j, t, qt, kt: (j, 0, kt[t]))
    return pl.pallas_call(
        body, name=name,
        grid_spec=pltpu.PrefetchScalarGridSpec(
            num_scalar_prefetch=2, grid=(4, len(pairs)),
            in_specs=[qside(Q_BLK), kside(K_BLK), kside(V_BLK), qside(0), frow_spec, qside(0), qside(0), qside(4)],
            out_specs=[pair, kside(0), kside(0), pair, frow_spec],
            scratch_shapes=[pltpu.VMEM((T, LANES), F32), pltpu.VMEM((T, LANES), F32), pltpu.VMEM((2, T), F32)]),
        out_shape=[_sds((L, FOX_W)), _sds((L, FOX_W)), _sds((L, FOX_W)), _sds((L, FOX_W)), _sds((4, 2, L))],
        compiler_params=_cp("parallel", "arbitrary"),
    )(qt, kt, z, z, z, fq, frow, o, lse, g_m)


def _shift_rows(v, s, down, row):
    n = v.shape[0]
    if down:
        return jnp.where(row >= s, pltpu.roll(v, s, 0), 0.0)
    return jnp.where(row < n - s, pltpu.roll(v, n - s, 0), 0.0)


def _window_sum(v, g, down, row):
    out = jnp.zeros_like(v)
    s = v
    for k in range(4):
        s = s + _shift_rows(s, 1 << k, down, row)
        out = jnp.where(g == k, s, out)
    return out


def _pool_inv_cnt(g, row):
    w = jnp.left_shift(2, g).astype(F32)
    return 1.0 / jnp.minimum(row.astype(F32) + 1.0, w)


def _pool_fwd(z, pool_w, scale, name):
    L = z.shape[0]

    def body(x_ref, w_ref, s_ref, y_ref, p_ref):
        g = pl.program_id(0)
        row = lax.broadcasted_iota(jnp.int32, (L, LANES), 0)
        x = x_ref[...]
        pooled = (_window_sum(x, g, True, row) * _pool_inv_cnt(g, row) - x).astype(BF16)
        p_ref[...] = pooled
        y_ref[...] = (_dot(pooled, w_ref[...].astype(BF16)) * s_ref[...]).astype(BF16)

    col = pl.BlockSpec((L, LANES), lambda g: (0, g))
    return pl.pallas_call(
        body, name=name, grid=(4,),
        in_specs=[col, pl.BlockSpec((None, LANES, LANES), lambda g: (g, 0, 0)), pl.BlockSpec((1, LANES), lambda g: (0, g))],
        out_specs=[col, col],
        out_shape=[_sds((L, 512), BF16), _sds((L, 512), BF16)],
        compiler_params=_cp("parallel"),
    )(z, pool_w, scale)


def _pool_bwd(g_m, pooled, pool_w, scale, name):
    L = g_m.shape[0]

    def body(g_ref, p_ref, w_ref, s_ref, gx_ref, gw_ref, gs_ref):
        g = pl.program_id(0)
        row = lax.broadcasted_iota(jnp.int32, (L, LANES), 0)
        gy = g_ref[...]
        pooled = p_ref[...]
        wb = w_ref[...].astype(BF16)
        lin = _dot(pooled, wb)
        gs_ref[...] = jnp.sum(gy * lin, axis=0, keepdims=True)
        glin = (gy * s_ref[...]).astype(BF16)
        gw_ref[...] = _dot(pooled, glin, TN)
        gp = _dot(glin, wb, NT)
        gx_ref[...] = _window_sum(gp * _pool_inv_cnt(g, row), g, False, row) - gp

    col = pl.BlockSpec((L, LANES), lambda g: (0, g))
    wspec = pl.BlockSpec((None, LANES, LANES), lambda g: (g, 0, 0))
    vec = pl.BlockSpec((1, LANES), lambda g: (0, g))
    return pl.pallas_call(
        body, name=name, grid=(4,),
        in_specs=[col, col, wspec, vec],
        out_specs=[col, wspec, vec],
        out_shape=[_sds((L, 512)), _sds((4, LANES, LANES)), _sds((1, 512))],
        compiler_params=_cp("parallel"),
    )(g_m, pooled, pool_w, scale)


SGU_CHUNKS = 4


def _sgu_ln(v, gam, bet):
    gv = _gelu(v)
    mu = jnp.mean(gv, axis=-1, keepdims=True)
    xc = gv - mu
    rs = lax.rsqrt(jnp.mean(xc * xc, axis=-1, keepdims=True) + EPS)
    xh = xc * rs
    return xh, rs, xh * gam + bet


def _tril_ws(w_ref, g):
    r = lax.broadcasted_iota(jnp.int32, (LANES, LANES), 0)
    c = lax.broadcasted_iota(jnp.int32, (LANES, LANES), 1)
    return jnp.where(r >= c, w_ref[g], 0.0).astype(BF16)


def _sgu_fwd(z, ln_g, ln_b, w_s, b_st, name):
    L = z.shape[0]
    rb = min(SGU_CHUNKS * LANES, L)

    def body(u_ref, v_ref, g_ref, b_ref, w_ref, bs_ref, y_ref):
        _, _, vln = _sgu_ln(v_ref[...], g_ref[...], b_ref[...])
        gu = _gelu(u_ref[...])
        vb = vln.astype(BF16)
        for g in range(4):
            ws = _tril_ws(w_ref, g)
            for n in range(rb // LANES):
                rows = slice(n * LANES, (n + 1) * LANES)
                cols = slice(g * LANES, (g + 1) * LANES)
                mixed = _dot(ws, vb[rows, cols]) + bs_ref[:, g:g + 1]
                y_ref[rows, cols] = (gu[rows, cols] * mixed).astype(BF16)

    vm = lambda shape: pl.BlockSpec(shape, lambda i: tuple(0 for _ in shape))
    return pl.pallas_call(
        body, name=name, grid=(L // rb,),
        in_specs=[pl.BlockSpec((rb, 512), lambda i: (i, 1)), pl.BlockSpec((rb, 512), lambda i: (i, 2)),
                  vm((1, 512)), vm((1, 512)), vm((4, LANES, LANES)), vm((LANES, 4))],
        out_specs=pl.BlockSpec((rb, 512), lambda i: (i, 0)),
        out_shape=_sds((L, 512), BF16),
        compiler_params=_cp("parallel"),
    )(z, z, ln_g, ln_b, w_s, b_st)


def _sgu_bwd(g_m, z, ln_g, ln_b, w_s, b_st, name):
    L = z.shape[0]
    rb = min(SGU_CHUNKS * LANES, L)

    def body(gy_ref, u_ref, v_ref, g_ref, b_ref, w_ref, bs_ref, gu_ref, gv_ref, gw_ref, gbs_ref, gg_ref, gb_ref):
        i = pl.program_id(0)

        @pl.when(i == 0)
        def _():
            gw_ref[...] = jnp.zeros_like(gw_ref)
            gbs_ref[...] = jnp.zeros_like(gbs_ref)
            gg_ref[...] = jnp.zeros_like(gg_ref)
            gb_ref[...] = jnp.zeros_like(gb_ref)

        v = v_ref[...]
        u = u_ref[...]
        gy = gy_ref[...]
        xh, rs, vln = _sgu_ln(v, g_ref[...], b_ref[...])
        gel_u = _gelu(u)
        gmix = gy * gel_u
        vb = vln.astype(BF16)
        gmb = gmix.astype(BF16)
        r = lax.broadcasted_iota(jnp.int32, (LANES, LANES), 0)
        c = lax.broadcasted_iota(jnp.int32, (LANES, LANES), 1)
        gvln_cols = []
        for g in range(4):
            ws = _tril_ws(w_ref, g)
            cols = slice(g * LANES, (g + 1) * LANES)
            gw = jnp.zeros((LANES, LANES), F32)
            gbs = jnp.zeros((LANES, 1), F32)
            parts = []
            for n in range(rb // LANES):
                rows = slice(n * LANES, (n + 1) * LANES)
                mixed = _dot(ws, vb[rows, cols]) + bs_ref[:, g:g + 1]
                gu_ref[rows, cols] = gy[rows, cols] * mixed * _gelu_grad(u[rows, cols])
                parts.append(_dot(ws, gmb[rows, cols], TN))
                gw = gw + _dot(gmb[rows, cols], vb[rows, cols], NT)
                gbs = gbs + jnp.sum(gmix[rows, cols], axis=1, keepdims=True)
            gvln_cols.append(jnp.concatenate(parts, axis=0))
            gw_ref[g] += jnp.where(r >= c, gw, 0.0)
            gbs_ref[:, g:g + 1] += gbs
        gvln = jnp.concatenate(gvln_cols, axis=1)
        gg_ref[...] += jnp.sum(gvln * xh, axis=0, keepdims=True)
        gb_ref[...] += jnp.sum(gvln, axis=0, keepdims=True)
        gxh = gvln * g_ref[...]
        ggv = rs * (gxh - jnp.mean(gxh, axis=-1, keepdims=True) - xh * jnp.mean(gxh * xh, axis=-1, keepdims=True))
        gv_ref[...] = ggv * _gelu_grad(v)

    vm = lambda shape: pl.BlockSpec(shape, lambda i: tuple(0 for _ in shape))
    blk = pl.BlockSpec((rb, 512), lambda i: (i, 0))
    return pl.pallas_call(
        body, name=name, grid=(L // rb,),
        in_specs=[pl.BlockSpec((rb, 512), lambda i: (i, 1)), pl.BlockSpec((rb, 512), lambda i: (i, 1)),
                  pl.BlockSpec((rb, 512), lambda i: (i, 2)),
                  vm((1, 512)), vm((1, 512)), vm((4, LANES, LANES)), vm((LANES, 4))],
        out_specs=[blk, blk, vm((4, LANES, LANES)), vm((LANES, 4)), vm((1, 512)), vm((1, 512))],
        out_shape=[_sds((L, 512)), _sds((L, 512)), _sds((4, LANES, LANES)), _sds((LANES, 4)),
                   _sds((1, 512)), _sds((1, 512))],
        compiler_params=_cp("arbitrary"),
    )(g_m, z, z, ln_g, ln_b, w_s, b_st)


def _adamw_math(w, g, m, v):
    nm = ADAM_B1 * m + (1.0 - ADAM_B1) * g
    nv = ADAM_B2 * v + (1.0 - ADAM_B2) * (g * g)
    m_hat = nm / (1.0 - ADAM_B1 ** ADAM_STEP)
    v_hat = nv / (1.0 - ADAM_B2 ** ADAM_STEP)
    delta = -ADAM_LR * (m_hat / (jnp.sqrt(v_hat) + ADAM_EPS) + ADAM_WD * w)
    return delta, nm, nv


def _sum_adamw(parts, w, m, v, name, layer=0, prev=None):
    n_layers, R, C = w.shape
    rb = 128 if R % 128 == 0 else R

    def body(p_ref, w_ref, m_ref, v_ref, *rest):
        g_ref, d_ref, nm_ref, nv_ref = rest[-4:]
        g = p_ref[0].astype(F32)
        for s in range(1, N_DEV):
            g = g + p_ref[s].astype(F32)
        d, nm, nv = _adamw_math(w_ref[...], g, m_ref[...], v_ref[...])
        g_ref[...] = g
        d_ref[...] = d
        nm_ref[...] = nm
        nv_ref[...] = nv

    blk = pl.BlockSpec((None, rb, C), lambda i: (layer, i, 0))
    prev = [] if prev is None else list(prev)
    return pl.pallas_call(
        body, name=name, grid=(R // rb,),
        in_specs=[pl.BlockSpec((N_DEV, rb, C), lambda i: (0, i, 0)), blk, blk, blk] + [ANY] * len(prev),
        out_specs=[blk] * 4, out_shape=[_sds((n_layers, R, C))] * 4,
        input_output_aliases={4 + k: k for k in range(len(prev))},
        compiler_params=_cp("parallel"),
    )(parts, w, m, v, *prev)


def _sum_pieces(parts, name):
    _, R, C = parts.shape

    def body(p_ref, g_ref):
        g = p_ref[0].astype(F32)
        for s in range(1, N_DEV):
            g = g + p_ref[s].astype(F32)
        g_ref[...] = g

    vm = pl.BlockSpec(memory_space=pltpu.VMEM)
    return pl.pallas_call(body, name=name, in_specs=[vm], out_specs=vm, out_shape=_sds((R, C)),
                          compiler_params=pltpu.CompilerParams(vmem_limit_bytes=VMEM_LIMIT))(parts)


def _adamw_many(ws, gs, ms, vs, name):
    n = len(ws)
    vm = pl.BlockSpec(memory_space=pltpu.VMEM)

    def body(*refs):
        w_refs, g_refs, m_refs, v_refs = refs[:n], refs[n:2 * n], refs[2 * n:3 * n], refs[3 * n:4 * n]
        d_refs, nm_refs, nv_refs = refs[4 * n:5 * n], refs[5 * n:6 * n], refs[6 * n:7 * n]
        for i in range(n):
            d, nm, nv = _adamw_math(w_refs[i][...], g_refs[i][...], m_refs[i][...], v_refs[i][...])
            d_refs[i][...] = d
            nm_refs[i][...] = nm
            nv_refs[i][...] = nv

    shapes = [_sds(w.shape) for w in ws]
    outs = pl.pallas_call(
        body, name=name, in_specs=[vm] * (4 * n), out_specs=[vm] * (3 * n), out_shape=shapes * 3,
        compiler_params=pltpu.CompilerParams(vmem_limit_bytes=VMEM_LIMIT),
    )(*ws, *gs, *ms, *vs)
    return list(outs[:n]), list(outs[n:2 * n]), list(outs[2 * n:])


def _mesh_pos():
    return lax.axis_index("x"), lax.axis_index("y"), lax.axis_index("c")


def _dev_index(p):
    return 4 * p[0] + 2 * p[1] + p[2]


HBM = pl.BlockSpec(memory_space=pltpu.HBM)
SEM = pl.BlockSpec(memory_space=pltpu.SEMAPHORE)
EFFECT = pltpu.SideEffectType.DATAFLOW_SIDE_EFFECTING


def _peer_list():
    x, y, c = _mesh_pos()
    peers = [(x ^ dx, y ^ dy, c ^ dc) for dx in range(2) for dy in range(2) for dc in range(2)][1:]
    return (x, y, c), peers


def _split_copy(src_ref, land_ref, send_sems, recv_sems, i, k, peer, slot, exchange):
    return pltpu.make_async_remote_copy(
        src_ref=src_ref.at[_dev_index(peer)] if exchange else src_ref, dst_ref=land_ref.at[slot],
        send_sem=send_sems.at[7 * i + k], recv_sem=recv_sems.at[7 * i + k], device_id=peer, device_id_type=MESH)


def _comm_start(groups, name, exchange, dep=None):
    sizes = [len(g) for g in groups]
    n = sum(sizes)
    srcs = [a for g in groups for a in g]
    my_index = _dev_index(_mesh_pos())
    lands = []
    for a in srcs:
        if exchange:
            own = lax.dynamic_slice(a, (my_index, 0, 0), (1,) + a.shape[1:])
            shape = a.shape
        else:
            own = a[None]
            shape = (N_DEV,) + a.shape
        lands.append(lax.dynamic_update_slice(lax.empty(shape, a.dtype), own, (my_index, 0, 0)))

    n_dep = 0 if dep is None else 1

    def body(*refs):
        src_refs, land_refs = refs[:n], refs[n:2 * n]
        sem_refs = refs[2 * n + n_dep:2 * n + n_dep + 2 * len(sizes)]
        token_ref = refs[-1]
        me, peers = _peer_list()
        mi = _dev_index(me)
        i = 0
        for gi, sz in enumerate(sizes):
            for j in range(sz):
                for k, peer in enumerate(peers):
                    _split_copy(src_refs[i], land_refs[i], sem_refs[2 * gi], sem_refs[2 * gi + 1], j, k, peer, mi,
                                exchange).start()
                i += 1
        token_ref[...] = jnp.zeros_like(token_ref)

    sem_shapes = []
    for sz in sizes:
        sem_shapes += [pltpu.SemaphoreType.DMA((7 * sz,)), pltpu.SemaphoreType.DMA((7 * sz,))]
    thru = [pltpu.HBM(a.shape, a.dtype) for a in srcs + lands]
    n_sem = len(sem_shapes)
    outs = pl.pallas_call(
        body, name=name,
        out_shape=tuple(sem_shapes + thru + [_sds((8, LANES))]),
        in_specs=[HBM] * (2 * n) + [ANY] * n_dep,
        out_specs=tuple([SEM] * n_sem + [HBM] * (2 * n) + [pl.BlockSpec(memory_space=pltpu.VMEM)]),
        input_output_aliases={i: n_sem + i for i in range(2 * n)},
        compiler_params=pltpu.CompilerParams(has_side_effects=EFFECT),
    )(*[pltpu.with_memory_space_constraint(a, pltpu.HBM) for a in srcs + lands], *([] if dep is None else [dep]))
    sems, thru_src, thru_land, token = outs[:n_sem], outs[n_sem:n_sem + n], outs[n_sem + n:n_sem + 2 * n], outs[-1]
    result, off = [], 0
    for gi, sz in enumerate(sizes):
        result.append((sems[2 * gi], sems[2 * gi + 1], list(thru_src[off:off + sz]), list(thru_land[off:off + sz])))
        off += sz
    return result, token


def _comm_wait(group, after, name, exchange):
    send_sems, recv_sems, srcs, lands = group
    n = len(srcs)
    after = list(after) if isinstance(after, (list, tuple)) else [after]

    def body(*refs):
        src_refs, land_refs = refs[:n], refs[n:2 * n]
        ssem, rsem = refs[2 * n], refs[2 * n + 1]
        me, peers = _peer_list()
        for i in range(n):
            for k, peer in enumerate(peers):
                cp = _split_copy(src_refs[i], land_refs[i], ssem, rsem, i, k, peer, _dev_index(peer), exchange)
                cp.wait_send()
                cp.wait_recv()

    outs = pl.pallas_call(
        body, name=name,
        out_shape=tuple(pltpu.HBM(a.shape, a.dtype) for a in srcs + lands),
        in_specs=[HBM] * (2 * n) + [SEM, SEM] + [ANY] * len(after),
        out_specs=tuple([HBM] * (2 * n)),
        input_output_aliases={i: i for i in range(2 * n)},
        compiler_params=pltpu.CompilerParams(has_side_effects=EFFECT),
    )(*srcs, *lands, send_sems, recv_sems, *after)
    return list(outs[n:])


def _tie(a, token):
    return a + token[0, 0].astype(a.dtype)


def _pack(arrs, rows):
    flat = jnp.concatenate([a.reshape(-1).astype(F32) for a in arrs])
    return jnp.pad(flat, (0, rows * LANES - flat.shape[0])).reshape(rows, LANES)


def _unpack(packed, shapes):
    flat = packed.reshape(-1)
    out, off = [], 0
    for s in shapes:
        n = math.prod(s)
        out.append(flat[off:off + n].reshape(s))
        off += n
    return out


def _packed_rows(shapes):
    n = sum(math.prod(s) for s in shapes)
    unit = N_DEV * 8 * LANES
    return -(-n // unit) * unit // LANES


def kernel(x, mix_pre_g, mix_post_g, mlp_pre_g, mlp_post_g, w_in_even, s5_lam_re, s5_lam_im, s5_log_dt, s5_b_re, s5_b_im, s5_c_re, s5_c_im, s5_d, s5_w_glu, fox_b_f, w_out_even, w_in_odd, pool_w, pool_scale, sgu_ln_g, sgu_ln_b, sgu_w_s, sgu_b_s, w_out_odd, mlp_w1, mlp_w2, loss_target, m_mix_pre_g, m_mix_post_g, m_mlp_pre_g, m_mlp_post_g, m_w_in_even, m_s5_lam_re, m_s5_lam_im, m_s5_log_dt, m_s5_b_re, m_s5_b_im, m_s5_c_re, m_s5_c_im, m_s5_d, m_s5_w_glu, m_fox_b_f, m_w_out_even, m_w_in_odd, m_pool_w, m_pool_scale, m_sgu_ln_g, m_sgu_ln_b, m_sgu_w_s, m_sgu_b_s, m_w_out_odd, m_mlp_w1, m_mlp_w2, v_mix_pre_g, v_mix_post_g, v_mlp_pre_g, v_mlp_post_g, v_w_in_even, v_s5_lam_re, v_s5_lam_im, v_s5_log_dt, v_s5_b_re, v_s5_b_im, v_s5_c_re, v_s5_c_im, v_s5_d, v_s5_w_glu, v_fox_b_f, v_w_out_even, v_w_in_odd, v_pool_w, v_pool_scale, v_sgu_ln_g, v_sgu_ln_b, v_sgu_w_s, v_sgu_b_s, v_w_out_odd, v_mlp_w1, v_mlp_w2):
    weights = dict(mix_pre_g=mix_pre_g, mix_post_g=mix_post_g, mlp_pre_g=mlp_pre_g, mlp_post_g=mlp_post_g, w_in_even=w_in_even, s5_lam_re=s5_lam_re, s5_lam_im=s5_lam_im, s5_log_dt=s5_log_dt, s5_b_re=s5_b_re, s5_b_im=s5_b_im, s5_c_re=s5_c_re, s5_c_im=s5_c_im, s5_d=s5_d, s5_w_glu=s5_w_glu, fox_b_f=fox_b_f, w_out_even=w_out_even, w_in_odd=w_in_odd, pool_w=pool_w, pool_scale=pool_scale, sgu_ln_g=sgu_ln_g, sgu_ln_b=sgu_ln_b, sgu_w_s=sgu_w_s, sgu_b_s=sgu_b_s, w_out_odd=w_out_odd, mlp_w1=mlp_w1, mlp_w2=mlp_w2)
    mom_m = dict(mix_pre_g=m_mix_pre_g, mix_post_g=m_mix_post_g, mlp_pre_g=m_mlp_pre_g, mlp_post_g=m_mlp_post_g, w_in_even=m_w_in_even, s5_lam_re=m_s5_lam_re, s5_lam_im=m_s5_lam_im, s5_log_dt=m_s5_log_dt, s5_b_re=m_s5_b_re, s5_b_im=m_s5_b_im, s5_c_re=m_s5_c_re, s5_c_im=m_s5_c_im, s5_d=m_s5_d, s5_w_glu=m_s5_w_glu, fox_b_f=m_fox_b_f, w_out_even=m_w_out_even, w_in_odd=m_w_in_odd, pool_w=m_pool_w, pool_scale=m_pool_scale, sgu_ln_g=m_sgu_ln_g, sgu_ln_b=m_sgu_ln_b, sgu_w_s=m_sgu_w_s, sgu_b_s=m_sgu_b_s, w_out_odd=m_w_out_odd, mlp_w1=m_mlp_w1, mlp_w2=m_mlp_w2)
    mom_v = dict(mix_pre_g=v_mix_pre_g, mix_post_g=v_mix_post_g, mlp_pre_g=v_mlp_pre_g, mlp_post_g=v_mlp_post_g, w_in_even=v_w_in_even, s5_lam_re=v_s5_lam_re, s5_lam_im=v_s5_lam_im, s5_log_dt=v_s5_log_dt, s5_b_re=v_s5_b_re, s5_b_im=v_s5_b_im, s5_c_re=v_s5_c_re, s5_c_im=v_s5_c_im, s5_d=v_s5_d, s5_w_glu=v_s5_w_glu, fox_b_f=v_fox_b_f, w_out_even=v_w_out_even, w_in_odd=v_w_in_odd, pool_w=v_pool_w, pool_scale=v_pool_scale, sgu_ln_g=v_sgu_ln_g, sgu_ln_b=v_sgu_ln_b, sgu_w_s=v_sgu_w_s, sgu_b_s=v_sgu_b_s, w_out_odd=v_w_out_odd, mlp_w1=v_mlp_w1, mlp_w2=v_mlp_w2)
    names = list(weights)
    L = x.shape[1]
    x0 = x[0]
    target = loss_target[0]
    my_index = 4 * lax.axis_index("x") + 2 * lax.axis_index("y") + lax.axis_index("c")

    small_vec = jnp.zeros((8, LANES), F32)
    small_vec = small_vec.at[0, :64].set(pool_scale[0]).at[1, :64].set(sgu_ln_g[0]).at[2, :64].set(sgu_ln_b[0])
    ag_groups, ag_token = _comm_start(
        [[jnp.transpose(w_in_even[0]).astype(BF16), small_vec],
         [s5_w_glu[0].astype(BF16), w_out_even[0].astype(BF16)],
         [mlp_w1[0].astype(BF16), mlp_w2[0].astype(BF16)],
         [jnp.transpose(w_in_odd[0]).astype(BF16), w_out_odd[0].astype(BF16), mlp_w1[1].astype(BF16), mlp_w2[1].astype(BF16)]],
        "ag_start", exchange=False)

    lam_r = jnp.concatenate([s5_lam_re.reshape(1, S5_NS), s5_lam_im.reshape(1, S5_NS)], axis=0)
    ldt_r = jnp.repeat(s5_log_dt.reshape(32), 64).reshape(1, S5_NS)
    lam_c = jnp.transpose(lam_r)
    ldt_c = jnp.transpose(ldt_r)
    b_t = jnp.stack([jnp.tile(s5_b_re.reshape(S5_NS, 16), (1, 8)), jnp.tile(s5_b_im.reshape(S5_NS, 16), (1, 8))])
    c_t = jnp.stack([jnp.tile(s5_c_re.reshape(S5_W, 64), (1, 8)), jnp.tile(s5_c_im.reshape(S5_W, 64), (1, 8))])
    bf_pad = jnp.pad(fox_b_f, ((0, 0), (0, LANES - 8)))
    b_st = jnp.transpose(sgu_b_s[0])

    h0, rx0 = _rms_fwd(x0, _tie(mix_pre_g[0:1], ag_token), "rms0")
    tabs, bset, cset = _s5_prep(lam_r, ldt_r, lam_c, ldt_c, b_t, c_t, "s5_prep")
    ag0 = _comm_wait(ag_groups[0], tabs, "ag_wait0", exchange=False)
    winT_e = jnp.pad(ag0[0].reshape(EVEN_IN, D_MODEL), ((0, EVEN_PAD - EVEN_IN), (0, 0)))
    pool_scale_f = ag0[1][:, 0, :64].reshape(1, 512)
    ln_g_f = ag0[1][:, 1, :64].reshape(1, 512)
    ln_b_f = ag0[1][:, 2, :64].reshape(1, 512)
    z0 = _mm(h0, winT_e, name="win_even", tb=True, bm=512, bn=EVEN_PAD)
    xs = _s5_scan_fwd(z0, bset, tabs, "s5_scan")
    ag1 = _comm_wait(ag_groups[1], xs, "ag_wait1", exchange=False)
    wglu = ag1[0].reshape(S5_W, S5_W)
    wout_e = ag1[1].reshape(D_MODEL, D_MODEL)
    ylin, ya = _s5_out_fwd(xs, cset, z0, s5_d, wglu, "s5_out")
    fcum, fq = _fox_f_fwd(z0, bf_pad, "fox_f")
    frow = jnp.transpose(fcum[:, :8]).reshape(4, 2, L)
    o_att, lse = _fox_fwd(z0, fq, frow, "fox_fwd")
    mix0 = [ya, o_att]
    x1, ry0, h1, rx1, y0 = _mm(mix0, wout_e, name="wout_even", epi=_epi_post_pre, extra=(x0,),
                               vecs=(mix_post_g[0:1], mlp_pre_g[0:1]), out_dtypes=POST_PRE_DTYPES,
                               out_kinds=POST_PRE_KINDS, bm=FUSED_ROWS)
    ag2 = _comm_wait(ag_groups[2], rx1, "ag_wait2", exchange=False)
    w1 = [ag2[0], None]
    w2 = [ag2[1].reshape(4 * D_MODEL, D_MODEL), None]
    p0, a0 = _mm(h1, w1[0], name="mlp0_w1", b3=True, out_dtypes=(BF16, BF16), epi=_epi_relu2, bm=512, bn=4 * D_MODEL)
    x2, ro0, h2, rx2, o0 = _mm(a0, w2[0], name="mlp0_w2", epi=_epi_post_pre, extra=(x1,),
                               vecs=(mlp_post_g[0:1], mix_pre_g[1:2]), out_dtypes=POST_PRE_DTYPES,
                               out_kinds=POST_PRE_KINDS, bm=FUSED_ROWS, bk=4 * D_MODEL)
    ag3 = _comm_wait(ag_groups[3], rx2, "ag_wait3", exchange=False)
    winT_o = ag3[0].reshape(ODD_IN, D_MODEL)
    wout_o = ag3[1].reshape(D_MODEL, D_MODEL)
    w1[1] = ag3[2]
    w2[1] = ag3[3].reshape(4 * D_MODEL, D_MODEL)
    z1 = _mm(h2, winT_o, name="win_odd", tb=True, bn=ODD_IN)
    yc, pooled = _pool_fwd(z1, pool_w[0], pool_scale_f, "pool_fwd")
    yd = _sgu_fwd(z1, ln_g_f, ln_b_f, sgu_w_s[0], b_st, "sgu_fwd")
    mix1 = [yc, yd]
    x3, ry1, h3, rx3, y1 = _mm(mix1, wout_o, name="wout_odd", epi=_epi_post_pre, extra=(x2,),
                               vecs=(mix_post_g[1:2], mlp_pre_g[1:2]), out_dtypes=POST_PRE_DTYPES,
                               out_kinds=POST_PRE_KINDS, bm=FUSED_ROWS)
    p1, a1 = _mm(h3, w1[1], name="mlp1_w1", b3=True, out_dtypes=(BF16, BF16), epi=_epi_relu2, bm=512, bn=4 * D_MODEL)
    gx4, g_o1, gg_mlp_post1, sq_lanes = _mm(
        a1, w2[1], name="mlp1_w2", epi=_epi_post_loss, extra=(x3, target), vecs=(mlp_post_g[1:2],),
        out_dtypes=(F32, BF16, F32, F32), out_kinds=("full", "full", "vsum", "vsum"), bm=FUSED_ROWS, bk=4 * D_MODEL)
    sq = sq_lanes[:, 0:1]

    g_p1 = _mm(g_o1, w2[1], name="b_mlp1_a", tb=True, out_dtypes=(BF16,), epi=_epi_relu2_bwd, extra=(p1,),
               bm=512, bn=4 * D_MODEL)
    gw2_1 = _mm(a1, g_o1, name="b_mlp1_w2", ta=True, bm=512, bk=L)
    gw1_1 = _mm(h3, g_p1, name="b_mlp1_w1", ta=True, out3=True, bn=512, bk=L)
    (ex1,), tok1 = _comm_start([[gw1_1, gw2_1.reshape(N_DEV, 512, D_MODEL)]], "ex_start1", exchange=True)
    g_x3, gg_mlp_pre1, g_y1, gg_mix_post1 = _mm(
        g_p1, w1[1], name="b_mlp1_h", tb=True, b3=True, epi=_epi_pre_post_bwd, extra=(x3, gx4, y1), cols=(rx3, ry1),
        vecs=(_tie(mlp_pre_g[1:2], tok1), mix_post_g[1:2]), out_dtypes=PRE_POST_BWD_DTYPES,
        out_kinds=PRE_POST_BWD_KINDS, bm=FUSED_ROWS, bk=4 * D_MODEL)
    g_mix1 = _mm(g_y1, wout_o, name="b_wout_odd_m", tb=True)
    gwout_o = _mm(mix1, g_y1, name="b_wout_odd_w", ta=True)
    g_xc, g_pool_w, g_pool_scale = _pool_bwd(g_mix1, pooled, pool_w[0], pool_scale_f, "pool_bwd")
    g_u1, g_v1, g_ws, g_bst, g_ln_g, g_ln_b = _sgu_bwd(g_mix1, z1, ln_g_f, ln_b_f, sgu_w_s[0], b_st, "sgu_bwd")
    g_z1 = [g_xc, g_u1, g_v1]
    gwinT_o = _mm(g_z1, h2, name="b_win_odd_w", ta=True)
    (ex2,), tok2 = _comm_start([[gwout_o.reshape(N_DEV, 128, D_MODEL), gwinT_o.reshape(N_DEV, ODD_IN // N_DEV, D_MODEL)]], "ex_start2", exchange=True)
    g_x2, gg_mix_pre1, g_o0, gg_mlp_post0 = _mm(
        g_z1, winT_o, name="b_win_odd_h", epi=_epi_pre_post_bwd, extra=(x2, g_x3, o0), cols=(rx2, ro0),
        vecs=(_tie(mix_pre_g[1:2], tok2), mlp_post_g[0:1]), out_dtypes=PRE_POST_BWD_DTYPES,
        out_kinds=PRE_POST_BWD_KINDS, bm=FUSED_ROWS)
    g_p0 = _mm(g_o0, w2[0], name="b_mlp0_a", tb=True, out_dtypes=(BF16,), epi=_epi_relu2_bwd, extra=(p0,),
               bm=512, bn=4 * D_MODEL)
    gw2_0 = _mm(a0, g_o0, name="b_mlp0_w2", ta=True, bm=512, bk=L)
    gw1_0 = _mm(h1, g_p0, name="b_mlp0_w1", ta=True, out3=True, bn=512, bk=L)
    (ex3,), tok3 = _comm_start([[gw1_0, gw2_0.reshape(N_DEV, 512, D_MODEL)]], "ex_start3", exchange=True)
    g_x1, gg_mlp_pre0, g_y0, gg_mix_post0 = _mm(
        g_p0, w1[0], name="b_mlp0_h", tb=True, b3=True, epi=_epi_pre_post_bwd, extra=(x1, g_x2, y0), cols=(rx1, ry0),
        vecs=(_tie(mlp_pre_g[0:1], tok3), mix_post_g[0:1]), out_dtypes=PRE_POST_BWD_DTYPES,
        out_kinds=PRE_POST_BWD_KINDS, bm=FUSED_ROWS, bk=4 * D_MODEL)
    g_mix0 = _mm(g_y0, wout_e, name="b_wout_even_m", tb=True)
    gwout_e = _mm(mix0, g_y0, name="b_wout_even_w", ta=True)
    gyl, gud, g_wglu, g_d = _s5_glu_bwd(g_mix0, ylin, z0, s5_d, wglu, "s5_glu_bwd")
    (ex4,), tok4 = _comm_start([[gwout_e.reshape(N_DEV, 128, D_MODEL), g_wglu.reshape(N_DEV, 64, S5_W)]], "ex_start4", exchange=True)
    g_u0, ga, gb_raw, gc_raw = _s5_scan_bwd(gyl, _tie(cset, tok4), xs, z0, bset, gud, tabs, "s5_scan_bwd")
    g_lam, g_ldt, g_b, g_c = _s5_param_bwd(lam_c, ldt_c, b_t, gb_raw, jnp.transpose(ga), gc_raw, "s5_param_bwd")
    dq, dk, dv, dfq, dfrow = _fox_bwd(z0, fq, frow, o_att, lse, g_mix0, "fox_bwd")
    dFk = jnp.pad(jnp.transpose(dfrow.reshape(8, L)), ((0, 0), (0, LANES - 8)))
    dfl, db_f = _fox_f_bwd(dFk, dfq, z0, bf_pad, "fox_f_bwd")
    g_z0 = [g_u0, dq, dk, dv, dfl]
    grad_x, gg_mix_pre0 = _mm(g_z0, winT_e, name="b_win_even_h", epi=_epi_pre_bwd, extra=(x0, g_x1), cols=(rx0,),
                              vecs=(mix_pre_g[0:1],), out_dtypes=(F32, F32), out_kinds=("full", "vsum"),
                              bm=FUSED_ROWS)

    small_grads = dict(
        mix_pre_g=jnp.concatenate([gg_mix_pre0, gg_mix_pre1]), mix_post_g=jnp.concatenate([gg_mix_post0, gg_mix_post1]),
        mlp_pre_g=jnp.concatenate([gg_mlp_pre0, gg_mlp_pre1]), mlp_post_g=jnp.concatenate([gg_mlp_post0, gg_mlp_post1]),
        s5_lam_re=g_lam[:, 0], s5_lam_im=g_lam[:, 1], s5_log_dt=g_ldt,
        s5_b_re=g_b[0, :, :16], s5_b_im=g_b[1, :, :16], s5_c_re=g_c[0, :, :64], s5_c_im=g_c[1, :, :64],
        s5_d=g_d, fox_b_f=db_f[:, :8], pool_w=g_pool_w, sgu_w_s=g_ws, sgu_b_s=jnp.transpose(g_bst),
        pool_scale=g_pool_scale, sgu_ln_g=g_ln_g, sgu_ln_b=g_ln_b)
    small_names = list(small_grads)
    full_shapes = [(512,) if nm in ("pool_scale", "sgu_ln_g", "sgu_ln_b") else weights[nm].shape for nm in small_names]
    full_shapes.append((1, 1))
    rows = _packed_rows(full_shapes)
    packed = _pack([small_grads[nm] for nm in small_names] + [sq], rows).reshape(N_DEV, rows // N_DEV, LANES)
    (exs,), tok_s = _comm_start([[packed]], "exs_start", exchange=True)
    gwinT_e = _mm(g_z0, h0, name="b_win_even_w", ta=True, bk=512, out_dtypes=(BF16,), dep=tok_s)
    (recv_small,) = _comm_wait(exs, gwinT_e, "exs_wait", exchange=True)
    piece = _sum_pieces(recv_small, "sum_small")
    (ags,), tok_a = _comm_start([[piece]], "ags_start", exchange=False)

    gwinT_e_pieces = gwinT_e[:EVEN_IN].reshape(N_DEV, EVEN_IN // N_DEV, D_MODEL)
    (ex5,), tok5 = _comm_start([[gwinT_e_pieces]], "ex_start5", exchange=True, dep=tok_a)
    r_w1_1, r_w2_1 = _comm_wait(ex1, tok5, "ex_wait1", exchange=True)
    r_wout_o, r_win_o = _comm_wait(ex2, tok5, "ex_wait2", exchange=True)
    r_w1_0, r_w2_0 = _comm_wait(ex3, tok5, "ex_wait3", exchange=True)
    r_wout_e, r_wglu = _comm_wait(ex4, tok5, "ex_wait4", exchange=True)

    res = {}
    for nm, parts in (("mlp_w1", (r_w1_0, r_w1_1)), ("mlp_w2", (r_w2_0, r_w2_1))):
        first = _sum_adamw(parts[0], weights[nm], mom_m[nm], mom_v[nm], "adamw_%s_0" % nm, layer=0)
        res[nm] = tuple(_sum_adamw(parts[1], weights[nm], mom_m[nm], mom_v[nm], "adamw_%s_1" % nm, layer=1, prev=first))
    big_parts = dict(s5_w_glu=r_wglu, w_out_even=r_wout_e, w_out_odd=r_wout_o)
    for nm, parts in big_parts.items():
        res[nm] = tuple(_sum_adamw(parts, weights[nm], mom_m[nm], mom_v[nm], "adamw_" + nm))
    done = [res[nm][1] for nm in ("mlp_w1", "mlp_w2", "s5_w_glu", "w_out_even", "w_out_odd")]

    (small_all,) = _comm_wait(ags, done, "ags_wait", exchange=False)
    small_full = _unpack(small_all.reshape(rows, LANES), full_shapes)
    loss = 0.5 * small_full.pop()[0, 0] / D_MODEL
    small_g = []
    for nm, g in zip(small_names, small_full):
        if nm in ("pool_scale", "sgu_ln_g", "sgu_ln_b"):
            g = lax.dynamic_slice(g, (my_index * 64,), (64,)).reshape(1, 64)
        small_g.append(g)
    sd, sm, sv = _adamw_many([weights[nm] for nm in small_names], small_g, [mom_m[nm] for nm in small_names],
                             [mom_v[nm] for nm in small_names], "adamw_small")
    for nm, g_, d_, m_, v_ in zip(small_names, small_g, sd, sm, sv):
        res[nm] = (g_, d_, m_, v_)
    done.append(sd[0])

    for nm, parts in (("w_in_odd", r_win_o), ("w_in_even", None)):
        if parts is None:
            (parts,) = _comm_wait(ex5, done, "ex_wait5", exchange=True)
        outs = _sum_adamw(parts, jnp.transpose(weights[nm], (0, 2, 1)), jnp.transpose(mom_m[nm], (0, 2, 1)),
                          jnp.transpose(mom_v[nm], (0, 2, 1)), "adamw_" + nm)
        res[nm] = tuple(jnp.transpose(o, (0, 2, 1)) for o in outs)
        done.append(res[nm][1])

    grads = [res[nm][0].reshape(weights[nm].shape) for nm in names]
    deltas = [res[nm][1].reshape(weights[nm].shape) for nm in names]
    new_m = [res[nm][2].reshape(weights[nm].shape) for nm in names]
    new_v = [res[nm][3].reshape(weights[nm].shape) for nm in names]
    return (loss, grad_x[None], *grads, *deltas, *new_m, *new_v)
```

```python
import functools
import math

import jax
import jax.numpy as jnp
from jax import lax
from jax.experimental import pallas as pl
from jax.experimental.pallas import tpu as pltpu

F32 = jnp.float32
BF16 = jnp.bfloat16
MESH = pl.DeviceIdType.MESH
ANY = pl.BlockSpec(memory_space=pl.ANY)

N_DEV = 8
D_MODEL = 1024
EPS = 1e-6
NORM_ROWS = 512
FUSED_ROWS = 512
S5_W = 512
S5_NS = 2048
SCAN_GROUPS = 4
SCAN_CHUNK = 1024
FOX_W = 512
EVEN_IN = 2056
EVEN_PAD = 2176
ODD_IN = 1536
LANES = 128
PIECE = 4 * D_MODEL // N_DEV
VMEM_LIMIT = 56 * 1024 * 1024

ADAM_LR = 0.001
ADAM_B1 = 0.9
ADAM_B2 = 0.999
ADAM_EPS = 1e-08
ADAM_WD = 0.01
ADAM_STEP = 10

NT = (((1,), (1,)), ((), ()))
TN = (((0,), (0,)), ((), ()))
NN = (((1,), (0,)), ((), ()))


def _cp(*sem):
    return pltpu.CompilerParams(dimension_semantics=sem, vmem_limit_bytes=VMEM_LIMIT)


def _sds(shape, dtype=F32):
    return jax.ShapeDtypeStruct(tuple(shape), dtype)


def _gelu(x):
    t = jnp.tanh(0.7978845608028654 * (x + 0.044715 * x * x * x))
    return 0.5 * x * (1.0 + t)


def _gelu_grad(x):
    t = jnp.tanh(0.7978845608028654 * (x + 0.044715 * x * x * x))
    du = 0.7978845608028654 * (1.0 + 3.0 * 0.044715 * x * x)
    return 0.5 * (1.0 + t) + 0.5 * x * (1.0 - t * t) * du


def _sigmoid(x):
    return 1.0 / (1.0 + jnp.exp(-x))


def _dot(a, b, dn=NN):
    return lax.dot_general(a, b, dn, preferred_element_type=F32)


def _mm(a, b, *, name, ta=False, tb=False, b3=False, out3=False, out_dtypes=(F32,), epi=None, extra=(),
        cols=(), vecs=(), out_kinds=None, bm=1024, bn=1024, bk=1024, dep=None):
    a_list = list(a) if isinstance(a, (list, tuple)) else [a]
    widths = [p.shape[1] for p in a_list]
    offs = [sum(widths[:i]) for i in range(len(widths))]
    na = len(a_list)
    M = sum(widths) if ta else a_list[0].shape[0]
    K = a_list[0].shape[0] if ta else sum(widths)
    if na > 1:
        assert not b3 and not tb
        bm, bk = (M, bk) if ta else (bm, K)
    pw = b.shape[2] if b3 else PIECE
    if b3:
        N = b.shape[1] if tb else b.shape[0] * pw
        assert (b.shape[0] * pw if tb else b.shape[1]) == K
    else:
        N = b.shape[0] if tb else b.shape[1]
    bm, bn, bk = min(bm, M), min(bn, N), min(bk, K)
    assert M % bm == 0 and N % bn == 0 and K % bk == 0, (name, M, N, K, bm, bn, bk)
    assert not (b3 or out3) or ((bk if tb else bn) % pw == 0 and bn % PIECE == 0)
    nk = K // bk
    n_extra = len(extra) + len(cols) + len(vecs)
    n_out = len(out_dtypes)
    out_kinds = tuple(out_kinds) if out_kinds is not None else ("full",) * n_out
    dn = (((0 if ta else 1,), (1 if tb else 0,)), ((), ()))

    use_acc = nk > 1

    def body(*refs):
        a_refs, b_ref = refs[:na], refs[na]
        a_ref = a_refs[0]
        e_refs = refs[na + 1:na + 1 + n_extra]
        first_out = na + 1 + n_extra + (0 if dep is None else 1)
        o_refs = refs[first_out:first_out + n_out]
        acc_ref = refs[-1] if use_acc else o_refs[0]
        i, k = pl.program_id(0), pl.program_id(2)

        def dot(a_v, b_v):
            return lax.dot_general(a_v.astype(BF16), b_v.astype(BF16), dn, preferred_element_type=F32)

        everything = slice(None)
        if na > 1 and ta:
            terms = [(pl.ds(off, w), everything, r, b_ref) for r, off, w in zip(a_refs, offs, widths)]
        elif na > 1:
            terms = [(everything, everything, r, b_ref.at[pl.ds(off, w), :]) for r, off, w in zip(a_refs, offs, widths)]
        elif not b3:
            terms = [(everything, everything, a_ref, b_ref)]
        elif tb:
            terms = [(everything, everything,
                      a_ref.at[pl.ds(t * pw, pw), :] if ta else a_ref.at[:, pl.ds(t * pw, pw)], b_ref.at[t])
                     for t in range(bk // pw)]
        else:
            terms = [(everything, pl.ds(t * pw, pw), a_ref, b_ref.at[t]) for t in range(bn // pw)]

        def finish(acc):
            outs = (acc,) if epi is None else epi(acc, *[e[...] for e in e_refs])
            for o_ref, o, kind in zip(o_refs, outs, out_kinds):
                if kind == "vsum":
                    @pl.when(i == 0)
                    def _(o_ref=o_ref, o=o):
                        o_ref[...] = o

                    @pl.when(i > 0)
                    def _(o_ref=o_ref, o=o):
                        o_ref[...] += o
                elif out3:
                    for t in range(bn // PIECE):
                        o_ref[t] = o[:, t * PIECE:(t + 1) * PIECE].astype(o_ref.dtype)
                else:
                    o_ref[...] = o.astype(o_ref.dtype)

        if nk == 1:
            bands = {}
            for rows, cols, a_r, b_r in terms:
                key = (getattr(rows, "start", None), getattr(cols, "start", None))
                val = dot(a_r[...], b_r[...])
                bands[key] = val if key not in bands else bands[key] + val
            vals = list(bands.values())
            if len(vals) == 1:
                finish(vals[0])
            else:
                finish(jnp.concatenate(vals, axis=0 if (na > 1 and ta) else 1))
            return

        @pl.when(k == 0)
        def _():
            acc_ref[...] = jnp.zeros_like(acc_ref)

        for rows, cols, a_r, b_r in terms:
            acc_ref[rows, cols] += dot(a_r[...], b_r[...])

        @pl.when(k == nk - 1)
        def _():
            finish(acc_ref[...])

    if na > 1:
        a_specs = [pl.BlockSpec((bk, w), lambda i, j, k: (k, 0)) if ta else pl.BlockSpec((bm, w), lambda i, j, k: (i, 0))
                   for w in widths]
    else:
        a_specs = [pl.BlockSpec((bk, bm), lambda i, j, k: (k, i)) if ta else
                   pl.BlockSpec((bm, bk), lambda i, j, k: (i, k))]
    if b3:
        if tb:
            b_spec = pl.BlockSpec((bk // pw, bn, pw), lambda i, j, k: (k, j, 0))
        else:
            b_spec = pl.BlockSpec((bn // pw, bk, pw), lambda i, j, k: (j, k, 0))
    else:
        b_spec = pl.BlockSpec((bn, bk), lambda i, j, k: (j, k)) if tb else pl.BlockSpec((bk, bn), lambda i, j, k: (k, j))
    e_specs = ([pl.BlockSpec((bm, bn), lambda i, j, k: (i, j)) for _ in extra]
               + [pl.BlockSpec((bm, 1), lambda i, j, k: (i, 0)) for _ in cols]
               + [pl.BlockSpec((1, bn), lambda i, j, k: (0, j)) for _ in vecs])
    if out3:
        o_specs = [pl.BlockSpec((bn // PIECE, bm, PIECE), lambda i, j, k: (j, i, 0)) for _ in out_dtypes]
        o_shapes = [_sds((N // PIECE, M, PIECE), dt) for dt in out_dtypes]
    else:
        spec_of = {"full": pl.BlockSpec((bm, bn), lambda i, j, k: (i, j)),
                   "col": pl.BlockSpec((bm, 1), lambda i, j, k: (i, 0)),
                   "vsum": pl.BlockSpec((1, bn), lambda i, j, k: (0, j))}
        shape_of = {"full": (M, N), "col": (M, 1), "vsum": (1, N)}
        o_specs = [spec_of[kind] for kind in out_kinds]
        o_shapes = [_sds(shape_of[kind], dt) for kind, dt in zip(out_kinds, out_dtypes)]
    assert "col" not in out_kinds or bn == N
    outs = pl.pallas_call(
        body, name=name, grid=(M // bm, N // bn, nk),
        in_specs=a_specs + [b_spec] + e_specs + ([] if dep is None else [ANY]),
        out_specs=o_specs, out_shape=o_shapes,
        scratch_shapes=[pltpu.VMEM((bm, bn), F32)] if use_acc else [],
        compiler_params=_cp("arbitrary" if "vsum" in out_kinds else "parallel", "parallel", "arbitrary"),
    )(*a_list, b, *extra, *cols, *vecs, *([] if dep is None else [dep]))
    return outs[0] if n_out == 1 else outs


def _epi_relu2(acc):
    r = jnp.maximum(acc, 0.0)
    return acc, r * r


def _epi_relu2_bwd(acc, p):
    return (acc * (2.0 * jnp.maximum(p.astype(F32), 0.0)),)


def _row_spec(rb, w=D_MODEL):
    return pl.BlockSpec((rb, w), lambda i: (i, 0))


def _vec_spec(w=D_MODEL):
    return pl.BlockSpec((1, w), lambda i: (0, 0))


def _rstd(v):
    return lax.rsqrt(jnp.mean(v * v, axis=-1, keepdims=True) + EPS)


def _rms_fwd(x, g, name):
    L = x.shape[0]
    rb = min(NORM_ROWS, L)

    def body(x_ref, g_ref, h_ref, r_ref):
        xv = x_ref[...]
        r = _rstd(xv)
        h_ref[...] = (xv * r * g_ref[...]).astype(BF16)
        r_ref[...] = r

    return pl.pallas_call(
        body, name=name, grid=(L // rb,),
        in_specs=[_row_spec(rb), _vec_spec()],
        out_specs=[_row_spec(rb), _row_spec(rb, 1)],
        out_shape=[_sds((L, D_MODEL), BF16), _sds((L, 1))],
        compiler_params=_cp("parallel"),
    )(x, g)


def _rms_bwd_rows(dy, xv, r, g):
    n = xv * r
    dyg = dy * g
    return r * (dyg - n * jnp.mean(dyg * n, axis=-1, keepdims=True)), n


POST_PRE_DTYPES = (F32, F32, BF16, F32, F32)
POST_PRE_KINDS = ("full", "col", "full", "col", "full")
PRE_POST_BWD_DTYPES = (F32, F32, BF16, F32)
PRE_POST_BWD_KINDS = ("full", "vsum", "full", "vsum")


def _epi_post_pre(y, x_in, g_post, g_pre):
    ry = _rstd(y)
    xo = x_in + y * ry * g_post
    rx = _rstd(xo)
    return xo, ry, xo * rx * g_pre, rx, y


def _epi_pre_post_bwd(gh, x, g_out, y_prev, rx, ry_prev, g_pre, g_post_prev):
    gx, n = _rms_bwd_rows(gh, x, rx, g_pre)
    gi = g_out + gx
    gy, ny = _rms_bwd_rows(gi, y_prev, ry_prev, g_post_prev)
    return gi, jnp.sum(gh * n, axis=0, keepdims=True), gy, jnp.sum(gi * ny, axis=0, keepdims=True)


def _epi_pre_bwd(gh, x, g_out, rx, g_pre):
    gx, n = _rms_bwd_rows(gh, x, rx, g_pre)
    return g_out + gx, jnp.sum(gh * n, axis=0, keepdims=True)


def _epi_post_loss(y, x_in, target, g_post):
    ry = _rstd(y)
    diff = x_in + y * ry * g_post - target
    gx = diff * (1.0 / D_MODEL)
    gy, n = _rms_bwd_rows(gx, y, ry, g_post)
    sq = jnp.broadcast_to(jnp.sum(diff * diff, keepdims=True), (1, y.shape[1]))
    return gx, gy, jnp.sum(gx * n, axis=0, keepdims=True), sq


def _cmul(ar, ai, br, bi):
    return ar * br - ai * bi, ar * bi + ai * br


def _zoh_cols(lr, li, ldt):
    dt = jnp.exp(ldt)
    mag = jnp.exp(lr * dt)
    ar = mag * jnp.cos(li * dt)
    ai = mag * jnp.sin(li * dt)
    den = lr * lr + li * li
    nr = ar - 1.0
    qr = (nr * lr + ai * li) / den
    qi = (ai * lr - nr * li) / den
    return dt, ar, ai, qr, qi, den


def _b_mask():
    r = lax.broadcasted_iota(jnp.int32, (S5_NS, LANES), 0)
    c = lax.broadcasted_iota(jnp.int32, (S5_NS, LANES), 1)
    return ((r >> 6) & 7) == (c >> 4)


def _c_mask():
    r = lax.broadcasted_iota(jnp.int32, (S5_W, 512), 0)
    c = lax.broadcasted_iota(jnp.int32, (S5_W, 512), 1)
    return ((r >> 4) & 7) == (c >> 6)


def _s5_prep(lam_r, ldt_r, lam_c, ldt_c, b_t, c_t, name):
    def body(lam_r_ref, ldt_r_ref, lam_c_ref, ldt_c_ref, b_ref, c_ref, tab_ref, bset_ref, cset_ref):
        lr, li = lam_r_ref[0:1, :], lam_r_ref[1:2, :]
        dt = jnp.exp(ldt_r_ref[...])
        mag = jnp.exp(lr * dt)
        p1r, p1i = mag * jnp.cos(li * dt), mag * jnp.sin(li * dt)
        p2r, p2i = _cmul(p1r, p1i, p1r, p1i)
        p3r, p3i = _cmul(p2r, p2i, p1r, p1i)
        p4r, p4i = _cmul(p2r, p2i, p2r, p2i)
        p5r, p5i = _cmul(p4r, p4i, p1r, p1i)
        p6r, p6i = _cmul(p4r, p4i, p2r, p2i)
        p7r, p7i = _cmul(p4r, p4i, p3r, p3i)
        p8r, p8i = _cmul(p4r, p4i, p4r, p4i)
        pw_r = [p1r, p2r, p3r, p4r, p5r, p6r, p7r, p8r]
        pw_i = [p1i, p2i, p3i, p4i, p5i, p6i, p7i, p8i]
        row = lax.broadcasted_iota(jnp.int32, (8, S5_NS), 0)
        zero = jnp.zeros((8, S5_NS), F32)

        def bc(v):
            return jnp.broadcast_to(v, (8, S5_NS))

        for d in range(2):
            sgn = 1.0 if d == 0 else -1.0
            for t, s in enumerate((1, 2, 4)):
                live = (row >= s) if d == 0 else (row <= 7 - s)
                tab_ref[d, 2 * t] = jnp.where(live, bc(pw_r[s - 1]), zero)
                tab_ref[d, 2 * t + 1] = jnp.where(live, bc(sgn * pw_i[s - 1]), zero)
            cr, ci = zero, zero
            for i in range(8):
                e = i if d == 0 else 7 - i
                cr = jnp.where(row == i, bc(pw_r[e]), cr)
                ci = jnp.where(row == i, bc(sgn * pw_i[e]), ci)
            tab_ref[d, 6] = cr
            tab_ref[d, 7] = ci

        _, _, _, qr, qi, _ = _zoh_cols(lam_c_ref[:, 0:1], lam_c_ref[:, 1:2], ldt_c_ref[...])
        bm = _b_mask()
        br, bi = b_ref[0], b_ref[1]
        bset_ref[0] = jnp.where(bm, qr * br - qi * bi, 0.0).astype(BF16)
        bset_ref[1] = jnp.where(bm, qr * bi + qi * br, 0.0).astype(BF16)
        cm = _c_mask()
        cset_ref[0] = jnp.where(cm, c_ref[0], 0.0).astype(BF16)
        cset_ref[1] = jnp.where(cm, c_ref[1], 0.0).astype(BF16)

    vm = pl.BlockSpec(memory_space=pltpu.VMEM)
    return pl.pallas_call(
        body, name=name, in_specs=[vm] * 6, out_specs=[vm] * 3,
        out_shape=[_sds((2, 8, 8, S5_NS)), _sds((2, S5_NS, LANES), BF16), _sds((2, S5_W, 512), BF16)],
        compiler_params=pltpu.CompilerParams(vmem_limit_bytes=VMEM_LIMIT),
    )(lam_r, ldt_r, lam_c, ldt_c, b_t, c_t)


SCAN_W = SCAN_GROUPS * LANES


def _scan_chunk(src_ref, dst_ref, tab_ref, carry_ref, nb, reverse, xs_ref=None, acc_ref=None):
    row = lax.broadcasted_iota(jnp.int32, (8, LANES), 0)

    def step(i, carry):
        b = (nb - 1 - i) if reverse else i
        off = pl.multiple_of(b * 8, 8)
        out = []
        for g in range(SCAN_GROUPS):
            lanes = pl.ds(g * LANES, LANES)
            cr, ci = carry[2 * g], carry[2 * g + 1]
            yr = src_ref[0, pl.ds(off, 8), lanes]
            yi = src_ref[1, pl.ds(off, 8), lanes]
            for t, s in enumerate((1, 2, 4)):
                sh = (8 - s) if reverse else s
                sr = pltpu.roll(yr, sh, 0)
                si = pltpu.roll(yi, sh, 0)
                mr, mi = tab_ref[2 * t, :, lanes], tab_ref[2 * t + 1, :, lanes]
                yr, yi = yr + mr * sr - mi * si, yi + mr * si + mi * sr
            pr, pi = tab_ref[6, :, lanes], tab_ref[7, :, lanes]
            yr, yi = yr + pr * cr - pi * ci, yi + pr * ci + pi * cr
            dst_ref[0, pl.ds(off, 8), lanes] = yr
            dst_ref[1, pl.ds(off, 8), lanes] = yi
            if xs_ref is not None:
                nr = jnp.where(row == 7, cr, pltpu.roll(yr, 7, 0))
                ni = jnp.where(row == 7, ci, pltpu.roll(yi, 7, 0))
                xr = xs_ref[0, pl.ds(off, 8), lanes]
                xi = xs_ref[1, pl.ds(off, 8), lanes]
                acc_ref[0, :, lanes] += xr * nr + xi * ni
                acc_ref[1, :, lanes] += xr * ni - xi * nr
            last = 0 if reverse else 7
            out += [jnp.broadcast_to(yr[last:last + 1, :], (8, LANES)),
                    jnp.broadcast_to(yi[last:last + 1, :], (8, LANES))]
        return tuple(out)

    init = []
    for g in range(SCAN_GROUPS):
        init += [carry_ref[0, :, pl.ds(g * LANES, LANES)], carry_ref[1, :, pl.ds(g * LANES, LANES)]]
    fin = lax.fori_loop(0, nb, step, tuple(init))
    for g in range(SCAN_GROUPS):
        carry_ref[0, :, pl.ds(g * LANES, LANES)] = fin[2 * g]
        carry_ref[1, :, pl.ds(g * LANES, LANES)] = fin[2 * g + 1]


def _s5_scan_fwd(z, bset, tabs, name):
    L = z.shape[0]
    tl = min(SCAN_CHUNK, L)
    nc = L // tl

    def body(u_ref, b_ref, tab_ref, x_ref, carry_ref):
        @pl.when(pl.program_id(1) == 0)
        def _():
            carry_ref[...] = jnp.zeros_like(carry_ref)

        u = u_ref[...].astype(BF16)
        x_ref[0] = _dot(u, b_ref[0], NT)
        x_ref[1] = _dot(u, b_ref[1], NT)
        _scan_chunk(x_ref, x_ref, tab_ref, carry_ref, tl // 8, False)

    return pl.pallas_call(
        body, name=name, grid=(S5_NS // SCAN_W, nc),
        in_specs=[pl.BlockSpec((tl, LANES), lambda j, c: (c, j)),
                  pl.BlockSpec((2, SCAN_W, LANES), lambda j, c: (0, j, 0)),
                  pl.BlockSpec((None, 8, 8, SCAN_W), lambda j, c: (0, 0, 0, j))],
        out_specs=pl.BlockSpec((2, tl, SCAN_W), lambda j, c: (0, c, j)),
        out_shape=_sds((2, L, S5_NS)),
        scratch_shapes=[pltpu.VMEM((2, 8, SCAN_W), F32)],
        compiler_params=_cp("parallel", "arbitrary"),
    )(z, bset, tabs)


def _s5_scan_bwd(gyl, cset, xs, z, bset, gud, tabs, name):
    L = z.shape[0]
    tl = min(SCAN_CHUNK, L)
    nc = L // tl

    def body(g_ref, c_ref, xs_ref, u_ref, b_ref, gud_ref, tab_ref, gu_ref, ga_ref, gb_ref, gc_ref,
             gx_ref, carry_ref, acc_ref):
        c = pl.program_id(1)

        @pl.when(c == 0)
        def _():
            carry_ref[...] = jnp.zeros_like(carry_ref)
            acc_ref[...] = jnp.zeros_like(acc_ref)
            gb_ref[...] = jnp.zeros_like(gb_ref)
            gc_ref[...] = jnp.zeros_like(gc_ref)

        gy = g_ref[...].astype(BF16)
        gx_ref[0] = _dot(gy, c_ref[0])
        gx_ref[1] = -_dot(gy, c_ref[1])
        gc_ref[0] += _dot(gy, xs_ref[0].astype(BF16), TN)
        gc_ref[1] -= _dot(gy, xs_ref[1].astype(BF16), TN)
        _scan_chunk(gx_ref, gx_ref, tab_ref, carry_ref, tl // 8, True, xs_ref, acc_ref)
        gr = gx_ref[0].astype(BF16)
        gi = gx_ref[1].astype(BF16)
        gu_ref[...] = gud_ref[...] + _dot(gr, b_ref[0]) + _dot(gi, b_ref[1])
        u = u_ref[...].astype(BF16)
        gb_ref[0] += _dot(gr, u, TN)
        gb_ref[1] += _dot(gi, u, TN)

        @pl.when(c == nc - 1)
        def _():
            ga_ref[0:1, :] = jnp.sum(acc_ref[0], axis=0, keepdims=True)
            ga_ref[1:2, :] = jnp.sum(acc_ref[1], axis=0, keepdims=True)

    rev = lambda j, c: (nc - 1 - c, j)
    col = pl.BlockSpec((tl, LANES), rev)
    return pl.pallas_call(
        body, name=name, grid=(S5_NS // SCAN_W, nc),
        in_specs=[col, pl.BlockSpec((2, LANES, SCAN_W), lambda j, c: (0, j, 0)),
                  pl.BlockSpec((2, tl, SCAN_W), lambda j, c: (0, nc - 1 - c, j)), col,
                  pl.BlockSpec((2, SCAN_W, LANES), lambda j, c: (0, j, 0)), col,
                  pl.BlockSpec((None, 8, 8, SCAN_W), lambda j, c: (1, 0, 0, j))],
        out_specs=[col, pl.BlockSpec((2, SCAN_W), lambda j, c: (0, j)),
                   pl.BlockSpec((2, SCAN_W, LANES), lambda j, c: (0, j, 0)),
                   pl.BlockSpec((2, LANES, SCAN_W), lambda j, c: (0, j, 0))],
        out_shape=[_sds((L, S5_W)), _sds((2, S5_NS)), _sds((2, S5_NS, LANES)), _sds((2, S5_W, 512))],
        scratch_shapes=[pltpu.VMEM((2, tl, SCAN_W), F32), pltpu.VMEM((2, 8, SCAN_W), F32),
                        pltpu.VMEM((2, 8, SCAN_W), F32)],
        compiler_params=_cp("parallel", "arbitrary"),
    )(gyl, cset, xs, z, bset, gud, tabs)


def _s5_out_fwd(xs, cset, z, dvec, wglu, name):
    L = z.shape[0]
    bl = min(256, L)

    def body(x_ref, c_ref, u_ref, d_ref, w_ref, ylin_ref, ya_ref):
        cols = []
        for j in range(4):
            xr = x_ref[0, :, 512 * j:512 * (j + 1)].astype(BF16)
            xi = x_ref[1, :, 512 * j:512 * (j + 1)].astype(BF16)
            cr = c_ref[0, LANES * j:LANES * (j + 1), :]
            ci = c_ref[1, LANES * j:LANES * (j + 1), :]
            cols.append(_dot(xr, cr, NT) - _dot(xi, ci, NT))
        ylin = jnp.concatenate(cols, axis=1) + d_ref[...] * u_ref[...]
        yg = _gelu(ylin)
        t = _dot(yg.astype(BF16), w_ref[...])
        ylin_ref[...] = ylin
        ya_ref[...] = (yg * _sigmoid(t)).astype(BF16)

    return pl.pallas_call(
        body, name=name, grid=(L // bl,),
        in_specs=[pl.BlockSpec((2, bl, S5_NS), lambda i: (0, i, 0)),
                  pl.BlockSpec((2, S5_W, 512), lambda i: (0, 0, 0)),
                  pl.BlockSpec((bl, S5_W), lambda i: (i, 0)),
                  pl.BlockSpec((1, S5_W), lambda i: (0, 0)),
                  pl.BlockSpec((S5_W, S5_W), lambda i: (0, 0))],
        out_specs=[pl.BlockSpec((bl, S5_W), lambda i: (i, 0))] * 2,
        out_shape=[_sds((L, S5_W)), _sds((L, S5_W), BF16)],
        compiler_params=_cp("parallel"),
    )(xs, cset, z, dvec, wglu)


def _s5_glu_bwd(g_m, ylin, z, dvec, wglu, name):
    L = z.shape[0]
    bl = min(256, L)

    def body(g_ref, ylin_ref, u_ref, d_ref, w_ref, gyl_ref, gud_ref, gw_ref, gd_ref):
        i = pl.program_id(0)
        ylin = ylin_ref[...]
        yg = _gelu(ylin)
        ygb = yg.astype(BF16)
        sg = _sigmoid(_dot(ygb, w_ref[...]))
        gya = g_ref[...]
        gt = gya * yg * sg * (1.0 - sg)
        gtb = gt.astype(BF16)
        gyg = gya * sg + _dot(gtb, w_ref[...], NT)
        gyl = gyg * _gelu_grad(ylin)
        gyl_ref[...] = gyl
        gud_ref[...] = gyl * d_ref[...]

        @pl.when(i == 0)
        def _():
            gw_ref[...] = jnp.zeros_like(gw_ref)
            gd_ref[...] = jnp.zeros_like(gd_ref)

        gw_ref[...] += _dot(ygb, gtb, TN)
        gd_ref[...] += jnp.sum(gyl * u_ref[...], axis=0, keepdims=True)

    blk = pl.BlockSpec((bl, S5_W), lambda i: (i, 0))
    return pl.pallas_call(
        body, name=name, grid=(L // bl,),
        in_specs=[blk, blk, blk, pl.BlockSpec((1, S5_W), lambda i: (0, 0)),
                  pl.BlockSpec((S5_W, S5_W), lambda i: (0, 0))],
        out_specs=[blk, blk, pl.BlockSpec((S5_W, S5_W), lambda i: (0, 0)), pl.BlockSpec((1, S5_W), lambda i: (0, 0))],
        out_shape=[_sds((L, S5_W)), _sds((L, S5_W)), _sds((S5_W, S5_W)), _sds((1, S5_W))],
        compiler_params=_cp("arbitrary"),
    )(g_m, ylin, z, dvec, wglu)


def _s5_param_bwd(lam_c, ldt_c, b_t, gb, ga_c, gc, name):
    def body(lam_ref, ldt_ref, b_ref, gb_ref, ga_ref, gc_ref, glam_ref, gldt_ref, gbo_ref, gco_ref):
        lr, li = lam_ref[:, 0:1], lam_ref[:, 1:2]
        dt, ar, ai, qr, qi, den = _zoh_cols(lr, li, ldt_ref[...])
        bm = _b_mask()
        gbr = jnp.where(bm, gb_ref[0], 0.0)
        gbi = jnp.where(bm, gb_ref[1], 0.0)
        br, bi = b_ref[0], b_ref[1]
        obr = gbr * qr + gbi * qi
        obi = gbi * qr - gbr * qi
        gqr = jnp.sum(gbr * br + gbi * bi, axis=1, keepdims=True)
        gqi = jnp.sum(gbi * br - gbr * bi, axis=1, keepdims=True)
        for s in (64, 32, 16):
            obr = obr + pltpu.roll(obr, s, 1)
            obi = obi + pltpu.roll(obi, s, 1)
        gbo_ref[0] = obr
        gbo_ref[1] = obi
        gar = ga_ref[:, 0:1] + (gqr * lr - gqi * li) / den
        gai = ga_ref[:, 1:2] + (gqr * li + gqi * lr) / den
        qlr = (qr * lr + qi * li) / den
        qli = (qi * lr - qr * li) / den
        glr = -(gqr * qlr + gqi * qli)
        gli = -(gqi * qlr - gqr * qli)
        glr = glr + dt * (gar * ar + gai * ai)
        gli = gli + dt * (gai * ar - gar * ai)
        wr, wi = _cmul(lr, li, ar, ai)
        gldt = (gar * wr + gai * wi) * dt
        glam_ref[:, 0:1] = glr
        glam_ref[:, 1:2] = gli
        r = lax.broadcasted_iota(jnp.int32, (S5_NS, 32), 0)
        c = lax.broadcasted_iota(jnp.int32, (S5_NS, 32), 1)
        gldt_ref[...] = jnp.sum(jnp.where((r >> 6) == c, gldt, 0.0), axis=0, keepdims=True)
        cm = _c_mask()
        for k in range(2):
            oc = jnp.where(cm, gc_ref[k], 0.0)
            for s in (256, 128, 64):
                oc = oc + pltpu.roll(oc, s, 1)
            gco_ref[k] = oc[:, 0:LANES]

    vm = pl.BlockSpec(memory_space=pltpu.VMEM)
    return pl.pallas_call(
        body, name=name, in_specs=[vm] * 6, out_specs=[vm] * 4,
        out_shape=[_sds((S5_NS, 2)), _sds((1, 32)), _sds((2, S5_NS, LANES)), _sds((2, S5_W, LANES))],
        compiler_params=pltpu.CompilerParams(vmem_limit_bytes=VMEM_LIMIT),
    )(lam_c, ldt_c, b_t, gb, ga_c, gc)


FL_BLK = EVEN_PAD // LANES - 1
Q_BLK, K_BLK, V_BLK = 4, 8, 12
NEG = -1e30


def _log_sigmoid(v):
    return jnp.minimum(v, 0.0) - jnp.log(1.0 + jnp.exp(-jnp.abs(v)))


def _fox_f_fwd(z, bf, name):
    L = z.shape[0]

    def body(fl_ref, b_ref, f_ref, fq_ref):
        row = lax.broadcasted_iota(jnp.int32, (L, LANES), 0)
        cs = _cumsum_rows(_log_sigmoid(fl_ref[...] + b_ref[...]), True, row)
        f_ref[...] = cs
        expand = (lax.broadcasted_iota(jnp.int32, (LANES, FOX_W), 0)
                  == (lax.broadcasted_iota(jnp.int32, (LANES, FOX_W), 1) >> 6)).astype(F32)
        fq_ref[...] = lax.dot_general(cs, expand, NN, precision=lax.Precision.HIGHEST, preferred_element_type=F32)

    return pl.pallas_call(
        body, name=name, grid=(1,),
        in_specs=[pl.BlockSpec((L, LANES), lambda i: (0, FL_BLK)), pl.BlockSpec((1, LANES), lambda i: (0, 0))],
        out_specs=[pl.BlockSpec((L, LANES), lambda i: (0, 0)), pl.BlockSpec((L, FOX_W), lambda i: (0, 0))],
        out_shape=[_sds((L, LANES)), _sds((L, FOX_W))],
        compiler_params=_cp("arbitrary"),
    )(z, bf)


def _fox_f_bwd(dFk, dfq, z, bf, name):
    L = z.shape[0]

    def body(dfk_ref, dfq_ref, fl_ref, b_ref, dfl_ref, db_ref):
        sel = (lax.broadcasted_iota(jnp.int32, (FOX_W, LANES), 0)
               == 64 * lax.broadcasted_iota(jnp.int32, (FOX_W, LANES), 1)).astype(F32)
        dfq_h = lax.dot_general(dfq_ref[...], sel, NN, precision=lax.Precision.HIGHEST, preferred_element_type=F32)
        row = lax.broadcasted_iota(jnp.int32, (L, LANES), 0)
        cs = _cumsum_rows(dfk_ref[...] + dfq_h, False, row)
        dfl = cs * _sigmoid(-(fl_ref[...] + b_ref[...]))
        dfl_ref[...] = dfl
        db_ref[...] = jnp.sum(dfl, axis=0, keepdims=True)

    return pl.pallas_call(
        body, name=name, grid=(1,),
        in_specs=[pl.BlockSpec((L, LANES), lambda i: (0, 0)), pl.BlockSpec((L, FOX_W), lambda i: (0, 0)),
                  pl.BlockSpec((L, LANES), lambda i: (0, FL_BLK)), pl.BlockSpec((1, LANES), lambda i: (0, 0))],
        out_specs=[pl.BlockSpec((L, LANES), lambda i: (0, 0)), pl.BlockSpec((1, LANES), lambda i: (0, 0))],
        out_shape=[_sds((L, LANES)), _sds((1, LANES))],
        compiler_params=_cp("arbitrary"),
    )(dFk, dfq, z, bf)


def _head_mask(hh):
    lane = lax.broadcasted_iota(jnp.int32, (1, LANES), 1)
    return (lane >> 6) == hh


FOX_T = 512


def _fox_head(x, hh):
    return jnp.where(_head_mask(hh), x, 0.0).astype(BF16)


def _fox_scores(qh, k, fq_ref, fr_ref, hh, causal):
    s = _dot(qh, k, NT) + (fq_ref[:, 64 * hh:64 * hh + 1] - fr_ref[hh:hh + 1, :])
    return s if causal is None else jnp.where(causal, s, NEG)


def _causal(T):
    return lax.broadcasted_iota(jnp.int32, (T, T), 1) <= lax.broadcasted_iota(jnp.int32, (T, T), 0)


def _fox_fwd(z, fq, frow, name):
    L = z.shape[0]
    T = min(FOX_T, L)
    nq = L // T

    def body(qt_ref, kt_ref, q_ref, k_ref, v_ref, fq_ref, fr_ref, o_ref, lse_ref, m_ref, l_ref, acc_ref):
        t = pl.program_id(1)
        qi, ki = qt_ref[t], kt_ref[t]

        @pl.when(ki == 0)
        def _():
            m_ref[...] = jnp.full_like(m_ref, NEG)
            l_ref[...] = jnp.zeros_like(l_ref)
            acc_ref[...] = jnp.zeros_like(acc_ref)

        def step(diagonal):
            q = q_ref[...] * 0.125
            k = k_ref[...].astype(BF16)
            v = v_ref[...].astype(BF16)
            causal = _causal(T) if diagonal else None
            s = jnp.concatenate([_fox_scores(_fox_head(q, hh), k, fq_ref, fr_ref, hh, causal) for hh in range(2)],
                                axis=0)
            m_old = m_ref[...]
            m_new = jnp.maximum(m_old, jnp.max(s, axis=1, keepdims=True))
            alpha = jnp.exp(m_old - m_new)
            p = jnp.exp(s - m_new)
            l_ref[...] = alpha * l_ref[...] + jnp.sum(p, axis=1, keepdims=True)
            m_ref[...] = m_new
            acc_ref[...] = alpha * acc_ref[...] + _dot(p.astype(BF16), v)

        @pl.when(ki < qi)
        def _():
            step(False)

        @pl.when(ki == qi)
        def _():
            step(True)
            h0 = _head_mask(0)
            l = l_ref[...]
            o_h = acc_ref[...] / l
            lse_h = m_ref[...] + jnp.log(l)
            o_ref[...] = jnp.where(h0, o_h[:T], o_h[T:])
            lse_ref[...] = jnp.where(h0, lse_h[:T], lse_h[T:])

    pairs = [(qi, ki) for qi in range(nq) for ki in range(qi + 1)]
    qt = jnp.asarray([p[0] for p in pairs], jnp.int32)
    kt = jnp.asarray([p[1] for p in pairs], jnp.int32)

    def qspec(base):
        return pl.BlockSpec((T, LANES), lambda j, t, qt, kt: (qt[t], base + j))

    def kspec(base):
        return pl.BlockSpec((T, LANES), lambda j, t, qt, kt: (kt[t], base + j))

    return pl.pallas_call(
        body, name=name,
        grid_spec=pltpu.PrefetchScalarGridSpec(
            num_scalar_prefetch=2, grid=(4, len(pairs)),
            in_specs=[qspec(Q_BLK), kspec(K_BLK), kspec(V_BLK), qspec(0),
                      pl.BlockSpec((None, 2, T), lambda j, t, qt, kt: (j, 0, kt[t]))],
            out_specs=[qspec(0), qspec(0)],
            scratch_shapes=[pltpu.VMEM((2 * T, 1), F32), pltpu.VMEM((2 * T, 1), F32),
                            pltpu.VMEM((2 * T, LANES), F32)]),
        out_shape=[_sds((L, FOX_W)), _sds((L, FOX_W))],
        compiler_params=_cp("parallel", "arbitrary"),
    )(qt, kt, z, z, z, fq, frow)


def _fox_bwd(z, fq, frow, o, lse, g_m, name):
    L = z.shape[0]
    T = min(FOX_T, L)
    nq = L // T

    pairs = [(qi, ki) for ki in range(nq) for qi in range(ki, nq)]
    qt = jnp.asarray([p[0] for p in pairs], jnp.int32)
    kt = jnp.asarray([p[1] for p in pairs], jnp.int32)

    def body(qt_ref, kt_ref, q_ref, k_ref, v_ref, fq_ref, fr_ref, o_ref, lse_ref, do_ref,
             dq_ref, dk_ref, dv_ref, dfq_ref, dfk_ref, dk_acc, dv_acc, df_acc):
        t = pl.program_id(1)
        qi, ki = qt_ref[t], kt_ref[t]

        @pl.when(t == 0)
        def _():
            dq_ref[...] = jnp.zeros_like(dq_ref)
            dfq_ref[...] = jnp.zeros_like(dfq_ref)

        @pl.when(qi == ki)
        def _():
            dk_acc[...] = jnp.zeros_like(dk_acc)
            dv_acc[...] = jnp.zeros_like(dv_acc)
            df_acc[...] = jnp.zeros_like(df_acc)

        def step(diagonal):
            q = q_ref[...] * 0.125
            qb = q.astype(BF16)
            k = k_ref[...].astype(BF16)
            v = v_ref[...].astype(BF16)
            do = do_ref[...]
            dob = do.astype(BF16)
            do_o = dob.astype(F32) * o_ref[...]
            causal = _causal(T) if diagonal else None
            dvs, dks, dqs, rss = [], [], [], []
            for hh in range(2):
                s = _fox_scores(_fox_head(q, hh), k, fq_ref, fr_ref, hh, causal)
                p = jnp.exp(s - lse_ref[:, 64 * hh:64 * hh + 1])
                dp = _dot(_fox_head(do, hh), v, NT)
                delta = jnp.sum(jnp.where(_head_mask(hh), do_o, 0.0), axis=1, keepdims=True)
                ds = p * (dp - delta)
                dsb = ds.astype(BF16)
                dvs.append(_dot(p.astype(BF16), dob, TN))
                dks.append(_dot(dsb, qb, TN))
                dqs.append(_dot(dsb, k))
                rss.append(jnp.sum(ds, axis=1, keepdims=True))
                df_acc[hh:hh + 1, :] -= jnp.sum(ds, axis=0, keepdims=True)
            h0 = _head_mask(0)
            dv_acc[...] += jnp.where(h0, dvs[0], dvs[1])
            dk_acc[...] += jnp.where(h0, dks[0], dks[1])
            rows = pl.ds(pl.multiple_of(qi * T, T), T)
            dq_ref[rows, :] += jnp.where(h0, dqs[0], dqs[1])
            dfq_ref[rows, :] += jnp.where(h0, rss[0], rss[1])

        @pl.when(qi > ki)
        def _():
            step(False)

        @pl.when(qi == ki)
        def _():
            step(True)

        @pl.when(qi == nq - 1)
        def _():
            dk_ref[...] = dk_acc[...]
            dv_ref[...] = dv_acc[...]
            dfk_ref[...] = df_acc[...]

        @pl.when(t == len(pairs) - 1)
        def _():
            dq_ref[...] = dq_ref[...] * 0.125

    def qside(base):
        return pl.BlockSpec((T, LANES), lambda j, t, qt, kt: (qt[t], base + j))

    def kside(base):
        return pl.BlockSpec((T, LANES), lambda j, t, qt, kt: (kt[t], base + j))

    pair = pl.BlockSpec((L, LANES), lambda j, t, qt, kt: (0, j))
    frow_spec = pl.BlockSpec((None, 2, T), lambda j, t, qt, kt: (j, 0, kt[t]))
    return pl.pallas_call(
        body, name=name,
        grid_spec=pltpu.PrefetchScalarGridSpec(
            num_scalar_prefetch=2, grid=(4, len(pairs)),
            in_specs=[qside(Q_BLK), kside(K_BLK), kside(V_BLK), qside(0), frow_spec, qside(0), qside(0), qside(4)],
            out_specs=[pair, kside(0), kside(0), pair, frow_spec],
            scratch_shapes=[pltpu.VMEM((T, LANES), F32), pltpu.VMEM((T, LANES), F32), pltpu.VMEM((2, T), F32)]),
        out_shape=[_sds((L, FOX_W)), _sds((L, FOX_W)), _sds((L, FOX_W)), _sds((L, FOX_W)), _sds((4, 2, L))],
        compiler_params=_cp("parallel", "arbitrary"),
    )(qt, kt, z, z, z, fq, frow, o, lse, g_m)


def _shift_rows(v, s, down, row):
    n = v.shape[0]
    if down:
        return jnp.where(row >= s, pltpu.roll(v, s, 0), 0.0)
    return jnp.where(row < n - s, pltpu.roll(v, n - s, 0), 0.0)


def _cumsum_rows(v, down, row):
    s = 1
    while s < v.shape[0]:
        v = v + _shift_rows(v, s, down, row)
        s *= 2
    return v


def _window_sum(v, g, down, row):
    out = jnp.zeros_like(v)
    s = v
    for k in range(4):
        s = s + _shift_rows(s, 1 << k, down, row)
        out = jnp.where(g == k, s, out)
    return out


def _pool_inv_cnt(g, row):
    w = jnp.left_shift(2, g).astype(F32)
    return 1.0 / jnp.minimum(row.astype(F32) + 1.0, w)


def _pool_fwd(z, pool_w, scale, name):
    L = z.shape[0]

    def body(x_ref, w_ref, s_ref, y_ref, p_ref):
        g = pl.program_id(0)
        row = lax.broadcasted_iota(jnp.int32, (L, LANES), 0)
        x = x_ref[...]
        pooled = (_window_sum(x, g, True, row) * _pool_inv_cnt(g, row) - x).astype(BF16)
        p_ref[...] = pooled
        y_ref[...] = (_dot(pooled, w_ref[...].astype(BF16)) * s_ref[...]).astype(BF16)

    col = pl.BlockSpec((L, LANES), lambda g: (0, g))
    return pl.pallas_call(
        body, name=name, grid=(4,),
        in_specs=[col, pl.BlockSpec((None, LANES, LANES), lambda g: (g, 0, 0)), pl.BlockSpec((1, LANES), lambda g: (0, g))],
        out_specs=[col, col],
        out_shape=[_sds((L, 512), BF16), _sds((L, 512), BF16)],
        compiler_params=_cp("parallel"),
    )(z, pool_w, scale)


def _pool_bwd(g_m, pooled, pool_w, scale, name):
    L = g_m.shape[0]

    def body(g_ref, p_ref, w_ref, s_ref, gx_ref, gw_ref, gs_ref):
        g = pl.program_id(0)
        row = lax.broadcasted_iota(jnp.int32, (L, LANES), 0)
        gy = g_ref[...]
        pooled = p_ref[...]
        wb = w_ref[...].astype(BF16)
        lin = _dot(pooled, wb)
        gs_ref[...] = jnp.sum(gy * lin, axis=0, keepdims=True)
        glin = (gy * s_ref[...]).astype(BF16)
        gw_ref[...] = _dot(pooled, glin, TN)
        gp = _dot(glin, wb, NT)
        gx_ref[...] = _window_sum(gp * _pool_inv_cnt(g, row), g, False, row) - gp

    col = pl.BlockSpec((L, LANES), lambda g: (0, g))
    wspec = pl.BlockSpec((None, LANES, LANES), lambda g: (g, 0, 0))
    vec = pl.BlockSpec((1, LANES), lambda g: (0, g))
    return pl.pallas_call(
        body, name=name, grid=(4,),
        in_specs=[col, col, wspec, vec],
        out_specs=[col, wspec, vec],
        out_shape=[_sds((L, 512)), _sds((4, LANES, LANES)), _sds((1, 512))],
        compiler_params=_cp("parallel"),
    )(g_m, pooled, pool_w, scale)


SGU_CHUNKS = 4


def _sgu_ln(v, gam, bet):
    gv = _gelu(v)
    mu = jnp.mean(gv, axis=-1, keepdims=True)
    xc = gv - mu
    rs = lax.rsqrt(jnp.mean(xc * xc, axis=-1, keepdims=True) + EPS)
    xh = xc * rs
    return xh, rs, xh * gam + bet


def _tril_ws(w_ref, g):
    r = lax.broadcasted_iota(jnp.int32, (LANES, LANES), 0)
    c = lax.broadcasted_iota(jnp.int32, (LANES, LANES), 1)
    return jnp.where(r >= c, w_ref[g], 0.0).astype(BF16)


def _sgu_fwd(z, ln_g, ln_b, w_s, b_st, name):
    L = z.shape[0]
    rb = min(SGU_CHUNKS * LANES, L)

    def body(u_ref, v_ref, g_ref, b_ref, w_ref, bs_ref, y_ref):
        _, _, vln = _sgu_ln(v_ref[...], g_ref[...], b_ref[...])
        gu = _gelu(u_ref[...])
        vb = vln.astype(BF16)
        for g in range(4):
            ws = _tril_ws(w_ref, g)
            for n in range(rb // LANES):
                rows = slice(n * LANES, (n + 1) * LANES)
                cols = slice(g * LANES, (g + 1) * LANES)
                mixed = _dot(ws, vb[rows, cols]) + bs_ref[:, g:g + 1]
                y_ref[rows, cols] = (gu[rows, cols] * mixed).astype(BF16)

    vm = lambda shape: pl.BlockSpec(shape, lambda i: tuple(0 for _ in shape))
    return pl.pallas_call(
        body, name=name, grid=(L // rb,),
        in_specs=[pl.BlockSpec((rb, 512), lambda i: (i, 1)), pl.BlockSpec((rb, 512), lambda i: (i, 2)),
                  vm((1, 512)), vm((1, 512)), vm((4, LANES, LANES)), vm((LANES, 4))],
        out_specs=pl.BlockSpec((rb, 512), lambda i: (i, 0)),
        out_shape=_sds((L, 512), BF16),
        compiler_params=_cp("parallel"),
    )(z, z, ln_g, ln_b, w_s, b_st)


def _sgu_bwd(g_m, z, ln_g, ln_b, w_s, b_st, name):
    L = z.shape[0]
    rb = min(SGU_CHUNKS * LANES, L)

    def body(gy_ref, u_ref, v_ref, g_ref, b_ref, w_ref, bs_ref, gu_ref, gv_ref, gw_ref, gbs_ref, gg_ref, gb_ref):
        i = pl.program_id(0)

        @pl.when(i == 0)
        def _():
            gw_ref[...] = jnp.zeros_like(gw_ref)
            gbs_ref[...] = jnp.zeros_like(gbs_ref)
            gg_ref[...] = jnp.zeros_like(gg_ref)
            gb_ref[...] = jnp.zeros_like(gb_ref)

        v = v_ref[...]
        u = u_ref[...]
        gy = gy_ref[...]
        xh, rs, vln = _sgu_ln(v, g_ref[...], b_ref[...])
        gel_u = _gelu(u)
        gmix = gy * gel_u
        vb = vln.astype(BF16)
        gmb = gmix.astype(BF16)
        r = lax.broadcasted_iota(jnp.int32, (LANES, LANES), 0)
        c = lax.broadcasted_iota(jnp.int32, (LANES, LANES), 1)
        gvln_cols = []
        for g in range(4):
            ws = _tril_ws(w_ref, g)
            cols = slice(g * LANES, (g + 1) * LANES)
            gw = jnp.zeros((LANES, LANES), F32)
            gbs = jnp.zeros((LANES, 1), F32)
            parts = []
            for n in range(rb // LANES):
                rows = slice(n * LANES, (n + 1) * LANES)
                mixed = _dot(ws, vb[rows, cols]) + bs_ref[:, g:g + 1]
                gu_ref[rows, cols] = gy[rows, cols] * mixed * _gelu_grad(u[rows, cols])
                parts.append(_dot(ws, gmb[rows, cols], TN))
                gw = gw + _dot(gmb[rows, cols], vb[rows, cols], NT)
                gbs = gbs + jnp.sum(gmix[rows, cols], axis=1, keepdims=True)
            gvln_cols.append(jnp.concatenate(parts, axis=0))
            gw_ref[g] += jnp.where(r >= c, gw, 0.0)
            gbs_ref[:, g:g + 1] += gbs
        gvln = jnp.concatenate(gvln_cols, axis=1)
        gg_ref[...] += jnp.sum(gvln * xh, axis=0, keepdims=True)
        gb_ref[...] += jnp.sum(gvln, axis=0, keepdims=True)
        gxh = gvln * g_ref[...]
        ggv = rs * (gxh - jnp.mean(gxh, axis=-1, keepdims=True) - xh * jnp.mean(gxh * xh, axis=-1, keepdims=True))
        gv_ref[...] = ggv * _gelu_grad(v)

    vm = lambda shape: pl.BlockSpec(shape, lambda i: tuple(0 for _ in shape))
    blk = pl.BlockSpec((rb, 512), lambda i: (i, 0))
    return pl.pallas_call(
        body, name=name, grid=(L // rb,),
        in_specs=[pl.BlockSpec((rb, 512), lambda i: (i, 1)), pl.BlockSpec((rb, 512), lambda i: (i, 1)),
                  pl.BlockSpec((rb, 512), lambda i: (i, 2)),
                  vm((1, 512)), vm((1, 512)), vm((4, LANES, LANES)), vm((LANES, 4))],
        out_specs=[blk, blk, vm((4, LANES, LANES)), vm((LANES, 4)), vm((1, 512)), vm((1, 512))],
        out_shape=[_sds((L, 512)), _sds((L, 512)), _sds((4, LANES, LANES)), _sds((LANES, 4)),
                   _sds((1, 512)), _sds((1, 512))],
        compiler_params=_cp("arbitrary"),
    )(g_m, z, z, ln_g, ln_b, w_s, b_st)


def _adamw_math(w, g, m, v):
    nm = ADAM_B1 * m + (1.0 - ADAM_B1) * g
    nv = ADAM_B2 * v + (1.0 - ADAM_B2) * (g * g)
    m_hat = nm / (1.0 - ADAM_B1 ** ADAM_STEP)
    v_hat = nv / (1.0 - ADAM_B2 ** ADAM_STEP)
    delta = -ADAM_LR * (m_hat / (jnp.sqrt(v_hat) + ADAM_EPS) + ADAM_WD * w)
    return delta, nm, nv


def _sum_adamw(parts, w, m, v, name, layer=0, prev=None):
    n_layers, R, C = w.shape
    rb = 128 if R % 128 == 0 else R

    def body(p_ref, w_ref, m_ref, v_ref, *rest):
        g_ref, d_ref, nm_ref, nv_ref = rest[-4:]
        g = p_ref[0].astype(F32)
        for s in range(1, N_DEV):
            g = g + p_ref[s].astype(F32)
        d, nm, nv = _adamw_math(w_ref[...], g, m_ref[...], v_ref[...])
        g_ref[...] = g
        d_ref[...] = d
        nm_ref[...] = nm
        nv_ref[...] = nv

    blk = pl.BlockSpec((None, rb, C), lambda i: (layer, i, 0))
    prev = [] if prev is None else list(prev)
    return pl.pallas_call(
        body, name=name, grid=(R // rb,),
        in_specs=[pl.BlockSpec((N_DEV, rb, C), lambda i: (0, i, 0)), blk, blk, blk] + [ANY] * len(prev),
        out_specs=[blk] * 4, out_shape=[_sds((n_layers, R, C))] * 4,
        input_output_aliases={4 + k: k for k in range(len(prev))},
        compiler_params=_cp("parallel"),
    )(parts, w, m, v, *prev)


def _sum_pieces(parts, name):
    _, R, C = parts.shape

    def body(p_ref, g_ref):
        g = p_ref[0].astype(F32)
        for s in range(1, N_DEV):
            g = g + p_ref[s].astype(F32)
        g_ref[...] = g

    vm = pl.BlockSpec(memory_space=pltpu.VMEM)
    return pl.pallas_call(body, name=name, in_specs=[vm], out_specs=vm, out_shape=_sds((R, C)),
                          compiler_params=pltpu.CompilerParams(vmem_limit_bytes=VMEM_LIMIT))(parts)


def _adamw_many(ws, gs, ms, vs, name):
    n = len(ws)
    vm = pl.BlockSpec(memory_space=pltpu.VMEM)

    def body(*refs):
        w_refs, g_refs, m_refs, v_refs = refs[:n], refs[n:2 * n], refs[2 * n:3 * n], refs[3 * n:4 * n]
        d_refs, nm_refs, nv_refs = refs[4 * n:5 * n], refs[5 * n:6 * n], refs[6 * n:7 * n]
        for i in range(n):
            d, nm, nv = _adamw_math(w_refs[i][...], g_refs[i][...], m_refs[i][...], v_refs[i][...])
            d_refs[i][...] = d
            nm_refs[i][...] = nm
            nv_refs[i][...] = nv

    shapes = [_sds(w.shape) for w in ws]
    outs = pl.pallas_call(
        body, name=name, in_specs=[vm] * (4 * n), out_specs=[vm] * (3 * n), out_shape=shapes * 3,
        compiler_params=pltpu.CompilerParams(vmem_limit_bytes=VMEM_LIMIT),
    )(*ws, *gs, *ms, *vs)
    return list(outs[:n]), list(outs[n:2 * n]), list(outs[2 * n:])


def _mesh_pos():
    return lax.axis_index("x"), lax.axis_index("y"), lax.axis_index("c")


def _dev_index(p):
    return 4 * p[0] + 2 * p[1] + p[2]


HBM = pl.BlockSpec(memory_space=pltpu.HBM)
SEM = pl.BlockSpec(memory_space=pltpu.SEMAPHORE)
EFFECT = pltpu.SideEffectType.DATAFLOW_SIDE_EFFECTING


def _peer_list():
    x, y, c = _mesh_pos()
    peers = [(x ^ dx, y ^ dy, c ^ dc) for dx in range(2) for dy in range(2) for dc in range(2)][1:]
    return (x, y, c), peers


def _split_copy(src_ref, land_ref, send_sems, recv_sems, i, k, peer, slot, exchange):
    return pltpu.make_async_remote_copy(
        src_ref=src_ref.at[_dev_index(peer)] if exchange else src_ref, dst_ref=land_ref.at[slot],
        send_sem=send_sems.at[7 * i + k], recv_sem=recv_sems.at[7 * i + k], device_id=peer, device_id_type=MESH)


def _comm_start(groups, name, exchange, dep=None):
    sizes = [len(g) for g in groups]
    n = sum(sizes)
    srcs = [a for g in groups for a in g]
    my_index = _dev_index(_mesh_pos())
    lands = []
    for a in srcs:
        if exchange:
            own = lax.dynamic_slice(a, (my_index, 0, 0), (1,) + a.shape[1:])
            shape = a.shape
        else:
            own = a[None]
            shape = (N_DEV,) + a.shape
        lands.append(lax.dynamic_update_slice(lax.empty(shape, a.dtype), own, (my_index, 0, 0)))

    n_dep = 0 if dep is None else 1

    def body(*refs):
        src_refs, land_refs = refs[:n], refs[n:2 * n]
        sem_refs = refs[2 * n + n_dep:2 * n + n_dep + 2 * len(sizes)]
        token_ref = refs[-1]
        me, peers = _peer_list()
        mi = _dev_index(me)
        i = 0
        for gi, sz in enumerate(sizes):
            for j in range(sz):
                for k, peer in enumerate(peers):
                    _split_copy(src_refs[i], land_refs[i], sem_refs[2 * gi], sem_refs[2 * gi + 1], j, k, peer, mi,
                                exchange).start()
                i += 1
        token_ref[...] = jnp.zeros_like(token_ref)

    sem_shapes = []
    for sz in sizes:
        sem_shapes += [pltpu.SemaphoreType.DMA((7 * sz,)), pltpu.SemaphoreType.DMA((7 * sz,))]
    thru = [pltpu.HBM(a.shape, a.dtype) for a in srcs + lands]
    n_sem = len(sem_shapes)
    outs = pl.pallas_call(
        body, name=name,
        out_shape=tuple(sem_shapes + thru + [_sds((8, LANES))]),
        in_specs=[HBM] * (2 * n) + [ANY] * n_dep,
        out_specs=tuple([SEM] * n_sem + [HBM] * (2 * n) + [pl.BlockSpec(memory_space=pltpu.VMEM)]),
        input_output_aliases={i: n_sem + i for i in range(2 * n)},
        compiler_params=pltpu.CompilerParams(has_side_effects=EFFECT),
    )(*[pltpu.with_memory_space_constraint(a, pltpu.HBM) for a in srcs + lands], *([] if dep is None else [dep]))
    sems, thru_src, thru_land, token = outs[:n_sem], outs[n_sem:n_sem + n], outs[n_sem + n:n_sem + 2 * n], outs[-1]
    result, off = [], 0
    for gi, sz in enumerate(sizes):
        result.append((sems[2 * gi], sems[2 * gi + 1], list(thru_src[off:off + sz]), list(thru_land[off:off + sz])))
        off += sz
    return result, token


def _comm_wait(group, after, name, exchange):
    send_sems, recv_sems, srcs, lands = group
    n = len(srcs)
    after = list(after) if isinstance(after, (list, tuple)) else [after]

    def body(*refs):
        src_refs, land_refs = refs[:n], refs[n:2 * n]
        ssem, rsem = refs[2 * n], refs[2 * n + 1]
        me, peers = _peer_list()
        for i in range(n):
            for k, peer in enumerate(peers):
                cp = _split_copy(src_refs[i], land_refs[i], ssem, rsem, i, k, peer, _dev_index(peer), exchange)
                cp.wait_send()
                cp.wait_recv()

    outs = pl.pallas_call(
        body, name=name,
        out_shape=tuple(pltpu.HBM(a.shape, a.dtype) for a in srcs + lands),
        in_specs=[HBM] * (2 * n) + [SEM, SEM] + [ANY] * len(after),
        out_specs=tuple([HBM] * (2 * n)),
        input_output_aliases={i: i for i in range(2 * n)},
        compiler_params=pltpu.CompilerParams(has_side_effects=EFFECT),
    )(*srcs, *lands, send_sems, recv_sems, *after)
    return list(outs[n:])


def _tie(a, token):
    return a + token[0, 0].astype(a.dtype)


def _pack(arrs, rows):
    flat = jnp.concatenate([a.reshape(-1).astype(F32) for a in arrs])
    return jnp.pad(flat, (0, rows * LANES - flat.shape[0])).reshape(rows, LANES)


def _unpack(packed, shapes):
    flat = packed.reshape(-1)
    out, off = [], 0
    for s in shapes:
        n = math.prod(s)
        out.append(flat[off:off + n].reshape(s))
        off += n
    return out


def _packed_rows(shapes):
    n = sum(math.prod(s) for s in shapes)
    unit = N_DEV * 8 * LANES
    return -(-n // unit) * unit // LANES


def kernel(x, mix_pre_g, mix_post_g, mlp_pre_g, mlp_post_g, w_in_even, s5_lam_re, s5_lam_im, s5_log_dt, s5_b_re, s5_b_im, s5_c_re, s5_c_im, s5_d, s5_w_glu, fox_b_f, w_out_even, w_in_odd, pool_w, pool_scale, sgu_ln_g, sgu_ln_b, sgu_w_s, sgu_b_s, w_out_odd, mlp_w1, mlp_w2, loss_target, m_mix_pre_g, m_mix_post_g, m_mlp_pre_g, m_mlp_post_g, m_w_in_even, m_s5_lam_re, m_s5_lam_im, m_s5_log_dt, m_s5_b_re, m_s5_b_im, m_s5_c_re, m_s5_c_im, m_s5_d, m_s5_w_glu, m_fox_b_f, m_w_out_even, m_w_in_odd, m_pool_w, m_pool_scale, m_sgu_ln_g, m_sgu_ln_b, m_sgu_w_s, m_sgu_b_s, m_w_out_odd, m_mlp_w1, m_mlp_w2, v_mix_pre_g, v_mix_post_g, v_mlp_pre_g, v_mlp_post_g, v_w_in_even, v_s5_lam_re, v_s5_lam_im, v_s5_log_dt, v_s5_b_re, v_s5_b_im, v_s5_c_re, v_s5_c_im, v_s5_d, v_s5_w_glu, v_fox_b_f, v_w_out_even, v_w_in_odd, v_pool_w, v_pool_scale, v_sgu_ln_g, v_sgu_ln_b, v_sgu_w_s, v_sgu_b_s, v_w_out_odd, v_mlp_w1, v_mlp_w2):
    weights = dict(mix_pre_g=mix_pre_g, mix_post_g=mix_post_g, mlp_pre_g=mlp_pre_g, mlp_post_g=mlp_post_g, w_in_even=w_in_even, s5_lam_re=s5_lam_re, s5_lam_im=s5_lam_im, s5_log_dt=s5_log_dt, s5_b_re=s5_b_re, s5_b_im=s5_b_im, s5_c_re=s5_c_re, s5_c_im=s5_c_im, s5_d=s5_d, s5_w_glu=s5_w_glu, fox_b_f=fox_b_f, w_out_even=w_out_even, w_in_odd=w_in_odd, pool_w=pool_w, pool_scale=pool_scale, sgu_ln_g=sgu_ln_g, sgu_ln_b=sgu_ln_b, sgu_w_s=sgu_w_s, sgu_b_s=sgu_b_s, w_out_odd=w_out_odd, mlp_w1=mlp_w1, mlp_w2=mlp_w2)
    mom_m = dict(mix_pre_g=m_mix_pre_g, mix_post_g=m_mix_post_g, mlp_pre_g=m_mlp_pre_g, mlp_post_g=m_mlp_post_g, w_in_even=m_w_in_even, s5_lam_re=m_s5_lam_re, s5_lam_im=m_s5_lam_im, s5_log_dt=m_s5_log_dt, s5_b_re=m_s5_b_re, s5_b_im=m_s5_b_im, s5_c_re=m_s5_c_re, s5_c_im=m_s5_c_im, s5_d=m_s5_d, s5_w_glu=m_s5_w_glu, fox_b_f=m_fox_b_f, w_out_even=m_w_out_even, w_in_odd=m_w_in_odd, pool_w=m_pool_w, pool_scale=m_pool_scale, sgu_ln_g=m_sgu_ln_g, sgu_ln_b=m_sgu_ln_b, sgu_w_s=m_sgu_w_s, sgu_b_s=m_sgu_b_s, w_out_odd=m_w_out_odd, mlp_w1=m_mlp_w1, mlp_w2=m_mlp_w2)
    mom_v = dict(mix_pre_g=v_mix_pre_g, mix_post_g=v_mix_post_g, mlp_pre_g=v_mlp_pre_g, mlp_post_g=v_mlp_post_g, w_in_even=v_w_in_even, s5_lam_re=v_s5_lam_re, s5_lam_im=v_s5_lam_im, s5_log_dt=v_s5_log_dt, s5_b_re=v_s5_b_re, s5_b_im=v_s5_b_im, s5_c_re=v_s5_c_re, s5_c_im=v_s5_c_im, s5_d=v_s5_d, s5_w_glu=v_s5_w_glu, fox_b_f=v_fox_b_f, w_out_even=v_w_out_even, w_in_odd=v_w_in_odd, pool_w=v_pool_w, pool_scale=v_pool_scale, sgu_ln_g=v_sgu_ln_g, sgu_ln_b=v_sgu_ln_b, sgu_w_s=v_sgu_w_s, sgu_b_s=v_sgu_b_s, w_out_odd=v_w_out_odd, mlp_w1=v_mlp_w1, mlp_w2=v_mlp_w2)
    names = list(weights)
    L = x.shape[1]
    x0 = x[0]
    target = loss_target[0]
    my_index = 4 * lax.axis_index("x") + 2 * lax.axis_index("y") + lax.axis_index("c")

    small_vec = jnp.zeros((8, LANES), F32)
    small_vec = small_vec.at[0, :64].set(pool_scale[0]).at[1, :64].set(sgu_ln_g[0]).at[2, :64].set(sgu_ln_b[0])
    ag_groups, ag_token = _comm_start(
        [[jnp.transpose(w_in_even[0]).astype(BF16), small_vec],
         [s5_w_glu[0].astype(BF16), w_out_even[0].astype(BF16)],
         [mlp_w1[0].astype(BF16), mlp_w2[0].astype(BF16)],
         [jnp.transpose(w_in_odd[0]).astype(BF16), w_out_odd[0].astype(BF16), mlp_w1[1].astype(BF16), mlp_w2[1].astype(BF16)]],
        "ag_start", exchange=False)

    lam_r = jnp.concatenate([s5_lam_re.reshape(1, S5_NS), s5_lam_im.reshape(1, S5_NS)], axis=0)
    ldt_r = jnp.repeat(s5_log_dt.reshape(32), 64).reshape(1, S5_NS)
    lam_c = jnp.transpose(lam_r)
    ldt_c = jnp.transpose(ldt_r)
    b_t = jnp.stack([jnp.tile(s5_b_re.reshape(S5_NS, 16), (1, 8)), jnp.tile(s5_b_im.reshape(S5_NS, 16), (1, 8))])
    c_t = jnp.stack([jnp.tile(s5_c_re.reshape(S5_W, 64), (1, 8)), jnp.tile(s5_c_im.reshape(S5_W, 64), (1, 8))])
    bf_pad = jnp.pad(fox_b_f, ((0, 0), (0, LANES - 8)))
    b_st = jnp.transpose(sgu_b_s[0])

    h0, rx0 = _rms_fwd(x0, _tie(mix_pre_g[0:1], ag_token), "rms0")
    tabs, bset, cset = _s5_prep(lam_r, ldt_r, lam_c, ldt_c, b_t, c_t, "s5_prep")
    ag0 = _comm_wait(ag_groups[0], tabs, "ag_wait0", exchange=False)
    winT_e = jnp.pad(ag0[0].reshape(EVEN_IN, D_MODEL), ((0, EVEN_PAD - EVEN_IN), (0, 0)))
    pool_scale_f = ag0[1][:, 0, :64].reshape(1, 512)
    ln_g_f = ag0[1][:, 1, :64].reshape(1, 512)
    ln_b_f = ag0[1][:, 2, :64].reshape(1, 512)
    z0 = _mm(h0, winT_e, name="win_even", tb=True, bm=512, bn=EVEN_PAD)
    xs = _s5_scan_fwd(z0, bset, tabs, "s5_scan")
    ag1 = _comm_wait(ag_groups[1], xs, "ag_wait1", exchange=False)
    wglu = ag1[0].reshape(S5_W, S5_W)
    wout_e = ag1[1].reshape(D_MODEL, D_MODEL)
    ylin, ya = _s5_out_fwd(xs, cset, z0, s5_d, wglu, "s5_out")
    fcum, fq = _fox_f_fwd(z0, bf_pad, "fox_f")
    frow = jnp.transpose(fcum[:, :8]).reshape(4, 2, L)
    o_att, lse = _fox_fwd(z0, fq, frow, "fox_fwd")
    mix0 = [ya, o_att]
    x1, ry0, h1, rx1, y0 = _mm(mix0, wout_e, name="wout_even", epi=_epi_post_pre, extra=(x0,),
                               vecs=(mix_post_g[0:1], mlp_pre_g[0:1]), out_dtypes=POST_PRE_DTYPES,
                               out_kinds=POST_PRE_KINDS, bm=FUSED_ROWS)
    ag2 = _comm_wait(ag_groups[2], rx1, "ag_wait2", exchange=False)
    w1 = [ag2[0], None]
    w2 = [ag2[1].reshape(4 * D_MODEL, D_MODEL), None]
    p0, a0 = _mm(h1, w1[0], name="mlp0_w1", b3=True, out_dtypes=(BF16, BF16), epi=_epi_relu2, bm=512, bn=4 * D_MODEL)
    x2, ro0, h2, rx2, o0 = _mm(a0, w2[0], name="mlp0_w2", epi=_epi_post_pre, extra=(x1,),
                               vecs=(mlp_post_g[0:1], mix_pre_g[1:2]), out_dtypes=POST_PRE_DTYPES,
                               out_kinds=POST_PRE_KINDS, bm=FUSED_ROWS, bk=4 * D_MODEL)
    ag3 = _comm_wait(ag_groups[3], rx2, "ag_wait3", exchange=False)
    winT_o = ag3[0].reshape(ODD_IN, D_MODEL)
    wout_o = ag3[1].reshape(D_MODEL, D_MODEL)
    w1[1] = ag3[2]
    w2[1] = ag3[3].reshape(4 * D_MODEL, D_MODEL)
    z1 = _mm(h2, winT_o, name="win_odd", tb=True, bn=ODD_IN)
    yc, pooled = _pool_fwd(z1, pool_w[0], pool_scale_f, "pool_fwd")
    yd = _sgu_fwd(z1, ln_g_f, ln_b_f, sgu_w_s[0], b_st, "sgu_fwd")
    mix1 = [yc, yd]
    x3, ry1, h3, rx3, y1 = _mm(mix1, wout_o, name="wout_odd", epi=_epi_post_pre, extra=(x2,),
                               vecs=(mix_post_g[1:2], mlp_pre_g[1:2]), out_dtypes=POST_PRE_DTYPES,
                               out_kinds=POST_PRE_KINDS, bm=FUSED_ROWS)
    p1, a1 = _mm(h3, w1[1], name="mlp1_w1", b3=True, out_dtypes=(BF16, BF16), epi=_epi_relu2, bm=512, bn=4 * D_MODEL)
    gx4, g_o1, gg_mlp_post1, sq_lanes = _mm(
        a1, w2[1], name="mlp1_w2", epi=_epi_post_loss, extra=(x3, target), vecs=(mlp_post_g[1:2],),
        out_dtypes=(F32, BF16, F32, F32), out_kinds=("full", "full", "vsum", "vsum"), bm=FUSED_ROWS, bk=4 * D_MODEL)
    sq = sq_lanes[:, 0:1]

    g_p1 = _mm(g_o1, w2[1], name="b_mlp1_a", tb=True, out_dtypes=(BF16,), epi=_epi_relu2_bwd, extra=(p1,),
               bm=512, bn=4 * D_MODEL)
    gw2_1 = _mm(a1, g_o1, name="b_mlp1_w2", ta=True, bm=512, bk=L)
    gw1_1 = _mm(h3, g_p1, name="b_mlp1_w1", ta=True, out3=True, bn=512, bk=L)
    (ex1,), tok1 = _comm_start([[gw1_1, gw2_1.reshape(N_DEV, 512, D_MODEL)]], "ex_start1", exchange=True)
    g_x3, gg_mlp_pre1, g_y1, gg_mix_post1 = _mm(
        g_p1, w1[1], name="b_mlp1_h", tb=True, b3=True, epi=_epi_pre_post_bwd, extra=(x3, gx4, y1), cols=(rx3, ry1),
        vecs=(_tie(mlp_pre_g[1:2], tok1), mix_post_g[1:2]), out_dtypes=PRE_POST_BWD_DTYPES,
        out_kinds=PRE_POST_BWD_KINDS, bm=FUSED_ROWS, bk=4 * D_MODEL)
    g_mix1 = _mm(g_y1, wout_o, name="b_wout_odd_m", tb=True)
    gwout_o = _mm(mix1, g_y1, name="b_wout_odd_w", ta=True)
    g_xc, g_pool_w, g_pool_scale = _pool_bwd(g_mix1, pooled, pool_w[0], pool_scale_f, "pool_bwd")
    g_u1, g_v1, g_ws, g_bst, g_ln_g, g_ln_b = _sgu_bwd(g_mix1, z1, ln_g_f, ln_b_f, sgu_w_s[0], b_st, "sgu_bwd")
    g_z1 = [g_xc, g_u1, g_v1]
    gwinT_o = _mm(g_z1, h2, name="b_win_odd_w", ta=True)
    (ex2,), tok2 = _comm_start([[gwout_o.reshape(N_DEV, 128, D_MODEL), gwinT_o.reshape(N_DEV, ODD_IN // N_DEV, D_MODEL)]], "ex_start2", exchange=True)
    g_x2, gg_mix_pre1, g_o0, gg_mlp_post0 = _mm(
        g_z1, winT_o, name="b_win_odd_h", epi=_epi_pre_post_bwd, extra=(x2, g_x3, o0), cols=(rx2, ro0),
        vecs=(_tie(mix_pre_g[1:2], tok2), mlp_post_g[0:1]), out_dtypes=PRE_POST_BWD_DTYPES,
        out_kinds=PRE_POST_BWD_KINDS, bm=FUSED_ROWS)
    g_p0 = _mm(g_o0, w2[0], name="b_mlp0_a", tb=True, out_dtypes=(BF16,), epi=_epi_relu2_bwd, extra=(p0,),
               bm=512, bn=4 * D_MODEL)
    gw2_0 = _mm(a0, g_o0, name="b_mlp0_w2", ta=True, bm=512, bk=L)
    gw1_0 = _mm(h1, g_p0, name="b_mlp0_w1", ta=True, out3=True, bn=512, bk=L)
    (ex3,), tok3 = _comm_start([[gw1_0, gw2_0.reshape(N_DEV, 512, D_MODEL)]], "ex_start3", exchange=True)
    g_x1, gg_mlp_pre0, g_y0, gg_mix_post0 = _mm(
        g_p0, w1[0], name="b_mlp0_h", tb=True, b3=True, epi=_epi_pre_post_bwd, extra=(x1, g_x2, y0), cols=(rx1, ry0),
        vecs=(_tie(mlp_pre_g[0:1], tok3), mix_post_g[0:1]), out_dtypes=PRE_POST_BWD_DTYPES,
        out_kinds=PRE_POST_BWD_KINDS, bm=FUSED_ROWS, bk=4 * D_MODEL)
    g_mix0 = _mm(g_y0, wout_e, name="b_wout_even_m", tb=True)
    gwout_e = _mm(mix0, g_y0, name="b_wout_even_w", ta=True)
    gyl, gud, g_wglu, g_d = _s5_glu_bwd(g_mix0, ylin, z0, s5_d, wglu, "s5_glu_bwd")
    (ex4,), tok4 = _comm_start([[gwout_e.reshape(N_DEV, 128, D_MODEL), g_wglu.reshape(N_DEV, 64, S5_W)]], "ex_start4", exchange=True)
    g_u0, ga, gb_raw, gc_raw = _s5_scan_bwd(gyl, _tie(cset, tok4), xs, z0, bset, gud, tabs, "s5_scan_bwd")
    g_lam, g_ldt, g_b, g_c = _s5_param_bwd(lam_c, ldt_c, b_t, gb_raw, jnp.transpose(ga), gc_raw, "s5_param_bwd")
    dq, dk, dv, dfq, dfrow = _fox_bwd(z0, fq, frow, o_att, lse, g_mix0, "fox_bwd")
    dFk = jnp.pad(jnp.transpose(dfrow.reshape(8, L)), ((0, 0), (0, LANES - 8)))
    dfl, db_f = _fox_f_bwd(dFk, dfq, z0, bf_pad, "fox_f_bwd")
    g_z0 = [g_u0, dq, dk, dv, dfl]
    grad_x, gg_mix_pre0 = _mm(g_z0, winT_e, name="b_win_even_h", epi=_epi_pre_bwd, extra=(x0, g_x1), cols=(rx0,),
                              vecs=(mix_pre_g[0:1],), out_dtypes=(F32, F32), out_kinds=("full", "vsum"),
                              bm=FUSED_ROWS)

    small_grads = dict(
        mix_pre_g=jnp.concatenate([gg_mix_pre0, gg_mix_pre1]), mix_post_g=jnp.concatenate([gg_mix_post0, gg_mix_post1]),
        mlp_pre_g=jnp.concatenate([gg_mlp_pre0, gg_mlp_pre1]), mlp_post_g=jnp.concatenate([gg_mlp_post0, gg_mlp_post1]),
        s5_lam_re=g_lam[:, 0], s5_lam_im=g_lam[:, 1], s5_log_dt=g_ldt,
        s5_b_re=g_b[0, :, :16], s5_b_im=g_b[1, :, :16], s5_c_re=g_c[0, :, :64], s5_c_im=g_c[1, :, :64],
        s5_d=g_d, fox_b_f=db_f[:, :8], pool_w=g_pool_w, sgu_w_s=g_ws, sgu_b_s=jnp.transpose(g_bst),
        pool_scale=g_pool_scale, sgu_ln_g=g_ln_g, sgu_ln_b=g_ln_b)
    small_names = list(small_grads)
    full_shapes = [(512,) if nm in ("pool_scale", "sgu_ln_g", "sgu_ln_b") else weights[nm].shape for nm in small_names]
    full_shapes.append((1, 1))
    rows = _packed_rows(full_shapes)
    packed = _pack([small_grads[nm] for nm in small_names] + [sq], rows).reshape(N_DEV, rows // N_DEV, LANES)
    (exs,), tok_s = _comm_start([[packed]], "exs_start", exchange=True)
    gwinT_e = _mm(g_z0, h0, name="b_win_even_w", ta=True, bk=512, out_dtypes=(BF16,), dep=tok_s)
    (recv_small,) = _comm_wait(exs, gwinT_e, "exs_wait", exchange=True)
    piece = _sum_pieces(recv_small, "sum_small")
    (ags,), tok_a = _comm_start([[piece]], "ags_start", exchange=False)

    gwinT_e_pieces = gwinT_e[:EVEN_IN].reshape(N_DEV, EVEN_IN // N_DEV, D_MODEL)
    (ex5,), tok5 = _comm_start([[gwinT_e_pieces]], "ex_start5", exchange=True, dep=tok_a)
    r_w1_1, r_w2_1 = _comm_wait(ex1, tok5, "ex_wait1", exchange=True)
    r_wout_o, r_win_o = _comm_wait(ex2, tok5, "ex_wait2", exchange=True)
    r_w1_0, r_w2_0 = _comm_wait(ex3, tok5, "ex_wait3", exchange=True)
    r_wout_e, r_wglu = _comm_wait(ex4, tok5, "ex_wait4", exchange=True)

    res = {}
    for nm, parts in (("mlp_w1", (r_w1_0, r_w1_1)), ("mlp_w2", (r_w2_0, r_w2_1))):
        first = _sum_adamw(parts[0], weights[nm], mom_m[nm], mom_v[nm], "adamw_%s_0" % nm, layer=0)
        res[nm] = tuple(_sum_adamw(parts[1], weights[nm], mom_m[nm], mom_v[nm], "adamw_%s_1" % nm, layer=1, prev=first))
    big_parts = dict(s5_w_glu=r_wglu, w_out_even=r_wout_e, w_out_odd=r_wout_o)
    for nm, parts in big_parts.items():
        res[nm] = tuple(_sum_adamw(parts, weights[nm], mom_m[nm], mom_v[nm], "adamw_" + nm))
    done = [res[nm][1] for nm in ("mlp_w1", "mlp_w2", "s5_w_glu", "w_out_even", "w_out_odd")]

    (small_all,) = _comm_wait(ags, done, "ags_wait", exchange=False)
    small_full = _unpack(small_all.reshape(rows, LANES), full_shapes)
    loss = 0.5 * small_full.pop()[0, 0] / D_MODEL
    small_g = []
    for nm, g in zip(small_names, small_full):
        if nm in ("pool_scale", "sgu_ln_g", "sgu_ln_b"):
            g = lax.dynamic_slice(g, (my_index * 64,), (64,)).reshape(1, 64)
        small_g.append(g)
    sd, sm, sv = _adamw_many([weights[nm] for nm in small_names], small_g, [mom_m[nm] for nm in small_names],
                             [mom_v[nm] for nm in small_names], "adamw_small")
    for nm, g_, d_, m_, v_ in zip(small_names, small_g, sd, sm, sv):
        res[nm] = (g_, d_, m_, v_)
    done.append(sd[0])

    for nm, parts in (("w_in_odd", r_win_o), ("w_in_even", None)):
        if parts is None:
            (parts,) = _comm_wait(ex5, done, "ex_wait5", exchange=True)
        outs = _sum_adamw(parts, jnp.transpose(weights[nm], (0, 2, 1)), jnp.transpose(mom_m[nm], (0, 2, 1)),
                          jnp.transpose(mom_v[nm], (0, 2, 1)), "adamw_" + nm)
        res[nm] = tuple(jnp.transpose(o, (0, 2, 1)) for o in outs)
        done.append(res[nm][1])

    grads = [res[nm][0].reshape(weights[nm].shape) for nm in names]
    deltas = [res[nm][1].reshape(weights[nm].shape) for nm in names]
    new_m = [res[nm][2].reshape(weights[nm].shape) for nm in names]
    new_v = [res[nm][3].reshape(weights[nm].shape) for nm in names]
    return (loss, grad_x[None], *grads, *deltas, *new_m, *new_v)
```

```python
import functools
import math

import jax
import jax.numpy as jnp
from jax import lax
from jax.experimental import pallas as pl
from jax.experimental.pallas import tpu as pltpu

F32 = jnp.float32
BF16 = jnp.bfloat16
MESH = pl.DeviceIdType.MESH
ANY = pl.BlockSpec(memory_space=pl.ANY)

N_DEV = 8
D_MODEL = 1024
EPS = 1e-6
NORM_ROWS = 512
FUSED_ROWS = 512
S5_W = 512
S5_NS = 2048
SCAN_GROUPS = 4
SCAN_CHUNK = 1024
FOX_W = 512
EVEN_IN = 2056
EVEN_PAD = 2176
ODD_IN = 1536
LANES = 128
PIECE = 4 * D_MODEL // N_DEV
VMEM_LIMIT = 56 * 1024 * 1024

ADAM_LR = 0.001
ADAM_B1 = 0.9
ADAM_B2 = 0.999
ADAM_EPS = 1e-08
ADAM_WD = 0.01
ADAM_STEP = 10

NT = (((1,), (1,)), ((), ()))
TN = (((0,), (0,)), ((), ()))
NN = (((1,), (0,)), ((), ()))


def _cp(*sem):
    return pltpu.CompilerParams(dimension_semantics=sem, vmem_limit_bytes=VMEM_LIMIT)


def _sds(shape, dtype=F32):
    return jax.ShapeDtypeStruct(tuple(shape), dtype)


def _gelu(x):
    t = jnp.tanh(0.7978845608028654 * (x + 0.044715 * x * x * x))
    return 0.5 * x * (1.0 + t)


def _gelu_grad(x):
    t = jnp.tanh(0.7978845608028654 * (x + 0.044715 * x * x * x))
    du = 0.7978845608028654 * (1.0 + 3.0 * 0.044715 * x * x)
    return 0.5 * (1.0 + t) + 0.5 * x * (1.0 - t * t) * du


def _sigmoid(x):
    return 1.0 / (1.0 + jnp.exp(-x))


def _dot(a, b, dn=NN):
    return lax.dot_general(a, b, dn, preferred_element_type=F32)


def _mm(a, b, *, name, ta=False, tb=False, b3=False, out3=False, out_dtypes=(F32,), epi=None, extra=(),
        cols=(), vecs=(), out_kinds=None, bm=1024, bn=1024, bk=1024, dep=None):
    a_list = list(a) if isinstance(a, (list, tuple)) else [a]
    widths = [p.shape[1] for p in a_list]
    offs = [sum(widths[:i]) for i in range(len(widths))]
    na = len(a_list)
    M = sum(widths) if ta else a_list[0].shape[0]
    K = a_list[0].shape[0] if ta else sum(widths)
    if na > 1:
        assert not b3 and not tb
        bm, bk = (M, bk) if ta else (bm, K)
    pw = b.shape[2] if b3 else PIECE
    if b3:
        N = b.shape[1] if tb else b.shape[0] * pw
        assert (b.shape[0] * pw if tb else b.shape[1]) == K
    else:
        N = b.shape[0] if tb else b.shape[1]
    bm, bn, bk = min(bm, M), min(bn, N), min(bk, K)
    assert M % bm == 0 and N % bn == 0 and K % bk == 0, (name, M, N, K, bm, bn, bk)
    assert not (b3 or out3) or ((bk if tb else bn) % pw == 0 and bn % PIECE == 0)
    nk = K // bk
    n_extra = len(extra) + len(cols) + len(vecs)
    n_out = len(out_dtypes)
    out_kinds = tuple(out_kinds) if out_kinds is not None else ("full",) * n_out
    dn = (((0 if ta else 1,), (1 if tb else 0,)), ((), ()))

    use_acc = nk > 1

    def body(*refs):
        a_refs, b_ref = refs[:na], refs[na]
        a_ref = a_refs[0]
        e_refs = refs[na + 1:na + 1 + n_extra]
        first_out = na + 1 + n_extra + (0 if dep is None else 1)
        o_refs = refs[first_out:first_out + n_out]
        acc_ref = refs[-1] if use_acc else o_refs[0]
        i, k = pl.program_id(0), pl.program_id(2)

        def dot(a_v, b_v):
            return lax.dot_general(a_v.astype(BF16), b_v.astype(BF16), dn, preferred_element_type=F32)

        everything = slice(None)
        if na > 1 and ta:
            terms = [(pl.ds(off, w), everything, r, b_ref) for r, off, w in zip(a_refs, offs, widths)]
        elif na > 1:
            terms = [(everything, everything, r, b_ref.at[pl.ds(off, w), :]) for r, off, w in zip(a_refs, offs, widths)]
        elif not b3:
            terms = [(everything, everything, a_ref, b_ref)]
        elif tb:
            terms = [(everything, everything,
                      a_ref.at[pl.ds(t * pw, pw), :] if ta else a_ref.at[:, pl.ds(t * pw, pw)], b_ref.at[t])
                     for t in range(bk // pw)]
        else:
            terms = [(everything, pl.ds(t * pw, pw), a_ref, b_ref.at[t]) for t in range(bn // pw)]

        def finish(acc):
            outs = (acc,) if epi is None else epi(acc, *[e[...] for e in e_refs])
            for o_ref, o, kind in zip(o_refs, outs, out_kinds):
                if kind == "vsum":
                    @pl.when(i == 0)
                    def _(o_ref=o_ref, o=o):
                        o_ref[...] = o

                    @pl.when(i > 0)
                    def _(o_ref=o_ref, o=o):
                        o_ref[...] += o
                elif out3:
                    for t in range(bn // PIECE):
                        o_ref[t] = o[:, t * PIECE:(t + 1) * PIECE].astype(o_ref.dtype)
                else:
                    o_ref[...] = o.astype(o_ref.dtype)

        if nk == 1:
            bands = {}
            for rows, cols, a_r, b_r in terms:
                key = (getattr(rows, "start", None), getattr(cols, "start", None))
                val = dot(a_r[...], b_r[...])
                bands[key] = val if key not in bands else bands[key] + val
            vals = list(bands.values())
            if len(vals) == 1:
                finish(vals[0])
            else:
                finish(jnp.concatenate(vals, axis=0 if (na > 1 and ta) else 1))
            return

        @pl.when(k == 0)
        def _():
            acc_ref[...] = jnp.zeros_like(acc_ref)

        for rows, cols, a_r, b_r in terms:
            acc_ref[rows, cols] += dot(a_r[...], b_r[...])

        @pl.when(k == nk - 1)
        def _():
            finish(acc_ref[...])

    if na > 1:
        a_specs = [pl.BlockSpec((bk, w), lambda i, j, k: (k, 0)) if ta else pl.BlockSpec((bm, w), lambda i, j, k: (i, 0))
                   for w in widths]
    else:
        a_specs = [pl.BlockSpec((bk, bm), lambda i, j, k: (k, i)) if ta else
                   pl.BlockSpec((bm, bk), lambda i, j, k: (i, k))]
    if b3:
        if tb:
            b_spec = pl.BlockSpec((bk // pw, bn, pw), lambda i, j, k: (k, j, 0))
        else:
            b_spec = pl.BlockSpec((bn // pw, bk, pw), lambda i, j, k: (j, k, 0))
    else:
        b_spec = pl.BlockSpec((bn, bk), lambda i, j, k: (j, k)) if tb else pl.BlockSpec((bk, bn), lambda i, j, k: (k, j))
    e_specs = ([pl.BlockSpec((bm, bn), lambda i, j, k: (i, j)) for _ in extra]
               + [pl.BlockSpec((bm, 1), lambda i, j, k: (i, 0)) for _ in cols]
               + [pl.BlockSpec((1, bn), lambda i, j, k: (0, j)) for _ in vecs])
    if out3:
        o_specs = [pl.BlockSpec((bn // PIECE, bm, PIECE), lambda i, j, k: (j, i, 0)) for _ in out_dtypes]
        o_shapes = [_sds((N // PIECE, M, PIECE), dt) for dt in out_dtypes]
    else:
        spec_of = {"full": pl.BlockSpec((bm, bn), lambda i, j, k: (i, j)),
                   "col": pl.BlockSpec((bm, 1), lambda i, j, k: (i, 0)),
                   "vsum": pl.BlockSpec((1, bn), lambda i, j, k: (0, j))}
        shape_of = {"full": (M, N), "col": (M, 1), "vsum": (1, N)}
        o_specs = [spec_of[kind] for kind in out_kinds]
        o_shapes = [_sds(shape_of[kind], dt) for kind, dt in zip(out_kinds, out_dtypes)]
    assert "col" not in out_kinds or bn == N
    outs = pl.pallas_call(
        body, name=name, grid=(M // bm, N // bn, nk),
        in_specs=a_specs + [b_spec] + e_specs + ([] if dep is None else [ANY]),
        out_specs=o_specs, out_shape=o_shapes,
        scratch_shapes=[pltpu.VMEM((bm, bn), F32)] if use_acc else [],
        compiler_params=_cp("arbitrary" if "vsum" in out_kinds else "parallel", "parallel", "arbitrary"),
    )(*a_list, b, *extra, *cols, *vecs, *([] if dep is None else [dep]))
    return outs[0] if n_out == 1 else outs


def _epi_relu2(acc):
    r = jnp.maximum(acc, 0.0)
    return acc, r * r


def _epi_relu2_bwd(acc, p):
    return (acc * (2.0 * jnp.maximum(p.astype(F32), 0.0)),)


def _row_spec(rb, w=D_MODEL):
    return pl.BlockSpec((rb, w), lambda i: (i, 0))


def _vec_spec(w=D_MODEL):
    return pl.BlockSpec((1, w), lambda i: (0, 0))


def _rstd(v):
    return lax.rsqrt(jnp.mean(v * v, axis=-1, keepdims=True) + EPS)


def _rms_fwd(x, g, name):
    L = x.shape[0]
    rb = min(NORM_ROWS, L)

    def body(x_ref, g_ref, h_ref, r_ref):
        xv = x_ref[...]
        r = _rstd(xv)
        h_ref[...] = (xv * r * g_ref[...]).astype(BF16)
        r_ref[...] = r

    return pl.pallas_call(
        body, name=name, grid=(L // rb,),
        in_specs=[_row_spec(rb), _vec_spec()],
        out_specs=[_row_spec(rb), _row_spec(rb, 1)],
        out_shape=[_sds((L, D_MODEL), BF16), _sds((L, 1))],
        compiler_params=_cp("parallel"),
    )(x, g)


def _rms_bwd_rows(dy, xv, r, g):
    n = xv * r
    dyg = dy * g
    return r * (dyg - n * jnp.mean(dyg * n, axis=-1, keepdims=True)), n


POST_PRE_DTYPES = (F32, F32, BF16, F32, F32)
POST_PRE_KINDS = ("full", "col", "full", "col", "full")
PRE_POST_BWD_DTYPES = (F32, F32, BF16, F32)
PRE_POST_BWD_KINDS = ("full", "vsum", "full", "vsum")


def _epi_post_pre(y, x_in, g_post, g_pre):
    ry = _rstd(y)
    xo = x_in + y * ry * g_post
    rx = _rstd(xo)
    return xo, ry, xo * rx * g_pre, rx, y


def _epi_pre_post_bwd(gh, x, g_out, y_prev, rx, ry_prev, g_pre, g_post_prev):
    gx, n = _rms_bwd_rows(gh, x, rx, g_pre)
    gi = g_out + gx
    gy, ny = _rms_bwd_rows(gi, y_prev, ry_prev, g_post_prev)
    return gi, jnp.sum(gh * n, axis=0, keepdims=True), gy, jnp.sum(gi * ny, axis=0, keepdims=True)


def _epi_pre_bwd(gh, x, g_out, rx, g_pre):
    gx, n = _rms_bwd_rows(gh, x, rx, g_pre)
    return g_out + gx, jnp.sum(gh * n, axis=0, keepdims=True)


def _epi_post_loss(y, x_in, target, g_post):
    ry = _rstd(y)
    diff = x_in + y * ry * g_post - target
    gx = diff * (1.0 / D_MODEL)
    gy, n = _rms_bwd_rows(gx, y, ry, g_post)
    sq = jnp.broadcast_to(jnp.sum(diff * diff, keepdims=True), (1, y.shape[1]))
    return gx, gy, jnp.sum(gx * n, axis=0, keepdims=True), sq


def _cmul(ar, ai, br, bi):
    return ar * br - ai * bi, ar * bi + ai * br


def _zoh_cols(lr, li, ldt):
    dt = jnp.exp(ldt)
    mag = jnp.exp(lr * dt)
    ar = mag * jnp.cos(li * dt)
    ai = mag * jnp.sin(li * dt)
    den = lr * lr + li * li
    nr = ar - 1.0
    qr = (nr * lr + ai * li) / den
    qi = (ai * lr - nr * li) / den
    return dt, ar, ai, qr, qi, den


def _b_mask():
    r = lax.broadcasted_iota(jnp.int32, (S5_NS, LANES), 0)
    c = lax.broadcasted_iota(jnp.int32, (S5_NS, LANES), 1)
    return ((r >> 6) & 7) == (c >> 4)


def _c_mask():
    r = lax.broadcasted_iota(jnp.int32, (S5_W, 512), 0)
    c = lax.broadcasted_iota(jnp.int32, (S5_W, 512), 1)
    return ((r >> 4) & 7) == (c >> 6)


def _s5_prep(lam_r, ldt_r, lam_c, ldt_c, b_t, c_t, name):
    def body(lam_r_ref, ldt_r_ref, lam_c_ref, ldt_c_ref, b_ref, c_ref, tab_ref, bset_ref, cset_ref):
        lr, li = lam_r_ref[0:1, :], lam_r_ref[1:2, :]
        dt = jnp.exp(ldt_r_ref[...])
        mag = jnp.exp(lr * dt)
        p1r, p1i = mag * jnp.cos(li * dt), mag * jnp.sin(li * dt)
        p2r, p2i = _cmul(p1r, p1i, p1r, p1i)
        p3r, p3i = _cmul(p2r, p2i, p1r, p1i)
        p4r, p4i = _cmul(p2r, p2i, p2r, p2i)
        p5r, p5i = _cmul(p4r, p4i, p1r, p1i)
        p6r, p6i = _cmul(p4r, p4i, p2r, p2i)
        p7r, p7i = _cmul(p4r, p4i, p3r, p3i)
        p8r, p8i = _cmul(p4r, p4i, p4r, p4i)
        pw_r = [p1r, p2r, p3r, p4r, p5r, p6r, p7r, p8r]
        pw_i = [p1i, p2i, p3i, p4i, p5i, p6i, p7i, p8i]
        row = lax.broadcasted_iota(jnp.int32, (8, S5_NS), 0)
        zero = jnp.zeros((8, S5_NS), F32)

        def bc(v):
            return jnp.broadcast_to(v, (8, S5_NS))

        for d in range(2):
            sgn = 1.0 if d == 0 else -1.0
            for t, s in enumerate((1, 2, 4)):
                live = (row >= s) if d == 0 else (row <= 7 - s)
                tab_ref[d, 2 * t] = jnp.where(live, bc(pw_r[s - 1]), zero)
                tab_ref[d, 2 * t + 1] = jnp.where(live, bc(sgn * pw_i[s - 1]), zero)
            cr, ci = zero, zero
            for i in range(8):
                e = i if d == 0 else 7 - i
                cr = jnp.where(row == i, bc(pw_r[e]), cr)
                ci = jnp.where(row == i, bc(sgn * pw_i[e]), ci)
            tab_ref[d, 6] = cr
            tab_ref[d, 7] = ci

        _, _, _, qr, qi, _ = _zoh_cols(lam_c_ref[:, 0:1], lam_c_ref[:, 1:2], ldt_c_ref[...])
        bm = _b_mask()
        br, bi = b_ref[0], b_ref[1]
        bset_ref[0] = jnp.where(bm, qr * br - qi * bi, 0.0).astype(BF16)
        bset_ref[1] = jnp.where(bm, qr * bi + qi * br, 0.0).astype(BF16)
        cm = _c_mask()
        cset_ref[0] = jnp.where(cm, c_ref[0], 0.0).astype(BF16)
        cset_ref[1] = jnp.where(cm, c_ref[1], 0.0).astype(BF16)

    vm = pl.BlockSpec(memory_space=pltpu.VMEM)
    return pl.pallas_call(
        body, name=name, in_specs=[vm] * 6, out_specs=[vm] * 3,
        out_shape=[_sds((2, 8, 8, S5_NS)), _sds((2, S5_NS, LANES), BF16), _sds((2, S5_W, 512), BF16)],
        compiler_params=pltpu.CompilerParams(vmem_limit_bytes=VMEM_LIMIT),
    )(lam_r, ldt_r, lam_c, ldt_c, b_t, c_t)


SCAN_W = SCAN_GROUPS * LANES


def _scan_chunk(src_ref, dst_ref, tab_ref, carry_ref, nb, reverse, xs_ref=None, acc_ref=None):
    row = lax.broadcasted_iota(jnp.int32, (8, LANES), 0)

    def step(i, carry):
        b = (nb - 1 - i) if reverse else i
        off = pl.multiple_of(b * 8, 8)
        out = []
        for g in range(SCAN_GROUPS):
            lanes = pl.ds(g * LANES, LANES)
            cr, ci = carry[2 * g], carry[2 * g + 1]
            yr = src_ref[0, pl.ds(off, 8), lanes]
            yi = src_ref[1, pl.ds(off, 8), lanes]
            for t, s in enumerate((1, 2, 4)):
                sh = (8 - s) if reverse else s
                sr = pltpu.roll(yr, sh, 0)
                si = pltpu.roll(yi, sh, 0)
                mr, mi = tab_ref[2 * t, :, lanes], tab_ref[2 * t + 1, :, lanes]
                yr, yi = yr + mr * sr - mi * si, yi + mr * si + mi * sr
            pr, pi = tab_ref[6, :, lanes], tab_ref[7, :, lanes]
            yr, yi = yr + pr * cr - pi * ci, yi + pr * ci + pi * cr
            dst_ref[0, pl.ds(off, 8), lanes] = yr
            dst_ref[1, pl.ds(off, 8), lanes] = yi
            if xs_ref is not None:
                nr = jnp.where(row == 7, cr, pltpu.roll(yr, 7, 0))
                ni = jnp.where(row == 7, ci, pltpu.roll(yi, 7, 0))
                xr = xs_ref[0, pl.ds(off, 8), lanes]
                xi = xs_ref[1, pl.ds(off, 8), lanes]
                acc_ref[0, :, lanes] += xr * nr + xi * ni
                acc_ref[1, :, lanes] += xr * ni - xi * nr
            last = 0 if reverse else 7
            out += [jnp.broadcast_to(yr[last:last + 1, :], (8, LANES)),
                    jnp.broadcast_to(yi[last:last + 1, :], (8, LANES))]
        return tuple(out)

    init = []
    for g in range(SCAN_GROUPS):
        init += [carry_ref[0, :, pl.ds(g * LANES, LANES)], carry_ref[1, :, pl.ds(g * LANES, LANES)]]
    fin = lax.fori_loop(0, nb, step, tuple(init))
    for g in range(SCAN_GROUPS):
        carry_ref[0, :, pl.ds(g * LANES, LANES)] = fin[2 * g]
        carry_ref[1, :, pl.ds(g * LANES, LANES)] = fin[2 * g + 1]


def _s5_scan_fwd(z, bset, cset, dvec, tabs, name):
    L = z.shape[0]
    tl = min(SCAN_CHUNK, L)
    nc = L // tl

    def body(u_ref, b_ref, c_ref, d_ref, tab_ref, x_ref, y_ref, carry_ref):
        @pl.when(pl.program_id(1) == 0)
        def _():
            carry_ref[...] = jnp.zeros_like(carry_ref)

        uf = u_ref[...]
        u = uf.astype(BF16)
        x_ref[0] = _dot(u, b_ref[0], NT)
        x_ref[1] = _dot(u, b_ref[1], NT)
        _scan_chunk(x_ref, x_ref, tab_ref, carry_ref, tl // 8, False)
        y_ref[...] = (_dot(x_ref[0].astype(BF16), c_ref[0], NT) - _dot(x_ref[1].astype(BF16), c_ref[1], NT)
                      + d_ref[...] * uf)

    col = pl.BlockSpec((tl, LANES), lambda j, c: (c, j))
    return pl.pallas_call(
        body, name=name, grid=(S5_NS // SCAN_W, nc),
        in_specs=[col, pl.BlockSpec((2, SCAN_W, LANES), lambda j, c: (0, j, 0)),
                  pl.BlockSpec((2, LANES, SCAN_W), lambda j, c: (0, j, 0)),
                  pl.BlockSpec((1, LANES), lambda j, c: (0, j)),
                  pl.BlockSpec((None, 8, 8, SCAN_W), lambda j, c: (0, 0, 0, j))],
        out_specs=[pl.BlockSpec((2, tl, SCAN_W), lambda j, c: (0, c, j)), col],
        out_shape=[_sds((2, L, S5_NS)), _sds((L, S5_W))],
        scratch_shapes=[pltpu.VMEM((2, 8, SCAN_W), F32)],
        compiler_params=_cp("parallel", "arbitrary"),
    )(z, bset, cset, dvec, tabs)


def _s5_scan_bwd(gyl, cset, xs, z, bset, gud, tabs, name):
    L = z.shape[0]
    tl = min(SCAN_CHUNK, L)
    nc = L // tl

    def body(g_ref, c_ref, xs_ref, u_ref, b_ref, gud_ref, tab_ref, gu_ref, ga_ref, gb_ref, gc_ref,
             gx_ref, carry_ref, acc_ref):
        c = pl.program_id(1)

        @pl.when(c == 0)
        def _():
            carry_ref[...] = jnp.zeros_like(carry_ref)
            acc_ref[...] = jnp.zeros_like(acc_ref)
            gb_ref[...] = jnp.zeros_like(gb_ref)
            gc_ref[...] = jnp.zeros_like(gc_ref)

        gy = g_ref[...].astype(BF16)
        gx_ref[0] = _dot(gy, c_ref[0])
        gx_ref[1] = -_dot(gy, c_ref[1])
        gc_ref[0] += _dot(gy, xs_ref[0].astype(BF16), TN)
        gc_ref[1] -= _dot(gy, xs_ref[1].astype(BF16), TN)
        _scan_chunk(gx_ref, gx_ref, tab_ref, carry_ref, tl // 8, True, xs_ref, acc_ref)
        gr = gx_ref[0].astype(BF16)
        gi = gx_ref[1].astype(BF16)
        gu_ref[...] = gud_ref[...] + _dot(gr, b_ref[0]) + _dot(gi, b_ref[1])
        u = u_ref[...].astype(BF16)
        gb_ref[0] += _dot(gr, u, TN)
        gb_ref[1] += _dot(gi, u, TN)

        @pl.when(c == nc - 1)
        def _():
            ga_ref[0:1, :] = jnp.sum(acc_ref[0], axis=0, keepdims=True)
            ga_ref[1:2, :] = jnp.sum(acc_ref[1], axis=0, keepdims=True)

    rev = lambda j, c: (nc - 1 - c, j)
    col = pl.BlockSpec((tl, LANES), rev)
    return pl.pallas_call(
        body, name=name, grid=(S5_NS // SCAN_W, nc),
        in_specs=[col, pl.BlockSpec((2, LANES, SCAN_W), lambda j, c: (0, j, 0)),
                  pl.BlockSpec((2, tl, SCAN_W), lambda j, c: (0, nc - 1 - c, j)), col,
                  pl.BlockSpec((2, SCAN_W, LANES), lambda j, c: (0, j, 0)), col,
                  pl.BlockSpec((None, 8, 8, SCAN_W), lambda j, c: (1, 0, 0, j))],
        out_specs=[col, pl.BlockSpec((2, SCAN_W), lambda j, c: (0, j)),
                   pl.BlockSpec((2, SCAN_W, LANES), lambda j, c: (0, j, 0)),
                   pl.BlockSpec((2, LANES, SCAN_W), lambda j, c: (0, j, 0))],
        out_shape=[_sds((L, S5_W)), _sds((2, S5_NS)), _sds((2, S5_NS, LANES)), _sds((2, S5_W, 512))],
        scratch_shapes=[pltpu.VMEM((2, tl, SCAN_W), F32), pltpu.VMEM((2, 8, SCAN_W), F32),
                        pltpu.VMEM((2, 8, SCAN_W), F32)],
        compiler_params=_cp("parallel", "arbitrary"),
    )(gyl, cset, xs, z, bset, gud, tabs)


def _s5_glu_fwd(ylin, wglu, name):
    L = ylin.shape[0]
    bl = min(1024, L)

    def body(ylin_ref, w_ref, ya_ref):
        yg = _gelu(ylin_ref[...])
        t = _dot(yg.astype(BF16), w_ref[...])
        ya_ref[...] = (yg * _sigmoid(t)).astype(BF16)

    return pl.pallas_call(
        body, name=name, grid=(L // bl,),
        in_specs=[pl.BlockSpec((bl, S5_W), lambda i: (i, 0)), pl.BlockSpec((S5_W, S5_W), lambda i: (0, 0))],
        out_specs=pl.BlockSpec((bl, S5_W), lambda i: (i, 0)),
        out_shape=_sds((L, S5_W), BF16),
        compiler_params=_cp("parallel"),
    )(ylin, wglu)


def _s5_glu_bwd(g_m, ylin, z, dvec, wglu, name):
    L = z.shape[0]
    bl = min(256, L)

    def body(g_ref, ylin_ref, u_ref, d_ref, w_ref, gyl_ref, gud_ref, gw_ref, gd_ref):
        i = pl.program_id(0)
        ylin = ylin_ref[...]
        yg = _gelu(ylin)
        ygb = yg.astype(BF16)
        sg = _sigmoid(_dot(ygb, w_ref[...]))
        gya = g_ref[...]
        gt = gya * yg * sg * (1.0 - sg)
        gtb = gt.astype(BF16)
        gyg = gya * sg + _dot(gtb, w_ref[...], NT)
        gyl = gyg * _gelu_grad(ylin)
        gyl_ref[...] = gyl
        gud_ref[...] = gyl * d_ref[...]

        @pl.when(i == 0)
        def _():
            gw_ref[...] = jnp.zeros_like(gw_ref)
            gd_ref[...] = jnp.zeros_like(gd_ref)

        gw_ref[...] += _dot(ygb, gtb, TN)
        gd_ref[...] += jnp.sum(gyl * u_ref[...], axis=0, keepdims=True)

    blk = pl.BlockSpec((bl, S5_W), lambda i: (i, 0))
    return pl.pallas_call(
        body, name=name, grid=(L // bl,),
        in_specs=[blk, blk, blk, pl.BlockSpec((1, S5_W), lambda i: (0, 0)),
                  pl.BlockSpec((S5_W, S5_W), lambda i: (0, 0))],
        out_specs=[blk, blk, pl.BlockSpec((S5_W, S5_W), lambda i: (0, 0)), pl.BlockSpec((1, S5_W), lambda i: (0, 0))],
        out_shape=[_sds((L, S5_W)), _sds((L, S5_W)), _sds((S5_W, S5_W)), _sds((1, S5_W))],
        compiler_params=_cp("arbitrary"),
    )(g_m, ylin, z, dvec, wglu)


def _s5_param_bwd(lam_c, ldt_c, b_t, gb, ga_c, gc, name):
    def body(lam_ref, ldt_ref, b_ref, gb_ref, ga_ref, gc_ref, glam_ref, gldt_ref, gbo_ref, gco_ref):
        lr, li = lam_ref[:, 0:1], lam_ref[:, 1:2]
        dt, ar, ai, qr, qi, den = _zoh_cols(lr, li, ldt_ref[...])
        bm = _b_mask()
        gbr = jnp.where(bm, gb_ref[0], 0.0)
        gbi = jnp.where(bm, gb_ref[1], 0.0)
        br, bi = b_ref[0], b_ref[1]
        obr = gbr * qr + gbi * qi
        obi = gbi * qr - gbr * qi
        gqr = jnp.sum(gbr * br + gbi * bi, axis=1, keepdims=True)
        gqi = jnp.sum(gbi * br - gbr * bi, axis=1, keepdims=True)
        for s in (64, 32, 16):
            obr = obr + pltpu.roll(obr, s, 1)
            obi = obi + pltpu.roll(obi, s, 1)
        gbo_ref[0] = obr
        gbo_ref[1] = obi
        gar = ga_ref[:, 0:1] + (gqr * lr - gqi * li) / den
        gai = ga_ref[:, 1:2] + (gqr * li + gqi * lr) / den
        qlr = (qr * lr + qi * li) / den
        qli = (qi * lr - qr * li) / den
        glr = -(gqr * qlr + gqi * qli)
        gli = -(gqi * qlr - gqr * qli)
        glr = glr + dt * (gar * ar + gai * ai)
        gli = gli + dt * (gai * ar - gar * ai)
        wr, wi = _cmul(lr, li, ar, ai)
        gldt = (gar * wr + gai * wi) * dt
        glam_ref[:, 0:1] = glr
        glam_ref[:, 1:2] = gli
        r = lax.broadcasted_iota(jnp.int32, (S5_NS, 32), 0)
        c = lax.broadcasted_iota(jnp.int32, (S5_NS, 32), 1)
        gldt_ref[...] = jnp.sum(jnp.where((r >> 6) == c, gldt, 0.0), axis=0, keepdims=True)
        cm = _c_mask()
        for k in range(2):
            oc = jnp.where(cm, gc_ref[k], 0.0)
            for s in (256, 128, 64):
                oc = oc + pltpu.roll(oc, s, 1)
            gco_ref[k] = oc[:, 0:LANES]

    vm = pl.BlockSpec(memory_space=pltpu.VMEM)
    return pl.pallas_call(
        body, name=name, in_specs=[vm] * 6, out_specs=[vm] * 4,
        out_shape=[_sds((S5_NS, 2)), _sds((1, 32)), _sds((2, S5_NS, LANES)), _sds((2, S5_W, LANES))],
        compiler_params=pltpu.CompilerParams(vmem_limit_bytes=VMEM_LIMIT),
    )(lam_c, ldt_c, b_t, gb, ga_c, gc)


FL_BLK = EVEN_PAD // LANES - 1
Q_BLK, K_BLK, V_BLK = 4, 8, 12
NEG = -1e30


def _log_sigmoid(v):
    return jnp.minimum(v, 0.0) - jnp.log(1.0 + jnp.exp(-jnp.abs(v)))


def _fox_f_fwd(z, bf, name):
    L = z.shape[0]

    def body(fl_ref, b_ref, f_ref, fq_ref):
        row = lax.broadcasted_iota(jnp.int32, (L, LANES), 0)
        cs = _cumsum_rows(_log_sigmoid(fl_ref[...] + b_ref[...]), True, row)
        f_ref[...] = cs
        expand = (lax.broadcasted_iota(jnp.int32, (LANES, FOX_W), 0)
                  == (lax.broadcasted_iota(jnp.int32, (LANES, FOX_W), 1) >> 6)).astype(F32)
        fq_ref[...] = lax.dot_general(cs, expand, NN, precision=lax.Precision.HIGHEST, preferred_element_type=F32)

    return pl.pallas_call(
        body, name=name, grid=(1,),
        in_specs=[pl.BlockSpec((L, LANES), lambda i: (0, FL_BLK)), pl.BlockSpec((1, LANES), lambda i: (0, 0))],
        out_specs=[pl.BlockSpec((L, LANES), lambda i: (0, 0)), pl.BlockSpec((L, FOX_W), lambda i: (0, 0))],
        out_shape=[_sds((L, LANES)), _sds((L, FOX_W))],
        compiler_params=_cp("arbitrary"),
    )(z, bf)


def _fox_f_bwd(dFk, dfq, z, bf, name):
    L = z.shape[0]

    def body(dfk_ref, dfq_ref, fl_ref, b_ref, dfl_ref, db_ref):
        sel = (lax.broadcasted_iota(jnp.int32, (FOX_W, LANES), 0)
               == 64 * lax.broadcasted_iota(jnp.int32, (FOX_W, LANES), 1)).astype(F32)
        dfq_h = lax.dot_general(dfq_ref[...], sel, NN, precision=lax.Precision.HIGHEST, preferred_element_type=F32)
        row = lax.broadcasted_iota(jnp.int32, (L, LANES), 0)
        cs = _cumsum_rows(dfk_ref[...] + dfq_h, False, row)
        dfl = cs * _sigmoid(-(fl_ref[...] + b_ref[...]))
        dfl_ref[...] = dfl
        db_ref[...] = jnp.sum(dfl, axis=0, keepdims=True)

    return pl.pallas_call(
        body, name=name, grid=(1,),
        in_specs=[pl.BlockSpec((L, LANES), lambda i: (0, 0)), pl.BlockSpec((L, FOX_W), lambda i: (0, 0)),
                  pl.BlockSpec((L, LANES), lambda i: (0, FL_BLK)), pl.BlockSpec((1, LANES), lambda i: (0, 0))],
        out_specs=[pl.BlockSpec((L, LANES), lambda i: (0, 0)), pl.BlockSpec((1, LANES), lambda i: (0, 0))],
        out_shape=[_sds((L, LANES)), _sds((1, LANES))],
        compiler_params=_cp("arbitrary"),
    )(dFk, dfq, z, bf)


def _head_mask(hh):
    lane = lax.broadcasted_iota(jnp.int32, (1, LANES), 1)
    return (lane >> 6) == hh


FOX_T = 512


def _fox_head(x, hh):
    return jnp.where(_head_mask(hh), x, 0.0).astype(BF16)


def _fox_scores(qh, k, fq_ref, fr_ref, hh, causal):
    s = _dot(qh, k, NT) + (fq_ref[:, 64 * hh:64 * hh + 1] - fr_ref[hh:hh + 1, :])
    return s if causal is None else jnp.where(causal, s, NEG)


def _causal(T):
    return lax.broadcasted_iota(jnp.int32, (T, T), 1) <= lax.broadcasted_iota(jnp.int32, (T, T), 0)


def _fox_fwd(z, fq, frow, name):
    L = z.shape[0]
    T = min(FOX_T, L)
    nq = L // T

    def body(qt_ref, kt_ref, q_ref, k_ref, v_ref, fq_ref, fr_ref, o_ref, lse_ref, m_ref, l_ref, acc_ref):
        t = pl.program_id(1)
        qi, ki = qt_ref[t], kt_ref[t]

        @pl.when(ki == 0)
        def _():
            m_ref[...] = jnp.full_like(m_ref, NEG)
            l_ref[...] = jnp.zeros_like(l_ref)
            acc_ref[...] = jnp.zeros_like(acc_ref)

        def step(diagonal):
            q = q_ref[...] * 0.125
            k = k_ref[...].astype(BF16)
            v = v_ref[...].astype(BF16)
            causal = _causal(T) if diagonal else None
            s = jnp.concatenate([_fox_scores(_fox_head(q, hh), k, fq_ref, fr_ref, hh, causal) for hh in range(2)],
                                axis=0)
            m_old = m_ref[...]
            m_new = jnp.maximum(m_old, jnp.max(s, axis=1, keepdims=True))
            alpha = jnp.exp(m_old - m_new)
            p = jnp.exp(s - m_new)
            l_ref[...] = alpha * l_ref[...] + jnp.sum(p, axis=1, keepdims=True)
            m_ref[...] = m_new
            acc_ref[...] = alpha * acc_ref[...] + _dot(p.astype(BF16), v)

        @pl.when(ki < qi)
        def _():
            step(False)

        @pl.when(ki == qi)
        def _():
            step(True)
            h0 = _head_mask(0)
            l = l_ref[...]
            o_h = acc_ref[...] / l
            lse_h = m_ref[...] + jnp.log(l)
            o_ref[...] = jnp.where(h0, o_h[:T], o_h[T:])
            lse_ref[...] = jnp.where(h0, lse_h[:T], lse_h[T:])

    pairs = [(qi, ki) for qi in range(nq) for ki in range(qi + 1)]
    qt = jnp.asarray([p[0] for p in pairs], jnp.int32)
    kt = jnp.asarray([p[1] for p in pairs], jnp.int32)

    def qspec(base):
        return pl.BlockSpec((T, LANES), lambda j, t, qt, kt: (qt[t], base + j))

    def kspec(base):
        return pl.BlockSpec((T, LANES), lambda j, t, qt, kt: (kt[t], base + j))

    return pl.pallas_call(
        body, name=name,
        grid_spec=pltpu.PrefetchScalarGridSpec(
            num_scalar_prefetch=2, grid=(4, len(pairs)),
            in_specs=[qspec(Q_BLK), kspec(K_BLK), kspec(V_BLK), qspec(0),
                      pl.BlockSpec((None, 2, T), lambda j, t, qt, kt: (j, 0, kt[t]))],
            out_specs=[qspec(0), qspec(0)],
            scratch_shapes=[pltpu.VMEM((2 * T, 1), F32), pltpu.VMEM((2 * T, 1), F32),
                            pltpu.VMEM((2 * T, LANES), F32)]),
        out_shape=[_sds((L, FOX_W)), _sds((L, FOX_W))],
        compiler_params=_cp("parallel", "arbitrary"),
    )(qt, kt, z, z, z, fq, frow)


def _fox_bwd(z, fq, frow, o, lse, g_m, name):
    L = z.shape[0]
    T = min(FOX_T, L)
    nq = L // T

    pairs = [(qi, ki) for ki in range(nq) for qi in range(ki, nq)]
    qt = jnp.asarray([p[0] for p in pairs], jnp.int32)
    kt = jnp.asarray([p[1] for p in pairs], jnp.int32)

    def body(qt_ref, kt_ref, q_ref, k_ref, v_ref, fq_ref, fr_ref, o_ref, lse_ref, do_ref,
             dq_ref, dk_ref, dv_ref, dfq_ref, dfk_ref, dk_acc, dv_acc, df_acc):
        t = pl.program_id(1)
        qi, ki = qt_ref[t], kt_ref[t]

        @pl.when(t == 0)
        def _():
            dq_ref[...] = jnp.zeros_like(dq_ref)
            dfq_ref[...] = jnp.zeros_like(dfq_ref)

        @pl.when(qi == ki)
        def _():
            dk_acc[...] = jnp.zeros_like(dk_acc)
            dv_acc[...] = jnp.zeros_like(dv_acc)
            df_acc[...] = jnp.zeros_like(df_acc)

        def step(diagonal):
            q = q_ref[...] * 0.125
            qb = q.astype(BF16)
            k = k_ref[...].astype(BF16)
            v = v_ref[...].astype(BF16)
            do = do_ref[...]
            dob = do.astype(BF16)
            do_o = dob.astype(F32) * o_ref[...]
            causal = _causal(T) if diagonal else None
            dvs, dks, dqs, rss = [], [], [], []
            for hh in range(2):
                s = _fox_scores(_fox_head(q, hh), k, fq_ref, fr_ref, hh, causal)
                p = jnp.exp(s - lse_ref[:, 64 * hh:64 * hh + 1])
                dp = _dot(_fox_head(do, hh), v, NT)
                delta = jnp.sum(jnp.where(_head_mask(hh), do_o, 0.0), axis=1, keepdims=True)
                ds = p * (dp - delta)
                dsb = ds.astype(BF16)
                dvs.append(_dot(p.astype(BF16), dob, TN))
                dks.append(_dot(dsb, qb, TN))
                dqs.append(_dot(dsb, k))
                rss.append(jnp.sum(ds, axis=1, keepdims=True))
                df_acc[hh:hh + 1, :] -= jnp.sum(ds, axis=0, keepdims=True)
            h0 = _head_mask(0)
            dv_acc[...] += jnp.where(h0, dvs[0], dvs[1])
            dk_acc[...] += jnp.where(h0, dks[0], dks[1])
            rows = pl.ds(pl.multiple_of(qi * T, T), T)
            dq_ref[rows, :] += jnp.where(h0, dqs[0], dqs[1])
            dfq_ref[rows, :] += jnp.where(h0, rss[0], rss[1])

        @pl.when(qi > ki)
        def _():
            step(False)

        @pl.when(qi == ki)
        def _():
            step(True)

        @pl.when(qi == nq - 1)
        def _():
            dk_ref[...] = dk_acc[...]
            dv_ref[...] = dv_acc[...]
            dfk_ref[...] = df_acc[...]

        @pl.when(t == len(pairs) - 1)
        def _():
            dq_ref[...] = dq_ref[...] * 0.125

    def qside(base):
        return pl.BlockSpec((T, LANES), lambda j, t, qt, kt: (qt[t], base + j))

    def kside(base):
        return pl.BlockSpec((T, LANES), lambda j, t, qt, kt: (kt[t], base + j))

    pair = pl.BlockSpec((L, LANES), lambda j, t, qt, kt: (0, j))
    frow_spec = pl.BlockSpec((None, 2, T), lambda j, t, qt, kt: (j, 0, kt[t]))
    return pl.pallas_call(
        body, name=name,
        grid_spec=pltpu.PrefetchScalarGridSpec(
            num_scalar_prefetch=2, grid=(4, len(pairs)),
            in_specs=[qside(Q_BLK), kside(K_BLK), kside(V_BLK), qside(0), frow_spec, qside(0), qside(0), qside(4)],
            out_specs=[pair, kside(0), kside(0), pair, frow_spec],
            scratch_shapes=[pltpu.VMEM((T, LANES), F32), pltpu.VMEM((T, LANES), F32), pltpu.VMEM((2, T), F32)]),
        out_shape=[_sds((L, FOX_W)), _sds((L, FOX_W)), _sds((L, FOX_W)), _sds((L, FOX_W)), _sds((4, 2, L))],
        compiler_params=_cp("parallel", "arbitrary"),
    )(qt, kt, z, z, z, fq, frow, o, lse, g_m)


def _shift_rows(v, s, down, row):
    n = v.shape[0]
    if down:
        return jnp.where(row >= s, pltpu.roll(v, s, 0), 0.0)
    return jnp.where(row < n - s, pltpu.roll(v, n - s, 0), 0.0)


def _cumsum_rows(v, down, row):
    s = 1
    while s < v.shape[0]:
        v = v + _shift_rows(v, s, down, row)
        s *= 2
    return v


def _window_sum(v, g, down, row):
    out = jnp.zeros_like(v)
    s = v
    for k in range(4):
        s = s + _shift_rows(s, 1 << k, down, row)
        out = jnp.where(g == k, s, out)
    return out


def _pool_inv_cnt(g, row):
    w = jnp.left_shift(2, g).astype(F32)
    return 1.0 / jnp.minimum(row.astype(F32) + 1.0, w)


def _pool_fwd(z, pool_w, scale, name):
    L = z.shape[0]

    def body(x_ref, w_ref, s_ref, y_ref, p_ref):
        g = pl.program_id(0)
        row = lax.broadcasted_iota(jnp.int32, (L, LANES), 0)
        x = x_ref[...]
        pooled = (_window_sum(x, g, True, row) * _pool_inv_cnt(g, row) - x).astype(BF16)
        p_ref[...] = pooled
        y_ref[...] = (_dot(pooled, w_ref[...].astype(BF16)) * s_ref[...]).astype(BF16)

    col = pl.BlockSpec((L, LANES), lambda g: (0, g))
    return pl.pallas_call(
        body, name=name, grid=(4,),
        in_specs=[col, pl.BlockSpec((None, LANES, LANES), lambda g: (g, 0, 0)), pl.BlockSpec((1, LANES), lambda g: (0, g))],
        out_specs=[col, col],
        out_shape=[_sds((L, 512), BF16), _sds((L, 512), BF16)],
        compiler_params=_cp("parallel"),
    )(z, pool_w, scale)


def _pool_bwd(g_m, pooled, pool_w, scale, name):
    L = g_m.shape[0]

    def body(g_ref, p_ref, w_ref, s_ref, gx_ref, gw_ref, gs_ref):
        g = pl.program_id(0)
        row = lax.broadcasted_iota(jnp.int32, (L, LANES), 0)
        gy = g_ref[...]
        pooled = p_ref[...]
        wb = w_ref[...].astype(BF16)
        lin = _dot(pooled, wb)
        gs_ref[...] = jnp.sum(gy * lin, axis=0, keepdims=True)
        glin = (gy * s_ref[...]).astype(BF16)
        gw_ref[...] = _dot(pooled, glin, TN)
        gp = _dot(glin, wb, NT)
        gx_ref[...] = _window_sum(gp * _pool_inv_cnt(g, row), g, False, row) - gp

    col = pl.BlockSpec((L, LANES), lambda g: (0, g))
    wspec = pl.BlockSpec((None, LANES, LANES), lambda g: (g, 0, 0))
    vec = pl.BlockSpec((1, LANES), lambda g: (0, g))
    return pl.pallas_call(
        body, name=name, grid=(4,),
        in_specs=[col, col, wspec, vec],
        out_specs=[col, wspec, vec],
        out_shape=[_sds((L, 512)), _sds((4, LANES, LANES)), _sds((1, 512))],
        compiler_params=_cp("parallel"),
    )(g_m, pooled, pool_w, scale)


SGU_CHUNKS = 4


def _sgu_ln(v, gam, bet):
    gv = _gelu(v)
    mu = jnp.mean(gv, axis=-1, keepdims=True)
    xc = gv - mu
    rs = lax.rsqrt(jnp.mean(xc * xc, axis=-1, keepdims=True) + EPS)
    xh = xc * rs
    return xh, rs, xh * gam + bet


def _tril_ws(w_ref, g):
    r = lax.broadcasted_iota(jnp.int32, (LANES, LANES), 0)
    c = lax.broadcasted_iota(jnp.int32, (LANES, LANES), 1)
    return jnp.where(r >= c, w_ref[g], 0.0).astype(BF16)


def _sgu_fwd(z, ln_g, ln_b, w_s, b_st, name):
    L = z.shape[0]
    rb = min(SGU_CHUNKS * LANES, L)

    def body(u_ref, v_ref, g_ref, b_ref, w_ref, bs_ref, y_ref):
        _, _, vln = _sgu_ln(v_ref[...], g_ref[...], b_ref[...])
        gu = _gelu(u_ref[...])
        vb = vln.astype(BF16)
        for g in range(4):
            ws = _tril_ws(w_ref, g)
            for n in range(rb // LANES):
                rows = slice(n * LANES, (n + 1) * LANES)
                cols = slice(g * LANES, (g + 1) * LANES)
                mixed = _dot(ws, vb[rows, cols]) + bs_ref[:, g:g + 1]
                y_ref[rows, cols] = (gu[rows, cols] * mixed).astype(BF16)

    vm = lambda shape: pl.BlockSpec(shape, lambda i: tuple(0 for _ in shape))
    return pl.pallas_call(
        body, name=name, grid=(L // rb,),
        in_specs=[pl.BlockSpec((rb, 512), lambda i: (i, 1)), pl.BlockSpec((rb, 512), lambda i: (i, 2)),
                  vm((1, 512)), vm((1, 512)), vm((4, LANES, LANES)), vm((LANES, 4))],
        out_specs=pl.BlockSpec((rb, 512), lambda i: (i, 0)),
        out_shape=_sds((L, 512), BF16),
        compiler_params=_cp("parallel"),
    )(z, z, ln_g, ln_b, w_s, b_st)


def _sgu_bwd(g_m, z, ln_g, ln_b, w_s, b_st, name):
    L = z.shape[0]
    rb = min(SGU_CHUNKS * LANES, L)

    def body(gy_ref, u_ref, v_ref, g_ref, b_ref, w_ref, bs_ref, gu_ref, gv_ref, gw_ref, gbs_ref, gg_ref, gb_ref):
        i = pl.program_id(0)

        @pl.when(i == 0)
        def _():
            gw_ref[...] = jnp.zeros_like(gw_ref)
            gbs_ref[...] = jnp.zeros_like(gbs_ref)
            gg_ref[...] = jnp.zeros_like(gg_ref)
            gb_ref[...] = jnp.zeros_like(gb_ref)

        v = v_ref[...]
        u = u_ref[...]
        gy = gy_ref[...]
        xh, rs, vln = _sgu_ln(v, g_ref[...], b_ref[...])
        gel_u = _gelu(u)
        gmix = gy * gel_u
        vb = vln.astype(BF16)
        gmb = gmix.astype(BF16)
        r = lax.broadcasted_iota(jnp.int32, (LANES, LANES), 0)
        c = lax.broadcasted_iota(jnp.int32, (LANES, LANES), 1)
        gvln_cols = []
        for g in range(4):
            ws = _tril_ws(w_ref, g)
            cols = slice(g * LANES, (g + 1) * LANES)
            gw = jnp.zeros((LANES, LANES), F32)
            gbs = jnp.zeros((LANES, 1), F32)
            parts = []
            for n in range(rb // LANES):
                rows = slice(n * LANES, (n + 1) * LANES)
                mixed = _dot(ws, vb[rows, cols]) + bs_ref[:, g:g + 1]
                gu_ref[rows, cols] = gy[rows, cols] * mixed * _gelu_grad(u[rows, cols])
                parts.append(_dot(ws, gmb[rows, cols], TN))
                gw = gw + _dot(gmb[rows, cols], vb[rows, cols], NT)
                gbs = gbs + jnp.sum(gmix[rows, cols], axis=1, keepdims=True)
            gvln_cols.append(jnp.concatenate(parts, axis=0))
            gw_ref[g] += jnp.where(r >= c, gw, 0.0)
            gbs_ref[:, g:g + 1] += gbs
        gvln = jnp.concatenate(gvln_cols, axis=1)
        gg_ref[...] += jnp.sum(gvln * xh, axis=0, keepdims=True)
        gb_ref[...] += jnp.sum(gvln, axis=0, keepdims=True)
        gxh = gvln * g_ref[...]
        ggv = rs * (gxh - jnp.mean(gxh, axis=-1, keepdims=True) - xh * jnp.mean(gxh * xh, axis=-1, keepdims=True))
        gv_ref[...] = ggv * _gelu_grad(v)

    vm = lambda shape: pl.BlockSpec(shape, lambda i: tuple(0 for _ in shape))
    blk = pl.BlockSpec((rb, 512), lambda i: (i, 0))
    return pl.pallas_call(
        body, name=name, grid=(L // rb,),
        in_specs=[pl.BlockSpec((rb, 512), lambda i: (i, 1)), pl.BlockSpec((rb, 512), lambda i: (i, 1)),
                  pl.BlockSpec((rb, 512), lambda i: (i, 2)),
                  vm((1, 512)), vm((1, 512)), vm((4, LANES, LANES)), vm((LANES, 4))],
        out_specs=[blk, blk, vm((4, LANES, LANES)), vm((LANES, 4)), vm((1, 512)), vm((1, 512))],
        out_shape=[_sds((L, 512)), _sds((L, 512)), _sds((4, LANES, LANES)), _sds((LANES, 4)),
                   _sds((1, 512)), _sds((1, 512))],
        compiler_params=_cp("arbitrary"),
    )(g_m, z, z, ln_g, ln_b, w_s, b_st)


def _adamw_math(w, g, m, v):
    nm = ADAM_B1 * m + (1.0 - ADAM_B1) * g
    nv = ADAM_B2 * v + (1.0 - ADAM_B2) * (g * g)
    m_hat = nm / (1.0 - ADAM_B1 ** ADAM_STEP)
    v_hat = nv / (1.0 - ADAM_B2 ** ADAM_STEP)
    delta = -ADAM_LR * (m_hat / (jnp.sqrt(v_hat) + ADAM_EPS) + ADAM_WD * w)
    return delta, nm, nv


def _sum_adamw(parts, w, m, v, name, layer=0, prev=None):
    n_layers, R, C = w.shape
    rb = 128 if R % 128 == 0 else R

    def body(p_ref, w_ref, m_ref, v_ref, *rest):
        g_ref, d_ref, nm_ref, nv_ref = rest[-4:]
        g = p_ref[0].astype(F32)
        for s in range(1, N_DEV):
            g = g + p_ref[s].astype(F32)
        d, nm, nv = _adamw_math(w_ref[...], g, m_ref[...], v_ref[...])
        g_ref[...] = g
        d_ref[...] = d
        nm_ref[...] = nm
        nv_ref[...] = nv

    blk = pl.BlockSpec((None, rb, C), lambda i: (layer, i, 0))
    prev = [] if prev is None else list(prev)
    return pl.pallas_call(
        body, name=name, grid=(R // rb,),
        in_specs=[pl.BlockSpec((N_DEV, rb, C), lambda i: (0, i, 0)), blk, blk, blk] + [ANY] * len(prev),
        out_specs=[blk] * 4, out_shape=[_sds((n_layers, R, C))] * 4,
        input_output_aliases={4 + k: k for k in range(len(prev))},
        compiler_params=_cp("parallel"),
    )(parts, w, m, v, *prev)


def _sum_pieces(parts, name):
    _, R, C = parts.shape

    def body(p_ref, g_ref):
        g = p_ref[0].astype(F32)
        for s in range(1, N_DEV):
            g = g + p_ref[s].astype(F32)
        g_ref[...] = g

    vm = pl.BlockSpec(memory_space=pltpu.VMEM)
    return pl.pallas_call(body, name=name, in_specs=[vm], out_specs=vm, out_shape=_sds((R, C)),
                          compiler_params=pltpu.CompilerParams(vmem_limit_bytes=VMEM_LIMIT))(parts)


def _adamw_many(ws, gs, ms, vs, name):
    n = len(ws)
    vm = pl.BlockSpec(memory_space=pltpu.VMEM)

    def body(*refs):
        w_refs, g_refs, m_refs, v_refs = refs[:n], refs[n:2 * n], refs[2 * n:3 * n], refs[3 * n:4 * n]
        d_refs, nm_refs, nv_refs = refs[4 * n:5 * n], refs[5 * n:6 * n], refs[6 * n:7 * n]
        for i in range(n):
            d, nm, nv = _adamw_math(w_refs[i][...], g_refs[i][...], m_refs[i][...], v_refs[i][...])
            d_refs[i][...] = d
            nm_refs[i][...] = nm
            nv_refs[i][...] = nv

    shapes = [_sds(w.shape) for w in ws]
    outs = pl.pallas_call(
        body, name=name, in_specs=[vm] * (4 * n), out_specs=[vm] * (3 * n), out_shape=shapes * 3,
        compiler_params=pltpu.CompilerParams(vmem_limit_bytes=VMEM_LIMIT),
    )(*ws, *gs, *ms, *vs)
    return list(outs[:n]), list(outs[n:2 * n]), list(outs[2 * n:])


def _mesh_pos():
    return lax.axis_index("x"), lax.axis_index("y"), lax.axis_index("c")


def _dev_index(p):
    return 4 * p[0] + 2 * p[1] + p[2]


HBM = pl.BlockSpec(memory_space=pltpu.HBM)
SEM = pl.BlockSpec(memory_space=pltpu.SEMAPHORE)
EFFECT = pltpu.SideEffectType.DATAFLOW_SIDE_EFFECTING


def _peer_list():
    x, y, c = _mesh_pos()
    peers = [(x ^ dx, y ^ dy, c ^ dc) for dx in range(2) for dy in range(2) for dc in range(2)][1:]
    return (x, y, c), peers


def _split_copy(src_ref, land_ref, send_sems, recv_sems, i, k, peer, slot, exchange):
    return pltpu.make_async_remote_copy(
        src_ref=src_ref.at[_dev_index(peer)] if exchange else src_ref, dst_ref=land_ref.at[slot],
        send_sem=send_sems.at[7 * i + k], recv_sem=recv_sems.at[7 * i + k], device_id=peer, device_id_type=MESH)


def _comm_start(groups, name, exchange, dep=None):
    sizes = [len(g) for g in groups]
    n = sum(sizes)
    srcs = [a for g in groups for a in g]
    my_index = _dev_index(_mesh_pos())
    lands = []
    for a in srcs:
        if exchange:
            own = lax.dynamic_slice(a, (my_index, 0, 0), (1,) + a.shape[1:])
            shape = a.shape
        else:
            own = a[None]
            shape = (N_DEV,) + a.shape
        lands.append(lax.dynamic_update_slice(lax.empty(shape, a.dtype), own, (my_index, 0, 0)))

    n_dep = 0 if dep is None else 1

    def body(*refs):
        src_refs, land_refs = refs[:n], refs[n:2 * n]
        sem_refs = refs[2 * n + n_dep:2 * n + n_dep + 2 * len(sizes)]
        token_ref = refs[-1]
        me, peers = _peer_list()
        mi = _dev_index(me)
        i = 0
        for gi, sz in enumerate(sizes):
            for j in range(sz):
                for k, peer in enumerate(peers):
                    _split_copy(src_refs[i], land_refs[i], sem_refs[2 * gi], sem_refs[2 * gi + 1], j, k, peer, mi,
                                exchange).start()
                i += 1
        token_ref[...] = jnp.zeros_like(token_ref)

    sem_shapes = []
    for sz in sizes:
        sem_shapes += [pltpu.SemaphoreType.DMA((7 * sz,)), pltpu.SemaphoreType.DMA((7 * sz,))]
    thru = [pltpu.HBM(a.shape, a.dtype) for a in srcs + lands]
    n_sem = len(sem_shapes)
    outs = pl.pallas_call(
        body, name=name,
        out_shape=tuple(sem_shapes + thru + [_sds((8, LANES))]),
        in_specs=[HBM] * (2 * n) + [ANY] * n_dep,
        out_specs=tuple([SEM] * n_sem + [HBM] * (2 * n) + [pl.BlockSpec(memory_space=pltpu.VMEM)]),
        input_output_aliases={i: n_sem + i for i in range(2 * n)},
        compiler_params=pltpu.CompilerParams(has_side_effects=EFFECT),
    )(*[pltpu.with_memory_space_constraint(a, pltpu.HBM) for a in srcs + lands], *([] if dep is None else [dep]))
    sems, thru_src, thru_land, token = outs[:n_sem], outs[n_sem:n_sem + n], outs[n_sem + n:n_sem + 2 * n], outs[-1]
    result, off = [], 0
    for gi, sz in enumerate(sizes):
        result.append((sems[2 * gi], sems[2 * gi + 1], list(thru_src[off:off + sz]), list(thru_land[off:off + sz])))
        off += sz
    return result, token


def _comm_wait(group, after, name, exchange):
    send_sems, recv_sems, srcs, lands = group
    n = len(srcs)
    after = list(after) if isinstance(after, (list, tuple)) else [after]

    def body(*refs):
        src_refs, land_refs = refs[:n], refs[n:2 * n]
        ssem, rsem = refs[2 * n], refs[2 * n + 1]
        me, peers = _peer_list()
        for i in range(n):
            for k, peer in enumerate(peers):
                cp = _split_copy(src_refs[i], land_refs[i], ssem, rsem, i, k, peer, _dev_index(peer), exchange)
                cp.wait_send()
                cp.wait_recv()

    outs = pl.pallas_call(
        body, name=name,
        out_shape=tuple(pltpu.HBM(a.shape, a.dtype) for a in srcs + lands),
        in_specs=[HBM] * (2 * n) + [SEM, SEM] + [ANY] * len(after),
        out_specs=tuple([HBM] * (2 * n)),
        input_output_aliases={i: i for i in range(2 * n)},
        compiler_params=pltpu.CompilerParams(has_side_effects=EFFECT),
    )(*srcs, *lands, send_sems, recv_sems, *after)
    return list(outs[n:])


def _tie(a, token):
    return a + token[0, 0].astype(a.dtype)


def _pack(arrs, rows):
    flat = jnp.concatenate([a.reshape(-1).astype(F32) for a in arrs])
    return jnp.pad(flat, (0, rows * LANES - flat.shape[0])).reshape(rows, LANES)


def _unpack(packed, shapes):
    flat = packed.reshape(-1)
    out, off = [], 0
    for s in shapes:
        n = math.prod(s)
        out.append(flat[off:off + n].reshape(s))
        off += n
    return out


def _packed_rows(shapes):
    n = sum(math.prod(s) for s in shapes)
    unit = N_DEV * 8 * LANES
    return -(-n // unit) * unit // LANES


def kernel(x, mix_pre_g, mix_post_g, mlp_pre_g, mlp_post_g, w_in_even, s5_lam_re, s5_lam_im, s5_log_dt, s5_b_re, s5_b_im, s5_c_re, s5_c_im, s5_d, s5_w_glu, fox_b_f, w_out_even, w_in_odd, pool_w, pool_scale, sgu_ln_g, sgu_ln_b, sgu_w_s, sgu_b_s, w_out_odd, mlp_w1, mlp_w2, loss_target, m_mix_pre_g, m_mix_post_g, m_mlp_pre_g, m_mlp_post_g, m_w_in_even, m_s5_lam_re, m_s5_lam_im, m_s5_log_dt, m_s5_b_re, m_s5_b_im, m_s5_c_re, m_s5_c_im, m_s5_d, m_s5_w_glu, m_fox_b_f, m_w_out_even, m_w_in_odd, m_pool_w, m_pool_scale, m_sgu_ln_g, m_sgu_ln_b, m_sgu_w_s, m_sgu_b_s, m_w_out_odd, m_mlp_w1, m_mlp_w2, v_mix_pre_g, v_mix_post_g, v_mlp_pre_g, v_mlp_post_g, v_w_in_even, v_s5_lam_re, v_s5_lam_im, v_s5_log_dt, v_s5_b_re, v_s5_b_im, v_s5_c_re, v_s5_c_im, v_s5_d, v_s5_w_glu, v_fox_b_f, v_w_out_even, v_w_in_odd, v_pool_w, v_pool_scale, v_sgu_ln_g, v_sgu_ln_b, v_sgu_w_s, v_sgu_b_s, v_w_out_odd, v_mlp_w1, v_mlp_w2):
    weights = dict(mix_pre_g=mix_pre_g, mix_post_g=mix_post_g, mlp_pre_g=mlp_pre_g, mlp_post_g=mlp_post_g, w_in_even=w_in_even, s5_lam_re=s5_lam_re, s5_lam_im=s5_lam_im, s5_log_dt=s5_log_dt, s5_b_re=s5_b_re, s5_b_im=s5_b_im, s5_c_re=s5_c_re, s5_c_im=s5_c_im, s5_d=s5_d, s5_w_glu=s5_w_glu, fox_b_f=fox_b_f, w_out_even=w_out_even, w_in_odd=w_in_odd, pool_w=pool_w, pool_scale=pool_scale, sgu_ln_g=sgu_ln_g, sgu_ln_b=sgu_ln_b, sgu_w_s=sgu_w_s, sgu_b_s=sgu_b_s, w_out_odd=w_out_odd, mlp_w1=mlp_w1, mlp_w2=mlp_w2)
    mom_m = dict(mix_pre_g=m_mix_pre_g, mix_post_g=m_mix_post_g, mlp_pre_g=m_mlp_pre_g, mlp_post_g=m_mlp_post_g, w_in_even=m_w_in_even, s5_lam_re=m_s5_lam_re, s5_lam_im=m_s5_lam_im, s5_log_dt=m_s5_log_dt, s5_b_re=m_s5_b_re, s5_b_im=m_s5_b_im, s5_c_re=m_s5_c_re, s5_c_im=m_s5_c_im, s5_d=m_s5_d, s5_w_glu=m_s5_w_glu, fox_b_f=m_fox_b_f, w_out_even=m_w_out_even, w_in_odd=m_w_in_odd, pool_w=m_pool_w, pool_scale=m_pool_scale, sgu_ln_g=m_sgu_ln_g, sgu_ln_b=m_sgu_ln_b, sgu_w_s=m_sgu_w_s, sgu_b_s=m_sgu_b_s, w_out_odd=m_w_out_odd, mlp_w1=m_mlp_w1, mlp_w2=m_mlp_w2)
    mom_v = dict(mix_pre_g=v_mix_pre_g, mix_post_g=v_mix_post_g, mlp_pre_g=v_mlp_pre_g, mlp_post_g=v_mlp_post_g, w_in_even=v_w_in_even, s5_lam_re=v_s5_lam_re, s5_lam_im=v_s5_lam_im, s5_log_dt=v_s5_log_dt, s5_b_re=v_s5_b_re, s5_b_im=v_s5_b_im, s5_c_re=v_s5_c_re, s5_c_im=v_s5_c_im, s5_d=v_s5_d, s5_w_glu=v_s5_w_glu, fox_b_f=v_fox_b_f, w_out_even=v_w_out_even, w_in_odd=v_w_in_odd, pool_w=v_pool_w, pool_scale=v_pool_scale, sgu_ln_g=v_sgu_ln_g, sgu_ln_b=v_sgu_ln_b, sgu_w_s=v_sgu_w_s, sgu_b_s=v_sgu_b_s, w_out_odd=v_w_out_odd, mlp_w1=v_mlp_w1, mlp_w2=v_mlp_w2)
    names = list(weights)
    L = x.shape[1]
    x0 = x[0]
    target = loss_target[0]
    my_index = 4 * lax.axis_index("x") + 2 * lax.axis_index("y") + lax.axis_index("c")

    small_vec = jnp.zeros((8, LANES), F32)
    small_vec = small_vec.at[0, :64].set(pool_scale[0]).at[1, :64].set(sgu_ln_g[0]).at[2, :64].set(sgu_ln_b[0])
    ag_groups, ag_token = _comm_start(
        [[jnp.transpose(w_in_even[0]).astype(BF16), small_vec],
         [s5_w_glu[0].astype(BF16), w_out_even[0].astype(BF16)],
         [mlp_w1[0].astype(BF16), mlp_w2[0].astype(BF16)],
         [jnp.transpose(w_in_odd[0]).astype(BF16), w_out_odd[0].astype(BF16), mlp_w1[1].astype(BF16), mlp_w2[1].astype(BF16)]],
        "ag_start", exchange=False)

    lam_r = jnp.concatenate([s5_lam_re.reshape(1, S5_NS), s5_lam_im.reshape(1, S5_NS)], axis=0)
    ldt_r = jnp.repeat(s5_log_dt.reshape(32), 64).reshape(1, S5_NS)
    lam_c = jnp.transpose(lam_r)
    ldt_c = jnp.transpose(ldt_r)
    b_t = jnp.stack([jnp.tile(s5_b_re.reshape(S5_NS, 16), (1, 8)), jnp.tile(s5_b_im.reshape(S5_NS, 16), (1, 8))])
    c_t = jnp.stack([jnp.tile(s5_c_re.reshape(S5_W, 64), (1, 8)), jnp.tile(s5_c_im.reshape(S5_W, 64), (1, 8))])
    bf_pad = jnp.pad(fox_b_f, ((0, 0), (0, LANES - 8)))
    b_st = jnp.transpose(sgu_b_s[0])

    h0, rx0 = _rms_fwd(x0, _tie(mix_pre_g[0:1], ag_token), "rms0")
    tabs, bset, cset = _s5_prep(lam_r, ldt_r, lam_c, ldt_c, b_t, c_t, "s5_prep")
    ag0 = _comm_wait(ag_groups[0], tabs, "ag_wait0", exchange=False)
    winT_e = jnp.pad(ag0[0].reshape(EVEN_IN, D_MODEL), ((0, EVEN_PAD - EVEN_IN), (0, 0)))
    pool_scale_f = ag0[1][:, 0, :64].reshape(1, 512)
    ln_g_f = ag0[1][:, 1, :64].reshape(1, 512)
    ln_b_f = ag0[1][:, 2, :64].reshape(1, 512)
    z0 = _mm(h0, winT_e, name="win_even", tb=True, bm=512, bn=EVEN_PAD)
    xs, ylin = _s5_scan_fwd(z0, bset, cset, s5_d, tabs, "s5_scan")
    ag1 = _comm_wait(ag_groups[1], ylin, "ag_wait1", exchange=False)
    wglu = ag1[0].reshape(S5_W, S5_W)
    wout_e = ag1[1].reshape(D_MODEL, D_MODEL)
    ya = _s5_glu_fwd(ylin, wglu, "s5_glu")
    fcum, fq = _fox_f_fwd(z0, bf_pad, "fox_f")
    frow = jnp.transpose(fcum[:, :8]).reshape(4, 2, L)
    o_att, lse = _fox_fwd(z0, fq, frow, "fox_fwd")
    mix0 = [ya, o_att]
    x1, ry0, h1, rx1, y0 = _mm(mix0, wout_e, name="wout_even", epi=_epi_post_pre, extra=(x0,),
                               vecs=(mix_post_g[0:1], mlp_pre_g[0:1]), out_dtypes=POST_PRE_DTYPES,
                               out_kinds=POST_PRE_KINDS, bm=FUSED_ROWS)
    ag2 = _comm_wait(ag_groups[2], rx1, "ag_wait2", exchange=False)
    w1 = [ag2[0], None]
    w2 = [ag2[1].reshape(4 * D_MODEL, D_MODEL), None]
    p0, a0 = _mm(h1, w1[0], name="mlp0_w1", b3=True, out_dtypes=(BF16, BF16), epi=_epi_relu2, bm=512, bn=4 * D_MODEL)
    x2, ro0, h2, rx2, o0 = _mm(a0, w2[0], name="mlp0_w2", epi=_epi_post_pre, extra=(x1,),
                               vecs=(mlp_post_g[0:1], mix_pre_g[1:2]), out_dtypes=POST_PRE_DTYPES,
                               out_kinds=POST_PRE_KINDS, bm=FUSED_ROWS, bk=4 * D_MODEL)
    ag3 = _comm_wait(ag_groups[3], rx2, "ag_wait3", exchange=False)
    winT_o = ag3[0].reshape(ODD_IN, D_MODEL)
    wout_o = ag3[1].reshape(D_MODEL, D_MODEL)
    w1[1] = ag3[2]
    w2[1] = ag3[3].reshape(4 * D_MODEL, D_MODEL)
    z1 = _mm(h2, winT_o, name="win_odd", tb=True, bn=ODD_IN)
    yc, pooled = _pool_fwd(z1, pool_w[0], pool_scale_f, "pool_fwd")
    yd = _sgu_fwd(z1, ln_g_f, ln_b_f, sgu_w_s[0], b_st, "sgu_fwd")
    mix1 = [yc, yd]
    x3, ry1, h3, rx3, y1 = _mm(mix1, wout_o, name="wout_odd", epi=_epi_post_pre, extra=(x2,),
                               vecs=(mix_post_g[1:2], mlp_pre_g[1:2]), out_dtypes=POST_PRE_DTYPES,
                               out_kinds=POST_PRE_KINDS, bm=FUSED_ROWS)
    p1, a1 = _mm(h3, w1[1], name="mlp1_w1", b3=True, out_dtypes=(BF16, BF16), epi=_epi_relu2, bm=512, bn=4 * D_MODEL)
    gx4, g_o1, gg_mlp_post1, sq_lanes = _mm(
        a1, w2[1], name="mlp1_w2", epi=_epi_post_loss, extra=(x3, target), vecs=(mlp_post_g[1:2],),
        out_dtypes=(F32, BF16, F32, F32), out_kinds=("full", "full", "vsum", "vsum"), bm=FUSED_ROWS, bk=4 * D_MODEL)
    sq = sq_lanes[:, 0:1]

    g_p1 = _mm(g_o1, w2[1], name="b_mlp1_a", tb=True, out_dtypes=(BF16,), epi=_epi_relu2_bwd, extra=(p1,),
               bm=512, bn=4 * D_MODEL)
    gw2_1 = _mm(a1, g_o1, name="b_mlp1_w2", ta=True, bm=512, bk=L)
    gw1_1 = _mm(h3, g_p1, name="b_mlp1_w1", ta=True, out3=True, bn=512, bk=L)
    (ex1,), tok1 = _comm_start([[gw1_1, gw2_1.reshape(N_DEV, 512, D_MODEL)]], "ex_start1", exchange=True)
    g_x3, gg_mlp_pre1, g_y1, gg_mix_post1 = _mm(
        g_p1, w1[1], name="b_mlp1_h", tb=True, b3=True, epi=_epi_pre_post_bwd, extra=(x3, gx4, y1), cols=(rx3, ry1),
        vecs=(_tie(mlp_pre_g[1:2], tok1), mix_post_g[1:2]), out_dtypes=PRE_POST_BWD_DTYPES,
        out_kinds=PRE_POST_BWD_KINDS, bm=FUSED_ROWS, bk=4 * D_MODEL)
    g_mix1 = _mm(g_y1, wout_o, name="b_wout_odd_m", tb=True)
    gwout_o = _mm(mix1, g_y1, name="b_wout_odd_w", ta=True)
    g_xc, g_pool_w, g_pool_scale = _pool_bwd(g_mix1, pooled, pool_w[0], pool_scale_f, "pool_bwd")
    g_u1, g_v1, g_ws, g_bst, g_ln_g, g_ln_b = _sgu_bwd(g_mix1, z1, ln_g_f, ln_b_f, sgu_w_s[0], b_st, "sgu_bwd")
    g_z1 = [g_xc, g_u1, g_v1]
    gwinT_o = _mm(g_z1, h2, name="b_win_odd_w", ta=True)
    (ex2,), tok2 = _comm_start([[gwout_o.reshape(N_DEV, 128, D_MODEL), gwinT_o.reshape(N_DEV, ODD_IN // N_DEV, D_MODEL)]], "ex_start2", exchange=True)
    g_x2, gg_mix_pre1, g_o0, gg_mlp_post0 = _mm(
        g_z1, winT_o, name="b_win_odd_h", epi=_epi_pre_post_bwd, extra=(x2, g_x3, o0), cols=(rx2, ro0),
        vecs=(_tie(mix_pre_g[1:2], tok2), mlp_post_g[0:1]), out_dtypes=PRE_POST_BWD_DTYPES,
        out_kinds=PRE_POST_BWD_KINDS, bm=FUSED_ROWS)
    g_p0 = _mm(g_o0, w2[0], name="b_mlp0_a", tb=True, out_dtypes=(BF16,), epi=_epi_relu2_bwd, extra=(p0,),
               bm=512, bn=4 * D_MODEL)
    gw2_0 = _mm(a0, g_o0, name="b_mlp0_w2", ta=True, bm=512, bk=L)
    gw1_0 = _mm(h1, g_p0, name="b_mlp0_w1", ta=True, out3=True, bn=512, bk=L)
    (ex3,), tok3 = _comm_start([[gw1_0, gw2_0.reshape(N_DEV, 512, D_MODEL)]], "ex_start3", exchange=True)
    g_x1, gg_mlp_pre0, g_y0, gg_mix_post0 = _mm(
        g_p0, w1[0], name="b_mlp0_h", tb=True, b3=True, epi=_epi_pre_post_bwd, extra=(x1, g_x2, y0), cols=(rx1, ry0),
        vecs=(_tie(mlp_pre_g[0:1], tok3), mix_post_g[0:1]), out_dtypes=PRE_POST_BWD_DTYPES,
        out_kinds=PRE_POST_BWD_KINDS, bm=FUSED_ROWS, bk=4 * D_MODEL)
    g_mix0 = _mm(g_y0, wout_e, name="b_wout_even_m", tb=True)
    gwout_e = _mm(mix0, g_y0, name="b_wout_even_w", ta=True)
    gyl, gud, g_wglu, g_d = _s5_glu_bwd(g_mix0, ylin, z0, s5_d, wglu, "s5_glu_bwd")
    (ex4,), tok4 = _comm_start([[gwout_e.reshape(N_DEV, 128, D_MODEL), g_wglu.reshape(N_DEV, 64, S5_W)]], "ex_start4", exchange=True)
    g_u0, ga, gb_raw, gc_raw = _s5_scan_bwd(gyl, _tie(cset, tok4), xs, z0, bset, gud, tabs, "s5_scan_bwd")
    g_lam, g_ldt, g_b, g_c = _s5_param_bwd(lam_c, ldt_c, b_t, gb_raw, jnp.transpose(ga), gc_raw, "s5_param_bwd")
    dq, dk, dv, dfq, dfrow = _fox_bwd(z0, fq, frow, o_att, lse, g_mix0, "fox_bwd")
    dFk = jnp.pad(jnp.transpose(dfrow.reshape(8, L)), ((0, 0), (0, LANES - 8)))
    dfl, db_f = _fox_f_bwd(dFk, dfq, z0, bf_pad, "fox_f_bwd")
    g_z0 = [g_u0, dq, dk, dv, dfl]
    grad_x, gg_mix_pre0 = _mm(g_z0, winT_e, name="b_win_even_h", epi=_epi_pre_bwd, extra=(x0, g_x1), cols=(rx0,),
                              vecs=(mix_pre_g[0:1],), out_dtypes=(F32, F32), out_kinds=("full", "vsum"),
                              bm=FUSED_ROWS)

    small_grads = dict(
        mix_pre_g=jnp.concatenate([gg_mix_pre0, gg_mix_pre1]), mix_post_g=jnp.concatenate([gg_mix_post0, gg_mix_post1]),
        mlp_pre_g=jnp.concatenate([gg_mlp_pre0, gg_mlp_pre1]), mlp_post_g=jnp.concatenate([gg_mlp_post0, gg_mlp_post1]),
        s5_lam_re=g_lam[:, 0], s5_lam_im=g_lam[:, 1], s5_log_dt=g_ldt,
        s5_b_re=g_b[0, :, :16], s5_b_im=g_b[1, :, :16], s5_c_re=g_c[0, :, :64], s5_c_im=g_c[1, :, :64],
        s5_d=g_d, fox_b_f=db_f[:, :8], pool_w=g_pool_w, sgu_w_s=g_ws, sgu_b_s=jnp.transpose(g_bst),
        pool_scale=g_pool_scale, sgu_ln_g=g_ln_g, sgu_ln_b=g_ln_b)
    small_names = list(small_grads)
    full_shapes = [(512,) if nm in ("pool_scale", "sgu_ln_g", "sgu_ln_b") else weights[nm].shape for nm in small_names]
    full_shapes.append((1, 1))
    rows = _packed_rows(full_shapes)
    packed = _pack([small_grads[nm] for nm in small_names] + [sq], rows).reshape(N_DEV, rows // N_DEV, LANES)
    (exs,), tok_s = _comm_start([[packed]], "exs_start", exchange=True)
    gwinT_e = _mm(g_z0, h0, name="b_win_even_w", ta=True, bk=512, out_dtypes=(BF16,), dep=tok_s)
    (recv_small,) = _comm_wait(exs, gwinT_e, "exs_wait", exchange=True)
    piece = _sum_pieces(recv_small, "sum_small")
    (ags,), tok_a = _comm_start([[piece]], "ags_start", exchange=False)

    gwinT_e_pieces = gwinT_e[:EVEN_IN].reshape(N_DEV, EVEN_IN // N_DEV, D_MODEL)
    (ex5,), tok5 = _comm_start([[gwinT_e_pieces]], "ex_start5", exchange=True, dep=tok_a)
    r_w1_1, r_w2_1 = _comm_wait(ex1, tok5, "ex_wait1", exchange=True)
    r_wout_o, r_win_o = _comm_wait(ex2, tok5, "ex_wait2", exchange=True)
    r_w1_0, r_w2_0 = _comm_wait(ex3, tok5, "ex_wait3", exchange=True)
    r_wout_e, r_wglu = _comm_wait(ex4, tok5, "ex_wait4", exchange=True)

    res = {}
    for nm, parts in (("mlp_w1", (r_w1_0, r_w1_1)), ("mlp_w2", (r_w2_0, r_w2_1))):
        first = _sum_adamw(parts[0], weights[nm], mom_m[nm], mom_v[nm], "adamw_%s_0" % nm, layer=0)
        res[nm] = tuple(_sum_adamw(parts[1], weights[nm], mom_m[nm], mom_v[nm], "adamw_%s_1" % nm, layer=1, prev=first))
    big_parts = dict(s5_w_glu=r_wglu, w_out_even=r_wout_e, w_out_odd=r_wout_o)
    for nm, parts in big_parts.items():
        res[nm] = tuple(_sum_adamw(parts, weights[nm], mom_m[nm], mom_v[nm], "adamw_" + nm))
    done = [res[nm][1] for nm in ("mlp_w1", "mlp_w2", "s5_w_glu", "w_out_even", "w_out_odd")]

    (small_all,) = _comm_wait(ags, done, "ags_wait", exchange=False)
    small_full = _unpack(small_all.reshape(rows, LANES), full_shapes)
    loss = 0.5 * small_full.pop()[0, 0] / D_MODEL
    small_g = []
    for nm, g in zip(small_names, small_full):
        if nm in ("pool_scale", "sgu_ln_g", "sgu_ln_b"):
            g = lax.dynamic_slice(g, (my_index * 64,), (64,)).reshape(1, 64)
        small_g.append(g)
    sd, sm, sv = _adamw_many([weights[nm] for nm in small_names], small_g, [mom_m[nm] for nm in small_names],
                             [mom_v[nm] for nm in small_names], "adamw_small")
    for nm, g_, d_, m_, v_ in zip(small_names, small_g, sd, sm, sv):
        res[nm] = (g_, d_, m_, v_)
    done.append(sd[0])

    for nm, parts in (("w_in_odd", r_win_o), ("w_in_even", None)):
        if parts is None:
            (parts,) = _comm_wait(ex5, done, "ex_wait5", exchange=True)
        outs = _sum_adamw(parts, jnp.transpose(weights[nm], (0, 2, 1)), jnp.transpose(mom_m[nm], (0, 2, 1)),
                          jnp.transpose(mom_v[nm], (0, 2, 1)), "adamw_" + nm)
        res[nm] = tuple(jnp.transpose(o, (0, 2, 1)) for o in outs)
        done.append(res[nm][1])

    grads = [res[nm][0].reshape(weights[nm].shape) for nm in names]
    deltas = [res[nm][1].reshape(weights[nm].shape) for nm in names]
    new_m = [res[nm][2].reshape(weights[nm].shape) for nm in names]
    new_v = [res[nm][3].reshape(weights[nm].shape) for nm in names]
    return (loss, grad_x[None], *grads, *deltas, *new_m, *new_v)
```

```python
import math

import jax
import jax.numpy as jnp
from jax import lax
from jax.experimental import pallas as pl
from jax.experimental.pallas import tpu as pltpu

F32 = jnp.float32
BF16 = jnp.bfloat16
MESH = pl.DeviceIdType.MESH
ANY = pl.BlockSpec(memory_space=pl.ANY)

N_DEV = 8
D_MODEL = 1024
EPS = 1e-6
NORM_ROWS = 512
FUSED_ROWS = 512
S5_W = 512
S5_NS = 2048
SCAN_GROUPS = 4
SCAN_CHUNK = 1024
FOX_W = 512
EVEN_IN = 2056
EVEN_PAD = 2176
ODD_IN = 1536
LANES = 128
PIECE = 4 * D_MODEL // N_DEV
VMEM_LIMIT = 56 * 1024 * 1024

ADAM_LR = 0.001
ADAM_B1 = 0.9
ADAM_B2 = 0.999
ADAM_EPS = 1e-08
ADAM_WD = 0.01
ADAM_STEP = 10

NT = (((1,), (1,)), ((), ()))
TN = (((0,), (0,)), ((), ()))
NN = (((1,), (0,)), ((), ()))


def _cp(*sem):
    return pltpu.CompilerParams(dimension_semantics=sem, vmem_limit_bytes=VMEM_LIMIT)


def _sds(shape, dtype=F32):
    return jax.ShapeDtypeStruct(tuple(shape), dtype)


def _gelu(x):
    t = jnp.tanh(0.7978845608028654 * (x + 0.044715 * x * x * x))
    return 0.5 * x * (1.0 + t)


def _gelu_grad(x):
    t = jnp.tanh(0.7978845608028654 * (x + 0.044715 * x * x * x))
    du = 0.7978845608028654 * (1.0 + 3.0 * 0.044715 * x * x)
    return 0.5 * (1.0 + t) + 0.5 * x * (1.0 - t * t) * du


def _sigmoid(x):
    return 1.0 / (1.0 + jnp.exp(-x))


def _dot(a, b, dn=NN):
    return lax.dot_general(a, b, dn, preferred_element_type=F32)


def _mm(a, b, *, name, ta=False, tb=False, b3=False, out3=False, out_dtypes=(F32,), epi=None, extra=(),
        cols=(), vecs=(), out_kinds=None, bm=1024, bn=1024, bk=1024, dep=None):
    a_list = list(a) if isinstance(a, (list, tuple)) else [a]
    widths = [p.shape[1] for p in a_list]
    offs = [sum(widths[:i]) for i in range(len(widths))]
    na = len(a_list)
    M = sum(widths) if ta else a_list[0].shape[0]
    K = a_list[0].shape[0] if ta else sum(widths)
    if na > 1:
        assert not b3 and not tb
        bm, bk = (M, bk) if ta else (bm, K)
    pw = b.shape[2] if b3 else PIECE
    if b3:
        N = b.shape[1] if tb else b.shape[0] * pw
        assert (b.shape[0] * pw if tb else b.shape[1]) == K
    else:
        N = b.shape[0] if tb else b.shape[1]
    bm, bn, bk = min(bm, M), min(bn, N), min(bk, K)
    assert M % bm == 0 and N % bn == 0 and K % bk == 0, (name, M, N, K, bm, bn, bk)
    assert not (b3 or out3) or ((bk if tb else bn) % pw == 0 and bn % PIECE == 0)
    nk = K // bk
    n_extra = len(extra) + len(cols) + len(vecs)
    n_out = len(out_dtypes)
    out_kinds = tuple(out_kinds) if out_kinds is not None else ("full",) * n_out
    dn = (((0 if ta else 1,), (1 if tb else 0,)), ((), ()))

    use_acc = nk > 1

    def body(*refs):
        a_refs, b_ref = refs[:na], refs[na]
        a_ref = a_refs[0]
        e_refs = refs[na + 1:na + 1 + n_extra]
        first_out = na + 1 + n_extra + (0 if dep is None else 1)
        o_refs = refs[first_out:first_out + n_out]
        acc_ref = refs[-1] if use_acc else o_refs[0]
        i, k = pl.program_id(0), pl.program_id(2)

        def dot(a_v, b_v):
            return lax.dot_general(a_v.astype(BF16), b_v.astype(BF16), dn, preferred_element_type=F32)

        everything = slice(None)
        if na > 1 and ta:
            terms = [(pl.ds(off, w), everything, r, b_ref) for r, off, w in zip(a_refs, offs, widths)]
        elif na > 1:
            terms = [(everything, everything, r, b_ref.at[pl.ds(off, w), :]) for r, off, w in zip(a_refs, offs, widths)]
        elif not b3:
            terms = [(everything, everything, a_ref, b_ref)]
        elif tb:
            terms = [(everything, everything,
                      a_ref.at[pl.ds(t * pw, pw), :] if ta else a_ref.at[:, pl.ds(t * pw, pw)], b_ref.at[t])
                     for t in range(bk // pw)]
        else:
            terms = [(everything, pl.ds(t * pw, pw), a_ref, b_ref.at[t]) for t in range(bn // pw)]

        def finish(acc):
            outs = (acc,) if epi is None else epi(acc, *[e[...] for e in e_refs])
            for o_ref, o, kind in zip(o_refs, outs, out_kinds):
                if kind == "vsum":
                    @pl.when(i == 0)
                    def _(o_ref=o_ref, o=o):
                        o_ref[...] = o

                    @pl.when(i > 0)
                    def _(o_ref=o_ref, o=o):
                        o_ref[...] += o
                elif out3:
                    for t in range(bn // PIECE):
                        o_ref[t] = o[:, t * PIECE:(t + 1) * PIECE].astype(o_ref.dtype)
                else:
                    o_ref[...] = o.astype(o_ref.dtype)

        if nk == 1:
            bands = {}
            for rows, cols, a_r, b_r in terms:
                key = (getattr(rows, "start", None), getattr(cols, "start", None))
                val = dot(a_r[...], b_r[...])
                bands[key] = val if key not in bands else bands[key] + val
            vals = list(bands.values())
            if len(vals) == 1:
                finish(vals[0])
            else:
                finish(jnp.concatenate(vals, axis=0 if (na > 1 and ta) else 1))
            return

        @pl.when(k == 0)
        def _():
            acc_ref[...] = jnp.zeros_like(acc_ref)

        for rows, cols, a_r, b_r in terms:
            acc_ref[rows, cols] += dot(a_r[...], b_r[...])

        @pl.when(k == nk - 1)
        def _():
            finish(acc_ref[...])

    if na > 1:
        a_specs = [pl.BlockSpec((bk, w), lambda i, j, k: (k, 0)) if ta else pl.BlockSpec((bm, w), lambda i, j, k: (i, 0))
                   for w in widths]
    else:
        a_specs = [pl.BlockSpec((bk, bm), lambda i, j, k: (k, i)) if ta else
                   pl.BlockSpec((bm, bk), lambda i, j, k: (i, k))]
    if b3:
        if tb:
            b_spec = pl.BlockSpec((bk // pw, bn, pw), lambda i, j, k: (k, j, 0))
        else:
            b_spec = pl.BlockSpec((bn // pw, bk, pw), lambda i, j, k: (j, k, 0))
    else:
        b_spec = pl.BlockSpec((bn, bk), lambda i, j, k: (j, k)) if tb else pl.BlockSpec((bk, bn), lambda i, j, k: (k, j))
    e_specs = ([pl.BlockSpec((bm, bn), lambda i, j, k: (i, j)) for _ in extra]
               + [pl.BlockSpec((bm, 1), lambda i, j, k: (i, 0)) for _ in cols]
               + [pl.BlockSpec((1, bn), lambda i, j, k: (0, j)) for _ in vecs])
    if out3:
        o_specs = [pl.BlockSpec((bn // PIECE, bm, PIECE), lambda i, j, k: (j, i, 0)) for _ in out_dtypes]
        o_shapes = [_sds((N // PIECE, M, PIECE), dt) for dt in out_dtypes]
    else:
        spec_of = {"full": pl.BlockSpec((bm, bn), lambda i, j, k: (i, j)),
                   "col": pl.BlockSpec((bm, 1), lambda i, j, k: (i, 0)),
                   "vsum": pl.BlockSpec((1, bn), lambda i, j, k: (0, j))}
        shape_of = {"full": (M, N), "col": (M, 1), "vsum": (1, N)}
        o_specs = [spec_of[kind] for kind in out_kinds]
        o_shapes = [_sds(shape_of[kind], dt) for kind, dt in zip(out_kinds, out_dtypes)]
    assert "col" not in out_kinds or bn == N
    outs = pl.pallas_call(
        body, name=name, grid=(M // bm, N // bn, nk),
        in_specs=a_specs + [b_spec] + e_specs + ([] if dep is None else [ANY]),
        out_specs=o_specs, out_shape=o_shapes,
        scratch_shapes=[pltpu.VMEM((bm, bn), F32)] if use_acc else [],
        compiler_params=_cp("arbitrary" if "vsum" in out_kinds else "parallel", "parallel", "arbitrary"),
    )(*a_list, b, *extra, *cols, *vecs, *([] if dep is None else [dep]))
    return outs[0] if n_out == 1 else outs


def _epi_relu2(acc):
    r = jnp.maximum(acc, 0.0)
    return acc, r * r


def _epi_relu2_bwd(acc, p):
    return (acc * (2.0 * jnp.maximum(p.astype(F32), 0.0)),)


def _row_spec(rb, w=D_MODEL):
    return pl.BlockSpec((rb, w), lambda i: (i, 0))


def _vec_spec(w=D_MODEL):
    return pl.BlockSpec((1, w), lambda i: (0, 0))


def _rstd(v):
    return lax.rsqrt(jnp.mean(v * v, axis=-1, keepdims=True) + EPS)


def _rms_fwd(x, g, name):
    L = x.shape[0]
    rb = min(NORM_ROWS, L)

    def body(x_ref, g_ref, h_ref, r_ref):
        xv = x_ref[...]
        r = _rstd(xv)
        h_ref[...] = (xv * r * g_ref[...]).astype(BF16)
        r_ref[...] = r

    return pl.pallas_call(
        body, name=name, grid=(L // rb,),
        in_specs=[_row_spec(rb), _vec_spec()],
        out_specs=[_row_spec(rb), _row_spec(rb, 1)],
        out_shape=[_sds((L, D_MODEL), BF16), _sds((L, 1))],
        compiler_params=_cp("parallel"),
    )(x, g)


def _rms_bwd_rows(dy, xv, r, g):
    n = xv * r
    dyg = dy * g
    return r * (dyg - n * jnp.mean(dyg * n, axis=-1, keepdims=True)), n


POST_PRE_DTYPES = (F32, F32, BF16, F32, F32)
POST_PRE_KINDS = ("full", "col", "full", "col", "full")
PRE_POST_BWD_DTYPES = (F32, F32, BF16, F32)
PRE_POST_BWD_KINDS = ("full", "vsum", "full", "vsum")


def _epi_post_pre(y, x_in, g_post, g_pre):
    ry = _rstd(y)
    xo = x_in + y * ry * g_post
    rx = _rstd(xo)
    return xo, ry, xo * rx * g_pre, rx, y


def _epi_pre_post_bwd(gh, x, g_out, y_prev, rx, ry_prev, g_pre, g_post_prev):
    gx, n = _rms_bwd_rows(gh, x, rx, g_pre)
    gi = g_out + gx
    gy, ny = _rms_bwd_rows(gi, y_prev, ry_prev, g_post_prev)
    return gi, jnp.sum(gh * n, axis=0, keepdims=True), gy, jnp.sum(gi * ny, axis=0, keepdims=True)


def _epi_pre_bwd(gh, x, g_out, rx, g_pre):
    gx, n = _rms_bwd_rows(gh, x, rx, g_pre)
    return g_out + gx, jnp.sum(gh * n, axis=0, keepdims=True)


def _epi_post_loss(y, x_in, target, g_post):
    ry = _rstd(y)
    diff = x_in + y * ry * g_post - target
    gx = diff * (1.0 / D_MODEL)
    gy, n = _rms_bwd_rows(gx, y, ry, g_post)
    sq = jnp.broadcast_to(jnp.sum(diff * diff, keepdims=True), (1, y.shape[1]))
    return gx, gy, jnp.sum(gx * n, axis=0, keepdims=True), sq


def _cmul(ar, ai, br, bi):
    return ar * br - ai * bi, ar * bi + ai * br


def _zoh_cols(lr, li, ldt):
    dt = jnp.exp(ldt)
    mag = jnp.exp(lr * dt)
    ar = mag * jnp.cos(li * dt)
    ai = mag * jnp.sin(li * dt)
    den = lr * lr + li * li
    nr = ar - 1.0
    qr = (nr * lr + ai * li) / den
    qi = (ai * lr - nr * li) / den
    return dt, ar, ai, qr, qi, den


def _b_mask():
    r = lax.broadcasted_iota(jnp.int32, (S5_NS, LANES), 0)
    c = lax.broadcasted_iota(jnp.int32, (S5_NS, LANES), 1)
    return ((r >> 6) & 7) == (c >> 4)


def _c_mask():
    r = lax.broadcasted_iota(jnp.int32, (S5_W, 512), 0)
    c = lax.broadcasted_iota(jnp.int32, (S5_W, 512), 1)
    return ((r >> 4) & 7) == (c >> 6)


def _s5_prep(lam_r, ldt_r, lam_c, ldt_c, b_t, c_t, name):
    def body(lam_r_ref, ldt_r_ref, lam_c_ref, ldt_c_ref, b_ref, c_ref, tab_ref, bset_ref, cset_ref):
        lr, li = lam_r_ref[0:1, :], lam_r_ref[1:2, :]
        dt = jnp.exp(ldt_r_ref[...])
        mag = jnp.exp(lr * dt)
        p1r, p1i = mag * jnp.cos(li * dt), mag * jnp.sin(li * dt)
        p2r, p2i = _cmul(p1r, p1i, p1r, p1i)
        p3r, p3i = _cmul(p2r, p2i, p1r, p1i)
        p4r, p4i = _cmul(p2r, p2i, p2r, p2i)
        p5r, p5i = _cmul(p4r, p4i, p1r, p1i)
        p6r, p6i = _cmul(p4r, p4i, p2r, p2i)
        p7r, p7i = _cmul(p4r, p4i, p3r, p3i)
        p8r, p8i = _cmul(p4r, p4i, p4r, p4i)
        pw_r = [p1r, p2r, p3r, p4r, p5r, p6r, p7r, p8r]
        pw_i = [p1i, p2i, p3i, p4i, p5i, p6i, p7i, p8i]
        row = lax.broadcasted_iota(jnp.int32, (8, S5_NS), 0)
        zero = jnp.zeros((8, S5_NS), F32)

        def bc(v):
            return jnp.broadcast_to(v, (8, S5_NS))

        for d in range(2):
            sgn = 1.0 if d == 0 else -1.0
            for t, s in enumerate((1, 2, 4)):
                live = (row >= s) if d == 0 else (row <= 7 - s)
                tab_ref[d, 2 * t] = jnp.where(live, bc(pw_r[s - 1]), zero)
                tab_ref[d, 2 * t + 1] = jnp.where(live, bc(sgn * pw_i[s - 1]), zero)
            cr, ci = zero, zero
            for i in range(8):
                e = i if d == 0 else 7 - i
                cr = jnp.where(row == i, bc(pw_r[e]), cr)
                ci = jnp.where(row == i, bc(sgn * pw_i[e]), ci)
            tab_ref[d, 6] = cr
            tab_ref[d, 7] = ci

        _, _, _, qr, qi, _ = _zoh_cols(lam_c_ref[:, 0:1], lam_c_ref[:, 1:2], ldt_c_ref[...])
        bm = _b_mask()
        br, bi = b_ref[0], b_ref[1]
        bset_ref[0] = jnp.where(bm, qr * br - qi * bi, 0.0).astype(BF16)
        bset_ref[1] = jnp.where(bm, qr * bi + qi * br, 0.0).astype(BF16)
        cm = _c_mask()
        cset_ref[0] = jnp.where(cm, c_ref[0], 0.0).astype(BF16)
        cset_ref[1] = jnp.where(cm, c_ref[1], 0.0).astype(BF16)

    vm = pl.BlockSpec(memory_space=pltpu.VMEM)
    return pl.pallas_call(
        body, name=name, in_specs=[vm] * 6, out_specs=[vm] * 3,
        out_shape=[_sds((2, 8, 8, S5_NS)), _sds((2, S5_NS, LANES), BF16), _sds((2, S5_W, 512), BF16)],
        compiler_params=pltpu.CompilerParams(vmem_limit_bytes=VMEM_LIMIT),
    )(lam_r, ldt_r, lam_c, ldt_c, b_t, c_t)


SCAN_W = SCAN_GROUPS * LANES


def _scan_chunk(src_ref, dst_ref, tab_ref, carry_ref, nb, reverse, xs_ref=None, acc_ref=None):
    row = lax.broadcasted_iota(jnp.int32, (8, LANES), 0)

    def step(i, carry):
        b = (nb - 1 - i) if reverse else i
        off = pl.multiple_of(b * 8, 8)
        out = []
        for g in range(SCAN_GROUPS):
            lanes = pl.ds(g * LANES, LANES)
            cr, ci = carry[2 * g], carry[2 * g + 1]
            yr = src_ref[0, pl.ds(off, 8), lanes]
            yi = src_ref[1, pl.ds(off, 8), lanes]
            for t, s in enumerate((1, 2, 4)):
                sh = (8 - s) if reverse else s
                sr = pltpu.roll(yr, sh, 0)
                si = pltpu.roll(yi, sh, 0)
                mr, mi = tab_ref[2 * t, :, lanes], tab_ref[2 * t + 1, :, lanes]
                yr, yi = yr + mr * sr - mi * si, yi + mr * si + mi * sr
            pr, pi = tab_ref[6, :, lanes], tab_ref[7, :, lanes]
            yr, yi = yr + pr * cr - pi * ci, yi + pr * ci + pi * cr
            dst_ref[0, pl.ds(off, 8), lanes] = yr
            dst_ref[1, pl.ds(off, 8), lanes] = yi
            if xs_ref is not None:
                nr = jnp.where(row == 7, cr, pltpu.roll(yr, 7, 0))
                ni = jnp.where(row == 7, ci, pltpu.roll(yi, 7, 0))
                xr = xs_ref[0, pl.ds(off, 8), lanes]
                xi = xs_ref[1, pl.ds(off, 8), lanes]
                acc_ref[0, :, lanes] += xr * nr + xi * ni
                acc_ref[1, :, lanes] += xr * ni - xi * nr
            last = 0 if reverse else 7
            out += [jnp.broadcast_to(yr[last:last + 1, :], (8, LANES)),
                    jnp.broadcast_to(yi[last:last + 1, :], (8, LANES))]
        return tuple(out)

    init = []
    for g in range(SCAN_GROUPS):
        init += [carry_ref[0, :, pl.ds(g * LANES, LANES)], carry_ref[1, :, pl.ds(g * LANES, LANES)]]
    fin = lax.fori_loop(0, nb, step, tuple(init))
    for g in range(SCAN_GROUPS):
        carry_ref[0, :, pl.ds(g * LANES, LANES)] = fin[2 * g]
        carry_ref[1, :, pl.ds(g * LANES, LANES)] = fin[2 * g + 1]


def _s5_scan_fwd(z, bset, cset, dvec, tabs, name):
    L = z.shape[0]
    tl = min(SCAN_CHUNK, L)
    nc = L // tl

    def body(u_ref, b_ref, c_ref, d_ref, tab_ref, x_ref, y_ref, carry_ref):
        @pl.when(pl.program_id(1) == 0)
        def _():
            carry_ref[...] = jnp.zeros_like(carry_ref)

        uf = u_ref[...]
        u = uf.astype(BF16)
        x_ref[0] = _dot(u, b_ref[0], NT)
        x_ref[1] = _dot(u, b_ref[1], NT)
        _scan_chunk(x_ref, x_ref, tab_ref, carry_ref, tl // 8, False)
        y_ref[...] = (_dot(x_ref[0].astype(BF16), c_ref[0], NT) - _dot(x_ref[1].astype(BF16), c_ref[1], NT)
                      + d_ref[...] * uf)

    col = pl.BlockSpec((tl, LANES), lambda j, c: (c, j))
    return pl.pallas_call(
        body, name=name, grid=(S5_NS // SCAN_W, nc),
        in_specs=[col, pl.BlockSpec((2, SCAN_W, LANES), lambda j, c: (0, j, 0)),
                  pl.BlockSpec((2, LANES, SCAN_W), lambda j, c: (0, j, 0)),
                  pl.BlockSpec((1, LANES), lambda j, c: (0, j)),
                  pl.BlockSpec((None, 8, 8, SCAN_W), lambda j, c: (0, 0, 0, j))],
        out_specs=[pl.BlockSpec((2, tl, SCAN_W), lambda j, c: (0, c, j)), col],
        out_shape=[_sds((2, L, S5_NS)), _sds((L, S5_W))],
        scratch_shapes=[pltpu.VMEM((2, 8, SCAN_W), F32)],
        compiler_params=_cp("parallel", "arbitrary"),
    )(z, bset, cset, dvec, tabs)


def _s5_scan_bwd(gyl, cset, xs, z, bset, gud, tabs, name):
    L = z.shape[0]
    tl = min(SCAN_CHUNK, L)
    nc = L // tl

    def body(g_ref, c_ref, xs_ref, u_ref, b_ref, gud_ref, tab_ref, gu_ref, ga_ref, gb_ref, gc_ref,
             gx_ref, carry_ref, acc_ref):
        c = pl.program_id(1)

        @pl.when(c == 0)
        def _():
            carry_ref[...] = jnp.zeros_like(carry_ref)
            acc_ref[...] = jnp.zeros_like(acc_ref)
            gb_ref[...] = jnp.zeros_like(gb_ref)
            gc_ref[...] = jnp.zeros_like(gc_ref)

        gy = g_ref[...].astype(BF16)
        gx_ref[0] = _dot(gy, c_ref[0])
        gx_ref[1] = -_dot(gy, c_ref[1])
        gc_ref[0] += _dot(gy, xs_ref[0].astype(BF16), TN)
        gc_ref[1] -= _dot(gy, xs_ref[1].astype(BF16), TN)
        _scan_chunk(gx_ref, gx_ref, tab_ref, carry_ref, tl // 8, True, xs_ref, acc_ref)
        gr = gx_ref[0].astype(BF16)
        gi = gx_ref[1].astype(BF16)
        gu_ref[...] = gud_ref[...] + _dot(gr, b_ref[0]) + _dot(gi, b_ref[1])
        u = u_ref[...].astype(BF16)
        gb_ref[0] += _dot(gr, u, TN)
        gb_ref[1] += _dot(gi, u, TN)

        @pl.when(c == nc - 1)
        def _():
            ga_ref[0:1, :] = jnp.sum(acc_ref[0], axis=0, keepdims=True)
            ga_ref[1:2, :] = jnp.sum(acc_ref[1], axis=0, keepdims=True)

    rev = lambda j, c: (nc - 1 - c, j)
    col = pl.BlockSpec((tl, LANES), rev)
    return pl.pallas_call(
        body, name=name, grid=(S5_NS // SCAN_W, nc),
        in_specs=[col, pl.BlockSpec((2, LANES, SCAN_W), lambda j, c: (0, j, 0)),
                  pl.BlockSpec((2, tl, SCAN_W), lambda j, c: (0, nc - 1 - c, j)), col,
                  pl.BlockSpec((2, SCAN_W, LANES), lambda j, c: (0, j, 0)), col,
                  pl.BlockSpec((None, 8, 8, SCAN_W), lambda j, c: (1, 0, 0, j))],
        out_specs=[col, pl.BlockSpec((2, SCAN_W), lambda j, c: (0, j)),
                   pl.BlockSpec((2, SCAN_W, LANES), lambda j, c: (0, j, 0)),
                   pl.BlockSpec((2, LANES, SCAN_W), lambda j, c: (0, j, 0))],
        out_shape=[_sds((L, S5_W)), _sds((2, S5_NS)), _sds((2, S5_NS, LANES)), _sds((2, S5_W, 512))],
        scratch_shapes=[pltpu.VMEM((2, tl, SCAN_W), F32), pltpu.VMEM((2, 8, SCAN_W), F32),
                        pltpu.VMEM((2, 8, SCAN_W), F32)],
        compiler_params=_cp("parallel", "arbitrary"),
    )(gyl, cset, xs, z, bset, gud, tabs)


def _s5_glu_fwd(ylin, wglu, name):
    L = ylin.shape[0]
    bl = min(1024, L)

    def body(ylin_ref, w_ref, ya_ref):
        yg = _gelu(ylin_ref[...])
        t = _dot(yg.astype(BF16), w_ref[...])
        ya_ref[...] = (yg * _sigmoid(t)).astype(BF16)

    return pl.pallas_call(
        body, name=name, grid=(L // bl,),
        in_specs=[pl.BlockSpec((bl, S5_W), lambda i: (i, 0)), pl.BlockSpec((S5_W, S5_W), lambda i: (0, 0))],
        out_specs=pl.BlockSpec((bl, S5_W), lambda i: (i, 0)),
        out_shape=_sds((L, S5_W), BF16),
        compiler_params=_cp("parallel"),
    )(ylin, wglu)


def _s5_glu_bwd(g_y, wout, ylin, z, dvec, wglu, name):
    L = z.shape[0]
    bl = min(256, L)

    def body(g_ref, wo_ref, ylin_ref, u_ref, d_ref, w_ref, gyl_ref, gud_ref, gw_ref, gd_ref):
        i = pl.program_id(0)
        ylin = ylin_ref[...]
        yg = _gelu(ylin)
        ygb = yg.astype(BF16)
        sg = _sigmoid(_dot(ygb, w_ref[...]))
        gya = _dot(g_ref[...], wo_ref[...], NT)
        gt = gya * yg * sg * (1.0 - sg)
        gtb = gt.astype(BF16)
        gyg = gya * sg + _dot(gtb, w_ref[...], NT)
        gyl = gyg * _gelu_grad(ylin)
        gyl_ref[...] = gyl
        gud_ref[...] = gyl * d_ref[...]

        @pl.when(i == 0)
        def _():
            gw_ref[...] = jnp.zeros_like(gw_ref)
            gd_ref[...] = jnp.zeros_like(gd_ref)

        gw_ref[...] += _dot(ygb, gtb, TN)
        gd_ref[...] += jnp.sum(gyl * u_ref[...], axis=0, keepdims=True)

    blk = pl.BlockSpec((bl, S5_W), lambda i: (i, 0))
    return pl.pallas_call(
        body, name=name, grid=(L // bl,),
        in_specs=[pl.BlockSpec((bl, D_MODEL), lambda i: (i, 0)), pl.BlockSpec((S5_W, D_MODEL), lambda i: (0, 0)),
                  blk, blk, pl.BlockSpec((1, S5_W), lambda i: (0, 0)), pl.BlockSpec((S5_W, S5_W), lambda i: (0, 0))],
        out_specs=[blk, blk, pl.BlockSpec((S5_W, S5_W), lambda i: (0, 0)), pl.BlockSpec((1, S5_W), lambda i: (0, 0))],
        out_shape=[_sds((L, S5_W)), _sds((L, S5_W)), _sds((S5_W, S5_W)), _sds((1, S5_W))],
        compiler_params=_cp("arbitrary"),
    )(g_y, wout, ylin, z, dvec, wglu)


def _s5_param_bwd(lam_c, ldt_c, b_t, gb, ga_c, gc, name):
    def body(lam_ref, ldt_ref, b_ref, gb_ref, ga_ref, gc_ref, glam_ref, gldt_ref, gbo_ref, gco_ref):
        lr, li = lam_ref[:, 0:1], lam_ref[:, 1:2]
        dt, ar, ai, qr, qi, den = _zoh_cols(lr, li, ldt_ref[...])
        bm = _b_mask()
        gbr = jnp.where(bm, gb_ref[0], 0.0)
        gbi = jnp.where(bm, gb_ref[1], 0.0)
        br, bi = b_ref[0], b_ref[1]
        obr = gbr * qr + gbi * qi
        obi = gbi * qr - gbr * qi
        gqr = jnp.sum(gbr * br + gbi * bi, axis=1, keepdims=True)
        gqi = jnp.sum(gbi * br - gbr * bi, axis=1, keepdims=True)
        for s in (64, 32, 16):
            obr = obr + pltpu.roll(obr, s, 1)
            obi = obi + pltpu.roll(obi, s, 1)
        gbo_ref[0] = obr
        gbo_ref[1] = obi
        gar = ga_ref[:, 0:1] + (gqr * lr - gqi * li) / den
        gai = ga_ref[:, 1:2] + (gqr * li + gqi * lr) / den
        qlr = (qr * lr + qi * li) / den
        qli = (qi * lr - qr * li) / den
        glr = -(gqr * qlr + gqi * qli)
        gli = -(gqi * qlr - gqr * qli)
        glr = glr + dt * (gar * ar + gai * ai)
        gli = gli + dt * (gai * ar - gar * ai)
        wr, wi = _cmul(lr, li, ar, ai)
        gldt = (gar * wr + gai * wi) * dt
        glam_ref[:, 0:1] = glr
        glam_ref[:, 1:2] = gli
        r = lax.broadcasted_iota(jnp.int32, (S5_NS, 32), 0)
        c = lax.broadcasted_iota(jnp.int32, (S5_NS, 32), 1)
        gldt_ref[...] = jnp.sum(jnp.where((r >> 6) == c, gldt, 0.0), axis=0, keepdims=True)
        cm = _c_mask()
        for k in range(2):
            oc = jnp.where(cm, gc_ref[k], 0.0)
            for s in (256, 128, 64):
                oc = oc + pltpu.roll(oc, s, 1)
            gco_ref[k] = oc[:, 0:LANES]

    vm = pl.BlockSpec(memory_space=pltpu.VMEM)
    return pl.pallas_call(
        body, name=name, in_specs=[vm] * 6, out_specs=[vm] * 4,
        out_shape=[_sds((S5_NS, 2)), _sds((1, 32)), _sds((2, S5_NS, LANES)), _sds((2, S5_W, LANES))],
        compiler_params=pltpu.CompilerParams(vmem_limit_bytes=VMEM_LIMIT),
    )(lam_c, ldt_c, b_t, gb, ga_c, gc)


FL_BLK = EVEN_PAD // LANES - 1
Q_BLK, K_BLK, V_BLK = 4, 8, 12
NEG = -1e30


def _log_sigmoid(v):
    return jnp.minimum(v, 0.0) - jnp.log(1.0 + jnp.exp(-jnp.abs(v)))


def _fox_f_fwd(z, bf, name):
    L = z.shape[0]

    def body(fl_ref, b_ref, f_ref, fq_ref):
        row = lax.broadcasted_iota(jnp.int32, (L, LANES), 0)
        cs = _cumsum_rows(_log_sigmoid(fl_ref[...] + b_ref[...]), True, row)
        f_ref[...] = cs
        expand = (lax.broadcasted_iota(jnp.int32, (LANES, FOX_W), 0)
                  == (lax.broadcasted_iota(jnp.int32, (LANES, FOX_W), 1) >> 6)).astype(F32)
        fq_ref[...] = lax.dot_general(cs, expand, NN, precision=lax.Precision.HIGHEST, preferred_element_type=F32)

    return pl.pallas_call(
        body, name=name, grid=(1,),
        in_specs=[pl.BlockSpec((L, LANES), lambda i: (0, FL_BLK)), pl.BlockSpec((1, LANES), lambda i: (0, 0))],
        out_specs=[pl.BlockSpec((L, LANES), lambda i: (0, 0)), pl.BlockSpec((L, FOX_W), lambda i: (0, 0))],
        out_shape=[_sds((L, LANES)), _sds((L, FOX_W))],
        compiler_params=_cp("arbitrary"),
    )(z, bf)


def _fox_f_bwd(dFk, dfq, z, bf, name):
    L = z.shape[0]

    def body(dfk_ref, dfq_ref, fl_ref, b_ref, dfl_ref, db_ref):
        sel = (lax.broadcasted_iota(jnp.int32, (FOX_W, LANES), 0)
               == 64 * lax.broadcasted_iota(jnp.int32, (FOX_W, LANES), 1)).astype(F32)
        dfq_h = lax.dot_general(dfq_ref[...], sel, NN, precision=lax.Precision.HIGHEST, preferred_element_type=F32)
        row = lax.broadcasted_iota(jnp.int32, (L, LANES), 0)
        cs = _cumsum_rows(dfk_ref[...] + dfq_h, False, row)
        dfl = cs * _sigmoid(-(fl_ref[...] + b_ref[...]))
        dfl_ref[...] = dfl
        db_ref[...] = jnp.sum(dfl, axis=0, keepdims=True)

    return pl.pallas_call(
        body, name=name, grid=(1,),
        in_specs=[pl.BlockSpec((L, LANES), lambda i: (0, 0)), pl.BlockSpec((L, FOX_W), lambda i: (0, 0)),
                  pl.BlockSpec((L, LANES), lambda i: (0, FL_BLK)), pl.BlockSpec((1, LANES), lambda i: (0, 0))],
        out_specs=[pl.BlockSpec((L, LANES), lambda i: (0, 0)), pl.BlockSpec((1, LANES), lambda i: (0, 0))],
        out_shape=[_sds((L, LANES)), _sds((1, LANES))],
        compiler_params=_cp("arbitrary"),
    )(dFk, dfq, z, bf)


def _head_mask(hh):
    lane = lax.broadcasted_iota(jnp.int32, (1, LANES), 1)
    return (lane >> 6) == hh


FOX_T = 512


def _fox_head(x, hh):
    return jnp.where(_head_mask(hh), x, 0.0).astype(BF16)


def _fox_scores(qh, k, fq_ref, fr_ref, hh, causal):
    s = _dot(qh, k, NT) + (fq_ref[:, 64 * hh:64 * hh + 1] - fr_ref[hh:hh + 1, :])
    return s if causal is None else jnp.where(causal, s, NEG)


def _causal(T):
    return lax.broadcasted_iota(jnp.int32, (T, T), 1) <= lax.broadcasted_iota(jnp.int32, (T, T), 0)


def _fox_fwd(z, fq, frow, name):
    L = z.shape[0]
    T = min(FOX_T, L)
    nq = L // T

    def body(qt_ref, kt_ref, q_ref, k_ref, v_ref, fq_ref, fr_ref, o_ref, lse_ref, m_ref, l_ref, acc_ref):
        t = pl.program_id(1)
        qi, ki = qt_ref[t], kt_ref[t]

        @pl.when(ki == 0)
        def _():
            m_ref[...] = jnp.full_like(m_ref, NEG)
            l_ref[...] = jnp.zeros_like(l_ref)
            acc_ref[...] = jnp.zeros_like(acc_ref)

        def step(diagonal):
            q = q_ref[...] * 0.125
            k = k_ref[...].astype(BF16)
            v = v_ref[...].astype(BF16)
            causal = _causal(T) if diagonal else None
            s = jnp.concatenate([_fox_scores(_fox_head(q, hh), k, fq_ref, fr_ref, hh, causal) for hh in range(2)],
                                axis=0)
            m_old = m_ref[...]
            m_new = jnp.maximum(m_old, jnp.max(s, axis=1, keepdims=True))
            alpha = jnp.exp(m_old - m_new)
            p = jnp.exp(s - m_new)
            l_ref[...] = alpha * l_ref[...] + jnp.sum(p, axis=1, keepdims=True)
            m_ref[...] = m_new
            acc_ref[...] = alpha * acc_ref[...] + _dot(p.astype(BF16), v)

        @pl.when(ki < qi)
        def _():
            step(False)

        @pl.when(ki == qi)
        def _():
            step(True)
            h0 = _head_mask(0)
            l = l_ref[...]
            o_h = acc_ref[...] / l
            lse_h = m_ref[...] + jnp.log(l)
            o_ref[...] = jnp.where(h0, o_h[:T], o_h[T:])
            lse_ref[...] = jnp.where(h0, lse_h[:T], lse_h[T:])

    pairs = [(qi, ki) for qi in range(nq) for ki in range(qi + 1)]
    qt = jnp.asarray([p[0] for p in pairs], jnp.int32)
    kt = jnp.asarray([p[1] for p in pairs], jnp.int32)

    def qspec(base):
        return pl.BlockSpec((T, LANES), lambda j, t, qt, kt: (qt[t], base + j))

    def kspec(base):
        return pl.BlockSpec((T, LANES), lambda j, t, qt, kt: (kt[t], base + j))

    return pl.pallas_call(
        body, name=name,
        grid_spec=pltpu.PrefetchScalarGridSpec(
            num_scalar_prefetch=2, grid=(4, len(pairs)),
            in_specs=[qspec(Q_BLK), kspec(K_BLK), kspec(V_BLK), qspec(0),
                      pl.BlockSpec((None, 2, T), lambda j, t, qt, kt: (j, 0, kt[t]))],
            out_specs=[qspec(0), qspec(0)],
            scratch_shapes=[pltpu.VMEM((2 * T, 1), F32), pltpu.VMEM((2 * T, 1), F32),
                            pltpu.VMEM((2 * T, LANES), F32)]),
        out_shape=[_sds((L, FOX_W)), _sds((L, FOX_W))],
        compiler_params=_cp("parallel", "arbitrary"),
    )(qt, kt, z, z, z, fq, frow)


def _fox_bwd(z, fq, frow, o, lse, g_m, name):
    L = z.shape[0]
    T = min(FOX_T, L)
    nq = L // T

    pairs = [(qi, ki) for ki in range(nq) for qi in range(ki, nq)]
    qt = jnp.asarray([p[0] for p in pairs], jnp.int32)
    kt = jnp.asarray([p[1] for p in pairs], jnp.int32)

    def body(qt_ref, kt_ref, q_ref, k_ref, v_ref, fq_ref, fr_ref, o_ref, lse_ref, do_ref,
             dq_ref, dk_ref, dv_ref, dfq_ref, dfk_ref, dk_acc, dv_acc, df_acc):
        t = pl.program_id(1)
        qi, ki = qt_ref[t], kt_ref[t]

        @pl.when(t == 0)
        def _():
            dq_ref[...] = jnp.zeros_like(dq_ref)
            dfq_ref[...] = jnp.zeros_like(dfq_ref)

        @pl.when(qi == ki)
        def _():
            dk_acc[...] = jnp.zeros_like(dk_acc)
            dv_acc[...] = jnp.zeros_like(dv_acc)
            df_acc[...] = jnp.zeros_like(df_acc)

        def step(diagonal):
            q = q_ref[...] * 0.125
            qb = q.astype(BF16)
            k = k_ref[...].astype(BF16)
            v = v_ref[...].astype(BF16)
            do = do_ref[...]
            dob = do.astype(BF16)
            do_o = dob.astype(F32) * o_ref[...]
            causal = _causal(T) if diagonal else None
            dvs, dks, dqs, rss = [], [], [], []
            for hh in range(2):
                s = _fox_scores(_fox_head(q, hh), k, fq_ref, fr_ref, hh, causal)
                p = jnp.exp(s - lse_ref[:, 64 * hh:64 * hh + 1])
                dp = _dot(_fox_head(do, hh), v, NT)
                delta = jnp.sum(jnp.where(_head_mask(hh), do_o, 0.0), axis=1, keepdims=True)
                ds = p * (dp - delta)
                dsb = ds.astype(BF16)
                dvs.append(_dot(p.astype(BF16), dob, TN))
                dks.append(_dot(dsb, qb, TN))
                dqs.append(_dot(dsb, k))
                rss.append(jnp.sum(ds, axis=1, keepdims=True))
                df_acc[hh:hh + 1, :] -= jnp.sum(ds, axis=0, keepdims=True)
            h0 = _head_mask(0)
            dv_acc[...] += jnp.where(h0, dvs[0], dvs[1])
            dk_acc[...] += jnp.where(h0, dks[0], dks[1])
            rows = pl.ds(pl.multiple_of(qi * T, T), T)
            dq_ref[rows, :] += jnp.where(h0, dqs[0], dqs[1])
            dfq_ref[rows, :] += jnp.where(h0, rss[0], rss[1])

        @pl.when(qi > ki)
        def _():
            step(False)

        @pl.when(qi == ki)
        def _():
            step(True)

        @pl.when(qi == nq - 1)
        def _():
            dk_ref[...] = dk_acc[...]
            dv_ref[...] = dv_acc[...]
            dfk_ref[...] = df_acc[...]

        @pl.when(t == len(pairs) - 1)
        def _():
            dq_ref[...] = dq_ref[...] * 0.125

    def qside(base):
        return pl.BlockSpec((T, LANES), lambda j, t, qt, kt: (qt[t], base + j))

    def kside(base):
        return pl.BlockSpec((T, LANES), lambda j, t, qt, kt: (kt[t], base + j))

    pair = pl.BlockSpec((L, LANES), lambda j, t, qt, kt: (0, j))
    frow_spec = pl.BlockSpec((None, 2, T), lambda j, t, qt, kt: (j, 0, kt[t]))
    return pl.pallas_call(
        body, name=name,
        grid_spec=pltpu.PrefetchScalarGridSpec(
            num_scalar_prefetch=2, grid=(4, len(pairs)),
            in_specs=[qside(Q_BLK), kside(K_BLK), kside(V_BLK), qside(0), frow_spec, qside(0), qside(0), qside(0)],
            out_specs=[pair, kside(0), kside(0), pair, frow_spec],
            scratch_shapes=[pltpu.VMEM((T, LANES), F32), pltpu.VMEM((T, LANES), F32), pltpu.VMEM((2, T), F32)]),
        out_shape=[_sds((L, FOX_W)), _sds((L, FOX_W)), _sds((L, FOX_W)), _sds((L, FOX_W)), _sds((4, 2, L))],
        compiler_params=_cp("parallel", "arbitrary"),
    )(qt, kt, z, z, z, fq, frow, o, lse, g_m)


def _shift_rows(v, s, down, row):
    n = v.shape[0]
    if down:
        return jnp.where(row >= s, pltpu.roll(v, s, 0), 0.0)
    return jnp.where(row < n - s, pltpu.roll(v, n - s, 0), 0.0)


def _cumsum_rows(v, down, row):
    s = 1
    while s < v.shape[0]:
        v = v + _shift_rows(v, s, down, row)
        s *= 2
    return v


def _window_sum(v, g, down, row):
    out = jnp.zeros_like(v)
    s = v
    for k in range(4):
        s = s + _shift_rows(s, 1 << k, down, row)
        out = jnp.where(g == k, s, out)
    return out


def _pool_inv_cnt(g, row):
    w = jnp.left_shift(2, g).astype(F32)
    return 1.0 / jnp.minimum(row.astype(F32) + 1.0, w)


def _pool_fwd(z, pool_w, scale, name):
    L = z.shape[0]

    def body(x_ref, w_ref, s_ref, y_ref, p_ref):
        g = pl.program_id(0)
        row = lax.broadcasted_iota(jnp.int32, (L, LANES), 0)
        x = x_ref[...]
        pooled = (_window_sum(x, g, True, row) * _pool_inv_cnt(g, row) - x).astype(BF16)
        p_ref[...] = pooled
        y_ref[...] = (_dot(pooled, w_ref[...].astype(BF16)) * s_ref[...]).astype(BF16)

    col = pl.BlockSpec((L, LANES), lambda g: (0, g))
    return pl.pallas_call(
        body, name=name, grid=(4,),
        in_specs=[col, pl.BlockSpec((None, LANES, LANES), lambda g: (g, 0, 0)), pl.BlockSpec((1, LANES), lambda g: (0, g))],
        out_specs=[col, col],
        out_shape=[_sds((L, 512), BF16), _sds((L, 512), BF16)],
        compiler_params=_cp("parallel"),
    )(z, pool_w, scale)


def _pool_bwd(g_y, wout, pooled, pool_w, scale, name):
    L = g_y.shape[0]

    def body(g_ref, wo_ref, p_ref, w_ref, s_ref, gx_ref, gw_ref, gs_ref):
        g = pl.program_id(0)
        row = lax.broadcasted_iota(jnp.int32, (L, LANES), 0)
        gy = _dot(g_ref[...], wo_ref[...], NT)
        pooled = p_ref[...]
        wb = w_ref[...].astype(BF16)
        lin = _dot(pooled, wb)
        gs_ref[...] = jnp.sum(gy * lin, axis=0, keepdims=True)
        glin = (gy * s_ref[...]).astype(BF16)
        gw_ref[...] = _dot(pooled, glin, TN)
        gp = _dot(glin, wb, NT)
        gx_ref[...] = _window_sum(gp * _pool_inv_cnt(g, row), g, False, row) - gp

    col = pl.BlockSpec((L, LANES), lambda g: (0, g))
    wspec = pl.BlockSpec((None, LANES, LANES), lambda g: (g, 0, 0))
    vec = pl.BlockSpec((1, LANES), lambda g: (0, g))
    return pl.pallas_call(
        body, name=name, grid=(4,),
        in_specs=[pl.BlockSpec((L, D_MODEL), lambda g: (0, 0)), pl.BlockSpec((LANES, D_MODEL), lambda g: (g, 0)),
                  col, wspec, vec],
        out_specs=[col, wspec, vec],
        out_shape=[_sds((L, 512)), _sds((4, LANES, LANES)), _sds((1, 512))],
        compiler_params=_cp("parallel"),
    )(g_y, wout, pooled, pool_w, scale)


SGU_CHUNKS = 4


def _sgu_ln(v, gam, bet):
    gv = _gelu(v)
    mu = jnp.mean(gv, axis=-1, keepdims=True)
    xc = gv - mu
    rs = lax.rsqrt(jnp.mean(xc * xc, axis=-1, keepdims=True) + EPS)
    xh = xc * rs
    return xh, rs, xh * gam + bet


def _tril_ws(w_ref, g):
    r = lax.broadcasted_iota(jnp.int32, (LANES, LANES), 0)
    c = lax.broadcasted_iota(jnp.int32, (LANES, LANES), 1)
    return jnp.where(r >= c, w_ref[g], 0.0).astype(BF16)


def _sgu_fwd(z, ln_g, ln_b, w_s, b_st, name):
    L = z.shape[0]
    rb = min(SGU_CHUNKS * LANES, L)

    def body(u_ref, v_ref, g_ref, b_ref, w_ref, bs_ref, y_ref):
        _, _, vln = _sgu_ln(v_ref[...], g_ref[...], b_ref[...])
        gu = _gelu(u_ref[...])
        vb = vln.astype(BF16)
        for g in range(4):
            ws = _tril_ws(w_ref, g)
            for n in range(rb // LANES):
                rows = slice(n * LANES, (n + 1) * LANES)
                cols = slice(g * LANES, (g + 1) * LANES)
                mixed = _dot(ws, vb[rows, cols]) + bs_ref[:, g:g + 1]
                y_ref[rows, cols] = (gu[rows, cols] * mixed).astype(BF16)

    vm = lambda shape: pl.BlockSpec(shape, lambda i: tuple(0 for _ in shape))
    return pl.pallas_call(
        body, name=name, grid=(L // rb,),
        in_specs=[pl.BlockSpec((rb, 512), lambda i: (i, 1)), pl.BlockSpec((rb, 512), lambda i: (i, 2)),
                  vm((1, 512)), vm((1, 512)), vm((4, LANES, LANES)), vm((LANES, 4))],
        out_specs=pl.BlockSpec((rb, 512), lambda i: (i, 0)),
        out_shape=_sds((L, 512), BF16),
        compiler_params=_cp("parallel"),
    )(z, z, ln_g, ln_b, w_s, b_st)


def _sgu_bwd(g_y, wout, z, ln_g, ln_b, w_s, b_st, name):
    L = z.shape[0]
    rb = min(SGU_CHUNKS * LANES, L)

    def body(gyo_ref, wo_ref, u_ref, v_ref, g_ref, b_ref, w_ref, bs_ref, gu_ref, gv_ref, gw_ref, gbs_ref, gg_ref,
             gb_ref):
        i = pl.program_id(0)

        @pl.when(i == 0)
        def _():
            gw_ref[...] = jnp.zeros_like(gw_ref)
            gbs_ref[...] = jnp.zeros_like(gbs_ref)
            gg_ref[...] = jnp.zeros_like(gg_ref)
            gb_ref[...] = jnp.zeros_like(gb_ref)

        v = v_ref[...]
        u = u_ref[...]
        gy = _dot(gyo_ref[...], wo_ref[...], NT)
        xh, rs, vln = _sgu_ln(v, g_ref[...], b_ref[...])
        gel_u = _gelu(u)
        gmix = gy * gel_u
        vb = vln.astype(BF16)
        gmb = gmix.astype(BF16)
        r = lax.broadcasted_iota(jnp.int32, (LANES, LANES), 0)
        c = lax.broadcasted_iota(jnp.int32, (LANES, LANES), 1)
        gvln_cols = []
        for g in range(4):
            ws = _tril_ws(w_ref, g)
            cols = slice(g * LANES, (g + 1) * LANES)
            gw = jnp.zeros((LANES, LANES), F32)
            gbs = jnp.zeros((LANES, 1), F32)
            parts = []
            for n in range(rb // LANES):
                rows = slice(n * LANES, (n + 1) * LANES)
                mixed = _dot(ws, vb[rows, cols]) + bs_ref[:, g:g + 1]
                gu_ref[rows, cols] = gy[rows, cols] * mixed * _gelu_grad(u[rows, cols])
                parts.append(_dot(ws, gmb[rows, cols], TN))
                gw = gw + _dot(gmb[rows, cols], vb[rows, cols], NT)
                gbs = gbs + jnp.sum(gmix[rows, cols], axis=1, keepdims=True)
            gvln_cols.append(jnp.concatenate(parts, axis=0))
            gw_ref[g] += jnp.where(r >= c, gw, 0.0)
            gbs_ref[:, g:g + 1] += gbs
        gvln = jnp.concatenate(gvln_cols, axis=1)
        gg_ref[...] += jnp.sum(gvln * xh, axis=0, keepdims=True)
        gb_ref[...] += jnp.sum(gvln, axis=0, keepdims=True)
        gxh = gvln * g_ref[...]
        ggv = rs * (gxh - jnp.mean(gxh, axis=-1, keepdims=True) - xh * jnp.mean(gxh * xh, axis=-1, keepdims=True))
        gv_ref[...] = ggv * _gelu_grad(v)

    vm = lambda shape: pl.BlockSpec(shape, lambda i: tuple(0 for _ in shape))
    blk = pl.BlockSpec((rb, 512), lambda i: (i, 0))
    return pl.pallas_call(
        body, name=name, grid=(L // rb,),
        in_specs=[pl.BlockSpec((rb, D_MODEL), lambda i: (i, 0)), pl.BlockSpec((512, D_MODEL), lambda i: (1, 0)),
                  pl.BlockSpec((rb, 512), lambda i: (i, 1)), pl.BlockSpec((rb, 512), lambda i: (i, 2)),
                  vm((1, 512)), vm((1, 512)), vm((4, LANES, LANES)), vm((LANES, 4))],
        out_specs=[blk, blk, vm((4, LANES, LANES)), vm((LANES, 4)), vm((1, 512)), vm((1, 512))],
        out_shape=[_sds((L, 512)), _sds((L, 512)), _sds((4, LANES, LANES)), _sds((LANES, 4)),
                   _sds((1, 512)), _sds((1, 512))],
        compiler_params=_cp("arbitrary"),
    )(g_y, wout, z, z, ln_g, ln_b, w_s, b_st)


def _adamw_math(w, g, m, v):
    nm = ADAM_B1 * m + (1.0 - ADAM_B1) * g
    nv = ADAM_B2 * v + (1.0 - ADAM_B2) * (g * g)
    m_hat = nm / (1.0 - ADAM_B1 ** ADAM_STEP)
    v_hat = nv / (1.0 - ADAM_B2 ** ADAM_STEP)
    delta = -ADAM_LR * (m_hat / (jnp.sqrt(v_hat) + ADAM_EPS) + ADAM_WD * w)
    return delta, nm, nv


def _sum_adamw(parts, w, m, v, name, layer=0, prev=None):
    n_layers, R, C = w.shape
    rb = 128 if R % 128 == 0 else R

    def body(p_ref, w_ref, m_ref, v_ref, *rest):
        g_ref, d_ref, nm_ref, nv_ref = rest[-4:]
        g = p_ref[0].astype(F32)
        for s in range(1, N_DEV):
            g = g + p_ref[s].astype(F32)
        d, nm, nv = _adamw_math(w_ref[...], g, m_ref[...], v_ref[...])
        g_ref[...] = g
        d_ref[...] = d
        nm_ref[...] = nm
        nv_ref[...] = nv

    blk = pl.BlockSpec((None, rb, C), lambda i: (layer, i, 0))
    prev = [] if prev is None else list(prev)
    return pl.pallas_call(
        body, name=name, grid=(R // rb,),
        in_specs=[pl.BlockSpec((N_DEV, rb, C), lambda i: (0, i, 0)), blk, blk, blk] + [ANY] * len(prev),
        out_specs=[blk] * 4, out_shape=[_sds((n_layers, R, C))] * 4,
        input_output_aliases={4 + k: k for k in range(len(prev))},
        compiler_params=_cp("parallel"),
    )(parts, w, m, v, *prev)


def _sum_pieces(parts, name):
    _, R, C = parts.shape

    def body(p_ref, g_ref):
        g = p_ref[0].astype(F32)
        for s in range(1, N_DEV):
            g = g + p_ref[s].astype(F32)
        g_ref[...] = g

    vm = pl.BlockSpec(memory_space=pltpu.VMEM)
    return pl.pallas_call(body, name=name, in_specs=[vm], out_specs=vm, out_shape=_sds((R, C)),
                          compiler_params=pltpu.CompilerParams(vmem_limit_bytes=VMEM_LIMIT))(parts)


def _adamw_many(ws, gs, ms, vs, name):
    n = len(ws)
    vm = pl.BlockSpec(memory_space=pltpu.VMEM)

    def body(*refs):
        w_refs, g_refs, m_refs, v_refs = refs[:n], refs[n:2 * n], refs[2 * n:3 * n], refs[3 * n:4 * n]
        d_refs, nm_refs, nv_refs = refs[4 * n:5 * n], refs[5 * n:6 * n], refs[6 * n:7 * n]
        for i in range(n):
            d, nm, nv = _adamw_math(w_refs[i][...], g_refs[i][...], m_refs[i][...], v_refs[i][...])
            d_refs[i][...] = d
            nm_refs[i][...] = nm
            nv_refs[i][...] = nv

    shapes = [_sds(w.shape) for w in ws]
    outs = pl.pallas_call(
        body, name=name, in_specs=[vm] * (4 * n), out_specs=[vm] * (3 * n), out_shape=shapes * 3,
        compiler_params=pltpu.CompilerParams(vmem_limit_bytes=VMEM_LIMIT),
    )(*ws, *gs, *ms, *vs)
    return list(outs[:n]), list(outs[n:2 * n]), list(outs[2 * n:])


def _mesh_pos():
    return lax.axis_index("x"), lax.axis_index("y"), lax.axis_index("c")


def _dev_index(p):
    return 4 * p[0] + 2 * p[1] + p[2]


HBM = pl.BlockSpec(memory_space=pltpu.HBM)
SEM = pl.BlockSpec(memory_space=pltpu.SEMAPHORE)
EFFECT = pltpu.SideEffectType.DATAFLOW_SIDE_EFFECTING


def _peer_list():
    x, y, c = _mesh_pos()
    peers = [(x ^ dx, y ^ dy, c ^ dc) for dx in range(2) for dy in range(2) for dc in range(2)][1:]
    return (x, y, c), peers


def _split_copy(src_ref, land_ref, send_sems, recv_sems, i, k, peer, slot, exchange):
    return pltpu.make_async_remote_copy(
        src_ref=src_ref.at[_dev_index(peer)] if exchange else src_ref, dst_ref=land_ref.at[slot],
        send_sem=send_sems.at[7 * i + k], recv_sem=recv_sems.at[7 * i + k], device_id=peer, device_id_type=MESH)


def _comm_start(groups, name, exchange, dep=None):
    sizes = [len(g) for g in groups]
    n = sum(sizes)
    srcs = [a for g in groups for a in g]
    my_index = _dev_index(_mesh_pos())
    lands = []
    for a in srcs:
        if exchange:
            own = lax.dynamic_slice(a, (my_index, 0, 0), (1,) + a.shape[1:])
            shape = a.shape
        else:
            own = a[None]
            shape = (N_DEV,) + a.shape
        lands.append(lax.dynamic_update_slice(lax.empty(shape, a.dtype), own, (my_index, 0, 0)))

    n_dep = 0 if dep is None else 1

    def body(*refs):
        src_refs, land_refs = refs[:n], refs[n:2 * n]
        sem_refs = refs[2 * n + n_dep:2 * n + n_dep + 2 * len(sizes)]
        token_ref = refs[-1]
        me, peers = _peer_list()
        mi = _dev_index(me)
        i = 0
        for gi, sz in enumerate(sizes):
            for j in range(sz):
                for k, peer in enumerate(peers):
                    _split_copy(src_refs[i], land_refs[i], sem_refs[2 * gi], sem_refs[2 * gi + 1], j, k, peer, mi,
                                exchange).start()
                i += 1
        token_ref[...] = jnp.zeros_like(token_ref)

    sem_shapes = []
    for sz in sizes:
        sem_shapes += [pltpu.SemaphoreType.DMA((7 * sz,)), pltpu.SemaphoreType.DMA((7 * sz,))]
    thru = [pltpu.HBM(a.shape, a.dtype) for a in srcs + lands]
    n_sem = len(sem_shapes)
    outs = pl.pallas_call(
        body, name=name,
        out_shape=tuple(sem_shapes + thru + [_sds((8, LANES))]),
        in_specs=[HBM] * (2 * n) + [ANY] * n_dep,
        out_specs=tuple([SEM] * n_sem + [HBM] * (2 * n) + [pl.BlockSpec(memory_space=pltpu.VMEM)]),
        input_output_aliases={i: n_sem + i for i in range(2 * n)},
        compiler_params=pltpu.CompilerParams(has_side_effects=EFFECT),
    )(*[pltpu.with_memory_space_constraint(a, pltpu.HBM) for a in srcs + lands], *([] if dep is None else [dep]))
    sems, thru_src, thru_land, token = outs[:n_sem], outs[n_sem:n_sem + n], outs[n_sem + n:n_sem + 2 * n], outs[-1]
    result, off = [], 0
    for gi, sz in enumerate(sizes):
        result.append((sems[2 * gi], sems[2 * gi + 1], list(thru_src[off:off + sz]), list(thru_land[off:off + sz])))
        off += sz
    return result, token


def _comm_wait(group, after, name, exchange):
    send_sems, recv_sems, srcs, lands = group
    n = len(srcs)
    after = list(after) if isinstance(after, (list, tuple)) else [after]

    def body(*refs):
        src_refs, land_refs = refs[:n], refs[n:2 * n]
        ssem, rsem = refs[2 * n], refs[2 * n + 1]
        me, peers = _peer_list()
        for i in range(n):
            for k, peer in enumerate(peers):
                cp = _split_copy(src_refs[i], land_refs[i], ssem, rsem, i, k, peer, _dev_index(peer), exchange)
                cp.wait_send()
                cp.wait_recv()

    outs = pl.pallas_call(
        body, name=name,
        out_shape=tuple(pltpu.HBM(a.shape, a.dtype) for a in srcs + lands),
        in_specs=[HBM] * (2 * n) + [SEM, SEM] + [ANY] * len(after),
        out_specs=tuple([HBM] * (2 * n)),
        input_output_aliases={i: i for i in range(2 * n)},
        compiler_params=pltpu.CompilerParams(has_side_effects=EFFECT),
    )(*srcs, *lands, send_sems, recv_sems, *after)
    return list(outs[n:])


def _tie(a, token):
    return a + token[0, 0].astype(a.dtype)


def _pack(arrs, rows):
    flat = jnp.concatenate([a.reshape(-1).astype(F32) for a in arrs])
    return jnp.pad(flat, (0, rows * LANES - flat.shape[0])).reshape(rows, LANES)


def _unpack(packed, shapes):
    flat = packed.reshape(-1)
    out, off = [], 0
    for s in shapes:
        n = math.prod(s)
        out.append(flat[off:off + n].reshape(s))
        off += n
    return out


def _packed_rows(shapes):
    n = sum(math.prod(s) for s in shapes)
    unit = N_DEV * 8 * LANES
    return -(-n // unit) * unit // LANES


def kernel(x, mix_pre_g, mix_post_g, mlp_pre_g, mlp_post_g, w_in_even, s5_lam_re, s5_lam_im, s5_log_dt, s5_b_re, s5_b_im, s5_c_re, s5_c_im, s5_d, s5_w_glu, fox_b_f, w_out_even, w_in_odd, pool_w, pool_scale, sgu_ln_g, sgu_ln_b, sgu_w_s, sgu_b_s, w_out_odd, mlp_w1, mlp_w2, loss_target, m_mix_pre_g, m_mix_post_g, m_mlp_pre_g, m_mlp_post_g, m_w_in_even, m_s5_lam_re, m_s5_lam_im, m_s5_log_dt, m_s5_b_re, m_s5_b_im, m_s5_c_re, m_s5_c_im, m_s5_d, m_s5_w_glu, m_fox_b_f, m_w_out_even, m_w_in_odd, m_pool_w, m_pool_scale, m_sgu_ln_g, m_sgu_ln_b, m_sgu_w_s, m_sgu_b_s, m_w_out_odd, m_mlp_w1, m_mlp_w2, v_mix_pre_g, v_mix_post_g, v_mlp_pre_g, v_mlp_post_g, v_w_in_even, v_s5_lam_re, v_s5_lam_im, v_s5_log_dt, v_s5_b_re, v_s5_b_im, v_s5_c_re, v_s5_c_im, v_s5_d, v_s5_w_glu, v_fox_b_f, v_w_out_even, v_w_in_odd, v_pool_w, v_pool_scale, v_sgu_ln_g, v_sgu_ln_b, v_sgu_w_s, v_sgu_b_s, v_w_out_odd, v_mlp_w1, v_mlp_w2):
    weights = dict(mix_pre_g=mix_pre_g, mix_post_g=mix_post_g, mlp_pre_g=mlp_pre_g, mlp_post_g=mlp_post_g, w_in_even=w_in_even, s5_lam_re=s5_lam_re, s5_lam_im=s5_lam_im, s5_log_dt=s5_log_dt, s5_b_re=s5_b_re, s5_b_im=s5_b_im, s5_c_re=s5_c_re, s5_c_im=s5_c_im, s5_d=s5_d, s5_w_glu=s5_w_glu, fox_b_f=fox_b_f, w_out_even=w_out_even, w_in_odd=w_in_odd, pool_w=pool_w, pool_scale=pool_scale, sgu_ln_g=sgu_ln_g, sgu_ln_b=sgu_ln_b, sgu_w_s=sgu_w_s, sgu_b_s=sgu_b_s, w_out_odd=w_out_odd, mlp_w1=mlp_w1, mlp_w2=mlp_w2)
    mom_m = dict(mix_pre_g=m_mix_pre_g, mix_post_g=m_mix_post_g, mlp_pre_g=m_mlp_pre_g, mlp_post_g=m_mlp_post_g, w_in_even=m_w_in_even, s5_lam_re=m_s5_lam_re, s5_lam_im=m_s5_lam_im, s5_log_dt=m_s5_log_dt, s5_b_re=m_s5_b_re, s5_b_im=m_s5_b_im, s5_c_re=m_s5_c_re, s5_c_im=m_s5_c_im, s5_d=m_s5_d, s5_w_glu=m_s5_w_glu, fox_b_f=m_fox_b_f, w_out_even=m_w_out_even, w_in_odd=m_w_in_odd, pool_w=m_pool_w, pool_scale=m_pool_scale, sgu_ln_g=m_sgu_ln_g, sgu_ln_b=m_sgu_ln_b, sgu_w_s=m_sgu_w_s, sgu_b_s=m_sgu_b_s, w_out_odd=m_w_out_odd, mlp_w1=m_mlp_w1, mlp_w2=m_mlp_w2)
    mom_v = dict(mix_pre_g=v_mix_pre_g, mix_post_g=v_mix_post_g, mlp_pre_g=v_mlp_pre_g, mlp_post_g=v_mlp_post_g, w_in_even=v_w_in_even, s5_lam_re=v_s5_lam_re, s5_lam_im=v_s5_lam_im, s5_log_dt=v_s5_log_dt, s5_b_re=v_s5_b_re, s5_b_im=v_s5_b_im, s5_c_re=v_s5_c_re, s5_c_im=v_s5_c_im, s5_d=v_s5_d, s5_w_glu=v_s5_w_glu, fox_b_f=v_fox_b_f, w_out_even=v_w_out_even, w_in_odd=v_w_in_odd, pool_w=v_pool_w, pool_scale=v_pool_scale, sgu_ln_g=v_sgu_ln_g, sgu_ln_b=v_sgu_ln_b, sgu_w_s=v_sgu_w_s, sgu_b_s=v_sgu_b_s, w_out_odd=v_w_out_odd, mlp_w1=v_mlp_w1, mlp_w2=v_mlp_w2)
    names = list(weights)
    L = x.shape[1]
    x0 = x[0]
    target = loss_target[0]
    my_index = 4 * lax.axis_index("x") + 2 * lax.axis_index("y") + lax.axis_index("c")

    small_vec = jnp.zeros((8, LANES), F32)
    small_vec = small_vec.at[0, :64].set(pool_scale[0]).at[1, :64].set(sgu_ln_g[0]).at[2, :64].set(sgu_ln_b[0])
    ag_groups, ag_token = _comm_start(
        [[jnp.transpose(w_in_even[0]).astype(BF16), small_vec],
         [s5_w_glu[0].astype(BF16), w_out_even[0].astype(BF16)],
         [mlp_w1[0].astype(BF16), mlp_w2[0].astype(BF16)],
         [jnp.transpose(w_in_odd[0]).astype(BF16), w_out_odd[0].astype(BF16), mlp_w1[1].astype(BF16), mlp_w2[1].astype(BF16)]],
        "ag_start", exchange=False)

    lam_r = jnp.concatenate([s5_lam_re.reshape(1, S5_NS), s5_lam_im.reshape(1, S5_NS)], axis=0)
    ldt_r = jnp.repeat(s5_log_dt.reshape(32), 64).reshape(1, S5_NS)
    lam_c = jnp.transpose(lam_r)
    ldt_c = jnp.transpose(ldt_r)
    b_t = jnp.stack([jnp.tile(s5_b_re.reshape(S5_NS, 16), (1, 8)), jnp.tile(s5_b_im.reshape(S5_NS, 16), (1, 8))])
    c_t = jnp.stack([jnp.tile(s5_c_re.reshape(S5_W, 64), (1, 8)), jnp.tile(s5_c_im.reshape(S5_W, 64), (1, 8))])
    bf_pad = jnp.pad(fox_b_f, ((0, 0), (0, LANES - 8)))
    b_st = jnp.transpose(sgu_b_s[0])

    h0, rx0 = _rms_fwd(x0, _tie(mix_pre_g[0:1], ag_token), "rms0")
    tabs, bset, cset = _s5_prep(lam_r, ldt_r, lam_c, ldt_c, b_t, c_t, "s5_prep")
    ag0 = _comm_wait(ag_groups[0], tabs, "ag_wait0", exchange=False)
    winT_e = jnp.pad(ag0[0].reshape(EVEN_IN, D_MODEL), ((0, EVEN_PAD - EVEN_IN), (0, 0)))
    pool_scale_f = ag0[1][:, 0, :64].reshape(1, 512)
    ln_g_f = ag0[1][:, 1, :64].reshape(1, 512)
    ln_b_f = ag0[1][:, 2, :64].reshape(1, 512)
    z0 = _mm(h0, winT_e, name="win_even", tb=True, bm=512, bn=EVEN_PAD)
    xs, ylin = _s5_scan_fwd(z0, bset, cset, s5_d, tabs, "s5_scan")
    ag1 = _comm_wait(ag_groups[1], ylin, "ag_wait1", exchange=False)
    wglu = ag1[0].reshape(S5_W, S5_W)
    wout_e = ag1[1].reshape(D_MODEL, D_MODEL)
    ya = _s5_glu_fwd(ylin, wglu, "s5_glu")
    fcum, fq = _fox_f_fwd(z0, bf_pad, "fox_f")
    frow = jnp.transpose(fcum[:, :8]).reshape(4, 2, L)
    o_att, lse = _fox_fwd(z0, fq, frow, "fox_fwd")
    mix0 = [ya, o_att]
    x1, ry0, h1, rx1, y0 = _mm(mix0, wout_e, name="wout_even", epi=_epi_post_pre, extra=(x0,),
                               vecs=(mix_post_g[0:1], mlp_pre_g[0:1]), out_dtypes=POST_PRE_DTYPES,
                               out_kinds=POST_PRE_KINDS, bm=FUSED_ROWS)
    ag2 = _comm_wait(ag_groups[2], rx1, "ag_wait2", exchange=False)
    w1 = [ag2[0], None]
    w2 = [ag2[1].reshape(4 * D_MODEL, D_MODEL), None]
    p0, a0 = _mm(h1, w1[0], name="mlp0_w1", b3=True, out_dtypes=(BF16, BF16), epi=_epi_relu2, bm=512, bn=4 * D_MODEL)
    x2, ro0, h2, rx2, o0 = _mm(a0, w2[0], name="mlp0_w2", epi=_epi_post_pre, extra=(x1,),
                               vecs=(mlp_post_g[0:1], mix_pre_g[1:2]), out_dtypes=POST_PRE_DTYPES,
                               out_kinds=POST_PRE_KINDS, bm=FUSED_ROWS, bk=4 * D_MODEL)
    ag3 = _comm_wait(ag_groups[3], rx2, "ag_wait3", exchange=False)
    winT_o = ag3[0].reshape(ODD_IN, D_MODEL)
    wout_o = ag3[1].reshape(D_MODEL, D_MODEL)
    w1[1] = ag3[2]
    w2[1] = ag3[3].reshape(4 * D_MODEL, D_MODEL)
    z1 = _mm(h2, winT_o, name="win_odd", tb=True, bn=ODD_IN)
    yc, pooled = _pool_fwd(z1, pool_w[0], pool_scale_f, "pool_fwd")
    yd = _sgu_fwd(z1, ln_g_f, ln_b_f, sgu_w_s[0], b_st, "sgu_fwd")
    mix1 = [yc, yd]
    x3, ry1, h3, rx3, y1 = _mm(mix1, wout_o, name="wout_odd", epi=_epi_post_pre, extra=(x2,),
                               vecs=(mix_post_g[1:2], mlp_pre_g[1:2]), out_dtypes=POST_PRE_DTYPES,
                               out_kinds=POST_PRE_KINDS, bm=FUSED_ROWS)
    p1, a1 = _mm(h3, w1[1], name="mlp1_w1", b3=True, out_dtypes=(BF16, BF16), epi=_epi_relu2, bm=512, bn=4 * D_MODEL)
    gx4, g_o1, gg_mlp_post1, sq_lanes = _mm(
        a1, w2[1], name="mlp1_w2", epi=_epi_post_loss, extra=(x3, target), vecs=(mlp_post_g[1:2],),
        out_dtypes=(F32, BF16, F32, F32), out_kinds=("full", "full", "vsum", "vsum"), bm=FUSED_ROWS, bk=4 * D_MODEL)
    sq = sq_lanes[:, 0:1]

    g_p1 = _mm(g_o1, w2[1], name="b_mlp1_a", tb=True, out_dtypes=(BF16,), epi=_epi_relu2_bwd, extra=(p1,),
               bm=512, bn=4 * D_MODEL)
    gw2_1 = _mm(a1, g_o1, name="b_mlp1_w2", ta=True, bm=512, bk=L)
    gw1_1 = _mm(h3, g_p1, name="b_mlp1_w1", ta=True, out3=True, bn=512, bk=L)
    (ex1,), tok1 = _comm_start([[gw1_1, gw2_1.reshape(N_DEV, 512, D_MODEL)]], "ex_start1", exchange=True)
    g_x3, gg_mlp_pre1, g_y1, gg_mix_post1 = _mm(
        g_p1, w1[1], name="b_mlp1_h", tb=True, b3=True, epi=_epi_pre_post_bwd, extra=(x3, gx4, y1), cols=(rx3, ry1),
        vecs=(_tie(mlp_pre_g[1:2], tok1), mix_post_g[1:2]), out_dtypes=PRE_POST_BWD_DTYPES,
        out_kinds=PRE_POST_BWD_KINDS, bm=FUSED_ROWS, bk=4 * D_MODEL)
    gwout_o = _mm(mix1, g_y1, name="b_wout_odd_w", ta=True)
    g_xc, g_pool_w, g_pool_scale = _pool_bwd(g_y1, wout_o, pooled, pool_w[0], pool_scale_f, "pool_bwd")
    g_u1, g_v1, g_ws, g_bst, g_ln_g, g_ln_b = _sgu_bwd(g_y1, wout_o, z1, ln_g_f, ln_b_f, sgu_w_s[0], b_st,
                                                       "sgu_bwd")
    g_z1 = [g_xc, g_u1, g_v1]
    gwinT_o = _mm(g_z1, h2, name="b_win_odd_w", ta=True)
    (ex2,), tok2 = _comm_start([[gwout_o.reshape(N_DEV, 128, D_MODEL), gwinT_o.reshape(N_DEV, ODD_IN // N_DEV, D_MODEL)]], "ex_start2", exchange=True)
    g_x2, gg_mix_pre1, g_o0, gg_mlp_post0 = _mm(
        g_z1, winT_o, name="b_win_odd_h", epi=_epi_pre_post_bwd, extra=(x2, g_x3, o0), cols=(rx2, ro0),
        vecs=(_tie(mix_pre_g[1:2], tok2), mlp_post_g[0:1]), out_dtypes=PRE_POST_BWD_DTYPES,
        out_kinds=PRE_POST_BWD_KINDS, bm=FUSED_ROWS)
    g_p0 = _mm(g_o0, w2[0], name="b_mlp0_a", tb=True, out_dtypes=(BF16,), epi=_epi_relu2_bwd, extra=(p0,),
               bm=512, bn=4 * D_MODEL)
    gw2_0 = _mm(a0, g_o0, name="b_mlp0_w2", ta=True, bm=512, bk=L)
    gw1_0 = _mm(h1, g_p0, name="b_mlp0_w1", ta=True, out3=True, bn=512, bk=L)
    (ex3,), tok3 = _comm_start([[gw1_0, gw2_0.reshape(N_DEV, 512, D_MODEL)]], "ex_start3", exchange=True)
    g_x1, gg_mlp_pre0, g_y0, gg_mix_post0 = _mm(
        g_p0, w1[0], name="b_mlp0_h", tb=True, b3=True, epi=_epi_pre_post_bwd, extra=(x1, g_x2, y0), cols=(rx1, ry0),
        vecs=(_tie(mlp_pre_g[0:1], tok3), mix_post_g[0:1]), out_dtypes=PRE_POST_BWD_DTYPES,
        out_kinds=PRE_POST_BWD_KINDS, bm=FUSED_ROWS, bk=4 * D_MODEL)
    g_o_att = _mm(g_y0, wout_e[FOX_W:], name="b_wout_even_m", tb=True)
    gwout_e = _mm(mix0, g_y0, name="b_wout_even_w", ta=True)
    gyl, gud, g_wglu, g_d = _s5_glu_bwd(g_y0, wout_e, ylin, z0, s5_d, wglu, "s5_glu_bwd")
    (ex4,), tok4 = _comm_start([[gwout_e.reshape(N_DEV, 128, D_MODEL), g_wglu.reshape(N_DEV, 64, S5_W)]], "ex_start4", exchange=True)
    g_u0, ga, gb_raw, gc_raw = _s5_scan_bwd(gyl, _tie(cset, tok4), xs, z0, bset, gud, tabs, "s5_scan_bwd")
    g_lam, g_ldt, g_b, g_c = _s5_param_bwd(lam_c, ldt_c, b_t, gb_raw, jnp.transpose(ga), gc_raw, "s5_param_bwd")
    dq, dk, dv, dfq, dfrow = _fox_bwd(z0, fq, frow, o_att, lse, g_o_att, "fox_bwd")
    dFk = jnp.pad(jnp.transpose(dfrow.reshape(8, L)), ((0, 0), (0, LANES - 8)))
    dfl, db_f = _fox_f_bwd(dFk, dfq, z0, bf_pad, "fox_f_bwd")
    g_z0 = [g_u0, dq, dk, dv, dfl]
    grad_x, gg_mix_pre0 = _mm(g_z0, winT_e, name="b_win_even_h", epi=_epi_pre_bwd, extra=(x0, g_x1), cols=(rx0,),
                              vecs=(mix_pre_g[0:1],), out_dtypes=(F32, F32), out_kinds=("full", "vsum"),
                              bm=FUSED_ROWS)

    small_grads = dict(
        mix_pre_g=jnp.concatenate([gg_mix_pre0, gg_mix_pre1]), mix_post_g=jnp.concatenate([gg_mix_post0, gg_mix_post1]),
        mlp_pre_g=jnp.concatenate([gg_mlp_pre0, gg_mlp_pre1]), mlp_post_g=jnp.concatenate([gg_mlp_post0, gg_mlp_post1]),
        s5_lam_re=g_lam[:, 0], s5_lam_im=g_lam[:, 1], s5_log_dt=g_ldt,
        s5_b_re=g_b[0, :, :16], s5_b_im=g_b[1, :, :16], s5_c_re=g_c[0, :, :64], s5_c_im=g_c[1, :, :64],
        s5_d=g_d, fox_b_f=db_f[:, :8], pool_w=g_pool_w, sgu_w_s=g_ws, sgu_b_s=jnp.transpose(g_bst),
        pool_scale=g_pool_scale, sgu_ln_g=g_ln_g, sgu_ln_b=g_ln_b)
    small_names = list(small_grads)
    full_shapes = [(512,) if nm in ("pool_scale", "sgu_ln_g", "sgu_ln_b") else weights[nm].shape for nm in small_names]
    full_shapes.append((1, 1))
    rows = _packed_rows(full_shapes)
    packed = _pack([small_grads[nm] for nm in small_names] + [sq], rows).reshape(N_DEV, rows // N_DEV, LANES)
    (exs,), tok_s = _comm_start([[packed]], "exs_start", exchange=True)
    gwinT_e = _mm(g_z0, h0, name="b_win_even_w", ta=True, bk=512, out_dtypes=(BF16,), dep=tok_s)
    (recv_small,) = _comm_wait(exs, gwinT_e, "exs_wait", exchange=True)
    piece = _sum_pieces(recv_small, "sum_small")
    (ags,), tok_a = _comm_start([[piece]], "ags_start", exchange=False)

    gwinT_e_pieces = gwinT_e[:EVEN_IN].reshape(N_DEV, EVEN_IN // N_DEV, D_MODEL)
    (ex5,), tok5 = _comm_start([[gwinT_e_pieces]], "ex_start5", exchange=True, dep=tok_a)
    r_w1_1, r_w2_1 = _comm_wait(ex1, tok5, "ex_wait1", exchange=True)
    r_wout_o, r_win_o = _comm_wait(ex2, tok5, "ex_wait2", exchange=True)
    r_w1_0, r_w2_0 = _comm_wait(ex3, tok5, "ex_wait3", exchange=True)
    r_wout_e, r_wglu = _comm_wait(ex4, tok5, "ex_wait4", exchange=True)

    res = {}
    for nm, parts in (("mlp_w1", (r_w1_0, r_w1_1)), ("mlp_w2", (r_w2_0, r_w2_1))):
        first = _sum_adamw(parts[0], weights[nm], mom_m[nm], mom_v[nm], "adamw_%s_0" % nm, layer=0)
        res[nm] = tuple(_sum_adamw(parts[1], weights[nm], mom_m[nm], mom_v[nm], "adamw_%s_1" % nm, layer=1, prev=first))
    big_parts = dict(s5_w_glu=r_wglu, w_out_even=r_wout_e, w_out_odd=r_wout_o)
    for nm, parts in big_parts.items():
        res[nm] = tuple(_sum_adamw(parts, weights[nm], mom_m[nm], mom_v[nm], "adamw_" + nm))
    done = [res[nm][1] for nm in ("mlp_w1", "mlp_w2", "s5_w_glu", "w_out_even", "w_out_odd")]

    (small_all,) = _comm_wait(ags, done, "ags_wait", exchange=False)
    small_full = _unpack(small_all.reshape(rows, LANES), full_shapes)
    loss = 0.5 * small_full.pop()[0, 0] / D_MODEL
    small_g = []
    for nm, g in zip(small_names, small_full):
        if nm in ("pool_scale", "sgu_ln_g", "sgu_ln_b"):
            g = lax.dynamic_slice(g, (my_index * 64,), (64,)).reshape(1, 64)
        small_g.append(g)
    sd, sm, sv = _adamw_many([weights[nm] for nm in small_names], small_g, [mom_m[nm] for nm in small_names],
                             [mom_v[nm] for nm in small_names], "adamw_small")
    for nm, g_, d_, m_, v_ in zip(small_names, small_g, sd, sm, sv):
        res[nm] = (g_, d_, m_, v_)
    done.append(sd[0])

    for nm, parts in (("w_in_odd", r_win_o), ("w_in_even", None)):
        if parts is None:
            (parts,) = _comm_wait(ex5, done, "ex_wait5", exchange=True)
        outs = _sum_adamw(parts, jnp.transpose(weights[nm], (0, 2, 1)), jnp.transpose(mom_m[nm], (0, 2, 1)),
                          jnp.transpose(mom_v[nm], (0, 2, 1)), "adamw_" + nm)
        res[nm] = tuple(jnp.transpose(o, (0, 2, 1)) for o in outs)
        done.append(res[nm][1])

    grads = [res[nm][0].reshape(weights[nm].shape) for nm in names]
    deltas = [res[nm][1].reshape(weights[nm].shape) for nm in names]
    new_m = [res[nm][2].reshape(weights[nm].shape) for nm in names]
    new_v = [res[nm][3].reshape(weights[nm].shape) for nm in names]
    return (loss, grad_x[None], *grads, *deltas, *new_m, *new_v)
```

```python
import math

import jax
import jax.numpy as jnp
from jax import lax
from jax.experimental import pallas as pl
from jax.experimental.pallas import tpu as pltpu

F32 = jnp.float32
BF16 = jnp.bfloat16
MESH = pl.DeviceIdType.MESH
ANY = pl.BlockSpec(memory_space=pl.ANY)

N_DEV = 8
D_MODEL = 1024
EPS = 1e-6
NORM_ROWS = 512
FUSED_ROWS = 512
S5_W = 512
S5_NS = 2048
SCAN_GROUPS = 4
SCAN_CHUNK = 1024
FOX_W = 512
EVEN_IN = 2056
EVEN_PAD = 2176
ODD_IN = 1536
LANES = 128
PIECE = 4 * D_MODEL // N_DEV
VMEM_LIMIT = 56 * 1024 * 1024

ADAM_LR = 0.001
ADAM_B1 = 0.9
ADAM_B2 = 0.999
ADAM_EPS = 1e-08
ADAM_WD = 0.01
ADAM_STEP = 10

NT = (((1,), (1,)), ((), ()))
TN = (((0,), (0,)), ((), ()))
NN = (((1,), (0,)), ((), ()))


def _cp(*sem):
    return pltpu.CompilerParams(dimension_semantics=sem, vmem_limit_bytes=VMEM_LIMIT)


def _sds(shape, dtype=F32):
    return jax.ShapeDtypeStruct(tuple(shape), dtype)


def _gelu(x):
    t = jnp.tanh(0.7978845608028654 * (x + 0.044715 * x * x * x))
    return 0.5 * x * (1.0 + t)


def _gelu_grad(x):
    t = jnp.tanh(0.7978845608028654 * (x + 0.044715 * x * x * x))
    du = 0.7978845608028654 * (1.0 + 3.0 * 0.044715 * x * x)
    return 0.5 * (1.0 + t) + 0.5 * x * (1.0 - t * t) * du


def _sigmoid(x):
    return 1.0 / (1.0 + jnp.exp(-x))


def _dot(a, b, dn=NN):
    return lax.dot_general(a, b, dn, preferred_element_type=F32)


def _mm(a, b, *, name, ta=False, tb=False, b3=False, out3=False, out_dtypes=(F32,), epi=None, extra=(),
        cols=(), vecs=(), out_kinds=None, bm=1024, bn=1024, bk=1024, dep=None):
    a_list = list(a) if isinstance(a, (list, tuple)) else [a]
    widths = [p.shape[1] for p in a_list]
    offs = [sum(widths[:i]) for i in range(len(widths))]
    na = len(a_list)
    M = sum(widths) if ta else a_list[0].shape[0]
    K = a_list[0].shape[0] if ta else sum(widths)
    if na > 1:
        assert not b3 and not tb
        bm, bk = (M, bk) if ta else (bm, K)
    pw = b.shape[2] if b3 else PIECE
    if b3:
        N = b.shape[1] if tb else b.shape[0] * pw
        assert (b.shape[0] * pw if tb else b.shape[1]) == K
    else:
        N = b.shape[0] if tb else b.shape[1]
    bm, bn, bk = min(bm, M), min(bn, N), min(bk, K)
    assert M % bm == 0 and N % bn == 0 and K % bk == 0, (name, M, N, K, bm, bn, bk)
    assert not (b3 or out3) or ((bk if tb else bn) % pw == 0 and bn % PIECE == 0)
    nk = K // bk
    n_extra = len(extra) + len(cols) + len(vecs)
    n_out = len(out_dtypes)
    out_kinds = tuple(out_kinds) if out_kinds is not None else ("full",) * n_out
    dn = (((0 if ta else 1,), (1 if tb else 0,)), ((), ()))

    use_acc = nk > 1

    def body(*refs):
        a_refs, b_ref = refs[:na], refs[na]
        a_ref = a_refs[0]
        e_refs = refs[na + 1:na + 1 + n_extra]
        first_out = na + 1 + n_extra + (0 if dep is None else 1)
        o_refs = refs[first_out:first_out + n_out]
        acc_ref = refs[-1] if use_acc else o_refs[0]
        i, k = pl.program_id(0), pl.program_id(2)

        def dot(a_v, b_v):
            return lax.dot_general(a_v.astype(BF16), b_v.astype(BF16), dn, preferred_element_type=F32)

        everything = slice(None)
        if na > 1 and ta:
            terms = [(pl.ds(off, w), everything, r, b_ref) for r, off, w in zip(a_refs, offs, widths)]
        elif na > 1:
            terms = [(everything, everything, r, b_ref.at[pl.ds(off, w), :]) for r, off, w in zip(a_refs, offs, widths)]
        elif not b3:
            terms = [(everything, everything, a_ref, b_ref)]
        elif tb:
            terms = [(everything, everything,
                      a_ref.at[pl.ds(t * pw, pw), :] if ta else a_ref.at[:, pl.ds(t * pw, pw)], b_ref.at[t])
                     for t in range(bk // pw)]
        else:
            terms = [(everything, pl.ds(t * pw, pw), a_ref, b_ref.at[t]) for t in range(bn // pw)]

        def finish(acc):
            outs = (acc,) if epi is None else epi(acc, *[e[...] for e in e_refs])
            for o_ref, o, kind in zip(o_refs, outs, out_kinds):
                if kind == "vsum":
                    @pl.when(i == 0)
                    def _(o_ref=o_ref, o=o):
                        o_ref[...] = o

                    @pl.when(i > 0)
                    def _(o_ref=o_ref, o=o):
                        o_ref[...] += o
                elif out3:
                    for t in range(bn // PIECE):
                        o_ref[t] = o[:, t * PIECE:(t + 1) * PIECE].astype(o_ref.dtype)
                else:
                    o_ref[...] = o.astype(o_ref.dtype)

        if nk == 1:
            bands = {}
            for rows, cols, a_r, b_r in terms:
                key = (getattr(rows, "start", None), getattr(cols, "start", None))
                val = dot(a_r[...], b_r[...])
                bands[key] = val if key not in bands else bands[key] + val
            vals = list(bands.values())
            if len(vals) == 1:
                finish(vals[0])
            else:
                finish(jnp.concatenate(vals, axis=0 if (na > 1 and ta) else 1))
            return

        @pl.when(k == 0)
        def _():
            acc_ref[...] = jnp.zeros_like(acc_ref)

        for rows, cols, a_r, b_r in terms:
            acc_ref[rows, cols] += dot(a_r[...], b_r[...])

        @pl.when(k == nk - 1)
        def _():
            finish(acc_ref[...])

    if na > 1:
        a_specs = [pl.BlockSpec((bk, w), lambda i, j, k: (k, 0)) if ta else pl.BlockSpec((bm, w), lambda i, j, k: (i, 0))
                   for w in widths]
    else:
        a_specs = [pl.BlockSpec((bk, bm), lambda i, j, k: (k, i)) if ta else
                   pl.BlockSpec((bm, bk), lambda i, j, k: (i, k))]
    if b3:
        if tb:
            b_spec = pl.BlockSpec((bk // pw, bn, pw), lambda i, j, k: (k, j, 0))
        else:
            b_spec = pl.BlockSpec((bn // pw, bk, pw), lambda i, j, k: (j, k, 0))
    else:
        b_spec = pl.BlockSpec((bn, bk), lambda i, j, k: (j, k)) if tb else pl.BlockSpec((bk, bn), lambda i, j, k: (k, j))
    e_specs = ([pl.BlockSpec((bm, bn), lambda i, j, k: (i, j)) for _ in extra]
               + [pl.BlockSpec((bm, 1), lambda i, j, k: (i, 0)) for _ in cols]
               + [pl.BlockSpec((1, bn), lambda i, j, k: (0, j)) for _ in vecs])
    if out3:
        o_specs = [pl.BlockSpec((bn // PIECE, bm, PIECE), lambda i, j, k: (j, i, 0)) for _ in out_dtypes]
        o_shapes = [_sds((N // PIECE, M, PIECE), dt) for dt in out_dtypes]
    else:
        spec_of = {"full": pl.BlockSpec((bm, bn), lambda i, j, k: (i, j)),
                   "col": pl.BlockSpec((bm, 1), lambda i, j, k: (i, 0)),
                   "vsum": pl.BlockSpec((1, bn), lambda i, j, k: (0, j))}
        shape_of = {"full": (M, N), "col": (M, 1), "vsum": (1, N)}
        o_specs = [spec_of[kind] for kind in out_kinds]
        o_shapes = [_sds(shape_of[kind], dt) for kind, dt in zip(out_kinds, out_dtypes)]
    assert "col" not in out_kinds or bn == N
    outs = pl.pallas_call(
        body, name=name, grid=(M // bm, N // bn, nk),
        in_specs=a_specs + [b_spec] + e_specs + ([] if dep is None else [ANY]),
        out_specs=o_specs, out_shape=o_shapes,
        scratch_shapes=[pltpu.VMEM((bm, bn), F32)] if use_acc else [],
        compiler_params=_cp("arbitrary" if "vsum" in out_kinds else "parallel", "parallel", "arbitrary"),
    )(*a_list, b, *extra, *cols, *vecs, *([] if dep is None else [dep]))
    return outs[0] if n_out == 1 else outs


def _epi_relu2(acc):
    r = jnp.maximum(acc, 0.0)
    return acc, r * r


def _epi_relu2_bwd(acc, p):
    return (acc * (2.0 * jnp.maximum(p.astype(F32), 0.0)),)


def _row_spec(rb, w=D_MODEL):
    return pl.BlockSpec((rb, w), lambda i: (i, 0))


def _vec_spec(w=D_MODEL):
    return pl.BlockSpec((1, w), lambda i: (0, 0))


def _rstd(v):
    return lax.rsqrt(jnp.mean(v * v, axis=-1, keepdims=True) + EPS)


def _rms_fwd(x, g, name):
    L = x.shape[0]
    rb = min(NORM_ROWS, L)

    def body(x_ref, g_ref, h_ref, r_ref):
        xv = x_ref[...]
        r = _rstd(xv)
        h_ref[...] = (xv * r * g_ref[...]).astype(BF16)
        r_ref[...] = r

    return pl.pallas_call(
        body, name=name, grid=(L // rb,),
        in_specs=[_row_spec(rb), _vec_spec()],
        out_specs=[_row_spec(rb), _row_spec(rb, 1)],
        out_shape=[_sds((L, D_MODEL), BF16), _sds((L, 1))],
        compiler_params=_cp("parallel"),
    )(x, g)


def _rms_bwd_rows(dy, xv, r, g):
    n = xv * r
    dyg = dy * g
    return r * (dyg - n * jnp.mean(dyg * n, axis=-1, keepdims=True)), n


POST_PRE_DTYPES = (F32, F32, BF16, F32, F32)
POST_PRE_KINDS = ("full", "col", "full", "col", "full")
PRE_POST_BWD_DTYPES = (F32, F32, BF16, F32)
PRE_POST_BWD_KINDS = ("full", "vsum", "full", "vsum")


def _epi_post_pre(y, x_in, g_post, g_pre):
    ry = _rstd(y)
    xo = x_in + y * ry * g_post
    rx = _rstd(xo)
    return xo, ry, xo * rx * g_pre, rx, y


def _epi_pre_post_bwd(gh, x, g_out, y_prev, rx, ry_prev, g_pre, g_post_prev):
    gx, n = _rms_bwd_rows(gh, x, rx, g_pre)
    gi = g_out + gx
    gy, ny = _rms_bwd_rows(gi, y_prev, ry_prev, g_post_prev)
    return gi, jnp.sum(gh * n, axis=0, keepdims=True), gy, jnp.sum(gi * ny, axis=0, keepdims=True)


def _epi_pre_bwd(gh, x, g_out, rx, g_pre):
    gx, n = _rms_bwd_rows(gh, x, rx, g_pre)
    return g_out + gx, jnp.sum(gh * n, axis=0, keepdims=True)


def _epi_post_loss(y, x_in, target, g_post):
    ry = _rstd(y)
    diff = x_in + y * ry * g_post - target
    gx = diff * (1.0 / D_MODEL)
    gy, n = _rms_bwd_rows(gx, y, ry, g_post)
    sq = jnp.broadcast_to(jnp.sum(diff * diff, keepdims=True), (1, y.shape[1]))
    return gx, gy, jnp.sum(gx * n, axis=0, keepdims=True), sq


def _cmul(ar, ai, br, bi):
    return ar * br - ai * bi, ar * bi + ai * br


def _zoh_cols(lr, li, ldt):
    dt = jnp.exp(ldt)
    mag = jnp.exp(lr * dt)
    ar = mag * jnp.cos(li * dt)
    ai = mag * jnp.sin(li * dt)
    den = lr * lr + li * li
    nr = ar - 1.0
    qr = (nr * lr + ai * li) / den
    qi = (ai * lr - nr * li) / den
    return dt, ar, ai, qr, qi, den


def _b_mask():
    r = lax.broadcasted_iota(jnp.int32, (S5_NS, LANES), 0)
    c = lax.broadcasted_iota(jnp.int32, (S5_NS, LANES), 1)
    return ((r >> 6) & 7) == (c >> 4)


def _c_mask():
    r = lax.broadcasted_iota(jnp.int32, (S5_W, 512), 0)
    c = lax.broadcasted_iota(jnp.int32, (S5_W, 512), 1)
    return ((r >> 4) & 7) == (c >> 6)


def _s5_prep(lam_r, ldt_r, lam_c, ldt_c, b_t, c_t, name):
    def body(lam_r_ref, ldt_r_ref, lam_c_ref, ldt_c_ref, b_ref, c_ref, tab_ref, bset_ref, cset_ref):
        lr, li = lam_r_ref[0:1, :], lam_r_ref[1:2, :]
        dt = jnp.exp(ldt_r_ref[...])
        mag = jnp.exp(lr * dt)
        p1r, p1i = mag * jnp.cos(li * dt), mag * jnp.sin(li * dt)
        p2r, p2i = _cmul(p1r, p1i, p1r, p1i)
        p3r, p3i = _cmul(p2r, p2i, p1r, p1i)
        p4r, p4i = _cmul(p2r, p2i, p2r, p2i)
        p5r, p5i = _cmul(p4r, p4i, p1r, p1i)
        p6r, p6i = _cmul(p4r, p4i, p2r, p2i)
        p7r, p7i = _cmul(p4r, p4i, p3r, p3i)
        p8r, p8i = _cmul(p4r, p4i, p4r, p4i)
        pw_r = [p1r, p2r, p3r, p4r, p5r, p6r, p7r, p8r]
        pw_i = [p1i, p2i, p3i, p4i, p5i, p6i, p7i, p8i]
        row = lax.broadcasted_iota(jnp.int32, (8, S5_NS), 0)
        zero = jnp.zeros((8, S5_NS), F32)

        def bc(v):
            return jnp.broadcast_to(v, (8, S5_NS))

        for d in range(2):
            sgn = 1.0 if d == 0 else -1.0
            for t, s in enumerate((1, 2, 4)):
                live = (row >= s) if d == 0 else (row <= 7 - s)
                tab_ref[d, 2 * t] = jnp.where(live, bc(pw_r[s - 1]), zero)
                tab_ref[d, 2 * t + 1] = jnp.where(live, bc(sgn * pw_i[s - 1]), zero)
            cr, ci = zero, zero
            for i in range(8):
                e = i if d == 0 else 7 - i
                cr = jnp.where(row == i, bc(pw_r[e]), cr)
                ci = jnp.where(row == i, bc(sgn * pw_i[e]), ci)
            tab_ref[d, 6] = cr
            tab_ref[d, 7] = ci

        _, _, _, qr, qi, _ = _zoh_cols(lam_c_ref[:, 0:1], lam_c_ref[:, 1:2], ldt_c_ref[...])
        bm = _b_mask()
        br, bi = b_ref[0], b_ref[1]
        bset_ref[0] = jnp.where(bm, qr * br - qi * bi, 0.0).astype(BF16)
        bset_ref[1] = jnp.where(bm, qr * bi + qi * br, 0.0).astype(BF16)
        cm = _c_mask()
        cset_ref[0] = jnp.where(cm, c_ref[0], 0.0).astype(BF16)
        cset_ref[1] = jnp.where(cm, c_ref[1], 0.0).astype(BF16)

    vm = pl.BlockSpec(memory_space=pltpu.VMEM)
    return pl.pallas_call(
        body, name=name, in_specs=[vm] * 6, out_specs=[vm] * 3,
        out_shape=[_sds((2, 8, 8, S5_NS)), _sds((2, S5_NS, LANES), BF16), _sds((2, S5_W, 512), BF16)],
        compiler_params=pltpu.CompilerParams(vmem_limit_bytes=VMEM_LIMIT),
    )(lam_r, ldt_r, lam_c, ldt_c, b_t, c_t)


SCAN_W = SCAN_GROUPS * LANES


def _scan_chunk(src_ref, dst_ref, tab_ref, carry_ref, nb, reverse, xs_ref=None, acc_ref=None):
    row = lax.broadcasted_iota(jnp.int32, (8, LANES), 0)

    def step(i, carry):
        b = (nb - 1 - i) if reverse else i
        off = pl.multiple_of(b * 8, 8)
        out = []
        for g in range(SCAN_GROUPS):
            lanes = pl.ds(g * LANES, LANES)
            cr, ci = carry[2 * g], carry[2 * g + 1]
            yr = src_ref[0, pl.ds(off, 8), lanes]
            yi = src_ref[1, pl.ds(off, 8), lanes]
            for t, s in enumerate((1, 2, 4)):
                sh = (8 - s) if reverse else s
                sr = pltpu.roll(yr, sh, 0)
                si = pltpu.roll(yi, sh, 0)
                mr, mi = tab_ref[2 * t, :, lanes], tab_ref[2 * t + 1, :, lanes]
                yr, yi = yr + mr * sr - mi * si, yi + mr * si + mi * sr
            pr, pi = tab_ref[6, :, lanes], tab_ref[7, :, lanes]
            yr, yi = yr + pr * cr - pi * ci, yi + pr * ci + pi * cr
            dst_ref[0, pl.ds(off, 8), lanes] = yr
            dst_ref[1, pl.ds(off, 8), lanes] = yi
            if xs_ref is not None:
                nr = jnp.where(row == 7, cr, pltpu.roll(yr, 7, 0))
                ni = jnp.where(row == 7, ci, pltpu.roll(yi, 7, 0))
                xr = xs_ref[0, pl.ds(off, 8), lanes]
                xi = xs_ref[1, pl.ds(off, 8), lanes]
                acc_ref[0, :, lanes] += xr * nr + xi * ni
                acc_ref[1, :, lanes] += xr * ni - xi * nr
            last = 0 if reverse else 7
            out += [jnp.broadcast_to(yr[last:last + 1, :], (8, LANES)),
                    jnp.broadcast_to(yi[last:last + 1, :], (8, LANES))]
        return tuple(out)

    init = []
    for g in range(SCAN_GROUPS):
        init += [carry_ref[0, :, pl.ds(g * LANES, LANES)], carry_ref[1, :, pl.ds(g * LANES, LANES)]]
    fin = lax.fori_loop(0, nb, step, tuple(init))
    for g in range(SCAN_GROUPS):
        carry_ref[0, :, pl.ds(g * LANES, LANES)] = fin[2 * g]
        carry_ref[1, :, pl.ds(g * LANES, LANES)] = fin[2 * g + 1]


def _s5_scan_fwd(z, bset, cset, dvec, tabs, name):
    L = z.shape[0]
    tl = min(SCAN_CHUNK, L)
    nc = L // tl

    def body(u_ref, b_ref, c_ref, d_ref, tab_ref, x_ref, y_ref, carry_ref):
        @pl.when(pl.program_id(1) == 0)
        def _():
            carry_ref[...] = jnp.zeros_like(carry_ref)

        uf = u_ref[...]
        u = uf.astype(BF16)
        x_ref[0] = _dot(u, b_ref[0], NT)
        x_ref[1] = _dot(u, b_ref[1], NT)
        _scan_chunk(x_ref, x_ref, tab_ref, carry_ref, tl // 8, False)
        y_ref[...] = (_dot(x_ref[0].astype(BF16), c_ref[0], NT) - _dot(x_ref[1].astype(BF16), c_ref[1], NT)
                      + d_ref[...] * uf)

    col = pl.BlockSpec((tl, LANES), lambda j, c: (c, j))
    return pl.pallas_call(
        body, name=name, grid=(S5_NS // SCAN_W, nc),
        in_specs=[col, pl.BlockSpec((2, SCAN_W, LANES), lambda j, c: (0, j, 0)),
                  pl.BlockSpec((2, LANES, SCAN_W), lambda j, c: (0, j, 0)),
                  pl.BlockSpec((1, LANES), lambda j, c: (0, j)),
                  pl.BlockSpec((None, 8, 8, SCAN_W), lambda j, c: (0, 0, 0, j))],
        out_specs=[pl.BlockSpec((2, tl, SCAN_W), lambda j, c: (0, c, j)), col],
        out_shape=[_sds((2, L, S5_NS)), _sds((L, S5_W))],
        scratch_shapes=[pltpu.VMEM((2, 8, SCAN_W), F32)],
        compiler_params=_cp("parallel", "arbitrary"),
    )(z, bset, cset, dvec, tabs)


def _s5_scan_bwd(gyl, cset, xs, z, bset, gud, tabs, name):
    L = z.shape[0]
    tl = min(SCAN_CHUNK, L)
    nc = L // tl

    def body(g_ref, c_ref, xs_ref, u_ref, b_ref, gud_ref, tab_ref, gu_ref, ga_ref, gb_ref, gc_ref,
             gx_ref, carry_ref, acc_ref):
        c = pl.program_id(1)

        @pl.when(c == 0)
        def _():
            carry_ref[...] = jnp.zeros_like(carry_ref)
            acc_ref[...] = jnp.zeros_like(acc_ref)
            gb_ref[...] = jnp.zeros_like(gb_ref)
            gc_ref[...] = jnp.zeros_like(gc_ref)

        gy = g_ref[...].astype(BF16)
        gx_ref[0] = _dot(gy, c_ref[0])
        gx_ref[1] = -_dot(gy, c_ref[1])
        gc_ref[0] += _dot(gy, xs_ref[0].astype(BF16), TN)
        gc_ref[1] -= _dot(gy, xs_ref[1].astype(BF16), TN)
        _scan_chunk(gx_ref, gx_ref, tab_ref, carry_ref, tl // 8, True, xs_ref, acc_ref)
        gr = gx_ref[0].astype(BF16)
        gi = gx_ref[1].astype(BF16)
        gu_ref[...] = gud_ref[...] + _dot(gr, b_ref[0]) + _dot(gi, b_ref[1])
        u = u_ref[...].astype(BF16)
        gb_ref[0] += _dot(gr, u, TN)
        gb_ref[1] += _dot(gi, u, TN)

        @pl.when(c == nc - 1)
        def _():
            ga_ref[0:1, :] = jnp.sum(acc_ref[0], axis=0, keepdims=True)
            ga_ref[1:2, :] = jnp.sum(acc_ref[1], axis=0, keepdims=True)

    rev = lambda j, c: (nc - 1 - c, j)
    col = pl.BlockSpec((tl, LANES), rev)
    return pl.pallas_call(
        body, name=name, grid=(S5_NS // SCAN_W, nc),
        in_specs=[col, pl.BlockSpec((2, LANES, SCAN_W), lambda j, c: (0, j, 0)),
                  pl.BlockSpec((2, tl, SCAN_W), lambda j, c: (0, nc - 1 - c, j)), col,
                  pl.BlockSpec((2, SCAN_W, LANES), lambda j, c: (0, j, 0)), col,
                  pl.BlockSpec((None, 8, 8, SCAN_W), lambda j, c: (1, 0, 0, j))],
        out_specs=[col, pl.BlockSpec((2, SCAN_W), lambda j, c: (0, j)),
                   pl.BlockSpec((2, SCAN_W, LANES), lambda j, c: (0, j, 0)),
                   pl.BlockSpec((2, LANES, SCAN_W), lambda j, c: (0, j, 0))],
        out_shape=[_sds((L, S5_W)), _sds((2, S5_NS)), _sds((2, S5_NS, LANES)), _sds((2, S5_W, 512))],
        scratch_shapes=[pltpu.VMEM((2, tl, SCAN_W), F32), pltpu.VMEM((2, 8, SCAN_W), F32),
                        pltpu.VMEM((2, 8, SCAN_W), F32)],
        compiler_params=_cp("parallel", "arbitrary"),
    )(gyl, cset, xs, z, bset, gud, tabs)


def _s5_glu_fwd(ylin, wglu, name):
    L = ylin.shape[0]
    bl = min(1024, L)

    def body(ylin_ref, w_ref, ya_ref):
        yg = _gelu(ylin_ref[...])
        t = _dot(yg.astype(BF16), w_ref[...])
        ya_ref[...] = (yg * _sigmoid(t)).astype(BF16)

    return pl.pallas_call(
        body, name=name, grid=(L // bl,),
        in_specs=[pl.BlockSpec((bl, S5_W), lambda i: (i, 0)), pl.BlockSpec((S5_W, S5_W), lambda i: (0, 0))],
        out_specs=pl.BlockSpec((bl, S5_W), lambda i: (i, 0)),
        out_shape=_sds((L, S5_W), BF16),
        compiler_params=_cp("parallel"),
    )(ylin, wglu)


def _s5_glu_bwd(g_y, wout, ylin, z, dvec, wglu, name):
    L = z.shape[0]
    bl = min(256, L)

    def body(g_ref, wo_ref, ylin_ref, u_ref, d_ref, w_ref, gyl_ref, gud_ref, gw_ref, gd_ref):
        i = pl.program_id(0)
        ylin = ylin_ref[...]
        yg = _gelu(ylin)
        ygb = yg.astype(BF16)
        sg = _sigmoid(_dot(ygb, w_ref[...]))
        gya = _dot(g_ref[...], wo_ref[...], NT)
        gt = gya * yg * sg * (1.0 - sg)
        gtb = gt.astype(BF16)
        gyg = gya * sg + _dot(gtb, w_ref[...], NT)
        gyl = gyg * _gelu_grad(ylin)
        gyl_ref[...] = gyl
        gud_ref[...] = gyl * d_ref[...]

        @pl.when(i == 0)
        def _():
            gw_ref[...] = jnp.zeros_like(gw_ref)
            gd_ref[...] = jnp.zeros_like(gd_ref)

        gw_ref[...] += _dot(ygb, gtb, TN)
        gd_ref[...] += jnp.sum(gyl * u_ref[...], axis=0, keepdims=True)

    blk = pl.BlockSpec((bl, S5_W), lambda i: (i, 0))
    return pl.pallas_call(
        body, name=name, grid=(L // bl,),
        in_specs=[pl.BlockSpec((bl, D_MODEL), lambda i: (i, 0)), pl.BlockSpec((S5_W, D_MODEL), lambda i: (0, 0)),
                  blk, blk, pl.BlockSpec((1, S5_W), lambda i: (0, 0)), pl.BlockSpec((S5_W, S5_W), lambda i: (0, 0))],
        out_specs=[blk, blk, pl.BlockSpec((S5_W, S5_W), lambda i: (0, 0)), pl.BlockSpec((1, S5_W), lambda i: (0, 0))],
        out_shape=[_sds((L, S5_W)), _sds((L, S5_W)), _sds((S5_W, S5_W)), _sds((1, S5_W))],
        compiler_params=_cp("arbitrary"),
    )(g_y, wout, ylin, z, dvec, wglu)


def _s5_param_bwd(lam_c, ldt_c, b_t, gb, ga_c, gc, name):
    def body(lam_ref, ldt_ref, b_ref, gb_ref, ga_ref, gc_ref, glam_ref, gldt_ref, gbo_ref, gco_ref):
        lr, li = lam_ref[:, 0:1], lam_ref[:, 1:2]
        dt, ar, ai, qr, qi, den = _zoh_cols(lr, li, ldt_ref[...])
        bm = _b_mask()
        gbr = jnp.where(bm, gb_ref[0], 0.0)
        gbi = jnp.where(bm, gb_ref[1], 0.0)
        br, bi = b_ref[0], b_ref[1]
        obr = gbr * qr + gbi * qi
        obi = gbi * qr - gbr * qi
        gqr = jnp.sum(gbr * br + gbi * bi, axis=1, keepdims=True)
        gqi = jnp.sum(gbi * br - gbr * bi, axis=1, keepdims=True)
        for s in (64, 32, 16):
            obr = obr + pltpu.roll(obr, s, 1)
            obi = obi + pltpu.roll(obi, s, 1)
        gbo_ref[0] = obr
        gbo_ref[1] = obi
        gar = ga_ref[:, 0:1] + (gqr * lr - gqi * li) / den
        gai = ga_ref[:, 1:2] + (gqr * li + gqi * lr) / den
        qlr = (qr * lr + qi * li) / den
        qli = (qi * lr - qr * li) / den
        glr = -(gqr * qlr + gqi * qli)
        gli = -(gqi * qlr - gqr * qli)
        glr = glr + dt * (gar * ar + gai * ai)
        gli = gli + dt * (gai * ar - gar * ai)
        wr, wi = _cmul(lr, li, ar, ai)
        gldt = (gar * wr + gai * wi) * dt
        glam_ref[:, 0:1] = glr
        glam_ref[:, 1:2] = gli
        r = lax.broadcasted_iota(jnp.int32, (S5_NS, 32), 0)
        c = lax.broadcasted_iota(jnp.int32, (S5_NS, 32), 1)
        gldt_ref[...] = jnp.sum(jnp.where((r >> 6) == c, gldt, 0.0), axis=0, keepdims=True)
        cm = _c_mask()
        for k in range(2):
            oc = jnp.where(cm, gc_ref[k], 0.0)
            for s in (256, 128, 64):
                oc = oc + pltpu.roll(oc, s, 1)
            gco_ref[k] = oc[:, 0:LANES]

    vm = pl.BlockSpec(memory_space=pltpu.VMEM)
    return pl.pallas_call(
        body, name=name, in_specs=[vm] * 6, out_specs=[vm] * 4,
        out_shape=[_sds((S5_NS, 2)), _sds((1, 32)), _sds((2, S5_NS, LANES)), _sds((2, S5_W, LANES))],
        compiler_params=pltpu.CompilerParams(vmem_limit_bytes=VMEM_LIMIT),
    )(lam_c, ldt_c, b_t, gb, ga_c, gc)


FL_BLK = EVEN_PAD // LANES - 1
Q_BLK, K_BLK, V_BLK = 4, 8, 12
NEG = -1e30


def _log_sigmoid(v):
    return jnp.minimum(v, 0.0) - jnp.log(1.0 + jnp.exp(-jnp.abs(v)))


def _fox_f_fwd(z, bf, name):
    L = z.shape[0]

    def body(fl_ref, b_ref, f_ref, fq_ref):
        row = lax.broadcasted_iota(jnp.int32, (L, LANES), 0)
        cs = _cumsum_rows(_log_sigmoid(fl_ref[...] + b_ref[...]), True, row)
        f_ref[...] = cs
        expand = (lax.broadcasted_iota(jnp.int32, (LANES, FOX_W), 0)
                  == (lax.broadcasted_iota(jnp.int32, (LANES, FOX_W), 1) >> 6)).astype(F32)
        fq_ref[...] = lax.dot_general(cs, expand, NN, precision=lax.Precision.HIGHEST, preferred_element_type=F32)

    return pl.pallas_call(
        body, name=name, grid=(1,),
        in_specs=[pl.BlockSpec((L, LANES), lambda i: (0, FL_BLK)), pl.BlockSpec((1, LANES), lambda i: (0, 0))],
        out_specs=[pl.BlockSpec((L, LANES), lambda i: (0, 0)), pl.BlockSpec((L, FOX_W), lambda i: (0, 0))],
        out_shape=[_sds((L, LANES)), _sds((L, FOX_W))],
        compiler_params=_cp("arbitrary"),
    )(z, bf)


def _fox_f_bwd(dFk, dfq, z, bf, name):
    L = z.shape[0]

    def body(dfk_ref, dfq_ref, fl_ref, b_ref, dfl_ref, db_ref):
        sel = (lax.broadcasted_iota(jnp.int32, (FOX_W, LANES), 0)
               == 64 * lax.broadcasted_iota(jnp.int32, (FOX_W, LANES), 1)).astype(F32)
        dfq_h = lax.dot_general(dfq_ref[...], sel, NN, precision=lax.Precision.HIGHEST, preferred_element_type=F32)
        row = lax.broadcasted_iota(jnp.int32, (L, LANES), 0)
        cs = _cumsum_rows(dfk_ref[...] + dfq_h, False, row)
        dfl = cs * _sigmoid(-(fl_ref[...] + b_ref[...]))
        dfl_ref[...] = dfl
        db_ref[...] = jnp.sum(dfl, axis=0, keepdims=True)

    return pl.pallas_call(
        body, name=name, grid=(1,),
        in_specs=[pl.BlockSpec((L, LANES), lambda i: (0, 0)), pl.BlockSpec((L, FOX_W), lambda i: (0, 0)),
                  pl.BlockSpec((L, LANES), lambda i: (0, FL_BLK)), pl.BlockSpec((1, LANES), lambda i: (0, 0))],
        out_specs=[pl.BlockSpec((L, LANES), lambda i: (0, 0)), pl.BlockSpec((1, LANES), lambda i: (0, 0))],
        out_shape=[_sds((L, LANES)), _sds((1, LANES))],
        compiler_params=_cp("arbitrary"),
    )(dFk, dfq, z, bf)


def _head_mask(hh):
    lane = lax.broadcasted_iota(jnp.int32, (1, LANES), 1)
    return (lane >> 6) == hh


FOX_T = 512


def _fox_head(x, hh):
    return jnp.where(_head_mask(hh), x, 0.0).astype(BF16)


def _fox_scores(qh, k, fq_ref, fr_ref, hh, causal):
    s = _dot(qh, k, NT) + (fq_ref[:, 64 * hh:64 * hh + 1] - fr_ref[hh:hh + 1, :])
    return s if causal is None else jnp.where(causal, s, NEG)


def _causal(T):
    return lax.broadcasted_iota(jnp.int32, (T, T), 1) <= lax.broadcasted_iota(jnp.int32, (T, T), 0)


def _fox_fwd(z, fq, frow, name):
    L = z.shape[0]
    T = min(FOX_T, L)
    nq = L // T

    def body(qt_ref, kt_ref, q_ref, k_ref, v_ref, fq_ref, fr_ref, o_ref, lse_ref, m_ref, l_ref, acc_ref):
        t = pl.program_id(1)
        qi, ki = qt_ref[t], kt_ref[t]

        @pl.when(ki == 0)
        def _():
            m_ref[...] = jnp.full_like(m_ref, NEG)
            l_ref[...] = jnp.zeros_like(l_ref)
            acc_ref[...] = jnp.zeros_like(acc_ref)

        def step(diagonal):
            q = q_ref[...] * 0.125
            k = k_ref[...].astype(BF16)
            v = v_ref[...].astype(BF16)
            causal = _causal(T) if diagonal else None
            s = jnp.concatenate([_fox_scores(_fox_head(q, hh), k, fq_ref, fr_ref, hh, causal) for hh in range(2)],
                                axis=0)
            m_old = m_ref[...]
            m_new = jnp.maximum(m_old, jnp.max(s, axis=1, keepdims=True))
            alpha = jnp.exp(m_old - m_new)
            p = jnp.exp(s - m_new)
            l_ref[...] = alpha * l_ref[...] + jnp.sum(p, axis=1, keepdims=True)
            m_ref[...] = m_new
            acc_ref[...] = alpha * acc_ref[...] + _dot(p.astype(BF16), v)

        @pl.when(ki < qi)
        def _():
            step(False)

        @pl.when(ki == qi)
        def _():
            step(True)
            h0 = _head_mask(0)
            l = l_ref[...]
            o_h = acc_ref[...] / l
            lse_h = m_ref[...] + jnp.log(l)
            o_ref[...] = jnp.where(h0, o_h[:T], o_h[T:])
            lse_ref[...] = jnp.where(h0, lse_h[:T], lse_h[T:])

    pairs = [(qi, ki) for qi in range(nq) for ki in range(qi + 1)]
    qt = jnp.asarray([p[0] for p in pairs], jnp.int32)
    kt = jnp.asarray([p[1] for p in pairs], jnp.int32)

    def qspec(base):
        return pl.BlockSpec((T, LANES), lambda j, t, qt, kt: (qt[t], base + j))

    def kspec(base):
        return pl.BlockSpec((T, LANES), lambda j, t, qt, kt: (kt[t], base + j))

    return pl.pallas_call(
        body, name=name,
        grid_spec=pltpu.PrefetchScalarGridSpec(
            num_scalar_prefetch=2, grid=(4, len(pairs)),
            in_specs=[qspec(Q_BLK), kspec(K_BLK), kspec(V_BLK), qspec(0),
                      pl.BlockSpec((None, 2, T), lambda j, t, qt, kt: (j, 0, kt[t]))],
            out_specs=[qspec(0), qspec(0)],
            scratch_shapes=[pltpu.VMEM((2 * T, 1), F32), pltpu.VMEM((2 * T, 1), F32),
                            pltpu.VMEM((2 * T, LANES), F32)]),
        out_shape=[_sds((L, FOX_W)), _sds((L, FOX_W))],
        compiler_params=_cp("parallel", "arbitrary"),
    )(qt, kt, z, z, z, fq, frow)


def _fox_bwd(z, fq, frow, o, lse, g_m, name):
    L = z.shape[0]
    T = min(FOX_T, L)
    nq = L // T

    pairs = [(qi, ki) for ki in range(nq) for qi in range(ki, nq)]
    qt = jnp.asarray([p[0] for p in pairs], jnp.int32)
    kt = jnp.asarray([p[1] for p in pairs], jnp.int32)

    def body(qt_ref, kt_ref, q_ref, k_ref, v_ref, fq_ref, fr_ref, o_ref, lse_ref, do_ref,
             dq_ref, dk_ref, dv_ref, dfq_ref, dfk_ref, dk_acc, dv_acc, df_acc):
        t = pl.program_id(1)
        qi, ki = qt_ref[t], kt_ref[t]

        @pl.when(t == 0)
        def _():
            dq_ref[...] = jnp.zeros_like(dq_ref)
            dfq_ref[...] = jnp.zeros_like(dfq_ref)

        @pl.when(qi == ki)
        def _():
            dk_acc[...] = jnp.zeros_like(dk_acc)
            dv_acc[...] = jnp.zeros_like(dv_acc)
            df_acc[...] = jnp.zeros_like(df_acc)

        def step(diagonal):
            q = q_ref[...] * 0.125
            qb = q.astype(BF16)
            k = k_ref[...].astype(BF16)
            v = v_ref[...].astype(BF16)
            do = do_ref[...]
            dob = do.astype(BF16)
            do_o = dob.astype(F32) * o_ref[...]
            causal = _causal(T) if diagonal else None
            dvs, dks, dqs, rss = [], [], [], []
            for hh in range(2):
                s = _fox_scores(_fox_head(q, hh), k, fq_ref, fr_ref, hh, causal)
                p = jnp.exp(s - lse_ref[:, 64 * hh:64 * hh + 1])
                dp = _dot(_fox_head(do, hh), v, NT)
                delta = jnp.sum(jnp.where(_head_mask(hh), do_o, 0.0), axis=1, keepdims=True)
                ds = p * (dp - delta)
                dsb = ds.astype(BF16)
                dvs.append(_dot(p.astype(BF16), dob, TN))
                dks.append(_dot(dsb, qb, TN))
                dqs.append(_dot(dsb, k))
                rss.append(jnp.sum(ds, axis=1, keepdims=True))
                df_acc[hh:hh + 1, :] -= jnp.sum(ds, axis=0, keepdims=True)
            h0 = _head_mask(0)
            dv_acc[...] += jnp.where(h0, dvs[0], dvs[1])
            dk_acc[...] += jnp.where(h0, dks[0], dks[1])
            rows = pl.ds(pl.multiple_of(qi * T, T), T)
            dq_ref[rows, :] += jnp.where(h0, dqs[0], dqs[1])
            dfq_ref[rows, :] += jnp.where(h0, rss[0], rss[1])

        @pl.when(qi > ki)
        def _():
            step(False)

        @pl.when(qi == ki)
        def _():
            step(True)

        @pl.when(qi == nq - 1)
        def _():
            dk_ref[...] = dk_acc[...]
            dv_ref[...] = dv_acc[...]
            dfk_ref[...] = df_acc[...]

        @pl.when(t == len(pairs) - 1)
        def _():
            dq_ref[...] = dq_ref[...] * 0.125

    def qside(base):
        return pl.BlockSpec((T, LANES), lambda j, t, qt, kt: (qt[t], base + j))

    def kside(base):
        return pl.BlockSpec((T, LANES), lambda j, t, qt, kt: (kt[t], base + j))

    pair = pl.BlockSpec((L, LANES), lambda j, t, qt, kt: (0, j))
    frow_spec = pl.BlockSpec((None, 2, T), lambda j, t, qt, kt: (j, 0, kt[t]))
    return pl.pallas_call(
        body, name=name,
        grid_spec=pltpu.PrefetchScalarGridSpec(
            num_scalar_prefetch=2, grid=(4, len(pairs)),
            in_specs=[qside(Q_BLK), kside(K_BLK), kside(V_BLK), qside(0), frow_spec, qside(0), qside(0), qside(0)],
            out_specs=[pair, kside(0), kside(0), pair, frow_spec],
            scratch_shapes=[pltpu.VMEM((T, LANES), F32), pltpu.VMEM((T, LANES), F32), pltpu.VMEM((2, T), F32)]),
        out_shape=[_sds((L, FOX_W)), _sds((L, FOX_W)), _sds((L, FOX_W)), _sds((L, FOX_W)), _sds((4, 2, L))],
        compiler_params=_cp("parallel", "arbitrary"),
    )(qt, kt, z, z, z, fq, frow, o, lse, g_m)


def _shift_rows(v, s, down, row):
    n = v.shape[0]
    if down:
        return jnp.where(row >= s, pltpu.roll(v, s, 0), 0.0)
    return jnp.where(row < n - s, pltpu.roll(v, n - s, 0), 0.0)


def _cumsum_rows(v, down, row):
    s = 1
    while s < v.shape[0]:
        v = v + _shift_rows(v, s, down, row)
        s *= 2
    return v


def _window_sum(v, g, down, row):
    out = jnp.zeros_like(v)
    s = v
    for k in range(4):
        s = s + _shift_rows(s, 1 << k, down, row)
        out = jnp.where(g == k, s, out)
    return out


def _pool_inv_cnt(g, row):
    w = jnp.left_shift(2, g).astype(F32)
    return 1.0 / jnp.minimum(row.astype(F32) + 1.0, w)


def _pool_fwd(z, pool_w, scale, name):
    L = z.shape[0]

    def body(x_ref, w_ref, s_ref, y_ref, p_ref):
        g = pl.program_id(0)
        row = lax.broadcasted_iota(jnp.int32, (L, LANES), 0)
        x = x_ref[...]
        pooled = (_window_sum(x, g, True, row) * _pool_inv_cnt(g, row) - x).astype(BF16)
        p_ref[...] = pooled
        y_ref[...] = (_dot(pooled, w_ref[...].astype(BF16)) * s_ref[...]).astype(BF16)

    col = pl.BlockSpec((L, LANES), lambda g: (0, g))
    return pl.pallas_call(
        body, name=name, grid=(4,),
        in_specs=[col, pl.BlockSpec((None, LANES, LANES), lambda g: (g, 0, 0)), pl.BlockSpec((1, LANES), lambda g: (0, g))],
        out_specs=[col, col],
        out_shape=[_sds((L, 512), BF16), _sds((L, 512), BF16)],
        compiler_params=_cp("parallel"),
    )(z, pool_w, scale)


def _pool_bwd(g_y, wout, pooled, pool_w, scale, name):
    L = g_y.shape[0]

    def body(g_ref, wo_ref, p_ref, w_ref, s_ref, gx_ref, gw_ref, gs_ref):
        g = pl.program_id(0)
        row = lax.broadcasted_iota(jnp.int32, (L, LANES), 0)
        gy = _dot(g_ref[...], wo_ref[...], NT)
        pooled = p_ref[...]
        wb = w_ref[...].astype(BF16)
        lin = _dot(pooled, wb)
        gs_ref[...] = jnp.sum(gy * lin, axis=0, keepdims=True)
        glin = (gy * s_ref[...]).astype(BF16)
        gw_ref[...] = _dot(pooled, glin, TN)
        gp = _dot(glin, wb, NT)
        gx_ref[...] = _window_sum(gp * _pool_inv_cnt(g, row), g, False, row) - gp

    col = pl.BlockSpec((L, LANES), lambda g: (0, g))
    wspec = pl.BlockSpec((None, LANES, LANES), lambda g: (g, 0, 0))
    vec = pl.BlockSpec((1, LANES), lambda g: (0, g))
    return pl.pallas_call(
        body, name=name, grid=(4,),
        in_specs=[pl.BlockSpec((L, D_MODEL), lambda g: (0, 0)), pl.BlockSpec((LANES, D_MODEL), lambda g: (g, 0)),
                  col, wspec, vec],
        out_specs=[col, wspec, vec],
        out_shape=[_sds((L, 512)), _sds((4, LANES, LANES)), _sds((1, 512))],
        compiler_params=_cp("parallel"),
    )(g_y, wout, pooled, pool_w, scale)


SGU_CHUNKS = 4


def _sgu_ln(v, gam, bet):
    gv = _gelu(v)
    mu = jnp.mean(gv, axis=-1, keepdims=True)
    xc = gv - mu
    rs = lax.rsqrt(jnp.mean(xc * xc, axis=-1, keepdims=True) + EPS)
    xh = xc * rs
    return xh, rs, xh * gam + bet


def _tril_ws(w_ref, g):
    r = lax.broadcasted_iota(jnp.int32, (LANES, LANES), 0)
    c = lax.broadcasted_iota(jnp.int32, (LANES, LANES), 1)
    return jnp.where(r >= c, w_ref[g], 0.0).astype(BF16)


def _sgu_fwd(z, ln_g, ln_b, w_s, b_st, name):
    L = z.shape[0]
    rb = min(SGU_CHUNKS * LANES, L)

    def body(u_ref, v_ref, g_ref, b_ref, w_ref, bs_ref, y_ref):
        _, _, vln = _sgu_ln(v_ref[...], g_ref[...], b_ref[...])
        gu = _gelu(u_ref[...])
        vb = vln.astype(BF16)
        for g in range(4):
            ws = _tril_ws(w_ref, g)
            for n in range(rb // LANES):
                rows = slice(n * LANES, (n + 1) * LANES)
                cols = slice(g * LANES, (g + 1) * LANES)
                mixed = _dot(ws, vb[rows, cols]) + bs_ref[:, g:g + 1]
                y_ref[rows, cols] = (gu[rows, cols] * mixed).astype(BF16)

    vm = lambda shape: pl.BlockSpec(shape, lambda i: tuple(0 for _ in shape))
    return pl.pallas_call(
        body, name=name, grid=(L // rb,),
        in_specs=[pl.BlockSpec((rb, 512), lambda i: (i, 1)), pl.BlockSpec((rb, 512), lambda i: (i, 2)),
                  vm((1, 512)), vm((1, 512)), vm((4, LANES, LANES)), vm((LANES, 4))],
        out_specs=pl.BlockSpec((rb, 512), lambda i: (i, 0)),
        out_shape=_sds((L, 512), BF16),
        compiler_params=_cp("parallel"),
    )(z, z, ln_g, ln_b, w_s, b_st)


def _sgu_bwd(g_y, wout, z, ln_g, ln_b, w_s, b_st, name):
    L = z.shape[0]
    rb = min(SGU_CHUNKS * LANES, L)

    def body(gyo_ref, wo_ref, u_ref, v_ref, g_ref, b_ref, w_ref, bs_ref, gu_ref, gv_ref, gw_ref, gbs_ref, gg_ref,
             gb_ref):
        i = pl.program_id(0)

        @pl.when(i == 0)
        def _():
            gw_ref[...] = jnp.zeros_like(gw_ref)
            gbs_ref[...] = jnp.zeros_like(gbs_ref)
            gg_ref[...] = jnp.zeros_like(gg_ref)
            gb_ref[...] = jnp.zeros_like(gb_ref)

        v = v_ref[...]
        u = u_ref[...]
        gy = _dot(gyo_ref[...], wo_ref[...], NT)
        xh, rs, vln = _sgu_ln(v, g_ref[...], b_ref[...])
        gel_u = _gelu(u)
        gmix = gy * gel_u
        vb = vln.astype(BF16)
        gmb = gmix.astype(BF16)
        r = lax.broadcasted_iota(jnp.int32, (LANES, LANES), 0)
        c = lax.broadcasted_iota(jnp.int32, (LANES, LANES), 1)
        gvln_cols = []
        for g in range(4):
            ws = _tril_ws(w_ref, g)
            cols = slice(g * LANES, (g + 1) * LANES)
            gw = jnp.zeros((LANES, LANES), F32)
            gbs = jnp.zeros((LANES, 1), F32)
            parts = []
            for n in range(rb // LANES):
                rows = slice(n * LANES, (n + 1) * LANES)
                mixed = _dot(ws, vb[rows, cols]) + bs_ref[:, g:g + 1]
                gu_ref[rows, cols] = gy[rows, cols] * mixed * _gelu_grad(u[rows, cols])
                parts.append(_dot(ws, gmb[rows, cols], TN))
                gw = gw + _dot(gmb[rows, cols], vb[rows, cols], NT)
                gbs = gbs + jnp.sum(gmix[rows, cols], axis=1, keepdims=True)
            gvln_cols.append(jnp.concatenate(parts, axis=0))
            gw_ref[g] += jnp.where(r >= c, gw, 0.0)
            gbs_ref[:, g:g + 1] += gbs
        gvln = jnp.concatenate(gvln_cols, axis=1)
        gg_ref[...] += jnp.sum(gvln * xh, axis=0, keepdims=True)
        gb_ref[...] += jnp.sum(gvln, axis=0, keepdims=True)
        gxh = gvln * g_ref[...]
        ggv = rs * (gxh - jnp.mean(gxh, axis=-1, keepdims=True) - xh * jnp.mean(gxh * xh, axis=-1, keepdims=True))
        gv_ref[...] = ggv * _gelu_grad(v)

    vm = lambda shape: pl.BlockSpec(shape, lambda i: tuple(0 for _ in shape))
    blk = pl.BlockSpec((rb, 512), lambda i: (i, 0))
    return pl.pallas_call(
        body, name=name, grid=(L // rb,),
        in_specs=[pl.BlockSpec((rb, D_MODEL), lambda i: (i, 0)), pl.BlockSpec((512, D_MODEL), lambda i: (1, 0)),
                  pl.BlockSpec((rb, 512), lambda i: (i, 1)), pl.BlockSpec((rb, 512), lambda i: (i, 2)),
                  vm((1, 512)), vm((1, 512)), vm((4, LANES, LANES)), vm((LANES, 4))],
        out_specs=[blk, blk, vm((4, LANES, LANES)), vm((LANES, 4)), vm((1, 512)), vm((1, 512))],
        out_shape=[_sds((L, 512)), _sds((L, 512)), _sds((4, LANES, LANES)), _sds((LANES, 4)),
                   _sds((1, 512)), _sds((1, 512))],
        compiler_params=_cp("arbitrary"),
    )(g_y, wout, z, z, ln_g, ln_b, w_s, b_st)


def _adamw_math(w, g, m, v):
    nm = ADAM_B1 * m + (1.0 - ADAM_B1) * g
    nv = ADAM_B2 * v + (1.0 - ADAM_B2) * (g * g)
    m_hat = nm / (1.0 - ADAM_B1 ** ADAM_STEP)
    v_hat = nv / (1.0 - ADAM_B2 ** ADAM_STEP)
    delta = -ADAM_LR * (m_hat / (jnp.sqrt(v_hat) + ADAM_EPS) + ADAM_WD * w)
    return delta, nm, nv


def _sum_adamw(parts, w, m, v, name, layer=0, prev=None):
    n_layers, R, C = w.shape
    rb = 128 if R % 128 == 0 else R

    def body(p_ref, w_ref, m_ref, v_ref, *rest):
        g_ref, d_ref, nm_ref, nv_ref = rest[-4:]
        g = p_ref[0].astype(F32)
        for s in range(1, N_DEV):
            g = g + p_ref[s].astype(F32)
        d, nm, nv = _adamw_math(w_ref[...], g, m_ref[...], v_ref[...])
        g_ref[...] = g
        d_ref[...] = d
        nm_ref[...] = nm
        nv_ref[...] = nv

    blk = pl.BlockSpec((None, rb, C), lambda i: (layer, i, 0))
    prev = [] if prev is None else list(prev)
    return pl.pallas_call(
        body, name=name, grid=(R // rb,),
        in_specs=[pl.BlockSpec((N_DEV, rb, C), lambda i: (0, i, 0)), blk, blk, blk] + [ANY] * len(prev),
        out_specs=[blk] * 4, out_shape=[_sds((n_layers, R, C))] * 4,
        input_output_aliases={4 + k: k for k in range(len(prev))},
        compiler_params=_cp("parallel"),
    )(parts, w, m, v, *prev)


def _sum_pieces(parts, name):
    _, R, C = parts.shape

    def body(p_ref, g_ref):
        g = p_ref[0].astype(F32)
        for s in range(1, N_DEV):
            g = g + p_ref[s].astype(F32)
        g_ref[...] = g

    vm = pl.BlockSpec(memory_space=pltpu.VMEM)
    return pl.pallas_call(body, name=name, in_specs=[vm], out_specs=vm, out_shape=_sds((R, C)),
                          compiler_params=pltpu.CompilerParams(vmem_limit_bytes=VMEM_LIMIT))(parts)


def _adamw_many(ws, gs, ms, vs, name):
    n = len(ws)
    vm = pl.BlockSpec(memory_space=pltpu.VMEM)

    def body(*refs):
        w_refs, g_refs, m_refs, v_refs = refs[:n], refs[n:2 * n], refs[2 * n:3 * n], refs[3 * n:4 * n]
        d_refs, nm_refs, nv_refs = refs[4 * n:5 * n], refs[5 * n:6 * n], refs[6 * n:7 * n]
        for i in range(n):
            d, nm, nv = _adamw_math(w_refs[i][...], g_refs[i][...], m_refs[i][...], v_refs[i][...])
            d_refs[i][...] = d
            nm_refs[i][...] = nm
            nv_refs[i][...] = nv

    shapes = [_sds(w.shape) for w in ws]
    outs = pl.pallas_call(
        body, name=name, in_specs=[vm] * (4 * n), out_specs=[vm] * (3 * n), out_shape=shapes * 3,
        compiler_params=pltpu.CompilerParams(vmem_limit_bytes=VMEM_LIMIT),
    )(*ws, *gs, *ms, *vs)
    return list(outs[:n]), list(outs[n:2 * n]), list(outs[2 * n:])


def _mesh_pos():
    return lax.axis_index("x"), lax.axis_index("y"), lax.axis_index("c")


def _dev_index(p):
    return 4 * p[0] + 2 * p[1] + p[2]


HBM = pl.BlockSpec(memory_space=pltpu.HBM)
SEM = pl.BlockSpec(memory_space=pltpu.SEMAPHORE)
EFFECT = pltpu.SideEffectType.DATAFLOW_SIDE_EFFECTING


def _peer_list():
    x, y, c = _mesh_pos()
    peers = [(x ^ dx, y ^ dy, c ^ dc) for dx in range(2) for dy in range(2) for dc in range(2)][1:]
    return (x, y, c), peers


def _split_copy(src_ref, land_ref, send_sems, recv_sems, i, k, peer, slot, exchange):
    return pltpu.make_async_remote_copy(
        src_ref=src_ref.at[_dev_index(peer)] if exchange else src_ref, dst_ref=land_ref.at[slot],
        send_sem=send_sems.at[7 * i + k], recv_sem=recv_sems.at[7 * i + k], device_id=peer, device_id_type=MESH)


def _own_copy(src_ref, land_ref, own_sems, i, slot, exchange):
    return pltpu.make_async_copy(src_ref.at[slot] if exchange else src_ref, land_ref.at[slot], own_sems.at[i])


def _comm_start(groups, name, exchange, dep=None):
    sizes = [len(g) for g in groups]
    n = sum(sizes)
    srcs = [a for g in groups for a in g]
    lands = [lax.empty(a.shape if exchange else (N_DEV,) + a.shape, a.dtype) for a in srcs]

    n_dep = 0 if dep is None else 1

    def body(*refs):
        src_refs, land_refs = refs[:n], refs[n:2 * n]
        sem_refs = refs[2 * n + n_dep:2 * n + n_dep + 3 * len(sizes)]
        token_ref = refs[-1]
        me, peers = _peer_list()
        mi = _dev_index(me)
        i = 0
        for gi, sz in enumerate(sizes):
            for j in range(sz):
                for k, peer in enumerate(peers):
                    _split_copy(src_refs[i], land_refs[i], sem_refs[3 * gi], sem_refs[3 * gi + 1], j, k, peer, mi,
                                exchange).start()
                _own_copy(src_refs[i], land_refs[i], sem_refs[3 * gi + 2], j, mi, exchange).start()
                i += 1
        token_ref[...] = jnp.zeros_like(token_ref)

    sem_shapes = []
    for sz in sizes:
        sem_shapes += [pltpu.SemaphoreType.DMA((7 * sz,)), pltpu.SemaphoreType.DMA((7 * sz,)),
                       pltpu.SemaphoreType.DMA((sz,))]
    thru = [pltpu.HBM(a.shape, a.dtype) for a in srcs + lands]
    n_sem = len(sem_shapes)
    outs = pl.pallas_call(
        body, name=name,
        out_shape=tuple(sem_shapes + thru + [_sds((8, LANES))]),
        in_specs=[HBM] * (2 * n) + [ANY] * n_dep,
        out_specs=tuple([SEM] * n_sem + [HBM] * (2 * n) + [pl.BlockSpec(memory_space=pltpu.VMEM)]),
        input_output_aliases={i: n_sem + i for i in range(2 * n)},
        compiler_params=pltpu.CompilerParams(has_side_effects=EFFECT),
    )(*[pltpu.with_memory_space_constraint(a, pltpu.HBM) for a in srcs + lands], *([] if dep is None else [dep]))
    sems, thru_src, thru_land, token = outs[:n_sem], outs[n_sem:n_sem + n], outs[n_sem + n:n_sem + 2 * n], outs[-1]
    result, off = [], 0
    for gi, sz in enumerate(sizes):
        result.append((*sems[3 * gi:3 * gi + 3], list(thru_src[off:off + sz]), list(thru_land[off:off + sz])))
        off += sz
    return result, token


def _comm_wait(group, after, name, exchange):
    send_sems, recv_sems, own_sems, srcs, lands = group
    n = len(srcs)
    after = list(after) if isinstance(after, (list, tuple)) else [after]

    def body(*refs):
        src_refs, land_refs = refs[:n], refs[n:2 * n]
        ssem, rsem, osem = refs[2 * n:2 * n + 3]
        me, peers = _peer_list()
        for i in range(n):
            for k, peer in enumerate(peers):
                cp = _split_copy(src_refs[i], land_refs[i], ssem, rsem, i, k, peer, _dev_index(peer), exchange)
                cp.wait_send()
                cp.wait_recv()
            _own_copy(src_refs[i], land_refs[i], osem, i, _dev_index(me), exchange).wait()

    outs = pl.pallas_call(
        body, name=name,
        out_shape=tuple(pltpu.HBM(a.shape, a.dtype) for a in srcs + lands),
        in_specs=[HBM] * (2 * n) + [SEM, SEM, SEM] + [ANY] * len(after),
        out_specs=tuple([HBM] * (2 * n)),
        input_output_aliases={i: i for i in range(2 * n)},
        compiler_params=pltpu.CompilerParams(has_side_effects=EFFECT),
    )(*srcs, *lands, send_sems, recv_sems, own_sems, *after)
    return list(outs[n:])


def _tie(a, token):
    return a + token[0, 0].astype(a.dtype)


def _pack(arrs, rows):
    flat = jnp.concatenate([a.reshape(-1).astype(F32) for a in arrs])
    return jnp.pad(flat, (0, rows * LANES - flat.shape[0])).reshape(rows, LANES)


def _unpack(packed, shapes):
    flat = packed.reshape(-1)
    out, off = [], 0
    for s in shapes:
        n = math.prod(s)
        out.append(flat[off:off + n].reshape(s))
        off += n
    return out


def _packed_rows(shapes):
    n = sum(math.prod(s) for s in shapes)
    unit = N_DEV * 8 * LANES
    return -(-n // unit) * unit // LANES


def kernel(x, mix_pre_g, mix_post_g, mlp_pre_g, mlp_post_g, w_in_even, s5_lam_re, s5_lam_im, s5_log_dt, s5_b_re, s5_b_im, s5_c_re, s5_c_im, s5_d, s5_w_glu, fox_b_f, w_out_even, w_in_odd, pool_w, pool_scale, sgu_ln_g, sgu_ln_b, sgu_w_s, sgu_b_s, w_out_odd, mlp_w1, mlp_w2, loss_target, m_mix_pre_g, m_mix_post_g, m_mlp_pre_g, m_mlp_post_g, m_w_in_even, m_s5_lam_re, m_s5_lam_im, m_s5_log_dt, m_s5_b_re, m_s5_b_im, m_s5_c_re, m_s5_c_im, m_s5_d, m_s5_w_glu, m_fox_b_f, m_w_out_even, m_w_in_odd, m_pool_w, m_pool_scale, m_sgu_ln_g, m_sgu_ln_b, m_sgu_w_s, m_sgu_b_s, m_w_out_odd, m_mlp_w1, m_mlp_w2, v_mix_pre_g, v_mix_post_g, v_mlp_pre_g, v_mlp_post_g, v_w_in_even, v_s5_lam_re, v_s5_lam_im, v_s5_log_dt, v_s5_b_re, v_s5_b_im, v_s5_c_re, v_s5_c_im, v_s5_d, v_s5_w_glu, v_fox_b_f, v_w_out_even, v_w_in_odd, v_pool_w, v_pool_scale, v_sgu_ln_g, v_sgu_ln_b, v_sgu_w_s, v_sgu_b_s, v_w_out_odd, v_mlp_w1, v_mlp_w2):
    weights = dict(mix_pre_g=mix_pre_g, mix_post_g=mix_post_g, mlp_pre_g=mlp_pre_g, mlp_post_g=mlp_post_g, w_in_even=w_in_even, s5_lam_re=s5_lam_re, s5_lam_im=s5_lam_im, s5_log_dt=s5_log_dt, s5_b_re=s5_b_re, s5_b_im=s5_b_im, s5_c_re=s5_c_re, s5_c_im=s5_c_im, s5_d=s5_d, s5_w_glu=s5_w_glu, fox_b_f=fox_b_f, w_out_even=w_out_even, w_in_odd=w_in_odd, pool_w=pool_w, pool_scale=pool_scale, sgu_ln_g=sgu_ln_g, sgu_ln_b=sgu_ln_b, sgu_w_s=sgu_w_s, sgu_b_s=sgu_b_s, w_out_odd=w_out_odd, mlp_w1=mlp_w1, mlp_w2=mlp_w2)
    mom_m = dict(mix_pre_g=m_mix_pre_g, mix_post_g=m_mix_post_g, mlp_pre_g=m_mlp_pre_g, mlp_post_g=m_mlp_post_g, w_in_even=m_w_in_even, s5_lam_re=m_s5_lam_re, s5_lam_im=m_s5_lam_im, s5_log_dt=m_s5_log_dt, s5_b_re=m_s5_b_re, s5_b_im=m_s5_b_im, s5_c_re=m_s5_c_re, s5_c_im=m_s5_c_im, s5_d=m_s5_d, s5_w_glu=m_s5_w_glu, fox_b_f=m_fox_b_f, w_out_even=m_w_out_even, w_in_odd=m_w_in_odd, pool_w=m_pool_w, pool_scale=m_pool_scale, sgu_ln_g=m_sgu_ln_g, sgu_ln_b=m_sgu_ln_b, sgu_w_s=m_sgu_w_s, sgu_b_s=m_sgu_b_s, w_out_odd=m_w_out_odd, mlp_w1=m_mlp_w1, mlp_w2=m_mlp_w2)
    mom_v = dict(mix_pre_g=v_mix_pre_g, mix_post_g=v_mix_post_g, mlp_pre_g=v_mlp_pre_g, mlp_post_g=v_mlp_post_g, w_in_even=v_w_in_even, s5_lam_re=v_s5_lam_re, s5_lam_im=v_s5_lam_im, s5_log_dt=v_s5_log_dt, s5_b_re=v_s5_b_re, s5_b_im=v_s5_b_im, s5_c_re=v_s5_c_re, s5_c_im=v_s5_c_im, s5_d=v_s5_d, s5_w_glu=v_s5_w_glu, fox_b_f=v_fox_b_f, w_out_even=v_w_out_even, w_in_odd=v_w_in_odd, pool_w=v_pool_w, pool_scale=v_pool_scale, sgu_ln_g=v_sgu_ln_g, sgu_ln_b=v_sgu_ln_b, sgu_w_s=v_sgu_w_s, sgu_b_s=v_sgu_b_s, w_out_odd=v_w_out_odd, mlp_w1=v_mlp_w1, mlp_w2=v_mlp_w2)
    names = list(weights)
    L = x.shape[1]
    x0 = x[0]
    target = loss_target[0]
    my_index = 4 * lax.axis_index("x") + 2 * lax.axis_index("y") + lax.axis_index("c")

    small_vec = jnp.zeros((8, LANES), F32)
    small_vec = small_vec.at[0, :64].set(pool_scale[0]).at[1, :64].set(sgu_ln_g[0]).at[2, :64].set(sgu_ln_b[0])
    ag_groups, ag_token = _comm_start(
        [[jnp.transpose(w_in_even[0]).astype(BF16), small_vec],
         [s5_w_glu[0].astype(BF16), w_out_even[0].astype(BF16)],
         [mlp_w1[0].astype(BF16), mlp_w2[0].astype(BF16)],
         [jnp.transpose(w_in_odd[0]).astype(BF16), w_out_odd[0].astype(BF16), mlp_w1[1].astype(BF16), mlp_w2[1].astype(BF16)]],
        "ag_start", exchange=False)

    lam_r = jnp.concatenate([s5_lam_re.reshape(1, S5_NS), s5_lam_im.reshape(1, S5_NS)], axis=0)
    ldt_r = jnp.repeat(s5_log_dt.reshape(32), 64).reshape(1, S5_NS)
    lam_c = jnp.transpose(lam_r)
    ldt_c = jnp.transpose(ldt_r)
    b_t = jnp.stack([jnp.tile(s5_b_re.reshape(S5_NS, 16), (1, 8)), jnp.tile(s5_b_im.reshape(S5_NS, 16), (1, 8))])
    c_t = jnp.stack([jnp.tile(s5_c_re.reshape(S5_W, 64), (1, 8)), jnp.tile(s5_c_im.reshape(S5_W, 64), (1, 8))])
    bf_pad = jnp.pad(fox_b_f, ((0, 0), (0, LANES - 8)))
    b_st = jnp.transpose(sgu_b_s[0])

    h0, rx0 = _rms_fwd(x0, _tie(mix_pre_g[0:1], ag_token), "rms0")
    tabs, bset, cset = _s5_prep(lam_r, ldt_r, lam_c, ldt_c, b_t, c_t, "s5_prep")
    ag0 = _comm_wait(ag_groups[0], tabs, "ag_wait0", exchange=False)
    winT_e = jnp.pad(ag0[0].reshape(EVEN_IN, D_MODEL), ((0, EVEN_PAD - EVEN_IN), (0, 0)))
    pool_scale_f = ag0[1][:, 0, :64].reshape(1, 512)
    ln_g_f = ag0[1][:, 1, :64].reshape(1, 512)
    ln_b_f = ag0[1][:, 2, :64].reshape(1, 512)
    z0 = _mm(h0, winT_e, name="win_even", tb=True, bm=512, bn=EVEN_PAD)
    xs, ylin = _s5_scan_fwd(z0, bset, cset, s5_d, tabs, "s5_scan")
    ag1 = _comm_wait(ag_groups[1], ylin, "ag_wait1", exchange=False)
    wglu = ag1[0].reshape(S5_W, S5_W)
    wout_e = ag1[1].reshape(D_MODEL, D_MODEL)
    ya = _s5_glu_fwd(ylin, wglu, "s5_glu")
    fcum, fq = _fox_f_fwd(z0, bf_pad, "fox_f")
    frow = jnp.transpose(fcum[:, :8]).reshape(4, 2, L)
    o_att, lse = _fox_fwd(z0, fq, frow, "fox_fwd")
    mix0 = [ya, o_att]
    x1, ry0, h1, rx1, y0 = _mm(mix0, wout_e, name="wout_even", epi=_epi_post_pre, extra=(x0,),
                               vecs=(mix_post_g[0:1], mlp_pre_g[0:1]), out_dtypes=POST_PRE_DTYPES,
                               out_kinds=POST_PRE_KINDS, bm=FUSED_ROWS)
    ag2 = _comm_wait(ag_groups[2], rx1, "ag_wait2", exchange=False)
    w1 = [ag2[0], None]
    w2 = [ag2[1].reshape(4 * D_MODEL, D_MODEL), None]
    p0, a0 = _mm(h1, w1[0], name="mlp0_w1", b3=True, out_dtypes=(BF16, BF16), epi=_epi_relu2, bm=512, bn=4 * D_MODEL)
    x2, ro0, h2, rx2, o0 = _mm(a0, w2[0], name="mlp0_w2", epi=_epi_post_pre, extra=(x1,),
                               vecs=(mlp_post_g[0:1], mix_pre_g[1:2]), out_dtypes=POST_PRE_DTYPES,
                               out_kinds=POST_PRE_KINDS, bm=FUSED_ROWS, bk=4 * D_MODEL)
    ag3 = _comm_wait(ag_groups[3], rx2, "ag_wait3", exchange=False)
    winT_o = ag3[0].reshape(ODD_IN, D_MODEL)
    wout_o = ag3[1].reshape(D_MODEL, D_MODEL)
    w1[1] = ag3[2]
    w2[1] = ag3[3].reshape(4 * D_MODEL, D_MODEL)
    z1 = _mm(h2, winT_o, name="win_odd", tb=True, bn=ODD_IN)
    yc, pooled = _pool_fwd(z1, pool_w[0], pool_scale_f, "pool_fwd")
    yd = _sgu_fwd(z1, ln_g_f, ln_b_f, sgu_w_s[0], b_st, "sgu_fwd")
    mix1 = [yc, yd]
    x3, ry1, h3, rx3, y1 = _mm(mix1, wout_o, name="wout_odd", epi=_epi_post_pre, extra=(x2,),
                               vecs=(mix_post_g[1:2], mlp_pre_g[1:2]), out_dtypes=POST_PRE_DTYPES,
                               out_kinds=POST_PRE_KINDS, bm=FUSED_ROWS)
    p1, a1 = _mm(h3, w1[1], name="mlp1_w1", b3=True, out_dtypes=(BF16, BF16), epi=_epi_relu2, bm=512, bn=4 * D_MODEL)
    gx4, g_o1, gg_mlp_post1, sq_lanes = _mm(
        a1, w2[1], name="mlp1_w2", epi=_epi_post_loss, extra=(x3, target), vecs=(mlp_post_g[1:2],),
        out_dtypes=(F32, BF16, F32, F32), out_kinds=("full", "full", "vsum", "vsum"), bm=FUSED_ROWS, bk=4 * D_MODEL)
    sq = sq_lanes[:, 0:1]

    g_p1 = _mm(g_o1, w2[1], name="b_mlp1_a", tb=True, out_dtypes=(BF16,), epi=_epi_relu2_bwd, extra=(p1,),
               bm=512, bn=4 * D_MODEL)
    gw2_1 = _mm(a1, g_o1, name="b_mlp1_w2", ta=True, bm=512, bk=L)
    gw1_1 = _mm(h3, g_p1, name="b_mlp1_w1", ta=True, out3=True, bn=512, bk=L)
    (ex1,), tok1 = _comm_start([[gw1_1, gw2_1.reshape(N_DEV, 512, D_MODEL)]], "ex_start1", exchange=True)
    g_x3, gg_mlp_pre1, g_y1, gg_mix_post1 = _mm(
        g_p1, w1[1], name="b_mlp1_h", tb=True, b3=True, epi=_epi_pre_post_bwd, extra=(x3, gx4, y1), cols=(rx3, ry1),
        vecs=(_tie(mlp_pre_g[1:2], tok1), mix_post_g[1:2]), out_dtypes=PRE_POST_BWD_DTYPES,
        out_kinds=PRE_POST_BWD_KINDS, bm=FUSED_ROWS, bk=4 * D_MODEL)
    gwout_o = _mm(mix1, g_y1, name="b_wout_odd_w", ta=True)
    g_xc, g_pool_w, g_pool_scale = _pool_bwd(g_y1, wout_o, pooled, pool_w[0], pool_scale_f, "pool_bwd")
    g_u1, g_v1, g_ws, g_bst, g_ln_g, g_ln_b = _sgu_bwd(g_y1, wout_o, z1, ln_g_f, ln_b_f, sgu_w_s[0], b_st,
                                                       "sgu_bwd")
    g_z1 = [g_xc, g_u1, g_v1]
    gwinT_o = _mm(g_z1, h2, name="b_win_odd_w", ta=True)
    (ex2,), tok2 = _comm_start([[gwout_o.reshape(N_DEV, 128, D_MODEL), gwinT_o.reshape(N_DEV, ODD_IN // N_DEV, D_MODEL)]], "ex_start2", exchange=True)
    g_x2, gg_mix_pre1, g_o0, gg_mlp_post0 = _mm(
        g_z1, winT_o, name="b_win_odd_h", epi=_epi_pre_post_bwd, extra=(x2, g_x3, o0), cols=(rx2, ro0),
        vecs=(_tie(mix_pre_g[1:2], tok2), mlp_post_g[0:1]), out_dtypes=PRE_POST_BWD_DTYPES,
        out_kinds=PRE_POST_BWD_KINDS, bm=FUSED_ROWS)
    g_p0 = _mm(g_o0, w2[0], name="b_mlp0_a", tb=True, out_dtypes=(BF16,), epi=_epi_relu2_bwd, extra=(p0,),
               bm=512, bn=4 * D_MODEL)
    gw2_0 = _mm(a0, g_o0, name="b_mlp0_w2", ta=True, bm=512, bk=L)
    gw1_0 = _mm(h1, g_p0, name="b_mlp0_w1", ta=True, out3=True, bn=512, bk=L)
    (ex3,), tok3 = _comm_start([[gw1_0, gw2_0.reshape(N_DEV, 512, D_MODEL)]], "ex_start3", exchange=True)
    g_x1, gg_mlp_pre0, g_y0, gg_mix_post0 = _mm(
        g_p0, w1[0], name="b_mlp0_h", tb=True, b3=True, epi=_epi_pre_post_bwd, extra=(x1, g_x2, y0), cols=(rx1, ry0),
        vecs=(_tie(mlp_pre_g[0:1], tok3), mix_post_g[0:1]), out_dtypes=PRE_POST_BWD_DTYPES,
        out_kinds=PRE_POST_BWD_KINDS, bm=FUSED_ROWS, bk=4 * D_MODEL)
    g_o_att = _mm(g_y0, wout_e[FOX_W:], name="b_wout_even_m", tb=True)
    gwout_e = _mm(mix0, g_y0, name="b_wout_even_w", ta=True)
    gyl, gud, g_wglu, g_d = _s5_glu_bwd(g_y0, wout_e, ylin, z0, s5_d, wglu, "s5_glu_bwd")
    (ex4,), tok4 = _comm_start([[gwout_e.reshape(N_DEV, 128, D_MODEL), g_wglu.reshape(N_DEV, 64, S5_W)]], "ex_start4", exchange=True)
    g_u0, ga, gb_raw, gc_raw = _s5_scan_bwd(gyl, _tie(cset, tok4), xs, z0, bset, gud, tabs, "s5_scan_bwd")
    g_lam, g_ldt, g_b, g_c = _s5_param_bwd(lam_c, ldt_c, b_t, gb_raw, jnp.transpose(ga), gc_raw, "s5_param_bwd")
    dq, dk, dv, dfq, dfrow = _fox_bwd(z0, fq, frow, o_att, lse, g_o_att, "fox_bwd")
    dFk = jnp.pad(jnp.transpose(dfrow.reshape(8, L)), ((0, 0), (0, LANES - 8)))
    dfl, db_f = _fox_f_bwd(dFk, dfq, z0, bf_pad, "fox_f_bwd")
    g_z0 = [g_u0, dq, dk, dv, dfl]
    grad_x, gg_mix_pre0 = _mm(g_z0, winT_e, name="b_win_even_h", epi=_epi_pre_bwd, extra=(x0, g_x1), cols=(rx0,),
                              vecs=(mix_pre_g[0:1],), out_dtypes=(F32, F32), out_kinds=("full", "vsum"),
                              bm=FUSED_ROWS)

    small_grads = dict(
        mix_pre_g=jnp.concatenate([gg_mix_pre0, gg_mix_pre1]), mix_post_g=jnp.concatenate([gg_mix_post0, gg_mix_post1]),
        mlp_pre_g=jnp.concatenate([gg_mlp_pre0, gg_mlp_pre1]), mlp_post_g=jnp.concatenate([gg_mlp_post0, gg_mlp_post1]),
        s5_lam_re=g_lam[:, 0], s5_lam_im=g_lam[:, 1], s5_log_dt=g_ldt,
        s5_b_re=g_b[0, :, :16], s5_b_im=g_b[1, :, :16], s5_c_re=g_c[0, :, :64], s5_c_im=g_c[1, :, :64],
        s5_d=g_d, fox_b_f=db_f[:, :8], pool_w=g_pool_w, sgu_w_s=g_ws, sgu_b_s=jnp.transpose(g_bst),
        pool_scale=g_pool_scale, sgu_ln_g=g_ln_g, sgu_ln_b=g_ln_b)
    small_names = list(small_grads)
    full_shapes = [(512,) if nm in ("pool_scale", "sgu_ln_g", "sgu_ln_b") else weights[nm].shape for nm in small_names]
    full_shapes.append((1, 1))
    rows = _packed_rows(full_shapes)
    packed = _pack([small_grads[nm] for nm in small_names] + [sq], rows).reshape(N_DEV, rows // N_DEV, LANES)
    (exs,), tok_s = _comm_start([[packed]], "exs_start", exchange=True)
    gwinT_e = _mm(g_z0, h0, name="b_win_even_w", ta=True, bk=512, out_dtypes=(BF16,), dep=tok_s)
    (recv_small,) = _comm_wait(exs, gwinT_e, "exs_wait", exchange=True)
    piece = _sum_pieces(recv_small, "sum_small")
    (ags,), tok_a = _comm_start([[piece]], "ags_start", exchange=False)

    gwinT_e_pieces = gwinT_e[:EVEN_IN].reshape(N_DEV, EVEN_IN // N_DEV, D_MODEL)
    (ex5,), tok5 = _comm_start([[gwinT_e_pieces]], "ex_start5", exchange=True, dep=tok_a)
    r_w1_1, r_w2_1 = _comm_wait(ex1, tok5, "ex_wait1", exchange=True)
    r_wout_o, r_win_o = _comm_wait(ex2, tok5, "ex_wait2", exchange=True)
    r_w1_0, r_w2_0 = _comm_wait(ex3, tok5, "ex_wait3", exchange=True)
    r_wout_e, r_wglu = _comm_wait(ex4, tok5, "ex_wait4", exchange=True)

    res = {}
    for nm, parts in (("mlp_w1", (r_w1_0, r_w1_1)), ("mlp_w2", (r_w2_0, r_w2_1))):
        first = _sum_adamw(parts[0], weights[nm], mom_m[nm], mom_v[nm], "adamw_%s_0" % nm, layer=0)
        res[nm] = tuple(_sum_adamw(parts[1], weights[nm], mom_m[nm], mom_v[nm], "adamw_%s_1" % nm, layer=1, prev=first))
    big_parts = dict(s5_w_glu=r_wglu, w_out_even=r_wout_e, w_out_odd=r_wout_o)
    for nm, parts in big_parts.items():
        res[nm] = tuple(_sum_adamw(parts, weights[nm], mom_m[nm], mom_v[nm], "adamw_" + nm))
    done = [res[nm][1] for nm in ("mlp_w1", "mlp_w2", "s5_w_glu", "w_out_even", "w_out_odd")]

    (small_all,) = _comm_wait(ags, done, "ags_wait", exchange=False)
    small_full = _unpack(small_all.reshape(rows, LANES), full_shapes)
    loss = 0.5 * small_full.pop()[0, 0] / D_MODEL
    small_g = []
    for nm, g in zip(small_names, small_full):
        if nm in ("pool_scale", "sgu_ln_g", "sgu_ln_b"):
            g = lax.dynamic_slice(g, (my_index * 64,), (64,)).reshape(1, 64)
        small_g.append(g)
    sd, sm, sv = _adamw_many([weights[nm] for nm in small_names], small_g, [mom_m[nm] for nm in small_names],
                             [mom_v[nm] for nm in small_names], "adamw_small")
    for nm, g_, d_, m_, v_ in zip(small_names, small_g, sd, sm, sv):
        res[nm] = (g_, d_, m_, v_)
    done.append(sd[0])

    for nm, parts in (("w_in_odd", r_win_o), ("w_in_even", None)):
        if parts is None:
            (parts,) = _comm_wait(ex5, done, "ex_wait5", exchange=True)
        outs = _sum_adamw(parts, jnp.transpose(weights[nm], (0, 2, 1)), jnp.transpose(mom_m[nm], (0, 2, 1)),
                          jnp.transpose(mom_v[nm], (0, 2, 1)), "adamw_" + nm)
        res[nm] = tuple(jnp.transpose(o, (0, 2, 1)) for o in outs)
        done.append(res[nm][1])

    grads = [res[nm][0].reshape(weights[nm].shape) for nm in names]
    deltas = [res[nm][1].reshape(weights[nm].shape) for nm in names]
    new_m = [res[nm][2].reshape(weights[nm].shape) for nm in names]
    new_v = [res[nm][3].reshape(weights[nm].shape) for nm in names]
    return (loss, grad_x[None], *grads, *deltas, *new_m, *new_v)
```

```python
import math

import jax
import jax.numpy as jnp
from jax import lax
from jax.experimental import pallas as pl
from jax.experimental.pallas import tpu as pltpu

F32 = jnp.float32
BF16 = jnp.bfloat16
MESH = pl.DeviceIdType.MESH
ANY = pl.BlockSpec(memory_space=pl.ANY)

N_DEV = 8
D_MODEL = 1024
EPS = 1e-6
NORM_ROWS = 512
FUSED_ROWS = 512
S5_W = 512
S5_NS = 2048
SCAN_GROUPS = 4
SCAN_CHUNK = 1024
FOX_W = 512
EVEN_IN = 2056
EVEN_PAD = 2176
ODD_IN = 1536
LANES = 128
PIECE = 4 * D_MODEL // N_DEV
VMEM_LIMIT = 56 * 1024 * 1024

ADAM_LR = 0.001
ADAM_B1 = 0.9
ADAM_B2 = 0.999
ADAM_EPS = 1e-08
ADAM_WD = 0.01
ADAM_STEP = 10

NT = (((1,), (1,)), ((), ()))
TN = (((0,), (0,)), ((), ()))
NN = (((1,), (0,)), ((), ()))


def _cp(*sem):
    return pltpu.CompilerParams(dimension_semantics=sem, vmem_limit_bytes=VMEM_LIMIT)


def _sds(shape, dtype=F32):
    return jax.ShapeDtypeStruct(tuple(shape), dtype)


def _gelu(x):
    t = jnp.tanh(0.7978845608028654 * (x + 0.044715 * x * x * x))
    return 0.5 * x * (1.0 + t)


def _gelu_grad(x):
    t = jnp.tanh(0.7978845608028654 * (x + 0.044715 * x * x * x))
    du = 0.7978845608028654 * (1.0 + 3.0 * 0.044715 * x * x)
    return 0.5 * (1.0 + t) + 0.5 * x * (1.0 - t * t) * du


def _sigmoid(x):
    return 1.0 / (1.0 + jnp.exp(-x))


def _dot(a, b, dn=NN):
    return lax.dot_general(a, b, dn, preferred_element_type=F32)


def _mm(a, b, *, name, ta=False, tb=False, b3=False, out3=False, out_dtypes=(F32,), epi=None, extra=(),
        cols=(), vecs=(), out_kinds=None, bm=1024, bn=1024, bk=1024, dep=None):
    a_list = list(a) if isinstance(a, (list, tuple)) else [a]
    widths = [p.shape[1] for p in a_list]
    offs = [sum(widths[:i]) for i in range(len(widths))]
    na = len(a_list)
    M = sum(widths) if ta else a_list[0].shape[0]
    K = a_list[0].shape[0] if ta else sum(widths)
    if na > 1:
        assert not b3 and not tb
        bm, bk = (M, bk) if ta else (bm, K)
    pw = b.shape[2] if b3 else PIECE
    if b3:
        N = b.shape[1] if tb else b.shape[0] * pw
        assert (b.shape[0] * pw if tb else b.shape[1]) == K
    else:
        N = b.shape[0] if tb else b.shape[1]
    bm, bn, bk = min(bm, M), min(bn, N), min(bk, K)
    assert M % bm == 0 and N % bn == 0 and K % bk == 0, (name, M, N, K, bm, bn, bk)
    assert not (b3 or out3) or ((bk if tb else bn) % pw == 0 and bn % PIECE == 0)
    nk = K // bk
    n_extra = len(extra) + len(cols) + len(vecs)
    n_out = len(out_dtypes)
    out_kinds = tuple(out_kinds) if out_kinds is not None else ("full",) * n_out
    dn = (((0 if ta else 1,), (1 if tb else 0,)), ((), ()))

    use_acc = nk > 1

    def body(*refs):
        a_refs, b_ref = refs[:na], refs[na]
        a_ref = a_refs[0]
        e_refs = refs[na + 1:na + 1 + n_extra]
        first_out = na + 1 + n_extra + (0 if dep is None else 1)
        o_refs = refs[first_out:first_out + n_out]
        acc_ref = refs[-1] if use_acc else o_refs[0]
        i, k = pl.program_id(0), pl.program_id(2)

        def dot(a_v, b_v):
            return lax.dot_general(a_v.astype(BF16), b_v.astype(BF16), dn, preferred_element_type=F32)

        everything = slice(None)
        if na > 1 and ta:
            terms = [(pl.ds(off, w), everything, r, b_ref) for r, off, w in zip(a_refs, offs, widths)]
        elif na > 1:
            terms = [(everything, everything, r, b_ref.at[pl.ds(off, w), :]) for r, off, w in zip(a_refs, offs, widths)]
        elif not b3:
            terms = [(everything, everything, a_ref, b_ref)]
        elif tb:
            terms = [(everything, everything,
                      a_ref.at[pl.ds(t * pw, pw), :] if ta else a_ref.at[:, pl.ds(t * pw, pw)], b_ref.at[t])
                     for t in range(bk // pw)]
        else:
            terms = [(everything, pl.ds(t * pw, pw), a_ref, b_ref.at[t]) for t in range(bn // pw)]

        def finish(acc):
            outs = (acc,) if epi is None else epi(acc, *[e[...] for e in e_refs])
            for o_ref, o, kind in zip(o_refs, outs, out_kinds):
                if kind == "vsum":
                    @pl.when(i == 0)
                    def _(o_ref=o_ref, o=o):
                        o_ref[...] = o

                    @pl.when(i > 0)
                    def _(o_ref=o_ref, o=o):
                        o_ref[...] += o
                elif out3:
                    for t in range(bn // PIECE):
                        o_ref[t] = o[:, t * PIECE:(t + 1) * PIECE].astype(o_ref.dtype)
                else:
                    o_ref[...] = o.astype(o_ref.dtype)

        if nk == 1:
            bands = {}
            for rows, cols, a_r, b_r in terms:
                key = (getattr(rows, "start", None), getattr(cols, "start", None))
                val = dot(a_r[...], b_r[...])
                bands[key] = val if key not in bands else bands[key] + val
            vals = list(bands.values())
            if len(vals) == 1:
                finish(vals[0])
            else:
                finish(jnp.concatenate(vals, axis=0 if (na > 1 and ta) else 1))
            return

        @pl.when(k == 0)
        def _():
            acc_ref[...] = jnp.zeros_like(acc_ref)

        for rows, cols, a_r, b_r in terms:
            acc_ref[rows, cols] += dot(a_r[...], b_r[...])

        @pl.when(k == nk - 1)
        def _():
            finish(acc_ref[...])

    if na > 1:
        a_specs = [pl.BlockSpec((bk, w), lambda i, j, k: (k, 0)) if ta else pl.BlockSpec((bm, w), lambda i, j, k: (i, 0))
                   for w in widths]
    else:
        a_specs = [pl.BlockSpec((bk, bm), lambda i, j, k: (k, i)) if ta else
                   pl.BlockSpec((bm, bk), lambda i, j, k: (i, k))]
    if b3:
        if tb:
            b_spec = pl.BlockSpec((bk // pw, bn, pw), lambda i, j, k: (k, j, 0))
        else:
            b_spec = pl.BlockSpec((bn // pw, bk, pw), lambda i, j, k: (j, k, 0))
    else:
        b_spec = pl.BlockSpec((bn, bk), lambda i, j, k: (j, k)) if tb else pl.BlockSpec((bk, bn), lambda i, j, k: (k, j))
    e_specs = ([pl.BlockSpec((bm, bn), lambda i, j, k: (i, j)) for _ in extra]
               + [pl.BlockSpec((bm, 1), lambda i, j, k: (i, 0)) for _ in cols]
               + [pl.BlockSpec((1, bn), lambda i, j, k: (0, j)) for _ in vecs])
    if out3:
        o_specs = [pl.BlockSpec((bn // PIECE, bm, PIECE), lambda i, j, k: (j, i, 0)) for _ in out_dtypes]
        o_shapes = [_sds((N // PIECE, M, PIECE), dt) for dt in out_dtypes]
    else:
        spec_of = {"full": pl.BlockSpec((bm, bn), lambda i, j, k: (i, j)),
                   "col": pl.BlockSpec((bm, 1), lambda i, j, k: (i, 0)),
                   "vsum": pl.BlockSpec((1, bn), lambda i, j, k: (0, j))}
        shape_of = {"full": (M, N), "col": (M, 1), "vsum": (1, N)}
        o_specs = [spec_of[kind] for kind in out_kinds]
        o_shapes = [_sds(shape_of[kind], dt) for kind, dt in zip(out_kinds, out_dtypes)]
    assert "col" not in out_kinds or bn == N
    outs = pl.pallas_call(
        body, name=name, grid=(M // bm, N // bn, nk),
        in_specs=a_specs + [b_spec] + e_specs + ([] if dep is None else [ANY]),
        out_specs=o_specs, out_shape=o_shapes,
        scratch_shapes=[pltpu.VMEM((bm, bn), F32)] if use_acc else [],
        compiler_params=_cp("arbitrary" if "vsum" in out_kinds else "parallel", "parallel", "arbitrary"),
    )(*a_list, b, *extra, *cols, *vecs, *([] if dep is None else [dep]))
    return outs[0] if n_out == 1 else outs


def _epi_relu2(acc):
    r = jnp.maximum(acc, 0.0)
    return acc, r * r


def _epi_relu2_bwd(acc, p):
    return (acc * (2.0 * jnp.maximum(p.astype(F32), 0.0)),)


def _row_spec(rb, w=D_MODEL):
    return pl.BlockSpec((rb, w), lambda i: (i, 0))


def _vec_spec(w=D_MODEL):
    return pl.BlockSpec((1, w), lambda i: (0, 0))


def _rstd(v):
    return lax.rsqrt(jnp.mean(v * v, axis=-1, keepdims=True) + EPS)


def _rms_fwd(x, g, name):
    L = x.shape[0]
    rb = min(NORM_ROWS, L)

    def body(x_ref, g_ref, h_ref, r_ref):
        xv = x_ref[...]
        r = _rstd(xv)
        h_ref[...] = (xv * r * g_ref[...]).astype(BF16)
        r_ref[...] = r

    return pl.pallas_call(
        body, name=name, grid=(L // rb,),
        in_specs=[_row_spec(rb), _vec_spec()],
        out_specs=[_row_spec(rb), _row_spec(rb, 1)],
        out_shape=[_sds((L, D_MODEL), BF16), _sds((L, 1))],
        compiler_params=_cp("parallel"),
    )(x, g)


def _rms_bwd_rows(dy, xv, r, g):
    n = xv * r
    dyg = dy * g
    return r * (dyg - n * jnp.mean(dyg * n, axis=-1, keepdims=True)), n


POST_PRE_DTYPES = (F32, F32, BF16, F32, F32)
POST_PRE_KINDS = ("full", "col", "full", "col", "full")
PRE_POST_BWD_DTYPES = (F32, F32, BF16, F32)
PRE_POST_BWD_KINDS = ("full", "vsum", "full", "vsum")


def _epi_post_pre(y, x_in, g_post, g_pre):
    ry = _rstd(y)
    xo = x_in + y * ry * g_post
    rx = _rstd(xo)
    return xo, ry, xo * rx * g_pre, rx, y


def _epi_pre_post_bwd(gh, x, g_out, y_prev, rx, ry_prev, g_pre, g_post_prev):
    gx, n = _rms_bwd_rows(gh, x, rx, g_pre)
    gi = g_out + gx
    gy, ny = _rms_bwd_rows(gi, y_prev, ry_prev, g_post_prev)
    return gi, jnp.sum(gh * n, axis=0, keepdims=True), gy, jnp.sum(gi * ny, axis=0, keepdims=True)


def _epi_pre_bwd(gh, x, g_out, rx, g_pre):
    gx, n = _rms_bwd_rows(gh, x, rx, g_pre)
    return g_out + gx, jnp.sum(gh * n, axis=0, keepdims=True)


def _epi_post_loss(y, x_in, target, g_post):
    ry = _rstd(y)
    diff = x_in + y * ry * g_post - target
    gx = diff * (1.0 / D_MODEL)
    gy, n = _rms_bwd_rows(gx, y, ry, g_post)
    sq = jnp.broadcast_to(jnp.sum(diff * diff, keepdims=True), (1, y.shape[1]))
    return gx, gy, jnp.sum(gx * n, axis=0, keepdims=True), sq


def _cmul(ar, ai, br, bi):
    return ar * br - ai * bi, ar * bi + ai * br


def _zoh_cols(lr, li, ldt):
    dt = jnp.exp(ldt)
    mag = jnp.exp(lr * dt)
    ar = mag * jnp.cos(li * dt)
    ai = mag * jnp.sin(li * dt)
    den = lr * lr + li * li
    nr = ar - 1.0
    qr = (nr * lr + ai * li) / den
    qi = (ai * lr - nr * li) / den
    return dt, ar, ai, qr, qi, den


def _b_mask():
    r = lax.broadcasted_iota(jnp.int32, (S5_NS, LANES), 0)
    c = lax.broadcasted_iota(jnp.int32, (S5_NS, LANES), 1)
    return ((r >> 6) & 7) == (c >> 4)


def _c_mask():
    r = lax.broadcasted_iota(jnp.int32, (S5_W, 512), 0)
    c = lax.broadcasted_iota(jnp.int32, (S5_W, 512), 1)
    return ((r >> 4) & 7) == (c >> 6)


def _s5_prep(lam_r, ldt_r, lam_c, ldt_c, b_t, c_t, name):
    def body(lam_r_ref, ldt_r_ref, lam_c_ref, ldt_c_ref, b_ref, c_ref, tab_ref, bset_ref, cset_ref):
        lr, li = lam_r_ref[0:1, :], lam_r_ref[1:2, :]
        dt = jnp.exp(ldt_r_ref[...])
        mag = jnp.exp(lr * dt)
        p1r, p1i = mag * jnp.cos(li * dt), mag * jnp.sin(li * dt)
        p2r, p2i = _cmul(p1r, p1i, p1r, p1i)
        p3r, p3i = _cmul(p2r, p2i, p1r, p1i)
        p4r, p4i = _cmul(p2r, p2i, p2r, p2i)
        p5r, p5i = _cmul(p4r, p4i, p1r, p1i)
        p6r, p6i = _cmul(p4r, p4i, p2r, p2i)
        p7r, p7i = _cmul(p4r, p4i, p3r, p3i)
        p8r, p8i = _cmul(p4r, p4i, p4r, p4i)
        pw_r = [p1r, p2r, p3r, p4r, p5r, p6r, p7r, p8r]
        pw_i = [p1i, p2i, p3i, p4i, p5i, p6i, p7i, p8i]
        row = lax.broadcasted_iota(jnp.int32, (8, S5_NS), 0)
        zero = jnp.zeros((8, S5_NS), F32)

        def bc(v):
            return jnp.broadcast_to(v, (8, S5_NS))

        for d in range(2):
            sgn = 1.0 if d == 0 else -1.0
            for t, s in enumerate((1, 2, 4)):
                live = (row >= s) if d == 0 else (row <= 7 - s)
                tab_ref[d, 2 * t] = jnp.where(live, bc(pw_r[s - 1]), zero)
                tab_ref[d, 2 * t + 1] = jnp.where(live, bc(sgn * pw_i[s - 1]), zero)
            cr, ci = zero, zero
            for i in range(8):
                e = i if d == 0 else 7 - i
                cr = jnp.where(row == i, bc(pw_r[e]), cr)
                ci = jnp.where(row == i, bc(sgn * pw_i[e]), ci)
            tab_ref[d, 6] = cr
            tab_ref[d, 7] = ci

        _, _, _, qr, qi, _ = _zoh_cols(lam_c_ref[:, 0:1], lam_c_ref[:, 1:2], ldt_c_ref[...])
        bm = _b_mask()
        br, bi = b_ref[0], b_ref[1]
        bset_ref[0] = jnp.where(bm, qr * br - qi * bi, 0.0).astype(BF16)
        bset_ref[1] = jnp.where(bm, qr * bi + qi * br, 0.0).astype(BF16)
        cm = _c_mask()
        cset_ref[0] = jnp.where(cm, c_ref[0], 0.0).astype(BF16)
        cset_ref[1] = jnp.where(cm, c_ref[1], 0.0).astype(BF16)

    vm = pl.BlockSpec(memory_space=pltpu.VMEM)
    return pl.pallas_call(
        body, name=name, in_specs=[vm] * 6, out_specs=[vm] * 3,
        out_shape=[_sds((2, 8, 8, S5_NS)), _sds((2, S5_NS, LANES), BF16), _sds((2, S5_W, 512), BF16)],
        compiler_params=pltpu.CompilerParams(vmem_limit_bytes=VMEM_LIMIT),
    )(lam_r, ldt_r, lam_c, ldt_c, b_t, c_t)


SCAN_W = SCAN_GROUPS * LANES


def _scan_chunk(src_ref, dst_ref, tab_ref, carry_ref, nb, reverse, xs_ref=None, acc_ref=None):
    row = lax.broadcasted_iota(jnp.int32, (8, LANES), 0)

    def step(i, carry):
        b = (nb - 1 - i) if reverse else i
        off = pl.multiple_of(b * 8, 8)
        out = []
        for g in range(SCAN_GROUPS):
            lanes = pl.ds(g * LANES, LANES)
            cr, ci = carry[2 * g], carry[2 * g + 1]
            yr = src_ref[0, pl.ds(off, 8), lanes]
            yi = src_ref[1, pl.ds(off, 8), lanes]
            for t, s in enumerate((1, 2, 4)):
                sh = (8 - s) if reverse else s
                sr = pltpu.roll(yr, sh, 0)
                si = pltpu.roll(yi, sh, 0)
                mr, mi = tab_ref[2 * t, :, lanes], tab_ref[2 * t + 1, :, lanes]
                yr, yi = yr + mr * sr - mi * si, yi + mr * si + mi * sr
            pr, pi = tab_ref[6, :, lanes], tab_ref[7, :, lanes]
            yr, yi = yr + pr * cr - pi * ci, yi + pr * ci + pi * cr
            dst_ref[0, pl.ds(off, 8), lanes] = yr
            dst_ref[1, pl.ds(off, 8), lanes] = yi
            if xs_ref is not None:
                nr = jnp.where(row == 7, cr, pltpu.roll(yr, 7, 0))
                ni = jnp.where(row == 7, ci, pltpu.roll(yi, 7, 0))
                xr = xs_ref[0, pl.ds(off, 8), lanes]
                xi = xs_ref[1, pl.ds(off, 8), lanes]
                acc_ref[0, :, lanes] += xr * nr + xi * ni
                acc_ref[1, :, lanes] += xr * ni - xi * nr
            last = 0 if reverse else 7
            out += [jnp.broadcast_to(yr[last:last + 1, :], (8, LANES)),
                    jnp.broadcast_to(yi[last:last + 1, :], (8, LANES))]
        return tuple(out)

    init = []
    for g in range(SCAN_GROUPS):
        init += [carry_ref[0, :, pl.ds(g * LANES, LANES)], carry_ref[1, :, pl.ds(g * LANES, LANES)]]
    fin = lax.fori_loop(0, nb, step, tuple(init))
    for g in range(SCAN_GROUPS):
        carry_ref[0, :, pl.ds(g * LANES, LANES)] = fin[2 * g]
        carry_ref[1, :, pl.ds(g * LANES, LANES)] = fin[2 * g + 1]


def _s5_scan_fwd(z, bset, cset, dvec, tabs, name):
    L = z.shape[0]
    tl = min(SCAN_CHUNK, L)
    nc = L // tl

    def body(u_ref, b_ref, c_ref, d_ref, tab_ref, x_ref, y_ref, carry_ref):
        @pl.when(pl.program_id(1) == 0)
        def _():
            carry_ref[...] = jnp.zeros_like(carry_ref)

        uf = u_ref[...]
        u = uf.astype(BF16)
        x_ref[0] = _dot(u, b_ref[0], NT)
        x_ref[1] = _dot(u, b_ref[1], NT)
        _scan_chunk(x_ref, x_ref, tab_ref, carry_ref, tl // 8, False)
        y_ref[...] = (_dot(x_ref[0].astype(BF16), c_ref[0], NT) - _dot(x_ref[1].astype(BF16), c_ref[1], NT)
                      + d_ref[...] * uf)

    col = pl.BlockSpec((tl, LANES), lambda j, c: (c, j))
    return pl.pallas_call(
        body, name=name, grid=(S5_NS // SCAN_W, nc),
        in_specs=[col, pl.BlockSpec((2, SCAN_W, LANES), lambda j, c: (0, j, 0)),
                  pl.BlockSpec((2, LANES, SCAN_W), lambda j, c: (0, j, 0)),
                  pl.BlockSpec((1, LANES), lambda j, c: (0, j)),
                  pl.BlockSpec((None, 8, 8, SCAN_W), lambda j, c: (0, 0, 0, j))],
        out_specs=[pl.BlockSpec((2, tl, SCAN_W), lambda j, c: (0, c, j)), col],
        out_shape=[_sds((2, L, S5_NS)), _sds((L, S5_W))],
        scratch_shapes=[pltpu.VMEM((2, 8, SCAN_W), F32)],
        compiler_params=_cp("parallel", "arbitrary"),
    )(z, bset, cset, dvec, tabs)


def _s5_scan_bwd(gyl, cset, xs, z, bset, gud, tabs, name):
    L = z.shape[0]
    tl = min(SCAN_CHUNK, L)
    nc = L // tl

    def body(g_ref, c_ref, xs_ref, u_ref, b_ref, gud_ref, tab_ref, gu_ref, ga_ref, gb_ref, gc_ref,
             gx_ref, carry_ref, acc_ref):
        c = pl.program_id(1)

        @pl.when(c == 0)
        def _():
            carry_ref[...] = jnp.zeros_like(carry_ref)
            acc_ref[...] = jnp.zeros_like(acc_ref)
            gb_ref[...] = jnp.zeros_like(gb_ref)
            gc_ref[...] = jnp.zeros_like(gc_ref)

        gy = g_ref[...].astype(BF16)
        gx_ref[0] = _dot(gy, c_ref[0])
        gx_ref[1] = -_dot(gy, c_ref[1])
        gc_ref[0] += _dot(gy, xs_ref[0].astype(BF16), TN)
        gc_ref[1] -= _dot(gy, xs_ref[1].astype(BF16), TN)
        _scan_chunk(gx_ref, gx_ref, tab_ref, carry_ref, tl // 8, True, xs_ref, acc_ref)
        gr = gx_ref[0].astype(BF16)
        gi = gx_ref[1].astype(BF16)
        gu_ref[...] = gud_ref[...] + _dot(gr, b_ref[0]) + _dot(gi, b_ref[1])
        u = u_ref[...].astype(BF16)
        gb_ref[0] += _dot(gr, u, TN)
        gb_ref[1] += _dot(gi, u, TN)

        @pl.when(c == nc - 1)
        def _():
            ga_ref[0:1, :] = jnp.sum(acc_ref[0], axis=0, keepdims=True)
            ga_ref[1:2, :] = jnp.sum(acc_ref[1], axis=0, keepdims=True)

    rev = lambda j, c: (nc - 1 - c, j)
    col = pl.BlockSpec((tl, LANES), rev)
    return pl.pallas_call(
        body, name=name, grid=(S5_NS // SCAN_W, nc),
        in_specs=[col, pl.BlockSpec((2, LANES, SCAN_W), lambda j, c: (0, j, 0)),
                  pl.BlockSpec((2, tl, SCAN_W), lambda j, c: (0, nc - 1 - c, j)), col,
                  pl.BlockSpec((2, SCAN_W, LANES), lambda j, c: (0, j, 0)), col,
                  pl.BlockSpec((None, 8, 8, SCAN_W), lambda j, c: (1, 0, 0, j))],
        out_specs=[col, pl.BlockSpec((2, SCAN_W), lambda j, c: (0, j)),
                   pl.BlockSpec((2, SCAN_W, LANES), lambda j, c: (0, j, 0)),
                   pl.BlockSpec((2, LANES, SCAN_W), lambda j, c: (0, j, 0))],
        out_shape=[_sds((L, S5_W)), _sds((2, S5_NS)), _sds((2, S5_NS, LANES)), _sds((2, S5_W, 512))],
        scratch_shapes=[pltpu.VMEM((2, tl, SCAN_W), F32), pltpu.VMEM((2, 8, SCAN_W), F32),
                        pltpu.VMEM((2, 8, SCAN_W), F32)],
        compiler_params=_cp("parallel", "arbitrary"),
    )(gyl, cset, xs, z, bset, gud, tabs)


def _s5_glu_fwd(ylin, wglu, name):
    L = ylin.shape[0]
    bl = min(1024, L)

    def body(ylin_ref, w_ref, ya_ref):
        yg = _gelu(ylin_ref[...])
        t = _dot(yg.astype(BF16), w_ref[...])
        ya_ref[...] = (yg * _sigmoid(t)).astype(BF16)

    return pl.pallas_call(
        body, name=name, grid=(L // bl,),
        in_specs=[pl.BlockSpec((bl, S5_W), lambda i: (i, 0)), pl.BlockSpec((S5_W, S5_W), lambda i: (0, 0))],
        out_specs=pl.BlockSpec((bl, S5_W), lambda i: (i, 0)),
        out_shape=_sds((L, S5_W), BF16),
        compiler_params=_cp("parallel"),
    )(ylin, wglu)


def _s5_glu_bwd(g_y, wout, ylin, z, dvec, wglu, name):
    L = z.shape[0]
    bl = min(256, L)

    def body(g_ref, wo_ref, ylin_ref, u_ref, d_ref, w_ref, gyl_ref, gud_ref, gw_ref, gd_ref):
        i = pl.program_id(0)
        ylin = ylin_ref[...]
        yg = _gelu(ylin)
        ygb = yg.astype(BF16)
        sg = _sigmoid(_dot(ygb, w_ref[...]))
        gya = _dot(g_ref[...], wo_ref[...], NT)
        gt = gya * yg * sg * (1.0 - sg)
        gtb = gt.astype(BF16)
        gyg = gya * sg + _dot(gtb, w_ref[...], NT)
        gyl = gyg * _gelu_grad(ylin)
        gyl_ref[...] = gyl
        gud_ref[...] = gyl * d_ref[...]

        @pl.when(i == 0)
        def _():
            gw_ref[...] = jnp.zeros_like(gw_ref)
            gd_ref[...] = jnp.zeros_like(gd_ref)

        gw_ref[...] += _dot(ygb, gtb, TN)
        gd_ref[...] += jnp.sum(gyl * u_ref[...], axis=0, keepdims=True)

    blk = pl.BlockSpec((bl, S5_W), lambda i: (i, 0))
    return pl.pallas_call(
        body, name=name, grid=(L // bl,),
        in_specs=[pl.BlockSpec((bl, D_MODEL), lambda i: (i, 0)), pl.BlockSpec((S5_W, D_MODEL), lambda i: (0, 0)),
                  blk, blk, pl.BlockSpec((1, S5_W), lambda i: (0, 0)), pl.BlockSpec((S5_W, S5_W), lambda i: (0, 0))],
        out_specs=[blk, blk, pl.BlockSpec((S5_W, S5_W), lambda i: (0, 0)), pl.BlockSpec((1, S5_W), lambda i: (0, 0))],
        out_shape=[_sds((L, S5_W)), _sds((L, S5_W)), _sds((S5_W, S5_W)), _sds((1, S5_W))],
        compiler_params=_cp("arbitrary"),
    )(g_y, wout, ylin, z, dvec, wglu)


def _s5_param_bwd(lam_c, ldt_c, b_t, gb, ga_c, gc, name):
    def body(lam_ref, ldt_ref, b_ref, gb_ref, ga_ref, gc_ref, glam_ref, gldt_ref, gbo_ref, gco_ref):
        lr, li = lam_ref[:, 0:1], lam_ref[:, 1:2]
        dt, ar, ai, qr, qi, den = _zoh_cols(lr, li, ldt_ref[...])
        bm = _b_mask()
        gbr = jnp.where(bm, gb_ref[0], 0.0)
        gbi = jnp.where(bm, gb_ref[1], 0.0)
        br, bi = b_ref[0], b_ref[1]
        obr = gbr * qr + gbi * qi
        obi = gbi * qr - gbr * qi
        gqr = jnp.sum(gbr * br + gbi * bi, axis=1, keepdims=True)
        gqi = jnp.sum(gbi * br - gbr * bi, axis=1, keepdims=True)
        for s in (64, 32, 16):
            obr = obr + pltpu.roll(obr, s, 1)
            obi = obi + pltpu.roll(obi, s, 1)
        gbo_ref[0] = obr
        gbo_ref[1] = obi
        gar = ga_ref[:, 0:1] + (gqr * lr - gqi * li) / den
        gai = ga_ref[:, 1:2] + (gqr * li + gqi * lr) / den
        qlr = (qr * lr + qi * li) / den
        qli = (qi * lr - qr * li) / den
        glr = -(gqr * qlr + gqi * qli)
        gli = -(gqi * qlr - gqr * qli)
        glr = glr + dt * (gar * ar + gai * ai)
        gli = gli + dt * (gai * ar - gar * ai)
        wr, wi = _cmul(lr, li, ar, ai)
        gldt = (gar * wr + gai * wi) * dt
        glam_ref[:, 0:1] = glr
        glam_ref[:, 1:2] = gli
        r = lax.broadcasted_iota(jnp.int32, (S5_NS, 32), 0)
        c = lax.broadcasted_iota(jnp.int32, (S5_NS, 32), 1)
        gldt_ref[...] = jnp.sum(jnp.where((r >> 6) == c, gldt, 0.0), axis=0, keepdims=True)
        cm = _c_mask()
        for k in range(2):
            oc = jnp.where(cm, gc_ref[k], 0.0)
            for s in (256, 128, 64):
                oc = oc + pltpu.roll(oc, s, 1)
            gco_ref[k] = oc[:, 0:LANES]

    vm = pl.BlockSpec(memory_space=pltpu.VMEM)
    return pl.pallas_call(
        body, name=name, in_specs=[vm] * 6, out_specs=[vm] * 4,
        out_shape=[_sds((S5_NS, 2)), _sds((1, 32)), _sds((2, S5_NS, LANES)), _sds((2, S5_W, LANES))],
        compiler_params=pltpu.CompilerParams(vmem_limit_bytes=VMEM_LIMIT),
    )(lam_c, ldt_c, b_t, gb, ga_c, gc)


FL_BLK = EVEN_PAD // LANES - 1
Q_BLK, K_BLK, V_BLK = 4, 8, 12
NEG = -1e30


def _log_sigmoid(v):
    return jnp.minimum(v, 0.0) - jnp.log(1.0 + jnp.exp(-jnp.abs(v)))


def _fox_f_fwd(z, bf, name):
    L = z.shape[0]

    def body(fl_ref, b_ref, f_ref, fq_ref):
        row = lax.broadcasted_iota(jnp.int32, (L, LANES), 0)
        cs = _cumsum_rows(_log_sigmoid(fl_ref[...] + b_ref[...]), True, row)
        f_ref[...] = cs
        expand = (lax.broadcasted_iota(jnp.int32, (LANES, FOX_W), 0)
                  == (lax.broadcasted_iota(jnp.int32, (LANES, FOX_W), 1) >> 6)).astype(F32)
        fq_ref[...] = lax.dot_general(cs, expand, NN, precision=lax.Precision.HIGHEST, preferred_element_type=F32)

    return pl.pallas_call(
        body, name=name, grid=(1,),
        in_specs=[pl.BlockSpec((L, LANES), lambda i: (0, FL_BLK)), pl.BlockSpec((1, LANES), lambda i: (0, 0))],
        out_specs=[pl.BlockSpec((L, LANES), lambda i: (0, 0)), pl.BlockSpec((L, FOX_W), lambda i: (0, 0))],
        out_shape=[_sds((L, LANES)), _sds((L, FOX_W))],
        compiler_params=_cp("arbitrary"),
    )(z, bf)


def _fox_f_bwd(dFk, dfq, z, bf, name):
    L = z.shape[0]

    def body(dfk_ref, dfq_ref, fl_ref, b_ref, dfl_ref, db_ref):
        sel = (lax.broadcasted_iota(jnp.int32, (FOX_W, LANES), 0)
               == 64 * lax.broadcasted_iota(jnp.int32, (FOX_W, LANES), 1)).astype(F32)
        dfq_h = lax.dot_general(dfq_ref[...], sel, NN, precision=lax.Precision.HIGHEST, preferred_element_type=F32)
        row = lax.broadcasted_iota(jnp.int32, (L, LANES), 0)
        cs = _cumsum_rows(dfk_ref[...] + dfq_h, False, row)
        dfl = cs * _sigmoid(-(fl_ref[...] + b_ref[...]))
        dfl_ref[...] = dfl
        db_ref[...] = jnp.sum(dfl, axis=0, keepdims=True)

    return pl.pallas_call(
        body, name=name, grid=(1,),
        in_specs=[pl.BlockSpec((L, LANES), lambda i: (0, 0)), pl.BlockSpec((L, FOX_W), lambda i: (0, 0)),
                  pl.BlockSpec((L, LANES), lambda i: (0, FL_BLK)), pl.BlockSpec((1, LANES), lambda i: (0, 0))],
        out_specs=[pl.BlockSpec((L, LANES), lambda i: (0, 0)), pl.BlockSpec((1, LANES), lambda i: (0, 0))],
        out_shape=[_sds((L, LANES)), _sds((1, LANES))],
        compiler_params=_cp("arbitrary"),
    )(dFk, dfq, z, bf)


def _head_mask(hh):
    lane = lax.broadcasted_iota(jnp.int32, (1, LANES), 1)
    return (lane >> 6) == hh


FOX_T = 512


def _fox_head(x, hh):
    return jnp.where(_head_mask(hh), x, 0.0).astype(BF16)


def _fox_scores(qh, k, fq_ref, fr_ref, hh, causal):
    if fq_ref is None:
        s = _dot(qh, k, NT) - fr_ref[hh:hh + 1, :]
    else:
        s = _dot(qh, k, NT) + (fq_ref[:, 64 * hh:64 * hh + 1] - fr_ref[hh:hh + 1, :])
    return s if causal is None else jnp.where(causal, s, NEG)


def _causal(T):
    return lax.broadcasted_iota(jnp.int32, (T, T), 1) <= lax.broadcasted_iota(jnp.int32, (T, T), 0)


def _fox_fwd(z, fq, frow, name):
    L = z.shape[0]
    T = min(FOX_T, L)
    nq = L // T

    def body(qt_ref, kt_ref, q_ref, k_ref, v_ref, fq_ref, fr_ref, o_ref, lse_ref, m_ref, l_ref, acc_ref):
        t = pl.program_id(1)
        qi, ki = qt_ref[t], kt_ref[t]

        @pl.when(ki == 0)
        def _():
            m_ref[...] = jnp.full_like(m_ref, NEG)
            l_ref[...] = jnp.zeros_like(l_ref)
            acc_ref[...] = jnp.zeros_like(acc_ref)

        def step(diagonal):
            q = q_ref[...] * 0.125
            k = k_ref[...].astype(BF16)
            v = v_ref[...].astype(BF16)
            causal = _causal(T) if diagonal else None
            s = jnp.concatenate([_fox_scores(_fox_head(q, hh), k, fq_ref, fr_ref, hh, causal) for hh in range(2)],
                                axis=0)
            m_old = m_ref[...]
            m_new = jnp.maximum(m_old, jnp.max(s, axis=1, keepdims=True))
            alpha = jnp.exp(m_old - m_new)
            p = jnp.exp(s - m_new)
            l_ref[...] = alpha * l_ref[...] + jnp.sum(p, axis=1, keepdims=True)
            m_ref[...] = m_new
            acc_ref[...] = alpha * acc_ref[...] + _dot(p.astype(BF16), v)

        @pl.when(ki < qi)
        def _():
            step(False)

        @pl.when(ki == qi)
        def _():
            step(True)
            h0 = _head_mask(0)
            l = l_ref[...]
            o_h = acc_ref[...] / l
            lse_h = m_ref[...] + jnp.log(l)
            o_ref[...] = jnp.where(h0, o_h[:T], o_h[T:])
            lse_ref[...] = jnp.where(h0, lse_h[:T], lse_h[T:]) - fq_ref[...]

    pairs = [(qi, ki) for qi in range(nq) for ki in range(qi + 1)]
    qt = jnp.asarray([p[0] for p in pairs], jnp.int32)
    kt = jnp.asarray([p[1] for p in pairs], jnp.int32)

    def qspec(base):
        return pl.BlockSpec((T, LANES), lambda j, t, qt, kt: (qt[t], base + j))

    def kspec(base):
        return pl.BlockSpec((T, LANES), lambda j, t, qt, kt: (kt[t], base + j))

    return pl.pallas_call(
        body, name=name,
        grid_spec=pltpu.PrefetchScalarGridSpec(
            num_scalar_prefetch=2, grid=(4, len(pairs)),
            in_specs=[qspec(Q_BLK), kspec(K_BLK), kspec(V_BLK), qspec(0),
                      pl.BlockSpec((None, 2, T), lambda j, t, qt, kt: (j, 0, kt[t]))],
            out_specs=[qspec(0), qspec(0)],
            scratch_shapes=[pltpu.VMEM((2 * T, 1), F32), pltpu.VMEM((2 * T, 1), F32),
                            pltpu.VMEM((2 * T, LANES), F32)]),
        out_shape=[_sds((L, FOX_W)), _sds((L, FOX_W))],
        compiler_params=_cp("parallel", "arbitrary"),
    )(qt, kt, z, z, z, fq, frow)


def _fox_bwd(z, frow, o, lse, g_m, name):
    L = z.shape[0]
    T = min(FOX_T, L)
    nq = L // T

    pairs = [(qi, ki) for ki in range(nq) for qi in range(ki, nq)]
    qt = jnp.asarray([p[0] for p in pairs], jnp.int32)
    kt = jnp.asarray([p[1] for p in pairs], jnp.int32)

    def body(qt_ref, kt_ref, q_ref, k_ref, v_ref, fr_ref, o_ref, lse_ref, do_ref,
             dq_ref, dk_ref, dv_ref, dfq_ref, dfk_ref, dk_acc, dv_acc, df_acc):
        t = pl.program_id(1)
        qi, ki = qt_ref[t], kt_ref[t]

        @pl.when(t == 0)
        def _():
            dq_ref[...] = jnp.zeros_like(dq_ref)
            dfq_ref[...] = jnp.zeros_like(dfq_ref)

        @pl.when(qi == ki)
        def _():
            dk_acc[...] = jnp.zeros_like(dk_acc)
            dv_acc[...] = jnp.zeros_like(dv_acc)
            df_acc[...] = jnp.zeros_like(df_acc)

        def step(diagonal):
            q = q_ref[...] * 0.125
            qb = q.astype(BF16)
            k = k_ref[...].astype(BF16)
            v = v_ref[...].astype(BF16)
            do = do_ref[...]
            dob = do.astype(BF16)
            do_o = dob.astype(F32) * o_ref[...]
            causal = _causal(T) if diagonal else None
            dvs, dks, dqs, rss = [], [], [], []
            for hh in range(2):
                s = _fox_scores(_fox_head(q, hh), k, None, fr_ref, hh, causal)
                p = jnp.exp(s - lse_ref[:, 64 * hh:64 * hh + 1])
                dp = _dot(_fox_head(do, hh), v, NT)
                delta = jnp.sum(jnp.where(_head_mask(hh), do_o, 0.0), axis=1, keepdims=True)
                ds = p * (dp - delta)
                dsb = ds.astype(BF16)
                dvs.append(_dot(p.astype(BF16), dob, TN))
                dks.append(_dot(dsb, qb, TN))
                dqs.append(_dot(dsb, k))
                rss.append(jnp.sum(ds, axis=1, keepdims=True))
                df_acc[hh:hh + 1, :] -= jnp.sum(ds, axis=0, keepdims=True)
            h0 = _head_mask(0)
            dv_acc[...] += jnp.where(h0, dvs[0], dvs[1])
            dk_acc[...] += jnp.where(h0, dks[0], dks[1])
            rows = pl.ds(pl.multiple_of(qi * T, T), T)
            dq_ref[rows, :] += jnp.where(h0, dqs[0], dqs[1])
            dfq_ref[rows, :] += jnp.where(h0, rss[0], rss[1])

        @pl.when(qi > ki)
        def _():
            step(False)

        @pl.when(qi == ki)
        def _():
            step(True)

        @pl.when(qi == nq - 1)
        def _():
            dk_ref[...] = dk_acc[...]
            dv_ref[...] = dv_acc[...]
            dfk_ref[...] = df_acc[...]

        @pl.when(t == len(pairs) - 1)
        def _():
            dq_ref[...] = dq_ref[...] * 0.125

    def qside(base):
        return pl.BlockSpec((T, LANES), lambda j, t, qt, kt: (qt[t], base + j))

    def kside(base):
        return pl.BlockSpec((T, LANES), lambda j, t, qt, kt: (kt[t], base + j))

    pair = pl.BlockSpec((L, LANES), lambda j, t, qt, kt: (0, j))
    frow_spec = pl.BlockSpec((None, 2, T), lambda j, t, qt, kt: (j, 0, kt[t]))
    return pl.pallas_call(
        body, name=name,
        grid_spec=pltpu.PrefetchScalarGridSpec(
            num_scalar_prefetch=2, grid=(4, len(pairs)),
            in_specs=[qside(Q_BLK), kside(K_BLK), kside(V_BLK), frow_spec, qside(0), qside(0), qside(0)],
            out_specs=[pair, kside(0), kside(0), pair, frow_spec],
            scratch_shapes=[pltpu.VMEM((T, LANES), F32), pltpu.VMEM((T, LANES), F32), pltpu.VMEM((2, T), F32)]),
        out_shape=[_sds((L, FOX_W)), _sds((L, FOX_W)), _sds((L, FOX_W)), _sds((L, FOX_W)), _sds((4, 2, L))],
        compiler_params=_cp("parallel", "arbitrary"),
    )(qt, kt, z, z, z, frow, o, lse, g_m)


def _shift_rows(v, s, down, row):
    n = v.shape[0]
    if down:
        return jnp.where(row >= s, pltpu.roll(v, s, 0), 0.0)
    return jnp.where(row < n - s, pltpu.roll(v, n - s, 0), 0.0)


def _cumsum_rows(v, down, row):
    s = 1
    while s < v.shape[0]:
        v = v + _shift_rows(v, s, down, row)
        s *= 2
    return v


def _window_sum(v, g, down, row):
    out = jnp.zeros_like(v)
    s = v
    for k in range(4):
        s = s + _shift_rows(s, 1 << k, down, row)
        out = jnp.where(g == k, s, out)
    return out


def _pool_inv_cnt(g, row):
    w = jnp.left_shift(2, g).astype(F32)
    return 1.0 / jnp.minimum(row.astype(F32) + 1.0, w)


def _pool_fwd(z, pool_w, scale, name):
    L = z.shape[0]

    def body(x_ref, w_ref, s_ref, y_ref, p_ref):
        g = pl.program_id(0)
        row = lax.broadcasted_iota(jnp.int32, (L, LANES), 0)
        x = x_ref[...]
        pooled = (_window_sum(x, g, True, row) * _pool_inv_cnt(g, row) - x).astype(BF16)
        p_ref[...] = pooled
        y_ref[...] = (_dot(pooled, w_ref[...].astype(BF16)) * s_ref[...]).astype(BF16)

    col = pl.BlockSpec((L, LANES), lambda g: (0, g))
    return pl.pallas_call(
        body, name=name, grid=(4,),
        in_specs=[col, pl.BlockSpec((None, LANES, LANES), lambda g: (g, 0, 0)), pl.BlockSpec((1, LANES), lambda g: (0, g))],
        out_specs=[col, col],
        out_shape=[_sds((L, 512), BF16), _sds((L, 512), BF16)],
        compiler_params=_cp("parallel"),
    )(z, pool_w, scale)


def _pool_bwd(g_y, wout, pooled, pool_w, scale, name):
    L = g_y.shape[0]

    def body(g_ref, wo_ref, p_ref, w_ref, s_ref, gx_ref, gw_ref, gs_ref):
        g = pl.program_id(0)
        row = lax.broadcasted_iota(jnp.int32, (L, LANES), 0)
        gy = _dot(g_ref[...], wo_ref[...], NT)
        pooled = p_ref[...]
        wb = w_ref[...].astype(BF16)
        lin = _dot(pooled, wb)
        gs_ref[...] = jnp.sum(gy * lin, axis=0, keepdims=True)
        glin = (gy * s_ref[...]).astype(BF16)
        gw_ref[...] = _dot(pooled, glin, TN)
        gp = _dot(glin, wb, NT)
        gx_ref[...] = _window_sum(gp * _pool_inv_cnt(g, row), g, False, row) - gp

    col = pl.BlockSpec((L, LANES), lambda g: (0, g))
    wspec = pl.BlockSpec((None, LANES, LANES), lambda g: (g, 0, 0))
    vec = pl.BlockSpec((1, LANES), lambda g: (0, g))
    return pl.pallas_call(
        body, name=name, grid=(4,),
        in_specs=[pl.BlockSpec((L, D_MODEL), lambda g: (0, 0)), pl.BlockSpec((LANES, D_MODEL), lambda g: (g, 0)),
                  col, wspec, vec],
        out_specs=[col, wspec, vec],
        out_shape=[_sds((L, 512)), _sds((4, LANES, LANES)), _sds((1, 512))],
        compiler_params=_cp("parallel"),
    )(g_y, wout, pooled, pool_w, scale)


SGU_CHUNKS = 4


def _sgu_ln(v, gam, bet):
    gv = _gelu(v)
    mu = jnp.mean(gv, axis=-1, keepdims=True)
    xc = gv - mu
    rs = lax.rsqrt(jnp.mean(xc * xc, axis=-1, keepdims=True) + EPS)
    xh = xc * rs
    return xh, rs, xh * gam + bet


def _tril_ws(w_ref, g):
    r = lax.broadcasted_iota(jnp.int32, (LANES, LANES), 0)
    c = lax.broadcasted_iota(jnp.int32, (LANES, LANES), 1)
    return jnp.where(r >= c, w_ref[g], 0.0).astype(BF16)


def _sgu_fwd(z, ln_g, ln_b, w_s, b_st, name):
    L = z.shape[0]
    rb = min(SGU_CHUNKS * LANES, L)

    def body(u_ref, v_ref, g_ref, b_ref, w_ref, bs_ref, y_ref):
        _, _, vln = _sgu_ln(v_ref[...], g_ref[...], b_ref[...])
        gu = _gelu(u_ref[...])
        vb = vln.astype(BF16)
        for g in range(4):
            ws = _tril_ws(w_ref, g)
            for n in range(rb // LANES):
                rows = slice(n * LANES, (n + 1) * LANES)
                cols = slice(g * LANES, (g + 1) * LANES)
                mixed = _dot(ws, vb[rows, cols]) + bs_ref[:, g:g + 1]
                y_ref[rows, cols] = (gu[rows, cols] * mixed).astype(BF16)

    vm = lambda shape: pl.BlockSpec(shape, lambda i: tuple(0 for _ in shape))
    return pl.pallas_call(
        body, name=name, grid=(L // rb,),
        in_specs=[pl.BlockSpec((rb, 512), lambda i: (i, 1)), pl.BlockSpec((rb, 512), lambda i: (i, 2)),
                  vm((1, 512)), vm((1, 512)), vm((4, LANES, LANES)), vm((LANES, 4))],
        out_specs=pl.BlockSpec((rb, 512), lambda i: (i, 0)),
        out_shape=_sds((L, 512), BF16),
        compiler_params=_cp("parallel"),
    )(z, z, ln_g, ln_b, w_s, b_st)


def _sgu_bwd(g_y, wout, z, ln_g, ln_b, w_s, b_st, name):
    L = z.shape[0]
    rb = min(SGU_CHUNKS * LANES, L)

    def body(gyo_ref, wo_ref, u_ref, v_ref, g_ref, b_ref, w_ref, bs_ref, gu_ref, gv_ref, gw_ref, gbs_ref, gg_ref,
             gb_ref):
        i = pl.program_id(0)

        @pl.when(i == 0)
        def _():
            gw_ref[...] = jnp.zeros_like(gw_ref)
            gbs_ref[...] = jnp.zeros_like(gbs_ref)
            gg_ref[...] = jnp.zeros_like(gg_ref)
            gb_ref[...] = jnp.zeros_like(gb_ref)

        v = v_ref[...]
        u = u_ref[...]
        gy = _dot(gyo_ref[...], wo_ref[...], NT)
        xh, rs, vln = _sgu_ln(v, g_ref[...], b_ref[...])
        gel_u = _gelu(u)
        gmix = gy * gel_u
        vb = vln.astype(BF16)
        gmb = gmix.astype(BF16)
        r = lax.broadcasted_iota(jnp.int32, (LANES, LANES), 0)
        c = lax.broadcasted_iota(jnp.int32, (LANES, LANES), 1)
        gvln_cols = []
        for g in range(4):
            ws = _tril_ws(w_ref, g)
            cols = slice(g * LANES, (g + 1) * LANES)
            gw = jnp.zeros((LANES, LANES), F32)
            gbs = jnp.zeros((LANES, 1), F32)
            parts = []
            for n in range(rb // LANES):
                rows = slice(n * LANES, (n + 1) * LANES)
                mixed = _dot(ws, vb[rows, cols]) + bs_ref[:, g:g + 1]
                gu_ref[rows, cols] = gy[rows, cols] * mixed * _gelu_grad(u[rows, cols])
                parts.append(_dot(ws, gmb[rows, cols], TN))
                gw = gw + _dot(gmb[rows, cols], vb[rows, cols], NT)
                gbs = gbs + jnp.sum(gmix[rows, cols], axis=1, keepdims=True)
            gvln_cols.append(jnp.concatenate(parts, axis=0))
            gw_ref[g] += jnp.where(r >= c, gw, 0.0)
            gbs_ref[:, g:g + 1] += gbs
        gvln = jnp.concatenate(gvln_cols, axis=1)
        gg_ref[...] += jnp.sum(gvln * xh, axis=0, keepdims=True)
        gb_ref[...] += jnp.sum(gvln, axis=0, keepdims=True)
        gxh = gvln * g_ref[...]
        ggv = rs * (gxh - jnp.mean(gxh, axis=-1, keepdims=True) - xh * jnp.mean(gxh * xh, axis=-1, keepdims=True))
        gv_ref[...] = ggv * _gelu_grad(v)

    vm = lambda shape: pl.BlockSpec(shape, lambda i: tuple(0 for _ in shape))
    blk = pl.BlockSpec((rb, 512), lambda i: (i, 0))
    return pl.pallas_call(
        body, name=name, grid=(L // rb,),
        in_specs=[pl.BlockSpec((rb, D_MODEL), lambda i: (i, 0)), pl.BlockSpec((512, D_MODEL), lambda i: (1, 0)),
                  pl.BlockSpec((rb, 512), lambda i: (i, 1)), pl.BlockSpec((rb, 512), lambda i: (i, 2)),
                  vm((1, 512)), vm((1, 512)), vm((4, LANES, LANES)), vm((LANES, 4))],
        out_specs=[blk, blk, vm((4, LANES, LANES)), vm((LANES, 4)), vm((1, 512)), vm((1, 512))],
        out_shape=[_sds((L, 512)), _sds((L, 512)), _sds((4, LANES, LANES)), _sds((LANES, 4)),
                   _sds((1, 512)), _sds((1, 512))],
        compiler_params=_cp("arbitrary"),
    )(g_y, wout, z, z, ln_g, ln_b, w_s, b_st)


def _adamw_math(w, g, m, v):
    nm = ADAM_B1 * m + (1.0 - ADAM_B1) * g
    nv = ADAM_B2 * v + (1.0 - ADAM_B2) * (g * g)
    m_hat = nm / (1.0 - ADAM_B1 ** ADAM_STEP)
    v_hat = nv / (1.0 - ADAM_B2 ** ADAM_STEP)
    delta = -ADAM_LR * (m_hat / (jnp.sqrt(v_hat) + ADAM_EPS) + ADAM_WD * w)
    return delta, nm, nv


def _sum_adamw(parts, w, m, v, name, layer=0, prev=None):
    n_layers, R, C = w.shape
    rb = 128 if R % 128 == 0 else R

    def body(p_ref, w_ref, m_ref, v_ref, *rest):
        g_ref, d_ref, nm_ref, nv_ref = rest[-4:]
        g = p_ref[0].astype(F32)
        for s in range(1, N_DEV):
            g = g + p_ref[s].astype(F32)
        d, nm, nv = _adamw_math(w_ref[...], g, m_ref[...], v_ref[...])
        g_ref[...] = g
        d_ref[...] = d
        nm_ref[...] = nm
        nv_ref[...] = nv

    blk = pl.BlockSpec((None, rb, C), lambda i: (layer, i, 0))
    prev = [] if prev is None else list(prev)
    return pl.pallas_call(
        body, name=name, grid=(R // rb,),
        in_specs=[pl.BlockSpec((N_DEV, rb, C), lambda i: (0, i, 0)), blk, blk, blk] + [ANY] * len(prev),
        out_specs=[blk] * 4, out_shape=[_sds((n_layers, R, C))] * 4,
        input_output_aliases={4 + k: k for k in range(len(prev))},
        compiler_params=_cp("parallel"),
    )(parts, w, m, v, *prev)


def _sum_pieces(parts, name):
    _, R, C = parts.shape

    def body(p_ref, g_ref):
        g = p_ref[0].astype(F32)
        for s in range(1, N_DEV):
            g = g + p_ref[s].astype(F32)
        g_ref[...] = g

    vm = pl.BlockSpec(memory_space=pltpu.VMEM)
    return pl.pallas_call(body, name=name, in_specs=[vm], out_specs=vm, out_shape=_sds((R, C)),
                          compiler_params=pltpu.CompilerParams(vmem_limit_bytes=VMEM_LIMIT))(parts)


def _adamw_many(ws, gs, ms, vs, name):
    n = len(ws)
    vm = pl.BlockSpec(memory_space=pltpu.VMEM)

    def body(*refs):
        w_refs, g_refs, m_refs, v_refs = refs[:n], refs[n:2 * n], refs[2 * n:3 * n], refs[3 * n:4 * n]
        d_refs, nm_refs, nv_refs = refs[4 * n:5 * n], refs[5 * n:6 * n], refs[6 * n:7 * n]
        for i in range(n):
            d, nm, nv = _adamw_math(w_refs[i][...], g_refs[i][...], m_refs[i][...], v_refs[i][...])
            d_refs[i][...] = d
            nm_refs[i][...] = nm
            nv_refs[i][...] = nv

    shapes = [_sds(w.shape) for w in ws]
    outs = pl.pallas_call(
        body, name=name, in_specs=[vm] * (4 * n), out_specs=[vm] * (3 * n), out_shape=shapes * 3,
        compiler_params=pltpu.CompilerParams(vmem_limit_bytes=VMEM_LIMIT),
    )(*ws, *gs, *ms, *vs)
    return list(outs[:n]), list(outs[n:2 * n]), list(outs[2 * n:])


def _mesh_pos():
    return lax.axis_index("x"), lax.axis_index("y"), lax.axis_index("c")


def _dev_index(p):
    return 4 * p[0] + 2 * p[1] + p[2]


HBM = pl.BlockSpec(memory_space=pltpu.HBM)
SEM = pl.BlockSpec(memory_space=pltpu.SEMAPHORE)
EFFECT = pltpu.SideEffectType.DATAFLOW_SIDE_EFFECTING


def _peer_list():
    x, y, c = _mesh_pos()
    peers = [(x ^ dx, y ^ dy, c ^ dc) for dx in range(2) for dy in range(2) for dc in range(2)][1:]
    return (x, y, c), peers


def _split_copy(src_ref, land_ref, send_sems, recv_sems, i, k, peer, slot, exchange):
    return pltpu.make_async_remote_copy(
        src_ref=src_ref.at[_dev_index(peer)] if exchange else src_ref, dst_ref=land_ref.at[slot],
        send_sem=send_sems.at[7 * i + k], recv_sem=recv_sems.at[7 * i + k], device_id=peer, device_id_type=MESH)


def _own_copy(src_ref, land_ref, own_sems, i, slot, exchange):
    return pltpu.make_async_copy(src_ref.at[slot] if exchange else src_ref, land_ref.at[slot], own_sems.at[i])


def _comm_start(groups, name, exchange, dep=None):
    sizes = [len(g) for g in groups]
    n = sum(sizes)
    srcs = [a for g in groups for a in g]
    lands = [lax.empty(a.shape if exchange else (N_DEV,) + a.shape, a.dtype) for a in srcs]

    n_dep = 0 if dep is None else 1

    def body(*refs):
        src_refs, land_refs = refs[:n], refs[n:2 * n]
        sem_refs = refs[2 * n + n_dep:2 * n + n_dep + 3 * len(sizes)]
        token_ref = refs[-1]
        me, peers = _peer_list()
        mi = _dev_index(me)
        i = 0
        for gi, sz in enumerate(sizes):
            for j in range(sz):
                for k, peer in enumerate(peers):
                    _split_copy(src_refs[i], land_refs[i], sem_refs[3 * gi], sem_refs[3 * gi + 1], j, k, peer, mi,
                                exchange).start()
                _own_copy(src_refs[i], land_refs[i], sem_refs[3 * gi + 2], j, mi, exchange).start()
                i += 1
        token_ref[...] = jnp.zeros_like(token_ref)

    sem_shapes = []
    for sz in sizes:
        sem_shapes += [pltpu.SemaphoreType.DMA((7 * sz,)), pltpu.SemaphoreType.DMA((7 * sz,)),
                       pltpu.SemaphoreType.DMA((sz,))]
    thru = [pltpu.HBM(a.shape, a.dtype) for a in srcs + lands]
    n_sem = len(sem_shapes)
    outs = pl.pallas_call(
        body, name=name,
        out_shape=tuple(sem_shapes + thru + [_sds((8, LANES))]),
        in_specs=[HBM] * (2 * n) + [ANY] * n_dep,
        out_specs=tuple([SEM] * n_sem + [HBM] * (2 * n) + [pl.BlockSpec(memory_space=pltpu.VMEM)]),
        input_output_aliases={i: n_sem + i for i in range(2 * n)},
        compiler_params=pltpu.CompilerParams(has_side_effects=EFFECT),
    )(*[pltpu.with_memory_space_constraint(a, pltpu.HBM) for a in srcs + lands], *([] if dep is None else [dep]))
    sems, thru_src, thru_land, token = outs[:n_sem], outs[n_sem:n_sem + n], outs[n_sem + n:n_sem + 2 * n], outs[-1]
    result, off = [], 0
    for gi, sz in enumerate(sizes):
        result.append((*sems[3 * gi:3 * gi + 3], list(thru_src[off:off + sz]), list(thru_land[off:off + sz])))
        off += sz
    return result, token


def _comm_wait(group, after, name, exchange):
    send_sems, recv_sems, own_sems, srcs, lands = group
    n = len(srcs)
    after = list(after) if isinstance(after, (list, tuple)) else [after]

    def body(*refs):
        src_refs, land_refs = refs[:n], refs[n:2 * n]
        ssem, rsem, osem = refs[2 * n:2 * n + 3]
        me, peers = _peer_list()
        for i in range(n):
            for k, peer in enumerate(peers):
                cp = _split_copy(src_refs[i], land_refs[i], ssem, rsem, i, k, peer, _dev_index(peer), exchange)
                cp.wait_send()
                cp.wait_recv()
            _own_copy(src_refs[i], land_refs[i], osem, i, _dev_index(me), exchange).wait()

    outs = pl.pallas_call(
        body, name=name,
        out_shape=tuple(pltpu.HBM(a.shape, a.dtype) for a in srcs + lands),
        in_specs=[HBM] * (2 * n) + [SEM, SEM, SEM] + [ANY] * len(after),
        out_specs=tuple([HBM] * (2 * n)),
        input_output_aliases={i: i for i in range(2 * n)},
        compiler_params=pltpu.CompilerParams(has_side_effects=EFFECT),
    )(*srcs, *lands, send_sems, recv_sems, own_sems, *after)
    return list(outs[n:])


def _tie(a, token):
    return a + token[0, 0].astype(a.dtype)


def _pack(arrs, rows):
    flat = jnp.concatenate([a.reshape(-1).astype(F32) for a in arrs])
    return jnp.pad(flat, (0, rows * LANES - flat.shape[0])).reshape(rows, LANES)


def _unpack(packed, shapes):
    flat = packed.reshape(-1)
    out, off = [], 0
    for s in shapes:
        n = math.prod(s)
        out.append(flat[off:off + n].reshape(s))
        off += n
    return out


def _packed_rows(shapes):
    n = sum(math.prod(s) for s in shapes)
    unit = N_DEV * 8 * LANES
    return -(-n // unit) * unit // LANES


def kernel(x, mix_pre_g, mix_post_g, mlp_pre_g, mlp_post_g, w_in_even, s5_lam_re, s5_lam_im, s5_log_dt, s5_b_re, s5_b_im, s5_c_re, s5_c_im, s5_d, s5_w_glu, fox_b_f, w_out_even, w_in_odd, pool_w, pool_scale, sgu_ln_g, sgu_ln_b, sgu_w_s, sgu_b_s, w_out_odd, mlp_w1, mlp_w2, loss_target, m_mix_pre_g, m_mix_post_g, m_mlp_pre_g, m_mlp_post_g, m_w_in_even, m_s5_lam_re, m_s5_lam_im, m_s5_log_dt, m_s5_b_re, m_s5_b_im, m_s5_c_re, m_s5_c_im, m_s5_d, m_s5_w_glu, m_fox_b_f, m_w_out_even, m_w_in_odd, m_pool_w, m_pool_scale, m_sgu_ln_g, m_sgu_ln_b, m_sgu_w_s, m_sgu_b_s, m_w_out_odd, m_mlp_w1, m_mlp_w2, v_mix_pre_g, v_mix_post_g, v_mlp_pre_g, v_mlp_post_g, v_w_in_even, v_s5_lam_re, v_s5_lam_im, v_s5_log_dt, v_s5_b_re, v_s5_b_im, v_s5_c_re, v_s5_c_im, v_s5_d, v_s5_w_glu, v_fox_b_f, v_w_out_even, v_w_in_odd, v_pool_w, v_pool_scale, v_sgu_ln_g, v_sgu_ln_b, v_sgu_w_s, v_sgu_b_s, v_w_out_odd, v_mlp_w1, v_mlp_w2):
    weights = dict(mix_pre_g=mix_pre_g, mix_post_g=mix_post_g, mlp_pre_g=mlp_pre_g, mlp_post_g=mlp_post_g, w_in_even=w_in_even, s5_lam_re=s5_lam_re, s5_lam_im=s5_lam_im, s5_log_dt=s5_log_dt, s5_b_re=s5_b_re, s5_b_im=s5_b_im, s5_c_re=s5_c_re, s5_c_im=s5_c_im, s5_d=s5_d, s5_w_glu=s5_w_glu, fox_b_f=fox_b_f, w_out_even=w_out_even, w_in_odd=w_in_odd, pool_w=pool_w, pool_scale=pool_scale, sgu_ln_g=sgu_ln_g, sgu_ln_b=sgu_ln_b, sgu_w_s=sgu_w_s, sgu_b_s=sgu_b_s, w_out_odd=w_out_odd, mlp_w1=mlp_w1, mlp_w2=mlp_w2)
    mom_m = dict(mix_pre_g=m_mix_pre_g, mix_post_g=m_mix_post_g, mlp_pre_g=m_mlp_pre_g, mlp_post_g=m_mlp_post_g, w_in_even=m_w_in_even, s5_lam_re=m_s5_lam_re, s5_lam_im=m_s5_lam_im, s5_log_dt=m_s5_log_dt, s5_b_re=m_s5_b_re, s5_b_im=m_s5_b_im, s5_c_re=m_s5_c_re, s5_c_im=m_s5_c_im, s5_d=m_s5_d, s5_w_glu=m_s5_w_glu, fox_b_f=m_fox_b_f, w_out_even=m_w_out_even, w_in_odd=m_w_in_odd, pool_w=m_pool_w, pool_scale=m_pool_scale, sgu_ln_g=m_sgu_ln_g, sgu_ln_b=m_sgu_ln_b, sgu_w_s=m_sgu_w_s, sgu_b_s=m_sgu_b_s, w_out_odd=m_w_out_odd, mlp_w1=m_mlp_w1, mlp_w2=m_mlp_w2)
    mom_v = dict(mix_pre_g=v_mix_pre_g, mix_post_g=v_mix_post_g, mlp_pre_g=v_mlp_pre_g, mlp_post_g=v_mlp_post_g, w_in_even=v_w_in_even, s5_lam_re=v_s5_lam_re, s5_lam_im=v_s5_lam_im, s5_log_dt=v_s5_log_dt, s5_b_re=v_s5_b_re, s5_b_im=v_s5_b_im, s5_c_re=v_s5_c_re, s5_c_im=v_s5_c_im, s5_d=v_s5_d, s5_w_glu=v_s5_w_glu, fox_b_f=v_fox_b_f, w_out_even=v_w_out_even, w_in_odd=v_w_in_odd, pool_w=v_pool_w, pool_scale=v_pool_scale, sgu_ln_g=v_sgu_ln_g, sgu_ln_b=v_sgu_ln_b, sgu_w_s=v_sgu_w_s, sgu_b_s=v_sgu_b_s, w_out_odd=v_w_out_odd, mlp_w1=v_mlp_w1, mlp_w2=v_mlp_w2)
    names = list(weights)
    L = x.shape[1]
    x0 = x[0]
    target = loss_target[0]
    my_index = 4 * lax.axis_index("x") + 2 * lax.axis_index("y") + lax.axis_index("c")

    small_vec = jnp.zeros((8, LANES), F32)
    small_vec = small_vec.at[0, :64].set(pool_scale[0]).at[1, :64].set(sgu_ln_g[0]).at[2, :64].set(sgu_ln_b[0])
    ag_groups, ag_token = _comm_start(
        [[jnp.transpose(w_in_even[0]).astype(BF16), small_vec],
         [s5_w_glu[0].astype(BF16), w_out_even[0].astype(BF16)],
         [mlp_w1[0].astype(BF16), mlp_w2[0].astype(BF16)],
         [jnp.transpose(w_in_odd[0]).astype(BF16), w_out_odd[0].astype(BF16), mlp_w1[1].astype(BF16), mlp_w2[1].astype(BF16)]],
        "ag_start", exchange=False)

    lam_r = jnp.concatenate([s5_lam_re.reshape(1, S5_NS), s5_lam_im.reshape(1, S5_NS)], axis=0)
    ldt_r = jnp.repeat(s5_log_dt.reshape(32), 64).reshape(1, S5_NS)
    lam_c = jnp.transpose(lam_r)
    ldt_c = jnp.transpose(ldt_r)
    b_t = jnp.stack([jnp.tile(s5_b_re.reshape(S5_NS, 16), (1, 8)), jnp.tile(s5_b_im.reshape(S5_NS, 16), (1, 8))])
    c_t = jnp.stack([jnp.tile(s5_c_re.reshape(S5_W, 64), (1, 8)), jnp.tile(s5_c_im.reshape(S5_W, 64), (1, 8))])
    bf_pad = jnp.pad(fox_b_f, ((0, 0), (0, LANES - 8)))
    b_st = jnp.transpose(sgu_b_s[0])

    h0, rx0 = _rms_fwd(x0, _tie(mix_pre_g[0:1], ag_token), "rms0")
    tabs, bset, cset = _s5_prep(lam_r, ldt_r, lam_c, ldt_c, b_t, c_t, "s5_prep")
    ag0 = _comm_wait(ag_groups[0], tabs, "ag_wait0", exchange=False)
    winT_e = jnp.pad(ag0[0].reshape(EVEN_IN, D_MODEL), ((0, EVEN_PAD - EVEN_IN), (0, 0)))
    pool_scale_f = ag0[1][:, 0, :64].reshape(1, 512)
    ln_g_f = ag0[1][:, 1, :64].reshape(1, 512)
    ln_b_f = ag0[1][:, 2, :64].reshape(1, 512)
    z0 = _mm(h0, winT_e, name="win_even", tb=True, bm=512, bn=EVEN_PAD)
    xs, ylin = _s5_scan_fwd(z0, bset, cset, s5_d, tabs, "s5_scan")
    ag1 = _comm_wait(ag_groups[1], ylin, "ag_wait1", exchange=False)
    wglu = ag1[0].reshape(S5_W, S5_W)
    wout_e = ag1[1].reshape(D_MODEL, D_MODEL)
    ya = _s5_glu_fwd(ylin, wglu, "s5_glu")
    fcum, fq = _fox_f_fwd(z0, bf_pad, "fox_f")
    frow = jnp.transpose(fcum[:, :8]).reshape(4, 2, L)
    o_att, lse = _fox_fwd(z0, fq, frow, "fox_fwd")
    mix0 = [ya, o_att]
    x1, ry0, h1, rx1, y0 = _mm(mix0, wout_e, name="wout_even", epi=_epi_post_pre, extra=(x0,),
                               vecs=(mix_post_g[0:1], mlp_pre_g[0:1]), out_dtypes=POST_PRE_DTYPES,
                               out_kinds=POST_PRE_KINDS, bm=FUSED_ROWS)
    ag2 = _comm_wait(ag_groups[2], rx1, "ag_wait2", exchange=False)
    w1 = [ag2[0], None]
    w2 = [ag2[1].reshape(4 * D_MODEL, D_MODEL), None]
    p0, a0 = _mm(h1, w1[0], name="mlp0_w1", b3=True, out_dtypes=(BF16, BF16), epi=_epi_relu2, bm=512, bn=4 * D_MODEL)
    x2, ro0, h2, rx2, o0 = _mm(a0, w2[0], name="mlp0_w2", epi=_epi_post_pre, extra=(x1,),
                               vecs=(mlp_post_g[0:1], mix_pre_g[1:2]), out_dtypes=POST_PRE_DTYPES,
                               out_kinds=POST_PRE_KINDS, bm=FUSED_ROWS, bk=4 * D_MODEL)
    ag3 = _comm_wait(ag_groups[3], rx2, "ag_wait3", exchange=False)
    winT_o = ag3[0].reshape(ODD_IN, D_MODEL)
    wout_o = ag3[1].reshape(D_MODEL, D_MODEL)
    w1[1] = ag3[2]
    w2[1] = ag3[3].reshape(4 * D_MODEL, D_MODEL)
    z1 = _mm(h2, winT_o, name="win_odd", tb=True, bn=ODD_IN)
    yc, pooled = _pool_fwd(z1, pool_w[0], pool_scale_f, "pool_fwd")
    yd = _sgu_fwd(z1, ln_g_f, ln_b_f, sgu_w_s[0], b_st, "sgu_fwd")
    mix1 = [yc, yd]
    x3, ry1, h3, rx3, y1 = _mm(mix1, wout_o, name="wout_odd", epi=_epi_post_pre, extra=(x2,),
                               vecs=(mix_post_g[1:2], mlp_pre_g[1:2]), out_dtypes=POST_PRE_DTYPES,
                               out_kinds=POST_PRE_KINDS, bm=FUSED_ROWS)
    p1, a1 = _mm(h3, w1[1], name="mlp1_w1", b3=True, out_dtypes=(BF16, BF16), epi=_epi_relu2, bm=512, bn=4 * D_MODEL)
    gx4, g_o1, gg_mlp_post1, sq_lanes = _mm(
        a1, w2[1], name="mlp1_w2", epi=_epi_post_loss, extra=(x3, target), vecs=(mlp_post_g[1:2],),
        out_dtypes=(F32, BF16, F32, F32), out_kinds=("full", "full", "vsum", "vsum"), bm=FUSED_ROWS, bk=4 * D_MODEL)
    sq = sq_lanes[:, 0:1]

    g_p1 = _mm(g_o1, w2[1], name="b_mlp1_a", tb=True, out_dtypes=(BF16,), epi=_epi_relu2_bwd, extra=(p1,),
               bm=512, bn=4 * D_MODEL)
    gw2_1 = _mm(a1, g_o1, name="b_mlp1_w2", ta=True, bm=512, bk=L)
    gw1_1 = _mm(h3, g_p1, name="b_mlp1_w1", ta=True, out3=True, bn=512, bk=L)
    (ex1,), tok1 = _comm_start([[gw1_1, gw2_1.reshape(N_DEV, 512, D_MODEL)]], "ex_start1", exchange=True)
    g_x3, gg_mlp_pre1, g_y1, gg_mix_post1 = _mm(
        g_p1, w1[1], name="b_mlp1_h", tb=True, b3=True, epi=_epi_pre_post_bwd, extra=(x3, gx4, y1), cols=(rx3, ry1),
        vecs=(_tie(mlp_pre_g[1:2], tok1), mix_post_g[1:2]), out_dtypes=PRE_POST_BWD_DTYPES,
        out_kinds=PRE_POST_BWD_KINDS, bm=FUSED_ROWS, bk=4 * D_MODEL)
    gwout_o = _mm(mix1, g_y1, name="b_wout_odd_w", ta=True)
    g_xc, g_pool_w, g_pool_scale = _pool_bwd(g_y1, wout_o, pooled, pool_w[0], pool_scale_f, "pool_bwd")
    g_u1, g_v1, g_ws, g_bst, g_ln_g, g_ln_b = _sgu_bwd(g_y1, wout_o, z1, ln_g_f, ln_b_f, sgu_w_s[0], b_st,
                                                       "sgu_bwd")
    g_z1 = [g_xc, g_u1, g_v1]
    gwinT_o = _mm(g_z1, h2, name="b_win_odd_w", ta=True)
    (ex2,), tok2 = _comm_start([[gwout_o.reshape(N_DEV, 128, D_MODEL), gwinT_o.reshape(N_DEV, ODD_IN // N_DEV, D_MODEL)]], "ex_start2", exchange=True)
    g_x2, gg_mix_pre1, g_o0, gg_mlp_post0 = _mm(
        g_z1, winT_o, name="b_win_odd_h", epi=_epi_pre_post_bwd, extra=(x2, g_x3, o0), cols=(rx2, ro0),
        vecs=(_tie(mix_pre_g[1:2], tok2), mlp_post_g[0:1]), out_dtypes=PRE_POST_BWD_DTYPES,
        out_kinds=PRE_POST_BWD_KINDS, bm=FUSED_ROWS)
    g_p0 = _mm(g_o0, w2[0], name="b_mlp0_a", tb=True, out_dtypes=(BF16,), epi=_epi_relu2_bwd, extra=(p0,),
               bm=512, bn=4 * D_MODEL)
    gw2_0 = _mm(a0, g_o0, name="b_mlp0_w2", ta=True, bm=512, bk=L)
    gw1_0 = _mm(h1, g_p0, name="b_mlp0_w1", ta=True, out3=True, bn=512, bk=L)
    (ex3,), tok3 = _comm_start([[gw1_0, gw2_0.reshape(N_DEV, 512, D_MODEL)]], "ex_start3", exchange=True)
    g_x1, gg_mlp_pre0, g_y0, gg_mix_post0 = _mm(
        g_p0, w1[0], name="b_mlp0_h", tb=True, b3=True, epi=_epi_pre_post_bwd, extra=(x1, g_x2, y0), cols=(rx1, ry0),
        vecs=(_tie(mlp_pre_g[0:1], tok3), mix_post_g[0:1]), out_dtypes=PRE_POST_BWD_DTYPES,
        out_kinds=PRE_POST_BWD_KINDS, bm=FUSED_ROWS, bk=4 * D_MODEL)
    g_o_att = _mm(g_y0, wout_e[FOX_W:], name="b_wout_even_m", tb=True)
    gwout_e = _mm(mix0, g_y0, name="b_wout_even_w", ta=True)
    gyl, gud, g_wglu, g_d = _s5_glu_bwd(g_y0, wout_e, ylin, z0, s5_d, wglu, "s5_glu_bwd")
    (ex4,), tok4 = _comm_start([[gwout_e.reshape(N_DEV, 128, D_MODEL), g_wglu.reshape(N_DEV, 64, S5_W)]], "ex_start4", exchange=True)
    g_u0, ga, gb_raw, gc_raw = _s5_scan_bwd(gyl, _tie(cset, tok4), xs, z0, bset, gud, tabs, "s5_scan_bwd")
    g_lam, g_ldt, g_b, g_c = _s5_param_bwd(lam_c, ldt_c, b_t, gb_raw, jnp.transpose(ga), gc_raw, "s5_param_bwd")
    dq, dk, dv, dfq, dfrow = _fox_bwd(z0, frow, o_att, lse, g_o_att, "fox_bwd")
    dFk = jnp.pad(jnp.transpose(dfrow.reshape(8, L)), ((0, 0), (0, LANES - 8)))
    dfl, db_f = _fox_f_bwd(dFk, dfq, z0, bf_pad, "fox_f_bwd")
    g_z0 = [g_u0, dq, dk, dv, dfl]
    grad_x, gg_mix_pre0 = _mm(g_z0, winT_e, name="b_win_even_h", epi=_epi_pre_bwd, extra=(x0, g_x1), cols=(rx0,),
                              vecs=(mix_pre_g[0:1],), out_dtypes=(F32, F32), out_kinds=("full", "vsum"),
                              bm=FUSED_ROWS)

    small_grads = dict(
        mix_pre_g=jnp.concatenate([gg_mix_pre0, gg_mix_pre1]), mix_post_g=jnp.concatenate([gg_mix_post0, gg_mix_post1]),
        mlp_pre_g=jnp.concatenate([gg_mlp_pre0, gg_mlp_pre1]), mlp_post_g=jnp.concatenate([gg_mlp_post0, gg_mlp_post1]),
        s5_lam_re=g_lam[:, 0], s5_lam_im=g_lam[:, 1], s5_log_dt=g_ldt,
        s5_b_re=g_b[0, :, :16], s5_b_im=g_b[1, :, :16], s5_c_re=g_c[0, :, :64], s5_c_im=g_c[1, :, :64],
        s5_d=g_d, fox_b_f=db_f[:, :8], pool_w=g_pool_w, sgu_w_s=g_ws, sgu_b_s=jnp.transpose(g_bst),
        pool_scale=g_pool_scale, sgu_ln_g=g_ln_g, sgu_ln_b=g_ln_b)
    small_names = list(small_grads)
    full_shapes = [(512,) if nm in ("pool_scale", "sgu_ln_g", "sgu_ln_b") else weights[nm].shape for nm in small_names]
    full_shapes.append((1, 1))
    rows = _packed_rows(full_shapes)
    packed = _pack([small_grads[nm] for nm in small_names] + [sq], rows).reshape(N_DEV, rows // N_DEV, LANES)
    (exs,), tok_s = _comm_start([[packed]], "exs_start", exchange=True)
    gwinT_e = _mm(g_z0, h0, name="b_win_even_w", ta=True, bk=512, out_dtypes=(BF16,), dep=tok_s)
    (recv_small,) = _comm_wait(exs, gwinT_e, "exs_wait", exchange=True)
    piece = _sum_pieces(recv_small, "sum_small")
    (ags,), tok_a = _comm_start([[piece]], "ags_start", exchange=False)

    gwinT_e_pieces = gwinT_e[:EVEN_IN].reshape(N_DEV, EVEN_IN // N_DEV, D_MODEL)
    (ex5,), tok5 = _comm_start([[gwinT_e_pieces]], "ex_start5", exchange=True, dep=tok_a)
    r_w1_1, r_w2_1 = _comm_wait(ex1, tok5, "ex_wait1", exchange=True)
    r_wout_o, r_win_o = _comm_wait(ex2, tok5, "ex_wait2", exchange=True)
    r_w1_0, r_w2_0 = _comm_wait(ex3, tok5, "ex_wait3", exchange=True)
    r_wout_e, r_wglu = _comm_wait(ex4, tok5, "ex_wait4", exchange=True)

    res = {}
    for nm, parts in (("mlp_w1", (r_w1_0, r_w1_1)), ("mlp_w2", (r_w2_0, r_w2_1))):
        first = _sum_adamw(parts[0], weights[nm], mom_m[nm], mom_v[nm], "adamw_%s_0" % nm, layer=0)
        res[nm] = tuple(_sum_adamw(parts[1], weights[nm], mom_m[nm], mom_v[nm], "adamw_%s_1" % nm, layer=1, prev=first))
    big_parts = dict(s5_w_glu=r_wglu, w_out_even=r_wout_e, w_out_odd=r_wout_o)
    for nm, parts in big_parts.items():
        res[nm] = tuple(_sum_adamw(parts, weights[nm], mom_m[nm], mom_v[nm], "adamw_" + nm))
    done = [res[nm][1] for nm in ("mlp_w1", "mlp_w2", "s5_w_glu", "w_out_even", "w_out_odd")]

    (small_all,) = _comm_wait(ags, done, "ags_wait", exchange=False)
    small_full = _unpack(small_all.reshape(rows, LANES), full_shapes)
    loss = 0.5 * small_full.pop()[0, 0] / D_MODEL
    small_g = []
    for nm, g in zip(small_names, small_full):
        if nm in ("pool_scale", "sgu_ln_g", "sgu_ln_b"):
            g = lax.dynamic_slice(g, (my_index * 64,), (64,)).reshape(1, 64)
        small_g.append(g)
    sd, sm, sv = _adamw_many([weights[nm] for nm in small_names], small_g, [mom_m[nm] for nm in small_names],
                             [mom_v[nm] for nm in small_names], "adamw_small")
    for nm, g_, d_, m_, v_ in zip(small_names, small_g, sd, sm, sv):
        res[nm] = (g_, d_, m_, v_)
    done.append(sd[0])

    for nm, parts in (("w_in_odd", r_win_o), ("w_in_even", None)):
        if parts is None:
            (parts,) = _comm_wait(ex5, done, "ex_wait5", exchange=True)
        outs = _sum_adamw(parts, jnp.transpose(weights[nm], (0, 2, 1)), jnp.transpose(mom_m[nm], (0, 2, 1)),
                          jnp.transpose(mom_v[nm], (0, 2, 1)), "adamw_" + nm)
        res[nm] = tuple(jnp.transpose(o, (0, 2, 1)) for o in outs)
        done.append(res[nm][1])

    grads = [res[nm][0].reshape(weights[nm].shape) for nm in names]
    deltas = [res[nm][1].reshape(weights[nm].shape) for nm in names]
    new_m = [res[nm][2].reshape(weights[nm].shape) for nm in names]
    new_v = [res[nm][3].reshape(weights[nm].shape) for nm in names]
    return (loss, grad_x[None], *grads, *deltas, *new_m, *new_v)
```

```python
import math

import jax
import jax.numpy as jnp
from jax import lax
from jax.experimental import pallas as pl
from jax.experimental.pallas import tpu as pltpu

F32 = jnp.float32
BF16 = jnp.bfloat16
MESH = pl.DeviceIdType.MESH
ANY = pl.BlockSpec(memory_space=pl.ANY)

N_DEV = 8
D_MODEL = 1024
EPS = 1e-6
NORM_ROWS = 512
FUSED_ROWS = 512
S5_W = 512
S5_NS = 2048
SCAN_GROUPS = 4
SCAN_CHUNK = 1024
FOX_W = 512
EVEN_IN = 2056
EVEN_PAD = 2176
ODD_IN = 1536
LANES = 128
PIECE = 4 * D_MODEL // N_DEV
VMEM_LIMIT = 56 * 1024 * 1024

ADAM_LR = 0.001
ADAM_B1 = 0.9
ADAM_B2 = 0.999
ADAM_EPS = 1e-08
ADAM_WD = 0.01
ADAM_STEP = 10

NT = (((1,), (1,)), ((), ()))
TN = (((0,), (0,)), ((), ()))
NN = (((1,), (0,)), ((), ()))


def _cp(*sem):
    return pltpu.CompilerParams(dimension_semantics=sem, vmem_limit_bytes=VMEM_LIMIT)


def _sds(shape, dtype=F32):
    return jax.ShapeDtypeStruct(tuple(shape), dtype)


def _gelu(x):
    t = jnp.tanh(0.7978845608028654 * (x + 0.044715 * x * x * x))
    return 0.5 * x * (1.0 + t)


def _gelu_grad(x):
    t = jnp.tanh(0.7978845608028654 * (x + 0.044715 * x * x * x))
    du = 0.7978845608028654 * (1.0 + 3.0 * 0.044715 * x * x)
    return 0.5 * (1.0 + t) + 0.5 * x * (1.0 - t * t) * du


def _sigmoid(x):
    return 1.0 / (1.0 + jnp.exp(-x))


def _dot(a, b, dn=NN):
    return lax.dot_general(a, b, dn, preferred_element_type=F32)


def _mm(a, b, *, name, ta=False, tb=False, b3=False, out3=False, out_dtypes=(F32,), epi=None, extra=(),
        cols=(), vecs=(), out_kinds=None, bm=1024, bn=1024, bk=1024, dep=None):
    a_list = list(a) if isinstance(a, (list, tuple)) else [a]
    widths = [p.shape[1] for p in a_list]
    offs = [sum(widths[:i]) for i in range(len(widths))]
    na = len(a_list)
    M = sum(widths) if ta else a_list[0].shape[0]
    K = a_list[0].shape[0] if ta else sum(widths)
    if na > 1:
        assert not b3 and not tb
        bm, bk = (M, bk) if ta else (bm, K)
    pw = b.shape[2] if b3 else PIECE
    if b3:
        N = b.shape[1] if tb else b.shape[0] * pw
        assert (b.shape[0] * pw if tb else b.shape[1]) == K
    else:
        N = b.shape[0] if tb else b.shape[1]
    bm, bn, bk = min(bm, M), min(bn, N), min(bk, K)
    assert M % bm == 0 and N % bn == 0 and K % bk == 0, (name, M, N, K, bm, bn, bk)
    assert not (b3 or out3) or ((bk if tb else bn) % pw == 0 and bn % PIECE == 0)
    nk = K // bk
    n_extra = len(extra) + len(cols) + len(vecs)
    n_out = len(out_dtypes)
    out_kinds = tuple(out_kinds) if out_kinds is not None else ("full",) * n_out
    dn = (((0 if ta else 1,), (1 if tb else 0,)), ((), ()))

    use_acc = nk > 1

    def body(*refs):
        a_refs, b_ref = refs[:na], refs[na]
        a_ref = a_refs[0]
        e_refs = refs[na + 1:na + 1 + n_extra]
        first_out = na + 1 + n_extra + (0 if dep is None else 1)
        o_refs = refs[first_out:first_out + n_out]
        acc_ref = refs[-1] if use_acc else o_refs[0]
        i, k = pl.program_id(0), pl.program_id(2)

        def dot(a_v, b_v):
            return lax.dot_general(a_v.astype(BF16), b_v.astype(BF16), dn, preferred_element_type=F32)

        everything = slice(None)
        if na > 1 and ta:
            terms = [(pl.ds(off, w), everything, r, b_ref) for r, off, w in zip(a_refs, offs, widths)]
        elif na > 1:
            terms = [(everything, everything, r, b_ref.at[pl.ds(off, w), :]) for r, off, w in zip(a_refs, offs, widths)]
        elif not b3:
            terms = [(everything, everything, a_ref, b_ref)]
        elif tb:
            terms = [(everything, everything,
                      a_ref.at[pl.ds(t * pw, pw), :] if ta else a_ref.at[:, pl.ds(t * pw, pw)], b_ref.at[t])
                     for t in range(bk // pw)]
        else:
            terms = [(everything, pl.ds(t * pw, pw), a_ref, b_ref.at[t]) for t in range(bn // pw)]

        def finish(acc):
            outs = (acc,) if epi is None else epi(acc, *[e[...] for e in e_refs])
            for o_ref, o, kind in zip(o_refs, outs, out_kinds):
                if kind == "vsum":
                    @pl.when(i == 0)
                    def _(o_ref=o_ref, o=o):
                        o_ref[...] = o

                    @pl.when(i > 0)
                    def _(o_ref=o_ref, o=o):
                        o_ref[...] += o
                elif out3:
                    for t in range(bn // PIECE):
                        o_ref[t] = o[:, t * PIECE:(t + 1) * PIECE].astype(o_ref.dtype)
                else:
                    o_ref[...] = o.astype(o_ref.dtype)

        if nk == 1:
            bands = {}
            for rows, cols, a_r, b_r in terms:
                key = (getattr(rows, "start", None), getattr(cols, "start", None))
                val = dot(a_r[...], b_r[...])
                bands[key] = val if key not in bands else bands[key] + val
            vals = list(bands.values())
            if len(vals) == 1:
                finish(vals[0])
            else:
                finish(jnp.concatenate(vals, axis=0 if (na > 1 and ta) else 1))
            return

        @pl.when(k == 0)
        def _():
            acc_ref[...] = jnp.zeros_like(acc_ref)

        for rows, cols, a_r, b_r in terms:
            acc_ref[rows, cols] += dot(a_r[...], b_r[...])

        @pl.when(k == nk - 1)
        def _():
            finish(acc_ref[...])

    if na > 1:
        a_specs = [pl.BlockSpec((bk, w), lambda i, j, k: (k, 0)) if ta else pl.BlockSpec((bm, w), lambda i, j, k: (i, 0))
                   for w in widths]
    else:
        a_specs = [pl.BlockSpec((bk, bm), lambda i, j, k: (k, i)) if ta else
                   pl.BlockSpec((bm, bk), lambda i, j, k: (i, k))]
    if b3:
        if tb:
            b_spec = pl.BlockSpec((bk // pw, bn, pw), lambda i, j, k: (k, j, 0))
        else:
            b_spec = pl.BlockSpec((bn // pw, bk, pw), lambda i, j, k: (j, k, 0))
    else:
        b_spec = pl.BlockSpec((bn, bk), lambda i, j, k: (j, k)) if tb else pl.BlockSpec((bk, bn), lambda i, j, k: (k, j))
    e_specs = ([pl.BlockSpec((bm, bn), lambda i, j, k: (i, j)) for _ in extra]
               + [pl.BlockSpec((bm, 1), lambda i, j, k: (i, 0)) for _ in cols]
               + [pl.BlockSpec((1, bn), lambda i, j, k: (0, j)) for _ in vecs])
    if out3:
        o_specs = [pl.BlockSpec((bn // PIECE, bm, PIECE), lambda i, j, k: (j, i, 0)) for _ in out_dtypes]
        o_shapes = [_sds((N // PIECE, M, PIECE), dt) for dt in out_dtypes]
    else:
        spec_of = {"full": pl.BlockSpec((bm, bn), lambda i, j, k: (i, j)),
                   "col": pl.BlockSpec((bm, 1), lambda i, j, k: (i, 0)),
                   "vsum": pl.BlockSpec((1, bn), lambda i, j, k: (0, j))}
        shape_of = {"full": (M, N), "col": (M, 1), "vsum": (1, N)}
        o_specs = [spec_of[kind] for kind in out_kinds]
        o_shapes = [_sds(shape_of[kind], dt) for kind, dt in zip(out_kinds, out_dtypes)]
    assert "col" not in out_kinds or bn == N
    outs = pl.pallas_call(
        body, name=name, grid=(M // bm, N // bn, nk),
        in_specs=a_specs + [b_spec] + e_specs + ([] if dep is None else [ANY]),
        out_specs=o_specs, out_shape=o_shapes,
        scratch_shapes=[pltpu.VMEM((bm, bn), F32)] if use_acc else [],
        compiler_params=_cp("arbitrary" if "vsum" in out_kinds else "parallel", "parallel", "arbitrary"),
    )(*a_list, b, *extra, *cols, *vecs, *([] if dep is None else [dep]))
    return outs[0] if n_out == 1 else outs


def _epi_relu2(acc):
    r = jnp.maximum(acc, 0.0)
    return acc, r * r


def _epi_relu2_bwd(acc, p):
    return (acc * (2.0 * jnp.maximum(p.astype(F32), 0.0)),)


def _row_spec(rb, w=D_MODEL):
    return pl.BlockSpec((rb, w), lambda i: (i, 0))


def _vec_spec(w=D_MODEL):
    return pl.BlockSpec((1, w), lambda i: (0, 0))


def _rstd(v):
    return lax.rsqrt(jnp.mean(v * v, axis=-1, keepdims=True) + EPS)


def _rms_fwd(x, g, name):
    L = x.shape[0]
    rb = min(NORM_ROWS, L)

    def body(x_ref, g_ref, h_ref, r_ref):
        xv = x_ref[...]
        r = _rstd(xv)
        h_ref[...] = (xv * r * g_ref[...]).astype(BF16)
        r_ref[...] = r

    return pl.pallas_call(
        body, name=name, grid=(L // rb,),
        in_specs=[_row_spec(rb), _vec_spec()],
        out_specs=[_row_spec(rb), _row_spec(rb, 1)],
        out_shape=[_sds((L, D_MODEL), BF16), _sds((L, 1))],
        compiler_params=_cp("parallel"),
    )(x, g)


def _rms_bwd_rows(dy, xv, r, g):
    n = xv * r
    dyg = dy * g
    return r * (dyg - n * jnp.mean(dyg * n, axis=-1, keepdims=True)), n


POST_PRE_DTYPES = (F32, F32, BF16, F32, F32)
POST_PRE_KINDS = ("full", "col", "full", "col", "full")
PRE_POST_BWD_DTYPES = (F32, F32, BF16, F32)
PRE_POST_BWD_KINDS = ("full", "vsum", "full", "vsum")


def _epi_post_pre(y, x_in, g_post, g_pre):
    ry = _rstd(y)
    xo = x_in + y * ry * g_post
    rx = _rstd(xo)
    return xo, ry, xo * rx * g_pre, rx, y


def _epi_pre_post_bwd(gh, x, g_out, y_prev, rx, ry_prev, g_pre, g_post_prev):
    gx, n = _rms_bwd_rows(gh, x, rx, g_pre)
    gi = g_out + gx
    gy, ny = _rms_bwd_rows(gi, y_prev, ry_prev, g_post_prev)
    return gi, jnp.sum(gh * n, axis=0, keepdims=True), gy, jnp.sum(gi * ny, axis=0, keepdims=True)


def _epi_pre_bwd(gh, x, g_out, rx, g_pre):
    gx, n = _rms_bwd_rows(gh, x, rx, g_pre)
    return g_out + gx, jnp.sum(gh * n, axis=0, keepdims=True)


def _epi_post_loss(y, x_in, target, g_post):
    ry = _rstd(y)
    diff = x_in + y * ry * g_post - target
    gx = diff * (1.0 / D_MODEL)
    gy, n = _rms_bwd_rows(gx, y, ry, g_post)
    sq = jnp.broadcast_to(jnp.sum(diff * diff, keepdims=True), (1, y.shape[1]))
    return gx, gy, jnp.sum(gx * n, axis=0, keepdims=True), sq


def _cmul(ar, ai, br, bi):
    return ar * br - ai * bi, ar * bi + ai * br


def _zoh_cols(lr, li, ldt):
    dt = jnp.exp(ldt)
    mag = jnp.exp(lr * dt)
    ar = mag * jnp.cos(li * dt)
    ai = mag * jnp.sin(li * dt)
    den = lr * lr + li * li
    nr = ar - 1.0
    qr = (nr * lr + ai * li) / den
    qi = (ai * lr - nr * li) / den
    return dt, ar, ai, qr, qi, den


def _b_mask():
    r = lax.broadcasted_iota(jnp.int32, (S5_NS, LANES), 0)
    c = lax.broadcasted_iota(jnp.int32, (S5_NS, LANES), 1)
    return ((r >> 6) & 7) == (c >> 4)


def _c_mask():
    r = lax.broadcasted_iota(jnp.int32, (S5_W, 512), 0)
    c = lax.broadcasted_iota(jnp.int32, (S5_W, 512), 1)
    return ((r >> 4) & 7) == (c >> 6)


def _s5_prep(lam_r, ldt_r, lam_c, ldt_c, b_t, c_t, name):
    def body(lam_r_ref, ldt_r_ref, lam_c_ref, ldt_c_ref, b_ref, c_ref, tab_ref, bset_ref, cset_ref):
        lr, li = lam_r_ref[0:1, :], lam_r_ref[1:2, :]
        dt = jnp.exp(ldt_r_ref[...])
        mag = jnp.exp(lr * dt)
        p1r, p1i = mag * jnp.cos(li * dt), mag * jnp.sin(li * dt)
        p2r, p2i = _cmul(p1r, p1i, p1r, p1i)
        p3r, p3i = _cmul(p2r, p2i, p1r, p1i)
        p4r, p4i = _cmul(p2r, p2i, p2r, p2i)
        p5r, p5i = _cmul(p4r, p4i, p1r, p1i)
        p6r, p6i = _cmul(p4r, p4i, p2r, p2i)
        p7r, p7i = _cmul(p4r, p4i, p3r, p3i)
        p8r, p8i = _cmul(p4r, p4i, p4r, p4i)
        pw_r = [p1r, p2r, p3r, p4r, p5r, p6r, p7r, p8r]
        pw_i = [p1i, p2i, p3i, p4i, p5i, p6i, p7i, p8i]
        row = lax.broadcasted_iota(jnp.int32, (8, S5_NS), 0)
        zero = jnp.zeros((8, S5_NS), F32)

        def bc(v):
            return jnp.broadcast_to(v, (8, S5_NS))

        for d in range(2):
            sgn = 1.0 if d == 0 else -1.0
            for t, s in enumerate((1, 2, 4)):
                live = (row >= s) if d == 0 else (row <= 7 - s)
                tab_ref[d, 2 * t] = jnp.where(live, bc(pw_r[s - 1]), zero)
                tab_ref[d, 2 * t + 1] = jnp.where(live, bc(sgn * pw_i[s - 1]), zero)
            cr, ci = zero, zero
            for i in range(8):
                e = i if d == 0 else 7 - i
                cr = jnp.where(row == i, bc(pw_r[e]), cr)
                ci = jnp.where(row == i, bc(sgn * pw_i[e]), ci)
            tab_ref[d, 6] = cr
            tab_ref[d, 7] = ci

        _, _, _, qr, qi, _ = _zoh_cols(lam_c_ref[:, 0:1], lam_c_ref[:, 1:2], ldt_c_ref[...])
        bm = _b_mask()
        br, bi = b_ref[0], b_ref[1]
        bset_ref[0] = jnp.where(bm, qr * br - qi * bi, 0.0).astype(BF16)
        bset_ref[1] = jnp.where(bm, qr * bi + qi * br, 0.0).astype(BF16)
        cm = _c_mask()
        cset_ref[0] = jnp.where(cm, c_ref[0], 0.0).astype(BF16)
        cset_ref[1] = jnp.where(cm, c_ref[1], 0.0).astype(BF16)

    vm = pl.BlockSpec(memory_space=pltpu.VMEM)
    return pl.pallas_call(
        body, name=name, in_specs=[vm] * 6, out_specs=[vm] * 3,
        out_shape=[_sds((2, 8, 8, S5_NS)), _sds((2, S5_NS, LANES), BF16), _sds((2, S5_W, 512), BF16)],
        compiler_params=pltpu.CompilerParams(vmem_limit_bytes=VMEM_LIMIT),
    )(lam_r, ldt_r, lam_c, ldt_c, b_t, c_t)


SCAN_W = SCAN_GROUPS * LANES


def _scan_chunk(src_ref, dst_ref, tab_ref, carry_ref, nb, reverse, xs_ref=None, acc_ref=None):
    row = lax.broadcasted_iota(jnp.int32, (8, LANES), 0)

    def step(i, carry):
        b = (nb - 1 - i) if reverse else i
        off = pl.multiple_of(b * 8, 8)
        out = []
        for g in range(SCAN_GROUPS):
            lanes = pl.ds(g * LANES, LANES)
            cr, ci = carry[2 * g], carry[2 * g + 1]
            yr = src_ref[0, pl.ds(off, 8), lanes]
            yi = src_ref[1, pl.ds(off, 8), lanes]
            for t, s in enumerate((1, 2, 4)):
                sh = (8 - s) if reverse else s
                sr = pltpu.roll(yr, sh, 0)
                si = pltpu.roll(yi, sh, 0)
                mr, mi = tab_ref[2 * t, :, lanes], tab_ref[2 * t + 1, :, lanes]
                yr, yi = yr + mr * sr - mi * si, yi + mr * si + mi * sr
            pr, pi = tab_ref[6, :, lanes], tab_ref[7, :, lanes]
            yr, yi = yr + pr * cr - pi * ci, yi + pr * ci + pi * cr
            dst_ref[0, pl.ds(off, 8), lanes] = yr
            dst_ref[1, pl.ds(off, 8), lanes] = yi
            if xs_ref is not None:
                nr = jnp.where(row == 7, cr, pltpu.roll(yr, 7, 0))
                ni = jnp.where(row == 7, ci, pltpu.roll(yi, 7, 0))
                xr = xs_ref[0, pl.ds(off, 8), lanes]
                xi = xs_ref[1, pl.ds(off, 8), lanes]
                acc_ref[0, :, lanes] += xr * nr + xi * ni
                acc_ref[1, :, lanes] += xr * ni - xi * nr
            last = 0 if reverse else 7
            out += [jnp.broadcast_to(yr[last:last + 1, :], (8, LANES)),
                    jnp.broadcast_to(yi[last:last + 1, :], (8, LANES))]
        return tuple(out)

    init = []
    for g in range(SCAN_GROUPS):
        init += [carry_ref[0, :, pl.ds(g * LANES, LANES)], carry_ref[1, :, pl.ds(g * LANES, LANES)]]
    fin = lax.fori_loop(0, nb, step, tuple(init))
    for g in range(SCAN_GROUPS):
        carry_ref[0, :, pl.ds(g * LANES, LANES)] = fin[2 * g]
        carry_ref[1, :, pl.ds(g * LANES, LANES)] = fin[2 * g + 1]


def _s5_scan_fwd(z, bset, cset, dvec, tabs, name):
    L = z.shape[0]
    tl = min(SCAN_CHUNK, L)
    nc = L // tl

    def body(u_ref, b_ref, c_ref, d_ref, tab_ref, x_ref, y_ref, carry_ref):
        @pl.when(pl.program_id(1) == 0)
        def _():
            carry_ref[...] = jnp.zeros_like(carry_ref)

        uf = u_ref[...]
        u = uf.astype(BF16)
        x_ref[0] = _dot(u, b_ref[0], NT)
        x_ref[1] = _dot(u, b_ref[1], NT)
        _scan_chunk(x_ref, x_ref, tab_ref, carry_ref, tl // 8, False)
        y_ref[...] = (_dot(x_ref[0].astype(BF16), c_ref[0], NT) - _dot(x_ref[1].astype(BF16), c_ref[1], NT)
                      + d_ref[...] * uf)

    col = pl.BlockSpec((tl, LANES), lambda j, c: (c, j))
    return pl.pallas_call(
        body, name=name, grid=(S5_NS // SCAN_W, nc),
        in_specs=[col, pl.BlockSpec((2, SCAN_W, LANES), lambda j, c: (0, j, 0)),
                  pl.BlockSpec((2, LANES, SCAN_W), lambda j, c: (0, j, 0)),
                  pl.BlockSpec((1, LANES), lambda j, c: (0, j)),
                  pl.BlockSpec((None, 8, 8, SCAN_W), lambda j, c: (0, 0, 0, j))],
        out_specs=[pl.BlockSpec((2, tl, SCAN_W), lambda j, c: (0, c, j)), col],
        out_shape=[_sds((2, L, S5_NS)), _sds((L, S5_W))],
        scratch_shapes=[pltpu.VMEM((2, 8, SCAN_W), F32)],
        compiler_params=_cp("parallel", "arbitrary"),
    )(z, bset, cset, dvec, tabs)


def _s5_scan_bwd(gyl, cset, xs, z, bset, gud, tabs, name):
    L = z.shape[0]
    tl = min(SCAN_CHUNK, L)
    nc = L // tl

    def body(g_ref, c_ref, xs_ref, u_ref, b_ref, gud_ref, tab_ref, gu_ref, ga_ref, gb_ref, gc_ref,
             gx_ref, carry_ref, acc_ref):
        c = pl.program_id(1)

        @pl.when(c == 0)
        def _():
            carry_ref[...] = jnp.zeros_like(carry_ref)
            acc_ref[...] = jnp.zeros_like(acc_ref)
            gb_ref[...] = jnp.zeros_like(gb_ref)
            gc_ref[...] = jnp.zeros_like(gc_ref)

        gy = g_ref[...].astype(BF16)
        gx_ref[0] = _dot(gy, c_ref[0])
        gx_ref[1] = -_dot(gy, c_ref[1])
        gc_ref[0] += _dot(gy, xs_ref[0].astype(BF16), TN)
        gc_ref[1] -= _dot(gy, xs_ref[1].astype(BF16), TN)
        _scan_chunk(gx_ref, gx_ref, tab_ref, carry_ref, tl // 8, True, xs_ref, acc_ref)
        gr = gx_ref[0].astype(BF16)
        gi = gx_ref[1].astype(BF16)
        gu_ref[...] = gud_ref[...] + _dot(gr, b_ref[0]) + _dot(gi, b_ref[1])
        u = u_ref[...].astype(BF16)
        gb_ref[0] += _dot(gr, u, TN)
        gb_ref[1] += _dot(gi, u, TN)

        @pl.when(c == nc - 1)
        def _():
            ga_ref[0:1, :] = jnp.sum(acc_ref[0], axis=0, keepdims=True)
            ga_ref[1:2, :] = jnp.sum(acc_ref[1], axis=0, keepdims=True)

    rev = lambda j, c: (nc - 1 - c, j)
    col = pl.BlockSpec((tl, LANES), rev)
    return pl.pallas_call(
        body, name=name, grid=(S5_NS // SCAN_W, nc),
        in_specs=[col, pl.BlockSpec((2, LANES, SCAN_W), lambda j, c: (0, j, 0)),
                  pl.BlockSpec((2, tl, SCAN_W), lambda j, c: (0, nc - 1 - c, j)), col,
                  pl.BlockSpec((2, SCAN_W, LANES), lambda j, c: (0, j, 0)), col,
                  pl.BlockSpec((None, 8, 8, SCAN_W), lambda j, c: (1, 0, 0, j))],
        out_specs=[col, pl.BlockSpec((2, SCAN_W), lambda j, c: (0, j)),
                   pl.BlockSpec((2, SCAN_W, LANES), lambda j, c: (0, j, 0)),
                   pl.BlockSpec((2, LANES, SCAN_W), lambda j, c: (0, j, 0))],
        out_shape=[_sds((L, S5_W)), _sds((2, S5_NS)), _sds((2, S5_NS, LANES)), _sds((2, S5_W, 512))],
        scratch_shapes=[pltpu.VMEM((2, tl, SCAN_W), F32), pltpu.VMEM((2, 8, SCAN_W), F32),
                        pltpu.VMEM((2, 8, SCAN_W), F32)],
        compiler_params=_cp("parallel", "arbitrary"),
    )(gyl, cset, xs, z, bset, gud, tabs)


def _s5_glu_fwd(ylin, wglu, name):
    L = ylin.shape[0]
    bl = min(1024, L)

    def body(ylin_ref, w_ref, ya_ref):
        yg = _gelu(ylin_ref[...])
        t = _dot(yg.astype(BF16), w_ref[...])
        ya_ref[...] = (yg * _sigmoid(t)).astype(BF16)

    return pl.pallas_call(
        body, name=name, grid=(L // bl,),
        in_specs=[pl.BlockSpec((bl, S5_W), lambda i: (i, 0)), pl.BlockSpec((S5_W, S5_W), lambda i: (0, 0))],
        out_specs=pl.BlockSpec((bl, S5_W), lambda i: (i, 0)),
        out_shape=_sds((L, S5_W), BF16),
        compiler_params=_cp("parallel"),
    )(ylin, wglu)


def _s5_glu_bwd(g_y, wout, ylin, z, dvec, wglu, name):
    L = z.shape[0]
    bl = min(256, L)

    def body(g_ref, wo_ref, ylin_ref, u_ref, d_ref, w_ref, gyl_ref, gud_ref, gw_ref, gd_ref):
        i = pl.program_id(0)
        ylin = ylin_ref[...]
        yg = _gelu(ylin)
        ygb = yg.astype(BF16)
        sg = _sigmoid(_dot(ygb, w_ref[...]))
        gya = _dot(g_ref[...], wo_ref[...], NT)
        gt = gya * yg * sg * (1.0 - sg)
        gtb = gt.astype(BF16)
        gyg = gya * sg + _dot(gtb, w_ref[...], NT)
        gyl = gyg * _gelu_grad(ylin)
        gyl_ref[...] = gyl
        gud_ref[...] = gyl * d_ref[...]

        @pl.when(i == 0)
        def _():
            gw_ref[...] = jnp.zeros_like(gw_ref)
            gd_ref[...] = jnp.zeros_like(gd_ref)

        gw_ref[...] += _dot(ygb, gtb, TN)
        gd_ref[...] += jnp.sum(gyl * u_ref[...], axis=0, keepdims=True)

    blk = pl.BlockSpec((bl, S5_W), lambda i: (i, 0))
    return pl.pallas_call(
        body, name=name, grid=(L // bl,),
        in_specs=[pl.BlockSpec((bl, D_MODEL), lambda i: (i, 0)), pl.BlockSpec((S5_W, D_MODEL), lambda i: (0, 0)),
                  blk, blk, pl.BlockSpec((1, S5_W), lambda i: (0, 0)), pl.BlockSpec((S5_W, S5_W), lambda i: (0, 0))],
        out_specs=[blk, blk, pl.BlockSpec((S5_W, S5_W), lambda i: (0, 0)), pl.BlockSpec((1, S5_W), lambda i: (0, 0))],
        out_shape=[_sds((L, S5_W)), _sds((L, S5_W)), _sds((S5_W, S5_W)), _sds((1, S5_W))],
        compiler_params=_cp("arbitrary"),
    )(g_y, wout, ylin, z, dvec, wglu)


def _s5_param_bwd(lam_c, ldt_c, b_t, gb, ga_c, gc, name):
    def body(lam_ref, ldt_ref, b_ref, gb_ref, ga_ref, gc_ref, glam_ref, gldt_ref, gbo_ref, gco_ref):
        lr, li = lam_ref[:, 0:1], lam_ref[:, 1:2]
        dt, ar, ai, qr, qi, den = _zoh_cols(lr, li, ldt_ref[...])
        bm = _b_mask()
        gbr = jnp.where(bm, gb_ref[0], 0.0)
        gbi = jnp.where(bm, gb_ref[1], 0.0)
        br, bi = b_ref[0], b_ref[1]
        obr = gbr * qr + gbi * qi
        obi = gbi * qr - gbr * qi
        gqr = jnp.sum(gbr * br + gbi * bi, axis=1, keepdims=True)
        gqi = jnp.sum(gbi * br - gbr * bi, axis=1, keepdims=True)
        for s in (64, 32, 16):
            obr = obr + pltpu.roll(obr, s, 1)
            obi = obi + pltpu.roll(obi, s, 1)
        gbo_ref[0] = obr
        gbo_ref[1] = obi
        gar = ga_ref[:, 0:1] + (gqr * lr - gqi * li) / den
        gai = ga_ref[:, 1:2] + (gqr * li + gqi * lr) / den
        qlr = (qr * lr + qi * li) / den
        qli = (qi * lr - qr * li) / den
        glr = -(gqr * qlr + gqi * qli)
        gli = -(gqi * qlr - gqr * qli)
        glr = glr + dt * (gar * ar + gai * ai)
        gli = gli + dt * (gai * ar - gar * ai)
        wr, wi = _cmul(lr, li, ar, ai)
        gldt = (gar * wr + gai * wi) * dt
        glam_ref[:, 0:1] = glr
        glam_ref[:, 1:2] = gli
        r = lax.broadcasted_iota(jnp.int32, (S5_NS, 32), 0)
        c = lax.broadcasted_iota(jnp.int32, (S5_NS, 32), 1)
        gldt_ref[...] = jnp.sum(jnp.where((r >> 6) == c, gldt, 0.0), axis=0, keepdims=True)
        cm = _c_mask()
        for k in range(2):
            oc = jnp.where(cm, gc_ref[k], 0.0)
            for s in (256, 128, 64):
                oc = oc + pltpu.roll(oc, s, 1)
            gco_ref[k] = oc[:, 0:LANES]

    vm = pl.BlockSpec(memory_space=pltpu.VMEM)
    return pl.pallas_call(
        body, name=name, in_specs=[vm] * 6, out_specs=[vm] * 4,
        out_shape=[_sds((S5_NS, 2)), _sds((1, 32)), _sds((2, S5_NS, LANES)), _sds((2, S5_W, LANES))],
        compiler_params=pltpu.CompilerParams(vmem_limit_bytes=VMEM_LIMIT),
    )(lam_c, ldt_c, b_t, gb, ga_c, gc)


FL_BLK = EVEN_PAD // LANES - 1
Q_BLK, K_BLK, V_BLK = 4, 8, 12
NEG = -1e30


def _log_sigmoid(v):
    return jnp.minimum(v, 0.0) - jnp.log(1.0 + jnp.exp(-jnp.abs(v)))


def _fox_f_fwd(z, bf, name):
    L = z.shape[0]

    def body(fl_ref, b_ref, f_ref, fq_ref):
        row = lax.broadcasted_iota(jnp.int32, (L, LANES), 0)
        cs = _cumsum_rows(_log_sigmoid(fl_ref[...] + b_ref[...]), True, row)
        f_ref[...] = cs
        expand = (lax.broadcasted_iota(jnp.int32, (LANES, FOX_W), 0)
                  == (lax.broadcasted_iota(jnp.int32, (LANES, FOX_W), 1) >> 6)).astype(F32)
        fq_ref[...] = lax.dot_general(cs, expand, NN, precision=lax.Precision.HIGHEST, preferred_element_type=F32)

    return pl.pallas_call(
        body, name=name, grid=(1,),
        in_specs=[pl.BlockSpec((L, LANES), lambda i: (0, FL_BLK)), pl.BlockSpec((1, LANES), lambda i: (0, 0))],
        out_specs=[pl.BlockSpec((L, LANES), lambda i: (0, 0)), pl.BlockSpec((L, FOX_W), lambda i: (0, 0))],
        out_shape=[_sds((L, LANES)), _sds((L, FOX_W))],
        compiler_params=_cp("arbitrary"),
    )(z, bf)


def _fox_f_bwd(dFk, dfq, z, bf, name):
    L = z.shape[0]

    def body(dfk_ref, dfq_ref, fl_ref, b_ref, dfl_ref, db_ref):
        sel = (lax.broadcasted_iota(jnp.int32, (FOX_W, LANES), 0)
               == 64 * lax.broadcasted_iota(jnp.int32, (FOX_W, LANES), 1)).astype(F32)
        dfq_h = lax.dot_general(dfq_ref[...], sel, NN, precision=lax.Precision.HIGHEST, preferred_element_type=F32)
        row = lax.broadcasted_iota(jnp.int32, (L, LANES), 0)
        cs = _cumsum_rows(dfk_ref[...] + dfq_h, False, row)
        dfl = cs * _sigmoid(-(fl_ref[...] + b_ref[...]))
        dfl_ref[...] = dfl
        db_ref[...] = jnp.sum(dfl, axis=0, keepdims=True)

    return pl.pallas_call(
        body, name=name, grid=(1,),
        in_specs=[pl.BlockSpec((L, LANES), lambda i: (0, 0)), pl.BlockSpec((L, FOX_W), lambda i: (0, 0)),
                  pl.BlockSpec((L, LANES), lambda i: (0, FL_BLK)), pl.BlockSpec((1, LANES), lambda i: (0, 0))],
        out_specs=[pl.BlockSpec((L, LANES), lambda i: (0, 0)), pl.BlockSpec((1, LANES), lambda i: (0, 0))],
        out_shape=[_sds((L, LANES)), _sds((1, LANES))],
        compiler_params=_cp("arbitrary"),
    )(dFk, dfq, z, bf)


def _head_mask(hh):
    lane = lax.broadcasted_iota(jnp.int32, (1, LANES), 1)
    return (lane >> 6) == hh


FOX_T = 512


def _fox_head(x, hh):
    return jnp.where(_head_mask(hh), x, 0.0).astype(BF16)


def _fox_scores(qh, k, fq_ref, fr_ref, hh, causal):
    if fq_ref is None:
        s = _dot(qh, k, NT) - fr_ref[hh:hh + 1, :]
    else:
        s = _dot(qh, k, NT) + (fq_ref[:, 64 * hh:64 * hh + 1] - fr_ref[hh:hh + 1, :])
    return s if causal is None else jnp.where(causal, s, NEG)


def _causal(T):
    return lax.broadcasted_iota(jnp.int32, (T, T), 1) <= lax.broadcasted_iota(jnp.int32, (T, T), 0)


def _fox_fwd(z, fq, frow, name):
    L = z.shape[0]
    T = min(FOX_T, L)
    nq = L // T

    def body(qt_ref, kt_ref, q_ref, k_ref, v_ref, fq_ref, fr_ref, o_ref, lse_ref, m_ref, l_ref, acc_ref):
        t = pl.program_id(1)
        qi, ki = qt_ref[t], kt_ref[t]

        @pl.when(ki == 0)
        def _():
            m_ref[...] = jnp.full_like(m_ref, NEG)
            l_ref[...] = jnp.zeros_like(l_ref)
            acc_ref[...] = jnp.zeros_like(acc_ref)

        def step(diagonal):
            q = q_ref[...] * 0.125
            k = k_ref[...].astype(BF16)
            v = v_ref[...].astype(BF16)
            causal = _causal(T) if diagonal else None
            s = jnp.concatenate([_fox_scores(_fox_head(q, hh), k, fq_ref, fr_ref, hh, causal) for hh in range(2)],
                                axis=0)
            m_old = m_ref[...]
            m_new = jnp.maximum(m_old, jnp.max(s, axis=1, keepdims=True))
            alpha = jnp.exp(m_old - m_new)
            p = jnp.exp(s - m_new)
            l_ref[...] = alpha * l_ref[...] + jnp.sum(p, axis=1, keepdims=True)
            m_ref[...] = m_new
            acc_ref[...] = alpha * acc_ref[...] + _dot(p.astype(BF16), v)

        @pl.when(ki < qi)
        def _():
            step(False)

        @pl.when(ki == qi)
        def _():
            step(True)
            h0 = _head_mask(0)
            l = l_ref[...]
            o_h = acc_ref[...] / l
            lse_h = m_ref[...] + jnp.log(l)
            o_ref[...] = jnp.where(h0, o_h[:T], o_h[T:])
            lse_ref[...] = jnp.where(h0, lse_h[:T], lse_h[T:]) - fq_ref[...]

    pairs = [(qi, ki) for qi in range(nq) for ki in range(qi + 1)]
    qt = jnp.asarray([p[0] for p in pairs], jnp.int32)
    kt = jnp.asarray([p[1] for p in pairs], jnp.int32)

    def qspec(base):
        return pl.BlockSpec((T, LANES), lambda j, t, qt, kt: (qt[t], base + j))

    def kspec(base):
        return pl.BlockSpec((T, LANES), lambda j, t, qt, kt: (kt[t], base + j))

    return pl.pallas_call(
        body, name=name,
        grid_spec=pltpu.PrefetchScalarGridSpec(
            num_scalar_prefetch=2, grid=(4, len(pairs)),
            in_specs=[qspec(Q_BLK), kspec(K_BLK), kspec(V_BLK), qspec(0),
                      pl.BlockSpec((None, 2, T), lambda j, t, qt, kt: (j, 0, kt[t]))],
            out_specs=[qspec(0), qspec(0)],
            scratch_shapes=[pltpu.VMEM((2 * T, 1), F32), pltpu.VMEM((2 * T, 1), F32),
                            pltpu.VMEM((2 * T, LANES), F32)]),
        out_shape=[_sds((L, FOX_W)), _sds((L, FOX_W))],
        compiler_params=_cp("parallel", "arbitrary"),
    )(qt, kt, z, z, z, fq, frow)


def _fox_bwd(z, frow, o, lse, g_m, name):
    L = z.shape[0]
    T = min(FOX_T, L)
    nq = L // T

    pairs = [(qi, ki) for ki in range(nq) for qi in range(ki, nq)]
    qt = jnp.asarray([p[0] for p in pairs], jnp.int32)
    kt = jnp.asarray([p[1] for p in pairs], jnp.int32)

    def body(qt_ref, kt_ref, q_ref, k_ref, v_ref, fr_ref, o_ref, lse_ref, do_ref,
             dq_ref, dk_ref, dv_ref, dfq_ref, dfk_ref, dk_acc, dv_acc, df_acc):
        t = pl.program_id(1)
        qi, ki = qt_ref[t], kt_ref[t]

        @pl.when(t == 0)
        def _():
            dq_ref[...] = jnp.zeros_like(dq_ref)
            dfq_ref[...] = jnp.zeros_like(dfq_ref)

        @pl.when(qi == ki)
        def _():
            dk_acc[...] = jnp.zeros_like(dk_acc)
            dv_acc[...] = jnp.zeros_like(dv_acc)
            df_acc[...] = jnp.zeros_like(df_acc)

        def step(diagonal):
            q = q_ref[...] * 0.125
            qb = q.astype(BF16)
            k = k_ref[...].astype(BF16)
            v = v_ref[...].astype(BF16)
            do = do_ref[...]
            dob = do.astype(BF16)
            do_o = dob.astype(F32) * o_ref[...]
            causal = _causal(T) if diagonal else None
            dvs, dks, dqs, rss = [], [], [], []
            for hh in range(2):
                s = _fox_scores(_fox_head(q, hh), k, None, fr_ref, hh, causal)
                p = jnp.exp(s - lse_ref[:, 64 * hh:64 * hh + 1])
                dp = _dot(_fox_head(do, hh), v, NT)
                delta = jnp.sum(jnp.where(_head_mask(hh), do_o, 0.0), axis=1, keepdims=True)
                ds = p * (dp - delta)
                dsb = ds.astype(BF16)
                dvs.append(_dot(p.astype(BF16), dob, TN))
                dks.append(_dot(dsb, qb, TN))
                dqs.append(_dot(dsb, k))
                rss.append(jnp.sum(ds, axis=1, keepdims=True))
                df_acc[hh:hh + 1, :] -= jnp.sum(ds, axis=0, keepdims=True)
            h0 = _head_mask(0)
            dv_acc[...] += jnp.where(h0, dvs[0], dvs[1])
            dk_acc[...] += jnp.where(h0, dks[0], dks[1])
            rows = pl.ds(pl.multiple_of(qi * T, T), T)
            dq_ref[rows, :] += jnp.where(h0, dqs[0], dqs[1])
            dfq_ref[rows, :] += jnp.where(h0, rss[0], rss[1])

        @pl.when(qi > ki)
        def _():
            step(False)

        @pl.when(qi == ki)
        def _():
            step(True)

        @pl.when(qi == nq - 1)
        def _():
            dk_ref[...] = dk_acc[...]
            dv_ref[...] = dv_acc[...]
            dfk_ref[...] = df_acc[...]

        @pl.when(t == len(pairs) - 1)
        def _():
            dq_ref[...] = dq_ref[...] * 0.125

    def qside(base):
        return pl.BlockSpec((T, LANES), lambda j, t, qt, kt: (qt[t], base + j))

    def kside(base):
        return pl.BlockSpec((T, LANES), lambda j, t, qt, kt: (kt[t], base + j))

    pair = pl.BlockSpec((L, LANES), lambda j, t, qt, kt: (0, j))
    frow_spec = pl.BlockSpec((None, 2, T), lambda j, t, qt, kt: (j, 0, kt[t]))
    return pl.pallas_call(
        body, name=name,
        grid_spec=pltpu.PrefetchScalarGridSpec(
            num_scalar_prefetch=2, grid=(4, len(pairs)),
            in_specs=[qside(Q_BLK), kside(K_BLK), kside(V_BLK), frow_spec, qside(0), qside(0), qside(0)],
            out_specs=[pair, kside(0), kside(0), pair, frow_spec],
            scratch_shapes=[pltpu.VMEM((T, LANES), F32), pltpu.VMEM((T, LANES), F32), pltpu.VMEM((2, T), F32)]),
        out_shape=[_sds((L, FOX_W)), _sds((L, FOX_W)), _sds((L, FOX_W)), _sds((L, FOX_W)), _sds((4, 2, L))],
        compiler_params=_cp("parallel", "arbitrary"),
    )(qt, kt, z, z, z, frow, o, lse, g_m)


def _shift_rows(v, s, down, row):
    n = v.shape[0]
    if down:
        return jnp.where(row >= s, pltpu.roll(v, s, 0), 0.0)
    return jnp.where(row < n - s, pltpu.roll(v, n - s, 0), 0.0)


def _cumsum_rows(v, down, row):
    s = 1
    while s < v.shape[0]:
        v = v + _shift_rows(v, s, down, row)
        s *= 2
    return v


def _window_sum(v, g, down, row):
    out = jnp.zeros_like(v)
    s = v
    for k in range(4):
        s = s + _shift_rows(s, 1 << k, down, row)
        out = jnp.where(g == k, s, out)
    return out


def _pool_inv_cnt(g, row):
    w = jnp.left_shift(2, g).astype(F32)
    return 1.0 / jnp.minimum(row.astype(F32) + 1.0, w)


def _pool_fwd(z, pool_w, scale, name):
    L = z.shape[0]

    def body(x_ref, w_ref, s_ref, y_ref, p_ref):
        g = pl.program_id(0)
        row = lax.broadcasted_iota(jnp.int32, (L, LANES), 0)
        x = x_ref[...]
        pooled = (_window_sum(x, g, True, row) * _pool_inv_cnt(g, row) - x).astype(BF16)
        p_ref[...] = pooled
        y_ref[...] = (_dot(pooled, w_ref[...].astype(BF16)) * s_ref[...]).astype(BF16)

    col = pl.BlockSpec((L, LANES), lambda g: (0, g))
    return pl.pallas_call(
        body, name=name, grid=(4,),
        in_specs=[col, pl.BlockSpec((None, LANES, LANES), lambda g: (g, 0, 0)), pl.BlockSpec((1, LANES), lambda g: (0, g))],
        out_specs=[col, col],
        out_shape=[_sds((L, 512), BF16), _sds((L, 512), BF16)],
        compiler_params=_cp("parallel"),
    )(z, pool_w, scale)


def _pool_bwd(g_y, wout, pooled, pool_w, scale, name):
    L = g_y.shape[0]

    def body(g_ref, wo_ref, p_ref, w_ref, s_ref, gx_ref, gw_ref, gs_ref):
        g = pl.program_id(0)
        row = lax.broadcasted_iota(jnp.int32, (L, LANES), 0)
        gy = _dot(g_ref[...], wo_ref[...], NT)
        pooled = p_ref[...]
        wb = w_ref[...].astype(BF16)
        lin = _dot(pooled, wb)
        gs_ref[...] = jnp.sum(gy * lin, axis=0, keepdims=True)
        glin = (gy * s_ref[...]).astype(BF16)
        gw_ref[...] = _dot(pooled, glin, TN)
        gp = _dot(glin, wb, NT)
        gx_ref[...] = _window_sum(gp * _pool_inv_cnt(g, row), g, False, row) - gp

    col = pl.BlockSpec((L, LANES), lambda g: (0, g))
    wspec = pl.BlockSpec((None, LANES, LANES), lambda g: (g, 0, 0))
    vec = pl.BlockSpec((1, LANES), lambda g: (0, g))
    return pl.pallas_call(
        body, name=name, grid=(4,),
        in_specs=[pl.BlockSpec((L, D_MODEL), lambda g: (0, 0)), pl.BlockSpec((LANES, D_MODEL), lambda g: (g, 0)),
                  col, wspec, vec],
        out_specs=[col, wspec, vec],
        out_shape=[_sds((L, 512)), _sds((4, LANES, LANES)), _sds((1, 512))],
        compiler_params=_cp("parallel"),
    )(g_y, wout, pooled, pool_w, scale)


SGU_CHUNKS = 4


def _sgu_ln(v, gam, bet):
    gv = _gelu(v)
    mu = jnp.mean(gv, axis=-1, keepdims=True)
    xc = gv - mu
    rs = lax.rsqrt(jnp.mean(xc * xc, axis=-1, keepdims=True) + EPS)
    xh = xc * rs
    return xh, rs, xh * gam + bet


def _tril_ws(w_ref, g):
    r = lax.broadcasted_iota(jnp.int32, (LANES, LANES), 0)
    c = lax.broadcasted_iota(jnp.int32, (LANES, LANES), 1)
    return jnp.where(r >= c, w_ref[g], 0.0).astype(BF16)


def _sgu_fwd(z, ln_g, ln_b, w_s, b_st, name):
    L = z.shape[0]
    rb = min(SGU_CHUNKS * LANES, L)

    def body(u_ref, v_ref, g_ref, b_ref, w_ref, bs_ref, y_ref):
        _, _, vln = _sgu_ln(v_ref[...], g_ref[...], b_ref[...])
        gu = _gelu(u_ref[...])
        vb = vln.astype(BF16)
        for g in range(4):
            ws = _tril_ws(w_ref, g)
            for n in range(rb // LANES):
                rows = slice(n * LANES, (n + 1) * LANES)
                cols = slice(g * LANES, (g + 1) * LANES)
                mixed = _dot(ws, vb[rows, cols]) + bs_ref[:, g:g + 1]
                y_ref[rows, cols] = (gu[rows, cols] * mixed).astype(BF16)

    vm = lambda shape: pl.BlockSpec(shape, lambda i: tuple(0 for _ in shape))
    return pl.pallas_call(
        body, name=name, grid=(L // rb,),
        in_specs=[pl.BlockSpec((rb, 512), lambda i: (i, 1)), pl.BlockSpec((rb, 512), lambda i: (i, 2)),
                  vm((1, 512)), vm((1, 512)), vm((4, LANES, LANES)), vm((LANES, 4))],
        out_specs=pl.BlockSpec((rb, 512), lambda i: (i, 0)),
        out_shape=_sds((L, 512), BF16),
        compiler_params=_cp("parallel"),
    )(z, z, ln_g, ln_b, w_s, b_st)


def _sgu_bwd(g_y, wout, z, ln_g, ln_b, w_s, b_st, name):
    L = z.shape[0]
    rb = min(SGU_CHUNKS * LANES, L)

    def body(gyo_ref, wo_ref, u_ref, v_ref, g_ref, b_ref, w_ref, bs_ref, gu_ref, gv_ref, gw_ref, gbs_ref, gg_ref,
             gb_ref):
        i = pl.program_id(0)

        @pl.when(i == 0)
        def _():
            gw_ref[...] = jnp.zeros_like(gw_ref)
            gbs_ref[...] = jnp.zeros_like(gbs_ref)
            gg_ref[...] = jnp.zeros_like(gg_ref)
            gb_ref[...] = jnp.zeros_like(gb_ref)

        v = v_ref[...]
        u = u_ref[...]
        gy = _dot(gyo_ref[...], wo_ref[...], NT)
        xh, rs, vln = _sgu_ln(v, g_ref[...], b_ref[...])
        gel_u = _gelu(u)
        gmix = gy * gel_u
        vb = vln.astype(BF16)
        gmb = gmix.astype(BF16)
        r = lax.broadcasted_iota(jnp.int32, (LANES, LANES), 0)
        c = lax.broadcasted_iota(jnp.int32, (LANES, LANES), 1)
        gvln_cols = []
        for g in range(4):
            ws = _tril_ws(w_ref, g)
            cols = slice(g * LANES, (g + 1) * LANES)
            gw = jnp.zeros((LANES, LANES), F32)
            gbs = jnp.zeros((LANES, 1), F32)
            parts = []
            for n in range(rb // LANES):
                rows = slice(n * LANES, (n + 1) * LANES)
                mixed = _dot(ws, vb[rows, cols]) + bs_ref[:, g:g + 1]
                gu_ref[rows, cols] = gy[rows, cols] * mixed * _gelu_grad(u[rows, cols])
                parts.append(_dot(ws, gmb[rows, cols], TN))
                gw = gw + _dot(gmb[rows, cols], vb[rows, cols], NT)
                gbs = gbs + jnp.sum(gmix[rows, cols], axis=1, keepdims=True)
            gvln_cols.append(jnp.concatenate(parts, axis=0))
            gw_ref[g] += jnp.where(r >= c, gw, 0.0)
            gbs_ref[:, g:g + 1] += gbs
        gvln = jnp.concatenate(gvln_cols, axis=1)
        gg_ref[...] += jnp.sum(gvln * xh, axis=0, keepdims=True)
        gb_ref[...] += jnp.sum(gvln, axis=0, keepdims=True)
        gxh = gvln * g_ref[...]
        ggv = rs * (gxh - jnp.mean(gxh, axis=-1, keepdims=True) - xh * jnp.mean(gxh * xh, axis=-1, keepdims=True))
        gv_ref[...] = ggv * _gelu_grad(v)

    vm = lambda shape: pl.BlockSpec(shape, lambda i: tuple(0 for _ in shape))
    blk = pl.BlockSpec((rb, 512), lambda i: (i, 0))
    return pl.pallas_call(
        body, name=name, grid=(L // rb,),
        in_specs=[pl.BlockSpec((rb, D_MODEL), lambda i: (i, 0)), pl.BlockSpec((512, D_MODEL), lambda i: (1, 0)),
                  pl.BlockSpec((rb, 512), lambda i: (i, 1)), pl.BlockSpec((rb, 512), lambda i: (i, 2)),
                  vm((1, 512)), vm((1, 512)), vm((4, LANES, LANES)), vm((LANES, 4))],
        out_specs=[blk, blk, vm((4, LANES, LANES)), vm((LANES, 4)), vm((1, 512)), vm((1, 512))],
        out_shape=[_sds((L, 512)), _sds((L, 512)), _sds((4, LANES, LANES)), _sds((LANES, 4)),
                   _sds((1, 512)), _sds((1, 512))],
        compiler_params=_cp("arbitrary"),
    )(g_y, wout, z, z, ln_g, ln_b, w_s, b_st)


def _adamw_math(w, g, m, v):
    nm = ADAM_B1 * m + (1.0 - ADAM_B1) * g
    nv = ADAM_B2 * v + (1.0 - ADAM_B2) * (g * g)
    m_hat = nm / (1.0 - ADAM_B1 ** ADAM_STEP)
    v_hat = nv / (1.0 - ADAM_B2 ** ADAM_STEP)
    delta = -ADAM_LR * (m_hat / (jnp.sqrt(v_hat) + ADAM_EPS) + ADAM_WD * w)
    return delta, nm, nv


def _sum_adamw(parts, w, m, v, name, layer=0, prev=None):
    n_layers, R, C = w.shape
    rb = 128 if R % 128 == 0 else R

    def body(p_ref, w_ref, m_ref, v_ref, *rest):
        g_ref, d_ref, nm_ref, nv_ref = rest[-4:]
        g = p_ref[0].astype(F32)
        for s in range(1, N_DEV):
            g = g + p_ref[s].astype(F32)
        d, nm, nv = _adamw_math(w_ref[...], g, m_ref[...], v_ref[...])
        g_ref[...] = g
        d_ref[...] = d
        nm_ref[...] = nm
        nv_ref[...] = nv

    blk = pl.BlockSpec((None, rb, C), lambda i: (layer, i, 0))
    prev = [] if prev is None else list(prev)
    return pl.pallas_call(
        body, name=name, grid=(R // rb,),
        in_specs=[pl.BlockSpec((N_DEV, rb, C), lambda i: (0, i, 0)), blk, blk, blk] + [ANY] * len(prev),
        out_specs=[blk] * 4, out_shape=[_sds((n_layers, R, C))] * 4,
        input_output_aliases={4 + k: k for k in range(len(prev))},
        compiler_params=_cp("parallel"),
    )(parts, w, m, v, *prev)


def _sum_pieces(parts, name):
    _, R, C = parts.shape

    def body(p_ref, g_ref):
        g = p_ref[0].astype(F32)
        for s in range(1, N_DEV):
            g = g + p_ref[s].astype(F32)
        g_ref[...] = g

    vm = pl.BlockSpec(memory_space=pltpu.VMEM)
    return pl.pallas_call(body, name=name, in_specs=[vm], out_specs=vm, out_shape=_sds((R, C)),
                          compiler_params=pltpu.CompilerParams(vmem_limit_bytes=VMEM_LIMIT))(parts)


def _adamw_many(ws, gs, ms, vs, name):
    n = len(ws)
    vm = pl.BlockSpec(memory_space=pltpu.VMEM)

    def body(*refs):
        w_refs, g_refs, m_refs, v_refs = refs[:n], refs[n:2 * n], refs[2 * n:3 * n], refs[3 * n:4 * n]
        d_refs, nm_refs, nv_refs = refs[4 * n:5 * n], refs[5 * n:6 * n], refs[6 * n:7 * n]
        for i in range(n):
            d, nm, nv = _adamw_math(w_refs[i][...], g_refs[i][...], m_refs[i][...], v_refs[i][...])
            d_refs[i][...] = d
            nm_refs[i][...] = nm
            nv_refs[i][...] = nv

    shapes = [_sds(w.shape) for w in ws]
    outs = pl.pallas_call(
        body, name=name, in_specs=[vm] * (4 * n), out_specs=[vm] * (3 * n), out_shape=shapes * 3,
        compiler_params=pltpu.CompilerParams(vmem_limit_bytes=VMEM_LIMIT),
    )(*ws, *gs, *ms, *vs)
    return list(outs[:n]), list(outs[n:2 * n]), list(outs[2 * n:])


def _mesh_pos():
    return lax.axis_index("x"), lax.axis_index("y"), lax.axis_index("c")


def _dev_index(p):
    return 4 * p[0] + 2 * p[1] + p[2]


HBM = pl.BlockSpec(memory_space=pltpu.HBM)
SEM = pl.BlockSpec(memory_space=pltpu.SEMAPHORE)
EFFECT = pltpu.SideEffectType.DATAFLOW_SIDE_EFFECTING


def _peer_list():
    x, y, c = _mesh_pos()
    peers = [(x ^ dx, y ^ dy, c ^ dc) for dx in range(2) for dy in range(2) for dc in range(2)][1:]
    return (x, y, c), peers


def _split_copy(src_ref, land_ref, send_sems, recv_sems, i, k, peer, slot, exchange):
    return pltpu.make_async_remote_copy(
        src_ref=src_ref.at[_dev_index(peer)] if exchange else src_ref, dst_ref=land_ref.at[slot],
        send_sem=send_sems.at[7 * i + k], recv_sem=recv_sems.at[7 * i + k], device_id=peer, device_id_type=MESH)


def _own_copy(src_ref, land_ref, own_sems, i, slot, exchange):
    return pltpu.make_async_copy(src_ref.at[slot] if exchange else src_ref, land_ref.at[slot], own_sems.at[i])


def _comm_start(groups, name, exchange, dep=None):
    sizes = [len(g) for g in groups]
    n = sum(sizes)
    srcs = [a for g in groups for a in g]
    per_group = [exchange] * len(groups) if isinstance(exchange, bool) else list(exchange)
    exchanged = [flag for flag, sz in zip(per_group, sizes) for _ in range(sz)]
    lands = [lax.empty(a.shape if ex else (N_DEV,) + a.shape, a.dtype) for a, ex in zip(srcs, exchanged)]

    n_dep = 0 if dep is None else 1

    def body(*refs):
        src_refs, land_refs = refs[:n], refs[n:2 * n]
        sem_refs = refs[2 * n + n_dep:2 * n + n_dep + 3 * len(sizes)]
        token_ref = refs[-1]
        me, peers = _peer_list()
        mi = _dev_index(me)
        i = 0
        for gi, sz in enumerate(sizes):
            for j in range(sz):
                for k, peer in enumerate(peers):
                    _split_copy(src_refs[i], land_refs[i], sem_refs[3 * gi], sem_refs[3 * gi + 1], j, k, peer, mi,
                                exchanged[i]).start()
                _own_copy(src_refs[i], land_refs[i], sem_refs[3 * gi + 2], j, mi, exchanged[i]).start()
                i += 1
        token_ref[...] = jnp.zeros_like(token_ref)

    sem_shapes = []
    for sz in sizes:
        sem_shapes += [pltpu.SemaphoreType.DMA((7 * sz,)), pltpu.SemaphoreType.DMA((7 * sz,)),
                       pltpu.SemaphoreType.DMA((sz,))]
    thru = [pltpu.HBM(a.shape, a.dtype) for a in srcs + lands]
    n_sem = len(sem_shapes)
    outs = pl.pallas_call(
        body, name=name,
        out_shape=tuple(sem_shapes + thru + [_sds((8, LANES))]),
        in_specs=[HBM] * (2 * n) + [ANY] * n_dep,
        out_specs=tuple([SEM] * n_sem + [HBM] * (2 * n) + [pl.BlockSpec(memory_space=pltpu.VMEM)]),
        input_output_aliases={i: n_sem + i for i in range(2 * n)},
        compiler_params=pltpu.CompilerParams(has_side_effects=EFFECT),
    )(*[pltpu.with_memory_space_constraint(a, pltpu.HBM) for a in srcs + lands], *([] if dep is None else [dep]))
    sems, thru_src, thru_land, token = outs[:n_sem], outs[n_sem:n_sem + n], outs[n_sem + n:n_sem + 2 * n], outs[-1]
    result, off = [], 0
    for gi, sz in enumerate(sizes):
        result.append((*sems[3 * gi:3 * gi + 3], list(thru_src[off:off + sz]), list(thru_land[off:off + sz])))
        off += sz
    return result, token


def _comm_wait(group, after, name, exchange):
    send_sems, recv_sems, own_sems, srcs, lands = group
    n = len(srcs)
    after = list(after) if isinstance(after, (list, tuple)) else [after]

    def body(*refs):
        src_refs, land_refs = refs[:n], refs[n:2 * n]
        ssem, rsem, osem = refs[2 * n:2 * n + 3]
        me, peers = _peer_list()
        for i in range(n):
            for k, peer in enumerate(peers):
                cp = _split_copy(src_refs[i], land_refs[i], ssem, rsem, i, k, peer, _dev_index(peer), exchange)
                cp.wait_send()
                cp.wait_recv()
            _own_copy(src_refs[i], land_refs[i], osem, i, _dev_index(me), exchange).wait()

    outs = pl.pallas_call(
        body, name=name,
        out_shape=tuple(pltpu.HBM(a.shape, a.dtype) for a in srcs + lands),
        in_specs=[HBM] * (2 * n) + [SEM, SEM, SEM] + [ANY] * len(after),
        out_specs=tuple([HBM] * (2 * n)),
        input_output_aliases={i: i for i in range(2 * n)},
        compiler_params=pltpu.CompilerParams(has_side_effects=EFFECT),
    )(*srcs, *lands, send_sems, recv_sems, own_sems, *after)
    return list(outs[n:])


def _tie(a, token):
    return a + token[0, 0].astype(a.dtype)


def _pack(arrs, rows):
    flat = jnp.concatenate([a.reshape(-1).astype(F32) for a in arrs])
    return jnp.pad(flat, (0, rows * LANES - flat.shape[0])).reshape(rows, LANES)


def _unpack(packed, shapes):
    flat = packed.reshape(-1)
    out, off = [], 0
    for s in shapes:
        n = math.prod(s)
        out.append(flat[off:off + n].reshape(s))
        off += n
    return out


def _packed_rows(shapes):
    n = sum(math.prod(s) for s in shapes)
    unit = N_DEV * 8 * LANES
    return -(-n // unit) * unit // LANES


def kernel(x, mix_pre_g, mix_post_g, mlp_pre_g, mlp_post_g, w_in_even, s5_lam_re, s5_lam_im, s5_log_dt, s5_b_re, s5_b_im, s5_c_re, s5_c_im, s5_d, s5_w_glu, fox_b_f, w_out_even, w_in_odd, pool_w, pool_scale, sgu_ln_g, sgu_ln_b, sgu_w_s, sgu_b_s, w_out_odd, mlp_w1, mlp_w2, loss_target, m_mix_pre_g, m_mix_post_g, m_mlp_pre_g, m_mlp_post_g, m_w_in_even, m_s5_lam_re, m_s5_lam_im, m_s5_log_dt, m_s5_b_re, m_s5_b_im, m_s5_c_re, m_s5_c_im, m_s5_d, m_s5_w_glu, m_fox_b_f, m_w_out_even, m_w_in_odd, m_pool_w, m_pool_scale, m_sgu_ln_g, m_sgu_ln_b, m_sgu_w_s, m_sgu_b_s, m_w_out_odd, m_mlp_w1, m_mlp_w2, v_mix_pre_g, v_mix_post_g, v_mlp_pre_g, v_mlp_post_g, v_w_in_even, v_s5_lam_re, v_s5_lam_im, v_s5_log_dt, v_s5_b_re, v_s5_b_im, v_s5_c_re, v_s5_c_im, v_s5_d, v_s5_w_glu, v_fox_b_f, v_w_out_even, v_w_in_odd, v_pool_w, v_pool_scale, v_sgu_ln_g, v_sgu_ln_b, v_sgu_w_s, v_sgu_b_s, v_w_out_odd, v_mlp_w1, v_mlp_w2):
    weights = dict(mix_pre_g=mix_pre_g, mix_post_g=mix_post_g, mlp_pre_g=mlp_pre_g, mlp_post_g=mlp_post_g, w_in_even=w_in_even, s5_lam_re=s5_lam_re, s5_lam_im=s5_lam_im, s5_log_dt=s5_log_dt, s5_b_re=s5_b_re, s5_b_im=s5_b_im, s5_c_re=s5_c_re, s5_c_im=s5_c_im, s5_d=s5_d, s5_w_glu=s5_w_glu, fox_b_f=fox_b_f, w_out_even=w_out_even, w_in_odd=w_in_odd, pool_w=pool_w, pool_scale=pool_scale, sgu_ln_g=sgu_ln_g, sgu_ln_b=sgu_ln_b, sgu_w_s=sgu_w_s, sgu_b_s=sgu_b_s, w_out_odd=w_out_odd, mlp_w1=mlp_w1, mlp_w2=mlp_w2)
    mom_m = dict(mix_pre_g=m_mix_pre_g, mix_post_g=m_mix_post_g, mlp_pre_g=m_mlp_pre_g, mlp_post_g=m_mlp_post_g, w_in_even=m_w_in_even, s5_lam_re=m_s5_lam_re, s5_lam_im=m_s5_lam_im, s5_log_dt=m_s5_log_dt, s5_b_re=m_s5_b_re, s5_b_im=m_s5_b_im, s5_c_re=m_s5_c_re, s5_c_im=m_s5_c_im, s5_d=m_s5_d, s5_w_glu=m_s5_w_glu, fox_b_f=m_fox_b_f, w_out_even=m_w_out_even, w_in_odd=m_w_in_odd, pool_w=m_pool_w, pool_scale=m_pool_scale, sgu_ln_g=m_sgu_ln_g, sgu_ln_b=m_sgu_ln_b, sgu_w_s=m_sgu_w_s, sgu_b_s=m_sgu_b_s, w_out_odd=m_w_out_odd, mlp_w1=m_mlp_w1, mlp_w2=m_mlp_w2)
    mom_v = dict(mix_pre_g=v_mix_pre_g, mix_post_g=v_mix_post_g, mlp_pre_g=v_mlp_pre_g, mlp_post_g=v_mlp_post_g, w_in_even=v_w_in_even, s5_lam_re=v_s5_lam_re, s5_lam_im=v_s5_lam_im, s5_log_dt=v_s5_log_dt, s5_b_re=v_s5_b_re, s5_b_im=v_s5_b_im, s5_c_re=v_s5_c_re, s5_c_im=v_s5_c_im, s5_d=v_s5_d, s5_w_glu=v_s5_w_glu, fox_b_f=v_fox_b_f, w_out_even=v_w_out_even, w_in_odd=v_w_in_odd, pool_w=v_pool_w, pool_scale=v_pool_scale, sgu_ln_g=v_sgu_ln_g, sgu_ln_b=v_sgu_ln_b, sgu_w_s=v_sgu_w_s, sgu_b_s=v_sgu_b_s, w_out_odd=v_w_out_odd, mlp_w1=v_mlp_w1, mlp_w2=v_mlp_w2)
    names = list(weights)
    L = x.shape[1]
    x0 = x[0]
    target = loss_target[0]
    my_index = 4 * lax.axis_index("x") + 2 * lax.axis_index("y") + lax.axis_index("c")

    small_vec = jnp.zeros((8, LANES), F32)
    small_vec = small_vec.at[0, :64].set(pool_scale[0]).at[1, :64].set(sgu_ln_g[0]).at[2, :64].set(sgu_ln_b[0])
    ag_groups, ag_token = _comm_start(
        [[jnp.transpose(w_in_even[0]).astype(BF16), small_vec],
         [s5_w_glu[0].astype(BF16), w_out_even[0].astype(BF16)],
         [mlp_w1[0].astype(BF16), mlp_w2[0].astype(BF16)],
         [jnp.transpose(w_in_odd[0]).astype(BF16), w_out_odd[0].astype(BF16), mlp_w1[1].astype(BF16), mlp_w2[1].astype(BF16)]],
        "ag_start", exchange=False)

    lam_r = jnp.concatenate([s5_lam_re.reshape(1, S5_NS), s5_lam_im.reshape(1, S5_NS)], axis=0)
    ldt_r = jnp.repeat(s5_log_dt.reshape(32), 64).reshape(1, S5_NS)
    lam_c = jnp.transpose(lam_r)
    ldt_c = jnp.transpose(ldt_r)
    b_t = jnp.stack([jnp.tile(s5_b_re.reshape(S5_NS, 16), (1, 8)), jnp.tile(s5_b_im.reshape(S5_NS, 16), (1, 8))])
    c_t = jnp.stack([jnp.tile(s5_c_re.reshape(S5_W, 64), (1, 8)), jnp.tile(s5_c_im.reshape(S5_W, 64), (1, 8))])
    bf_pad = jnp.pad(fox_b_f, ((0, 0), (0, LANES - 8)))
    b_st = jnp.transpose(sgu_b_s[0])

    h0, rx0 = _rms_fwd(x0, _tie(mix_pre_g[0:1], ag_token), "rms0")
    tabs, bset, cset = _s5_prep(lam_r, ldt_r, lam_c, ldt_c, b_t, c_t, "s5_prep")
    ag0 = _comm_wait(ag_groups[0], tabs, "ag_wait0", exchange=False)
    winT_e = jnp.pad(ag0[0].reshape(EVEN_IN, D_MODEL), ((0, EVEN_PAD - EVEN_IN), (0, 0)))
    pool_scale_f = ag0[1][:, 0, :64].reshape(1, 512)
    ln_g_f = ag0[1][:, 1, :64].reshape(1, 512)
    ln_b_f = ag0[1][:, 2, :64].reshape(1, 512)
    z0 = _mm(h0, winT_e, name="win_even", tb=True, bm=512, bn=EVEN_PAD)
    xs, ylin = _s5_scan_fwd(z0, bset, cset, s5_d, tabs, "s5_scan")
    ag1 = _comm_wait(ag_groups[1], ylin, "ag_wait1", exchange=False)
    wglu = ag1[0].reshape(S5_W, S5_W)
    wout_e = ag1[1].reshape(D_MODEL, D_MODEL)
    ya = _s5_glu_fwd(ylin, wglu, "s5_glu")
    fcum, fq = _fox_f_fwd(z0, bf_pad, "fox_f")
    frow = jnp.transpose(fcum[:, :8]).reshape(4, 2, L)
    o_att, lse = _fox_fwd(z0, fq, frow, "fox_fwd")
    mix0 = [ya, o_att]
    x1, ry0, h1, rx1, y0 = _mm(mix0, wout_e, name="wout_even", epi=_epi_post_pre, extra=(x0,),
                               vecs=(mix_post_g[0:1], mlp_pre_g[0:1]), out_dtypes=POST_PRE_DTYPES,
                               out_kinds=POST_PRE_KINDS, bm=FUSED_ROWS)
    ag2 = _comm_wait(ag_groups[2], rx1, "ag_wait2", exchange=False)
    w1 = [ag2[0], None]
    w2 = [ag2[1].reshape(4 * D_MODEL, D_MODEL), None]
    p0, a0 = _mm(h1, w1[0], name="mlp0_w1", b3=True, out_dtypes=(BF16, BF16), epi=_epi_relu2, bm=512, bn=4 * D_MODEL)
    x2, ro0, h2, rx2, o0 = _mm(a0, w2[0], name="mlp0_w2", epi=_epi_post_pre, extra=(x1,),
                               vecs=(mlp_post_g[0:1], mix_pre_g[1:2]), out_dtypes=POST_PRE_DTYPES,
                               out_kinds=POST_PRE_KINDS, bm=FUSED_ROWS, bk=4 * D_MODEL)
    ag3 = _comm_wait(ag_groups[3], rx2, "ag_wait3", exchange=False)
    winT_o = ag3[0].reshape(ODD_IN, D_MODEL)
    wout_o = ag3[1].reshape(D_MODEL, D_MODEL)
    w1[1] = ag3[2]
    w2[1] = ag3[3].reshape(4 * D_MODEL, D_MODEL)
    z1 = _mm(h2, winT_o, name="win_odd", tb=True, bn=ODD_IN)
    yc, pooled = _pool_fwd(z1, pool_w[0], pool_scale_f, "pool_fwd")
    yd = _sgu_fwd(z1, ln_g_f, ln_b_f, sgu_w_s[0], b_st, "sgu_fwd")
    mix1 = [yc, yd]
    x3, ry1, h3, rx3, y1 = _mm(mix1, wout_o, name="wout_odd", epi=_epi_post_pre, extra=(x2,),
                               vecs=(mix_post_g[1:2], mlp_pre_g[1:2]), out_dtypes=POST_PRE_DTYPES,
                               out_kinds=POST_PRE_KINDS, bm=FUSED_ROWS)
    p1, a1 = _mm(h3, w1[1], name="mlp1_w1", b3=True, out_dtypes=(BF16, BF16), epi=_epi_relu2, bm=512, bn=4 * D_MODEL)
    gx4, g_o1, gg_mlp_post1, sq_lanes = _mm(
        a1, w2[1], name="mlp1_w2", epi=_epi_post_loss, extra=(x3, target), vecs=(mlp_post_g[1:2],),
        out_dtypes=(F32, BF16, F32, F32), out_kinds=("full", "full", "vsum", "vsum"), bm=FUSED_ROWS, bk=4 * D_MODEL)
    sq = sq_lanes[:, 0:1]

    g_p1 = _mm(g_o1, w2[1], name="b_mlp1_a", tb=True, out_dtypes=(BF16,), epi=_epi_relu2_bwd, extra=(p1,),
               bm=512, bn=4 * D_MODEL)
    gw2_1 = _mm(a1, g_o1, name="b_mlp1_w2", ta=True, bm=512, bk=L)
    gw1_1 = _mm(h3, g_p1, name="b_mlp1_w1", ta=True, out3=True, bn=512, bk=L)
    (ex1,), tok1 = _comm_start([[gw1_1, gw2_1.reshape(N_DEV, 512, D_MODEL)]], "ex_start1", exchange=True)
    g_x3, gg_mlp_pre1, g_y1, gg_mix_post1 = _mm(
        g_p1, w1[1], name="b_mlp1_h", tb=True, b3=True, epi=_epi_pre_post_bwd, extra=(x3, gx4, y1), cols=(rx3, ry1),
        vecs=(_tie(mlp_pre_g[1:2], tok1), mix_post_g[1:2]), out_dtypes=PRE_POST_BWD_DTYPES,
        out_kinds=PRE_POST_BWD_KINDS, bm=FUSED_ROWS, bk=4 * D_MODEL)
    gwout_o = _mm(mix1, g_y1, name="b_wout_odd_w", ta=True)
    g_xc, g_pool_w, g_pool_scale = _pool_bwd(g_y1, wout_o, pooled, pool_w[0], pool_scale_f, "pool_bwd")
    g_u1, g_v1, g_ws, g_bst, g_ln_g, g_ln_b = _sgu_bwd(g_y1, wout_o, z1, ln_g_f, ln_b_f, sgu_w_s[0], b_st,
                                                       "sgu_bwd")
    g_z1 = [g_xc, g_u1, g_v1]
    gwinT_o = _mm(g_z1, h2, name="b_win_odd_w", ta=True)
    (ex2,), tok2 = _comm_start([[gwout_o.reshape(N_DEV, 128, D_MODEL), gwinT_o.reshape(N_DEV, ODD_IN // N_DEV, D_MODEL)]], "ex_start2", exchange=True)
    g_x2, gg_mix_pre1, g_o0, gg_mlp_post0 = _mm(
        g_z1, winT_o, name="b_win_odd_h", epi=_epi_pre_post_bwd, extra=(x2, g_x3, o0), cols=(rx2, ro0),
        vecs=(_tie(mix_pre_g[1:2], tok2), mlp_post_g[0:1]), out_dtypes=PRE_POST_BWD_DTYPES,
        out_kinds=PRE_POST_BWD_KINDS, bm=FUSED_ROWS)
    g_p0 = _mm(g_o0, w2[0], name="b_mlp0_a", tb=True, out_dtypes=(BF16,), epi=_epi_relu2_bwd, extra=(p0,),
               bm=512, bn=4 * D_MODEL)
    gw2_0 = _mm(a0, g_o0, name="b_mlp0_w2", ta=True, bm=512, bk=L)
    gw1_0 = _mm(h1, g_p0, name="b_mlp0_w1", ta=True, out3=True, bn=512, bk=L)
    (ex3,), tok3 = _comm_start([[gw1_0, gw2_0.reshape(N_DEV, 512, D_MODEL)]], "ex_start3", exchange=True)
    g_x1, gg_mlp_pre0, g_y0, gg_mix_post0 = _mm(
        g_p0, w1[0], name="b_mlp0_h", tb=True, b3=True, epi=_epi_pre_post_bwd, extra=(x1, g_x2, y0), cols=(rx1, ry0),
        vecs=(_tie(mlp_pre_g[0:1], tok3), mix_post_g[0:1]), out_dtypes=PRE_POST_BWD_DTYPES,
        out_kinds=PRE_POST_BWD_KINDS, bm=FUSED_ROWS, bk=4 * D_MODEL)
    g_o_att = _mm(g_y0, wout_e[FOX_W:], name="b_wout_even_m", tb=True)
    gwout_e = _mm(mix0, g_y0, name="b_wout_even_w", ta=True)
    gyl, gud, g_wglu, g_d = _s5_glu_bwd(g_y0, wout_e, ylin, z0, s5_d, wglu, "s5_glu_bwd")
    (ex4,), tok4 = _comm_start([[gwout_e.reshape(N_DEV, 128, D_MODEL), g_wglu.reshape(N_DEV, 64, S5_W)]], "ex_start4", exchange=True)
    g_u0, ga, gb_raw, gc_raw = _s5_scan_bwd(gyl, _tie(cset, tok4), xs, z0, bset, gud, tabs, "s5_scan_bwd")
    g_lam, g_ldt, g_b, g_c = _s5_param_bwd(lam_c, ldt_c, b_t, gb_raw, jnp.transpose(ga), gc_raw, "s5_param_bwd")
    dq, dk, dv, dfq, dfrow = _fox_bwd(z0, frow, o_att, lse, g_o_att, "fox_bwd")
    dFk = jnp.pad(jnp.transpose(dfrow.reshape(8, L)), ((0, 0), (0, LANES - 8)))
    dfl, db_f = _fox_f_bwd(dFk, dfq, z0, bf_pad, "fox_f_bwd")
    g_z0 = [g_u0, dq, dk, dv, dfl]
    grad_x, gg_mix_pre0 = _mm(g_z0, winT_e, name="b_win_even_h", epi=_epi_pre_bwd, extra=(x0, g_x1), cols=(rx0,),
                              vecs=(mix_pre_g[0:1],), out_dtypes=(F32, F32), out_kinds=("full", "vsum"),
                              bm=FUSED_ROWS)

    small_grads = dict(
        mix_pre_g=jnp.concatenate([gg_mix_pre0, gg_mix_pre1]), mix_post_g=jnp.concatenate([gg_mix_post0, gg_mix_post1]),
        mlp_pre_g=jnp.concatenate([gg_mlp_pre0, gg_mlp_pre1]), mlp_post_g=jnp.concatenate([gg_mlp_post0, gg_mlp_post1]),
        s5_lam_re=g_lam[:, 0], s5_lam_im=g_lam[:, 1], s5_log_dt=g_ldt,
        s5_b_re=g_b[0, :, :16], s5_b_im=g_b[1, :, :16], s5_c_re=g_c[0, :, :64], s5_c_im=g_c[1, :, :64],
        s5_d=g_d, fox_b_f=db_f[:, :8], pool_w=g_pool_w, sgu_w_s=g_ws, sgu_b_s=jnp.transpose(g_bst),
        pool_scale=g_pool_scale, sgu_ln_g=g_ln_g, sgu_ln_b=g_ln_b)
    small_names = list(small_grads)
    full_shapes = [(512,) if nm in ("pool_scale", "sgu_ln_g", "sgu_ln_b") else weights[nm].shape for nm in small_names]
    full_shapes.append((1, 1))
    rows = _packed_rows(full_shapes)
    packed = _pack([small_grads[nm] for nm in small_names] + [sq], rows).reshape(N_DEV, rows // N_DEV, LANES)
    (exs,), tok_s = _comm_start([[packed]], "exs_start", exchange=True)
    gwinT_e = _mm(g_z0, h0, name="b_win_even_w", ta=True, bk=512, out_dtypes=(BF16,), dep=tok_s)
    (recv_small,) = _comm_wait(exs, gwinT_e, "exs_wait", exchange=True)
    piece = _sum_pieces(recv_small, "sum_small")
    gwinT_e_pieces = gwinT_e[:EVEN_IN].reshape(N_DEV, EVEN_IN // N_DEV, D_MODEL)
    (ags, ex5), tok5 = _comm_start([[piece], [gwinT_e_pieces]], "ags_ex_start5", exchange=(False, True))
    r_w1_1, r_w2_1 = _comm_wait(ex1, tok5, "ex_wait1", exchange=True)
    r_wout_o, r_win_o = _comm_wait(ex2, tok5, "ex_wait2", exchange=True)
    r_w1_0, r_w2_0 = _comm_wait(ex3, tok5, "ex_wait3", exchange=True)
    r_wout_e, r_wglu = _comm_wait(ex4, tok5, "ex_wait4", exchange=True)

    res = {}
    for nm, parts in (("mlp_w1", (r_w1_0, r_w1_1)), ("mlp_w2", (r_w2_0, r_w2_1))):
        first = _sum_adamw(parts[0], weights[nm], mom_m[nm], mom_v[nm], "adamw_%s_0" % nm, layer=0)
        res[nm] = tuple(_sum_adamw(parts[1], weights[nm], mom_m[nm], mom_v[nm], "adamw_%s_1" % nm, layer=1, prev=first))
    big_parts = dict(s5_w_glu=r_wglu, w_out_even=r_wout_e, w_out_odd=r_wout_o)
    for nm, parts in big_parts.items():
        res[nm] = tuple(_sum_adamw(parts, weights[nm], mom_m[nm], mom_v[nm], "adamw_" + nm))
    done = [res[nm][1] for nm in ("mlp_w1", "mlp_w2", "s5_w_glu", "w_out_even", "w_out_odd")]

    (small_all,) = _comm_wait(ags, done, "ags_wait", exchange=False)
    small_full = _unpack(small_all.reshape(rows, LANES), full_shapes)
    loss = 0.5 * small_full.pop()[0, 0] / D_MODEL
    small_g = []
    for nm, g in zip(small_names, small_full):
        if nm in ("pool_scale", "sgu_ln_g", "sgu_ln_b"):
            g = lax.dynamic_slice(g, (my_index * 64,), (64,)).reshape(1, 64)
        small_g.append(g)

    def turned(arrs):
        return [jnp.swapaxes(a, -1, -2) if nm in ("s5_b_re", "s5_b_im") else a for nm, a in zip(small_names, arrs)]

    sd, sm, sv = _adamw_many(turned([weights[nm] for nm in small_names]), turned(small_g),
                             turned([mom_m[nm] for nm in small_names]), turned([mom_v[nm] for nm in small_names]),
                             "adamw_small")
    for nm, g_, d_, m_, v_ in zip(small_names, small_g, turned(sd), turned(sm), turned(sv)):
        res[nm] = (g_, d_, m_, v_)
    done.append(sd[0])

    for nm, parts in (("w_in_odd", r_win_o), ("w_in_even", None)):
        if parts is None:
            (parts,) = _comm_wait(ex5, done, "ex_wait5", exchange=True)
        outs = _sum_adamw(parts, jnp.transpose(weights[nm], (0, 2, 1)), jnp.transpose(mom_m[nm], (0, 2, 1)),
                          jnp.transpose(mom_v[nm], (0, 2, 1)), "adamw_" + nm)
        res[nm] = tuple(jnp.transpose(o, (0, 2, 1)) for o in outs)
        done.append(res[nm][1])

    grads = [res[nm][0].reshape(weights[nm].shape) for nm in names]
    deltas = [res[nm][1].reshape(weights[nm].shape) for nm in names]
    new_m = [res[nm][2].reshape(weights[nm].shape) for nm in names]
    new_v = [res[nm][3].reshape(weights[nm].shape) for nm in names]
    return (loss, grad_x[None], *grads, *deltas, *new_m, *new_v)
```

```python
import math

import jax
import jax.numpy as jnp
from jax import lax
from jax.experimental import pallas as pl
from jax.experimental.pallas import tpu as pltpu

F32 = jnp.float32
BF16 = jnp.bfloat16
MESH = pl.DeviceIdType.MESH
ANY = pl.BlockSpec(memory_space=pl.ANY)

N_DEV = 8
D_MODEL = 1024
EPS = 1e-6
NORM_ROWS = 512
FUSED_ROWS = 512
S5_W = 512
S5_NS = 2048
SCAN_GROUPS = 4
SCAN_CHUNK = 1024
FOX_W = 512
EVEN_IN = 2056
EVEN_PAD = 2176
ODD_IN = 1536
LANES = 128
PIECE = 4 * D_MODEL // N_DEV
VMEM_LIMIT = 56 * 1024 * 1024

ADAM_LR = 0.001
ADAM_B1 = 0.9
ADAM_B2 = 0.999
ADAM_EPS = 1e-08
ADAM_WD = 0.01
ADAM_STEP = 10

NT = (((1,), (1,)), ((), ()))
TN = (((0,), (0,)), ((), ()))
NN = (((1,), (0,)), ((), ()))


def _cp(*sem):
    return pltpu.CompilerParams(dimension_semantics=sem, vmem_limit_bytes=VMEM_LIMIT)


def _sds(shape, dtype=F32):
    return jax.ShapeDtypeStruct(tuple(shape), dtype)


def _gelu(x):
    t = jnp.tanh(0.7978845608028654 * (x + 0.044715 * x * x * x))
    return 0.5 * x * (1.0 + t)


def _gelu_grad(x):
    t = jnp.tanh(0.7978845608028654 * (x + 0.044715 * x * x * x))
    du = 0.7978845608028654 * (1.0 + 3.0 * 0.044715 * x * x)
    return 0.5 * (1.0 + t) + 0.5 * x * (1.0 - t * t) * du


def _sigmoid(x):
    return 1.0 / (1.0 + jnp.exp(-x))


def _dot(a, b, dn=NN):
    return lax.dot_general(a, b, dn, preferred_element_type=F32)


def _mm(a, b, *, name, ta=False, tb=False, b3=False, out3=False, out_dtypes=(F32,), epi=None, extra=(),
        cols=(), vecs=(), out_kinds=None, bm=1024, bn=1024, bk=1024, dep=None):
    a_list = list(a) if isinstance(a, (list, tuple)) else [a]
    widths = [p.shape[1] for p in a_list]
    offs = [sum(widths[:i]) for i in range(len(widths))]
    na = len(a_list)
    M = sum(widths) if ta else a_list[0].shape[0]
    K = a_list[0].shape[0] if ta else sum(widths)
    if na > 1:
        assert not b3 and not tb
        bm, bk = (M, bk) if ta else (bm, K)
    pw = b.shape[2] if b3 else PIECE
    if b3:
        N = b.shape[1] if tb else b.shape[0] * pw
        assert (b.shape[0] * pw if tb else b.shape[1]) == K
    else:
        N = b.shape[0] if tb else b.shape[1]
    bm, bn, bk = min(bm, M), min(bn, N), min(bk, K)
    assert M % bm == 0 and N % bn == 0 and K % bk == 0, (name, M, N, K, bm, bn, bk)
    assert not (b3 or out3) or ((bk if tb else bn) % pw == 0 and bn % PIECE == 0)
    nk = K // bk
    n_extra = len(extra) + len(cols) + len(vecs)
    n_out = len(out_dtypes)
    out_kinds = tuple(out_kinds) if out_kinds is not None else ("full",) * n_out
    dn = (((0 if ta else 1,), (1 if tb else 0,)), ((), ()))

    use_acc = nk > 1

    def body(*refs):
        a_refs, b_ref = refs[:na], refs[na]
        a_ref = a_refs[0]
        e_refs = refs[na + 1:na + 1 + n_extra]
        first_out = na + 1 + n_extra + (0 if dep is None else 1)
        o_refs = refs[first_out:first_out + n_out]
        acc_ref = refs[-1] if use_acc else o_refs[0]
        i, k = pl.program_id(0), pl.program_id(2)

        def dot(a_v, b_v):
            return lax.dot_general(a_v.astype(BF16), b_v.astype(BF16), dn, preferred_element_type=F32)

        everything = slice(None)
        if na > 1 and ta:
            terms = [(pl.ds(off, w), everything, r, b_ref) for r, off, w in zip(a_refs, offs, widths)]
        elif na > 1:
            terms = [(everything, everything, r, b_ref.at[pl.ds(off, w), :]) for r, off, w in zip(a_refs, offs, widths)]
        elif not b3:
            terms = [(everything, everything, a_ref, b_ref)]
        elif tb:
            terms = [(everything, everything,
                      a_ref.at[pl.ds(t * pw, pw), :] if ta else a_ref.at[:, pl.ds(t * pw, pw)], b_ref.at[t])
                     for t in range(bk // pw)]
        else:
            terms = [(everything, pl.ds(t * pw, pw), a_ref, b_ref.at[t]) for t in range(bn // pw)]

        def finish(acc):
            outs = (acc,) if epi is None else epi(acc, *[e[...] for e in e_refs])
            for o_ref, o, kind in zip(o_refs, outs, out_kinds):
                if kind == "vsum":
                    @pl.when(i == 0)
                    def _(o_ref=o_ref, o=o):
                        o_ref[...] = o

                    @pl.when(i > 0)
                    def _(o_ref=o_ref, o=o):
                        o_ref[...] += o
                elif out3:
                    for t in range(bn // PIECE):
                        o_ref[t] = o[:, t * PIECE:(t + 1) * PIECE].astype(o_ref.dtype)
                else:
                    o_ref[...] = o.astype(o_ref.dtype)

        if nk == 1:
            bands = {}
            for rows, cols, a_r, b_r in terms:
                key = (getattr(rows, "start", None), getattr(cols, "start", None))
                val = dot(a_r[...], b_r[...])
                bands[key] = val if key not in bands else bands[key] + val
            vals = list(bands.values())
            if len(vals) == 1:
                finish(vals[0])
            else:
                finish(jnp.concatenate(vals, axis=0 if (na > 1 and ta) else 1))
            return

        @pl.when(k == 0)
        def _():
            acc_ref[...] = jnp.zeros_like(acc_ref)

        for rows, cols, a_r, b_r in terms:
            acc_ref[rows, cols] += dot(a_r[...], b_r[...])

        @pl.when(k == nk - 1)
        def _():
            finish(acc_ref[...])

    if na > 1:
        a_specs = [pl.BlockSpec((bk, w), lambda i, j, k: (k, 0)) if ta else pl.BlockSpec((bm, w), lambda i, j, k: (i, 0))
                   for w in widths]
    else:
        a_specs = [pl.BlockSpec((bk, bm), lambda i, j, k: (k, i)) if ta else
                   pl.BlockSpec((bm, bk), lambda i, j, k: (i, k))]
    if b3:
        if tb:
            b_spec = pl.BlockSpec((bk // pw, bn, pw), lambda i, j, k: (k, j, 0))
        else:
            b_spec = pl.BlockSpec((bn // pw, bk, pw), lambda i, j, k: (j, k, 0))
    else:
        b_spec = pl.BlockSpec((bn, bk), lambda i, j, k: (j, k)) if tb else pl.BlockSpec((bk, bn), lambda i, j, k: (k, j))
    e_specs = ([pl.BlockSpec((bm, bn), lambda i, j, k: (i, j)) for _ in extra]
               + [pl.BlockSpec((bm, 1), lambda i, j, k: (i, 0)) for _ in cols]
               + [pl.BlockSpec((1, bn), lambda i, j, k: (0, j)) for _ in vecs])
    if out3:
        o_specs = [pl.BlockSpec((bn // PIECE, bm, PIECE), lambda i, j, k: (j, i, 0)) for _ in out_dtypes]
        o_shapes = [_sds((N // PIECE, M, PIECE), dt) for dt in out_dtypes]
    else:
        spec_of = {"full": pl.BlockSpec((bm, bn), lambda i, j, k: (i, j)),
                   "col": pl.BlockSpec((bm, 1), lambda i, j, k: (i, 0)),
                   "vsum": pl.BlockSpec((1, bn), lambda i, j, k: (0, j))}
        shape_of = {"full": (M, N), "col": (M, 1), "vsum": (1, N)}
        o_specs = [spec_of[kind] for kind in out_kinds]
        o_shapes = [_sds(shape_of[kind], dt) for kind, dt in zip(out_kinds, out_dtypes)]
    assert "col" not in out_kinds or bn == N
    outs = pl.pallas_call(
        body, name=name, grid=(M // bm, N // bn, nk),
        in_specs=a_specs + [b_spec] + e_specs + ([] if dep is None else [ANY]),
        out_specs=o_specs, out_shape=o_shapes,
        scratch_shapes=[pltpu.VMEM((bm, bn), F32)] if use_acc else [],
        compiler_params=_cp("arbitrary" if "vsum" in out_kinds else "parallel", "parallel", "arbitrary"),
    )(*a_list, b, *extra, *cols, *vecs, *([] if dep is None else [dep]))
    return outs[0] if n_out == 1 else outs


def _epi_relu2(acc):
    r = jnp.maximum(acc, 0.0)
    return acc, r * r


def _epi_relu2_bwd(acc, p):
    return (acc * (2.0 * jnp.maximum(p.astype(F32), 0.0)),)


def _row_spec(rb, w=D_MODEL):
    return pl.BlockSpec((rb, w), lambda i: (i, 0))


def _vec_spec(w=D_MODEL):
    return pl.BlockSpec((1, w), lambda i: (0, 0))


def _rstd(v):
    return lax.rsqrt(jnp.mean(v * v, axis=-1, keepdims=True) + EPS)


def _rms_fwd(x, g, name):
    L = x.shape[0]
    rb = min(NORM_ROWS, L)

    def body(x_ref, g_ref, h_ref, r_ref):
        xv = x_ref[...]
        r = _rstd(xv)
        h_ref[...] = (xv * r * g_ref[...]).astype(BF16)
        r_ref[...] = r

    return pl.pallas_call(
        body, name=name, grid=(L // rb,),
        in_specs=[_row_spec(rb), _vec_spec()],
        out_specs=[_row_spec(rb), _row_spec(rb, 1)],
        out_shape=[_sds((L, D_MODEL), BF16), _sds((L, 1))],
        compiler_params=_cp("parallel"),
    )(x, g)


def _rms_bwd_rows(dy, xv, r, g):
    n = xv * r
    dyg = dy * g
    return r * (dyg - n * jnp.mean(dyg * n, axis=-1, keepdims=True)), n


POST_PRE_DTYPES = (F32, F32, BF16, F32, F32)
POST_PRE_KINDS = ("full", "col", "full", "col", "full")
PRE_POST_BWD_DTYPES = (F32, F32, BF16, F32)
PRE_POST_BWD_KINDS = ("full", "vsum", "full", "vsum")


def _epi_post_pre(y, x_in, g_post, g_pre):
    ry = _rstd(y)
    xo = x_in + y * ry * g_post
    rx = _rstd(xo)
    return xo, ry, xo * rx * g_pre, rx, y


def _epi_pre_post_bwd(gh, x, g_out, y_prev, rx, ry_prev, g_pre, g_post_prev):
    gx, n = _rms_bwd_rows(gh, x, rx, g_pre)
    gi = g_out + gx
    gy, ny = _rms_bwd_rows(gi, y_prev, ry_prev, g_post_prev)
    return gi, jnp.sum(gh * n, axis=0, keepdims=True), gy, jnp.sum(gi * ny, axis=0, keepdims=True)


def _epi_pre_bwd(gh, x, g_out, rx, g_pre):
    gx, n = _rms_bwd_rows(gh, x, rx, g_pre)
    return g_out + gx, jnp.sum(gh * n, axis=0, keepdims=True)


def _epi_post_loss(y, x_in, target, g_post):
    ry = _rstd(y)
    diff = x_in + y * ry * g_post - target
    gx = diff * (1.0 / D_MODEL)
    gy, n = _rms_bwd_rows(gx, y, ry, g_post)
    sq = jnp.broadcast_to(jnp.sum(diff * diff, keepdims=True), (1, y.shape[1]))
    return gx, gy, jnp.sum(gx * n, axis=0, keepdims=True), sq


def _cmul(ar, ai, br, bi):
    return ar * br - ai * bi, ar * bi + ai * br


def _zoh_cols(lr, li, ldt):
    dt = jnp.exp(ldt)
    mag = jnp.exp(lr * dt)
    ar = mag * jnp.cos(li * dt)
    ai = mag * jnp.sin(li * dt)
    den = lr * lr + li * li
    nr = ar - 1.0
    qr = (nr * lr + ai * li) / den
    qi = (ai * lr - nr * li) / den
    return dt, ar, ai, qr, qi, den


def _b_mask():
    r = lax.broadcasted_iota(jnp.int32, (S5_NS, LANES), 0)
    c = lax.broadcasted_iota(jnp.int32, (S5_NS, LANES), 1)
    return ((r >> 6) & 7) == (c >> 4)


def _c_mask():
    r = lax.broadcasted_iota(jnp.int32, (S5_W, 512), 0)
    c = lax.broadcasted_iota(jnp.int32, (S5_W, 512), 1)
    return ((r >> 4) & 7) == (c >> 6)


def _s5_prep(lam_r, ldt_r, lam_c, ldt_c, b_t, c_t, name):
    def body(lam_r_ref, ldt_r_ref, lam_c_ref, ldt_c_ref, b_ref, c_ref, tab_ref, bset_ref, cset_ref):
        lr, li = lam_r_ref[0:1, :], lam_r_ref[1:2, :]
        dt = jnp.exp(ldt_r_ref[...])
        mag = jnp.exp(lr * dt)
        p1r, p1i = mag * jnp.cos(li * dt), mag * jnp.sin(li * dt)
        p2r, p2i = _cmul(p1r, p1i, p1r, p1i)
        p3r, p3i = _cmul(p2r, p2i, p1r, p1i)
        p4r, p4i = _cmul(p2r, p2i, p2r, p2i)
        p5r, p5i = _cmul(p4r, p4i, p1r, p1i)
        p6r, p6i = _cmul(p4r, p4i, p2r, p2i)
        p7r, p7i = _cmul(p4r, p4i, p3r, p3i)
        p8r, p8i = _cmul(p4r, p4i, p4r, p4i)
        pw_r = [p1r, p2r, p3r, p4r, p5r, p6r, p7r, p8r]
        pw_i = [p1i, p2i, p3i, p4i, p5i, p6i, p7i, p8i]
        row = lax.broadcasted_iota(jnp.int32, (8, S5_NS), 0)
        zero = jnp.zeros((8, S5_NS), F32)

        def bc(v):
            return jnp.broadcast_to(v, (8, S5_NS))

        for d in range(2):
            sgn = 1.0 if d == 0 else -1.0
            for t, s in enumerate((1, 2, 4)):
                live = (row >= s) if d == 0 else (row <= 7 - s)
                tab_ref[d, 2 * t] = jnp.where(live, bc(pw_r[s - 1]), zero)
                tab_ref[d, 2 * t + 1] = jnp.where(live, bc(sgn * pw_i[s - 1]), zero)
            cr, ci = zero, zero
            for i in range(8):
                e = i if d == 0 else 7 - i
                cr = jnp.where(row == i, bc(pw_r[e]), cr)
                ci = jnp.where(row == i, bc(sgn * pw_i[e]), ci)
            tab_ref[d, 6] = cr
            tab_ref[d, 7] = ci

        _, _, _, qr, qi, _ = _zoh_cols(lam_c_ref[:, 0:1], lam_c_ref[:, 1:2], ldt_c_ref[...])
        bm = _b_mask()
        br, bi = b_ref[0], b_ref[1]
        bset_ref[0] = jnp.where(bm, qr * br - qi * bi, 0.0).astype(BF16)
        bset_ref[1] = jnp.where(bm, qr * bi + qi * br, 0.0).astype(BF16)
        cm = _c_mask()
        cset_ref[0] = jnp.where(cm, c_ref[0], 0.0).astype(BF16)
        cset_ref[1] = jnp.where(cm, c_ref[1], 0.0).astype(BF16)

    vm = pl.BlockSpec(memory_space=pltpu.VMEM)
    return pl.pallas_call(
        body, name=name, in_specs=[vm] * 6, out_specs=[vm] * 3,
        out_shape=[_sds((2, 8, 8, S5_NS)), _sds((2, S5_NS, LANES), BF16), _sds((2, S5_W, 512), BF16)],
        compiler_params=pltpu.CompilerParams(vmem_limit_bytes=VMEM_LIMIT),
    )(lam_r, ldt_r, lam_c, ldt_c, b_t, c_t)


SCAN_W = SCAN_GROUPS * LANES


def _scan_chunk(src_ref, dst_ref, tab_ref, carry_ref, nb, reverse, xs_ref=None, acc_ref=None):
    row = lax.broadcasted_iota(jnp.int32, (8, LANES), 0)

    def step(i, carry):
        b = (nb - 1 - i) if reverse else i
        off = pl.multiple_of(b * 8, 8)
        out = []
        for g in range(SCAN_GROUPS):
            lanes = pl.ds(g * LANES, LANES)
            cr, ci = carry[2 * g], carry[2 * g + 1]
            yr = src_ref[0, pl.ds(off, 8), lanes]
            yi = src_ref[1, pl.ds(off, 8), lanes]
            for t, s in enumerate((1, 2, 4)):
                sh = (8 - s) if reverse else s
                sr = pltpu.roll(yr, sh, 0)
                si = pltpu.roll(yi, sh, 0)
                mr, mi = tab_ref[2 * t, :, lanes], tab_ref[2 * t + 1, :, lanes]
                yr, yi = yr + mr * sr - mi * si, yi + mr * si + mi * sr
            pr, pi = tab_ref[6, :, lanes], tab_ref[7, :, lanes]
            yr, yi = yr + pr * cr - pi * ci, yi + pr * ci + pi * cr
            dst_ref[0, pl.ds(off, 8), lanes] = yr
            dst_ref[1, pl.ds(off, 8), lanes] = yi
            if xs_ref is not None:
                nr = jnp.where(row == 7, cr, pltpu.roll(yr, 7, 0))
                ni = jnp.where(row == 7, ci, pltpu.roll(yi, 7, 0))
                xr = xs_ref[0, pl.ds(off, 8), lanes]
                xi = xs_ref[1, pl.ds(off, 8), lanes]
                acc_ref[0, :, lanes] += xr * nr + xi * ni
                acc_ref[1, :, lanes] += xr * ni - xi * nr
            last = 0 if reverse else 7
            out += [jnp.broadcast_to(yr[last:last + 1, :], (8, LANES)),
                    jnp.broadcast_to(yi[last:last + 1, :], (8, LANES))]
        return tuple(out)

    init = []
    for g in range(SCAN_GROUPS):
        init += [carry_ref[0, :, pl.ds(g * LANES, LANES)], carry_ref[1, :, pl.ds(g * LANES, LANES)]]
    fin = lax.fori_loop(0, nb, step, tuple(init))
    for g in range(SCAN_GROUPS):
        carry_ref[0, :, pl.ds(g * LANES, LANES)] = fin[2 * g]
        carry_ref[1, :, pl.ds(g * LANES, LANES)] = fin[2 * g + 1]


def _s5_scan_fwd(z, bset, cset, dvec, tabs, name):
    L = z.shape[0]
    tl = min(SCAN_CHUNK, L)
    nc = L // tl

    def body(u_ref, b_ref, c_ref, d_ref, tab_ref, x_ref, y_ref, carry_ref):
        @pl.when(pl.program_id(1) == 0)
        def _():
            carry_ref[...] = jnp.zeros_like(carry_ref)

        uf = u_ref[...]
        u = uf.astype(BF16)
        x_ref[0] = _dot(u, b_ref[0], NT)
        x_ref[1] = _dot(u, b_ref[1], NT)
        _scan_chunk(x_ref, x_ref, tab_ref, carry_ref, tl // 8, False)
        y_ref[...] = (_dot(x_ref[0].astype(BF16), c_ref[0], NT) - _dot(x_ref[1].astype(BF16), c_ref[1], NT)
                      + d_ref[...] * uf)

    col = pl.BlockSpec((tl, LANES), lambda j, c: (c, j))
    return pl.pallas_call(
        body, name=name, grid=(S5_NS // SCAN_W, nc),
        in_specs=[col, pl.BlockSpec((2, SCAN_W, LANES), lambda j, c: (0, j, 0)),
                  pl.BlockSpec((2, LANES, SCAN_W), lambda j, c: (0, j, 0)),
                  pl.BlockSpec((1, LANES), lambda j, c: (0, j)),
                  pl.BlockSpec((None, 8, 8, SCAN_W), lambda j, c: (0, 0, 0, j))],
        out_specs=[pl.BlockSpec((2, tl, SCAN_W), lambda j, c: (0, c, j)), col],
        out_shape=[_sds((2, L, S5_NS)), _sds((L, S5_W))],
        scratch_shapes=[pltpu.VMEM((2, 8, SCAN_W), F32)],
        compiler_params=_cp("parallel", "arbitrary"),
    )(z, bset, cset, dvec, tabs)


def _s5_scan_bwd(gyl, cset, xs, z, bset, gud, tabs, name):
    L = z.shape[0]
    tl = min(SCAN_CHUNK, L)
    nc = L // tl

    def body(g_ref, c_ref, xs_ref, u_ref, b_ref, gud_ref, tab_ref, gu_ref, ga_ref, gb_ref, gc_ref,
             gx_ref, carry_ref, acc_ref):
        c = pl.program_id(1)

        @pl.when(c == 0)
        def _():
            carry_ref[...] = jnp.zeros_like(carry_ref)
            acc_ref[...] = jnp.zeros_like(acc_ref)
            gb_ref[...] = jnp.zeros_like(gb_ref)
            gc_ref[...] = jnp.zeros_like(gc_ref)

        gy = g_ref[...].astype(BF16)
        gx_ref[0] = _dot(gy, c_ref[0])
        gx_ref[1] = -_dot(gy, c_ref[1])
        gc_ref[0] += _dot(gy, xs_ref[0].astype(BF16), TN)
        gc_ref[1] -= _dot(gy, xs_ref[1].astype(BF16), TN)
        _scan_chunk(gx_ref, gx_ref, tab_ref, carry_ref, tl // 8, True, xs_ref, acc_ref)
        gr = gx_ref[0].astype(BF16)
        gi = gx_ref[1].astype(BF16)
        gu_ref[...] = gud_ref[...] + _dot(gr, b_ref[0]) + _dot(gi, b_ref[1])
        u = u_ref[...].astype(BF16)
        gb_ref[0] += _dot(gr, u, TN)
        gb_ref[1] += _dot(gi, u, TN)

        @pl.when(c == nc - 1)
        def _():
            ga_ref[0:1, :] = jnp.sum(acc_ref[0], axis=0, keepdims=True)
            ga_ref[1:2, :] = jnp.sum(acc_ref[1], axis=0, keepdims=True)

    rev = lambda j, c: (nc - 1 - c, j)
    col = pl.BlockSpec((tl, LANES), rev)
    return pl.pallas_call(
        body, name=name, grid=(S5_NS // SCAN_W, nc),
        in_specs=[col, pl.BlockSpec((2, LANES, SCAN_W), lambda j, c: (0, j, 0)),
                  pl.BlockSpec((2, tl, SCAN_W), lambda j, c: (0, nc - 1 - c, j)), col,
                  pl.BlockSpec((2, SCAN_W, LANES), lambda j, c: (0, j, 0)), col,
                  pl.BlockSpec((None, 8, 8, SCAN_W), lambda j, c: (1, 0, 0, j))],
        out_specs=[col, pl.BlockSpec((2, SCAN_W), lambda j, c: (0, j)),
                   pl.BlockSpec((2, SCAN_W, LANES), lambda j, c: (0, j, 0)),
                   pl.BlockSpec((2, LANES, SCAN_W), lambda j, c: (0, j, 0))],
        out_shape=[_sds((L, S5_W)), _sds((2, S5_NS)), _sds((2, S5_NS, LANES)), _sds((2, S5_W, 512))],
        scratch_shapes=[pltpu.VMEM((2, tl, SCAN_W), F32), pltpu.VMEM((2, 8, SCAN_W), F32),
                        pltpu.VMEM((2, 8, SCAN_W), F32)],
        compiler_params=_cp("parallel", "arbitrary"),
    )(gyl, cset, xs, z, bset, gud, tabs)


def _s5_glu_fwd(ylin, wglu, name):
    L = ylin.shape[0]
    bl = min(1024, L)

    def body(ylin_ref, w_ref, ya_ref):
        yg = _gelu(ylin_ref[...])
        t = _dot(yg.astype(BF16), w_ref[...])
        ya_ref[...] = (yg * _sigmoid(t)).astype(BF16)

    return pl.pallas_call(
        body, name=name, grid=(L // bl,),
        in_specs=[pl.BlockSpec((bl, S5_W), lambda i: (i, 0)), pl.BlockSpec((S5_W, S5_W), lambda i: (0, 0))],
        out_specs=pl.BlockSpec((bl, S5_W), lambda i: (i, 0)),
        out_shape=_sds((L, S5_W), BF16),
        compiler_params=_cp("parallel"),
    )(ylin, wglu)


def _s5_glu_bwd(g_y, wout, ylin, z, dvec, wglu, name):
    L = z.shape[0]
    bl = min(256, L)

    def body(g_ref, wo_ref, ylin_ref, u_ref, d_ref, w_ref, gyl_ref, gud_ref, gw_ref, gd_ref):
        i = pl.program_id(0)
        ylin = ylin_ref[...]
        yg = _gelu(ylin)
        ygb = yg.astype(BF16)
        sg = _sigmoid(_dot(ygb, w_ref[...]))
        gya = _dot(g_ref[...], wo_ref[...], NT)
        gt = gya * yg * sg * (1.0 - sg)
        gtb = gt.astype(BF16)
        gyg = gya * sg + _dot(gtb, w_ref[...], NT)
        gyl = gyg * _gelu_grad(ylin)
        gyl_ref[...] = gyl
        gud_ref[...] = gyl * d_ref[...]

        @pl.when(i == 0)
        def _():
            gw_ref[...] = jnp.zeros_like(gw_ref)
            gd_ref[...] = jnp.zeros_like(gd_ref)

        gw_ref[...] += _dot(ygb, gtb, TN)
        gd_ref[...] += jnp.sum(gyl * u_ref[...], axis=0, keepdims=True)

    blk = pl.BlockSpec((bl, S5_W), lambda i: (i, 0))
    return pl.pallas_call(
        body, name=name, grid=(L // bl,),
        in_specs=[pl.BlockSpec((bl, D_MODEL), lambda i: (i, 0)), pl.BlockSpec((S5_W, D_MODEL), lambda i: (0, 0)),
                  blk, blk, pl.BlockSpec((1, S5_W), lambda i: (0, 0)), pl.BlockSpec((S5_W, S5_W), lambda i: (0, 0))],
        out_specs=[blk, blk, pl.BlockSpec((S5_W, S5_W), lambda i: (0, 0)), pl.BlockSpec((1, S5_W), lambda i: (0, 0))],
        out_shape=[_sds((L, S5_W)), _sds((L, S5_W)), _sds((S5_W, S5_W)), _sds((1, S5_W))],
        compiler_params=_cp("arbitrary"),
    )(g_y, wout, ylin, z, dvec, wglu)


def _s5_param_bwd(lam_c, ldt_c, b_t, gb, ga_c, gc, name):
    def body(lam_ref, ldt_ref, b_ref, gb_ref, ga_ref, gc_ref, glam_ref, gldt_ref, gbo_ref, gco_ref):
        lr, li = lam_ref[:, 0:1], lam_ref[:, 1:2]
        dt, ar, ai, qr, qi, den = _zoh_cols(lr, li, ldt_ref[...])
        bm = _b_mask()
        gbr = jnp.where(bm, gb_ref[0], 0.0)
        gbi = jnp.where(bm, gb_ref[1], 0.0)
        br, bi = b_ref[0], b_ref[1]
        obr = gbr * qr + gbi * qi
        obi = gbi * qr - gbr * qi
        gqr = jnp.sum(gbr * br + gbi * bi, axis=1, keepdims=True)
        gqi = jnp.sum(gbi * br - gbr * bi, axis=1, keepdims=True)
        for s in (64, 32, 16):
            obr = obr + pltpu.roll(obr, s, 1)
            obi = obi + pltpu.roll(obi, s, 1)
        gbo_ref[0] = obr
        gbo_ref[1] = obi
        gar = ga_ref[:, 0:1] + (gqr * lr - gqi * li) / den
        gai = ga_ref[:, 1:2] + (gqr * li + gqi * lr) / den
        qlr = (qr * lr + qi * li) / den
        qli = (qi * lr - qr * li) / den
        glr = -(gqr * qlr + gqi * qli)
        gli = -(gqi * qlr - gqr * qli)
        glr = glr + dt * (gar * ar + gai * ai)
        gli = gli + dt * (gai * ar - gar * ai)
        wr, wi = _cmul(lr, li, ar, ai)
        gldt = (gar * wr + gai * wi) * dt
        glam_ref[:, 0:1] = glr
        glam_ref[:, 1:2] = gli
        r = lax.broadcasted_iota(jnp.int32, (S5_NS, 32), 0)
        c = lax.broadcasted_iota(jnp.int32, (S5_NS, 32), 1)
        gldt_ref[...] = jnp.sum(jnp.where((r >> 6) == c, gldt, 0.0), axis=0, keepdims=True)
        cm = _c_mask()
        for k in range(2):
            oc = jnp.where(cm, gc_ref[k], 0.0)
            for s in (256, 128, 64):
                oc = oc + pltpu.roll(oc, s, 1)
            gco_ref[k] = oc[:, 0:LANES]

    vm = pl.BlockSpec(memory_space=pltpu.VMEM)
    return pl.pallas_call(
        body, name=name, in_specs=[vm] * 6, out_specs=[vm] * 4,
        out_shape=[_sds((S5_NS, 2)), _sds((1, 32)), _sds((2, S5_NS, LANES)), _sds((2, S5_W, LANES))],
        compiler_params=pltpu.CompilerParams(vmem_limit_bytes=VMEM_LIMIT),
    )(lam_c, ldt_c, b_t, gb, ga_c, gc)


FL_BLK = EVEN_PAD // LANES - 1
Q_BLK, K_BLK, V_BLK = 4, 8, 12
NEG = -1e30


def _log_sigmoid(v):
    return jnp.minimum(v, 0.0) - jnp.log(1.0 + jnp.exp(-jnp.abs(v)))


def _fox_f_fwd(z, bf, name):
    L = z.shape[0]

    def body(fl_ref, b_ref, f_ref, fq_ref):
        row = lax.broadcasted_iota(jnp.int32, (L, LANES), 0)
        cs = _cumsum_rows(_log_sigmoid(fl_ref[...] + b_ref[...]), True, row)
        f_ref[...] = cs
        expand = (lax.broadcasted_iota(jnp.int32, (LANES, FOX_W), 0)
                  == (lax.broadcasted_iota(jnp.int32, (LANES, FOX_W), 1) >> 6)).astype(F32)
        fq_ref[...] = lax.dot_general(cs, expand, NN, precision=lax.Precision.HIGHEST, preferred_element_type=F32)

    return pl.pallas_call(
        body, name=name, grid=(1,),
        in_specs=[pl.BlockSpec((L, LANES), lambda i: (0, FL_BLK)), pl.BlockSpec((1, LANES), lambda i: (0, 0))],
        out_specs=[pl.BlockSpec((L, LANES), lambda i: (0, 0)), pl.BlockSpec((L, FOX_W), lambda i: (0, 0))],
        out_shape=[_sds((L, LANES)), _sds((L, FOX_W))],
        compiler_params=_cp("arbitrary"),
    )(z, bf)


def _fox_f_bwd(dFk, dfq, z, bf, name):
    L = z.shape[0]

    def body(dfk_ref, dfq_ref, fl_ref, b_ref, dfl_ref, db_ref):
        sel = (lax.broadcasted_iota(jnp.int32, (FOX_W, LANES), 0)
               == 64 * lax.broadcasted_iota(jnp.int32, (FOX_W, LANES), 1)).astype(F32)
        dfq_h = lax.dot_general(dfq_ref[...], sel, NN, precision=lax.Precision.HIGHEST, preferred_element_type=F32)
        row = lax.broadcasted_iota(jnp.int32, (L, LANES), 0)
        cs = _cumsum_rows(dfk_ref[...] + dfq_h, False, row)
        dfl = cs * _sigmoid(-(fl_ref[...] + b_ref[...]))
        dfl_ref[...] = dfl
        db_ref[...] = jnp.sum(dfl, axis=0, keepdims=True)

    return pl.pallas_call(
        body, name=name, grid=(1,),
        in_specs=[pl.BlockSpec((L, LANES), lambda i: (0, 0)), pl.BlockSpec((L, FOX_W), lambda i: (0, 0)),
                  pl.BlockSpec((L, LANES), lambda i: (0, FL_BLK)), pl.BlockSpec((1, LANES), lambda i: (0, 0))],
        out_specs=[pl.BlockSpec((L, LANES), lambda i: (0, 0)), pl.BlockSpec((1, LANES), lambda i: (0, 0))],
        out_shape=[_sds((L, LANES)), _sds((1, LANES))],
        compiler_params=_cp("arbitrary"),
    )(dFk, dfq, z, bf)


def _head_mask(hh):
    lane = lax.broadcasted_iota(jnp.int32, (1, LANES), 1)
    return (lane >> 6) == hh


FOX_T = 512


def _fox_head(x, hh):
    return jnp.where(_head_mask(hh), x, 0.0).astype(BF16)


def _fox_scores(qh, k, fq_ref, fr_ref, hh, causal):
    if fq_ref is None:
        s = _dot(qh, k, NT) - fr_ref[hh:hh + 1, :]
    else:
        s = _dot(qh, k, NT) + (fq_ref[:, 64 * hh:64 * hh + 1] - fr_ref[hh:hh + 1, :])
    return s if causal is None else jnp.where(causal, s, NEG)


def _causal(T):
    return lax.broadcasted_iota(jnp.int32, (T, T), 1) <= lax.broadcasted_iota(jnp.int32, (T, T), 0)


def _fox_fwd(z, fq, frow, name):
    L = z.shape[0]
    T = min(FOX_T, L)
    nq = L // T

    def body(qt_ref, kt_ref, q_ref, k_ref, v_ref, fq_ref, fr_ref, o_ref, lse_ref, m_ref, l_ref, acc_ref):
        t = pl.program_id(1)
        qi, ki = qt_ref[t], kt_ref[t]

        @pl.when(ki == 0)
        def _():
            m_ref[...] = jnp.full_like(m_ref, NEG)
            l_ref[...] = jnp.zeros_like(l_ref)
            acc_ref[...] = jnp.zeros_like(acc_ref)

        def step(diagonal):
            q = q_ref[...] * 0.125
            k = k_ref[...].astype(BF16)
            v = v_ref[...].astype(BF16)
            causal = _causal(T) if diagonal else None
            s = jnp.concatenate([_fox_scores(_fox_head(q, hh), k, fq_ref, fr_ref, hh, causal) for hh in range(2)],
                                axis=0)
            m_old = m_ref[...]
            m_new = jnp.maximum(m_old, jnp.max(s, axis=1, keepdims=True))
            alpha = jnp.exp(m_old - m_new)
            p = jnp.exp(s - m_new)
            l_ref[...] = alpha * l_ref[...] + jnp.sum(p, axis=1, keepdims=True)
            m_ref[...] = m_new
            acc_ref[...] = alpha * acc_ref[...] + _dot(p.astype(BF16), v)

        @pl.when(ki < qi)
        def _():
            step(False)

        @pl.when(ki == qi)
        def _():
            step(True)
            h0 = _head_mask(0)
            l = l_ref[...]
            o_h = acc_ref[...] / l
            lse_h = m_ref[...] + jnp.log(l)
            o_ref[...] = jnp.where(h0, o_h[:T], o_h[T:])
            lse_ref[...] = jnp.where(h0, lse_h[:T], lse_h[T:]) - fq_ref[...]

    pairs = [(qi, ki) for qi in range(nq) for ki in range(qi + 1)]
    qt = jnp.asarray([p[0] for p in pairs], jnp.int32)
    kt = jnp.asarray([p[1] for p in pairs], jnp.int32)

    def qspec(base):
        return pl.BlockSpec((T, LANES), lambda j, t, qt, kt: (qt[t], base + j))

    def kspec(base):
        return pl.BlockSpec((T, LANES), lambda j, t, qt, kt: (kt[t], base + j))

    return pl.pallas_call(
        body, name=name,
        grid_spec=pltpu.PrefetchScalarGridSpec(
            num_scalar_prefetch=2, grid=(4, len(pairs)),
            in_specs=[qspec(Q_BLK), kspec(K_BLK), kspec(V_BLK), qspec(0),
                      pl.BlockSpec((None, 2, T), lambda j, t, qt, kt: (j, 0, kt[t]))],
            out_specs=[qspec(0), qspec(0)],
            scratch_shapes=[pltpu.VMEM((2 * T, 1), F32), pltpu.VMEM((2 * T, 1), F32),
                            pltpu.VMEM((2 * T, LANES), F32)]),
        out_shape=[_sds((L, FOX_W)), _sds((L, FOX_W))],
        compiler_params=_cp("parallel", "arbitrary"),
    )(qt, kt, z, z, z, fq, frow)


def _fox_bwd(z, frow, o, lse, g_m, name):
    L = z.shape[0]
    T = min(FOX_T, L)
    nq = L // T

    pairs = [(qi, ki) for ki in range(nq) for qi in range(ki, nq)]
    qt = jnp.asarray([p[0] for p in pairs], jnp.int32)
    kt = jnp.asarray([p[1] for p in pairs], jnp.int32)

    def body(qt_ref, kt_ref, q_ref, k_ref, v_ref, fr_ref, o_ref, lse_ref, do_ref,
             dq_ref, dk_ref, dv_ref, dfq_ref, dfk_ref, dk_acc, dv_acc, df_acc):
        t = pl.program_id(1)
        qi, ki = qt_ref[t], kt_ref[t]

        @pl.when(t == 0)
        def _():
            dq_ref[...] = jnp.zeros_like(dq_ref)
            dfq_ref[...] = jnp.zeros_like(dfq_ref)

        @pl.when(qi == ki)
        def _():
            dk_acc[...] = jnp.zeros_like(dk_acc)
            dv_acc[...] = jnp.zeros_like(dv_acc)
            df_acc[...] = jnp.zeros_like(df_acc)

        def step(diagonal):
            q = q_ref[...] * 0.125
            qb = q.astype(BF16)
            k = k_ref[...].astype(BF16)
            v = v_ref[...].astype(BF16)
            do = do_ref[...]
            dob = do.astype(BF16)
            do_o = dob.astype(F32) * o_ref[...]
            causal = _causal(T) if diagonal else None
            dvs, dks, dqs, rss = [], [], [], []
            for hh in range(2):
                s = _fox_scores(_fox_head(q, hh), k, None, fr_ref, hh, causal)
                p = jnp.exp(s - lse_ref[:, 64 * hh:64 * hh + 1])
                dp = _dot(_fox_head(do, hh), v, NT)
                delta = jnp.sum(jnp.where(_head_mask(hh), do_o, 0.0), axis=1, keepdims=True)
                ds = p * (dp - delta)
                dsb = ds.astype(BF16)
                dvs.append(_dot(p.astype(BF16), dob, TN))
                dks.append(_dot(dsb, qb, TN))
                dqs.append(_dot(dsb, k))
                rss.append(jnp.sum(ds, axis=1, keepdims=True))
                df_acc[hh:hh + 1, :] -= jnp.sum(ds, axis=0, keepdims=True)
            h0 = _head_mask(0)
            dv_acc[...] += jnp.where(h0, dvs[0], dvs[1])
            dk_acc[...] += jnp.where(h0, dks[0], dks[1])
            rows = pl.ds(pl.multiple_of(qi * T, T), T)
            dq_ref[rows, :] += jnp.where(h0, dqs[0], dqs[1])
            dfq_ref[rows, :] += jnp.where(h0, rss[0], rss[1])

        @pl.when(qi > ki)
        def _():
            step(False)

        @pl.when(qi == ki)
        def _():
            step(True)

        @pl.when(qi == nq - 1)
        def _():
            dk_ref[...] = dk_acc[...]
            dv_ref[...] = dv_acc[...]
            dfk_ref[...] = df_acc[...]

        @pl.when(t == len(pairs) - 1)
        def _():
            dq_ref[...] = dq_ref[...] * 0.125

    def qside(base):
        return pl.BlockSpec((T, LANES), lambda j, t, qt, kt: (qt[t], base + j))

    def kside(base):
        return pl.BlockSpec((T, LANES), lambda j, t, qt, kt: (kt[t], base + j))

    pair = pl.BlockSpec((L, LANES), lambda j, t, qt, kt: (0, j))
    frow_spec = pl.BlockSpec((None, 2, T), lambda j, t, qt, kt: (j, 0, kt[t]))
    return pl.pallas_call(
        body, name=name,
        grid_spec=pltpu.PrefetchScalarGridSpec(
            num_scalar_prefetch=2, grid=(4, len(pairs)),
            in_specs=[qside(Q_BLK), kside(K_BLK), kside(V_BLK), frow_spec, qside(0), qside(0), qside(0)],
            out_specs=[pair, kside(0), kside(0), pair, frow_spec],
            scratch_shapes=[pltpu.VMEM((T, LANES), F32), pltpu.VMEM((T, LANES), F32), pltpu.VMEM((2, T), F32)]),
        out_shape=[_sds((L, FOX_W)), _sds((L, FOX_W)), _sds((L, FOX_W)), _sds((L, FOX_W)), _sds((4, 2, L))],
        compiler_params=_cp("parallel", "arbitrary"),
    )(qt, kt, z, z, z, frow, o, lse, g_m)


def _shift_rows(v, s, down, row):
    n = v.shape[0]
    if down:
        return jnp.where(row >= s, pltpu.roll(v, s, 0), 0.0)
    return jnp.where(row < n - s, pltpu.roll(v, n - s, 0), 0.0)


def _cumsum_rows(v, down, row):
    s = 1
    while s < v.shape[0]:
        v = v + _shift_rows(v, s, down, row)
        s *= 2
    return v


def _window_sum(v, g, down, row):
    out = jnp.zeros_like(v)
    s = v
    for k in range(4):
        s = s + _shift_rows(s, 1 << k, down, row)
        out = jnp.where(g == k, s, out)
    return out


def _pool_inv_cnt(g, row):
    w = jnp.left_shift(2, g).astype(F32)
    return 1.0 / jnp.minimum(row.astype(F32) + 1.0, w)


def _pool_fwd(z, pool_w, scale, name):
    L = z.shape[0]

    def body(x_ref, w_ref, s_ref, y_ref, p_ref):
        g = pl.program_id(0)
        row = lax.broadcasted_iota(jnp.int32, (L, LANES), 0)
        x = x_ref[...]
        pooled = (_window_sum(x, g, True, row) * _pool_inv_cnt(g, row) - x).astype(BF16)
        p_ref[...] = pooled
        y_ref[...] = (_dot(pooled, w_ref[...].astype(BF16)) * s_ref[...]).astype(BF16)

    col = pl.BlockSpec((L, LANES), lambda g: (0, g))
    return pl.pallas_call(
        body, name=name, grid=(4,),
        in_specs=[col, pl.BlockSpec((None, LANES, LANES), lambda g: (g, 0, 0)), pl.BlockSpec((1, LANES), lambda g: (0, g))],
        out_specs=[col, col],
        out_shape=[_sds((L, 512), BF16), _sds((L, 512), BF16)],
        compiler_params=_cp("parallel"),
    )(z, pool_w, scale)


def _pool_bwd(g_y, wout, pooled, pool_w, scale, name):
    L = g_y.shape[0]

    def body(g_ref, wo_ref, p_ref, w_ref, s_ref, gx_ref, gw_ref, gs_ref):
        g = pl.program_id(0)
        row = lax.broadcasted_iota(jnp.int32, (L, LANES), 0)
        gy = _dot(g_ref[...], wo_ref[...], NT)
        pooled = p_ref[...]
        wb = w_ref[...].astype(BF16)
        lin = _dot(pooled, wb)
        gs_ref[...] = jnp.sum(gy * lin, axis=0, keepdims=True)
        glin = (gy * s_ref[...]).astype(BF16)
        gw_ref[...] = _dot(pooled, glin, TN)
        gp = _dot(glin, wb, NT)
        gx_ref[...] = _window_sum(gp * _pool_inv_cnt(g, row), g, False, row) - gp

    col = pl.BlockSpec((L, LANES), lambda g: (0, g))
    wspec = pl.BlockSpec((None, LANES, LANES), lambda g: (g, 0, 0))
    vec = pl.BlockSpec((1, LANES), lambda g: (0, g))
    return pl.pallas_call(
        body, name=name, grid=(4,),
        in_specs=[pl.BlockSpec((L, D_MODEL), lambda g: (0, 0)), pl.BlockSpec((LANES, D_MODEL), lambda g: (g, 0)),
                  col, wspec, vec],
        out_specs=[col, wspec, vec],
        out_shape=[_sds((L, 512)), _sds((4, LANES, LANES)), _sds((1, 512))],
        compiler_params=_cp("parallel"),
    )(g_y, wout, pooled, pool_w, scale)


SGU_CHUNKS = 4


def _sgu_ln(v, gam, bet):
    gv = _gelu(v)
    mu = jnp.mean(gv, axis=-1, keepdims=True)
    xc = gv - mu
    rs = lax.rsqrt(jnp.mean(xc * xc, axis=-1, keepdims=True) + EPS)
    xh = xc * rs
    return xh, rs, xh * gam + bet


def _tril_ws(w_ref, g):
    r = lax.broadcasted_iota(jnp.int32, (LANES, LANES), 0)
    c = lax.broadcasted_iota(jnp.int32, (LANES, LANES), 1)
    return jnp.where(r >= c, w_ref[g], 0.0).astype(BF16)


def _sgu_fwd(z, ln_g, ln_b, w_s, b_st, name):
    L = z.shape[0]
    rb = min(SGU_CHUNKS * LANES, L)

    def body(u_ref, v_ref, g_ref, b_ref, w_ref, bs_ref, y_ref):
        _, _, vln = _sgu_ln(v_ref[...], g_ref[...], b_ref[...])
        gu = _gelu(u_ref[...])
        vb = vln.astype(BF16)
        for g in range(4):
            ws = _tril_ws(w_ref, g)
            for n in range(rb // LANES):
                rows = slice(n * LANES, (n + 1) * LANES)
                cols = slice(g * LANES, (g + 1) * LANES)
                mixed = _dot(ws, vb[rows, cols]) + bs_ref[:, g:g + 1]
                y_ref[rows, cols] = (gu[rows, cols] * mixed).astype(BF16)

    vm = lambda shape: pl.BlockSpec(shape, lambda i: tuple(0 for _ in shape))
    return pl.pallas_call(
        body, name=name, grid=(L // rb,),
        in_specs=[pl.BlockSpec((rb, 512), lambda i: (i, 1)), pl.BlockSpec((rb, 512), lambda i: (i, 2)),
                  vm((1, 512)), vm((1, 512)), vm((4, LANES, LANES)), vm((LANES, 4))],
        out_specs=pl.BlockSpec((rb, 512), lambda i: (i, 0)),
        out_shape=_sds((L, 512), BF16),
        compiler_params=_cp("parallel"),
    )(z, z, ln_g, ln_b, w_s, b_st)


def _sgu_bwd(g_y, wout, z, ln_g, ln_b, w_s, b_st, name):
    L = z.shape[0]
    rb = min(SGU_CHUNKS * LANES, L)

    def body(gyo_ref, wo_ref, u_ref, v_ref, g_ref, b_ref, w_ref, bs_ref, gu_ref, gv_ref, gw_ref, gbs_ref, gg_ref,
             gb_ref):
        i = pl.program_id(0)

        @pl.when(i == 0)
        def _():
            gw_ref[...] = jnp.zeros_like(gw_ref)
            gbs_ref[...] = jnp.zeros_like(gbs_ref)
            gg_ref[...] = jnp.zeros_like(gg_ref)
            gb_ref[...] = jnp.zeros_like(gb_ref)

        v = v_ref[...]
        u = u_ref[...]
        gy = _dot(gyo_ref[...], wo_ref[...], NT)
        xh, rs, vln = _sgu_ln(v, g_ref[...], b_ref[...])
        gel_u = _gelu(u)
        gmix = gy * gel_u
        vb = vln.astype(BF16)
        gmb = gmix.astype(BF16)
        r = lax.broadcasted_iota(jnp.int32, (LANES, LANES), 0)
        c = lax.broadcasted_iota(jnp.int32, (LANES, LANES), 1)
        gvln_cols = []
        for g in range(4):
            ws = _tril_ws(w_ref, g)
            cols = slice(g * LANES, (g + 1) * LANES)
            gw = jnp.zeros((LANES, LANES), F32)
            gbs = jnp.zeros((LANES, 1), F32)
            parts = []
            for n in range(rb // LANES):
                rows = slice(n * LANES, (n + 1) * LANES)
                mixed = _dot(ws, vb[rows, cols]) + bs_ref[:, g:g + 1]
                gu_ref[rows, cols] = gy[rows, cols] * mixed * _gelu_grad(u[rows, cols])
                parts.append(_dot(ws, gmb[rows, cols], TN))
                gw = gw + _dot(gmb[rows, cols], vb[rows, cols], NT)
                gbs = gbs + jnp.sum(gmix[rows, cols], axis=1, keepdims=True)
            gvln_cols.append(jnp.concatenate(parts, axis=0))
            gw_ref[g] += jnp.where(r >= c, gw, 0.0)
            gbs_ref[:, g:g + 1] += gbs
        gvln = jnp.concatenate(gvln_cols, axis=1)
        gg_ref[...] += jnp.sum(gvln * xh, axis=0, keepdims=True)
        gb_ref[...] += jnp.sum(gvln, axis=0, keepdims=True)
        gxh = gvln * g_ref[...]
        ggv = rs * (gxh - jnp.mean(gxh, axis=-1, keepdims=True) - xh * jnp.mean(gxh * xh, axis=-1, keepdims=True))
        gv_ref[...] = ggv * _gelu_grad(v)

    vm = lambda shape: pl.BlockSpec(shape, lambda i: tuple(0 for _ in shape))
    blk = pl.BlockSpec((rb, 512), lambda i: (i, 0))
    return pl.pallas_call(
        body, name=name, grid=(L // rb,),
        in_specs=[pl.BlockSpec((rb, D_MODEL), lambda i: (i, 0)), pl.BlockSpec((512, D_MODEL), lambda i: (1, 0)),
                  pl.BlockSpec((rb, 512), lambda i: (i, 1)), pl.BlockSpec((rb, 512), lambda i: (i, 2)),
                  vm((1, 512)), vm((1, 512)), vm((4, LANES, LANES)), vm((LANES, 4))],
        out_specs=[blk, blk, vm((4, LANES, LANES)), vm((LANES, 4)), vm((1, 512)), vm((1, 512))],
        out_shape=[_sds((L, 512)), _sds((L, 512)), _sds((4, LANES, LANES)), _sds((LANES, 4)),
                   _sds((1, 512)), _sds((1, 512))],
        compiler_params=_cp("arbitrary"),
    )(g_y, wout, z, z, ln_g, ln_b, w_s, b_st)


def _adamw_math(w, g, m, v):
    nm = ADAM_B1 * m + (1.0 - ADAM_B1) * g
    nv = ADAM_B2 * v + (1.0 - ADAM_B2) * (g * g)
    m_hat = nm / (1.0 - ADAM_B1 ** ADAM_STEP)
    v_hat = nv / (1.0 - ADAM_B2 ** ADAM_STEP)
    delta = -ADAM_LR * (m_hat / (jnp.sqrt(v_hat) + ADAM_EPS) + ADAM_WD * w)
    return delta, nm, nv


def _sum_adamw(parts, w, m, v, name, layer=0, prev=None):
    n_layers, R, C = w.shape
    rb = 128 if R % 128 == 0 else R

    def body(p_ref, w_ref, m_ref, v_ref, *rest):
        g_ref, d_ref, nm_ref, nv_ref = rest[-4:]
        g = p_ref[0].astype(F32)
        for s in range(1, N_DEV):
            g = g + p_ref[s].astype(F32)
        d, nm, nv = _adamw_math(w_ref[...], g, m_ref[...], v_ref[...])
        g_ref[...] = g
        d_ref[...] = d
        nm_ref[...] = nm
        nv_ref[...] = nv

    blk = pl.BlockSpec((None, rb, C), lambda i: (layer, i, 0))
    prev = [] if prev is None else list(prev)
    return pl.pallas_call(
        body, name=name, grid=(R // rb,),
        in_specs=[pl.BlockSpec((N_DEV, rb, C), lambda i: (0, i, 0)), blk, blk, blk] + [ANY] * len(prev),
        out_specs=[blk] * 4, out_shape=[_sds((n_layers, R, C))] * 4,
        input_output_aliases={4 + k: k for k in range(len(prev))},
        compiler_params=_cp("parallel"),
    )(parts, w, m, v, *prev)


def _sum_pieces(parts, name):
    _, R, C = parts.shape

    def body(p_ref, g_ref):
        g = p_ref[0].astype(F32)
        for s in range(1, N_DEV):
            g = g + p_ref[s].astype(F32)
        g_ref[...] = g

    vm = pl.BlockSpec(memory_space=pltpu.VMEM)
    return pl.pallas_call(body, name=name, in_specs=[vm], out_specs=vm, out_shape=_sds((R, C)),
                          compiler_params=pltpu.CompilerParams(vmem_limit_bytes=VMEM_LIMIT))(parts)


def _adamw_many(ws, gs, ms, vs, name):
    n = len(ws)
    vm = pl.BlockSpec(memory_space=pltpu.VMEM)

    def body(*refs):
        w_refs, g_refs, m_refs, v_refs = refs[:n], refs[n:2 * n], refs[2 * n:3 * n], refs[3 * n:4 * n]
        d_refs, nm_refs, nv_refs = refs[4 * n:5 * n], refs[5 * n:6 * n], refs[6 * n:7 * n]
        for i in range(n):
            d, nm, nv = _adamw_math(w_refs[i][...], g_refs[i][...], m_refs[i][...], v_refs[i][...])
            d_refs[i][...] = d
            nm_refs[i][...] = nm
            nv_refs[i][...] = nv

    shapes = [_sds(w.shape) for w in ws]
    outs = pl.pallas_call(
        body, name=name, in_specs=[vm] * (4 * n), out_specs=[vm] * (3 * n), out_shape=shapes * 3,
        compiler_params=pltpu.CompilerParams(vmem_limit_bytes=VMEM_LIMIT),
    )(*ws, *gs, *ms, *vs)
    return list(outs[:n]), list(outs[n:2 * n]), list(outs[2 * n:])


def _mesh_pos():
    return lax.axis_index("x"), lax.axis_index("y"), lax.axis_index("c")


def _dev_index(p):
    return 4 * p[0] + 2 * p[1] + p[2]


HBM = pl.BlockSpec(memory_space=pltpu.HBM)
SEM = pl.BlockSpec(memory_space=pltpu.SEMAPHORE)
EFFECT = pltpu.SideEffectType.DATAFLOW_SIDE_EFFECTING


def _peer_list():
    x, y, c = _mesh_pos()
    peers = [(x ^ dx, y ^ dy, c ^ dc) for dx in range(2) for dy in range(2) for dc in range(2)][1:]
    return (x, y, c), peers


def _split_copy(src_ref, land_ref, send_sems, recv_sems, i, k, peer, slot, exchange):
    return pltpu.make_async_remote_copy(
        src_ref=src_ref.at[_dev_index(peer)] if exchange else src_ref, dst_ref=land_ref.at[slot],
        send_sem=send_sems.at[7 * i + k], recv_sem=recv_sems.at[7 * i + k], device_id=peer, device_id_type=MESH)


def _own_copy(src_ref, land_ref, own_sems, i, slot, exchange):
    return pltpu.make_async_copy(src_ref.at[slot] if exchange else src_ref, land_ref.at[slot], own_sems.at[i])


def _comm_start(groups, name, exchange, dep=None):
    sizes = [len(g) for g in groups]
    n = sum(sizes)
    srcs = [a for g in groups for a in g]
    per_group = [exchange] * len(groups) if isinstance(exchange, bool) else list(exchange)
    exchanged = [flag for flag, sz in zip(per_group, sizes) for _ in range(sz)]
    lands = [lax.empty(a.shape if ex else (N_DEV,) + a.shape, a.dtype) for a, ex in zip(srcs, exchanged)]

    n_dep = 0 if dep is None else 1

    def body(*refs):
        src_refs, land_refs = refs[:n], refs[n:2 * n]
        sem_refs = refs[2 * n + n_dep:2 * n + n_dep + 3 * len(sizes)]
        token_ref = refs[-1]
        me, peers = _peer_list()
        mi = _dev_index(me)
        i = 0
        for gi, sz in enumerate(sizes):
            for j in range(sz):
                for k, peer in enumerate(peers):
                    _split_copy(src_refs[i], land_refs[i], sem_refs[3 * gi], sem_refs[3 * gi + 1], j, k, peer, mi,
                                exchanged[i]).start()
                _own_copy(src_refs[i], land_refs[i], sem_refs[3 * gi + 2], j, mi, exchanged[i]).start()
                i += 1
        token_ref[...] = jnp.zeros_like(token_ref)

    sem_shapes = []
    for sz in sizes:
        sem_shapes += [pltpu.SemaphoreType.DMA((7 * sz,)), pltpu.SemaphoreType.DMA((7 * sz,)),
                       pltpu.SemaphoreType.DMA((sz,))]
    thru = [pltpu.HBM(a.shape, a.dtype) for a in srcs + lands]
    n_sem = len(sem_shapes)
    outs = pl.pallas_call(
        body, name=name,
        out_shape=tuple(sem_shapes + thru + [_sds((8, LANES))]),
        in_specs=[HBM] * (2 * n) + [ANY] * n_dep,
        out_specs=tuple([SEM] * n_sem + [HBM] * (2 * n) + [pl.BlockSpec(memory_space=pltpu.VMEM)]),
        input_output_aliases={i: n_sem + i for i in range(2 * n)},
        compiler_params=pltpu.CompilerParams(has_side_effects=EFFECT),
    )(*[pltpu.with_memory_space_constraint(a, pltpu.HBM) for a in srcs + lands], *([] if dep is None else [dep]))
    sems, thru_src, thru_land, token = outs[:n_sem], outs[n_sem:n_sem + n], outs[n_sem + n:n_sem + 2 * n], outs[-1]
    result, off = [], 0
    for gi, sz in enumerate(sizes):
        result.append((*sems[3 * gi:3 * gi + 3], list(thru_src[off:off + sz]), list(thru_land[off:off + sz])))
        off += sz
    return result, token


def _comm_wait(group, after, name, exchange):
    send_sems, recv_sems, own_sems, srcs, lands = group
    n = len(srcs)
    after = list(after) if isinstance(after, (list, tuple)) else [after]

    def body(*refs):
        src_refs, land_refs = refs[:n], refs[n:2 * n]
        ssem, rsem, osem = refs[2 * n:2 * n + 3]
        me, peers = _peer_list()
        for i in range(n):
            for k, peer in enumerate(peers):
                cp = _split_copy(src_refs[i], land_refs[i], ssem, rsem, i, k, peer, _dev_index(peer), exchange)
                cp.wait_send()
                cp.wait_recv()
            _own_copy(src_refs[i], land_refs[i], osem, i, _dev_index(me), exchange).wait()

    outs = pl.pallas_call(
        body, name=name,
        out_shape=tuple(pltpu.HBM(a.shape, a.dtype) for a in srcs + lands),
        in_specs=[HBM] * (2 * n) + [SEM, SEM, SEM] + [ANY] * len(after),
        out_specs=tuple([HBM] * (2 * n)),
        input_output_aliases={i: i for i in range(2 * n)},
        compiler_params=pltpu.CompilerParams(has_side_effects=EFFECT),
    )(*srcs, *lands, send_sems, recv_sems, own_sems, *after)
    return list(outs[n:])


def _tie(a, token):
    return a + token[0, 0].astype(a.dtype)


def _pack(arrs, rows):
    flat = jnp.concatenate([a.reshape(-1).astype(F32) for a in arrs])
    return jnp.pad(flat, (0, rows * LANES - flat.shape[0])).reshape(rows, LANES)


def _unpack(packed, shapes):
    flat = packed.reshape(-1)
    out, off = [], 0
    for s in shapes:
        n = math.prod(s)
        out.append(flat[off:off + n].reshape(s))
        off += n
    return out


def _packed_rows(shapes):
    n = sum(math.prod(s) for s in shapes)
    unit = N_DEV * 8 * LANES
    return -(-n // unit) * unit // LANES


def kernel(x, mix_pre_g, mix_post_g, mlp_pre_g, mlp_post_g, w_in_even, s5_lam_re, s5_lam_im, s5_log_dt, s5_b_re, s5_b_im, s5_c_re, s5_c_im, s5_d, s5_w_glu, fox_b_f, w_out_even, w_in_odd, pool_w, pool_scale, sgu_ln_g, sgu_ln_b, sgu_w_s, sgu_b_s, w_out_odd, mlp_w1, mlp_w2, loss_target, m_mix_pre_g, m_mix_post_g, m_mlp_pre_g, m_mlp_post_g, m_w_in_even, m_s5_lam_re, m_s5_lam_im, m_s5_log_dt, m_s5_b_re, m_s5_b_im, m_s5_c_re, m_s5_c_im, m_s5_d, m_s5_w_glu, m_fox_b_f, m_w_out_even, m_w_in_odd, m_pool_w, m_pool_scale, m_sgu_ln_g, m_sgu_ln_b, m_sgu_w_s, m_sgu_b_s, m_w_out_odd, m_mlp_w1, m_mlp_w2, v_mix_pre_g, v_mix_post_g, v_mlp_pre_g, v_mlp_post_g, v_w_in_even, v_s5_lam_re, v_s5_lam_im, v_s5_log_dt, v_s5_b_re, v_s5_b_im, v_s5_c_re, v_s5_c_im, v_s5_d, v_s5_w_glu, v_fox_b_f, v_w_out_even, v_w_in_odd, v_pool_w, v_pool_scale, v_sgu_ln_g, v_sgu_ln_b, v_sgu_w_s, v_sgu_b_s, v_w_out_odd, v_mlp_w1, v_mlp_w2):
    weights = dict(mix_pre_g=mix_pre_g, mix_post_g=mix_post_g, mlp_pre_g=mlp_pre_g, mlp_post_g=mlp_post_g, w_in_even=w_in_even, s5_lam_re=s5_lam_re, s5_lam_im=s5_lam_im, s5_log_dt=s5_log_dt, s5_b_re=s5_b_re, s5_b_im=s5_b_im, s5_c_re=s5_c_re, s5_c_im=s5_c_im, s5_d=s5_d, s5_w_glu=s5_w_glu, fox_b_f=fox_b_f, w_out_even=w_out_even, w_in_odd=w_in_odd, pool_w=pool_w, pool_scale=pool_scale, sgu_ln_g=sgu_ln_g, sgu_ln_b=sgu_ln_b, sgu_w_s=sgu_w_s, sgu_b_s=sgu_b_s, w_out_odd=w_out_odd, mlp_w1=mlp_w1, mlp_w2=mlp_w2)
    mom_m = dict(mix_pre_g=m_mix_pre_g, mix_post_g=m_mix_post_g, mlp_pre_g=m_mlp_pre_g, mlp_post_g=m_mlp_post_g, w_in_even=m_w_in_even, s5_lam_re=m_s5_lam_re, s5_lam_im=m_s5_lam_im, s5_log_dt=m_s5_log_dt, s5_b_re=m_s5_b_re, s5_b_im=m_s5_b_im, s5_c_re=m_s5_c_re, s5_c_im=m_s5_c_im, s5_d=m_s5_d, s5_w_glu=m_s5_w_glu, fox_b_f=m_fox_b_f, w_out_even=m_w_out_even, w_in_odd=m_w_in_odd, pool_w=m_pool_w, pool_scale=m_pool_scale, sgu_ln_g=m_sgu_ln_g, sgu_ln_b=m_sgu_ln_b, sgu_w_s=m_sgu_w_s, sgu_b_s=m_sgu_b_s, w_out_odd=m_w_out_odd, mlp_w1=m_mlp_w1, mlp_w2=m_mlp_w2)
    mom_v = dict(mix_pre_g=v_mix_pre_g, mix_post_g=v_mix_post_g, mlp_pre_g=v_mlp_pre_g, mlp_post_g=v_mlp_post_g, w_in_even=v_w_in_even, s5_lam_re=v_s5_lam_re, s5_lam_im=v_s5_lam_im, s5_log_dt=v_s5_log_dt, s5_b_re=v_s5_b_re, s5_b_im=v_s5_b_im, s5_c_re=v_s5_c_re, s5_c_im=v_s5_c_im, s5_d=v_s5_d, s5_w_glu=v_s5_w_glu, fox_b_f=v_fox_b_f, w_out_even=v_w_out_even, w_in_odd=v_w_in_odd, pool_w=v_pool_w, pool_scale=v_pool_scale, sgu_ln_g=v_sgu_ln_g, sgu_ln_b=v_sgu_ln_b, sgu_w_s=v_sgu_w_s, sgu_b_s=v_sgu_b_s, w_out_odd=v_w_out_odd, mlp_w1=v_mlp_w1, mlp_w2=v_mlp_w2)
    names = list(weights)
    L = x.shape[1]
    x0 = x[0]
    target = loss_target[0]
    my_index = 4 * lax.axis_index("x") + 2 * lax.axis_index("y") + lax.axis_index("c")

    small_vec = jnp.zeros((8, LANES), F32)
    small_vec = small_vec.at[0, :64].set(pool_scale[0]).at[1, :64].set(sgu_ln_g[0]).at[2, :64].set(sgu_ln_b[0])
    ag_groups, ag_token = _comm_start(
        [[jnp.transpose(w_in_even[0]).astype(BF16), small_vec],
         [s5_w_glu[0].astype(BF16), w_out_even[0].astype(BF16)],
         [mlp_w1[0].astype(BF16), mlp_w2[0].astype(BF16)],
         [jnp.transpose(w_in_odd[0]).astype(BF16), w_out_odd[0].astype(BF16), mlp_w1[1].astype(BF16), mlp_w2[1].astype(BF16)]],
        "ag_start", exchange=False)

    lam_r = jnp.concatenate([s5_lam_re.reshape(1, S5_NS), s5_lam_im.reshape(1, S5_NS)], axis=0)
    ldt_r = jnp.repeat(s5_log_dt.reshape(32), 64).reshape(1, S5_NS)
    lam_c = jnp.transpose(lam_r)
    ldt_c = jnp.transpose(ldt_r)
    b_t = jnp.stack([jnp.tile(s5_b_re.reshape(S5_NS, 16), (1, 8)), jnp.tile(s5_b_im.reshape(S5_NS, 16), (1, 8))])
    c_t = jnp.stack([jnp.tile(s5_c_re.reshape(S5_W, 64), (1, 8)), jnp.tile(s5_c_im.reshape(S5_W, 64), (1, 8))])
    bf_pad = jnp.pad(fox_b_f, ((0, 0), (0, LANES - 8)))
    b_st = jnp.transpose(sgu_b_s[0])

    h0, rx0 = _rms_fwd(x0, _tie(mix_pre_g[0:1], ag_token), "rms0")
    tabs, bset, cset = _s5_prep(lam_r, ldt_r, lam_c, ldt_c, b_t, c_t, "s5_prep")
    ag0 = _comm_wait(ag_groups[0], tabs, "ag_wait0", exchange=False)
    winT_e = jnp.pad(ag0[0].reshape(EVEN_IN, D_MODEL), ((0, EVEN_PAD - EVEN_IN), (0, 0)))
    pool_scale_f = ag0[1][:, 0, :64].reshape(1, 512)
    ln_g_f = ag0[1][:, 1, :64].reshape(1, 512)
    ln_b_f = ag0[1][:, 2, :64].reshape(1, 512)
    z0 = _mm(h0, winT_e, name="win_even", tb=True, bm=512, bn=EVEN_PAD)
    xs, ylin = _s5_scan_fwd(z0, bset, cset, s5_d, tabs, "s5_scan")
    ag1 = _comm_wait(ag_groups[1], ylin, "ag_wait1", exchange=False)
    wglu = ag1[0].reshape(S5_W, S5_W)
    wout_e = ag1[1].reshape(D_MODEL, D_MODEL)
    ya = _s5_glu_fwd(ylin, wglu, "s5_glu")
    fcum, fq = _fox_f_fwd(z0, bf_pad, "fox_f")
    frow = jnp.transpose(fcum[:, :8]).reshape(4, 2, L)
    o_att, lse = _fox_fwd(z0, fq, frow, "fox_fwd")
    mix0 = [ya, o_att]
    x1, ry0, h1, rx1, y0 = _mm(mix0, wout_e, name="wout_even", epi=_epi_post_pre, extra=(x0,),
                               vecs=(mix_post_g[0:1], mlp_pre_g[0:1]), out_dtypes=POST_PRE_DTYPES,
                               out_kinds=POST_PRE_KINDS, bm=FUSED_ROWS)
    ag2 = _comm_wait(ag_groups[2], rx1, "ag_wait2", exchange=False)
    w1 = [ag2[0], None]
    w2 = [ag2[1].reshape(4 * D_MODEL, D_MODEL), None]
    p0, a0 = _mm(h1, w1[0], name="mlp0_w1", b3=True, out_dtypes=(BF16, BF16), epi=_epi_relu2, bm=512, bn=4 * D_MODEL)
    x2, ro0, h2, rx2, o0 = _mm(a0, w2[0], name="mlp0_w2", epi=_epi_post_pre, extra=(x1,),
                               vecs=(mlp_post_g[0:1], mix_pre_g[1:2]), out_dtypes=POST_PRE_DTYPES,
                               out_kinds=POST_PRE_KINDS, bm=FUSED_ROWS, bk=4 * D_MODEL)
    ag3 = _comm_wait(ag_groups[3], rx2, "ag_wait3", exchange=False)
    winT_o = ag3[0].reshape(ODD_IN, D_MODEL)
    wout_o = ag3[1].reshape(D_MODEL, D_MODEL)
    w1[1] = ag3[2]
    w2[1] = ag3[3].reshape(4 * D_MODEL, D_MODEL)
    z1 = _mm(h2, winT_o, name="win_odd", tb=True, bn=ODD_IN)
    yc, pooled = _pool_fwd(z1, pool_w[0], pool_scale_f, "pool_fwd")
    yd = _sgu_fwd(z1, ln_g_f, ln_b_f, sgu_w_s[0], b_st, "sgu_fwd")
    mix1 = [yc, yd]
    x3, ry1, h3, rx3, y1 = _mm(mix1, wout_o, name="wout_odd", epi=_epi_post_pre, extra=(x2,),
                               vecs=(mix_post_g[1:2], mlp_pre_g[1:2]), out_dtypes=POST_PRE_DTYPES,
                               out_kinds=POST_PRE_KINDS, bm=FUSED_ROWS)
    p1, a1 = _mm(h3, w1[1], name="mlp1_w1", b3=True, out_dtypes=(BF16, BF16), epi=_epi_relu2, bm=512, bn=4 * D_MODEL)
    gx4, g_o1, gg_mlp_post1, sq_lanes = _mm(
        a1, w2[1], name="mlp1_w2", epi=_epi_post_loss, extra=(x3, target), vecs=(mlp_post_g[1:2],),
        out_dtypes=(F32, BF16, F32, F32), out_kinds=("full", "full", "vsum", "vsum"), bm=FUSED_ROWS, bk=4 * D_MODEL)
    sq = sq_lanes[:, 0:1]

    g_p1 = _mm(g_o1, w2[1], name="b_mlp1_a", tb=True, out_dtypes=(BF16,), epi=_epi_relu2_bwd, extra=(p1,),
               bm=512, bn=4 * D_MODEL)
    gw2_1 = _mm(a1, g_o1, name="b_mlp1_w2", ta=True, bm=512, bk=L)
    gw1_1 = _mm(h3, g_p1, name="b_mlp1_w1", ta=True, out3=True, bn=512, bk=L)
    (ex1,), tok1 = _comm_start([[gw1_1, gw2_1.reshape(N_DEV, 512, D_MODEL)]], "ex_start1", exchange=True)
    g_x3, gg_mlp_pre1, g_y1, gg_mix_post1 = _mm(
        g_p1, w1[1], name="b_mlp1_h", tb=True, b3=True, epi=_epi_pre_post_bwd, extra=(x3, gx4, y1), cols=(rx3, ry1),
        vecs=(_tie(mlp_pre_g[1:2], tok1), mix_post_g[1:2]), out_dtypes=PRE_POST_BWD_DTYPES,
        out_kinds=PRE_POST_BWD_KINDS, bm=FUSED_ROWS, bk=4 * D_MODEL)
    gwout_o = _mm(mix1, g_y1, name="b_wout_odd_w", ta=True)
    g_xc, g_pool_w, g_pool_scale = _pool_bwd(g_y1, wout_o, pooled, pool_w[0], pool_scale_f, "pool_bwd")
    g_u1, g_v1, g_ws, g_bst, g_ln_g, g_ln_b = _sgu_bwd(g_y1, wout_o, z1, ln_g_f, ln_b_f, sgu_w_s[0], b_st,
                                                       "sgu_bwd")
    g_z1 = [g_xc, g_u1, g_v1]
    gwinT_o = _mm(g_z1, h2, name="b_win_odd_w", ta=True)
    (ex2,), tok2 = _comm_start([[gwout_o.reshape(N_DEV, 128, D_MODEL), gwinT_o.reshape(N_DEV, ODD_IN // N_DEV, D_MODEL)]], "ex_start2", exchange=True)
    g_x2, gg_mix_pre1, g_o0, gg_mlp_post0 = _mm(
        g_z1, winT_o, name="b_win_odd_h", epi=_epi_pre_post_bwd, extra=(x2, g_x3, o0), cols=(rx2, ro0),
        vecs=(_tie(mix_pre_g[1:2], tok2), mlp_post_g[0:1]), out_dtypes=PRE_POST_BWD_DTYPES,
        out_kinds=PRE_POST_BWD_KINDS, bm=FUSED_ROWS)
    g_p0 = _mm(g_o0, w2[0], name="b_mlp0_a", tb=True, out_dtypes=(BF16,), epi=_epi_relu2_bwd, extra=(p0,),
               bm=512, bn=4 * D_MODEL)
    gw2_0 = _mm(a0, g_o0, name="b_mlp0_w2", ta=True, bm=512, bk=L)
    gw1_0 = _mm(h1, g_p0, name="b_mlp0_w1", ta=True, out3=True, bn=512, bk=L)
    (ex3,), tok3 = _comm_start([[gw1_0, gw2_0.reshape(N_DEV, 512, D_MODEL)]], "ex_start3", exchange=True)
    g_x1, gg_mlp_pre0, g_y0, gg_mix_post0 = _mm(
        g_p0, w1[0], name="b_mlp0_h", tb=True, b3=True, epi=_epi_pre_post_bwd, extra=(x1, g_x2, y0), cols=(rx1, ry0),
        vecs=(_tie(mlp_pre_g[0:1], tok3), mix_post_g[0:1]), out_dtypes=PRE_POST_BWD_DTYPES,
        out_kinds=PRE_POST_BWD_KINDS, bm=FUSED_ROWS, bk=4 * D_MODEL)
    g_o_att = _mm(g_y0, wout_e[FOX_W:], name="b_wout_even_m", tb=True)
    gwout_e = _mm(mix0, g_y0, name="b_wout_even_w", ta=True)
    gyl, gud, g_wglu, g_d = _s5_glu_bwd(g_y0, wout_e, ylin, z0, s5_d, wglu, "s5_glu_bwd")
    (ex4,), tok4 = _comm_start([[gwout_e.reshape(N_DEV, 128, D_MODEL), g_wglu.reshape(N_DEV, 64, S5_W)]], "ex_start4", exchange=True)
    g_u0, ga, gb_raw, gc_raw = _s5_scan_bwd(gyl, _tie(cset, tok4), xs, z0, bset, gud, tabs, "s5_scan_bwd")
    g_lam, g_ldt, g_b, g_c = _s5_param_bwd(lam_c, ldt_c, b_t, gb_raw, jnp.transpose(ga), gc_raw, "s5_param_bwd")
    dq, dk, dv, dfq, dfrow = _fox_bwd(z0, frow, o_att, lse, g_o_att, "fox_bwd")
    dFk = jnp.pad(jnp.transpose(dfrow.reshape(8, L)), ((0, 0), (0, LANES - 8)))
    dfl, db_f = _fox_f_bwd(dFk, dfq, z0, bf_pad, "fox_f_bwd")
    g_z0 = [g_u0, dq, dk, dv, dfl]
    grad_x, gg_mix_pre0 = _mm(g_z0, winT_e, name="b_win_even_h", epi=_epi_pre_bwd, extra=(x0, g_x1), cols=(rx0,),
                              vecs=(mix_pre_g[0:1],), out_dtypes=(F32, F32), out_kinds=("full", "vsum"),
                              bm=FUSED_ROWS)

    small_grads = dict(
        mix_pre_g=jnp.concatenate([gg_mix_pre0, gg_mix_pre1]), mix_post_g=jnp.concatenate([gg_mix_post0, gg_mix_post1]),
        mlp_pre_g=jnp.concatenate([gg_mlp_pre0, gg_mlp_pre1]), mlp_post_g=jnp.concatenate([gg_mlp_post0, gg_mlp_post1]),
        s5_lam_re=g_lam[:, 0], s5_lam_im=g_lam[:, 1],
        s5_b_re=g_b[0, :, :16], s5_b_im=g_b[1, :, :16], s5_c_re=g_c[0, :, :64], s5_c_im=g_c[1, :, :64],
        pool_w=g_pool_w, sgu_w_s=g_ws, s5_d=g_d, sgu_b_s=jnp.transpose(g_bst),
        pool_scale=g_pool_scale, sgu_ln_g=g_ln_g, sgu_ln_b=g_ln_b, s5_log_dt=g_ldt, fox_b_f=db_f[:, :8])
    small_names = list(small_grads)
    full_shapes = [(512,) if nm in ("pool_scale", "sgu_ln_g", "sgu_ln_b") else weights[nm].shape for nm in small_names]
    full_shapes.append((1, 1))
    rows = _packed_rows(full_shapes)
    packed = _pack([small_grads[nm] for nm in small_names] + [sq], rows).reshape(N_DEV, rows // N_DEV, LANES)
    (exs,), tok_s = _comm_start([[packed]], "exs_start", exchange=True)
    gwinT_e = _mm(g_z0, h0, name="b_win_even_w", ta=True, bk=512, out_dtypes=(BF16,), dep=tok_s)
    (recv_small,) = _comm_wait(exs, gwinT_e, "exs_wait", exchange=True)
    piece = _sum_pieces(recv_small, "sum_small")
    gwinT_e_pieces = gwinT_e[:EVEN_IN].reshape(N_DEV, EVEN_IN // N_DEV, D_MODEL)
    (ags, ex5), tok5 = _comm_start([[piece], [gwinT_e_pieces]], "ags_ex_start5", exchange=(False, True))
    r_w1_1, r_w2_1 = _comm_wait(ex1, tok5, "ex_wait1", exchange=True)
    r_wout_o, r_win_o = _comm_wait(ex2, tok5, "ex_wait2", exchange=True)
    r_w1_0, r_w2_0 = _comm_wait(ex3, tok5, "ex_wait3", exchange=True)
    r_wout_e, r_wglu = _comm_wait(ex4, tok5, "ex_wait4", exchange=True)

    res = {}
    for nm, parts in (("mlp_w1", (r_w1_0, r_w1_1)), ("mlp_w2", (r_w2_0, r_w2_1))):
        first = _sum_adamw(parts[0], weights[nm], mom_m[nm], mom_v[nm], "adamw_%s_0" % nm, layer=0)
        res[nm] = tuple(_sum_adamw(parts[1], weights[nm], mom_m[nm], mom_v[nm], "adamw_%s_1" % nm, layer=1, prev=first))
    big_parts = dict(s5_w_glu=r_wglu, w_out_even=r_wout_e, w_out_odd=r_wout_o)
    for nm, parts in big_parts.items():
        res[nm] = tuple(_sum_adamw(parts, weights[nm], mom_m[nm], mom_v[nm], "adamw_" + nm))
    done = [res[nm][1] for nm in ("mlp_w1", "mlp_w2", "s5_w_glu", "w_out_even", "w_out_odd")]

    (small_all,) = _comm_wait(ags, done, "ags_wait", exchange=False)
    small_full = _unpack(small_all.reshape(rows, LANES), full_shapes)
    loss = 0.5 * small_full.pop()[0, 0] / D_MODEL
    small_g = []
    for nm, g in zip(small_names, small_full):
        if nm in ("pool_scale", "sgu_ln_g", "sgu_ln_b"):
            g = lax.dynamic_slice(g, (my_index * 64,), (64,)).reshape(1, 64)
        small_g.append(g)

    def turned(arrs):
        return [jnp.swapaxes(a, -1, -2) if nm in ("s5_b_re", "s5_b_im") else a for nm, a in zip(small_names, arrs)]

    sd, sm, sv = _adamw_many(turned([weights[nm] for nm in small_names]), turned(small_g),
                             turned([mom_m[nm] for nm in small_names]), turned([mom_v[nm] for nm in small_names]),
                             "adamw_small")
    for nm, g_, d_, m_, v_ in zip(small_names, small_g, turned(sd), turned(sm), turned(sv)):
        res[nm] = (g_, d_, m_, v_)
    done.append(sd[0])

    for nm, parts in (("w_in_odd", r_win_o), ("w_in_even", None)):
        if parts is None:
            (parts,) = _comm_wait(ex5, done, "ex_wait5", exchange=True)
        outs = _sum_adamw(parts, jnp.transpose(weights[nm], (0, 2, 1)), jnp.transpose(mom_m[nm], (0, 2, 1)),
                          jnp.transpose(mom_v[nm], (0, 2, 1)), "adamw_" + nm)
        res[nm] = tuple(jnp.transpose(o, (0, 2, 1)) for o in outs)
        done.append(res[nm][1])

    grads = [res[nm][0].reshape(weights[nm].shape) for nm in names]
    deltas = [res[nm][1].reshape(weights[nm].shape) for nm in names]
    new_m = [res[nm][2].reshape(weights[nm].shape) for nm in names]
    new_v = [res[nm][3].reshape(weights[nm].shape) for nm in names]
    return (loss, grad_x[None], *grads, *deltas, *new_m, *new_v)
```

```python
import math

import jax
import jax.numpy as jnp
from jax import lax
from jax.experimental import pallas as pl
from jax.experimental.pallas import tpu as pltpu

F32 = jnp.float32
BF16 = jnp.bfloat16
MESH = pl.DeviceIdType.MESH
ANY = pl.BlockSpec(memory_space=pl.ANY)

N_DEV = 8
D_MODEL = 1024
EPS = 1e-6
NORM_ROWS = 512
FUSED_ROWS = 512
S5_W = 512
S5_NS = 2048
SCAN_GROUPS = 4
SCAN_CHUNK = 1024
FOX_W = 512
EVEN_IN = 2056
EVEN_PAD = 2176
ODD_IN = 1536
LANES = 128
PIECE = 4 * D_MODEL // N_DEV
VMEM_LIMIT = 56 * 1024 * 1024

ADAM_LR = 0.001
ADAM_B1 = 0.9
ADAM_B2 = 0.999
ADAM_EPS = 1e-08
ADAM_WD = 0.01
ADAM_STEP = 10

NT = (((1,), (1,)), ((), ()))
TN = (((0,), (0,)), ((), ()))
NN = (((1,), (0,)), ((), ()))


def _cp(*sem):
    return pltpu.CompilerParams(dimension_semantics=sem, vmem_limit_bytes=VMEM_LIMIT)


def _sds(shape, dtype=F32):
    return jax.ShapeDtypeStruct(tuple(shape), dtype)


def _gelu(x):
    t = jnp.tanh(0.7978845608028654 * (x + 0.044715 * x * x * x))
    return 0.5 * x * (1.0 + t)


def _gelu_grad(x):
    t = jnp.tanh(0.7978845608028654 * (x + 0.044715 * x * x * x))
    du = 0.7978845608028654 * (1.0 + 3.0 * 0.044715 * x * x)
    return 0.5 * (1.0 + t) + 0.5 * x * (1.0 - t * t) * du


def _sigmoid(x):
    return 1.0 / (1.0 + jnp.exp(-x))


def _dot(a, b, dn=NN):
    return lax.dot_general(a, b, dn, preferred_element_type=F32)


def _mm(a, b, *, name, ta=False, tb=False, b3=False, out3=False, out_dtypes=(F32,), epi=None, extra=(),
        cols=(), vecs=(), out_kinds=None, bm=1024, bn=1024, bk=1024, dep=None):
    a_list = list(a) if isinstance(a, (list, tuple)) else [a]
    widths = [p.shape[1] for p in a_list]
    offs = [sum(widths[:i]) for i in range(len(widths))]
    na = len(a_list)
    M = sum(widths) if ta else a_list[0].shape[0]
    K = a_list[0].shape[0] if ta else sum(widths)
    if na > 1:
        assert not b3 and not tb
        bm, bk = (M, bk) if ta else (bm, K)
    pw = b.shape[2] if b3 else PIECE
    if b3:
        N = b.shape[1] if tb else b.shape[0] * pw
        assert (b.shape[0] * pw if tb else b.shape[1]) == K
    else:
        N = b.shape[0] if tb else b.shape[1]
    bm, bn, bk = min(bm, M), min(bn, N), min(bk, K)
    assert M % bm == 0 and N % bn == 0 and K % bk == 0, (name, M, N, K, bm, bn, bk)
    assert not (b3 or out3) or ((bk if tb else bn) % pw == 0 and bn % PIECE == 0)
    nk = K // bk
    n_extra = len(extra) + len(cols) + len(vecs)
    n_out = len(out_dtypes)
    out_kinds = tuple(out_kinds) if out_kinds is not None else ("full",) * n_out
    dn = (((0 if ta else 1,), (1 if tb else 0,)), ((), ()))

    use_acc = nk > 1

    def body(*refs):
        a_refs, b_ref = refs[:na], refs[na]
        a_ref = a_refs[0]
        e_refs = refs[na + 1:na + 1 + n_extra]
        first_out = na + 1 + n_extra + (0 if dep is None else 1)
        o_refs = refs[first_out:first_out + n_out]
        acc_ref = refs[-1] if use_acc else o_refs[0]
        i, k = pl.program_id(0), pl.program_id(2)

        def dot(a_v, b_v):
            return lax.dot_general(a_v.astype(BF16), b_v.astype(BF16), dn, preferred_element_type=F32)

        everything = slice(None)
        if na > 1 and ta:
            terms = [(pl.ds(off, w), everything, r, b_ref) for r, off, w in zip(a_refs, offs, widths)]
        elif na > 1:
            terms = [(everything, everything, r, b_ref.at[pl.ds(off, w), :]) for r, off, w in zip(a_refs, offs, widths)]
        elif not b3:
            terms = [(everything, everything, a_ref, b_ref)]
        elif tb:
            terms = [(everything, everything,
                      a_ref.at[pl.ds(t * pw, pw), :] if ta else a_ref.at[:, pl.ds(t * pw, pw)], b_ref.at[t])
                     for t in range(bk // pw)]
        else:
            terms = [(everything, pl.ds(t * pw, pw), a_ref, b_ref.at[t]) for t in range(bn // pw)]

        def finish(acc):
            outs = (acc,) if epi is None else epi(acc, *[e[...] for e in e_refs])
            for o_ref, o, kind in zip(o_refs, outs, out_kinds):
                if kind == "vsum":
                    @pl.when(i == 0)
                    def _(o_ref=o_ref, o=o):
                        o_ref[...] = o

                    @pl.when(i > 0)
                    def _(o_ref=o_ref, o=o):
                        o_ref[...] += o
                elif out3:
                    for t in range(bn // PIECE):
                        o_ref[t] = o[:, t * PIECE:(t + 1) * PIECE].astype(o_ref.dtype)
                else:
                    o_ref[...] = o.astype(o_ref.dtype)

        if nk == 1:
            bands = {}
            for rows, cols, a_r, b_r in terms:
                key = (getattr(rows, "start", None), getattr(cols, "start", None))
                val = dot(a_r[...], b_r[...])
                bands[key] = val if key not in bands else bands[key] + val
            vals = list(bands.values())
            if len(vals) == 1:
                finish(vals[0])
            else:
                finish(jnp.concatenate(vals, axis=0 if (na > 1 and ta) else 1))
            return

        @pl.when(k == 0)
        def _():
            acc_ref[...] = jnp.zeros_like(acc_ref)

        for rows, cols, a_r, b_r in terms:
            acc_ref[rows, cols] += dot(a_r[...], b_r[...])

        @pl.when(k == nk - 1)
        def _():
            finish(acc_ref[...])

    if na > 1:
        a_specs = [pl.BlockSpec((bk, w), lambda i, j, k: (k, 0)) if ta else pl.BlockSpec((bm, w), lambda i, j, k: (i, 0))
                   for w in widths]
    else:
        a_specs = [pl.BlockSpec((bk, bm), lambda i, j, k: (k, i)) if ta else
                   pl.BlockSpec((bm, bk), lambda i, j, k: (i, k))]
    if b3:
        if tb:
            b_spec = pl.BlockSpec((bk // pw, bn, pw), lambda i, j, k: (k, j, 0))
        else:
            b_spec = pl.BlockSpec((bn // pw, bk, pw), lambda i, j, k: (j, k, 0))
    else:
        b_spec = pl.BlockSpec((bn, bk), lambda i, j, k: (j, k)) if tb else pl.BlockSpec((bk, bn), lambda i, j, k: (k, j))
    e_specs = ([pl.BlockSpec((bm, bn), lambda i, j, k: (i, j)) for _ in extra]
               + [pl.BlockSpec((bm, 1), lambda i, j, k: (i, 0)) for _ in cols]
               + [pl.BlockSpec((1, bn), lambda i, j, k: (0, j)) for _ in vecs])
    if out3:
        o_specs = [pl.BlockSpec((bn // PIECE, bm, PIECE), lambda i, j, k: (j, i, 0)) for _ in out_dtypes]
        o_shapes = [_sds((N // PIECE, M, PIECE), dt) for dt in out_dtypes]
    else:
        spec_of = {"full": pl.BlockSpec((bm, bn), lambda i, j, k: (i, j)),
                   "col": pl.BlockSpec((bm, 1), lambda i, j, k: (i, 0)),
                   "vsum": pl.BlockSpec((1, bn), lambda i, j, k: (0, j))}
        shape_of = {"full": (M, N), "col": (M, 1), "vsum": (1, N)}
        o_specs = [spec_of[kind] for kind in out_kinds]
        o_shapes = [_sds(shape_of[kind], dt) for kind, dt in zip(out_kinds, out_dtypes)]
    assert "col" not in out_kinds or bn == N
    outs = pl.pallas_call(
        body, name=name, grid=(M // bm, N // bn, nk),
        in_specs=a_specs + [b_spec] + e_specs + ([] if dep is None else [ANY]),
        out_specs=o_specs, out_shape=o_shapes,
        scratch_shapes=[pltpu.VMEM((bm, bn), F32)] if use_acc else [],
        compiler_params=_cp("arbitrary" if "vsum" in out_kinds else "parallel", "parallel", "arbitrary"),
    )(*a_list, b, *extra, *cols, *vecs, *([] if dep is None else [dep]))
    return outs[0] if n_out == 1 else outs


def _epi_relu2(acc):
    r = jnp.maximum(acc, 0.0)
    return acc, r * r


def _epi_relu2_bwd(acc, p):
    return (acc * (2.0 * jnp.maximum(p.astype(F32), 0.0)),)


def _row_spec(rb, w=D_MODEL):
    return pl.BlockSpec((rb, w), lambda i: (i, 0))


def _vec_spec(w=D_MODEL):
    return pl.BlockSpec((1, w), lambda i: (0, 0))


def _rstd(v):
    return lax.rsqrt(jnp.mean(v * v, axis=-1, keepdims=True) + EPS)


def _rms_fwd(x, g, name):
    L = x.shape[0]
    rb = min(NORM_ROWS, L)

    def body(x_ref, g_ref, h_ref, r_ref):
        xv = x_ref[...]
        r = _rstd(xv)
        h_ref[...] = (xv * r * g_ref[...]).astype(BF16)
        r_ref[...] = r

    return pl.pallas_call(
        body, name=name, grid=(L // rb,),
        in_specs=[_row_spec(rb), _vec_spec()],
        out_specs=[_row_spec(rb), _row_spec(rb, 1)],
        out_shape=[_sds((L, D_MODEL), BF16), _sds((L, 1))],
        compiler_params=_cp("parallel"),
    )(x, g)


def _rms_bwd_rows(dy, xv, r, g):
    n = xv * r
    dyg = dy * g
    return r * (dyg - n * jnp.mean(dyg * n, axis=-1, keepdims=True)), n


POST_PRE_DTYPES = (F32, F32, BF16, F32, F32)
POST_PRE_KINDS = ("full", "col", "full", "col", "full")
PRE_POST_BWD_DTYPES = (F32, F32, BF16, F32)
PRE_POST_BWD_KINDS = ("full", "vsum", "full", "vsum")


def _epi_post_pre(y, x_in, g_post, g_pre):
    ry = _rstd(y)
    xo = x_in + y * ry * g_post
    rx = _rstd(xo)
    return xo, ry, xo * rx * g_pre, rx, y


def _epi_pre_post_bwd(gh, x, g_out, y_prev, rx, ry_prev, g_pre, g_post_prev):
    gx, n = _rms_bwd_rows(gh, x, rx, g_pre)
    gi = g_out + gx
    gy, ny = _rms_bwd_rows(gi, y_prev, ry_prev, g_post_prev)
    return gi, jnp.sum(gh * n, axis=0, keepdims=True), gy, jnp.sum(gi * ny, axis=0, keepdims=True)


def _epi_pre_bwd(gh, x, g_out, rx, g_pre):
    gx, n = _rms_bwd_rows(gh, x, rx, g_pre)
    return g_out + gx, jnp.sum(gh * n, axis=0, keepdims=True)


def _epi_post_loss(y, x_in, target, g_post):
    ry = _rstd(y)
    diff = x_in + y * ry * g_post - target
    gx = diff * (1.0 / D_MODEL)
    gy, n = _rms_bwd_rows(gx, y, ry, g_post)
    sq = jnp.broadcast_to(jnp.sum(diff * diff, keepdims=True), (1, y.shape[1]))
    return gx, gy, jnp.sum(gx * n, axis=0, keepdims=True), sq


def _cmul(ar, ai, br, bi):
    return ar * br - ai * bi, ar * bi + ai * br


def _zoh_cols(lr, li, ldt):
    dt = jnp.exp(ldt)
    mag = jnp.exp(lr * dt)
    ar = mag * jnp.cos(li * dt)
    ai = mag * jnp.sin(li * dt)
    den = lr * lr + li * li
    nr = ar - 1.0
    qr = (nr * lr + ai * li) / den
    qi = (ai * lr - nr * li) / den
    return dt, ar, ai, qr, qi, den


def _b_mask():
    r = lax.broadcasted_iota(jnp.int32, (S5_NS, LANES), 0)
    c = lax.broadcasted_iota(jnp.int32, (S5_NS, LANES), 1)
    return ((r >> 6) & 7) == (c >> 4)


def _c_mask():
    r = lax.broadcasted_iota(jnp.int32, (S5_W, 512), 0)
    c = lax.broadcasted_iota(jnp.int32, (S5_W, 512), 1)
    return ((r >> 4) & 7) == (c >> 6)


def _s5_prep(lam_r, ldt_r, lam_c, ldt_c, b_t, c_t, name):
    def body(lam_r_ref, ldt_r_ref, lam_c_ref, ldt_c_ref, b_ref, c_ref, tab_ref, bset_ref, cset_ref):
        lr, li = lam_r_ref[0:1, :], lam_r_ref[1:2, :]
        dt = jnp.exp(ldt_r_ref[...])
        mag = jnp.exp(lr * dt)
        p1r, p1i = mag * jnp.cos(li * dt), mag * jnp.sin(li * dt)
        p2r, p2i = _cmul(p1r, p1i, p1r, p1i)
        p3r, p3i = _cmul(p2r, p2i, p1r, p1i)
        p4r, p4i = _cmul(p2r, p2i, p2r, p2i)
        p5r, p5i = _cmul(p4r, p4i, p1r, p1i)
        p6r, p6i = _cmul(p4r, p4i, p2r, p2i)
        p7r, p7i = _cmul(p4r, p4i, p3r, p3i)
        p8r, p8i = _cmul(p4r, p4i, p4r, p4i)
        pw_r = [p1r, p2r, p3r, p4r, p5r, p6r, p7r, p8r]
        pw_i = [p1i, p2i, p3i, p4i, p5i, p6i, p7i, p8i]
        row = lax.broadcasted_iota(jnp.int32, (8, S5_NS), 0)
        zero = jnp.zeros((8, S5_NS), F32)

        def bc(v):
            return jnp.broadcast_to(v, (8, S5_NS))

        for d in range(2):
            sgn = 1.0 if d == 0 else -1.0
            for t, s in enumerate((1, 2, 4)):
                live = (row >= s) if d == 0 else (row <= 7 - s)
                tab_ref[d, 2 * t] = jnp.where(live, bc(pw_r[s - 1]), zero)
                tab_ref[d, 2 * t + 1] = jnp.where(live, bc(sgn * pw_i[s - 1]), zero)
            cr, ci = zero, zero
            for i in range(8):
                e = i if d == 0 else 7 - i
                cr = jnp.where(row == i, bc(pw_r[e]), cr)
                ci = jnp.where(row == i, bc(sgn * pw_i[e]), ci)
            tab_ref[d, 6] = cr
            tab_ref[d, 7] = ci

        _, _, _, qr, qi, _ = _zoh_cols(lam_c_ref[:, 0:1], lam_c_ref[:, 1:2], ldt_c_ref[...])
        bm = _b_mask()
        br, bi = b_ref[0], b_ref[1]
        bset_ref[0] = jnp.where(bm, qr * br - qi * bi, 0.0).astype(BF16)
        bset_ref[1] = jnp.where(bm, qr * bi + qi * br, 0.0).astype(BF16)
        cm = _c_mask()
        cset_ref[0] = jnp.where(cm, c_ref[0], 0.0).astype(BF16)
        cset_ref[1] = jnp.where(cm, c_ref[1], 0.0).astype(BF16)

    vm = pl.BlockSpec(memory_space=pltpu.VMEM)
    return pl.pallas_call(
        body, name=name, in_specs=[vm] * 6, out_specs=[vm] * 3,
        out_shape=[_sds((2, 8, 8, S5_NS)), _sds((2, S5_NS, LANES), BF16), _sds((2, S5_W, 512), BF16)],
        compiler_params=pltpu.CompilerParams(vmem_limit_bytes=VMEM_LIMIT),
    )(lam_r, ldt_r, lam_c, ldt_c, b_t, c_t)


SCAN_W = SCAN_GROUPS * LANES


def _scan_chunk(src_ref, dst_ref, tab_ref, carry_ref, nb, reverse, xs_ref=None, acc_ref=None):
    row = lax.broadcasted_iota(jnp.int32, (8, LANES), 0)

    def step(i, carry):
        b = (nb - 1 - i) if reverse else i
        off = pl.multiple_of(b * 8, 8)
        out = []
        for g in range(SCAN_GROUPS):
            lanes = pl.ds(g * LANES, LANES)
            cr, ci = carry[2 * g], carry[2 * g + 1]
            yr = src_ref[0, pl.ds(off, 8), lanes]
            yi = src_ref[1, pl.ds(off, 8), lanes]
            for t, s in enumerate((1, 2, 4)):
                sh = (8 - s) if reverse else s
                sr = pltpu.roll(yr, sh, 0)
                si = pltpu.roll(yi, sh, 0)
                mr, mi = tab_ref[2 * t, :, lanes], tab_ref[2 * t + 1, :, lanes]
                yr, yi = yr + mr * sr - mi * si, yi + mr * si + mi * sr
            pr, pi = tab_ref[6, :, lanes], tab_ref[7, :, lanes]
            yr, yi = yr + pr * cr - pi * ci, yi + pr * ci + pi * cr
            dst_ref[0, pl.ds(off, 8), lanes] = yr
            dst_ref[1, pl.ds(off, 8), lanes] = yi
            if xs_ref is not None:
                nr = jnp.where(row == 7, cr, pltpu.roll(yr, 7, 0))
                ni = jnp.where(row == 7, ci, pltpu.roll(yi, 7, 0))
                xr = xs_ref[0, pl.ds(off, 8), lanes]
                xi = xs_ref[1, pl.ds(off, 8), lanes]
                acc_ref[0, :, lanes] += xr * nr + xi * ni
                acc_ref[1, :, lanes] += xr * ni - xi * nr
            last = 0 if reverse else 7
            out += [jnp.broadcast_to(yr[last:last + 1, :], (8, LANES)),
                    jnp.broadcast_to(yi[last:last + 1, :], (8, LANES))]
        return tuple(out)

    init = []
    for g in range(SCAN_GROUPS):
        init += [carry_ref[0, :, pl.ds(g * LANES, LANES)], carry_ref[1, :, pl.ds(g * LANES, LANES)]]
    fin = lax.fori_loop(0, nb, step, tuple(init))
    for g in range(SCAN_GROUPS):
        carry_ref[0, :, pl.ds(g * LANES, LANES)] = fin[2 * g]
        carry_ref[1, :, pl.ds(g * LANES, LANES)] = fin[2 * g + 1]


def _s5_scan_fwd(z, bset, cset, dvec, tabs, name):
    L = z.shape[0]
    tl = min(SCAN_CHUNK, L)
    nc = L // tl

    def body(u_ref, b_ref, c_ref, d_ref, tab_ref, x_ref, y_ref, carry_ref):
        @pl.when(pl.program_id(1) == 0)
        def _():
            carry_ref[...] = jnp.zeros_like(carry_ref)

        uf = u_ref[...]
        u = uf.astype(BF16)
        x_ref[0] = _dot(u, b_ref[0], NT)
        x_ref[1] = _dot(u, b_ref[1], NT)
        _scan_chunk(x_ref, x_ref, tab_ref, carry_ref, tl // 8, False)
        y_ref[...] = (_dot(x_ref[0].astype(BF16), c_ref[0], NT) - _dot(x_ref[1].astype(BF16), c_ref[1], NT)
                      + d_ref[...] * uf)

    col = pl.BlockSpec((tl, LANES), lambda j, c: (c, j))
    return pl.pallas_call(
        body, name=name, grid=(S5_NS // SCAN_W, nc),
        in_specs=[col, pl.BlockSpec((2, SCAN_W, LANES), lambda j, c: (0, j, 0)),
                  pl.BlockSpec((2, LANES, SCAN_W), lambda j, c: (0, j, 0)),
                  pl.BlockSpec((1, LANES), lambda j, c: (0, j)),
                  pl.BlockSpec((None, 8, 8, SCAN_W), lambda j, c: (0, 0, 0, j))],
        out_specs=[pl.BlockSpec((2, tl, SCAN_W), lambda j, c: (0, c, j)), col],
        out_shape=[_sds((2, L, S5_NS)), _sds((L, S5_W))],
        scratch_shapes=[pltpu.VMEM((2, 8, SCAN_W), F32)],
        compiler_params=_cp("parallel", "arbitrary"),
    )(z, bset, cset, dvec, tabs)


def _s5_scan_bwd(gyl, cset, xs, z, bset, gud, tabs, name):
    L = z.shape[0]
    tl = min(SCAN_CHUNK, L)
    nc = L // tl

    def body(g_ref, c_ref, xs_ref, u_ref, b_ref, gud_ref, tab_ref, gu_ref, ga_ref, gb_ref, gc_ref,
             gx_ref, carry_ref, acc_ref):
        c = pl.program_id(1)

        @pl.when(c == 0)
        def _():
            carry_ref[...] = jnp.zeros_like(carry_ref)
            acc_ref[...] = jnp.zeros_like(acc_ref)
            gb_ref[...] = jnp.zeros_like(gb_ref)
            gc_ref[...] = jnp.zeros_like(gc_ref)

        gy = g_ref[...].astype(BF16)
        gx_ref[0] = _dot(gy, c_ref[0])
        gx_ref[1] = -_dot(gy, c_ref[1])
        gc_ref[0] += _dot(gy, xs_ref[0].astype(BF16), TN)
        gc_ref[1] -= _dot(gy, xs_ref[1].astype(BF16), TN)
        _scan_chunk(gx_ref, gx_ref, tab_ref, carry_ref, tl // 8, True, xs_ref, acc_ref)
        gr = gx_ref[0].astype(BF16)
        gi = gx_ref[1].astype(BF16)
        gu_ref[...] = gud_ref[...] + _dot(gr, b_ref[0]) + _dot(gi, b_ref[1])
        u = u_ref[...].astype(BF16)
        gb_ref[0] += _dot(gr, u, TN)
        gb_ref[1] += _dot(gi, u, TN)

        @pl.when(c == nc - 1)
        def _():
            ga_ref[0:1, :] = jnp.sum(acc_ref[0], axis=0, keepdims=True)
            ga_ref[1:2, :] = jnp.sum(acc_ref[1], axis=0, keepdims=True)

    rev = lambda j, c: (nc - 1 - c, j)
    col = pl.BlockSpec((tl, LANES), rev)
    return pl.pallas_call(
        body, name=name, grid=(S5_NS // SCAN_W, nc),
        in_specs=[col, pl.BlockSpec((2, LANES, SCAN_W), lambda j, c: (0, j, 0)),
                  pl.BlockSpec((2, tl, SCAN_W), lambda j, c: (0, nc - 1 - c, j)), col,
                  pl.BlockSpec((2, SCAN_W, LANES), lambda j, c: (0, j, 0)), col,
                  pl.BlockSpec((None, 8, 8, SCAN_W), lambda j, c: (1, 0, 0, j))],
        out_specs=[col, pl.BlockSpec((2, SCAN_W), lambda j, c: (0, j)),
                   pl.BlockSpec((2, SCAN_W, LANES), lambda j, c: (0, j, 0)),
                   pl.BlockSpec((2, LANES, SCAN_W), lambda j, c: (0, j, 0))],
        out_shape=[_sds((L, S5_W)), _sds((2, S5_NS)), _sds((2, S5_NS, LANES)), _sds((2, S5_W, 512))],
        scratch_shapes=[pltpu.VMEM((2, tl, SCAN_W), F32), pltpu.VMEM((2, 8, SCAN_W), F32),
                        pltpu.VMEM((2, 8, SCAN_W), F32)],
        compiler_params=_cp("parallel", "arbitrary"),
    )(gyl, cset, xs, z, bset, gud, tabs)


def _s5_glu_fwd(ylin, wglu, name):
    L = ylin.shape[0]
    bl = min(1024, L)

    def body(ylin_ref, w_ref, ya_ref):
        yg = _gelu(ylin_ref[...])
        t = _dot(yg.astype(BF16), w_ref[...])
        ya_ref[...] = (yg * _sigmoid(t)).astype(BF16)

    return pl.pallas_call(
        body, name=name, grid=(L // bl,),
        in_specs=[pl.BlockSpec((bl, S5_W), lambda i: (i, 0)), pl.BlockSpec((S5_W, S5_W), lambda i: (0, 0))],
        out_specs=pl.BlockSpec((bl, S5_W), lambda i: (i, 0)),
        out_shape=_sds((L, S5_W), BF16),
        compiler_params=_cp("parallel"),
    )(ylin, wglu)


def _s5_glu_bwd(g_y, wout, ylin, z, dvec, wglu, name):
    L = z.shape[0]
    bl = min(256, L)

    def body(g_ref, wo_ref, ylin_ref, u_ref, d_ref, w_ref, gyl_ref, gud_ref, gw_ref, gd_ref):
        i = pl.program_id(0)
        ylin = ylin_ref[...]
        yg = _gelu(ylin)
        ygb = yg.astype(BF16)
        sg = _sigmoid(_dot(ygb, w_ref[...]))
        gya = _dot(g_ref[...], wo_ref[...], NT)
        gt = gya * yg * sg * (1.0 - sg)
        gtb = gt.astype(BF16)
        gyg = gya * sg + _dot(gtb, w_ref[...], NT)
        gyl = gyg * _gelu_grad(ylin)
        gyl_ref[...] = gyl
        gud_ref[...] = gyl * d_ref[...]

        @pl.when(i == 0)
        def _():
            gw_ref[...] = jnp.zeros_like(gw_ref)
            gd_ref[...] = jnp.zeros_like(gd_ref)

        gw_ref[...] += _dot(ygb, gtb, TN)
        gd_ref[...] += jnp.sum(gyl * u_ref[...], axis=0, keepdims=True)

    blk = pl.BlockSpec((bl, S5_W), lambda i: (i, 0))
    return pl.pallas_call(
        body, name=name, grid=(L // bl,),
        in_specs=[pl.BlockSpec((bl, D_MODEL), lambda i: (i, 0)), pl.BlockSpec((S5_W, D_MODEL), lambda i: (0, 0)),
                  blk, blk, pl.BlockSpec((1, S5_W), lambda i: (0, 0)), pl.BlockSpec((S5_W, S5_W), lambda i: (0, 0))],
        out_specs=[blk, blk, pl.BlockSpec((S5_W, S5_W), lambda i: (0, 0)), pl.BlockSpec((1, S5_W), lambda i: (0, 0))],
        out_shape=[_sds((L, S5_W)), _sds((L, S5_W)), _sds((S5_W, S5_W)), _sds((1, S5_W))],
        compiler_params=_cp("arbitrary"),
    )(g_y, wout, ylin, z, dvec, wglu)


def _s5_param_bwd(lam_c, ldt_c, b_t, gb, ga_c, gc, name):
    def body(lam_ref, ldt_ref, b_ref, gb_ref, ga_ref, gc_ref, glam_ref, gldt_ref, gbo_ref, gco_ref):
        lr, li = lam_ref[:, 0:1], lam_ref[:, 1:2]
        dt, ar, ai, qr, qi, den = _zoh_cols(lr, li, ldt_ref[...])
        bm = _b_mask()
        gbr = jnp.where(bm, gb_ref[0], 0.0)
        gbi = jnp.where(bm, gb_ref[1], 0.0)
        br, bi = b_ref[0], b_ref[1]
        obr = gbr * qr + gbi * qi
        obi = gbi * qr - gbr * qi
        gqr = jnp.sum(gbr * br + gbi * bi, axis=1, keepdims=True)
        gqi = jnp.sum(gbi * br - gbr * bi, axis=1, keepdims=True)
        for s in (64, 32, 16):
            obr = obr + pltpu.roll(obr, s, 1)
            obi = obi + pltpu.roll(obi, s, 1)
        gbo_ref[0] = obr
        gbo_ref[1] = obi
        gar = ga_ref[:, 0:1] + (gqr * lr - gqi * li) / den
        gai = ga_ref[:, 1:2] + (gqr * li + gqi * lr) / den
        qlr = (qr * lr + qi * li) / den
        qli = (qi * lr - qr * li) / den
        glr = -(gqr * qlr + gqi * qli)
        gli = -(gqi * qlr - gqr * qli)
        glr = glr + dt * (gar * ar + gai * ai)
        gli = gli + dt * (gai * ar - gar * ai)
        wr, wi = _cmul(lr, li, ar, ai)
        gldt = (gar * wr + gai * wi) * dt
        glam_ref[:, 0:1] = glr
        glam_ref[:, 1:2] = gli
        r = lax.broadcasted_iota(jnp.int32, (S5_NS, 32), 0)
        c = lax.broadcasted_iota(jnp.int32, (S5_NS, 32), 1)
        gldt_ref[...] = jnp.sum(jnp.where((r >> 6) == c, gldt, 0.0), axis=0, keepdims=True)
        cm = _c_mask()
        for k in range(2):
            oc = jnp.where(cm, gc_ref[k], 0.0)
            for s in (256, 128, 64):
                oc = oc + pltpu.roll(oc, s, 1)
            gco_ref[k] = oc[:, 0:LANES]

    vm = pl.BlockSpec(memory_space=pltpu.VMEM)
    return pl.pallas_call(
        body, name=name, in_specs=[vm] * 6, out_specs=[vm] * 4,
        out_shape=[_sds((S5_NS, 2)), _sds((1, 32)), _sds((2, S5_NS, LANES)), _sds((2, S5_W, LANES))],
        compiler_params=pltpu.CompilerParams(vmem_limit_bytes=VMEM_LIMIT),
    )(lam_c, ldt_c, b_t, gb, ga_c, gc)


FL_BLK = EVEN_PAD // LANES - 1
Q_BLK, K_BLK, V_BLK = 4, 8, 12
NEG = -1e30


def _log_sigmoid(v):
    return jnp.minimum(v, 0.0) - jnp.log(1.0 + jnp.exp(-jnp.abs(v)))


def _fox_f_fwd(z, bf, name):
    L = z.shape[0]

    def body(fl_ref, b_ref, f_ref, fq_ref):
        row = lax.broadcasted_iota(jnp.int32, (L, LANES), 0)
        cs = _cumsum_rows(_log_sigmoid(fl_ref[...] + b_ref[...]), True, row)
        f_ref[...] = cs
        expand = (lax.broadcasted_iota(jnp.int32, (LANES, FOX_W), 0)
                  == (lax.broadcasted_iota(jnp.int32, (LANES, FOX_W), 1) >> 6)).astype(F32)
        fq_ref[...] = lax.dot_general(cs, expand, NN, precision=lax.Precision.HIGHEST, preferred_element_type=F32)

    return pl.pallas_call(
        body, name=name, grid=(1,),
        in_specs=[pl.BlockSpec((L, LANES), lambda i: (0, FL_BLK)), pl.BlockSpec((1, LANES), lambda i: (0, 0))],
        out_specs=[pl.BlockSpec((L, LANES), lambda i: (0, 0)), pl.BlockSpec((L, FOX_W), lambda i: (0, 0))],
        out_shape=[_sds((L, LANES)), _sds((L, FOX_W))],
        compiler_params=_cp("arbitrary"),
    )(z, bf)


def _fox_f_bwd(dFk, dfq, z, bf, name):
    L = z.shape[0]

    def body(dfk_ref, dfq_ref, fl_ref, b_ref, dfl_ref, db_ref):
        sel = (lax.broadcasted_iota(jnp.int32, (FOX_W, LANES), 0)
               == 64 * lax.broadcasted_iota(jnp.int32, (FOX_W, LANES), 1)).astype(F32)
        dfq_h = lax.dot_general(dfq_ref[...], sel, NN, precision=lax.Precision.HIGHEST, preferred_element_type=F32)
        row = lax.broadcasted_iota(jnp.int32, (L, LANES), 0)
        cs = _cumsum_rows(dfk_ref[...] + dfq_h, False, row)
        dfl = cs * _sigmoid(-(fl_ref[...] + b_ref[...]))
        dfl_ref[...] = dfl
        db_ref[...] = jnp.sum(dfl, axis=0, keepdims=True)

    return pl.pallas_call(
        body, name=name, grid=(1,),
        in_specs=[pl.BlockSpec((L, LANES), lambda i: (0, 0)), pl.BlockSpec((L, FOX_W), lambda i: (0, 0)),
                  pl.BlockSpec((L, LANES), lambda i: (0, FL_BLK)), pl.BlockSpec((1, LANES), lambda i: (0, 0))],
        out_specs=[pl.BlockSpec((L, LANES), lambda i: (0, 0)), pl.BlockSpec((1, LANES), lambda i: (0, 0))],
        out_shape=[_sds((L, LANES)), _sds((1, LANES))],
        compiler_params=_cp("arbitrary"),
    )(dFk, dfq, z, bf)


def _head_mask(hh):
    lane = lax.broadcasted_iota(jnp.int32, (1, LANES), 1)
    return (lane >> 6) == hh


FOX_T = 512


def _fox_head(x, hh):
    return jnp.where(_head_mask(hh), x, 0.0).astype(BF16)


def _fox_scores(qh, k, fq_ref, fr_ref, hh, causal):
    if fq_ref is None:
        s = _dot(qh, k, NT) - fr_ref[hh:hh + 1, :]
    else:
        s = _dot(qh, k, NT) + (fq_ref[:, 64 * hh:64 * hh + 1] - fr_ref[hh:hh + 1, :])
    return s if causal is None else jnp.where(causal, s, NEG)


def _causal(T):
    return lax.broadcasted_iota(jnp.int32, (T, T), 1) <= lax.broadcasted_iota(jnp.int32, (T, T), 0)


def _fox_fwd(z, fq, frow, name):
    L = z.shape[0]
    T = min(FOX_T, L)
    nq = L // T

    def body(qt_ref, kt_ref, q_ref, k_ref, v_ref, fq_ref, fr_ref, o_ref, lse_ref, m_ref, l_ref, acc_ref):
        t = pl.program_id(1)
        qi, ki = qt_ref[t], kt_ref[t]

        @pl.when(ki == 0)
        def _():
            m_ref[...] = jnp.full_like(m_ref, NEG)
            l_ref[...] = jnp.zeros_like(l_ref)
            acc_ref[...] = jnp.zeros_like(acc_ref)

        def step(diagonal):
            q = q_ref[...] * 0.125
            k = k_ref[...].astype(BF16)
            v = v_ref[...].astype(BF16)
            causal = _causal(T) if diagonal else None
            s = jnp.concatenate([_fox_scores(_fox_head(q, hh), k, fq_ref, fr_ref, hh, causal) for hh in range(2)],
                                axis=0)
            m_old = m_ref[...]
            m_new = jnp.maximum(m_old, jnp.max(s, axis=1, keepdims=True))
            alpha = jnp.exp(m_old - m_new)
            p = jnp.exp(s - m_new)
            l_ref[...] = alpha * l_ref[...] + jnp.sum(p, axis=1, keepdims=True)
            m_ref[...] = m_new
            acc_ref[...] = alpha * acc_ref[...] + _dot(p.astype(BF16), v)

        @pl.when(ki < qi)
        def _():
            step(False)

        @pl.when(ki == qi)
        def _():
            step(True)
            h0 = _head_mask(0)
            l = l_ref[...]
            o_h = acc_ref[...] / l
            lse_h = m_ref[...] + jnp.log(l)
            o_ref[...] = jnp.where(h0, o_h[:T], o_h[T:])
            lse_ref[...] = jnp.where(h0, lse_h[:T], lse_h[T:]) - fq_ref[...]

    pairs = [(qi, ki) for qi in range(nq) for ki in range(qi + 1)]
    qt = jnp.asarray([p[0] for p in pairs], jnp.int32)
    kt = jnp.asarray([p[1] for p in pairs], jnp.int32)

    def qspec(base):
        return pl.BlockSpec((T, LANES), lambda j, t, qt, kt: (qt[t], base + j))

    def kspec(base):
        return pl.BlockSpec((T, LANES), lambda j, t, qt, kt: (kt[t], base + j))

    return pl.pallas_call(
        body, name=name,
        grid_spec=pltpu.PrefetchScalarGridSpec(
            num_scalar_prefetch=2, grid=(4, len(pairs)),
            in_specs=[qspec(Q_BLK), kspec(K_BLK), kspec(V_BLK), qspec(0),
                      pl.BlockSpec((None, 2, T), lambda j, t, qt, kt: (j, 0, kt[t]))],
            out_specs=[qspec(0), qspec(0)],
            scratch_shapes=[pltpu.VMEM((2 * T, 1), F32), pltpu.VMEM((2 * T, 1), F32),
                            pltpu.VMEM((2 * T, LANES), F32)]),
        out_shape=[_sds((L, FOX_W)), _sds((L, FOX_W))],
        compiler_params=_cp("parallel", "arbitrary"),
    )(qt, kt, z, z, z, fq, frow)


def _fox_bwd(z, frow, o, lse, g_m, name):
    L = z.shape[0]
    T = min(FOX_T, L)
    nq = L // T

    pairs = [(qi, ki) for ki in range(nq) for qi in range(ki, nq)]
    qt = jnp.asarray([p[0] for p in pairs], jnp.int32)
    kt = jnp.asarray([p[1] for p in pairs], jnp.int32)

    def body(qt_ref, kt_ref, q_ref, k_ref, v_ref, fr_ref, o_ref, lse_ref, do_ref,
             dq_ref, dk_ref, dv_ref, dfq_ref, dfk_ref, dk_acc, dv_acc, df_acc):
        t = pl.program_id(1)
        qi, ki = qt_ref[t], kt_ref[t]

        @pl.when(t == 0)
        def _():
            dq_ref[...] = jnp.zeros_like(dq_ref)
            dfq_ref[...] = jnp.zeros_like(dfq_ref)

        @pl.when(qi == ki)
        def _():
            dk_acc[...] = jnp.zeros_like(dk_acc)
            dv_acc[...] = jnp.zeros_like(dv_acc)
            df_acc[...] = jnp.zeros_like(df_acc)

        def step(diagonal):
            q = q_ref[...] * 0.125
            qb = q.astype(BF16)
            k = k_ref[...].astype(BF16)
            v = v_ref[...].astype(BF16)
            do = do_ref[...]
            dob = do.astype(BF16)
            do_o = dob.astype(F32) * o_ref[...]
            causal = _causal(T) if diagonal else None
            dvs, dks, dqs, rss = [], [], [], []
            for hh in range(2):
                s = _fox_scores(_fox_head(q, hh), k, None, fr_ref, hh, causal)
                p = jnp.exp(s - lse_ref[:, 64 * hh:64 * hh + 1])
                dp = _dot(_fox_head(do, hh), v, NT)
                delta = jnp.sum(jnp.where(_head_mask(hh), do_o, 0.0), axis=1, keepdims=True)
                ds = p * (dp - delta)
                dsb = ds.astype(BF16)
                dvs.append(_dot(p.astype(BF16), dob, TN))
                dks.append(_dot(dsb, qb, TN))
                dqs.append(_dot(dsb, k))
                rss.append(jnp.sum(ds, axis=1, keepdims=True))
                df_acc[hh:hh + 1, :] -= jnp.sum(ds, axis=0, keepdims=True)
            h0 = _head_mask(0)
            dv_acc[...] += jnp.where(h0, dvs[0], dvs[1])
            dk_acc[...] += jnp.where(h0, dks[0], dks[1])
            rows = pl.ds(pl.multiple_of(qi * T, T), T)
            dq_ref[rows, :] += jnp.where(h0, dqs[0], dqs[1])
            dfq_ref[rows, :] += jnp.where(h0, rss[0], rss[1])

        @pl.when(qi > ki)
        def _():
            step(False)

        @pl.when(qi == ki)
        def _():
            step(True)

        @pl.when(qi == nq - 1)
        def _():
            dk_ref[...] = dk_acc[...]
            dv_ref[...] = dv_acc[...]
            dfk_ref[...] = df_acc[...]

        @pl.when(t == len(pairs) - 1)
        def _():
            dq_ref[...] = dq_ref[...] * 0.125

    def qside(base):
        return pl.BlockSpec((T, LANES), lambda j, t, qt, kt: (qt[t], base + j))

    def kside(base):
        return pl.BlockSpec((T, LANES), lambda j, t, qt, kt: (kt[t], base + j))

    pair = pl.BlockSpec((L, LANES), lambda j, t, qt, kt: (0, j))
    frow_spec = pl.BlockSpec((None, 2, T), lambda j, t, qt, kt: (j, 0, kt[t]))
    return pl.pallas_call(
        body, name=name,
        grid_spec=pltpu.PrefetchScalarGridSpec(
            num_scalar_prefetch=2, grid=(4, len(pairs)),
            in_specs=[qside(Q_BLK), kside(K_BLK), kside(V_BLK), frow_spec, qside(0), qside(0), qside(0)],
            out_specs=[pair, kside(0), kside(0), pair, frow_spec],
            scratch_shapes=[pltpu.VMEM((T, LANES), F32), pltpu.VMEM((T, LANES), F32), pltpu.VMEM((2, T), F32)]),
        out_shape=[_sds((L, FOX_W)), _sds((L, FOX_W)), _sds((L, FOX_W)), _sds((L, FOX_W)), _sds((4, 2, L))],
        compiler_params=_cp("parallel", "arbitrary"),
    )(qt, kt, z, z, z, frow, o, lse, g_m)


def _shift_rows(v, s, down, row):
    n = v.shape[0]
    if down:
        return jnp.where(row >= s, pltpu.roll(v, s, 0), 0.0)
    return jnp.where(row < n - s, pltpu.roll(v, n - s, 0), 0.0)


def _cumsum_rows(v, down, row):
    s = 1
    while s < v.shape[0]:
        v = v + _shift_rows(v, s, down, row)
        s *= 2
    return v


def _window_sum(v, g, down, row):
    out = jnp.zeros_like(v)
    s = v
    for k in range(4):
        s = s + _shift_rows(s, 1 << k, down, row)
        out = jnp.where(g == k, s, out)
    return out


def _pool_inv_cnt(g, row):
    w = jnp.left_shift(2, g).astype(F32)
    return 1.0 / jnp.minimum(row.astype(F32) + 1.0, w)


def _pool_fwd(z, pool_w, scale, name):
    L = z.shape[0]

    def body(x_ref, w_ref, s_ref, y_ref, p_ref):
        g = pl.program_id(0)
        row = lax.broadcasted_iota(jnp.int32, (L, LANES), 0)
        x = x_ref[...]
        pooled = (_window_sum(x, g, True, row) * _pool_inv_cnt(g, row) - x).astype(BF16)
        p_ref[...] = pooled
        y_ref[...] = (_dot(pooled, w_ref[...].astype(BF16)) * s_ref[...]).astype(BF16)

    col = pl.BlockSpec((L, LANES), lambda g: (0, g))
    return pl.pallas_call(
        body, name=name, grid=(4,),
        in_specs=[col, pl.BlockSpec((None, LANES, LANES), lambda g: (g, 0, 0)), pl.BlockSpec((1, LANES), lambda g: (0, g))],
        out_specs=[col, col],
        out_shape=[_sds((L, 512), BF16), _sds((L, 512), BF16)],
        compiler_params=_cp("parallel"),
    )(z, pool_w, scale)


def _pool_bwd(g_y, wout, pooled, pool_w, scale, name):
    L = g_y.shape[0]

    def body(g_ref, wo_ref, p_ref, w_ref, s_ref, gx_ref, gw_ref, gs_ref):
        g = pl.program_id(0)
        row = lax.broadcasted_iota(jnp.int32, (L, LANES), 0)
        gy = _dot(g_ref[...], wo_ref[...], NT)
        pooled = p_ref[...]
        wb = w_ref[...].astype(BF16)
        lin = _dot(pooled, wb)
        gs_ref[...] = jnp.sum(gy * lin, axis=0, keepdims=True)
        glin = (gy * s_ref[...]).astype(BF16)
        gw_ref[...] = _dot(pooled, glin, TN)
        gp = _dot(glin, wb, NT)
        gx_ref[...] = _window_sum(gp * _pool_inv_cnt(g, row), g, False, row) - gp

    col = pl.BlockSpec((L, LANES), lambda g: (0, g))
    wspec = pl.BlockSpec((None, LANES, LANES), lambda g: (g, 0, 0))
    vec = pl.BlockSpec((1, LANES), lambda g: (0, g))
    return pl.pallas_call(
        body, name=name, grid=(4,),
        in_specs=[pl.BlockSpec((L, D_MODEL), lambda g: (0, 0)), pl.BlockSpec((LANES, D_MODEL), lambda g: (g, 0)),
                  col, wspec, vec],
        out_specs=[col, wspec, vec],
        out_shape=[_sds((L, 512)), _sds((4, LANES, LANES)), _sds((1, 512))],
        compiler_params=_cp("parallel"),
    )(g_y, wout, pooled, pool_w, scale)


SGU_CHUNKS = 4


def _sgu_ln(v, gam, bet):
    gv = _gelu(v)
    mu = jnp.mean(gv, axis=-1, keepdims=True)
    xc = gv - mu
    rs = lax.rsqrt(jnp.mean(xc * xc, axis=-1, keepdims=True) + EPS)
    xh = xc * rs
    return xh, rs, xh * gam + bet


def _tril_ws(w_ref, g):
    r = lax.broadcasted_iota(jnp.int32, (LANES, LANES), 0)
    c = lax.broadcasted_iota(jnp.int32, (LANES, LANES), 1)
    return jnp.where(r >= c, w_ref[g], 0.0).astype(BF16)


def _sgu_fwd(z, ln_g, ln_b, w_s, b_st, name):
    L = z.shape[0]
    rb = min(SGU_CHUNKS * LANES, L)

    def body(u_ref, v_ref, g_ref, b_ref, w_ref, bs_ref, y_ref):
        _, _, vln = _sgu_ln(v_ref[...], g_ref[...], b_ref[...])
        gu = _gelu(u_ref[...])
        vb = vln.astype(BF16)
        for g in range(4):
            ws = _tril_ws(w_ref, g)
            for n in range(rb // LANES):
                rows = slice(n * LANES, (n + 1) * LANES)
                cols = slice(g * LANES, (g + 1) * LANES)
                mixed = _dot(ws, vb[rows, cols]) + bs_ref[:, g:g + 1]
                y_ref[rows, cols] = (gu[rows, cols] * mixed).astype(BF16)

    vm = lambda shape: pl.BlockSpec(shape, lambda i: tuple(0 for _ in shape))
    return pl.pallas_call(
        body, name=name, grid=(L // rb,),
        in_specs=[pl.BlockSpec((rb, 512), lambda i: (i, 1)), pl.BlockSpec((rb, 512), lambda i: (i, 2)),
                  vm((1, 512)), vm((1, 512)), vm((4, LANES, LANES)), vm((LANES, 4))],
        out_specs=pl.BlockSpec((rb, 512), lambda i: (i, 0)),
        out_shape=_sds((L, 512), BF16),
        compiler_params=_cp("parallel"),
    )(z, z, ln_g, ln_b, w_s, b_st)


def _sgu_bwd(g_y, wout, z, ln_g, ln_b, w_s, b_st, name):
    L = z.shape[0]
    rb = min(SGU_CHUNKS * LANES, L)

    def body(gyo_ref, wo_ref, u_ref, v_ref, g_ref, b_ref, w_ref, bs_ref, gu_ref, gv_ref, gw_ref, gbs_ref, gg_ref,
             gb_ref):
        i = pl.program_id(0)

        @pl.when(i == 0)
        def _():
            gw_ref[...] = jnp.zeros_like(gw_ref)
            gbs_ref[...] = jnp.zeros_like(gbs_ref)
            gg_ref[...] = jnp.zeros_like(gg_ref)
            gb_ref[...] = jnp.zeros_like(gb_ref)

        v = v_ref[...]
        u = u_ref[...]
        gy = _dot(gyo_ref[...], wo_ref[...], NT)
        xh, rs, vln = _sgu_ln(v, g_ref[...], b_ref[...])
        gel_u = _gelu(u)
        gmix = gy * gel_u
        vb = vln.astype(BF16)
        gmb = gmix.astype(BF16)
        r = lax.broadcasted_iota(jnp.int32, (LANES, LANES), 0)
        c = lax.broadcasted_iota(jnp.int32, (LANES, LANES), 1)
        gvln_cols = []
        for g in range(4):
            ws = _tril_ws(w_ref, g)
            cols = slice(g * LANES, (g + 1) * LANES)
            gw = jnp.zeros((LANES, LANES), F32)
            gbs = jnp.zeros((LANES, 1), F32)
            parts = []
            for n in range(rb // LANES):
                rows = slice(n * LANES, (n + 1) * LANES)
                mixed = _dot(ws, vb[rows, cols]) + bs_ref[:, g:g + 1]
                gu_ref[rows, cols] = gy[rows, cols] * mixed * _gelu_grad(u[rows, cols])
                parts.append(_dot(ws, gmb[rows, cols], TN))
                gw = gw + _dot(gmb[rows, cols], vb[rows, cols], NT)
                gbs = gbs + jnp.sum(gmix[rows, cols], axis=1, keepdims=True)
            gvln_cols.append(jnp.concatenate(parts, axis=0))
            gw_ref[g] += jnp.where(r >= c, gw, 0.0)
            gbs_ref[:, g:g + 1] += gbs
        gvln = jnp.concatenate(gvln_cols, axis=1)
        gg_ref[...] += jnp.sum(gvln * xh, axis=0, keepdims=True)
        gb_ref[...] += jnp.sum(gvln, axis=0, keepdims=True)
        gxh = gvln * g_ref[...]
        ggv = rs * (gxh - jnp.mean(gxh, axis=-1, keepdims=True) - xh * jnp.mean(gxh * xh, axis=-1, keepdims=True))
        gv_ref[...] = ggv * _gelu_grad(v)

    vm = lambda shape: pl.BlockSpec(shape, lambda i: tuple(0 for _ in shape))
    blk = pl.BlockSpec((rb, 512), lambda i: (i, 0))
    return pl.pallas_call(
        body, name=name, grid=(L // rb,),
        in_specs=[pl.BlockSpec((rb, D_MODEL), lambda i: (i, 0)), pl.BlockSpec((512, D_MODEL), lambda i: (1, 0)),
                  pl.BlockSpec((rb, 512), lambda i: (i, 1)), pl.BlockSpec((rb, 512), lambda i: (i, 2)),
                  vm((1, 512)), vm((1, 512)), vm((4, LANES, LANES)), vm((LANES, 4))],
        out_specs=[blk, blk, vm((4, LANES, LANES)), vm((LANES, 4)), vm((1, 512)), vm((1, 512))],
        out_shape=[_sds((L, 512)), _sds((L, 512)), _sds((4, LANES, LANES)), _sds((LANES, 4)),
                   _sds((1, 512)), _sds((1, 512))],
        compiler_params=_cp("arbitrary"),
    )(g_y, wout, z, z, ln_g, ln_b, w_s, b_st)


def _adamw_math(w, g, m, v):
    nm = ADAM_B1 * m + (1.0 - ADAM_B1) * g
    nv = ADAM_B2 * v + (1.0 - ADAM_B2) * (g * g)
    m_hat = nm / (1.0 - ADAM_B1 ** ADAM_STEP)
    v_hat = nv / (1.0 - ADAM_B2 ** ADAM_STEP)
    delta = -ADAM_LR * (m_hat / (jnp.sqrt(v_hat) + ADAM_EPS) + ADAM_WD * w)
    return delta, nm, nv


def _sum_adamw(parts, w, m, v, name, layer=0, prev=None):
    n_layers, R, C = w.shape
    rb = 128 if R % 128 == 0 else R

    def body(p_ref, w_ref, m_ref, v_ref, *rest):
        g_ref, d_ref, nm_ref, nv_ref = rest[-4:]
        g = p_ref[0].astype(F32)
        for s in range(1, N_DEV):
            g = g + p_ref[s].astype(F32)
        d, nm, nv = _adamw_math(w_ref[...], g, m_ref[...], v_ref[...])
        g_ref[...] = g
        d_ref[...] = d
        nm_ref[...] = nm
        nv_ref[...] = nv

    blk = pl.BlockSpec((None, rb, C), lambda i: (layer, i, 0))
    prev = [] if prev is None else list(prev)
    return pl.pallas_call(
        body, name=name, grid=(R // rb,),
        in_specs=[pl.BlockSpec((N_DEV, rb, C), lambda i: (0, i, 0)), blk, blk, blk] + [ANY] * len(prev),
        out_specs=[blk] * 4, out_shape=[_sds((n_layers, R, C))] * 4,
        input_output_aliases={4 + k: k for k in range(len(prev))},
        compiler_params=_cp("parallel"),
    )(parts, w, m, v, *prev)


def _sum_adamw_rows(parts, w, m, v, name):
    R, _, C = w.shape

    def body(p_ref, w_ref, m_ref, v_ref, g_ref, d_ref, nm_ref, nv_ref):
        g = p_ref[0].astype(F32)
        for s in range(1, N_DEV):
            g = g + p_ref[s].astype(F32)
        d, nm, nv = _adamw_math(w_ref[:, 0, :], g, m_ref[:, 0, :], v_ref[:, 0, :])
        g_ref[:, 0, :] = g
        d_ref[:, 0, :] = d
        nm_ref[:, 0, :] = nm
        nv_ref[:, 0, :] = nv

    vm = pl.BlockSpec(memory_space=pltpu.VMEM)
    return pl.pallas_call(body, name=name, in_specs=[vm] * 4, out_specs=[vm] * 4, out_shape=[_sds((R, 1, C))] * 4,
                          compiler_params=pltpu.CompilerParams(vmem_limit_bytes=VMEM_LIMIT))(parts, w, m, v)


def _sum_pieces(parts, name):
    _, R, C = parts.shape

    def body(p_ref, g_ref):
        g = p_ref[0].astype(F32)
        for s in range(1, N_DEV):
            g = g + p_ref[s].astype(F32)
        g_ref[...] = g

    vm = pl.BlockSpec(memory_space=pltpu.VMEM)
    return pl.pallas_call(body, name=name, in_specs=[vm], out_specs=vm, out_shape=_sds((R, C)),
                          compiler_params=pltpu.CompilerParams(vmem_limit_bytes=VMEM_LIMIT))(parts)


def _adamw_many(ws, gs, ms, vs, name):
    n = len(ws)
    vm = pl.BlockSpec(memory_space=pltpu.VMEM)

    def body(*refs):
        w_refs, g_refs, m_refs, v_refs = refs[:n], refs[n:2 * n], refs[2 * n:3 * n], refs[3 * n:4 * n]
        d_refs, nm_refs, nv_refs = refs[4 * n:5 * n], refs[5 * n:6 * n], refs[6 * n:7 * n]
        for i in range(n):
            d, nm, nv = _adamw_math(w_refs[i][...], g_refs[i][...], m_refs[i][...], v_refs[i][...])
            d_refs[i][...] = d
            nm_refs[i][...] = nm
            nv_refs[i][...] = nv

    shapes = [_sds(w.shape) for w in ws]
    outs = pl.pallas_call(
        body, name=name, in_specs=[vm] * (4 * n), out_specs=[vm] * (3 * n), out_shape=shapes * 3,
        compiler_params=pltpu.CompilerParams(vmem_limit_bytes=VMEM_LIMIT),
    )(*ws, *gs, *ms, *vs)
    return list(outs[:n]), list(outs[n:2 * n]), list(outs[2 * n:])


def _mesh_pos():
    return lax.axis_index("x"), lax.axis_index("y"), lax.axis_index("c")


def _dev_index(p):
    return 4 * p[0] + 2 * p[1] + p[2]


HBM = pl.BlockSpec(memory_space=pltpu.HBM)
SEM = pl.BlockSpec(memory_space=pltpu.SEMAPHORE)
EFFECT = pltpu.SideEffectType.DATAFLOW_SIDE_EFFECTING


def _peer_list():
    x, y, c = _mesh_pos()
    peers = [(x ^ dx, y ^ dy, c ^ dc) for dx in range(2) for dy in range(2) for dc in range(2)][1:]
    return (x, y, c), peers


def _split_copy(src_ref, land_ref, send_sems, recv_sems, i, k, peer, slot, exchange):
    return pltpu.make_async_remote_copy(
        src_ref=src_ref.at[_dev_index(peer)] if exchange else src_ref, dst_ref=land_ref.at[slot],
        send_sem=send_sems.at[7 * i + k], recv_sem=recv_sems.at[7 * i + k], device_id=peer, device_id_type=MESH)


def _own_copy(src_ref, land_ref, own_sems, i, slot, exchange):
    return pltpu.make_async_copy(src_ref.at[slot] if exchange else src_ref, land_ref.at[slot], own_sems.at[i])


def _comm_start(groups, name, exchange, dep=None):
    sizes = [len(g) for g in groups]
    n = sum(sizes)
    srcs = [a for g in groups for a in g]
    per_group = [exchange] * len(groups) if isinstance(exchange, bool) else list(exchange)
    exchanged = [flag for flag, sz in zip(per_group, sizes) for _ in range(sz)]
    lands = [lax.empty(a.shape if ex else (N_DEV,) + a.shape, a.dtype) for a, ex in zip(srcs, exchanged)]

    n_dep = 0 if dep is None else 1

    def body(*refs):
        src_refs, land_refs = refs[:n], refs[n:2 * n]
        sem_refs = refs[2 * n + n_dep:2 * n + n_dep + 3 * len(sizes)]
        token_ref = refs[-1]
        me, peers = _peer_list()
        mi = _dev_index(me)
        i = 0
        for gi, sz in enumerate(sizes):
            for j in range(sz):
                for k, peer in enumerate(peers):
                    _split_copy(src_refs[i], land_refs[i], sem_refs[3 * gi], sem_refs[3 * gi + 1], j, k, peer, mi,
                                exchanged[i]).start()
                _own_copy(src_refs[i], land_refs[i], sem_refs[3 * gi + 2], j, mi, exchanged[i]).start()
                i += 1
        token_ref[...] = jnp.zeros_like(token_ref)

    sem_shapes = []
    for sz in sizes:
        sem_shapes += [pltpu.SemaphoreType.DMA((7 * sz,)), pltpu.SemaphoreType.DMA((7 * sz,)),
                       pltpu.SemaphoreType.DMA((sz,))]
    thru = [pltpu.HBM(a.shape, a.dtype) for a in srcs + lands]
    n_sem = len(sem_shapes)
    outs = pl.pallas_call(
        body, name=name,
        out_shape=tuple(sem_shapes + thru + [_sds((8, LANES))]),
        in_specs=[HBM] * (2 * n) + [ANY] * n_dep,
        out_specs=tuple([SEM] * n_sem + [HBM] * (2 * n) + [pl.BlockSpec(memory_space=pltpu.VMEM)]),
        input_output_aliases={i: n_sem + i for i in range(2 * n)},
        compiler_params=pltpu.CompilerParams(has_side_effects=EFFECT),
    )(*[pltpu.with_memory_space_constraint(a, pltpu.HBM) for a in srcs + lands], *([] if dep is None else [dep]))
    sems, thru_src, thru_land, token = outs[:n_sem], outs[n_sem:n_sem + n], outs[n_sem + n:n_sem + 2 * n], outs[-1]
    result, off = [], 0
    for gi, sz in enumerate(sizes):
        result.append((*sems[3 * gi:3 * gi + 3], list(thru_src[off:off + sz]), list(thru_land[off:off + sz])))
        off += sz
    return result, token


def _comm_wait(group, after, name, exchange):
    send_sems, recv_sems, own_sems, srcs, lands = group
    n = len(srcs)
    after = list(after) if isinstance(after, (list, tuple)) else [after]

    def body(*refs):
        src_refs, land_refs = refs[:n], refs[n:2 * n]
        ssem, rsem, osem = refs[2 * n:2 * n + 3]
        me, peers = _peer_list()
        for i in range(n):
            for k, peer in enumerate(peers):
                cp = _split_copy(src_refs[i], land_refs[i], ssem, rsem, i, k, peer, _dev_index(peer), exchange)
                cp.wait_send()
                cp.wait_recv()
            _own_copy(src_refs[i], land_refs[i], osem, i, _dev_index(me), exchange).wait()

    outs = pl.pallas_call(
        body, name=name,
        out_shape=tuple(pltpu.HBM(a.shape, a.dtype) for a in srcs + lands),
        in_specs=[HBM] * (2 * n) + [SEM, SEM, SEM] + [ANY] * len(after),
        out_specs=tuple([HBM] * (2 * n)),
        input_output_aliases={i: i for i in range(2 * n)},
        compiler_params=pltpu.CompilerParams(has_side_effects=EFFECT),
    )(*srcs, *lands, send_sems, recv_sems, own_sems, *after)
    return list(outs[n:])


def _tie(a, token):
    return a + token[0, 0].astype(a.dtype)


def _pack(arrs, rows):
    flat = jnp.concatenate([a.reshape(-1).astype(F32) for a in arrs])
    return jnp.pad(flat, (0, rows * LANES - flat.shape[0])).reshape(rows, LANES)


def _unpack(packed, shapes):
    flat = packed.reshape(-1)
    out, off = [], 0
    for s in shapes:
        n = math.prod(s)
        out.append(flat[off:off + n].reshape(s))
        off += n
    return out


def _packed_rows(shapes):
    n = sum(math.prod(s) for s in shapes)
    unit = N_DEV * 8 * LANES
    return -(-n // unit) * unit // LANES


def kernel(x, mix_pre_g, mix_post_g, mlp_pre_g, mlp_post_g, w_in_even, s5_lam_re, s5_lam_im, s5_log_dt, s5_b_re, s5_b_im, s5_c_re, s5_c_im, s5_d, s5_w_glu, fox_b_f, w_out_even, w_in_odd, pool_w, pool_scale, sgu_ln_g, sgu_ln_b, sgu_w_s, sgu_b_s, w_out_odd, mlp_w1, mlp_w2, loss_target, m_mix_pre_g, m_mix_post_g, m_mlp_pre_g, m_mlp_post_g, m_w_in_even, m_s5_lam_re, m_s5_lam_im, m_s5_log_dt, m_s5_b_re, m_s5_b_im, m_s5_c_re, m_s5_c_im, m_s5_d, m_s5_w_glu, m_fox_b_f, m_w_out_even, m_w_in_odd, m_pool_w, m_pool_scale, m_sgu_ln_g, m_sgu_ln_b, m_sgu_w_s, m_sgu_b_s, m_w_out_odd, m_mlp_w1, m_mlp_w2, v_mix_pre_g, v_mix_post_g, v_mlp_pre_g, v_mlp_post_g, v_w_in_even, v_s5_lam_re, v_s5_lam_im, v_s5_log_dt, v_s5_b_re, v_s5_b_im, v_s5_c_re, v_s5_c_im, v_s5_d, v_s5_w_glu, v_fox_b_f, v_w_out_even, v_w_in_odd, v_pool_w, v_pool_scale, v_sgu_ln_g, v_sgu_ln_b, v_sgu_w_s, v_sgu_b_s, v_w_out_odd, v_mlp_w1, v_mlp_w2):
    weights = dict(mix_pre_g=mix_pre_g, mix_post_g=mix_post_g, mlp_pre_g=mlp_pre_g, mlp_post_g=mlp_post_g, w_in_even=w_in_even, s5_lam_re=s5_lam_re, s5_lam_im=s5_lam_im, s5_log_dt=s5_log_dt, s5_b_re=s5_b_re, s5_b_im=s5_b_im, s5_c_re=s5_c_re, s5_c_im=s5_c_im, s5_d=s5_d, s5_w_glu=s5_w_glu, fox_b_f=fox_b_f, w_out_even=w_out_even, w_in_odd=w_in_odd, pool_w=pool_w, pool_scale=pool_scale, sgu_ln_g=sgu_ln_g, sgu_ln_b=sgu_ln_b, sgu_w_s=sgu_w_s, sgu_b_s=sgu_b_s, w_out_odd=w_out_odd, mlp_w1=mlp_w1, mlp_w2=mlp_w2)
    mom_m = dict(mix_pre_g=m_mix_pre_g, mix_post_g=m_mix_post_g, mlp_pre_g=m_mlp_pre_g, mlp_post_g=m_mlp_post_g, w_in_even=m_w_in_even, s5_lam_re=m_s5_lam_re, s5_lam_im=m_s5_lam_im, s5_log_dt=m_s5_log_dt, s5_b_re=m_s5_b_re, s5_b_im=m_s5_b_im, s5_c_re=m_s5_c_re, s5_c_im=m_s5_c_im, s5_d=m_s5_d, s5_w_glu=m_s5_w_glu, fox_b_f=m_fox_b_f, w_out_even=m_w_out_even, w_in_odd=m_w_in_odd, pool_w=m_pool_w, pool_scale=m_pool_scale, sgu_ln_g=m_sgu_ln_g, sgu_ln_b=m_sgu_ln_b, sgu_w_s=m_sgu_w_s, sgu_b_s=m_sgu_b_s, w_out_odd=m_w_out_odd, mlp_w1=m_mlp_w1, mlp_w2=m_mlp_w2)
    mom_v = dict(mix_pre_g=v_mix_pre_g, mix_post_g=v_mix_post_g, mlp_pre_g=v_mlp_pre_g, mlp_post_g=v_mlp_post_g, w_in_even=v_w_in_even, s5_lam_re=v_s5_lam_re, s5_lam_im=v_s5_lam_im, s5_log_dt=v_s5_log_dt, s5_b_re=v_s5_b_re, s5_b_im=v_s5_b_im, s5_c_re=v_s5_c_re, s5_c_im=v_s5_c_im, s5_d=v_s5_d, s5_w_glu=v_s5_w_glu, fox_b_f=v_fox_b_f, w_out_even=v_w_out_even, w_in_odd=v_w_in_odd, pool_w=v_pool_w, pool_scale=v_pool_scale, sgu_ln_g=v_sgu_ln_g, sgu_ln_b=v_sgu_ln_b, sgu_w_s=v_sgu_w_s, sgu_b_s=v_sgu_b_s, w_out_odd=v_w_out_odd, mlp_w1=v_mlp_w1, mlp_w2=v_mlp_w2)
    names = list(weights)
    L = x.shape[1]
    x0 = x[0]
    target = loss_target[0]
    my_index = 4 * lax.axis_index("x") + 2 * lax.axis_index("y") + lax.axis_index("c")

    small_vec = jnp.zeros((8, LANES), F32)
    small_vec = small_vec.at[0, :64].set(pool_scale[0]).at[1, :64].set(sgu_ln_g[0]).at[2, :64].set(sgu_ln_b[0])
    ag_groups, ag_token = _comm_start(
        [[jnp.transpose(w_in_even[0]).astype(BF16), small_vec],
         [s5_w_glu[0].astype(BF16), w_out_even[0].astype(BF16)],
         [mlp_w1[0].astype(BF16), mlp_w2[0].astype(BF16)],
         [jnp.transpose(w_in_odd[0]).astype(BF16), w_out_odd[0].astype(BF16), mlp_w1[1].astype(BF16), mlp_w2[1].astype(BF16)]],
        "ag_start", exchange=False)

    lam_r = jnp.concatenate([s5_lam_re.reshape(1, S5_NS), s5_lam_im.reshape(1, S5_NS)], axis=0)
    ldt_r = jnp.repeat(s5_log_dt.reshape(32), 64).reshape(1, S5_NS)
    lam_c = jnp.transpose(lam_r)
    ldt_c = jnp.transpose(ldt_r)
    b_t = jnp.stack([jnp.tile(s5_b_re.reshape(S5_NS, 16), (1, 8)), jnp.tile(s5_b_im.reshape(S5_NS, 16), (1, 8))])
    c_t = jnp.stack([jnp.tile(s5_c_re.reshape(S5_W, 64), (1, 8)), jnp.tile(s5_c_im.reshape(S5_W, 64), (1, 8))])
    bf_pad = jnp.pad(fox_b_f, ((0, 0), (0, LANES - 8)))
    b_st = jnp.transpose(sgu_b_s[0])

    h0, rx0 = _rms_fwd(x0, _tie(mix_pre_g[0:1], ag_token), "rms0")
    tabs, bset, cset = _s5_prep(lam_r, ldt_r, lam_c, ldt_c, b_t, c_t, "s5_prep")
    ag0 = _comm_wait(ag_groups[0], tabs, "ag_wait0", exchange=False)
    winT_e = jnp.pad(ag0[0].reshape(EVEN_IN, D_MODEL), ((0, EVEN_PAD - EVEN_IN), (0, 0)))
    pool_scale_f = ag0[1][:, 0, :64].reshape(1, 512)
    ln_g_f = ag0[1][:, 1, :64].reshape(1, 512)
    ln_b_f = ag0[1][:, 2, :64].reshape(1, 512)
    z0 = _mm(h0, winT_e, name="win_even", tb=True, bm=512, bn=EVEN_PAD)
    xs, ylin = _s5_scan_fwd(z0, bset, cset, s5_d, tabs, "s5_scan")
    ag1 = _comm_wait(ag_groups[1], ylin, "ag_wait1", exchange=False)
    wglu = ag1[0].reshape(S5_W, S5_W)
    wout_e = ag1[1].reshape(D_MODEL, D_MODEL)
    ya = _s5_glu_fwd(ylin, wglu, "s5_glu")
    fcum, fq = _fox_f_fwd(z0, bf_pad, "fox_f")
    frow = jnp.transpose(fcum[:, :8]).reshape(4, 2, L)
    o_att, lse = _fox_fwd(z0, fq, frow, "fox_fwd")
    mix0 = [ya, o_att]
    x1, ry0, h1, rx1, y0 = _mm(mix0, wout_e, name="wout_even", epi=_epi_post_pre, extra=(x0,),
                               vecs=(mix_post_g[0:1], mlp_pre_g[0:1]), out_dtypes=POST_PRE_DTYPES,
                               out_kinds=POST_PRE_KINDS, bm=FUSED_ROWS)
    ag2 = _comm_wait(ag_groups[2], rx1, "ag_wait2", exchange=False)
    w1 = [ag2[0], None]
    w2 = [ag2[1].reshape(4 * D_MODEL, D_MODEL), None]
    p0, a0 = _mm(h1, w1[0], name="mlp0_w1", b3=True, out_dtypes=(BF16, BF16), epi=_epi_relu2, bm=512, bn=4 * D_MODEL)
    x2, ro0, h2, rx2, o0 = _mm(a0, w2[0], name="mlp0_w2", epi=_epi_post_pre, extra=(x1,),
                               vecs=(mlp_post_g[0:1], mix_pre_g[1:2]), out_dtypes=POST_PRE_DTYPES,
                               out_kinds=POST_PRE_KINDS, bm=FUSED_ROWS, bk=4 * D_MODEL)
    ag3 = _comm_wait(ag_groups[3], rx2, "ag_wait3", exchange=False)
    winT_o = ag3[0].reshape(ODD_IN, D_MODEL)
    wout_o = ag3[1].reshape(D_MODEL, D_MODEL)
    w1[1] = ag3[2]
    w2[1] = ag3[3].reshape(4 * D_MODEL, D_MODEL)
    z1 = _mm(h2, winT_o, name="win_odd", tb=True, bn=ODD_IN)
    yc, pooled = _pool_fwd(z1, pool_w[0], pool_scale_f, "pool_fwd")
    yd = _sgu_fwd(z1, ln_g_f, ln_b_f, sgu_w_s[0], b_st, "sgu_fwd")
    mix1 = [yc, yd]
    x3, ry1, h3, rx3, y1 = _mm(mix1, wout_o, name="wout_odd", epi=_epi_post_pre, extra=(x2,),
                               vecs=(mix_post_g[1:2], mlp_pre_g[1:2]), out_dtypes=POST_PRE_DTYPES,
                               out_kinds=POST_PRE_KINDS, bm=FUSED_ROWS)
    p1, a1 = _mm(h3, w1[1], name="mlp1_w1", b3=True, out_dtypes=(BF16, BF16), epi=_epi_relu2, bm=512, bn=4 * D_MODEL)
    gx4, g_o1, gg_mlp_post1, sq_lanes = _mm(
        a1, w2[1], name="mlp1_w2", epi=_epi_post_loss, extra=(x3, target), vecs=(mlp_post_g[1:2],),
        out_dtypes=(F32, BF16, F32, F32), out_kinds=("full", "full", "vsum", "vsum"), bm=FUSED_ROWS, bk=4 * D_MODEL)
    sq = sq_lanes[:, 0:1]

    g_p1 = _mm(g_o1, w2[1], name="b_mlp1_a", tb=True, out_dtypes=(BF16,), epi=_epi_relu2_bwd, extra=(p1,),
               bm=512, bn=4 * D_MODEL)
    gw2_1 = _mm(a1, g_o1, name="b_mlp1_w2", ta=True, bm=512, bk=L)
    gw1_1 = _mm(h3, g_p1, name="b_mlp1_w1", ta=True, out3=True, bn=512, bk=L)
    (ex1,), tok1 = _comm_start([[gw1_1, gw2_1.reshape(N_DEV, 512, D_MODEL)]], "ex_start1", exchange=True)
    g_x3, gg_mlp_pre1, g_y1, gg_mix_post1 = _mm(
        g_p1, w1[1], name="b_mlp1_h", tb=True, b3=True, epi=_epi_pre_post_bwd, extra=(x3, gx4, y1), cols=(rx3, ry1),
        vecs=(_tie(mlp_pre_g[1:2], tok1), mix_post_g[1:2]), out_dtypes=PRE_POST_BWD_DTYPES,
        out_kinds=PRE_POST_BWD_KINDS, bm=FUSED_ROWS, bk=4 * D_MODEL)
    gwout_o = _mm(mix1, g_y1, name="b_wout_odd_w", ta=True)
    g_xc, g_pool_w, g_pool_scale = _pool_bwd(g_y1, wout_o, pooled, pool_w[0], pool_scale_f, "pool_bwd")
    g_u1, g_v1, g_ws, g_bst, g_ln_g, g_ln_b = _sgu_bwd(g_y1, wout_o, z1, ln_g_f, ln_b_f, sgu_w_s[0], b_st,
                                                       "sgu_bwd")
    g_z1 = [g_xc, g_u1, g_v1]
    gwinT_o = _mm(g_z1, h2, name="b_win_odd_w", ta=True)
    (ex2,), tok2 = _comm_start([[gwout_o.reshape(N_DEV, 128, D_MODEL), gwinT_o.reshape(N_DEV, ODD_IN // N_DEV, D_MODEL)]], "ex_start2", exchange=True)
    g_x2, gg_mix_pre1, g_o0, gg_mlp_post0 = _mm(
        g_z1, winT_o, name="b_win_odd_h", epi=_epi_pre_post_bwd, extra=(x2, g_x3, o0), cols=(rx2, ro0),
        vecs=(_tie(mix_pre_g[1:2], tok2), mlp_post_g[0:1]), out_dtypes=PRE_POST_BWD_DTYPES,
        out_kinds=PRE_POST_BWD_KINDS, bm=FUSED_ROWS)
    g_p0 = _mm(g_o0, w2[0], name="b_mlp0_a", tb=True, out_dtypes=(BF16,), epi=_epi_relu2_bwd, extra=(p0,),
               bm=512, bn=4 * D_MODEL)
    gw2_0 = _mm(a0, g_o0, name="b_mlp0_w2", ta=True, bm=512, bk=L)
    gw1_0 = _mm(h1, g_p0, name="b_mlp0_w1", ta=True, out3=True, bn=512, bk=L)
    (ex3,), tok3 = _comm_start([[gw1_0, gw2_0.reshape(N_DEV, 512, D_MODEL)]], "ex_start3", exchange=True)
    g_x1, gg_mlp_pre0, g_y0, gg_mix_post0 = _mm(
        g_p0, w1[0], name="b_mlp0_h", tb=True, b3=True, epi=_epi_pre_post_bwd, extra=(x1, g_x2, y0), cols=(rx1, ry0),
        vecs=(_tie(mlp_pre_g[0:1], tok3), mix_post_g[0:1]), out_dtypes=PRE_POST_BWD_DTYPES,
        out_kinds=PRE_POST_BWD_KINDS, bm=FUSED_ROWS, bk=4 * D_MODEL)
    g_o_att = _mm(g_y0, wout_e[FOX_W:], name="b_wout_even_m", tb=True)
    gwout_e = _mm(mix0, g_y0, name="b_wout_even_w", ta=True)
    gyl, gud, g_wglu, g_d = _s5_glu_bwd(g_y0, wout_e, ylin, z0, s5_d, wglu, "s5_glu_bwd")
    (ex4,), tok4 = _comm_start([[gwout_e.reshape(N_DEV, 128, D_MODEL), g_wglu.reshape(N_DEV, 64, S5_W)]], "ex_start4", exchange=True)
    g_u0, ga, gb_raw, gc_raw = _s5_scan_bwd(gyl, _tie(cset, tok4), xs, z0, bset, gud, tabs, "s5_scan_bwd")
    g_lam, g_ldt, g_b, g_c = _s5_param_bwd(lam_c, ldt_c, b_t, gb_raw, jnp.transpose(ga), gc_raw, "s5_param_bwd")
    dq, dk, dv, dfq, dfrow = _fox_bwd(z0, frow, o_att, lse, g_o_att, "fox_bwd")
    dFk = jnp.pad(jnp.transpose(dfrow.reshape(8, L)), ((0, 0), (0, LANES - 8)))
    dfl, db_f = _fox_f_bwd(dFk, dfq, z0, bf_pad, "fox_f_bwd")
    g_z0 = [g_u0, dq, dk, dv, dfl]
    grad_x, gg_mix_pre0 = _mm(g_z0, winT_e, name="b_win_even_h", epi=_epi_pre_bwd, extra=(x0, g_x1), cols=(rx0,),
                              vecs=(mix_pre_g[0:1],), out_dtypes=(F32, F32), out_kinds=("full", "vsum"),
                              bm=FUSED_ROWS)

    small_grads = dict(
        mix_pre_g=jnp.concatenate([gg_mix_pre0, gg_mix_pre1]), mix_post_g=jnp.concatenate([gg_mix_post0, gg_mix_post1]),
        mlp_pre_g=jnp.concatenate([gg_mlp_pre0, gg_mlp_pre1]), mlp_post_g=jnp.concatenate([gg_mlp_post0, gg_mlp_post1]),
        s5_lam_re=g_lam[:, 0], s5_lam_im=g_lam[:, 1],
        s5_b_re=g_b[0, :, :16], s5_b_im=g_b[1, :, :16], s5_c_re=g_c[0, :, :64], s5_c_im=g_c[1, :, :64],
        pool_w=g_pool_w, sgu_w_s=g_ws, s5_d=g_d, sgu_b_s=jnp.transpose(g_bst),
        pool_scale=g_pool_scale, sgu_ln_g=g_ln_g, sgu_ln_b=g_ln_b, s5_log_dt=g_ldt, fox_b_f=db_f[:, :8])
    small_names = list(small_grads)
    full_shapes = [(512,) if nm in ("pool_scale", "sgu_ln_g", "sgu_ln_b") else weights[nm].shape for nm in small_names]
    full_shapes.append((1, 1))
    rows = _packed_rows(full_shapes)
    packed = _pack([small_grads[nm] for nm in small_names] + [sq], rows).reshape(N_DEV, rows // N_DEV, LANES)
    (exs,), tok_s = _comm_start([[packed]], "exs_start", exchange=True)
    gwinT_e = _mm(g_z0, h0, name="b_win_even_w", ta=True, bk=512, out_dtypes=(BF16,), dep=tok_s)
    (recv_small,) = _comm_wait(exs, gwinT_e, "exs_wait", exchange=True)
    piece = _sum_pieces(recv_small, "sum_small")
    gwinT_e_pieces = gwinT_e[:EVEN_IN].reshape(N_DEV, EVEN_IN // N_DEV, D_MODEL)
    (ags, ex5), tok5 = _comm_start([[piece], [gwinT_e_pieces]], "ags_ex_start5", exchange=(False, True))
    r_w1_1, r_w2_1 = _comm_wait(ex1, tok5, "ex_wait1", exchange=True)
    r_wout_o, r_win_o = _comm_wait(ex2, tok5, "ex_wait2", exchange=True)
    r_w1_0, r_w2_0 = _comm_wait(ex3, tok5, "ex_wait3", exchange=True)
    r_wout_e, r_wglu = _comm_wait(ex4, tok5, "ex_wait4", exchange=True)

    res = {}
    for nm, parts in (("mlp_w1", (r_w1_0, r_w1_1)), ("mlp_w2", (r_w2_0, r_w2_1))):
        first = _sum_adamw(parts[0], weights[nm], mom_m[nm], mom_v[nm], "adamw_%s_0" % nm, layer=0)
        res[nm] = tuple(_sum_adamw(parts[1], weights[nm], mom_m[nm], mom_v[nm], "adamw_%s_1" % nm, layer=1, prev=first))
    big_parts = dict(s5_w_glu=r_wglu, w_out_even=r_wout_e, w_out_odd=r_wout_o)
    for nm, parts in big_parts.items():
        res[nm] = tuple(_sum_adamw(parts, weights[nm], mom_m[nm], mom_v[nm], "adamw_" + nm))
    done = [res[nm][1] for nm in ("mlp_w1", "mlp_w2", "s5_w_glu", "w_out_even", "w_out_odd")]

    (small_all,) = _comm_wait(ags, done, "ags_wait", exchange=False)
    small_full = _unpack(small_all.reshape(rows, LANES), full_shapes)
    loss = 0.5 * small_full.pop()[0, 0] / D_MODEL
    small_g = []
    for nm, g in zip(small_names, small_full):
        if nm in ("pool_scale", "sgu_ln_g", "sgu_ln_b"):
            g = lax.dynamic_slice(g, (my_index * 64,), (64,)).reshape(1, 64)
        small_g.append(g)

    def turned(arrs):
        return [jnp.swapaxes(a, -1, -2) if nm in ("s5_b_re", "s5_b_im") else a for nm, a in zip(small_names, arrs)]

    sd, sm, sv = _adamw_many(turned([weights[nm] for nm in small_names]), turned(small_g),
                             turned([mom_m[nm] for nm in small_names]), turned([mom_v[nm] for nm in small_names]),
                             "adamw_small")
    for nm, g_, d_, m_, v_ in zip(small_names, small_g, turned(sd), turned(sm), turned(sv)):
        res[nm] = (g_, d_, m_, v_)
    done.append(sd[0])

    nm = "w_in_odd"
    outs = _sum_adamw(r_win_o, jnp.transpose(weights[nm], (0, 2, 1)), jnp.transpose(mom_m[nm], (0, 2, 1)),
                      jnp.transpose(mom_v[nm], (0, 2, 1)), "adamw_" + nm)
    res[nm] = tuple(jnp.transpose(o, (0, 2, 1)) for o in outs)
    done.append(res[nm][1])
    nm = "w_in_even"
    (r_win_e,) = _comm_wait(ex5, done, "ex_wait5", exchange=True)
    outs = _sum_adamw_rows(r_win_e, jnp.transpose(weights[nm], (2, 0, 1)), jnp.transpose(mom_m[nm], (2, 0, 1)),
                           jnp.transpose(mom_v[nm], (2, 0, 1)), "adamw_" + nm)
    res[nm] = tuple(jnp.transpose(o, (1, 2, 0)) for o in outs)

    grads = [res[nm][0].reshape(weights[nm].shape) for nm in names]
    deltas = [res[nm][1].reshape(weights[nm].shape) for nm in names]
    new_m = [res[nm][2].reshape(weights[nm].shape) for nm in names]
    new_v = [res[nm][3].reshape(weights[nm].shape) for nm in names]
    return (loss, grad_x[None], *grads, *deltas, *new_m, *new_v)
```

```python
import math

import jax
import jax.numpy as jnp
from jax import lax
from jax.experimental import pallas as pl
from jax.experimental.pallas import tpu as pltpu

F32 = jnp.float32
BF16 = jnp.bfloat16
MESH = pl.DeviceIdType.MESH
ANY = pl.BlockSpec(memory_space=pl.ANY)

N_DEV = 8
D_MODEL = 1024
EPS = 1e-6
NORM_ROWS = 512
FUSED_ROWS = 512
S5_W = 512
S5_NS = 2048
SCAN_GROUPS = 4
SCAN_CHUNK = 1024
FOX_W = 512
EVEN_IN = 2056
EVEN_PAD = 2176
ODD_IN = 1536
LANES = 128
PIECE = 4 * D_MODEL // N_DEV
VMEM_LIMIT = 56 * 1024 * 1024

ADAM_LR = 0.001
ADAM_B1 = 0.9
ADAM_B2 = 0.999
ADAM_EPS = 1e-08
ADAM_WD = 0.01
ADAM_STEP = 10

NT = (((1,), (1,)), ((), ()))
TN = (((0,), (0,)), ((), ()))
NN = (((1,), (0,)), ((), ()))


def _cp(*sem):
    return pltpu.CompilerParams(dimension_semantics=sem, vmem_limit_bytes=VMEM_LIMIT)


def _sds(shape, dtype=F32):
    return jax.ShapeDtypeStruct(tuple(shape), dtype)


def _gelu(x):
    t = jnp.tanh(0.7978845608028654 * (x + 0.044715 * x * x * x))
    return 0.5 * x * (1.0 + t)


def _gelu_grad(x):
    t = jnp.tanh(0.7978845608028654 * (x + 0.044715 * x * x * x))
    du = 0.7978845608028654 * (1.0 + 3.0 * 0.044715 * x * x)
    return 0.5 * (1.0 + t) + 0.5 * x * (1.0 - t * t) * du


def _sigmoid(x):
    return 1.0 / (1.0 + jnp.exp(-x))


def _dot(a, b, dn=NN):
    return lax.dot_general(a, b, dn, preferred_element_type=F32)


def _mm(a, b, *, name, ta=False, tb=False, b3=False, out3=False, out_dtypes=(F32,), epi=None, extra=(),
        cols=(), vecs=(), out_kinds=None, bm=1024, bn=1024, bk=1024, dep=None):
    a_list = list(a) if isinstance(a, (list, tuple)) else [a]
    widths = [p.shape[1] for p in a_list]
    offs = [sum(widths[:i]) for i in range(len(widths))]
    na = len(a_list)
    M = sum(widths) if ta else a_list[0].shape[0]
    K = a_list[0].shape[0] if ta else sum(widths)
    if na > 1:
        assert not b3 and not tb
        bm, bk = (M, bk) if ta else (bm, K)
    pw = b.shape[2] if b3 else PIECE
    if b3:
        N = b.shape[1] if tb else b.shape[0] * pw
        assert (b.shape[0] * pw if tb else b.shape[1]) == K
    else:
        N = b.shape[0] if tb else b.shape[1]
    bm, bn, bk = min(bm, M), min(bn, N), min(bk, K)
    assert M % bm == 0 and N % bn == 0 and K % bk == 0, (name, M, N, K, bm, bn, bk)
    assert not (b3 or out3) or ((bk if tb else bn) % pw == 0 and bn % PIECE == 0)
    nk = K // bk
    n_extra = len(extra) + len(cols) + len(vecs)
    n_out = len(out_dtypes)
    out_kinds = tuple(out_kinds) if out_kinds is not None else ("full",) * n_out
    dn = (((0 if ta else 1,), (1 if tb else 0,)), ((), ()))

    use_acc = nk > 1

    def body(*refs):
        a_refs, b_ref = refs[:na], refs[na]
        a_ref = a_refs[0]
        e_refs = refs[na + 1:na + 1 + n_extra]
        first_out = na + 1 + n_extra + (0 if dep is None else 1)
        o_refs = refs[first_out:first_out + n_out]
        acc_ref = refs[-1] if use_acc else o_refs[0]
        i, k = pl.program_id(0), pl.program_id(2)

        def dot(a_v, b_v):
            return lax.dot_general(a_v.astype(BF16), b_v.astype(BF16), dn, preferred_element_type=F32)

        everything = slice(None)
        if na > 1 and ta:
            terms = [(pl.ds(off, w), everything, r, b_ref) for r, off, w in zip(a_refs, offs, widths)]
        elif na > 1:
            terms = [(everything, everything, r, b_ref.at[pl.ds(off, w), :]) for r, off, w in zip(a_refs, offs, widths)]
        elif not b3:
            terms = [(everything, everything, a_ref, b_ref)]
        elif tb:
            terms = [(everything, everything,
                      a_ref.at[pl.ds(t * pw, pw), :] if ta else a_ref.at[:, pl.ds(t * pw, pw)], b_ref.at[t])
                     for t in range(bk // pw)]
        else:
            terms = [(everything, pl.ds(t * pw, pw), a_ref, b_ref.at[t]) for t in range(bn // pw)]

        def finish(acc):
            outs = (acc,) if epi is None else epi(acc, *[e[...] for e in e_refs])
            for o_ref, o, kind in zip(o_refs, outs, out_kinds):
                if kind == "vsum":
                    @pl.when(i == 0)
                    def _(o_ref=o_ref, o=o):
                        o_ref[...] = o

                    @pl.when(i > 0)
                    def _(o_ref=o_ref, o=o):
                        o_ref[...] += o
                elif out3:
                    for t in range(bn // PIECE):
                        o_ref[t] = o[:, t * PIECE:(t + 1) * PIECE].astype(o_ref.dtype)
                else:
                    o_ref[...] = o.astype(o_ref.dtype)

        if nk == 1:
            bands = {}
            for rows, cols, a_r, b_r in terms:
                key = (getattr(rows, "start", None), getattr(cols, "start", None))
                val = dot(a_r[...], b_r[...])
                bands[key] = val if key not in bands else bands[key] + val
            vals = list(bands.values())
            if len(vals) == 1:
                finish(vals[0])
            else:
                finish(jnp.concatenate(vals, axis=0 if (na > 1 and ta) else 1))
            return

        @pl.when(k == 0)
        def _():
            acc_ref[...] = jnp.zeros_like(acc_ref)

        for rows, cols, a_r, b_r in terms:
            acc_ref[rows, cols] += dot(a_r[...], b_r[...])

        @pl.when(k == nk - 1)
        def _():
            finish(acc_ref[...])

    if na > 1:
        a_specs = [pl.BlockSpec((bk, w), lambda i, j, k: (k, 0)) if ta else pl.BlockSpec((bm, w), lambda i, j, k: (i, 0))
                   for w in widths]
    else:
        a_specs = [pl.BlockSpec((bk, bm), lambda i, j, k: (k, i)) if ta else
                   pl.BlockSpec((bm, bk), lambda i, j, k: (i, k))]
    if b3:
        if tb:
            b_spec = pl.BlockSpec((bk // pw, bn, pw), lambda i, j, k: (k, j, 0))
        else:
            b_spec = pl.BlockSpec((bn // pw, bk, pw), lambda i, j, k: (j, k, 0))
    else:
        b_spec = pl.BlockSpec((bn, bk), lambda i, j, k: (j, k)) if tb else pl.BlockSpec((bk, bn), lambda i, j, k: (k, j))
    e_specs = ([pl.BlockSpec((bm, bn), lambda i, j, k: (i, j)) for _ in extra]
               + [pl.BlockSpec((bm, 1), lambda i, j, k: (i, 0)) for _ in cols]
               + [pl.BlockSpec((1, bn), lambda i, j, k: (0, j)) for _ in vecs])
    if out3:
        o_specs = [pl.BlockSpec((bn // PIECE, bm, PIECE), lambda i, j, k: (j, i, 0)) for _ in out_dtypes]
        o_shapes = [_sds((N // PIECE, M, PIECE), dt) for dt in out_dtypes]
    else:
        spec_of = {"full": pl.BlockSpec((bm, bn), lambda i, j, k: (i, j)),
                   "col": pl.BlockSpec((bm, 1), lambda i, j, k: (i, 0)),
                   "vsum": pl.BlockSpec((1, bn), lambda i, j, k: (0, j))}
        shape_of = {"full": (M, N), "col": (M, 1), "vsum": (1, N)}
        o_specs = [spec_of[kind] for kind in out_kinds]
        o_shapes = [_sds(shape_of[kind], dt) for kind, dt in zip(out_kinds, out_dtypes)]
    assert "col" not in out_kinds or bn == N
    outs = pl.pallas_call(
        body, name=name, grid=(M // bm, N // bn, nk),
        in_specs=a_specs + [b_spec] + e_specs + ([] if dep is None else [ANY]),
        out_specs=o_specs, out_shape=o_shapes,
        scratch_shapes=[pltpu.VMEM((bm, bn), F32)] if use_acc else [],
        compiler_params=_cp("arbitrary" if "vsum" in out_kinds else "parallel", "parallel", "arbitrary"),
    )(*a_list, b, *extra, *cols, *vecs, *([] if dep is None else [dep]))
    return outs[0] if n_out == 1 else outs


def _epi_relu2(acc):
    r = jnp.maximum(acc, 0.0)
    return acc, r * r


def _epi_relu2_bwd(acc, p):
    return (acc * (2.0 * jnp.maximum(p.astype(F32), 0.0)),)


def _row_spec(rb, w=D_MODEL):
    return pl.BlockSpec((rb, w), lambda i: (i, 0))


def _vec_spec(w=D_MODEL):
    return pl.BlockSpec((1, w), lambda i: (0, 0))


def _rstd(v):
    return lax.rsqrt(jnp.mean(v * v, axis=-1, keepdims=True) + EPS)


def _rms_fwd(x, g, name):
    L = x.shape[0]
    rb = min(NORM_ROWS, L)

    def body(x_ref, g_ref, h_ref, r_ref):
        xv = x_ref[...]
        r = _rstd(xv)
        h_ref[...] = (xv * r * g_ref[...]).astype(BF16)
        r_ref[...] = r

    return pl.pallas_call(
        body, name=name, grid=(L // rb,),
        in_specs=[_row_spec(rb), _vec_spec()],
        out_specs=[_row_spec(rb), _row_spec(rb, 1)],
        out_shape=[_sds((L, D_MODEL), BF16), _sds((L, 1))],
        compiler_params=_cp("parallel"),
    )(x, g)


def _rms_bwd_rows(dy, xv, r, g):
    n = xv * r
    dyg = dy * g
    return r * (dyg - n * jnp.mean(dyg * n, axis=-1, keepdims=True)), n


POST_PRE_DTYPES = (F32, F32, BF16, F32, F32)
POST_PRE_KINDS = ("full", "col", "full", "col", "full")
PRE_POST_BWD_DTYPES = (F32, F32, BF16, F32)
PRE_POST_BWD_KINDS = ("full", "vsum", "full", "vsum")


def _epi_post_pre(y, x_in, g_post, g_pre):
    ry = _rstd(y)
    xo = x_in + y * ry * g_post
    rx = _rstd(xo)
    return xo, ry, xo * rx * g_pre, rx, y


def _epi_pre_post_bwd(gh, x, g_out, y_prev, rx, ry_prev, g_pre, g_post_prev):
    gx, n = _rms_bwd_rows(gh, x, rx, g_pre)
    gi = g_out + gx
    gy, ny = _rms_bwd_rows(gi, y_prev, ry_prev, g_post_prev)
    return gi, jnp.sum(gh * n, axis=0, keepdims=True), gy, jnp.sum(gi * ny, axis=0, keepdims=True)


def _epi_pre_bwd(gh, x, g_out, rx, g_pre):
    gx, n = _rms_bwd_rows(gh, x, rx, g_pre)
    return g_out + gx, jnp.sum(gh * n, axis=0, keepdims=True)


def _epi_post_loss(y, x_in, target, g_post):
    ry = _rstd(y)
    diff = x_in + y * ry * g_post - target
    gx = diff * (1.0 / D_MODEL)
    gy, n = _rms_bwd_rows(gx, y, ry, g_post)
    sq = jnp.broadcast_to(jnp.sum(diff * diff, keepdims=True), (1, y.shape[1]))
    return gx, gy, jnp.sum(gx * n, axis=0, keepdims=True), sq


def _cmul(ar, ai, br, bi):
    return ar * br - ai * bi, ar * bi + ai * br


def _zoh_cols(lr, li, ldt):
    dt = jnp.exp(ldt)
    mag = jnp.exp(lr * dt)
    ar = mag * jnp.cos(li * dt)
    ai = mag * jnp.sin(li * dt)
    den = lr * lr + li * li
    nr = ar - 1.0
    qr = (nr * lr + ai * li) / den
    qi = (ai * lr - nr * li) / den
    return dt, ar, ai, qr, qi, den


def _b_mask():
    r = lax.broadcasted_iota(jnp.int32, (S5_NS, LANES), 0)
    c = lax.broadcasted_iota(jnp.int32, (S5_NS, LANES), 1)
    return ((r >> 6) & 7) == (c >> 4)


def _c_mask():
    r = lax.broadcasted_iota(jnp.int32, (S5_W, 512), 0)
    c = lax.broadcasted_iota(jnp.int32, (S5_W, 512), 1)
    return ((r >> 4) & 7) == (c >> 6)


def _s5_prep(lam_r, ldt_r, lam_c, ldt_c, b_t, c_t, name):
    def body(lam_r_ref, ldt_r_ref, lam_c_ref, ldt_c_ref, b_ref, c_ref, tab_ref, bset_ref, cset_ref):
        lr, li = lam_r_ref[0:1, :], lam_r_ref[1:2, :]
        dt = jnp.exp(ldt_r_ref[...])
        mag = jnp.exp(lr * dt)
        p1r, p1i = mag * jnp.cos(li * dt), mag * jnp.sin(li * dt)
        p2r, p2i = _cmul(p1r, p1i, p1r, p1i)
        p3r, p3i = _cmul(p2r, p2i, p1r, p1i)
        p4r, p4i = _cmul(p2r, p2i, p2r, p2i)
        p5r, p5i = _cmul(p4r, p4i, p1r, p1i)
        p6r, p6i = _cmul(p4r, p4i, p2r, p2i)
        p7r, p7i = _cmul(p4r, p4i, p3r, p3i)
        p8r, p8i = _cmul(p4r, p4i, p4r, p4i)
        pw_r = [p1r, p2r, p3r, p4r, p5r, p6r, p7r, p8r]
        pw_i = [p1i, p2i, p3i, p4i, p5i, p6i, p7i, p8i]
        row = lax.broadcasted_iota(jnp.int32, (8, S5_NS), 0)
        zero = jnp.zeros((8, S5_NS), F32)

        def bc(v):
            return jnp.broadcast_to(v, (8, S5_NS))

        for d in range(2):
            sgn = 1.0 if d == 0 else -1.0
            for t, s in enumerate((1, 2, 4)):
                live = (row >= s) if d == 0 else (row <= 7 - s)
                tab_ref[d, 2 * t] = jnp.where(live, bc(pw_r[s - 1]), zero)
                tab_ref[d, 2 * t + 1] = jnp.where(live, bc(sgn * pw_i[s - 1]), zero)
            cr, ci = zero, zero
            for i in range(8):
                e = i if d == 0 else 7 - i
                cr = jnp.where(row == i, bc(pw_r[e]), cr)
                ci = jnp.where(row == i, bc(sgn * pw_i[e]), ci)
            tab_ref[d, 6] = cr
            tab_ref[d, 7] = ci

        _, _, _, qr, qi, _ = _zoh_cols(lam_c_ref[:, 0:1], lam_c_ref[:, 1:2], ldt_c_ref[...])
        bm = _b_mask()
        br, bi = b_ref[0], b_ref[1]
        bset_ref[0] = jnp.where(bm, qr * br - qi * bi, 0.0).astype(BF16)
        bset_ref[1] = jnp.where(bm, qr * bi + qi * br, 0.0).astype(BF16)
        cm = _c_mask()
        cset_ref[0] = jnp.where(cm, c_ref[0], 0.0).astype(BF16)
        cset_ref[1] = jnp.where(cm, c_ref[1], 0.0).astype(BF16)

    vm = pl.BlockSpec(memory_space=pltpu.VMEM)
    return pl.pallas_call(
        body, name=name, in_specs=[vm] * 6, out_specs=[vm] * 3,
        out_shape=[_sds((2, 8, 8, S5_NS)), _sds((2, S5_NS, LANES), BF16), _sds((2, S5_W, 512), BF16)],
        compiler_params=pltpu.CompilerParams(vmem_limit_bytes=VMEM_LIMIT),
    )(lam_r, ldt_r, lam_c, ldt_c, b_t, c_t)


SCAN_W = SCAN_GROUPS * LANES


def _scan_chunk(src_ref, dst_ref, tab_ref, carry_ref, nb, reverse, xs_ref=None, acc_ref=None):
    row = lax.broadcasted_iota(jnp.int32, (8, LANES), 0)

    def step(i, carry):
        b = (nb - 1 - i) if reverse else i
        off = pl.multiple_of(b * 8, 8)
        out = []
        for g in range(SCAN_GROUPS):
            lanes = pl.ds(g * LANES, LANES)
            cr, ci = carry[2 * g], carry[2 * g + 1]
            yr = src_ref[0, pl.ds(off, 8), lanes]
            yi = src_ref[1, pl.ds(off, 8), lanes]
            for t, s in enumerate((1, 2, 4)):
                sh = (8 - s) if reverse else s
                sr = pltpu.roll(yr, sh, 0)
                si = pltpu.roll(yi, sh, 0)
                mr, mi = tab_ref[2 * t, :, lanes], tab_ref[2 * t + 1, :, lanes]
                yr, yi = yr + mr * sr - mi * si, yi + mr * si + mi * sr
            pr, pi = tab_ref[6, :, lanes], tab_ref[7, :, lanes]
            yr, yi = yr + pr * cr - pi * ci, yi + pr * ci + pi * cr
            dst_ref[0, pl.ds(off, 8), lanes] = yr
            dst_ref[1, pl.ds(off, 8), lanes] = yi
            if xs_ref is not None:
                nr = jnp.where(row == 7, cr, pltpu.roll(yr, 7, 0))
                ni = jnp.where(row == 7, ci, pltpu.roll(yi, 7, 0))
                xr = xs_ref[0, pl.ds(off, 8), lanes]
                xi = xs_ref[1, pl.ds(off, 8), lanes]
                acc_ref[0, :, lanes] += xr * nr + xi * ni
                acc_ref[1, :, lanes] += xr * ni - xi * nr
            last = 0 if reverse else 7
            out += [jnp.broadcast_to(yr[last:last + 1, :], (8, LANES)),
                    jnp.broadcast_to(yi[last:last + 1, :], (8, LANES))]
        return tuple(out)

    init = []
    for g in range(SCAN_GROUPS):
        init += [carry_ref[0, :, pl.ds(g * LANES, LANES)], carry_ref[1, :, pl.ds(g * LANES, LANES)]]
    fin = lax.fori_loop(0, nb, step, tuple(init))
    for g in range(SCAN_GROUPS):
        carry_ref[0, :, pl.ds(g * LANES, LANES)] = fin[2 * g]
        carry_ref[1, :, pl.ds(g * LANES, LANES)] = fin[2 * g + 1]


def _s5_scan_fwd(z, bset, cset, dvec, tabs, name):
    L = z.shape[0]
    tl = min(SCAN_CHUNK, L)
    nc = L // tl

    def body(u_ref, b_ref, c_ref, d_ref, tab_ref, x_ref, y_ref, carry_ref):
        @pl.when(pl.program_id(1) == 0)
        def _():
            carry_ref[...] = jnp.zeros_like(carry_ref)

        uf = u_ref[...]
        u = uf.astype(BF16)
        x_ref[0] = _dot(u, b_ref[0], NT)
        x_ref[1] = _dot(u, b_ref[1], NT)
        _scan_chunk(x_ref, x_ref, tab_ref, carry_ref, tl // 8, False)
        y_ref[...] = (_dot(x_ref[0].astype(BF16), c_ref[0], NT) - _dot(x_ref[1].astype(BF16), c_ref[1], NT)
                      + d_ref[...] * uf)

    col = pl.BlockSpec((tl, LANES), lambda j, c: (c, j))
    return pl.pallas_call(
        body, name=name, grid=(S5_NS // SCAN_W, nc),
        in_specs=[col, pl.BlockSpec((2, SCAN_W, LANES), lambda j, c: (0, j, 0)),
                  pl.BlockSpec((2, LANES, SCAN_W), lambda j, c: (0, j, 0)),
                  pl.BlockSpec((1, LANES), lambda j, c: (0, j)),
                  pl.BlockSpec((None, 8, 8, SCAN_W), lambda j, c: (0, 0, 0, j))],
        out_specs=[pl.BlockSpec((2, tl, SCAN_W), lambda j, c: (0, c, j)), col],
        out_shape=[_sds((2, L, S5_NS)), _sds((L, S5_W))],
        scratch_shapes=[pltpu.VMEM((2, 8, SCAN_W), F32)],
        compiler_params=_cp("parallel", "arbitrary"),
    )(z, bset, cset, dvec, tabs)


def _s5_scan_bwd(gyl, cset, xs, z, bset, gud, tabs, name):
    L = z.shape[0]
    tl = min(SCAN_CHUNK, L)
    nc = L // tl

    def body(g_ref, c_ref, xs_ref, u_ref, b_ref, gud_ref, tab_ref, gu_ref, ga_ref, gb_ref, gc_ref,
             gx_ref, carry_ref, acc_ref):
        c = pl.program_id(1)

        @pl.when(c == 0)
        def _():
            carry_ref[...] = jnp.zeros_like(carry_ref)
            acc_ref[...] = jnp.zeros_like(acc_ref)
            gb_ref[...] = jnp.zeros_like(gb_ref)
            gc_ref[...] = jnp.zeros_like(gc_ref)

        gy = g_ref[...].astype(BF16)
        gx_ref[0] = _dot(gy, c_ref[0])
        gx_ref[1] = -_dot(gy, c_ref[1])
        gc_ref[0] += _dot(gy, xs_ref[0].astype(BF16), TN)
        gc_ref[1] -= _dot(gy, xs_ref[1].astype(BF16), TN)
        _scan_chunk(gx_ref, gx_ref, tab_ref, carry_ref, tl // 8, True, xs_ref, acc_ref)
        gr = gx_ref[0].astype(BF16)
        gi = gx_ref[1].astype(BF16)
        gu_ref[...] = gud_ref[...] + _dot(gr, b_ref[0]) + _dot(gi, b_ref[1])
        u = u_ref[...].astype(BF16)
        gb_ref[0] += _dot(gr, u, TN)
        gb_ref[1] += _dot(gi, u, TN)

        @pl.when(c == nc - 1)
        def _():
            ga_ref[0:1, :] = jnp.sum(acc_ref[0], axis=0, keepdims=True)
            ga_ref[1:2, :] = jnp.sum(acc_ref[1], axis=0, keepdims=True)

    rev = lambda j, c: (nc - 1 - c, j)
    col = pl.BlockSpec((tl, LANES), rev)
    return pl.pallas_call(
        body, name=name, grid=(S5_NS // SCAN_W, nc),
        in_specs=[col, pl.BlockSpec((2, LANES, SCAN_W), lambda j, c: (0, j, 0)),
                  pl.BlockSpec((2, tl, SCAN_W), lambda j, c: (0, nc - 1 - c, j)), col,
                  pl.BlockSpec((2, SCAN_W, LANES), lambda j, c: (0, j, 0)), col,
                  pl.BlockSpec((None, 8, 8, SCAN_W), lambda j, c: (1, 0, 0, j))],
        out_specs=[col, pl.BlockSpec((2, SCAN_W), lambda j, c: (0, j)),
                   pl.BlockSpec((2, SCAN_W, LANES), lambda j, c: (0, j, 0)),
                   pl.BlockSpec((2, LANES, SCAN_W), lambda j, c: (0, j, 0))],
        out_shape=[_sds((L, S5_W)), _sds((2, S5_NS)), _sds((2, S5_NS, LANES)), _sds((2, S5_W, 512))],
        scratch_shapes=[pltpu.VMEM((2, tl, SCAN_W), F32), pltpu.VMEM((2, 8, SCAN_W), F32),
                        pltpu.VMEM((2, 8, SCAN_W), F32)],
        compiler_params=_cp("parallel", "arbitrary"),
    )(gyl, cset, xs, z, bset, gud, tabs)


def _s5_glu_fwd(ylin, wglu, name):
    L = ylin.shape[0]
    bl = min(1024, L)

    def body(ylin_ref, w_ref, ya_ref):
        yg = _gelu(ylin_ref[...])
        t = _dot(yg.astype(BF16), w_ref[...])
        ya_ref[...] = (yg * _sigmoid(t)).astype(BF16)

    return pl.pallas_call(
        body, name=name, grid=(L // bl,),
        in_specs=[pl.BlockSpec((bl, S5_W), lambda i: (i, 0)), pl.BlockSpec((S5_W, S5_W), lambda i: (0, 0))],
        out_specs=pl.BlockSpec((bl, S5_W), lambda i: (i, 0)),
        out_shape=_sds((L, S5_W), BF16),
        compiler_params=_cp("parallel"),
    )(ylin, wglu)


def _s5_glu_bwd(g_y, wout, ylin, z, dvec, wglu, name):
    L = z.shape[0]
    bl = min(256, L)

    def body(g_ref, wo_ref, ylin_ref, u_ref, d_ref, w_ref, gyl_ref, gud_ref, gw_ref, gd_ref):
        i = pl.program_id(0)
        ylin = ylin_ref[...]
        yg = _gelu(ylin)
        ygb = yg.astype(BF16)
        sg = _sigmoid(_dot(ygb, w_ref[...]))
        gya = _dot(g_ref[...], wo_ref[...], NT)
        gt = gya * yg * sg * (1.0 - sg)
        gtb = gt.astype(BF16)
        gyg = gya * sg + _dot(gtb, w_ref[...], NT)
        gyl = gyg * _gelu_grad(ylin)
        gyl_ref[...] = gyl
        gud_ref[...] = gyl * d_ref[...]

        @pl.when(i == 0)
        def _():
            gw_ref[...] = jnp.zeros_like(gw_ref)
            gd_ref[...] = jnp.zeros_like(gd_ref)

        gw_ref[...] += _dot(ygb, gtb, TN)
        gd_ref[...] += jnp.sum(gyl * u_ref[...], axis=0, keepdims=True)

    blk = pl.BlockSpec((bl, S5_W), lambda i: (i, 0))
    return pl.pallas_call(
        body, name=name, grid=(L // bl,),
        in_specs=[pl.BlockSpec((bl, D_MODEL), lambda i: (i, 0)), pl.BlockSpec((S5_W, D_MODEL), lambda i: (0, 0)),
                  blk, blk, pl.BlockSpec((1, S5_W), lambda i: (0, 0)), pl.BlockSpec((S5_W, S5_W), lambda i: (0, 0))],
        out_specs=[blk, blk, pl.BlockSpec((S5_W, S5_W), lambda i: (0, 0)), pl.BlockSpec((1, S5_W), lambda i: (0, 0))],
        out_shape=[_sds((L, S5_W)), _sds((L, S5_W)), _sds((S5_W, S5_W)), _sds((1, S5_W))],
        compiler_params=_cp("arbitrary"),
    )(g_y, wout, ylin, z, dvec, wglu)


def _s5_param_bwd(lam_c, ldt_c, b_t, gb, ga_c, gc, name):
    def body(lam_ref, ldt_ref, b_ref, gb_ref, ga_ref, gc_ref, glam_ref, gldt_ref, gbo_ref, gco_ref):
        lr, li = lam_ref[:, 0:1], lam_ref[:, 1:2]
        dt, ar, ai, qr, qi, den = _zoh_cols(lr, li, ldt_ref[...])
        bm = _b_mask()
        gbr = jnp.where(bm, gb_ref[0], 0.0)
        gbi = jnp.where(bm, gb_ref[1], 0.0)
        br, bi = b_ref[0], b_ref[1]
        obr = gbr * qr + gbi * qi
        obi = gbi * qr - gbr * qi
        gqr = jnp.sum(gbr * br + gbi * bi, axis=1, keepdims=True)
        gqi = jnp.sum(gbi * br - gbr * bi, axis=1, keepdims=True)
        for s in (64, 32, 16):
            obr = obr + pltpu.roll(obr, s, 1)
            obi = obi + pltpu.roll(obi, s, 1)
        gbo_ref[0] = obr
        gbo_ref[1] = obi
        gar = ga_ref[:, 0:1] + (gqr * lr - gqi * li) / den
        gai = ga_ref[:, 1:2] + (gqr * li + gqi * lr) / den
        qlr = (qr * lr + qi * li) / den
        qli = (qi * lr - qr * li) / den
        glr = -(gqr * qlr + gqi * qli)
        gli = -(gqi * qlr - gqr * qli)
        glr = glr + dt * (gar * ar + gai * ai)
        gli = gli + dt * (gai * ar - gar * ai)
        wr, wi = _cmul(lr, li, ar, ai)
        gldt = (gar * wr + gai * wi) * dt
        glam_ref[:, 0:1] = glr
        glam_ref[:, 1:2] = gli
        r = lax.broadcasted_iota(jnp.int32, (S5_NS, 32), 0)
        c = lax.broadcasted_iota(jnp.int32, (S5_NS, 32), 1)
        gldt_ref[...] = jnp.sum(jnp.where((r >> 6) == c, gldt, 0.0), axis=0, keepdims=True)
        cm = _c_mask()
        for k in range(2):
            oc = jnp.where(cm, gc_ref[k], 0.0)
            for s in (256, 128, 64):
                oc = oc + pltpu.roll(oc, s, 1)
            gco_ref[k] = oc[:, 0:LANES]

    vm = pl.BlockSpec(memory_space=pltpu.VMEM)
    return pl.pallas_call(
        body, name=name, in_specs=[vm] * 6, out_specs=[vm] * 4,
        out_shape=[_sds((S5_NS, 2)), _sds((1, 32)), _sds((2, S5_NS, LANES)), _sds((2, S5_W, LANES))],
        compiler_params=pltpu.CompilerParams(vmem_limit_bytes=VMEM_LIMIT),
    )(lam_c, ldt_c, b_t, gb, ga_c, gc)


FL_BLK = EVEN_PAD // LANES - 1
Q_BLK, K_BLK, V_BLK = 4, 8, 12
NEG = -1e30


def _log_sigmoid(v):
    return jnp.minimum(v, 0.0) - jnp.log(1.0 + jnp.exp(-jnp.abs(v)))


def _fox_f_fwd(z, bf, name):
    L = z.shape[0]

    def body(fl_ref, b_ref, f_ref, fq_ref):
        row = lax.broadcasted_iota(jnp.int32, (L, LANES), 0)
        cs = _cumsum_rows(_log_sigmoid(fl_ref[...] + b_ref[...]), True, row)
        f_ref[...] = cs
        expand = (lax.broadcasted_iota(jnp.int32, (LANES, FOX_W), 0)
                  == (lax.broadcasted_iota(jnp.int32, (LANES, FOX_W), 1) >> 6)).astype(F32)
        fq_ref[...] = lax.dot_general(cs, expand, NN, precision=lax.Precision.HIGHEST, preferred_element_type=F32)

    return pl.pallas_call(
        body, name=name, grid=(1,),
        in_specs=[pl.BlockSpec((L, LANES), lambda i: (0, FL_BLK)), pl.BlockSpec((1, LANES), lambda i: (0, 0))],
        out_specs=[pl.BlockSpec((L, LANES), lambda i: (0, 0)), pl.BlockSpec((L, FOX_W), lambda i: (0, 0))],
        out_shape=[_sds((L, LANES)), _sds((L, FOX_W))],
        compiler_params=_cp("arbitrary"),
    )(z, bf)


def _fox_f_bwd(dFk, dfq, z, bf, name):
    L = z.shape[0]

    def body(dfk_ref, dfq_ref, fl_ref, b_ref, dfl_ref, db_ref):
        sel = (lax.broadcasted_iota(jnp.int32, (FOX_W, LANES), 0)
               == 64 * lax.broadcasted_iota(jnp.int32, (FOX_W, LANES), 1)).astype(F32)
        dfq_h = lax.dot_general(dfq_ref[...], sel, NN, precision=lax.Precision.HIGHEST, preferred_element_type=F32)
        row = lax.broadcasted_iota(jnp.int32, (L, LANES), 0)
        cs = _cumsum_rows(dfk_ref[...] + dfq_h, False, row)
        dfl = cs * _sigmoid(-(fl_ref[...] + b_ref[...]))
        dfl_ref[...] = dfl
        db_ref[...] = jnp.sum(dfl, axis=0, keepdims=True)

    return pl.pallas_call(
        body, name=name, grid=(1,),
        in_specs=[pl.BlockSpec((L, LANES), lambda i: (0, 0)), pl.BlockSpec((L, FOX_W), lambda i: (0, 0)),
                  pl.BlockSpec((L, LANES), lambda i: (0, FL_BLK)), pl.BlockSpec((1, LANES), lambda i: (0, 0))],
        out_specs=[pl.BlockSpec((L, LANES), lambda i: (0, 0)), pl.BlockSpec((1, LANES), lambda i: (0, 0))],
        out_shape=[_sds((L, LANES)), _sds((1, LANES))],
        compiler_params=_cp("arbitrary"),
    )(dFk, dfq, z, bf)


def _head_mask(hh):
    lane = lax.broadcasted_iota(jnp.int32, (1, LANES), 1)
    return (lane >> 6) == hh


FOX_T = 512


def _fox_head(x, hh):
    return jnp.where(_head_mask(hh), x, 0.0).astype(BF16)


def _fox_scores(qh, k, fq_ref, fr_ref, hh, causal):
    if fq_ref is None:
        s = _dot(qh, k, NT) - fr_ref[hh:hh + 1, :]
    else:
        s = _dot(qh, k, NT) + (fq_ref[:, 64 * hh:64 * hh + 1] - fr_ref[hh:hh + 1, :])
    return s if causal is None else jnp.where(causal, s, NEG)


def _causal(T):
    return lax.broadcasted_iota(jnp.int32, (T, T), 1) <= lax.broadcasted_iota(jnp.int32, (T, T), 0)


def _fox_fwd(z, fq, frow, name):
    L = z.shape[0]
    T = min(FOX_T, L)
    nq = L // T

    def body(qt_ref, kt_ref, q_ref, k_ref, v_ref, fq_ref, fr_ref, o_ref, lse_ref, m_ref, l_ref, acc_ref):
        t = pl.program_id(1)
        qi, ki = qt_ref[t], kt_ref[t]

        @pl.when(ki == 0)
        def _():
            m_ref[...] = jnp.full_like(m_ref, NEG)
            l_ref[...] = jnp.zeros_like(l_ref)
            acc_ref[...] = jnp.zeros_like(acc_ref)

        def step(diagonal):
            q = q_ref[...] * 0.125
            k = k_ref[...].astype(BF16)
            v = v_ref[...].astype(BF16)
            causal = _causal(T) if diagonal else None
            s = jnp.concatenate([_fox_scores(_fox_head(q, hh), k, fq_ref, fr_ref, hh, causal) for hh in range(2)],
                                axis=0)
            m_old = m_ref[...]
            m_new = jnp.maximum(m_old, jnp.max(s, axis=1, keepdims=True))
            alpha = jnp.exp(m_old - m_new)
            p = jnp.exp(s - m_new)
            l_ref[...] = alpha * l_ref[...] + jnp.sum(p, axis=1, keepdims=True)
            m_ref[...] = m_new
            acc_ref[...] = alpha * acc_ref[...] + _dot(p.astype(BF16), v)

        @pl.when(ki < qi)
        def _():
            step(False)

        @pl.when(ki == qi)
        def _():
            step(True)
            h0 = _head_mask(0)
            l = l_ref[...]
            o_h = acc_ref[...] / l
            lse_h = m_ref[...] + jnp.log(l)
            o_ref[...] = jnp.where(h0, o_h[:T], o_h[T:])
            lse_ref[...] = jnp.where(h0, lse_h[:T], lse_h[T:]) - fq_ref[...]

    pairs = [(qi, ki) for qi in range(nq) for ki in range(qi + 1)]
    qt = jnp.asarray([p[0] for p in pairs], jnp.int32)
    kt = jnp.asarray([p[1] for p in pairs], jnp.int32)

    def qspec(base):
        return pl.BlockSpec((T, LANES), lambda j, t, qt, kt: (qt[t], base + j))

    def kspec(base):
        return pl.BlockSpec((T, LANES), lambda j, t, qt, kt: (kt[t], base + j))

    return pl.pallas_call(
        body, name=name,
        grid_spec=pltpu.PrefetchScalarGridSpec(
            num_scalar_prefetch=2, grid=(4, len(pairs)),
            in_specs=[qspec(Q_BLK), kspec(K_BLK), kspec(V_BLK), qspec(0),
                      pl.BlockSpec((None, 2, T), lambda j, t, qt, kt: (j, 0, kt[t]))],
            out_specs=[qspec(0), qspec(0)],
            scratch_shapes=[pltpu.VMEM((2 * T, 1), F32), pltpu.VMEM((2 * T, 1), F32),
                            pltpu.VMEM((2 * T, LANES), F32)]),
        out_shape=[_sds((L, FOX_W)), _sds((L, FOX_W))],
        compiler_params=_cp("parallel", "arbitrary"),
    )(qt, kt, z, z, z, fq, frow)


def _fox_bwd(z, frow, o, lse, g_m, name):
    L = z.shape[0]
    T = min(FOX_T, L)
    nq = L // T

    pairs = [(qi, ki) for ki in range(nq) for qi in range(ki, nq)]
    qt = jnp.asarray([p[0] for p in pairs], jnp.int32)
    kt = jnp.asarray([p[1] for p in pairs], jnp.int32)

    def body(qt_ref, kt_ref, q_ref, k_ref, v_ref, fr_ref, o_ref, lse_ref, do_ref,
             dq_ref, dk_ref, dv_ref, dfq_ref, dfk_ref, dk_acc, dv_acc, df_acc):
        t = pl.program_id(1)
        qi, ki = qt_ref[t], kt_ref[t]

        @pl.when(t == 0)
        def _():
            dq_ref[...] = jnp.zeros_like(dq_ref)
            dfq_ref[...] = jnp.zeros_like(dfq_ref)

        @pl.when(qi == ki)
        def _():
            dk_acc[...] = jnp.zeros_like(dk_acc)
            dv_acc[...] = jnp.zeros_like(dv_acc)
            df_acc[...] = jnp.zeros_like(df_acc)

        def step(diagonal):
            q = q_ref[...] * 0.125
            qb = q.astype(BF16)
            k = k_ref[...].astype(BF16)
            v = v_ref[...].astype(BF16)
            do = do_ref[...]
            dob = do.astype(BF16)
            do_o = dob.astype(F32) * o_ref[...]
            causal = _causal(T) if diagonal else None
            dvs, dks, dqs, rss = [], [], [], []
            for hh in range(2):
                s = _fox_scores(_fox_head(q, hh), k, None, fr_ref, hh, causal)
                p = jnp.exp(s - lse_ref[:, 64 * hh:64 * hh + 1])
                dp = _dot(_fox_head(do, hh), v, NT)
                delta = jnp.sum(jnp.where(_head_mask(hh), do_o, 0.0), axis=1, keepdims=True)
                ds = p * (dp - delta)
                dsb = ds.astype(BF16)
                dvs.append(_dot(p.astype(BF16), dob, TN))
                dks.append(_dot(dsb, qb, TN))
                dqs.append(_dot(dsb, k))
                rss.append(jnp.sum(ds, axis=1, keepdims=True))
                df_acc[hh:hh + 1, :] -= jnp.sum(ds, axis=0, keepdims=True)
            h0 = _head_mask(0)
            dv_acc[...] += jnp.where(h0, dvs[0], dvs[1])
            dk_acc[...] += jnp.where(h0, dks[0], dks[1])
            rows = pl.ds(pl.multiple_of(qi * T, T), T)
            dq_ref[rows, :] += jnp.where(h0, dqs[0], dqs[1])
            dfq_ref[rows, :] += jnp.where(h0, rss[0], rss[1])

        @pl.when(qi > ki)
        def _():
            step(False)

        @pl.when(qi == ki)
        def _():
            step(True)

        @pl.when(qi == nq - 1)
        def _():
            dk_ref[...] = dk_acc[...]
            dv_ref[...] = dv_acc[...]
            dfk_ref[...] = df_acc[...]

        @pl.when(t == len(pairs) - 1)
        def _():
            dq_ref[...] = dq_ref[...] * 0.125

    def qside(base):
        return pl.BlockSpec((T, LANES), lambda j, t, qt, kt: (qt[t], base + j))

    def kside(base):
        return pl.BlockSpec((T, LANES), lambda j, t, qt, kt: (kt[t], base + j))

    pair = pl.BlockSpec((L, LANES), lambda j, t, qt, kt: (0, j))
    frow_spec = pl.BlockSpec((None, 2, T), lambda j, t, qt, kt: (j, 0, kt[t]))
    return pl.pallas_call(
        body, name=name,
        grid_spec=pltpu.PrefetchScalarGridSpec(
            num_scalar_prefetch=2, grid=(4, len(pairs)),
            in_specs=[qside(Q_BLK), kside(K_BLK), kside(V_BLK), frow_spec, qside(0), qside(0), qside(0)],
            out_specs=[pair, kside(0), kside(0), pair, frow_spec],
            scratch_shapes=[pltpu.VMEM((T, LANES), F32), pltpu.VMEM((T, LANES), F32), pltpu.VMEM((2, T), F32)]),
        out_shape=[_sds((L, FOX_W)), _sds((L, FOX_W)), _sds((L, FOX_W)), _sds((L, FOX_W)), _sds((4, 2, L))],
        compiler_params=_cp("parallel", "arbitrary"),
    )(qt, kt, z, z, z, frow, o, lse, g_m)


def _shift_rows(v, s, down, row):
    n = v.shape[0]
    if down:
        return jnp.where(row >= s, pltpu.roll(v, s, 0), 0.0)
    return jnp.where(row < n - s, pltpu.roll(v, n - s, 0), 0.0)


def _cumsum_rows(v, down, row):
    s = 1
    while s < v.shape[0]:
        v = v + _shift_rows(v, s, down, row)
        s *= 2
    return v


def _window_sum(v, g, down, row):
    out = jnp.zeros_like(v)
    s = v
    for k in range(4):
        s = s + _shift_rows(s, 1 << k, down, row)
        out = jnp.where(g == k, s, out)
    return out


def _pool_inv_cnt(g, row):
    w = jnp.left_shift(2, g).astype(F32)
    return 1.0 / jnp.minimum(row.astype(F32) + 1.0, w)


def _pool_fwd(z, pool_w, scale, name):
    L = z.shape[0]

    def body(x_ref, w_ref, s_ref, y_ref, p_ref):
        g = pl.program_id(0)
        row = lax.broadcasted_iota(jnp.int32, (L, LANES), 0)
        x = x_ref[...]
        pooled = (_window_sum(x, g, True, row) * _pool_inv_cnt(g, row) - x).astype(BF16)
        p_ref[...] = pooled
        y_ref[...] = (_dot(pooled, w_ref[...].astype(BF16)) * s_ref[...]).astype(BF16)

    col = pl.BlockSpec((L, LANES), lambda g: (0, g))
    return pl.pallas_call(
        body, name=name, grid=(4,),
        in_specs=[col, pl.BlockSpec((None, LANES, LANES), lambda g: (g, 0, 0)), pl.BlockSpec((1, LANES), lambda g: (0, g))],
        out_specs=[col, col],
        out_shape=[_sds((L, 512), BF16), _sds((L, 512), BF16)],
        compiler_params=_cp("parallel"),
    )(z, pool_w, scale)


def _pool_bwd(g_y, wout, pooled, pool_w, scale, name):
    L = g_y.shape[0]

    def body(g_ref, wo_ref, p_ref, w_ref, s_ref, gx_ref, gw_ref, gs_ref):
        g = pl.program_id(0)
        row = lax.broadcasted_iota(jnp.int32, (L, LANES), 0)
        gy = _dot(g_ref[...], wo_ref[...], NT)
        pooled = p_ref[...]
        wb = w_ref[...].astype(BF16)
        lin = _dot(pooled, wb)
        gs_ref[...] = jnp.sum(gy * lin, axis=0, keepdims=True)
        glin = (gy * s_ref[...]).astype(BF16)
        gw_ref[...] = _dot(pooled, glin, TN)
        gp = _dot(glin, wb, NT)
        gx_ref[...] = _window_sum(gp * _pool_inv_cnt(g, row), g, False, row) - gp

    col = pl.BlockSpec((L, LANES), lambda g: (0, g))
    wspec = pl.BlockSpec((None, LANES, LANES), lambda g: (g, 0, 0))
    vec = pl.BlockSpec((1, LANES), lambda g: (0, g))
    return pl.pallas_call(
        body, name=name, grid=(4,),
        in_specs=[pl.BlockSpec((L, D_MODEL), lambda g: (0, 0)), pl.BlockSpec((LANES, D_MODEL), lambda g: (g, 0)),
                  col, wspec, vec],
        out_specs=[col, wspec, vec],
        out_shape=[_sds((L, 512)), _sds((4, LANES, LANES)), _sds((1, 512))],
        compiler_params=_cp("parallel"),
    )(g_y, wout, pooled, pool_w, scale)


SGU_CHUNKS = 4


def _sgu_ln(v, gam, bet):
    gv = _gelu(v)
    mu = jnp.mean(gv, axis=-1, keepdims=True)
    xc = gv - mu
    rs = lax.rsqrt(jnp.mean(xc * xc, axis=-1, keepdims=True) + EPS)
    xh = xc * rs
    return xh, rs, xh * gam + bet


def _tril_ws(w_ref, g):
    r = lax.broadcasted_iota(jnp.int32, (LANES, LANES), 0)
    c = lax.broadcasted_iota(jnp.int32, (LANES, LANES), 1)
    return jnp.where(r >= c, w_ref[g], 0.0).astype(BF16)


def _sgu_fwd(z, ln_g, ln_b, w_s, b_st, name):
    L = z.shape[0]
    rb = min(SGU_CHUNKS * LANES, L)

    def body(u_ref, v_ref, g_ref, b_ref, w_ref, bs_ref, y_ref):
        _, _, vln = _sgu_ln(v_ref[...], g_ref[...], b_ref[...])
        gu = _gelu(u_ref[...])
        vb = vln.astype(BF16)
        for g in range(4):
            ws = _tril_ws(w_ref, g)
            for n in range(rb // LANES):
                rows = slice(n * LANES, (n + 1) * LANES)
                cols = slice(g * LANES, (g + 1) * LANES)
                mixed = _dot(ws, vb[rows, cols]) + bs_ref[:, g:g + 1]
                y_ref[rows, cols] = (gu[rows, cols] * mixed).astype(BF16)

    vm = lambda shape: pl.BlockSpec(shape, lambda i: tuple(0 for _ in shape))
    return pl.pallas_call(
        body, name=name, grid=(L // rb,),
        in_specs=[pl.BlockSpec((rb, 512), lambda i: (i, 1)), pl.BlockSpec((rb, 512), lambda i: (i, 2)),
                  vm((1, 512)), vm((1, 512)), vm((4, LANES, LANES)), vm((LANES, 4))],
        out_specs=pl.BlockSpec((rb, 512), lambda i: (i, 0)),
        out_shape=_sds((L, 512), BF16),
        compiler_params=_cp("parallel"),
    )(z, z, ln_g, ln_b, w_s, b_st)


def _sgu_bwd(g_y, wout, z, ln_g, ln_b, w_s, b_st, name):
    L = z.shape[0]
    rb = min(SGU_CHUNKS * LANES, L)

    def body(gyo_ref, wo_ref, u_ref, v_ref, g_ref, b_ref, w_ref, bs_ref, gu_ref, gv_ref, gw_ref, gbs_ref, gg_ref,
             gb_ref):
        i = pl.program_id(0)

        @pl.when(i == 0)
        def _():
            gw_ref[...] = jnp.zeros_like(gw_ref)
            gbs_ref[...] = jnp.zeros_like(gbs_ref)
            gg_ref[...] = jnp.zeros_like(gg_ref)
            gb_ref[...] = jnp.zeros_like(gb_ref)

        v = v_ref[...]
        u = u_ref[...]
        gy = _dot(gyo_ref[...], wo_ref[...], NT)
        xh, rs, vln = _sgu_ln(v, g_ref[...], b_ref[...])
        gel_u = _gelu(u)
        gmix = gy * gel_u
        vb = vln.astype(BF16)
        gmb = gmix.astype(BF16)
        r = lax.broadcasted_iota(jnp.int32, (LANES, LANES), 0)
        c = lax.broadcasted_iota(jnp.int32, (LANES, LANES), 1)
        gvln_cols = []
        for g in range(4):
            ws = _tril_ws(w_ref, g)
            cols = slice(g * LANES, (g + 1) * LANES)
            gw = jnp.zeros((LANES, LANES), F32)
            gbs = jnp.zeros((LANES, 1), F32)
            parts = []
            for n in range(rb // LANES):
                rows = slice(n * LANES, (n + 1) * LANES)
                mixed = _dot(ws, vb[rows, cols]) + bs_ref[:, g:g + 1]
                gu_ref[rows, cols] = gy[rows, cols] * mixed * _gelu_grad(u[rows, cols])
                parts.append(_dot(ws, gmb[rows, cols], TN))
                gw = gw + _dot(gmb[rows, cols], vb[rows, cols], NT)
                gbs = gbs + jnp.sum(gmix[rows, cols], axis=1, keepdims=True)
            gvln_cols.append(jnp.concatenate(parts, axis=0))
            gw_ref[g] += jnp.where(r >= c, gw, 0.0)
            gbs_ref[:, g:g + 1] += gbs
        gvln = jnp.concatenate(gvln_cols, axis=1)
        gg_ref[...] += jnp.sum(gvln * xh, axis=0, keepdims=True)
        gb_ref[...] += jnp.sum(gvln, axis=0, keepdims=True)
        gxh = gvln * g_ref[...]
        ggv = rs * (gxh - jnp.mean(gxh, axis=-1, keepdims=True) - xh * jnp.mean(gxh * xh, axis=-1, keepdims=True))
        gv_ref[...] = ggv * _gelu_grad(v)

    vm = lambda shape: pl.BlockSpec(shape, lambda i: tuple(0 for _ in shape))
    blk = pl.BlockSpec((rb, 512), lambda i: (i, 0))
    return pl.pallas_call(
        body, name=name, grid=(L // rb,),
        in_specs=[pl.BlockSpec((rb, D_MODEL), lambda i: (i, 0)), pl.BlockSpec((512, D_MODEL), lambda i: (1, 0)),
                  pl.BlockSpec((rb, 512), lambda i: (i, 1)), pl.BlockSpec((rb, 512), lambda i: (i, 2)),
                  vm((1, 512)), vm((1, 512)), vm((4, LANES, LANES)), vm((LANES, 4))],
        out_specs=[blk, blk, vm((4, LANES, LANES)), vm((LANES, 4)), vm((1, 512)), vm((1, 512))],
        out_shape=[_sds((L, 512)), _sds((L, 512)), _sds((4, LANES, LANES)), _sds((LANES, 4)),
                   _sds((1, 512)), _sds((1, 512))],
        compiler_params=_cp("arbitrary"),
    )(g_y, wout, z, z, ln_g, ln_b, w_s, b_st)


def _adamw_math(w, g, m, v):
    nm = ADAM_B1 * m + (1.0 - ADAM_B1) * g
    nv = ADAM_B2 * v + (1.0 - ADAM_B2) * (g * g)
    m_hat = nm / (1.0 - ADAM_B1 ** ADAM_STEP)
    v_hat = nv / (1.0 - ADAM_B2 ** ADAM_STEP)
    delta = -ADAM_LR * (m_hat / (jnp.sqrt(v_hat) + ADAM_EPS) + ADAM_WD * w)
    return delta, nm, nv


def _sum_adamw(parts, w, m, v, name, layer=0, prev=None):
    n_layers, R, C = w.shape
    rb = 128 if R % 128 == 0 else R

    def body(p_ref, w_ref, m_ref, v_ref, *rest):
        g_ref, d_ref, nm_ref, nv_ref = rest[-4:]
        g = p_ref[0].astype(F32)
        for s in range(1, N_DEV):
            g = g + p_ref[s].astype(F32)
        d, nm, nv = _adamw_math(w_ref[...], g, m_ref[...], v_ref[...])
        g_ref[...] = g
        d_ref[...] = d
        nm_ref[...] = nm
        nv_ref[...] = nv

    blk = pl.BlockSpec((None, rb, C), lambda i: (layer, i, 0))
    prev = [] if prev is None else list(prev)
    return pl.pallas_call(
        body, name=name, grid=(R // rb,),
        in_specs=[pl.BlockSpec((N_DEV, rb, C), lambda i: (0, i, 0)), blk, blk, blk] + [ANY] * len(prev),
        out_specs=[blk] * 4, out_shape=[_sds((n_layers, R, C))] * 4,
        input_output_aliases={4 + k: k for k in range(len(prev))},
        compiler_params=_cp("parallel"),
    )(parts, w, m, v, *prev)


def _sum_adamw_rows(parts, w, m, v, name):
    R, _, C = w.shape

    def body(p_ref, w_ref, m_ref, v_ref, g_ref, d_ref, nm_ref, nv_ref):
        g = p_ref[0].astype(F32)
        for s in range(1, N_DEV):
            g = g + p_ref[s].astype(F32)
        d, nm, nv = _adamw_math(w_ref[:, 0, :], g, m_ref[:, 0, :], v_ref[:, 0, :])
        g_ref[:, 0, :] = g
        d_ref[:, 0, :] = d
        nm_ref[:, 0, :] = nm
        nv_ref[:, 0, :] = nv

    vm = pl.BlockSpec(memory_space=pltpu.VMEM)
    return pl.pallas_call(body, name=name, in_specs=[vm] * 4, out_specs=[vm] * 4, out_shape=[_sds((R, 1, C))] * 4,
                          compiler_params=pltpu.CompilerParams(vmem_limit_bytes=VMEM_LIMIT))(parts, w, m, v)


def _sum_pieces(parts, name):
    _, R, C = parts.shape

    def body(p_ref, g_ref):
        g = p_ref[0].astype(F32)
        for s in range(1, N_DEV):
            g = g + p_ref[s].astype(F32)
        g_ref[...] = g

    vm = pl.BlockSpec(memory_space=pltpu.VMEM)
    return pl.pallas_call(body, name=name, in_specs=[vm], out_specs=vm, out_shape=_sds((R, C)),
                          compiler_params=pltpu.CompilerParams(vmem_limit_bytes=VMEM_LIMIT))(parts)


def _adamw_many(ws, gs, ms, vs, name):
    n = len(ws)
    vm = pl.BlockSpec(memory_space=pltpu.VMEM)

    def body(*refs):
        w_refs, g_refs, m_refs, v_refs = refs[:n], refs[n:2 * n], refs[2 * n:3 * n], refs[3 * n:4 * n]
        d_refs, nm_refs, nv_refs = refs[4 * n:5 * n], refs[5 * n:6 * n], refs[6 * n:7 * n]
        for i in range(n):
            d, nm, nv = _adamw_math(w_refs[i][...], g_refs[i][...], m_refs[i][...], v_refs[i][...])
            d_refs[i][...] = d
            nm_refs[i][...] = nm
            nv_refs[i][...] = nv

    shapes = [_sds(w.shape) for w in ws]
    outs = pl.pallas_call(
        body, name=name, in_specs=[vm] * (4 * n), out_specs=[vm] * (3 * n), out_shape=shapes * 3,
        compiler_params=pltpu.CompilerParams(vmem_limit_bytes=VMEM_LIMIT),
    )(*ws, *gs, *ms, *vs)
    return list(outs[:n]), list(outs[n:2 * n]), list(outs[2 * n:])


def _mesh_pos():
    return lax.axis_index("x"), lax.axis_index("y"), lax.axis_index("c")


def _dev_index(p):
    return 4 * p[0] + 2 * p[1] + p[2]


HBM = pl.BlockSpec(memory_space=pltpu.HBM)
SEM = pl.BlockSpec(memory_space=pltpu.SEMAPHORE)
EFFECT = pltpu.SideEffectType.DATAFLOW_SIDE_EFFECTING


def _peer_list():
    x, y, c = _mesh_pos()
    peers = [(x ^ dx, y ^ dy, c ^ dc) for dx in range(2) for dy in range(2) for dc in range(2)][1:]
    return (x, y, c), peers


def _split_copy(src_ref, land_ref, send_sems, recv_sems, i, k, peer, slot, exchange):
    return pltpu.make_async_remote_copy(
        src_ref=src_ref.at[_dev_index(peer)] if exchange else src_ref, dst_ref=land_ref.at[slot],
        send_sem=send_sems.at[7 * i + k], recv_sem=recv_sems.at[7 * i + k], device_id=peer, device_id_type=MESH)


def _own_copy(src_ref, land_ref, own_sems, i, slot, exchange):
    return pltpu.make_async_copy(src_ref.at[slot] if exchange else src_ref, land_ref.at[slot], own_sems.at[i])


def _comm_start(groups, name, exchange, dep=None):
    sizes = [len(g) for g in groups]
    n = sum(sizes)
    srcs = [a for g in groups for a in g]
    per_group = [exchange] * len(groups) if isinstance(exchange, bool) else list(exchange)
    exchanged = [flag for flag, sz in zip(per_group, sizes) for _ in range(sz)]
    lands = [lax.empty(a.shape if ex else (N_DEV,) + a.shape, a.dtype) for a, ex in zip(srcs, exchanged)]

    n_dep = 0 if dep is None else 1

    def body(*refs):
        src_refs, land_refs = refs[:n], refs[n:2 * n]
        sem_refs = refs[2 * n + n_dep:2 * n + n_dep + 3 * len(sizes)]
        token_ref = refs[-1]
        me, peers = _peer_list()
        mi = _dev_index(me)
        i = 0
        for gi, sz in enumerate(sizes):
            for j in range(sz):
                for k, peer in enumerate(peers):
                    _split_copy(src_refs[i], land_refs[i], sem_refs[3 * gi], sem_refs[3 * gi + 1], j, k, peer, mi,
                                exchanged[i]).start()
                _own_copy(src_refs[i], land_refs[i], sem_refs[3 * gi + 2], j, mi, exchanged[i]).start()
                i += 1
        token_ref[...] = jnp.zeros_like(token_ref)

    sem_shapes = []
    for sz in sizes:
        sem_shapes += [pltpu.SemaphoreType.DMA((7 * sz,)), pltpu.SemaphoreType.DMA((7 * sz,)),
                       pltpu.SemaphoreType.DMA((sz,))]
    thru = [pltpu.HBM(a.shape, a.dtype) for a in srcs + lands]
    n_sem = len(sem_shapes)
    outs = pl.pallas_call(
        body, name=name,
        out_shape=tuple(sem_shapes + thru + [_sds((8, LANES))]),
        in_specs=[HBM] * (2 * n) + [ANY] * n_dep,
        out_specs=tuple([SEM] * n_sem + [HBM] * (2 * n) + [pl.BlockSpec(memory_space=pltpu.VMEM)]),
        input_output_aliases={i: n_sem + i for i in range(2 * n)},
        compiler_params=pltpu.CompilerParams(has_side_effects=EFFECT),
    )(*[pltpu.with_memory_space_constraint(a, pltpu.HBM) for a in srcs + lands], *([] if dep is None else [dep]))
    sems, thru_src, thru_land, token = outs[:n_sem], outs[n_sem:n_sem + n], outs[n_sem + n:n_sem + 2 * n], outs[-1]
    result, off = [], 0
    for gi, sz in enumerate(sizes):
        result.append((*sems[3 * gi:3 * gi + 3], list(thru_src[off:off + sz]), list(thru_land[off:off + sz])))
        off += sz
    return result, token


def _comm_wait(group, after, name, exchange):
    send_sems, recv_sems, own_sems, srcs, lands = group
    n = len(srcs)
    after = list(after) if isinstance(after, (list, tuple)) else [after]

    def body(*refs):
        src_refs, land_refs = refs[:n], refs[n:2 * n]
        ssem, rsem, osem = refs[2 * n:2 * n + 3]
        me, peers = _peer_list()
        for i in range(n):
            for k, peer in enumerate(peers):
                cp = _split_copy(src_refs[i], land_refs[i], ssem, rsem, i, k, peer, _dev_index(peer), exchange)
                cp.wait_send()
                cp.wait_recv()
            _own_copy(src_refs[i], land_refs[i], osem, i, _dev_index(me), exchange).wait()

    outs = pl.pallas_call(
        body, name=name,
        out_shape=tuple(pltpu.HBM(a.shape, a.dtype) for a in srcs + lands),
        in_specs=[HBM] * (2 * n) + [SEM, SEM, SEM] + [ANY] * len(after),
        out_specs=tuple([HBM] * (2 * n)),
        input_output_aliases={i: i for i in range(2 * n)},
        compiler_params=pltpu.CompilerParams(has_side_effects=EFFECT),
    )(*srcs, *lands, send_sems, recv_sems, own_sems, *after)
    return list(outs[n:])


def _tie(a, token):
    return a + token[0, 0].astype(a.dtype)


def _pack(arrs, rows):
    flat = jnp.concatenate([a.reshape(-1).astype(F32) for a in arrs])
    return jnp.pad(flat, (0, rows * LANES - flat.shape[0])).reshape(rows, LANES)


def _unpack(packed, shapes):
    flat = packed.reshape(-1)
    out, off = [], 0
    for s in shapes:
        n = math.prod(s)
        out.append(flat[off:off + n].reshape(s))
        off += n
    return out


def _packed_rows(shapes):
    n = sum(math.prod(s) for s in shapes)
    unit = N_DEV * 8 * LANES
    return -(-n // unit) * unit // LANES


def kernel(x, mix_pre_g, mix_post_g, mlp_pre_g, mlp_post_g, w_in_even, s5_lam_re, s5_lam_im, s5_log_dt, s5_b_re, s5_b_im, s5_c_re, s5_c_im, s5_d, s5_w_glu, fox_b_f, w_out_even, w_in_odd, pool_w, pool_scale, sgu_ln_g, sgu_ln_b, sgu_w_s, sgu_b_s, w_out_odd, mlp_w1, mlp_w2, loss_target, m_mix_pre_g, m_mix_post_g, m_mlp_pre_g, m_mlp_post_g, m_w_in_even, m_s5_lam_re, m_s5_lam_im, m_s5_log_dt, m_s5_b_re, m_s5_b_im, m_s5_c_re, m_s5_c_im, m_s5_d, m_s5_w_glu, m_fox_b_f, m_w_out_even, m_w_in_odd, m_pool_w, m_pool_scale, m_sgu_ln_g, m_sgu_ln_b, m_sgu_w_s, m_sgu_b_s, m_w_out_odd, m_mlp_w1, m_mlp_w2, v_mix_pre_g, v_mix_post_g, v_mlp_pre_g, v_mlp_post_g, v_w_in_even, v_s5_lam_re, v_s5_lam_im, v_s5_log_dt, v_s5_b_re, v_s5_b_im, v_s5_c_re, v_s5_c_im, v_s5_d, v_s5_w_glu, v_fox_b_f, v_w_out_even, v_w_in_odd, v_pool_w, v_pool_scale, v_sgu_ln_g, v_sgu_ln_b, v_sgu_w_s, v_sgu_b_s, v_w_out_odd, v_mlp_w1, v_mlp_w2):
    weights = dict(mix_pre_g=mix_pre_g, mix_post_g=mix_post_g, mlp_pre_g=mlp_pre_g, mlp_post_g=mlp_post_g, w_in_even=w_in_even, s5_lam_re=s5_lam_re, s5_lam_im=s5_lam_im, s5_log_dt=s5_log_dt, s5_b_re=s5_b_re, s5_b_im=s5_b_im, s5_c_re=s5_c_re, s5_c_im=s5_c_im, s5_d=s5_d, s5_w_glu=s5_w_glu, fox_b_f=fox_b_f, w_out_even=w_out_even, w_in_odd=w_in_odd, pool_w=pool_w, pool_scale=pool_scale, sgu_ln_g=sgu_ln_g, sgu_ln_b=sgu_ln_b, sgu_w_s=sgu_w_s, sgu_b_s=sgu_b_s, w_out_odd=w_out_odd, mlp_w1=mlp_w1, mlp_w2=mlp_w2)
    mom_m = dict(mix_pre_g=m_mix_pre_g, mix_post_g=m_mix_post_g, mlp_pre_g=m_mlp_pre_g, mlp_post_g=m_mlp_post_g, w_in_even=m_w_in_even, s5_lam_re=m_s5_lam_re, s5_lam_im=m_s5_lam_im, s5_log_dt=m_s5_log_dt, s5_b_re=m_s5_b_re, s5_b_im=m_s5_b_im, s5_c_re=m_s5_c_re, s5_c_im=m_s5_c_im, s5_d=m_s5_d, s5_w_glu=m_s5_w_glu, fox_b_f=m_fox_b_f, w_out_even=m_w_out_even, w_in_odd=m_w_in_odd, pool_w=m_pool_w, pool_scale=m_pool_scale, sgu_ln_g=m_sgu_ln_g, sgu_ln_b=m_sgu_ln_b, sgu_w_s=m_sgu_w_s, sgu_b_s=m_sgu_b_s, w_out_odd=m_w_out_odd, mlp_w1=m_mlp_w1, mlp_w2=m_mlp_w2)
    mom_v = dict(mix_pre_g=v_mix_pre_g, mix_post_g=v_mix_post_g, mlp_pre_g=v_mlp_pre_g, mlp_post_g=v_mlp_post_g, w_in_even=v_w_in_even, s5_lam_re=v_s5_lam_re, s5_lam_im=v_s5_lam_im, s5_log_dt=v_s5_log_dt, s5_b_re=v_s5_b_re, s5_b_im=v_s5_b_im, s5_c_re=v_s5_c_re, s5_c_im=v_s5_c_im, s5_d=v_s5_d, s5_w_glu=v_s5_w_glu, fox_b_f=v_fox_b_f, w_out_even=v_w_out_even, w_in_odd=v_w_in_odd, pool_w=v_pool_w, pool_scale=v_pool_scale, sgu_ln_g=v_sgu_ln_g, sgu_ln_b=v_sgu_ln_b, sgu_w_s=v_sgu_w_s, sgu_b_s=v_sgu_b_s, w_out_odd=v_w_out_odd, mlp_w1=v_mlp_w1, mlp_w2=v_mlp_w2)
    names = list(weights)
    L = x.shape[1]
    x0 = x[0]
    target = loss_target[0]
    my_index = 4 * lax.axis_index("x") + 2 * lax.axis_index("y") + lax.axis_index("c")

    small_vec = jnp.zeros((8, LANES), F32)
    small_vec = small_vec.at[0, :64].set(pool_scale[0]).at[1, :64].set(sgu_ln_g[0]).at[2, :64].set(sgu_ln_b[0])
    ag_groups, ag_token = _comm_start(
        [[jnp.transpose(w_in_even[0]).astype(BF16), small_vec],
         [s5_w_glu[0].astype(BF16), w_out_even[0].astype(BF16)],
         [mlp_w1[0].astype(BF16), mlp_w2[0].astype(BF16)],
         [jnp.transpose(w_in_odd[0]).astype(BF16), w_out_odd[0].astype(BF16), mlp_w1[1].astype(BF16), mlp_w2[1].astype(BF16)]],
        "ag_start", exchange=False)

    lam_r = jnp.concatenate([s5_lam_re.reshape(1, S5_NS), s5_lam_im.reshape(1, S5_NS)], axis=0)
    ldt_r = jnp.repeat(s5_log_dt.reshape(32), 64).reshape(1, S5_NS)
    lam_c = jnp.transpose(lam_r)
    ldt_c = jnp.transpose(ldt_r)
    b_t = jnp.stack([jnp.tile(s5_b_re.reshape(S5_NS, 16), (1, 8)), jnp.tile(s5_b_im.reshape(S5_NS, 16), (1, 8))])
    c_t = jnp.stack([jnp.tile(s5_c_re.reshape(S5_W, 64), (1, 8)), jnp.tile(s5_c_im.reshape(S5_W, 64), (1, 8))])
    bf_pad = jnp.pad(fox_b_f, ((0, 0), (0, LANES - 8)))
    b_st = jnp.transpose(sgu_b_s[0])

    h0, rx0 = _rms_fwd(x0, _tie(mix_pre_g[0:1], ag_token), "rms0")
    tabs, bset, cset = _s5_prep(lam_r, ldt_r, lam_c, ldt_c, b_t, c_t, "s5_prep")
    ag0 = _comm_wait(ag_groups[0], tabs, "ag_wait0", exchange=False)
    winT_e = jnp.pad(ag0[0].reshape(EVEN_IN, D_MODEL), ((0, EVEN_PAD - EVEN_IN), (0, 0)))
    pool_scale_f = ag0[1][:, 0, :64].reshape(1, 512)
    ln_g_f = ag0[1][:, 1, :64].reshape(1, 512)
    ln_b_f = ag0[1][:, 2, :64].reshape(1, 512)
    z0 = _mm(h0, winT_e, name="win_even", tb=True, bm=512, bn=EVEN_PAD)
    xs, ylin = _s5_scan_fwd(z0, bset, cset, s5_d, tabs, "s5_scan")
    ag1 = _comm_wait(ag_groups[1], ylin, "ag_wait1", exchange=False)
    wglu = ag1[0].reshape(S5_W, S5_W)
    wout_e = ag1[1].reshape(D_MODEL, D_MODEL)
    ya = _s5_glu_fwd(ylin, wglu, "s5_glu")
    fcum, fq = _fox_f_fwd(z0, bf_pad, "fox_f")
    frow = jnp.transpose(fcum[:, :8]).reshape(4, 2, L)
    o_att, lse = _fox_fwd(z0, fq, frow, "fox_fwd")
    mix0 = [ya, o_att]
    x1, ry0, h1, rx1, y0 = _mm(mix0, wout_e, name="wout_even", epi=_epi_post_pre, extra=(x0,),
                               vecs=(mix_post_g[0:1], mlp_pre_g[0:1]), out_dtypes=POST_PRE_DTYPES,
                               out_kinds=POST_PRE_KINDS, bm=FUSED_ROWS)
    ag2 = _comm_wait(ag_groups[2], rx1, "ag_wait2", exchange=False)
    w1 = [ag2[0], None]
    w2 = [ag2[1].reshape(4 * D_MODEL, D_MODEL), None]
    p0, a0 = _mm(h1, w1[0], name="mlp0_w1", b3=True, out_dtypes=(BF16, BF16), epi=_epi_relu2, bm=512, bn=4 * D_MODEL)
    x2, ro0, h2, rx2, o0 = _mm(a0, w2[0], name="mlp0_w2", epi=_epi_post_pre, extra=(x1,),
                               vecs=(mlp_post_g[0:1], mix_pre_g[1:2]), out_dtypes=POST_PRE_DTYPES,
                               out_kinds=POST_PRE_KINDS, bm=FUSED_ROWS, bk=4 * D_MODEL)
    ag3 = _comm_wait(ag_groups[3], rx2, "ag_wait3", exchange=False)
    winT_o = ag3[0].reshape(ODD_IN, D_MODEL)
    wout_o = ag3[1].reshape(D_MODEL, D_MODEL)
    w1[1] = ag3[2]
    w2[1] = ag3[3].reshape(4 * D_MODEL, D_MODEL)
    z1 = _mm(h2, winT_o, name="win_odd", tb=True, bn=ODD_IN)
    yc, pooled = _pool_fwd(z1, pool_w[0], pool_scale_f, "pool_fwd")
    yd = _sgu_fwd(z1, ln_g_f, ln_b_f, sgu_w_s[0], b_st, "sgu_fwd")
    mix1 = [yc, yd]
    x3, ry1, h3, rx3, y1 = _mm(mix1, wout_o, name="wout_odd", epi=_epi_post_pre, extra=(x2,),
                               vecs=(mix_post_g[1:2], mlp_pre_g[1:2]), out_dtypes=POST_PRE_DTYPES,
                               out_kinds=POST_PRE_KINDS, bm=FUSED_ROWS)
    p1, a1 = _mm(h3, w1[1], name="mlp1_w1", b3=True, out_dtypes=(BF16, BF16), epi=_epi_relu2, bm=512, bn=4 * D_MODEL)
    gx4, g_o1, gg_mlp_post1, sq_lanes = _mm(
        a1, w2[1], name="mlp1_w2", epi=_epi_post_loss, extra=(x3, target), vecs=(mlp_post_g[1:2],),
        out_dtypes=(F32, BF16, F32, F32), out_kinds=("full", "full", "vsum", "vsum"), bm=FUSED_ROWS, bk=4 * D_MODEL)
    sq = sq_lanes[:, 0:1]

    g_p1 = _mm(g_o1, w2[1], name="b_mlp1_a", tb=True, out_dtypes=(BF16,), epi=_epi_relu2_bwd, extra=(p1,),
               bm=512, bn=4 * D_MODEL)
    gw2_1 = _mm(a1, g_o1, name="b_mlp1_w2", ta=True, bm=512, bk=L)
    gw1_1 = _mm(h3, g_p1, name="b_mlp1_w1", ta=True, out3=True, bn=512, bk=L)
    (ex1,), tok1 = _comm_start([[gw1_1, gw2_1.reshape(N_DEV, 512, D_MODEL)]], "ex_start1", exchange=True)
    g_x3, gg_mlp_pre1, g_y1, gg_mix_post1 = _mm(
        g_p1, w1[1], name="b_mlp1_h", tb=True, b3=True, epi=_epi_pre_post_bwd, extra=(x3, gx4, y1), cols=(rx3, ry1),
        vecs=(_tie(mlp_pre_g[1:2], tok1), mix_post_g[1:2]), out_dtypes=PRE_POST_BWD_DTYPES,
        out_kinds=PRE_POST_BWD_KINDS, bm=FUSED_ROWS, bk=4 * D_MODEL)
    gwout_o = _mm(mix1, g_y1, name="b_wout_odd_w", ta=True)
    g_xc, g_pool_w, g_pool_scale = _pool_bwd(g_y1, wout_o, pooled, pool_w[0], pool_scale_f, "pool_bwd")
    g_u1, g_v1, g_ws, g_bst, g_ln_g, g_ln_b = _sgu_bwd(g_y1, wout_o, z1, ln_g_f, ln_b_f, sgu_w_s[0], b_st,
                                                       "sgu_bwd")
    g_z1 = [g_xc, g_u1, g_v1]
    gwinT_o = _mm(g_z1, h2, name="b_win_odd_w", ta=True)
    (ex2,), tok2 = _comm_start([[gwout_o.reshape(N_DEV, 128, D_MODEL), gwinT_o.reshape(N_DEV, ODD_IN // N_DEV, D_MODEL)]], "ex_start2", exchange=True)
    g_x2, gg_mix_pre1, g_o0, gg_mlp_post0 = _mm(
        g_z1, winT_o, name="b_win_odd_h", epi=_epi_pre_post_bwd, extra=(x2, g_x3, o0), cols=(rx2, ro0),
        vecs=(_tie(mix_pre_g[1:2], tok2), mlp_post_g[0:1]), out_dtypes=PRE_POST_BWD_DTYPES,
        out_kinds=PRE_POST_BWD_KINDS, bm=FUSED_ROWS)
    g_p0 = _mm(g_o0, w2[0], name="b_mlp0_a", tb=True, out_dtypes=(BF16,), epi=_epi_relu2_bwd, extra=(p0,),
               bm=512, bn=4 * D_MODEL)
    gw2_0 = _mm(a0, g_o0, name="b_mlp0_w2", ta=True, bm=512, bk=L)
    gw1_0 = _mm(h1, g_p0, name="b_mlp0_w1", ta=True, out3=True, bn=512, bk=L)
    (ex3,), tok3 = _comm_start([[gw1_0, gw2_0.reshape(N_DEV, 512, D_MODEL)]], "ex_start3", exchange=True)
    g_x1, gg_mlp_pre0, g_y0, gg_mix_post0 = _mm(
        g_p0, w1[0], name="b_mlp0_h", tb=True, b3=True, epi=_epi_pre_post_bwd, extra=(x1, g_x2, y0), cols=(rx1, ry0),
        vecs=(_tie(mlp_pre_g[0:1], tok3), mix_post_g[0:1]), out_dtypes=PRE_POST_BWD_DTYPES,
        out_kinds=PRE_POST_BWD_KINDS, bm=FUSED_ROWS, bk=4 * D_MODEL)
    g_o_att = _mm(g_y0, wout_e[FOX_W:], name="b_wout_even_m", tb=True)
    gwout_e = _mm(mix0, g_y0, name="b_wout_even_w", ta=True)
    gyl, gud, g_wglu, g_d = _s5_glu_bwd(g_y0, wout_e, ylin, z0, s5_d, wglu, "s5_glu_bwd")
    (ex4,), tok4 = _comm_start([[gwout_e.reshape(N_DEV, 128, D_MODEL), g_wglu.reshape(N_DEV, 64, S5_W)]], "ex_start4", exchange=True)
    g_u0, ga, gb_raw, gc_raw = _s5_scan_bwd(gyl, _tie(cset, tok4), xs, z0, bset, gud, tabs, "s5_scan_bwd")
    g_lam, g_ldt, g_b, g_c = _s5_param_bwd(lam_c, ldt_c, b_t, gb_raw, jnp.transpose(ga), gc_raw, "s5_param_bwd")
    dq, dk, dv, dfq, dfrow = _fox_bwd(z0, frow, o_att, lse, g_o_att, "fox_bwd")
    dFk = jnp.pad(jnp.transpose(dfrow.reshape(8, L)), ((0, 0), (0, LANES - 8)))
    dfl, db_f = _fox_f_bwd(dFk, dfq, z0, bf_pad, "fox_f_bwd")
    g_z0 = [g_u0, dq, dk, dv, dfl]
    grad_x, gg_mix_pre0 = _mm(g_z0, winT_e, name="b_win_even_h", epi=_epi_pre_bwd, extra=(x0, g_x1), cols=(rx0,),
                              vecs=(mix_pre_g[0:1],), out_dtypes=(F32, F32), out_kinds=("full", "vsum"),
                              bm=FUSED_ROWS)

    small_grads = dict(
        mix_pre_g=jnp.concatenate([gg_mix_pre0, gg_mix_pre1]), mix_post_g=jnp.concatenate([gg_mix_post0, gg_mix_post1]),
        mlp_pre_g=jnp.concatenate([gg_mlp_pre0, gg_mlp_pre1]), mlp_post_g=jnp.concatenate([gg_mlp_post0, gg_mlp_post1]),
        s5_lam_re=g_lam[:, 0], s5_lam_im=g_lam[:, 1],
        s5_b_re=g_b[0, :, :16], s5_b_im=g_b[1, :, :16], s5_c_re=g_c[0, :, :64], s5_c_im=g_c[1, :, :64],
        pool_w=g_pool_w, sgu_w_s=g_ws, s5_d=g_d, sgu_b_s=jnp.transpose(g_bst),
        pool_scale=g_pool_scale, sgu_ln_g=g_ln_g, sgu_ln_b=g_ln_b, s5_log_dt=g_ldt, fox_b_f=db_f[:, :8])
    small_names = list(small_grads)
    full_shapes = [(512,) if nm in ("pool_scale", "sgu_ln_g", "sgu_ln_b") else weights[nm].shape for nm in small_names]
    full_shapes.append((1, 1))
    rows = _packed_rows(full_shapes)
    packed = _pack([small_grads[nm] for nm in small_names] + [sq], rows).reshape(N_DEV, rows // N_DEV, LANES)
    (exs,), tok_s = _comm_start([[packed]], "exs_start", exchange=True)
    gwinT_e = _mm(g_z0, h0, name="b_win_even_w", ta=True, out_dtypes=(BF16,), dep=tok_s)
    (recv_small,) = _comm_wait(exs, gwinT_e, "exs_wait", exchange=True)
    piece = _sum_pieces(recv_small, "sum_small")
    gwinT_e_pieces = gwinT_e[:EVEN_IN].reshape(N_DEV, EVEN_IN // N_DEV, D_MODEL)
    (ags, ex5), tok5 = _comm_start([[piece], [gwinT_e_pieces]], "ags_ex_start5", exchange=(False, True))
    r_w1_1, r_w2_1 = _comm_wait(ex1, tok5, "ex_wait1", exchange=True)
    r_wout_o, r_win_o = _comm_wait(ex2, tok5, "ex_wait2", exchange=True)
    r_w1_0, r_w2_0 = _comm_wait(ex3, tok5, "ex_wait3", exchange=True)
    r_wout_e, r_wglu = _comm_wait(ex4, tok5, "ex_wait4", exchange=True)

    res = {}
    for nm, parts in (("mlp_w1", (r_w1_0, r_w1_1)), ("mlp_w2", (r_w2_0, r_w2_1))):
        first = _sum_adamw(parts[0], weights[nm], mom_m[nm], mom_v[nm], "adamw_%s_0" % nm, layer=0)
        res[nm] = tuple(_sum_adamw(parts[1], weights[nm], mom_m[nm], mom_v[nm], "adamw_%s_1" % nm, layer=1, prev=first))
    big_parts = dict(s5_w_glu=r_wglu, w_out_even=r_wout_e, w_out_odd=r_wout_o)
    for nm, parts in big_parts.items():
        res[nm] = tuple(_sum_adamw(parts, weights[nm], mom_m[nm], mom_v[nm], "adamw_" + nm))
    done = [res[nm][1] for nm in ("mlp_w1", "mlp_w2", "s5_w_glu", "w_out_even", "w_out_odd")]

    (small_all,) = _comm_wait(ags, done, "ags_wait", exchange=False)
    small_full = _unpack(small_all.reshape(rows, LANES), full_shapes)
    loss = 0.5 * small_full.pop()[0, 0] / D_MODEL
    small_g = []
    for nm, g in zip(small_names, small_full):
        if nm in ("pool_scale", "sgu_ln_g", "sgu_ln_b"):
            g = lax.dynamic_slice(g, (my_index * 64,), (64,)).reshape(1, 64)
        small_g.append(g)

    def turned(arrs):
        return [jnp.swapaxes(a, -1, -2) if nm in ("s5_b_re", "s5_b_im") else a for nm, a in zip(small_names, arrs)]

    sd, sm, sv = _adamw_many(turned([weights[nm] for nm in small_names]), turned(small_g),
                             turned([mom_m[nm] for nm in small_names]), turned([mom_v[nm] for nm in small_names]),
                             "adamw_small")
    for nm, g_, d_, m_, v_ in zip(small_names, small_g, turned(sd), turned(sm), turned(sv)):
        res[nm] = (g_, d_, m_, v_)
    done.append(sd[0])

    nm = "w_in_odd"
    outs = _sum_adamw(r_win_o, jnp.transpose(weights[nm], (0, 2, 1)), jnp.transpose(mom_m[nm], (0, 2, 1)),
                      jnp.transpose(mom_v[nm], (0, 2, 1)), "adamw_" + nm)
    res[nm] = tuple(jnp.transpose(o, (0, 2, 1)) for o in outs)
    done.append(res[nm][1])
    nm = "w_in_even"
    (r_win_e,) = _comm_wait(ex5, done, "ex_wait5", exchange=True)
    outs = _sum_adamw_rows(r_win_e, jnp.transpose(weights[nm], (2, 0, 1)), jnp.transpose(mom_m[nm], (2, 0, 1)),
                           jnp.transpose(mom_v[nm], (2, 0, 1)), "adamw_" + nm)
    res[nm] = tuple(jnp.transpose(o, (1, 2, 0)) for o in outs)

    grads = [res[nm][0].reshape(weights[nm].shape) for nm in names]
    deltas = [res[nm][1].reshape(weights[nm].shape) for nm in names]
    new_m = [res[nm][2].reshape(weights[nm].shape) for nm in names]
    new_v = [res[nm][3].reshape(weights[nm].shape) for nm in names]
    return (loss, grad_x[None], *grads, *deltas, *new_m, *new_v)
```

```python
import math

import jax
import jax.numpy as jnp
from jax import lax
from jax.experimental import pallas as pl
from jax.experimental.pallas import tpu as pltpu

F32 = jnp.float32
BF16 = jnp.bfloat16
MESH = pl.DeviceIdType.MESH
ANY = pl.BlockSpec(memory_space=pl.ANY)

N_DEV = 8
D_MODEL = 1024
EPS = 1e-6
NORM_ROWS = 512
FUSED_ROWS = 512
S5_W = 512
S5_NS = 2048
SCAN_GROUPS = 4
SCAN_CHUNK = 1024
FOX_W = 512
EVEN_IN = 2056
EVEN_PAD = 2176
ODD_IN = 1536
LANES = 128
PIECE = 4 * D_MODEL // N_DEV
VMEM_LIMIT = 56 * 1024 * 1024

ADAM_LR = 0.001
ADAM_B1 = 0.9
ADAM_B2 = 0.999
ADAM_EPS = 1e-08
ADAM_WD = 0.01
ADAM_STEP = 10

NT = (((1,), (1,)), ((), ()))
TN = (((0,), (0,)), ((), ()))
NN = (((1,), (0,)), ((), ()))


def _cp(*sem):
    return pltpu.CompilerParams(dimension_semantics=sem, vmem_limit_bytes=VMEM_LIMIT)


def _sds(shape, dtype=F32):
    return jax.ShapeDtypeStruct(tuple(shape), dtype)


def _gelu(x):
    t = jnp.tanh(0.7978845608028654 * (x + 0.044715 * x * x * x))
    return 0.5 * x * (1.0 + t)


def _gelu_grad(x):
    t = jnp.tanh(0.7978845608028654 * (x + 0.044715 * x * x * x))
    du = 0.7978845608028654 * (1.0 + 3.0 * 0.044715 * x * x)
    return 0.5 * (1.0 + t) + 0.5 * x * (1.0 - t * t) * du


def _sigmoid(x):
    return 1.0 / (1.0 + jnp.exp(-x))


def _dot(a, b, dn=NN):
    return lax.dot_general(a, b, dn, preferred_element_type=F32)


def _mm(a, b, *, name, ta=False, tb=False, b3=False, out3=False, out_dtypes=(F32,), epi=None, extra=(),
        cols=(), vecs=(), out_kinds=None, bm=1024, bn=1024, bk=1024, dep=None):
    a_list = list(a) if isinstance(a, (list, tuple)) else [a]
    widths = [p.shape[1] for p in a_list]
    offs = [sum(widths[:i]) for i in range(len(widths))]
    na = len(a_list)
    M = sum(widths) if ta else a_list[0].shape[0]
    K = a_list[0].shape[0] if ta else sum(widths)
    if na > 1:
        assert not b3 and not tb
        bm, bk = (M, bk) if ta else (bm, K)
    pw = b.shape[2] if b3 else PIECE
    if b3:
        N = b.shape[1] if tb else b.shape[0] * pw
        assert (b.shape[0] * pw if tb else b.shape[1]) == K
    else:
        N = b.shape[0] if tb else b.shape[1]
    bm, bn, bk = min(bm, M), min(bn, N), min(bk, K)
    assert M % bm == 0 and N % bn == 0 and K % bk == 0, (name, M, N, K, bm, bn, bk)
    assert not (b3 or out3) or ((bk if tb else bn) % pw == 0 and bn % PIECE == 0)
    nk = K // bk
    n_extra = len(extra) + len(cols) + len(vecs)
    n_out = len(out_dtypes)
    out_kinds = tuple(out_kinds) if out_kinds is not None else ("full",) * n_out
    dn = (((0 if ta else 1,), (1 if tb else 0,)), ((), ()))

    use_acc = nk > 1

    def body(*refs):
        a_refs, b_ref = refs[:na], refs[na]
        a_ref = a_refs[0]
        e_refs = refs[na + 1:na + 1 + n_extra]
        first_out = na + 1 + n_extra + (0 if dep is None else 1)
        o_refs = refs[first_out:first_out + n_out]
        acc_ref = refs[-1] if use_acc else o_refs[0]
        i, k = pl.program_id(0), pl.program_id(2)

        def dot(a_v, b_v):
            return lax.dot_general(a_v.astype(BF16), b_v.astype(BF16), dn, preferred_element_type=F32)

        everything = slice(None)
        if na > 1 and ta:
            terms = [(pl.ds(off, w), everything, r, b_ref) for r, off, w in zip(a_refs, offs, widths)]
        elif na > 1:
            terms = [(everything, everything, r, b_ref.at[pl.ds(off, w), :]) for r, off, w in zip(a_refs, offs, widths)]
        elif not b3:
            terms = [(everything, everything, a_ref, b_ref)]
        elif tb:
            terms = [(everything, everything,
                      a_ref.at[pl.ds(t * pw, pw), :] if ta else a_ref.at[:, pl.ds(t * pw, pw)], b_ref.at[t])
                     for t in range(bk // pw)]
        else:
            terms = [(everything, pl.ds(t * pw, pw), a_ref, b_ref.at[t]) for t in range(bn // pw)]

        def finish(acc):
            outs = (acc,) if epi is None else epi(acc, *[e[...] for e in e_refs])
            for o_ref, o, kind in zip(o_refs, outs, out_kinds):
                if kind == "vsum":
                    @pl.when(i == 0)
                    def _(o_ref=o_ref, o=o):
                        o_ref[...] = o

                    @pl.when(i > 0)
                    def _(o_ref=o_ref, o=o):
                        o_ref[...] += o
                elif out3:
                    for t in range(bn // PIECE):
                        o_ref[t] = o[:, t * PIECE:(t + 1) * PIECE].astype(o_ref.dtype)
                else:
                    o_ref[...] = o.astype(o_ref.dtype)

        if nk == 1:
            bands = {}
            for rows, cols, a_r, b_r in terms:
                key = (getattr(rows, "start", None), getattr(cols, "start", None))
                val = dot(a_r[...], b_r[...])
                bands[key] = val if key not in bands else bands[key] + val
            vals = list(bands.values())
            if len(vals) == 1:
                finish(vals[0])
            else:
                finish(jnp.concatenate(vals, axis=0 if (na > 1 and ta) else 1))
            return

        @pl.when(k == 0)
        def _():
            acc_ref[...] = jnp.zeros_like(acc_ref)

        for rows, cols, a_r, b_r in terms:
            acc_ref[rows, cols] += dot(a_r[...], b_r[...])

        @pl.when(k == nk - 1)
        def _():
            finish(acc_ref[...])

    if na > 1:
        a_specs = [pl.BlockSpec((bk, w), lambda i, j, k: (k, 0)) if ta else pl.BlockSpec((bm, w), lambda i, j, k: (i, 0))
                   for w in widths]
    else:
        a_specs = [pl.BlockSpec((bk, bm), lambda i, j, k: (k, i)) if ta else
                   pl.BlockSpec((bm, bk), lambda i, j, k: (i, k))]
    if b3:
        if tb:
            b_spec = pl.BlockSpec((bk // pw, bn, pw), lambda i, j, k: (k, j, 0))
        else:
            b_spec = pl.BlockSpec((bn // pw, bk, pw), lambda i, j, k: (j, k, 0))
    else:
        b_spec = pl.BlockSpec((bn, bk), lambda i, j, k: (j, k)) if tb else pl.BlockSpec((bk, bn), lambda i, j, k: (k, j))
    e_specs = ([pl.BlockSpec((bm, bn), lambda i, j, k: (i, j)) for _ in extra]
               + [pl.BlockSpec((bm, 1), lambda i, j, k: (i, 0)) for _ in cols]
               + [pl.BlockSpec((1, bn), lambda i, j, k: (0, j)) for _ in vecs])
    if out3:
        o_specs = [pl.BlockSpec((bn // PIECE, bm, PIECE), lambda i, j, k: (j, i, 0)) for _ in out_dtypes]
        o_shapes = [_sds((N // PIECE, M, PIECE), dt) for dt in out_dtypes]
    else:
        spec_of = {"full": pl.BlockSpec((bm, bn), lambda i, j, k: (i, j)),
                   "col": pl.BlockSpec((bm, 1), lambda i, j, k: (i, 0)),
                   "vsum": pl.BlockSpec((1, bn), lambda i, j, k: (0, j))}
        shape_of = {"full": (M, N), "col": (M, 1), "vsum": (1, N)}
        o_specs = [spec_of[kind] for kind in out_kinds]
        o_shapes = [_sds(shape_of[kind], dt) for kind, dt in zip(out_kinds, out_dtypes)]
    assert "col" not in out_kinds or bn == N
    outs = pl.pallas_call(
        body, name=name, grid=(M // bm, N // bn, nk),
        in_specs=a_specs + [b_spec] + e_specs + ([] if dep is None else [ANY]),
        out_specs=o_specs, out_shape=o_shapes,
        scratch_shapes=[pltpu.VMEM((bm, bn), F32)] if use_acc else [],
        compiler_params=_cp("arbitrary" if "vsum" in out_kinds else "parallel", "parallel", "arbitrary"),
    )(*a_list, b, *extra, *cols, *vecs, *([] if dep is None else [dep]))
    return outs[0] if n_out == 1 else outs


def _epi_relu2(acc):
    r = jnp.maximum(acc, 0.0)
    return acc, r * r


def _epi_relu2_bwd(acc, p):
    return (acc * (2.0 * jnp.maximum(p.astype(F32), 0.0)),)


def _row_spec(rb, w=D_MODEL):
    return pl.BlockSpec((rb, w), lambda i: (i, 0))


def _vec_spec(w=D_MODEL):
    return pl.BlockSpec((1, w), lambda i: (0, 0))


def _rstd(v):
    return lax.rsqrt(jnp.mean(v * v, axis=-1, keepdims=True) + EPS)


def _rms_fwd(x, g, name):
    L = x.shape[0]
    rb = min(NORM_ROWS, L)

    def body(x_ref, g_ref, h_ref, r_ref):
        xv = x_ref[...]
        r = _rstd(xv)
        h_ref[...] = (xv * r * g_ref[...]).astype(BF16)
        r_ref[...] = r

    return pl.pallas_call(
        body, name=name, grid=(L // rb,),
        in_specs=[_row_spec(rb), _vec_spec()],
        out_specs=[_row_spec(rb), _row_spec(rb, 1)],
        out_shape=[_sds((L, D_MODEL), BF16), _sds((L, 1))],
        compiler_params=_cp("parallel"),
    )(x, g)


def _rms_bwd_rows(dy, xv, r, g):
    n = xv * r
    dyg = dy * g
    return r * (dyg - n * jnp.mean(dyg * n, axis=-1, keepdims=True)), n


POST_PRE_DTYPES = (F32, F32, BF16, F32, F32)
POST_PRE_KINDS = ("full", "col", "full", "col", "full")
PRE_POST_BWD_DTYPES = (F32, F32, BF16, F32)
PRE_POST_BWD_KINDS = ("full", "vsum", "full", "vsum")


def _epi_post_pre(y, x_in, g_post, g_pre):
    ry = _rstd(y)
    xo = x_in + y * ry * g_post
    rx = _rstd(xo)
    return xo, ry, xo * rx * g_pre, rx, y


def _epi_pre_post_bwd(gh, x, g_out, y_prev, rx, ry_prev, g_pre, g_post_prev):
    gx, n = _rms_bwd_rows(gh, x, rx, g_pre)
    gi = g_out + gx
    gy, ny = _rms_bwd_rows(gi, y_prev, ry_prev, g_post_prev)
    return gi, jnp.sum(gh * n, axis=0, keepdims=True), gy, jnp.sum(gi * ny, axis=0, keepdims=True)


def _epi_pre_bwd(gh, x, g_out, rx, g_pre):
    gx, n = _rms_bwd_rows(gh, x, rx, g_pre)
    return g_out + gx, jnp.sum(gh * n, axis=0, keepdims=True)


def _epi_post_loss(y, x_in, target, g_post):
    ry = _rstd(y)
    diff = x_in + y * ry * g_post - target
    gx = diff * (1.0 / D_MODEL)
    gy, n = _rms_bwd_rows(gx, y, ry, g_post)
    sq = jnp.broadcast_to(jnp.sum(diff * diff, keepdims=True), (1, y.shape[1]))
    return gx, gy, jnp.sum(gx * n, axis=0, keepdims=True), sq


def _cmul(ar, ai, br, bi):
    return ar * br - ai * bi, ar * bi + ai * br


def _zoh_cols(lr, li, ldt):
    dt = jnp.exp(ldt)
    mag = jnp.exp(lr * dt)
    ar = mag * jnp.cos(li * dt)
    ai = mag * jnp.sin(li * dt)
    den = lr * lr + li * li
    nr = ar - 1.0
    qr = (nr * lr + ai * li) / den
    qi = (ai * lr - nr * li) / den
    return dt, ar, ai, qr, qi, den


def _b_mask():
    r = lax.broadcasted_iota(jnp.int32, (S5_NS, LANES), 0)
    c = lax.broadcasted_iota(jnp.int32, (S5_NS, LANES), 1)
    return ((r >> 6) & 7) == (c >> 4)


def _c_mask():
    r = lax.broadcasted_iota(jnp.int32, (S5_W, 512), 0)
    c = lax.broadcasted_iota(jnp.int32, (S5_W, 512), 1)
    return ((r >> 4) & 7) == (c >> 6)


def _s5_prep(lam_r, ldt_r, lam_c, ldt_c, b_t, c_t, name):
    def body(lam_r_ref, ldt_r_ref, lam_c_ref, ldt_c_ref, b_ref, c_ref, tab_ref, bset_ref, cset_ref):
        lr, li = lam_r_ref[0:1, :], lam_r_ref[1:2, :]
        dt = jnp.exp(ldt_r_ref[...])
        mag = jnp.exp(lr * dt)
        p1r, p1i = mag * jnp.cos(li * dt), mag * jnp.sin(li * dt)
        p2r, p2i = _cmul(p1r, p1i, p1r, p1i)
        p3r, p3i = _cmul(p2r, p2i, p1r, p1i)
        p4r, p4i = _cmul(p2r, p2i, p2r, p2i)
        p5r, p5i = _cmul(p4r, p4i, p1r, p1i)
        p6r, p6i = _cmul(p4r, p4i, p2r, p2i)
        p7r, p7i = _cmul(p4r, p4i, p3r, p3i)
        p8r, p8i = _cmul(p4r, p4i, p4r, p4i)
        pw_r = [p1r, p2r, p3r, p4r, p5r, p6r, p7r, p8r]
        pw_i = [p1i, p2i, p3i, p4i, p5i, p6i, p7i, p8i]
        row = lax.broadcasted_iota(jnp.int32, (8, S5_NS), 0)
        zero = jnp.zeros((8, S5_NS), F32)

        def bc(v):
            return jnp.broadcast_to(v, (8, S5_NS))

        for d in range(2):
            sgn = 1.0 if d == 0 else -1.0
            for t, s in enumerate((1, 2, 4)):
                live = (row >= s) if d == 0 else (row <= 7 - s)
                tab_ref[d, 2 * t] = jnp.where(live, bc(pw_r[s - 1]), zero)
                tab_ref[d, 2 * t + 1] = jnp.where(live, bc(sgn * pw_i[s - 1]), zero)
            cr, ci = zero, zero
            for i in range(8):
                e = i if d == 0 else 7 - i
                cr = jnp.where(row == i, bc(pw_r[e]), cr)
                ci = jnp.where(row == i, bc(sgn * pw_i[e]), ci)
            tab_ref[d, 6] = cr
            tab_ref[d, 7] = ci

        _, _, _, qr, qi, _ = _zoh_cols(lam_c_ref[:, 0:1], lam_c_ref[:, 1:2], ldt_c_ref[...])
        bm = _b_mask()
        br, bi = b_ref[0], b_ref[1]
        bset_ref[0] = jnp.where(bm, qr * br - qi * bi, 0.0).astype(BF16)
        bset_ref[1] = jnp.where(bm, qr * bi + qi * br, 0.0).astype(BF16)
        cm = _c_mask()
        cset_ref[0] = jnp.where(cm, c_ref[0], 0.0).astype(BF16)
        cset_ref[1] = jnp.where(cm, c_ref[1], 0.0).astype(BF16)

    vm = pl.BlockSpec(memory_space=pltpu.VMEM)
    return pl.pallas_call(
        body, name=name, in_specs=[vm] * 6, out_specs=[vm] * 3,
        out_shape=[_sds((2, 8, 8, S5_NS)), _sds((2, S5_NS, LANES), BF16), _sds((2, S5_W, 512), BF16)],
        compiler_params=pltpu.CompilerParams(vmem_limit_bytes=VMEM_LIMIT),
    )(lam_r, ldt_r, lam_c, ldt_c, b_t, c_t)


SCAN_W = SCAN_GROUPS * LANES


def _scan_chunk(src_ref, dst_ref, tab_ref, carry_ref, nb, reverse, xs_ref=None, acc_ref=None):
    row = lax.broadcasted_iota(jnp.int32, (8, LANES), 0)

    def step(i, carry):
        b = (nb - 1 - i) if reverse else i
        off = pl.multiple_of(b * 8, 8)
        out = []
        for g in range(SCAN_GROUPS):
            lanes = pl.ds(g * LANES, LANES)
            cr, ci = carry[2 * g], carry[2 * g + 1]
            yr = src_ref[0, pl.ds(off, 8), lanes]
            yi = src_ref[1, pl.ds(off, 8), lanes]
            for t, s in enumerate((1, 2, 4)):
                sh = (8 - s) if reverse else s
                sr = pltpu.roll(yr, sh, 0)
                si = pltpu.roll(yi, sh, 0)
                mr, mi = tab_ref[2 * t, :, lanes], tab_ref[2 * t + 1, :, lanes]
                yr, yi = yr + mr * sr - mi * si, yi + mr * si + mi * sr
            pr, pi = tab_ref[6, :, lanes], tab_ref[7, :, lanes]
            yr, yi = yr + pr * cr - pi * ci, yi + pr * ci + pi * cr
            dst_ref[0, pl.ds(off, 8), lanes] = yr
            dst_ref[1, pl.ds(off, 8), lanes] = yi
            if xs_ref is not None:
                nr = jnp.where(row == 7, cr, pltpu.roll(yr, 7, 0))
                ni = jnp.where(row == 7, ci, pltpu.roll(yi, 7, 0))
                xr = xs_ref[0, pl.ds(off, 8), lanes]
                xi = xs_ref[1, pl.ds(off, 8), lanes]
                acc_ref[0, :, lanes] += xr * nr + xi * ni
                acc_ref[1, :, lanes] += xr * ni - xi * nr
            last = 0 if reverse else 7
            out += [jnp.broadcast_to(yr[last:last + 1, :], (8, LANES)),
                    jnp.broadcast_to(yi[last:last + 1, :], (8, LANES))]
        return tuple(out)

    init = []
    for g in range(SCAN_GROUPS):
        init += [carry_ref[0, :, pl.ds(g * LANES, LANES)], carry_ref[1, :, pl.ds(g * LANES, LANES)]]
    fin = lax.fori_loop(0, nb, step, tuple(init))
    for g in range(SCAN_GROUPS):
        carry_ref[0, :, pl.ds(g * LANES, LANES)] = fin[2 * g]
        carry_ref[1, :, pl.ds(g * LANES, LANES)] = fin[2 * g + 1]


def _s5_scan_fwd(z, bset, cset, dvec, tabs, name):
    L = z.shape[0]
    tl = min(SCAN_CHUNK, L)
    nc = L // tl

    def body(u_ref, b_ref, c_ref, d_ref, tab_ref, x_ref, y_ref, carry_ref):
        @pl.when(pl.program_id(1) == 0)
        def _():
            carry_ref[...] = jnp.zeros_like(carry_ref)

        uf = u_ref[...]
        u = uf.astype(BF16)
        x_ref[0] = _dot(u, b_ref[0], NT)
        x_ref[1] = _dot(u, b_ref[1], NT)
        _scan_chunk(x_ref, x_ref, tab_ref, carry_ref, tl // 8, False)
        y_ref[...] = (_dot(x_ref[0].astype(BF16), c_ref[0], NT) - _dot(x_ref[1].astype(BF16), c_ref[1], NT)
                      + d_ref[...] * uf)

    col = pl.BlockSpec((tl, LANES), lambda j, c: (c, j))
    return pl.pallas_call(
        body, name=name, grid=(S5_NS // SCAN_W, nc),
        in_specs=[col, pl.BlockSpec((2, SCAN_W, LANES), lambda j, c: (0, j, 0)),
                  pl.BlockSpec((2, LANES, SCAN_W), lambda j, c: (0, j, 0)),
                  pl.BlockSpec((1, LANES), lambda j, c: (0, j)),
                  pl.BlockSpec((None, 8, 8, SCAN_W), lambda j, c: (0, 0, 0, j))],
        out_specs=[pl.BlockSpec((2, tl, SCAN_W), lambda j, c: (0, c, j)), col],
        out_shape=[_sds((2, L, S5_NS)), _sds((L, S5_W))],
        scratch_shapes=[pltpu.VMEM((2, 8, SCAN_W), F32)],
        compiler_params=_cp("parallel", "arbitrary"),
    )(z, bset, cset, dvec, tabs)


def _s5_scan_bwd(gyl, cset, xs, z, bset, gud, tabs, name):
    L = z.shape[0]
    tl = min(SCAN_CHUNK, L)
    nc = L // tl

    def body(g_ref, c_ref, xs_ref, u_ref, b_ref, gud_ref, tab_ref, gu_ref, ga_ref, gb_ref, gc_ref,
             gx_ref, carry_ref, acc_ref):
        c = pl.program_id(1)

        @pl.when(c == 0)
        def _():
            carry_ref[...] = jnp.zeros_like(carry_ref)
            acc_ref[...] = jnp.zeros_like(acc_ref)
            gb_ref[...] = jnp.zeros_like(gb_ref)
            gc_ref[...] = jnp.zeros_like(gc_ref)

        gy = g_ref[...].astype(BF16)
        gx_ref[0] = _dot(gy, c_ref[0])
        gx_ref[1] = -_dot(gy, c_ref[1])
        gc_ref[0] += _dot(gy, xs_ref[0].astype(BF16), TN)
        gc_ref[1] -= _dot(gy, xs_ref[1].astype(BF16), TN)
        _scan_chunk(gx_ref, gx_ref, tab_ref, carry_ref, tl // 8, True, xs_ref, acc_ref)
        gr = gx_ref[0].astype(BF16)
        gi = gx_ref[1].astype(BF16)
        gu_ref[...] = gud_ref[...] + _dot(gr, b_ref[0]) + _dot(gi, b_ref[1])
        u = u_ref[...].astype(BF16)
        gb_ref[0] += _dot(gr, u, TN)
        gb_ref[1] += _dot(gi, u, TN)

        @pl.when(c == nc - 1)
        def _():
            ga_ref[0:1, :] = jnp.sum(acc_ref[0], axis=0, keepdims=True)
            ga_ref[1:2, :] = jnp.sum(acc_ref[1], axis=0, keepdims=True)

    rev = lambda j, c: (nc - 1 - c, j)
    col = pl.BlockSpec((tl, LANES), rev)
    return pl.pallas_call(
        body, name=name, grid=(S5_NS // SCAN_W, nc),
        in_specs=[col, pl.BlockSpec((2, LANES, SCAN_W), lambda j, c: (0, j, 0)),
                  pl.BlockSpec((2, tl, SCAN_W), lambda j, c: (0, nc - 1 - c, j)), col,
                  pl.BlockSpec((2, SCAN_W, LANES), lambda j, c: (0, j, 0)), col,
                  pl.BlockSpec((None, 8, 8, SCAN_W), lambda j, c: (1, 0, 0, j))],
        out_specs=[col, pl.BlockSpec((2, SCAN_W), lambda j, c: (0, j)),
                   pl.BlockSpec((2, SCAN_W, LANES), lambda j, c: (0, j, 0)),
                   pl.BlockSpec((2, LANES, SCAN_W), lambda j, c: (0, j, 0))],
        out_shape=[_sds((L, S5_W)), _sds((2, S5_NS)), _sds((2, S5_NS, LANES)), _sds((2, S5_W, 512))],
        scratch_shapes=[pltpu.VMEM((2, tl, SCAN_W), F32), pltpu.VMEM((2, 8, SCAN_W), F32),
                        pltpu.VMEM((2, 8, SCAN_W), F32)],
        compiler_params=_cp("parallel", "arbitrary"),
    )(gyl, cset, xs, z, bset, gud, tabs)


def _s5_glu_fwd(ylin, wglu, name):
    L = ylin.shape[0]
    bl = min(1024, L)

    def body(ylin_ref, w_ref, ya_ref):
        yg = _gelu(ylin_ref[...])
        t = _dot(yg.astype(BF16), w_ref[...])
        ya_ref[...] = (yg * _sigmoid(t)).astype(BF16)

    return pl.pallas_call(
        body, name=name, grid=(L // bl,),
        in_specs=[pl.BlockSpec((bl, S5_W), lambda i: (i, 0)), pl.BlockSpec((S5_W, S5_W), lambda i: (0, 0))],
        out_specs=pl.BlockSpec((bl, S5_W), lambda i: (i, 0)),
        out_shape=_sds((L, S5_W), BF16),
        compiler_params=_cp("parallel"),
    )(ylin, wglu)


def _s5_glu_bwd(g_y, wout, ylin, z, dvec, wglu, name):
    L = z.shape[0]
    bl = min(256, L)

    def body(g_ref, wo_ref, ylin_ref, u_ref, d_ref, w_ref, gyl_ref, gud_ref, gw_ref, gd_ref):
        i = pl.program_id(0)
        ylin = ylin_ref[...]
        yg = _gelu(ylin)
        ygb = yg.astype(BF16)
        sg = _sigmoid(_dot(ygb, w_ref[...]))
        gya = _dot(g_ref[...], wo_ref[...], NT)
        gt = gya * yg * sg * (1.0 - sg)
        gtb = gt.astype(BF16)
        gyg = gya * sg + _dot(gtb, w_ref[...], NT)
        gyl = gyg * _gelu_grad(ylin)
        gyl_ref[...] = gyl
        gud_ref[...] = gyl * d_ref[...]

        @pl.when(i == 0)
        def _():
            gw_ref[...] = jnp.zeros_like(gw_ref)
            gd_ref[...] = jnp.zeros_like(gd_ref)

        gw_ref[...] += _dot(ygb, gtb, TN)
        gd_ref[...] += jnp.sum(gyl * u_ref[...], axis=0, keepdims=True)

    blk = pl.BlockSpec((bl, S5_W), lambda i: (i, 0))
    return pl.pallas_call(
        body, name=name, grid=(L // bl,),
        in_specs=[pl.BlockSpec((bl, D_MODEL), lambda i: (i, 0)), pl.BlockSpec((S5_W, D_MODEL), lambda i: (0, 0)),
                  blk, blk, pl.BlockSpec((1, S5_W), lambda i: (0, 0)), pl.BlockSpec((S5_W, S5_W), lambda i: (0, 0))],
        out_specs=[blk, blk, pl.BlockSpec((S5_W, S5_W), lambda i: (0, 0)), pl.BlockSpec((1, S5_W), lambda i: (0, 0))],
        out_shape=[_sds((L, S5_W)), _sds((L, S5_W)), _sds((S5_W, S5_W)), _sds((1, S5_W))],
        compiler_params=_cp("arbitrary"),
    )(g_y, wout, ylin, z, dvec, wglu)


def _s5_param_bwd(lam_c, ldt_c, b_t, gb, ga_c, gc, name):
    def body(lam_ref, ldt_ref, b_ref, gb_ref, ga_ref, gc_ref, glam_ref, gldt_ref, gbo_ref, gco_ref):
        lr, li = lam_ref[:, 0:1], lam_ref[:, 1:2]
        dt, ar, ai, qr, qi, den = _zoh_cols(lr, li, ldt_ref[...])
        bm = _b_mask()
        gbr = jnp.where(bm, gb_ref[0], 0.0)
        gbi = jnp.where(bm, gb_ref[1], 0.0)
        br, bi = b_ref[0], b_ref[1]
        obr = gbr * qr + gbi * qi
        obi = gbi * qr - gbr * qi
        gqr = jnp.sum(gbr * br + gbi * bi, axis=1, keepdims=True)
        gqi = jnp.sum(gbi * br - gbr * bi, axis=1, keepdims=True)
        for s in (64, 32, 16):
            obr = obr + pltpu.roll(obr, s, 1)
            obi = obi + pltpu.roll(obi, s, 1)
        gbo_ref[0] = obr
        gbo_ref[1] = obi
        gar = ga_ref[:, 0:1] + (gqr * lr - gqi * li) / den
        gai = ga_ref[:, 1:2] + (gqr * li + gqi * lr) / den
        qlr = (qr * lr + qi * li) / den
        qli = (qi * lr - qr * li) / den
        glr = -(gqr * qlr + gqi * qli)
        gli = -(gqi * qlr - gqr * qli)
        glr = glr + dt * (gar * ar + gai * ai)
        gli = gli + dt * (gai * ar - gar * ai)
        wr, wi = _cmul(lr, li, ar, ai)
        gldt = (gar * wr + gai * wi) * dt
        glam_ref[:, 0:1] = glr
        glam_ref[:, 1:2] = gli
        r = lax.broadcasted_iota(jnp.int32, (S5_NS, 32), 0)
        c = lax.broadcasted_iota(jnp.int32, (S5_NS, 32), 1)
        gldt_ref[...] = jnp.sum(jnp.where((r >> 6) == c, gldt, 0.0), axis=0, keepdims=True)
        cm = _c_mask()
        for k in range(2):
            oc = jnp.where(cm, gc_ref[k], 0.0)
            for s in (256, 128, 64):
                oc = oc + pltpu.roll(oc, s, 1)
            gco_ref[k] = oc[:, 0:LANES]

    vm = pl.BlockSpec(memory_space=pltpu.VMEM)
    return pl.pallas_call(
        body, name=name, in_specs=[vm] * 6, out_specs=[vm] * 4,
        out_shape=[_sds((S5_NS, 2)), _sds((1, 32)), _sds((2, S5_NS, LANES)), _sds((2, S5_W, LANES))],
        compiler_params=pltpu.CompilerParams(vmem_limit_bytes=VMEM_LIMIT),
    )(lam_c, ldt_c, b_t, gb, ga_c, gc)


FL_BLK = EVEN_PAD // LANES - 1
Q_BLK, K_BLK, V_BLK = 4, 8, 12
NEG = -1e30


def _log_sigmoid(v):
    return jnp.minimum(v, 0.0) - jnp.log(1.0 + jnp.exp(-jnp.abs(v)))


def _fox_f_fwd(z, bf, name):
    L = z.shape[0]

    def body(fl_ref, b_ref, f_ref, fq_ref):
        row = lax.broadcasted_iota(jnp.int32, (L, LANES), 0)
        cs = _cumsum_rows(_log_sigmoid(fl_ref[...] + b_ref[...]), True, row)
        f_ref[...] = cs
        expand = (lax.broadcasted_iota(jnp.int32, (LANES, FOX_W), 0)
                  == (lax.broadcasted_iota(jnp.int32, (LANES, FOX_W), 1) >> 6)).astype(F32)
        fq_ref[...] = lax.dot_general(cs, expand, NN, precision=lax.Precision.HIGHEST, preferred_element_type=F32)

    return pl.pallas_call(
        body, name=name, grid=(1,),
        in_specs=[pl.BlockSpec((L, LANES), lambda i: (0, FL_BLK)), pl.BlockSpec((1, LANES), lambda i: (0, 0))],
        out_specs=[pl.BlockSpec((L, LANES), lambda i: (0, 0)), pl.BlockSpec((L, FOX_W), lambda i: (0, 0))],
        out_shape=[_sds((L, LANES)), _sds((L, FOX_W))],
        compiler_params=_cp("arbitrary"),
    )(z, bf)


def _fox_f_bwd(dFk, dfq, z, bf, name):
    L = z.shape[0]

    def body(dfk_ref, dfq_ref, fl_ref, b_ref, dfl_ref, db_ref):
        sel = (lax.broadcasted_iota(jnp.int32, (FOX_W, LANES), 0)
               == 64 * lax.broadcasted_iota(jnp.int32, (FOX_W, LANES), 1)).astype(F32)
        dfq_h = lax.dot_general(dfq_ref[...], sel, NN, precision=lax.Precision.HIGHEST, preferred_element_type=F32)
        row = lax.broadcasted_iota(jnp.int32, (L, LANES), 0)
        cs = _cumsum_rows(dfk_ref[...] + dfq_h, False, row)
        dfl = cs * _sigmoid(-(fl_ref[...] + b_ref[...]))
        dfl_ref[...] = dfl
        db_ref[...] = jnp.sum(dfl, axis=0, keepdims=True)

    return pl.pallas_call(
        body, name=name, grid=(1,),
        in_specs=[pl.BlockSpec((L, LANES), lambda i: (0, 0)), pl.BlockSpec((L, FOX_W), lambda i: (0, 0)),
                  pl.BlockSpec((L, LANES), lambda i: (0, FL_BLK)), pl.BlockSpec((1, LANES), lambda i: (0, 0))],
        out_specs=[pl.BlockSpec((L, LANES), lambda i: (0, 0)), pl.BlockSpec((1, LANES), lambda i: (0, 0))],
        out_shape=[_sds((L, LANES)), _sds((1, LANES))],
        compiler_params=_cp("arbitrary"),
    )(dFk, dfq, z, bf)


def _head_mask(hh):
    lane = lax.broadcasted_iota(jnp.int32, (1, LANES), 1)
    return (lane >> 6) == hh


FOX_T = 512


def _fox_head(x, hh):
    return jnp.where(_head_mask(hh), x, 0.0).astype(BF16)


def _fox_scores(qh, k, fq_ref, fr_ref, hh, causal):
    if fq_ref is None:
        s = _dot(qh, k, NT) - fr_ref[hh:hh + 1, :]
    else:
        s = _dot(qh, k, NT) + (fq_ref[:, 64 * hh:64 * hh + 1] - fr_ref[hh:hh + 1, :])
    return s if causal is None else jnp.where(causal, s, NEG)


def _causal(T):
    return lax.broadcasted_iota(jnp.int32, (T, T), 1) <= lax.broadcasted_iota(jnp.int32, (T, T), 0)


def _fox_fwd(z, fq, frow, name):
    L = z.shape[0]
    T = min(FOX_T, L)
    nq = L // T

    def body(qt_ref, kt_ref, q_ref, k_ref, v_ref, fq_ref, fr_ref, o_ref, lse_ref, m_ref, l_ref, acc_ref):
        t = pl.program_id(1)
        qi, ki = qt_ref[t], kt_ref[t]

        @pl.when(ki == 0)
        def _():
            m_ref[...] = jnp.full_like(m_ref, NEG)
            l_ref[...] = jnp.zeros_like(l_ref)
            acc_ref[...] = jnp.zeros_like(acc_ref)

        def step(diagonal):
            q = q_ref[...] * 0.125
            k = k_ref[...].astype(BF16)
            v = v_ref[...].astype(BF16)
            causal = _causal(T) if diagonal else None
            s = jnp.concatenate([_fox_scores(_fox_head(q, hh), k, fq_ref, fr_ref, hh, causal) for hh in range(2)],
                                axis=0)
            m_old = m_ref[...]
            m_new = jnp.maximum(m_old, jnp.max(s, axis=1, keepdims=True))
            alpha = jnp.exp(m_old - m_new)
            p = jnp.exp(s - m_new)
            l_ref[...] = alpha * l_ref[...] + jnp.sum(p, axis=1, keepdims=True)
            m_ref[...] = m_new
            acc_ref[...] = alpha * acc_ref[...] + _dot(p.astype(BF16), v)

        @pl.when(ki < qi)
        def _():
            step(False)

        @pl.when(ki == qi)
        def _():
            step(True)
            h0 = _head_mask(0)
            l = l_ref[...]
            o_h = acc_ref[...] / l
            lse_h = m_ref[...] + jnp.log(l)
            o_ref[...] = jnp.where(h0, o_h[:T], o_h[T:])
            lse_ref[...] = jnp.where(h0, lse_h[:T], lse_h[T:]) - fq_ref[...]

    pairs = [(qi, ki) for qi in range(nq) for ki in range(qi + 1)]
    qt = jnp.asarray([p[0] for p in pairs], jnp.int32)
    kt = jnp.asarray([p[1] for p in pairs], jnp.int32)

    def qspec(base):
        return pl.BlockSpec((T, LANES), lambda j, t, qt, kt: (qt[t], base + j))

    def kspec(base):
        return pl.BlockSpec((T, LANES), lambda j, t, qt, kt: (kt[t], base + j))

    return pl.pallas_call(
        body, name=name,
        grid_spec=pltpu.PrefetchScalarGridSpec(
            num_scalar_prefetch=2, grid=(4, len(pairs)),
            in_specs=[qspec(Q_BLK), kspec(K_BLK), kspec(V_BLK), qspec(0),
                      pl.BlockSpec((None, 2, T), lambda j, t, qt, kt: (j, 0, kt[t]))],
            out_specs=[qspec(0), qspec(0)],
            scratch_shapes=[pltpu.VMEM((2 * T, 1), F32), pltpu.VMEM((2 * T, 1), F32),
                            pltpu.VMEM((2 * T, LANES), F32)]),
        out_shape=[_sds((L, FOX_W)), _sds((L, FOX_W))],
        compiler_params=_cp("parallel", "arbitrary"),
    )(qt, kt, z, z, z, fq, frow)


def _fox_bwd(z, frow, o, lse, g_m, name):
    L = z.shape[0]
    T = min(FOX_T, L)
    nq = L // T

    pairs = [(qi, ki) for ki in range(nq) for qi in range(ki, nq)]
    qt = jnp.asarray([p[0] for p in pairs], jnp.int32)
    kt = jnp.asarray([p[1] for p in pairs], jnp.int32)

    def body(qt_ref, kt_ref, q_ref, k_ref, v_ref, fr_ref, o_ref, lse_ref, do_ref,
             dq_ref, dk_ref, dv_ref, dfq_ref, dfk_ref, dk_acc, dv_acc, df_acc):
        t = pl.program_id(1)
        qi, ki = qt_ref[t], kt_ref[t]

        @pl.when(t == 0)
        def _():
            dq_ref[...] = jnp.zeros_like(dq_ref)
            dfq_ref[...] = jnp.zeros_like(dfq_ref)

        @pl.when(qi == ki)
        def _():
            dk_acc[...] = jnp.zeros_like(dk_acc)
            dv_acc[...] = jnp.zeros_like(dv_acc)
            df_acc[...] = jnp.zeros_like(df_acc)

        def step(diagonal):
            q = q_ref[...] * 0.125
            qb = q.astype(BF16)
            k = k_ref[...].astype(BF16)
            v = v_ref[...].astype(BF16)
            do = do_ref[...]
            dob = do.astype(BF16)
            do_o = dob.astype(F32) * o_ref[...]
            causal = _causal(T) if diagonal else None
            dvs, dks, dqs, rss = [], [], [], []
            for hh in range(2):
                s = _fox_scores(_fox_head(q, hh), k, None, fr_ref, hh, causal)
                p = jnp.exp(s - lse_ref[:, 64 * hh:64 * hh + 1])
                dp = _dot(_fox_head(do, hh), v, NT)
                delta = jnp.sum(jnp.where(_head_mask(hh), do_o, 0.0), axis=1, keepdims=True)
                ds = p * (dp - delta)
                dsb = ds.astype(BF16)
                dvs.append(_dot(p.astype(BF16), dob, TN))
                dks.append(_dot(dsb, qb, TN))
                dqs.append(_dot(dsb, k))
                rss.append(jnp.sum(ds, axis=1, keepdims=True))
                df_acc[hh:hh + 1, :] -= jnp.sum(ds, axis=0, keepdims=True)
            h0 = _head_mask(0)
            dv_acc[...] += jnp.where(h0, dvs[0], dvs[1])
            dk_acc[...] += jnp.where(h0, dks[0], dks[1])
            rows = pl.ds(pl.multiple_of(qi * T, T), T)
            dq_ref[rows, :] += jnp.where(h0, dqs[0], dqs[1])
            dfq_ref[rows, :] += jnp.where(h0, rss[0], rss[1])

        @pl.when(qi > ki)
        def _():
            step(False)

        @pl.when(qi == ki)
        def _():
            step(True)

        @pl.when(qi == nq - 1)
        def _():
            dk_ref[...] = dk_acc[...]
            dv_ref[...] = dv_acc[...]
            dfk_ref[...] = df_acc[...]

        @pl.when(t == len(pairs) - 1)
        def _():
            dq_ref[...] = dq_ref[...] * 0.125

    def qside(base):
        return pl.BlockSpec((T, LANES), lambda j, t, qt, kt: (qt[t], base + j))

    def kside(base):
        return pl.BlockSpec((T, LANES), lambda j, t, qt, kt: (kt[t], base + j))

    pair = pl.BlockSpec((L, LANES), lambda j, t, qt, kt: (0, j))
    frow_spec = pl.BlockSpec((None, 2, T), lambda j, t, qt, kt: (j, 0, kt[t]))
    return pl.pallas_call(
        body, name=name,
        grid_spec=pltpu.PrefetchScalarGridSpec(
            num_scalar_prefetch=2, grid=(4, len(pairs)),
            in_specs=[qside(Q_BLK), kside(K_BLK), kside(V_BLK), frow_spec, qside(0), qside(0), qside(0)],
            out_specs=[pair, kside(0), kside(0), pair, frow_spec],
            scratch_shapes=[pltpu.VMEM((T, LANES), F32), pltpu.VMEM((T, LANES), F32), pltpu.VMEM((2, T), F32)]),
        out_shape=[_sds((L, FOX_W)), _sds((L, FOX_W)), _sds((L, FOX_W)), _sds((L, FOX_W)), _sds((4, 2, L))],
        compiler_params=_cp("parallel", "arbitrary"),
    )(qt, kt, z, z, z, frow, o, lse, g_m)


def _shift_rows(v, s, down, row):
    n = v.shape[0]
    if down:
        return jnp.where(row >= s, pltpu.roll(v, s, 0), 0.0)
    return jnp.where(row < n - s, pltpu.roll(v, n - s, 0), 0.0)


def _cumsum_rows(v, down, row):
    s = 1
    while s < v.shape[0]:
        v = v + _shift_rows(v, s, down, row)
        s *= 2
    return v


def _window_sum(v, g, down, row):
    out = jnp.zeros_like(v)
    s = v
    for k in range(4):
        s = s + _shift_rows(s, 1 << k, down, row)
        out = jnp.where(g == k, s, out)
    return out


def _pool_inv_cnt(g, row):
    w = jnp.left_shift(2, g).astype(F32)
    return 1.0 / jnp.minimum(row.astype(F32) + 1.0, w)


def _pool_fwd(z, pool_w, scale, name):
    L = z.shape[0]

    def body(x_ref, w_ref, s_ref, y_ref, p_ref):
        g = pl.program_id(0)
        row = lax.broadcasted_iota(jnp.int32, (L, LANES), 0)
        x = x_ref[...]
        pooled = (_window_sum(x, g, True, row) * _pool_inv_cnt(g, row) - x).astype(BF16)
        p_ref[...] = pooled
        y_ref[...] = (_dot(pooled, w_ref[...].astype(BF16)) * s_ref[...]).astype(BF16)

    col = pl.BlockSpec((L, LANES), lambda g: (0, g))
    return pl.pallas_call(
        body, name=name, grid=(4,),
        in_specs=[col, pl.BlockSpec((None, LANES, LANES), lambda g: (g, 0, 0)), pl.BlockSpec((1, LANES), lambda g: (0, g))],
        out_specs=[col, col],
        out_shape=[_sds((L, 512), BF16), _sds((L, 512), BF16)],
        compiler_params=_cp("parallel"),
    )(z, pool_w, scale)


def _pool_bwd(g_y, wout, pooled, pool_w, scale, name):
    L = g_y.shape[0]

    def body(g_ref, wo_ref, p_ref, w_ref, s_ref, gx_ref, gw_ref, gs_ref):
        g = pl.program_id(0)
        row = lax.broadcasted_iota(jnp.int32, (L, LANES), 0)
        gy = _dot(g_ref[...], wo_ref[...], NT)
        pooled = p_ref[...]
        wb = w_ref[...].astype(BF16)
        lin = _dot(pooled, wb)
        gs_ref[...] = jnp.sum(gy * lin, axis=0, keepdims=True)
        glin = (gy * s_ref[...]).astype(BF16)
        gw_ref[...] = _dot(pooled, glin, TN)
        gp = _dot(glin, wb, NT)
        gx_ref[...] = _window_sum(gp * _pool_inv_cnt(g, row), g, False, row) - gp

    col = pl.BlockSpec((L, LANES), lambda g: (0, g))
    wspec = pl.BlockSpec((None, LANES, LANES), lambda g: (g, 0, 0))
    vec = pl.BlockSpec((1, LANES), lambda g: (0, g))
    return pl.pallas_call(
        body, name=name, grid=(4,),
        in_specs=[pl.BlockSpec((L, D_MODEL), lambda g: (0, 0)), pl.BlockSpec((LANES, D_MODEL), lambda g: (g, 0)),
                  col, wspec, vec],
        out_specs=[col, wspec, vec],
        out_shape=[_sds((L, 512)), _sds((4, LANES, LANES)), _sds((1, 512))],
        compiler_params=_cp("parallel"),
    )(g_y, wout, pooled, pool_w, scale)


SGU_CHUNKS = 4


def _sgu_ln(v, gam, bet):
    gv = _gelu(v)
    mu = jnp.mean(gv, axis=-1, keepdims=True)
    xc = gv - mu
    rs = lax.rsqrt(jnp.mean(xc * xc, axis=-1, keepdims=True) + EPS)
    xh = xc * rs
    return xh, rs, xh * gam + bet


def _tril_ws(w_ref, g):
    r = lax.broadcasted_iota(jnp.int32, (LANES, LANES), 0)
    c = lax.broadcasted_iota(jnp.int32, (LANES, LANES), 1)
    return jnp.where(r >= c, w_ref[g], 0.0).astype(BF16)


def _sgu_fwd(z, ln_g, ln_b, w_s, b_st, name):
    L = z.shape[0]
    rb = min(SGU_CHUNKS * LANES, L)

    def body(u_ref, v_ref, g_ref, b_ref, w_ref, bs_ref, y_ref):
        _, _, vln = _sgu_ln(v_ref[...], g_ref[...], b_ref[...])
        gu = _gelu(u_ref[...])
        vb = vln.astype(BF16)
        for g in range(4):
            ws = _tril_ws(w_ref, g)
            for n in range(rb // LANES):
                rows = slice(n * LANES, (n + 1) * LANES)
                cols = slice(g * LANES, (g + 1) * LANES)
                mixed = _dot(ws, vb[rows, cols]) + bs_ref[:, g:g + 1]
                y_ref[rows, cols] = (gu[rows, cols] * mixed).astype(BF16)

    vm = lambda shape: pl.BlockSpec(shape, lambda i: tuple(0 for _ in shape))
    return pl.pallas_call(
        body, name=name, grid=(L // rb,),
        in_specs=[pl.BlockSpec((rb, 512), lambda i: (i, 1)), pl.BlockSpec((rb, 512), lambda i: (i, 2)),
                  vm((1, 512)), vm((1, 512)), vm((4, LANES, LANES)), vm((LANES, 4))],
        out_specs=pl.BlockSpec((rb, 512), lambda i: (i, 0)),
        out_shape=_sds((L, 512), BF16),
        compiler_params=_cp("parallel"),
    )(z, z, ln_g, ln_b, w_s, b_st)


def _sgu_bwd(g_y, wout, z, ln_g, ln_b, w_s, b_st, name):
    L = z.shape[0]
    rb = min(SGU_CHUNKS * LANES, L)

    def body(gyo_ref, wo_ref, u_ref, v_ref, g_ref, b_ref, w_ref, bs_ref, gu_ref, gv_ref, gw_ref, gbs_ref, gg_ref,
             gb_ref):
        i = pl.program_id(0)

        @pl.when(i == 0)
        def _():
            gw_ref[...] = jnp.zeros_like(gw_ref)
            gbs_ref[...] = jnp.zeros_like(gbs_ref)
            gg_ref[...] = jnp.zeros_like(gg_ref)
            gb_ref[...] = jnp.zeros_like(gb_ref)

        v = v_ref[...]
        u = u_ref[...]
        gy = _dot(gyo_ref[...], wo_ref[...], NT)
        xh, rs, vln = _sgu_ln(v, g_ref[...], b_ref[...])
        gel_u = _gelu(u)
        gmix = gy * gel_u
        vb = vln.astype(BF16)
        gmb = gmix.astype(BF16)
        r = lax.broadcasted_iota(jnp.int32, (LANES, LANES), 0)
        c = lax.broadcasted_iota(jnp.int32, (LANES, LANES), 1)
        gvln_cols = []
        for g in range(4):
            ws = _tril_ws(w_ref, g)
            cols = slice(g * LANES, (g + 1) * LANES)
            gw = jnp.zeros((LANES, LANES), F32)
            gbs = jnp.zeros((LANES, 1), F32)
            parts = []
            for n in range(rb // LANES):
                rows = slice(n * LANES, (n + 1) * LANES)
                mixed = _dot(ws, vb[rows, cols]) + bs_ref[:, g:g + 1]
                gu_ref[rows, cols] = gy[rows, cols] * mixed * _gelu_grad(u[rows, cols])
                parts.append(_dot(ws, gmb[rows, cols], TN))
                gw = gw + _dot(gmb[rows, cols], vb[rows, cols], NT)
                gbs = gbs + jnp.sum(gmix[rows, cols], axis=1, keepdims=True)
            gvln_cols.append(jnp.concatenate(parts, axis=0))
            gw_ref[g] += jnp.where(r >= c, gw, 0.0)
            gbs_ref[:, g:g + 1] += gbs
        gvln = jnp.concatenate(gvln_cols, axis=1)
        gg_ref[...] += jnp.sum(gvln * xh, axis=0, keepdims=True)
        gb_ref[...] += jnp.sum(gvln, axis=0, keepdims=True)
        gxh = gvln * g_ref[...]
        ggv = rs * (gxh - jnp.mean(gxh, axis=-1, keepdims=True) - xh * jnp.mean(gxh * xh, axis=-1, keepdims=True))
        gv_ref[...] = ggv * _gelu_grad(v)

    vm = lambda shape: pl.BlockSpec(shape, lambda i: tuple(0 for _ in shape))
    blk = pl.BlockSpec((rb, 512), lambda i: (i, 0))
    return pl.pallas_call(
        body, name=name, grid=(L // rb,),
        in_specs=[pl.BlockSpec((rb, D_MODEL), lambda i: (i, 0)), pl.BlockSpec((512, D_MODEL), lambda i: (1, 0)),
                  pl.BlockSpec((rb, 512), lambda i: (i, 1)), pl.BlockSpec((rb, 512), lambda i: (i, 2)),
                  vm((1, 512)), vm((1, 512)), vm((4, LANES, LANES)), vm((LANES, 4))],
        out_specs=[blk, blk, vm((4, LANES, LANES)), vm((LANES, 4)), vm((1, 512)), vm((1, 512))],
        out_shape=[_sds((L, 512)), _sds((L, 512)), _sds((4, LANES, LANES)), _sds((LANES, 4)),
                   _sds((1, 512)), _sds((1, 512))],
        compiler_params=_cp("arbitrary"),
    )(g_y, wout, z, z, ln_g, ln_b, w_s, b_st)


def _adamw_math(w, g, m, v):
    nm = ADAM_B1 * m + (1.0 - ADAM_B1) * g
    nv = ADAM_B2 * v + (1.0 - ADAM_B2) * (g * g)
    m_hat = nm / (1.0 - ADAM_B1 ** ADAM_STEP)
    v_hat = nv / (1.0 - ADAM_B2 ** ADAM_STEP)
    delta = -ADAM_LR * (m_hat / (jnp.sqrt(v_hat) + ADAM_EPS) + ADAM_WD * w)
    return delta, nm, nv


def _sum_adamw(parts, w, m, v, name, layer=0, prev=None):
    n_layers, R, C = w.shape
    rb = 128 if R % 128 == 0 else R

    def body(p_ref, w_ref, m_ref, v_ref, *rest):
        g_ref, d_ref, nm_ref, nv_ref = rest[-4:]
        g = p_ref[0].astype(F32)
        for s in range(1, N_DEV):
            g = g + p_ref[s].astype(F32)
        d, nm, nv = _adamw_math(w_ref[...], g, m_ref[...], v_ref[...])
        g_ref[...] = g
        d_ref[...] = d
        nm_ref[...] = nm
        nv_ref[...] = nv

    blk = pl.BlockSpec((None, rb, C), lambda i: (layer, i, 0))
    prev = [] if prev is None else list(prev)
    return pl.pallas_call(
        body, name=name, grid=(R // rb,),
        in_specs=[pl.BlockSpec((N_DEV, rb, C), lambda i: (0, i, 0)), blk, blk, blk] + [ANY] * len(prev),
        out_specs=[blk] * 4, out_shape=[_sds((n_layers, R, C))] * 4,
        input_output_aliases={4 + k: k for k in range(len(prev))},
        compiler_params=_cp("parallel"),
    )(parts, w, m, v, *prev)


def _sum_adamw_rows(parts, w, m, v, name):
    R, _, C = w.shape

    def body(p_ref, w_ref, m_ref, v_ref, g_ref, d_ref, nm_ref, nv_ref):
        g = p_ref[0].astype(F32)
        for s in range(1, N_DEV):
            g = g + p_ref[s].astype(F32)
        d, nm, nv = _adamw_math(w_ref[:, 0, :], g, m_ref[:, 0, :], v_ref[:, 0, :])
        g_ref[:, 0, :] = g
        d_ref[:, 0, :] = d
        nm_ref[:, 0, :] = nm
        nv_ref[:, 0, :] = nv

    vm = pl.BlockSpec(memory_space=pltpu.VMEM)
    return pl.pallas_call(body, name=name, in_specs=[vm] * 4, out_specs=[vm] * 4, out_shape=[_sds((R, 1, C))] * 4,
                          compiler_params=pltpu.CompilerParams(vmem_limit_bytes=VMEM_LIMIT))(parts, w, m, v)


def _sum_pieces(parts, name):
    _, R, C = parts.shape

    def body(p_ref, g_ref):
        g = p_ref[0].astype(F32)
        for s in range(1, N_DEV):
            g = g + p_ref[s].astype(F32)
        g_ref[...] = g

    vm = pl.BlockSpec(memory_space=pltpu.VMEM)
    return pl.pallas_call(body, name=name, in_specs=[vm], out_specs=vm, out_shape=_sds((R, C)),
                          compiler_params=pltpu.CompilerParams(vmem_limit_bytes=VMEM_LIMIT))(parts)


def _adamw_many(ws, gs, ms, vs, name):
    n = len(ws)
    vm = pl.BlockSpec(memory_space=pltpu.VMEM)

    def body(*refs):
        w_refs, g_refs, m_refs, v_refs = refs[:n], refs[n:2 * n], refs[2 * n:3 * n], refs[3 * n:4 * n]
        d_refs, nm_refs, nv_refs = refs[4 * n:5 * n], refs[5 * n:6 * n], refs[6 * n:7 * n]
        for i in range(n):
            d, nm, nv = _adamw_math(w_refs[i][...], g_refs[i][...], m_refs[i][...], v_refs[i][...])
            d_refs[i][...] = d
            nm_refs[i][...] = nm
            nv_refs[i][...] = nv

    shapes = [_sds(w.shape) for w in ws]
    outs = pl.pallas_call(
        body, name=name, in_specs=[vm] * (4 * n), out_specs=[vm] * (3 * n), out_shape=shapes * 3,
        compiler_params=pltpu.CompilerParams(vmem_limit_bytes=VMEM_LIMIT),
    )(*ws, *gs, *ms, *vs)
    return list(outs[:n]), list(outs[n:2 * n]), list(outs[2 * n:])


def _mesh_pos():
    return lax.axis_index("x"), lax.axis_index("y"), lax.axis_index("c")


def _dev_index(p):
    return 4 * p[0] + 2 * p[1] + p[2]


HBM = pl.BlockSpec(memory_space=pltpu.HBM)
SEM = pl.BlockSpec(memory_space=pltpu.SEMAPHORE)
EFFECT = pltpu.SideEffectType.DATAFLOW_SIDE_EFFECTING


def _peer_list():
    x, y, c = _mesh_pos()
    peers = [(x ^ dx, y ^ dy, c ^ dc) for dx in range(2) for dy in range(2) for dc in range(2)][1:]
    return (x, y, c), peers


def _split_copy(src_ref, land_ref, send_sems, recv_sems, i, k, peer, slot, exchange):
    return pltpu.make_async_remote_copy(
        src_ref=src_ref.at[_dev_index(peer)] if exchange else src_ref, dst_ref=land_ref.at[slot],
        send_sem=send_sems.at[7 * i + k], recv_sem=recv_sems.at[7 * i + k], device_id=peer, device_id_type=MESH)


def _own_copy(src_ref, land_ref, own_sems, i, slot, exchange):
    return pltpu.make_async_copy(src_ref.at[slot] if exchange else src_ref, land_ref.at[slot], own_sems.at[i])


def _comm_start(groups, name, exchange, dep=None):
    sizes = [len(g) for g in groups]
    n = sum(sizes)
    srcs = [a for g in groups for a in g]
    per_group = [exchange] * len(groups) if isinstance(exchange, bool) else list(exchange)
    exchanged = [flag for flag, sz in zip(per_group, sizes) for _ in range(sz)]
    lands = [lax.empty(a.shape if ex else (N_DEV,) + a.shape, a.dtype) for a, ex in zip(srcs, exchanged)]

    n_dep = 0 if dep is None else 1

    def body(*refs):
        src_refs, land_refs = refs[:n], refs[n:2 * n]
        sem_refs = refs[2 * n + n_dep:2 * n + n_dep + 3 * len(sizes)]
        token_ref = refs[-1]
        me, peers = _peer_list()
        mi = _dev_index(me)
        i = 0
        for gi, sz in enumerate(sizes):
            for j in range(sz):
                for k, peer in enumerate(peers):
                    _split_copy(src_refs[i], land_refs[i], sem_refs[3 * gi], sem_refs[3 * gi + 1], j, k, peer, mi,
                                exchanged[i]).start()
                _own_copy(src_refs[i], land_refs[i], sem_refs[3 * gi + 2], j, mi, exchanged[i]).start()
                i += 1
        token_ref[...] = jnp.zeros_like(token_ref)

    sem_shapes = []
    for sz in sizes:
        sem_shapes += [pltpu.SemaphoreType.DMA((7 * sz,)), pltpu.SemaphoreType.DMA((7 * sz,)),
                       pltpu.SemaphoreType.DMA((sz,))]
    thru = [pltpu.HBM(a.shape, a.dtype) for a in srcs + lands]
    n_sem = len(sem_shapes)
    outs = pl.pallas_call(
        body, name=name,
        out_shape=tuple(sem_shapes + thru + [_sds((8, LANES))]),
        in_specs=[HBM] * (2 * n) + [ANY] * n_dep,
        out_specs=tuple([SEM] * n_sem + [HBM] * (2 * n) + [pl.BlockSpec(memory_space=pltpu.VMEM)]),
        input_output_aliases={i: n_sem + i for i in range(2 * n)},
        compiler_params=pltpu.CompilerParams(has_side_effects=EFFECT),
    )(*[pltpu.with_memory_space_constraint(a, pltpu.HBM) for a in srcs + lands], *([] if dep is None else [dep]))
    sems, thru_src, thru_land, token = outs[:n_sem], outs[n_sem:n_sem + n], outs[n_sem + n:n_sem + 2 * n], outs[-1]
    result, off = [], 0
    for gi, sz in enumerate(sizes):
        result.append((*sems[3 * gi:3 * gi + 3], list(thru_src[off:off + sz]), list(thru_land[off:off + sz])))
        off += sz
    return result, token


def _comm_wait(group, after, name, exchange):
    send_sems, recv_sems, own_sems, srcs, lands = group
    n = len(srcs)
    after = list(after) if isinstance(after, (list, tuple)) else [after]

    def body(*refs):
        src_refs, land_refs = refs[:n], refs[n:2 * n]
        ssem, rsem, osem = refs[2 * n:2 * n + 3]
        me, peers = _peer_list()
        for i in range(n):
            for k, peer in enumerate(peers):
                cp = _split_copy(src_refs[i], land_refs[i], ssem, rsem, i, k, peer, _dev_index(peer), exchange)
                cp.wait_send()
                cp.wait_recv()
            _own_copy(src_refs[i], land_refs[i], osem, i, _dev_index(me), exchange).wait()

    outs = pl.pallas_call(
        body, name=name,
        out_shape=tuple(pltpu.HBM(a.shape, a.dtype) for a in srcs + lands),
        in_specs=[HBM] * (2 * n) + [SEM, SEM, SEM] + [ANY] * len(after),
        out_specs=tuple([HBM] * (2 * n)),
        input_output_aliases={i: i for i in range(2 * n)},
        compiler_params=pltpu.CompilerParams(has_side_effects=EFFECT),
    )(*srcs, *lands, send_sems, recv_sems, own_sems, *after)
    return list(outs[n:])


def _tie(a, token):
    return a + token[0, 0].astype(a.dtype)


def _pack(arrs, rows):
    flat = jnp.concatenate([a.reshape(-1).astype(F32) for a in arrs])
    return jnp.pad(flat, (0, rows * LANES - flat.shape[0])).reshape(rows, LANES)


def _unpack(packed, shapes):
    flat = packed.reshape(-1)
    out, off = [], 0
    for s in shapes:
        n = math.prod(s)
        out.append(flat[off:off + n].reshape(s))
        off += n
    return out


def _packed_rows(shapes):
    n = sum(math.prod(s) for s in shapes)
    unit = N_DEV * 8 * LANES
    return -(-n // unit) * unit // LANES


def kernel(x, mix_pre_g, mix_post_g, mlp_pre_g, mlp_post_g, w_in_even, s5_lam_re, s5_lam_im, s5_log_dt, s5_b_re, s5_b_im, s5_c_re, s5_c_im, s5_d, s5_w_glu, fox_b_f, w_out_even, w_in_odd, pool_w, pool_scale, sgu_ln_g, sgu_ln_b, sgu_w_s, sgu_b_s, w_out_odd, mlp_w1, mlp_w2, loss_target, m_mix_pre_g, m_mix_post_g, m_mlp_pre_g, m_mlp_post_g, m_w_in_even, m_s5_lam_re, m_s5_lam_im, m_s5_log_dt, m_s5_b_re, m_s5_b_im, m_s5_c_re, m_s5_c_im, m_s5_d, m_s5_w_glu, m_fox_b_f, m_w_out_even, m_w_in_odd, m_pool_w, m_pool_scale, m_sgu_ln_g, m_sgu_ln_b, m_sgu_w_s, m_sgu_b_s, m_w_out_odd, m_mlp_w1, m_mlp_w2, v_mix_pre_g, v_mix_post_g, v_mlp_pre_g, v_mlp_post_g, v_w_in_even, v_s5_lam_re, v_s5_lam_im, v_s5_log_dt, v_s5_b_re, v_s5_b_im, v_s5_c_re, v_s5_c_im, v_s5_d, v_s5_w_glu, v_fox_b_f, v_w_out_even, v_w_in_odd, v_pool_w, v_pool_scale, v_sgu_ln_g, v_sgu_ln_b, v_sgu_w_s, v_sgu_b_s, v_w_out_odd, v_mlp_w1, v_mlp_w2):
    weights = dict(mix_pre_g=mix_pre_g, mix_post_g=mix_post_g, mlp_pre_g=mlp_pre_g, mlp_post_g=mlp_post_g, w_in_even=w_in_even, s5_lam_re=s5_lam_re, s5_lam_im=s5_lam_im, s5_log_dt=s5_log_dt, s5_b_re=s5_b_re, s5_b_im=s5_b_im, s5_c_re=s5_c_re, s5_c_im=s5_c_im, s5_d=s5_d, s5_w_glu=s5_w_glu, fox_b_f=fox_b_f, w_out_even=w_out_even, w_in_odd=w_in_odd, pool_w=pool_w, pool_scale=pool_scale, sgu_ln_g=sgu_ln_g, sgu_ln_b=sgu_ln_b, sgu_w_s=sgu_w_s, sgu_b_s=sgu_b_s, w_out_odd=w_out_odd, mlp_w1=mlp_w1, mlp_w2=mlp_w2)
    mom_m = dict(mix_pre_g=m_mix_pre_g, mix_post_g=m_mix_post_g, mlp_pre_g=m_mlp_pre_g, mlp_post_g=m_mlp_post_g, w_in_even=m_w_in_even, s5_lam_re=m_s5_lam_re, s5_lam_im=m_s5_lam_im, s5_log_dt=m_s5_log_dt, s5_b_re=m_s5_b_re, s5_b_im=m_s5_b_im, s5_c_re=m_s5_c_re, s5_c_im=m_s5_c_im, s5_d=m_s5_d, s5_w_glu=m_s5_w_glu, fox_b_f=m_fox_b_f, w_out_even=m_w_out_even, w_in_odd=m_w_in_odd, pool_w=m_pool_w, pool_scale=m_pool_scale, sgu_ln_g=m_sgu_ln_g, sgu_ln_b=m_sgu_ln_b, sgu_w_s=m_sgu_w_s, sgu_b_s=m_sgu_b_s, w_out_odd=m_w_out_odd, mlp_w1=m_mlp_w1, mlp_w2=m_mlp_w2)
    mom_v = dict(mix_pre_g=v_mix_pre_g, mix_post_g=v_mix_post_g, mlp_pre_g=v_mlp_pre_g, mlp_post_g=v_mlp_post_g, w_in_even=v_w_in_even, s5_lam_re=v_s5_lam_re, s5_lam_im=v_s5_lam_im, s5_log_dt=v_s5_log_dt, s5_b_re=v_s5_b_re, s5_b_im=v_s5_b_im, s5_c_re=v_s5_c_re, s5_c_im=v_s5_c_im, s5_d=v_s5_d, s5_w_glu=v_s5_w_glu, fox_b_f=v_fox_b_f, w_out_even=v_w_out_even, w_in_odd=v_w_in_odd, pool_w=v_pool_w, pool_scale=v_pool_scale, sgu_ln_g=v_sgu_ln_g, sgu_ln_b=v_sgu_ln_b, sgu_w_s=v_sgu_w_s, sgu_b_s=v_sgu_b_s, w_out_odd=v_w_out_odd, mlp_w1=v_mlp_w1, mlp_w2=v_mlp_w2)
    names = list(weights)
    L = x.shape[1]
    x0 = x[0]
    target = loss_target[0]
    my_index = 4 * lax.axis_index("x") + 2 * lax.axis_index("y") + lax.axis_index("c")

    small_vec = jnp.pad(jnp.concatenate([pool_scale, sgu_ln_g, sgu_ln_b]), ((0, 5), (0, LANES - 64)))
    ag_groups, ag_token = _comm_start(
        [[jnp.transpose(w_in_even[0]).astype(BF16), small_vec],
         [s5_w_glu[0].astype(BF16), w_out_even[0].astype(BF16)],
         [mlp_w1[0].astype(BF16), mlp_w2[0].astype(BF16)],
         [jnp.transpose(w_in_odd[0]).astype(BF16), w_out_odd[0].astype(BF16), mlp_w1[1].astype(BF16), mlp_w2[1].astype(BF16)]],
        "ag_start", exchange=False)

    lam_r = jnp.concatenate([s5_lam_re.reshape(1, S5_NS), s5_lam_im.reshape(1, S5_NS)], axis=0)
    ldt_r = jnp.repeat(s5_log_dt.reshape(32), 64).reshape(1, S5_NS)
    lam_c = jnp.transpose(lam_r)
    ldt_c = jnp.transpose(ldt_r)
    b_t = jnp.stack([jnp.tile(s5_b_re.reshape(S5_NS, 16), (1, 8)), jnp.tile(s5_b_im.reshape(S5_NS, 16), (1, 8))])
    c_t = jnp.stack([jnp.tile(s5_c_re.reshape(S5_W, 64), (1, 8)), jnp.tile(s5_c_im.reshape(S5_W, 64), (1, 8))])
    bf_pad = jnp.pad(fox_b_f, ((0, 0), (0, LANES - 8)))
    b_st = jnp.transpose(sgu_b_s[0])

    h0, rx0 = _rms_fwd(x0, _tie(mix_pre_g[0:1], ag_token), "rms0")
    tabs, bset, cset = _s5_prep(lam_r, ldt_r, lam_c, ldt_c, b_t, c_t, "s5_prep")
    ag0 = _comm_wait(ag_groups[0], tabs, "ag_wait0", exchange=False)
    winT_e = jnp.pad(ag0[0].reshape(EVEN_IN, D_MODEL), ((0, EVEN_PAD - EVEN_IN), (0, 0)))
    pool_scale_f = ag0[1][:, 0, :64].reshape(1, 512)
    ln_g_f = ag0[1][:, 1, :64].reshape(1, 512)
    ln_b_f = ag0[1][:, 2, :64].reshape(1, 512)
    z0 = _mm(h0, winT_e, name="win_even", tb=True, bm=512, bn=EVEN_PAD)
    xs, ylin = _s5_scan_fwd(z0, bset, cset, s5_d, tabs, "s5_scan")
    ag1 = _comm_wait(ag_groups[1], ylin, "ag_wait1", exchange=False)
    wglu = ag1[0].reshape(S5_W, S5_W)
    wout_e = ag1[1].reshape(D_MODEL, D_MODEL)
    ya = _s5_glu_fwd(ylin, wglu, "s5_glu")
    fcum, fq = _fox_f_fwd(z0, bf_pad, "fox_f")
    frow = jnp.transpose(fcum[:, :8]).reshape(4, 2, L)
    o_att, lse = _fox_fwd(z0, fq, frow, "fox_fwd")
    mix0 = [ya, o_att]
    x1, ry0, h1, rx1, y0 = _mm(mix0, wout_e, name="wout_even", epi=_epi_post_pre, extra=(x0,),
                               vecs=(mix_post_g[0:1], mlp_pre_g[0:1]), out_dtypes=POST_PRE_DTYPES,
                               out_kinds=POST_PRE_KINDS, bm=FUSED_ROWS)
    ag2 = _comm_wait(ag_groups[2], rx1, "ag_wait2", exchange=False)
    w1 = [ag2[0], None]
    w2 = [ag2[1].reshape(4 * D_MODEL, D_MODEL), None]
    p0, a0 = _mm(h1, w1[0], name="mlp0_w1", b3=True, out_dtypes=(BF16, BF16), epi=_epi_relu2, bm=512, bn=4 * D_MODEL)
    x2, ro0, h2, rx2, o0 = _mm(a0, w2[0], name="mlp0_w2", epi=_epi_post_pre, extra=(x1,),
                               vecs=(mlp_post_g[0:1], mix_pre_g[1:2]), out_dtypes=POST_PRE_DTYPES,
                               out_kinds=POST_PRE_KINDS, bm=FUSED_ROWS, bk=4 * D_MODEL)
    ag3 = _comm_wait(ag_groups[3], rx2, "ag_wait3", exchange=False)
    winT_o = ag3[0].reshape(ODD_IN, D_MODEL)
    wout_o = ag3[1].reshape(D_MODEL, D_MODEL)
    w1[1] = ag3[2]
    w2[1] = ag3[3].reshape(4 * D_MODEL, D_MODEL)
    z1 = _mm(h2, winT_o, name="win_odd", tb=True, bn=ODD_IN)
    yc, pooled = _pool_fwd(z1, pool_w[0], pool_scale_f, "pool_fwd")
    yd = _sgu_fwd(z1, ln_g_f, ln_b_f, sgu_w_s[0], b_st, "sgu_fwd")
    mix1 = [yc, yd]
    x3, ry1, h3, rx3, y1 = _mm(mix1, wout_o, name="wout_odd", epi=_epi_post_pre, extra=(x2,),
                               vecs=(mix_post_g[1:2], mlp_pre_g[1:2]), out_dtypes=POST_PRE_DTYPES,
                               out_kinds=POST_PRE_KINDS, bm=FUSED_ROWS)
    p1, a1 = _mm(h3, w1[1], name="mlp1_w1", b3=True, out_dtypes=(BF16, BF16), epi=_epi_relu2, bm=512, bn=4 * D_MODEL)
    gx4, g_o1, gg_mlp_post1, sq_lanes = _mm(
        a1, w2[1], name="mlp1_w2", epi=_epi_post_loss, extra=(x3, target), vecs=(mlp_post_g[1:2],),
        out_dtypes=(F32, BF16, F32, F32), out_kinds=("full", "full", "vsum", "vsum"), bm=FUSED_ROWS, bk=4 * D_MODEL)
    sq = sq_lanes[:, 0:1]

    g_p1 = _mm(g_o1, w2[1], name="b_mlp1_a", tb=True, out_dtypes=(BF16,), epi=_epi_relu2_bwd, extra=(p1,),
               bm=512, bn=4 * D_MODEL)
    gw2_1 = _mm(a1, g_o1, name="b_mlp1_w2", ta=True, bm=512, bk=L)
    gw1_1 = _mm(h3, g_p1, name="b_mlp1_w1", ta=True, out3=True, bn=512, bk=L)
    (ex1,), tok1 = _comm_start([[gw1_1, gw2_1.reshape(N_DEV, 512, D_MODEL)]], "ex_start1", exchange=True)
    g_x3, gg_mlp_pre1, g_y1, gg_mix_post1 = _mm(
        g_p1, w1[1], name="b_mlp1_h", tb=True, b3=True, epi=_epi_pre_post_bwd, extra=(x3, gx4, y1), cols=(rx3, ry1),
        vecs=(_tie(mlp_pre_g[1:2], tok1), mix_post_g[1:2]), out_dtypes=PRE_POST_BWD_DTYPES,
        out_kinds=PRE_POST_BWD_KINDS, bm=FUSED_ROWS, bk=4 * D_MODEL)
    gwout_o = _mm(mix1, g_y1, name="b_wout_odd_w", ta=True)
    g_xc, g_pool_w, g_pool_scale = _pool_bwd(g_y1, wout_o, pooled, pool_w[0], pool_scale_f, "pool_bwd")
    g_u1, g_v1, g_ws, g_bst, g_ln_g, g_ln_b = _sgu_bwd(g_y1, wout_o, z1, ln_g_f, ln_b_f, sgu_w_s[0], b_st,
                                                       "sgu_bwd")
    g_z1 = [g_xc, g_u1, g_v1]
    gwinT_o = _mm(g_z1, h2, name="b_win_odd_w", ta=True)
    (ex2,), tok2 = _comm_start([[gwout_o.reshape(N_DEV, 128, D_MODEL), gwinT_o.reshape(N_DEV, ODD_IN // N_DEV, D_MODEL)]], "ex_start2", exchange=True)
    g_x2, gg_mix_pre1, g_o0, gg_mlp_post0 = _mm(
        g_z1, winT_o, name="b_win_odd_h", epi=_epi_pre_post_bwd, extra=(x2, g_x3, o0), cols=(rx2, ro0),
        vecs=(_tie(mix_pre_g[1:2], tok2), mlp_post_g[0:1]), out_dtypes=PRE_POST_BWD_DTYPES,
        out_kinds=PRE_POST_BWD_KINDS, bm=FUSED_ROWS)
    g_p0 = _mm(g_o0, w2[0], name="b_mlp0_a", tb=True, out_dtypes=(BF16,), epi=_epi_relu2_bwd, extra=(p0,),
               bm=512, bn=4 * D_MODEL)
    gw2_0 = _mm(a0, g_o0, name="b_mlp0_w2", ta=True, bm=512, bk=L)
    gw1_0 = _mm(h1, g_p0, name="b_mlp0_w1", ta=True, out3=True, bn=512, bk=L)
    (ex3,), tok3 = _comm_start([[gw1_0, gw2_0.reshape(N_DEV, 512, D_MODEL)]], "ex_start3", exchange=True)
    g_x1, gg_mlp_pre0, g_y0, gg_mix_post0 = _mm(
        g_p0, w1[0], name="b_mlp0_h", tb=True, b3=True, epi=_epi_pre_post_bwd, extra=(x1, g_x2, y0), cols=(rx1, ry0),
        vecs=(_tie(mlp_pre_g[0:1], tok3), mix_post_g[0:1]), out_dtypes=PRE_POST_BWD_DTYPES,
        out_kinds=PRE_POST_BWD_KINDS, bm=FUSED_ROWS, bk=4 * D_MODEL)
    g_o_att = _mm(g_y0, wout_e[FOX_W:], name="b_wout_even_m", tb=True)
    gwout_e = _mm(mix0, g_y0, name="b_wout_even_w", ta=True)
    gyl, gud, g_wglu, g_d = _s5_glu_bwd(g_y0, wout_e, ylin, z0, s5_d, wglu, "s5_glu_bwd")
    (ex4,), tok4 = _comm_start([[gwout_e.reshape(N_DEV, 128, D_MODEL), g_wglu.reshape(N_DEV, 64, S5_W)]], "ex_start4", exchange=True)
    g_u0, ga, gb_raw, gc_raw = _s5_scan_bwd(gyl, _tie(cset, tok4), xs, z0, bset, gud, tabs, "s5_scan_bwd")
    g_lam, g_ldt, g_b, g_c = _s5_param_bwd(lam_c, ldt_c, b_t, gb_raw, jnp.transpose(ga), gc_raw, "s5_param_bwd")
    dq, dk, dv, dfq, dfrow = _fox_bwd(z0, frow, o_att, lse, g_o_att, "fox_bwd")
    dFk = jnp.pad(jnp.transpose(dfrow.reshape(8, L)), ((0, 0), (0, LANES - 8)))
    dfl, db_f = _fox_f_bwd(dFk, dfq, z0, bf_pad, "fox_f_bwd")
    g_z0 = [g_u0, dq, dk, dv, dfl]
    grad_x, gg_mix_pre0 = _mm(g_z0, winT_e, name="b_win_even_h", epi=_epi_pre_bwd, extra=(x0, g_x1), cols=(rx0,),
                              vecs=(mix_pre_g[0:1],), out_dtypes=(F32, F32), out_kinds=("full", "vsum"),
                              bm=FUSED_ROWS)

    small_grads = dict(
        mix_pre_g=jnp.concatenate([gg_mix_pre0, gg_mix_pre1]), mix_post_g=jnp.concatenate([gg_mix_post0, gg_mix_post1]),
        mlp_pre_g=jnp.concatenate([gg_mlp_pre0, gg_mlp_pre1]), mlp_post_g=jnp.concatenate([gg_mlp_post0, gg_mlp_post1]),
        s5_lam_re=g_lam[:, 0], s5_lam_im=g_lam[:, 1],
        s5_b_re=g_b[0, :, :16], s5_b_im=g_b[1, :, :16], s5_c_re=g_c[0, :, :64], s5_c_im=g_c[1, :, :64],
        pool_w=g_pool_w, sgu_w_s=g_ws, s5_d=g_d, sgu_b_s=jnp.transpose(g_bst),
        pool_scale=g_pool_scale, sgu_ln_g=g_ln_g, sgu_ln_b=g_ln_b, s5_log_dt=g_ldt, fox_b_f=db_f[:, :8])
    small_names = list(small_grads)
    full_shapes = [(512,) if nm in ("pool_scale", "sgu_ln_g", "sgu_ln_b") else weights[nm].shape for nm in small_names]
    full_shapes.append((1, 1))
    rows = _packed_rows(full_shapes)
    packed = _pack([small_grads[nm] for nm in small_names] + [sq], rows).reshape(N_DEV, rows // N_DEV, LANES)
    (exs,), tok_s = _comm_start([[packed]], "exs_start", exchange=True)
    gwinT_e = _mm(g_z0, h0, name="b_win_even_w", ta=True, out_dtypes=(BF16,), dep=tok_s)
    (recv_small,) = _comm_wait(exs, gwinT_e, "exs_wait", exchange=True)
    piece = _sum_pieces(recv_small, "sum_small")
    gwinT_e_pieces = gwinT_e[:EVEN_IN].reshape(N_DEV, EVEN_IN // N_DEV, D_MODEL)
    (ags, ex5), tok5 = _comm_start([[piece], [gwinT_e_pieces]], "ags_ex_start5", exchange=(False, True))
    r_w1_1, r_w2_1 = _comm_wait(ex1, tok5, "ex_wait1", exchange=True)
    r_wout_o, r_win_o = _comm_wait(ex2, tok5, "ex_wait2", exchange=True)
    r_w1_0, r_w2_0 = _comm_wait(ex3, tok5, "ex_wait3", exchange=True)
    r_wout_e, r_wglu = _comm_wait(ex4, tok5, "ex_wait4", exchange=True)

    res = {}
    for nm, parts in (("mlp_w1", (r_w1_0, r_w1_1)), ("mlp_w2", (r_w2_0, r_w2_1))):
        first = _sum_adamw(parts[0], weights[nm], mom_m[nm], mom_v[nm], "adamw_%s_0" % nm, layer=0)
        res[nm] = tuple(_sum_adamw(parts[1], weights[nm], mom_m[nm], mom_v[nm], "adamw_%s_1" % nm, layer=1, prev=first))
    big_parts = dict(s5_w_glu=r_wglu, w_out_even=r_wout_e, w_out_odd=r_wout_o)
    for nm, parts in big_parts.items():
        res[nm] = tuple(_sum_adamw(parts, weights[nm], mom_m[nm], mom_v[nm], "adamw_" + nm))
    done = [res[nm][1] for nm in ("mlp_w1", "mlp_w2", "s5_w_glu", "w_out_even", "w_out_odd")]

    (small_all,) = _comm_wait(ags, done, "ags_wait", exchange=False)
    small_full = _unpack(small_all.reshape(rows, LANES), full_shapes)
    loss = 0.5 * small_full.pop()[0, 0] / D_MODEL
    small_g = []
    for nm, g in zip(small_names, small_full):
        if nm in ("pool_scale", "sgu_ln_g", "sgu_ln_b"):
            g = lax.dynamic_slice(g, (my_index * 64,), (64,)).reshape(1, 64)
        small_g.append(g)

    def turned(arrs):
        return [jnp.swapaxes(a, -1, -2) if nm in ("s5_b_re", "s5_b_im") else a for nm, a in zip(small_names, arrs)]

    sd, sm, sv = _adamw_many(turned([weights[nm] for nm in small_names]), turned(small_g),
                             turned([mom_m[nm] for nm in small_names]), turned([mom_v[nm] for nm in small_names]),
                             "adamw_small")
    for nm, g_, d_, m_, v_ in zip(small_names, small_g, turned(sd), turned(sm), turned(sv)):
        res[nm] = (g_, d_, m_, v_)
    done.append(sd[0])

    nm = "w_in_odd"
    outs = _sum_adamw(r_win_o, jnp.transpose(weights[nm], (0, 2, 1)), jnp.transpose(mom_m[nm], (0, 2, 1)),
                      jnp.transpose(mom_v[nm], (0, 2, 1)), "adamw_" + nm)
    res[nm] = tuple(jnp.transpose(o, (0, 2, 1)) for o in outs)
    done.append(outs[1])
    nm = "w_in_even"
    (r_win_e,) = _comm_wait(ex5, done, "ex_wait5", exchange=True)
    outs = _sum_adamw_rows(r_win_e, jnp.transpose(weights[nm], (2, 0, 1)), jnp.transpose(mom_m[nm], (2, 0, 1)),
                           jnp.transpose(mom_v[nm], (2, 0, 1)), "adamw_" + nm)
    res[nm] = tuple(jnp.transpose(o, (1, 2, 0)) for o in outs)

    grads = [res[nm][0].reshape(weights[nm].shape) for nm in names]
    deltas = [res[nm][1].reshape(weights[nm].shape) for nm in names]
    new_m = [res[nm][2].reshape(weights[nm].shape) for nm in names]
    new_v = [res[nm][3].reshape(weights[nm].shape) for nm in names]
    return (loss, grad_x[None], *grads, *deltas, *new_m, *new_v)
```

```python
import math

import jax
import jax.numpy as jnp
from jax import lax
from jax.experimental import pallas as pl
from jax.experimental.pallas import tpu as pltpu

F32 = jnp.float32
BF16 = jnp.bfloat16
MESH = pl.DeviceIdType.MESH
ANY = pl.BlockSpec(memory_space=pl.ANY)

N_DEV = 8
D_MODEL = 1024
EPS = 1e-6
NORM_ROWS = 512
FUSED_ROWS = 512
S5_W = 512
S5_NS = 2048
SCAN_GROUPS = 4
SCAN_CHUNK = 1024
FOX_W = 512
EVEN_IN = 2056
EVEN_PAD = 2176
ODD_IN = 1536
LANES = 128
PIECE = 4 * D_MODEL // N_DEV
VMEM_LIMIT = 56 * 1024 * 1024

ADAM_LR = 0.001
ADAM_B1 = 0.9
ADAM_B2 = 0.999
ADAM_EPS = 1e-08
ADAM_WD = 0.01
ADAM_STEP = 10

NT = (((1,), (1,)), ((), ()))
TN = (((0,), (0,)), ((), ()))
NN = (((1,), (0,)), ((), ()))


def _cp(*sem):
    return pltpu.CompilerParams(dimension_semantics=sem, vmem_limit_bytes=VMEM_LIMIT)


def _sds(shape, dtype=F32):
    return jax.ShapeDtypeStruct(tuple(shape), dtype)


def _gelu(x):
    t = jnp.tanh(0.7978845608028654 * (x + 0.044715 * x * x * x))
    return 0.5 * x * (1.0 + t)


def _gelu_grad(x):
    t = jnp.tanh(0.7978845608028654 * (x + 0.044715 * x * x * x))
    du = 0.7978845608028654 * (1.0 + 3.0 * 0.044715 * x * x)
    return 0.5 * (1.0 + t) + 0.5 * x * (1.0 - t * t) * du


def _sigmoid(x):
    return 1.0 / (1.0 + jnp.exp(-x))


def _dot(a, b, dn=NN):
    return lax.dot_general(a, b, dn, preferred_element_type=F32)


def _mm(a, b, *, name, ta=False, tb=False, b3=False, out3=False, out_dtypes=(F32,), epi=None, extra=(),
        cols=(), vecs=(), out_kinds=None, bm=1024, bn=1024, bk=1024, dep=None):
    a_list = list(a) if isinstance(a, (list, tuple)) else [a]
    widths = [p.shape[1] for p in a_list]
    offs = [sum(widths[:i]) for i in range(len(widths))]
    na = len(a_list)
    M = sum(widths) if ta else a_list[0].shape[0]
    K = a_list[0].shape[0] if ta else sum(widths)
    if na > 1:
        assert not b3 and not tb
        bm, bk = (M, bk) if ta else (bm, K)
    pw = b.shape[2] if b3 else PIECE
    if b3:
        N = b.shape[1] if tb else b.shape[0] * pw
        assert (b.shape[0] * pw if tb else b.shape[1]) == K
    else:
        N = b.shape[0] if tb else b.shape[1]
    bm, bn, bk = min(bm, M), min(bn, N), min(bk, K)
    assert M % bm == 0 and N % bn == 0 and K % bk == 0, (name, M, N, K, bm, bn, bk)
    assert not (b3 or out3) or ((bk if tb else bn) % pw == 0 and bn % PIECE == 0)
    nk = K // bk
    n_extra = len(extra) + len(cols) + len(vecs)
    n_out = len(out_dtypes)
    out_kinds = tuple(out_kinds) if out_kinds is not None else ("full",) * n_out
    dn = (((0 if ta else 1,), (1 if tb else 0,)), ((), ()))

    use_acc = nk > 1

    def body(*refs):
        a_refs, b_ref = refs[:na], refs[na]
        a_ref = a_refs[0]
        e_refs = refs[na + 1:na + 1 + n_extra]
        first_out = na + 1 + n_extra + (0 if dep is None else 1)
        o_refs = refs[first_out:first_out + n_out]
        acc_ref = refs[-1] if use_acc else o_refs[0]
        i, k = pl.program_id(0), pl.program_id(2)

        def dot(a_v, b_v):
            return lax.dot_general(a_v.astype(BF16), b_v.astype(BF16), dn, preferred_element_type=F32)

        everything = slice(None)
        if na > 1 and ta:
            terms = [(pl.ds(off, w), everything, r, b_ref) for r, off, w in zip(a_refs, offs, widths)]
        elif na > 1:
            terms = [(everything, everything, r, b_ref.at[pl.ds(off, w), :]) for r, off, w in zip(a_refs, offs, widths)]
        elif not b3:
            terms = [(everything, everything, a_ref, b_ref)]
        elif tb:
            terms = [(everything, everything,
                      a_ref.at[pl.ds(t * pw, pw), :] if ta else a_ref.at[:, pl.ds(t * pw, pw)], b_ref.at[t])
                     for t in range(bk // pw)]
        else:
            terms = [(everything, pl.ds(t * pw, pw), a_ref, b_ref.at[t]) for t in range(bn // pw)]

        def finish(acc):
            outs = (acc,) if epi is None else epi(acc, *[e[...] for e in e_refs])
            for o_ref, o, kind in zip(o_refs, outs, out_kinds):
                if kind == "vsum":
                    @pl.when(i == 0)
                    def _(o_ref=o_ref, o=o):
                        o_ref[...] = o

                    @pl.when(i > 0)
                    def _(o_ref=o_ref, o=o):
                        o_ref[...] += o
                elif out3:
                    for t in range(bn // PIECE):
                        o_ref[t] = o[:, t * PIECE:(t + 1) * PIECE].astype(o_ref.dtype)
                else:
                    o_ref[...] = o.astype(o_ref.dtype)

        if nk == 1:
            bands = {}
            for rows, cols, a_r, b_r in terms:
                key = (getattr(rows, "start", None), getattr(cols, "start", None))
                val = dot(a_r[...], b_r[...])
                bands[key] = val if key not in bands else bands[key] + val
            vals = list(bands.values())
            if len(vals) == 1:
                finish(vals[0])
            else:
                finish(jnp.concatenate(vals, axis=0 if (na > 1 and ta) else 1))
            return

        @pl.when(k == 0)
        def _():
            acc_ref[...] = jnp.zeros_like(acc_ref)

        for rows, cols, a_r, b_r in terms:
            acc_ref[rows, cols] += dot(a_r[...], b_r[...])

        @pl.when(k == nk - 1)
        def _():
            finish(acc_ref[...])

    if na > 1:
        a_specs = [pl.BlockSpec((bk, w), lambda i, j, k: (k, 0)) if ta else pl.BlockSpec((bm, w), lambda i, j, k: (i, 0))
                   for w in widths]
    else:
        a_specs = [pl.BlockSpec((bk, bm), lambda i, j, k: (k, i)) if ta else
                   pl.BlockSpec((bm, bk), lambda i, j, k: (i, k))]
    if b3:
        if tb:
            b_spec = pl.BlockSpec((bk // pw, bn, pw), lambda i, j, k: (k, j, 0))
        else:
            b_spec = pl.BlockSpec((bn // pw, bk, pw), lambda i, j, k: (j, k, 0))
    else:
        b_spec = pl.BlockSpec((bn, bk), lambda i, j, k: (j, k)) if tb else pl.BlockSpec((bk, bn), lambda i, j, k: (k, j))
    e_specs = ([pl.BlockSpec((bm, bn), lambda i, j, k: (i, j)) for _ in extra]
               + [pl.BlockSpec((bm, 1), lambda i, j, k: (i, 0)) for _ in cols]
               + [pl.BlockSpec((1, bn), lambda i, j, k: (0, j)) for _ in vecs])
    if out3:
        o_specs = [pl.BlockSpec((bn // PIECE, bm, PIECE), lambda i, j, k: (j, i, 0)) for _ in out_dtypes]
        o_shapes = [_sds((N // PIECE, M, PIECE), dt) for dt in out_dtypes]
    else:
        spec_of = {"full": pl.BlockSpec((bm, bn), lambda i, j, k: (i, j)),
                   "col": pl.BlockSpec((bm, 1), lambda i, j, k: (i, 0)),
                   "vsum": pl.BlockSpec((1, bn), lambda i, j, k: (0, j))}
        shape_of = {"full": (M, N), "col": (M, 1), "vsum": (1, N)}
        o_specs = [spec_of[kind] for kind in out_kinds]
        o_shapes = [_sds(shape_of[kind], dt) for kind, dt in zip(out_kinds, out_dtypes)]
    assert "col" not in out_kinds or bn == N
    outs = pl.pallas_call(
        body, name=name, grid=(M // bm, N // bn, nk),
        in_specs=a_specs + [b_spec] + e_specs + ([] if dep is None else [ANY]),
        out_specs=o_specs, out_shape=o_shapes,
        scratch_shapes=[pltpu.VMEM((bm, bn), F32)] if use_acc else [],
        compiler_params=_cp("arbitrary" if "vsum" in out_kinds else "parallel", "parallel", "arbitrary"),
    )(*a_list, b, *extra, *cols, *vecs, *([] if dep is None else [dep]))
    return outs[0] if n_out == 1 else outs


def _epi_relu2(acc):
    r = jnp.maximum(acc, 0.0)
    return acc, r * r


def _epi_relu2_bwd(acc, p):
    return (acc * (2.0 * jnp.maximum(p.astype(F32), 0.0)),)


def _row_spec(rb, w=D_MODEL):
    return pl.BlockSpec((rb, w), lambda i: (i, 0))


def _vec_spec(w=D_MODEL):
    return pl.BlockSpec((1, w), lambda i: (0, 0))


def _rstd(v):
    return lax.rsqrt(jnp.mean(v * v, axis=-1, keepdims=True) + EPS)


def _rms_fwd(x, g, name):
    L = x.shape[0]
    rb = min(NORM_ROWS, L)

    def body(x_ref, g_ref, h_ref, r_ref):
        xv = x_ref[...]
        r = _rstd(xv)
        h_ref[...] = (xv * r * g_ref[...]).astype(BF16)
        r_ref[...] = r

    return pl.pallas_call(
        body, name=name, grid=(L // rb,),
        in_specs=[_row_spec(rb), _vec_spec()],
        out_specs=[_row_spec(rb), _row_spec(rb, 1)],
        out_shape=[_sds((L, D_MODEL), BF16), _sds((L, 1))],
        compiler_params=_cp("parallel"),
    )(x, g)


def _rms_bwd_rows(dy, xv, r, g):
    n = xv * r
    dyg = dy * g
    return r * (dyg - n * jnp.mean(dyg * n, axis=-1, keepdims=True)), n


POST_PRE_DTYPES = (F32, F32, BF16, F32, F32)
POST_PRE_KINDS = ("full", "col", "full", "col", "full")
PRE_POST_BWD_DTYPES = (F32, F32, BF16, F32)
PRE_POST_BWD_KINDS = ("full", "vsum", "full", "vsum")


def _epi_post_pre(y, x_in, g_post, g_pre):
    ry = _rstd(y)
    xo = x_in + y * ry * g_post
    rx = _rstd(xo)
    return xo, ry, xo * rx * g_pre, rx, y


def _epi_pre_post_bwd(gh, x, g_out, y_prev, rx, ry_prev, g_pre, g_post_prev):
    gx, n = _rms_bwd_rows(gh, x, rx, g_pre)
    gi = g_out + gx
    gy, ny = _rms_bwd_rows(gi, y_prev, ry_prev, g_post_prev)
    return gi, jnp.sum(gh * n, axis=0, keepdims=True), gy, jnp.sum(gi * ny, axis=0, keepdims=True)


def _epi_pre_bwd(gh, x, g_out, rx, g_pre):
    gx, n = _rms_bwd_rows(gh, x, rx, g_pre)
    return g_out + gx, jnp.sum(gh * n, axis=0, keepdims=True)


def _epi_post_loss(y, x_in, target, g_post):
    ry = _rstd(y)
    diff = x_in + y * ry * g_post - target
    gx = diff * (1.0 / D_MODEL)
    gy, n = _rms_bwd_rows(gx, y, ry, g_post)
    sq = jnp.broadcast_to(jnp.sum(diff * diff, keepdims=True), (1, y.shape[1]))
    return gx, gy, jnp.sum(gx * n, axis=0, keepdims=True), sq


def _cmul(ar, ai, br, bi):
    return ar * br - ai * bi, ar * bi + ai * br


def _zoh_cols(lr, li, ldt):
    dt = jnp.exp(ldt)
    mag = jnp.exp(lr * dt)
    ar = mag * jnp.cos(li * dt)
    ai = mag * jnp.sin(li * dt)
    den = lr * lr + li * li
    nr = ar - 1.0
    qr = (nr * lr + ai * li) / den
    qi = (ai * lr - nr * li) / den
    return dt, ar, ai, qr, qi, den


def _b_mask():
    r = lax.broadcasted_iota(jnp.int32, (S5_NS, LANES), 0)
    c = lax.broadcasted_iota(jnp.int32, (S5_NS, LANES), 1)
    return ((r >> 6) & 7) == (c >> 4)


def _c_mask():
    r = lax.broadcasted_iota(jnp.int32, (S5_W, 512), 0)
    c = lax.broadcasted_iota(jnp.int32, (S5_W, 512), 1)
    return ((r >> 4) & 7) == (c >> 6)


def _s5_prep(lam_r, ldt_r, lam_c, ldt_c, b_t, c_t, name):
    def body(lam_r_ref, ldt_r_ref, lam_c_ref, ldt_c_ref, b_ref, c_ref, tab_ref, bset_ref, cset_ref):
        lr, li = lam_r_ref[0:1, :], lam_r_ref[1:2, :]
        dt = jnp.exp(ldt_r_ref[...])
        mag = jnp.exp(lr * dt)
        p1r, p1i = mag * jnp.cos(li * dt), mag * jnp.sin(li * dt)
        p2r, p2i = _cmul(p1r, p1i, p1r, p1i)
        p3r, p3i = _cmul(p2r, p2i, p1r, p1i)
        p4r, p4i = _cmul(p2r, p2i, p2r, p2i)
        p5r, p5i = _cmul(p4r, p4i, p1r, p1i)
        p6r, p6i = _cmul(p4r, p4i, p2r, p2i)
        p7r, p7i = _cmul(p4r, p4i, p3r, p3i)
        p8r, p8i = _cmul(p4r, p4i, p4r, p4i)
        pw_r = [p1r, p2r, p3r, p4r, p5r, p6r, p7r, p8r]
        pw_i = [p1i, p2i, p3i, p4i, p5i, p6i, p7i, p8i]
        row = lax.broadcasted_iota(jnp.int32, (8, S5_NS), 0)
        zero = jnp.zeros((8, S5_NS), F32)

        def bc(v):
            return jnp.broadcast_to(v, (8, S5_NS))

        for d in range(2):
            sgn = 1.0 if d == 0 else -1.0
            for t, s in enumerate((1, 2, 4)):
                live = (row >= s) if d == 0 else (row <= 7 - s)
                tab_ref[d, 2 * t] = jnp.where(live, bc(pw_r[s - 1]), zero)
                tab_ref[d, 2 * t + 1] = jnp.where(live, bc(sgn * pw_i[s - 1]), zero)
            cr, ci = zero, zero
            for i in range(8):
                e = i if d == 0 else 7 - i
                cr = jnp.where(row == i, bc(pw_r[e]), cr)
                ci = jnp.where(row == i, bc(sgn * pw_i[e]), ci)
            tab_ref[d, 6] = cr
            tab_ref[d, 7] = ci

        _, _, _, qr, qi, _ = _zoh_cols(lam_c_ref[:, 0:1], lam_c_ref[:, 1:2], ldt_c_ref[...])
        bm = _b_mask()
        br, bi = b_ref[0], b_ref[1]
        bset_ref[0] = jnp.where(bm, qr * br - qi * bi, 0.0).astype(BF16)
        bset_ref[1] = jnp.where(bm, qr * bi + qi * br, 0.0).astype(BF16)
        cm = _c_mask()
        cset_ref[0] = jnp.where(cm, c_ref[0], 0.0).astype(BF16)
        cset_ref[1] = jnp.where(cm, c_ref[1], 0.0).astype(BF16)

    vm = pl.BlockSpec(memory_space=pltpu.VMEM)
    return pl.pallas_call(
        body, name=name, in_specs=[vm] * 6, out_specs=[vm] * 3,
        out_shape=[_sds((2, 8, 8, S5_NS)), _sds((2, S5_NS, LANES), BF16), _sds((2, S5_W, 512), BF16)],
        compiler_params=pltpu.CompilerParams(vmem_limit_bytes=VMEM_LIMIT),
    )(lam_r, ldt_r, lam_c, ldt_c, b_t, c_t)


SCAN_W = SCAN_GROUPS * LANES


def _scan_chunk(src_ref, dst_ref, tab_ref, carry_ref, nb, reverse, xs_ref=None, acc_ref=None):
    row = lax.broadcasted_iota(jnp.int32, (8, LANES), 0)

    def step(i, carry):
        b = (nb - 1 - i) if reverse else i
        off = pl.multiple_of(b * 8, 8)
        out = []
        for g in range(SCAN_GROUPS):
            lanes = pl.ds(g * LANES, LANES)
            cr, ci = carry[2 * g], carry[2 * g + 1]
            yr = src_ref[0, pl.ds(off, 8), lanes]
            yi = src_ref[1, pl.ds(off, 8), lanes]
            for t, s in enumerate((1, 2, 4)):
                sh = (8 - s) if reverse else s
                sr = pltpu.roll(yr, sh, 0)
                si = pltpu.roll(yi, sh, 0)
                mr, mi = tab_ref[2 * t, :, lanes], tab_ref[2 * t + 1, :, lanes]
                yr, yi = yr + mr * sr - mi * si, yi + mr * si + mi * sr
            pr, pi = tab_ref[6, :, lanes], tab_ref[7, :, lanes]
            yr, yi = yr + pr * cr - pi * ci, yi + pr * ci + pi * cr
            dst_ref[0, pl.ds(off, 8), lanes] = yr
            dst_ref[1, pl.ds(off, 8), lanes] = yi
            if xs_ref is not None:
                nr = jnp.where(row == 7, cr, pltpu.roll(yr, 7, 0))
                ni = jnp.where(row == 7, ci, pltpu.roll(yi, 7, 0))
                xr = xs_ref[0, pl.ds(off, 8), lanes]
                xi = xs_ref[1, pl.ds(off, 8), lanes]
                acc_ref[0, :, lanes] += xr * nr + xi * ni
                acc_ref[1, :, lanes] += xr * ni - xi * nr
            last = 0 if reverse else 7
            out += [jnp.broadcast_to(yr[last:last + 1, :], (8, LANES)),
                    jnp.broadcast_to(yi[last:last + 1, :], (8, LANES))]
        return tuple(out)

    init = []
    for g in range(SCAN_GROUPS):
        init += [carry_ref[0, :, pl.ds(g * LANES, LANES)], carry_ref[1, :, pl.ds(g * LANES, LANES)]]
    fin = lax.fori_loop(0, nb, step, tuple(init))
    for g in range(SCAN_GROUPS):
        carry_ref[0, :, pl.ds(g * LANES, LANES)] = fin[2 * g]
        carry_ref[1, :, pl.ds(g * LANES, LANES)] = fin[2 * g + 1]


def _s5_scan_fwd(z, bset, cset, dvec, tabs, name):
    L = z.shape[0]
    tl = min(SCAN_CHUNK, L)
    nc = L // tl

    def body(u_ref, b_ref, c_ref, d_ref, tab_ref, x_ref, y_ref, carry_ref):
        @pl.when(pl.program_id(1) == 0)
        def _():
            carry_ref[...] = jnp.zeros_like(carry_ref)

        uf = u_ref[...]
        u = uf.astype(BF16)
        x_ref[0] = _dot(u, b_ref[0], NT)
        x_ref[1] = _dot(u, b_ref[1], NT)
        _scan_chunk(x_ref, x_ref, tab_ref, carry_ref, tl // 8, False)
        y_ref[...] = (_dot(x_ref[0].astype(BF16), c_ref[0], NT) - _dot(x_ref[1].astype(BF16), c_ref[1], NT)
                      + d_ref[...] * uf)

    col = pl.BlockSpec((tl, LANES), lambda j, c: (c, j))
    return pl.pallas_call(
        body, name=name, grid=(S5_NS // SCAN_W, nc),
        in_specs=[col, pl.BlockSpec((2, SCAN_W, LANES), lambda j, c: (0, j, 0)),
                  pl.BlockSpec((2, LANES, SCAN_W), lambda j, c: (0, j, 0)),
                  pl.BlockSpec((1, LANES), lambda j, c: (0, j)),
                  pl.BlockSpec((None, 8, 8, SCAN_W), lambda j, c: (0, 0, 0, j))],
        out_specs=[pl.BlockSpec((2, tl, SCAN_W), lambda j, c: (0, c, j)), col],
        out_shape=[_sds((2, L, S5_NS)), _sds((L, S5_W))],
        scratch_shapes=[pltpu.VMEM((2, 8, SCAN_W), F32)],
        compiler_params=_cp("parallel", "arbitrary"),
    )(z, bset, cset, dvec, tabs)


def _s5_scan_bwd(gyl, cset, xs, z, bset, gud, tabs, name):
    L = z.shape[0]
    tl = min(SCAN_CHUNK, L)
    nc = L // tl

    def body(g_ref, c_ref, xs_ref, u_ref, b_ref, gud_ref, tab_ref, gu_ref, ga_ref, gb_ref, gc_ref,
             gx_ref, carry_ref, acc_ref):
        c = pl.program_id(1)

        @pl.when(c == 0)
        def _():
            carry_ref[...] = jnp.zeros_like(carry_ref)
            acc_ref[...] = jnp.zeros_like(acc_ref)
            gb_ref[...] = jnp.zeros_like(gb_ref)
            gc_ref[...] = jnp.zeros_like(gc_ref)

        gy = g_ref[...].astype(BF16)
        gx_ref[0] = _dot(gy, c_ref[0])
        gx_ref[1] = -_dot(gy, c_ref[1])
        gc_ref[0] += _dot(gy, xs_ref[0].astype(BF16), TN)
        gc_ref[1] -= _dot(gy, xs_ref[1].astype(BF16), TN)
        _scan_chunk(gx_ref, gx_ref, tab_ref, carry_ref, tl // 8, True, xs_ref, acc_ref)
        gr = gx_ref[0].astype(BF16)
        gi = gx_ref[1].astype(BF16)
        gu_ref[...] = gud_ref[...] + _dot(gr, b_ref[0]) + _dot(gi, b_ref[1])
        u = u_ref[...].astype(BF16)
        gb_ref[0] += _dot(gr, u, TN)
        gb_ref[1] += _dot(gi, u, TN)

        @pl.when(c == nc - 1)
        def _():
            ga_ref[0:1, :] = jnp.sum(acc_ref[0], axis=0, keepdims=True)
            ga_ref[1:2, :] = jnp.sum(acc_ref[1], axis=0, keepdims=True)

    rev = lambda j, c: (nc - 1 - c, j)
    col = pl.BlockSpec((tl, LANES), rev)
    return pl.pallas_call(
        body, name=name, grid=(S5_NS // SCAN_W, nc),
        in_specs=[col, pl.BlockSpec((2, LANES, SCAN_W), lambda j, c: (0, j, 0)),
                  pl.BlockSpec((2, tl, SCAN_W), lambda j, c: (0, nc - 1 - c, j)), col,
                  pl.BlockSpec((2, SCAN_W, LANES), lambda j, c: (0, j, 0)), col,
                  pl.BlockSpec((None, 8, 8, SCAN_W), lambda j, c: (1, 0, 0, j))],
        out_specs=[col, pl.BlockSpec((2, SCAN_W), lambda j, c: (0, j)),
                   pl.BlockSpec((2, SCAN_W, LANES), lambda j, c: (0, j, 0)),
                   pl.BlockSpec((2, LANES, SCAN_W), lambda j, c: (0, j, 0))],
        out_shape=[_sds((L, S5_W)), _sds((2, S5_NS)), _sds((2, S5_NS, LANES)), _sds((2, S5_W, 512))],
        scratch_shapes=[pltpu.VMEM((2, tl, SCAN_W), F32), pltpu.VMEM((2, 8, SCAN_W), F32),
                        pltpu.VMEM((2, 8, SCAN_W), F32)],
        compiler_params=_cp("parallel", "arbitrary"),
    )(gyl, cset, xs, z, bset, gud, tabs)


def _s5_glu_fwd(ylin, wglu, name):
    L = ylin.shape[0]
    bl = min(1024, L)

    def body(ylin_ref, w_ref, ya_ref):
        yg = _gelu(ylin_ref[...])
        t = _dot(yg.astype(BF16), w_ref[...])
        ya_ref[...] = (yg * _sigmoid(t)).astype(BF16)

    return pl.pallas_call(
        body, name=name, grid=(L // bl,),
        in_specs=[pl.BlockSpec((bl, S5_W), lambda i: (i, 0)), pl.BlockSpec((S5_W, S5_W), lambda i: (0, 0))],
        out_specs=pl.BlockSpec((bl, S5_W), lambda i: (i, 0)),
        out_shape=_sds((L, S5_W), BF16),
        compiler_params=_cp("parallel"),
    )(ylin, wglu)


def _s5_glu_bwd(g_y, wout, ylin, z, dvec, wglu, name):
    L = z.shape[0]
    bl = min(256, L)

    def body(g_ref, wo_ref, ylin_ref, u_ref, d_ref, w_ref, gyl_ref, gud_ref, gw_ref, gd_ref):
        i = pl.program_id(0)
        ylin = ylin_ref[...]
        yg = _gelu(ylin)
        ygb = yg.astype(BF16)
        sg = _sigmoid(_dot(ygb, w_ref[...]))
        gya = _dot(g_ref[...], wo_ref[...], NT)
        gt = gya * yg * sg * (1.0 - sg)
        gtb = gt.astype(BF16)
        gyg = gya * sg + _dot(gtb, w_ref[...], NT)
        gyl = gyg * _gelu_grad(ylin)
        gyl_ref[...] = gyl
        gud_ref[...] = gyl * d_ref[...]

        @pl.when(i == 0)
        def _():
            gw_ref[...] = jnp.zeros_like(gw_ref)
            gd_ref[...] = jnp.zeros_like(gd_ref)

        gw_ref[...] += _dot(ygb, gtb, TN)
        gd_ref[...] += jnp.sum(gyl * u_ref[...], axis=0, keepdims=True)

    blk = pl.BlockSpec((bl, S5_W), lambda i: (i, 0))
    return pl.pallas_call(
        body, name=name, grid=(L // bl,),
        in_specs=[pl.BlockSpec((bl, D_MODEL), lambda i: (i, 0)), pl.BlockSpec((S5_W, D_MODEL), lambda i: (0, 0)),
                  blk, blk, pl.BlockSpec((1, S5_W), lambda i: (0, 0)), pl.BlockSpec((S5_W, S5_W), lambda i: (0, 0))],
        out_specs=[blk, blk, pl.BlockSpec((S5_W, S5_W), lambda i: (0, 0)), pl.BlockSpec((1, S5_W), lambda i: (0, 0))],
        out_shape=[_sds((L, S5_W)), _sds((L, S5_W)), _sds((S5_W, S5_W)), _sds((1, S5_W))],
        compiler_params=_cp("arbitrary"),
    )(g_y, wout, ylin, z, dvec, wglu)


def _s5_param_bwd(lam_c, ldt_c, b_t, gb, ga_c, gc, name):
    def body(lam_ref, ldt_ref, b_ref, gb_ref, ga_ref, gc_ref, glam_ref, gldt_ref, gbo_ref, gco_ref):
        lr, li = lam_ref[:, 0:1], lam_ref[:, 1:2]
        dt, ar, ai, qr, qi, den = _zoh_cols(lr, li, ldt_ref[...])
        bm = _b_mask()
        gbr = jnp.where(bm, gb_ref[0], 0.0)
        gbi = jnp.where(bm, gb_ref[1], 0.0)
        br, bi = b_ref[0], b_ref[1]
        obr = gbr * qr + gbi * qi
        obi = gbi * qr - gbr * qi
        gqr = jnp.sum(gbr * br + gbi * bi, axis=1, keepdims=True)
        gqi = jnp.sum(gbi * br - gbr * bi, axis=1, keepdims=True)
        for s in (64, 32, 16):
            obr = obr + pltpu.roll(obr, s, 1)
            obi = obi + pltpu.roll(obi, s, 1)
        gbo_ref[0] = obr
        gbo_ref[1] = obi
        gar = ga_ref[:, 0:1] + (gqr * lr - gqi * li) / den
        gai = ga_ref[:, 1:2] + (gqr * li + gqi * lr) / den
        qlr = (qr * lr + qi * li) / den
        qli = (qi * lr - qr * li) / den
        glr = -(gqr * qlr + gqi * qli)
        gli = -(gqi * qlr - gqr * qli)
        glr = glr + dt * (gar * ar + gai * ai)
        gli = gli + dt * (gai * ar - gar * ai)
        wr, wi = _cmul(lr, li, ar, ai)
        gldt = (gar * wr + gai * wi) * dt
        glam_ref[:, 0:1] = glr
        glam_ref[:, 1:2] = gli
        r = lax.broadcasted_iota(jnp.int32, (S5_NS, 32), 0)
        c = lax.broadcasted_iota(jnp.int32, (S5_NS, 32), 1)
        gldt_ref[...] = jnp.sum(jnp.where((r >> 6) == c, gldt, 0.0), axis=0, keepdims=True)
        cm = _c_mask()
        for k in range(2):
            oc = jnp.where(cm, gc_ref[k], 0.0)
            for s in (256, 128, 64):
                oc = oc + pltpu.roll(oc, s, 1)
            gco_ref[k] = oc[:, 0:LANES]

    vm = pl.BlockSpec(memory_space=pltpu.VMEM)
    return pl.pallas_call(
        body, name=name, in_specs=[vm] * 6, out_specs=[vm] * 4,
        out_shape=[_sds((S5_NS, 2)), _sds((1, 32)), _sds((2, S5_NS, LANES)), _sds((2, S5_W, LANES))],
        compiler_params=pltpu.CompilerParams(vmem_limit_bytes=VMEM_LIMIT),
    )(lam_c, ldt_c, b_t, gb, ga_c, gc)


FL_BLK = EVEN_PAD // LANES - 1
Q_BLK, K_BLK, V_BLK = 4, 8, 12
NEG = -1e30


def _log_sigmoid(v):
    return jnp.minimum(v, 0.0) - jnp.log(1.0 + jnp.exp(-jnp.abs(v)))


def _fox_f_fwd(z, bf, name):
    L = z.shape[0]

    def body(fl_ref, b_ref, f_ref, fq_ref):
        row = lax.broadcasted_iota(jnp.int32, (L, LANES), 0)
        cs = _cumsum_rows(_log_sigmoid(fl_ref[...] + b_ref[...]), True, row)
        f_ref[...] = cs
        expand = (lax.broadcasted_iota(jnp.int32, (LANES, FOX_W), 0)
                  == (lax.broadcasted_iota(jnp.int32, (LANES, FOX_W), 1) >> 6)).astype(F32)
        fq_ref[...] = lax.dot_general(cs, expand, NN, precision=lax.Precision.HIGHEST, preferred_element_type=F32)

    return pl.pallas_call(
        body, name=name, grid=(1,),
        in_specs=[pl.BlockSpec((L, LANES), lambda i: (0, FL_BLK)), pl.BlockSpec((1, LANES), lambda i: (0, 0))],
        out_specs=[pl.BlockSpec((L, LANES), lambda i: (0, 0)), pl.BlockSpec((L, FOX_W), lambda i: (0, 0))],
        out_shape=[_sds((L, LANES)), _sds((L, FOX_W))],
        compiler_params=_cp("arbitrary"),
    )(z, bf)


def _fox_f_bwd(dFk, dfq, z, bf, name):
    L = z.shape[0]

    def body(dfk_ref, dfq_ref, fl_ref, b_ref, dfl_ref, db_ref):
        sel = (lax.broadcasted_iota(jnp.int32, (FOX_W, LANES), 0)
               == 64 * lax.broadcasted_iota(jnp.int32, (FOX_W, LANES), 1)).astype(F32)
        dfq_h = lax.dot_general(dfq_ref[...], sel, NN, precision=lax.Precision.HIGHEST, preferred_element_type=F32)
        row = lax.broadcasted_iota(jnp.int32, (L, LANES), 0)
        cs = _cumsum_rows(dfk_ref[...] + dfq_h, False, row)
        dfl = cs * _sigmoid(-(fl_ref[...] + b_ref[...]))
        dfl_ref[...] = dfl
        db_ref[...] = jnp.sum(dfl, axis=0, keepdims=True)

    return pl.pallas_call(
        body, name=name, grid=(1,),
        in_specs=[pl.BlockSpec((L, LANES), lambda i: (0, 0)), pl.BlockSpec((L, FOX_W), lambda i: (0, 0)),
                  pl.BlockSpec((L, LANES), lambda i: (0, FL_BLK)), pl.BlockSpec((1, LANES), lambda i: (0, 0))],
        out_specs=[pl.BlockSpec((L, LANES), lambda i: (0, 0)), pl.BlockSpec((1, LANES), lambda i: (0, 0))],
        out_shape=[_sds((L, LANES)), _sds((1, LANES))],
        compiler_params=_cp("arbitrary"),
    )(dFk, dfq, z, bf)


def _head_mask(hh):
    lane = lax.broadcasted_iota(jnp.int32, (1, LANES), 1)
    return (lane >> 6) == hh


FOX_T = 512


def _fox_head(x, hh):
    return jnp.where(_head_mask(hh), x, 0.0).astype(BF16)


def _fox_scores(qh, k, fq_ref, fr_ref, hh, causal):
    if fq_ref is None:
        s = _dot(qh, k, NT) - fr_ref[hh:hh + 1, :]
    else:
        s = _dot(qh, k, NT) + (fq_ref[:, 64 * hh:64 * hh + 1] - fr_ref[hh:hh + 1, :])
    return s if causal is None else jnp.where(causal, s, NEG)


def _causal(T):
    return lax.broadcasted_iota(jnp.int32, (T, T), 1) <= lax.broadcasted_iota(jnp.int32, (T, T), 0)


def _fox_fwd(z, fq, frow, name):
    L = z.shape[0]
    T = min(FOX_T, L)
    nq = L // T

    def body(qt_ref, kt_ref, q_ref, k_ref, v_ref, fq_ref, fr_ref, o_ref, lse_ref, m_ref, l_ref, acc_ref):
        t = pl.program_id(1)
        qi, ki = qt_ref[t], kt_ref[t]

        @pl.when(ki == 0)
        def _():
            m_ref[...] = jnp.full_like(m_ref, NEG)
            l_ref[...] = jnp.zeros_like(l_ref)
            acc_ref[...] = jnp.zeros_like(acc_ref)

        def step(diagonal):
            q = q_ref[...] * 0.125
            k = k_ref[...].astype(BF16)
            v = v_ref[...].astype(BF16)
            causal = _causal(T) if diagonal else None
            s = jnp.concatenate([_fox_scores(_fox_head(q, hh), k, fq_ref, fr_ref, hh, causal) for hh in range(2)],
                                axis=0)
            m_old = m_ref[...]
            m_new = jnp.maximum(m_old, jnp.max(s, axis=1, keepdims=True))
            alpha = jnp.exp(m_old - m_new)
            p = jnp.exp(s - m_new)
            l_ref[...] = alpha * l_ref[...] + jnp.sum(p, axis=1, keepdims=True)
            m_ref[...] = m_new
            acc_ref[...] = alpha * acc_ref[...] + _dot(p.astype(BF16), v)

        @pl.when(ki < qi)
        def _():
            step(False)

        @pl.when(ki == qi)
        def _():
            step(True)
            h0 = _head_mask(0)
            l = l_ref[...]
            o_h = acc_ref[...] / l
            lse_h = m_ref[...] + jnp.log(l)
            o_ref[...] = jnp.where(h0, o_h[:T], o_h[T:])
            lse_ref[...] = jnp.where(h0, lse_h[:T], lse_h[T:]) - fq_ref[...]

    pairs = [(qi, ki) for qi in range(nq) for ki in range(qi + 1)]
    qt = jnp.asarray([p[0] for p in pairs], jnp.int32)
    kt = jnp.asarray([p[1] for p in pairs], jnp.int32)

    def qspec(base):
        return pl.BlockSpec((T, LANES), lambda j, t, qt, kt: (qt[t], base + j))

    def kspec(base):
        return pl.BlockSpec((T, LANES), lambda j, t, qt, kt: (kt[t], base + j))

    return pl.pallas_call(
        body, name=name,
        grid_spec=pltpu.PrefetchScalarGridSpec(
            num_scalar_prefetch=2, grid=(4, len(pairs)),
            in_specs=[qspec(Q_BLK), kspec(K_BLK), kspec(V_BLK), qspec(0),
                      pl.BlockSpec((None, 2, T), lambda j, t, qt, kt: (j, 0, kt[t]))],
            out_specs=[qspec(0), qspec(0)],
            scratch_shapes=[pltpu.VMEM((2 * T, 1), F32), pltpu.VMEM((2 * T, 1), F32),
                            pltpu.VMEM((2 * T, LANES), F32)]),
        out_shape=[_sds((L, FOX_W)), _sds((L, FOX_W))],
        compiler_params=_cp("parallel", "arbitrary"),
    )(qt, kt, z, z, z, fq, frow)


def _fox_bwd(z, frow, o, lse, g_m, name):
    L = z.shape[0]
    T = min(FOX_T, L)
    nq = L // T

    pairs = [(qi, ki) for ki in range(nq) for qi in range(ki, nq)]
    qt = jnp.asarray([p[0] for p in pairs], jnp.int32)
    kt = jnp.asarray([p[1] for p in pairs], jnp.int32)

    def body(qt_ref, kt_ref, q_ref, k_ref, v_ref, fr_ref, o_ref, lse_ref, do_ref,
             dq_ref, dk_ref, dv_ref, dfq_ref, dfk_ref, dk_acc, dv_acc, df_acc):
        t = pl.program_id(1)
        qi, ki = qt_ref[t], kt_ref[t]

        @pl.when(t == 0)
        def _():
            dq_ref[...] = jnp.zeros_like(dq_ref)
            dfq_ref[...] = jnp.zeros_like(dfq_ref)

        @pl.when(qi == ki)
        def _():
            dk_acc[...] = jnp.zeros_like(dk_acc)
            dv_acc[...] = jnp.zeros_like(dv_acc)
            df_acc[...] = jnp.zeros_like(df_acc)

        def step(diagonal):
            q = q_ref[...] * 0.125
            qb = q.astype(BF16)
            k = k_ref[...].astype(BF16)
            v = v_ref[...].astype(BF16)
            do = do_ref[...]
            dob = do.astype(BF16)
            do_o = dob.astype(F32) * o_ref[...]
            causal = _causal(T) if diagonal else None
            dvs, dks, dqs, rss = [], [], [], []
            for hh in range(2):
                s = _fox_scores(_fox_head(q, hh), k, None, fr_ref, hh, causal)
                p = jnp.exp(s - lse_ref[:, 64 * hh:64 * hh + 1])
                dp = _dot(_fox_head(do, hh), v, NT)
                delta = jnp.sum(jnp.where(_head_mask(hh), do_o, 0.0), axis=1, keepdims=True)
                ds = p * (dp - delta)
                dsb = ds.astype(BF16)
                dvs.append(_dot(p.astype(BF16), dob, TN))
                dks.append(_dot(dsb, qb, TN))
                dqs.append(_dot(dsb, k))
                rss.append(jnp.sum(ds, axis=1, keepdims=True))
                df_acc[hh:hh + 1, :] -= jnp.sum(ds, axis=0, keepdims=True)
            h0 = _head_mask(0)
            dv_acc[...] += jnp.where(h0, dvs[0], dvs[1])
            dk_acc[...] += jnp.where(h0, dks[0], dks[1])
            rows = pl.ds(pl.multiple_of(qi * T, T), T)
            dq_ref[rows, :] += jnp.where(h0, dqs[0], dqs[1])
            dfq_ref[rows, :] += jnp.where(h0, rss[0], rss[1])

        @pl.when(qi > ki)
        def _():
            step(False)

        @pl.when(qi == ki)
        def _():
            step(True)

        @pl.when(qi == nq - 1)
        def _():
            dk_ref[...] = dk_acc[...]
            dv_ref[...] = dv_acc[...]
            dfk_ref[...] = df_acc[...]

        @pl.when(t == len(pairs) - 1)
        def _():
            dq_ref[...] = dq_ref[...] * 0.125

    def qside(base):
        return pl.BlockSpec((T, LANES), lambda j, t, qt, kt: (qt[t], base + j))

    def kside(base):
        return pl.BlockSpec((T, LANES), lambda j, t, qt, kt: (kt[t], base + j))

    pair = pl.BlockSpec((L, LANES), lambda j, t, qt, kt: (0, j))
    frow_spec = pl.BlockSpec((None, 2, T), lambda j, t, qt, kt: (j, 0, kt[t]))
    return pl.pallas_call(
        body, name=name,
        grid_spec=pltpu.PrefetchScalarGridSpec(
            num_scalar_prefetch=2, grid=(4, len(pairs)),
            in_specs=[qside(Q_BLK), kside(K_BLK), kside(V_BLK), frow_spec, qside(0), qside(0), qside(0)],
            out_specs=[pair, kside(0), kside(0), pair, frow_spec],
            scratch_shapes=[pltpu.VMEM((T, LANES), F32), pltpu.VMEM((T, LANES), F32), pltpu.VMEM((2, T), F32)]),
        out_shape=[_sds((L, FOX_W)), _sds((L, FOX_W)), _sds((L, FOX_W)), _sds((L, FOX_W)), _sds((4, 2, L))],
        compiler_params=_cp("parallel", "arbitrary"),
    )(qt, kt, z, z, z, frow, o, lse, g_m)


def _shift_rows(v, s, down, row):
    n = v.shape[0]
    if down:
        return jnp.where(row >= s, pltpu.roll(v, s, 0), 0.0)
    return jnp.where(row < n - s, pltpu.roll(v, n - s, 0), 0.0)


def _cumsum_rows(v, down, row):
    s = 1
    while s < v.shape[0]:
        v = v + _shift_rows(v, s, down, row)
        s *= 2
    return v


def _window_sum(v, g, down, row):
    out = jnp.zeros_like(v)
    s = v
    for k in range(4):
        s = s + _shift_rows(s, 1 << k, down, row)
        out = jnp.where(g == k, s, out)
    return out


def _pool_inv_cnt(g, row):
    w = jnp.left_shift(2, g).astype(F32)
    return 1.0 / jnp.minimum(row.astype(F32) + 1.0, w)


def _pool_fwd(z, pool_w, scale, name):
    L = z.shape[0]

    def body(x_ref, w_ref, s_ref, y_ref, p_ref):
        g = pl.program_id(0)
        row = lax.broadcasted_iota(jnp.int32, (L, LANES), 0)
        x = x_ref[...]
        pooled = (_window_sum(x, g, True, row) * _pool_inv_cnt(g, row) - x).astype(BF16)
        p_ref[...] = pooled
        y_ref[...] = (_dot(pooled, w_ref[...].astype(BF16)) * s_ref[...]).astype(BF16)

    col = pl.BlockSpec((L, LANES), lambda g: (0, g))
    return pl.pallas_call(
        body, name=name, grid=(4,),
        in_specs=[col, pl.BlockSpec((None, LANES, LANES), lambda g: (g, 0, 0)), pl.BlockSpec((1, LANES), lambda g: (0, g))],
        out_specs=[col, col],
        out_shape=[_sds((L, 512), BF16), _sds((L, 512), BF16)],
        compiler_params=_cp("parallel"),
    )(z, pool_w, scale)


def _pool_bwd(g_y, wout, pooled, pool_w, scale, name):
    L = g_y.shape[0]

    def body(g_ref, wo_ref, p_ref, w_ref, s_ref, gx_ref, gw_ref, gs_ref):
        g = pl.program_id(0)
        row = lax.broadcasted_iota(jnp.int32, (L, LANES), 0)
        gy = _dot(g_ref[...], wo_ref[...], NT)
        pooled = p_ref[...]
        wb = w_ref[...].astype(BF16)
        lin = _dot(pooled, wb)
        gs_ref[...] = jnp.sum(gy * lin, axis=0, keepdims=True)
        glin = (gy * s_ref[...]).astype(BF16)
        gw_ref[...] = _dot(pooled, glin, TN)
        gp = _dot(glin, wb, NT)
        gx_ref[...] = _window_sum(gp * _pool_inv_cnt(g, row), g, False, row) - gp

    col = pl.BlockSpec((L, LANES), lambda g: (0, g))
    wspec = pl.BlockSpec((None, LANES, LANES), lambda g: (g, 0, 0))
    vec = pl.BlockSpec((1, LANES), lambda g: (0, g))
    return pl.pallas_call(
        body, name=name, grid=(4,),
        in_specs=[pl.BlockSpec((L, D_MODEL), lambda g: (0, 0)), pl.BlockSpec((LANES, D_MODEL), lambda g: (g, 0)),
                  col, wspec, vec],
        out_specs=[col, wspec, vec],
        out_shape=[_sds((L, 512)), _sds((4, LANES, LANES)), _sds((1, 512))],
        compiler_params=_cp("parallel"),
    )(g_y, wout, pooled, pool_w, scale)


SGU_CHUNKS = 4


def _sgu_ln(v, gam, bet):
    gv = _gelu(v)
    mu = jnp.mean(gv, axis=-1, keepdims=True)
    xc = gv - mu
    rs = lax.rsqrt(jnp.mean(xc * xc, axis=-1, keepdims=True) + EPS)
    xh = xc * rs
    return xh, rs, xh * gam + bet


def _tril_ws(w_ref, g):
    r = lax.broadcasted_iota(jnp.int32, (LANES, LANES), 0)
    c = lax.broadcasted_iota(jnp.int32, (LANES, LANES), 1)
    return jnp.where(r >= c, w_ref[g], 0.0).astype(BF16)


def _sgu_fwd(z, ln_g, ln_b, w_s, b_st, name):
    L = z.shape[0]
    rb = min(SGU_CHUNKS * LANES, L)

    def body(u_ref, v_ref, g_ref, b_ref, w_ref, bs_ref, y_ref):
        _, _, vln = _sgu_ln(v_ref[...], g_ref[...], b_ref[...])
        gu = _gelu(u_ref[...])
        vb = vln.astype(BF16)
        for g in range(4):
            ws = _tril_ws(w_ref, g)
            for n in range(rb // LANES):
                rows = slice(n * LANES, (n + 1) * LANES)
                cols = slice(g * LANES, (g + 1) * LANES)
                mixed = _dot(ws, vb[rows, cols]) + bs_ref[:, g:g + 1]
                y_ref[rows, cols] = (gu[rows, cols] * mixed).astype(BF16)

    vm = lambda shape: pl.BlockSpec(shape, lambda i: tuple(0 for _ in shape))
    return pl.pallas_call(
        body, name=name, grid=(L // rb,),
        in_specs=[pl.BlockSpec((rb, 512), lambda i: (i, 1)), pl.BlockSpec((rb, 512), lambda i: (i, 2)),
                  vm((1, 512)), vm((1, 512)), vm((4, LANES, LANES)), vm((LANES, 4))],
        out_specs=pl.BlockSpec((rb, 512), lambda i: (i, 0)),
        out_shape=_sds((L, 512), BF16),
        compiler_params=_cp("parallel"),
    )(z, z, ln_g, ln_b, w_s, b_st)


def _sgu_bwd(g_y, wout, z, ln_g, ln_b, w_s, b_st, name):
    L = z.shape[0]
    rb = min(SGU_CHUNKS * LANES, L)

    def body(gyo_ref, wo_ref, u_ref, v_ref, g_ref, b_ref, w_ref, bs_ref, gu_ref, gv_ref, gw_ref, gbs_ref, gg_ref,
             gb_ref):
        i = pl.program_id(0)

        @pl.when(i == 0)
        def _():
            gw_ref[...] = jnp.zeros_like(gw_ref)
            gbs_ref[...] = jnp.zeros_like(gbs_ref)
            gg_ref[...] = jnp.zeros_like(gg_ref)
            gb_ref[...] = jnp.zeros_like(gb_ref)

        v = v_ref[...]
        u = u_ref[...]
        gy = _dot(gyo_ref[...], wo_ref[...], NT)
        xh, rs, vln = _sgu_ln(v, g_ref[...], b_ref[...])
        gel_u = _gelu(u)
        gmix = gy * gel_u
        vb = vln.astype(BF16)
        gmb = gmix.astype(BF16)
        r = lax.broadcasted_iota(jnp.int32, (LANES, LANES), 0)
        c = lax.broadcasted_iota(jnp.int32, (LANES, LANES), 1)
        gvln_cols = []
        for g in range(4):
            ws = _tril_ws(w_ref, g)
            cols = slice(g * LANES, (g + 1) * LANES)
            gw = jnp.zeros((LANES, LANES), F32)
            gbs = jnp.zeros((LANES, 1), F32)
            parts = []
            for n in range(rb // LANES):
                rows = slice(n * LANES, (n + 1) * LANES)
                mixed = _dot(ws, vb[rows, cols]) + bs_ref[:, g:g + 1]
                gu_ref[rows, cols] = gy[rows, cols] * mixed * _gelu_grad(u[rows, cols])
                parts.append(_dot(ws, gmb[rows, cols], TN))
                gw = gw + _dot(gmb[rows, cols], vb[rows, cols], NT)
                gbs = gbs + jnp.sum(gmix[rows, cols], axis=1, keepdims=True)
            gvln_cols.append(jnp.concatenate(parts, axis=0))
            gw_ref[g] += jnp.where(r >= c, gw, 0.0)
            gbs_ref[:, g:g + 1] += gbs
        gvln = jnp.concatenate(gvln_cols, axis=1)
        gg_ref[...] += jnp.sum(gvln * xh, axis=0, keepdims=True)
        gb_ref[...] += jnp.sum(gvln, axis=0, keepdims=True)
        gxh = gvln * g_ref[...]
        ggv = rs * (gxh - jnp.mean(gxh, axis=-1, keepdims=True) - xh * jnp.mean(gxh * xh, axis=-1, keepdims=True))
        gv_ref[...] = ggv * _gelu_grad(v)

    vm = lambda shape: pl.BlockSpec(shape, lambda i: tuple(0 for _ in shape))
    blk = pl.BlockSpec((rb, 512), lambda i: (i, 0))
    return pl.pallas_call(
        body, name=name, grid=(L // rb,),
        in_specs=[pl.BlockSpec((rb, D_MODEL), lambda i: (i, 0)), pl.BlockSpec((512, D_MODEL), lambda i: (1, 0)),
                  pl.BlockSpec((rb, 512), lambda i: (i, 1)), pl.BlockSpec((rb, 512), lambda i: (i, 2)),
                  vm((1, 512)), vm((1, 512)), vm((4, LANES, LANES)), vm((LANES, 4))],
        out_specs=[blk, blk, vm((4, LANES, LANES)), vm((LANES, 4)), vm((1, 512)), vm((1, 512))],
        out_shape=[_sds((L, 512)), _sds((L, 512)), _sds((4, LANES, LANES)), _sds((LANES, 4)),
                   _sds((1, 512)), _sds((1, 512))],
        compiler_params=_cp("arbitrary"),
    )(g_y, wout, z, z, ln_g, ln_b, w_s, b_st)


def _adamw_math(w, g, m, v):
    nm = ADAM_B1 * m + (1.0 - ADAM_B1) * g
    nv = ADAM_B2 * v + (1.0 - ADAM_B2) * (g * g)
    m_hat = nm / (1.0 - ADAM_B1 ** ADAM_STEP)
    v_hat = nv / (1.0 - ADAM_B2 ** ADAM_STEP)
    delta = -ADAM_LR * (m_hat / (jnp.sqrt(v_hat) + ADAM_EPS) + ADAM_WD * w)
    return delta, nm, nv


def _sum_adamw(parts, w, m, v, name, layer=0, prev=None):
    n_layers, R, C = w.shape
    rb = 128 if R % 128 == 0 else R

    def body(p_ref, w_ref, m_ref, v_ref, *rest):
        g_ref, d_ref, nm_ref, nv_ref = rest[-4:]
        g = p_ref[0].astype(F32)
        for s in range(1, N_DEV):
            g = g + p_ref[s].astype(F32)
        d, nm, nv = _adamw_math(w_ref[...], g, m_ref[...], v_ref[...])
        g_ref[...] = g
        d_ref[...] = d
        nm_ref[...] = nm
        nv_ref[...] = nv

    blk = pl.BlockSpec((None, rb, C), lambda i: (layer, i, 0))
    prev = [] if prev is None else list(prev)
    return pl.pallas_call(
        body, name=name, grid=(R // rb,),
        in_specs=[pl.BlockSpec((N_DEV, rb, C), lambda i: (0, i, 0)), blk, blk, blk] + [ANY] * len(prev),
        out_specs=[blk] * 4, out_shape=[_sds((n_layers, R, C))] * 4,
        input_output_aliases={4 + k: k for k in range(len(prev))},
        compiler_params=_cp("parallel"),
    )(parts, w, m, v, *prev)


def _sum_adamw_rows(parts, w, m, v, name):
    R, _, C = w.shape

    def body(p_ref, w_ref, m_ref, v_ref, g_ref, d_ref, nm_ref, nv_ref):
        g = p_ref[0].astype(F32)
        for s in range(1, N_DEV):
            g = g + p_ref[s].astype(F32)
        d, nm, nv = _adamw_math(w_ref[:, 0, :], g, m_ref[:, 0, :], v_ref[:, 0, :])
        g_ref[:, 0, :] = g
        d_ref[:, 0, :] = d
        nm_ref[:, 0, :] = nm
        nv_ref[:, 0, :] = nv

    vm = pl.BlockSpec(memory_space=pltpu.VMEM)
    return pl.pallas_call(body, name=name, in_specs=[vm] * 4, out_specs=[vm] * 4, out_shape=[_sds((R, 1, C))] * 4,
                          compiler_params=pltpu.CompilerParams(vmem_limit_bytes=VMEM_LIMIT))(parts, w, m, v)


def _sum_pieces(parts, name):
    _, R, C = parts.shape

    def body(p_ref, g_ref):
        g = p_ref[0].astype(F32)
        for s in range(1, N_DEV):
            g = g + p_ref[s].astype(F32)
        g_ref[...] = g

    vm = pl.BlockSpec(memory_space=pltpu.VMEM)
    return pl.pallas_call(body, name=name, in_specs=[vm], out_specs=vm, out_shape=_sds((R, C)),
                          compiler_params=pltpu.CompilerParams(vmem_limit_bytes=VMEM_LIMIT))(parts)


def _adamw_many(ws, gs, ms, vs, name):
    n = len(ws)
    vm = pl.BlockSpec(memory_space=pltpu.VMEM)

    def body(*refs):
        w_refs, g_refs, m_refs, v_refs = refs[:n], refs[n:2 * n], refs[2 * n:3 * n], refs[3 * n:4 * n]
        d_refs, nm_refs, nv_refs = refs[4 * n:5 * n], refs[5 * n:6 * n], refs[6 * n:7 * n]
        for i in range(n):
            d, nm, nv = _adamw_math(w_refs[i][...], g_refs[i][...], m_refs[i][...], v_refs[i][...])
            d_refs[i][...] = d
            nm_refs[i][...] = nm
            nv_refs[i][...] = nv

    shapes = [_sds(w.shape) for w in ws]
    outs = pl.pallas_call(
        body, name=name, in_specs=[vm] * (4 * n), out_specs=[vm] * (3 * n), out_shape=shapes * 3,
        compiler_params=pltpu.CompilerParams(vmem_limit_bytes=VMEM_LIMIT),
    )(*ws, *gs, *ms, *vs)
    return list(outs[:n]), list(outs[n:2 * n]), list(outs[2 * n:])


def _mesh_pos():
    return lax.axis_index("x"), lax.axis_index("y"), lax.axis_index("c")


def _dev_index(p):
    return 4 * p[0] + 2 * p[1] + p[2]


HBM = pl.BlockSpec(memory_space=pltpu.HBM)
SEM = pl.BlockSpec(memory_space=pltpu.SEMAPHORE)
EFFECT = pltpu.SideEffectType.DATAFLOW_SIDE_EFFECTING


def _peer_list():
    x, y, c = _mesh_pos()
    peers = [(x ^ dx, y ^ dy, c ^ dc) for dx in range(2) for dy in range(2) for dc in range(2)][1:]
    return (x, y, c), peers


def _split_copy(src_ref, land_ref, send_sems, recv_sems, i, k, peer, slot, exchange):
    return pltpu.make_async_remote_copy(
        src_ref=src_ref.at[_dev_index(peer)] if exchange else src_ref, dst_ref=land_ref.at[slot],
        send_sem=send_sems.at[7 * i + k], recv_sem=recv_sems.at[7 * i + k], device_id=peer, device_id_type=MESH)


def _own_copy(src_ref, land_ref, own_sems, i, slot, exchange):
    return pltpu.make_async_copy(src_ref.at[slot] if exchange else src_ref, land_ref.at[slot], own_sems.at[i])


def _comm_start(groups, name, exchange, dep=None):
    sizes = [len(g) for g in groups]
    n = sum(sizes)
    srcs = [a for g in groups for a in g]
    per_group = [exchange] * len(groups) if isinstance(exchange, bool) else list(exchange)
    exchanged = [flag for flag, sz in zip(per_group, sizes) for _ in range(sz)]
    lands = [lax.empty(a.shape if ex else (N_DEV,) + a.shape, a.dtype) for a, ex in zip(srcs, exchanged)]

    n_dep = 0 if dep is None else 1

    def body(*refs):
        src_refs, land_refs = refs[:n], refs[n:2 * n]
        sem_refs = refs[2 * n + n_dep:2 * n + n_dep + 3 * len(sizes)]
        token_ref = refs[-1]
        me, peers = _peer_list()
        mi = _dev_index(me)
        i = 0
        for gi, sz in enumerate(sizes):
            for j in range(sz):
                for k, peer in enumerate(peers):
                    _split_copy(src_refs[i], land_refs[i], sem_refs[3 * gi], sem_refs[3 * gi + 1], j, k, peer, mi,
                                exchanged[i]).start()
                _own_copy(src_refs[i], land_refs[i], sem_refs[3 * gi + 2], j, mi, exchanged[i]).start()
                i += 1
        token_ref[...] = jnp.zeros_like(token_ref)

    sem_shapes = []
    for sz in sizes:
        sem_shapes += [pltpu.SemaphoreType.DMA((7 * sz,)), pltpu.SemaphoreType.DMA((7 * sz,)),
                       pltpu.SemaphoreType.DMA((sz,))]
    thru = [pltpu.HBM(a.shape, a.dtype) for a in srcs + lands]
    n_sem = len(sem_shapes)
    outs = pl.pallas_call(
        body, name=name,
        out_shape=tuple(sem_shapes + thru + [_sds((8, LANES))]),
        in_specs=[HBM] * (2 * n) + [ANY] * n_dep,
        out_specs=tuple([SEM] * n_sem + [HBM] * (2 * n) + [pl.BlockSpec(memory_space=pltpu.VMEM)]),
        input_output_aliases={i: n_sem + i for i in range(2 * n)},
        compiler_params=pltpu.CompilerParams(has_side_effects=EFFECT),
    )(*[pltpu.with_memory_space_constraint(a, pltpu.HBM) for a in srcs + lands], *([] if dep is None else [dep]))
    sems, thru_src, thru_land, token = outs[:n_sem], outs[n_sem:n_sem + n], outs[n_sem + n:n_sem + 2 * n], outs[-1]
    result, off = [], 0
    for gi, sz in enumerate(sizes):
        result.append((*sems[3 * gi:3 * gi + 3], list(thru_src[off:off + sz]), list(thru_land[off:off + sz])))
        off += sz
    return result, token


def _comm_wait(group, after, name, exchange):
    send_sems, recv_sems, own_sems, srcs, lands = group
    n = len(srcs)
    after = list(after) if isinstance(after, (list, tuple)) else [after]

    def body(*refs):
        src_refs, land_refs = refs[:n], refs[n:2 * n]
        ssem, rsem, osem = refs[2 * n:2 * n + 3]
        me, peers = _peer_list()
        for i in range(n):
            for k, peer in enumerate(peers):
                cp = _split_copy(src_refs[i], land_refs[i], ssem, rsem, i, k, peer, _dev_index(peer), exchange)
                cp.wait_send()
                cp.wait_recv()
            _own_copy(src_refs[i], land_refs[i], osem, i, _dev_index(me), exchange).wait()

    outs = pl.pallas_call(
        body, name=name,
        out_shape=tuple(pltpu.HBM(a.shape, a.dtype) for a in srcs + lands),
        in_specs=[HBM] * (2 * n) + [SEM, SEM, SEM] + [ANY] * len(after),
        out_specs=tuple([HBM] * (2 * n)),
        input_output_aliases={i: i for i in range(2 * n)},
        compiler_params=pltpu.CompilerParams(has_side_effects=EFFECT),
    )(*srcs, *lands, send_sems, recv_sems, own_sems, *after)
    return list(outs[n:])


def _tie(a, token):
    return a + token[0, 0].astype(a.dtype)


def _pack(arrs, rows):
    flat = jnp.concatenate([a.reshape(-1).astype(F32) for a in arrs])
    return jnp.pad(flat, (0, rows * LANES - flat.shape[0])).reshape(rows, LANES)


def _unpack(packed, shapes):
    flat = packed.reshape(-1)
    out, off = [], 0
    for s in shapes:
        n = math.prod(s)
        out.append(flat[off:off + n].reshape(s))
        off += n
    return out


def _packed_rows(shapes):
    n = sum(math.prod(s) for s in shapes)
    unit = N_DEV * 8 * LANES
    return -(-n // unit) * unit // LANES


def kernel(x, mix_pre_g, mix_post_g, mlp_pre_g, mlp_post_g, w_in_even, s5_lam_re, s5_lam_im, s5_log_dt, s5_b_re, s5_b_im, s5_c_re, s5_c_im, s5_d, s5_w_glu, fox_b_f, w_out_even, w_in_odd, pool_w, pool_scale, sgu_ln_g, sgu_ln_b, sgu_w_s, sgu_b_s, w_out_odd, mlp_w1, mlp_w2, loss_target, m_mix_pre_g, m_mix_post_g, m_mlp_pre_g, m_mlp_post_g, m_w_in_even, m_s5_lam_re, m_s5_lam_im, m_s5_log_dt, m_s5_b_re, m_s5_b_im, m_s5_c_re, m_s5_c_im, m_s5_d, m_s5_w_glu, m_fox_b_f, m_w_out_even, m_w_in_odd, m_pool_w, m_pool_scale, m_sgu_ln_g, m_sgu_ln_b, m_sgu_w_s, m_sgu_b_s, m_w_out_odd, m_mlp_w1, m_mlp_w2, v_mix_pre_g, v_mix_post_g, v_mlp_pre_g, v_mlp_post_g, v_w_in_even, v_s5_lam_re, v_s5_lam_im, v_s5_log_dt, v_s5_b_re, v_s5_b_im, v_s5_c_re, v_s5_c_im, v_s5_d, v_s5_w_glu, v_fox_b_f, v_w_out_even, v_w_in_odd, v_pool_w, v_pool_scale, v_sgu_ln_g, v_sgu_ln_b, v_sgu_w_s, v_sgu_b_s, v_w_out_odd, v_mlp_w1, v_mlp_w2):
    weights = dict(mix_pre_g=mix_pre_g, mix_post_g=mix_post_g, mlp_pre_g=mlp_pre_g, mlp_post_g=mlp_post_g, w_in_even=w_in_even, s5_lam_re=s5_lam_re, s5_lam_im=s5_lam_im, s5_log_dt=s5_log_dt, s5_b_re=s5_b_re, s5_b_im=s5_b_im, s5_c_re=s5_c_re, s5_c_im=s5_c_im, s5_d=s5_d, s5_w_glu=s5_w_glu, fox_b_f=fox_b_f, w_out_even=w_out_even, w_in_odd=w_in_odd, pool_w=pool_w, pool_scale=pool_scale, sgu_ln_g=sgu_ln_g, sgu_ln_b=sgu_ln_b, sgu_w_s=sgu_w_s, sgu_b_s=sgu_b_s, w_out_odd=w_out_odd, mlp_w1=mlp_w1, mlp_w2=mlp_w2)
    mom_m = dict(mix_pre_g=m_mix_pre_g, mix_post_g=m_mix_post_g, mlp_pre_g=m_mlp_pre_g, mlp_post_g=m_mlp_post_g, w_in_even=m_w_in_even, s5_lam_re=m_s5_lam_re, s5_lam_im=m_s5_lam_im, s5_log_dt=m_s5_log_dt, s5_b_re=m_s5_b_re, s5_b_im=m_s5_b_im, s5_c_re=m_s5_c_re, s5_c_im=m_s5_c_im, s5_d=m_s5_d, s5_w_glu=m_s5_w_glu, fox_b_f=m_fox_b_f, w_out_even=m_w_out_even, w_in_odd=m_w_in_odd, pool_w=m_pool_w, pool_scale=m_pool_scale, sgu_ln_g=m_sgu_ln_g, sgu_ln_b=m_sgu_ln_b, sgu_w_s=m_sgu_w_s, sgu_b_s=m_sgu_b_s, w_out_odd=m_w_out_odd, mlp_w1=m_mlp_w1, mlp_w2=m_mlp_w2)
    mom_v = dict(mix_pre_g=v_mix_pre_g, mix_post_g=v_mix_post_g, mlp_pre_g=v_mlp_pre_g, mlp_post_g=v_mlp_post_g, w_in_even=v_w_in_even, s5_lam_re=v_s5_lam_re, s5_lam_im=v_s5_lam_im, s5_log_dt=v_s5_log_dt, s5_b_re=v_s5_b_re, s5_b_im=v_s5_b_im, s5_c_re=v_s5_c_re, s5_c_im=v_s5_c_im, s5_d=v_s5_d, s5_w_glu=v_s5_w_glu, fox_b_f=v_fox_b_f, w_out_even=v_w_out_even, w_in_odd=v_w_in_odd, pool_w=v_pool_w, pool_scale=v_pool_scale, sgu_ln_g=v_sgu_ln_g, sgu_ln_b=v_sgu_ln_b, sgu_w_s=v_sgu_w_s, sgu_b_s=v_sgu_b_s, w_out_odd=v_w_out_odd, mlp_w1=v_mlp_w1, mlp_w2=v_mlp_w2)
    names = list(weights)
    L = x.shape[1]
    x0 = x[0]
    target = loss_target[0]
    my_index = 4 * lax.axis_index("x") + 2 * lax.axis_index("y") + lax.axis_index("c")

    small_vec = jnp.zeros((8, LANES), F32)
    small_vec = small_vec.at[0, :64].set(pool_scale[0]).at[1, :64].set(sgu_ln_g[0]).at[2, :64].set(sgu_ln_b[0])
    ag_groups, ag_token = _comm_start(
        [[jnp.transpose(w_in_even[0]).astype(BF16), small_vec],
         [s5_w_glu[0].astype(BF16), w_out_even[0].astype(BF16)],
         [mlp_w1[0].astype(BF16), mlp_w2[0].astype(BF16)],
         [jnp.transpose(w_in_odd[0]).astype(BF16), w_out_odd[0].astype(BF16), mlp_w1[1].astype(BF16), mlp_w2[1].astype(BF16)]],
        "ag_start", exchange=False)

    lam_r = jnp.concatenate([s5_lam_re.reshape(1, S5_NS), s5_lam_im.reshape(1, S5_NS)], axis=0)
    ldt_r = jnp.repeat(s5_log_dt.reshape(32), 64).reshape(1, S5_NS)
    lam_c = jnp.transpose(lam_r)
    ldt_c = jnp.transpose(ldt_r)
    b_t = jnp.stack([jnp.tile(s5_b_re.reshape(S5_NS, 16), (1, 8)), jnp.tile(s5_b_im.reshape(S5_NS, 16), (1, 8))])
    c_t = jnp.stack([jnp.tile(s5_c_re.reshape(S5_W, 64), (1, 8)), jnp.tile(s5_c_im.reshape(S5_W, 64), (1, 8))])
    bf_pad = jnp.pad(fox_b_f, ((0, 0), (0, LANES - 8)))
    b_st = jnp.transpose(sgu_b_s[0])

    h0, rx0 = _rms_fwd(x0, _tie(mix_pre_g[0:1], ag_token), "rms0")
    tabs, bset, cset = _s5_prep(lam_r, ldt_r, lam_c, ldt_c, b_t, c_t, "s5_prep")
    ag0 = _comm_wait(ag_groups[0], tabs, "ag_wait0", exchange=False)
    winT_e = jnp.pad(ag0[0].reshape(EVEN_IN, D_MODEL), ((0, EVEN_PAD - EVEN_IN), (0, 0)))
    pool_scale_f = ag0[1][:, 0, :64].reshape(1, 512)
    ln_g_f = ag0[1][:, 1, :64].reshape(1, 512)
    ln_b_f = ag0[1][:, 2, :64].reshape(1, 512)
    z0 = _mm(h0, winT_e, name="win_even", tb=True, bm=512, bn=EVEN_PAD)
    xs, ylin = _s5_scan_fwd(z0, bset, cset, s5_d, tabs, "s5_scan")
    ag1 = _comm_wait(ag_groups[1], ylin, "ag_wait1", exchange=False)
    wglu = ag1[0].reshape(S5_W, S5_W)
    wout_e = ag1[1].reshape(D_MODEL, D_MODEL)
    ya = _s5_glu_fwd(ylin, wglu, "s5_glu")
    fcum, fq = _fox_f_fwd(z0, bf_pad, "fox_f")
    frow = jnp.transpose(fcum[:, :8]).reshape(4, 2, L)
    o_att, lse = _fox_fwd(z0, fq, frow, "fox_fwd")
    mix0 = [ya, o_att]
    x1, ry0, h1, rx1, y0 = _mm(mix0, wout_e, name="wout_even", epi=_epi_post_pre, extra=(x0,),
                               vecs=(mix_post_g[0:1], mlp_pre_g[0:1]), out_dtypes=POST_PRE_DTYPES,
                               out_kinds=POST_PRE_KINDS, bm=FUSED_ROWS)
    ag2 = _comm_wait(ag_groups[2], rx1, "ag_wait2", exchange=False)
    w1 = [ag2[0], None]
    w2 = [ag2[1].reshape(4 * D_MODEL, D_MODEL), None]
    p0, a0 = _mm(h1, w1[0], name="mlp0_w1", b3=True, out_dtypes=(BF16, BF16), epi=_epi_relu2, bm=512, bn=4 * D_MODEL)
    x2, ro0, h2, rx2, o0 = _mm(a0, w2[0], name="mlp0_w2", epi=_epi_post_pre, extra=(x1,),
                               vecs=(mlp_post_g[0:1], mix_pre_g[1:2]), out_dtypes=POST_PRE_DTYPES,
                               out_kinds=POST_PRE_KINDS, bm=FUSED_ROWS, bk=4 * D_MODEL)
    ag3 = _comm_wait(ag_groups[3], rx2, "ag_wait3", exchange=False)
    winT_o = ag3[0].reshape(ODD_IN, D_MODEL)
    wout_o = ag3[1].reshape(D_MODEL, D_MODEL)
    w1[1] = ag3[2]
    w2[1] = ag3[3].reshape(4 * D_MODEL, D_MODEL)
    z1 = _mm(h2, winT_o, name="win_odd", tb=True, bn=ODD_IN)
    yc, pooled = _pool_fwd(z1, pool_w[0], pool_scale_f, "pool_fwd")
    yd = _sgu_fwd(z1, ln_g_f, ln_b_f, sgu_w_s[0], b_st, "sgu_fwd")
    mix1 = [yc, yd]
    x3, ry1, h3, rx3, y1 = _mm(mix1, wout_o, name="wout_odd", epi=_epi_post_pre, extra=(x2,),
                               vecs=(mix_post_g[1:2], mlp_pre_g[1:2]), out_dtypes=POST_PRE_DTYPES,
                               out_kinds=POST_PRE_KINDS, bm=FUSED_ROWS)
    p1, a1 = _mm(h3, w1[1], name="mlp1_w1", b3=True, out_dtypes=(BF16, BF16), epi=_epi_relu2, bm=512, bn=4 * D_MODEL)
    gx4, g_o1, gg_mlp_post1, sq_lanes = _mm(
        a1, w2[1], name="mlp1_w2", epi=_epi_post_loss, extra=(x3, target), vecs=(mlp_post_g[1:2],),
        out_dtypes=(F32, BF16, F32, F32), out_kinds=("full", "full", "vsum", "vsum"), bm=FUSED_ROWS, bk=4 * D_MODEL)
    sq = sq_lanes[:, 0:1]

    g_p1 = _mm(g_o1, w2[1], name="b_mlp1_a", tb=True, out_dtypes=(BF16,), epi=_epi_relu2_bwd, extra=(p1,),
               bm=512, bn=4 * D_MODEL)
    gw2_1 = _mm(a1, g_o1, name="b_mlp1_w2", ta=True, bm=512, bk=L)
    gw1_1 = _mm(h3, g_p1, name="b_mlp1_w1", ta=True, out3=True, bn=512, bk=L)
    (ex1,), tok1 = _comm_start([[gw1_1, gw2_1.reshape(N_DEV, 512, D_MODEL)]], "ex_start1", exchange=True)
    g_x3, gg_mlp_pre1, g_y1, gg_mix_post1 = _mm(
        g_p1, w1[1], name="b_mlp1_h", tb=True, b3=True, epi=_epi_pre_post_bwd, extra=(x3, gx4, y1), cols=(rx3, ry1),
        vecs=(_tie(mlp_pre_g[1:2], tok1), mix_post_g[1:2]), out_dtypes=PRE_POST_BWD_DTYPES,
        out_kinds=PRE_POST_BWD_KINDS, bm=FUSED_ROWS, bk=4 * D_MODEL)
    gwout_o = _mm(mix1, g_y1, name="b_wout_odd_w", ta=True)
    g_xc, g_pool_w, g_pool_scale = _pool_bwd(g_y1, wout_o, pooled, pool_w[0], pool_scale_f, "pool_bwd")
    g_u1, g_v1, g_ws, g_bst, g_ln_g, g_ln_b = _sgu_bwd(g_y1, wout_o, z1, ln_g_f, ln_b_f, sgu_w_s[0], b_st,
                                                       "sgu_bwd")
    g_z1 = [g_xc, g_u1, g_v1]
    gwinT_o = _mm(g_z1, h2, name="b_win_odd_w", ta=True)
    (ex2,), tok2 = _comm_start([[gwout_o.reshape(N_DEV, 128, D_MODEL), gwinT_o.reshape(N_DEV, ODD_IN // N_DEV, D_MODEL)]], "ex_start2", exchange=True)
    g_x2, gg_mix_pre1, g_o0, gg_mlp_post0 = _mm(
        g_z1, winT_o, name="b_win_odd_h", epi=_epi_pre_post_bwd, extra=(x2, g_x3, o0), cols=(rx2, ro0),
        vecs=(_tie(mix_pre_g[1:2], tok2), mlp_post_g[0:1]), out_dtypes=PRE_POST_BWD_DTYPES,
        out_kinds=PRE_POST_BWD_KINDS, bm=FUSED_ROWS)
    g_p0 = _mm(g_o0, w2[0], name="b_mlp0_a", tb=True, out_dtypes=(BF16,), epi=_epi_relu2_bwd, extra=(p0,),
               bm=512, bn=4 * D_MODEL)
    gw2_0 = _mm(a0, g_o0, name="b_mlp0_w2", ta=True, bm=512, bk=L)
    gw1_0 = _mm(h1, g_p0, name="b_mlp0_w1", ta=True, out3=True, bn=512, bk=L)
    (ex3,), tok3 = _comm_start([[gw1_0, gw2_0.reshape(N_DEV, 512, D_MODEL)]], "ex_start3", exchange=True)
    g_x1, gg_mlp_pre0, g_y0, gg_mix_post0 = _mm(
        g_p0, w1[0], name="b_mlp0_h", tb=True, b3=True, epi=_epi_pre_post_bwd, extra=(x1, g_x2, y0), cols=(rx1, ry0),
        vecs=(_tie(mlp_pre_g[0:1], tok3), mix_post_g[0:1]), out_dtypes=PRE_POST_BWD_DTYPES,
        out_kinds=PRE_POST_BWD_KINDS, bm=FUSED_ROWS, bk=4 * D_MODEL)
    g_o_att = _mm(g_y0, wout_e[FOX_W:], name="b_wout_even_m", tb=True)
    gwout_e = _mm(mix0, g_y0, name="b_wout_even_w", ta=True)
    gyl, gud, g_wglu, g_d = _s5_glu_bwd(g_y0, wout_e, ylin, z0, s5_d, wglu, "s5_glu_bwd")
    (ex4,), tok4 = _comm_start([[gwout_e.reshape(N_DEV, 128, D_MODEL), g_wglu.reshape(N_DEV, 64, S5_W)]], "ex_start4", exchange=True)
    g_u0, ga, gb_raw, gc_raw = _s5_scan_bwd(gyl, _tie(cset, tok4), xs, z0, bset, gud, tabs, "s5_scan_bwd")
    g_lam, g_ldt, g_b, g_c = _s5_param_bwd(lam_c, ldt_c, b_t, gb_raw, jnp.transpose(ga), gc_raw, "s5_param_bwd")
    dq, dk, dv, dfq, dfrow = _fox_bwd(z0, frow, o_att, lse, g_o_att, "fox_bwd")
    dFk = jnp.pad(jnp.transpose(dfrow.reshape(8, L)), ((0, 0), (0, LANES - 8)))
    dfl, db_f = _fox_f_bwd(dFk, dfq, z0, bf_pad, "fox_f_bwd")
    g_z0 = [g_u0, dq, dk, dv, dfl]
    grad_x, gg_mix_pre0 = _mm(g_z0, winT_e, name="b_win_even_h", epi=_epi_pre_bwd, extra=(x0, g_x1), cols=(rx0,),
                              vecs=(mix_pre_g[0:1],), out_dtypes=(F32, F32), out_kinds=("full", "vsum"),
                              bm=FUSED_ROWS)

    small_grads = dict(
        mix_pre_g=jnp.concatenate([gg_mix_pre0, gg_mix_pre1]), mix_post_g=jnp.concatenate([gg_mix_post0, gg_mix_post1]),
        mlp_pre_g=jnp.concatenate([gg_mlp_pre0, gg_mlp_pre1]), mlp_post_g=jnp.concatenate([gg_mlp_post0, gg_mlp_post1]),
        s5_lam_re=g_lam[:, 0], s5_lam_im=g_lam[:, 1],
        s5_b_re=g_b[0, :, :16], s5_b_im=g_b[1, :, :16], s5_c_re=g_c[0, :, :64], s5_c_im=g_c[1, :, :64],
        pool_w=g_pool_w, sgu_w_s=g_ws, s5_d=g_d, sgu_b_s=jnp.transpose(g_bst),
        pool_scale=g_pool_scale, sgu_ln_g=g_ln_g, sgu_ln_b=g_ln_b, s5_log_dt=g_ldt, fox_b_f=db_f[:, :8])
    small_names = list(small_grads)
    full_shapes = [(512,) if nm in ("pool_scale", "sgu_ln_g", "sgu_ln_b") else weights[nm].shape for nm in small_names]
    full_shapes.append((1, 1))
    rows = _packed_rows(full_shapes)
    packed = _pack([small_grads[nm] for nm in small_names] + [sq], rows).reshape(N_DEV, rows // N_DEV, LANES)
    (exs,), tok_s = _comm_start([[packed]], "exs_start", exchange=True)
    gwinT_e = _mm(g_z0, h0, name="b_win_even_w", ta=True, out_dtypes=(BF16,), dep=tok_s)
    (recv_small,) = _comm_wait(exs, gwinT_e, "exs_wait", exchange=True)
    piece = _sum_pieces(recv_small, "sum_small")
    gwinT_e_pieces = gwinT_e[:EVEN_IN].reshape(N_DEV, EVEN_IN // N_DEV, D_MODEL)
    (ags, ex5), tok5 = _comm_start([[piece], [gwinT_e_pieces]], "ags_ex_start5", exchange=(False, True))
    r_w1_1, r_w2_1 = _comm_wait(ex1, tok5, "ex_wait1", exchange=True)
    r_wout_o, r_win_o = _comm_wait(ex2, tok5, "ex_wait2", exchange=True)
    r_w1_0, r_w2_0 = _comm_wait(ex3, tok5, "ex_wait3", exchange=True)
    r_wout_e, r_wglu = _comm_wait(ex4, tok5, "ex_wait4", exchange=True)

    res = {}
    for nm, parts in (("mlp_w1", (r_w1_0, r_w1_1)), ("mlp_w2", (r_w2_0, r_w2_1))):
        first = _sum_adamw(parts[0], weights[nm], mom_m[nm], mom_v[nm], "adamw_%s_0" % nm, layer=0)
        res[nm] = tuple(_sum_adamw(parts[1], weights[nm], mom_m[nm], mom_v[nm], "adamw_%s_1" % nm, layer=1, prev=first))
    big_parts = dict(s5_w_glu=r_wglu, w_out_even=r_wout_e, w_out_odd=r_wout_o)
    for nm, parts in big_parts.items():
        res[nm] = tuple(_sum_adamw(parts, weights[nm], mom_m[nm], mom_v[nm], "adamw_" + nm))
    done = [res[nm][1] for nm in ("mlp_w1", "mlp_w2", "s5_w_glu", "w_out_even", "w_out_odd")]

    (small_all,) = _comm_wait(ags, done, "ags_wait", exchange=False)
    small_full = _unpack(small_all.reshape(rows, LANES), full_shapes)
    loss = 0.5 * small_full.pop()[0, 0] / D_MODEL
    small_g = []
    for nm, g in zip(small_names, small_full):
        if nm in ("pool_scale", "sgu_ln_g", "sgu_ln_b"):
            g = lax.dynamic_slice(g, (my_index * 64,), (64,)).reshape(1, 64)
        small_g.append(g)

    def turned(arrs):
        return [jnp.swapaxes(a, -1, -2) if nm in ("s5_b_re", "s5_b_im") else a for nm, a in zip(small_names, arrs)]

    sd, sm, sv = _adamw_many(turned([weights[nm] for nm in small_names]), turned(small_g),
                             turned([mom_m[nm] for nm in small_names]), turned([mom_v[nm] for nm in small_names]),
                             "adamw_small")
    for nm, g_, d_, m_, v_ in zip(small_names, small_g, turned(sd), turned(sm), turned(sv)):
        res[nm] = (g_, d_, m_, v_)
    done.append(sd[0])

    nm = "w_in_odd"
    outs = _sum_adamw(r_win_o, jnp.transpose(weights[nm], (0, 2, 1)), jnp.transpose(mom_m[nm], (0, 2, 1)),
                      jnp.transpose(mom_v[nm], (0, 2, 1)), "adamw_" + nm)
    res[nm] = tuple(jnp.transpose(o, (0, 2, 1)) for o in outs)
    done.append(outs[1])
    nm = "w_in_even"
    (r_win_e,) = _comm_wait(ex5, done, "ex_wait5", exchange=True)
    outs = _sum_adamw_rows(r_win_e, jnp.transpose(weights[nm], (2, 0, 1)), jnp.transpose(mom_m[nm], (2, 0, 1)),
                           jnp.transpose(mom_v[nm], (2, 0, 1)), "adamw_" + nm)
    res[nm] = tuple(jnp.transpose(o, (1, 2, 0)) for o in outs)

    grads = [res[nm][0].reshape(weights[nm].shape) for nm in names]
    deltas = [res[nm][1].reshape(weights[nm].shape) for nm in names]
    new_m = [res[nm][2].reshape(weights[nm].shape) for nm in names]
    new_v = [res[nm][3].reshape(weights[nm].shape) for nm in names]
    return (loss, grad_x[None], *grads, *deltas, *new_m, *new_v)
```

```python
import math

import jax
import jax.numpy as jnp
from jax import lax
from jax.experimental import pallas as pl
from jax.experimental.pallas import tpu as pltpu

F32 = jnp.float32
BF16 = jnp.bfloat16
MESH = pl.DeviceIdType.MESH
ANY = pl.BlockSpec(memory_space=pl.ANY)

N_DEV = 8
D_MODEL = 1024
EPS = 1e-6
NORM_ROWS = 512
FUSED_ROWS = 512
S5_W = 512
S5_NS = 2048
SCAN_GROUPS = 4
SCAN_CHUNK = 1024
FOX_W = 512
EVEN_IN = 2056
EVEN_PAD = 2176
ODD_IN = 1536
LANES = 128
PIECE = 4 * D_MODEL // N_DEV
VMEM_LIMIT = 56 * 1024 * 1024

ADAM_LR = 0.001
ADAM_B1 = 0.9
ADAM_B2 = 0.999
ADAM_EPS = 1e-08
ADAM_WD = 0.01
ADAM_STEP = 10

NT = (((1,), (1,)), ((), ()))
TN = (((0,), (0,)), ((), ()))
NN = (((1,), (0,)), ((), ()))


def _cp(*sem):
    return pltpu.CompilerParams(dimension_semantics=sem, vmem_limit_bytes=VMEM_LIMIT)


def _sds(shape, dtype=F32):
    return jax.ShapeDtypeStruct(tuple(shape), dtype)


def _gelu(x):
    t = jnp.tanh(0.7978845608028654 * (x + 0.044715 * x * x * x))
    return 0.5 * x * (1.0 + t)


def _gelu_grad(x):
    t = jnp.tanh(0.7978845608028654 * (x + 0.044715 * x * x * x))
    du = 0.7978845608028654 * (1.0 + 3.0 * 0.044715 * x * x)
    return 0.5 * (1.0 + t) + 0.5 * x * (1.0 - t * t) * du


def _sigmoid(x):
    return 1.0 / (1.0 + jnp.exp(-x))


def _dot(a, b, dn=NN):
    return lax.dot_general(a, b, dn, preferred_element_type=F32)


def _mm(a, b, *, name, ta=False, tb=False, b3=False, out3=False, out_dtypes=(F32,), epi=None, extra=(),
        cols=(), vecs=(), out_kinds=None, bm=1024, bn=1024, bk=1024, dep=None):
    a_list = list(a) if isinstance(a, (list, tuple)) else [a]
    widths = [p.shape[1] for p in a_list]
    offs = [sum(widths[:i]) for i in range(len(widths))]
    na = len(a_list)
    M = sum(widths) if ta else a_list[0].shape[0]
    K = a_list[0].shape[0] if ta else sum(widths)
    if na > 1:
        assert not b3 and not tb
        bm, bk = (M, bk) if ta else (bm, K)
    pw = b.shape[2] if b3 else PIECE
    if b3:
        N = b.shape[1] if tb else b.shape[0] * pw
        assert (b.shape[0] * pw if tb else b.shape[1]) == K
    else:
        N = b.shape[0] if tb else b.shape[1]
    bm, bn, bk = min(bm, M), min(bn, N), min(bk, K)
    assert M % bm == 0 and N % bn == 0 and K % bk == 0, (name, M, N, K, bm, bn, bk)
    assert not (b3 or out3) or ((bk if tb else bn) % pw == 0 and bn % PIECE == 0)
    nk = K // bk
    n_extra = len(extra) + len(cols) + len(vecs)
    n_out = len(out_dtypes)
    out_kinds = tuple(out_kinds) if out_kinds is not None else ("full",) * n_out
    dn = (((0 if ta else 1,), (1 if tb else 0,)), ((), ()))

    use_acc = nk > 1

    def body(*refs):
        a_refs, b_ref = refs[:na], refs[na]
        a_ref = a_refs[0]
        e_refs = refs[na + 1:na + 1 + n_extra]
        first_out = na + 1 + n_extra + (0 if dep is None else 1)
        o_refs = refs[first_out:first_out + n_out]
        acc_ref = refs[-1] if use_acc else o_refs[0]
        i, k = pl.program_id(0), pl.program_id(2)

        def dot(a_v, b_v):
            return lax.dot_general(a_v.astype(BF16), b_v.astype(BF16), dn, preferred_element_type=F32)

        everything = slice(None)
        if na > 1 and ta:
            terms = [(pl.ds(off, w), everything, r, b_ref) for r, off, w in zip(a_refs, offs, widths)]
        elif na > 1:
            terms = [(everything, everything, r, b_ref.at[pl.ds(off, w), :]) for r, off, w in zip(a_refs, offs, widths)]
        elif not b3:
            terms = [(everything, everything, a_ref, b_ref)]
        elif tb:
            terms = [(everything, everything,
                      a_ref.at[pl.ds(t * pw, pw), :] if ta else a_ref.at[:, pl.ds(t * pw, pw)], b_ref.at[t])
                     for t in range(bk // pw)]
        else:
            terms = [(everything, pl.ds(t * pw, pw), a_ref, b_ref.at[t]) for t in range(bn // pw)]

        def finish(acc):
            outs = (acc,) if epi is None else epi(acc, *[e[...] for e in e_refs])
            for o_ref, o, kind in zip(o_refs, outs, out_kinds):
                if kind == "vsum":
                    @pl.when(i == 0)
                    def _(o_ref=o_ref, o=o):
                        o_ref[...] = o

                    @pl.when(i > 0)
                    def _(o_ref=o_ref, o=o):
                        o_ref[...] += o
                elif out3:
                    for t in range(bn // PIECE):
                        o_ref[t] = o[:, t * PIECE:(t + 1) * PIECE].astype(o_ref.dtype)
                else:
                    o_ref[...] = o.astype(o_ref.dtype)

        if nk == 1:
            bands = {}
            for rows, cols, a_r, b_r in terms:
                key = (getattr(rows, "start", None), getattr(cols, "start", None))
                val = dot(a_r[...], b_r[...])
                bands[key] = val if key not in bands else bands[key] + val
            vals = list(bands.values())
            if len(vals) == 1:
                finish(vals[0])
            else:
                finish(jnp.concatenate(vals, axis=0 if (na > 1 and ta) else 1))
            return

        @pl.when(k == 0)
        def _():
            acc_ref[...] = jnp.zeros_like(acc_ref)

        for rows, cols, a_r, b_r in terms:
            acc_ref[rows, cols] += dot(a_r[...], b_r[...])

        @pl.when(k == nk - 1)
        def _():
            finish(acc_ref[...])

    if na > 1:
        a_specs = [pl.BlockSpec((bk, w), lambda i, j, k: (k, 0)) if ta else pl.BlockSpec((bm, w), lambda i, j, k: (i, 0))
                   for w in widths]
    else:
        a_specs = [pl.BlockSpec((bk, bm), lambda i, j, k: (k, i)) if ta else
                   pl.BlockSpec((bm, bk), lambda i, j, k: (i, k))]
    if b3:
        if tb:
            b_spec = pl.BlockSpec((bk // pw, bn, pw), lambda i, j, k: (k, j, 0))
        else:
            b_spec = pl.BlockSpec((bn // pw, bk, pw), lambda i, j, k: (j, k, 0))
    else:
        b_spec = pl.BlockSpec((bn, bk), lambda i, j, k: (j, k)) if tb else pl.BlockSpec((bk, bn), lambda i, j, k: (k, j))
    e_specs = ([pl.BlockSpec((bm, bn), lambda i, j, k: (i, j)) for _ in extra]
               + [pl.BlockSpec((bm, 1), lambda i, j, k: (i, 0)) for _ in cols]
               + [pl.BlockSpec((1, bn), lambda i, j, k: (0, j)) for _ in vecs])
    if out3:
        o_specs = [pl.BlockSpec((bn // PIECE, bm, PIECE), lambda i, j, k: (j, i, 0)) for _ in out_dtypes]
        o_shapes = [_sds((N // PIECE, M, PIECE), dt) for dt in out_dtypes]
    else:
        spec_of = {"full": pl.BlockSpec((bm, bn), lambda i, j, k: (i, j)),
                   "col": pl.BlockSpec((bm, 1), lambda i, j, k: (i, 0)),
                   "vsum": pl.BlockSpec((1, bn), lambda i, j, k: (0, j))}
        shape_of = {"full": (M, N), "col": (M, 1), "vsum": (1, N)}
        o_specs = [spec_of[kind] for kind in out_kinds]
        o_shapes = [_sds(shape_of[kind], dt) for kind, dt in zip(out_kinds, out_dtypes)]
    assert "col" not in out_kinds or bn == N
    outs = pl.pallas_call(
        body, name=name, grid=(M // bm, N // bn, nk),
        in_specs=a_specs + [b_spec] + e_specs + ([] if dep is None else [ANY]),
        out_specs=o_specs, out_shape=o_shapes,
        scratch_shapes=[pltpu.VMEM((bm, bn), F32)] if use_acc else [],
        compiler_params=_cp("arbitrary" if "vsum" in out_kinds else "parallel", "parallel", "arbitrary"),
    )(*a_list, b, *extra, *cols, *vecs, *([] if dep is None else [dep]))
    return outs[0] if n_out == 1 else outs


def _epi_relu2(acc):
    r = jnp.maximum(acc, 0.0)
    return acc, r * r


def _epi_relu2_bwd(acc, p):
    return (acc * (2.0 * jnp.maximum(p.astype(F32), 0.0)),)


def _row_spec(rb, w=D_MODEL):
    return pl.BlockSpec((rb, w), lambda i: (i, 0))


def _vec_spec(w=D_MODEL):
    return pl.BlockSpec((1, w), lambda i: (0, 0))


def _rstd(v):
    return lax.rsqrt(jnp.mean(v * v, axis=-1, keepdims=True) + EPS)


def _rms_fwd(x, g, name):
    L = x.shape[0]
    rb = min(NORM_ROWS, L)

    def body(x_ref, g_ref, h_ref, r_ref):
        xv = x_ref[...]
        r = _rstd(xv)
        h_ref[...] = (xv * r * g_ref[...]).astype(BF16)
        r_ref[...] = r

    return pl.pallas_call(
        body, name=name, grid=(L // rb,),
        in_specs=[_row_spec(rb), _vec_spec()],
        out_specs=[_row_spec(rb), _row_spec(rb, 1)],
        out_shape=[_sds((L, D_MODEL), BF16), _sds((L, 1))],
        compiler_params=_cp("parallel"),
    )(x, g)


def _rms_bwd_rows(dy, xv, r, g):
    n = xv * r
    dyg = dy * g
    return r * (dyg - n * jnp.mean(dyg * n, axis=-1, keepdims=True)), n


POST_PRE_DTYPES = (F32, F32, BF16, F32, F32)
POST_PRE_KINDS = ("full", "col", "full", "col", "full")
PRE_POST_BWD_DTYPES = (F32, F32, BF16, F32)
PRE_POST_BWD_KINDS = ("full", "vsum", "full", "vsum")


def _epi_post_pre(y, x_in, g_post, g_pre):
    ry = _rstd(y)
    xo = x_in + y * ry * g_post
    rx = _rstd(xo)
    return xo, ry, xo * rx * g_pre, rx, y


def _epi_pre_post_bwd(gh, x, g_out, y_prev, rx, ry_prev, g_pre, g_post_prev):
    gx, n = _rms_bwd_rows(gh, x, rx, g_pre)
    gi = g_out + gx
    gy, ny = _rms_bwd_rows(gi, y_prev, ry_prev, g_post_prev)
    return gi, jnp.sum(gh * n, axis=0, keepdims=True), gy, jnp.sum(gi * ny, axis=0, keepdims=True)


def _epi_pre_bwd(gh, x, g_out, rx, g_pre):
    gx, n = _rms_bwd_rows(gh, x, rx, g_pre)
    return g_out + gx, jnp.sum(gh * n, axis=0, keepdims=True)


def _epi_post_loss(y, x_in, target, g_post):
    ry = _rstd(y)
    diff = x_in + y * ry * g_post - target
    gx = diff * (1.0 / D_MODEL)
    gy, n = _rms_bwd_rows(gx, y, ry, g_post)
    sq = jnp.broadcast_to(jnp.sum(diff * diff, keepdims=True), (1, y.shape[1]))
    return gx, gy, jnp.sum(gx * n, axis=0, keepdims=True), sq


def _cmul(ar, ai, br, bi):
    return ar * br - ai * bi, ar * bi + ai * br


def _zoh_cols(lr, li, ldt):
    dt = jnp.exp(ldt)
    mag = jnp.exp(lr * dt)
    ar = mag * jnp.cos(li * dt)
    ai = mag * jnp.sin(li * dt)
    den = lr * lr + li * li
    nr = ar - 1.0
    qr = (nr * lr + ai * li) / den
    qi = (ai * lr - nr * li) / den
    return dt, ar, ai, qr, qi, den


def _b_mask():
    r = lax.broadcasted_iota(jnp.int32, (S5_NS, LANES), 0)
    c = lax.broadcasted_iota(jnp.int32, (S5_NS, LANES), 1)
    return ((r >> 6) & 7) == (c >> 4)


def _c_mask():
    r = lax.broadcasted_iota(jnp.int32, (S5_W, 512), 0)
    c = lax.broadcasted_iota(jnp.int32, (S5_W, 512), 1)
    return ((r >> 4) & 7) == (c >> 6)


def _s5_prep(lam_r, ldt_r, lam_c, ldt_c, b_t, c_t, name):
    def body(lam_r_ref, ldt_r_ref, lam_c_ref, ldt_c_ref, b_ref, c_ref, tab_ref, bset_ref, cset_ref):
        lr, li = lam_r_ref[0:1, :], lam_r_ref[1:2, :]
        dt = jnp.exp(ldt_r_ref[...])
        mag = jnp.exp(lr * dt)
        p1r, p1i = mag * jnp.cos(li * dt), mag * jnp.sin(li * dt)
        p2r, p2i = _cmul(p1r, p1i, p1r, p1i)
        p3r, p3i = _cmul(p2r, p2i, p1r, p1i)
        p4r, p4i = _cmul(p2r, p2i, p2r, p2i)
        p5r, p5i = _cmul(p4r, p4i, p1r, p1i)
        p6r, p6i = _cmul(p4r, p4i, p2r, p2i)
        p7r, p7i = _cmul(p4r, p4i, p3r, p3i)
        p8r, p8i = _cmul(p4r, p4i, p4r, p4i)
        pw_r = [p1r, p2r, p3r, p4r, p5r, p6r, p7r, p8r]
        pw_i = [p1i, p2i, p3i, p4i, p5i, p6i, p7i, p8i]
        row = lax.broadcasted_iota(jnp.int32, (8, S5_NS), 0)
        zero = jnp.zeros((8, S5_NS), F32)

        def bc(v):
            return jnp.broadcast_to(v, (8, S5_NS))

        for d in range(2):
            sgn = 1.0 if d == 0 else -1.0
            for t, s in enumerate((1, 2, 4)):
                live = (row >= s) if d == 0 else (row <= 7 - s)
                tab_ref[d, 2 * t] = jnp.where(live, bc(pw_r[s - 1]), zero)
                tab_ref[d, 2 * t + 1] = jnp.where(live, bc(sgn * pw_i[s - 1]), zero)
            cr, ci = zero, zero
            for i in range(8):
                e = i if d == 0 else 7 - i
                cr = jnp.where(row == i, bc(pw_r[e]), cr)
                ci = jnp.where(row == i, bc(sgn * pw_i[e]), ci)
            tab_ref[d, 6] = cr
            tab_ref[d, 7] = ci

        _, _, _, qr, qi, _ = _zoh_cols(lam_c_ref[:, 0:1], lam_c_ref[:, 1:2], ldt_c_ref[...])
        bm = _b_mask()
        br, bi = b_ref[0], b_ref[1]
        bset_ref[0] = jnp.where(bm, qr * br - qi * bi, 0.0).astype(BF16)
        bset_ref[1] = jnp.where(bm, qr * bi + qi * br, 0.0).astype(BF16)
        cm = _c_mask()
        cset_ref[0] = jnp.where(cm, c_ref[0], 0.0).astype(BF16)
        cset_ref[1] = jnp.where(cm, c_ref[1], 0.0).astype(BF16)

    vm = pl.BlockSpec(memory_space=pltpu.VMEM)
    return pl.pallas_call(
        body, name=name, in_specs=[vm] * 6, out_specs=[vm] * 3,
        out_shape=[_sds((2, 8, 8, S5_NS)), _sds((2, S5_NS, LANES), BF16), _sds((2, S5_W, 512), BF16)],
        compiler_params=pltpu.CompilerParams(vmem_limit_bytes=VMEM_LIMIT),
    )(lam_r, ldt_r, lam_c, ldt_c, b_t, c_t)


SCAN_W = SCAN_GROUPS * LANES


def _scan_chunk(src_ref, dst_ref, tab_ref, carry_ref, nb, reverse, xs_ref=None, acc_ref=None):
    row = lax.broadcasted_iota(jnp.int32, (8, LANES), 0)

    def step(i, carry):
        b = (nb - 1 - i) if reverse else i
        off = pl.multiple_of(b * 8, 8)
        out = []
        for g in range(SCAN_GROUPS):
            lanes = pl.ds(g * LANES, LANES)
            cr, ci = carry[2 * g], carry[2 * g + 1]
            yr = src_ref[0, pl.ds(off, 8), lanes]
            yi = src_ref[1, pl.ds(off, 8), lanes]
            for t, s in enumerate((1, 2, 4)):
                sh = (8 - s) if reverse else s
                sr = pltpu.roll(yr, sh, 0)
                si = pltpu.roll(yi, sh, 0)
                mr, mi = tab_ref[2 * t, :, lanes], tab_ref[2 * t + 1, :, lanes]
                yr, yi = yr + mr * sr - mi * si, yi + mr * si + mi * sr
            pr, pi = tab_ref[6, :, lanes], tab_ref[7, :, lanes]
            yr, yi = yr + pr * cr - pi * ci, yi + pr * ci + pi * cr
            dst_ref[0, pl.ds(off, 8), lanes] = yr
            dst_ref[1, pl.ds(off, 8), lanes] = yi
            if xs_ref is not None:
                nr = jnp.where(row == 7, cr, pltpu.roll(yr, 7, 0))
                ni = jnp.where(row == 7, ci, pltpu.roll(yi, 7, 0))
                xr = xs_ref[0, pl.ds(off, 8), lanes]
                xi = xs_ref[1, pl.ds(off, 8), lanes]
                acc_ref[0, :, lanes] += xr * nr + xi * ni
                acc_ref[1, :, lanes] += xr * ni - xi * nr
            last = 0 if reverse else 7
            out += [jnp.broadcast_to(yr[last:last + 1, :], (8, LANES)),
                    jnp.broadcast_to(yi[last:last + 1, :], (8, LANES))]
        return tuple(out)

    init = []
    for g in range(SCAN_GROUPS):
        init += [carry_ref[0, :, pl.ds(g * LANES, LANES)], carry_ref[1, :, pl.ds(g * LANES, LANES)]]
    fin = lax.fori_loop(0, nb, step, tuple(init))
    for g in range(SCAN_GROUPS):
        carry_ref[0, :, pl.ds(g * LANES, LANES)] = fin[2 * g]
        carry_ref[1, :, pl.ds(g * LANES, LANES)] = fin[2 * g + 1]


def _s5_scan_fwd(z, bset, cset, dvec, tabs, name):
    L = z.shape[0]
    tl = min(SCAN_CHUNK, L)
    nc = L // tl

    def body(u_ref, b_ref, c_ref, d_ref, tab_ref, x_ref, y_ref, carry_ref):
        @pl.when(pl.program_id(1) == 0)
        def _():
            carry_ref[...] = jnp.zeros_like(carry_ref)

        uf = u_ref[...]
        u = uf.astype(BF16)
        x_ref[0] = _dot(u, b_ref[0], NT)
        x_ref[1] = _dot(u, b_ref[1], NT)
        _scan_chunk(x_ref, x_ref, tab_ref, carry_ref, tl // 8, False)
        y_ref[...] = (_dot(x_ref[0].astype(BF16), c_ref[0], NT) - _dot(x_ref[1].astype(BF16), c_ref[1], NT)
                      + d_ref[...] * uf)

    col = pl.BlockSpec((tl, LANES), lambda j, c: (c, j))
    return pl.pallas_call(
        body, name=name, grid=(S5_NS // SCAN_W, nc),
        in_specs=[col, pl.BlockSpec((2, SCAN_W, LANES), lambda j, c: (0, j, 0)),
                  pl.BlockSpec((2, LANES, SCAN_W), lambda j, c: (0, j, 0)),
                  pl.BlockSpec((1, LANES), lambda j, c: (0, j)),
                  pl.BlockSpec((None, 8, 8, SCAN_W), lambda j, c: (0, 0, 0, j))],
        out_specs=[pl.BlockSpec((2, tl, SCAN_W), lambda j, c: (0, c, j)), col],
        out_shape=[_sds((2, L, S5_NS)), _sds((L, S5_W))],
        scratch_shapes=[pltpu.VMEM((2, 8, SCAN_W), F32)],
        compiler_params=_cp("parallel", "arbitrary"),
    )(z, bset, cset, dvec, tabs)


def _s5_scan_bwd(gyl, cset, xs, z, bset, gud, tabs, name):
    L = z.shape[0]
    tl = min(SCAN_CHUNK, L)
    nc = L // tl

    def body(g_ref, c_ref, xs_ref, u_ref, b_ref, gud_ref, tab_ref, gu_ref, ga_ref, gb_ref, gc_ref,
             gx_ref, carry_ref, acc_ref):
        c = pl.program_id(1)

        @pl.when(c == 0)
        def _():
            carry_ref[...] = jnp.zeros_like(carry_ref)
            acc_ref[...] = jnp.zeros_like(acc_ref)
            gb_ref[...] = jnp.zeros_like(gb_ref)
            gc_ref[...] = jnp.zeros_like(gc_ref)

        gy = g_ref[...].astype(BF16)
        gx_ref[0] = _dot(gy, c_ref[0])
        gx_ref[1] = -_dot(gy, c_ref[1])
        gc_ref[0] += _dot(gy, xs_ref[0].astype(BF16), TN)
        gc_ref[1] -= _dot(gy, xs_ref[1].astype(BF16), TN)
        _scan_chunk(gx_ref, gx_ref, tab_ref, carry_ref, tl // 8, True, xs_ref, acc_ref)
        gr = gx_ref[0].astype(BF16)
        gi = gx_ref[1].astype(BF16)
        gu_ref[...] = gud_ref[...] + _dot(gr, b_ref[0]) + _dot(gi, b_ref[1])
        u = u_ref[...].astype(BF16)
        gb_ref[0] += _dot(gr, u, TN)
        gb_ref[1] += _dot(gi, u, TN)

        @pl.when(c == nc - 1)
        def _():
            ga_ref[0:1, :] = jnp.sum(acc_ref[0], axis=0, keepdims=True)
            ga_ref[1:2, :] = jnp.sum(acc_ref[1], axis=0, keepdims=True)

    rev = lambda j, c: (nc - 1 - c, j)
    col = pl.BlockSpec((tl, LANES), rev)
    return pl.pallas_call(
        body, name=name, grid=(S5_NS // SCAN_W, nc),
        in_specs=[col, pl.BlockSpec((2, LANES, SCAN_W), lambda j, c: (0, j, 0)),
                  pl.BlockSpec((2, tl, SCAN_W), lambda j, c: (0, nc - 1 - c, j)), col,
                  pl.BlockSpec((2, SCAN_W, LANES), lambda j, c: (0, j, 0)), col,
                  pl.BlockSpec((None, 8, 8, SCAN_W), lambda j, c: (1, 0, 0, j))],
        out_specs=[col, pl.BlockSpec((2, SCAN_W), lambda j, c: (0, j)),
                   pl.BlockSpec((2, SCAN_W, LANES), lambda j, c: (0, j, 0)),
                   pl.BlockSpec((2, LANES, SCAN_W), lambda j, c: (0, j, 0))],
        out_shape=[_sds((L, S5_W)), _sds((2, S5_NS)), _sds((2, S5_NS, LANES)), _sds((2, S5_W, 512))],
        scratch_shapes=[pltpu.VMEM((2, tl, SCAN_W), F32), pltpu.VMEM((2, 8, SCAN_W), F32),
                        pltpu.VMEM((2, 8, SCAN_W), F32)],
        compiler_params=_cp("parallel", "arbitrary"),
    )(gyl, cset, xs, z, bset, gud, tabs)


def _s5_glu_fwd(ylin, wglu, name):
    L = ylin.shape[0]
    bl = min(1024, L)

    def body(ylin_ref, w_ref, ya_ref):
        yg = _gelu(ylin_ref[...])
        t = _dot(yg.astype(BF16), w_ref[...])
        ya_ref[...] = (yg * _sigmoid(t)).astype(BF16)

    return pl.pallas_call(
        body, name=name, grid=(L // bl,),
        in_specs=[pl.BlockSpec((bl, S5_W), lambda i: (i, 0)), pl.BlockSpec((S5_W, S5_W), lambda i: (0, 0))],
        out_specs=pl.BlockSpec((bl, S5_W), lambda i: (i, 0)),
        out_shape=_sds((L, S5_W), BF16),
        compiler_params=_cp("parallel"),
    )(ylin, wglu)


def _s5_glu_bwd(g_y, wout, ylin, z, dvec, wglu, name):
    L = z.shape[0]
    bl = min(256, L)

    def body(g_ref, wo_ref, ylin_ref, u_ref, d_ref, w_ref, gyl_ref, gud_ref, gw_ref, gd_ref):
        i = pl.program_id(0)
        ylin = ylin_ref[...]
        yg = _gelu(ylin)
        ygb = yg.astype(BF16)
        sg = _sigmoid(_dot(ygb, w_ref[...]))
        gya = _dot(g_ref[...], wo_ref[...], NT)
        gt = gya * yg * sg * (1.0 - sg)
        gtb = gt.astype(BF16)
        gyg = gya * sg + _dot(gtb, w_ref[...], NT)
        gyl = gyg * _gelu_grad(ylin)
        gyl_ref[...] = gyl
        gud_ref[...] = gyl * d_ref[...]

        @pl.when(i == 0)
        def _():
            gw_ref[...] = jnp.zeros_like(gw_ref)
            gd_ref[...] = jnp.zeros_like(gd_ref)

        gw_ref[...] += _dot(ygb, gtb, TN)
        gd_ref[...] += jnp.sum(gyl * u_ref[...], axis=0, keepdims=True)

    blk = pl.BlockSpec((bl, S5_W), lambda i: (i, 0))
    return pl.pallas_call(
        body, name=name, grid=(L // bl,),
        in_specs=[pl.BlockSpec((bl, D_MODEL), lambda i: (i, 0)), pl.BlockSpec((S5_W, D_MODEL), lambda i: (0, 0)),
                  blk, blk, pl.BlockSpec((1, S5_W), lambda i: (0, 0)), pl.BlockSpec((S5_W, S5_W), lambda i: (0, 0))],
        out_specs=[blk, blk, pl.BlockSpec((S5_W, S5_W), lambda i: (0, 0)), pl.BlockSpec((1, S5_W), lambda i: (0, 0))],
        out_shape=[_sds((L, S5_W)), _sds((L, S5_W)), _sds((S5_W, S5_W)), _sds((1, S5_W))],
        compiler_params=_cp("arbitrary"),
    )(g_y, wout, ylin, z, dvec, wglu)


def _s5_param_bwd(lam_c, ldt_c, b_t, gb, ga_c, gc, name):
    def body(lam_ref, ldt_ref, b_ref, gb_ref, ga_ref, gc_ref, glam_ref, gldt_ref, gbo_ref, gco_ref):
        lr, li = lam_ref[:, 0:1], lam_ref[:, 1:2]
        dt, ar, ai, qr, qi, den = _zoh_cols(lr, li, ldt_ref[...])
        bm = _b_mask()
        gbr = jnp.where(bm, gb_ref[0], 0.0)
        gbi = jnp.where(bm, gb_ref[1], 0.0)
        br, bi = b_ref[0], b_ref[1]
        obr = gbr * qr + gbi * qi
        obi = gbi * qr - gbr * qi
        gqr = jnp.sum(gbr * br + gbi * bi, axis=1, keepdims=True)
        gqi = jnp.sum(gbi * br - gbr * bi, axis=1, keepdims=True)
        for s in (64, 32, 16):
            obr = obr + pltpu.roll(obr, s, 1)
            obi = obi + pltpu.roll(obi, s, 1)
        gbo_ref[0] = obr
        gbo_ref[1] = obi
        gar = ga_ref[:, 0:1] + (gqr * lr - gqi * li) / den
        gai = ga_ref[:, 1:2] + (gqr * li + gqi * lr) / den
        qlr = (qr * lr + qi * li) / den
        qli = (qi * lr - qr * li) / den
        glr = -(gqr * qlr + gqi * qli)
        gli = -(gqi * qlr - gqr * qli)
        glr = glr + dt * (gar * ar + gai * ai)
        gli = gli + dt * (gai * ar - gar * ai)
        wr, wi = _cmul(lr, li, ar, ai)
        gldt = (gar * wr + gai * wi) * dt
        glam_ref[:, 0:1] = glr
        glam_ref[:, 1:2] = gli
        r = lax.broadcasted_iota(jnp.int32, (S5_NS, 32), 0)
        c = lax.broadcasted_iota(jnp.int32, (S5_NS, 32), 1)
        gldt_ref[...] = jnp.sum(jnp.where((r >> 6) == c, gldt, 0.0), axis=0, keepdims=True)
        cm = _c_mask()
        for k in range(2):
            oc = jnp.where(cm, gc_ref[k], 0.0)
            for s in (256, 128, 64):
                oc = oc + pltpu.roll(oc, s, 1)
            gco_ref[k] = oc[:, 0:LANES]

    vm = pl.BlockSpec(memory_space=pltpu.VMEM)
    return pl.pallas_call(
        body, name=name, in_specs=[vm] * 6, out_specs=[vm] * 4,
        out_shape=[_sds((S5_NS, 2)), _sds((1, 32)), _sds((2, S5_NS, LANES)), _sds((2, S5_W, LANES))],
        compiler_params=pltpu.CompilerParams(vmem_limit_bytes=VMEM_LIMIT),
    )(lam_c, ldt_c, b_t, gb, ga_c, gc)


FL_BLK = EVEN_PAD // LANES - 1
Q_BLK, K_BLK, V_BLK = 4, 8, 12
NEG = -1e30


def _log_sigmoid(v):
    return jnp.minimum(v, 0.0) - jnp.log(1.0 + jnp.exp(-jnp.abs(v)))


def _fox_f_fwd(z, bf, name):
    L = z.shape[0]

    def body(fl_ref, b_ref, f_ref, fq_ref):
        row = lax.broadcasted_iota(jnp.int32, (L, LANES), 0)
        cs = _cumsum_rows(_log_sigmoid(fl_ref[...] + b_ref[...]), True, row)
        f_ref[...] = cs
        expand = (lax.broadcasted_iota(jnp.int32, (LANES, FOX_W), 0)
                  == (lax.broadcasted_iota(jnp.int32, (LANES, FOX_W), 1) >> 6)).astype(F32)
        fq_ref[...] = lax.dot_general(cs, expand, NN, precision=lax.Precision.HIGHEST, preferred_element_type=F32)

    return pl.pallas_call(
        body, name=name, grid=(1,),
        in_specs=[pl.BlockSpec((L, LANES), lambda i: (0, FL_BLK)), pl.BlockSpec((1, LANES), lambda i: (0, 0))],
        out_specs=[pl.BlockSpec((L, LANES), lambda i: (0, 0)), pl.BlockSpec((L, FOX_W), lambda i: (0, 0))],
        out_shape=[_sds((L, LANES)), _sds((L, FOX_W))],
        compiler_params=_cp("arbitrary"),
    )(z, bf)


def _fox_f_bwd(dFk, dfq, z, bf, name):
    L = z.shape[0]

    def body(dfk_ref, dfq_ref, fl_ref, b_ref, dfl_ref, db_ref):
        sel = (lax.broadcasted_iota(jnp.int32, (FOX_W, LANES), 0)
               == 64 * lax.broadcasted_iota(jnp.int32, (FOX_W, LANES), 1)).astype(F32)
        dfq_h = lax.dot_general(dfq_ref[...], sel, NN, precision=lax.Precision.HIGHEST, preferred_element_type=F32)
        row = lax.broadcasted_iota(jnp.int32, (L, LANES), 0)
        cs = _cumsum_rows(dfk_ref[...] + dfq_h, False, row)
        dfl = cs * _sigmoid(-(fl_ref[...] + b_ref[...]))
        dfl_ref[...] = dfl
        db_ref[...] = jnp.sum(dfl, axis=0, keepdims=True)

    return pl.pallas_call(
        body, name=name, grid=(1,),
        in_specs=[pl.BlockSpec((L, LANES), lambda i: (0, 0)), pl.BlockSpec((L, FOX_W), lambda i: (0, 0)),
                  pl.BlockSpec((L, LANES), lambda i: (0, FL_BLK)), pl.BlockSpec((1, LANES), lambda i: (0, 0))],
        out_specs=[pl.BlockSpec((L, LANES), lambda i: (0, 0)), pl.BlockSpec((1, LANES), lambda i: (0, 0))],
        out_shape=[_sds((L, LANES)), _sds((1, LANES))],
        compiler_params=_cp("arbitrary"),
    )(dFk, dfq, z, bf)


def _head_mask(hh):
    lane = lax.broadcasted_iota(jnp.int32, (1, LANES), 1)
    return (lane >> 6) == hh


FOX_T = 512


def _fox_head(x, hh):
    return jnp.where(_head_mask(hh), x, 0.0).astype(BF16)


def _fox_scores(qh, k, fq_ref, fr_ref, hh, causal):
    if fq_ref is None:
        s = _dot(qh, k, NT) - fr_ref[hh:hh + 1, :]
    else:
        s = _dot(qh, k, NT) + (fq_ref[:, 64 * hh:64 * hh + 1] - fr_ref[hh:hh + 1, :])
    return s if causal is None else jnp.where(causal, s, NEG)


def _causal(T):
    return lax.broadcasted_iota(jnp.int32, (T, T), 1) <= lax.broadcasted_iota(jnp.int32, (T, T), 0)


def _fox_fwd(z, fq, frow, name):
    L = z.shape[0]
    T = min(FOX_T, L)
    nq = L // T

    def body(qt_ref, kt_ref, q_ref, k_ref, v_ref, fq_ref, fr_ref, o_ref, lse_ref, m_ref, l_ref, acc_ref):
        t = pl.program_id(1)
        qi, ki = qt_ref[t], kt_ref[t]

        @pl.when(ki == 0)
        def _():
            m_ref[...] = jnp.full_like(m_ref, NEG)
            l_ref[...] = jnp.zeros_like(l_ref)
            acc_ref[...] = jnp.zeros_like(acc_ref)

        def step(diagonal):
            q = q_ref[...] * 0.125
            k = k_ref[...].astype(BF16)
            v = v_ref[...].astype(BF16)
            causal = _causal(T) if diagonal else None
            s = jnp.concatenate([_fox_scores(_fox_head(q, hh), k, fq_ref, fr_ref, hh, causal) for hh in range(2)],
                                axis=0)
            m_old = m_ref[...]
            m_new = jnp.maximum(m_old, jnp.max(s, axis=1, keepdims=True))
            alpha = jnp.exp(m_old - m_new)
            p = jnp.exp(s - m_new)
            l_ref[...] = alpha * l_ref[...] + jnp.sum(p, axis=1, keepdims=True)
            m_ref[...] = m_new
            acc_ref[...] = alpha * acc_ref[...] + _dot(p.astype(BF16), v)

        @pl.when(ki < qi)
        def _():
            step(False)

        @pl.when(ki == qi)
        def _():
            step(True)
            h0 = _head_mask(0)
            l = l_ref[...]
            o_h = acc_ref[...] / l
            lse_h = m_ref[...] + jnp.log(l)
            o_ref[...] = jnp.where(h0, o_h[:T], o_h[T:])
            lse_ref[...] = jnp.where(h0, lse_h[:T], lse_h[T:]) - fq_ref[...]

    pairs = [(qi, ki) for qi in range(nq) for ki in range(qi + 1)]
    qt = jnp.asarray([p[0] for p in pairs], jnp.int32)
    kt = jnp.asarray([p[1] for p in pairs], jnp.int32)

    def qspec(base):
        return pl.BlockSpec((T, LANES), lambda j, t, qt, kt: (qt[t], base + j))

    def kspec(base):
        return pl.BlockSpec((T, LANES), lambda j, t, qt, kt: (kt[t], base + j))

    return pl.pallas_call(
        body, name=name,
        grid_spec=pltpu.PrefetchScalarGridSpec(
            num_scalar_prefetch=2, grid=(4, len(pairs)),
            in_specs=[qspec(Q_BLK), kspec(K_BLK), kspec(V_BLK), qspec(0),
                      pl.BlockSpec((None, 2, T), lambda j, t, qt, kt: (j, 0, kt[t]))],
            out_specs=[qspec(0), qspec(0)],
            scratch_shapes=[pltpu.VMEM((2 * T, 1), F32), pltpu.VMEM((2 * T, 1), F32),
                            pltpu.VMEM((2 * T, LANES), F32)]),
        out_shape=[_sds((L, FOX_W)), _sds((L, FOX_W))],
        compiler_params=_cp("parallel", "arbitrary"),
    )(qt, kt, z, z, z, fq, frow)


def _fox_bwd(z, frow, o, lse, g_m, name):
    L = z.shape[0]
    T = min(FOX_T, L)
    nq = L // T

    pairs = [(qi, ki) for ki in range(nq) for qi in range(ki, nq)]
    qt = jnp.asarray([p[0] for p in pairs], jnp.int32)
    kt = jnp.asarray([p[1] for p in pairs], jnp.int32)

    def body(qt_ref, kt_ref, q_ref, k_ref, v_ref, fr_ref, o_ref, lse_ref, do_ref,
             dq_ref, dk_ref, dv_ref, dfq_ref, dfk_ref, dk_acc, dv_acc, df_acc):
        t = pl.program_id(1)
        qi, ki = qt_ref[t], kt_ref[t]

        @pl.when(t == 0)
        def _():
            dq_ref[...] = jnp.zeros_like(dq_ref)
            dfq_ref[...] = jnp.zeros_like(dfq_ref)

        @pl.when(qi == ki)
        def _():
            dk_acc[...] = jnp.zeros_like(dk_acc)
            dv_acc[...] = jnp.zeros_like(dv_acc)
            df_acc[...] = jnp.zeros_like(df_acc)

        def step(diagonal):
            q = q_ref[...] * 0.125
            qb = q.astype(BF16)
            k = k_ref[...].astype(BF16)
            v = v_ref[...].astype(BF16)
            do = do_ref[...]
            dob = do.astype(BF16)
            do_o = dob.astype(F32) * o_ref[...]
            causal = _causal(T) if diagonal else None
            dvs, dks, dqs, rss = [], [], [], []
            for hh in range(2):
                s = _fox_scores(_fox_head(q, hh), k, None, fr_ref, hh, causal)
                p = jnp.exp(s - lse_ref[:, 64 * hh:64 * hh + 1])
                dp = _dot(_fox_head(do, hh), v, NT)
                delta = jnp.sum(jnp.where(_head_mask(hh), do_o, 0.0), axis=1, keepdims=True)
                ds = p * (dp - delta)
                dsb = ds.astype(BF16)
                dvs.append(_dot(p.astype(BF16), dob, TN))
                dks.append(_dot(dsb, qb, TN))
                dqs.append(_dot(dsb, k))
                rss.append(jnp.sum(ds, axis=1, keepdims=True))
                df_acc[hh:hh + 1, :] -= jnp.sum(ds, axis=0, keepdims=True)
            h0 = _head_mask(0)
            dv_acc[...] += jnp.where(h0, dvs[0], dvs[1])
            dk_acc[...] += jnp.where(h0, dks[0], dks[1])
            rows = pl.ds(pl.multiple_of(qi * T, T), T)
            dq_ref[rows, :] += jnp.where(h0, dqs[0], dqs[1])
            dfq_ref[rows, :] += jnp.where(h0, rss[0], rss[1])

        @pl.when(qi > ki)
        def _():
            step(False)

        @pl.when(qi == ki)
        def _():
            step(True)

        @pl.when(qi == nq - 1)
        def _():
            dk_ref[...] = dk_acc[...]
            dv_ref[...] = dv_acc[...]
            dfk_ref[...] = df_acc[...]

        @pl.when(t == len(pairs) - 1)
        def _():
            dq_ref[...] = dq_ref[...] * 0.125

    def qside(base):
        return pl.BlockSpec((T, LANES), lambda j, t, qt, kt: (qt[t], base + j))

    def kside(base):
        return pl.BlockSpec((T, LANES), lambda j, t, qt, kt: (kt[t], base + j))

    pair = pl.BlockSpec((L, LANES), lambda j, t, qt, kt: (0, j))
    frow_spec = pl.BlockSpec((None, 2, T), lambda j, t, qt, kt: (j, 0, kt[t]))
    return pl.pallas_call(
        body, name=name,
        grid_spec=pltpu.PrefetchScalarGridSpec(
            num_scalar_prefetch=2, grid=(4, len(pairs)),
            in_specs=[qside(Q_BLK), kside(K_BLK), kside(V_BLK), frow_spec, qside(0), qside(0), qside(0)],
            out_specs=[pair, kside(0), kside(0), pair, frow_spec],
            scratch_shapes=[pltpu.VMEM((T, LANES), F32), pltpu.VMEM((T, LANES), F32), pltpu.VMEM((2, T), F32)]),
        out_shape=[_sds((L, FOX_W)), _sds((L, FOX_W)), _sds((L, FOX_W)), _sds((L, FOX_W)), _sds((4, 2, L))],
        compiler_params=_cp("parallel", "arbitrary"),
    )(qt, kt, z, z, z, frow, o, lse, g_m)


def _shift_rows(v, s, down, row):
    n = v.shape[0]
    if down:
        return jnp.where(row >= s, pltpu.roll(v, s, 0), 0.0)
    return jnp.where(row < n - s, pltpu.roll(v, n - s, 0), 0.0)


def _cumsum_rows(v, down, row):
    s = 1
    while s < v.shape[0]:
        v = v + _shift_rows(v, s, down, row)
        s *= 2
    return v


def _window_sum(v, g, down, row):
    out = jnp.zeros_like(v)
    s = v
    for k in range(4):
        s = s + _shift_rows(s, 1 << k, down, row)
        out = jnp.where(g == k, s, out)
    return out


def _pool_inv_cnt(g, row):
    w = jnp.left_shift(2, g).astype(F32)
    return 1.0 / jnp.minimum(row.astype(F32) + 1.0, w)


def _pool_fwd(z, pool_w, scale, name):
    L = z.shape[0]

    def body(x_ref, w_ref, s_ref, y_ref, p_ref):
        g = pl.program_id(0)
        row = lax.broadcasted_iota(jnp.int32, (L, LANES), 0)
        x = x_ref[...]
        pooled = (_window_sum(x, g, True, row) * _pool_inv_cnt(g, row) - x).astype(BF16)
        p_ref[...] = pooled
        y_ref[...] = (_dot(pooled, w_ref[...].astype(BF16)) * s_ref[...]).astype(BF16)

    col = pl.BlockSpec((L, LANES), lambda g: (0, g))
    return pl.pallas_call(
        body, name=name, grid=(4,),
        in_specs=[col, pl.BlockSpec((None, LANES, LANES), lambda g: (g, 0, 0)), pl.BlockSpec((1, LANES), lambda g: (0, g))],
        out_specs=[col, col],
        out_shape=[_sds((L, 512), BF16), _sds((L, 512), BF16)],
        compiler_params=_cp("parallel"),
    )(z, pool_w, scale)


def _pool_bwd(g_y, wout, pooled, pool_w, scale, name):
    L = g_y.shape[0]

    def body(g_ref, wo_ref, p_ref, w_ref, s_ref, gx_ref, gw_ref, gs_ref):
        g = pl.program_id(0)
        row = lax.broadcasted_iota(jnp.int32, (L, LANES), 0)
        gy = _dot(g_ref[...], wo_ref[...], NT)
        pooled = p_ref[...]
        wb = w_ref[...].astype(BF16)
        lin = _dot(pooled, wb)
        gs_ref[...] = jnp.sum(gy * lin, axis=0, keepdims=True)
        glin = (gy * s_ref[...]).astype(BF16)
        gw_ref[...] = _dot(pooled, glin, TN)
        gp = _dot(glin, wb, NT)
        gx_ref[...] = _window_sum(gp * _pool_inv_cnt(g, row), g, False, row) - gp

    col = pl.BlockSpec((L, LANES), lambda g: (0, g))
    wspec = pl.BlockSpec((None, LANES, LANES), lambda g: (g, 0, 0))
    vec = pl.BlockSpec((1, LANES), lambda g: (0, g))
    return pl.pallas_call(
        body, name=name, grid=(4,),
        in_specs=[pl.BlockSpec((L, D_MODEL), lambda g: (0, 0)), pl.BlockSpec((LANES, D_MODEL), lambda g: (g, 0)),
                  col, wspec, vec],
        out_specs=[col, wspec, vec],
        out_shape=[_sds((L, 512)), _sds((4, LANES, LANES)), _sds((1, 512))],
        compiler_params=_cp("parallel"),
    )(g_y, wout, pooled, pool_w, scale)


SGU_CHUNKS = 4


def _sgu_ln(v, gam, bet):
    gv = _gelu(v)
    mu = jnp.mean(gv, axis=-1, keepdims=True)
    xc = gv - mu
    rs = lax.rsqrt(jnp.mean(xc * xc, axis=-1, keepdims=True) + EPS)
    xh = xc * rs
    return xh, rs, xh * gam + bet


def _tril_ws(w_ref, g):
    r = lax.broadcasted_iota(jnp.int32, (LANES, LANES), 0)
    c = lax.broadcasted_iota(jnp.int32, (LANES, LANES), 1)
    return jnp.where(r >= c, w_ref[g], 0.0).astype(BF16)


def _sgu_fwd(z, ln_g, ln_b, w_s, b_st, name):
    L = z.shape[0]
    rb = min(SGU_CHUNKS * LANES, L)

    def body(u_ref, v_ref, g_ref, b_ref, w_ref, bs_ref, y_ref):
        _, _, vln = _sgu_ln(v_ref[...], g_ref[...], b_ref[...])
        gu = _gelu(u_ref[...])
        vb = vln.astype(BF16)
        for g in range(4):
            ws = _tril_ws(w_ref, g)
            for n in range(rb // LANES):
                rows = slice(n * LANES, (n + 1) * LANES)
                cols = slice(g * LANES, (g + 1) * LANES)
                mixed = _dot(ws, vb[rows, cols]) + bs_ref[:, g:g + 1]
                y_ref[rows, cols] = (gu[rows, cols] * mixed).astype(BF16)

    vm = lambda shape: pl.BlockSpec(shape, lambda i: tuple(0 for _ in shape))
    return pl.pallas_call(
        body, name=name, grid=(L // rb,),
        in_specs=[pl.BlockSpec((rb, 512), lambda i: (i, 1)), pl.BlockSpec((rb, 512), lambda i: (i, 2)),
                  vm((1, 512)), vm((1, 512)), vm((4, LANES, LANES)), vm((LANES, 4))],
        out_specs=pl.BlockSpec((rb, 512), lambda i: (i, 0)),
        out_shape=_sds((L, 512), BF16),
        compiler_params=_cp("parallel"),
    )(z, z, ln_g, ln_b, w_s, b_st)


def _sgu_bwd(g_y, wout, z, ln_g, ln_b, w_s, b_st, name):
    L = z.shape[0]
    rb = min(SGU_CHUNKS * LANES, L)

    def body(gyo_ref, wo_ref, u_ref, v_ref, g_ref, b_ref, w_ref, bs_ref, gu_ref, gv_ref, gw_ref, gbs_ref, gg_ref,
             gb_ref):
        i = pl.program_id(0)

        @pl.when(i == 0)
        def _():
            gw_ref[...] = jnp.zeros_like(gw_ref)
            gbs_ref[...] = jnp.zeros_like(gbs_ref)
            gg_ref[...] = jnp.zeros_like(gg_ref)
            gb_ref[...] = jnp.zeros_like(gb_ref)

        v = v_ref[...]
        u = u_ref[...]
        gy = _dot(gyo_ref[...], wo_ref[...], NT)
        xh, rs, vln = _sgu_ln(v, g_ref[...], b_ref[...])
        gel_u = _gelu(u)
        gmix = gy * gel_u
        vb = vln.astype(BF16)
        gmb = gmix.astype(BF16)
        r = lax.broadcasted_iota(jnp.int32, (LANES, LANES), 0)
        c = lax.broadcasted_iota(jnp.int32, (LANES, LANES), 1)
        gvln_cols = []
        for g in range(4):
            ws = _tril_ws(w_ref, g)
            cols = slice(g * LANES, (g + 1) * LANES)
            gw = jnp.zeros((LANES, LANES), F32)
            gbs = jnp.zeros((LANES, 1), F32)
            parts = []
            for n in range(rb // LANES):
                rows = slice(n * LANES, (n + 1) * LANES)
                mixed = _dot(ws, vb[rows, cols]) + bs_ref[:, g:g + 1]
                gu_ref[rows, cols] = gy[rows, cols] * mixed * _gelu_grad(u[rows, cols])
                parts.append(_dot(ws, gmb[rows, cols], TN))
                gw = gw + _dot(gmb[rows, cols], vb[rows, cols], NT)
                gbs = gbs + jnp.sum(gmix[rows, cols], axis=1, keepdims=True)
            gvln_cols.append(jnp.concatenate(parts, axis=0))
            gw_ref[g] += jnp.where(r >= c, gw, 0.0)
            gbs_ref[:, g:g + 1] += gbs
        gvln = jnp.concatenate(gvln_cols, axis=1)
        gg_ref[...] += jnp.sum(gvln * xh, axis=0, keepdims=True)
        gb_ref[...] += jnp.sum(gvln, axis=0, keepdims=True)
        gxh = gvln * g_ref[...]
        ggv = rs * (gxh - jnp.mean(gxh, axis=-1, keepdims=True) - xh * jnp.mean(gxh * xh, axis=-1, keepdims=True))
        gv_ref[...] = ggv * _gelu_grad(v)

    vm = lambda shape: pl.BlockSpec(shape, lambda i: tuple(0 for _ in shape))
    blk = pl.BlockSpec((rb, 512), lambda i: (i, 0))
    return pl.pallas_call(
        body, name=name, grid=(L // rb,),
        in_specs=[pl.BlockSpec((rb, D_MODEL), lambda i: (i, 0)), pl.BlockSpec((512, D_MODEL), lambda i: (1, 0)),
                  pl.BlockSpec((rb, 512), lambda i: (i, 1)), pl.BlockSpec((rb, 512), lambda i: (i, 2)),
                  vm((1, 512)), vm((1, 512)), vm((4, LANES, LANES)), vm((LANES, 4))],
        out_specs=[blk, blk, vm((4, LANES, LANES)), vm((LANES, 4)), vm((1, 512)), vm((1, 512))],
        out_shape=[_sds((L, 512)), _sds((L, 512)), _sds((4, LANES, LANES)), _sds((LANES, 4)),
                   _sds((1, 512)), _sds((1, 512))],
        compiler_params=_cp("arbitrary"),
    )(g_y, wout, z, z, ln_g, ln_b, w_s, b_st)


def _adamw_math(w, g, m, v):
    nm = ADAM_B1 * m + (1.0 - ADAM_B1) * g
    nv = ADAM_B2 * v + (1.0 - ADAM_B2) * (g * g)
    m_hat = nm / (1.0 - ADAM_B1 ** ADAM_STEP)
    v_hat = nv / (1.0 - ADAM_B2 ** ADAM_STEP)
    delta = -ADAM_LR * (m_hat / (jnp.sqrt(v_hat) + ADAM_EPS) + ADAM_WD * w)
    return delta, nm, nv


def _sum_adamw(parts, w, m, v, name):
    n_layers, R, C = w.shape
    assert len(parts) == n_layers
    rb = 128 if R % 128 == 0 else R
    nb = R // rb

    def body(*refs):
        p_refs = refs[:n_layers]
        w_ref, m_ref, v_ref, g_ref, d_ref, nm_ref, nv_ref = refs[n_layers:]
        for k, p_ref in enumerate(p_refs):
            @pl.when(pl.program_id(0) == k)
            def _(p_ref=p_ref):
                g = p_ref[0].astype(F32)
                for s in range(1, N_DEV):
                    g = g + p_ref[s].astype(F32)
                d, nm, nv = _adamw_math(w_ref[...], g, m_ref[...], v_ref[...])
                g_ref[...] = g
                d_ref[...] = d
                nm_ref[...] = nm
                nv_ref[...] = nv

    def part_spec(k):
        return pl.BlockSpec((N_DEV, rb, C), lambda l, i: (0, jnp.where(l == k, i, jnp.where(l < k, 0, nb - 1)), 0))

    blk = pl.BlockSpec((None, rb, C), lambda l, i: (l, i, 0))
    return pl.pallas_call(
        body, name=name, grid=(n_layers, nb),
        in_specs=[part_spec(k) for k in range(n_layers)] + [blk, blk, blk],
        out_specs=[blk] * 4, out_shape=[_sds((n_layers, R, C))] * 4,
        compiler_params=_cp("arbitrary", "arbitrary"),
    )(*parts, w, m, v)


def _sum_adamw_rows(parts, w, m, v, name):
    R, _, C = w.shape

    def body(p_ref, w_ref, m_ref, v_ref, g_ref, d_ref, nm_ref, nv_ref):
        g = p_ref[0].astype(F32)
        for s in range(1, N_DEV):
            g = g + p_ref[s].astype(F32)
        d, nm, nv = _adamw_math(w_ref[:, 0, :], g, m_ref[:, 0, :], v_ref[:, 0, :])
        g_ref[:, 0, :] = g
        d_ref[:, 0, :] = d
        nm_ref[:, 0, :] = nm
        nv_ref[:, 0, :] = nv

    vm = pl.BlockSpec(memory_space=pltpu.VMEM)
    return pl.pallas_call(body, name=name, in_specs=[vm] * 4, out_specs=[vm] * 4, out_shape=[_sds((R, 1, C))] * 4,
                          compiler_params=pltpu.CompilerParams(vmem_limit_bytes=VMEM_LIMIT))(parts, w, m, v)


def _sum_pieces(parts, name):
    _, R, C = parts.shape

    def body(p_ref, g_ref):
        g = p_ref[0].astype(F32)
        for s in range(1, N_DEV):
            g = g + p_ref[s].astype(F32)
        g_ref[...] = g

    vm = pl.BlockSpec(memory_space=pltpu.VMEM)
    return pl.pallas_call(body, name=name, in_specs=[vm], out_specs=vm, out_shape=_sds((R, C)),
                          compiler_params=pltpu.CompilerParams(vmem_limit_bytes=VMEM_LIMIT))(parts)


def _adamw_many(ws, gs, ms, vs, name):
    n = len(ws)
    vm = pl.BlockSpec(memory_space=pltpu.VMEM)

    def body(*refs):
        w_refs, g_refs, m_refs, v_refs = refs[:n], refs[n:2 * n], refs[2 * n:3 * n], refs[3 * n:4 * n]
        d_refs, nm_refs, nv_refs = refs[4 * n:5 * n], refs[5 * n:6 * n], refs[6 * n:7 * n]
        for i in range(n):
            d, nm, nv = _adamw_math(w_refs[i][...], g_refs[i][...], m_refs[i][...], v_refs[i][...])
            d_refs[i][...] = d
            nm_refs[i][...] = nm
            nv_refs[i][...] = nv

    shapes = [_sds(w.shape) for w in ws]
    outs = pl.pallas_call(
        body, name=name, in_specs=[vm] * (4 * n), out_specs=[vm] * (3 * n), out_shape=shapes * 3,
        compiler_params=pltpu.CompilerParams(vmem_limit_bytes=VMEM_LIMIT),
    )(*ws, *gs, *ms, *vs)
    return list(outs[:n]), list(outs[n:2 * n]), list(outs[2 * n:])


def _mesh_pos():
    return lax.axis_index("x"), lax.axis_index("y"), lax.axis_index("c")


def _dev_index(p):
    return 4 * p[0] + 2 * p[1] + p[2]


HBM = pl.BlockSpec(memory_space=pltpu.HBM)
SEM = pl.BlockSpec(memory_space=pltpu.SEMAPHORE)
EFFECT = pltpu.SideEffectType.DATAFLOW_SIDE_EFFECTING


def _peer_list():
    x, y, c = _mesh_pos()
    peers = [(x ^ dx, y ^ dy, c ^ dc) for dx in range(2) for dy in range(2) for dc in range(2)][1:]
    return (x, y, c), peers


def _split_copy(src_ref, land_ref, send_sems, recv_sems, i, k, peer, slot, exchange):
    return pltpu.make_async_remote_copy(
        src_ref=src_ref.at[_dev_index(peer)] if exchange else src_ref, dst_ref=land_ref.at[slot],
        send_sem=send_sems.at[7 * i + k], recv_sem=recv_sems.at[7 * i + k], device_id=peer, device_id_type=MESH)


def _own_copy(src_ref, land_ref, own_sems, i, slot, exchange):
    return pltpu.make_async_copy(src_ref.at[slot] if exchange else src_ref, land_ref.at[slot], own_sems.at[i])


def _comm_start(groups, name, exchange, dep=None):
    sizes = [len(g) for g in groups]
    n = sum(sizes)
    srcs = [a for g in groups for a in g]
    per_group = [exchange] * len(groups) if isinstance(exchange, bool) else list(exchange)
    exchanged = [flag for flag, sz in zip(per_group, sizes) for _ in range(sz)]
    lands = [lax.empty(a.shape if ex else (N_DEV,) + a.shape, a.dtype) for a, ex in zip(srcs, exchanged)]

    n_dep = 0 if dep is None else 1

    def body(*refs):
        src_refs, land_refs = refs[:n], refs[n:2 * n]
        sem_refs = refs[2 * n + n_dep:2 * n + n_dep + 3 * len(sizes)]
        token_ref = refs[-1]
        me, peers = _peer_list()
        mi = _dev_index(me)
        i = 0
        for gi, sz in enumerate(sizes):
            for j in range(sz):
                for k, peer in enumerate(peers):
                    _split_copy(src_refs[i], land_refs[i], sem_refs[3 * gi], sem_refs[3 * gi + 1], j, k, peer, mi,
                                exchanged[i]).start()
                _own_copy(src_refs[i], land_refs[i], sem_refs[3 * gi + 2], j, mi, exchanged[i]).start()
                i += 1
        token_ref[...] = jnp.zeros_like(token_ref)

    sem_shapes = []
    for sz in sizes:
        sem_shapes += [pltpu.SemaphoreType.DMA((7 * sz,)), pltpu.SemaphoreType.DMA((7 * sz,)),
                       pltpu.SemaphoreType.DMA((sz,))]
    thru = [pltpu.HBM(a.shape, a.dtype) for a in srcs + lands]
    n_sem = len(sem_shapes)
    outs = pl.pallas_call(
        body, name=name,
        out_shape=tuple(sem_shapes + thru + [_sds((8, LANES))]),
        in_specs=[HBM] * (2 * n) + [ANY] * n_dep,
        out_specs=tuple([SEM] * n_sem + [HBM] * (2 * n) + [pl.BlockSpec(memory_space=pltpu.VMEM)]),
        input_output_aliases={i: n_sem + i for i in range(2 * n)},
        compiler_params=pltpu.CompilerParams(has_side_effects=EFFECT),
    )(*[pltpu.with_memory_space_constraint(a, pltpu.HBM) for a in srcs + lands], *([] if dep is None else [dep]))
    sems, thru_src, thru_land, token = outs[:n_sem], outs[n_sem:n_sem + n], outs[n_sem + n:n_sem + 2 * n], outs[-1]
    result, off = [], 0
    for gi, sz in enumerate(sizes):
        result.append((*sems[3 * gi:3 * gi + 3], list(thru_src[off:off + sz]), list(thru_land[off:off + sz])))
        off += sz
    return result, token


def _comm_wait(group, after, name, exchange):
    send_sems, recv_sems, own_sems, srcs, lands = group
    n = len(srcs)
    after = list(after) if isinstance(after, (list, tuple)) else [after]

    def body(*refs):
        src_refs, land_refs = refs[:n], refs[n:2 * n]
        ssem, rsem, osem = refs[2 * n:2 * n + 3]
        me, peers = _peer_list()
        for i in range(n):
            for k, peer in enumerate(peers):
                cp = _split_copy(src_refs[i], land_refs[i], ssem, rsem, i, k, peer, _dev_index(peer), exchange)
                cp.wait_send()
                cp.wait_recv()
            _own_copy(src_refs[i], land_refs[i], osem, i, _dev_index(me), exchange).wait()

    outs = pl.pallas_call(
        body, name=name,
        out_shape=tuple(pltpu.HBM(a.shape, a.dtype) for a in srcs + lands),
        in_specs=[HBM] * (2 * n) + [SEM, SEM, SEM] + [ANY] * len(after),
        out_specs=tuple([HBM] * (2 * n)),
        input_output_aliases={i: i for i in range(2 * n)},
        compiler_params=pltpu.CompilerParams(has_side_effects=EFFECT),
    )(*srcs, *lands, send_sems, recv_sems, own_sems, *after)
    return list(outs[n:])


def _tie(a, token):
    return a + token[0, 0].astype(a.dtype)


def _pack(arrs, rows):
    flat = jnp.concatenate([a.reshape(-1).astype(F32) for a in arrs])
    return jnp.pad(flat, (0, rows * LANES - flat.shape[0])).reshape(rows, LANES)


def _unpack(packed, shapes):
    flat = packed.reshape(-1)
    out, off = [], 0
    for s in shapes:
        n = math.prod(s)
        out.append(flat[off:off + n].reshape(s))
        off += n
    return out


def _packed_rows(shapes):
    n = sum(math.prod(s) for s in shapes)
    unit = N_DEV * 8 * LANES
    return -(-n // unit) * unit // LANES


def kernel(x, mix_pre_g, mix_post_g, mlp_pre_g, mlp_post_g, w_in_even, s5_lam_re, s5_lam_im, s5_log_dt, s5_b_re, s5_b_im, s5_c_re, s5_c_im, s5_d, s5_w_glu, fox_b_f, w_out_even, w_in_odd, pool_w, pool_scale, sgu_ln_g, sgu_ln_b, sgu_w_s, sgu_b_s, w_out_odd, mlp_w1, mlp_w2, loss_target, m_mix_pre_g, m_mix_post_g, m_mlp_pre_g, m_mlp_post_g, m_w_in_even, m_s5_lam_re, m_s5_lam_im, m_s5_log_dt, m_s5_b_re, m_s5_b_im, m_s5_c_re, m_s5_c_im, m_s5_d, m_s5_w_glu, m_fox_b_f, m_w_out_even, m_w_in_odd, m_pool_w, m_pool_scale, m_sgu_ln_g, m_sgu_ln_b, m_sgu_w_s, m_sgu_b_s, m_w_out_odd, m_mlp_w1, m_mlp_w2, v_mix_pre_g, v_mix_post_g, v_mlp_pre_g, v_mlp_post_g, v_w_in_even, v_s5_lam_re, v_s5_lam_im, v_s5_log_dt, v_s5_b_re, v_s5_b_im, v_s5_c_re, v_s5_c_im, v_s5_d, v_s5_w_glu, v_fox_b_f, v_w_out_even, v_w_in_odd, v_pool_w, v_pool_scale, v_sgu_ln_g, v_sgu_ln_b, v_sgu_w_s, v_sgu_b_s, v_w_out_odd, v_mlp_w1, v_mlp_w2):
    weights = dict(mix_pre_g=mix_pre_g, mix_post_g=mix_post_g, mlp_pre_g=mlp_pre_g, mlp_post_g=mlp_post_g, w_in_even=w_in_even, s5_lam_re=s5_lam_re, s5_lam_im=s5_lam_im, s5_log_dt=s5_log_dt, s5_b_re=s5_b_re, s5_b_im=s5_b_im, s5_c_re=s5_c_re, s5_c_im=s5_c_im, s5_d=s5_d, s5_w_glu=s5_w_glu, fox_b_f=fox_b_f, w_out_even=w_out_even, w_in_odd=w_in_odd, pool_w=pool_w, pool_scale=pool_scale, sgu_ln_g=sgu_ln_g, sgu_ln_b=sgu_ln_b, sgu_w_s=sgu_w_s, sgu_b_s=sgu_b_s, w_out_odd=w_out_odd, mlp_w1=mlp_w1, mlp_w2=mlp_w2)
    mom_m = dict(mix_pre_g=m_mix_pre_g, mix_post_g=m_mix_post_g, mlp_pre_g=m_mlp_pre_g, mlp_post_g=m_mlp_post_g, w_in_even=m_w_in_even, s5_lam_re=m_s5_lam_re, s5_lam_im=m_s5_lam_im, s5_log_dt=m_s5_log_dt, s5_b_re=m_s5_b_re, s5_b_im=m_s5_b_im, s5_c_re=m_s5_c_re, s5_c_im=m_s5_c_im, s5_d=m_s5_d, s5_w_glu=m_s5_w_glu, fox_b_f=m_fox_b_f, w_out_even=m_w_out_even, w_in_odd=m_w_in_odd, pool_w=m_pool_w, pool_scale=m_pool_scale, sgu_ln_g=m_sgu_ln_g, sgu_ln_b=m_sgu_ln_b, sgu_w_s=m_sgu_w_s, sgu_b_s=m_sgu_b_s, w_out_odd=m_w_out_odd, mlp_w1=m_mlp_w1, mlp_w2=m_mlp_w2)
    mom_v = dict(mix_pre_g=v_mix_pre_g, mix_post_g=v_mix_post_g, mlp_pre_g=v_mlp_pre_g, mlp_post_g=v_mlp_post_g, w_in_even=v_w_in_even, s5_lam_re=v_s5_lam_re, s5_lam_im=v_s5_lam_im, s5_log_dt=v_s5_log_dt, s5_b_re=v_s5_b_re, s5_b_im=v_s5_b_im, s5_c_re=v_s5_c_re, s5_c_im=v_s5_c_im, s5_d=v_s5_d, s5_w_glu=v_s5_w_glu, fox_b_f=v_fox_b_f, w_out_even=v_w_out_even, w_in_odd=v_w_in_odd, pool_w=v_pool_w, pool_scale=v_pool_scale, sgu_ln_g=v_sgu_ln_g, sgu_ln_b=v_sgu_ln_b, sgu_w_s=v_sgu_w_s, sgu_b_s=v_sgu_b_s, w_out_odd=v_w_out_odd, mlp_w1=v_mlp_w1, mlp_w2=v_mlp_w2)
    names = list(weights)
    L = x.shape[1]
    x0 = x[0]
    target = loss_target[0]
    my_index = 4 * lax.axis_index("x") + 2 * lax.axis_index("y") + lax.axis_index("c")

    small_vec = jnp.zeros((8, LANES), F32)
    small_vec = small_vec.at[0, :64].set(pool_scale[0]).at[1, :64].set(sgu_ln_g[0]).at[2, :64].set(sgu_ln_b[0])
    ag_groups, ag_token = _comm_start(
        [[jnp.transpose(w_in_even[0]).astype(BF16), small_vec],
         [s5_w_glu[0].astype(BF16), w_out_even[0].astype(BF16)],
         [mlp_w1[0].astype(BF16), mlp_w2[0].astype(BF16)],
         [jnp.transpose(w_in_odd[0]).astype(BF16), w_out_odd[0].astype(BF16), mlp_w1[1].astype(BF16), mlp_w2[1].astype(BF16)]],
        "ag_start", exchange=False)

    lam_r = jnp.concatenate([s5_lam_re.reshape(1, S5_NS), s5_lam_im.reshape(1, S5_NS)], axis=0)
    ldt_r = jnp.repeat(s5_log_dt.reshape(32), 64).reshape(1, S5_NS)
    lam_c = jnp.transpose(lam_r)
    ldt_c = jnp.transpose(ldt_r)
    b_t = jnp.stack([jnp.tile(s5_b_re.reshape(S5_NS, 16), (1, 8)), jnp.tile(s5_b_im.reshape(S5_NS, 16), (1, 8))])
    c_t = jnp.stack([jnp.tile(s5_c_re.reshape(S5_W, 64), (1, 8)), jnp.tile(s5_c_im.reshape(S5_W, 64), (1, 8))])
    bf_pad = jnp.pad(fox_b_f, ((0, 0), (0, LANES - 8)))
    b_st = jnp.transpose(sgu_b_s[0])

    h0, rx0 = _rms_fwd(x0, _tie(mix_pre_g[0:1], ag_token), "rms0")
    tabs, bset, cset = _s5_prep(lam_r, ldt_r, lam_c, ldt_c, b_t, c_t, "s5_prep")
    ag0 = _comm_wait(ag_groups[0], tabs, "ag_wait0", exchange=False)
    winT_e = jnp.pad(ag0[0].reshape(EVEN_IN, D_MODEL), ((0, EVEN_PAD - EVEN_IN), (0, 0)))
    pool_scale_f = ag0[1][:, 0, :64].reshape(1, 512)
    ln_g_f = ag0[1][:, 1, :64].reshape(1, 512)
    ln_b_f = ag0[1][:, 2, :64].reshape(1, 512)
    z0 = _mm(h0, winT_e, name="win_even", tb=True, bm=512, bn=EVEN_PAD)
    xs, ylin = _s5_scan_fwd(z0, bset, cset, s5_d, tabs, "s5_scan")
    ag1 = _comm_wait(ag_groups[1], ylin, "ag_wait1", exchange=False)
    wglu = ag1[0].reshape(S5_W, S5_W)
    wout_e = ag1[1].reshape(D_MODEL, D_MODEL)
    ya = _s5_glu_fwd(ylin, wglu, "s5_glu")
    fcum, fq = _fox_f_fwd(z0, bf_pad, "fox_f")
    frow = jnp.transpose(fcum[:, :8]).reshape(4, 2, L)
    o_att, lse = _fox_fwd(z0, fq, frow, "fox_fwd")
    mix0 = [ya, o_att]
    x1, ry0, h1, rx1, y0 = _mm(mix0, wout_e, name="wout_even", epi=_epi_post_pre, extra=(x0,),
                               vecs=(mix_post_g[0:1], mlp_pre_g[0:1]), out_dtypes=POST_PRE_DTYPES,
                               out_kinds=POST_PRE_KINDS, bm=FUSED_ROWS)
    ag2 = _comm_wait(ag_groups[2], rx1, "ag_wait2", exchange=False)
    w1 = [ag2[0], None]
    w2 = [ag2[1].reshape(4 * D_MODEL, D_MODEL), None]
    p0, a0 = _mm(h1, w1[0], name="mlp0_w1", b3=True, out_dtypes=(BF16, BF16), epi=_epi_relu2, bm=512, bn=4 * D_MODEL)
    x2, ro0, h2, rx2, o0 = _mm(a0, w2[0], name="mlp0_w2", epi=_epi_post_pre, extra=(x1,),
                               vecs=(mlp_post_g[0:1], mix_pre_g[1:2]), out_dtypes=POST_PRE_DTYPES,
                               out_kinds=POST_PRE_KINDS, bm=FUSED_ROWS, bk=4 * D_MODEL)
    ag3 = _comm_wait(ag_groups[3], rx2, "ag_wait3", exchange=False)
    winT_o = ag3[0].reshape(ODD_IN, D_MODEL)
    wout_o = ag3[1].reshape(D_MODEL, D_MODEL)
    w1[1] = ag3[2]
    w2[1] = ag3[3].reshape(4 * D_MODEL, D_MODEL)
    z1 = _mm(h2, winT_o, name="win_odd", tb=True, bn=ODD_IN)
    yc, pooled = _pool_fwd(z1, pool_w[0], pool_scale_f, "pool_fwd")
    yd = _sgu_fwd(z1, ln_g_f, ln_b_f, sgu_w_s[0], b_st, "sgu_fwd")
    mix1 = [yc, yd]
    x3, ry1, h3, rx3, y1 = _mm(mix1, wout_o, name="wout_odd", epi=_epi_post_pre, extra=(x2,),
                               vecs=(mix_post_g[1:2], mlp_pre_g[1:2]), out_dtypes=POST_PRE_DTYPES,
                               out_kinds=POST_PRE_KINDS, bm=FUSED_ROWS)
    p1, a1 = _mm(h3, w1[1], name="mlp1_w1", b3=True, out_dtypes=(BF16, BF16), epi=_epi_relu2, bm=512, bn=4 * D_MODEL)
    gx4, g_o1, gg_mlp_post1, sq_lanes = _mm(
        a1, w2[1], name="mlp1_w2", epi=_epi_post_loss, extra=(x3, target), vecs=(mlp_post_g[1:2],),
        out_dtypes=(F32, BF16, F32, F32), out_kinds=("full", "full", "vsum", "vsum"), bm=FUSED_ROWS, bk=4 * D_MODEL)
    sq = sq_lanes[:, 0:1]

    g_p1 = _mm(g_o1, w2[1], name="b_mlp1_a", tb=True, out_dtypes=(BF16,), epi=_epi_relu2_bwd, extra=(p1,),
               bm=512, bn=4 * D_MODEL)
    gw2_1 = _mm(a1, g_o1, name="b_mlp1_w2", ta=True, bm=512, bk=L)
    gw1_1 = _mm(h3, g_p1, name="b_mlp1_w1", ta=True, out3=True, bn=512, bk=L)
    (ex1,), tok1 = _comm_start([[gw1_1, gw2_1.reshape(N_DEV, 512, D_MODEL)]], "ex_start1", exchange=True)
    g_x3, gg_mlp_pre1, g_y1, gg_mix_post1 = _mm(
        g_p1, w1[1], name="b_mlp1_h", tb=True, b3=True, epi=_epi_pre_post_bwd, extra=(x3, gx4, y1), cols=(rx3, ry1),
        vecs=(_tie(mlp_pre_g[1:2], tok1), mix_post_g[1:2]), out_dtypes=PRE_POST_BWD_DTYPES,
        out_kinds=PRE_POST_BWD_KINDS, bm=FUSED_ROWS, bk=4 * D_MODEL)
    gwout_o = _mm(mix1, g_y1, name="b_wout_odd_w", ta=True)
    g_xc, g_pool_w, g_pool_scale = _pool_bwd(g_y1, wout_o, pooled, pool_w[0], pool_scale_f, "pool_bwd")
    g_u1, g_v1, g_ws, g_bst, g_ln_g, g_ln_b = _sgu_bwd(g_y1, wout_o, z1, ln_g_f, ln_b_f, sgu_w_s[0], b_st,
                                                       "sgu_bwd")
    g_z1 = [g_xc, g_u1, g_v1]
    gwinT_o = _mm(g_z1, h2, name="b_win_odd_w", ta=True)
    (ex2,), tok2 = _comm_start([[gwout_o.reshape(N_DEV, 128, D_MODEL), gwinT_o.reshape(N_DEV, ODD_IN // N_DEV, D_MODEL)]], "ex_start2", exchange=True)
    g_x2, gg_mix_pre1, g_o0, gg_mlp_post0 = _mm(
        g_z1, winT_o, name="b_win_odd_h", epi=_epi_pre_post_bwd, extra=(x2, g_x3, o0), cols=(rx2, ro0),
        vecs=(_tie(mix_pre_g[1:2], tok2), mlp_post_g[0:1]), out_dtypes=PRE_POST_BWD_DTYPES,
        out_kinds=PRE_POST_BWD_KINDS, bm=FUSED_ROWS)
    g_p0 = _mm(g_o0, w2[0], name="b_mlp0_a", tb=True, out_dtypes=(BF16,), epi=_epi_relu2_bwd, extra=(p0,),
               bm=512, bn=4 * D_MODEL)
    gw2_0 = _mm(a0, g_o0, name="b_mlp0_w2", ta=True, bm=512, bk=L)
    gw1_0 = _mm(h1, g_p0, name="b_mlp0_w1", ta=True, out3=True, bn=512, bk=L)
    (ex3,), tok3 = _comm_start([[gw1_0, gw2_0.reshape(N_DEV, 512, D_MODEL)]], "ex_start3", exchange=True)
    g_x1, gg_mlp_pre0, g_y0, gg_mix_post0 = _mm(
        g_p0, w1[0], name="b_mlp0_h", tb=True, b3=True, epi=_epi_pre_post_bwd, extra=(x1, g_x2, y0), cols=(rx1, ry0),
        vecs=(_tie(mlp_pre_g[0:1], tok3), mix_post_g[0:1]), out_dtypes=PRE_POST_BWD_DTYPES,
        out_kinds=PRE_POST_BWD_KINDS, bm=FUSED_ROWS, bk=4 * D_MODEL)
    g_o_att = _mm(g_y0, wout_e[FOX_W:], name="b_wout_even_m", tb=True)
    gwout_e = _mm(mix0, g_y0, name="b_wout_even_w", ta=True)
    gyl, gud, g_wglu, g_d = _s5_glu_bwd(g_y0, wout_e, ylin, z0, s5_d, wglu, "s5_glu_bwd")
    (ex4,), tok4 = _comm_start([[gwout_e.reshape(N_DEV, 128, D_MODEL), g_wglu.reshape(N_DEV, 64, S5_W)]], "ex_start4", exchange=True)
    g_u0, ga, gb_raw, gc_raw = _s5_scan_bwd(gyl, _tie(cset, tok4), xs, z0, bset, gud, tabs, "s5_scan_bwd")
    g_lam, g_ldt, g_b, g_c = _s5_param_bwd(lam_c, ldt_c, b_t, gb_raw, jnp.transpose(ga), gc_raw, "s5_param_bwd")
    dq, dk, dv, dfq, dfrow = _fox_bwd(z0, frow, o_att, lse, g_o_att, "fox_bwd")
    dFk = jnp.pad(jnp.transpose(dfrow.reshape(8, L)), ((0, 0), (0, LANES - 8)))
    dfl, db_f = _fox_f_bwd(dFk, dfq, z0, bf_pad, "fox_f_bwd")
    g_z0 = [g_u0, dq, dk, dv, dfl]
    grad_x, gg_mix_pre0 = _mm(g_z0, winT_e, name="b_win_even_h", epi=_epi_pre_bwd, extra=(x0, g_x1), cols=(rx0,),
                              vecs=(mix_pre_g[0:1],), out_dtypes=(F32, F32), out_kinds=("full", "vsum"),
                              bm=FUSED_ROWS)

    small_grads = dict(
        mix_pre_g=jnp.concatenate([gg_mix_pre0, gg_mix_pre1]), mix_post_g=jnp.concatenate([gg_mix_post0, gg_mix_post1]),
        mlp_pre_g=jnp.concatenate([gg_mlp_pre0, gg_mlp_pre1]), mlp_post_g=jnp.concatenate([gg_mlp_post0, gg_mlp_post1]),
        s5_lam_re=g_lam[:, 0], s5_lam_im=g_lam[:, 1],
        s5_b_re=g_b[0, :, :16], s5_b_im=g_b[1, :, :16], s5_c_re=g_c[0, :, :64], s5_c_im=g_c[1, :, :64],
        pool_w=g_pool_w, sgu_w_s=g_ws, s5_d=g_d, sgu_b_s=jnp.transpose(g_bst),
        pool_scale=g_pool_scale, sgu_ln_g=g_ln_g, sgu_ln_b=g_ln_b, s5_log_dt=g_ldt, fox_b_f=db_f[:, :8])
    small_names = list(small_grads)
    full_shapes = [(512,) if nm in ("pool_scale", "sgu_ln_g", "sgu_ln_b") else weights[nm].shape for nm in small_names]
    full_shapes.append((1, 1))
    rows = _packed_rows(full_shapes)
    packed = _pack([small_grads[nm] for nm in small_names] + [sq], rows).reshape(N_DEV, rows // N_DEV, LANES)
    (exs,), tok_s = _comm_start([[packed]], "exs_start", exchange=True)
    gwinT_e = _mm(g_z0, h0, name="b_win_even_w", ta=True, out_dtypes=(BF16,), dep=tok_s)
    (recv_small,) = _comm_wait(exs, gwinT_e, "exs_wait", exchange=True)
    piece = _sum_pieces(recv_small, "sum_small")
    gwinT_e_pieces = gwinT_e[:EVEN_IN].reshape(N_DEV, EVEN_IN // N_DEV, D_MODEL)
    (ags, ex5), tok5 = _comm_start([[piece], [gwinT_e_pieces]], "ags_ex_start5", exchange=(False, True))
    r_w1_1, r_w2_1 = _comm_wait(ex1, tok5, "ex_wait1", exchange=True)
    r_wout_o, r_win_o = _comm_wait(ex2, tok5, "ex_wait2", exchange=True)
    r_w1_0, r_w2_0 = _comm_wait(ex3, tok5, "ex_wait3", exchange=True)
    r_wout_e, r_wglu = _comm_wait(ex4, tok5, "ex_wait4", exchange=True)

    res = {}
    big_parts = dict(mlp_w1=[r_w1_0, r_w1_1], mlp_w2=[r_w2_0, r_w2_1], s5_w_glu=[r_wglu], w_out_even=[r_wout_e],
                     w_out_odd=[r_wout_o])
    for nm, parts in big_parts.items():
        res[nm] = tuple(_sum_adamw(parts, weights[nm], mom_m[nm], mom_v[nm], "adamw_" + nm))
    done = [res[nm][1] for nm in ("mlp_w1", "mlp_w2", "s5_w_glu", "w_out_even", "w_out_odd")]

    (small_all,) = _comm_wait(ags, done, "ags_wait", exchange=False)
    small_full = _unpack(small_all.reshape(rows, LANES), full_shapes)
    loss = 0.5 * small_full.pop()[0, 0] / D_MODEL
    small_g = []
    for nm, g in zip(small_names, small_full):
        if nm in ("pool_scale", "sgu_ln_g", "sgu_ln_b"):
            g = lax.dynamic_slice(g, (my_index * 64,), (64,)).reshape(1, 64)
        small_g.append(g)

    def turned(arrs):
        return [jnp.swapaxes(a, -1, -2) if nm in ("s5_b_re", "s5_b_im") else a for nm, a in zip(small_names, arrs)]

    sd, sm, sv = _adamw_many(turned([weights[nm] for nm in small_names]), turned(small_g),
                             turned([mom_m[nm] for nm in small_names]), turned([mom_v[nm] for nm in small_names]),
                             "adamw_small")
    for nm, g_, d_, m_, v_ in zip(small_names, small_g, turned(sd), turned(sm), turned(sv)):
        res[nm] = (g_, d_, m_, v_)
    done.append(sd[0])

    nm = "w_in_odd"
    outs = _sum_adamw([r_win_o], jnp.transpose(weights[nm], (0, 2, 1)), jnp.transpose(mom_m[nm], (0, 2, 1)),
                      jnp.transpose(mom_v[nm], (0, 2, 1)), "adamw_" + nm)
    res[nm] = tuple(jnp.transpose(o, (0, 2, 1)) for o in outs)
    done.append(res[nm][1])
    nm = "w_in_even"
    (r_win_e,) = _comm_wait(ex5, done, "ex_wait5", exchange=True)
    outs = _sum_adamw_rows(r_win_e, jnp.transpose(weights[nm], (2, 0, 1)), jnp.transpose(mom_m[nm], (2, 0, 1)),
                           jnp.transpose(mom_v[nm], (2, 0, 1)), "adamw_" + nm)
    res[nm] = tuple(jnp.transpose(o, (1, 2, 0)) for o in outs)

    grads = [res[nm][0].reshape(weights[nm].shape) for nm in names]
    deltas = [res[nm][1].reshape(weights[nm].shape) for nm in names]
    new_m = [res[nm][2].reshape(weights[nm].shape) for nm in names]
    new_v = [res[nm][3].reshape(weights[nm].shape) for nm in names]
    return (loss, grad_x[None], *grads, *deltas, *new_m, *new_v)
```

```python
import math

import jax
import jax.numpy as jnp
from jax import lax
from jax.experimental import pallas as pl
from jax.experimental.pallas import tpu as pltpu

F32 = jnp.float32
BF16 = jnp.bfloat16
MESH = pl.DeviceIdType.MESH
ANY = pl.BlockSpec(memory_space=pl.ANY)

N_DEV = 8
D_MODEL = 1024
EPS = 1e-6
NORM_ROWS = 512
FUSED_ROWS = 512
S5_W = 512
S5_NS = 2048
SCAN_GROUPS = 4
SCAN_CHUNK = 1024
FOX_W = 512
EVEN_IN = 2056
EVEN_PAD = 2176
ODD_IN = 1536
LANES = 128
PIECE = 4 * D_MODEL // N_DEV
VMEM_LIMIT = 56 * 1024 * 1024

ADAM_LR = 0.001
ADAM_B1 = 0.9
ADAM_B2 = 0.999
ADAM_EPS = 1e-08
ADAM_WD = 0.01
ADAM_STEP = 10

NT = (((1,), (1,)), ((), ()))
TN = (((0,), (0,)), ((), ()))
NN = (((1,), (0,)), ((), ()))


def _cp(*sem):
    return pltpu.CompilerParams(dimension_semantics=sem, vmem_limit_bytes=VMEM_LIMIT)


def _sds(shape, dtype=F32):
    return jax.ShapeDtypeStruct(tuple(shape), dtype)


def _gelu(x):
    t = jnp.tanh(0.7978845608028654 * (x + 0.044715 * x * x * x))
    return 0.5 * x * (1.0 + t)


def _gelu_grad(x):
    t = jnp.tanh(0.7978845608028654 * (x + 0.044715 * x * x * x))
    du = 0.7978845608028654 * (1.0 + 3.0 * 0.044715 * x * x)
    return 0.5 * (1.0 + t) + 0.5 * x * (1.0 - t * t) * du


def _sigmoid(x):
    return 1.0 / (1.0 + jnp.exp(-x))


def _dot(a, b, dn=NN):
    return lax.dot_general(a, b, dn, preferred_element_type=F32)


def _mm(a, b, *, name, ta=False, tb=False, b3=False, out3=False, out_dtypes=(F32,), epi=None, extra=(),
        cols=(), vecs=(), out_kinds=None, bm=1024, bn=1024, bk=1024, dep=None):
    a_list = list(a) if isinstance(a, (list, tuple)) else [a]
    widths = [p.shape[1] for p in a_list]
    offs = [sum(widths[:i]) for i in range(len(widths))]
    na = len(a_list)
    M = sum(widths) if ta else a_list[0].shape[0]
    K = a_list[0].shape[0] if ta else sum(widths)
    if na > 1:
        assert not b3 and not tb
        bm, bk = (M, bk) if ta else (bm, K)
    pw = b.shape[2] if b3 else PIECE
    if b3:
        N = b.shape[1] if tb else b.shape[0] * pw
        assert (b.shape[0] * pw if tb else b.shape[1]) == K
    else:
        N = b.shape[0] if tb else b.shape[1]
    bm, bn, bk = min(bm, M), min(bn, N), min(bk, K)
    assert M % bm == 0 and N % bn == 0 and K % bk == 0, (name, M, N, K, bm, bn, bk)
    assert not (b3 or out3) or ((bk if tb else bn) % pw == 0 and bn % PIECE == 0)
    nk = K // bk
    n_extra = len(extra) + len(cols) + len(vecs)
    n_out = len(out_dtypes)
    out_kinds = tuple(out_kinds) if out_kinds is not None else ("full",) * n_out
    dn = (((0 if ta else 1,), (1 if tb else 0,)), ((), ()))

    use_acc = nk > 1

    def body(*refs):
        a_refs, b_ref = refs[:na], refs[na]
        a_ref = a_refs[0]
        e_refs = refs[na + 1:na + 1 + n_extra]
        first_out = na + 1 + n_extra + (0 if dep is None else 1)
        o_refs = refs[first_out:first_out + n_out]
        acc_ref = refs[-1] if use_acc else o_refs[0]
        i, k = pl.program_id(0), pl.program_id(2)

        def dot(a_v, b_v):
            return lax.dot_general(a_v.astype(BF16), b_v.astype(BF16), dn, preferred_element_type=F32)

        everything = slice(None)
        if na > 1 and ta:
            terms = [(pl.ds(off, w), everything, r, b_ref) for r, off, w in zip(a_refs, offs, widths)]
        elif na > 1:
            terms = [(everything, everything, r, b_ref.at[pl.ds(off, w), :]) for r, off, w in zip(a_refs, offs, widths)]
        elif not b3:
            terms = [(everything, everything, a_ref, b_ref)]
        elif tb:
            terms = [(everything, everything,
                      a_ref.at[pl.ds(t * pw, pw), :] if ta else a_ref.at[:, pl.ds(t * pw, pw)], b_ref.at[t])
                     for t in range(bk // pw)]
        else:
            terms = [(everything, pl.ds(t * pw, pw), a_ref, b_ref.at[t]) for t in range(bn // pw)]

        def finish(acc):
            outs = (acc,) if epi is None else epi(acc, *[e[...] for e in e_refs])
            for o_ref, o, kind in zip(o_refs, outs, out_kinds):
                if kind == "vsum":
                    @pl.when(i == 0)
                    def _(o_ref=o_ref, o=o):
                        o_ref[...] = o

                    @pl.when(i > 0)
                    def _(o_ref=o_ref, o=o):
                        o_ref[...] += o
                elif out3:
                    for t in range(bn // PIECE):
                        o_ref[t] = o[:, t * PIECE:(t + 1) * PIECE].astype(o_ref.dtype)
                else:
                    o_ref[...] = o.astype(o_ref.dtype)

        if nk == 1:
            bands = {}
            for rows, cols, a_r, b_r in terms:
                key = (getattr(rows, "start", None), getattr(cols, "start", None))
                val = dot(a_r[...], b_r[...])
                bands[key] = val if key not in bands else bands[key] + val
            vals = list(bands.values())
            if len(vals) == 1:
                finish(vals[0])
            else:
                finish(jnp.concatenate(vals, axis=0 if (na > 1 and ta) else 1))
            return

        @pl.when(k == 0)
        def _():
            acc_ref[...] = jnp.zeros_like(acc_ref)

        for rows, cols, a_r, b_r in terms:
            acc_ref[rows, cols] += dot(a_r[...], b_r[...])

        @pl.when(k == nk - 1)
        def _():
            finish(acc_ref[...])

    if na > 1:
        a_specs = [pl.BlockSpec((bk, w), lambda i, j, k: (k, 0)) if ta else pl.BlockSpec((bm, w), lambda i, j, k: (i, 0))
                   for w in widths]
    else:
        a_specs = [pl.BlockSpec((bk, bm), lambda i, j, k: (k, i)) if ta else
                   pl.BlockSpec((bm, bk), lambda i, j, k: (i, k))]
    if b3:
        if tb:
            b_spec = pl.BlockSpec((bk // pw, bn, pw), lambda i, j, k: (k, j, 0))
        else:
            b_spec = pl.BlockSpec((bn // pw, bk, pw), lambda i, j, k: (j, k, 0))
    else:
        b_spec = pl.BlockSpec((bn, bk), lambda i, j, k: (j, k)) if tb else pl.BlockSpec((bk, bn), lambda i, j, k: (k, j))
    e_specs = ([pl.BlockSpec((bm, bn), lambda i, j, k: (i, j)) for _ in extra]
               + [pl.BlockSpec((bm, 1), lambda i, j, k: (i, 0)) for _ in cols]
               + [pl.BlockSpec((1, bn), lambda i, j, k: (0, j)) for _ in vecs])
    if out3:
        o_specs = [pl.BlockSpec((bn // PIECE, bm, PIECE), lambda i, j, k: (j, i, 0)) for _ in out_dtypes]
        o_shapes = [_sds((N // PIECE, M, PIECE), dt) for dt in out_dtypes]
    else:
        spec_of = {"full": pl.BlockSpec((bm, bn), lambda i, j, k: (i, j)),
                   "col": pl.BlockSpec((bm, 1), lambda i, j, k: (i, 0)),
                   "vsum": pl.BlockSpec((1, bn), lambda i, j, k: (0, j))}
        shape_of = {"full": (M, N), "col": (M, 1), "vsum": (1, N)}
        o_specs = [spec_of[kind] for kind in out_kinds]
        o_shapes = [_sds(shape_of[kind], dt) for kind, dt in zip(out_kinds, out_dtypes)]
    assert "col" not in out_kinds or bn == N
    outs = pl.pallas_call(
        body, name=name, grid=(M // bm, N // bn, nk),
        in_specs=a_specs + [b_spec] + e_specs + ([] if dep is None else [ANY]),
        out_specs=o_specs, out_shape=o_shapes,
        scratch_shapes=[pltpu.VMEM((bm, bn), F32)] if use_acc else [],
        compiler_params=_cp("arbitrary" if "vsum" in out_kinds else "parallel", "parallel", "arbitrary"),
    )(*a_list, b, *extra, *cols, *vecs, *([] if dep is None else [dep]))
    return outs[0] if n_out == 1 else outs


def _epi_relu2(acc):
    r = jnp.maximum(acc, 0.0)
    return acc, r * r


def _epi_relu2_bwd(acc, p):
    return (acc * (2.0 * jnp.maximum(p.astype(F32), 0.0)),)


def _row_spec(rb, w=D_MODEL):
    return pl.BlockSpec((rb, w), lambda i: (i, 0))


def _vec_spec(w=D_MODEL):
    return pl.BlockSpec((1, w), lambda i: (0, 0))


def _rstd(v):
    return lax.rsqrt(jnp.mean(v * v, axis=-1, keepdims=True) + EPS)


def _rms_fwd(x, g, name):
    L = x.shape[0]
    rb = min(NORM_ROWS, L)

    def body(x_ref, g_ref, h_ref, r_ref):
        xv = x_ref[...]
        r = _rstd(xv)
        h_ref[...] = (xv * r * g_ref[...]).astype(BF16)
        r_ref[...] = r

    return pl.pallas_call(
        body, name=name, grid=(L // rb,),
        in_specs=[_row_spec(rb), _vec_spec()],
        out_specs=[_row_spec(rb), _row_spec(rb, 1)],
        out_shape=[_sds((L, D_MODEL), BF16), _sds((L, 1))],
        compiler_params=_cp("parallel"),
    )(x, g)


def _rms_bwd_rows(dy, xv, r, g):
    n = xv * r
    dyg = dy * g
    return r * (dyg - n * jnp.mean(dyg * n, axis=-1, keepdims=True)), n


POST_PRE_DTYPES = (F32, F32, BF16, F32, F32)
POST_PRE_KINDS = ("full", "col", "full", "col", "full")
PRE_POST_BWD_DTYPES = (F32, F32, BF16, F32)
PRE_POST_BWD_KINDS = ("full", "vsum", "full", "vsum")


def _epi_post_pre(y, x_in, g_post, g_pre):
    ry = _rstd(y)
    xo = x_in + y * ry * g_post
    rx = _rstd(xo)
    return xo, ry, xo * rx * g_pre, rx, y


def _epi_pre_post_bwd(gh, x, g_out, y_prev, rx, ry_prev, g_pre, g_post_prev):
    gx, n = _rms_bwd_rows(gh, x, rx, g_pre)
    gi = g_out + gx
    gy, ny = _rms_bwd_rows(gi, y_prev, ry_prev, g_post_prev)
    return gi, jnp.sum(gh * n, axis=0, keepdims=True), gy, jnp.sum(gi * ny, axis=0, keepdims=True)


def _epi_pre_bwd(gh, x, g_out, rx, g_pre):
    gx, n = _rms_bwd_rows(gh, x, rx, g_pre)
    return g_out + gx, jnp.sum(gh * n, axis=0, keepdims=True)


def _epi_post_loss(y, x_in, target, g_post):
    ry = _rstd(y)
    diff = x_in + y * ry * g_post - target
    gx = diff * (1.0 / D_MODEL)
    gy, n = _rms_bwd_rows(gx, y, ry, g_post)
    sq = jnp.broadcast_to(jnp.sum(diff * diff, keepdims=True), (1, y.shape[1]))
    return gx, gy, jnp.sum(gx * n, axis=0, keepdims=True), sq


def _cmul(ar, ai, br, bi):
    return ar * br - ai * bi, ar * bi + ai * br


def _zoh_cols(lr, li, ldt):
    dt = jnp.exp(ldt)
    mag = jnp.exp(lr * dt)
    ar = mag * jnp.cos(li * dt)
    ai = mag * jnp.sin(li * dt)
    den = lr * lr + li * li
    nr = ar - 1.0
    qr = (nr * lr + ai * li) / den
    qi = (ai * lr - nr * li) / den
    return dt, ar, ai, qr, qi, den


def _b_mask():
    r = lax.broadcasted_iota(jnp.int32, (S5_NS, LANES), 0)
    c = lax.broadcasted_iota(jnp.int32, (S5_NS, LANES), 1)
    return ((r >> 6) & 7) == (c >> 4)


def _c_mask():
    r = lax.broadcasted_iota(jnp.int32, (S5_W, 512), 0)
    c = lax.broadcasted_iota(jnp.int32, (S5_W, 512), 1)
    return ((r >> 4) & 7) == (c >> 6)


def _s5_prep(lam_r, ldt_r, lam_c, ldt_c, b_t, c_t, name):
    def body(lam_r_ref, ldt_r_ref, lam_c_ref, ldt_c_ref, b_ref, c_ref, tab_ref, bset_ref, cset_ref):
        lr, li = lam_r_ref[0:1, :], lam_r_ref[1:2, :]
        dt = jnp.exp(ldt_r_ref[...])
        mag = jnp.exp(lr * dt)
        p1r, p1i = mag * jnp.cos(li * dt), mag * jnp.sin(li * dt)
        p2r, p2i = _cmul(p1r, p1i, p1r, p1i)
        p3r, p3i = _cmul(p2r, p2i, p1r, p1i)
        p4r, p4i = _cmul(p2r, p2i, p2r, p2i)
        p5r, p5i = _cmul(p4r, p4i, p1r, p1i)
        p6r, p6i = _cmul(p4r, p4i, p2r, p2i)
        p7r, p7i = _cmul(p4r, p4i, p3r, p3i)
        p8r, p8i = _cmul(p4r, p4i, p4r, p4i)
        pw_r = [p1r, p2r, p3r, p4r, p5r, p6r, p7r, p8r]
        pw_i = [p1i, p2i, p3i, p4i, p5i, p6i, p7i, p8i]
        row = lax.broadcasted_iota(jnp.int32, (8, S5_NS), 0)
        zero = jnp.zeros((8, S5_NS), F32)

        def bc(v):
            return jnp.broadcast_to(v, (8, S5_NS))

        for d in range(2):
            sgn = 1.0 if d == 0 else -1.0
            for t, s in enumerate((1, 2, 4)):
                live = (row >= s) if d == 0 else (row <= 7 - s)
                tab_ref[d, 2 * t] = jnp.where(live, bc(pw_r[s - 1]), zero)
                tab_ref[d, 2 * t + 1] = jnp.where(live, bc(sgn * pw_i[s - 1]), zero)
            cr, ci = zero, zero
            for i in range(8):
                e = i if d == 0 else 7 - i
                cr = jnp.where(row == i, bc(pw_r[e]), cr)
                ci = jnp.where(row == i, bc(sgn * pw_i[e]), ci)
            tab_ref[d, 6] = cr
            tab_ref[d, 7] = ci

        _, _, _, qr, qi, _ = _zoh_cols(lam_c_ref[:, 0:1], lam_c_ref[:, 1:2], ldt_c_ref[...])
        bm = _b_mask()
        br, bi = b_ref[0], b_ref[1]
        bset_ref[0] = jnp.where(bm, qr * br - qi * bi, 0.0).astype(BF16)
        bset_ref[1] = jnp.where(bm, qr * bi + qi * br, 0.0).astype(BF16)
        cm = _c_mask()
        cset_ref[0] = jnp.where(cm, c_ref[0], 0.0).astype(BF16)
        cset_ref[1] = jnp.where(cm, c_ref[1], 0.0).astype(BF16)

    vm = pl.BlockSpec(memory_space=pltpu.VMEM)
    return pl.pallas_call(
        body, name=name, in_specs=[vm] * 6, out_specs=[vm] * 3,
        out_shape=[_sds((2, 8, 8, S5_NS)), _sds((2, S5_NS, LANES), BF16), _sds((2, S5_W, 512), BF16)],
        compiler_params=pltpu.CompilerParams(vmem_limit_bytes=VMEM_LIMIT),
    )(lam_r, ldt_r, lam_c, ldt_c, b_t, c_t)


SCAN_W = SCAN_GROUPS * LANES


def _scan_chunk(src_ref, dst_ref, tab_ref, carry_ref, nb, reverse, xs_ref=None, acc_ref=None):
    row = lax.broadcasted_iota(jnp.int32, (8, LANES), 0)

    def step(i, carry):
        b = (nb - 1 - i) if reverse else i
        off = pl.multiple_of(b * 8, 8)
        out = []
        for g in range(SCAN_GROUPS):
            lanes = pl.ds(g * LANES, LANES)
            cr, ci = carry[2 * g], carry[2 * g + 1]
            yr = src_ref[0, pl.ds(off, 8), lanes]
            yi = src_ref[1, pl.ds(off, 8), lanes]
            for t, s in enumerate((1, 2, 4)):
                sh = (8 - s) if reverse else s
                sr = pltpu.roll(yr, sh, 0)
                si = pltpu.roll(yi, sh, 0)
                mr, mi = tab_ref[2 * t, :, lanes], tab_ref[2 * t + 1, :, lanes]
                yr, yi = yr + mr * sr - mi * si, yi + mr * si + mi * sr
            pr, pi = tab_ref[6, :, lanes], tab_ref[7, :, lanes]
            yr, yi = yr + pr * cr - pi * ci, yi + pr * ci + pi * cr
            dst_ref[0, pl.ds(off, 8), lanes] = yr
            dst_ref[1, pl.ds(off, 8), lanes] = yi
            if xs_ref is not None:
                nr = jnp.where(row == 7, cr, pltpu.roll(yr, 7, 0))
                ni = jnp.where(row == 7, ci, pltpu.roll(yi, 7, 0))
                xr = xs_ref[0, pl.ds(off, 8), lanes]
                xi = xs_ref[1, pl.ds(off, 8), lanes]
                acc_ref[0, :, lanes] += xr * nr + xi * ni
                acc_ref[1, :, lanes] += xr * ni - xi * nr
            last = 0 if reverse else 7
            out += [jnp.broadcast_to(yr[last:last + 1, :], (8, LANES)),
                    jnp.broadcast_to(yi[last:last + 1, :], (8, LANES))]
        return tuple(out)

    init = []
    for g in range(SCAN_GROUPS):
        init += [carry_ref[0, :, pl.ds(g * LANES, LANES)], carry_ref[1, :, pl.ds(g * LANES, LANES)]]
    fin = lax.fori_loop(0, nb, step, tuple(init))
    for g in range(SCAN_GROUPS):
        carry_ref[0, :, pl.ds(g * LANES, LANES)] = fin[2 * g]
        carry_ref[1, :, pl.ds(g * LANES, LANES)] = fin[2 * g + 1]


def _s5_scan_fwd(z, bset, cset, dvec, tabs, name):
    L = z.shape[0]
    tl = min(SCAN_CHUNK, L)
    nc = L // tl

    def body(u_ref, b_ref, c_ref, d_ref, tab_ref, x_ref, y_ref, carry_ref):
        @pl.when(pl.program_id(1) == 0)
        def _():
            carry_ref[...] = jnp.zeros_like(carry_ref)

        uf = u_ref[...]
        u = uf.astype(BF16)
        x_ref[0] = _dot(u, b_ref[0], NT)
        x_ref[1] = _dot(u, b_ref[1], NT)
        _scan_chunk(x_ref, x_ref, tab_ref, carry_ref, tl // 8, False)
        y_ref[...] = (_dot(x_ref[0].astype(BF16), c_ref[0], NT) - _dot(x_ref[1].astype(BF16), c_ref[1], NT)
                      + d_ref[...] * uf)

    col = pl.BlockSpec((tl, LANES), lambda j, c: (c, j))
    return pl.pallas_call(
        body, name=name, grid=(S5_NS // SCAN_W, nc),
        in_specs=[col, pl.BlockSpec((2, SCAN_W, LANES), lambda j, c: (0, j, 0)),
                  pl.BlockSpec((2, LANES, SCAN_W), lambda j, c: (0, j, 0)),
                  pl.BlockSpec((1, LANES), lambda j, c: (0, j)),
                  pl.BlockSpec((None, 8, 8, SCAN_W), lambda j, c: (0, 0, 0, j))],
        out_specs=[pl.BlockSpec((2, tl, SCAN_W), lambda j, c: (0, c, j)), col],
        out_shape=[_sds((2, L, S5_NS)), _sds((L, S5_W))],
        scratch_shapes=[pltpu.VMEM((2, 8, SCAN_W), F32)],
        compiler_params=_cp("parallel", "arbitrary"),
    )(z, bset, cset, dvec, tabs)


def _s5_scan_bwd(gyl, cset, xs, z, bset, gud, tabs, name):
    L = z.shape[0]
    tl = min(SCAN_CHUNK, L)
    nc = L // tl

    def body(g_ref, c_ref, xs_ref, u_ref, b_ref, gud_ref, tab_ref, gu_ref, ga_ref, gb_ref, gc_ref,
             gx_ref, carry_ref, acc_ref):
        c = pl.program_id(1)

        @pl.when(c == 0)
        def _():
            carry_ref[...] = jnp.zeros_like(carry_ref)
            acc_ref[...] = jnp.zeros_like(acc_ref)
            gb_ref[...] = jnp.zeros_like(gb_ref)
            gc_ref[...] = jnp.zeros_like(gc_ref)

        gy = g_ref[...].astype(BF16)
        gx_ref[0] = _dot(gy, c_ref[0])
        gx_ref[1] = -_dot(gy, c_ref[1])
        gc_ref[0] += _dot(gy, xs_ref[0].astype(BF16), TN)
        gc_ref[1] -= _dot(gy, xs_ref[1].astype(BF16), TN)
        _scan_chunk(gx_ref, gx_ref, tab_ref, carry_ref, tl // 8, True, xs_ref, acc_ref)
        gr = gx_ref[0].astype(BF16)
        gi = gx_ref[1].astype(BF16)
        gu_ref[...] = gud_ref[...] + _dot(gr, b_ref[0]) + _dot(gi, b_ref[1])
        u = u_ref[...].astype(BF16)
        gb_ref[0] += _dot(gr, u, TN)
        gb_ref[1] += _dot(gi, u, TN)

        @pl.when(c == nc - 1)
        def _():
            ga_ref[0:1, :] = jnp.sum(acc_ref[0], axis=0, keepdims=True)
            ga_ref[1:2, :] = jnp.sum(acc_ref[1], axis=0, keepdims=True)

    rev = lambda j, c: (nc - 1 - c, j)
    col = pl.BlockSpec((tl, LANES), rev)
    return pl.pallas_call(
        body, name=name, grid=(S5_NS // SCAN_W, nc),
        in_specs=[col, pl.BlockSpec((2, LANES, SCAN_W), lambda j, c: (0, j, 0)),
                  pl.BlockSpec((2, tl, SCAN_W), lambda j, c: (0, nc - 1 - c, j)), col,
                  pl.BlockSpec((2, SCAN_W, LANES), lambda j, c: (0, j, 0)), col,
                  pl.BlockSpec((None, 8, 8, SCAN_W), lambda j, c: (1, 0, 0, j))],
        out_specs=[col, pl.BlockSpec((2, SCAN_W), lambda j, c: (0, j)),
                   pl.BlockSpec((2, SCAN_W, LANES), lambda j, c: (0, j, 0)),
                   pl.BlockSpec((2, LANES, SCAN_W), lambda j, c: (0, j, 0))],
        out_shape=[_sds((L, S5_W)), _sds((2, S5_NS)), _sds((2, S5_NS, LANES)), _sds((2, S5_W, 512))],
        scratch_shapes=[pltpu.VMEM((2, tl, SCAN_W), F32), pltpu.VMEM((2, 8, SCAN_W), F32),
                        pltpu.VMEM((2, 8, SCAN_W), F32)],
        compiler_params=_cp("parallel", "arbitrary"),
    )(gyl, cset, xs, z, bset, gud, tabs)


def _s5_glu_fwd(ylin, wglu, name):
    L = ylin.shape[0]
    bl = min(1024, L)

    def body(ylin_ref, w_ref, ya_ref):
        yg = _gelu(ylin_ref[...])
        t = _dot(yg.astype(BF16), w_ref[...])
        ya_ref[...] = (yg * _sigmoid(t)).astype(BF16)

    return pl.pallas_call(
        body, name=name, grid=(L // bl,),
        in_specs=[pl.BlockSpec((bl, S5_W), lambda i: (i, 0)), pl.BlockSpec((S5_W, S5_W), lambda i: (0, 0))],
        out_specs=pl.BlockSpec((bl, S5_W), lambda i: (i, 0)),
        out_shape=_sds((L, S5_W), BF16),
        compiler_params=_cp("parallel"),
    )(ylin, wglu)


def _s5_glu_bwd(g_y, wout, ylin, z, dvec, wglu, name):
    L = z.shape[0]
    bl = min(256, L)

    def body(g_ref, wo_ref, ylin_ref, u_ref, d_ref, w_ref, gyl_ref, gud_ref, gw_ref, gd_ref):
        i = pl.program_id(0)
        ylin = ylin_ref[...]
        yg = _gelu(ylin)
        ygb = yg.astype(BF16)
        sg = _sigmoid(_dot(ygb, w_ref[...]))
        gya = _dot(g_ref[...], wo_ref[...], NT)
        gt = gya * yg * sg * (1.0 - sg)
        gtb = gt.astype(BF16)
        gyg = gya * sg + _dot(gtb, w_ref[...], NT)
        gyl = gyg * _gelu_grad(ylin)
        gyl_ref[...] = gyl
        gud_ref[...] = gyl * d_ref[...]

        @pl.when(i == 0)
        def _():
            gw_ref[...] = jnp.zeros_like(gw_ref)
            gd_ref[...] = jnp.zeros_like(gd_ref)

        gw_ref[...] += _dot(ygb, gtb, TN)
        gd_ref[...] += jnp.sum(gyl * u_ref[...], axis=0, keepdims=True)

    blk = pl.BlockSpec((bl, S5_W), lambda i: (i, 0))
    return pl.pallas_call(
        body, name=name, grid=(L // bl,),
        in_specs=[pl.BlockSpec((bl, D_MODEL), lambda i: (i, 0)), pl.BlockSpec((S5_W, D_MODEL), lambda i: (0, 0)),
                  blk, blk, pl.BlockSpec((1, S5_W), lambda i: (0, 0)), pl.BlockSpec((S5_W, S5_W), lambda i: (0, 0))],
        out_specs=[blk, blk, pl.BlockSpec((S5_W, S5_W), lambda i: (0, 0)), pl.BlockSpec((1, S5_W), lambda i: (0, 0))],
        out_shape=[_sds((L, S5_W)), _sds((L, S5_W)), _sds((S5_W, S5_W)), _sds((1, S5_W))],
        compiler_params=_cp("arbitrary"),
    )(g_y, wout, ylin, z, dvec, wglu)


def _s5_param_bwd(lam_c, ldt_c, b_t, gb, ga_c, gc, name):
    def body(lam_ref, ldt_ref, b_ref, gb_ref, ga_ref, gc_ref, glam_ref, gldt_ref, gbo_ref, gco_ref):
        lr, li = lam_ref[:, 0:1], lam_ref[:, 1:2]
        dt, ar, ai, qr, qi, den = _zoh_cols(lr, li, ldt_ref[...])
        bm = _b_mask()
        gbr = jnp.where(bm, gb_ref[0], 0.0)
        gbi = jnp.where(bm, gb_ref[1], 0.0)
        br, bi = b_ref[0], b_ref[1]
        obr = gbr * qr + gbi * qi
        obi = gbi * qr - gbr * qi
        gqr = jnp.sum(gbr * br + gbi * bi, axis=1, keepdims=True)
        gqi = jnp.sum(gbi * br - gbr * bi, axis=1, keepdims=True)
        for s in (64, 32, 16):
            obr = obr + pltpu.roll(obr, s, 1)
            obi = obi + pltpu.roll(obi, s, 1)
        gbo_ref[0] = obr
        gbo_ref[1] = obi
        gar = ga_ref[:, 0:1] + (gqr * lr - gqi * li) / den
        gai = ga_ref[:, 1:2] + (gqr * li + gqi * lr) / den
        qlr = (qr * lr + qi * li) / den
        qli = (qi * lr - qr * li) / den
        glr = -(gqr * qlr + gqi * qli)
        gli = -(gqi * qlr - gqr * qli)
        glr = glr + dt * (gar * ar + gai * ai)
        gli = gli + dt * (gai * ar - gar * ai)
        wr, wi = _cmul(lr, li, ar, ai)
        gldt = (gar * wr + gai * wi) * dt
        glam_ref[:, 0:1] = glr
        glam_ref[:, 1:2] = gli
        r = lax.broadcasted_iota(jnp.int32, (S5_NS, 32), 0)
        c = lax.broadcasted_iota(jnp.int32, (S5_NS, 32), 1)
        gldt_ref[...] = jnp.sum(jnp.where((r >> 6) == c, gldt, 0.0), axis=0, keepdims=True)
        cm = _c_mask()
        for k in range(2):
            oc = jnp.where(cm, gc_ref[k], 0.0)
            for s in (256, 128, 64):
                oc = oc + pltpu.roll(oc, s, 1)
            gco_ref[k] = oc[:, 0:LANES]

    vm = pl.BlockSpec(memory_space=pltpu.VMEM)
    return pl.pallas_call(
        body, name=name, in_specs=[vm] * 6, out_specs=[vm] * 4,
        out_shape=[_sds((S5_NS, 2)), _sds((1, 32)), _sds((2, S5_NS, LANES)), _sds((2, S5_W, LANES))],
        compiler_params=pltpu.CompilerParams(vmem_limit_bytes=VMEM_LIMIT),
    )(lam_c, ldt_c, b_t, gb, ga_c, gc)


FL_BLK = EVEN_PAD // LANES - 1
Q_BLK, K_BLK, V_BLK = 4, 8, 12
NEG = -1e30


def _log_sigmoid(v):
    return jnp.minimum(v, 0.0) - jnp.log(1.0 + jnp.exp(-jnp.abs(v)))


def _fox_f_fwd(z, bf, name):
    L = z.shape[0]

    def body(fl_ref, b_ref, f_ref, fq_ref):
        row = lax.broadcasted_iota(jnp.int32, (L, LANES), 0)
        cs = _cumsum_rows(_log_sigmoid(fl_ref[...] + b_ref[...]), True, row)
        f_ref[...] = cs
        expand = (lax.broadcasted_iota(jnp.int32, (LANES, FOX_W), 0)
                  == (lax.broadcasted_iota(jnp.int32, (LANES, FOX_W), 1) >> 6)).astype(F32)
        fq_ref[...] = lax.dot_general(cs, expand, NN, precision=lax.Precision.HIGHEST, preferred_element_type=F32)

    return pl.pallas_call(
        body, name=name, grid=(1,),
        in_specs=[pl.BlockSpec((L, LANES), lambda i: (0, FL_BLK)), pl.BlockSpec((1, LANES), lambda i: (0, 0))],
        out_specs=[pl.BlockSpec((L, LANES), lambda i: (0, 0)), pl.BlockSpec((L, FOX_W), lambda i: (0, 0))],
        out_shape=[_sds((L, LANES)), _sds((L, FOX_W))],
        compiler_params=_cp("arbitrary"),
    )(z, bf)


def _fox_f_bwd(dFk, dfq, z, bf, name):
    L = z.shape[0]

    def body(dfk_ref, dfq_ref, fl_ref, b_ref, dfl_ref, db_ref):
        sel = (lax.broadcasted_iota(jnp.int32, (FOX_W, LANES), 0)
               == 64 * lax.broadcasted_iota(jnp.int32, (FOX_W, LANES), 1)).astype(F32)
        dfq_h = lax.dot_general(dfq_ref[...], sel, NN, precision=lax.Precision.HIGHEST, preferred_element_type=F32)
        row = lax.broadcasted_iota(jnp.int32, (L, LANES), 0)
        cs = _cumsum_rows(dfk_ref[...] + dfq_h, False, row)
        dfl = cs * _sigmoid(-(fl_ref[...] + b_ref[...]))
        dfl_ref[...] = dfl
        db_ref[...] = jnp.sum(dfl, axis=0, keepdims=True)

    return pl.pallas_call(
        body, name=name, grid=(1,),
        in_specs=[pl.BlockSpec((L, LANES), lambda i: (0, 0)), pl.BlockSpec((L, FOX_W), lambda i: (0, 0)),
                  pl.BlockSpec((L, LANES), lambda i: (0, FL_BLK)), pl.BlockSpec((1, LANES), lambda i: (0, 0))],
        out_specs=[pl.BlockSpec((L, LANES), lambda i: (0, 0)), pl.BlockSpec((1, LANES), lambda i: (0, 0))],
        out_shape=[_sds((L, LANES)), _sds((1, LANES))],
        compiler_params=_cp("arbitrary"),
    )(dFk, dfq, z, bf)


def _head_mask(hh):
    lane = lax.broadcasted_iota(jnp.int32, (1, LANES), 1)
    return (lane >> 6) == hh


FOX_T = 512


def _fox_head(x, hh):
    return jnp.where(_head_mask(hh), x, 0.0).astype(BF16)


def _fox_scores(qh, k, fq_ref, fr_ref, hh, causal):
    if fq_ref is None:
        s = _dot(qh, k, NT) - fr_ref[hh:hh + 1, :]
    else:
        s = _dot(qh, k, NT) + (fq_ref[:, 64 * hh:64 * hh + 1] - fr_ref[hh:hh + 1, :])
    return s if causal is None else jnp.where(causal, s, NEG)


def _causal(T):
    return lax.broadcasted_iota(jnp.int32, (T, T), 1) <= lax.broadcasted_iota(jnp.int32, (T, T), 0)


def _fox_fwd(z, fq, frow, name):
    L = z.shape[0]
    T = min(FOX_T, L)
    nq = L // T

    def body(qt_ref, kt_ref, q_ref, k_ref, v_ref, fq_ref, fr_ref, o_ref, lse_ref, m_ref, l_ref, acc_ref):
        t = pl.program_id(1)
        qi, ki = qt_ref[t], kt_ref[t]

        @pl.when(ki == 0)
        def _():
            m_ref[...] = jnp.full_like(m_ref, NEG)
            l_ref[...] = jnp.zeros_like(l_ref)
            acc_ref[...] = jnp.zeros_like(acc_ref)

        def step(diagonal):
            q = q_ref[...] * 0.125
            k = k_ref[...].astype(BF16)
            v = v_ref[...].astype(BF16)
            causal = _causal(T) if diagonal else None
            s = jnp.concatenate([_fox_scores(_fox_head(q, hh), k, fq_ref, fr_ref, hh, causal) for hh in range(2)],
                                axis=0)
            m_old = m_ref[...]
            m_new = jnp.maximum(m_old, jnp.max(s, axis=1, keepdims=True))
            alpha = jnp.exp(m_old - m_new)
            p = jnp.exp(s - m_new)
            l_ref[...] = alpha * l_ref[...] + jnp.sum(p, axis=1, keepdims=True)
            m_ref[...] = m_new
            acc_ref[...] = alpha * acc_ref[...] + _dot(p.astype(BF16), v)

        @pl.when(ki < qi)
        def _():
            step(False)

        @pl.when(ki == qi)
        def _():
            step(True)
            h0 = _head_mask(0)
            l = l_ref[...]
            o_h = acc_ref[...] / l
            lse_h = m_ref[...] + jnp.log(l)
            o_ref[...] = jnp.where(h0, o_h[:T], o_h[T:])
            lse_ref[...] = jnp.where(h0, lse_h[:T], lse_h[T:]) - fq_ref[...]

    pairs = [(qi, ki) for qi in range(nq) for ki in range(qi + 1)]
    qt = jnp.asarray([p[0] for p in pairs], jnp.int32)
    kt = jnp.asarray([p[1] for p in pairs], jnp.int32)

    def qspec(base):
        return pl.BlockSpec((T, LANES), lambda j, t, qt, kt: (qt[t], base + j))

    def kspec(base):
        return pl.BlockSpec((T, LANES), lambda j, t, qt, kt: (kt[t], base + j))

    return pl.pallas_call(
        body, name=name,
        grid_spec=pltpu.PrefetchScalarGridSpec(
            num_scalar_prefetch=2, grid=(4, len(pairs)),
            in_specs=[qspec(Q_BLK), kspec(K_BLK), kspec(V_BLK), qspec(0),
                      pl.BlockSpec((None, 2, T), lambda j, t, qt, kt: (j, 0, kt[t]))],
            out_specs=[qspec(0), qspec(0)],
            scratch_shapes=[pltpu.VMEM((2 * T, 1), F32), pltpu.VMEM((2 * T, 1), F32),
                            pltpu.VMEM((2 * T, LANES), F32)]),
        out_shape=[_sds((L, FOX_W)), _sds((L, FOX_W))],
        compiler_params=_cp("parallel", "arbitrary"),
    )(qt, kt, z, z, z, fq, frow)


def _fox_bwd(z, frow, o, lse, g_m, name):
    L = z.shape[0]
    T = min(FOX_T, L)
    nq = L // T

    pairs = [(qi, ki) for ki in range(nq) for qi in range(ki, nq)]
    qt = jnp.asarray([p[0] for p in pairs], jnp.int32)
    kt = jnp.asarray([p[1] for p in pairs], jnp.int32)

    def body(qt_ref, kt_ref, q_ref, k_ref, v_ref, fr_ref, o_ref, lse_ref, do_ref,
             dq_ref, dk_ref, dv_ref, dfq_ref, dfk_ref, dk_acc, dv_acc, df_acc):
        t = pl.program_id(1)
        qi, ki = qt_ref[t], kt_ref[t]

        @pl.when(t == 0)
        def _():
            dq_ref[...] = jnp.zeros_like(dq_ref)
            dfq_ref[...] = jnp.zeros_like(dfq_ref)

        @pl.when(qi == ki)
        def _():
            dk_acc[...] = jnp.zeros_like(dk_acc)
            dv_acc[...] = jnp.zeros_like(dv_acc)
            df_acc[...] = jnp.zeros_like(df_acc)

        def step(diagonal):
            q = q_ref[...] * 0.125
            qb = q.astype(BF16)
            k = k_ref[...].astype(BF16)
            v = v_ref[...].astype(BF16)
            do = do_ref[...]
            dob = do.astype(BF16)
            do_o = dob.astype(F32) * o_ref[...]
            causal = _causal(T) if diagonal else None
            dvs, dks, dqs, rss = [], [], [], []
            for hh in range(2):
                s = _fox_scores(_fox_head(q, hh), k, None, fr_ref, hh, causal)
                p = jnp.exp(s - lse_ref[:, 64 * hh:64 * hh + 1])
                dp = _dot(_fox_head(do, hh), v, NT)
                delta = jnp.sum(jnp.where(_head_mask(hh), do_o, 0.0), axis=1, keepdims=True)
                ds = p * (dp - delta)
                dsb = ds.astype(BF16)
                dvs.append(_dot(p.astype(BF16), dob, TN))
                dks.append(_dot(dsb, qb, TN))
                dqs.append(_dot(dsb, k))
                rss.append(jnp.sum(ds, axis=1, keepdims=True))
                df_acc[hh:hh + 1, :] -= jnp.sum(ds, axis=0, keepdims=True)
            h0 = _head_mask(0)
            dv_acc[...] += jnp.where(h0, dvs[0], dvs[1])
            dk_acc[...] += jnp.where(h0, dks[0], dks[1])
            rows = pl.ds(pl.multiple_of(qi * T, T), T)
            dq_ref[rows, :] += jnp.where(h0, dqs[0], dqs[1])
            dfq_ref[rows, :] += jnp.where(h0, rss[0], rss[1])

        @pl.when(qi > ki)
        def _():
            step(False)

        @pl.when(qi == ki)
        def _():
            step(True)

        @pl.when(qi == nq - 1)
        def _():
            dk_ref[...] = dk_acc[...]
            dv_ref[...] = dv_acc[...]
            dfk_ref[...] = df_acc[...]

        @pl.when(t == len(pairs) - 1)
        def _():
            dq_ref[...] = dq_ref[...] * 0.125

    def qside(base):
        return pl.BlockSpec((T, LANES), lambda j, t, qt, kt: (qt[t], base + j))

    def kside(base):
        return pl.BlockSpec((T, LANES), lambda j, t, qt, kt: (kt[t], base + j))

    pair = pl.BlockSpec((L, LANES), lambda j, t, qt, kt: (0, j))
    frow_spec = pl.BlockSpec((None, 2, T), lambda j, t, qt, kt: (j, 0, kt[t]))
    return pl.pallas_call(
        body, name=name,
        grid_spec=pltpu.PrefetchScalarGridSpec(
            num_scalar_prefetch=2, grid=(4, len(pairs)),
            in_specs=[qside(Q_BLK), kside(K_BLK), kside(V_BLK), frow_spec, qside(0), qside(0), qside(0)],
            out_specs=[pair, kside(0), kside(0), pair, frow_spec],
            scratch_shapes=[pltpu.VMEM((T, LANES), F32), pltpu.VMEM((T, LANES), F32), pltpu.VMEM((2, T), F32)]),
        out_shape=[_sds((L, FOX_W)), _sds((L, FOX_W)), _sds((L, FOX_W)), _sds((L, FOX_W)), _sds((4, 2, L))],
        compiler_params=_cp("parallel", "arbitrary"),
    )(qt, kt, z, z, z, frow, o, lse, g_m)


def _shift_rows(v, s, down, row):
    n = v.shape[0]
    if down:
        return jnp.where(row >= s, pltpu.roll(v, s, 0), 0.0)
    return jnp.where(row < n - s, pltpu.roll(v, n - s, 0), 0.0)


def _cumsum_rows(v, down, row):
    s = 1
    while s < v.shape[0]:
        v = v + _shift_rows(v, s, down, row)
        s *= 2
    return v


def _window_sum(v, g, down, row):
    out = jnp.zeros_like(v)
    s = v
    for k in range(4):
        s = s + _shift_rows(s, 1 << k, down, row)
        out = jnp.where(g == k, s, out)
    return out


def _pool_inv_cnt(g, row):
    w = jnp.left_shift(2, g).astype(F32)
    return 1.0 / jnp.minimum(row.astype(F32) + 1.0, w)


def _pool_fwd(z, pool_w, scale, name):
    L = z.shape[0]

    def body(x_ref, w_ref, s_ref, y_ref, p_ref):
        g = pl.program_id(0)
        row = lax.broadcasted_iota(jnp.int32, (L, LANES), 0)
        x = x_ref[...]
        pooled = (_window_sum(x, g, True, row) * _pool_inv_cnt(g, row) - x).astype(BF16)
        p_ref[...] = pooled
        y_ref[...] = (_dot(pooled, w_ref[...].astype(BF16)) * s_ref[...]).astype(BF16)

    col = pl.BlockSpec((L, LANES), lambda g: (0, g))
    return pl.pallas_call(
        body, name=name, grid=(4,),
        in_specs=[col, pl.BlockSpec((None, LANES, LANES), lambda g: (g, 0, 0)), pl.BlockSpec((1, LANES), lambda g: (0, g))],
        out_specs=[col, col],
        out_shape=[_sds((L, 512), BF16), _sds((L, 512), BF16)],
        compiler_params=_cp("parallel"),
    )(z, pool_w, scale)


def _pool_bwd(g_y, wout, pooled, pool_w, scale, name):
    L = g_y.shape[0]

    def body(g_ref, wo_ref, p_ref, w_ref, s_ref, gx_ref, gw_ref, gs_ref):
        g = pl.program_id(0)
        row = lax.broadcasted_iota(jnp.int32, (L, LANES), 0)
        gy = _dot(g_ref[...], wo_ref[...], NT)
        pooled = p_ref[...]
        wb = w_ref[...].astype(BF16)
        lin = _dot(pooled, wb)
        gs_ref[...] = jnp.sum(gy * lin, axis=0, keepdims=True)
        glin = (gy * s_ref[...]).astype(BF16)
        gw_ref[...] = _dot(pooled, glin, TN)
        gp = _dot(glin, wb, NT)
        gx_ref[...] = _window_sum(gp * _pool_inv_cnt(g, row), g, False, row) - gp

    col = pl.BlockSpec((L, LANES), lambda g: (0, g))
    wspec = pl.BlockSpec((None, LANES, LANES), lambda g: (g, 0, 0))
    vec = pl.BlockSpec((1, LANES), lambda g: (0, g))
    return pl.pallas_call(
        body, name=name, grid=(4,),
        in_specs=[pl.BlockSpec((L, D_MODEL), lambda g: (0, 0)), pl.BlockSpec((LANES, D_MODEL), lambda g: (g, 0)),
                  col, wspec, vec],
        out_specs=[col, wspec, vec],
        out_shape=[_sds((L, 512)), _sds((4, LANES, LANES)), _sds((1, 512))],
        compiler_params=_cp("parallel"),
    )(g_y, wout, pooled, pool_w, scale)


SGU_CHUNKS = 4


def _sgu_ln(v, gam, bet):
    gv = _gelu(v)
    mu = jnp.mean(gv, axis=-1, keepdims=True)
    xc = gv - mu
    rs = lax.rsqrt(jnp.mean(xc * xc, axis=-1, keepdims=True) + EPS)
    xh = xc * rs
    return xh, rs, xh * gam + bet


def _tril_ws(w_ref, g):
    r = lax.broadcasted_iota(jnp.int32, (LANES, LANES), 0)
    c = lax.broadcasted_iota(jnp.int32, (LANES, LANES), 1)
    return jnp.where(r >= c, w_ref[g], 0.0).astype(BF16)


def _sgu_fwd(z, ln_g, ln_b, w_s, b_st, name):
    L = z.shape[0]
    rb = min(SGU_CHUNKS * LANES, L)

    def body(u_ref, v_ref, g_ref, b_ref, w_ref, bs_ref, y_ref):
        _, _, vln = _sgu_ln(v_ref[...], g_ref[...], b_ref[...])
        gu = _gelu(u_ref[...])
        vb = vln.astype(BF16)
        for g in range(4):
            ws = _tril_ws(w_ref, g)
            for n in range(rb // LANES):
                rows = slice(n * LANES, (n + 1) * LANES)
                cols = slice(g * LANES, (g + 1) * LANES)
                mixed = _dot(ws, vb[rows, cols]) + bs_ref[:, g:g + 1]
                y_ref[rows, cols] = (gu[rows, cols] * mixed).astype(BF16)

    vm = lambda shape: pl.BlockSpec(shape, lambda i: tuple(0 for _ in shape))
    return pl.pallas_call(
        body, name=name, grid=(L // rb,),
        in_specs=[pl.BlockSpec((rb, 512), lambda i: (i, 1)), pl.BlockSpec((rb, 512), lambda i: (i, 2)),
                  vm((1, 512)), vm((1, 512)), vm((4, LANES, LANES)), vm((LANES, 4))],
        out_specs=pl.BlockSpec((rb, 512), lambda i: (i, 0)),
        out_shape=_sds((L, 512), BF16),
        compiler_params=_cp("parallel"),
    )(z, z, ln_g, ln_b, w_s, b_st)


def _sgu_bwd(g_y, wout, z, ln_g, ln_b, w_s, b_st, name):
    L = z.shape[0]
    rb = min(SGU_CHUNKS * LANES, L)

    def body(gyo_ref, wo_ref, u_ref, v_ref, g_ref, b_ref, w_ref, bs_ref, gu_ref, gv_ref, gw_ref, gbs_ref, gg_ref,
             gb_ref):
        i = pl.program_id(0)

        @pl.when(i == 0)
        def _():
            gw_ref[...] = jnp.zeros_like(gw_ref)
            gbs_ref[...] = jnp.zeros_like(gbs_ref)
            gg_ref[...] = jnp.zeros_like(gg_ref)
            gb_ref[...] = jnp.zeros_like(gb_ref)

        v = v_ref[...]
        u = u_ref[...]
        gy = _dot(gyo_ref[...], wo_ref[...], NT)
        xh, rs, vln = _sgu_ln(v, g_ref[...], b_ref[...])
        gel_u = _gelu(u)
        gmix = gy * gel_u
        vb = vln.astype(BF16)
        gmb = gmix.astype(BF16)
        r = lax.broadcasted_iota(jnp.int32, (LANES, LANES), 0)
        c = lax.broadcasted_iota(jnp.int32, (LANES, LANES), 1)
        gvln_cols = []
        for g in range(4):
            ws = _tril_ws(w_ref, g)
            cols = slice(g * LANES, (g + 1) * LANES)
            gw = jnp.zeros((LANES, LANES), F32)
            gbs = jnp.zeros((LANES, 1), F32)
            parts = []
            for n in range(rb // LANES):
                rows = slice(n * LANES, (n + 1) * LANES)
                mixed = _dot(ws, vb[rows, cols]) + bs_ref[:, g:g + 1]
                gu_ref[rows, cols] = gy[rows, cols] * mixed * _gelu_grad(u[rows, cols])
                parts.append(_dot(ws, gmb[rows, cols], TN))
                gw = gw + _dot(gmb[rows, cols], vb[rows, cols], NT)
                gbs = gbs + jnp.sum(gmix[rows, cols], axis=1, keepdims=True)
            gvln_cols.append(jnp.concatenate(parts, axis=0))
            gw_ref[g] += jnp.where(r >= c, gw, 0.0)
            gbs_ref[:, g:g + 1] += gbs
        gvln = jnp.concatenate(gvln_cols, axis=1)
        gg_ref[...] += jnp.sum(gvln * xh, axis=0, keepdims=True)
        gb_ref[...] += jnp.sum(gvln, axis=0, keepdims=True)
        gxh = gvln * g_ref[...]
        ggv = rs * (gxh - jnp.mean(gxh, axis=-1, keepdims=True) - xh * jnp.mean(gxh * xh, axis=-1, keepdims=True))
        gv_ref[...] = ggv * _gelu_grad(v)

    vm = lambda shape: pl.BlockSpec(shape, lambda i: tuple(0 for _ in shape))
    blk = pl.BlockSpec((rb, 512), lambda i: (i, 0))
    return pl.pallas_call(
        body, name=name, grid=(L // rb,),
        in_specs=[pl.BlockSpec((rb, D_MODEL), lambda i: (i, 0)), pl.BlockSpec((512, D_MODEL), lambda i: (1, 0)),
                  pl.BlockSpec((rb, 512), lambda i: (i, 1)), pl.BlockSpec((rb, 512), lambda i: (i, 2)),
                  vm((1, 512)), vm((1, 512)), vm((4, LANES, LANES)), vm((LANES, 4))],
        out_specs=[blk, blk, vm((4, LANES, LANES)), vm((LANES, 4)), vm((1, 512)), vm((1, 512))],
        out_shape=[_sds((L, 512)), _sds((L, 512)), _sds((4, LANES, LANES)), _sds((LANES, 4)),
                   _sds((1, 512)), _sds((1, 512))],
        compiler_params=_cp("arbitrary"),
    )(g_y, wout, z, z, ln_g, ln_b, w_s, b_st)


def _adamw_math(w, g, m, v):
    nm = ADAM_B1 * m + (1.0 - ADAM_B1) * g
    nv = ADAM_B2 * v + (1.0 - ADAM_B2) * (g * g)
    m_hat = nm / (1.0 - ADAM_B1 ** ADAM_STEP)
    v_hat = nv / (1.0 - ADAM_B2 ** ADAM_STEP)
    delta = -ADAM_LR * (m_hat / (jnp.sqrt(v_hat) + ADAM_EPS) + ADAM_WD * w)
    return delta, nm, nv


def _sum_adamw(parts, w, m, v, name):
    n_layers, R, C = w.shape
    assert len(parts) == n_layers
    rb = next((b for b in (256, 128) if R % b == 0), R)
    nb = R // rb

    def body(*refs):
        p_refs = refs[:n_layers]
        w_ref, m_ref, v_ref, g_ref, d_ref, nm_ref, nv_ref = refs[n_layers:]
        for k, p_ref in enumerate(p_refs):
            @pl.when(pl.program_id(0) == k)
            def _(p_ref=p_ref):
                g = p_ref[0].astype(F32)
                for s in range(1, N_DEV):
                    g = g + p_ref[s].astype(F32)
                d, nm, nv = _adamw_math(w_ref[...], g, m_ref[...], v_ref[...])
                g_ref[...] = g
                d_ref[...] = d
                nm_ref[...] = nm
                nv_ref[...] = nv

    def part_spec(k):
        return pl.BlockSpec((N_DEV, rb, C), lambda l, i: (0, jnp.where(l == k, i, jnp.where(l < k, 0, nb - 1)), 0))

    blk = pl.BlockSpec((None, rb, C), lambda l, i: (l, i, 0))
    return pl.pallas_call(
        body, name=name, grid=(n_layers, nb),
        in_specs=[part_spec(k) for k in range(n_layers)] + [blk, blk, blk],
        out_specs=[blk] * 4, out_shape=[_sds((n_layers, R, C))] * 4,
        compiler_params=_cp("arbitrary", "arbitrary"),
    )(*parts, w, m, v)


def _sum_adamw_rows(parts, w, m, v, name):
    R, _, C = w.shape

    def body(p_ref, w_ref, m_ref, v_ref, g_ref, d_ref, nm_ref, nv_ref):
        g = p_ref[0].astype(F32)
        for s in range(1, N_DEV):
            g = g + p_ref[s].astype(F32)
        d, nm, nv = _adamw_math(w_ref[:, 0, :], g, m_ref[:, 0, :], v_ref[:, 0, :])
        g_ref[:, 0, :] = g
        d_ref[:, 0, :] = d
        nm_ref[:, 0, :] = nm
        nv_ref[:, 0, :] = nv

    vm = pl.BlockSpec(memory_space=pltpu.VMEM)
    return pl.pallas_call(body, name=name, in_specs=[vm] * 4, out_specs=[vm] * 4, out_shape=[_sds((R, 1, C))] * 4,
                          compiler_params=pltpu.CompilerParams(vmem_limit_bytes=VMEM_LIMIT))(parts, w, m, v)


def _sum_pieces(parts, name):
    _, R, C = parts.shape

    def body(p_ref, g_ref):
        g = p_ref[0].astype(F32)
        for s in range(1, N_DEV):
            g = g + p_ref[s].astype(F32)
        g_ref[...] = g

    vm = pl.BlockSpec(memory_space=pltpu.VMEM)
    return pl.pallas_call(body, name=name, in_specs=[vm], out_specs=vm, out_shape=_sds((R, C)),
                          compiler_params=pltpu.CompilerParams(vmem_limit_bytes=VMEM_LIMIT))(parts)


def _adamw_many(ws, gs, ms, vs, name):
    n = len(ws)
    vm = pl.BlockSpec(memory_space=pltpu.VMEM)

    def body(*refs):
        w_refs, g_refs, m_refs, v_refs = refs[:n], refs[n:2 * n], refs[2 * n:3 * n], refs[3 * n:4 * n]
        d_refs, nm_refs, nv_refs = refs[4 * n:5 * n], refs[5 * n:6 * n], refs[6 * n:7 * n]
        for i in range(n):
            d, nm, nv = _adamw_math(w_refs[i][...], g_refs[i][...], m_refs[i][...], v_refs[i][...])
            d_refs[i][...] = d
            nm_refs[i][...] = nm
            nv_refs[i][...] = nv

    shapes = [_sds(w.shape) for w in ws]
    outs = pl.pallas_call(
        body, name=name, in_specs=[vm] * (4 * n), out_specs=[vm] * (3 * n), out_shape=shapes * 3,
        compiler_params=pltpu.CompilerParams(vmem_limit_bytes=VMEM_LIMIT),
    )(*ws, *gs, *ms, *vs)
    return list(outs[:n]), list(outs[n:2 * n]), list(outs[2 * n:])


def _mesh_pos():
    return lax.axis_index("x"), lax.axis_index("y"), lax.axis_index("c")


def _dev_index(p):
    return 4 * p[0] + 2 * p[1] + p[2]


HBM = pl.BlockSpec(memory_space=pltpu.HBM)
SEM = pl.BlockSpec(memory_space=pltpu.SEMAPHORE)
EFFECT = pltpu.SideEffectType.DATAFLOW_SIDE_EFFECTING


def _peer_list():
    x, y, c = _mesh_pos()
    peers = [(x ^ dx, y ^ dy, c ^ dc) for dx in range(2) for dy in range(2) for dc in range(2)][1:]
    return (x, y, c), peers


def _split_copy(src_ref, land_ref, send_sems, recv_sems, i, k, peer, slot, exchange):
    return pltpu.make_async_remote_copy(
        src_ref=src_ref.at[_dev_index(peer)] if exchange else src_ref, dst_ref=land_ref.at[slot],
        send_sem=send_sems.at[7 * i + k], recv_sem=recv_sems.at[7 * i + k], device_id=peer, device_id_type=MESH)


def _own_copy(src_ref, land_ref, own_sems, i, slot, exchange):
    return pltpu.make_async_copy(src_ref.at[slot] if exchange else src_ref, land_ref.at[slot], own_sems.at[i])


def _comm_start(groups, name, exchange, dep=None):
    sizes = [len(g) for g in groups]
    n = sum(sizes)
    srcs = [a for g in groups for a in g]
    per_group = [exchange] * len(groups) if isinstance(exchange, bool) else list(exchange)
    exchanged = [flag for flag, sz in zip(per_group, sizes) for _ in range(sz)]
    lands = [lax.empty(a.shape if ex else (N_DEV,) + a.shape, a.dtype) for a, ex in zip(srcs, exchanged)]

    n_dep = 0 if dep is None else 1

    def body(*refs):
        src_refs, land_refs = refs[:n], refs[n:2 * n]
        sem_refs = refs[2 * n + n_dep:2 * n + n_dep + 3 * len(sizes)]
        token_ref = refs[-1]
        me, peers = _peer_list()
        mi = _dev_index(me)
        i = 0
        for gi, sz in enumerate(sizes):
            for j in range(sz):
                for k, peer in enumerate(peers):
                    _split_copy(src_refs[i], land_refs[i], sem_refs[3 * gi], sem_refs[3 * gi + 1], j, k, peer, mi,
                                exchanged[i]).start()
                _own_copy(src_refs[i], land_refs[i], sem_refs[3 * gi + 2], j, mi, exchanged[i]).start()
                i += 1
        token_ref[...] = jnp.zeros_like(token_ref)

    sem_shapes = []
    for sz in sizes:
        sem_shapes += [pltpu.SemaphoreType.DMA((7 * sz,)), pltpu.SemaphoreType.DMA((7 * sz,)),
                       pltpu.SemaphoreType.DMA((sz,))]
    thru = [pltpu.HBM(a.shape, a.dtype) for a in srcs + lands]
    n_sem = len(sem_shapes)
    outs = pl.pallas_call(
        body, name=name,
        out_shape=tuple(sem_shapes + thru + [_sds((8, LANES))]),
        in_specs=[HBM] * (2 * n) + [ANY] * n_dep,
        out_specs=tuple([SEM] * n_sem + [HBM] * (2 * n) + [pl.BlockSpec(memory_space=pltpu.VMEM)]),
        input_output_aliases={i: n_sem + i for i in range(2 * n)},
        compiler_params=pltpu.CompilerParams(has_side_effects=EFFECT),
    )(*[pltpu.with_memory_space_constraint(a, pltpu.HBM) for a in srcs + lands], *([] if dep is None else [dep]))
    sems, thru_src, thru_land, token = outs[:n_sem], outs[n_sem:n_sem + n], outs[n_sem + n:n_sem + 2 * n], outs[-1]
    result, off = [], 0
    for gi, sz in enumerate(sizes):
        result.append((*sems[3 * gi:3 * gi + 3], list(thru_src[off:off + sz]), list(thru_land[off:off + sz])))
        off += sz
    return result, token


def _comm_wait(group, after, name, exchange):
    send_sems, recv_sems, own_sems, srcs, lands = group
    n = len(srcs)
    after = list(after) if isinstance(after, (list, tuple)) else [after]

    def body(*refs):
        src_refs, land_refs = refs[:n], refs[n:2 * n]
        ssem, rsem, osem = refs[2 * n:2 * n + 3]
        me, peers = _peer_list()
        for i in range(n):
            for k, peer in enumerate(peers):
                cp = _split_copy(src_refs[i], land_refs[i], ssem, rsem, i, k, peer, _dev_index(peer), exchange)
                cp.wait_send()
                cp.wait_recv()
            _own_copy(src_refs[i], land_refs[i], osem, i, _dev_index(me), exchange).wait()

    outs = pl.pallas_call(
        body, name=name,
        out_shape=tuple(pltpu.HBM(a.shape, a.dtype) for a in srcs + lands),
        in_specs=[HBM] * (2 * n) + [SEM, SEM, SEM] + [ANY] * len(after),
        out_specs=tuple([HBM] * (2 * n)),
        input_output_aliases={i: i for i in range(2 * n)},
        compiler_params=pltpu.CompilerParams(has_side_effects=EFFECT),
    )(*srcs, *lands, send_sems, recv_sems, own_sems, *after)
    return list(outs[n:])


def _tie(a, token):
    return a + token[0, 0].astype(a.dtype)


def _pack(arrs, rows):
    flat = jnp.concatenate([a.reshape(-1).astype(F32) for a in arrs])
    return jnp.pad(flat, (0, rows * LANES - flat.shape[0])).reshape(rows, LANES)


def _unpack(packed, shapes):
    flat = packed.reshape(-1)
    out, off = [], 0
    for s in shapes:
        n = math.prod(s)
        out.append(flat[off:off + n].reshape(s))
        off += n
    return out


def _packed_rows(shapes):
    n = sum(math.prod(s) for s in shapes)
    unit = N_DEV * 8 * LANES
    return -(-n // unit) * unit // LANES


def kernel(x, mix_pre_g, mix_post_g, mlp_pre_g, mlp_post_g, w_in_even, s5_lam_re, s5_lam_im, s5_log_dt, s5_b_re, s5_b_im, s5_c_re, s5_c_im, s5_d, s5_w_glu, fox_b_f, w_out_even, w_in_odd, pool_w, pool_scale, sgu_ln_g, sgu_ln_b, sgu_w_s, sgu_b_s, w_out_odd, mlp_w1, mlp_w2, loss_target, m_mix_pre_g, m_mix_post_g, m_mlp_pre_g, m_mlp_post_g, m_w_in_even, m_s5_lam_re, m_s5_lam_im, m_s5_log_dt, m_s5_b_re, m_s5_b_im, m_s5_c_re, m_s5_c_im, m_s5_d, m_s5_w_glu, m_fox_b_f, m_w_out_even, m_w_in_odd, m_pool_w, m_pool_scale, m_sgu_ln_g, m_sgu_ln_b, m_sgu_w_s, m_sgu_b_s, m_w_out_odd, m_mlp_w1, m_mlp_w2, v_mix_pre_g, v_mix_post_g, v_mlp_pre_g, v_mlp_post_g, v_w_in_even, v_s5_lam_re, v_s5_lam_im, v_s5_log_dt, v_s5_b_re, v_s5_b_im, v_s5_c_re, v_s5_c_im, v_s5_d, v_s5_w_glu, v_fox_b_f, v_w_out_even, v_w_in_odd, v_pool_w, v_pool_scale, v_sgu_ln_g, v_sgu_ln_b, v_sgu_w_s, v_sgu_b_s, v_w_out_odd, v_mlp_w1, v_mlp_w2):
    weights = dict(mix_pre_g=mix_pre_g, mix_post_g=mix_post_g, mlp_pre_g=mlp_pre_g, mlp_post_g=mlp_post_g, w_in_even=w_in_even, s5_lam_re=s5_lam_re, s5_lam_im=s5_lam_im, s5_log_dt=s5_log_dt, s5_b_re=s5_b_re, s5_b_im=s5_b_im, s5_c_re=s5_c_re, s5_c_im=s5_c_im, s5_d=s5_d, s5_w_glu=s5_w_glu, fox_b_f=fox_b_f, w_out_even=w_out_even, w_in_odd=w_in_odd, pool_w=pool_w, pool_scale=pool_scale, sgu_ln_g=sgu_ln_g, sgu_ln_b=sgu_ln_b, sgu_w_s=sgu_w_s, sgu_b_s=sgu_b_s, w_out_odd=w_out_odd, mlp_w1=mlp_w1, mlp_w2=mlp_w2)
    mom_m = dict(mix_pre_g=m_mix_pre_g, mix_post_g=m_mix_post_g, mlp_pre_g=m_mlp_pre_g, mlp_post_g=m_mlp_post_g, w_in_even=m_w_in_even, s5_lam_re=m_s5_lam_re, s5_lam_im=m_s5_lam_im, s5_log_dt=m_s5_log_dt, s5_b_re=m_s5_b_re, s5_b_im=m_s5_b_im, s5_c_re=m_s5_c_re, s5_c_im=m_s5_c_im, s5_d=m_s5_d, s5_w_glu=m_s5_w_glu, fox_b_f=m_fox_b_f, w_out_even=m_w_out_even, w_in_odd=m_w_in_odd, pool_w=m_pool_w, pool_scale=m_pool_scale, sgu_ln_g=m_sgu_ln_g, sgu_ln_b=m_sgu_ln_b, sgu_w_s=m_sgu_w_s, sgu_b_s=m_sgu_b_s, w_out_odd=m_w_out_odd, mlp_w1=m_mlp_w1, mlp_w2=m_mlp_w2)
    mom_v = dict(mix_pre_g=v_mix_pre_g, mix_post_g=v_mix_post_g, mlp_pre_g=v_mlp_pre_g, mlp_post_g=v_mlp_post_g, w_in_even=v_w_in_even, s5_lam_re=v_s5_lam_re, s5_lam_im=v_s5_lam_im, s5_log_dt=v_s5_log_dt, s5_b_re=v_s5_b_re, s5_b_im=v_s5_b_im, s5_c_re=v_s5_c_re, s5_c_im=v_s5_c_im, s5_d=v_s5_d, s5_w_glu=v_s5_w_glu, fox_b_f=v_fox_b_f, w_out_even=v_w_out_even, w_in_odd=v_w_in_odd, pool_w=v_pool_w, pool_scale=v_pool_scale, sgu_ln_g=v_sgu_ln_g, sgu_ln_b=v_sgu_ln_b, sgu_w_s=v_sgu_w_s, sgu_b_s=v_sgu_b_s, w_out_odd=v_w_out_odd, mlp_w1=v_mlp_w1, mlp_w2=v_mlp_w2)
    names = list(weights)
    L = x.shape[1]
    x0 = x[0]
    target = loss_target[0]
    my_index = 4 * lax.axis_index("x") + 2 * lax.axis_index("y") + lax.axis_index("c")

    small_vec = jnp.zeros((8, LANES), F32)
    small_vec = small_vec.at[0, :64].set(pool_scale[0]).at[1, :64].set(sgu_ln_g[0]).at[2, :64].set(sgu_ln_b[0])
    ag_groups, ag_token = _comm_start(
        [[jnp.transpose(w_in_even[0]).astype(BF16), small_vec],
         [s5_w_glu[0].astype(BF16), w_out_even[0].astype(BF16)],
         [mlp_w1[0].astype(BF16), mlp_w2[0].astype(BF16)],
         [jnp.transpose(w_in_odd[0]).astype(BF16), w_out_odd[0].astype(BF16), mlp_w1[1].astype(BF16), mlp_w2[1].astype(BF16)]],
        "ag_start", exchange=False)

    lam_r = jnp.concatenate([s5_lam_re.reshape(1, S5_NS), s5_lam_im.reshape(1, S5_NS)], axis=0)
    ldt_r = jnp.repeat(s5_log_dt.reshape(32), 64).reshape(1, S5_NS)
    lam_c = jnp.transpose(lam_r)
    ldt_c = jnp.transpose(ldt_r)
    b_t = jnp.stack([jnp.tile(s5_b_re.reshape(S5_NS, 16), (1, 8)), jnp.tile(s5_b_im.reshape(S5_NS, 16), (1, 8))])
    c_t = jnp.stack([jnp.tile(s5_c_re.reshape(S5_W, 64), (1, 8)), jnp.tile(s5_c_im.reshape(S5_W, 64), (1, 8))])
    bf_pad = jnp.pad(fox_b_f, ((0, 0), (0, LANES - 8)))
    b_st = jnp.transpose(sgu_b_s[0])

    h0, rx0 = _rms_fwd(x0, _tie(mix_pre_g[0:1], ag_token), "rms0")
    tabs, bset, cset = _s5_prep(lam_r, ldt_r, lam_c, ldt_c, b_t, c_t, "s5_prep")
    ag0 = _comm_wait(ag_groups[0], tabs, "ag_wait0", exchange=False)
    winT_e = jnp.pad(ag0[0].reshape(EVEN_IN, D_MODEL), ((0, EVEN_PAD - EVEN_IN), (0, 0)))
    pool_scale_f = ag0[1][:, 0, :64].reshape(1, 512)
    ln_g_f = ag0[1][:, 1, :64].reshape(1, 512)
    ln_b_f = ag0[1][:, 2, :64].reshape(1, 512)
    z0 = _mm(h0, winT_e, name="win_even", tb=True, bm=512, bn=EVEN_PAD)
    xs, ylin = _s5_scan_fwd(z0, bset, cset, s5_d, tabs, "s5_scan")
    ag1 = _comm_wait(ag_groups[1], ylin, "ag_wait1", exchange=False)
    wglu = ag1[0].reshape(S5_W, S5_W)
    wout_e = ag1[1].reshape(D_MODEL, D_MODEL)
    ya = _s5_glu_fwd(ylin, wglu, "s5_glu")
    fcum, fq = _fox_f_fwd(z0, bf_pad, "fox_f")
    frow = jnp.transpose(fcum[:, :8]).reshape(4, 2, L)
    o_att, lse = _fox_fwd(z0, fq, frow, "fox_fwd")
    mix0 = [ya, o_att]
    x1, ry0, h1, rx1, y0 = _mm(mix0, wout_e, name="wout_even", epi=_epi_post_pre, extra=(x0,),
                               vecs=(mix_post_g[0:1], mlp_pre_g[0:1]), out_dtypes=POST_PRE_DTYPES,
                               out_kinds=POST_PRE_KINDS, bm=FUSED_ROWS)
    ag2 = _comm_wait(ag_groups[2], rx1, "ag_wait2", exchange=False)
    w1 = [ag2[0], None]
    w2 = [ag2[1].reshape(4 * D_MODEL, D_MODEL), None]
    p0, a0 = _mm(h1, w1[0], name="mlp0_w1", b3=True, out_dtypes=(BF16, BF16), epi=_epi_relu2, bm=512, bn=4 * D_MODEL)
    x2, ro0, h2, rx2, o0 = _mm(a0, w2[0], name="mlp0_w2", epi=_epi_post_pre, extra=(x1,),
                               vecs=(mlp_post_g[0:1], mix_pre_g[1:2]), out_dtypes=POST_PRE_DTYPES,
                               out_kinds=POST_PRE_KINDS, bm=FUSED_ROWS, bk=4 * D_MODEL)
    ag3 = _comm_wait(ag_groups[3], rx2, "ag_wait3", exchange=False)
    winT_o = ag3[0].reshape(ODD_IN, D_MODEL)
    wout_o = ag3[1].reshape(D_MODEL, D_MODEL)
    w1[1] = ag3[2]
    w2[1] = ag3[3].reshape(4 * D_MODEL, D_MODEL)
    z1 = _mm(h2, winT_o, name="win_odd", tb=True, bn=ODD_IN)
    yc, pooled = _pool_fwd(z1, pool_w[0], pool_scale_f, "pool_fwd")
    yd = _sgu_fwd(z1, ln_g_f, ln_b_f, sgu_w_s[0], b_st, "sgu_fwd")
    mix1 = [yc, yd]
    x3, ry1, h3, rx3, y1 = _mm(mix1, wout_o, name="wout_odd", epi=_epi_post_pre, extra=(x2,),
                               vecs=(mix_post_g[1:2], mlp_pre_g[1:2]), out_dtypes=POST_PRE_DTYPES,
                               out_kinds=POST_PRE_KINDS, bm=FUSED_ROWS)
    p1, a1 = _mm(h3, w1[1], name="mlp1_w1", b3=True, out_dtypes=(BF16, BF16), epi=_epi_relu2, bm=512, bn=4 * D_MODEL)
    gx4, g_o1, gg_mlp_post1, sq_lanes = _mm(
        a1, w2[1], name="mlp1_w2", epi=_epi_post_loss, extra=(x3, target), vecs=(mlp_post_g[1:2],),
        out_dtypes=(F32, BF16, F32, F32), out_kinds=("full", "full", "vsum", "vsum"), bm=FUSED_ROWS, bk=4 * D_MODEL)
    sq = sq_lanes[:, 0:1]

    g_p1 = _mm(g_o1, w2[1], name="b_mlp1_a", tb=True, out_dtypes=(BF16,), epi=_epi_relu2_bwd, extra=(p1,),
               bm=512, bn=4 * D_MODEL)
    gw2_1 = _mm(a1, g_o1, name="b_mlp1_w2", ta=True, bm=512, bk=L)
    gw1_1 = _mm(h3, g_p1, name="b_mlp1_w1", ta=True, out3=True, bn=512, bk=L)
    (ex1,), tok1 = _comm_start([[gw1_1, gw2_1.reshape(N_DEV, 512, D_MODEL)]], "ex_start1", exchange=True)
    g_x3, gg_mlp_pre1, g_y1, gg_mix_post1 = _mm(
        g_p1, w1[1], name="b_mlp1_h", tb=True, b3=True, epi=_epi_pre_post_bwd, extra=(x3, gx4, y1), cols=(rx3, ry1),
        vecs=(_tie(mlp_pre_g[1:2], tok1), mix_post_g[1:2]), out_dtypes=PRE_POST_BWD_DTYPES,
        out_kinds=PRE_POST_BWD_KINDS, bm=FUSED_ROWS, bk=4 * D_MODEL)
    gwout_o = _mm(mix1, g_y1, name="b_wout_odd_w", ta=True)
    g_xc, g_pool_w, g_pool_scale = _pool_bwd(g_y1, wout_o, pooled, pool_w[0], pool_scale_f, "pool_bwd")
    g_u1, g_v1, g_ws, g_bst, g_ln_g, g_ln_b = _sgu_bwd(g_y1, wout_o, z1, ln_g_f, ln_b_f, sgu_w_s[0], b_st,
                                                       "sgu_bwd")
    g_z1 = [g_xc, g_u1, g_v1]
    gwinT_o = _mm(g_z1, h2, name="b_win_odd_w", ta=True)
    (ex2,), tok2 = _comm_start([[gwout_o.reshape(N_DEV, 128, D_MODEL), gwinT_o.reshape(N_DEV, ODD_IN // N_DEV, D_MODEL)]], "ex_start2", exchange=True)
    g_x2, gg_mix_pre1, g_o0, gg_mlp_post0 = _mm(
        g_z1, winT_o, name="b_win_odd_h", epi=_epi_pre_post_bwd, extra=(x2, g_x3, o0), cols=(rx2, ro0),
        vecs=(_tie(mix_pre_g[1:2], tok2), mlp_post_g[0:1]), out_dtypes=PRE_POST_BWD_DTYPES,
        out_kinds=PRE_POST_BWD_KINDS, bm=FUSED_ROWS)
    g_p0 = _mm(g_o0, w2[0], name="b_mlp0_a", tb=True, out_dtypes=(BF16,), epi=_epi_relu2_bwd, extra=(p0,),
               bm=512, bn=4 * D_MODEL)
    gw2_0 = _mm(a0, g_o0, name="b_mlp0_w2", ta=True, bm=512, bk=L)
    gw1_0 = _mm(h1, g_p0, name="b_mlp0_w1", ta=True, out3=True, bn=512, bk=L)
    (ex3,), tok3 = _comm_start([[gw1_0, gw2_0.reshape(N_DEV, 512, D_MODEL)]], "ex_start3", exchange=True)
    g_x1, gg_mlp_pre0, g_y0, gg_mix_post0 = _mm(
        g_p0, w1[0], name="b_mlp0_h", tb=True, b3=True, epi=_epi_pre_post_bwd, extra=(x1, g_x2, y0), cols=(rx1, ry0),
        vecs=(_tie(mlp_pre_g[0:1], tok3), mix_post_g[0:1]), out_dtypes=PRE_POST_BWD_DTYPES,
        out_kinds=PRE_POST_BWD_KINDS, bm=FUSED_ROWS, bk=4 * D_MODEL)
    g_o_att = _mm(g_y0, wout_e[FOX_W:], name="b_wout_even_m", tb=True)
    gwout_e = _mm(mix0, g_y0, name="b_wout_even_w", ta=True)
    gyl, gud, g_wglu, g_d = _s5_glu_bwd(g_y0, wout_e, ylin, z0, s5_d, wglu, "s5_glu_bwd")
    (ex4,), tok4 = _comm_start([[gwout_e.reshape(N_DEV, 128, D_MODEL), g_wglu.reshape(N_DEV, 64, S5_W)]], "ex_start4", exchange=True)
    g_u0, ga, gb_raw, gc_raw = _s5_scan_bwd(gyl, _tie(cset, tok4), xs, z0, bset, gud, tabs, "s5_scan_bwd")
    g_lam, g_ldt, g_b, g_c = _s5_param_bwd(lam_c, ldt_c, b_t, gb_raw, jnp.transpose(ga), gc_raw, "s5_param_bwd")
    dq, dk, dv, dfq, dfrow = _fox_bwd(z0, frow, o_att, lse, g_o_att, "fox_bwd")
    dFk = jnp.pad(jnp.transpose(dfrow.reshape(8, L)), ((0, 0), (0, LANES - 8)))
    dfl, db_f = _fox_f_bwd(dFk, dfq, z0, bf_pad, "fox_f_bwd")
    g_z0 = [g_u0, dq, dk, dv, dfl]
    grad_x, gg_mix_pre0 = _mm(g_z0, winT_e, name="b_win_even_h", epi=_epi_pre_bwd, extra=(x0, g_x1), cols=(rx0,),
                              vecs=(mix_pre_g[0:1],), out_dtypes=(F32, F32), out_kinds=("full", "vsum"),
                              bm=FUSED_ROWS)

    small_grads = dict(
        mix_pre_g=jnp.concatenate([gg_mix_pre0, gg_mix_pre1]), mix_post_g=jnp.concatenate([gg_mix_post0, gg_mix_post1]),
        mlp_pre_g=jnp.concatenate([gg_mlp_pre0, gg_mlp_pre1]), mlp_post_g=jnp.concatenate([gg_mlp_post0, gg_mlp_post1]),
        s5_lam_re=g_lam[:, 0], s5_lam_im=g_lam[:, 1],
        s5_b_re=g_b[0, :, :16], s5_b_im=g_b[1, :, :16], s5_c_re=g_c[0, :, :64], s5_c_im=g_c[1, :, :64],
        pool_w=g_pool_w, sgu_w_s=g_ws, s5_d=g_d, sgu_b_s=jnp.transpose(g_bst),
        pool_scale=g_pool_scale, sgu_ln_g=g_ln_g, sgu_ln_b=g_ln_b, s5_log_dt=g_ldt, fox_b_f=db_f[:, :8])
    small_names = list(small_grads)
    full_shapes = [(512,) if nm in ("pool_scale", "sgu_ln_g", "sgu_ln_b") else weights[nm].shape for nm in small_names]
    full_shapes.append((1, 1))
    rows = _packed_rows(full_shapes)
    packed = _pack([small_grads[nm] for nm in small_names] + [sq], rows).reshape(N_DEV, rows // N_DEV, LANES)
    (exs,), tok_s = _comm_start([[packed]], "exs_start", exchange=True)
    gwinT_e = _mm(g_z0, h0, name="b_win_even_w", ta=True, out_dtypes=(BF16,), dep=tok_s)
    (recv_small,) = _comm_wait(exs, gwinT_e, "exs_wait", exchange=True)
    piece = _sum_pieces(recv_small, "sum_small")
    gwinT_e_pieces = gwinT_e[:EVEN_IN].reshape(N_DEV, EVEN_IN // N_DEV, D_MODEL)
    (ags, ex5), tok5 = _comm_start([[piece], [gwinT_e_pieces]], "ags_ex_start5", exchange=(False, True))
    r_w1_1, r_w2_1 = _comm_wait(ex1, tok5, "ex_wait1", exchange=True)
    r_wout_o, r_win_o = _comm_wait(ex2, tok5, "ex_wait2", exchange=True)
    r_w1_0, r_w2_0 = _comm_wait(ex3, tok5, "ex_wait3", exchange=True)
    r_wout_e, r_wglu = _comm_wait(ex4, tok5, "ex_wait4", exchange=True)

    res = {}
    big_parts = dict(mlp_w1=[r_w1_0, r_w1_1], mlp_w2=[r_w2_0, r_w2_1], s5_w_glu=[r_wglu], w_out_even=[r_wout_e],
                     w_out_odd=[r_wout_o])
    for nm, parts in big_parts.items():
        res[nm] = tuple(_sum_adamw(parts, weights[nm], mom_m[nm], mom_v[nm], "adamw_" + nm))
    done = [res[nm][1] for nm in ("mlp_w1", "mlp_w2", "s5_w_glu", "w_out_even", "w_out_odd")]

    (small_all,) = _comm_wait(ags, done, "ags_wait", exchange=False)
    small_full = _unpack(small_all.reshape(rows, LANES), full_shapes)
    loss = 0.5 * small_full.pop()[0, 0] / D_MODEL
    small_g = []
    for nm, g in zip(small_names, small_full):
        if nm in ("pool_scale", "sgu_ln_g", "sgu_ln_b"):
            g = lax.dynamic_slice(g, (my_index * 64,), (64,)).reshape(1, 64)
        small_g.append(g)

    def turned(arrs):
        return [jnp.swapaxes(a, -1, -2) if nm in ("s5_b_re", "s5_b_im") else a for nm, a in zip(small_names, arrs)]

    sd, sm, sv = _adamw_many(turned([weights[nm] for nm in small_names]), turned(small_g),
                             turned([mom_m[nm] for nm in small_names]), turned([mom_v[nm] for nm in small_names]),
                             "adamw_small")
    for nm, g_, d_, m_, v_ in zip(small_names, small_g, turned(sd), turned(sm), turned(sv)):
        res[nm] = (g_, d_, m_, v_)
    done.append(sd[0])

    nm = "w_in_odd"
    outs = _sum_adamw([r_win_o], jnp.transpose(weights[nm], (0, 2, 1)), jnp.transpose(mom_m[nm], (0, 2, 1)),
                      jnp.transpose(mom_v[nm], (0, 2, 1)), "adamw_" + nm)
    res[nm] = tuple(jnp.transpose(o, (0, 2, 1)) for o in outs)
    done.append(res[nm][1])
    nm = "w_in_even"
    (r_win_e,) = _comm_wait(ex5, done, "ex_wait5", exchange=True)
    outs = _sum_adamw_rows(r_win_e, jnp.transpose(weights[nm], (2, 0, 1)), jnp.transpose(mom_m[nm], (2, 0, 1)),
                           jnp.transpose(mom_v[nm], (2, 0, 1)), "adamw_" + nm)
    res[nm] = tuple(jnp.transpose(o, (1, 2, 0)) for o in outs)

    grads = [res[nm][0].reshape(weights[nm].shape) for nm in names]
    deltas = [res[nm][1].reshape(weights[nm].shape) for nm in names]
    new_m = [res[nm][2].reshape(weights[nm].shape) for nm in names]
    new_v = [res[nm][3].reshape(weights[nm].shape) for nm in names]
    return (loss, grad_x[None], *grads, *deltas, *new_m, *new_v)
```

```python
import math

import jax
import jax.numpy as jnp
from jax import lax
from jax.experimental import pallas as pl
from jax.experimental.pallas import tpu as pltpu

F32 = jnp.float32
BF16 = jnp.bfloat16
MESH = pl.DeviceIdType.MESH
ANY = pl.BlockSpec(memory_space=pl.ANY)

N_DEV = 8
D_MODEL = 1024
EPS = 1e-6
NORM_ROWS = 512
FUSED_ROWS = 512
S5_W = 512
S5_NS = 2048
SCAN_GROUPS = 4
SCAN_CHUNK = 1024
FOX_W = 512
EVEN_IN = 2056
EVEN_PAD = 2176
ODD_IN = 1536
LANES = 128
PIECE = 4 * D_MODEL // N_DEV
VMEM_LIMIT = 56 * 1024 * 1024

ADAM_LR = 0.001
ADAM_B1 = 0.9
ADAM_B2 = 0.999
ADAM_EPS = 1e-08
ADAM_WD = 0.01
ADAM_STEP = 10

NT = (((1,), (1,)), ((), ()))
TN = (((0,), (0,)), ((), ()))
NN = (((1,), (0,)), ((), ()))


def _cp(*sem):
    return pltpu.CompilerParams(dimension_semantics=sem, vmem_limit_bytes=VMEM_LIMIT)


def _sds(shape, dtype=F32):
    return jax.ShapeDtypeStruct(tuple(shape), dtype)


def _gelu(x):
    t = jnp.tanh(0.7978845608028654 * (x + 0.044715 * x * x * x))
    return 0.5 * x * (1.0 + t)


def _gelu_grad(x):
    t = jnp.tanh(0.7978845608028654 * (x + 0.044715 * x * x * x))
    du = 0.7978845608028654 * (1.0 + 3.0 * 0.044715 * x * x)
    return 0.5 * (1.0 + t) + 0.5 * x * (1.0 - t * t) * du


def _sigmoid(x):
    return 1.0 / (1.0 + jnp.exp(-x))


def _dot(a, b, dn=NN):
    return lax.dot_general(a, b, dn, preferred_element_type=F32)


def _mm(a, b, *, name, ta=False, tb=False, b3=False, out3=False, out_dtypes=(F32,), epi=None, extra=(),
        cols=(), vecs=(), out_kinds=None, bm=1024, bn=1024, bk=1024, dep=None):
    a_list = list(a) if isinstance(a, (list, tuple)) else [a]
    widths = [p.shape[1] for p in a_list]
    offs = [sum(widths[:i]) for i in range(len(widths))]
    na = len(a_list)
    M = sum(widths) if ta else a_list[0].shape[0]
    K = a_list[0].shape[0] if ta else sum(widths)
    if na > 1:
        assert not b3 and not tb
        bm, bk = (M, bk) if ta else (bm, K)
    pw = b.shape[2] if b3 else PIECE
    if b3:
        N = b.shape[1] if tb else b.shape[0] * pw
        assert (b.shape[0] * pw if tb else b.shape[1]) == K
    else:
        N = b.shape[0] if tb else b.shape[1]
    bm, bn, bk = min(bm, M), min(bn, N), min(bk, K)
    assert M % bm == 0 and N % bn == 0 and K % bk == 0, (name, M, N, K, bm, bn, bk)
    assert not (b3 or out3) or ((bk if tb else bn) % pw == 0 and bn % PIECE == 0)
    nk = K // bk
    n_extra = len(extra) + len(cols) + len(vecs)
    n_out = len(out_dtypes)
    out_kinds = tuple(out_kinds) if out_kinds is not None else ("full",) * n_out
    dn = (((0 if ta else 1,), (1 if tb else 0,)), ((), ()))

    use_acc = nk > 1

    def body(*refs):
        a_refs, b_ref = refs[:na], refs[na]
        a_ref = a_refs[0]
        e_refs = refs[na + 1:na + 1 + n_extra]
        first_out = na + 1 + n_extra + (0 if dep is None else 1)
        o_refs = refs[first_out:first_out + n_out]
        acc_ref = refs[-1] if use_acc else o_refs[0]
        i, k = pl.program_id(0), pl.program_id(2)

        def dot(a_v, b_v):
            return lax.dot_general(a_v.astype(BF16), b_v.astype(BF16), dn, preferred_element_type=F32)

        everything = slice(None)
        if na > 1 and ta:
            terms = [(pl.ds(off, w), everything, r, b_ref) for r, off, w in zip(a_refs, offs, widths)]
        elif na > 1:
            terms = [(everything, everything, r, b_ref.at[pl.ds(off, w), :]) for r, off, w in zip(a_refs, offs, widths)]
        elif not b3:
            terms = [(everything, everything, a_ref, b_ref)]
        elif tb:
            terms = [(everything, everything,
                      a_ref.at[pl.ds(t * pw, pw), :] if ta else a_ref.at[:, pl.ds(t * pw, pw)], b_ref.at[t])
                     for t in range(bk // pw)]
        else:
            terms = [(everything, pl.ds(t * pw, pw), a_ref, b_ref.at[t]) for t in range(bn // pw)]

        def finish(acc):
            outs = (acc,) if epi is None else epi(acc, *[e[...] for e in e_refs])
            for o_ref, o, kind in zip(o_refs, outs, out_kinds):
                if kind == "vsum":
                    @pl.when(i == 0)
                    def _(o_ref=o_ref, o=o):
                        o_ref[...] = o

                    @pl.when(i > 0)
                    def _(o_ref=o_ref, o=o):
                        o_ref[...] += o
                elif out3:
                    for t in range(bn // PIECE):
                        o_ref[t] = o[:, t * PIECE:(t + 1) * PIECE].astype(o_ref.dtype)
                else:
                    o_ref[...] = o.astype(o_ref.dtype)

        if nk == 1:
            bands = {}
            for rows, cols, a_r, b_r in terms:
                key = (getattr(rows, "start", None), getattr(cols, "start", None))
                val = dot(a_r[...], b_r[...])
                bands[key] = val if key not in bands else bands[key] + val
            vals = list(bands.values())
            if len(vals) == 1:
                finish(vals[0])
            else:
                finish(jnp.concatenate(vals, axis=0 if (na > 1 and ta) else 1))
            return

        @pl.when(k == 0)
        def _():
            acc_ref[...] = jnp.zeros_like(acc_ref)

        for rows, cols, a_r, b_r in terms:
            acc_ref[rows, cols] += dot(a_r[...], b_r[...])

        @pl.when(k == nk - 1)
        def _():
            finish(acc_ref[...])

    if na > 1:
        a_specs = [pl.BlockSpec((bk, w), lambda i, j, k: (k, 0)) if ta else pl.BlockSpec((bm, w), lambda i, j, k: (i, 0))
                   for w in widths]
    else:
        a_specs = [pl.BlockSpec((bk, bm), lambda i, j, k: (k, i)) if ta else
                   pl.BlockSpec((bm, bk), lambda i, j, k: (i, k))]
    if b3:
        if tb:
            b_spec = pl.BlockSpec((bk // pw, bn, pw), lambda i, j, k: (k, j, 0))
        else:
            b_spec = pl.BlockSpec((bn // pw, bk, pw), lambda i, j, k: (j, k, 0))
    else:
        b_spec = pl.BlockSpec((bn, bk), lambda i, j, k: (j, k)) if tb else pl.BlockSpec((bk, bn), lambda i, j, k: (k, j))
    e_specs = ([pl.BlockSpec((bm, bn), lambda i, j, k: (i, j)) for _ in extra]
               + [pl.BlockSpec((bm, 1), lambda i, j, k: (i, 0)) for _ in cols]
               + [pl.BlockSpec((1, bn), lambda i, j, k: (0, j)) for _ in vecs])
    if out3:
        o_specs = [pl.BlockSpec((bn // PIECE, bm, PIECE), lambda i, j, k: (j, i, 0)) for _ in out_dtypes]
        o_shapes = [_sds((N // PIECE, M, PIECE), dt) for dt in out_dtypes]
    else:
        spec_of = {"full": pl.BlockSpec((bm, bn), lambda i, j, k: (i, j)),
                   "col": pl.BlockSpec((bm, 1), lambda i, j, k: (i, 0)),
                   "vsum": pl.BlockSpec((1, bn), lambda i, j, k: (0, j))}
        shape_of = {"full": (M, N), "col": (M, 1), "vsum": (1, N)}
        o_specs = [spec_of[kind] for kind in out_kinds]
        o_shapes = [_sds(shape_of[kind], dt) for kind, dt in zip(out_kinds, out_dtypes)]
    assert "col" not in out_kinds or bn == N
    outs = pl.pallas_call(
        body, name=name, grid=(M // bm, N // bn, nk),
        in_specs=a_specs + [b_spec] + e_specs + ([] if dep is None else [ANY]),
        out_specs=o_specs, out_shape=o_shapes,
        scratch_shapes=[pltpu.VMEM((bm, bn), F32)] if use_acc else [],
        compiler_params=_cp("arbitrary" if "vsum" in out_kinds else "parallel", "parallel", "arbitrary"),
    )(*a_list, b, *extra, *cols, *vecs, *([] if dep is None else [dep]))
    return outs[0] if n_out == 1 else outs


def _epi_relu2(acc):
    r = jnp.maximum(acc, 0.0)
    return acc, r * r


def _epi_relu2_bwd(acc, p):
    return (acc * (2.0 * jnp.maximum(p.astype(F32), 0.0)),)


def _row_spec(rb, w=D_MODEL):
    return pl.BlockSpec((rb, w), lambda i: (i, 0))


def _vec_spec(w=D_MODEL):
    return pl.BlockSpec((1, w), lambda i: (0, 0))


def _rstd(v):
    return lax.rsqrt(jnp.mean(v * v, axis=-1, keepdims=True) + EPS)


def _rms_fwd(x, g, name):
    L = x.shape[0]
    rb = min(NORM_ROWS, L)

    def body(x_ref, g_ref, h_ref, r_ref):
        xv = x_ref[...]
        r = _rstd(xv)
        h_ref[...] = (xv * r * g_ref[...]).astype(BF16)
        r_ref[...] = r

    return pl.pallas_call(
        body, name=name, grid=(L // rb,),
        in_specs=[_row_spec(rb), _vec_spec()],
        out_specs=[_row_spec(rb), _row_spec(rb, 1)],
        out_shape=[_sds((L, D_MODEL), BF16), _sds((L, 1))],
        compiler_params=_cp("parallel"),
    )(x, g)


def _rms_bwd_rows(dy, xv, r, g):
    n = xv * r
    dyg = dy * g
    return r * (dyg - n * jnp.mean(dyg * n, axis=-1, keepdims=True)), n


POST_PRE_DTYPES = (F32, F32, BF16, F32, F32)
POST_PRE_KINDS = ("full", "col", "full", "col", "full")
PRE_POST_BWD_DTYPES = (F32, F32, BF16, F32)
PRE_POST_BWD_KINDS = ("full", "vsum", "full", "vsum")


def _epi_post_pre(y, x_in, g_post, g_pre):
    ry = _rstd(y)
    xo = x_in + y * ry * g_post
    rx = _rstd(xo)
    return xo, ry, xo * rx * g_pre, rx, y


def _epi_pre_post_bwd(gh, x, g_out, y_prev, rx, ry_prev, g_pre, g_post_prev):
    gx, n = _rms_bwd_rows(gh, x, rx, g_pre)
    gi = g_out + gx
    gy, ny = _rms_bwd_rows(gi, y_prev, ry_prev, g_post_prev)
    return gi, jnp.sum(gh * n, axis=0, keepdims=True), gy, jnp.sum(gi * ny, axis=0, keepdims=True)


def _epi_pre_bwd(gh, x, g_out, rx, g_pre):
    gx, n = _rms_bwd_rows(gh, x, rx, g_pre)
    return g_out + gx, jnp.sum(gh * n, axis=0, keepdims=True)


def _epi_post_loss(y, x_in, target, g_post):
    ry = _rstd(y)
    diff = x_in + y * ry * g_post - target
    gx = diff * (1.0 / D_MODEL)
    gy, n = _rms_bwd_rows(gx, y, ry, g_post)
    sq = jnp.broadcast_to(jnp.sum(diff * diff, keepdims=True), (1, y.shape[1]))
    return gx, gy, jnp.sum(gx * n, axis=0, keepdims=True), sq


def _cmul(ar, ai, br, bi):
    return ar * br - ai * bi, ar * bi + ai * br


def _zoh_cols(lr, li, ldt):
    dt = jnp.exp(ldt)
    mag = jnp.exp(lr * dt)
    ar = mag * jnp.cos(li * dt)
    ai = mag * jnp.sin(li * dt)
    den = lr * lr + li * li
    nr = ar - 1.0
    qr = (nr * lr + ai * li) / den
    qi = (ai * lr - nr * li) / den
    return dt, ar, ai, qr, qi, den


def _b_mask():
    r = lax.broadcasted_iota(jnp.int32, (S5_NS, LANES), 0)
    c = lax.broadcasted_iota(jnp.int32, (S5_NS, LANES), 1)
    return ((r >> 6) & 7) == (c >> 4)


def _c_mask():
    r = lax.broadcasted_iota(jnp.int32, (S5_W, 512), 0)
    c = lax.broadcasted_iota(jnp.int32, (S5_W, 512), 1)
    return ((r >> 4) & 7) == (c >> 6)


def _s5_prep(lam_r, ldt_r, lam_c, ldt_c, b_t, c_t, name):
    def body(lam_r_ref, ldt_r_ref, lam_c_ref, ldt_c_ref, b_ref, c_ref, tab_ref, bset_ref, cset_ref):
        lr, li = lam_r_ref[0:1, :], lam_r_ref[1:2, :]
        dt = jnp.exp(ldt_r_ref[...])
        mag = jnp.exp(lr * dt)
        p1r, p1i = mag * jnp.cos(li * dt), mag * jnp.sin(li * dt)
        p2r, p2i = _cmul(p1r, p1i, p1r, p1i)
        p3r, p3i = _cmul(p2r, p2i, p1r, p1i)
        p4r, p4i = _cmul(p2r, p2i, p2r, p2i)
        p5r, p5i = _cmul(p4r, p4i, p1r, p1i)
        p6r, p6i = _cmul(p4r, p4i, p2r, p2i)
        p7r, p7i = _cmul(p4r, p4i, p3r, p3i)
        p8r, p8i = _cmul(p4r, p4i, p4r, p4i)
        pw_r = [p1r, p2r, p3r, p4r, p5r, p6r, p7r, p8r]
        pw_i = [p1i, p2i, p3i, p4i, p5i, p6i, p7i, p8i]
        row = lax.broadcasted_iota(jnp.int32, (8, S5_NS), 0)
        zero = jnp.zeros((8, S5_NS), F32)

        def bc(v):
            return jnp.broadcast_to(v, (8, S5_NS))

        for d in range(2):
            sgn = 1.0 if d == 0 else -1.0
            for t, s in enumerate((1, 2, 4)):
                live = (row >= s) if d == 0 else (row <= 7 - s)
                tab_ref[d, 2 * t] = jnp.where(live, bc(pw_r[s - 1]), zero)
                tab_ref[d, 2 * t + 1] = jnp.where(live, bc(sgn * pw_i[s - 1]), zero)
            cr, ci = zero, zero
            for i in range(8):
                e = i if d == 0 else 7 - i
                cr = jnp.where(row == i, bc(pw_r[e]), cr)
                ci = jnp.where(row == i, bc(sgn * pw_i[e]), ci)
            tab_ref[d, 6] = cr
            tab_ref[d, 7] = ci

        _, _, _, qr, qi, _ = _zoh_cols(lam_c_ref[:, 0:1], lam_c_ref[:, 1:2], ldt_c_ref[...])
        bm = _b_mask()
        br, bi = b_ref[0], b_ref[1]
        bset_ref[0] = jnp.where(bm, qr * br - qi * bi, 0.0).astype(BF16)
        bset_ref[1] = jnp.where(bm, qr * bi + qi * br, 0.0).astype(BF16)
        cm = _c_mask()
        cset_ref[0] = jnp.where(cm, c_ref[0], 0.0).astype(BF16)
        cset_ref[1] = jnp.where(cm, c_ref[1], 0.0).astype(BF16)

    vm = pl.BlockSpec(memory_space=pltpu.VMEM)
    return pl.pallas_call(
        body, name=name, in_specs=[vm] * 6, out_specs=[vm] * 3,
        out_shape=[_sds((2, 8, 8, S5_NS)), _sds((2, S5_NS, LANES), BF16), _sds((2, S5_W, 512), BF16)],
        compiler_params=pltpu.CompilerParams(vmem_limit_bytes=VMEM_LIMIT),
    )(lam_r, ldt_r, lam_c, ldt_c, b_t, c_t)


SCAN_W = SCAN_GROUPS * LANES


def _scan_chunk(src_ref, dst_ref, tab_ref, carry_ref, nb, reverse, xs_ref=None, acc_ref=None):
    row = lax.broadcasted_iota(jnp.int32, (8, LANES), 0)

    def step(i, carry):
        b = (nb - 1 - i) if reverse else i
        off = pl.multiple_of(b * 8, 8)
        out = []
        for g in range(SCAN_GROUPS):
            lanes = pl.ds(g * LANES, LANES)
            cr, ci = carry[2 * g], carry[2 * g + 1]
            yr = src_ref[0, pl.ds(off, 8), lanes]
            yi = src_ref[1, pl.ds(off, 8), lanes]
            for t, s in enumerate((1, 2, 4)):
                sh = (8 - s) if reverse else s
                sr = pltpu.roll(yr, sh, 0)
                si = pltpu.roll(yi, sh, 0)
                mr, mi = tab_ref[2 * t, :, lanes], tab_ref[2 * t + 1, :, lanes]
                yr, yi = yr + mr * sr - mi * si, yi + mr * si + mi * sr
            pr, pi = tab_ref[6, :, lanes], tab_ref[7, :, lanes]
            yr, yi = yr + pr * cr - pi * ci, yi + pr * ci + pi * cr
            dst_ref[0, pl.ds(off, 8), lanes] = yr
            dst_ref[1, pl.ds(off, 8), lanes] = yi
            if xs_ref is not None:
                nr = jnp.where(row == 7, cr, pltpu.roll(yr, 7, 0))
                ni = jnp.where(row == 7, ci, pltpu.roll(yi, 7, 0))
                xr = xs_ref[0, pl.ds(off, 8), lanes]
                xi = xs_ref[1, pl.ds(off, 8), lanes]
                acc_ref[0, :, lanes] += xr * nr + xi * ni
                acc_ref[1, :, lanes] += xr * ni - xi * nr
            last = 0 if reverse else 7
            out += [jnp.broadcast_to(yr[last:last + 1, :], (8, LANES)),
                    jnp.broadcast_to(yi[last:last + 1, :], (8, LANES))]
        return tuple(out)

    init = []
    for g in range(SCAN_GROUPS):
        init += [carry_ref[0, :, pl.ds(g * LANES, LANES)], carry_ref[1, :, pl.ds(g * LANES, LANES)]]
    fin = lax.fori_loop(0, nb, step, tuple(init))
    for g in range(SCAN_GROUPS):
        carry_ref[0, :, pl.ds(g * LANES, LANES)] = fin[2 * g]
        carry_ref[1, :, pl.ds(g * LANES, LANES)] = fin[2 * g + 1]


def _s5_scan_fwd(z, bset, cset, dvec, tabs, name):
    L = z.shape[0]
    tl = min(SCAN_CHUNK, L)
    nc = L // tl

    def body(u_ref, b_ref, c_ref, d_ref, tab_ref, x_ref, y_ref, carry_ref):
        @pl.when(pl.program_id(1) == 0)
        def _():
            carry_ref[...] = jnp.zeros_like(carry_ref)

        uf = u_ref[...]
        u = uf.astype(BF16)
        x_ref[0] = _dot(u, b_ref[0], NT)
        x_ref[1] = _dot(u, b_ref[1], NT)
        _scan_chunk(x_ref, x_ref, tab_ref, carry_ref, tl // 8, False)
        y_ref[...] = (_dot(x_ref[0].astype(BF16), c_ref[0], NT) - _dot(x_ref[1].astype(BF16), c_ref[1], NT)
                      + d_ref[...] * uf)

    col = pl.BlockSpec((tl, LANES), lambda j, c: (c, j))
    return pl.pallas_call(
        body, name=name, grid=(S5_NS // SCAN_W, nc),
        in_specs=[col, pl.BlockSpec((2, SCAN_W, LANES), lambda j, c: (0, j, 0)),
                  pl.BlockSpec((2, LANES, SCAN_W), lambda j, c: (0, j, 0)),
                  pl.BlockSpec((1, LANES), lambda j, c: (0, j)),
                  pl.BlockSpec((None, 8, 8, SCAN_W), lambda j, c: (0, 0, 0, j))],
        out_specs=[pl.BlockSpec((2, tl, SCAN_W), lambda j, c: (0, c, j)), col],
        out_shape=[_sds((2, L, S5_NS)), _sds((L, S5_W))],
        scratch_shapes=[pltpu.VMEM((2, 8, SCAN_W), F32)],
        compiler_params=_cp("parallel", "arbitrary"),
    )(z, bset, cset, dvec, tabs)


def _s5_scan_bwd(gyl, cset, xs, z, bset, gud, tabs, name):
    L = z.shape[0]
    tl = min(SCAN_CHUNK, L)
    nc = L // tl

    def body(g_ref, c_ref, xs_ref, u_ref, b_ref, gud_ref, tab_ref, gu_ref, ga_ref, gb_ref, gc_ref,
             gx_ref, carry_ref, acc_ref):
        c = pl.program_id(1)

        @pl.when(c == 0)
        def _():
            carry_ref[...] = jnp.zeros_like(carry_ref)
            acc_ref[...] = jnp.zeros_like(acc_ref)
            gb_ref[...] = jnp.zeros_like(gb_ref)
            gc_ref[...] = jnp.zeros_like(gc_ref)

        gy = g_ref[...].astype(BF16)
        gx_ref[0] = _dot(gy, c_ref[0])
        gx_ref[1] = -_dot(gy, c_ref[1])
        gc_ref[0] += _dot(gy, xs_ref[0].astype(BF16), TN)
        gc_ref[1] -= _dot(gy, xs_ref[1].astype(BF16), TN)
        _scan_chunk(gx_ref, gx_ref, tab_ref, carry_ref, tl // 8, True, xs_ref, acc_ref)
        gr = gx_ref[0].astype(BF16)
        gi = gx_ref[1].astype(BF16)
        gu_ref[...] = gud_ref[...] + _dot(gr, b_ref[0]) + _dot(gi, b_ref[1])
        u = u_ref[...].astype(BF16)
        gb_ref[0] += _dot(gr, u, TN)
        gb_ref[1] += _dot(gi, u, TN)

        @pl.when(c == nc - 1)
        def _():
            ga_ref[0:1, :] = jnp.sum(acc_ref[0], axis=0, keepdims=True)
            ga_ref[1:2, :] = jnp.sum(acc_ref[1], axis=0, keepdims=True)

    rev = lambda j, c: (nc - 1 - c, j)
    col = pl.BlockSpec((tl, LANES), rev)
    return pl.pallas_call(
        body, name=name, grid=(S5_NS // SCAN_W, nc),
        in_specs=[col, pl.BlockSpec((2, LANES, SCAN_W), lambda j, c: (0, j, 0)),
                  pl.BlockSpec((2, tl, SCAN_W), lambda j, c: (0, nc - 1 - c, j)), col,
                  pl.BlockSpec((2, SCAN_W, LANES), lambda j, c: (0, j, 0)), col,
                  pl.BlockSpec((None, 8, 8, SCAN_W), lambda j, c: (1, 0, 0, j))],
        out_specs=[col, pl.BlockSpec((2, SCAN_W), lambda j, c: (0, j)),
                   pl.BlockSpec((2, SCAN_W, LANES), lambda j, c: (0, j, 0)),
                   pl.BlockSpec((2, LANES, SCAN_W), lambda j, c: (0, j, 0))],
        out_shape=[_sds((L, S5_W)), _sds((2, S5_NS)), _sds((2, S5_NS, LANES)), _sds((2, S5_W, 512))],
        scratch_shapes=[pltpu.VMEM((2, tl, SCAN_W), F32), pltpu.VMEM((2, 8, SCAN_W), F32),
                        pltpu.VMEM((2, 8, SCAN_W), F32)],
        compiler_params=_cp("parallel", "arbitrary"),
    )(gyl, cset, xs, z, bset, gud, tabs)


def _s5_glu_fwd(ylin, wglu, name):
    L = ylin.shape[0]
    bl = min(1024, L)

    def body(ylin_ref, w_ref, ya_ref):
        yg = _gelu(ylin_ref[...])
        t = _dot(yg.astype(BF16), w_ref[...])
        ya_ref[...] = (yg * _sigmoid(t)).astype(BF16)

    return pl.pallas_call(
        body, name=name, grid=(L // bl,),
        in_specs=[pl.BlockSpec((bl, S5_W), lambda i: (i, 0)), pl.BlockSpec((S5_W, S5_W), lambda i: (0, 0))],
        out_specs=pl.BlockSpec((bl, S5_W), lambda i: (i, 0)),
        out_shape=_sds((L, S5_W), BF16),
        compiler_params=_cp("parallel"),
    )(ylin, wglu)


def _s5_glu_bwd(g_y, wout, ylin, z, dvec, wglu, name):
    L = z.shape[0]
    bl = min(256, L)

    def body(g_ref, wo_ref, ylin_ref, u_ref, d_ref, w_ref, gyl_ref, gud_ref, gw_ref, gd_ref):
        i = pl.program_id(0)
        ylin = ylin_ref[...]
        yg = _gelu(ylin)
        ygb = yg.astype(BF16)
        sg = _sigmoid(_dot(ygb, w_ref[...]))
        gya = _dot(g_ref[...], wo_ref[...], NT)
        gt = gya * yg * sg * (1.0 - sg)
        gtb = gt.astype(BF16)
        gyg = gya * sg + _dot(gtb, w_ref[...], NT)
        gyl = gyg * _gelu_grad(ylin)
        gyl_ref[...] = gyl
        gud_ref[...] = gyl * d_ref[...]

        @pl.when(i == 0)
        def _():
            gw_ref[...] = jnp.zeros_like(gw_ref)
            gd_ref[...] = jnp.zeros_like(gd_ref)

        gw_ref[...] += _dot(ygb, gtb, TN)
        gd_ref[...] += jnp.sum(gyl * u_ref[...], axis=0, keepdims=True)

    blk = pl.BlockSpec((bl, S5_W), lambda i: (i, 0))
    return pl.pallas_call(
        body, name=name, grid=(L // bl,),
        in_specs=[pl.BlockSpec((bl, D_MODEL), lambda i: (i, 0)), pl.BlockSpec((S5_W, D_MODEL), lambda i: (0, 0)),
                  blk, blk, pl.BlockSpec((1, S5_W), lambda i: (0, 0)), pl.BlockSpec((S5_W, S5_W), lambda i: (0, 0))],
        out_specs=[blk, blk, pl.BlockSpec((S5_W, S5_W), lambda i: (0, 0)), pl.BlockSpec((1, S5_W), lambda i: (0, 0))],
        out_shape=[_sds((L, S5_W)), _sds((L, S5_W)), _sds((S5_W, S5_W)), _sds((1, S5_W))],
        compiler_params=_cp("arbitrary"),
    )(g_y, wout, ylin, z, dvec, wglu)


def _s5_param_bwd(lam_c, ldt_c, b_t, gb, ga_c, gc, name):
    def body(lam_ref, ldt_ref, b_ref, gb_ref, ga_ref, gc_ref, glam_ref, gldt_ref, gbo_ref, gco_ref):
        lr, li = lam_ref[:, 0:1], lam_ref[:, 1:2]
        dt, ar, ai, qr, qi, den = _zoh_cols(lr, li, ldt_ref[...])
        bm = _b_mask()
        gbr = jnp.where(bm, gb_ref[0], 0.0)
        gbi = jnp.where(bm, gb_ref[1], 0.0)
        br, bi = b_ref[0], b_ref[1]
        obr = gbr * qr + gbi * qi
        obi = gbi * qr - gbr * qi
        gqr = jnp.sum(gbr * br + gbi * bi, axis=1, keepdims=True)
        gqi = jnp.sum(gbi * br - gbr * bi, axis=1, keepdims=True)
        for s in (64, 32, 16):
            obr = obr + pltpu.roll(obr, s, 1)
            obi = obi + pltpu.roll(obi, s, 1)
        gbo_ref[0] = obr
        gbo_ref[1] = obi
        gar = ga_ref[:, 0:1] + (gqr * lr - gqi * li) / den
        gai = ga_ref[:, 1:2] + (gqr * li + gqi * lr) / den
        qlr = (qr * lr + qi * li) / den
        qli = (qi * lr - qr * li) / den
        glr = -(gqr * qlr + gqi * qli)
        gli = -(gqi * qlr - gqr * qli)
        glr = glr + dt * (gar * ar + gai * ai)
        gli = gli + dt * (gai * ar - gar * ai)
        wr, wi = _cmul(lr, li, ar, ai)
        gldt = (gar * wr + gai * wi) * dt
        glam_ref[:, 0:1] = glr
        glam_ref[:, 1:2] = gli
        r = lax.broadcasted_iota(jnp.int32, (S5_NS, 32), 0)
        c = lax.broadcasted_iota(jnp.int32, (S5_NS, 32), 1)
        gldt_ref[...] = jnp.sum(jnp.where((r >> 6) == c, gldt, 0.0), axis=0, keepdims=True)
        cm = _c_mask()
        for k in range(2):
            oc = jnp.where(cm, gc_ref[k], 0.0)
            for s in (256, 128, 64):
                oc = oc + pltpu.roll(oc, s, 1)
            gco_ref[k] = oc[:, 0:LANES]

    vm = pl.BlockSpec(memory_space=pltpu.VMEM)
    return pl.pallas_call(
        body, name=name, in_specs=[vm] * 6, out_specs=[vm] * 4,
        out_shape=[_sds((S5_NS, 2)), _sds((1, 32)), _sds((2, S5_NS, LANES)), _sds((2, S5_W, LANES))],
        compiler_params=pltpu.CompilerParams(vmem_limit_bytes=VMEM_LIMIT),
    )(lam_c, ldt_c, b_t, gb, ga_c, gc)


FL_BLK = EVEN_PAD // LANES - 1
Q_BLK, K_BLK, V_BLK = 4, 8, 12
NEG = -1e30


def _log_sigmoid(v):
    return jnp.minimum(v, 0.0) - jnp.log(1.0 + jnp.exp(-jnp.abs(v)))


def _fox_f_fwd(z, bf, name):
    L = z.shape[0]

    def body(fl_ref, b_ref, f_ref, fq_ref):
        row = lax.broadcasted_iota(jnp.int32, (L, LANES), 0)
        cs = _cumsum_rows(_log_sigmoid(fl_ref[...] + b_ref[...]), True, row)
        f_ref[...] = cs
        expand = (lax.broadcasted_iota(jnp.int32, (LANES, FOX_W), 0)
                  == (lax.broadcasted_iota(jnp.int32, (LANES, FOX_W), 1) >> 6)).astype(F32)
        fq_ref[...] = lax.dot_general(cs, expand, NN, precision=lax.Precision.HIGHEST, preferred_element_type=F32)

    return pl.pallas_call(
        body, name=name, grid=(1,),
        in_specs=[pl.BlockSpec((L, LANES), lambda i: (0, FL_BLK)), pl.BlockSpec((1, LANES), lambda i: (0, 0))],
        out_specs=[pl.BlockSpec((L, LANES), lambda i: (0, 0)), pl.BlockSpec((L, FOX_W), lambda i: (0, 0))],
        out_shape=[_sds((L, LANES)), _sds((L, FOX_W))],
        compiler_params=_cp("arbitrary"),
    )(z, bf)


def _fox_f_bwd(dFk, dfq, z, bf, name):
    L = z.shape[0]

    def body(dfk_ref, dfq_ref, fl_ref, b_ref, dfl_ref, db_ref):
        sel = (lax.broadcasted_iota(jnp.int32, (FOX_W, LANES), 0)
               == 64 * lax.broadcasted_iota(jnp.int32, (FOX_W, LANES), 1)).astype(F32)
        dfq_h = lax.dot_general(dfq_ref[...], sel, NN, precision=lax.Precision.HIGHEST, preferred_element_type=F32)
        row = lax.broadcasted_iota(jnp.int32, (L, LANES), 0)
        cs = _cumsum_rows(dfk_ref[...] + dfq_h, False, row)
        dfl = cs * _sigmoid(-(fl_ref[...] + b_ref[...]))
        dfl_ref[...] = dfl
        db_ref[...] = jnp.sum(dfl, axis=0, keepdims=True)

    return pl.pallas_call(
        body, name=name, grid=(1,),
        in_specs=[pl.BlockSpec((L, LANES), lambda i: (0, 0)), pl.BlockSpec((L, FOX_W), lambda i: (0, 0)),
                  pl.BlockSpec((L, LANES), lambda i: (0, FL_BLK)), pl.BlockSpec((1, LANES), lambda i: (0, 0))],
        out_specs=[pl.BlockSpec((L, LANES), lambda i: (0, 0)), pl.BlockSpec((1, LANES), lambda i: (0, 0))],
        out_shape=[_sds((L, LANES)), _sds((1, LANES))],
        compiler_params=_cp("arbitrary"),
    )(dFk, dfq, z, bf)


def _head_mask(hh):
    lane = lax.broadcasted_iota(jnp.int32, (1, LANES), 1)
    return (lane >> 6) == hh


FOX_T = 512


def _fox_head(x, hh):
    return jnp.where(_head_mask(hh), x, 0.0).astype(BF16)


def _fox_scores(qh, k, fq_ref, fr_ref, hh, causal):
    if fq_ref is None:
        s = _dot(qh, k, NT) - fr_ref[hh:hh + 1, :]
    else:
        s = _dot(qh, k, NT) + (fq_ref[:, 64 * hh:64 * hh + 1] - fr_ref[hh:hh + 1, :])
    return s if causal is None else jnp.where(causal, s, NEG)


def _causal(T):
    return lax.broadcasted_iota(jnp.int32, (T, T), 1) <= lax.broadcasted_iota(jnp.int32, (T, T), 0)


def _fox_fwd(z, fq, frow, name):
    L = z.shape[0]
    T = min(FOX_T, L)
    nq = L // T

    def body(qt_ref, kt_ref, q_ref, k_ref, v_ref, fq_ref, fr_ref, o_ref, lse_ref, m_ref, l_ref, acc_ref):
        t = pl.program_id(1)
        qi, ki = qt_ref[t], kt_ref[t]

        @pl.when(ki == 0)
        def _():
            m_ref[...] = jnp.full_like(m_ref, NEG)
            l_ref[...] = jnp.zeros_like(l_ref)
            acc_ref[...] = jnp.zeros_like(acc_ref)

        def step(diagonal):
            q = q_ref[...] * 0.125
            k = k_ref[...].astype(BF16)
            v = v_ref[...].astype(BF16)
            causal = _causal(T) if diagonal else None
            s = jnp.concatenate([_fox_scores(_fox_head(q, hh), k, fq_ref, fr_ref, hh, causal) for hh in range(2)],
                                axis=0)
            m_old = m_ref[...]
            m_new = jnp.maximum(m_old, jnp.max(s, axis=1, keepdims=True))
            alpha = jnp.exp(m_old - m_new)
            p = jnp.exp(s - m_new)
            l_ref[...] = alpha * l_ref[...] + jnp.sum(p, axis=1, keepdims=True)
            m_ref[...] = m_new
            acc_ref[...] = alpha * acc_ref[...] + _dot(p.astype(BF16), v)

        @pl.when(ki < qi)
        def _():
            step(False)

        @pl.when(ki == qi)
        def _():
            step(True)
            h0 = _head_mask(0)
            l = l_ref[...]
            o_h = acc_ref[...] / l
            lse_h = m_ref[...] + jnp.log(l)
            o_ref[...] = jnp.where(h0, o_h[:T], o_h[T:])
            lse_ref[...] = jnp.where(h0, lse_h[:T], lse_h[T:]) - fq_ref[...]

    pairs = [(qi, ki) for qi in range(nq) for ki in range(qi + 1)]
    qt = jnp.asarray([p[0] for p in pairs], jnp.int32)
    kt = jnp.asarray([p[1] for p in pairs], jnp.int32)

    def qspec(base):
        return pl.BlockSpec((T, LANES), lambda j, t, qt, kt: (qt[t], base + j))

    def kspec(base):
        return pl.BlockSpec((T, LANES), lambda j, t, qt, kt: (kt[t], base + j))

    return pl.pallas_call(
        body, name=name,
        grid_spec=pltpu.PrefetchScalarGridSpec(
            num_scalar_prefetch=2, grid=(4, len(pairs)),
            in_specs=[qspec(Q_BLK), kspec(K_BLK), kspec(V_BLK), qspec(0),
                      pl.BlockSpec((None, 2, T), lambda j, t, qt, kt: (j, 0, kt[t]))],
            out_specs=[qspec(0), qspec(0)],
            scratch_shapes=[pltpu.VMEM((2 * T, 1), F32), pltpu.VMEM((2 * T, 1), F32),
                            pltpu.VMEM((2 * T, LANES), F32)]),
        out_shape=[_sds((L, FOX_W)), _sds((L, FOX_W))],
        compiler_params=_cp("parallel", "arbitrary"),
    )(qt, kt, z, z, z, fq, frow)


def _fox_bwd(z, frow, o, lse, g_m, name):
    L = z.shape[0]
    T = min(FOX_T, L)
    nq = L // T

    pairs = [(qi, ki) for ki in range(nq) for qi in range(ki, nq)]
    qt = jnp.asarray([p[0] for p in pairs], jnp.int32)
    kt = jnp.asarray([p[1] for p in pairs], jnp.int32)

    def body(qt_ref, kt_ref, q_ref, k_ref, v_ref, fr_ref, o_ref, lse_ref, do_ref,
             dq_ref, dk_ref, dv_ref, dfq_ref, dfk_ref, dk_acc, dv_acc, df_acc):
        t = pl.program_id(1)
        qi, ki = qt_ref[t], kt_ref[t]

        @pl.when(t == 0)
        def _():
            dq_ref[...] = jnp.zeros_like(dq_ref)
            dfq_ref[...] = jnp.zeros_like(dfq_ref)

        @pl.when(qi == ki)
        def _():
            dk_acc[...] = jnp.zeros_like(dk_acc)
            dv_acc[...] = jnp.zeros_like(dv_acc)
            df_acc[...] = jnp.zeros_like(df_acc)

        def step(diagonal):
            q = q_ref[...] * 0.125
            qb = q.astype(BF16)
            k = k_ref[...].astype(BF16)
            v = v_ref[...].astype(BF16)
            do = do_ref[...]
            dob = do.astype(BF16)
            do_o = dob.astype(F32) * o_ref[...]
            causal = _causal(T) if diagonal else None
            dvs, dks, dqs, rss = [], [], [], []
            for hh in range(2):
                s = _fox_scores(_fox_head(q, hh), k, None, fr_ref, hh, causal)
                p = jnp.exp(s - lse_ref[:, 64 * hh:64 * hh + 1])
                dp = _dot(_fox_head(do, hh), v, NT)
                delta = jnp.sum(jnp.where(_head_mask(hh), do_o, 0.0), axis=1, keepdims=True)
                ds = p * (dp - delta)
                dsb = ds.astype(BF16)
                dvs.append(_dot(p.astype(BF16), dob, TN))
                dks.append(_dot(dsb, qb, TN))
                dqs.append(_dot(dsb, k))
                rss.append(jnp.sum(ds, axis=1, keepdims=True))
                df_acc[hh:hh + 1, :] -= jnp.sum(ds, axis=0, keepdims=True)
            h0 = _head_mask(0)
            dv_acc[...] += jnp.where(h0, dvs[0], dvs[1])
            dk_acc[...] += jnp.where(h0, dks[0], dks[1])
            rows = pl.ds(pl.multiple_of(qi * T, T), T)
            dq_ref[rows, :] += jnp.where(h0, dqs[0], dqs[1])
            dfq_ref[rows, :] += jnp.where(h0, rss[0], rss[1])

        @pl.when(qi > ki)
        def _():
            step(False)

        @pl.when(qi == ki)
        def _():
            step(True)

        @pl.when(qi == nq - 1)
        def _():
            dk_ref[...] = dk_acc[...]
            dv_ref[...] = dv_acc[...]
            dfk_ref[...] = df_acc[...]

        @pl.when(t == len(pairs) - 1)
        def _():
            dq_ref[...] = dq_ref[...] * 0.125

    def qside(base):
        return pl.BlockSpec((T, LANES), lambda j, t, qt, kt: (qt[t], base + j))

    def kside(base):
        return pl.BlockSpec((T, LANES), lambda j, t, qt, kt: (kt[t], base + j))

    pair = pl.BlockSpec((L, LANES), lambda j, t, qt, kt: (0, j))
    frow_spec = pl.BlockSpec((None, 2, T), lambda j, t, qt, kt: (j, 0, kt[t]))
    return pl.pallas_call(
        body, name=name,
        grid_spec=pltpu.PrefetchScalarGridSpec(
            num_scalar_prefetch=2, grid=(4, len(pairs)),
            in_specs=[qside(Q_BLK), kside(K_BLK), kside(V_BLK), frow_spec, qside(0), qside(0), qside(0)],
            out_specs=[pair, kside(0), kside(0), pair, frow_spec],
            scratch_shapes=[pltpu.VMEM((T, LANES), F32), pltpu.VMEM((T, LANES), F32), pltpu.VMEM((2, T), F32)]),
        out_shape=[_sds((L, FOX_W)), _sds((L, FOX_W)), _sds((L, FOX_W)), _sds((L, FOX_W)), _sds((4, 2, L))],
        compiler_params=_cp("parallel", "arbitrary"),
    )(qt, kt, z, z, z, frow, o, lse, g_m)


def _shift_rows(v, s, down, row):
    n = v.shape[0]
    if down:
        return jnp.where(row >= s, pltpu.roll(v, s, 0), 0.0)
    return jnp.where(row < n - s, pltpu.roll(v, n - s, 0), 0.0)


def _cumsum_rows(v, down, row):
    s = 1
    while s < v.shape[0]:
        v = v + _shift_rows(v, s, down, row)
        s *= 2
    return v


def _window_sum(v, g, down, row):
    out = jnp.zeros_like(v)
    s = v
    for k in range(4):
        s = s + _shift_rows(s, 1 << k, down, row)
        out = jnp.where(g == k, s, out)
    return out


def _pool_inv_cnt(g, row):
    w = jnp.left_shift(2, g).astype(F32)
    return 1.0 / jnp.minimum(row.astype(F32) + 1.0, w)


def _pool_fwd(z, pool_w, scale, name):
    L = z.shape[0]

    def body(x_ref, w_ref, s_ref, y_ref, p_ref):
        g = pl.program_id(0)
        row = lax.broadcasted_iota(jnp.int32, (L, LANES), 0)
        x = x_ref[...]
        pooled = (_window_sum(x, g, True, row) * _pool_inv_cnt(g, row) - x).astype(BF16)
        p_ref[...] = pooled
        y_ref[...] = (_dot(pooled, w_ref[...].astype(BF16)) * s_ref[...]).astype(BF16)

    col = pl.BlockSpec((L, LANES), lambda g: (0, g))
    return pl.pallas_call(
        body, name=name, grid=(4,),
        in_specs=[col, pl.BlockSpec((None, LANES, LANES), lambda g: (g, 0, 0)), pl.BlockSpec((1, LANES), lambda g: (0, g))],
        out_specs=[col, col],
        out_shape=[_sds((L, 512), BF16), _sds((L, 512), BF16)],
        compiler_params=_cp("parallel"),
    )(z, pool_w, scale)


def _pool_bwd(g_y, wout, pooled, pool_w, scale, name):
    L = g_y.shape[0]

    def body(g_ref, wo_ref, p_ref, w_ref, s_ref, gx_ref, gw_ref, gs_ref):
        g = pl.program_id(0)
        row = lax.broadcasted_iota(jnp.int32, (L, LANES), 0)
        gy = _dot(g_ref[...], wo_ref[...], NT)
        pooled = p_ref[...]
        wb = w_ref[...].astype(BF16)
        lin = _dot(pooled, wb)
        gs_ref[...] = jnp.sum(gy * lin, axis=0, keepdims=True)
        glin = (gy * s_ref[...]).astype(BF16)
        gw_ref[...] = _dot(pooled, glin, TN)
        gp = _dot(glin, wb, NT)
        gx_ref[...] = _window_sum(gp * _pool_inv_cnt(g, row), g, False, row) - gp

    col = pl.BlockSpec((L, LANES), lambda g: (0, g))
    wspec = pl.BlockSpec((None, LANES, LANES), lambda g: (g, 0, 0))
    vec = pl.BlockSpec((1, LANES), lambda g: (0, g))
    return pl.pallas_call(
        body, name=name, grid=(4,),
        in_specs=[pl.BlockSpec((L, D_MODEL), lambda g: (0, 0)), pl.BlockSpec((LANES, D_MODEL), lambda g: (g, 0)),
                  col, wspec, vec],
        out_specs=[col, wspec, vec],
        out_shape=[_sds((L, 512)), _sds((4, LANES, LANES)), _sds((1, 512))],
        compiler_params=_cp("parallel"),
    )(g_y, wout, pooled, pool_w, scale)


SGU_CHUNKS = 4


def _sgu_ln(v, gam, bet):
    gv = _gelu(v)
    mu = jnp.mean(gv, axis=-1, keepdims=True)
    xc = gv - mu
    rs = lax.rsqrt(jnp.mean(xc * xc, axis=-1, keepdims=True) + EPS)
    xh = xc * rs
    return xh, rs, xh * gam + bet


def _tril_ws(w_ref, g):
    r = lax.broadcasted_iota(jnp.int32, (LANES, LANES), 0)
    c = lax.broadcasted_iota(jnp.int32, (LANES, LANES), 1)
    return jnp.where(r >= c, w_ref[g], 0.0).astype(BF16)


def _sgu_fwd(z, ln_g, ln_b, w_s, b_st, name):
    L = z.shape[0]
    rb = min(SGU_CHUNKS * LANES, L)

    def body(u_ref, v_ref, g_ref, b_ref, w_ref, bs_ref, y_ref):
        _, _, vln = _sgu_ln(v_ref[...], g_ref[...], b_ref[...])
        gu = _gelu(u_ref[...])
        vb = vln.astype(BF16)
        for g in range(4):
            ws = _tril_ws(w_ref, g)
            for n in range(rb // LANES):
                rows = slice(n * LANES, (n + 1) * LANES)
                cols = slice(g * LANES, (g + 1) * LANES)
                mixed = _dot(ws, vb[rows, cols]) + bs_ref[:, g:g + 1]
                y_ref[rows, cols] = (gu[rows, cols] * mixed).astype(BF16)

    vm = lambda shape: pl.BlockSpec(shape, lambda i: tuple(0 for _ in shape))
    return pl.pallas_call(
        body, name=name, grid=(L // rb,),
        in_specs=[pl.BlockSpec((rb, 512), lambda i: (i, 1)), pl.BlockSpec((rb, 512), lambda i: (i, 2)),
                  vm((1, 512)), vm((1, 512)), vm((4, LANES, LANES)), vm((LANES, 4))],
        out_specs=pl.BlockSpec((rb, 512), lambda i: (i, 0)),
        out_shape=_sds((L, 512), BF16),
        compiler_params=_cp("parallel"),
    )(z, z, ln_g, ln_b, w_s, b_st)


def _sgu_bwd(g_y, wout, z, ln_g, ln_b, w_s, b_st, name):
    L = z.shape[0]
    rb = min(SGU_CHUNKS * LANES, L)

    def body(gyo_ref, wo_ref, u_ref, v_ref, g_ref, b_ref, w_ref, bs_ref, gu_ref, gv_ref, gw_ref, gbs_ref, gg_ref,
             gb_ref):
        i = pl.program_id(0)

        @pl.when(i == 0)
        def _():
            gw_ref[...] = jnp.zeros_like(gw_ref)
            gbs_ref[...] = jnp.zeros_like(gbs_ref)
            gg_ref[...] = jnp.zeros_like(gg_ref)
            gb_ref[...] = jnp.zeros_like(gb_ref)

        v = v_ref[...]
        u = u_ref[...]
        gy = _dot(gyo_ref[...], wo_ref[...], NT)
        xh, rs, vln = _sgu_ln(v, g_ref[...], b_ref[...])
        gel_u = _gelu(u)
        gmix = gy * gel_u
        vb = vln.astype(BF16)
        gmb = gmix.astype(BF16)
        r = lax.broadcasted_iota(jnp.int32, (LANES, LANES), 0)
        c = lax.broadcasted_iota(jnp.int32, (LANES, LANES), 1)
        gvln_cols = []
        for g in range(4):
            ws = _tril_ws(w_ref, g)
            cols = slice(g * LANES, (g + 1) * LANES)
            gw = jnp.zeros((LANES, LANES), F32)
            gbs = jnp.zeros((LANES, 1), F32)
            parts = []
            for n in range(rb // LANES):
                rows = slice(n * LANES, (n + 1) * LANES)
                mixed = _dot(ws, vb[rows, cols]) + bs_ref[:, g:g + 1]
                gu_ref[rows, cols] = gy[rows, cols] * mixed * _gelu_grad(u[rows, cols])
                parts.append(_dot(ws, gmb[rows, cols], TN))
                gw = gw + _dot(gmb[rows, cols], vb[rows, cols], NT)
                gbs = gbs + jnp.sum(gmix[rows, cols], axis=1, keepdims=True)
            gvln_cols.append(jnp.concatenate(parts, axis=0))
            gw_ref[g] += jnp.where(r >= c, gw, 0.0)
            gbs_ref[:, g:g + 1] += gbs
        gvln = jnp.concatenate(gvln_cols, axis=1)
        gg_ref[...] += jnp.sum(gvln * xh, axis=0, keepdims=True)
        gb_ref[...] += jnp.sum(gvln, axis=0, keepdims=True)
        gxh = gvln * g_ref[...]
        ggv = rs * (gxh - jnp.mean(gxh, axis=-1, keepdims=True) - xh * jnp.mean(gxh * xh, axis=-1, keepdims=True))
        gv_ref[...] = ggv * _gelu_grad(v)

    vm = lambda shape: pl.BlockSpec(shape, lambda i: tuple(0 for _ in shape))
    blk = pl.BlockSpec((rb, 512), lambda i: (i, 0))
    return pl.pallas_call(
        body, name=name, grid=(L // rb,),
        in_specs=[pl.BlockSpec((rb, D_MODEL), lambda i: (i, 0)), pl.BlockSpec((512, D_MODEL), lambda i: (1, 0)),
                  pl.BlockSpec((rb, 512), lambda i: (i, 1)), pl.BlockSpec((rb, 512), lambda i: (i, 2)),
                  vm((1, 512)), vm((1, 512)), vm((4, LANES, LANES)), vm((LANES, 4))],
        out_specs=[blk, blk, vm((4, LANES, LANES)), vm((LANES, 4)), vm((1, 512)), vm((1, 512))],
        out_shape=[_sds((L, 512)), _sds((L, 512)), _sds((4, LANES, LANES)), _sds((LANES, 4)),
                   _sds((1, 512)), _sds((1, 512))],
        compiler_params=_cp("arbitrary"),
    )(g_y, wout, z, z, ln_g, ln_b, w_s, b_st)


def _adamw_math(w, g, m, v):
    nm = ADAM_B1 * m + (1.0 - ADAM_B1) * g
    nv = ADAM_B2 * v + (1.0 - ADAM_B2) * (g * g)
    m_hat = nm / (1.0 - ADAM_B1 ** ADAM_STEP)
    v_hat = nv / (1.0 - ADAM_B2 ** ADAM_STEP)
    delta = -ADAM_LR * (m_hat / (jnp.sqrt(v_hat) + ADAM_EPS) + ADAM_WD * w)
    return delta, nm, nv


def _sum_adamw(parts, w, m, v, name):
    n_layers, R, C = w.shape
    assert len(parts) == n_layers
    rb = next((b for b in (256, 128) if R % b == 0), R)
    nb = R // rb

    def body(*refs):
        p_refs = refs[:n_layers]
        w_ref, m_ref, v_ref, g_ref, d_ref, nm_ref, nv_ref = refs[n_layers:]
        for k, p_ref in enumerate(p_refs):
            @pl.when(pl.program_id(0) == k)
            def _(p_ref=p_ref):
                g = p_ref[0].astype(F32)
                for s in range(1, N_DEV):
                    g = g + p_ref[s].astype(F32)
                d, nm, nv = _adamw_math(w_ref[...], g, m_ref[...], v_ref[...])
                g_ref[...] = g
                d_ref[...] = d
                nm_ref[...] = nm
                nv_ref[...] = nv

    def part_spec(k):
        return pl.BlockSpec((N_DEV, rb, C), lambda l, i: (0, jnp.where(l == k, i, jnp.where(l < k, 0, nb - 1)), 0))

    blk = pl.BlockSpec((None, rb, C), lambda l, i: (l, i, 0))
    return pl.pallas_call(
        body, name=name, grid=(n_layers, nb),
        in_specs=[part_spec(k) for k in range(n_layers)] + [blk, blk, blk],
        out_specs=[blk] * 4, out_shape=[_sds((n_layers, R, C))] * 4,
        compiler_params=_cp("arbitrary", "arbitrary"),
    )(*parts, w, m, v)


def _sum_adamw_rows(parts, w, m, v, name):
    R, _, C = w.shape

    def body(p_ref, w_ref, m_ref, v_ref, g_ref, d_ref, nm_ref, nv_ref):
        g = p_ref[0].astype(F32)
        for s in range(1, N_DEV):
            g = g + p_ref[s].astype(F32)
        d, nm, nv = _adamw_math(w_ref[:, 0, :], g, m_ref[:, 0, :], v_ref[:, 0, :])
        g_ref[:, 0, :] = g
        d_ref[:, 0, :] = d
        nm_ref[:, 0, :] = nm
        nv_ref[:, 0, :] = nv

    vm = pl.BlockSpec(memory_space=pltpu.VMEM)
    return pl.pallas_call(body, name=name, in_specs=[vm] * 4, out_specs=[vm] * 4, out_shape=[_sds((R, 1, C))] * 4,
                          compiler_params=pltpu.CompilerParams(vmem_limit_bytes=VMEM_LIMIT))(parts, w, m, v)


def _sum_pieces(parts, name):
    _, R, C = parts.shape

    def body(p_ref, g_ref):
        g = p_ref[0].astype(F32)
        for s in range(1, N_DEV):
            g = g + p_ref[s].astype(F32)
        g_ref[...] = g

    vm = pl.BlockSpec(memory_space=pltpu.VMEM)
    return pl.pallas_call(body, name=name, in_specs=[vm], out_specs=vm, out_shape=_sds((R, C)),
                          compiler_params=pltpu.CompilerParams(vmem_limit_bytes=VMEM_LIMIT))(parts)


def _adamw_many(ws, gs, ms, vs, name):
    n = len(ws)
    vm = pl.BlockSpec(memory_space=pltpu.VMEM)

    def body(*refs):
        w_refs, g_refs, m_refs, v_refs = refs[:n], refs[n:2 * n], refs[2 * n:3 * n], refs[3 * n:4 * n]
        d_refs, nm_refs, nv_refs = refs[4 * n:5 * n], refs[5 * n:6 * n], refs[6 * n:7 * n]
        for i in range(n):
            d, nm, nv = _adamw_math(w_refs[i][...], g_refs[i][...], m_refs[i][...], v_refs[i][...])
            d_refs[i][...] = d
            nm_refs[i][...] = nm
            nv_refs[i][...] = nv

    shapes = [_sds(w.shape) for w in ws]
    outs = pl.pallas_call(
        body, name=name, in_specs=[vm] * (4 * n), out_specs=[vm] * (3 * n), out_shape=shapes * 3,
        compiler_params=pltpu.CompilerParams(vmem_limit_bytes=VMEM_LIMIT),
    )(*ws, *gs, *ms, *vs)
    return list(outs[:n]), list(outs[n:2 * n]), list(outs[2 * n:])


def _mesh_pos():
    return lax.axis_index("x"), lax.axis_index("y"), lax.axis_index("c")


def _dev_index(p):
    return 4 * p[0] + 2 * p[1] + p[2]


HBM = pl.BlockSpec(memory_space=pltpu.HBM)
SEM = pl.BlockSpec(memory_space=pltpu.SEMAPHORE)
EFFECT = pltpu.SideEffectType.DATAFLOW_SIDE_EFFECTING


def _peer_list():
    x, y, c = _mesh_pos()
    peers = [(x ^ dx, y ^ dy, c ^ dc) for dx in range(2) for dy in range(2) for dc in range(2)][1:]
    return (x, y, c), peers


def _split_copy(src_ref, land_ref, send_sems, recv_sems, i, k, peer, slot, exchange):
    return pltpu.make_async_remote_copy(
        src_ref=src_ref.at[_dev_index(peer)] if exchange else src_ref, dst_ref=land_ref.at[slot],
        send_sem=send_sems.at[7 * i + k], recv_sem=recv_sems.at[7 * i + k], device_id=peer, device_id_type=MESH)


def _own_copy(src_ref, land_ref, own_sems, i, slot, exchange):
    return pltpu.make_async_copy(src_ref.at[slot] if exchange else src_ref, land_ref.at[slot], own_sems.at[i])


def _comm_start(groups, name, exchange, dep=None):
    sizes = [len(g) for g in groups]
    n = sum(sizes)
    srcs = [a for g in groups for a in g]
    per_group = [exchange] * len(groups) if isinstance(exchange, bool) else list(exchange)
    exchanged = [flag for flag, sz in zip(per_group, sizes) for _ in range(sz)]
    lands = [lax.empty(a.shape if ex else (N_DEV,) + a.shape, a.dtype) for a, ex in zip(srcs, exchanged)]

    n_dep = 0 if dep is None else 1

    def body(*refs):
        src_refs, land_refs = refs[:n], refs[n:2 * n]
        sem_refs = refs[2 * n + n_dep:2 * n + n_dep + 3 * len(sizes)]
        token_ref = refs[-1]
        me, peers = _peer_list()
        mi = _dev_index(me)
        i = 0
        for gi, sz in enumerate(sizes):
            for j in range(sz):
                for k, peer in enumerate(peers):
                    _split_copy(src_refs[i], land_refs[i], sem_refs[3 * gi], sem_refs[3 * gi + 1], j, k, peer, mi,
                                exchanged[i]).start()
                _own_copy(src_refs[i], land_refs[i], sem_refs[3 * gi + 2], j, mi, exchanged[i]).start(priority=1)
                i += 1
        token_ref[...] = jnp.zeros_like(token_ref)

    sem_shapes = []
    for sz in sizes:
        sem_shapes += [pltpu.SemaphoreType.DMA((7 * sz,)), pltpu.SemaphoreType.DMA((7 * sz,)),
                       pltpu.SemaphoreType.DMA((sz,))]
    thru = [pltpu.HBM(a.shape, a.dtype) for a in srcs + lands]
    n_sem = len(sem_shapes)
    outs = pl.pallas_call(
        body, name=name,
        out_shape=tuple(sem_shapes + thru + [_sds((8, LANES))]),
        in_specs=[HBM] * (2 * n) + [ANY] * n_dep,
        out_specs=tuple([SEM] * n_sem + [HBM] * (2 * n) + [pl.BlockSpec(memory_space=pltpu.VMEM)]),
        input_output_aliases={i: n_sem + i for i in range(2 * n)},
        compiler_params=pltpu.CompilerParams(has_side_effects=EFFECT),
    )(*[pltpu.with_memory_space_constraint(a, pltpu.HBM) for a in srcs + lands], *([] if dep is None else [dep]))
    sems, thru_src, thru_land, token = outs[:n_sem], outs[n_sem:n_sem + n], outs[n_sem + n:n_sem + 2 * n], outs[-1]
    result, off = [], 0
    for gi, sz in enumerate(sizes):
        result.append((*sems[3 * gi:3 * gi + 3], list(thru_src[off:off + sz]), list(thru_land[off:off + sz])))
        off += sz
    return result, token


def _comm_wait(group, after, name, exchange):
    send_sems, recv_sems, own_sems, srcs, lands = group
    n = len(srcs)
    after = list(after) if isinstance(after, (list, tuple)) else [after]

    def body(*refs):
        src_refs, land_refs = refs[:n], refs[n:2 * n]
        ssem, rsem, osem = refs[2 * n:2 * n + 3]
        me, peers = _peer_list()
        for i in range(n):
            for k, peer in enumerate(peers):
                cp = _split_copy(src_refs[i], land_refs[i], ssem, rsem, i, k, peer, _dev_index(peer), exchange)
                cp.wait_send()
                cp.wait_recv()
            _own_copy(src_refs[i], land_refs[i], osem, i, _dev_index(me), exchange).wait()

    outs = pl.pallas_call(
        body, name=name,
        out_shape=tuple(pltpu.HBM(a.shape, a.dtype) for a in srcs + lands),
        in_specs=[HBM] * (2 * n) + [SEM, SEM, SEM] + [ANY] * len(after),
        out_specs=tuple([HBM] * (2 * n)),
        input_output_aliases={i: i for i in range(2 * n)},
        compiler_params=pltpu.CompilerParams(has_side_effects=EFFECT),
    )(*srcs, *lands, send_sems, recv_sems, own_sems, *after)
    return list(outs[n:])


def _tie(a, token):
    return a + token[0, 0].astype(a.dtype)


def _pack(arrs, rows):
    flat = jnp.concatenate([a.reshape(-1).astype(F32) for a in arrs])
    return jnp.pad(flat, (0, rows * LANES - flat.shape[0])).reshape(rows, LANES)


def _unpack(packed, shapes):
    flat = packed.reshape(-1)
    out, off = [], 0
    for s in shapes:
        n = math.prod(s)
        out.append(flat[off:off + n].reshape(s))
        off += n
    return out


def _packed_rows(shapes):
    n = sum(math.prod(s) for s in shapes)
    unit = N_DEV * 8 * LANES
    return -(-n // unit) * unit // LANES


def kernel(x, mix_pre_g, mix_post_g, mlp_pre_g, mlp_post_g, w_in_even, s5_lam_re, s5_lam_im, s5_log_dt, s5_b_re, s5_b_im, s5_c_re, s5_c_im, s5_d, s5_w_glu, fox_b_f, w_out_even, w_in_odd, pool_w, pool_scale, sgu_ln_g, sgu_ln_b, sgu_w_s, sgu_b_s, w_out_odd, mlp_w1, mlp_w2, loss_target, m_mix_pre_g, m_mix_post_g, m_mlp_pre_g, m_mlp_post_g, m_w_in_even, m_s5_lam_re, m_s5_lam_im, m_s5_log_dt, m_s5_b_re, m_s5_b_im, m_s5_c_re, m_s5_c_im, m_s5_d, m_s5_w_glu, m_fox_b_f, m_w_out_even, m_w_in_odd, m_pool_w, m_pool_scale, m_sgu_ln_g, m_sgu_ln_b, m_sgu_w_s, m_sgu_b_s, m_w_out_odd, m_mlp_w1, m_mlp_w2, v_mix_pre_g, v_mix_post_g, v_mlp_pre_g, v_mlp_post_g, v_w_in_even, v_s5_lam_re, v_s5_lam_im, v_s5_log_dt, v_s5_b_re, v_s5_b_im, v_s5_c_re, v_s5_c_im, v_s5_d, v_s5_w_glu, v_fox_b_f, v_w_out_even, v_w_in_odd, v_pool_w, v_pool_scale, v_sgu_ln_g, v_sgu_ln_b, v_sgu_w_s, v_sgu_b_s, v_w_out_odd, v_mlp_w1, v_mlp_w2):
    weights = dict(mix_pre_g=mix_pre_g, mix_post_g=mix_post_g, mlp_pre_g=mlp_pre_g, mlp_post_g=mlp_post_g, w_in_even=w_in_even, s5_lam_re=s5_lam_re, s5_lam_im=s5_lam_im, s5_log_dt=s5_log_dt, s5_b_re=s5_b_re, s5_b_im=s5_b_im, s5_c_re=s5_c_re, s5_c_im=s5_c_im, s5_d=s5_d, s5_w_glu=s5_w_glu, fox_b_f=fox_b_f, w_out_even=w_out_even, w_in_odd=w_in_odd, pool_w=pool_w, pool_scale=pool_scale, sgu_ln_g=sgu_ln_g, sgu_ln_b=sgu_ln_b, sgu_w_s=sgu_w_s, sgu_b_s=sgu_b_s, w_out_odd=w_out_odd, mlp_w1=mlp_w1, mlp_w2=mlp_w2)
    mom_m = dict(mix_pre_g=m_mix_pre_g, mix_post_g=m_mix_post_g, mlp_pre_g=m_mlp_pre_g, mlp_post_g=m_mlp_post_g, w_in_even=m_w_in_even, s5_lam_re=m_s5_lam_re, s5_lam_im=m_s5_lam_im, s5_log_dt=m_s5_log_dt, s5_b_re=m_s5_b_re, s5_b_im=m_s5_b_im, s5_c_re=m_s5_c_re, s5_c_im=m_s5_c_im, s5_d=m_s5_d, s5_w_glu=m_s5_w_glu, fox_b_f=m_fox_b_f, w_out_even=m_w_out_even, w_in_odd=m_w_in_odd, pool_w=m_pool_w, pool_scale=m_pool_scale, sgu_ln_g=m_sgu_ln_g, sgu_ln_b=m_sgu_ln_b, sgu_w_s=m_sgu_w_s, sgu_b_s=m_sgu_b_s, w_out_odd=m_w_out_odd, mlp_w1=m_mlp_w1, mlp_w2=m_mlp_w2)
    mom_v = dict(mix_pre_g=v_mix_pre_g, mix_post_g=v_mix_post_g, mlp_pre_g=v_mlp_pre_g, mlp_post_g=v_mlp_post_g, w_in_even=v_w_in_even, s5_lam_re=v_s5_lam_re, s5_lam_im=v_s5_lam_im, s5_log_dt=v_s5_log_dt, s5_b_re=v_s5_b_re, s5_b_im=v_s5_b_im, s5_c_re=v_s5_c_re, s5_c_im=v_s5_c_im, s5_d=v_s5_d, s5_w_glu=v_s5_w_glu, fox_b_f=v_fox_b_f, w_out_even=v_w_out_even, w_in_odd=v_w_in_odd, pool_w=v_pool_w, pool_scale=v_pool_scale, sgu_ln_g=v_sgu_ln_g, sgu_ln_b=v_sgu_ln_b, sgu_w_s=v_sgu_w_s, sgu_b_s=v_sgu_b_s, w_out_odd=v_w_out_odd, mlp_w1=v_mlp_w1, mlp_w2=v_mlp_w2)
    names = list(weights)
    L = x.shape[1]
    x0 = x[0]
    target = loss_target[0]
    my_index = 4 * lax.axis_index("x") + 2 * lax.axis_index("y") + lax.axis_index("c")

    small_vec = jnp.zeros((8, LANES), F32)
    small_vec = small_vec.at[0, :64].set(pool_scale[0]).at[1, :64].set(sgu_ln_g[0]).at[2, :64].set(sgu_ln_b[0])
    ag_groups, ag_token = _comm_start(
        [[jnp.transpose(w_in_even[0]).astype(BF16), small_vec],
         [s5_w_glu[0].astype(BF16), w_out_even[0].astype(BF16)],
         [mlp_w1[0].astype(BF16), mlp_w2[0].astype(BF16)],
         [jnp.transpose(w_in_odd[0]).astype(BF16), w_out_odd[0].astype(BF16), mlp_w1[1].astype(BF16), mlp_w2[1].astype(BF16)]],
        "ag_start", exchange=False)

    lam_r = jnp.concatenate([s5_lam_re.reshape(1, S5_NS), s5_lam_im.reshape(1, S5_NS)], axis=0)
    ldt_r = jnp.repeat(s5_log_dt.reshape(32), 64).reshape(1, S5_NS)
    lam_c = jnp.transpose(lam_r)
    ldt_c = jnp.transpose(ldt_r)
    b_t = jnp.stack([jnp.tile(s5_b_re.reshape(S5_NS, 16), (1, 8)), jnp.tile(s5_b_im.reshape(S5_NS, 16), (1, 8))])
    c_t = jnp.stack([jnp.tile(s5_c_re.reshape(S5_W, 64), (1, 8)), jnp.tile(s5_c_im.reshape(S5_W, 64), (1, 8))])
    bf_pad = jnp.pad(fox_b_f, ((0, 0), (0, LANES - 8)))
    b_st = jnp.transpose(sgu_b_s[0])

    h0, rx0 = _rms_fwd(x0, _tie(mix_pre_g[0:1], ag_token), "rms0")
    tabs, bset, cset = _s5_prep(lam_r, ldt_r, lam_c, ldt_c, b_t, c_t, "s5_prep")
    ag0 = _comm_wait(ag_groups[0], tabs, "ag_wait0", exchange=False)
    winT_e = jnp.pad(ag0[0].reshape(EVEN_IN, D_MODEL), ((0, EVEN_PAD - EVEN_IN), (0, 0)))
    pool_scale_f = ag0[1][:, 0, :64].reshape(1, 512)
    ln_g_f = ag0[1][:, 1, :64].reshape(1, 512)
    ln_b_f = ag0[1][:, 2, :64].reshape(1, 512)
    z0 = _mm(h0, winT_e, name="win_even", tb=True, bm=512, bn=EVEN_PAD)
    xs, ylin = _s5_scan_fwd(z0, bset, cset, s5_d, tabs, "s5_scan")
    ag1 = _comm_wait(ag_groups[1], ylin, "ag_wait1", exchange=False)
    wglu = ag1[0].reshape(S5_W, S5_W)
    wout_e = ag1[1].reshape(D_MODEL, D_MODEL)
    ya = _s5_glu_fwd(ylin, wglu, "s5_glu")
    fcum, fq = _fox_f_fwd(z0, bf_pad, "fox_f")
    frow = jnp.transpose(fcum[:, :8]).reshape(4, 2, L)
    o_att, lse = _fox_fwd(z0, fq, frow, "fox_fwd")
    mix0 = [ya, o_att]
    x1, ry0, h1, rx1, y0 = _mm(mix0, wout_e, name="wout_even", epi=_epi_post_pre, extra=(x0,),
                               vecs=(mix_post_g[0:1], mlp_pre_g[0:1]), out_dtypes=POST_PRE_DTYPES,
                               out_kinds=POST_PRE_KINDS, bm=FUSED_ROWS)
    ag2 = _comm_wait(ag_groups[2], rx1, "ag_wait2", exchange=False)
    w1 = [ag2[0], None]
    w2 = [ag2[1].reshape(4 * D_MODEL, D_MODEL), None]
    p0, a0 = _mm(h1, w1[0], name="mlp0_w1", b3=True, out_dtypes=(BF16, BF16), epi=_epi_relu2, bm=512, bn=4 * D_MODEL)
    x2, ro0, h2, rx2, o0 = _mm(a0, w2[0], name="mlp0_w2", epi=_epi_post_pre, extra=(x1,),
                               vecs=(mlp_post_g[0:1], mix_pre_g[1:2]), out_dtypes=POST_PRE_DTYPES,
                               out_kinds=POST_PRE_KINDS, bm=FUSED_ROWS, bk=4 * D_MODEL)
    ag3 = _comm_wait(ag_groups[3], rx2, "ag_wait3", exchange=False)
    winT_o = ag3[0].reshape(ODD_IN, D_MODEL)
    wout_o = ag3[1].reshape(D_MODEL, D_MODEL)
    w1[1] = ag3[2]
    w2[1] = ag3[3].reshape(4 * D_MODEL, D_MODEL)
    z1 = _mm(h2, winT_o, name="win_odd", tb=True, bn=ODD_IN)
    yc, pooled = _pool_fwd(z1, pool_w[0], pool_scale_f, "pool_fwd")
    yd = _sgu_fwd(z1, ln_g_f, ln_b_f, sgu_w_s[0], b_st, "sgu_fwd")
    mix1 = [yc, yd]
    x3, ry1, h3, rx3, y1 = _mm(mix1, wout_o, name="wout_odd", epi=_epi_post_pre, extra=(x2,),
                               vecs=(mix_post_g[1:2], mlp_pre_g[1:2]), out_dtypes=POST_PRE_DTYPES,
                               out_kinds=POST_PRE_KINDS, bm=FUSED_ROWS)
    p1, a1 = _mm(h3, w1[1], name="mlp1_w1", b3=True, out_dtypes=(BF16, BF16), epi=_epi_relu2, bm=512, bn=4 * D_MODEL)
    gx4, g_o1, gg_mlp_post1, sq_lanes = _mm(
        a1, w2[1], name="mlp1_w2", epi=_epi_post_loss, extra=(x3, target), vecs=(mlp_post_g[1:2],),
        out_dtypes=(F32, BF16, F32, F32), out_kinds=("full", "full", "vsum", "vsum"), bm=FUSED_ROWS, bk=4 * D_MODEL)
    sq = sq_lanes[:, 0:1]

    g_p1 = _mm(g_o1, w2[1], name="b_mlp1_a", tb=True, out_dtypes=(BF16,), epi=_epi_relu2_bwd, extra=(p1,),
               bm=512, bn=4 * D_MODEL)
    gw2_1 = _mm(a1, g_o1, name="b_mlp1_w2", ta=True, bm=512, bk=L)
    gw1_1 = _mm(h3, g_p1, name="b_mlp1_w1", ta=True, out3=True, bn=512, bk=L)
    (ex1,), tok1 = _comm_start([[gw1_1, gw2_1.reshape(N_DEV, 512, D_MODEL)]], "ex_start1", exchange=True)
    g_x3, gg_mlp_pre1, g_y1, gg_mix_post1 = _mm(
        g_p1, w1[1], name="b_mlp1_h", tb=True, b3=True, epi=_epi_pre_post_bwd, extra=(x3, gx4, y1), cols=(rx3, ry1),
        vecs=(_tie(mlp_pre_g[1:2], tok1), mix_post_g[1:2]), out_dtypes=PRE_POST_BWD_DTYPES,
        out_kinds=PRE_POST_BWD_KINDS, bm=FUSED_ROWS, bk=4 * D_MODEL)
    gwout_o = _mm(mix1, g_y1, name="b_wout_odd_w", ta=True)
    g_xc, g_pool_w, g_pool_scale = _pool_bwd(g_y1, wout_o, pooled, pool_w[0], pool_scale_f, "pool_bwd")
    g_u1, g_v1, g_ws, g_bst, g_ln_g, g_ln_b = _sgu_bwd(g_y1, wout_o, z1, ln_g_f, ln_b_f, sgu_w_s[0], b_st,
                                                       "sgu_bwd")
    g_z1 = [g_xc, g_u1, g_v1]
    gwinT_o = _mm(g_z1, h2, name="b_win_odd_w", ta=True)
    (ex2,), tok2 = _comm_start([[gwout_o.reshape(N_DEV, 128, D_MODEL), gwinT_o.reshape(N_DEV, ODD_IN // N_DEV, D_MODEL)]], "ex_start2", exchange=True)
    g_x2, gg_mix_pre1, g_o0, gg_mlp_post0 = _mm(
        g_z1, winT_o, name="b_win_odd_h", epi=_epi_pre_post_bwd, extra=(x2, g_x3, o0), cols=(rx2, ro0),
        vecs=(_tie(mix_pre_g[1:2], tok2), mlp_post_g[0:1]), out_dtypes=PRE_POST_BWD_DTYPES,
        out_kinds=PRE_POST_BWD_KINDS, bm=FUSED_ROWS)
    g_p0 = _mm(g_o0, w2[0], name="b_mlp0_a", tb=True, out_dtypes=(BF16,), epi=_epi_relu2_bwd, extra=(p0,),
               bm=512, bn=4 * D_MODEL)
    gw2_0 = _mm(a0, g_o0, name="b_mlp0_w2", ta=True, bm=512, bk=L)
    gw1_0 = _mm(h1, g_p0, name="b_mlp0_w1", ta=True, out3=True, bn=512, bk=L)
    (ex3,), tok3 = _comm_start([[gw1_0, gw2_0.reshape(N_DEV, 512, D_MODEL)]], "ex_start3", exchange=True)
    g_x1, gg_mlp_pre0, g_y0, gg_mix_post0 = _mm(
        g_p0, w1[0], name="b_mlp0_h", tb=True, b3=True, epi=_epi_pre_post_bwd, extra=(x1, g_x2, y0), cols=(rx1, ry0),
        vecs=(_tie(mlp_pre_g[0:1], tok3), mix_post_g[0:1]), out_dtypes=PRE_POST_BWD_DTYPES,
        out_kinds=PRE_POST_BWD_KINDS, bm=FUSED_ROWS, bk=4 * D_MODEL)
    g_o_att = _mm(g_y0, wout_e[FOX_W:], name="b_wout_even_m", tb=True)
    gwout_e = _mm(mix0, g_y0, name="b_wout_even_w", ta=True)
    gyl, gud, g_wglu, g_d = _s5_glu_bwd(g_y0, wout_e, ylin, z0, s5_d, wglu, "s5_glu_bwd")
    (ex4,), tok4 = _comm_start([[gwout_e.reshape(N_DEV, 128, D_MODEL), g_wglu.reshape(N_DEV, 64, S5_W)]], "ex_start4", exchange=True)
    g_u0, ga, gb_raw, gc_raw = _s5_scan_bwd(gyl, _tie(cset, tok4), xs, z0, bset, gud, tabs, "s5_scan_bwd")
    g_lam, g_ldt, g_b, g_c = _s5_param_bwd(lam_c, ldt_c, b_t, gb_raw, jnp.transpose(ga), gc_raw, "s5_param_bwd")
    dq, dk, dv, dfq, dfrow = _fox_bwd(z0, frow, o_att, lse, g_o_att, "fox_bwd")
    dFk = jnp.pad(jnp.transpose(dfrow.reshape(8, L)), ((0, 0), (0, LANES - 8)))
    dfl, db_f = _fox_f_bwd(dFk, dfq, z0, bf_pad, "fox_f_bwd")
    g_z0 = [g_u0, dq, dk, dv, dfl]
    grad_x, gg_mix_pre0 = _mm(g_z0, winT_e, name="b_win_even_h", epi=_epi_pre_bwd, extra=(x0, g_x1), cols=(rx0,),
                              vecs=(mix_pre_g[0:1],), out_dtypes=(F32, F32), out_kinds=("full", "vsum"),
                              bm=FUSED_ROWS)

    small_grads = dict(
        mix_pre_g=jnp.concatenate([gg_mix_pre0, gg_mix_pre1]), mix_post_g=jnp.concatenate([gg_mix_post0, gg_mix_post1]),
        mlp_pre_g=jnp.concatenate([gg_mlp_pre0, gg_mlp_pre1]), mlp_post_g=jnp.concatenate([gg_mlp_post0, gg_mlp_post1]),
        s5_lam_re=g_lam[:, 0], s5_lam_im=g_lam[:, 1],
        s5_b_re=g_b[0, :, :16], s5_b_im=g_b[1, :, :16], s5_c_re=g_c[0, :, :64], s5_c_im=g_c[1, :, :64],
        pool_w=g_pool_w, sgu_w_s=g_ws, s5_d=g_d, sgu_b_s=jnp.transpose(g_bst),
        pool_scale=g_pool_scale, sgu_ln_g=g_ln_g, sgu_ln_b=g_ln_b, s5_log_dt=g_ldt, fox_b_f=db_f[:, :8])
    small_names = list(small_grads)
    full_shapes = [(512,) if nm in ("pool_scale", "sgu_ln_g", "sgu_ln_b") else weights[nm].shape for nm in small_names]
    full_shapes.append((1, 1))
    rows = _packed_rows(full_shapes)
    packed = _pack([small_grads[nm] for nm in small_names] + [sq], rows).reshape(N_DEV, rows // N_DEV, LANES)
    (exs,), tok_s = _comm_start([[packed]], "exs_start", exchange=True)
    gwinT_e = _mm(g_z0, h0, name="b_win_even_w", ta=True, out_dtypes=(BF16,), dep=tok_s)
    (recv_small,) = _comm_wait(exs, gwinT_e, "exs_wait", exchange=True)
    piece = _sum_pieces(recv_small, "sum_small")
    gwinT_e_pieces = gwinT_e[:EVEN_IN].reshape(N_DEV, EVEN_IN // N_DEV, D_MODEL)
    (ags, ex5), tok5 = _comm_start([[piece], [gwinT_e_pieces]], "ags_ex_start5", exchange=(False, True))
    r_w1_1, r_w2_1 = _comm_wait(ex1, tok5, "ex_wait1", exchange=True)
    r_wout_o, r_win_o = _comm_wait(ex2, tok5, "ex_wait2", exchange=True)
    r_w1_0, r_w2_0 = _comm_wait(ex3, tok5, "ex_wait3", exchange=True)
    r_wout_e, r_wglu = _comm_wait(ex4, tok5, "ex_wait4", exchange=True)

    res = {}
    big_parts = dict(mlp_w1=[r_w1_0, r_w1_1], mlp_w2=[r_w2_0, r_w2_1], s5_w_glu=[r_wglu], w_out_even=[r_wout_e],
                     w_out_odd=[r_wout_o])
    for nm, parts in big_parts.items():
        res[nm] = tuple(_sum_adamw(parts, weights[nm], mom_m[nm], mom_v[nm], "adamw_" + nm))
    done = [res[nm][1] for nm in ("mlp_w1", "mlp_w2", "s5_w_glu", "w_out_even", "w_out_odd")]

    (small_all,) = _comm_wait(ags, done, "ags_wait", exchange=False)
    small_full = _unpack(small_all.reshape(rows, LANES), full_shapes)
    loss = 0.5 * small_full.pop()[0, 0] / D_MODEL
    small_g = []
    for nm, g in zip(small_names, small_full):
        if nm in ("pool_scale", "sgu_ln_g", "sgu_ln_b"):
            g = lax.dynamic_slice(g, (my_index * 64,), (64,)).reshape(1, 64)
        small_g.append(g)

    def turned(arrs):
        return [jnp.swapaxes(a, -1, -2) if nm in ("s5_b_re", "s5_b_im") else a for nm, a in zip(small_names, arrs)]

    sd, sm, sv = _adamw_many(turned([weights[nm] for nm in small_names]), turned(small_g),
                             turned([mom_m[nm] for nm in small_names]), turned([mom_v[nm] for nm in small_names]),
                             "adamw_small")
    for nm, g_, d_, m_, v_ in zip(small_names, small_g, turned(sd), turned(sm), turned(sv)):
        res[nm] = (g_, d_, m_, v_)
    done.append(sd[0])

    nm = "w_in_odd"
    outs = _sum_adamw([r_win_o], jnp.transpose(weights[nm], (0, 2, 1)), jnp.transpose(mom_m[nm], (0, 2, 1)),
                      jnp.transpose(mom_v[nm], (0, 2, 1)), "adamw_" + nm)
    res[nm] = tuple(jnp.transpose(o, (0, 2, 1)) for o in outs)
    done.append(res[nm][1])
    nm = "w_in_even"
    (r_win_e,) = _comm_wait(ex5, done, "ex_wait5", exchange=True)
    outs = _sum_adamw_rows(r_win_e, jnp.transpose(weights[nm], (2, 0, 1)), jnp.transpose(mom_m[nm], (2, 0, 1)),
                           jnp.transpose(mom_v[nm], (2, 0, 1)), "adamw_" + nm)
    res[nm] = tuple(jnp.transpose(o, (1, 2, 0)) for o in outs)

    grads = [res[nm][0].reshape(weights[nm].shape) for nm in names]
    deltas = [res[nm][1].reshape(weights[nm].shape) for nm in names]
    new_m = [res[nm][2].reshape(weights[nm].shape) for nm in names]
    new_v = [res[nm][3].reshape(weights[nm].shape) for nm in names]
    return (loss, grad_x[None], *grads, *deltas, *new_m, *new_v)
```

```python
import math

import jax
import jax.numpy as jnp
from jax import lax
from jax.experimental import pallas as pl
from jax.experimental.pallas import tpu as pltpu

F32 = jnp.float32
BF16 = jnp.bfloat16
MESH = pl.DeviceIdType.MESH
ANY = pl.BlockSpec(memory_space=pl.ANY)

N_DEV = 8
D_MODEL = 1024
EPS = 1e-6
NORM_ROWS = 512
FUSED_ROWS = 512
S5_W = 512
S5_NS = 2048
SCAN_GROUPS = 4
SCAN_CHUNK = 1024
FOX_W = 512
EVEN_IN = 2056
EVEN_PAD = 2176
ODD_IN = 1536
LANES = 128
PIECE = 4 * D_MODEL // N_DEV
VMEM_LIMIT = 56 * 1024 * 1024

ADAM_LR = 0.001
ADAM_B1 = 0.9
ADAM_B2 = 0.999
ADAM_EPS = 1e-08
ADAM_WD = 0.01
ADAM_STEP = 10

NT = (((1,), (1,)), ((), ()))
TN = (((0,), (0,)), ((), ()))
NN = (((1,), (0,)), ((), ()))


def _cp(*sem):
    return pltpu.CompilerParams(dimension_semantics=sem, vmem_limit_bytes=VMEM_LIMIT)


def _sds(shape, dtype=F32):
    return jax.ShapeDtypeStruct(tuple(shape), dtype)


def _gelu(x):
    t = jnp.tanh(0.7978845608028654 * (x + 0.044715 * x * x * x))
    return 0.5 * x * (1.0 + t)


def _gelu_grad(x):
    t = jnp.tanh(0.7978845608028654 * (x + 0.044715 * x * x * x))
    du = 0.7978845608028654 * (1.0 + 3.0 * 0.044715 * x * x)
    return 0.5 * (1.0 + t) + 0.5 * x * (1.0 - t * t) * du


def _sigmoid(x):
    return 1.0 / (1.0 + jnp.exp(-x))


def _dot(a, b, dn=NN):
    return lax.dot_general(a, b, dn, preferred_element_type=F32)


def _mm(a, b, *, name, ta=False, tb=False, b3=False, out3=False, out_dtypes=(F32,), epi=None, extra=(),
        cols=(), vecs=(), out_kinds=None, bm=1024, bn=1024, bk=1024, dep=None):
    a_list = list(a) if isinstance(a, (list, tuple)) else [a]
    widths = [p.shape[1] for p in a_list]
    offs = [sum(widths[:i]) for i in range(len(widths))]
    na = len(a_list)
    M = sum(widths) if ta else a_list[0].shape[0]
    K = a_list[0].shape[0] if ta else sum(widths)
    if na > 1:
        assert not b3 and not tb
        bm, bk = (M, bk) if ta else (bm, K)
    pw = b.shape[2] if b3 else PIECE
    if b3:
        N = b.shape[1] if tb else b.shape[0] * pw
        assert (b.shape[0] * pw if tb else b.shape[1]) == K
    else:
        N = b.shape[0] if tb else b.shape[1]
    bm, bn, bk = min(bm, M), min(bn, N), min(bk, K)
    assert M % bm == 0 and N % bn == 0 and K % bk == 0, (name, M, N, K, bm, bn, bk)
    assert not (b3 or out3) or ((bk if tb else bn) % pw == 0 and bn % PIECE == 0)
    nk = K // bk
    n_extra = len(extra) + len(cols) + len(vecs)
    n_out = len(out_dtypes)
    out_kinds = tuple(out_kinds) if out_kinds is not None else ("full",) * n_out
    dn = (((0 if ta else 1,), (1 if tb else 0,)), ((), ()))

    use_acc = nk > 1

    def body(*refs):
        a_refs, b_ref = refs[:na], refs[na]
        a_ref = a_refs[0]
        e_refs = refs[na + 1:na + 1 + n_extra]
        first_out = na + 1 + n_extra + (0 if dep is None else 1)
        o_refs = refs[first_out:first_out + n_out]
        acc_ref = refs[-1] if use_acc else o_refs[0]
        i, k = pl.program_id(0), pl.program_id(2)

        def dot(a_v, b_v):
            return lax.dot_general(a_v.astype(BF16), b_v.astype(BF16), dn, preferred_element_type=F32)

        everything = slice(None)
        if na > 1 and ta:
            terms = [(pl.ds(off, w), everything, r, b_ref) for r, off, w in zip(a_refs, offs, widths)]
        elif na > 1:
            terms = [(everything, everything, r, b_ref.at[pl.ds(off, w), :]) for r, off, w in zip(a_refs, offs, widths)]
        elif not b3:
            terms = [(everything, everything, a_ref, b_ref)]
        elif tb:
            terms = [(everything, everything,
                      a_ref.at[pl.ds(t * pw, pw), :] if ta else a_ref.at[:, pl.ds(t * pw, pw)], b_ref.at[t])
                     for t in range(bk // pw)]
        else:
            terms = [(everything, pl.ds(t * pw, pw), a_ref, b_ref.at[t]) for t in range(bn // pw)]

        def finish(acc):
            outs = (acc,) if epi is None else epi(acc, *[e[...] for e in e_refs])
            for o_ref, o, kind in zip(o_refs, outs, out_kinds):
                if kind == "vsum":
                    @pl.when(i == 0)
                    def _(o_ref=o_ref, o=o):
                        o_ref[...] = o

                    @pl.when(i > 0)
                    def _(o_ref=o_ref, o=o):
                        o_ref[...] += o
                elif out3:
                    for t in range(bn // PIECE):
                        o_ref[t] = o[:, t * PIECE:(t + 1) * PIECE].astype(o_ref.dtype)
                else:
                    o_ref[...] = o.astype(o_ref.dtype)

        if nk == 1:
            bands = {}
            for rows, cols, a_r, b_r in terms:
                key = (getattr(rows, "start", None), getattr(cols, "start", None))
                val = dot(a_r[...], b_r[...])
                bands[key] = val if key not in bands else bands[key] + val
            vals = list(bands.values())
            if len(vals) == 1:
                finish(vals[0])
            else:
                finish(jnp.concatenate(vals, axis=0 if (na > 1 and ta) else 1))
            return

        @pl.when(k == 0)
        def _():
            acc_ref[...] = jnp.zeros_like(acc_ref)

        for rows, cols, a_r, b_r in terms:
            acc_ref[rows, cols] += dot(a_r[...], b_r[...])

        @pl.when(k == nk - 1)
        def _():
            finish(acc_ref[...])

    if na > 1:
        a_specs = [pl.BlockSpec((bk, w), lambda i, j, k: (k, 0)) if ta else pl.BlockSpec((bm, w), lambda i, j, k: (i, 0))
                   for w in widths]
    else:
        a_specs = [pl.BlockSpec((bk, bm), lambda i, j, k: (k, i)) if ta else
                   pl.BlockSpec((bm, bk), lambda i, j, k: (i, k))]
    if b3:
        if tb:
            b_spec = pl.BlockSpec((bk // pw, bn, pw), lambda i, j, k: (k, j, 0))
        else:
            b_spec = pl.BlockSpec((bn // pw, bk, pw), lambda i, j, k: (j, k, 0))
    else:
        b_spec = pl.BlockSpec((bn, bk), lambda i, j, k: (j, k)) if tb else pl.BlockSpec((bk, bn), lambda i, j, k: (k, j))
    e_specs = ([pl.BlockSpec((bm, bn), lambda i, j, k: (i, j)) for _ in extra]
               + [pl.BlockSpec((bm, 1), lambda i, j, k: (i, 0)) for _ in cols]
               + [pl.BlockSpec((1, bn), lambda i, j, k: (0, j)) for _ in vecs])
    if out3:
        o_specs = [pl.BlockSpec((bn // PIECE, bm, PIECE), lambda i, j, k: (j, i, 0)) for _ in out_dtypes]
        o_shapes = [_sds((N // PIECE, M, PIECE), dt) for dt in out_dtypes]
    else:
        spec_of = {"full": pl.BlockSpec((bm, bn), lambda i, j, k: (i, j)),
                   "col": pl.BlockSpec((bm, 1), lambda i, j, k: (i, 0)),
                   "vsum": pl.BlockSpec((1, bn), lambda i, j, k: (0, j))}
        shape_of = {"full": (M, N), "col": (M, 1), "vsum": (1, N)}
        o_specs = [spec_of[kind] for kind in out_kinds]
        o_shapes = [_sds(shape_of[kind], dt) for kind, dt in zip(out_kinds, out_dtypes)]
    assert "col" not in out_kinds or bn == N
    outs = pl.pallas_call(
        body, name=name, grid=(M // bm, N // bn, nk),
        in_specs=a_specs + [b_spec] + e_specs + ([] if dep is None else [ANY]),
        out_specs=o_specs, out_shape=o_shapes,
        scratch_shapes=[pltpu.VMEM((bm, bn), F32)] if use_acc else [],
        compiler_params=_cp("arbitrary" if "vsum" in out_kinds else "parallel", "parallel", "arbitrary"),
    )(*a_list, b, *extra, *cols, *vecs, *([] if dep is None else [dep]))
    return outs[0] if n_out == 1 else outs


def _epi_relu2(acc):
    r = jnp.maximum(acc, 0.0)
    return acc, r * r


def _epi_relu2_bwd(acc, p):
    return (acc * (2.0 * jnp.maximum(p.astype(F32), 0.0)),)


def _row_spec(rb, w=D_MODEL):
    return pl.BlockSpec((rb, w), lambda i: (i, 0))


def _vec_spec(w=D_MODEL):
    return pl.BlockSpec((1, w), lambda i: (0, 0))


def _rstd(v):
    return lax.rsqrt(jnp.mean(v * v, axis=-1, keepdims=True) + EPS)


def _rms_fwd(x, g, name):
    L = x.shape[0]
    rb = min(NORM_ROWS, L)

    def body(x_ref, g_ref, h_ref, r_ref):
        xv = x_ref[...]
        r = _rstd(xv)
        h_ref[...] = (xv * r * g_ref[...]).astype(BF16)
        r_ref[...] = r

    return pl.pallas_call(
        body, name=name, grid=(L // rb,),
        in_specs=[_row_spec(rb), _vec_spec()],
        out_specs=[_row_spec(rb), _row_spec(rb, 1)],
        out_shape=[_sds((L, D_MODEL), BF16), _sds((L, 1))],
        compiler_params=_cp("parallel"),
    )(x, g)


def _rms_bwd_rows(dy, xv, r, g):
    n = xv * r
    dyg = dy * g
    return r * (dyg - n * jnp.mean(dyg * n, axis=-1, keepdims=True)), n


POST_PRE_DTYPES = (F32, F32, BF16, F32, F32)
POST_PRE_KINDS = ("full", "col", "full", "col", "full")
PRE_POST_BWD_DTYPES = (F32, F32, BF16, F32)
PRE_POST_BWD_KINDS = ("full", "vsum", "full", "vsum")


def _epi_post_pre(y, x_in, g_post, g_pre):
    ry = _rstd(y)
    xo = x_in + y * ry * g_post
    rx = _rstd(xo)
    return xo, ry, xo * rx * g_pre, rx, y


def _epi_pre_post_bwd(gh, x, g_out, y_prev, rx, ry_prev, g_pre, g_post_prev):
    gx, n = _rms_bwd_rows(gh, x, rx, g_pre)
    gi = g_out + gx
    gy, ny = _rms_bwd_rows(gi, y_prev, ry_prev, g_post_prev)
    return gi, jnp.sum(gh * n, axis=0, keepdims=True), gy, jnp.sum(gi * ny, axis=0, keepdims=True)


def _epi_pre_bwd(gh, x, g_out, rx, g_pre):
    gx, n = _rms_bwd_rows(gh, x, rx, g_pre)
    return g_out + gx, jnp.sum(gh * n, axis=0, keepdims=True)


def _epi_post_loss(y, x_in, target, g_post):
    ry = _rstd(y)
    diff = x_in + y * ry * g_post - target
    gx = diff * (1.0 / D_MODEL)
    gy, n = _rms_bwd_rows(gx, y, ry, g_post)
    sq = jnp.broadcast_to(jnp.sum(diff * diff, keepdims=True), (1, y.shape[1]))
    return gx, gy, jnp.sum(gx * n, axis=0, keepdims=True), sq


def _cmul(ar, ai, br, bi):
    return ar * br - ai * bi, ar * bi + ai * br


def _zoh_cols(lr, li, ldt):
    dt = jnp.exp(ldt)
    mag = jnp.exp(lr * dt)
    ar = mag * jnp.cos(li * dt)
    ai = mag * jnp.sin(li * dt)
    den = lr * lr + li * li
    nr = ar - 1.0
    qr = (nr * lr + ai * li) / den
    qi = (ai * lr - nr * li) / den
    return dt, ar, ai, qr, qi, den


def _b_mask():
    r = lax.broadcasted_iota(jnp.int32, (S5_NS, LANES), 0)
    c = lax.broadcasted_iota(jnp.int32, (S5_NS, LANES), 1)
    return ((r >> 6) & 7) == (c >> 4)


def _c_mask():
    r = lax.broadcasted_iota(jnp.int32, (S5_W, 512), 0)
    c = lax.broadcasted_iota(jnp.int32, (S5_W, 512), 1)
    return ((r >> 4) & 7) == (c >> 6)


def _s5_prep(lam_r, ldt_r, lam_c, ldt_c, b_t, c_t, name):
    def body(lam_r_ref, ldt_r_ref, lam_c_ref, ldt_c_ref, b_ref, c_ref, tab_ref, bset_ref, cset_ref):
        lr, li = lam_r_ref[0:1, :], lam_r_ref[1:2, :]
        dt = jnp.exp(ldt_r_ref[...])
        mag = jnp.exp(lr * dt)
        p1r, p1i = mag * jnp.cos(li * dt), mag * jnp.sin(li * dt)
        p2r, p2i = _cmul(p1r, p1i, p1r, p1i)
        p3r, p3i = _cmul(p2r, p2i, p1r, p1i)
        p4r, p4i = _cmul(p2r, p2i, p2r, p2i)
        p5r, p5i = _cmul(p4r, p4i, p1r, p1i)
        p6r, p6i = _cmul(p4r, p4i, p2r, p2i)
        p7r, p7i = _cmul(p4r, p4i, p3r, p3i)
        p8r, p8i = _cmul(p4r, p4i, p4r, p4i)
        pw_r = [p1r, p2r, p3r, p4r, p5r, p6r, p7r, p8r]
        pw_i = [p1i, p2i, p3i, p4i, p5i, p6i, p7i, p8i]
        row = lax.broadcasted_iota(jnp.int32, (8, S5_NS), 0)
        zero = jnp.zeros((8, S5_NS), F32)

        def bc(v):
            return jnp.broadcast_to(v, (8, S5_NS))

        for d in range(2):
            sgn = 1.0 if d == 0 else -1.0
            for t, s in enumerate((1, 2, 4)):
                live = (row >= s) if d == 0 else (row <= 7 - s)
                tab_ref[d, 2 * t] = jnp.where(live, bc(pw_r[s - 1]), zero)
                tab_ref[d, 2 * t + 1] = jnp.where(live, bc(sgn * pw_i[s - 1]), zero)
            cr, ci = zero, zero
            for i in range(8):
                e = i if d == 0 else 7 - i
                cr = jnp.where(row == i, bc(pw_r[e]), cr)
                ci = jnp.where(row == i, bc(sgn * pw_i[e]), ci)
            tab_ref[d, 6] = cr
            tab_ref[d, 7] = ci

        _, _, _, qr, qi, _ = _zoh_cols(lam_c_ref[:, 0:1], lam_c_ref[:, 1:2], ldt_c_ref[...])
        bm = _b_mask()
        br, bi = b_ref[0], b_ref[1]
        bset_ref[0] = jnp.where(bm, qr * br - qi * bi, 0.0).astype(BF16)
        bset_ref[1] = jnp.where(bm, qr * bi + qi * br, 0.0).astype(BF16)
        cm = _c_mask()
        cset_ref[0] = jnp.where(cm, c_ref[0], 0.0).astype(BF16)
        cset_ref[1] = jnp.where(cm, c_ref[1], 0.0).astype(BF16)

    vm = pl.BlockSpec(memory_space=pltpu.VMEM)
    return pl.pallas_call(
        body, name=name, in_specs=[vm] * 6, out_specs=[vm] * 3,
        out_shape=[_sds((2, 8, 8, S5_NS)), _sds((2, S5_NS, LANES), BF16), _sds((2, S5_W, 512), BF16)],
        compiler_params=pltpu.CompilerParams(vmem_limit_bytes=VMEM_LIMIT),
    )(lam_r, ldt_r, lam_c, ldt_c, b_t, c_t)


SCAN_W = SCAN_GROUPS * LANES


def _scan_chunk(src_ref, dst_ref, tab_ref, carry_ref, nb, reverse, xs_ref=None, acc_ref=None):
    row = lax.broadcasted_iota(jnp.int32, (8, LANES), 0)

    def step(i, carry):
        b = (nb - 1 - i) if reverse else i
        off = pl.multiple_of(b * 8, 8)
        out = []
        for g in range(SCAN_GROUPS):
            lanes = pl.ds(g * LANES, LANES)
            cr, ci = carry[2 * g], carry[2 * g + 1]
            yr = src_ref[0, pl.ds(off, 8), lanes]
            yi = src_ref[1, pl.ds(off, 8), lanes]
            for t, s in enumerate((1, 2, 4)):
                sh = (8 - s) if reverse else s
                sr = pltpu.roll(yr, sh, 0)
                si = pltpu.roll(yi, sh, 0)
                mr, mi = tab_ref[2 * t, :, lanes], tab_ref[2 * t + 1, :, lanes]
                yr, yi = yr + mr * sr - mi * si, yi + mr * si + mi * sr
            pr, pi = tab_ref[6, :, lanes], tab_ref[7, :, lanes]
            yr, yi = yr + pr * cr - pi * ci, yi + pr * ci + pi * cr
            dst_ref[0, pl.ds(off, 8), lanes] = yr
            dst_ref[1, pl.ds(off, 8), lanes] = yi
            if xs_ref is not None:
                nr = jnp.where(row == 7, cr, pltpu.roll(yr, 7, 0))
                ni = jnp.where(row == 7, ci, pltpu.roll(yi, 7, 0))
                xr = xs_ref[0, pl.ds(off, 8), lanes]
                xi = xs_ref[1, pl.ds(off, 8), lanes]
                acc_ref[0, :, lanes] += xr * nr + xi * ni
                acc_ref[1, :, lanes] += xr * ni - xi * nr
            last = 0 if reverse else 7
            out += [jnp.broadcast_to(yr[last:last + 1, :], (8, LANES)),
                    jnp.broadcast_to(yi[last:last + 1, :], (8, LANES))]
        return tuple(out)

    init = []
    for g in range(SCAN_GROUPS):
        init += [carry_ref[0, :, pl.ds(g * LANES, LANES)], carry_ref[1, :, pl.ds(g * LANES, LANES)]]
    fin = lax.fori_loop(0, nb, step, tuple(init))
    for g in range(SCAN_GROUPS):
        carry_ref[0, :, pl.ds(g * LANES, LANES)] = fin[2 * g]
        carry_ref[1, :, pl.ds(g * LANES, LANES)] = fin[2 * g + 1]


def _s5_scan_fwd(z, bset, cset, dvec, tabs, name):
    L = z.shape[0]
    tl = min(SCAN_CHUNK, L)
    nc = L // tl

    def body(u_ref, b_ref, c_ref, d_ref, tab_ref, x_ref, y_ref, carry_ref):
        @pl.when(pl.program_id(1) == 0)
        def _():
            carry_ref[...] = jnp.zeros_like(carry_ref)

        uf = u_ref[...]
        u = uf.astype(BF16)
        x_ref[0] = _dot(u, b_ref[0], NT)
        x_ref[1] = _dot(u, b_ref[1], NT)
        _scan_chunk(x_ref, x_ref, tab_ref, carry_ref, tl // 8, False)
        y_ref[...] = (_dot(x_ref[0].astype(BF16), c_ref[0], NT) - _dot(x_ref[1].astype(BF16), c_ref[1], NT)
                      + d_ref[...] * uf)

    col = pl.BlockSpec((tl, LANES), lambda j, c: (c, j))
    return pl.pallas_call(
        body, name=name, grid=(S5_NS // SCAN_W, nc),
        in_specs=[col, pl.BlockSpec((2, SCAN_W, LANES), lambda j, c: (0, j, 0)),
                  pl.BlockSpec((2, LANES, SCAN_W), lambda j, c: (0, j, 0)),
                  pl.BlockSpec((1, LANES), lambda j, c: (0, j)),
                  pl.BlockSpec((None, 8, 8, SCAN_W), lambda j, c: (0, 0, 0, j))],
        out_specs=[pl.BlockSpec((2, tl, SCAN_W), lambda j, c: (0, c, j)), col],
        out_shape=[_sds((2, L, S5_NS)), _sds((L, S5_W))],
        scratch_shapes=[pltpu.VMEM((2, 8, SCAN_W), F32)],
        compiler_params=_cp("parallel", "arbitrary"),
    )(z, bset, cset, dvec, tabs)


def _s5_scan_bwd(gyl, cset, xs, z, bset, gud, tabs, name):
    L = z.shape[0]
    tl = min(SCAN_CHUNK, L)
    nc = L // tl

    def body(g_ref, c_ref, xs_ref, u_ref, b_ref, gud_ref, tab_ref, gu_ref, ga_ref, gb_ref, gc_ref,
             gx_ref, carry_ref, acc_ref):
        c = pl.program_id(1)

        @pl.when(c == 0)
        def _():
            carry_ref[...] = jnp.zeros_like(carry_ref)
            acc_ref[...] = jnp.zeros_like(acc_ref)
            gb_ref[...] = jnp.zeros_like(gb_ref)
            gc_ref[...] = jnp.zeros_like(gc_ref)

        gy = g_ref[...].astype(BF16)
        gx_ref[0] = _dot(gy, c_ref[0])
        gx_ref[1] = -_dot(gy, c_ref[1])
        gc_ref[0] += _dot(gy, xs_ref[0].astype(BF16), TN)
        gc_ref[1] -= _dot(gy, xs_ref[1].astype(BF16), TN)
        _scan_chunk(gx_ref, gx_ref, tab_ref, carry_ref, tl // 8, True, xs_ref, acc_ref)
        gr = gx_ref[0].astype(BF16)
        gi = gx_ref[1].astype(BF16)
        gu_ref[...] = gud_ref[...] + _dot(gr, b_ref[0]) + _dot(gi, b_ref[1])
        u = u_ref[...].astype(BF16)
        gb_ref[0] += _dot(gr, u, TN)
        gb_ref[1] += _dot(gi, u, TN)

        @pl.when(c == nc - 1)
        def _():
            ga_ref[0:1, :] = jnp.sum(acc_ref[0], axis=0, keepdims=True)
            ga_ref[1:2, :] = jnp.sum(acc_ref[1], axis=0, keepdims=True)

    rev = lambda j, c: (nc - 1 - c, j)
    col = pl.BlockSpec((tl, LANES), rev)
    return pl.pallas_call(
        body, name=name, grid=(S5_NS // SCAN_W, nc),
        in_specs=[col, pl.BlockSpec((2, LANES, SCAN_W), lambda j, c: (0, j, 0)),
                  pl.BlockSpec((2, tl, SCAN_W), lambda j, c: (0, nc - 1 - c, j)), col,
                  pl.BlockSpec((2, SCAN_W, LANES), lambda j, c: (0, j, 0)), col,
                  pl.BlockSpec((None, 8, 8, SCAN_W), lambda j, c: (1, 0, 0, j))],
        out_specs=[col, pl.BlockSpec((2, SCAN_W), lambda j, c: (0, j)),
                   pl.BlockSpec((2, SCAN_W, LANES), lambda j, c: (0, j, 0)),
                   pl.BlockSpec((2, LANES, SCAN_W), lambda j, c: (0, j, 0))],
        out_shape=[_sds((L, S5_W)), _sds((2, S5_NS)), _sds((2, S5_NS, LANES)), _sds((2, S5_W, 512))],
        scratch_shapes=[pltpu.VMEM((2, tl, SCAN_W), F32), pltpu.VMEM((2, 8, SCAN_W), F32),
                        pltpu.VMEM((2, 8, SCAN_W), F32)],
        compiler_params=_cp("parallel", "arbitrary"),
    )(gyl, cset, xs, z, bset, gud, tabs)


def _s5_glu_fwd(ylin, wglu, name):
    L = ylin.shape[0]
    bl = min(1024, L)

    def body(ylin_ref, w_ref, ya_ref):
        yg = _gelu(ylin_ref[...])
        t = _dot(yg.astype(BF16), w_ref[...])
        ya_ref[...] = (yg * _sigmoid(t)).astype(BF16)

    return pl.pallas_call(
        body, name=name, grid=(L // bl,),
        in_specs=[pl.BlockSpec((bl, S5_W), lambda i: (i, 0)), pl.BlockSpec((S5_W, S5_W), lambda i: (0, 0))],
        out_specs=pl.BlockSpec((bl, S5_W), lambda i: (i, 0)),
        out_shape=_sds((L, S5_W), BF16),
        compiler_params=_cp("parallel"),
    )(ylin, wglu)


def _s5_glu_bwd(g_y, wout, ylin, z, dvec, wglu, name):
    L = z.shape[0]
    bl = min(256, L)

    def body(g_ref, wo_ref, ylin_ref, u_ref, d_ref, w_ref, gyl_ref, gud_ref, gw_ref, gd_ref):
        i = pl.program_id(0)
        ylin = ylin_ref[...]
        yg = _gelu(ylin)
        ygb = yg.astype(BF16)
        sg = _sigmoid(_dot(ygb, w_ref[...]))
        gya = _dot(g_ref[...], wo_ref[...], NT)
        gt = gya * yg * sg * (1.0 - sg)
        gtb = gt.astype(BF16)
        gyg = gya * sg + _dot(gtb, w_ref[...], NT)
        gyl = gyg * _gelu_grad(ylin)
        gyl_ref[...] = gyl
        gud_ref[...] = gyl * d_ref[...]

        @pl.when(i == 0)
        def _():
            gw_ref[...] = jnp.zeros_like(gw_ref)
            gd_ref[...] = jnp.zeros_like(gd_ref)

        gw_ref[...] += _dot(ygb, gtb, TN)
        gd_ref[...] += jnp.sum(gyl * u_ref[...], axis=0, keepdims=True)

    blk = pl.BlockSpec((bl, S5_W), lambda i: (i, 0))
    return pl.pallas_call(
        body, name=name, grid=(L // bl,),
        in_specs=[pl.BlockSpec((bl, D_MODEL), lambda i: (i, 0)), pl.BlockSpec((S5_W, D_MODEL), lambda i: (0, 0)),
                  blk, blk, pl.BlockSpec((1, S5_W), lambda i: (0, 0)), pl.BlockSpec((S5_W, S5_W), lambda i: (0, 0))],
        out_specs=[blk, blk, pl.BlockSpec((S5_W, S5_W), lambda i: (0, 0)), pl.BlockSpec((1, S5_W), lambda i: (0, 0))],
        out_shape=[_sds((L, S5_W)), _sds((L, S5_W)), _sds((S5_W, S5_W)), _sds((1, S5_W))],
        compiler_params=_cp("arbitrary"),
    )(g_y, wout, ylin, z, dvec, wglu)


def _s5_param_bwd(lam_c, ldt_c, b_t, gb, ga_c, gc, name):
    def body(lam_ref, ldt_ref, b_ref, gb_ref, ga_ref, gc_ref, glam_ref, gldt_ref, gbo_ref, gco_ref):
        lr, li = lam_ref[:, 0:1], lam_ref[:, 1:2]
        dt, ar, ai, qr, qi, den = _zoh_cols(lr, li, ldt_ref[...])
        bm = _b_mask()
        gbr = jnp.where(bm, gb_ref[0], 0.0)
        gbi = jnp.where(bm, gb_ref[1], 0.0)
        br, bi = b_ref[0], b_ref[1]
        obr = gbr * qr + gbi * qi
        obi = gbi * qr - gbr * qi
        gqr = jnp.sum(gbr * br + gbi * bi, axis=1, keepdims=True)
        gqi = jnp.sum(gbi * br - gbr * bi, axis=1, keepdims=True)
        for s in (64, 32, 16):
            obr = obr + pltpu.roll(obr, s, 1)
            obi = obi + pltpu.roll(obi, s, 1)
        gbo_ref[0] = obr
        gbo_ref[1] = obi
        gar = ga_ref[:, 0:1] + (gqr * lr - gqi * li) / den
        gai = ga_ref[:, 1:2] + (gqr * li + gqi * lr) / den
        qlr = (qr * lr + qi * li) / den
        qli = (qi * lr - qr * li) / den
        glr = -(gqr * qlr + gqi * qli)
        gli = -(gqi * qlr - gqr * qli)
        glr = glr + dt * (gar * ar + gai * ai)
        gli = gli + dt * (gai * ar - gar * ai)
        wr, wi = _cmul(lr, li, ar, ai)
        gldt = (gar * wr + gai * wi) * dt
        glam_ref[:, 0:1] = glr
        glam_ref[:, 1:2] = gli
        r = lax.broadcasted_iota(jnp.int32, (S5_NS, 32), 0)
        c = lax.broadcasted_iota(jnp.int32, (S5_NS, 32), 1)
        gldt_ref[...] = jnp.sum(jnp.where((r >> 6) == c, gldt, 0.0), axis=0, keepdims=True)
        cm = _c_mask()
        for k in range(2):
            oc = jnp.where(cm, gc_ref[k], 0.0)
            for s in (256, 128, 64):
                oc = oc + pltpu.roll(oc, s, 1)
            gco_ref[k] = oc[:, 0:LANES]

    vm = pl.BlockSpec(memory_space=pltpu.VMEM)
    return pl.pallas_call(
        body, name=name, in_specs=[vm] * 6, out_specs=[vm] * 4,
        out_shape=[_sds((S5_NS, 2)), _sds((1, 32)), _sds((2, S5_NS, LANES)), _sds((2, S5_W, LANES))],
        compiler_params=pltpu.CompilerParams(vmem_limit_bytes=VMEM_LIMIT),
    )(lam_c, ldt_c, b_t, gb, ga_c, gc)


FL_BLK = EVEN_PAD // LANES - 1
Q_BLK, K_BLK, V_BLK = 4, 8, 12
NEG = -1e30


def _log_sigmoid(v):
    return jnp.minimum(v, 0.0) - jnp.log(1.0 + jnp.exp(-jnp.abs(v)))


def _fox_f_fwd(z, bf, name):
    L = z.shape[0]

    def body(fl_ref, b_ref, f_ref, fq_ref):
        row = lax.broadcasted_iota(jnp.int32, (L, LANES), 0)
        cs = _cumsum_rows(_log_sigmoid(fl_ref[...] + b_ref[...]), True, row)
        f_ref[...] = cs
        expand = (lax.broadcasted_iota(jnp.int32, (LANES, FOX_W), 0)
                  == (lax.broadcasted_iota(jnp.int32, (LANES, FOX_W), 1) >> 6)).astype(F32)
        fq_ref[...] = lax.dot_general(cs, expand, NN, precision=lax.Precision.HIGHEST, preferred_element_type=F32)

    return pl.pallas_call(
        body, name=name, grid=(1,),
        in_specs=[pl.BlockSpec((L, LANES), lambda i: (0, FL_BLK)), pl.BlockSpec((1, LANES), lambda i: (0, 0))],
        out_specs=[pl.BlockSpec((L, LANES), lambda i: (0, 0)), pl.BlockSpec((L, FOX_W), lambda i: (0, 0))],
        out_shape=[_sds((L, LANES)), _sds((L, FOX_W))],
        compiler_params=_cp("arbitrary"),
    )(z, bf)


def _fox_f_bwd(dFk, dfq, z, bf, name):
    L = z.shape[0]

    def body(dfk_ref, dfq_ref, fl_ref, b_ref, dfl_ref, db_ref):
        sel = (lax.broadcasted_iota(jnp.int32, (FOX_W, LANES), 0)
               == 64 * lax.broadcasted_iota(jnp.int32, (FOX_W, LANES), 1)).astype(F32)
        dfq_h = lax.dot_general(dfq_ref[...], sel, NN, precision=lax.Precision.HIGHEST, preferred_element_type=F32)
        row = lax.broadcasted_iota(jnp.int32, (L, LANES), 0)
        cs = _cumsum_rows(dfk_ref[...] + dfq_h, False, row)
        dfl = cs * _sigmoid(-(fl_ref[...] + b_ref[...]))
        dfl_ref[...] = dfl
        db_ref[...] = jnp.sum(dfl, axis=0, keepdims=True)

    return pl.pallas_call(
        body, name=name, grid=(1,),
        in_specs=[pl.BlockSpec((L, LANES), lambda i: (0, 0)), pl.BlockSpec((L, FOX_W), lambda i: (0, 0)),
                  pl.BlockSpec((L, LANES), lambda i: (0, FL_BLK)), pl.BlockSpec((1, LANES), lambda i: (0, 0))],
        out_specs=[pl.BlockSpec((L, LANES), lambda i: (0, 0)), pl.BlockSpec((1, LANES), lambda i: (0, 0))],
        out_shape=[_sds((L, LANES)), _sds((1, LANES))],
        compiler_params=_cp("arbitrary"),
    )(dFk, dfq, z, bf)


def _head_mask(hh):
    lane = lax.broadcasted_iota(jnp.int32, (1, LANES), 1)
    return (lane >> 6) == hh


FOX_T = 512


def _fox_head(x, hh):
    return jnp.where(_head_mask(hh), x, 0.0).astype(BF16)


def _fox_scores(qh, k, fq_ref, fr_ref, hh, causal):
    if fq_ref is None:
        s = _dot(qh, k, NT) - fr_ref[hh:hh + 1, :]
    else:
        s = _dot(qh, k, NT) + (fq_ref[:, 64 * hh:64 * hh + 1] - fr_ref[hh:hh + 1, :])
    return s if causal is None else jnp.where(causal, s, NEG)


def _causal(T):
    return lax.broadcasted_iota(jnp.int32, (T, T), 1) <= lax.broadcasted_iota(jnp.int32, (T, T), 0)


def _fox_fwd(z, fq, frow, name):
    L = z.shape[0]
    T = min(FOX_T, L)
    nq = L // T

    def body(qt_ref, kt_ref, q_ref, k_ref, v_ref, fq_ref, fr_ref, o_ref, lse_ref, m_ref, l_ref, acc_ref):
        t = pl.program_id(1)
        qi, ki = qt_ref[t], kt_ref[t]

        @pl.when(ki == 0)
        def _():
            m_ref[...] = jnp.full_like(m_ref, NEG)
            l_ref[...] = jnp.zeros_like(l_ref)
            acc_ref[...] = jnp.zeros_like(acc_ref)

        def step(diagonal):
            q = q_ref[...] * 0.125
            k = k_ref[...].astype(BF16)
            v = v_ref[...].astype(BF16)
            causal = _causal(T) if diagonal else None
            s = jnp.concatenate([_fox_scores(_fox_head(q, hh), k, fq_ref, fr_ref, hh, causal) for hh in range(2)],
                                axis=0)
            m_old = m_ref[...]
            m_new = jnp.maximum(m_old, jnp.max(s, axis=1, keepdims=True))
            alpha = jnp.exp(m_old - m_new)
            p = jnp.exp(s - m_new)
            l_ref[...] = alpha * l_ref[...] + jnp.sum(p, axis=1, keepdims=True)
            m_ref[...] = m_new
            acc_ref[...] = alpha * acc_ref[...] + _dot(p.astype(BF16), v)

        @pl.when(ki < qi)
        def _():
            step(False)

        @pl.when(ki == qi)
        def _():
            step(True)
            h0 = _head_mask(0)
            l = l_ref[...]
            o_h = acc_ref[...] / l
            lse_h = m_ref[...] + jnp.log(l)
            o_ref[...] = jnp.where(h0, o_h[:T], o_h[T:])
            lse_ref[...] = jnp.where(h0, lse_h[:T], lse_h[T:]) - fq_ref[...]

    pairs = [(qi, ki) for qi in range(nq) for ki in range(qi + 1)]
    qt = jnp.asarray([p[0] for p in pairs], jnp.int32)
    kt = jnp.asarray([p[1] for p in pairs], jnp.int32)

    def qspec(base):
        return pl.BlockSpec((T, LANES), lambda j, t, qt, kt: (qt[t], base + j))

    def kspec(base):
        return pl.BlockSpec((T, LANES), lambda j, t, qt, kt: (kt[t], base + j))

    return pl.pallas_call(
        body, name=name,
        grid_spec=pltpu.PrefetchScalarGridSpec(
            num_scalar_prefetch=2, grid=(4, len(pairs)),
            in_specs=[qspec(Q_BLK), kspec(K_BLK), kspec(V_BLK), qspec(0),
                      pl.BlockSpec((None, 2, T), lambda j, t, qt, kt: (j, 0, kt[t]))],
            out_specs=[qspec(0), qspec(0)],
            scratch_shapes=[pltpu.VMEM((2 * T, 1), F32), pltpu.VMEM((2 * T, 1), F32),
                            pltpu.VMEM((2 * T, LANES), F32)]),
        out_shape=[_sds((L, FOX_W)), _sds((L, FOX_W))],
        compiler_params=_cp("parallel", "arbitrary"),
    )(qt, kt, z, z, z, fq, frow)


def _fox_bwd(z, frow, o, lse, g_m, name):
    L = z.shape[0]
    T = min(FOX_T, L)
    nq = L // T

    pairs = [(qi, ki) for ki in range(nq) for qi in range(ki, nq)]
    qt = jnp.asarray([p[0] for p in pairs], jnp.int32)
    kt = jnp.asarray([p[1] for p in pairs], jnp.int32)

    def body(qt_ref, kt_ref, q_ref, k_ref, v_ref, fr_ref, o_ref, lse_ref, do_ref,
             dq_ref, dk_ref, dv_ref, dfq_ref, dfk_ref, dk_acc, dv_acc, df_acc):
        t = pl.program_id(1)
        qi, ki = qt_ref[t], kt_ref[t]

        @pl.when(t == 0)
        def _():
            dq_ref[...] = jnp.zeros_like(dq_ref)
            dfq_ref[...] = jnp.zeros_like(dfq_ref)

        @pl.when(qi == ki)
        def _():
            dk_acc[...] = jnp.zeros_like(dk_acc)
            dv_acc[...] = jnp.zeros_like(dv_acc)
            df_acc[...] = jnp.zeros_like(df_acc)

        def step(diagonal):
            q = q_ref[...] * 0.125
            qb = q.astype(BF16)
            k = k_ref[...].astype(BF16)
            v = v_ref[...].astype(BF16)
            do = do_ref[...]
            dob = do.astype(BF16)
            do_o = dob.astype(F32) * o_ref[...]
            causal = _causal(T) if diagonal else None
            dvs, dks, dqs, rss = [], [], [], []
            for hh in range(2):
                s = _fox_scores(_fox_head(q, hh), k, None, fr_ref, hh, causal)
                p = jnp.exp(s - lse_ref[:, 64 * hh:64 * hh + 1])
                dp = _dot(_fox_head(do, hh), v, NT)
                delta = jnp.sum(jnp.where(_head_mask(hh), do_o, 0.0), axis=1, keepdims=True)
                ds = p * (dp - delta)
                dsb = ds.astype(BF16)
                dvs.append(_dot(p.astype(BF16), dob, TN))
                dks.append(_dot(dsb, qb, TN))
                dqs.append(_dot(dsb, k))
                rss.append(jnp.sum(ds, axis=1, keepdims=True))
                df_acc[hh:hh + 1, :] -= jnp.sum(ds, axis=0, keepdims=True)
            h0 = _head_mask(0)
            dv_acc[...] += jnp.where(h0, dvs[0], dvs[1])
            dk_acc[...] += jnp.where(h0, dks[0], dks[1])
            rows = pl.ds(pl.multiple_of(qi * T, T), T)
            dq_ref[rows, :] += jnp.where(h0, dqs[0], dqs[1])
            dfq_ref[rows, :] += jnp.where(h0, rss[0], rss[1])

        @pl.when(qi > ki)
        def _():
            step(False)

        @pl.when(qi == ki)
        def _():
            step(True)

        @pl.when(qi == nq - 1)
        def _():
            dk_ref[...] = dk_acc[...]
            dv_ref[...] = dv_acc[...]
            dfk_ref[...] = df_acc[...]

        @pl.when(t == len(pairs) - 1)
        def _():
            dq_ref[...] = dq_ref[...] * 0.125

    def qside(base):
        return pl.BlockSpec((T, LANES), lambda j, t, qt, kt: (qt[t], base + j))

    def kside(base):
        return pl.BlockSpec((T, LANES), lambda j, t, qt, kt: (kt[t], base + j))

    pair = pl.BlockSpec((L, LANES), lambda j, t, qt, kt: (0, j))
    frow_spec = pl.BlockSpec((None, 2, T), lambda j, t, qt, kt: (j, 0, kt[t]))
    return pl.pallas_call(
        body, name=name,
        grid_spec=pltpu.PrefetchScalarGridSpec(
            num_scalar_prefetch=2, grid=(4, len(pairs)),
            in_specs=[qside(Q_BLK), kside(K_BLK), kside(V_BLK), frow_spec, qside(0), qside(0), qside(0)],
            out_specs=[pair, kside(0), kside(0), pair, frow_spec],
            scratch_shapes=[pltpu.VMEM((T, LANES), F32), pltpu.VMEM((T, LANES), F32), pltpu.VMEM((2, T), F32)]),
        out_shape=[_sds((L, FOX_W)), _sds((L, FOX_W)), _sds((L, FOX_W)), _sds((L, FOX_W)), _sds((4, 2, L))],
        compiler_params=_cp("parallel", "arbitrary"),
    )(qt, kt, z, z, z, frow, o, lse, g_m)


def _shift_rows(v, s, down, row):
    n = v.shape[0]
    if down:
        return jnp.where(row >= s, pltpu.roll(v, s, 0), 0.0)
    return jnp.where(row < n - s, pltpu.roll(v, n - s, 0), 0.0)


def _cumsum_rows(v, down, row):
    s = 1
    while s < v.shape[0]:
        v = v + _shift_rows(v, s, down, row)
        s *= 2
    return v


def _window_sum(v, g, down, row):
    out = jnp.zeros_like(v)
    s = v
    for k in range(4):
        s = s + _shift_rows(s, 1 << k, down, row)
        out = jnp.where(g == k, s, out)
    return out


def _pool_inv_cnt(g, row):
    w = jnp.left_shift(2, g).astype(F32)
    return 1.0 / jnp.minimum(row.astype(F32) + 1.0, w)


def _pool_fwd(z, pool_w, scale, name):
    L = z.shape[0]

    def body(x_ref, w_ref, s_ref, y_ref, p_ref):
        g = pl.program_id(0)
        row = lax.broadcasted_iota(jnp.int32, (L, LANES), 0)
        x = x_ref[...]
        pooled = (_window_sum(x, g, True, row) * _pool_inv_cnt(g, row) - x).astype(BF16)
        p_ref[...] = pooled
        y_ref[...] = (_dot(pooled, w_ref[...].astype(BF16)) * s_ref[...]).astype(BF16)

    col = pl.BlockSpec((L, LANES), lambda g: (0, g))
    return pl.pallas_call(
        body, name=name, grid=(4,),
        in_specs=[col, pl.BlockSpec((None, LANES, LANES), lambda g: (g, 0, 0)), pl.BlockSpec((1, LANES), lambda g: (0, g))],
        out_specs=[col, col],
        out_shape=[_sds((L, 512), BF16), _sds((L, 512), BF16)],
        compiler_params=_cp("parallel"),
    )(z, pool_w, scale)


def _pool_bwd(g_y, wout, pooled, pool_w, scale, name):
    L = g_y.shape[0]

    def body(g_ref, wo_ref, p_ref, w_ref, s_ref, gx_ref, gw_ref, gs_ref):
        g = pl.program_id(0)
        row = lax.broadcasted_iota(jnp.int32, (L, LANES), 0)
        gy = _dot(g_ref[...], wo_ref[...], NT)
        pooled = p_ref[...]
        wb = w_ref[...].astype(BF16)
        lin = _dot(pooled, wb)
        gs_ref[...] = jnp.sum(gy * lin, axis=0, keepdims=True)
        glin = (gy * s_ref[...]).astype(BF16)
        gw_ref[...] = _dot(pooled, glin, TN)
        gp = _dot(glin, wb, NT)
        gx_ref[...] = _window_sum(gp * _pool_inv_cnt(g, row), g, False, row) - gp

    col = pl.BlockSpec((L, LANES), lambda g: (0, g))
    wspec = pl.BlockSpec((None, LANES, LANES), lambda g: (g, 0, 0))
    vec = pl.BlockSpec((1, LANES), lambda g: (0, g))
    return pl.pallas_call(
        body, name=name, grid=(4,),
        in_specs=[pl.BlockSpec((L, D_MODEL), lambda g: (0, 0)), pl.BlockSpec((LANES, D_MODEL), lambda g: (g, 0)),
                  col, wspec, vec],
        out_specs=[col, wspec, vec],
        out_shape=[_sds((L, 512)), _sds((4, LANES, LANES)), _sds((1, 512))],
        compiler_params=_cp("parallel"),
    )(g_y, wout, pooled, pool_w, scale)


SGU_CHUNKS = 4


def _sgu_ln(v, gam, bet):
    gv = _gelu(v)
    mu = jnp.mean(gv, axis=-1, keepdims=True)
    xc = gv - mu
    rs = lax.rsqrt(jnp.mean(xc * xc, axis=-1, keepdims=True) + EPS)
    xh = xc * rs
    return xh, rs, xh * gam + bet


def _tril_ws(w_ref, g):
    r = lax.broadcasted_iota(jnp.int32, (LANES, LANES), 0)
    c = lax.broadcasted_iota(jnp.int32, (LANES, LANES), 1)
    return jnp.where(r >= c, w_ref[g], 0.0).astype(BF16)


def _sgu_fwd(z, ln_g, ln_b, w_s, b_st, name):
    L = z.shape[0]
    rb = min(SGU_CHUNKS * LANES, L)

    def body(u_ref, v_ref, g_ref, b_ref, w_ref, bs_ref, y_ref):
        _, _, vln = _sgu_ln(v_ref[...], g_ref[...], b_ref[...])
        gu = _gelu(u_ref[...])
        vb = vln.astype(BF16)
        for g in range(4):
            ws = _tril_ws(w_ref, g)
            for n in range(rb // LANES):
                rows = slice(n * LANES, (n + 1) * LANES)
                cols = slice(g * LANES, (g + 1) * LANES)
                mixed = _dot(ws, vb[rows, cols]) + bs_ref[:, g:g + 1]
                y_ref[rows, cols] = (gu[rows, cols] * mixed).astype(BF16)

    vm = lambda shape: pl.BlockSpec(shape, lambda i: tuple(0 for _ in shape))
    return pl.pallas_call(
        body, name=name, grid=(L // rb,),
        in_specs=[pl.BlockSpec((rb, 512), lambda i: (i, 1)), pl.BlockSpec((rb, 512), lambda i: (i, 2)),
                  vm((1, 512)), vm((1, 512)), vm((4, LANES, LANES)), vm((LANES, 4))],
        out_specs=pl.BlockSpec((rb, 512), lambda i: (i, 0)),
        out_shape=_sds((L, 512), BF16),
        compiler_params=_cp("parallel"),
    )(z, z, ln_g, ln_b, w_s, b_st)


def _sgu_bwd(g_y, wout, z, ln_g, ln_b, w_s, b_st, name):
    L = z.shape[0]
    rb = min(SGU_CHUNKS * LANES, L)

    def body(gyo_ref, wo_ref, u_ref, v_ref, g_ref, b_ref, w_ref, bs_ref, gu_ref, gv_ref, gw_ref, gbs_ref, gg_ref,
             gb_ref):
        i = pl.program_id(0)

        @pl.when(i == 0)
        def _():
            gw_ref[...] = jnp.zeros_like(gw_ref)
            gbs_ref[...] = jnp.zeros_like(gbs_ref)
            gg_ref[...] = jnp.zeros_like(gg_ref)
            gb_ref[...] = jnp.zeros_like(gb_ref)

        v = v_ref[...]
        u = u_ref[...]
        gy = _dot(gyo_ref[...], wo_ref[...], NT)
        xh, rs, vln = _sgu_ln(v, g_ref[...], b_ref[...])
        gel_u = _gelu(u)
        gmix = gy * gel_u
        vb = vln.astype(BF16)
        gmb = gmix.astype(BF16)
        r = lax.broadcasted_iota(jnp.int32, (LANES, LANES), 0)
        c = lax.broadcasted_iota(jnp.int32, (LANES, LANES), 1)
        gvln_cols = []
        for g in range(4):
            ws = _tril_ws(w_ref, g)
            cols = slice(g * LANES, (g + 1) * LANES)
            gw = jnp.zeros((LANES, LANES), F32)
            gbs = jnp.zeros((LANES, 1), F32)
            parts = []
            for n in range(rb // LANES):
                rows = slice(n * LANES, (n + 1) * LANES)
                mixed = _dot(ws, vb[rows, cols]) + bs_ref[:, g:g + 1]
                gu_ref[rows, cols] = gy[rows, cols] * mixed * _gelu_grad(u[rows, cols])
                parts.append(_dot(ws, gmb[rows, cols], TN))
                gw = gw + _dot(gmb[rows, cols], vb[rows, cols], NT)
                gbs = gbs + jnp.sum(gmix[rows, cols], axis=1, keepdims=True)
            gvln_cols.append(jnp.concatenate(parts, axis=0))
            gw_ref[g] += jnp.where(r >= c, gw, 0.0)
            gbs_ref[:, g:g + 1] += gbs
        gvln = jnp.concatenate(gvln_cols, axis=1)
        gg_ref[...] += jnp.sum(gvln * xh, axis=0, keepdims=True)
        gb_ref[...] += jnp.sum(gvln, axis=0, keepdims=True)
        gxh = gvln * g_ref[...]
        ggv = rs * (gxh - jnp.mean(gxh, axis=-1, keepdims=True) - xh * jnp.mean(gxh * xh, axis=-1, keepdims=True))
        gv_ref[...] = ggv * _gelu_grad(v)

    vm = lambda shape: pl.BlockSpec(shape, lambda i: tuple(0 for _ in shape))
    blk = pl.BlockSpec((rb, 512), lambda i: (i, 0))
    return pl.pallas_call(
        body, name=name, grid=(L // rb,),
        in_specs=[pl.BlockSpec((rb, D_MODEL), lambda i: (i, 0)), pl.BlockSpec((512, D_MODEL), lambda i: (1, 0)),
                  pl.BlockSpec((rb, 512), lambda i: (i, 1)), pl.BlockSpec((rb, 512), lambda i: (i, 2)),
                  vm((1, 512)), vm((1, 512)), vm((4, LANES, LANES)), vm((LANES, 4))],
        out_specs=[blk, blk, vm((4, LANES, LANES)), vm((LANES, 4)), vm((1, 512)), vm((1, 512))],
        out_shape=[_sds((L, 512)), _sds((L, 512)), _sds((4, LANES, LANES)), _sds((LANES, 4)),
                   _sds((1, 512)), _sds((1, 512))],
        compiler_params=_cp("arbitrary"),
    )(g_y, wout, z, z, ln_g, ln_b, w_s, b_st)


def _adamw_math(w, g, m, v):
    nm = ADAM_B1 * m + (1.0 - ADAM_B1) * g
    nv = ADAM_B2 * v + (1.0 - ADAM_B2) * (g * g)
    m_hat = nm / (1.0 - ADAM_B1 ** ADAM_STEP)
    v_hat = nv / (1.0 - ADAM_B2 ** ADAM_STEP)
    delta = -ADAM_LR * (m_hat / (jnp.sqrt(v_hat) + ADAM_EPS) + ADAM_WD * w)
    return delta, nm, nv


def _sum_adamw(parts, w, m, v, name):
    n_layers, R, C = w.shape
    assert len(parts) == n_layers
    rb = next((b for b in (256, 128) if R % b == 0), R)
    nb = R // rb

    def body(*refs):
        p_refs = refs[:n_layers]
        w_ref, m_ref, v_ref, g_ref, d_ref, nm_ref, nv_ref = refs[n_layers:]
        for k, p_ref in enumerate(p_refs):
            @pl.when(pl.program_id(0) == k)
            def _(p_ref=p_ref):
                g = p_ref[0].astype(F32)
                for s in range(1, N_DEV):
                    g = g + p_ref[s].astype(F32)
                d, nm, nv = _adamw_math(w_ref[...], g, m_ref[...], v_ref[...])
                g_ref[...] = g
                d_ref[...] = d
                nm_ref[...] = nm
                nv_ref[...] = nv

    def part_spec(k):
        return pl.BlockSpec((N_DEV, rb, C), lambda l, i: (0, jnp.where(l == k, i, jnp.where(l < k, 0, nb - 1)), 0))

    blk = pl.BlockSpec((None, rb, C), lambda l, i: (l, i, 0))
    return pl.pallas_call(
        body, name=name, grid=(n_layers, nb),
        in_specs=[part_spec(k) for k in range(n_layers)] + [blk, blk, blk],
        out_specs=[blk] * 4, out_shape=[_sds((n_layers, R, C))] * 4,
        compiler_params=_cp("arbitrary", "arbitrary"),
    )(*parts, w, m, v)


def _sum_adamw_rows(parts, w, m, v, name):
    R, _, C = w.shape

    def body(p_ref, w_ref, m_ref, v_ref, g_ref, d_ref, nm_ref, nv_ref):
        g = p_ref[0].astype(F32)
        for s in range(1, N_DEV):
            g = g + p_ref[s].astype(F32)
        d, nm, nv = _adamw_math(w_ref[:, 0, :], g, m_ref[:, 0, :], v_ref[:, 0, :])
        g_ref[:, 0, :] = g
        d_ref[:, 0, :] = d
        nm_ref[:, 0, :] = nm
        nv_ref[:, 0, :] = nv

    vm = pl.BlockSpec(memory_space=pltpu.VMEM)
    return pl.pallas_call(body, name=name, in_specs=[vm] * 4, out_specs=[vm] * 4, out_shape=[_sds((R, 1, C))] * 4,
                          compiler_params=pltpu.CompilerParams(vmem_limit_bytes=VMEM_LIMIT))(parts, w, m, v)


def _sum_pieces(parts, name):
    _, R, C = parts.shape

    def body(p_ref, g_ref):
        g = p_ref[0].astype(F32)
        for s in range(1, N_DEV):
            g = g + p_ref[s].astype(F32)
        g_ref[...] = g

    vm = pl.BlockSpec(memory_space=pltpu.VMEM)
    return pl.pallas_call(body, name=name, in_specs=[vm], out_specs=vm, out_shape=_sds((R, C)),
                          compiler_params=pltpu.CompilerParams(vmem_limit_bytes=VMEM_LIMIT))(parts)


def _sum_adamw_many(parts, ws, ms, vs, name):
    n = len(ws)
    vm = pl.BlockSpec(memory_space=pltpu.VMEM)

    def body(*refs):
        p_refs, w_refs, m_refs, v_refs = refs[:n], refs[n:2 * n], refs[2 * n:3 * n], refs[3 * n:4 * n]
        o_refs = refs[4 * n:]
        for i in range(n):
            g = p_refs[i][0].astype(F32)
            for s in range(1, N_DEV):
                g = g + p_refs[i][s].astype(F32)
            d, nm, nv = _adamw_math(w_refs[i][0], g, m_refs[i][0], v_refs[i][0])
            for o_ref, o in zip(o_refs[4 * i:4 * i + 4], (g, d, nm, nv)):
                o_ref[0] = o

    shapes = [_sds(w.shape) for w in ws for _ in range(4)]
    outs = pl.pallas_call(
        body, name=name, in_specs=[vm] * (4 * n), out_specs=[vm] * (4 * n), out_shape=shapes,
        compiler_params=pltpu.CompilerParams(vmem_limit_bytes=VMEM_LIMIT),
    )(*parts, *ws, *ms, *vs)
    return [tuple(outs[4 * i:4 * i + 4]) for i in range(n)]


def _adamw_many(ws, gs, ms, vs, name):
    n = len(ws)
    vm = pl.BlockSpec(memory_space=pltpu.VMEM)

    def body(*refs):
        w_refs, g_refs, m_refs, v_refs = refs[:n], refs[n:2 * n], refs[2 * n:3 * n], refs[3 * n:4 * n]
        d_refs, nm_refs, nv_refs = refs[4 * n:5 * n], refs[5 * n:6 * n], refs[6 * n:7 * n]
        for i in range(n):
            d, nm, nv = _adamw_math(w_refs[i][...], g_refs[i][...], m_refs[i][...], v_refs[i][...])
            d_refs[i][...] = d
            nm_refs[i][...] = nm
            nv_refs[i][...] = nv

    shapes = [_sds(w.shape) for w in ws]
    outs = pl.pallas_call(
        body, name=name, in_specs=[vm] * (4 * n), out_specs=[vm] * (3 * n), out_shape=shapes * 3,
        compiler_params=pltpu.CompilerParams(vmem_limit_bytes=VMEM_LIMIT),
    )(*ws, *gs, *ms, *vs)
    return list(outs[:n]), list(outs[n:2 * n]), list(outs[2 * n:])


def _mesh_pos():
    return lax.axis_index("x"), lax.axis_index("y"), lax.axis_index("c")


def _dev_index(p):
    return 4 * p[0] + 2 * p[1] + p[2]


HBM = pl.BlockSpec(memory_space=pltpu.HBM)
SEM = pl.BlockSpec(memory_space=pltpu.SEMAPHORE)
EFFECT = pltpu.SideEffectType.DATAFLOW_SIDE_EFFECTING


def _peer_list():
    x, y, c = _mesh_pos()
    peers = [(x ^ dx, y ^ dy, c ^ dc) for dx in range(2) for dy in range(2) for dc in range(2)][1:]
    return (x, y, c), peers


def _split_copy(src_ref, land_ref, send_sems, recv_sems, i, k, peer, slot, exchange):
    return pltpu.make_async_remote_copy(
        src_ref=src_ref.at[_dev_index(peer)] if exchange else src_ref, dst_ref=land_ref.at[slot],
        send_sem=send_sems.at[7 * i + k], recv_sem=recv_sems.at[7 * i + k], device_id=peer, device_id_type=MESH)


def _own_copy(src_ref, land_ref, own_sems, i, slot, exchange):
    return pltpu.make_async_copy(src_ref.at[slot] if exchange else src_ref, land_ref.at[slot], own_sems.at[i])


def _comm_start(groups, name, exchange, dep=None):
    sizes = [len(g) for g in groups]
    n = sum(sizes)
    srcs = [a for g in groups for a in g]
    per_group = [exchange] * len(groups) if isinstance(exchange, bool) else list(exchange)
    exchanged = [flag for flag, sz in zip(per_group, sizes) for _ in range(sz)]
    lands = [lax.empty(a.shape if ex else (N_DEV,) + a.shape, a.dtype) for a, ex in zip(srcs, exchanged)]

    n_dep = 0 if dep is None else 1

    def body(*refs):
        src_refs, land_refs = refs[:n], refs[n:2 * n]
        sem_refs = refs[2 * n + n_dep:2 * n + n_dep + 3 * len(sizes)]
        token_ref = refs[-1]
        me, peers = _peer_list()
        mi = _dev_index(me)
        i = 0
        for gi, sz in enumerate(sizes):
            for j in range(sz):
                for k, peer in enumerate(peers):
                    _split_copy(src_refs[i], land_refs[i], sem_refs[3 * gi], sem_refs[3 * gi + 1], j, k, peer, mi,
                                exchanged[i]).start()
                _own_copy(src_refs[i], land_refs[i], sem_refs[3 * gi + 2], j, mi, exchanged[i]).start()
                i += 1
        token_ref[...] = jnp.zeros_like(token_ref)

    sem_shapes = []
    for sz in sizes:
        sem_shapes += [pltpu.SemaphoreType.DMA((7 * sz,)), pltpu.SemaphoreType.DMA((7 * sz,)),
                       pltpu.SemaphoreType.DMA((sz,))]
    thru = [pltpu.HBM(a.shape, a.dtype) for a in srcs + lands]
    n_sem = len(sem_shapes)
    outs = pl.pallas_call(
        body, name=name,
        out_shape=tuple(sem_shapes + thru + [_sds((8, LANES))]),
        in_specs=[HBM] * (2 * n) + [ANY] * n_dep,
        out_specs=tuple([SEM] * n_sem + [HBM] * (2 * n) + [pl.BlockSpec(memory_space=pltpu.VMEM)]),
        input_output_aliases={i: n_sem + i for i in range(2 * n)},
        compiler_params=pltpu.CompilerParams(has_side_effects=EFFECT),
    )(*[pltpu.with_memory_space_constraint(a, pltpu.HBM) for a in srcs + lands], *([] if dep is None else [dep]))
    sems, thru_src, thru_land, token = outs[:n_sem], outs[n_sem:n_sem + n], outs[n_sem + n:n_sem + 2 * n], outs[-1]
    result, off = [], 0
    for gi, sz in enumerate(sizes):
        result.append((*sems[3 * gi:3 * gi + 3], list(thru_src[off:off + sz]), list(thru_land[off:off + sz])))
        off += sz
    return result, token


def _comm_wait(group, after, name, exchange):
    send_sems, recv_sems, own_sems, srcs, lands = group
    n = len(srcs)
    after = list(after) if isinstance(after, (list, tuple)) else [after]

    def body(*refs):
        src_refs, land_refs = refs[:n], refs[n:2 * n]
        ssem, rsem, osem = refs[2 * n:2 * n + 3]
        me, peers = _peer_list()
        for i in range(n):
            for k, peer in enumerate(peers):
                cp = _split_copy(src_refs[i], land_refs[i], ssem, rsem, i, k, peer, _dev_index(peer), exchange)
                cp.wait_send()
                cp.wait_recv()
            _own_copy(src_refs[i], land_refs[i], osem, i, _dev_index(me), exchange).wait()

    outs = pl.pallas_call(
        body, name=name,
        out_shape=tuple(pltpu.HBM(a.shape, a.dtype) for a in srcs + lands),
        in_specs=[HBM] * (2 * n) + [SEM, SEM, SEM] + [ANY] * len(after),
        out_specs=tuple([HBM] * (2 * n)),
        input_output_aliases={i: i for i in range(2 * n)},
        compiler_params=pltpu.CompilerParams(has_side_effects=EFFECT),
    )(*srcs, *lands, send_sems, recv_sems, own_sems, *after)
    return list(outs[n:])


def _tie(a, token):
    return a + token[0, 0].astype(a.dtype)


def _pack(arrs, rows):
    flat = jnp.concatenate([a.reshape(-1).astype(F32) for a in arrs])
    return jnp.pad(flat, (0, rows * LANES - flat.shape[0])).reshape(rows, LANES)


def _unpack(packed, shapes):
    flat = packed.reshape(-1)
    out, off = [], 0
    for s in shapes:
        n = math.prod(s)
        out.append(flat[off:off + n].reshape(s))
        off += n
    return out


def _packed_rows(shapes):
    n = sum(math.prod(s) for s in shapes)
    unit = N_DEV * 8 * LANES
    return -(-n // unit) * unit // LANES


def kernel(x, mix_pre_g, mix_post_g, mlp_pre_g, mlp_post_g, w_in_even, s5_lam_re, s5_lam_im, s5_log_dt, s5_b_re, s5_b_im, s5_c_re, s5_c_im, s5_d, s5_w_glu, fox_b_f, w_out_even, w_in_odd, pool_w, pool_scale, sgu_ln_g, sgu_ln_b, sgu_w_s, sgu_b_s, w_out_odd, mlp_w1, mlp_w2, loss_target, m_mix_pre_g, m_mix_post_g, m_mlp_pre_g, m_mlp_post_g, m_w_in_even, m_s5_lam_re, m_s5_lam_im, m_s5_log_dt, m_s5_b_re, m_s5_b_im, m_s5_c_re, m_s5_c_im, m_s5_d, m_s5_w_glu, m_fox_b_f, m_w_out_even, m_w_in_odd, m_pool_w, m_pool_scale, m_sgu_ln_g, m_sgu_ln_b, m_sgu_w_s, m_sgu_b_s, m_w_out_odd, m_mlp_w1, m_mlp_w2, v_mix_pre_g, v_mix_post_g, v_mlp_pre_g, v_mlp_post_g, v_w_in_even, v_s5_lam_re, v_s5_lam_im, v_s5_log_dt, v_s5_b_re, v_s5_b_im, v_s5_c_re, v_s5_c_im, v_s5_d, v_s5_w_glu, v_fox_b_f, v_w_out_even, v_w_in_odd, v_pool_w, v_pool_scale, v_sgu_ln_g, v_sgu_ln_b, v_sgu_w_s, v_sgu_b_s, v_w_out_odd, v_mlp_w1, v_mlp_w2):
    weights = dict(mix_pre_g=mix_pre_g, mix_post_g=mix_post_g, mlp_pre_g=mlp_pre_g, mlp_post_g=mlp_post_g, w_in_even=w_in_even, s5_lam_re=s5_lam_re, s5_lam_im=s5_lam_im, s5_log_dt=s5_log_dt, s5_b_re=s5_b_re, s5_b_im=s5_b_im, s5_c_re=s5_c_re, s5_c_im=s5_c_im, s5_d=s5_d, s5_w_glu=s5_w_glu, fox_b_f=fox_b_f, w_out_even=w_out_even, w_in_odd=w_in_odd, pool_w=pool_w, pool_scale=pool_scale, sgu_ln_g=sgu_ln_g, sgu_ln_b=sgu_ln_b, sgu_w_s=sgu_w_s, sgu_b_s=sgu_b_s, w_out_odd=w_out_odd, mlp_w1=mlp_w1, mlp_w2=mlp_w2)
    mom_m = dict(mix_pre_g=m_mix_pre_g, mix_post_g=m_mix_post_g, mlp_pre_g=m_mlp_pre_g, mlp_post_g=m_mlp_post_g, w_in_even=m_w_in_even, s5_lam_re=m_s5_lam_re, s5_lam_im=m_s5_lam_im, s5_log_dt=m_s5_log_dt, s5_b_re=m_s5_b_re, s5_b_im=m_s5_b_im, s5_c_re=m_s5_c_re, s5_c_im=m_s5_c_im, s5_d=m_s5_d, s5_w_glu=m_s5_w_glu, fox_b_f=m_fox_b_f, w_out_even=m_w_out_even, w_in_odd=m_w_in_odd, pool_w=m_pool_w, pool_scale=m_pool_scale, sgu_ln_g=m_sgu_ln_g, sgu_ln_b=m_sgu_ln_b, sgu_w_s=m_sgu_w_s, sgu_b_s=m_sgu_b_s, w_out_odd=m_w_out_odd, mlp_w1=m_mlp_w1, mlp_w2=m_mlp_w2)
    mom_v = dict(mix_pre_g=v_mix_pre_g, mix_post_g=v_mix_post_g, mlp_pre_g=v_mlp_pre_g, mlp_post_g=v_mlp_post_g, w_in_even=v_w_in_even, s5_lam_re=v_s5_lam_re, s5_lam_im=v_s5_lam_im, s5_log_dt=v_s5_log_dt, s5_b_re=v_s5_b_re, s5_b_im=v_s5_b_im, s5_c_re=v_s5_c_re, s5_c_im=v_s5_c_im, s5_d=v_s5_d, s5_w_glu=v_s5_w_glu, fox_b_f=v_fox_b_f, w_out_even=v_w_out_even, w_in_odd=v_w_in_odd, pool_w=v_pool_w, pool_scale=v_pool_scale, sgu_ln_g=v_sgu_ln_g, sgu_ln_b=v_sgu_ln_b, sgu_w_s=v_sgu_w_s, sgu_b_s=v_sgu_b_s, w_out_odd=v_w_out_odd, mlp_w1=v_mlp_w1, mlp_w2=v_mlp_w2)
    names = list(weights)
    L = x.shape[1]
    x0 = x[0]
    target = loss_target[0]
    my_index = 4 * lax.axis_index("x") + 2 * lax.axis_index("y") + lax.axis_index("c")

    small_vec = jnp.zeros((8, LANES), F32)
    small_vec = small_vec.at[0, :64].set(pool_scale[0]).at[1, :64].set(sgu_ln_g[0]).at[2, :64].set(sgu_ln_b[0])
    ag_groups, ag_token = _comm_start(
        [[jnp.transpose(w_in_even[0]).astype(BF16), small_vec],
         [s5_w_glu[0].astype(BF16), w_out_even[0].astype(BF16)],
         [mlp_w1[0].astype(BF16), mlp_w2[0].astype(BF16)],
         [jnp.transpose(w_in_odd[0]).astype(BF16), w_out_odd[0].astype(BF16), mlp_w1[1].astype(BF16), mlp_w2[1].astype(BF16)]],
        "ag_start", exchange=False)

    lam_r = jnp.concatenate([s5_lam_re.reshape(1, S5_NS), s5_lam_im.reshape(1, S5_NS)], axis=0)
    ldt_r = jnp.repeat(s5_log_dt.reshape(32), 64).reshape(1, S5_NS)
    lam_c = jnp.transpose(lam_r)
    ldt_c = jnp.transpose(ldt_r)
    b_t = jnp.stack([jnp.tile(s5_b_re.reshape(S5_NS, 16), (1, 8)), jnp.tile(s5_b_im.reshape(S5_NS, 16), (1, 8))])
    c_t = jnp.stack([jnp.tile(s5_c_re.reshape(S5_W, 64), (1, 8)), jnp.tile(s5_c_im.reshape(S5_W, 64), (1, 8))])
    bf_pad = jnp.pad(fox_b_f, ((0, 0), (0, LANES - 8)))
    b_st = jnp.transpose(sgu_b_s[0])

    h0, rx0 = _rms_fwd(x0, _tie(mix_pre_g[0:1], ag_token), "rms0")
    tabs, bset, cset = _s5_prep(lam_r, ldt_r, lam_c, ldt_c, b_t, c_t, "s5_prep")
    ag0 = _comm_wait(ag_groups[0], tabs, "ag_wait0", exchange=False)
    winT_e = jnp.pad(ag0[0].reshape(EVEN_IN, D_MODEL), ((0, EVEN_PAD - EVEN_IN), (0, 0)))
    pool_scale_f = ag0[1][:, 0, :64].reshape(1, 512)
    ln_g_f = ag0[1][:, 1, :64].reshape(1, 512)
    ln_b_f = ag0[1][:, 2, :64].reshape(1, 512)
    z0 = _mm(h0, winT_e, name="win_even", tb=True, bm=512, bn=EVEN_PAD)
    xs, ylin = _s5_scan_fwd(z0, bset, cset, s5_d, tabs, "s5_scan")
    ag1 = _comm_wait(ag_groups[1], ylin, "ag_wait1", exchange=False)
    wglu = ag1[0].reshape(S5_W, S5_W)
    wout_e = ag1[1].reshape(D_MODEL, D_MODEL)
    ya = _s5_glu_fwd(ylin, wglu, "s5_glu")
    fcum, fq = _fox_f_fwd(z0, bf_pad, "fox_f")
    frow = jnp.transpose(fcum[:, :8]).reshape(4, 2, L)
    o_att, lse = _fox_fwd(z0, fq, frow, "fox_fwd")
    mix0 = [ya, o_att]
    x1, ry0, h1, rx1, y0 = _mm(mix0, wout_e, name="wout_even", epi=_epi_post_pre, extra=(x0,),
                               vecs=(mix_post_g[0:1], mlp_pre_g[0:1]), out_dtypes=POST_PRE_DTYPES,
                               out_kinds=POST_PRE_KINDS, bm=FUSED_ROWS)
    ag2 = _comm_wait(ag_groups[2], rx1, "ag_wait2", exchange=False)
    w1 = [ag2[0], None]
    w2 = [ag2[1].reshape(4 * D_MODEL, D_MODEL), None]
    p0, a0 = _mm(h1, w1[0], name="mlp0_w1", b3=True, out_dtypes=(BF16, BF16), epi=_epi_relu2, bm=512, bn=4 * D_MODEL)
    x2, ro0, h2, rx2, o0 = _mm(a0, w2[0], name="mlp0_w2", epi=_epi_post_pre, extra=(x1,),
                               vecs=(mlp_post_g[0:1], mix_pre_g[1:2]), out_dtypes=POST_PRE_DTYPES,
                               out_kinds=POST_PRE_KINDS, bm=FUSED_ROWS, bk=4 * D_MODEL)
    ag3 = _comm_wait(ag_groups[3], rx2, "ag_wait3", exchange=False)
    winT_o = ag3[0].reshape(ODD_IN, D_MODEL)
    wout_o = ag3[1].reshape(D_MODEL, D_MODEL)
    w1[1] = ag3[2]
    w2[1] = ag3[3].reshape(4 * D_MODEL, D_MODEL)
    z1 = _mm(h2, winT_o, name="win_odd", tb=True, bn=ODD_IN)
    yc, pooled = _pool_fwd(z1, pool_w[0], pool_scale_f, "pool_fwd")
    yd = _sgu_fwd(z1, ln_g_f, ln_b_f, sgu_w_s[0], b_st, "sgu_fwd")
    mix1 = [yc, yd]
    x3, ry1, h3, rx3, y1 = _mm(mix1, wout_o, name="wout_odd", epi=_epi_post_pre, extra=(x2,),
                               vecs=(mix_post_g[1:2], mlp_pre_g[1:2]), out_dtypes=POST_PRE_DTYPES,
                               out_kinds=POST_PRE_KINDS, bm=FUSED_ROWS)
    p1, a1 = _mm(h3, w1[1], name="mlp1_w1", b3=True, out_dtypes=(BF16, BF16), epi=_epi_relu2, bm=512, bn=4 * D_MODEL)
    gx4, g_o1, gg_mlp_post1, sq_lanes = _mm(
        a1, w2[1], name="mlp1_w2", epi=_epi_post_loss, extra=(x3, target), vecs=(mlp_post_g[1:2],),
        out_dtypes=(F32, BF16, F32, F32), out_kinds=("full", "full", "vsum", "vsum"), bm=FUSED_ROWS, bk=4 * D_MODEL)
    sq = sq_lanes[:, 0:1]

    g_p1 = _mm(g_o1, w2[1], name="b_mlp1_a", tb=True, out_dtypes=(BF16,), epi=_epi_relu2_bwd, extra=(p1,),
               bm=512, bn=4 * D_MODEL)
    gw2_1 = _mm(a1, g_o1, name="b_mlp1_w2", ta=True, bm=512, bk=L)
    gw1_1 = _mm(h3, g_p1, name="b_mlp1_w1", ta=True, out3=True, bn=512, bk=L)
    (ex1,), tok1 = _comm_start([[gw1_1, gw2_1.reshape(N_DEV, 512, D_MODEL)]], "ex_start1", exchange=True)
    g_x3, gg_mlp_pre1, g_y1, gg_mix_post1 = _mm(
        g_p1, w1[1], name="b_mlp1_h", tb=True, b3=True, epi=_epi_pre_post_bwd, extra=(x3, gx4, y1), cols=(rx3, ry1),
        vecs=(_tie(mlp_pre_g[1:2], tok1), mix_post_g[1:2]), out_dtypes=PRE_POST_BWD_DTYPES,
        out_kinds=PRE_POST_BWD_KINDS, bm=FUSED_ROWS, bk=4 * D_MODEL)
    gwout_o = _mm(mix1, g_y1, name="b_wout_odd_w", ta=True)
    g_xc, g_pool_w, g_pool_scale = _pool_bwd(g_y1, wout_o, pooled, pool_w[0], pool_scale_f, "pool_bwd")
    g_u1, g_v1, g_ws, g_bst, g_ln_g, g_ln_b = _sgu_bwd(g_y1, wout_o, z1, ln_g_f, ln_b_f, sgu_w_s[0], b_st,
                                                       "sgu_bwd")
    g_z1 = [g_xc, g_u1, g_v1]
    gwinT_o = _mm(g_z1, h2, name="b_win_odd_w", ta=True)
    (ex2,), tok2 = _comm_start([[gwout_o.reshape(N_DEV, 128, D_MODEL), gwinT_o.reshape(N_DEV, ODD_IN // N_DEV, D_MODEL)]], "ex_start2", exchange=True)
    g_x2, gg_mix_pre1, g_o0, gg_mlp_post0 = _mm(
        g_z1, winT_o, name="b_win_odd_h", epi=_epi_pre_post_bwd, extra=(x2, g_x3, o0), cols=(rx2, ro0),
        vecs=(_tie(mix_pre_g[1:2], tok2), mlp_post_g[0:1]), out_dtypes=PRE_POST_BWD_DTYPES,
        out_kinds=PRE_POST_BWD_KINDS, bm=FUSED_ROWS)
    g_p0 = _mm(g_o0, w2[0], name="b_mlp0_a", tb=True, out_dtypes=(BF16,), epi=_epi_relu2_bwd, extra=(p0,),
               bm=512, bn=4 * D_MODEL)
    gw2_0 = _mm(a0, g_o0, name="b_mlp0_w2", ta=True, bm=512, bk=L)
    gw1_0 = _mm(h1, g_p0, name="b_mlp0_w1", ta=True, out3=True, bn=512, bk=L)
    (ex3,), tok3 = _comm_start([[gw1_0, gw2_0.reshape(N_DEV, 512, D_MODEL)]], "ex_start3", exchange=True)
    g_x1, gg_mlp_pre0, g_y0, gg_mix_post0 = _mm(
        g_p0, w1[0], name="b_mlp0_h", tb=True, b3=True, epi=_epi_pre_post_bwd, extra=(x1, g_x2, y0), cols=(rx1, ry0),
        vecs=(_tie(mlp_pre_g[0:1], tok3), mix_post_g[0:1]), out_dtypes=PRE_POST_BWD_DTYPES,
        out_kinds=PRE_POST_BWD_KINDS, bm=FUSED_ROWS, bk=4 * D_MODEL)
    g_o_att = _mm(g_y0, wout_e[FOX_W:], name="b_wout_even_m", tb=True)
    gwout_e = _mm(mix0, g_y0, name="b_wout_even_w", ta=True)
    gyl, gud, g_wglu, g_d = _s5_glu_bwd(g_y0, wout_e, ylin, z0, s5_d, wglu, "s5_glu_bwd")
    (ex4,), tok4 = _comm_start([[gwout_e.reshape(N_DEV, 128, D_MODEL), g_wglu.reshape(N_DEV, 64, S5_W)]], "ex_start4", exchange=True)
    g_u0, ga, gb_raw, gc_raw = _s5_scan_bwd(gyl, _tie(cset, tok4), xs, z0, bset, gud, tabs, "s5_scan_bwd")
    g_lam, g_ldt, g_b, g_c = _s5_param_bwd(lam_c, ldt_c, b_t, gb_raw, jnp.transpose(ga), gc_raw, "s5_param_bwd")
    dq, dk, dv, dfq, dfrow = _fox_bwd(z0, frow, o_att, lse, g_o_att, "fox_bwd")
    dFk = jnp.pad(jnp.transpose(dfrow.reshape(8, L)), ((0, 0), (0, LANES - 8)))
    dfl, db_f = _fox_f_bwd(dFk, dfq, z0, bf_pad, "fox_f_bwd")
    g_z0 = [g_u0, dq, dk, dv, dfl]
    grad_x, gg_mix_pre0 = _mm(g_z0, winT_e, name="b_win_even_h", epi=_epi_pre_bwd, extra=(x0, g_x1), cols=(rx0,),
                              vecs=(mix_pre_g[0:1],), out_dtypes=(F32, F32), out_kinds=("full", "vsum"),
                              bm=FUSED_ROWS)

    small_grads = dict(
        mix_pre_g=jnp.concatenate([gg_mix_pre0, gg_mix_pre1]), mix_post_g=jnp.concatenate([gg_mix_post0, gg_mix_post1]),
        mlp_pre_g=jnp.concatenate([gg_mlp_pre0, gg_mlp_pre1]), mlp_post_g=jnp.concatenate([gg_mlp_post0, gg_mlp_post1]),
        s5_lam_re=g_lam[:, 0], s5_lam_im=g_lam[:, 1],
        s5_b_re=g_b[0, :, :16], s5_b_im=g_b[1, :, :16], s5_c_re=g_c[0, :, :64], s5_c_im=g_c[1, :, :64],
        pool_w=g_pool_w, sgu_w_s=g_ws, s5_d=g_d, sgu_b_s=jnp.transpose(g_bst),
        pool_scale=g_pool_scale, sgu_ln_g=g_ln_g, sgu_ln_b=g_ln_b, s5_log_dt=g_ldt, fox_b_f=db_f[:, :8])
    small_names = list(small_grads)
    full_shapes = [(512,) if nm in ("pool_scale", "sgu_ln_g", "sgu_ln_b") else weights[nm].shape for nm in small_names]
    full_shapes.append((1, 1))
    rows = _packed_rows(full_shapes)
    packed = _pack([small_grads[nm] for nm in small_names] + [sq], rows).reshape(N_DEV, rows // N_DEV, LANES)
    (exs,), tok_s = _comm_start([[packed]], "exs_start", exchange=True)
    gwinT_e = _mm(g_z0, h0, name="b_win_even_w", ta=True, out_dtypes=(BF16,), dep=tok_s)
    (recv_small,) = _comm_wait(exs, gwinT_e, "exs_wait", exchange=True)
    piece = _sum_pieces(recv_small, "sum_small")
    gwinT_e_pieces = gwinT_e[:EVEN_IN].reshape(N_DEV, EVEN_IN // N_DEV, D_MODEL)
    (ags, ex5), tok5 = _comm_start([[piece], [gwinT_e_pieces]], "ags_ex_start5", exchange=(False, True))
    r_w1_1, r_w2_1 = _comm_wait(ex1, tok5, "ex_wait1", exchange=True)
    r_wout_o, r_win_o = _comm_wait(ex2, tok5, "ex_wait2", exchange=True)
    r_w1_0, r_w2_0 = _comm_wait(ex3, tok5, "ex_wait3", exchange=True)
    r_wout_e, r_wglu = _comm_wait(ex4, tok5, "ex_wait4", exchange=True)

    res = {}
    big_parts = dict(mlp_w1=[r_w1_0, r_w1_1], mlp_w2=[r_w2_0, r_w2_1])
    for nm, parts in big_parts.items():
        res[nm] = tuple(_sum_adamw(parts, weights[nm], mom_m[nm], mom_v[nm], "adamw_" + nm))
    one_layer = dict(s5_w_glu=r_wglu, w_out_even=r_wout_e, w_out_odd=r_wout_o)
    for nm, outs in zip(one_layer, _sum_adamw_many(list(one_layer.values()), [weights[nm] for nm in one_layer],
                                                   [mom_m[nm] for nm in one_layer], [mom_v[nm] for nm in one_layer],
                                                   "adamw_one_layer")):
        res[nm] = outs
    done = [res[nm][1] for nm in ("mlp_w1", "mlp_w2", "s5_w_glu", "w_out_even", "w_out_odd")]

    (small_all,) = _comm_wait(ags, done, "ags_wait", exchange=False)
    small_full = _unpack(small_all.reshape(rows, LANES), full_shapes)
    loss = 0.5 * small_full.pop()[0, 0] / D_MODEL
    small_g = []
    for nm, g in zip(small_names, small_full):
        if nm in ("pool_scale", "sgu_ln_g", "sgu_ln_b"):
            g = lax.dynamic_slice(g, (my_index * 64,), (64,)).reshape(1, 64)
        small_g.append(g)

    def turned(arrs):
        return [jnp.swapaxes(a, -1, -2) if nm in ("s5_b_re", "s5_b_im") else a for nm, a in zip(small_names, arrs)]

    sd, sm, sv = _adamw_many(turned([weights[nm] for nm in small_names]), turned(small_g),
                             turned([mom_m[nm] for nm in small_names]), turned([mom_v[nm] for nm in small_names]),
                             "adamw_small")
    for nm, g_, d_, m_, v_ in zip(small_names, small_g, turned(sd), turned(sm), turned(sv)):
        res[nm] = (g_, d_, m_, v_)
    done.append(sd[0])

    nm = "w_in_odd"
    outs = _sum_adamw([r_win_o], jnp.transpose(weights[nm], (0, 2, 1)), jnp.transpose(mom_m[nm], (0, 2, 1)),
                      jnp.transpose(mom_v[nm], (0, 2, 1)), "adamw_" + nm)
    res[nm] = tuple(jnp.transpose(o, (0, 2, 1)) for o in outs)
    done.append(res[nm][1])
    nm = "w_in_even"
    (r_win_e,) = _comm_wait(ex5, done, "ex_wait5", exchange=True)
    outs = _sum_adamw_rows(r_win_e, jnp.transpose(weights[nm], (2, 0, 1)), jnp.transpose(mom_m[nm], (2, 0, 1)),
                           jnp.transpose(mom_v[nm], (2, 0, 1)), "adamw_" + nm)
    res[nm] = tuple(jnp.transpose(o, (1, 2, 0)) for o in outs)

    grads = [res[nm][0].reshape(weights[nm].shape) for nm in names]
    deltas = [res[nm][1].reshape(weights[nm].shape) for nm in names]
    new_m = [res[nm][2].reshape(weights[nm].shape) for nm in names]
    new_v = [res[nm][3].reshape(weights[nm].shape) for nm in names]
    return (loss, grad_x[None], *grads, *deltas, *new_m, *new_v)
```
